```python
import jax
import jax.numpy as jnp
from jax import lax
import numpy as np

D_MODEL = 1024
BATCH = 8
SEQ = 4096
DEPTH = 1

HEAD_DIM = 64
ATTN_WIDTH = D_MODEL // 2
N_ATTN_HEADS = ATTN_WIDTH // HEAD_DIM
N_KV_HEADS = 2
KV_WIDTH = N_KV_HEADS * HEAD_DIM
WINDOW = 128
BLOCK = 128
ROPE_THETA = 500000.0
ROTARY_DIM = HEAD_DIM // 4
POOL_WINDOWS = (2, 4, 8, 16)
N_POOL_GROUPS = len(POOL_WINDOWS)
POOL_WIDTH = D_MODEL // 2
POOL_GROUP_WIDTH = POOL_WIDTH // N_POOL_GROUPS
MIX_WIDTH = ATTN_WIDTH + POOL_WIDTH
IN_WIDTH = ATTN_WIDTH + 2 * KV_WIDTH + POOL_WIDTH
D_FF = 2816
EPS = 1e-6

kernel_name = 'hybrid_window_gqa_multiscale_pool_macaron'


def rms_norm(x, g):
    xf = x.astype(jnp.float32)
    y = xf * lax.rsqrt(jnp.mean(xf * xf, axis=-1, keepdims=True) + EPS)
    return (y * g.astype(jnp.float32)).astype(x.dtype)


def swiglu(x, w_gate, w_up, w_down):
    return (jax.nn.silu(x @ w_gate) * (x @ w_up)) @ w_down


def rope_tables(positions):
    inv_freq = ROPE_THETA ** (-jnp.arange(0, ROTARY_DIM, 2, dtype=jnp.float32) / ROTARY_DIM)
    ang = positions.astype(jnp.float32)[:, None] * inv_freq[None, :]
    emb = jnp.concatenate([ang, ang], axis=-1)
    return jnp.cos(emb)[None, :, None, :], jnp.sin(emb)[None, :, None, :]


def apply_partial_rope(t, cos, sin):
    tf = t.astype(jnp.float32)
    rot, rest = tf[..., :ROTARY_DIM], tf[..., ROTARY_DIM:]
    half = ROTARY_DIM // 2
    rot_half = jnp.concatenate([-rot[..., half:], rot[..., :half]], axis=-1)
    rot = rot * cos + rot_half * sin
    return jnp.concatenate([rot, rest], axis=-1).astype(t.dtype)


def window_attention(q, k, v, sink):
    B, S, H, D = q.shape
    nb = S // BLOCK
    G = H // N_KV_HEADS
    qb = q.reshape(B, nb, BLOCK, N_KV_HEADS, G, D)

    def bands(t):
        tp = jnp.pad(t, ((0, 0), (BLOCK, BLOCK), (0, 0), (0, 0)))
        tb = tp.reshape(B, nb + 2, BLOCK, N_KV_HEADS, D)
        return jnp.concatenate([tb[:, :-2], tb[:, 1:-1], tb[:, 2:]], axis=2)

    kb, vb = bands(k), bands(v)
    scores = jnp.einsum('bnqkgd,bnskd->bnkgqs', qb, kb,
                        preferred_element_type=jnp.float32) * (D ** -0.5)
    qpos = jnp.arange(nb)[:, None] * BLOCK + jnp.arange(BLOCK)[None, :]
    kpos = jnp.arange(nb)[:, None] * BLOCK + jnp.arange(3 * BLOCK)[None, :] - BLOCK
    dist = qpos[:, :, None] - kpos[:, None, :]
    valid = (jnp.abs(dist) <= WINDOW) & (kpos[:, None, :] >= 0) & (kpos[:, None, :] < S)
    scores = jnp.where(valid[None, :, None, None], scores, -1e30)
    sink_l = sink.astype(jnp.float32).reshape(N_KV_HEADS, G)[None, None, :, :, None, None]
    m = jnp.maximum(jnp.max(scores, axis=-1, keepdims=True), sink_l)
    p = jnp.exp(scores - m)
    denom = jnp.sum(p, axis=-1, keepdims=True) + jnp.exp(sink_l - m)
    p = (p / denom).astype(v.dtype)
    out = jnp.einsum('bnkgqs,bnskd->bnqkgd', p, vb)
    return out.reshape(B, S, H * D)


def multiscale_pool(u, w_pool, pool_scale):
    B, S, _ = u.shape
    uf = u.astype(jnp.float32)
    csum = jnp.concatenate([jnp.zeros((B, 1, POOL_WIDTH), jnp.float32),
                            jnp.cumsum(uf, axis=1)], axis=1)
    t = jnp.arange(S)
    means = []
    for gi, w in enumerate(POOL_WINDOWS):
        cg = csum[..., gi * POOL_GROUP_WIDTH:(gi + 1) * POOL_GROUP_WIDTH]
        half = w // 2

        def win_mean(lo, hi):
            a = jnp.clip(lo, 0, S)
            b = jnp.clip(hi + 1, 0, S)
            s = jnp.take(cg, b, axis=1) - jnp.take(cg, a, axis=1)
            return s / (b - a).astype(jnp.float32)[None, :, None]

        means.append(0.5 * (win_mean(t - half, t + half - 1) + win_mean(t - half + 1, t + half)))
    mean = jnp.concatenate(means, axis=-1)
    d = (mean - uf).reshape(B, S, N_POOL_GROUPS, POOL_GROUP_WIDTH)
    y = jnp.einsum('bsgc,gcd->bsgd', d, w_pool.astype(jnp.float32)).reshape(B, S, POOL_WIDTH)
    return (y * pool_scale.astype(jnp.float32)).astype(u.dtype)


def _fwd_setup_inputs(seed: int = 0) -> dict:
    key = jax.random.key(seed)
    ks = jax.random.split(key, 18)
    f32 = jnp.float32
    L = DEPTH

    def nrm(k, shape, scale):
        return jax.random.normal(k, shape, f32) * scale

    def gain(k, shape, s=0.05):
        return 1.0 + s * jax.random.normal(k, shape, f32)

    return {
        'x': nrm(ks[0], (BATCH, SEQ, D_MODEL), 1.0),
        'ffn1_norm': gain(ks[1], (L, D_MODEL)),
        'ffn1_w_gate': nrm(ks[2], (L, D_MODEL, D_FF), D_MODEL ** -0.5),
        'ffn1_w_up': nrm(ks[3], (L, D_MODEL, D_FF), D_MODEL ** -0.5),
        'ffn1_w_down': nrm(ks[4], (L, D_FF, D_MODEL), D_FF ** -0.5),
        'mix_norm': gain(ks[5], (L, D_MODEL)),
        'w_in': nrm(ks[6], (L, D_MODEL, IN_WIDTH), D_MODEL ** -0.5),
        'sink_logits': nrm(ks[7], (L, N_ATTN_HEADS), 0.5),
        'pool_w': nrm(ks[8], (L, N_POOL_GROUPS, POOL_GROUP_WIDTH, POOL_GROUP_WIDTH), POOL_GROUP_WIDTH ** -0.5),
        'pool_scale': gain(ks[9], (L, POOL_WIDTH), 0.1),
        'w_out': nrm(ks[10], (L, MIX_WIDTH, D_MODEL), MIX_WIDTH ** -0.5),
        'ffn2_norm': gain(ks[11], (L, D_MODEL)),
        'ffn2_w_gate': nrm(ks[12], (L, D_MODEL, D_FF), D_MODEL ** -0.5),
        'ffn2_w_up': nrm(ks[13], (L, D_MODEL, D_FF), D_MODEL ** -0.5),
        'ffn2_w_down': nrm(ks[14], (L, D_FF, D_MODEL), D_FF ** -0.5),
        'final_norm': gain(ks[15], (D_MODEL,)),
    }


def _fwd_reference(x, ffn1_norm, ffn1_w_gate, ffn1_w_up, ffn1_w_down, mix_norm, w_in,
              sink_logits, pool_w, pool_scale, w_out, ffn2_norm, ffn2_w_gate,
              ffn2_w_up, ffn2_w_down, final_norm):
    B, S, _ = x.shape
    positions = jnp.arange(S, dtype=jnp.int32)
    cos, sin = rope_tables(positions)
    h = x
    for l in range(DEPTH):
        h = h + 0.5 * swiglu(rms_norm(h, ffn1_norm[l]), ffn1_w_gate[l], ffn1_w_up[l], ffn1_w_down[l])
        u = rms_norm(h, mix_norm[l]) @ w_in[l]
        q, k, v, pc = jnp.split(u, [ATTN_WIDTH, ATTN_WIDTH + KV_WIDTH, ATTN_WIDTH + 2 * KV_WIDTH], axis=-1)
        q = apply_partial_rope(q.reshape(B, S, N_ATTN_HEADS, HEAD_DIM), cos, sin)
        k = apply_partial_rope(k.reshape(B, S, N_KV_HEADS, HEAD_DIM), cos, sin)
        v = v.reshape(B, S, N_KV_HEADS, HEAD_DIM)
        a = window_attention(q, k, v, sink_logits[l])
        p = multiscale_pool(pc, pool_w[l], pool_scale[l])
        h = h + jnp.concatenate([a, p], axis=-1) @ w_out[l]
        h = h + 0.5 * swiglu(rms_norm(h, ffn2_norm[l]), ffn2_w_gate[l], ffn2_w_up[l], ffn2_w_down[l])
    return rms_norm(h, final_norm)


import jax as _jax
import jax.numpy as _jnp

TWIN_FORMAT = 'train_step'
FWD_PARAMS = ['x', 'ffn1_norm', 'ffn1_w_gate', 'ffn1_w_up', 'ffn1_w_down', 'mix_norm', 'w_in', 'sink_logits', 'pool_w', 'pool_scale', 'w_out', 'ffn2_norm', 'ffn2_w_gate', 'ffn2_w_up', 'ffn2_w_down', 'final_norm']
TWIN_WEIGHTS = ['ffn1_norm', 'ffn1_w_gate', 'ffn1_w_up', 'ffn1_w_down', 'mix_norm', 'w_in', 'sink_logits', 'pool_w', 'pool_scale', 'w_out', 'ffn2_norm', 'ffn2_w_gate', 'ffn2_w_up', 'ffn2_w_down', 'final_norm']
TWIN_DIFF_INPUT = 'x'
TWIN_INPUTS = ['x', 'ffn1_norm', 'ffn1_w_gate', 'ffn1_w_up', 'ffn1_w_down', 'mix_norm', 'w_in', 'sink_logits', 'pool_w', 'pool_scale', 'w_out', 'ffn2_norm', 'ffn2_w_gate', 'ffn2_w_up', 'ffn2_w_down', 'final_norm', 'loss_target', 'm_ffn1_norm', 'm_ffn1_w_gate', 'm_ffn1_w_up', 'm_ffn1_w_down', 'm_mix_norm', 'm_w_in', 'm_sink_logits', 'm_pool_w', 'm_pool_scale', 'm_w_out', 'm_ffn2_norm', 'm_ffn2_w_gate', 'm_ffn2_w_up', 'm_ffn2_w_down', 'm_final_norm', 'v_ffn1_norm', 'v_ffn1_w_gate', 'v_ffn1_w_up', 'v_ffn1_w_down', 'v_mix_norm', 'v_w_in', 'v_sink_logits', 'v_pool_w', 'v_pool_scale', 'v_w_out', 'v_ffn2_norm', 'v_ffn2_w_gate', 'v_ffn2_w_up', 'v_ffn2_w_down', 'v_final_norm']
TWIN_OUTPUTS = ['loss', 'grad_x', 'grad_ffn1_norm', 'grad_ffn1_w_gate', 'grad_ffn1_w_up', 'grad_ffn1_w_down', 'grad_mix_norm', 'grad_w_in', 'grad_sink_logits', 'grad_pool_w', 'grad_pool_scale', 'grad_w_out', 'grad_ffn2_norm', 'grad_ffn2_w_gate', 'grad_ffn2_w_up', 'grad_ffn2_w_down', 'grad_final_norm', 'delta_ffn1_norm', 'delta_ffn1_w_gate', 'delta_ffn1_w_up', 'delta_ffn1_w_down', 'delta_mix_norm', 'delta_w_in', 'delta_sink_logits', 'delta_pool_w', 'delta_pool_scale', 'delta_w_out', 'delta_ffn2_norm', 'delta_ffn2_w_gate', 'delta_ffn2_w_up', 'delta_ffn2_w_down', 'delta_final_norm', 'new_m_ffn1_norm', 'new_m_ffn1_w_gate', 'new_m_ffn1_w_up', 'new_m_ffn1_w_down', 'new_m_mix_norm', 'new_m_w_in', 'new_m_sink_logits', 'new_m_pool_w', 'new_m_pool_scale', 'new_m_w_out', 'new_m_ffn2_norm', 'new_m_ffn2_w_gate', 'new_m_ffn2_w_up', 'new_m_ffn2_w_down', 'new_m_final_norm', 'new_v_ffn1_norm', 'new_v_ffn1_w_gate', 'new_v_ffn1_w_up', 'new_v_ffn1_w_down', 'new_v_mix_norm', 'new_v_w_in', 'new_v_sink_logits', 'new_v_pool_w', 'new_v_pool_scale', 'new_v_w_out', 'new_v_ffn2_norm', 'new_v_ffn2_w_gate', 'new_v_ffn2_w_up', 'new_v_ffn2_w_down', 'new_v_final_norm']
TWIN_LEAF_KINDS = {'loss': 'loss', 'grad_x': 'grad_x', 'grad_ffn1_norm': 'grad_w', 'grad_ffn1_w_gate': 'grad_w', 'grad_ffn1_w_up': 'grad_w', 'grad_ffn1_w_down': 'grad_w', 'grad_mix_norm': 'grad_w', 'grad_w_in': 'grad_w', 'grad_sink_logits': 'grad_w', 'grad_pool_w': 'grad_w', 'grad_pool_scale': 'grad_w', 'grad_w_out': 'grad_w', 'grad_ffn2_norm': 'grad_w', 'grad_ffn2_w_gate': 'grad_w', 'grad_ffn2_w_up': 'grad_w', 'grad_ffn2_w_down': 'grad_w', 'grad_final_norm': 'grad_w', 'delta_ffn1_norm': 'delta_w', 'delta_ffn1_w_gate': 'delta_w', 'delta_ffn1_w_up': 'delta_w', 'delta_ffn1_w_down': 'delta_w', 'delta_mix_norm': 'delta_w', 'delta_w_in': 'delta_w', 'delta_sink_logits': 'delta_w', 'delta_pool_w': 'delta_w', 'delta_pool_scale': 'delta_w', 'delta_w_out': 'delta_w', 'delta_ffn2_norm': 'delta_w', 'delta_ffn2_w_gate': 'delta_w', 'delta_ffn2_w_up': 'delta_w', 'delta_ffn2_w_down': 'delta_w', 'delta_final_norm': 'delta_w', 'new_m_ffn1_norm': 'new_m', 'new_m_ffn1_w_gate': 'new_m', 'new_m_ffn1_w_up': 'new_m', 'new_m_ffn1_w_down': 'new_m', 'new_m_mix_norm': 'new_m', 'new_m_w_in': 'new_m', 'new_m_sink_logits': 'new_m', 'new_m_pool_w': 'new_m', 'new_m_pool_scale': 'new_m', 'new_m_w_out': 'new_m', 'new_m_ffn2_norm': 'new_m', 'new_m_ffn2_w_gate': 'new_m', 'new_m_ffn2_w_up': 'new_m', 'new_m_ffn2_w_down': 'new_m', 'new_m_final_norm': 'new_m', 'new_v_ffn1_norm': 'new_v', 'new_v_ffn1_w_gate': 'new_v', 'new_v_ffn1_w_up': 'new_v', 'new_v_ffn1_w_down': 'new_v', 'new_v_mix_norm': 'new_v', 'new_v_w_in': 'new_v', 'new_v_sink_logits': 'new_v', 'new_v_pool_w': 'new_v', 'new_v_pool_scale': 'new_v', 'new_v_w_out': 'new_v', 'new_v_ffn2_norm': 'new_v', 'new_v_ffn2_w_gate': 'new_v', 'new_v_ffn2_w_up': 'new_v', 'new_v_ffn2_w_down': 'new_v', 'new_v_final_norm': 'new_v'}


def _forward(args):
    return _fwd_reference(*[args[k] for k in FWD_PARAMS])


def _output_shape():
    out = _jax.eval_shape(lambda: _forward(_fwd_setup_inputs(0)))
    return out.shape, out.dtype

N_MICROBATCH = 1
ADAM_LR = 0.001
ADAM_B1 = 0.9
ADAM_B2 = 0.999
ADAM_EPS = 1e-08
ADAM_WD = 0.01
ADAM_STEP = 10
PER_EXAMPLE_BATCH_AXIS = {'x': 0, 'loss_target': 0}
SHARED_INPUTS = []
_WEIGHT_DTYPES = {'ffn1_norm': _jnp.float32, 'ffn1_w_gate': _jnp.float32, 'ffn1_w_up': _jnp.float32, 'ffn1_w_down': _jnp.float32, 'mix_norm': _jnp.float32, 'w_in': _jnp.float32, 'sink_logits': _jnp.float32, 'pool_w': _jnp.float32, 'pool_scale': _jnp.float32, 'w_out': _jnp.float32, 'ffn2_norm': _jnp.float32, 'ffn2_w_gate': _jnp.float32, 'ffn2_w_up': _jnp.float32, 'ffn2_w_down': _jnp.float32, 'final_norm': _jnp.float32}
MOMENT_SCALE = {'ffn1_norm': 7.791649e-02, 'ffn1_w_gate': 3.289584e-02, 'ffn1_w_up': 3.179732e-02, 'ffn1_w_down': 5.275280e-02, 'mix_norm': 1.028763e-01, 'w_in': 8.581091e-02, 'sink_logits': 3.249127e-04, 'pool_w': 1.392236e-01, 'pool_scale': 1.547805e-01, 'w_out': 9.823800e-02, 'ffn2_norm': 5.962857e-02, 'ffn2_w_gate': 2.668429e-02, 'ffn2_w_up': 2.602400e-02, 'ffn2_w_down': 4.326444e-02, 'final_norm': 3.199146e+01}


def _to_microbatches(a, axis):
    t = _jnp.moveaxis(a, axis, 0)
    t = t.reshape((N_MICROBATCH, t.shape[0] // N_MICROBATCH) + t.shape[1:])
    return _jnp.moveaxis(t, 1, axis + 1)


def setup_inputs(seed: int = 0) -> dict:
    inp = _fwd_setup_inputs(seed)
    key = _jax.random.fold_in(_jax.random.key(seed), 7919)
    shape, _ = _output_shape()
    out = dict(inp)
    out["loss_target"] = _jax.random.normal(_jax.random.fold_in(key, 0), shape, _jnp.float32)
    for i, name in enumerate(TWIN_WEIGHTS):
        w = inp[name].astype(_jnp.float32)
        if MOMENT_SCALE is None:
            s = _jnp.sqrt(_jnp.mean(_jnp.square(w)) + 1e-30)
        else:
            s = MOMENT_SCALE[name]
        km, kv = _jax.random.split(_jax.random.fold_in(key, i + 1))
        out[name] = w
        out["m_" + name] = s * _jax.random.normal(km, w.shape, _jnp.float32)
        out["v_" + name] = (s * s) * _jax.random.uniform(kv, w.shape, _jnp.float32, 0.5, 1.5)
    if N_MICROBATCH > 1:
        for name, axis in PER_EXAMPLE_BATCH_AXIS.items():
            out[name] = _to_microbatches(out[name], axis)
    return {'x': out['x'], 'ffn1_norm': out['ffn1_norm'], 'ffn1_w_gate': out['ffn1_w_gate'], 'ffn1_w_up': out['ffn1_w_up'], 'ffn1_w_down': out['ffn1_w_down'], 'mix_norm': out['mix_norm'], 'w_in': out['w_in'], 'sink_logits': out['sink_logits'], 'pool_w': out['pool_w'], 'pool_scale': out['pool_scale'], 'w_out': out['w_out'], 'ffn2_norm': out['ffn2_norm'], 'ffn2_w_gate': out['ffn2_w_gate'], 'ffn2_w_up': out['ffn2_w_up'], 'ffn2_w_down': out['ffn2_w_down'], 'final_norm': out['final_norm'], 'loss_target': out['loss_target'], 'm_ffn1_norm': out['m_ffn1_norm'], 'm_ffn1_w_gate': out['m_ffn1_w_gate'], 'm_ffn1_w_up': out['m_ffn1_w_up'], 'm_ffn1_w_down': out['m_ffn1_w_down'], 'm_mix_norm': out['m_mix_norm'], 'm_w_in': out['m_w_in'], 'm_sink_logits': out['m_sink_logits'], 'm_pool_w': out['m_pool_w'], 'm_pool_scale': out['m_pool_scale'], 'm_w_out': out['m_w_out'], 'm_ffn2_norm': out['m_ffn2_norm'], 'm_ffn2_w_gate': out['m_ffn2_w_gate'], 'm_ffn2_w_up': out['m_ffn2_w_up'], 'm_ffn2_w_down': out['m_ffn2_w_down'], 'm_final_norm': out['m_final_norm'], 'v_ffn1_norm': out['v_ffn1_norm'], 'v_ffn1_w_gate': out['v_ffn1_w_gate'], 'v_ffn1_w_up': out['v_ffn1_w_up'], 'v_ffn1_w_down': out['v_ffn1_w_down'], 'v_mix_norm': out['v_mix_norm'], 'v_w_in': out['v_w_in'], 'v_sink_logits': out['v_sink_logits'], 'v_pool_w': out['v_pool_w'], 'v_pool_scale': out['v_pool_scale'], 'v_w_out': out['v_w_out'], 'v_ffn2_norm': out['v_ffn2_norm'], 'v_ffn2_w_gate': out['v_ffn2_w_gate'], 'v_ffn2_w_up': out['v_ffn2_w_up'], 'v_ffn2_w_down': out['v_ffn2_w_down'], 'v_final_norm': out['v_final_norm']}


def _loss(weights, diff, rest, loss_target):
    with _jax.named_scope("forward"):
        args = {**rest, TWIN_DIFF_INPUT: diff, **{k: w.astype(_WEIGHT_DTYPES[k]) for k, w in weights.items()}}
        y = _forward(args)
    with _jax.named_scope("loss_head"):
        err = _jnp.square(y.astype(_jnp.float32) - loss_target)
        return 0.5 * _jnp.sum(_jnp.mean(err, axis=-1)) if err.ndim else 0.5 * err


def _adamw(w, g, m, v):
    m = ADAM_B1 * m + (1.0 - ADAM_B1) * g
    v = ADAM_B2 * v + (1.0 - ADAM_B2) * _jnp.square(g)
    m_hat = m / (1.0 - ADAM_B1 ** ADAM_STEP)
    v_hat = v / (1.0 - ADAM_B2 ** ADAM_STEP)
    delta = -ADAM_LR * (m_hat / (_jnp.sqrt(v_hat) + ADAM_EPS) + ADAM_WD * w)
    return delta, m, v


def reference(x, ffn1_norm, ffn1_w_gate, ffn1_w_up, ffn1_w_down, mix_norm, w_in, sink_logits, pool_w, pool_scale, w_out, ffn2_norm, ffn2_w_gate, ffn2_w_up, ffn2_w_down, final_norm, loss_target, m_ffn1_norm, m_ffn1_w_gate, m_ffn1_w_up, m_ffn1_w_down, m_mix_norm, m_w_in, m_sink_logits, m_pool_w, m_pool_scale, m_w_out, m_ffn2_norm, m_ffn2_w_gate, m_ffn2_w_up, m_ffn2_w_down, m_final_norm, v_ffn1_norm, v_ffn1_w_gate, v_ffn1_w_up, v_ffn1_w_down, v_mix_norm, v_w_in, v_sink_logits, v_pool_w, v_pool_scale, v_w_out, v_ffn2_norm, v_ffn2_w_gate, v_ffn2_w_up, v_ffn2_w_down, v_final_norm):
    given = dict(x=x, ffn1_norm=ffn1_norm, ffn1_w_gate=ffn1_w_gate, ffn1_w_up=ffn1_w_up, ffn1_w_down=ffn1_w_down, mix_norm=mix_norm, w_in=w_in, sink_logits=sink_logits, pool_w=pool_w, pool_scale=pool_scale, w_out=w_out, ffn2_norm=ffn2_norm, ffn2_w_gate=ffn2_w_gate, ffn2_w_up=ffn2_w_up, ffn2_w_down=ffn2_w_down, final_norm=final_norm, loss_target=loss_target, m_ffn1_norm=m_ffn1_norm, m_ffn1_w_gate=m_ffn1_w_gate, m_ffn1_w_up=m_ffn1_w_up, m_ffn1_w_down=m_ffn1_w_down, m_mix_norm=m_mix_norm, m_w_in=m_w_in, m_sink_logits=m_sink_logits, m_pool_w=m_pool_w, m_pool_scale=m_pool_scale, m_w_out=m_w_out, m_ffn2_norm=m_ffn2_norm, m_ffn2_w_gate=m_ffn2_w_gate, m_ffn2_w_up=m_ffn2_w_up, m_ffn2_w_down=m_ffn2_w_down, m_final_norm=m_final_norm, v_ffn1_norm=v_ffn1_norm, v_ffn1_w_gate=v_ffn1_w_gate, v_ffn1_w_up=v_ffn1_w_up, v_ffn1_w_down=v_ffn1_w_down, v_mix_norm=v_mix_norm, v_w_in=v_w_in, v_sink_logits=v_sink_logits, v_pool_w=v_pool_w, v_pool_scale=v_pool_scale, v_w_out=v_w_out, v_ffn2_norm=v_ffn2_norm, v_ffn2_w_gate=v_ffn2_w_gate, v_ffn2_w_up=v_ffn2_w_up, v_ffn2_w_down=v_ffn2_w_down, v_final_norm=v_final_norm)
    weights = {n: given[n] for n in TWIN_WEIGHTS}
    shared = {n: given[n] for n in SHARED_INPUTS}
    per_example = {n: given[n] for n in ['x']}
    grad_fn = _jax.value_and_grad(_loss, argnums=(0, 1))

    def one_microbatch(ex, loss_target):
        ex = dict(ex)
        diff = ex.pop(TWIN_DIFF_INPUT)
        return grad_fn(weights, diff, {**shared, **ex}, loss_target)

    if N_MICROBATCH == 1:
        loss, (grad_w, grad_x) = one_microbatch(per_example, given["loss_target"])
    else:
        def body(carry, xs):
            loss_sum, grad_sum = carry
            l_k, (gw_k, gx_k) = one_microbatch(xs[0], xs[1])
            with _jax.named_scope("update"):
                return (loss_sum + l_k, _jax.tree.map(_jnp.add, grad_sum, gw_k)), gx_k

        init = (_jnp.zeros((), _jnp.float32), _jax.tree.map(_jnp.zeros_like, weights))
        (loss, grad_w), grad_x = _jax.lax.scan(body, init, (per_example, given["loss_target"]))
    with _jax.named_scope("update"):
        delta_w, new_m, new_v = {}, {}, {}
        for n in TWIN_WEIGHTS:
            delta_w[n], new_m[n], new_v[n] = _adamw(weights[n], grad_w[n], given["m_" + n], given["v_" + n])
    return (loss, grad_x, *[grad_w[n] for n in TWIN_WEIGHTS], *[delta_w[n] for n in TWIN_WEIGHTS],
            *[new_m[n] for n in TWIN_WEIGHTS], *[new_v[n] for n in TWIN_WEIGHTS])
```

```python
import functools

import jax
import jax.numpy as jnp
import numpy as np
from jax import lax
from jax.experimental import pallas as pl
from jax.experimental.pallas import tpu as pltpu

F32 = jnp.float32
BF16 = jnp.bfloat16
MESH = pl.DeviceIdType.MESH
N_DEV = 8

EPS = 1e-6
HEAD_DIM = 64
N_HEADS = 8
N_KV = 2
GROUP = N_HEADS // N_KV
ATTN_W = N_HEADS * HEAD_DIM
KV_W = N_KV * HEAD_DIM
POOL_W = 512
POOL_G = 4
POOL_GW = POOL_W // POOL_G
POOL_WINDOWS = (2, 4, 8, 16)
BLK = 128
ROT = 16
ROPE_THETA = 500000.0
SCORE_SCALE = HEAD_DIM ** -0.5

ADAM_LR, ADAM_B1, ADAM_B2, ADAM_EPS, ADAM_WD, ADAM_STEP = 0.001, 0.9, 0.999, 1e-08, 0.01, 10

VMEM_LIMIT = 56 * 1024 * 1024


def _cparams(sem=None, **kw):
    if sem is not None:
        kw["dimension_semantics"] = sem
    return pltpu.CompilerParams(vmem_limit_bytes=VMEM_LIMIT, **kw)


def _whole(shape):
    nd = len(shape)
    return pl.BlockSpec(shape, lambda *_: (0,) * nd, pipeline_mode=pl.Buffered(1))


def _sigmoid(z):
    return 1.0 / (1.0 + jnp.exp(-z))


def _dot_nt(a, b):
    return lax.dot_general(a, b, (((1,), (1,)), ((), ())), preferred_element_type=F32)


def _dot_nn(a, b):
    return lax.dot_general(a, b, (((1,), (0,)), ((), ())), preferred_element_type=F32)


def _dot_tn(a, b):
    return lax.dot_general(a, b, (((0,), (0,)), ((), ())), preferred_element_type=F32)


def _rope_tables(s):
    inv_freq = ROPE_THETA ** (-np.arange(0, ROT, 2, dtype=np.float64) / ROT)
    ang = np.arange(s, dtype=np.float64)[:, None] * inv_freq[None, :]
    c = np.ones((s, HEAD_DIM)); sa = np.zeros((s, HEAD_DIM)); sb = np.zeros((s, HEAD_DIM))
    c[:, :8] = np.cos(ang); c[:, 8:16] = np.cos(ang)
    sa[:, :8] = -np.sin(ang)
    sb[:, 8:16] = np.sin(ang)
    t = lambda a: jnp.asarray(np.tile(a, (1, 2)).astype(np.float32))
    return t(c), t(sa), t(sb)


def _pool_weight(gi, t, s_pos, s):
    half = POOL_WINDOWS[gi] // 2

    def win(lo, hi):
        a = np.clip(lo, 0, s); b = np.clip(hi + 1, 0, s)
        inside = (s_pos >= a) & (s_pos < b)
        return inside / np.maximum(b - a, 1)

    w = 0.5 * (win(t - half, t + half - 1) + win(t - half + 1, t + half)) - (t == s_pos)
    return w * ((t >= 0) & (t < s) & (s_pos >= 0) & (s_pos < s))


def _pool_tables(s):
    nb = s // BLK
    fwd = np.zeros((3, POOL_G, BLK, 3 * BLK), np.float32)
    bwd = np.zeros((3, POOL_G, BLK, 3 * BLK), np.float32)
    for vi, n in enumerate((0, 1 if nb > 2 else 0, nb - 1)):
        i = n * BLK + np.arange(BLK)[:, None]
        j = (n - 1) * BLK + np.arange(3 * BLK)[None, :]
        for gi in range(POOL_G):
            fwd[vi, gi] = _pool_weight(gi, i, j, s)
            bwd[vi, gi] = _pool_weight(gi, j, i, s)
    return jnp.asarray(fwd, dtype=BF16), jnp.asarray(bwd, dtype=BF16)


def _variant_index(n, nb):
    return jnp.where(n == 0, 0, jnp.where(n == nb - 1, 2, 1))


def _all_gather(arrays, name):
    n_arr = len(arrays)

    def body(*refs):
        srcs = refs[:n_arr]
        outs = refs[n_arr:2 * n_arr]
        send_sems, recv_sems, local_sems = refs[2 * n_arr:]
        x, y, c = lax.axis_index("x"), lax.axis_index("y"), lax.axis_index("c")
        me, sibling = (x, y, c), (x, y, 1 - c)
        chips = [(1 - x, y), (x, 1 - y), (1 - x, 1 - y)]

        def slot(a, dev):
            return outs[a].at[4 * dev[0] + 2 * dev[1] + dev[2]]

        def copy(a, k, block, to, src=None):
            return pltpu.make_async_remote_copy(
                src_ref=slot(a, block) if src is None else src, dst_ref=slot(a, block),
                send_sem=send_sems.at[a, k], recv_sem=recv_sems.at[a, k], device_id=to, device_id_type=MESH)

        mine = [pltpu.make_async_copy(srcs[a], slot(a, me), local_sems.at[a]) for a in range(n_arr)]
        for cp in mine:
            cp.start()
        first = []
        for a in range(n_arr):
            first.append(copy(a, 0, me, sibling, src=srcs[a]))
            first += [copy(a, 1 + j, me, (*chip, c), src=srcs[a]) for j, chip in enumerate(chips)]
        for cp in first:
            cp.start()
        passed = []
        for j, chip in enumerate(chips):
            for a in range(n_arr):
                copy(a, 1 + j, (*chip, c), me).wait_recv()
                fwd = copy(a, 4 + j, (*chip, c), sibling)
                fwd.start()
                passed.append(fwd)
        for a in range(n_arr):
            copy(a, 0, sibling, me).wait_recv()
            for j, chip in enumerate(chips):
                copy(a, 4 + j, (*chip, 1 - c), me).wait_recv()
        for cp in first + passed:
            cp.wait_send()
        for cp in mine:
            cp.wait()

    any_spec = pl.BlockSpec(memory_space=pl.ANY)
    return pl.pallas_call(
        body, name=name,
        out_shape=[jax.ShapeDtypeStruct((N_DEV,) + a.shape, a.dtype) for a in arrays],
        in_specs=[any_spec] * n_arr, out_specs=[any_spec] * n_arr,
        scratch_shapes=[pltpu.SemaphoreType.DMA((n_arr, 7)), pltpu.SemaphoreType.DMA((n_arr, 7)),
                        pltpu.SemaphoreType.DMA((n_arr,))],
    )(*arrays)


def _rs_stage1(gbs, name):
    n_arr = len(gbs)

    def body(*refs):
        srcs = refs[:n_arr]
        outs = refs[n_arr:2 * n_arr]
        send_sems, recv_sems = refs[2 * n_arr:]
        x, y, c = lax.axis_index("x"), lax.axis_index("y"), lax.axis_index("c")
        sibling = (x, y, 1 - c)
        copies = []
        for a in range(n_arr):
            for q in range(4):
                dev = 4 * (q // 2) + 2 * (q % 2) + (1 - c)
                copies.append(pltpu.make_async_remote_copy(
                    src_ref=srcs[a].at[dev], dst_ref=outs[a].at[q], send_sem=send_sems.at[a, q],
                    recv_sem=recv_sems.at[a, q], device_id=sibling, device_id_type=MESH))
        for cp in copies:
            cp.start()
        for cp in copies:
            cp.wait_recv()
        for cp in copies:
            cp.wait_send()

    any_spec = pl.BlockSpec(memory_space=pl.ANY)
    return pl.pallas_call(
        body, name=name,
        out_shape=[jax.ShapeDtypeStruct((4,) + g.shape[1:], g.dtype) for g in gbs],
        in_specs=[any_spec] * n_arr, out_specs=[any_spec] * n_arr,
        scratch_shapes=[pltpu.SemaphoreType.DMA((n_arr, 4)), pltpu.SemaphoreType.DMA((n_arr, 4))],
    )(*gbs)


def _rs_stage2(pbs, name):
    n_arr = len(pbs)

    def body(*refs):
        srcs = refs[:n_arr]
        outs = refs[n_arr:2 * n_arr]
        send_sems, recv_sems = refs[2 * n_arr:]
        x, y, c = lax.axis_index("x"), lax.axis_index("y"), lax.axis_index("c")
        chips = [(1 - x, y), (x, 1 - y), (1 - x, 1 - y)]
        copies = []
        for a in range(n_arr):
            for j, chip in enumerate(chips):
                copies.append(pltpu.make_async_remote_copy(
                    src_ref=srcs[a].at[2 * chip[0] + chip[1]], dst_ref=outs[a].at[j], send_sem=send_sems.at[a, j],
                    recv_sem=recv_sems.at[a, j], device_id=(*chip, c), device_id_type=MESH))
        for cp in copies:
            cp.start()
        for cp in copies:
            cp.wait_recv()
        for cp in copies:
            cp.wait_send()

    any_spec = pl.BlockSpec(memory_space=pl.ANY)
    return pl.pallas_call(
        body, name=name,
        out_shape=[jax.ShapeDtypeStruct((3,) + p.shape[1:], p.dtype) for p in pbs],
        in_specs=[any_spec] * n_arr, out_specs=[any_spec] * n_arr,
        scratch_shapes=[pltpu.SemaphoreType.DMA((n_arr, 3)), pltpu.SemaphoreType.DMA((n_arr, 3))],
    )(*pbs)


def _rs_sum1(pos, g, r1, name):
    _, rows, d = g.shape
    tr = rows
    for cand in (512, 352, 256, 160, 128):
        if rows % cand == 0:
            tr = cand
            break

    def body(pos_ref, g_ref, r_ref, p_ref, pb_ref):
        p = g_ref[...] + r_ref[...].astype(F32)
        p_ref[...] = p
        pb_ref[...] = p.astype(BF16)

    def g_map(q, i, pos_ref):
        return (4 * (q // 2) + 2 * (q % 2) + pos_ref[2], i, 0)

    blk = lambda q, i, pos_ref: (q, i, 0)
    return pl.pallas_call(
        body, name=name,
        grid_spec=pltpu.PrefetchScalarGridSpec(
            num_scalar_prefetch=1, grid=(4, rows // tr),
            in_specs=[pl.BlockSpec((None, tr, d), g_map), pl.BlockSpec((None, tr, d), blk)],
            out_specs=[pl.BlockSpec((None, tr, d), blk), pl.BlockSpec((None, tr, d), blk)]),
        out_shape=[jax.ShapeDtypeStruct((4, rows, d), F32), jax.ShapeDtypeStruct((4, rows, d), BF16)],
        compiler_params=_cparams(("arbitrary", "arbitrary")),
    )(pos, g, r1)


def _rs_sum2(pos, p, r2, name):
    _, rows, d = p.shape
    tr = rows
    for cand in (512, 352, 256, 160, 128):
        if rows % cand == 0:
            tr = cand
            break

    def body(pos_ref, p_ref, r_ref, o_ref):
        r = r_ref[...].astype(F32)
        o_ref[...] = ((p_ref[...] + r[0]) + r[1]) + r[2]

    return pl.pallas_call(
        body, name=name,
        grid_spec=pltpu.PrefetchScalarGridSpec(
            num_scalar_prefetch=1, grid=(rows // tr,),
            in_specs=[pl.BlockSpec((None, tr, d), lambda i, pos_ref: (2 * pos_ref[0] + pos_ref[1], i, 0)),
                      pl.BlockSpec((3, tr, d), lambda i, pos_ref: (0, i, 0))],
            out_specs=pl.BlockSpec((tr, d), lambda i, pos_ref: (i, 0))),
        out_shape=jax.ShapeDtypeStruct((rows, d), F32),
        compiler_params=_cparams(("arbitrary",)),
    )(pos, p, r2)


def _ffn_chunk(f):
    for cand in (256, 128):
        if f % cand == 0:
            return cand
    return f


def _ffn_fwd(h, g, wg_t, wu_t, wd, name, tm):
    s, d = h.shape
    f = wd.shape[0]
    tf = _ffn_chunk(f)

    def body(h_ref, g_ref, wg_ref, wu_ref, wd_ref, o_ref, n_ref, gate_ref, up_ref):
        x = h_ref[...]
        r = lax.rsqrt(jnp.mean(x * x, axis=-1, keepdims=True) + EPS)
        nb = (x * r * g_ref[...]).astype(BF16)
        n_ref[...] = nb
        acc = jnp.zeros((tm, d), F32)
        for j in range(f // tf):
            sl = slice(j * tf, (j + 1) * tf)
            gate = _dot_nt(nb, wg_ref[sl, :])
            up = _dot_nt(nb, wu_ref[sl, :])
            gate_ref[:, sl] = gate.astype(BF16)
            up_ref[:, sl] = up.astype(BF16)
            act = (gate * _sigmoid(gate) * up).astype(BF16)
            acc = acc + _dot_nn(act, wd_ref[sl, :])
        o_ref[...] = x + 0.5 * acc

    row = lambda w: pl.BlockSpec((tm, w), lambda i: (i, 0))
    return pl.pallas_call(
        body, name=name, grid=(s // tm,),
        in_specs=[row(d), _whole((1, d)), _whole((f, d)), _whole((f, d)), _whole((f, d))],
        out_specs=[row(d), row(d), row(f), row(f)],
        out_shape=[jax.ShapeDtypeStruct((s, d), F32), jax.ShapeDtypeStruct((s, d), BF16),
                   jax.ShapeDtypeStruct((s, f), BF16), jax.ShapeDtypeStruct((s, f), BF16)],
        compiler_params=_cparams(("arbitrary",)),
    )(h, g, wg_t, wu_t, wd)


def _ffn_bwd(h_in, dh_out, gate, up, g, wg_t, wu_t, wd, name, tm):
    s, d = h_in.shape
    f = gate.shape[1]
    tf = _ffn_chunk(f)

    def body(h_ref, dh_ref, gate_ref, up_ref, g_ref, wg_ref, wu_ref, wd_ref,
             o_ref, dg_ref, dgate_ref, dup_ref, act_ref, dhh_ref):
        x = h_ref[...]
        dh = dh_ref[...]
        r = lax.rsqrt(jnp.mean(x * x, axis=-1, keepdims=True) + EPS)
        xhat = x * r
        dhh = (0.5 * dh).astype(BF16)
        dhh_ref[...] = dhh
        dn = jnp.zeros((tm, d), F32)
        for j in range(f // tf):
            sl = slice(j * tf, (j + 1) * tf)
            gt = gate_ref[:, sl].astype(F32)
            u = up_ref[:, sl].astype(F32)
            dact = _dot_nt(dhh, wd_ref[sl, :])
            sg = _sigmoid(gt)
            silu = gt * sg
            act_ref[:, sl] = (silu * u).astype(BF16)
            dup = (dact * silu).astype(BF16)
            dgate = (dact * u * (sg * (1.0 + gt * (1.0 - sg)))).astype(BF16)
            dup_ref[:, sl] = dup
            dgate_ref[:, sl] = dgate
            dn = dn + _dot_nn(dgate, wg_ref[sl, :]) + _dot_nn(dup, wu_ref[sl, :])
        dxhat = dn * g_ref[...]
        o_ref[...] = dh + r * (dxhat - xhat * jnp.mean(dxhat * xhat, axis=-1, keepdims=True))

        @pl.when(pl.program_id(0) == 0)
        def _():
            dg_ref[...] = jnp.zeros_like(dg_ref)

        dg_ref[...] += jnp.sum(dn * xhat, axis=0, keepdims=True)

    row = lambda w: pl.BlockSpec((tm, w), lambda i: (i, 0))
    return pl.pallas_call(
        body, name=name, grid=(s // tm,),
        in_specs=[row(d), row(d), row(f), row(f), _whole((1, d)), _whole((f, d)), _whole((f, d)), _whole((f, d))],
        out_specs=[row(d), pl.BlockSpec((1, d), lambda i: (0, 0)), row(f), row(f), row(f), row(d)],
        out_shape=[jax.ShapeDtypeStruct((s, d), F32), jax.ShapeDtypeStruct((1, d), F32),
                   jax.ShapeDtypeStruct((s, f), BF16), jax.ShapeDtypeStruct((s, f), BF16),
                   jax.ShapeDtypeStruct((s, f), BF16), jax.ShapeDtypeStruct((s, d), BF16)],
        compiler_params=_cparams(("arbitrary",)),
    )(h_in, dh_out, gate, up, g, wg_t, wu_t, wd)


def _wgrad(a, b, name, tf, tk):
    s, f = a.shape
    d = b.shape[1]
    nk = s // tk

    def body(a_ref, b_ref, o_ref, ob_ref):
        k = pl.program_id(1)

        @pl.when(k == 0)
        def _():
            o_ref[...] = jnp.zeros_like(o_ref)

        o_ref[...] += _dot_tn(a_ref[...], b_ref[...])

        @pl.when(k == nk - 1)
        def _():
            ob_ref[...] = o_ref[...].astype(BF16)

    return pl.pallas_call(
        body, name=name, grid=(f // tf, nk),
        in_specs=[pl.BlockSpec((tk, tf), lambda i, k: (k, i)), pl.BlockSpec((tk, d), lambda i, k: (k, 0))],
        out_specs=[pl.BlockSpec((tf, d), lambda i, k: (i, 0)), pl.BlockSpec((tf, d), lambda i, k: (i, 0))],
        out_shape=[jax.ShapeDtypeStruct((f, d), F32), jax.ShapeDtypeStruct((f, d), BF16)],
        compiler_params=_cparams(("arbitrary", "arbitrary")),
    )(a, b)


def _rope(t, c, sa, sb, reps):
    c, sa, sb = (jnp.tile(v, (1, reps)) if reps > 1 else v for v in (c, sa, sb))
    w = t.shape[1]
    return t * c + pltpu.roll(t, w - 8, 1) * sa + pltpu.roll(t, 8, 1) * sb


def _rope_bwd(dt, c, sa, sb, reps):
    c, sa, sb = (jnp.tile(v, (1, reps)) if reps > 1 else v for v in (c, sa, sb))
    w = dt.shape[1]
    return dt * c + pltpu.roll(dt * sa, 8, 1) + pltpu.roll(dt * sb, w - 8, 1)


def _mix_in(h, g, win_t, tabs, name, tm):
    s, d = h.shape
    n_in = win_t.shape[0]

    def body(h_ref, g_ref, w_ref, c_ref, sa_ref, sb_ref, q_ref, k_ref, v_ref, pc_ref, n_ref):
        x = h_ref[...]
        r = lax.rsqrt(jnp.mean(x * x, axis=-1, keepdims=True) + EPS)
        nb = (x * r * g_ref[...]).astype(BF16)
        n_ref[...] = nb
        u = _dot_nt(nb, w_ref[...])
        c, sa, sb = c_ref[...], sa_ref[...], sb_ref[...]
        q_ref[...] = _rope(u[:, :ATTN_W], c, sa, sb, ATTN_W // 128).astype(BF16)
        k_ref[...] = _rope(u[:, ATTN_W:ATTN_W + KV_W], c, sa, sb, 1).astype(BF16)
        v_ref[...] = u[:, ATTN_W + KV_W:ATTN_W + 2 * KV_W].astype(BF16)
        pc_ref[...] = u[:, ATTN_W + 2 * KV_W:]

    row = lambda w: pl.BlockSpec((tm, w), lambda i: (i, 0))
    return pl.pallas_call(
        body, name=name, grid=(s // tm,),
        in_specs=[row(d), _whole((1, d)), _whole((n_in, d)), row(128), row(128), row(128)],
        out_specs=[row(ATTN_W), row(KV_W), row(KV_W), row(POOL_W), row(d)],
        out_shape=[jax.ShapeDtypeStruct((s, ATTN_W), BF16), jax.ShapeDtypeStruct((s, KV_W), BF16),
                   jax.ShapeDtypeStruct((s, KV_W), BF16), jax.ShapeDtypeStruct((s, POOL_W), F32),
                   jax.ShapeDtypeStruct((s, d), BF16)],
        compiler_params=_cparams(("arbitrary",)),
    )(h, g, win_t, *tabs)


def _band_mask(n, nb):
    i = lax.broadcasted_iota(jnp.int32, (BLK, 3 * BLK), 0)
    j = lax.broadcasted_iota(jnp.int32, (BLK, 3 * BLK), 1)
    kpos = (n - 1) * BLK + j
    return (j >= i) & (j <= i + 2 * BLK) & (kpos >= 0) & (kpos < nb * BLK)


def _softmax_band(qh, kb, valid, sink):
    sc = _dot_nt(qh, kb)
    sc = jnp.where(valid, sc, -1e30)
    m = jnp.maximum(jnp.max(sc, axis=-1, keepdims=True), sink)
    e = jnp.exp(sc - m)
    es = jnp.exp(sink - m)
    inv = 1.0 / (jnp.sum(e, axis=-1, keepdims=True) + es)
    return e * inv, es * inv


def _attn_pool_fwd(q, k, v, pc, sink, pool_w, pool_scale, pband, name):
    s = q.shape[0]
    nb = s // BLK

    def body(sink_ref, q_ref, k0, k1, k2, v0, v1, v2, p0, p1, p2, pw_ref, ps_ref, pb_ref, o_ref):
        n = pl.program_id(0)
        valid = _band_mask(n, nb)
        kb = jnp.concatenate([k0[...], k1[...], k2[...]], axis=0)
        vb = jnp.concatenate([v0[...], v1[...], v2[...]], axis=0)
        qs = q_ref[...] * SCORE_SCALE
        for h in range(N_HEADS):
            kh = h // GROUP
            ksl = slice(kh * HEAD_DIM, (kh + 1) * HEAD_DIM)
            p, _ = _softmax_band(qs[:, h * HEAD_DIM:(h + 1) * HEAD_DIM], kb[:, ksl], valid, sink_ref[0, h])
            o_ref[:, h * HEAD_DIM:(h + 1) * HEAD_DIM] = _dot_nn(p.astype(BF16), vb[:, ksl]).astype(BF16)
        ext = jnp.concatenate([p0[...], p1[...], p2[...]], axis=0).astype(BF16)
        for gi in range(POOL_G):
            gsl = slice(gi * POOL_GW, (gi + 1) * POOL_GW)
            dg = _dot_nn(pb_ref[gi], ext[:, gsl])
            yg = _dot_nn(dg.astype(BF16), pw_ref[gi].astype(BF16))
            o_ref[:, ATTN_W + gi * POOL_GW:ATTN_W + (gi + 1) * POOL_GW] = (yg * ps_ref[:, gsl]).astype(BF16)

    def band(w):
        return [pl.BlockSpec((BLK, w), lambda n: (jnp.maximum(n - 1, 0), 0)),
                pl.BlockSpec((BLK, w), lambda n: (n, 0)),
                pl.BlockSpec((BLK, w), lambda n: (jnp.minimum(n + 1, nb - 1), 0))]

    return pl.pallas_call(
        body, name=name, grid=(nb,),
        in_specs=[pl.BlockSpec(memory_space=pltpu.SMEM), pl.BlockSpec((BLK, ATTN_W), lambda n: (n, 0)),
                  *band(KV_W), *band(KV_W), *band(POOL_W),
                  _whole((POOL_G, POOL_GW, POOL_GW)), _whole((1, POOL_W)),
                  pl.BlockSpec((None, POOL_G, BLK, 3 * BLK), lambda n: (_variant_index(n, nb), 0, 0, 0))],
        out_specs=pl.BlockSpec((BLK, ATTN_W + POOL_W), lambda n: (n, 0)),
        out_shape=jax.ShapeDtypeStruct((s, ATTN_W + POOL_W), BF16),
        compiler_params=_cparams(("arbitrary",)),
    )(sink, q, k, k, k, v, v, v, pc, pc, pc, pool_w, pool_scale, pband)


def _attn_pool_bwd(q, k, v, pc, dmix, sink, pool_w, pool_scale, pband, ptband, name):
    s = q.shape[0]
    nb = s // BLK

    def body(sink_ref, q_ref, k0, k1, k2, v0, v1, v2, p0, p1, p2, da_ref, d0, d1, d2, pw_ref, ps_ref, pb_ref, ptb_ref,
             dq_ref, dk_ref, dv_ref, dpc_ref, dsink_ref, dpw_ref, dps_ref):
        n = pl.program_id(0)

        @pl.when(n == 0)
        def _():
            dsink_ref[...] = jnp.zeros_like(dsink_ref)
            dpw_ref[...] = jnp.zeros_like(dpw_ref)
            dps_ref[...] = jnp.zeros_like(dps_ref)

        valid = _band_mask(n, nb)
        kb = jnp.concatenate([k0[...], k1[...], k2[...]], axis=0)
        vb = jnp.concatenate([v0[...], v1[...], v2[...]], axis=0)
        qb = q_ref[...]
        qs = qb * SCORE_SCALE
        da = da_ref[...]
        for kh in range(N_KV):
            ksl = slice(kh * HEAD_DIM, (kh + 1) * HEAD_DIM)
            dk_acc = jnp.zeros((3 * BLK, HEAD_DIM), F32)
            dv_acc = jnp.zeros((3 * BLK, HEAD_DIM), F32)
            for hh in range(GROUP):
                h = kh * GROUP + hh
                hsl = slice(h * HEAD_DIM, (h + 1) * HEAD_DIM)
                p, ps = _softmax_band(qs[:, hsl], kb[:, ksl], valid, sink_ref[0, h])
                do = da[:, hsl].astype(BF16)
                dv_acc = dv_acc + _dot_tn(p.astype(BF16), do)
                dp = _dot_nt(do, vb[:, ksl])
                delta = jnp.sum(p * dp, axis=-1, keepdims=True)
                ds = (p * (dp - delta)).astype(BF16)
                dsink_ref[h:h + 1, :] += jnp.broadcast_to(jnp.sum(-ps * delta, axis=0, keepdims=True), (1, 128))
                dq_ref[:, hsl] = _dot_nn(ds, kb[:, ksl]) * SCORE_SCALE
                dk_acc = dk_acc + _dot_tn(ds, qb[:, hsl])
            for t in range(3):
                dk_ref[t, :, ksl] = dk_acc[t * BLK:(t + 1) * BLK] * SCORE_SCALE
                dv_ref[t, :, ksl] = dv_acc[t * BLK:(t + 1) * BLK]
        ext = jnp.concatenate([p0[...], p1[...], p2[...]], axis=0).astype(BF16)
        dpe = jnp.concatenate([d0[...], d1[...], d2[...]], axis=0)
        dpc_cur = d1[...]
        for gi in range(POOL_G):
            gsl = slice(gi * POOL_GW, (gi + 1) * POOL_GW)
            wg = pw_ref[gi].astype(BF16)
            sc = ps_ref[:, gsl]
            dgr = _dot_nn(pb_ref[gi], ext[:, gsl])
            dgb = dgr.astype(BF16)
            yg = _dot_nn(dgb, wg)
            dps_ref[:, gsl] += jnp.sum(dpc_cur[:, gsl] * yg, axis=0, keepdims=True)
            dpw_ref[gi] += _dot_tn(dgb, (dpc_cur[:, gsl] * sc).astype(BF16))
            dd = _dot_nt((dpe[:, gsl] * sc).astype(BF16), wg)
            dpc_ref[:, gsl] = _dot_nn(ptb_ref[gi], dd.astype(BF16))

    def band(w, col=0):
        return [pl.BlockSpec((BLK, w), lambda n: (jnp.maximum(n - 1, 0), col)),
                pl.BlockSpec((BLK, w), lambda n: (n, col)),
                pl.BlockSpec((BLK, w), lambda n: (jnp.minimum(n + 1, nb - 1), col))]

    tab = lambda: pl.BlockSpec((None, POOL_G, BLK, 3 * BLK), lambda n: (_variant_index(n, nb), 0, 0, 0))
    fixed = lambda shape: pl.BlockSpec(shape, lambda n: (0,) * len(shape))
    return pl.pallas_call(
        body, name=name, grid=(nb,),
        in_specs=[pl.BlockSpec(memory_space=pltpu.SMEM), pl.BlockSpec((BLK, ATTN_W), lambda n: (n, 0)),
                  *band(KV_W), *band(KV_W), *band(POOL_W),
                  pl.BlockSpec((BLK, ATTN_W), lambda n: (n, 0)), *band(POOL_W, 1),
                  _whole((POOL_G, POOL_GW, POOL_GW)), _whole((1, POOL_W)), tab(), tab()],
        out_specs=[pl.BlockSpec((BLK, ATTN_W), lambda n: (n, 0)),
                   pl.BlockSpec((None, 3, BLK, KV_W), lambda n: (n, 0, 0, 0)),
                   pl.BlockSpec((None, 3, BLK, KV_W), lambda n: (n, 0, 0, 0)),
                   pl.BlockSpec((BLK, POOL_W), lambda n: (n, 0)),
                   fixed((N_HEADS, 128)), fixed((POOL_G, POOL_GW, POOL_GW)), fixed((1, POOL_W))],
        out_shape=[jax.ShapeDtypeStruct((s, ATTN_W), F32), jax.ShapeDtypeStruct((nb, 3, BLK, KV_W), F32),
                   jax.ShapeDtypeStruct((nb, 3, BLK, KV_W), F32), jax.ShapeDtypeStruct((s, POOL_W), F32),
                   jax.ShapeDtypeStruct((N_HEADS, 128), F32),
                   jax.ShapeDtypeStruct((POOL_G, POOL_GW, POOL_GW), F32), jax.ShapeDtypeStruct((1, POOL_W), F32)],
        compiler_params=_cparams(("arbitrary",)),
    )(sink, q, k, k, k, v, v, v, pc, pc, pc, dmix, dmix, dmix, dmix, pool_w, pool_scale, pband, ptband)


def _mix_out(h, mix, w_out, name, tm):
    s, d = h.shape
    w = mix.shape[1]

    def body(h_ref, m_ref, w_ref, o_ref):
        o_ref[...] = h_ref[...] + _dot_nn(m_ref[...], w_ref[...])

    row = lambda c: pl.BlockSpec((tm, c), lambda i: (i, 0))
    return pl.pallas_call(
        body, name=name, grid=(s // tm,), in_specs=[row(d), row(w), _whole((w, d))], out_specs=row(d),
        out_shape=jax.ShapeDtypeStruct((s, d), F32), compiler_params=_cparams(("arbitrary",)),
    )(h, mix, w_out)


def _mix_out_bwd(dh, w_out, name, tm):
    s, d = dh.shape
    w = w_out.shape[0]

    def body(dh_ref, w_ref, o_ref, dhb_ref):
        dhb = dh_ref[...].astype(BF16)
        dhb_ref[...] = dhb
        o_ref[...] = _dot_nt(dhb, w_ref[...])

    row = lambda c: pl.BlockSpec((tm, c), lambda i: (i, 0))
    return pl.pallas_call(
        body, name=name, grid=(s // tm,), in_specs=[row(d), _whole((w, d))], out_specs=[row(w), row(d)],
        out_shape=[jax.ShapeDtypeStruct((s, w), F32), jax.ShapeDtypeStruct((s, d), BF16)],
        compiler_params=_cparams(("arbitrary",)),
    )(dh, w_out)


def _mix_in_bwd(h, dh, g, win_t, dq, dkp, dvp, dpc, tabs, name):
    s, d = h.shape
    nb = s // BLK
    n_in = win_t.shape[0]

    def body(h_ref, dh_ref, g_ref, w_ref, dq_ref, ka, kb_, kc, va, vb_, vc, dpc_ref, c_ref, sa_ref, sb_ref,
             o_ref, du_ref, dg_ref):
        n = pl.program_id(0)
        lo = (n > 0).astype(F32)
        hi = (n < nb - 1).astype(F32)
        dk = ka[...] * lo + kb_[...] + kc[...] * hi
        dv = va[...] * lo + vb_[...] + vc[...] * hi
        c, sa, sb = c_ref[...], sa_ref[...], sb_ref[...]
        du = jnp.concatenate([_rope_bwd(dq_ref[...], c, sa, sb, ATTN_W // 128), _rope_bwd(dk, c, sa, sb, 1), dv,
                              dpc_ref[...]], axis=1).astype(BF16)
        du_ref[...] = du
        dn = _dot_nn(du, w_ref[...])
        x = h_ref[...]
        r = lax.rsqrt(jnp.mean(x * x, axis=-1, keepdims=True) + EPS)
        xhat = x * r
        dxhat = dn * g_ref[...]
        o_ref[...] = dh_ref[...] + r * (dxhat - xhat * jnp.mean(dxhat * xhat, axis=-1, keepdims=True))

        @pl.when(n == 0)
        def _():
            dg_ref[...] = jnp.zeros_like(dg_ref)

        dg_ref[...] += jnp.sum(dn * xhat, axis=0, keepdims=True)

    row = lambda w: pl.BlockSpec((BLK, w), lambda n: (n, 0))
    parts = [pl.BlockSpec((None, None, BLK, KV_W), lambda n: (jnp.maximum(n - 1, 0), 2, 0, 0)),
             pl.BlockSpec((None, None, BLK, KV_W), lambda n: (n, 1, 0, 0)),
             pl.BlockSpec((None, None, BLK, KV_W), lambda n: (jnp.minimum(n + 1, nb - 1), 0, 0, 0))]
    return pl.pallas_call(
        body, name=name, grid=(nb,),
        in_specs=[row(d), row(d), _whole((1, d)), _whole((n_in, d)), row(ATTN_W), *parts, *parts, row(POOL_W),
                  row(128), row(128), row(128)],
        out_specs=[row(d), row(n_in), pl.BlockSpec((1, d), lambda n: (0, 0))],
        out_shape=[jax.ShapeDtypeStruct((s, d), F32), jax.ShapeDtypeStruct((s, n_in), BF16),
                   jax.ShapeDtypeStruct((1, d), F32)],
        compiler_params=_cparams(("arbitrary",)),
    )(h, dh, g, win_t, dq, dkp, dkp, dkp, dvp, dvp, dvp, dpc, *tabs)


def _final(h, g, target, name, tm):
    s, d = h.shape

    def body(h_ref, g_ref, t_ref, loss_ref, dh_ref, dg_ref):
        @pl.when(pl.program_id(0) == 0)
        def _():
            loss_ref[...] = jnp.zeros_like(loss_ref)
            dg_ref[...] = jnp.zeros_like(dg_ref)

        x = h_ref[...]
        gg = g_ref[...]
        r = lax.rsqrt(jnp.mean(x * x, axis=-1, keepdims=True) + EPS)
        xhat = x * r
        e = xhat * gg - t_ref[...]
        per_tok = jnp.mean(e * e, axis=-1, keepdims=True)
        loss_ref[...] += 0.5 * jnp.sum(per_tok, axis=0, keepdims=True)
        dy = e * (1.0 / d)
        dg_ref[...] += jnp.sum(dy * xhat, axis=0, keepdims=True)
        dxhat = dy * gg
        dh_ref[...] = r * (dxhat - xhat * jnp.mean(dxhat * xhat, axis=-1, keepdims=True))

    row = pl.BlockSpec((tm, d), lambda i: (i, 0))
    return pl.pallas_call(
        body, name=name, grid=(s // tm,), in_specs=[row, _whole((1, d)), row],
        out_specs=[pl.BlockSpec((1, 1), lambda i: (0, 0)), row, pl.BlockSpec((1, d), lambda i: (0, 0))],
        out_shape=[jax.ShapeDtypeStruct((1, 1), F32), jax.ShapeDtypeStruct((s, d), F32),
                   jax.ShapeDtypeStruct((1, d), F32)],
        compiler_params=_cparams(("arbitrary",)),
    )(h, g, target)


def _adam_math(w, g, m, v):
    m = ADAM_B1 * m + (1.0 - ADAM_B1) * g
    v = ADAM_B2 * v + (1.0 - ADAM_B2) * (g * g)
    m_hat = m / (1.0 - ADAM_B1 ** ADAM_STEP)
    v_hat = v / (1.0 - ADAM_B2 ** ADAM_STEP)
    delta = -ADAM_LR * (m_hat / (jnp.sqrt(v_hat) + ADAM_EPS) + ADAM_WD * w)
    return delta, m, v


def _adam(w, g, m, v, name):
    rows, cols = w.shape
    tr = rows
    for cand in (256, 128, 64, 32, 16, 8):
        if rows % cand == 0 and rows > cand:
            tr = cand
            break

    def body(w_ref, g_ref, m_ref, v_ref, d_ref, nm_ref, nv_ref):
        d_ref[...], nm_ref[...], nv_ref[...] = _adam_math(w_ref[...], g_ref[...], m_ref[...], v_ref[...])

    blk = pl.BlockSpec((tr, cols), lambda i: (i, 0))
    return pl.pallas_call(
        body, name=name, grid=(rows // tr,), in_specs=[blk] * 4, out_specs=[blk] * 3,
        out_shape=[jax.ShapeDtypeStruct((rows, cols), F32)] * 3, compiler_params=_cparams(("arbitrary",)),
    )(w, g, m, v)


def _adam_small(w, parts, m, v, name):
    rows, cols = w.shape

    def body(w_ref, p_ref, m_ref, v_ref, g_ref, d_ref, nm_ref, nv_ref):
        g = p_ref[0]
        for k in range(1, N_DEV):
            g = g + p_ref[k]
        g_ref[...] = g
        d_ref[...], nm_ref[...], nv_ref[...] = _adam_math(w_ref[...], g, m_ref[...], v_ref[...])

    return pl.pallas_call(
        body, name=name, out_shape=[jax.ShapeDtypeStruct((rows, cols), F32)] * 4,
    )(w, parts, m, v)


def _pack_small(norm1, normm, norm2, normf, sink, pool_w, pool_scale):
    scale_rows = jnp.pad(pool_scale.reshape(4, 128), ((0, 4), (0, 0)))
    sink_rows = jnp.pad(sink.reshape(1, N_HEADS), ((0, 7), (0, 128 - N_HEADS)))
    return jnp.concatenate([pool_w.reshape(512, 128), norm1.reshape(8, 128), normm.reshape(8, 128),
                            norm2.reshape(8, 128), normf.reshape(8, 128), scale_rows, sink_rows], axis=0)


def _unpack_small(p):
    return dict(pool_w=p[:512].reshape(1, POOL_G, POOL_GW, POOL_GW), ffn1_norm=p[512:520].reshape(1, 1024),
                mix_norm=p[520:528].reshape(1, 1024), ffn2_norm=p[528:536].reshape(1, 1024),
                final_norm=p[536:544].reshape(1024), pool_scale=p[544:548].reshape(1, POOL_W),
                sink_logits=p[552, :N_HEADS].reshape(1, N_HEADS))


def kernel(x, ffn1_norm, ffn1_w_gate, ffn1_w_up, ffn1_w_down, mix_norm, w_in, sink_logits, pool_w, pool_scale, w_out, ffn2_norm, ffn2_w_gate, ffn2_w_up, ffn2_w_down, final_norm, loss_target, m_ffn1_norm, m_ffn1_w_gate, m_ffn1_w_up, m_ffn1_w_down, m_mix_norm, m_w_in, m_sink_logits, m_pool_w, m_pool_scale, m_w_out, m_ffn2_norm, m_ffn2_w_gate, m_ffn2_w_up, m_ffn2_w_down, m_final_norm, v_ffn1_norm, v_ffn1_w_gate, v_ffn1_w_up, v_ffn1_w_down, v_mix_norm, v_w_in, v_sink_logits, v_pool_w, v_pool_scale, v_w_out, v_ffn2_norm, v_ffn2_w_gate, v_ffn2_w_up, v_ffn2_w_down, v_final_norm):
    s, d = x.shape[1], x.shape[2]
    fk = ffn1_w_gate.shape[2]
    f = N_DEV * fk
    ink = w_in.shape[2]
    n_in = N_DEV * ink
    mixk = w_out.shape[1]
    tm = min(512, s)
    tm_bwd = min(256, s)
    pos = jnp.stack([lax.axis_index("x"), lax.axis_index("y"), lax.axis_index("c")]).astype(jnp.int32)

    t_bf = lambda w: w[0].T.astype(BF16)
    local = [t_bf(ffn1_w_gate), t_bf(ffn1_w_up), ffn1_w_down[0].astype(BF16), t_bf(ffn2_w_gate), t_bf(ffn2_w_up),
             ffn2_w_down[0].astype(BF16), t_bf(w_in), w_out[0].astype(BF16)]
    full = [a.reshape(N_DEV * a.shape[1], d) for a in _all_gather(local, "gather_weights")]
    wg1, wu1, wd1, wg2, wu2, wd2, win_t, wout = full

    tabs = _rope_tables(s)
    pband, ptband = _pool_tables(s)
    g1, gm, g2, gf = ffn1_norm, mix_norm, ffn2_norm, final_norm.reshape(1, d)

    x0 = x[0]
    h1, n1, gate1, up1 = _ffn_fwd(x0, g1, wg1, wu1, wd1, "ffn1_fwd", tm)
    q, k, v, pc, n2 = _mix_in(h1, gm, win_t, tabs, "mix_in", tm)
    mix = _attn_pool_fwd(q, k, v, pc, sink_logits, pool_w[0], pool_scale, pband, "attn_pool_fwd")
    h2 = _mix_out(h1, mix, wout, "mix_out", tm)
    h3, n3, gate2, up2 = _ffn_fwd(h2, g2, wg2, wu2, wd2, "ffn2_fwd", tm)
    loss_part, dh3, dgf = _final(h3, gf, loss_target[0], "final", tm)

    tk = min(512, s)
    f_half = f // 2 if (f // 2) % 128 == 0 else f
    dh2, dg2, dgate2, dup2, act2, dhh3 = _ffn_bwd(h2, dh3, gate2, up2, g2, wg2, wu2, wd2, "ffn2_bwd", tm_bwd)
    gw = {}
    gw["g2"] = _wgrad(dgate2, n3, "wgrad_gate2", f_half, tk)
    gw["u2"] = _wgrad(dup2, n3, "wgrad_up2", f_half, tk)
    gw["d2"] = _wgrad(act2, dhh3, "wgrad_down2", f_half, tk)
    dmix, dh2b = _mix_out_bwd(dh2, wout, "mix_out_bwd", tm)
    gw["out"] = _wgrad(mix, dh2b, "wgrad_out", mix.shape[1], tk)
    dq, dkp, dvp, dpc, dsink, dpw, dps = _attn_pool_bwd(q, k, v, pc, dmix, sink_logits, pool_w[0], pool_scale,
                                                        pband, ptband, "attn_pool_bwd")
    dh1, du, dgm = _mix_in_bwd(h1, dh2, gm, win_t, dq, dkp, dvp, dpc, tabs, "mix_in_bwd")
    gw["in"] = _wgrad(du, n2, "wgrad_in", n_in, tk)
    dx, dg1, dgate1, dup1, act1, dhh1 = _ffn_bwd(x0, dh1, gate1, up1, g1, wg1, wu1, wd1, "ffn1_bwd", tm_bwd)
    gw["g1"] = _wgrad(dgate1, n1, "wgrad_gate1", f_half, tk)
    gw["u1"] = _wgrad(dup1, n1, "wgrad_up1", f_half, tk)
    gw["d1"] = _wgrad(act1, dhh1, "wgrad_down1", f_half, tk)

    order = ["g1", "u1", "d1", "g2", "u2", "d2"]
    shard = lambda a, rows: a.reshape(N_DEV, rows, d)
    g_f32 = [shard(gw[n][0], fk) for n in order] + [shard(gw["in"][0], ink), shard(gw["out"][0], mixk)]
    g_b16 = [shard(gw[n][1], fk) for n in order] + [shard(gw["in"][1], ink), shard(gw["out"][1], mixk)]
    r1 = _rs_stage1(g_b16, "rs_stage1")
    sums1 = [_rs_sum1(pos, g, r, f"rs_sum1_{i}") for i, (g, r) in enumerate(zip(g_f32, r1))]
    r2 = _rs_stage2([sb for _, sb in sums1], "rs_stage2")
    grads = [_rs_sum2(pos, p, r, f"rs_sum2_{i}") for i, ((p, _), r) in enumerate(zip(sums1, r2))]

    small_part = _pack_small(dg1, dgm, dg2, dgf, dsink[:, 0], dpw, dps)
    (small_all,) = _all_gather([small_part], "gather_small_grads")
    pk = lambda a, b, c_, e, s_, pw_, psc: _pack_small(a, b, c_, e, s_[0], pw_[0], psc)
    small_w = pk(ffn1_norm, mix_norm, ffn2_norm, final_norm, sink_logits, pool_w, pool_scale)
    small_m = pk(m_ffn1_norm, m_mix_norm, m_ffn2_norm, m_final_norm, m_sink_logits, m_pool_w, m_pool_scale)
    small_v = pk(v_ffn1_norm, v_mix_norm, v_ffn2_norm, v_final_norm, v_sink_logits, v_pool_w, v_pool_scale)
    sg, sd, sm, sv = [_unpack_small(a) for a in _adam_small(small_w, small_all, small_m, small_v, "adam_small")]

    big = {}
    names = ["ffn1_w_gate", "ffn1_w_up", "ffn1_w_down", "ffn2_w_gate", "ffn2_w_up", "ffn2_w_down", "w_in", "w_out"]
    transposed = [True, True, False, True, True, False, True, False]
    ws = [ffn1_w_gate, ffn1_w_up, ffn1_w_down, ffn2_w_gate, ffn2_w_up, ffn2_w_down, w_in, w_out]
    ms = [m_ffn1_w_gate, m_ffn1_w_up, m_ffn1_w_down, m_ffn2_w_gate, m_ffn2_w_up, m_ffn2_w_down, m_w_in, m_w_out]
    vs = [v_ffn1_w_gate, v_ffn1_w_up, v_ffn1_w_down, v_ffn2_w_gate, v_ffn2_w_up, v_ffn2_w_down, v_w_in, v_w_out]
    for nm, tr, g, w, m, vv in zip(names, transposed, grads, ws, ms, vs):
        g = g.T if tr else g
        dl, nm_, nv_ = _adam(w[0], g, m[0], vv[0], "adam_" + nm)
        big[nm] = (g[None], dl[None], nm_[None], nv_[None])

    loss = lax.psum(loss_part[0, 0], ("x", "y", "c"))
    all_names = ["ffn1_norm", "ffn1_w_gate", "ffn1_w_up", "ffn1_w_down", "mix_norm", "w_in", "sink_logits", "pool_w",
                 "pool_scale", "w_out", "ffn2_norm", "ffn2_w_gate", "ffn2_w_up", "ffn2_w_down", "final_norm"]
    outs = [loss, dx[None]]
    for idx, src in enumerate((sg, sd, sm, sv)):
        for nm in all_names:
            outs.append(big[nm][idx] if nm in big else src[nm])
    return tuple(outs)
```

```python
import functools

import jax
import jax.numpy as jnp
import numpy as np
from jax import lax
from jax.experimental import pallas as pl
from jax.experimental.pallas import tpu as pltpu

F32 = jnp.float32
BF16 = jnp.bfloat16
MESH = pl.DeviceIdType.MESH
N_DEV = 8

EPS = 1e-6
HEAD_DIM = 64
N_HEADS = 8
N_KV = 2
GROUP = N_HEADS // N_KV
ATTN_W = N_HEADS * HEAD_DIM
KV_W = N_KV * HEAD_DIM
POOL_W = 512
POOL_G = 4
POOL_GW = POOL_W // POOL_G
POOL_WINDOWS = (2, 4, 8, 16)
BLK = 128
ROT = 16
ROPE_THETA = 500000.0
SCORE_SCALE = HEAD_DIM ** -0.5

ADAM_LR, ADAM_B1, ADAM_B2, ADAM_EPS, ADAM_WD, ADAM_STEP = 0.001, 0.9, 0.999, 1e-08, 0.01, 10

VMEM_LIMIT = 56 * 1024 * 1024


def _cparams(sem=None, **kw):
    if sem is not None:
        kw["dimension_semantics"] = sem
    return pltpu.CompilerParams(vmem_limit_bytes=VMEM_LIMIT, **kw)


def _whole(shape):
    nd = len(shape)
    return pl.BlockSpec(shape, lambda *_: (0,) * nd, pipeline_mode=pl.Buffered(1))


def _sigmoid(z):
    return 1.0 / (1.0 + jnp.exp(-z))


def _dot_nt(a, b):
    return lax.dot_general(a, b, (((1,), (1,)), ((), ())), preferred_element_type=F32)


def _dot_nn(a, b):
    return lax.dot_general(a, b, (((1,), (0,)), ((), ())), preferred_element_type=F32)


def _dot_tn(a, b):
    return lax.dot_general(a, b, (((0,), (0,)), ((), ())), preferred_element_type=F32)


def _rope_tables(s):
    inv_freq = ROPE_THETA ** (-np.arange(0, ROT, 2, dtype=np.float64) / ROT)
    ang = np.arange(s, dtype=np.float64)[:, None] * inv_freq[None, :]
    c = np.ones((s, HEAD_DIM)); sa = np.zeros((s, HEAD_DIM)); sb = np.zeros((s, HEAD_DIM))
    c[:, :8] = np.cos(ang); c[:, 8:16] = np.cos(ang)
    sa[:, :8] = -np.sin(ang)
    sb[:, 8:16] = np.sin(ang)
    t = lambda a: jnp.asarray(np.tile(a, (1, 2)).astype(np.float32))
    return t(c), t(sa), t(sb)


def _pool_weight(gi, t, s_pos, s):
    half = POOL_WINDOWS[gi] // 2

    def win(lo, hi):
        a = np.clip(lo, 0, s); b = np.clip(hi + 1, 0, s)
        inside = (s_pos >= a) & (s_pos < b)
        return inside / np.maximum(b - a, 1)

    w = 0.5 * (win(t - half, t + half - 1) + win(t - half + 1, t + half)) - (t == s_pos)
    return w * ((t >= 0) & (t < s) & (s_pos >= 0) & (s_pos < s))


def _pool_tables(s):
    nb = s // BLK
    fwd = np.zeros((3, POOL_G, BLK, 3 * BLK), np.float32)
    bwd = np.zeros((3, POOL_G, BLK, 3 * BLK), np.float32)
    for vi, n in enumerate((0, 1 if nb > 2 else 0, nb - 1)):
        i = n * BLK + np.arange(BLK)[:, None]
        j = (n - 1) * BLK + np.arange(3 * BLK)[None, :]
        for gi in range(POOL_G):
            fwd[vi, gi] = _pool_weight(gi, i, j, s)
            bwd[vi, gi] = _pool_weight(gi, j, i, s)
    return jnp.asarray(fwd, dtype=BF16), jnp.asarray(bwd, dtype=BF16)


def _variant_index(n, nb):
    return jnp.where(n == 0, 0, jnp.where(n == nb - 1, 2, 1))


class _Exchange:
    inputs = ()
    out_shapes = ()
    sems = ()

    def start(self, srcs, outs, sems):
        raise NotImplementedError

    def finish(self, srcs, outs, sems):
        raise NotImplementedError


class _AllGather(_Exchange):
    def __init__(self, arrays):
        n = len(arrays)
        self.inputs = list(arrays)
        self.out_shapes = [jax.ShapeDtypeStruct((N_DEV,) + a.shape, a.dtype) for a in arrays]
        self.sems = [pltpu.SemaphoreType.DMA((n, 7)), pltpu.SemaphoreType.DMA((n, 7)), pltpu.SemaphoreType.DMA((n,))]

    def _parts(self, srcs, outs, sems):
        send_sems, recv_sems, local_sems = sems
        n = len(srcs)
        x, y, c = lax.axis_index("x"), lax.axis_index("y"), lax.axis_index("c")
        me, sibling = (x, y, c), (x, y, 1 - c)
        chips = [(1 - x, y), (x, 1 - y), (1 - x, 1 - y)]

        def slot(a, dev):
            return outs[a].at[4 * dev[0] + 2 * dev[1] + dev[2]]

        def copy(a, k, block, to, src=None):
            return pltpu.make_async_remote_copy(
                src_ref=slot(a, block) if src is None else src, dst_ref=slot(a, block),
                send_sem=send_sems.at[a, k], recv_sem=recv_sems.at[a, k], device_id=to, device_id_type=MESH)

        mine = [pltpu.make_async_copy(srcs[a], slot(a, me), local_sems.at[a]) for a in range(n)]
        first = []
        for a in range(n):
            first.append(copy(a, 0, me, sibling, src=srcs[a]))
            first += [copy(a, 1 + j, me, (*chip, c), src=srcs[a]) for j, chip in enumerate(chips)]
        return n, c, me, sibling, chips, copy, mine, first

    def start(self, srcs, outs, sems):
        _, _, _, _, _, _, mine, first = self._parts(srcs, outs, sems)
        for cp in mine + first:
            cp.start()

    def finish(self, srcs, outs, sems):
        n, c, me, sibling, chips, copy, mine, first = self._parts(srcs, outs, sems)
        passed = []
        for j, chip in enumerate(chips):
            for a in range(n):
                copy(a, 1 + j, (*chip, c), me).wait_recv()
                fwd = copy(a, 4 + j, (*chip, c), sibling)
                fwd.start()
                passed.append(fwd)
        for a in range(n):
            copy(a, 0, sibling, me).wait_recv()
            for j, chip in enumerate(chips):
                copy(a, 4 + j, (*chip, 1 - c), me).wait_recv()
        for cp in first + passed:
            cp.wait_send()
        for cp in mine:
            cp.wait()


class _RsStage1(_Exchange):
    def __init__(self, gbs):
        n = len(gbs)
        self.inputs = list(gbs)
        self.out_shapes = [jax.ShapeDtypeStruct((4,) + g.shape[1:], g.dtype) for g in gbs]
        self.sems = [pltpu.SemaphoreType.DMA((n, 4)), pltpu.SemaphoreType.DMA((n, 4))]

    def _copies(self, srcs, outs, sems):
        send_sems, recv_sems = sems
        x, y, c = lax.axis_index("x"), lax.axis_index("y"), lax.axis_index("c")
        return [pltpu.make_async_remote_copy(
            src_ref=srcs[a].at[4 * (q // 2) + 2 * (q % 2) + (1 - c)], dst_ref=outs[a].at[q],
            send_sem=send_sems.at[a, q], recv_sem=recv_sems.at[a, q], device_id=(x, y, 1 - c), device_id_type=MESH)
            for a in range(len(srcs)) for q in range(4)]

    def start(self, srcs, outs, sems):
        for cp in self._copies(srcs, outs, sems):
            cp.start()

    def finish(self, srcs, outs, sems):
        copies = self._copies(srcs, outs, sems)
        for cp in copies:
            cp.wait_recv()
        for cp in copies:
            cp.wait_send()


class _RsStage2(_RsStage1):
    def __init__(self, pbs):
        n = len(pbs)
        self.inputs = list(pbs)
        self.out_shapes = [jax.ShapeDtypeStruct((3,) + p.shape[1:], p.dtype) for p in pbs]
        self.sems = [pltpu.SemaphoreType.DMA((n, 3)), pltpu.SemaphoreType.DMA((n, 3))]

    def _copies(self, srcs, outs, sems):
        send_sems, recv_sems = sems
        x, y, c = lax.axis_index("x"), lax.axis_index("y"), lax.axis_index("c")
        chips = [(1 - x, y), (x, 1 - y), (1 - x, 1 - y)]
        return [pltpu.make_async_remote_copy(
            src_ref=srcs[a].at[2 * chip[0] + chip[1]], dst_ref=outs[a].at[j], send_sem=send_sems.at[a, j],
            recv_sem=recv_sems.at[a, j], device_id=(*chip, c), device_id_type=MESH)
            for a in range(len(srcs)) for j, chip in enumerate(chips)]


_ANY = pl.BlockSpec(memory_space=pl.ANY)


def _run_exchange(ex, name):
    n_in, n_out = len(ex.inputs), len(ex.out_shapes)

    def body(*refs):
        srcs, outs, sems = refs[:n_in], refs[n_in:n_in + n_out], refs[n_in + n_out:]
        ex.start(srcs, outs, sems)
        ex.finish(srcs, outs, sems)

    return pl.pallas_call(
        body, name=name, out_shape=list(ex.out_shapes), in_specs=[_ANY] * n_in, out_specs=[_ANY] * n_out,
        scratch_shapes=list(ex.sems),
    )(*ex.inputs)


def _call(body, name, grid, in_specs, out_specs, out_shape, args, sem, carry=None, **kw):
    if carry is None:
        return pl.pallas_call(body, name=name, grid=grid, in_specs=in_specs, out_specs=out_specs, out_shape=out_shape,
                              compiler_params=_cparams(sem), **kw)(*args)
    n_in, n_out = len(in_specs), len(out_specs)
    nc_in, nc_out = len(carry.inputs), len(carry.out_shapes)

    def carried(*refs):
        ins = refs[:n_in]
        c_in = refs[n_in:n_in + nc_in]
        outs = refs[n_in + nc_in:n_in + nc_in + n_out]
        c_out = refs[n_in + nc_in + n_out:n_in + nc_in + n_out + nc_out]
        sems = refs[n_in + nc_in + n_out + nc_out:]
        ids = [pl.program_id(i) for i in range(len(grid))]
        is_first = functools.reduce(jnp.logical_and, [i == 0 for i in ids])
        is_last = functools.reduce(jnp.logical_and, [i == g - 1 for i, g in zip(ids, grid)])

        @pl.when(is_first)
        def _():
            carry.start(c_in, c_out, sems)

        body(*ins, *outs)

        @pl.when(is_last)
        def _():
            carry.finish(c_in, c_out, sems)

    return pl.pallas_call(
        carried, name=name, grid=grid, in_specs=list(in_specs) + [_ANY] * nc_in,
        out_specs=list(out_specs) + [_ANY] * nc_out, out_shape=list(out_shape) + list(carry.out_shapes),
        scratch_shapes=list(carry.sems), compiler_params=_cparams(sem), **kw)(*args, *carry.inputs)


def _rs_sum1(pos, g, r1, name):
    _, rows, d = g.shape
    tr = rows
    for cand in (512, 352, 256, 160, 128):
        if rows % cand == 0:
            tr = cand
            break

    def body(pos_ref, g_ref, r_ref, p_ref, pb_ref):
        p = g_ref[...] + r_ref[...].astype(F32)
        p_ref[...] = p
        pb_ref[...] = p.astype(BF16)

    def g_map(q, i, pos_ref):
        return (4 * (q // 2) + 2 * (q % 2) + pos_ref[2], i, 0)

    blk = lambda q, i, pos_ref: (q, i, 0)
    return pl.pallas_call(
        body, name=name,
        grid_spec=pltpu.PrefetchScalarGridSpec(
            num_scalar_prefetch=1, grid=(4, rows // tr),
            in_specs=[pl.BlockSpec((None, tr, d), g_map), pl.BlockSpec((None, tr, d), blk)],
            out_specs=[pl.BlockSpec((None, tr, d), blk), pl.BlockSpec((None, tr, d), blk)]),
        out_shape=[jax.ShapeDtypeStruct((4, rows, d), F32), jax.ShapeDtypeStruct((4, rows, d), BF16)],
        compiler_params=_cparams(("arbitrary", "arbitrary")),
    )(pos, g, r1)


def _rs_sum2(pos, p, r2, name):
    _, rows, d = p.shape
    tr = rows
    for cand in (512, 352, 256, 160, 128):
        if rows % cand == 0:
            tr = cand
            break

    def body(pos_ref, p_ref, r_ref, o_ref):
        r = r_ref[...].astype(F32)
        o_ref[...] = ((p_ref[...] + r[0]) + r[1]) + r[2]

    return pl.pallas_call(
        body, name=name,
        grid_spec=pltpu.PrefetchScalarGridSpec(
            num_scalar_prefetch=1, grid=(rows // tr,),
            in_specs=[pl.BlockSpec((None, tr, d), lambda i, pos_ref: (2 * pos_ref[0] + pos_ref[1], i, 0)),
                      pl.BlockSpec((3, tr, d), lambda i, pos_ref: (0, i, 0))],
            out_specs=pl.BlockSpec((tr, d), lambda i, pos_ref: (i, 0))),
        out_shape=jax.ShapeDtypeStruct((rows, d), F32),
        compiler_params=_cparams(("arbitrary",)),
    )(pos, p, r2)


def _ffn_chunk(f):
    for cand in (256, 128):
        if f % cand == 0:
            return cand
    return f


def _ffn_fwd(h, g, wg_t, wu_t, wd, name, tm, carry=None):
    s, d = h.shape
    f = wd.shape[0]
    tf = _ffn_chunk(f)

    def body(h_ref, g_ref, wg_ref, wu_ref, wd_ref, o_ref, n_ref, gate_ref, up_ref):
        x = h_ref[...]
        r = lax.rsqrt(jnp.mean(x * x, axis=-1, keepdims=True) + EPS)
        nb = (x * r * g_ref[...]).astype(BF16)
        n_ref[...] = nb
        acc = jnp.zeros((tm, d), F32)
        for j in range(f // tf):
            sl = slice(j * tf, (j + 1) * tf)
            gate = _dot_nt(nb, wg_ref[sl, :])
            up = _dot_nt(nb, wu_ref[sl, :])
            gate_ref[:, sl] = gate.astype(BF16)
            up_ref[:, sl] = up.astype(BF16)
            act = (gate * _sigmoid(gate) * up).astype(BF16)
            acc = acc + _dot_nn(act, wd_ref[sl, :])
        o_ref[...] = x + 0.5 * acc

    row = lambda w: pl.BlockSpec((tm, w), lambda i: (i, 0))
    return _call(
        body, name, (s // tm,),
        [row(d), _whole((1, d)), _whole((f, d)), _whole((f, d)), _whole((f, d))],
        [row(d), row(d), row(f), row(f)],
        [jax.ShapeDtypeStruct((s, d), F32), jax.ShapeDtypeStruct((s, d), BF16),
         jax.ShapeDtypeStruct((s, f), BF16), jax.ShapeDtypeStruct((s, f), BF16)],
        (h, g, wg_t, wu_t, wd), ("arbitrary",), carry)


def _ffn_bwd(h_in, dh_out, gate, up, g, wg_t, wu_t, wd, name, tm):
    s, d = h_in.shape
    f = gate.shape[1]
    tf = _ffn_chunk(f)

    def body(h_ref, dh_ref, gate_ref, up_ref, g_ref, wg_ref, wu_ref, wd_ref,
             o_ref, dg_ref, dgate_ref, dup_ref, act_ref, dhh_ref):
        x = h_ref[...]
        dh = dh_ref[...]
        r = lax.rsqrt(jnp.mean(x * x, axis=-1, keepdims=True) + EPS)
        xhat = x * r
        dhh = (0.5 * dh).astype(BF16)
        dhh_ref[...] = dhh
        dn = jnp.zeros((tm, d), F32)
        for j in range(f // tf):
            sl = slice(j * tf, (j + 1) * tf)
            gt = gate_ref[:, sl].astype(F32)
            u = up_ref[:, sl].astype(F32)
            dact = _dot_nt(dhh, wd_ref[sl, :])
            sg = _sigmoid(gt)
            silu = gt * sg
            act_ref[:, sl] = (silu * u).astype(BF16)
            dup = (dact * silu).astype(BF16)
            dgate = (dact * u * (sg * (1.0 + gt * (1.0 - sg)))).astype(BF16)
            dup_ref[:, sl] = dup
            dgate_ref[:, sl] = dgate
            dn = dn + _dot_nn(dgate, wg_ref[sl, :]) + _dot_nn(dup, wu_ref[sl, :])
        dxhat = dn * g_ref[...]
        o_ref[...] = dh + r * (dxhat - xhat * jnp.mean(dxhat * xhat, axis=-1, keepdims=True))

        @pl.when(pl.program_id(0) == 0)
        def _():
            dg_ref[...] = jnp.zeros_like(dg_ref)

        dg_ref[...] += jnp.sum(dn * xhat, axis=0, keepdims=True)

    row = lambda w: pl.BlockSpec((tm, w), lambda i: (i, 0))
    return pl.pallas_call(
        body, name=name, grid=(s // tm,),
        in_specs=[row(d), row(d), row(f), row(f), _whole((1, d)), _whole((f, d)), _whole((f, d)), _whole((f, d))],
        out_specs=[row(d), pl.BlockSpec((1, d), lambda i: (0, 0)), row(f), row(f), row(f), row(d)],
        out_shape=[jax.ShapeDtypeStruct((s, d), F32), jax.ShapeDtypeStruct((1, d), F32),
                   jax.ShapeDtypeStruct((s, f), BF16), jax.ShapeDtypeStruct((s, f), BF16),
                   jax.ShapeDtypeStruct((s, f), BF16), jax.ShapeDtypeStruct((s, d), BF16)],
        compiler_params=_cparams(("arbitrary",)),
    )(h_in, dh_out, gate, up, g, wg_t, wu_t, wd)


def _wgrad(a, b, name, tf, tk, carry=None):
    s, f = a.shape
    d = b.shape[1]
    nk = s // tk

    def body(a_ref, b_ref, o_ref, ob_ref):
        k = pl.program_id(1)

        @pl.when(k == 0)
        def _():
            o_ref[...] = jnp.zeros_like(o_ref)

        o_ref[...] += _dot_tn(a_ref[...], b_ref[...])

        @pl.when(k == nk - 1)
        def _():
            ob_ref[...] = o_ref[...].astype(BF16)

    return _call(
        body, name, (f // tf, nk),
        [pl.BlockSpec((tk, tf), lambda i, k: (k, i)), pl.BlockSpec((tk, d), lambda i, k: (k, 0))],
        [pl.BlockSpec((tf, d), lambda i, k: (i, 0)), pl.BlockSpec((tf, d), lambda i, k: (i, 0))],
        [jax.ShapeDtypeStruct((f, d), F32), jax.ShapeDtypeStruct((f, d), BF16)],
        (a, b), ("arbitrary", "arbitrary"), carry)


def _rope(t, c, sa, sb, reps):
    c, sa, sb = (jnp.tile(v, (1, reps)) if reps > 1 else v for v in (c, sa, sb))
    w = t.shape[1]
    return t * c + pltpu.roll(t, w - 8, 1) * sa + pltpu.roll(t, 8, 1) * sb


def _rope_bwd(dt, c, sa, sb, reps):
    c, sa, sb = (jnp.tile(v, (1, reps)) if reps > 1 else v for v in (c, sa, sb))
    w = dt.shape[1]
    return dt * c + pltpu.roll(dt * sa, 8, 1) + pltpu.roll(dt * sb, w - 8, 1)


def _mix_in(h, g, win_t, tabs, name, tm):
    s, d = h.shape
    n_in = win_t.shape[0]

    def body(h_ref, g_ref, w_ref, c_ref, sa_ref, sb_ref, q_ref, k_ref, v_ref, pc_ref, n_ref):
        x = h_ref[...]
        r = lax.rsqrt(jnp.mean(x * x, axis=-1, keepdims=True) + EPS)
        nb = (x * r * g_ref[...]).astype(BF16)
        n_ref[...] = nb
        u = _dot_nt(nb, w_ref[...])
        c, sa, sb = c_ref[...], sa_ref[...], sb_ref[...]
        q_ref[...] = _rope(u[:, :ATTN_W], c, sa, sb, ATTN_W // 128).astype(BF16)
        k_ref[...] = _rope(u[:, ATTN_W:ATTN_W + KV_W], c, sa, sb, 1).astype(BF16)
        v_ref[...] = u[:, ATTN_W + KV_W:ATTN_W + 2 * KV_W].astype(BF16)
        pc_ref[...] = u[:, ATTN_W + 2 * KV_W:]

    row = lambda w: pl.BlockSpec((tm, w), lambda i: (i, 0))
    return pl.pallas_call(
        body, name=name, grid=(s // tm,),
        in_specs=[row(d), _whole((1, d)), _whole((n_in, d)), row(128), row(128), row(128)],
        out_specs=[row(ATTN_W), row(KV_W), row(KV_W), row(POOL_W), row(d)],
        out_shape=[jax.ShapeDtypeStruct((s, ATTN_W), BF16), jax.ShapeDtypeStruct((s, KV_W), BF16),
                   jax.ShapeDtypeStruct((s, KV_W), BF16), jax.ShapeDtypeStruct((s, POOL_W), F32),
                   jax.ShapeDtypeStruct((s, d), BF16)],
        compiler_params=_cparams(("arbitrary",)),
    )(h, g, win_t, *tabs)


def _band_mask(n, nb):
    i = lax.broadcasted_iota(jnp.int32, (BLK, 3 * BLK), 0)
    j = lax.broadcasted_iota(jnp.int32, (BLK, 3 * BLK), 1)
    kpos = (n - 1) * BLK + j
    return (j >= i) & (j <= i + 2 * BLK) & (kpos >= 0) & (kpos < nb * BLK)


def _softmax_band(qh, kb, valid, sink):
    sc = _dot_nt(qh, kb)
    sc = jnp.where(valid, sc, -1e30)
    m = jnp.maximum(jnp.max(sc, axis=-1, keepdims=True), sink)
    e = jnp.exp(sc - m)
    es = jnp.exp(sink - m)
    inv = 1.0 / (jnp.sum(e, axis=-1, keepdims=True) + es)
    return e * inv, es * inv


def _attn_pool_fwd(q, k, v, pc, sink, pool_w, pool_scale, pband, name, carry=None):
    s = q.shape[0]
    nb = s // BLK

    def body(sink_ref, q_ref, k0, k1, k2, v0, v1, v2, p0, p1, p2, pw_ref, ps_ref, pb_ref, o_ref):
        n = pl.program_id(0)
        valid = _band_mask(n, nb)
        kb = jnp.concatenate([k0[...], k1[...], k2[...]], axis=0)
        vb = jnp.concatenate([v0[...], v1[...], v2[...]], axis=0)
        qs = q_ref[...] * SCORE_SCALE
        for h in range(N_HEADS):
            kh = h // GROUP
            ksl = slice(kh * HEAD_DIM, (kh + 1) * HEAD_DIM)
            p, _ = _softmax_band(qs[:, h * HEAD_DIM:(h + 1) * HEAD_DIM], kb[:, ksl], valid, sink_ref[0, h])
            o_ref[:, h * HEAD_DIM:(h + 1) * HEAD_DIM] = _dot_nn(p.astype(BF16), vb[:, ksl]).astype(BF16)
        ext = jnp.concatenate([p0[...], p1[...], p2[...]], axis=0).astype(BF16)
        for gi in range(POOL_G):
            gsl = slice(gi * POOL_GW, (gi + 1) * POOL_GW)
            dg = _dot_nn(pb_ref[gi], ext[:, gsl])
            yg = _dot_nn(dg.astype(BF16), pw_ref[gi].astype(BF16))
            o_ref[:, ATTN_W + gi * POOL_GW:ATTN_W + (gi + 1) * POOL_GW] = (yg * ps_ref[:, gsl]).astype(BF16)

    def band(w):
        return [pl.BlockSpec((BLK, w), lambda n: (jnp.maximum(n - 1, 0), 0)),
                pl.BlockSpec((BLK, w), lambda n: (n, 0)),
                pl.BlockSpec((BLK, w), lambda n: (jnp.minimum(n + 1, nb - 1), 0))]

    return _call(
        body, name, (nb,),
        [pl.BlockSpec(memory_space=pltpu.SMEM), pl.BlockSpec((BLK, ATTN_W), lambda n: (n, 0)),
         *band(KV_W), *band(KV_W), *band(POOL_W),
         _whole((POOL_G, POOL_GW, POOL_GW)), _whole((1, POOL_W)),
         pl.BlockSpec((None, POOL_G, BLK, 3 * BLK), lambda n: (_variant_index(n, nb), 0, 0, 0))],
        [pl.BlockSpec((BLK, ATTN_W + POOL_W), lambda n: (n, 0))],
        [jax.ShapeDtypeStruct((s, ATTN_W + POOL_W), BF16)],
        (sink, q, k, k, k, v, v, v, pc, pc, pc, pool_w, pool_scale, pband), ("arbitrary",), carry)


def _attn_pool_bwd(q, k, v, pc, dmix, sink, pool_w, pool_scale, pband, ptband, name, carry=None):
    s = q.shape[0]
    nb = s // BLK

    def body(sink_ref, q_ref, k0, k1, k2, v0, v1, v2, p0, p1, p2, da_ref, d0, d1, d2, pw_ref, ps_ref, pb_ref, ptb_ref,
             dq_ref, dk_ref, dv_ref, dpc_ref, dsink_ref, dpw_ref, dps_ref):
        n = pl.program_id(0)

        @pl.when(n == 0)
        def _():
            dsink_ref[...] = jnp.zeros_like(dsink_ref)
            dpw_ref[...] = jnp.zeros_like(dpw_ref)
            dps_ref[...] = jnp.zeros_like(dps_ref)

        valid = _band_mask(n, nb)
        kb = jnp.concatenate([k0[...], k1[...], k2[...]], axis=0)
        vb = jnp.concatenate([v0[...], v1[...], v2[...]], axis=0)
        qb = q_ref[...]
        qs = qb * SCORE_SCALE
        da = da_ref[...]
        for kh in range(N_KV):
            ksl = slice(kh * HEAD_DIM, (kh + 1) * HEAD_DIM)
            dk_acc = jnp.zeros((3 * BLK, HEAD_DIM), F32)
            dv_acc = jnp.zeros((3 * BLK, HEAD_DIM), F32)
            for hh in range(GROUP):
                h = kh * GROUP + hh
                hsl = slice(h * HEAD_DIM, (h + 1) * HEAD_DIM)
                p, ps = _softmax_band(qs[:, hsl], kb[:, ksl], valid, sink_ref[0, h])
                do = da[:, hsl].astype(BF16)
                dv_acc = dv_acc + _dot_tn(p.astype(BF16), do)
                dp = _dot_nt(do, vb[:, ksl])
                delta = jnp.sum(p * dp, axis=-1, keepdims=True)
                ds = (p * (dp - delta)).astype(BF16)
                dsink_ref[h:h + 1, :] += jnp.broadcast_to(jnp.sum(-ps * delta, axis=0, keepdims=True), (1, 128))
                dq_ref[:, hsl] = _dot_nn(ds, kb[:, ksl]) * SCORE_SCALE
                dk_acc = dk_acc + _dot_tn(ds, qb[:, hsl])
            for t in range(3):
                dk_ref[t, :, ksl] = dk_acc[t * BLK:(t + 1) * BLK] * SCORE_SCALE
                dv_ref[t, :, ksl] = dv_acc[t * BLK:(t + 1) * BLK]
        ext = jnp.concatenate([p0[...], p1[...], p2[...]], axis=0).astype(BF16)
        dpe = jnp.concatenate([d0[...], d1[...], d2[...]], axis=0)
        dpc_cur = d1[...]
        for gi in range(POOL_G):
            gsl = slice(gi * POOL_GW, (gi + 1) * POOL_GW)
            wg = pw_ref[gi].astype(BF16)
            sc = ps_ref[:, gsl]
            dgr = _dot_nn(pb_ref[gi], ext[:, gsl])
            dgb = dgr.astype(BF16)
            yg = _dot_nn(dgb, wg)
            dps_ref[:, gsl] += jnp.sum(dpc_cur[:, gsl] * yg, axis=0, keepdims=True)
            dpw_ref[gi] += _dot_tn(dgb, (dpc_cur[:, gsl] * sc).astype(BF16))
            dd = _dot_nt((dpe[:, gsl] * sc).astype(BF16), wg)
            dpc_ref[:, gsl] = _dot_nn(ptb_ref[gi], dd.astype(BF16))

    def band(w, col=0):
        return [pl.BlockSpec((BLK, w), lambda n: (jnp.maximum(n - 1, 0), col)),
                pl.BlockSpec((BLK, w), lambda n: (n, col)),
                pl.BlockSpec((BLK, w), lambda n: (jnp.minimum(n + 1, nb - 1), col))]

    tab = lambda: pl.BlockSpec((None, POOL_G, BLK, 3 * BLK), lambda n: (_variant_index(n, nb), 0, 0, 0))
    fixed = lambda shape: pl.BlockSpec(shape, lambda n: (0,) * len(shape))
    return _call(
        body, name, (nb,),
        [pl.BlockSpec(memory_space=pltpu.SMEM), pl.BlockSpec((BLK, ATTN_W), lambda n: (n, 0)),
         *band(KV_W), *band(KV_W), *band(POOL_W),
         pl.BlockSpec((BLK, ATTN_W), lambda n: (n, 0)), *band(POOL_W, 1),
         _whole((POOL_G, POOL_GW, POOL_GW)), _whole((1, POOL_W)), tab(), tab()],
        [pl.BlockSpec((BLK, ATTN_W), lambda n: (n, 0)),
         pl.BlockSpec((None, 3, BLK, KV_W), lambda n: (n, 0, 0, 0)),
         pl.BlockSpec((None, 3, BLK, KV_W), lambda n: (n, 0, 0, 0)),
         pl.BlockSpec((BLK, POOL_W), lambda n: (n, 0)),
         fixed((N_HEADS, 128)), fixed((POOL_G, POOL_GW, POOL_GW)), fixed((1, POOL_W))],
        [jax.ShapeDtypeStruct((s, ATTN_W), F32), jax.ShapeDtypeStruct((nb, 3, BLK, KV_W), F32),
         jax.ShapeDtypeStruct((nb, 3, BLK, KV_W), F32), jax.ShapeDtypeStruct((s, POOL_W), F32),
         jax.ShapeDtypeStruct((N_HEADS, 128), F32),
         jax.ShapeDtypeStruct((POOL_G, POOL_GW, POOL_GW), F32), jax.ShapeDtypeStruct((1, POOL_W), F32)],
        (sink, q, k, k, k, v, v, v, pc, pc, pc, dmix, dmix, dmix, dmix, pool_w, pool_scale, pband, ptband),
        ("arbitrary",), carry)


def _mix_out(h, mix, w_out, name, tm):
    s, d = h.shape
    w = mix.shape[1]

    def body(h_ref, m_ref, w_ref, o_ref):
        o_ref[...] = h_ref[...] + _dot_nn(m_ref[...], w_ref[...])

    row = lambda c: pl.BlockSpec((tm, c), lambda i: (i, 0))
    return pl.pallas_call(
        body, name=name, grid=(s // tm,), in_specs=[row(d), row(w), _whole((w, d))], out_specs=row(d),
        out_shape=jax.ShapeDtypeStruct((s, d), F32), compiler_params=_cparams(("arbitrary",)),
    )(h, mix, w_out)


def _mix_out_bwd(dh, w_out, name, tm):
    s, d = dh.shape
    w = w_out.shape[0]

    def body(dh_ref, w_ref, o_ref, dhb_ref):
        dhb = dh_ref[...].astype(BF16)
        dhb_ref[...] = dhb
        o_ref[...] = _dot_nt(dhb, w_ref[...])

    row = lambda c: pl.BlockSpec((tm, c), lambda i: (i, 0))
    return pl.pallas_call(
        body, name=name, grid=(s // tm,), in_specs=[row(d), _whole((w, d))], out_specs=[row(w), row(d)],
        out_shape=[jax.ShapeDtypeStruct((s, w), F32), jax.ShapeDtypeStruct((s, d), BF16)],
        compiler_params=_cparams(("arbitrary",)),
    )(dh, w_out)


def _mix_in_bwd(h, dh, g, win_t, dq, dkp, dvp, dpc, tabs, name):
    s, d = h.shape
    nb = s // BLK
    n_in = win_t.shape[0]

    def body(h_ref, dh_ref, g_ref, w_ref, dq_ref, ka, kb_, kc, va, vb_, vc, dpc_ref, c_ref, sa_ref, sb_ref,
             o_ref, du_ref, dg_ref):
        n = pl.program_id(0)
        lo = (n > 0).astype(F32)
        hi = (n < nb - 1).astype(F32)
        dk = ka[...] * lo + kb_[...] + kc[...] * hi
        dv = va[...] * lo + vb_[...] + vc[...] * hi
        c, sa, sb = c_ref[...], sa_ref[...], sb_ref[...]
        du = jnp.concatenate([_rope_bwd(dq_ref[...], c, sa, sb, ATTN_W // 128), _rope_bwd(dk, c, sa, sb, 1), dv,
                              dpc_ref[...]], axis=1).astype(BF16)
        du_ref[...] = du
        dn = _dot_nn(du, w_ref[...])
        x = h_ref[...]
        r = lax.rsqrt(jnp.mean(x * x, axis=-1, keepdims=True) + EPS)
        xhat = x * r
        dxhat = dn * g_ref[...]
        o_ref[...] = dh_ref[...] + r * (dxhat - xhat * jnp.mean(dxhat * xhat, axis=-1, keepdims=True))

        @pl.when(n == 0)
        def _():
            dg_ref[...] = jnp.zeros_like(dg_ref)

        dg_ref[...] += jnp.sum(dn * xhat, axis=0, keepdims=True)

    row = lambda w: pl.BlockSpec((BLK, w), lambda n: (n, 0))
    parts = [pl.BlockSpec((None, None, BLK, KV_W), lambda n: (jnp.maximum(n - 1, 0), 2, 0, 0)),
             pl.BlockSpec((None, None, BLK, KV_W), lambda n: (n, 1, 0, 0)),
             pl.BlockSpec((None, None, BLK, KV_W), lambda n: (jnp.minimum(n + 1, nb - 1), 0, 0, 0))]
    return pl.pallas_call(
        body, name=name, grid=(nb,),
        in_specs=[row(d), row(d), _whole((1, d)), _whole((n_in, d)), row(ATTN_W), *parts, *parts, row(POOL_W),
                  row(128), row(128), row(128)],
        out_specs=[row(d), row(n_in), pl.BlockSpec((1, d), lambda n: (0, 0))],
        out_shape=[jax.ShapeDtypeStruct((s, d), F32), jax.ShapeDtypeStruct((s, n_in), BF16),
                   jax.ShapeDtypeStruct((1, d), F32)],
        compiler_params=_cparams(("arbitrary",)),
    )(h, dh, g, win_t, dq, dkp, dkp, dkp, dvp, dvp, dvp, dpc, *tabs)


def _final(h, g, target, name, tm):
    s, d = h.shape

    def body(h_ref, g_ref, t_ref, loss_ref, dh_ref, dg_ref):
        @pl.when(pl.program_id(0) == 0)
        def _():
            loss_ref[...] = jnp.zeros_like(loss_ref)
            dg_ref[...] = jnp.zeros_like(dg_ref)

        x = h_ref[...]
        gg = g_ref[...]
        r = lax.rsqrt(jnp.mean(x * x, axis=-1, keepdims=True) + EPS)
        xhat = x * r
        e = xhat * gg - t_ref[...]
        per_tok = jnp.mean(e * e, axis=-1, keepdims=True)
        loss_ref[...] += 0.5 * jnp.sum(per_tok, axis=0, keepdims=True)
        dy = e * (1.0 / d)
        dg_ref[...] += jnp.sum(dy * xhat, axis=0, keepdims=True)
        dxhat = dy * gg
        dh_ref[...] = r * (dxhat - xhat * jnp.mean(dxhat * xhat, axis=-1, keepdims=True))

    row = pl.BlockSpec((tm, d), lambda i: (i, 0))
    return pl.pallas_call(
        body, name=name, grid=(s // tm,), in_specs=[row, _whole((1, d)), row],
        out_specs=[pl.BlockSpec((1, 1), lambda i: (0, 0)), row, pl.BlockSpec((1, d), lambda i: (0, 0))],
        out_shape=[jax.ShapeDtypeStruct((1, 1), F32), jax.ShapeDtypeStruct((s, d), F32),
                   jax.ShapeDtypeStruct((1, d), F32)],
        compiler_params=_cparams(("arbitrary",)),
    )(h, g, target)


def _adam_math(w, g, m, v):
    m = ADAM_B1 * m + (1.0 - ADAM_B1) * g
    v = ADAM_B2 * v + (1.0 - ADAM_B2) * (g * g)
    m_hat = m / (1.0 - ADAM_B1 ** ADAM_STEP)
    v_hat = v / (1.0 - ADAM_B2 ** ADAM_STEP)
    delta = -ADAM_LR * (m_hat / (jnp.sqrt(v_hat) + ADAM_EPS) + ADAM_WD * w)
    return delta, m, v


def _adam(w, g, m, v, name):
    rows, cols = w.shape
    tr = rows
    for cand in (256, 128, 64, 32, 16, 8):
        if rows % cand == 0 and rows > cand:
            tr = cand
            break

    def body(w_ref, g_ref, m_ref, v_ref, d_ref, nm_ref, nv_ref):
        d_ref[...], nm_ref[...], nv_ref[...] = _adam_math(w_ref[...], g_ref[...], m_ref[...], v_ref[...])

    blk = pl.BlockSpec((tr, cols), lambda i: (i, 0))
    return pl.pallas_call(
        body, name=name, grid=(rows // tr,), in_specs=[blk] * 4, out_specs=[blk] * 3,
        out_shape=[jax.ShapeDtypeStruct((rows, cols), F32)] * 3, compiler_params=_cparams(("arbitrary",)),
    )(w, g, m, v)


def _adam_small(w, parts, m, v, name):
    rows, cols = w.shape

    def body(w_ref, p_ref, m_ref, v_ref, g_ref, d_ref, nm_ref, nv_ref):
        g = p_ref[0]
        for k in range(1, N_DEV):
            g = g + p_ref[k]
        g_ref[...] = g
        d_ref[...], nm_ref[...], nv_ref[...] = _adam_math(w_ref[...], g, m_ref[...], v_ref[...])

    return pl.pallas_call(
        body, name=name, out_shape=[jax.ShapeDtypeStruct((rows, cols), F32)] * 4,
    )(w, parts, m, v)


def _pack_small(norm1, normm, norm2, normf, sink, pool_w, pool_scale):
    scale_rows = jnp.pad(pool_scale.reshape(4, 128), ((0, 4), (0, 0)))
    sink_rows = jnp.pad(sink.reshape(1, N_HEADS), ((0, 7), (0, 128 - N_HEADS)))
    return jnp.concatenate([pool_w.reshape(512, 128), norm1.reshape(8, 128), normm.reshape(8, 128),
                            norm2.reshape(8, 128), normf.reshape(8, 128), scale_rows, sink_rows], axis=0)


def _unpack_small(p):
    return dict(pool_w=p[:512].reshape(1, POOL_G, POOL_GW, POOL_GW), ffn1_norm=p[512:520].reshape(1, 1024),
                mix_norm=p[520:528].reshape(1, 1024), ffn2_norm=p[528:536].reshape(1, 1024),
                final_norm=p[536:544].reshape(1024), pool_scale=p[544:548].reshape(1, POOL_W),
                sink_logits=p[552, :N_HEADS].reshape(1, N_HEADS))


def kernel(x, ffn1_norm, ffn1_w_gate, ffn1_w_up, ffn1_w_down, mix_norm, w_in, sink_logits, pool_w, pool_scale, w_out, ffn2_norm, ffn2_w_gate, ffn2_w_up, ffn2_w_down, final_norm, loss_target, m_ffn1_norm, m_ffn1_w_gate, m_ffn1_w_up, m_ffn1_w_down, m_mix_norm, m_w_in, m_sink_logits, m_pool_w, m_pool_scale, m_w_out, m_ffn2_norm, m_ffn2_w_gate, m_ffn2_w_up, m_ffn2_w_down, m_final_norm, v_ffn1_norm, v_ffn1_w_gate, v_ffn1_w_up, v_ffn1_w_down, v_mix_norm, v_w_in, v_sink_logits, v_pool_w, v_pool_scale, v_w_out, v_ffn2_norm, v_ffn2_w_gate, v_ffn2_w_up, v_ffn2_w_down, v_final_norm):
    s, d = x.shape[1], x.shape[2]
    fk = ffn1_w_gate.shape[2]
    f = N_DEV * fk
    ink = w_in.shape[2]
    n_in = N_DEV * ink
    mixk = w_out.shape[1]
    tm = min(512, s)
    tm_bwd = min(256, s)
    pos = jnp.stack([lax.axis_index("x"), lax.axis_index("y"), lax.axis_index("c")]).astype(jnp.int32)

    t_bf = lambda w: w[0].T.astype(BF16)
    full = lambda a: a.reshape(N_DEV * a.shape[1], d)
    first = [t_bf(ffn1_w_gate), t_bf(ffn1_w_up), ffn1_w_down[0].astype(BF16)]
    wg1, wu1, wd1 = map(full, _run_exchange(_AllGather(first), "gather_ffn1"))
    second = _AllGather([t_bf(w_in), w_out[0].astype(BF16), t_bf(ffn2_w_gate), t_bf(ffn2_w_up)])
    third = _AllGather([ffn2_w_down[0].astype(BF16)])

    tabs = _rope_tables(s)
    pband, ptband = _pool_tables(s)
    g1, gm, g2, gf = ffn1_norm, mix_norm, ffn2_norm, final_norm.reshape(1, d)

    x0 = x[0]
    h1, n1, gate1, up1, *gathered = _ffn_fwd(x0, g1, wg1, wu1, wd1, "ffn1_fwd", tm, carry=second)
    win_t, wout, wg2, wu2 = map(full, gathered)
    q, k, v, pc, n2 = _mix_in(h1, gm, win_t, tabs, "mix_in", tm)
    mix, wd2 = _attn_pool_fwd(q, k, v, pc, sink_logits, pool_w[0], pool_scale, pband, "attn_pool_fwd", carry=third)
    wd2 = full(wd2)
    h2 = _mix_out(h1, mix, wout, "mix_out", tm)
    h3, n3, gate2, up2 = _ffn_fwd(h2, g2, wg2, wu2, wd2, "ffn2_fwd", tm)
    loss_part, dh3, dgf = _final(h3, gf, loss_target[0], "final", tm)

    tk = min(512, s)
    f_half = f // 2 if (f // 2) % 128 == 0 else f
    gw, sum1, recv2 = {}, {}, {}

    def stage1(keys, rows):
        r1 = _run_exchange(_RsStage1([gw[key][1].reshape(N_DEV, rows, d) for key in keys]), "rs1_" + keys[0])
        for key, r in zip(keys, r1):
            sum1[key] = _rs_sum1(pos, gw[key][0].reshape(N_DEV, rows, d), r, "rs_sum1_" + key)

    def stage2(keys):
        return _RsStage2([sum1[key][1] for key in keys])

    dh2, dg2, dgate2, dup2, act2, dhh3 = _ffn_bwd(h2, dh3, gate2, up2, g2, wg2, wu2, wd2, "ffn2_bwd", tm_bwd)
    gw["g2"] = _wgrad(dgate2, n3, "wgrad_gate2", f_half, tk)
    gw["u2"] = _wgrad(dup2, n3, "wgrad_up2", f_half, tk)
    gw["d2"] = _wgrad(act2, dhh3, "wgrad_down2", f_half, tk)
    stage1(["g2", "u2", "d2"], fk)
    dmix, dh2b = _mix_out_bwd(dh2, wout, "mix_out_bwd", tm)
    gw["out"] = _wgrad(mix, dh2b, "wgrad_out", mix.shape[1], tk)
    stage1(["out"], mixk)
    dq, dkp, dvp, dpc, dsink, dpw, dps, *r2 = _attn_pool_bwd(
        q, k, v, pc, dmix, sink_logits, pool_w[0], pool_scale, pband, ptband, "attn_pool_bwd",
        carry=stage2(["g2", "u2", "d2", "out"]))
    recv2.update(zip(["g2", "u2", "d2", "out"], r2))
    dh1, du, dgm = _mix_in_bwd(h1, dh2, gm, win_t, dq, dkp, dvp, dpc, tabs, "mix_in_bwd")
    gw["in"] = _wgrad(du, n2, "wgrad_in", n_in, tk)
    stage1(["in"], ink)
    dx, dg1, dgate1, dup1, act1, dhh1 = _ffn_bwd(x0, dh1, gate1, up1, g1, wg1, wu1, wd1, "ffn1_bwd", tm_bwd)
    *gw["g1"], recv2["in"] = _wgrad(dgate1, n1, "wgrad_gate1", f_half, tk, carry=stage2(["in"]))
    stage1(["g1"], fk)
    *gw["u1"], recv2["g1"] = _wgrad(dup1, n1, "wgrad_up1", f_half, tk, carry=stage2(["g1"]))
    stage1(["u1"], fk)
    *gw["d1"], recv2["u1"] = _wgrad(act1, dhh1, "wgrad_down1", f_half, tk, carry=stage2(["u1"]))
    stage1(["d1"], fk)
    (recv2["d1"],) = _run_exchange(stage2(["d1"]), "rs2_d1")
    grads = [_rs_sum2(pos, sum1[key][0], recv2[key], "rs_sum2_" + key)
             for key in ["g1", "u1", "d1", "g2", "u2", "d2", "in", "out"]]

    small_part = _pack_small(dg1, dgm, dg2, dgf, dsink[:, 0], dpw, dps)
    (small_all,) = _run_exchange(_AllGather([small_part]), "gather_small_grads")
    pk = lambda a, b, c_, e, s_, pw_, psc: _pack_small(a, b, c_, e, s_[0], pw_[0], psc)
    small_w = pk(ffn1_norm, mix_norm, ffn2_norm, final_norm, sink_logits, pool_w, pool_scale)
    small_m = pk(m_ffn1_norm, m_mix_norm, m_ffn2_norm, m_final_norm, m_sink_logits, m_pool_w, m_pool_scale)
    small_v = pk(v_ffn1_norm, v_mix_norm, v_ffn2_norm, v_final_norm, v_sink_logits, v_pool_w, v_pool_scale)
    sg, sd, sm, sv = [_unpack_small(a) for a in _adam_small(small_w, small_all, small_m, small_v, "adam_small")]

    big = {}
    names = ["ffn1_w_gate", "ffn1_w_up", "ffn1_w_down", "ffn2_w_gate", "ffn2_w_up", "ffn2_w_down", "w_in", "w_out"]
    transposed = [True, True, False, True, True, False, True, False]
    ws = [ffn1_w_gate, ffn1_w_up, ffn1_w_down, ffn2_w_gate, ffn2_w_up, ffn2_w_down, w_in, w_out]
    ms = [m_ffn1_w_gate, m_ffn1_w_up, m_ffn1_w_down, m_ffn2_w_gate, m_ffn2_w_up, m_ffn2_w_down, m_w_in, m_w_out]
    vs = [v_ffn1_w_gate, v_ffn1_w_up, v_ffn1_w_down, v_ffn2_w_gate, v_ffn2_w_up, v_ffn2_w_down, v_w_in, v_w_out]
    for nm, tr, g, w, m, vv in zip(names, transposed, grads, ws, ms, vs):
        g = g.T if tr else g
        dl, nm_, nv_ = _adam(w[0], g, m[0], vv[0], "adam_" + nm)
        big[nm] = (g[None], dl[None], nm_[None], nv_[None])

    loss = lax.psum(loss_part[0, 0], ("x", "y", "c"))
    all_names = ["ffn1_norm", "ffn1_w_gate", "ffn1_w_up", "ffn1_w_down", "mix_norm", "w_in", "sink_logits", "pool_w",
                 "pool_scale", "w_out", "ffn2_norm", "ffn2_w_gate", "ffn2_w_up", "ffn2_w_down", "final_norm"]
    outs = [loss, dx[None]]
    for idx, src in enumerate((sg, sd, sm, sv)):
        for nm in all_names:
            outs.append(big[nm][idx] if nm in big else src[nm])
    return tuple(outs)
```

```python
import functools

import jax
import jax.numpy as jnp
import numpy as np
from jax import lax
from jax.experimental import pallas as pl
from jax.experimental.pallas import tpu as pltpu

F32 = jnp.float32
BF16 = jnp.bfloat16
MESH = pl.DeviceIdType.MESH
N_DEV = 8

EPS = 1e-6
HEAD_DIM = 64
N_HEADS = 8
N_KV = 2
GROUP = N_HEADS // N_KV
ATTN_W = N_HEADS * HEAD_DIM
KV_W = N_KV * HEAD_DIM
POOL_W = 512
POOL_G = 4
POOL_GW = POOL_W // POOL_G
POOL_WINDOWS = (2, 4, 8, 16)
BLK = 128
ROT = 16
ROPE_THETA = 500000.0
SCORE_SCALE = HEAD_DIM ** -0.5

ADAM_LR, ADAM_B1, ADAM_B2, ADAM_EPS, ADAM_WD, ADAM_STEP = 0.001, 0.9, 0.999, 1e-08, 0.01, 10

VMEM_LIMIT = 56 * 1024 * 1024


def _cparams(sem=None, **kw):
    if sem is not None:
        kw["dimension_semantics"] = sem
    return pltpu.CompilerParams(vmem_limit_bytes=VMEM_LIMIT, **kw)


def _whole(shape):
    nd = len(shape)
    return pl.BlockSpec(shape, lambda *_: (0,) * nd, pipeline_mode=pl.Buffered(1))


def _sigmoid(z):
    return 1.0 / (1.0 + jnp.exp(-z))


def _dot_nt(a, b):
    return lax.dot_general(a, b, (((1,), (1,)), ((), ())), preferred_element_type=F32)


def _dot_nn(a, b):
    return lax.dot_general(a, b, (((1,), (0,)), ((), ())), preferred_element_type=F32)


def _dot_tn(a, b):
    return lax.dot_general(a, b, (((0,), (0,)), ((), ())), preferred_element_type=F32)


def _rope_tables(s):
    inv_freq = ROPE_THETA ** (-np.arange(0, ROT, 2, dtype=np.float64) / ROT)
    ang = np.arange(s, dtype=np.float64)[:, None] * inv_freq[None, :]
    c = np.ones((s, HEAD_DIM)); sa = np.zeros((s, HEAD_DIM)); sb = np.zeros((s, HEAD_DIM))
    c[:, :8] = np.cos(ang); c[:, 8:16] = np.cos(ang)
    sa[:, :8] = -np.sin(ang)
    sb[:, 8:16] = np.sin(ang)
    t = lambda a: jnp.asarray(np.tile(a, (1, 2)).astype(np.float32))
    return t(c), t(sa), t(sb)


def _pool_weight(gi, t, s_pos, s):
    half = POOL_WINDOWS[gi] // 2

    def win(lo, hi):
        a = np.clip(lo, 0, s); b = np.clip(hi + 1, 0, s)
        inside = (s_pos >= a) & (s_pos < b)
        return inside / np.maximum(b - a, 1)

    w = 0.5 * (win(t - half, t + half - 1) + win(t - half + 1, t + half)) - (t == s_pos)
    return w * ((t >= 0) & (t < s) & (s_pos >= 0) & (s_pos < s))


def _pool_tables(s):
    nb = s // BLK
    fwd = np.zeros((3, POOL_G, BLK, 3 * BLK), np.float32)
    bwd = np.zeros((3, POOL_G, BLK, 3 * BLK), np.float32)
    for vi, n in enumerate((0, 1 if nb > 2 else 0, nb - 1)):
        i = n * BLK + np.arange(BLK)[:, None]
        j = (n - 1) * BLK + np.arange(3 * BLK)[None, :]
        for gi in range(POOL_G):
            fwd[vi, gi] = _pool_weight(gi, i, j, s)
            bwd[vi, gi] = _pool_weight(gi, j, i, s)
    return jnp.asarray(fwd, dtype=BF16), jnp.asarray(bwd, dtype=BF16)


def _variant_index(n, nb):
    return jnp.where(n == 0, 0, jnp.where(n == nb - 1, 2, 1))


class _Exchange:
    inputs = ()
    out_shapes = ()
    sems = ()

    def start(self, srcs, outs, sems):
        raise NotImplementedError

    def finish(self, srcs, outs, sems):
        raise NotImplementedError


class _AllGather(_Exchange):
    def __init__(self, arrays):
        n = len(arrays)
        self.inputs = list(arrays)
        self.out_shapes = [jax.ShapeDtypeStruct((N_DEV,) + a.shape, a.dtype) for a in arrays]
        self.sems = [pltpu.SemaphoreType.DMA((n, 7)), pltpu.SemaphoreType.DMA((n, 7)), pltpu.SemaphoreType.DMA((n,))]

    def _parts(self, srcs, outs, sems):
        send_sems, recv_sems, local_sems = sems
        n = len(srcs)
        x, y, c = lax.axis_index("x"), lax.axis_index("y"), lax.axis_index("c")
        me, sibling = (x, y, c), (x, y, 1 - c)
        chips = [(1 - x, y), (x, 1 - y), (1 - x, 1 - y)]

        def slot(a, dev):
            return outs[a].at[4 * dev[0] + 2 * dev[1] + dev[2]]

        def copy(a, k, block, to, src=None):
            return pltpu.make_async_remote_copy(
                src_ref=slot(a, block) if src is None else src, dst_ref=slot(a, block),
                send_sem=send_sems.at[a, k], recv_sem=recv_sems.at[a, k], device_id=to, device_id_type=MESH)

        mine = [pltpu.make_async_copy(srcs[a], slot(a, me), local_sems.at[a]) for a in range(n)]
        first = []
        for a in range(n):
            first.append(copy(a, 0, me, sibling, src=srcs[a]))
            first += [copy(a, 1 + j, me, (*chip, c), src=srcs[a]) for j, chip in enumerate(chips)]
        return n, c, me, sibling, chips, copy, mine, first

    def start(self, srcs, outs, sems):
        _, _, _, _, _, _, mine, first = self._parts(srcs, outs, sems)
        for cp in mine + first:
            cp.start()

    def finish(self, srcs, outs, sems):
        n, c, me, sibling, chips, copy, mine, first = self._parts(srcs, outs, sems)
        passed = []
        for j, chip in enumerate(chips):
            for a in range(n):
                copy(a, 1 + j, (*chip, c), me).wait_recv()
                fwd = copy(a, 4 + j, (*chip, c), sibling)
                fwd.start()
                passed.append(fwd)
        for a in range(n):
            copy(a, 0, sibling, me).wait_recv()
            for j, chip in enumerate(chips):
                copy(a, 4 + j, (*chip, 1 - c), me).wait_recv()
        for cp in first + passed:
            cp.wait_send()
        for cp in mine:
            cp.wait()


class _RsStage1(_Exchange):
    def __init__(self, gbs):
        n = len(gbs)
        self.inputs = list(gbs)
        self.out_shapes = [jax.ShapeDtypeStruct((4,) + g.shape[1:], g.dtype) for g in gbs]
        self.sems = [pltpu.SemaphoreType.DMA((n, 4)), pltpu.SemaphoreType.DMA((n, 4))]

    def _copies(self, srcs, outs, sems):
        send_sems, recv_sems = sems
        x, y, c = lax.axis_index("x"), lax.axis_index("y"), lax.axis_index("c")
        return [pltpu.make_async_remote_copy(
            src_ref=srcs[a].at[4 * (q // 2) + 2 * (q % 2) + (1 - c)], dst_ref=outs[a].at[q],
            send_sem=send_sems.at[a, q], recv_sem=recv_sems.at[a, q], device_id=(x, y, 1 - c), device_id_type=MESH)
            for a in range(len(srcs)) for q in range(4)]

    def start(self, srcs, outs, sems):
        for cp in self._copies(srcs, outs, sems):
            cp.start()

    def finish(self, srcs, outs, sems):
        copies = self._copies(srcs, outs, sems)
        for cp in copies:
            cp.wait_recv()
        for cp in copies:
            cp.wait_send()


class _RsStage2(_RsStage1):
    def __init__(self, pbs):
        n = len(pbs)
        self.inputs = list(pbs)
        self.out_shapes = [jax.ShapeDtypeStruct((3,) + p.shape[1:], p.dtype) for p in pbs]
        self.sems = [pltpu.SemaphoreType.DMA((n, 3)), pltpu.SemaphoreType.DMA((n, 3))]

    def _copies(self, srcs, outs, sems):
        send_sems, recv_sems = sems
        x, y, c = lax.axis_index("x"), lax.axis_index("y"), lax.axis_index("c")
        chips = [(1 - x, y), (x, 1 - y), (1 - x, 1 - y)]
        return [pltpu.make_async_remote_copy(
            src_ref=srcs[a].at[2 * chip[0] + chip[1]], dst_ref=outs[a].at[j], send_sem=send_sems.at[a, j],
            recv_sem=recv_sems.at[a, j], device_id=(*chip, c), device_id_type=MESH)
            for a in range(len(srcs)) for j, chip in enumerate(chips)]


_ANY = pl.BlockSpec(memory_space=pl.ANY)


def _run_exchange(ex, name):
    n_in, n_out = len(ex.inputs), len(ex.out_shapes)

    def body(*refs):
        srcs, outs, sems = refs[:n_in], refs[n_in:n_in + n_out], refs[n_in + n_out:]
        ex.start(srcs, outs, sems)
        ex.finish(srcs, outs, sems)

    return pl.pallas_call(
        body, name=name, out_shape=list(ex.out_shapes), in_specs=[_ANY] * n_in, out_specs=[_ANY] * n_out,
        scratch_shapes=list(ex.sems),
    )(*ex.inputs)


def _call(body, name, grid, in_specs, out_specs, out_shape, args, sem, carry=None, **kw):
    if carry is None:
        return pl.pallas_call(body, name=name, grid=grid, in_specs=in_specs, out_specs=out_specs, out_shape=out_shape,
                              compiler_params=_cparams(sem), **kw)(*args)
    n_in, n_out = len(in_specs), len(out_specs)
    nc_in, nc_out = len(carry.inputs), len(carry.out_shapes)

    def carried(*refs):
        ins = refs[:n_in]
        c_in = refs[n_in:n_in + nc_in]
        outs = refs[n_in + nc_in:n_in + nc_in + n_out]
        c_out = refs[n_in + nc_in + n_out:n_in + nc_in + n_out + nc_out]
        sems = refs[n_in + nc_in + n_out + nc_out:]
        ids = [pl.program_id(i) for i in range(len(grid))]
        is_first = functools.reduce(jnp.logical_and, [i == 0 for i in ids])
        is_last = functools.reduce(jnp.logical_and, [i == g - 1 for i, g in zip(ids, grid)])

        @pl.when(is_first)
        def _():
            carry.start(c_in, c_out, sems)

        body(*ins, *outs)

        @pl.when(is_last)
        def _():
            carry.finish(c_in, c_out, sems)

    return pl.pallas_call(
        carried, name=name, grid=grid, in_specs=list(in_specs) + [_ANY] * nc_in,
        out_specs=list(out_specs) + [_ANY] * nc_out, out_shape=list(out_shape) + list(carry.out_shapes),
        scratch_shapes=list(carry.sems), compiler_params=_cparams(sem), **kw)(*args, *carry.inputs)


def _rs_sum1(pos, g, r1, name):
    _, rows, d = g.shape
    tr = rows
    for cand in (512, 352, 256, 160, 128):
        if rows % cand == 0:
            tr = cand
            break

    def body(pos_ref, g_ref, r_ref, p_ref, pb_ref):
        p = g_ref[...] + r_ref[...].astype(F32)
        p_ref[...] = p
        pb_ref[...] = p.astype(BF16)

    def g_map(q, i, pos_ref):
        return (4 * (q // 2) + 2 * (q % 2) + pos_ref[2], i, 0)

    blk = lambda q, i, pos_ref: (q, i, 0)
    return pl.pallas_call(
        body, name=name,
        grid_spec=pltpu.PrefetchScalarGridSpec(
            num_scalar_prefetch=1, grid=(4, rows // tr),
            in_specs=[pl.BlockSpec((None, tr, d), g_map), pl.BlockSpec((None, tr, d), blk)],
            out_specs=[pl.BlockSpec((None, tr, d), blk), pl.BlockSpec((None, tr, d), blk)]),
        out_shape=[jax.ShapeDtypeStruct((4, rows, d), F32), jax.ShapeDtypeStruct((4, rows, d), BF16)],
        compiler_params=_cparams(("arbitrary", "arbitrary")),
    )(pos, g, r1)


def _rs_sum2_adam(pos, p, r2, w, m, v, name):
    _, rows, d = p.shape
    tr = rows // 2 if rows % 16 == 0 else rows

    def body(pos_ref, p_ref, r_ref, w_ref, m_ref, v_ref, g_ref, d_ref, nm_ref, nv_ref):
        r = r_ref[...].astype(F32)
        g = ((p_ref[...] + r[0]) + r[1]) + r[2]
        g_ref[...] = g
        d_ref[...], nm_ref[...], nv_ref[...] = _adam_math(w_ref[...], g, m_ref[...], v_ref[...])

    blk = pl.BlockSpec((tr, d), lambda i, pos_ref: (i, 0))
    return pl.pallas_call(
        body, name=name,
        grid_spec=pltpu.PrefetchScalarGridSpec(
            num_scalar_prefetch=1, grid=(rows // tr,),
            in_specs=[pl.BlockSpec((None, tr, d), lambda i, pos_ref: (2 * pos_ref[0] + pos_ref[1], i, 0)),
                      pl.BlockSpec((3, tr, d), lambda i, pos_ref: (0, i, 0)), blk, blk, blk],
            out_specs=[blk] * 4),
        out_shape=[jax.ShapeDtypeStruct((rows, d), F32)] * 4,
        compiler_params=_cparams(("arbitrary",)),
    )(pos, p, r2, w, m, v)


def _ffn_chunk(f):
    for cand in (256, 128):
        if f % cand == 0:
            return cand
    return f


def _ffn_fwd(h, g, wg_t, wu_t, wd, name, tm, carry=None):
    s, d = h.shape
    f = wd.shape[0]
    tf = _ffn_chunk(f)

    def body(h_ref, g_ref, wg_ref, wu_ref, wd_ref, o_ref, n_ref, gate_ref, up_ref):
        x = h_ref[...]
        r = lax.rsqrt(jnp.mean(x * x, axis=-1, keepdims=True) + EPS)
        nb = (x * r * g_ref[...]).astype(BF16)
        n_ref[...] = nb
        acc = jnp.zeros((tm, d), F32)
        for j in range(f // tf):
            sl = slice(j * tf, (j + 1) * tf)
            gate = _dot_nt(nb, wg_ref[sl, :])
            up = _dot_nt(nb, wu_ref[sl, :])
            gate_ref[:, sl] = gate.astype(BF16)
            up_ref[:, sl] = up.astype(BF16)
            act = (gate * _sigmoid(gate) * up).astype(BF16)
            acc = acc + _dot_nn(act, wd_ref[sl, :])
        o_ref[...] = x + 0.5 * acc

    row = lambda w: pl.BlockSpec((tm, w), lambda i: (i, 0))
    return _call(
        body, name, (s // tm,),
        [row(d), _whole((1, d)), _whole((f, d)), _whole((f, d)), _whole((f, d))],
        [row(d), row(d), row(f), row(f)],
        [jax.ShapeDtypeStruct((s, d), F32), jax.ShapeDtypeStruct((s, d), BF16),
         jax.ShapeDtypeStruct((s, f), BF16), jax.ShapeDtypeStruct((s, f), BF16)],
        (h, g, wg_t, wu_t, wd), ("arbitrary",), carry)


def _ffn_bwd(h_in, dh_out, gate, up, g, wg_t, wu_t, wd, name, tm):
    s, d = h_in.shape
    f = gate.shape[1]
    tf = _ffn_chunk(f)

    def body(h_ref, dh_ref, gate_ref, up_ref, g_ref, wg_ref, wu_ref, wd_ref,
             o_ref, dg_ref, dgate_ref, dup_ref, act_ref, dhh_ref):
        x = h_ref[...]
        dh = dh_ref[...]
        r = lax.rsqrt(jnp.mean(x * x, axis=-1, keepdims=True) + EPS)
        xhat = x * r
        dhh = (0.5 * dh).astype(BF16)
        dhh_ref[...] = dhh
        dn = jnp.zeros((tm, d), F32)
        for j in range(f // tf):
            sl = slice(j * tf, (j + 1) * tf)
            gt = gate_ref[:, sl].astype(F32)
            u = up_ref[:, sl].astype(F32)
            dact = _dot_nt(dhh, wd_ref[sl, :])
            sg = _sigmoid(gt)
            silu = gt * sg
            act_ref[:, sl] = (silu * u).astype(BF16)
            dup = (dact * silu).astype(BF16)
            dgate = (dact * u * (sg * (1.0 + gt * (1.0 - sg)))).astype(BF16)
            dup_ref[:, sl] = dup
            dgate_ref[:, sl] = dgate
            dn = dn + _dot_nn(dgate, wg_ref[sl, :]) + _dot_nn(dup, wu_ref[sl, :])
        dxhat = dn * g_ref[...]
        o_ref[...] = dh + r * (dxhat - xhat * jnp.mean(dxhat * xhat, axis=-1, keepdims=True))

        @pl.when(pl.program_id(0) == 0)
        def _():
            dg_ref[...] = jnp.zeros_like(dg_ref)

        dg_ref[...] += jnp.sum(dn * xhat, axis=0, keepdims=True)

    row = lambda w: pl.BlockSpec((tm, w), lambda i: (i, 0))
    return pl.pallas_call(
        body, name=name, grid=(s // tm,),
        in_specs=[row(d), row(d), row(f), row(f), _whole((1, d)), _whole((f, d)), _whole((f, d)), _whole((f, d))],
        out_specs=[row(d), pl.BlockSpec((1, d), lambda i: (0, 0)), row(f), row(f), row(f), row(d)],
        out_shape=[jax.ShapeDtypeStruct((s, d), F32), jax.ShapeDtypeStruct((1, d), F32),
                   jax.ShapeDtypeStruct((s, f), BF16), jax.ShapeDtypeStruct((s, f), BF16),
                   jax.ShapeDtypeStruct((s, f), BF16), jax.ShapeDtypeStruct((s, d), BF16)],
        compiler_params=_cparams(("arbitrary",)),
    )(h_in, dh_out, gate, up, g, wg_t, wu_t, wd)


def _wgrad(a, b, name, tf, tk, carry=None):
    s, f = a.shape
    d = b.shape[1]
    nk = s // tk

    def body(a_ref, b_ref, o_ref, ob_ref):
        k = pl.program_id(1)

        @pl.when(k == 0)
        def _():
            o_ref[...] = jnp.zeros_like(o_ref)

        o_ref[...] += _dot_tn(a_ref[...], b_ref[...])

        @pl.when(k == nk - 1)
        def _():
            ob_ref[...] = o_ref[...].astype(BF16)

    return _call(
        body, name, (f // tf, nk),
        [pl.BlockSpec((tk, tf), lambda i, k: (k, i)), pl.BlockSpec((tk, d), lambda i, k: (k, 0))],
        [pl.BlockSpec((tf, d), lambda i, k: (i, 0)), pl.BlockSpec((tf, d), lambda i, k: (i, 0))],
        [jax.ShapeDtypeStruct((f, d), F32), jax.ShapeDtypeStruct((f, d), BF16)],
        (a, b), ("arbitrary", "arbitrary"), carry)


def _rope(t, c, sa, sb, reps):
    c, sa, sb = (jnp.tile(v, (1, reps)) if reps > 1 else v for v in (c, sa, sb))
    w = t.shape[1]
    return t * c + pltpu.roll(t, w - 8, 1) * sa + pltpu.roll(t, 8, 1) * sb


def _rope_bwd(dt, c, sa, sb, reps):
    c, sa, sb = (jnp.tile(v, (1, reps)) if reps > 1 else v for v in (c, sa, sb))
    w = dt.shape[1]
    return dt * c + pltpu.roll(dt * sa, 8, 1) + pltpu.roll(dt * sb, w - 8, 1)


def _mix_in(h, g, win_t, tabs, name, tm):
    s, d = h.shape
    n_in = win_t.shape[0]

    def body(h_ref, g_ref, w_ref, c_ref, sa_ref, sb_ref, q_ref, k_ref, v_ref, pc_ref, n_ref):
        x = h_ref[...]
        r = lax.rsqrt(jnp.mean(x * x, axis=-1, keepdims=True) + EPS)
        nb = (x * r * g_ref[...]).astype(BF16)
        n_ref[...] = nb
        u = _dot_nt(nb, w_ref[...])
        c, sa, sb = c_ref[...], sa_ref[...], sb_ref[...]
        q_ref[...] = _rope(u[:, :ATTN_W], c, sa, sb, ATTN_W // 128).astype(BF16)
        k_ref[...] = _rope(u[:, ATTN_W:ATTN_W + KV_W], c, sa, sb, 1).astype(BF16)
        v_ref[...] = u[:, ATTN_W + KV_W:ATTN_W + 2 * KV_W].astype(BF16)
        pc_ref[...] = u[:, ATTN_W + 2 * KV_W:]

    row = lambda w: pl.BlockSpec((tm, w), lambda i: (i, 0))
    return pl.pallas_call(
        body, name=name, grid=(s // tm,),
        in_specs=[row(d), _whole((1, d)), _whole((n_in, d)), row(128), row(128), row(128)],
        out_specs=[row(ATTN_W), row(KV_W), row(KV_W), row(POOL_W), row(d)],
        out_shape=[jax.ShapeDtypeStruct((s, ATTN_W), BF16), jax.ShapeDtypeStruct((s, KV_W), BF16),
                   jax.ShapeDtypeStruct((s, KV_W), BF16), jax.ShapeDtypeStruct((s, POOL_W), F32),
                   jax.ShapeDtypeStruct((s, d), BF16)],
        compiler_params=_cparams(("arbitrary",)),
    )(h, g, win_t, *tabs)


def _band_mask(n, nb, transposed):
    shape = (3 * BLK, 2 * BLK) if transposed else (2 * BLK, 3 * BLK)
    i = lax.broadcasted_iota(jnp.int32, shape, 1 if transposed else 0) % BLK
    j = lax.broadcasted_iota(jnp.int32, shape, 0 if transposed else 1)
    kpos = (n - 1) * BLK + j
    return (j >= i) & (j <= i + 2 * BLK) & (kpos >= 0) & (kpos < nb * BLK)


def _block_diag(t, kh):
    tf = t.astype(F32)
    tr = pltpu.roll(tf, HEAD_DIM, 1)
    lo = lax.broadcasted_iota(jnp.int32, tf.shape, 1) < HEAD_DIM
    top, bot = (tf, tr) if kh == 0 else (tr, tf)
    return jnp.concatenate([jnp.where(lo, top, 0.0), jnp.where(lo, 0.0, bot)], axis=0).astype(BF16)


def _fold_diag(tbd):
    lo = lax.broadcasted_iota(jnp.int32, (3 * BLK, 2 * HEAD_DIM), 1) < HEAD_DIM
    t = jnp.where(lo, tbd[:3 * BLK], tbd[3 * BLK:])
    return t + pltpu.roll(t, HEAD_DIM, 1)


def _stack_pairs(x, kh):
    return jnp.concatenate([x[:, (2 * kh) * 128:(2 * kh + 1) * 128], x[:, (2 * kh + 1) * 128:(2 * kh + 2) * 128]], axis=0)


def _sink_of(sink_ref, kh, half, axis):
    shape = (2 * BLK, 1) if axis == 0 else (1, 2 * BLK)
    first = lax.broadcasted_iota(jnp.int32, shape, axis) < BLK
    return jnp.where(first, sink_ref[0, GROUP * kh + half], sink_ref[0, GROUP * kh + 2 + half])


def _softmax_sink(sc, valid, sink, axis):
    sc = jnp.where(valid, sc, -1e30)
    m = jnp.maximum(jnp.max(sc, axis=axis, keepdims=True), sink)
    e = jnp.exp(sc - m)
    es = jnp.exp(sink - m)
    inv = 1.0 / (jnp.sum(e, axis=axis, keepdims=True) + es)
    return e * inv, es * inv


def _attn_pool_fwd(q, k, v, pc, sink, pool_w, pool_scale, pband, name, carry=None):
    s = q.shape[0]
    nb = s // BLK

    def body(sink_ref, q_ref, k0, k1, k2, v0, v1, v2, p0, p1, p2, pw_ref, ps_ref, pb_ref, o_ref):
        n = pl.program_id(0)
        valid = _band_mask(n, nb, False)
        kb = jnp.concatenate([k0[...], k1[...], k2[...]], axis=0)
        vb = jnp.concatenate([v0[...], v1[...], v2[...]], axis=0)
        qs = q_ref[...] * SCORE_SCALE
        for kh in range(N_KV):
            sc = _dot_nt(_stack_pairs(qs, kh), _block_diag(kb, kh))
            p = [_softmax_sink(sc[:, half * 3 * BLK:(half + 1) * 3 * BLK], valid, _sink_of(sink_ref, kh, half, 0), 1)[0]
                 for half in range(2)]
            o2 = _dot_nn(jnp.concatenate(p, axis=1).astype(BF16), _block_diag(vb, kh)).astype(BF16)
            o_ref[:, (2 * kh) * 128:(2 * kh + 1) * 128] = o2[:BLK]
            o_ref[:, (2 * kh + 1) * 128:(2 * kh + 2) * 128] = o2[BLK:]
        ext = jnp.concatenate([p0[...], p1[...], p2[...]], axis=0).astype(BF16)
        for gi in range(POOL_G):
            gsl = slice(gi * POOL_GW, (gi + 1) * POOL_GW)
            dg = _dot_nn(pb_ref[gi], ext[:, gsl])
            yg = _dot_nn(dg.astype(BF16), pw_ref[gi].astype(BF16))
            o_ref[:, ATTN_W + gi * POOL_GW:ATTN_W + (gi + 1) * POOL_GW] = (yg * ps_ref[:, gsl]).astype(BF16)

    def band(w):
        return [pl.BlockSpec((BLK, w), lambda n: (jnp.maximum(n - 1, 0), 0)),
                pl.BlockSpec((BLK, w), lambda n: (n, 0)),
                pl.BlockSpec((BLK, w), lambda n: (jnp.minimum(n + 1, nb - 1), 0))]

    return _call(
        body, name, (nb,),
        [pl.BlockSpec(memory_space=pltpu.SMEM), pl.BlockSpec((BLK, ATTN_W), lambda n: (n, 0)),
         *band(KV_W), *band(KV_W), *band(POOL_W),
         _whole((POOL_G, POOL_GW, POOL_GW)), _whole((1, POOL_W)),
         pl.BlockSpec((None, POOL_G, BLK, 3 * BLK), lambda n: (_variant_index(n, nb), 0, 0, 0))],
        [pl.BlockSpec((BLK, ATTN_W + POOL_W), lambda n: (n, 0))],
        [jax.ShapeDtypeStruct((s, ATTN_W + POOL_W), BF16)],
        (sink, q, k, k, k, v, v, v, pc, pc, pc, pool_w, pool_scale, pband), ("arbitrary",), carry)


def _attn_pool_bwd(q, k, v, pc, dmix, sink, pool_w, pool_scale, pband, ptband, name, carry=None):
    s = q.shape[0]
    nb = s // BLK

    def body(sink_ref, q_ref, k0, k1, k2, v0, v1, v2, p0, p1, p2, da_ref, d0, d1, d2, pw_ref, ps_ref, pb_ref, ptb_ref,
             dq_ref, dk_ref, dv_ref, dpc_ref, dsink_ref, dpw_ref, dps_ref):
        n = pl.program_id(0)

        @pl.when(n == 0)
        def _():
            dsink_ref[...] = jnp.zeros_like(dsink_ref)
            dpw_ref[...] = jnp.zeros_like(dpw_ref)
            dps_ref[...] = jnp.zeros_like(dps_ref)

        valid = _band_mask(n, nb, True)
        kb = jnp.concatenate([k0[...], k1[...], k2[...]], axis=0)
        vb = jnp.concatenate([v0[...], v1[...], v2[...]], axis=0)
        qb = q_ref[...]
        qs = qb * SCORE_SCALE
        da = da_ref[...].astype(BF16)
        dk_fold, dv_fold = [], []
        for kh in range(N_KV):
            kbd, vbd = _block_diag(kb, kh), _block_diag(vb, kh)
            q2, do2 = _stack_pairs(qb, kh), _stack_pairs(da, kh)
            sc_t = _dot_nt(kbd, _stack_pairs(qs, kh))
            dp_t = _dot_nt(vbd, do2)
            p_t, ds_t = [], []
            for half in range(2):
                rows = slice(half * 3 * BLK, (half + 1) * 3 * BLK)
                p, ps = _softmax_sink(sc_t[rows], valid, _sink_of(sink_ref, kh, half, 1), 0)
                delta = jnp.sum(p * dp_t[rows], axis=0, keepdims=True)
                p_t.append(p.astype(BF16))
                ds_t.append((p * (dp_t[rows] - delta)).astype(BF16))
                dsk = -ps * delta
                for pair in range(2):
                    h = GROUP * kh + 2 * pair + half
                    part = jnp.sum(dsk[:, pair * BLK:(pair + 1) * BLK], axis=1, keepdims=True)
                    dsink_ref[h:h + 1, :] += jnp.broadcast_to(part, (1, 128))
            p_t = jnp.concatenate(p_t, axis=0)
            ds_t = jnp.concatenate(ds_t, axis=0)
            dq2 = _dot_tn(ds_t, kbd) * SCORE_SCALE
            dq_ref[:, (2 * kh) * 128:(2 * kh + 1) * 128] = dq2[:BLK]
            dq_ref[:, (2 * kh + 1) * 128:(2 * kh + 2) * 128] = dq2[BLK:]
            dk_fold.append(_fold_diag(_dot_nn(ds_t, q2)) * SCORE_SCALE)
            dv_fold.append(_fold_diag(_dot_nn(p_t, do2)))
        lo = lax.broadcasted_iota(jnp.int32, (3 * BLK, KV_W), 1) < HEAD_DIM
        dk_all = jnp.where(lo, dk_fold[0], dk_fold[1])
        dv_all = jnp.where(lo, dv_fold[0], dv_fold[1])
        for t in range(3):
            dk_ref[t] = dk_all[t * BLK:(t + 1) * BLK]
            dv_ref[t] = dv_all[t * BLK:(t + 1) * BLK]
        ext = jnp.concatenate([p0[...], p1[...], p2[...]], axis=0).astype(BF16)
        dpe = jnp.concatenate([d0[...], d1[...], d2[...]], axis=0)
        dpc_cur = d1[...]
        for gi in range(POOL_G):
            gsl = slice(gi * POOL_GW, (gi + 1) * POOL_GW)
            wg = pw_ref[gi].astype(BF16)
            sc = ps_ref[:, gsl]
            dgr = _dot_nn(pb_ref[gi], ext[:, gsl])
            dgb = dgr.astype(BF16)
            yg = _dot_nn(dgb, wg)
            dps_ref[:, gsl] += jnp.sum(dpc_cur[:, gsl] * yg, axis=0, keepdims=True)
            dpw_ref[gi] += _dot_tn(dgb, (dpc_cur[:, gsl] * sc).astype(BF16))
            dd = _dot_nt((dpe[:, gsl] * sc).astype(BF16), wg)
            dpc_ref[:, gsl] = _dot_nn(ptb_ref[gi], dd.astype(BF16))

    def band(w, col=0):
        return [pl.BlockSpec((BLK, w), lambda n: (jnp.maximum(n - 1, 0), col)),
                pl.BlockSpec((BLK, w), lambda n: (n, col)),
                pl.BlockSpec((BLK, w), lambda n: (jnp.minimum(n + 1, nb - 1), col))]

    tab = lambda: pl.BlockSpec((None, POOL_G, BLK, 3 * BLK), lambda n: (_variant_index(n, nb), 0, 0, 0))
    fixed = lambda shape: pl.BlockSpec(shape, lambda n: (0,) * len(shape))
    return _call(
        body, name, (nb,),
        [pl.BlockSpec(memory_space=pltpu.SMEM), pl.BlockSpec((BLK, ATTN_W), lambda n: (n, 0)),
         *band(KV_W), *band(KV_W), *band(POOL_W),
         pl.BlockSpec((BLK, ATTN_W), lambda n: (n, 0)), *band(POOL_W, 1),
         _whole((POOL_G, POOL_GW, POOL_GW)), _whole((1, POOL_W)), tab(), tab()],
        [pl.BlockSpec((BLK, ATTN_W), lambda n: (n, 0)),
         pl.BlockSpec((None, 3, BLK, KV_W), lambda n: (n, 0, 0, 0)),
         pl.BlockSpec((None, 3, BLK, KV_W), lambda n: (n, 0, 0, 0)),
         pl.BlockSpec((BLK, POOL_W), lambda n: (n, 0)),
         fixed((N_HEADS, 128)), fixed((POOL_G, POOL_GW, POOL_GW)), fixed((1, POOL_W))],
        [jax.ShapeDtypeStruct((s, ATTN_W), F32), jax.ShapeDtypeStruct((nb, 3, BLK, KV_W), F32),
         jax.ShapeDtypeStruct((nb, 3, BLK, KV_W), F32), jax.ShapeDtypeStruct((s, POOL_W), F32),
         jax.ShapeDtypeStruct((N_HEADS, 128), F32),
         jax.ShapeDtypeStruct((POOL_G, POOL_GW, POOL_GW), F32), jax.ShapeDtypeStruct((1, POOL_W), F32)],
        (sink, q, k, k, k, v, v, v, pc, pc, pc, dmix, dmix, dmix, dmix, pool_w, pool_scale, pband, ptband),
        ("arbitrary",), carry)


def _mix_out(h, mix, w_out, name, tm):
    s, d = h.shape
    w = mix.shape[1]

    def body(h_ref, m_ref, w_ref, o_ref):
        o_ref[...] = h_ref[...] + _dot_nn(m_ref[...], w_ref[...])

    row = lambda c: pl.BlockSpec((tm, c), lambda i: (i, 0))
    return pl.pallas_call(
        body, name=name, grid=(s // tm,), in_specs=[row(d), row(w), _whole((w, d))], out_specs=row(d),
        out_shape=jax.ShapeDtypeStruct((s, d), F32), compiler_params=_cparams(("arbitrary",)),
    )(h, mix, w_out)


def _mix_out_bwd(dh, w_out, name, tm):
    s, d = dh.shape
    w = w_out.shape[0]

    def body(dh_ref, w_ref, o_ref, dhb_ref):
        dhb = dh_ref[...].astype(BF16)
        dhb_ref[...] = dhb
        o_ref[...] = _dot_nt(dhb, w_ref[...])

    row = lambda c: pl.BlockSpec((tm, c), lambda i: (i, 0))
    return pl.pallas_call(
        body, name=name, grid=(s // tm,), in_specs=[row(d), _whole((w, d))], out_specs=[row(w), row(d)],
        out_shape=[jax.ShapeDtypeStruct((s, w), F32), jax.ShapeDtypeStruct((s, d), BF16)],
        compiler_params=_cparams(("arbitrary",)),
    )(dh, w_out)


def _mix_in_bwd(h, dh, g, win_t, dq, dkp, dvp, dpc, tabs, name):
    s, d = h.shape
    nb = s // BLK
    n_in = win_t.shape[0]

    def body(h_ref, dh_ref, g_ref, w_ref, dq_ref, ka, kb_, kc, va, vb_, vc, dpc_ref, c_ref, sa_ref, sb_ref,
             o_ref, du_ref, dg_ref):
        n = pl.program_id(0)
        lo = (n > 0).astype(F32)
        hi = (n < nb - 1).astype(F32)
        dk = ka[...] * lo + kb_[...] + kc[...] * hi
        dv = va[...] * lo + vb_[...] + vc[...] * hi
        c, sa, sb = c_ref[...], sa_ref[...], sb_ref[...]
        du = jnp.concatenate([_rope_bwd(dq_ref[...], c, sa, sb, ATTN_W // 128), _rope_bwd(dk, c, sa, sb, 1), dv,
                              dpc_ref[...]], axis=1).astype(BF16)
        du_ref[...] = du
        dn = _dot_nn(du, w_ref[...])
        x = h_ref[...]
        r = lax.rsqrt(jnp.mean(x * x, axis=-1, keepdims=True) + EPS)
        xhat = x * r
        dxhat = dn * g_ref[...]
        o_ref[...] = dh_ref[...] + r * (dxhat - xhat * jnp.mean(dxhat * xhat, axis=-1, keepdims=True))

        @pl.when(n == 0)
        def _():
            dg_ref[...] = jnp.zeros_like(dg_ref)

        dg_ref[...] += jnp.sum(dn * xhat, axis=0, keepdims=True)

    row = lambda w: pl.BlockSpec((BLK, w), lambda n: (n, 0))
    parts = [pl.BlockSpec((None, None, BLK, KV_W), lambda n: (jnp.maximum(n - 1, 0), 2, 0, 0)),
             pl.BlockSpec((None, None, BLK, KV_W), lambda n: (n, 1, 0, 0)),
             pl.BlockSpec((None, None, BLK, KV_W), lambda n: (jnp.minimum(n + 1, nb - 1), 0, 0, 0))]
    return pl.pallas_call(
        body, name=name, grid=(nb,),
        in_specs=[row(d), row(d), _whole((1, d)), _whole((n_in, d)), row(ATTN_W), *parts, *parts, row(POOL_W),
                  row(128), row(128), row(128)],
        out_specs=[row(d), row(n_in), pl.BlockSpec((1, d), lambda n: (0, 0))],
        out_shape=[jax.ShapeDtypeStruct((s, d), F32), jax.ShapeDtypeStruct((s, n_in), BF16),
                   jax.ShapeDtypeStruct((1, d), F32)],
        compiler_params=_cparams(("arbitrary",)),
    )(h, dh, g, win_t, dq, dkp, dkp, dkp, dvp, dvp, dvp, dpc, *tabs)


def _final(h, g, target, name, tm):
    s, d = h.shape

    def body(h_ref, g_ref, t_ref, loss_ref, dh_ref, dg_ref):
        @pl.when(pl.program_id(0) == 0)
        def _():
            loss_ref[...] = jnp.zeros_like(loss_ref)
            dg_ref[...] = jnp.zeros_like(dg_ref)

        x = h_ref[...]
        gg = g_ref[...]
        r = lax.rsqrt(jnp.mean(x * x, axis=-1, keepdims=True) + EPS)
        xhat = x * r
        e = xhat * gg - t_ref[...]
        per_tok = jnp.mean(e * e, axis=-1, keepdims=True)
        loss_ref[...] += 0.5 * jnp.sum(per_tok, axis=0, keepdims=True)
        dy = e * (1.0 / d)
        dg_ref[...] += jnp.sum(dy * xhat, axis=0, keepdims=True)
        dxhat = dy * gg
        dh_ref[...] = r * (dxhat - xhat * jnp.mean(dxhat * xhat, axis=-1, keepdims=True))

    row = pl.BlockSpec((tm, d), lambda i: (i, 0))
    return pl.pallas_call(
        body, name=name, grid=(s // tm,), in_specs=[row, _whole((1, d)), row],
        out_specs=[pl.BlockSpec((1, 1), lambda i: (0, 0)), row, pl.BlockSpec((1, d), lambda i: (0, 0))],
        out_shape=[jax.ShapeDtypeStruct((1, 1), F32), jax.ShapeDtypeStruct((s, d), F32),
                   jax.ShapeDtypeStruct((1, d), F32)],
        compiler_params=_cparams(("arbitrary",)),
    )(h, g, target)


def _adam_math(w, g, m, v):
    m = ADAM_B1 * m + (1.0 - ADAM_B1) * g
    v = ADAM_B2 * v + (1.0 - ADAM_B2) * (g * g)
    m_hat = m / (1.0 - ADAM_B1 ** ADAM_STEP)
    v_hat = v / (1.0 - ADAM_B2 ** ADAM_STEP)
    delta = -ADAM_LR * (m_hat / (jnp.sqrt(v_hat) + ADAM_EPS) + ADAM_WD * w)
    return delta, m, v


def _adam_small(w, parts, m, v, name):
    rows, cols = w.shape

    def body(w_ref, p_ref, m_ref, v_ref, g_ref, d_ref, nm_ref, nv_ref):
        g = p_ref[0]
        for k in range(1, N_DEV):
            g = g + p_ref[k]
        g_ref[...] = g
        d_ref[...], nm_ref[...], nv_ref[...] = _adam_math(w_ref[...], g, m_ref[...], v_ref[...])

    return pl.pallas_call(
        body, name=name, out_shape=[jax.ShapeDtypeStruct((rows, cols), F32)] * 4,
    )(w, parts, m, v)


def _pack_small(norm1, normm, norm2, normf, sink, pool_w, pool_scale):
    scale_rows = jnp.pad(pool_scale.reshape(4, 128), ((0, 4), (0, 0)))
    sink_rows = jnp.pad(sink.reshape(1, N_HEADS), ((0, 7), (0, 128 - N_HEADS)))
    return jnp.concatenate([pool_w.reshape(512, 128), norm1.reshape(8, 128), normm.reshape(8, 128),
                            norm2.reshape(8, 128), normf.reshape(8, 128), scale_rows, sink_rows], axis=0)


def _unpack_small(p):
    return dict(pool_w=p[:512].reshape(1, POOL_G, POOL_GW, POOL_GW), ffn1_norm=p[512:520].reshape(1, 1024),
                mix_norm=p[520:528].reshape(1, 1024), ffn2_norm=p[528:536].reshape(1, 1024),
                final_norm=p[536:544].reshape(1024), pool_scale=p[544:548].reshape(1, POOL_W),
                sink_logits=p[552, :N_HEADS].reshape(1, N_HEADS))


def kernel(x, ffn1_norm, ffn1_w_gate, ffn1_w_up, ffn1_w_down, mix_norm, w_in, sink_logits, pool_w, pool_scale, w_out, ffn2_norm, ffn2_w_gate, ffn2_w_up, ffn2_w_down, final_norm, loss_target, m_ffn1_norm, m_ffn1_w_gate, m_ffn1_w_up, m_ffn1_w_down, m_mix_norm, m_w_in, m_sink_logits, m_pool_w, m_pool_scale, m_w_out, m_ffn2_norm, m_ffn2_w_gate, m_ffn2_w_up, m_ffn2_w_down, m_final_norm, v_ffn1_norm, v_ffn1_w_gate, v_ffn1_w_up, v_ffn1_w_down, v_mix_norm, v_w_in, v_sink_logits, v_pool_w, v_pool_scale, v_w_out, v_ffn2_norm, v_ffn2_w_gate, v_ffn2_w_up, v_ffn2_w_down, v_final_norm):
    s, d = x.shape[1], x.shape[2]
    fk = ffn1_w_gate.shape[2]
    f = N_DEV * fk
    ink = w_in.shape[2]
    n_in = N_DEV * ink
    mixk = w_out.shape[1]
    tm = min(512, s)
    tm_bwd = min(256, s)
    pos = jnp.stack([lax.axis_index("x"), lax.axis_index("y"), lax.axis_index("c")]).astype(jnp.int32)

    t_bf = lambda w: w[0].T.astype(BF16)
    full = lambda a: a.reshape(N_DEV * a.shape[1], d)
    first = [t_bf(ffn1_w_gate), t_bf(ffn1_w_up), ffn1_w_down[0].astype(BF16)]
    wg1, wu1, wd1 = map(full, _run_exchange(_AllGather(first), "gather_ffn1"))
    second = _AllGather([t_bf(w_in), w_out[0].astype(BF16), t_bf(ffn2_w_gate), t_bf(ffn2_w_up)])
    third = _AllGather([ffn2_w_down[0].astype(BF16)])

    tabs = _rope_tables(s)
    pband, ptband = _pool_tables(s)
    g1, gm, g2, gf = ffn1_norm, mix_norm, ffn2_norm, final_norm.reshape(1, d)

    x0 = x[0]
    h1, n1, gate1, up1, *gathered = _ffn_fwd(x0, g1, wg1, wu1, wd1, "ffn1_fwd", tm, carry=second)
    win_t, wout, wg2, wu2 = map(full, gathered)
    q, k, v, pc, n2 = _mix_in(h1, gm, win_t, tabs, "mix_in", tm)
    mix, wd2 = _attn_pool_fwd(q, k, v, pc, sink_logits, pool_w[0], pool_scale, pband, "attn_pool_fwd", carry=third)
    wd2 = full(wd2)
    h2 = _mix_out(h1, mix, wout, "mix_out", tm)
    h3, n3, gate2, up2 = _ffn_fwd(h2, g2, wg2, wu2, wd2, "ffn2_fwd", tm)
    loss_part, dh3, dgf = _final(h3, gf, loss_target[0], "final", tm)

    tk = min(512, s)
    f_half = f // 2 if (f // 2) % 128 == 0 else f
    gw, sum1, recv2 = {}, {}, {}

    def stage1(keys, rows):
        r1 = _run_exchange(_RsStage1([gw[key][1].reshape(N_DEV, rows, d) for key in keys]), "rs1_" + keys[0])
        for key, r in zip(keys, r1):
            sum1[key] = _rs_sum1(pos, gw[key][0].reshape(N_DEV, rows, d), r, "rs_sum1_" + key)

    def stage2(keys):
        return _RsStage2([sum1[key][1] for key in keys])

    dh2, dg2, dgate2, dup2, act2, dhh3 = _ffn_bwd(h2, dh3, gate2, up2, g2, wg2, wu2, wd2, "ffn2_bwd", tm_bwd)
    gw["g2"] = _wgrad(dgate2, n3, "wgrad_gate2", f_half, tk)
    gw["u2"] = _wgrad(dup2, n3, "wgrad_up2", f_half, tk)
    gw["d2"] = _wgrad(act2, dhh3, "wgrad_down2", f_half, tk)
    stage1(["g2", "u2", "d2"], fk)
    dmix, dh2b = _mix_out_bwd(dh2, wout, "mix_out_bwd", tm)
    gw["out"] = _wgrad(mix, dh2b, "wgrad_out", mix.shape[1], tk)
    stage1(["out"], mixk)
    dq, dkp, dvp, dpc, dsink, dpw, dps, *r2 = _attn_pool_bwd(
        q, k, v, pc, dmix, sink_logits, pool_w[0], pool_scale, pband, ptband, "attn_pool_bwd",
        carry=stage2(["g2", "u2", "d2", "out"]))
    recv2.update(zip(["g2", "u2", "d2", "out"], r2))
    dh1, du, dgm = _mix_in_bwd(h1, dh2, gm, win_t, dq, dkp, dvp, dpc, tabs, "mix_in_bwd")
    gw["in"] = _wgrad(du, n2, "wgrad_in", n_in, tk)
    stage1(["in"], ink)
    dx, dg1, dgate1, dup1, act1, dhh1 = _ffn_bwd(x0, dh1, gate1, up1, g1, wg1, wu1, wd1, "ffn1_bwd", tm_bwd)
    *gw["g1"], recv2["in"] = _wgrad(dgate1, n1, "wgrad_gate1", f_half, tk, carry=stage2(["in"]))
    stage1(["g1"], fk)
    *gw["u1"], recv2["g1"] = _wgrad(dup1, n1, "wgrad_up1", f_half, tk, carry=stage2(["g1"]))
    stage1(["u1"], fk)
    *gw["d1"], recv2["u1"] = _wgrad(act1, dhh1, "wgrad_down1", f_half, tk, carry=stage2(["u1"]))
    stage1(["d1"], fk)
    (recv2["d1"],) = _run_exchange(stage2(["d1"]), "rs2_d1")

    small_part = _pack_small(dg1, dgm, dg2, dgf, dsink[:, 0], dpw, dps)
    (small_all,) = _run_exchange(_AllGather([small_part]), "gather_small_grads")
    pk = lambda a, b, c_, e, s_, pw_, psc: _pack_small(a, b, c_, e, s_[0], pw_[0], psc)
    small_w = pk(ffn1_norm, mix_norm, ffn2_norm, final_norm, sink_logits, pool_w, pool_scale)
    small_m = pk(m_ffn1_norm, m_mix_norm, m_ffn2_norm, m_final_norm, m_sink_logits, m_pool_w, m_pool_scale)
    small_v = pk(v_ffn1_norm, v_mix_norm, v_ffn2_norm, v_final_norm, v_sink_logits, v_pool_w, v_pool_scale)
    sg, sd, sm, sv = [_unpack_small(a) for a in _adam_small(small_w, small_all, small_m, small_v, "adam_small")]

    big = {}
    keys = ["g1", "u1", "d1", "g2", "u2", "d2", "in", "out"]
    names = ["ffn1_w_gate", "ffn1_w_up", "ffn1_w_down", "ffn2_w_gate", "ffn2_w_up", "ffn2_w_down", "w_in", "w_out"]
    transposed = [True, True, False, True, True, False, True, False]
    ws = [ffn1_w_gate, ffn1_w_up, ffn1_w_down, ffn2_w_gate, ffn2_w_up, ffn2_w_down, w_in, w_out]
    ms = [m_ffn1_w_gate, m_ffn1_w_up, m_ffn1_w_down, m_ffn2_w_gate, m_ffn2_w_up, m_ffn2_w_down, m_w_in, m_w_out]
    vs = [v_ffn1_w_gate, v_ffn1_w_up, v_ffn1_w_down, v_ffn2_w_gate, v_ffn2_w_up, v_ffn2_w_down, v_w_in, v_w_out]
    for key, nm, tr, w, m, vv in zip(keys, names, transposed, ws, ms, vs):
        view = (lambda a: jnp.swapaxes(a, 1, 2)[0]) if tr else (lambda a: a[0])
        back = (lambda a: jnp.swapaxes(a[None], 1, 2)) if tr else (lambda a: a[None])
        res = _rs_sum2_adam(pos, sum1[key][0], recv2[key], view(w), view(m), view(vv), "adam_" + nm)
        big[nm] = tuple(back(a) for a in res)

    loss = lax.psum(loss_part[0, 0], ("x", "y", "c"))
    all_names = ["ffn1_norm", "ffn1_w_gate", "ffn1_w_up", "ffn1_w_down", "mix_norm", "w_in", "sink_logits", "pool_w",
                 "pool_scale", "w_out", "ffn2_norm", "ffn2_w_gate", "ffn2_w_up", "ffn2_w_down", "final_norm"]
    outs = [loss, dx[None]]
    for idx, src in enumerate((sg, sd, sm, sv)):
        for nm in all_names:
            outs.append(big[nm][idx] if nm in big else src[nm])
    return tuple(outs)
```

```python
import functools

import jax
import jax.numpy as jnp
import numpy as np
from jax import lax
from jax.experimental import pallas as pl
from jax.experimental.pallas import tpu as pltpu

F32 = jnp.float32
BF16 = jnp.bfloat16
MESH = pl.DeviceIdType.MESH
N_DEV = 8

EPS = 1e-6
HEAD_DIM = 64
N_HEADS = 8
N_KV = 2
GROUP = N_HEADS // N_KV
ATTN_W = N_HEADS * HEAD_DIM
KV_W = N_KV * HEAD_DIM
POOL_W = 512
POOL_G = 4
POOL_GW = POOL_W // POOL_G
POOL_WINDOWS = (2, 4, 8, 16)
BLK = 128
ROT = 16
ROPE_THETA = 500000.0
SCORE_SCALE = HEAD_DIM ** -0.5

ADAM_LR, ADAM_B1, ADAM_B2, ADAM_EPS, ADAM_WD, ADAM_STEP = 0.001, 0.9, 0.999, 1e-08, 0.01, 10

VMEM_LIMIT = 56 * 1024 * 1024


def _cparams(sem=None, **kw):
    if sem is not None:
        kw["dimension_semantics"] = sem
    return pltpu.CompilerParams(vmem_limit_bytes=VMEM_LIMIT, **kw)


def _whole(shape):
    nd = len(shape)
    return pl.BlockSpec(shape, lambda *_: (0,) * nd, pipeline_mode=pl.Buffered(1))


def _sigmoid(z):
    return 1.0 / (1.0 + jnp.exp(-z))


def _dot_nt(a, b):
    return lax.dot_general(a, b, (((1,), (1,)), ((), ())), preferred_element_type=F32)


def _dot_nn(a, b):
    return lax.dot_general(a, b, (((1,), (0,)), ((), ())), preferred_element_type=F32)


def _dot_tn(a, b):
    return lax.dot_general(a, b, (((0,), (0,)), ((), ())), preferred_element_type=F32)


def _rope_tables(s):
    inv_freq = ROPE_THETA ** (-np.arange(0, ROT, 2, dtype=np.float64) / ROT)
    ang = np.arange(s, dtype=np.float64)[:, None] * inv_freq[None, :]
    c = np.ones((s, HEAD_DIM)); sa = np.zeros((s, HEAD_DIM)); sb = np.zeros((s, HEAD_DIM))
    c[:, :8] = np.cos(ang); c[:, 8:16] = np.cos(ang)
    sa[:, :8] = -np.sin(ang)
    sb[:, 8:16] = np.sin(ang)
    t = lambda a: jnp.asarray(np.tile(a, (1, 2)).astype(np.float32))
    return t(c), t(sa), t(sb)


def _pool_weight(gi, t, s_pos, s):
    half = POOL_WINDOWS[gi] // 2

    def win(lo, hi):
        a = np.clip(lo, 0, s); b = np.clip(hi + 1, 0, s)
        inside = (s_pos >= a) & (s_pos < b)
        return inside / np.maximum(b - a, 1)

    w = 0.5 * (win(t - half, t + half - 1) + win(t - half + 1, t + half)) - (t == s_pos)
    return w * ((t >= 0) & (t < s) & (s_pos >= 0) & (s_pos < s))


def _pool_tables(s):
    nb = s // BLK
    fwd = np.zeros((3, POOL_G, BLK, 3 * BLK), np.float32)
    bwd = np.zeros((3, POOL_G, BLK, 3 * BLK), np.float32)
    for vi, n in enumerate((0, 1 if nb > 2 else 0, nb - 1)):
        i = n * BLK + np.arange(BLK)[:, None]
        j = (n - 1) * BLK + np.arange(3 * BLK)[None, :]
        for gi in range(POOL_G):
            fwd[vi, gi] = _pool_weight(gi, i, j, s)
            bwd[vi, gi] = _pool_weight(gi, j, i, s)
    return jnp.asarray(fwd, dtype=BF16), jnp.asarray(bwd, dtype=BF16)


def _variant_index(n, nb):
    return jnp.where(n == 0, 0, jnp.where(n == nb - 1, 2, 1))


class _Exchange:
    inputs = ()
    out_shapes = ()
    sems = ()

    def start(self, srcs, outs, sems):
        raise NotImplementedError

    def finish(self, srcs, outs, sems):
        raise NotImplementedError


class _AllGather(_Exchange):
    def __init__(self, arrays):
        n = len(arrays)
        self.inputs = list(arrays)
        self.out_shapes = [jax.ShapeDtypeStruct((N_DEV,) + a.shape, a.dtype) for a in arrays]
        self.sems = [pltpu.SemaphoreType.DMA((n, 7)), pltpu.SemaphoreType.DMA((n, 7)), pltpu.SemaphoreType.DMA((n,))]

    def _parts(self, srcs, outs, sems):
        send_sems, recv_sems, local_sems = sems
        n = len(srcs)
        x, y, c = lax.axis_index("x"), lax.axis_index("y"), lax.axis_index("c")
        me, sibling = (x, y, c), (x, y, 1 - c)
        chips = [(1 - x, y), (x, 1 - y), (1 - x, 1 - y)]

        def slot(a, dev):
            return outs[a].at[4 * dev[0] + 2 * dev[1] + dev[2]]

        def copy(a, k, block, to, src=None):
            return pltpu.make_async_remote_copy(
                src_ref=slot(a, block) if src is None else src, dst_ref=slot(a, block),
                send_sem=send_sems.at[a, k], recv_sem=recv_sems.at[a, k], device_id=to, device_id_type=MESH)

        mine = [pltpu.make_async_copy(srcs[a], slot(a, me), local_sems.at[a]) for a in range(n)]
        first = []
        for a in range(n):
            first.append(copy(a, 0, me, sibling, src=srcs[a]))
            first += [copy(a, 1 + j, me, (*chip, c), src=srcs[a]) for j, chip in enumerate(chips)]
        return n, c, me, sibling, chips, copy, mine, first

    def start(self, srcs, outs, sems):
        _, _, _, _, _, _, mine, first = self._parts(srcs, outs, sems)
        for cp in mine + first:
            cp.start()

    def finish(self, srcs, outs, sems):
        n, c, me, sibling, chips, copy, mine, first = self._parts(srcs, outs, sems)
        passed = []
        for j, chip in enumerate(chips):
            for a in range(n):
                copy(a, 1 + j, (*chip, c), me).wait_recv()
                fwd = copy(a, 4 + j, (*chip, c), sibling)
                fwd.start()
                passed.append(fwd)
        for a in range(n):
            copy(a, 0, sibling, me).wait_recv()
            for j, chip in enumerate(chips):
                copy(a, 4 + j, (*chip, 1 - c), me).wait_recv()
        for cp in first + passed:
            cp.wait_send()
        for cp in mine:
            cp.wait()


class _RsStage1(_Exchange):
    def __init__(self, gbs):
        n = len(gbs)
        self.inputs = list(gbs)
        self.out_shapes = [jax.ShapeDtypeStruct((4,) + g.shape[1:], g.dtype) for g in gbs]
        self.sems = [pltpu.SemaphoreType.DMA((n, 4)), pltpu.SemaphoreType.DMA((n, 4))]

    def _copies(self, srcs, outs, sems):
        send_sems, recv_sems = sems
        x, y, c = lax.axis_index("x"), lax.axis_index("y"), lax.axis_index("c")
        return [pltpu.make_async_remote_copy(
            src_ref=srcs[a].at[4 * (q // 2) + 2 * (q % 2) + (1 - c)], dst_ref=outs[a].at[q],
            send_sem=send_sems.at[a, q], recv_sem=recv_sems.at[a, q], device_id=(x, y, 1 - c), device_id_type=MESH)
            for a in range(len(srcs)) for q in range(4)]

    def start(self, srcs, outs, sems):
        for cp in self._copies(srcs, outs, sems):
            cp.start()

    def finish(self, srcs, outs, sems):
        copies = self._copies(srcs, outs, sems)
        for cp in copies:
            cp.wait_recv()
        for cp in copies:
            cp.wait_send()


class _RsStage2(_RsStage1):
    def __init__(self, pbs):
        n = len(pbs)
        self.inputs = list(pbs)
        self.out_shapes = [jax.ShapeDtypeStruct((3,) + p.shape[1:], p.dtype) for p in pbs]
        self.sems = [pltpu.SemaphoreType.DMA((n, 3)), pltpu.SemaphoreType.DMA((n, 3))]

    def _copies(self, srcs, outs, sems):
        send_sems, recv_sems = sems
        x, y, c = lax.axis_index("x"), lax.axis_index("y"), lax.axis_index("c")
        chips = [(1 - x, y), (x, 1 - y), (1 - x, 1 - y)]
        return [pltpu.make_async_remote_copy(
            src_ref=srcs[a].at[2 * chip[0] + chip[1]], dst_ref=outs[a].at[j], send_sem=send_sems.at[a, j],
            recv_sem=recv_sems.at[a, j], device_id=(*chip, c), device_id_type=MESH)
            for a in range(len(srcs)) for j, chip in enumerate(chips)]


class _DirectGather(_Exchange):
    def __init__(self, arrays):
        n = len(arrays)
        self.inputs = list(arrays)
        self.out_shapes = [jax.ShapeDtypeStruct((N_DEV,) + a.shape, a.dtype) for a in arrays]
        self.sems = [pltpu.SemaphoreType.DMA((n, 7)), pltpu.SemaphoreType.DMA((n, 7)), pltpu.SemaphoreType.DMA((n,))]

    def _copies(self, srcs, outs, sems):
        send_sems, recv_sems, local_sems = sems
        x, y, c = lax.axis_index("x"), lax.axis_index("y"), lax.axis_index("c")
        me = 4 * x + 2 * y + c
        remote, local = [], []
        for a in range(len(srcs)):
            local.append(pltpu.make_async_copy(srcs[a], outs[a].at[me], local_sems.at[a]))
            for k in range(1, N_DEV):
                peer = (x ^ (k >> 2), y ^ ((k >> 1) & 1), c ^ (k & 1))
                remote.append(pltpu.make_async_remote_copy(
                    src_ref=srcs[a], dst_ref=outs[a].at[me], send_sem=send_sems.at[a, k - 1],
                    recv_sem=recv_sems.at[a, k - 1], device_id=peer, device_id_type=MESH))
        return remote, local

    def start(self, srcs, outs, sems):
        remote, local = self._copies(srcs, outs, sems)
        for cp in local + remote:
            cp.start()

    def finish(self, srcs, outs, sems):
        remote, local = self._copies(srcs, outs, sems)
        for cp in remote:
            cp.wait_recv()
        for cp in remote:
            cp.wait_send()
        for cp in local:
            cp.wait()


class _Both(_Exchange):
    def __init__(self, a, b):
        self.a, self.b = a, b
        self.inputs = list(a.inputs) + list(b.inputs)
        self.out_shapes = list(a.out_shapes) + list(b.out_shapes)
        self.sems = list(a.sems) + list(b.sems)

    def _split(self, srcs, outs, sems):
        na, oa, sa = len(self.a.inputs), len(self.a.out_shapes), len(self.a.sems)
        return (srcs[:na], outs[:oa], sems[:sa]), (srcs[na:], outs[oa:], sems[sa:])

    def start(self, srcs, outs, sems):
        pa, pb = self._split(srcs, outs, sems)
        self.a.start(*pa)
        self.b.start(*pb)

    def finish(self, srcs, outs, sems):
        pa, pb = self._split(srcs, outs, sems)
        self.a.finish(*pa)
        self.b.finish(*pb)


_ANY = pl.BlockSpec(memory_space=pl.ANY)


def _run_exchange(ex, name):
    n_in, n_out = len(ex.inputs), len(ex.out_shapes)

    def body(*refs):
        srcs, outs, sems = refs[:n_in], refs[n_in:n_in + n_out], refs[n_in + n_out:]
        ex.start(srcs, outs, sems)
        ex.finish(srcs, outs, sems)

    return pl.pallas_call(
        body, name=name, out_shape=list(ex.out_shapes), in_specs=[_ANY] * n_in, out_specs=[_ANY] * n_out,
        scratch_shapes=list(ex.sems),
    )(*ex.inputs)


def _call(body, name, grid, in_specs, out_specs, out_shape, args, sem, carry=None, **kw):
    if carry is None:
        return pl.pallas_call(functools.partial(body), name=name, grid=grid, in_specs=in_specs, out_specs=out_specs,
                              out_shape=out_shape, compiler_params=_cparams(sem), **kw)(*args)
    n_in, n_out = len(in_specs), len(out_specs)
    nc_in, nc_out = len(carry.inputs), len(carry.out_shapes)

    def carried(*refs):
        ins = refs[:n_in]
        c_in = refs[n_in:n_in + nc_in]
        outs = refs[n_in + nc_in:n_in + nc_in + n_out]
        c_out = refs[n_in + nc_in + n_out:n_in + nc_in + n_out + nc_out]
        sems = refs[n_in + nc_in + n_out + nc_out:]
        ids = [pl.program_id(i) for i in range(len(grid))]
        is_first = functools.reduce(jnp.logical_and, [i == 0 for i in ids])
        is_last = functools.reduce(jnp.logical_and, [i == g - 1 for i, g in zip(ids, grid)])

        @pl.when(is_first)
        def _():
            carry.start(c_in, c_out, sems)

        body(*ins, *outs)

        @pl.when(is_last)
        def _():
            carry.finish(c_in, c_out, sems)

    return pl.pallas_call(
        carried, name=name, grid=grid, in_specs=list(in_specs) + [_ANY] * nc_in,
        out_specs=list(out_specs) + [_ANY] * nc_out, out_shape=list(out_shape) + list(carry.out_shapes),
        scratch_shapes=list(carry.sems), compiler_params=_cparams(sem), **kw)(*args, *carry.inputs)


def _rs_sum1(pos, g, r1, name):
    _, rows, d = g.shape
    tr = rows
    for cand in (512, 352, 256, 160, 128):
        if rows % cand == 0:
            tr = cand
            break

    def body(pos_ref, g_ref, r_ref, p_ref, pb_ref):
        p = g_ref[...] + r_ref[...].astype(F32)
        p_ref[...] = p
        pb_ref[...] = p.astype(BF16)

    def g_map(q, i, pos_ref):
        return (4 * (q // 2) + 2 * (q % 2) + pos_ref[2], i, 0)

    blk = lambda q, i, pos_ref: (q, i, 0)
    return pl.pallas_call(
        body, name=name,
        grid_spec=pltpu.PrefetchScalarGridSpec(
            num_scalar_prefetch=1, grid=(4, rows // tr),
            in_specs=[pl.BlockSpec((None, tr, d), g_map), pl.BlockSpec((None, tr, d), blk)],
            out_specs=[pl.BlockSpec((None, tr, d), blk), pl.BlockSpec((None, tr, d), blk)]),
        out_shape=[jax.ShapeDtypeStruct((4, rows, d), F32), jax.ShapeDtypeStruct((4, rows, d), BF16)],
        compiler_params=_cparams(("arbitrary", "arbitrary")),
    )(pos, g, r1)


def _rs_sum2_adam(pos, p, r2, w, m, v, name):
    _, rows, d = p.shape
    tr = rows // 2 if rows % 16 == 0 else rows

    def body(pos_ref, p_ref, r_ref, w_ref, m_ref, v_ref, g_ref, d_ref, nm_ref, nv_ref):
        r = r_ref[...].astype(F32)
        g = ((p_ref[...] + r[0]) + r[1]) + r[2]
        g_ref[...] = g
        d_ref[...], nm_ref[...], nv_ref[...] = _adam_math(w_ref[...], g, m_ref[...], v_ref[...])

    blk = pl.BlockSpec((tr, d), lambda i, pos_ref: (i, 0))
    return pl.pallas_call(
        body, name=name,
        grid_spec=pltpu.PrefetchScalarGridSpec(
            num_scalar_prefetch=1, grid=(rows // tr,),
            in_specs=[pl.BlockSpec((None, tr, d), lambda i, pos_ref: (2 * pos_ref[0] + pos_ref[1], i, 0)),
                      pl.BlockSpec((3, tr, d), lambda i, pos_ref: (0, i, 0)), blk, blk, blk],
            out_specs=[blk] * 4),
        out_shape=[jax.ShapeDtypeStruct((rows, d), F32)] * 4,
        compiler_params=_cparams(("arbitrary",)),
    )(pos, p, r2, w, m, v)


def _ffn_chunk(f):
    for cand in (256, 128):
        if f % cand == 0:
            return cand
    return f


def _ffn_fwd(h, g, wg_t, wu_t, wd, name, tm, carry=None):
    s, d = h.shape
    f = wd.shape[0]
    tf = _ffn_chunk(f)

    def body(h_ref, g_ref, wg_ref, wu_ref, wd_ref, o_ref, n_ref, gate_ref, up_ref, act_ref):
        x = h_ref[...]
        r = lax.rsqrt(jnp.mean(x * x, axis=-1, keepdims=True) + EPS)
        nb = (x * r * g_ref[...]).astype(BF16)
        n_ref[...] = nb
        for j in range(f // tf):
            sl = slice(j * tf, (j + 1) * tf)
            gate = _dot_nt(nb, wg_ref[sl, :])
            up = _dot_nt(nb, wu_ref[sl, :])
            gate_ref[:, sl] = gate.astype(BF16)
            up_ref[:, sl] = up.astype(BF16)
            act_ref[:, sl] = (gate * _sigmoid(gate) * up).astype(BF16)
        o_ref[...] = x + 0.5 * _dot_nn(act_ref[...], wd_ref[...])

    row = lambda w: pl.BlockSpec((tm, w), lambda i: (i, 0))
    return _call(
        body, name, (s // tm,),
        [row(d), _whole((1, d)), _whole((f, d)), _whole((f, d)), _whole((f, d))],
        [row(d), row(d), row(f), row(f), row(f)],
        [jax.ShapeDtypeStruct((s, d), F32), jax.ShapeDtypeStruct((s, d), BF16),
         jax.ShapeDtypeStruct((s, f), BF16), jax.ShapeDtypeStruct((s, f), BF16), jax.ShapeDtypeStruct((s, f), BF16)],
        (h, g, wg_t, wu_t, wd), ("arbitrary",), carry)


def _gate_grads(dh_ref, gate_ref, up_ref, wd_ref, dgate_ref, dup_ref, dhh_ref, tf):
    dhh = (0.5 * dh_ref[...]).astype(BF16)
    dhh_ref[...] = dhh
    for j in range(gate_ref.shape[1] // tf):
        sl = slice(j * tf, (j + 1) * tf)
        gt = gate_ref[:, sl].astype(F32)
        u = up_ref[:, sl].astype(F32)
        dact = _dot_nt(dhh, wd_ref[sl, :])
        sg = _sigmoid(gt)
        dup_ref[:, sl] = (dact * (gt * sg)).astype(BF16)
        dgate_ref[:, sl] = (dact * u * (sg * (1.0 + gt * (1.0 - sg)))).astype(BF16)


def _input_grad(h_ref, dh_ref, dgate_ref, dup_ref, g_ref, wg_ref, wu_ref, o_ref, dg_ref):
    x = h_ref[...]
    r = lax.rsqrt(jnp.mean(x * x, axis=-1, keepdims=True) + EPS)
    xhat = x * r
    dn = _dot_nn(dgate_ref[...], wg_ref[...]) + _dot_nn(dup_ref[...], wu_ref[...])
    dxhat = dn * g_ref[...]
    o_ref[...] = dh_ref[...] + r * (dxhat - xhat * jnp.mean(dxhat * xhat, axis=-1, keepdims=True))

    @pl.when(pl.program_id(0) == 0)
    def _():
        dg_ref[...] = jnp.zeros_like(dg_ref)

    dg_ref[...] += jnp.sum(dn * xhat, axis=0, keepdims=True)


def _ffn_bwd(h_in, dh_out, gate, up, g, wg_t, wu_t, wd, name, tm):
    s, d = h_in.shape
    f = gate.shape[1]
    tf = _ffn_chunk(f)

    def body(h_ref, dh_ref, gate_ref, up_ref, g_ref, wg_ref, wu_ref, wd_ref,
             o_ref, dg_ref, dgate_ref, dup_ref, dhh_ref):
        _gate_grads(dh_ref, gate_ref, up_ref, wd_ref, dgate_ref, dup_ref, dhh_ref, tf)
        _input_grad(h_ref, dh_ref, dgate_ref, dup_ref, g_ref, wg_ref, wu_ref, o_ref, dg_ref)

    row = lambda w: pl.BlockSpec((tm, w), lambda i: (i, 0))
    return pl.pallas_call(
        body, name=name, grid=(s // tm,),
        in_specs=[row(d), row(d), row(f), row(f), _whole((1, d)), _whole((f, d)), _whole((f, d)), _whole((f, d))],
        out_specs=[row(d), pl.BlockSpec((1, d), lambda i: (0, 0)), row(f), row(f), row(d)],
        out_shape=[jax.ShapeDtypeStruct((s, d), F32), jax.ShapeDtypeStruct((1, d), F32),
                   jax.ShapeDtypeStruct((s, f), BF16), jax.ShapeDtypeStruct((s, f), BF16),
                   jax.ShapeDtypeStruct((s, d), BF16)],
        compiler_params=_cparams(("arbitrary",)),
    )(h_in, dh_out, gate, up, g, wg_t, wu_t, wd)


def _ffn_bwd_gates(dh_out, gate, up, wd, name, tm, carry=None):
    s, d = dh_out.shape
    f = gate.shape[1]
    tf = _ffn_chunk(f)

    def body(dh_ref, gate_ref, up_ref, wd_ref, dgate_ref, dup_ref, dhh_ref):
        _gate_grads(dh_ref, gate_ref, up_ref, wd_ref, dgate_ref, dup_ref, dhh_ref, tf)

    row = lambda w: pl.BlockSpec((tm, w), lambda i: (i, 0))
    return _call(
        body, name, (s // tm,), [row(d), row(f), row(f), _whole((f, d))], [row(f), row(f), row(d)],
        [jax.ShapeDtypeStruct((s, f), BF16), jax.ShapeDtypeStruct((s, f), BF16), jax.ShapeDtypeStruct((s, d), BF16)],
        (dh_out, gate, up, wd), ("arbitrary",), carry)


def _ffn_bwd_input(h_in, dh_out, dgate, dup, g, wg_t, wu_t, name, tm, carry=None):
    s, d = h_in.shape
    f = dgate.shape[1]

    row = lambda w: pl.BlockSpec((tm, w), lambda i: (i, 0))
    return _call(
        _input_grad, name, (s // tm,),
        [row(d), row(d), row(f), row(f), _whole((1, d)), _whole((f, d)), _whole((f, d))],
        [row(d), pl.BlockSpec((1, d), lambda i: (0, 0))],
        [jax.ShapeDtypeStruct((s, d), F32), jax.ShapeDtypeStruct((1, d), F32)],
        (h_in, dh_out, dgate, dup, g, wg_t, wu_t), ("arbitrary",), carry)


def _wgrad(a, b, name, tf, carry=None):
    s, f = a.shape
    d = b.shape[1]

    def body(a_ref, b_ref, o_ref, ob_ref):
        acc = _dot_tn(a_ref[...], b_ref[...])
        o_ref[...] = acc
        ob_ref[...] = acc.astype(BF16)

    return _call(
        body, name, (f // tf,),
        [pl.BlockSpec((s, tf), lambda i: (0, i)), _whole((s, d))],
        [pl.BlockSpec((tf, d), lambda i: (i, 0)), pl.BlockSpec((tf, d), lambda i: (i, 0))],
        [jax.ShapeDtypeStruct((f, d), F32), jax.ShapeDtypeStruct((f, d), BF16)],
        (a, b), ("arbitrary",), carry)


def _rope(t, c, sa, sb, reps):
    c, sa, sb = (jnp.tile(v, (1, reps)) if reps > 1 else v for v in (c, sa, sb))
    w = t.shape[1]
    return t * c + pltpu.roll(t, w - 8, 1) * sa + pltpu.roll(t, 8, 1) * sb


def _rope_bwd(dt, c, sa, sb, reps):
    c, sa, sb = (jnp.tile(v, (1, reps)) if reps > 1 else v for v in (c, sa, sb))
    w = dt.shape[1]
    return dt * c + pltpu.roll(dt * sa, 8, 1) + pltpu.roll(dt * sb, w - 8, 1)


def _mix_in(h, g, win_t, tabs, name, tm):
    s, d = h.shape
    n_in = win_t.shape[0]

    def body(h_ref, g_ref, w_ref, c_ref, sa_ref, sb_ref, q_ref, k_ref, v_ref, pc_ref, n_ref):
        x = h_ref[...]
        r = lax.rsqrt(jnp.mean(x * x, axis=-1, keepdims=True) + EPS)
        nb = (x * r * g_ref[...]).astype(BF16)
        n_ref[...] = nb
        u = _dot_nt(nb, w_ref[...])
        c, sa, sb = c_ref[...], sa_ref[...], sb_ref[...]
        q_ref[...] = _rope(u[:, :ATTN_W], c, sa, sb, ATTN_W // 128).astype(BF16)
        k_ref[...] = _rope(u[:, ATTN_W:ATTN_W + KV_W], c, sa, sb, 1).astype(BF16)
        v_ref[...] = u[:, ATTN_W + KV_W:ATTN_W + 2 * KV_W].astype(BF16)
        pc_ref[...] = u[:, ATTN_W + 2 * KV_W:]

    row = lambda w: pl.BlockSpec((tm, w), lambda i: (i, 0))
    return pl.pallas_call(
        body, name=name, grid=(s // tm,),
        in_specs=[row(d), _whole((1, d)), _whole((n_in, d)), row(128), row(128), row(128)],
        out_specs=[row(ATTN_W), row(KV_W), row(KV_W), row(POOL_W), row(d)],
        out_shape=[jax.ShapeDtypeStruct((s, ATTN_W), BF16), jax.ShapeDtypeStruct((s, KV_W), BF16),
                   jax.ShapeDtypeStruct((s, KV_W), BF16), jax.ShapeDtypeStruct((s, POOL_W), F32),
                   jax.ShapeDtypeStruct((s, d), BF16)],
        compiler_params=_cparams(("arbitrary",)),
    )(h, g, win_t, *tabs)


def _band_mask(n, nb, transposed):
    shape = (3 * BLK, 2 * BLK) if transposed else (2 * BLK, 3 * BLK)
    i = lax.broadcasted_iota(jnp.int32, shape, 1 if transposed else 0) % BLK
    j = lax.broadcasted_iota(jnp.int32, shape, 0 if transposed else 1)
    kpos = (n - 1) * BLK + j
    return (j >= i) & (j <= i + 2 * BLK) & (kpos >= 0) & (kpos < nb * BLK)


def _block_diag(t, kh):
    tf = t.astype(F32)
    tr = pltpu.roll(tf, HEAD_DIM, 1)
    lo = lax.broadcasted_iota(jnp.int32, tf.shape, 1) < HEAD_DIM
    top, bot = (tf, tr) if kh == 0 else (tr, tf)
    return jnp.concatenate([jnp.where(lo, top, 0.0), jnp.where(lo, 0.0, bot)], axis=0).astype(BF16)


def _fold_diag(tbd):
    lo = lax.broadcasted_iota(jnp.int32, (3 * BLK, 2 * HEAD_DIM), 1) < HEAD_DIM
    t = jnp.where(lo, tbd[:3 * BLK], tbd[3 * BLK:])
    return t + pltpu.roll(t, HEAD_DIM, 1)


def _stack_pairs(x, kh):
    return jnp.concatenate([x[:, (2 * kh) * 128:(2 * kh + 1) * 128], x[:, (2 * kh + 1) * 128:(2 * kh + 2) * 128]], axis=0)


def _sink_of(sink_ref, kh, half, axis):
    shape = (2 * BLK, 1) if axis == 0 else (1, 2 * BLK)
    first = lax.broadcasted_iota(jnp.int32, shape, axis) < BLK
    return jnp.where(first, sink_ref[0, GROUP * kh + half], sink_ref[0, GROUP * kh + 2 + half])


def _softmax_sink(sc, valid, sink, axis):
    sc = jnp.where(valid, sc, -1e30)
    m = jnp.maximum(jnp.max(sc, axis=axis, keepdims=True), sink)
    e = jnp.exp(sc - m)
    es = jnp.exp(sink - m)
    inv = 1.0 / (jnp.sum(e, axis=axis, keepdims=True) + es)
    return e * inv, es * inv


def _attn_pool_fwd(q, k, v, pc, sink, pool_w, pool_scale, pband, name, carry=None):
    s = q.shape[0]
    nb = s // BLK

    def body(sink_ref, q_ref, k0, k1, k2, v0, v1, v2, p0, p1, p2, pw_ref, ps_ref, pb_ref, o_ref):
        n = pl.program_id(0)
        valid = _band_mask(n, nb, False)
        kb = jnp.concatenate([k0[...], k1[...], k2[...]], axis=0)
        vb = jnp.concatenate([v0[...], v1[...], v2[...]], axis=0)
        qs = q_ref[...] * SCORE_SCALE
        for kh in range(N_KV):
            sc = _dot_nt(_stack_pairs(qs, kh), _block_diag(kb, kh))
            p = [_softmax_sink(sc[:, half * 3 * BLK:(half + 1) * 3 * BLK], valid, _sink_of(sink_ref, kh, half, 0), 1)[0]
                 for half in range(2)]
            o2 = _dot_nn(jnp.concatenate(p, axis=1).astype(BF16), _block_diag(vb, kh)).astype(BF16)
            o_ref[:, (2 * kh) * 128:(2 * kh + 1) * 128] = o2[:BLK]
            o_ref[:, (2 * kh + 1) * 128:(2 * kh + 2) * 128] = o2[BLK:]
        ext = jnp.concatenate([p0[...], p1[...], p2[...]], axis=0).astype(BF16)
        for gi in range(POOL_G):
            gsl = slice(gi * POOL_GW, (gi + 1) * POOL_GW)
            dg = _dot_nn(pb_ref[gi], ext[:, gsl])
            yg = _dot_nn(dg.astype(BF16), pw_ref[gi].astype(BF16))
            o_ref[:, ATTN_W + gi * POOL_GW:ATTN_W + (gi + 1) * POOL_GW] = (yg * ps_ref[:, gsl]).astype(BF16)

    def band(w):
        return [pl.BlockSpec((BLK, w), lambda n: (jnp.maximum(n - 1, 0), 0)),
                pl.BlockSpec((BLK, w), lambda n: (n, 0)),
                pl.BlockSpec((BLK, w), lambda n: (jnp.minimum(n + 1, nb - 1), 0))]

    return _call(
        body, name, (nb,),
        [pl.BlockSpec(memory_space=pltpu.SMEM), pl.BlockSpec((BLK, ATTN_W), lambda n: (n, 0)),
         *band(KV_W), *band(KV_W), *band(POOL_W),
         _whole((POOL_G, POOL_GW, POOL_GW)), _whole((1, POOL_W)),
         pl.BlockSpec((None, POOL_G, BLK, 3 * BLK), lambda n: (_variant_index(n, nb), 0, 0, 0))],
        [pl.BlockSpec((BLK, ATTN_W + POOL_W), lambda n: (n, 0))],
        [jax.ShapeDtypeStruct((s, ATTN_W + POOL_W), BF16)],
        (sink, q, k, k, k, v, v, v, pc, pc, pc, pool_w, pool_scale, pband), ("arbitrary",), carry)


def _attn_pool_bwd(q, k, v, pc, dmix, sink, pool_w, pool_scale, pband, ptband, name, carry=None):
    s = q.shape[0]
    nb = s // BLK

    def body(sink_ref, q_ref, k0, k1, k2, v0, v1, v2, p0, p1, p2, da_ref, d0, d1, d2, pw_ref, ps_ref, pb_ref, ptb_ref,
             dq_ref, dk_ref, dv_ref, dpc_ref, dsink_ref, dpw_ref, dps_ref):
        n = pl.program_id(0)

        @pl.when(n == 0)
        def _():
            dsink_ref[...] = jnp.zeros_like(dsink_ref)
            dpw_ref[...] = jnp.zeros_like(dpw_ref)
            dps_ref[...] = jnp.zeros_like(dps_ref)

        valid = _band_mask(n, nb, True)
        kb = jnp.concatenate([k0[...], k1[...], k2[...]], axis=0)
        vb = jnp.concatenate([v0[...], v1[...], v2[...]], axis=0)
        qb = q_ref[...]
        qs = qb * SCORE_SCALE
        da = da_ref[...].astype(BF16)
        dk_fold, dv_fold = [], []
        for kh in range(N_KV):
            kbd, vbd = _block_diag(kb, kh), _block_diag(vb, kh)
            q2, do2 = _stack_pairs(qb, kh), _stack_pairs(da, kh)
            sc_t = _dot_nt(kbd, _stack_pairs(qs, kh))
            dp_t = _dot_nt(vbd, do2)
            p_t, ds_t = [], []
            for half in range(2):
                rows = slice(half * 3 * BLK, (half + 1) * 3 * BLK)
                p, ps = _softmax_sink(sc_t[rows], valid, _sink_of(sink_ref, kh, half, 1), 0)
                delta = jnp.sum(p * dp_t[rows], axis=0, keepdims=True)
                p_t.append(p.astype(BF16))
                ds_t.append((p * (dp_t[rows] - delta)).astype(BF16))
                dsk = -ps * delta
                for pair in range(2):
                    h = GROUP * kh + 2 * pair + half
                    part = jnp.sum(dsk[:, pair * BLK:(pair + 1) * BLK], axis=1, keepdims=True)
                    dsink_ref[h:h + 1, :] += jnp.broadcast_to(part, (1, 128))
            p_t = jnp.concatenate(p_t, axis=0)
            ds_t = jnp.concatenate(ds_t, axis=0)
            dq2 = _dot_tn(ds_t, kbd) * SCORE_SCALE
            dq_ref[:, (2 * kh) * 128:(2 * kh + 1) * 128] = dq2[:BLK]
            dq_ref[:, (2 * kh + 1) * 128:(2 * kh + 2) * 128] = dq2[BLK:]
            dk_fold.append(_fold_diag(_dot_nn(ds_t, q2)) * SCORE_SCALE)
            dv_fold.append(_fold_diag(_dot_nn(p_t, do2)))
        lo = lax.broadcasted_iota(jnp.int32, (3 * BLK, KV_W), 1) < HEAD_DIM
        dk_all = jnp.where(lo, dk_fold[0], dk_fold[1])
        dv_all = jnp.where(lo, dv_fold[0], dv_fold[1])
        for t in range(3):
            dk_ref[t] = dk_all[t * BLK:(t + 1) * BLK]
            dv_ref[t] = dv_all[t * BLK:(t + 1) * BLK]
        ext = jnp.concatenate([p0[...], p1[...], p2[...]], axis=0).astype(BF16)
        dpe = jnp.concatenate([d0[...], d1[...], d2[...]], axis=0)
        dpc_cur = d1[...]
        for gi in range(POOL_G):
            gsl = slice(gi * POOL_GW, (gi + 1) * POOL_GW)
            wg = pw_ref[gi].astype(BF16)
            sc = ps_ref[:, gsl]
            dgr = _dot_nn(pb_ref[gi], ext[:, gsl])
            dgb = dgr.astype(BF16)
            yg = _dot_nn(dgb, wg)
            dps_ref[:, gsl] += jnp.sum(dpc_cur[:, gsl] * yg, axis=0, keepdims=True)
            dpw_ref[gi] += _dot_tn(dgb, (dpc_cur[:, gsl] * sc).astype(BF16))
            dd = _dot_nt((dpe[:, gsl] * sc).astype(BF16), wg)
            dpc_ref[:, gsl] = _dot_nn(ptb_ref[gi], dd.astype(BF16))

    def band(w, col=0):
        return [pl.BlockSpec((BLK, w), lambda n: (jnp.maximum(n - 1, 0), col)),
                pl.BlockSpec((BLK, w), lambda n: (n, col)),
                pl.BlockSpec((BLK, w), lambda n: (jnp.minimum(n + 1, nb - 1), col))]

    tab = lambda: pl.BlockSpec((None, POOL_G, BLK, 3 * BLK), lambda n: (_variant_index(n, nb), 0, 0, 0))
    fixed = lambda shape: pl.BlockSpec(shape, lambda n: (0,) * len(shape))
    return _call(
        body, name, (nb,),
        [pl.BlockSpec(memory_space=pltpu.SMEM), pl.BlockSpec((BLK, ATTN_W), lambda n: (n, 0)),
         *band(KV_W), *band(KV_W), *band(POOL_W),
         pl.BlockSpec((BLK, ATTN_W), lambda n: (n, 0)), *band(POOL_W, 1),
         _whole((POOL_G, POOL_GW, POOL_GW)), _whole((1, POOL_W)), tab(), tab()],
        [pl.BlockSpec((BLK, ATTN_W), lambda n: (n, 0)),
         pl.BlockSpec((None, 3, BLK, KV_W), lambda n: (n, 0, 0, 0)),
         pl.BlockSpec((None, 3, BLK, KV_W), lambda n: (n, 0, 0, 0)),
         pl.BlockSpec((BLK, POOL_W), lambda n: (n, 0)),
         fixed((N_HEADS, 128)), fixed((POOL_G, POOL_GW, POOL_GW)), fixed((1, POOL_W))],
        [jax.ShapeDtypeStruct((s, ATTN_W), F32), jax.ShapeDtypeStruct((nb, 3, BLK, KV_W), F32),
         jax.ShapeDtypeStruct((nb, 3, BLK, KV_W), F32), jax.ShapeDtypeStruct((s, POOL_W), F32),
         jax.ShapeDtypeStruct((N_HEADS, 128), F32),
         jax.ShapeDtypeStruct((POOL_G, POOL_GW, POOL_GW), F32), jax.ShapeDtypeStruct((1, POOL_W), F32)],
        (sink, q, k, k, k, v, v, v, pc, pc, pc, dmix, dmix, dmix, dmix, pool_w, pool_scale, pband, ptband),
        ("arbitrary",), carry)


def _mix_out(h, mix, w_out, name, tm):
    s, d = h.shape
    w = mix.shape[1]

    def body(h_ref, m_ref, w_ref, o_ref):
        o_ref[...] = h_ref[...] + _dot_nn(m_ref[...], w_ref[...])

    row = lambda c: pl.BlockSpec((tm, c), lambda i: (i, 0))
    return pl.pallas_call(
        body, name=name, grid=(s // tm,), in_specs=[row(d), row(w), _whole((w, d))], out_specs=row(d),
        out_shape=jax.ShapeDtypeStruct((s, d), F32), compiler_params=_cparams(("arbitrary",)),
    )(h, mix, w_out)


def _mix_out_bwd(dh, w_out, name, tm):
    s, d = dh.shape
    w = w_out.shape[0]

    def body(dh_ref, w_ref, o_ref, dhb_ref):
        dhb = dh_ref[...].astype(BF16)
        dhb_ref[...] = dhb
        o_ref[...] = _dot_nt(dhb, w_ref[...])

    row = lambda c: pl.BlockSpec((tm, c), lambda i: (i, 0))
    return pl.pallas_call(
        body, name=name, grid=(s // tm,), in_specs=[row(d), _whole((w, d))], out_specs=[row(w), row(d)],
        out_shape=[jax.ShapeDtypeStruct((s, w), F32), jax.ShapeDtypeStruct((s, d), BF16)],
        compiler_params=_cparams(("arbitrary",)),
    )(dh, w_out)


def _mix_in_bwd(h, dh, g, win_t, dq, dkp, dvp, dpc, tabs, name):
    s, d = h.shape
    nb = s // BLK
    n_in = win_t.shape[0]

    def body(h_ref, dh_ref, g_ref, w_ref, dq_ref, ka, kb_, kc, va, vb_, vc, dpc_ref, c_ref, sa_ref, sb_ref,
             o_ref, du_ref, dg_ref):
        n = pl.program_id(0)
        lo = (n > 0).astype(F32)
        hi = (n < nb - 1).astype(F32)
        dk = ka[...] * lo + kb_[...] + kc[...] * hi
        dv = va[...] * lo + vb_[...] + vc[...] * hi
        c, sa, sb = c_ref[...], sa_ref[...], sb_ref[...]
        du = jnp.concatenate([_rope_bwd(dq_ref[...], c, sa, sb, ATTN_W // 128), _rope_bwd(dk, c, sa, sb, 1), dv,
                              dpc_ref[...]], axis=1).astype(BF16)
        du_ref[...] = du
        dn = _dot_nn(du, w_ref[...])
        x = h_ref[...]
        r = lax.rsqrt(jnp.mean(x * x, axis=-1, keepdims=True) + EPS)
        xhat = x * r
        dxhat = dn * g_ref[...]
        o_ref[...] = dh_ref[...] + r * (dxhat - xhat * jnp.mean(dxhat * xhat, axis=-1, keepdims=True))

        @pl.when(n == 0)
        def _():
            dg_ref[...] = jnp.zeros_like(dg_ref)

        dg_ref[...] += jnp.sum(dn * xhat, axis=0, keepdims=True)

    row = lambda w: pl.BlockSpec((BLK, w), lambda n: (n, 0))
    parts = [pl.BlockSpec((None, None, BLK, KV_W), lambda n: (jnp.maximum(n - 1, 0), 2, 0, 0)),
             pl.BlockSpec((None, None, BLK, KV_W), lambda n: (n, 1, 0, 0)),
             pl.BlockSpec((None, None, BLK, KV_W), lambda n: (jnp.minimum(n + 1, nb - 1), 0, 0, 0))]
    return pl.pallas_call(
        body, name=name, grid=(nb,),
        in_specs=[row(d), row(d), _whole((1, d)), _whole((n_in, d)), row(ATTN_W), *parts, *parts, row(POOL_W),
                  row(128), row(128), row(128)],
        out_specs=[row(d), row(n_in), pl.BlockSpec((1, d), lambda n: (0, 0))],
        out_shape=[jax.ShapeDtypeStruct((s, d), F32), jax.ShapeDtypeStruct((s, n_in), BF16),
                   jax.ShapeDtypeStruct((1, d), F32)],
        compiler_params=_cparams(("arbitrary",)),
    )(h, dh, g, win_t, dq, dkp, dkp, dkp, dvp, dvp, dvp, dpc, *tabs)


def _final(h, g, target, name, tm):
    s, d = h.shape

    def body(h_ref, g_ref, t_ref, loss_ref, dh_ref, dg_ref):
        @pl.when(pl.program_id(0) == 0)
        def _():
            loss_ref[...] = jnp.zeros_like(loss_ref)
            dg_ref[...] = jnp.zeros_like(dg_ref)

        x = h_ref[...]
        gg = g_ref[...]
        r = lax.rsqrt(jnp.mean(x * x, axis=-1, keepdims=True) + EPS)
        xhat = x * r
        e = xhat * gg - t_ref[...]
        per_tok = jnp.mean(e * e, axis=-1, keepdims=True)
        loss_ref[...] += 0.5 * jnp.sum(per_tok, axis=0, keepdims=True)
        dy = e * (1.0 / d)
        dg_ref[...] += jnp.sum(dy * xhat, axis=0, keepdims=True)
        dxhat = dy * gg
        dh_ref[...] = r * (dxhat - xhat * jnp.mean(dxhat * xhat, axis=-1, keepdims=True))

    row = pl.BlockSpec((tm, d), lambda i: (i, 0))
    return pl.pallas_call(
        body, name=name, grid=(s // tm,), in_specs=[row, _whole((1, d)), row],
        out_specs=[pl.BlockSpec((1, 1), lambda i: (0, 0)), row, pl.BlockSpec((1, d), lambda i: (0, 0))],
        out_shape=[jax.ShapeDtypeStruct((1, 1), F32), jax.ShapeDtypeStruct((s, d), F32),
                   jax.ShapeDtypeStruct((1, d), F32)],
        compiler_params=_cparams(("arbitrary",)),
    )(h, g, target)


def _adam_math(w, g, m, v):
    m = ADAM_B1 * m + (1.0 - ADAM_B1) * g
    v = ADAM_B2 * v + (1.0 - ADAM_B2) * (g * g)
    m_hat = m / (1.0 - ADAM_B1 ** ADAM_STEP)
    v_hat = v / (1.0 - ADAM_B2 ** ADAM_STEP)
    delta = -ADAM_LR * (m_hat / (jnp.sqrt(v_hat) + ADAM_EPS) + ADAM_WD * w)
    return delta, m, v


def _adam_small(w, parts, late, m, v, name):
    rows, cols = w.shape

    def body(w_ref, p_ref, l_ref, m_ref, v_ref, g_ref, d_ref, nm_ref, nv_ref):
        g, gl = p_ref[0], l_ref[0]
        for k in range(1, N_DEV):
            g = g + p_ref[k]
            gl = gl + l_ref[k]
        g_ref[...] = g
        g_ref[SMALL_NORM1:SMALL_NORM1 + 8, :] = g[SMALL_NORM1:SMALL_NORM1 + 8] + gl
        d_ref[...], nm_ref[...], nv_ref[...] = _adam_math(w_ref[...], g_ref[...], m_ref[...], v_ref[...])

    return pl.pallas_call(
        body, name=name, out_shape=[jax.ShapeDtypeStruct((rows, cols), F32)] * 4,
    )(w, parts, late, m, v)


SMALL_NORM1 = 512


def _pack_small(norm1, normm, norm2, normf, sink, pool_w, pool_scale, loss=None):
    scale_rows = jnp.pad(pool_scale.reshape(4, 128), ((0, 4), (0, 0)))
    last_rows = jnp.pad(sink.reshape(1, N_HEADS), ((0, 7), (0, 128 - N_HEADS)))
    if loss is not None:
        last_rows = last_rows + jnp.pad(loss.reshape(1, 1), ((1, 6), (0, 127)))
    return jnp.concatenate([pool_w.reshape(512, 128), norm1.reshape(8, 128), normm.reshape(8, 128),
                            norm2.reshape(8, 128), normf.reshape(8, 128), scale_rows, last_rows], axis=0)


def _unpack_small(p):
    return dict(pool_w=p[:512].reshape(1, POOL_G, POOL_GW, POOL_GW), ffn1_norm=p[512:520].reshape(1, 1024),
                mix_norm=p[520:528].reshape(1, 1024), ffn2_norm=p[528:536].reshape(1, 1024),
                final_norm=p[536:544].reshape(1024), pool_scale=p[544:548].reshape(1, POOL_W),
                sink_logits=p[552, :N_HEADS].reshape(1, N_HEADS), loss=p[553, 0])


def kernel(x, ffn1_norm, ffn1_w_gate, ffn1_w_up, ffn1_w_down, mix_norm, w_in, sink_logits, pool_w, pool_scale, w_out, ffn2_norm, ffn2_w_gate, ffn2_w_up, ffn2_w_down, final_norm, loss_target, m_ffn1_norm, m_ffn1_w_gate, m_ffn1_w_up, m_ffn1_w_down, m_mix_norm, m_w_in, m_sink_logits, m_pool_w, m_pool_scale, m_w_out, m_ffn2_norm, m_ffn2_w_gate, m_ffn2_w_up, m_ffn2_w_down, m_final_norm, v_ffn1_norm, v_ffn1_w_gate, v_ffn1_w_up, v_ffn1_w_down, v_mix_norm, v_w_in, v_sink_logits, v_pool_w, v_pool_scale, v_w_out, v_ffn2_norm, v_ffn2_w_gate, v_ffn2_w_up, v_ffn2_w_down, v_final_norm):
    s, d = x.shape[1], x.shape[2]
    fk = ffn1_w_gate.shape[2]
    f = N_DEV * fk
    ink = w_in.shape[2]
    n_in = N_DEV * ink
    mixk = w_out.shape[1]
    tm = min(512, s)
    tm_bwd = min(256, s)
    pos = jnp.stack([lax.axis_index("x"), lax.axis_index("y"), lax.axis_index("c")]).astype(jnp.int32)

    t_bf = lambda w: w[0].T.astype(BF16)
    full = lambda a: a.reshape(N_DEV * a.shape[1], d)
    first = [t_bf(ffn1_w_gate), t_bf(ffn1_w_up), ffn1_w_down[0].astype(BF16)]
    wg1, wu1, wd1 = map(full, _run_exchange(_AllGather(first), "gather_ffn1"))
    second = _AllGather([t_bf(w_in), w_out[0].astype(BF16), t_bf(ffn2_w_gate), t_bf(ffn2_w_up)])
    third = _AllGather([ffn2_w_down[0].astype(BF16)])

    tabs = _rope_tables(s)
    pband, ptband = _pool_tables(s)
    g1, gm, g2, gf = ffn1_norm, mix_norm, ffn2_norm, final_norm.reshape(1, d)

    x0 = x[0]
    h1, n1, gate1, up1, act1, *gathered = _ffn_fwd(x0, g1, wg1, wu1, wd1, "ffn1_fwd", tm, carry=second)
    win_t, wout, wg2, wu2 = map(full, gathered)
    q, k, v, pc, n2 = _mix_in(h1, gm, win_t, tabs, "mix_in", tm)
    mix, wd2 = _attn_pool_fwd(q, k, v, pc, sink_logits, pool_w[0], pool_scale, pband, "attn_pool_fwd", carry=third)
    wd2 = full(wd2)
    h2 = _mix_out(h1, mix, wout, "mix_out", tm)
    h3, n3, gate2, up2, act2 = _ffn_fwd(h2, g2, wg2, wu2, wd2, "ffn2_fwd", tm)
    loss_part, dh3, dgf = _final(h3, gf, loss_target[0], "final", tm)

    tw = 256 if f % 256 == 0 else 128
    gw, sum1, recv2 = {}, {}, {}

    def stage1(keys, rows):
        r1 = _run_exchange(_RsStage1([gw[key][1].reshape(N_DEV, rows, d) for key in keys]), "rs1_" + keys[0])
        for key, r in zip(keys, r1):
            sum1[key] = _rs_sum1(pos, gw[key][0].reshape(N_DEV, rows, d), r, "rs_sum1_" + key)

    def stage2(keys):
        return _RsStage2([sum1[key][1] for key in keys])

    dh2, dg2, dgate2, dup2, dhh3 = _ffn_bwd(h2, dh3, gate2, up2, g2, wg2, wu2, wd2, "ffn2_bwd", tm_bwd)
    gw["g2"] = _wgrad(dgate2, n3, "wgrad_gate2", tw)
    gw["u2"] = _wgrad(dup2, n3, "wgrad_up2", tw)
    gw["d2"] = _wgrad(act2, dhh3, "wgrad_down2", tw)
    stage1(["g2", "u2", "d2"], fk)
    dmix, dh2b = _mix_out_bwd(dh2, wout, "mix_out_bwd", tm)
    gw["out"] = _wgrad(mix, dh2b, "wgrad_out", tw)
    stage1(["out"], mixk)
    dq, dkp, dvp, dpc, dsink, dpw, dps, *r2 = _attn_pool_bwd(
        q, k, v, pc, dmix, sink_logits, pool_w[0], pool_scale, pband, ptband, "attn_pool_bwd",
        carry=stage2(["g2", "u2", "d2", "out"]))
    recv2.update(zip(["g2", "u2", "d2", "out"], r2))
    dh1, du, dgm = _mix_in_bwd(h1, dh2, gm, win_t, dq, dkp, dvp, dpc, tabs, "mix_in_bwd")
    gw["in"] = _wgrad(du, n2, "wgrad_in", tw)
    stage1(["in"], ink)
    small_part = _pack_small(jnp.zeros_like(dgm), dgm, dg2, dgf, dsink[:, 0], dpw, dps, loss_part)
    dgate1, dup1, dhh1, small_all = _ffn_bwd_gates(dh1, gate1, up1, wd1, "ffn1_bwd_gates", tm,
                                                   carry=_AllGather([small_part]))
    *gw["g1"], recv2["in"] = _wgrad(dgate1, n1, "wgrad_gate1", tw, carry=stage2(["in"]))
    stage1(["g1"], fk)
    *gw["u1"], recv2["g1"] = _wgrad(dup1, n1, "wgrad_up1", tw, carry=stage2(["g1"]))
    stage1(["u1"], fk)
    *gw["d1"], recv2["u1"] = _wgrad(act1, dhh1, "wgrad_down1", tw, carry=stage2(["u1"]))
    stage1(["d1"], fk)
    dx, dg1, recv2["d1"] = _ffn_bwd_input(x0, dh1, dgate1, dup1, g1, wg1, wu1, "ffn1_bwd_input", tm,
                                          carry=stage2(["d1"]))

    (dg1_all,) = _run_exchange(_DirectGather([dg1.reshape(8, 128)]), "gather_norm1_grad")
    pk = lambda a, b, c_, e, s_, pw_, psc: _pack_small(a, b, c_, e, s_[0], pw_[0], psc)
    small_w = pk(ffn1_norm, mix_norm, ffn2_norm, final_norm, sink_logits, pool_w, pool_scale)
    small_m = pk(m_ffn1_norm, m_mix_norm, m_ffn2_norm, m_final_norm, m_sink_logits, m_pool_w, m_pool_scale)
    small_v = pk(v_ffn1_norm, v_mix_norm, v_ffn2_norm, v_final_norm, v_sink_logits, v_pool_w, v_pool_scale)
    sg, sd, sm, sv = [_unpack_small(a)
                      for a in _adam_small(small_w, small_all, dg1_all, small_m, small_v, "adam_small")]

    big = {}
    keys = ["g1", "u1", "d1", "g2", "u2", "d2", "in", "out"]
    names = ["ffn1_w_gate", "ffn1_w_up", "ffn1_w_down", "ffn2_w_gate", "ffn2_w_up", "ffn2_w_down", "w_in", "w_out"]
    transposed = [True, True, False, True, True, False, True, False]
    ws = [ffn1_w_gate, ffn1_w_up, ffn1_w_down, ffn2_w_gate, ffn2_w_up, ffn2_w_down, w_in, w_out]
    ms = [m_ffn1_w_gate, m_ffn1_w_up, m_ffn1_w_down, m_ffn2_w_gate, m_ffn2_w_up, m_ffn2_w_down, m_w_in, m_w_out]
    vs = [v_ffn1_w_gate, v_ffn1_w_up, v_ffn1_w_down, v_ffn2_w_gate, v_ffn2_w_up, v_ffn2_w_down, v_w_in, v_w_out]
    for key, nm, tr, w, m, vv in zip(keys, names, transposed, ws, ms, vs):
        view = (lambda a: jnp.swapaxes(a, 1, 2)[0]) if tr else (lambda a: a[0])
        back = (lambda a: jnp.swapaxes(a[None], 1, 2)) if tr else (lambda a: a[None])
        res = _rs_sum2_adam(pos, sum1[key][0], recv2[key], view(w), view(m), view(vv), "adam_" + nm)
        big[nm] = tuple(back(a) for a in res)

    loss = sg["loss"]
    all_names = ["ffn1_norm", "ffn1_w_gate", "ffn1_w_up", "ffn1_w_down", "mix_norm", "w_in", "sink_logits", "pool_w",
                 "pool_scale", "w_out", "ffn2_norm", "ffn2_w_gate", "ffn2_w_up", "ffn2_w_down", "final_norm"]
    outs = [loss, dx[None]]
    for idx, src in enumerate((sg, sd, sm, sv)):
        for nm in all_names:
            outs.append(big[nm][idx] if nm in big else src[nm])
    return tuple(outs)
```

```python
import functools

import jax
import jax.numpy as jnp
import numpy as np
from jax import lax
from jax.experimental import pallas as pl
from jax.experimental.pallas import tpu as pltpu

F32 = jnp.float32
BF16 = jnp.bfloat16
MESH = pl.DeviceIdType.MESH
N_DEV = 8

EPS = 1e-6
HEAD_DIM = 64
N_HEADS = 8
N_KV = 2
GROUP = N_HEADS // N_KV
ATTN_W = N_HEADS * HEAD_DIM
KV_W = N_KV * HEAD_DIM
POOL_W = 512
POOL_G = 4
POOL_GW = POOL_W // POOL_G
POOL_WINDOWS = (2, 4, 8, 16)
BLK = 128
ROT = 16
ROPE_THETA = 500000.0
SCORE_SCALE = HEAD_DIM ** -0.5

ADAM_LR, ADAM_B1, ADAM_B2, ADAM_EPS, ADAM_WD, ADAM_STEP = 0.001, 0.9, 0.999, 1e-08, 0.01, 10

VMEM_LIMIT = 56 * 1024 * 1024


def _cparams(sem=None, **kw):
    if sem is not None:
        kw["dimension_semantics"] = sem
    return pltpu.CompilerParams(vmem_limit_bytes=VMEM_LIMIT, **kw)


def _whole(shape):
    nd = len(shape)
    return pl.BlockSpec(shape, lambda *_: (0,) * nd, pipeline_mode=pl.Buffered(1))


def _sigmoid(z):
    return 1.0 / (1.0 + jnp.exp(-z))


def _dot_nt(a, b):
    return lax.dot_general(a, b, (((1,), (1,)), ((), ())), preferred_element_type=F32)


def _dot_nn(a, b):
    return lax.dot_general(a, b, (((1,), (0,)), ((), ())), preferred_element_type=F32)


def _dot_tn(a, b):
    return lax.dot_general(a, b, (((0,), (0,)), ((), ())), preferred_element_type=F32)


def _rope_tables(s):
    inv_freq = ROPE_THETA ** (-np.arange(0, ROT, 2, dtype=np.float64) / ROT)
    ang = np.arange(s, dtype=np.float64)[:, None] * inv_freq[None, :]
    c = np.ones((s, HEAD_DIM)); sa = np.zeros((s, HEAD_DIM)); sb = np.zeros((s, HEAD_DIM))
    c[:, :8] = np.cos(ang); c[:, 8:16] = np.cos(ang)
    sa[:, :8] = -np.sin(ang)
    sb[:, 8:16] = np.sin(ang)
    t = lambda a: jnp.asarray(np.tile(a, (1, 2)).astype(np.float32))
    return t(c), t(sa), t(sb)


def _pool_weight(gi, t, s_pos, s):
    half = POOL_WINDOWS[gi] // 2

    def win(lo, hi):
        a = np.clip(lo, 0, s); b = np.clip(hi + 1, 0, s)
        inside = (s_pos >= a) & (s_pos < b)
        return inside / np.maximum(b - a, 1)

    w = 0.5 * (win(t - half, t + half - 1) + win(t - half + 1, t + half)) - (t == s_pos)
    return w * ((t >= 0) & (t < s) & (s_pos >= 0) & (s_pos < s))


def _pool_tables(s):
    nb = s // BLK
    fwd = np.zeros((3, POOL_G, BLK, 3 * BLK), np.float32)
    bwd = np.zeros((3, POOL_G, BLK, 3 * BLK), np.float32)
    for vi, n in enumerate((0, 1 if nb > 2 else 0, nb - 1)):
        i = n * BLK + np.arange(BLK)[:, None]
        j = (n - 1) * BLK + np.arange(3 * BLK)[None, :]
        for gi in range(POOL_G):
            fwd[vi, gi] = _pool_weight(gi, i, j, s)
            bwd[vi, gi] = _pool_weight(gi, j, i, s)
    return jnp.asarray(fwd, dtype=BF16), jnp.asarray(bwd, dtype=BF16)


def _variant_index(n, nb):
    return jnp.where(n == 0, 0, jnp.where(n == nb - 1, 2, 1))


class _Exchange:
    inputs = ()
    out_shapes = ()
    sems = ()

    def start(self, srcs, outs, sems):
        raise NotImplementedError

    def finish(self, srcs, outs, sems):
        raise NotImplementedError


class _AllGather(_Exchange):
    def __init__(self, arrays):
        n = len(arrays)
        self.inputs = list(arrays)
        self.out_shapes = [jax.ShapeDtypeStruct((N_DEV,) + a.shape, a.dtype) for a in arrays]
        self.sems = [pltpu.SemaphoreType.DMA((n, 7)), pltpu.SemaphoreType.DMA((n, 7)), pltpu.SemaphoreType.DMA((n,))]

    def _parts(self, srcs, outs, sems):
        send_sems, recv_sems, local_sems = sems
        n = len(srcs)
        x, y, c = lax.axis_index("x"), lax.axis_index("y"), lax.axis_index("c")
        me, sibling = (x, y, c), (x, y, 1 - c)
        chips = [(1 - x, y), (x, 1 - y), (1 - x, 1 - y)]

        def slot(a, dev):
            return outs[a].at[4 * dev[0] + 2 * dev[1] + dev[2]]

        def copy(a, k, block, to, src=None):
            return pltpu.make_async_remote_copy(
                src_ref=slot(a, block) if src is None else src, dst_ref=slot(a, block),
                send_sem=send_sems.at[a, k], recv_sem=recv_sems.at[a, k], device_id=to, device_id_type=MESH)

        mine = [pltpu.make_async_copy(srcs[a], slot(a, me), local_sems.at[a]) for a in range(n)]
        first = []
        for a in range(n):
            first.append(copy(a, 0, me, sibling, src=srcs[a]))
            first += [copy(a, 1 + j, me, (*chip, c), src=srcs[a]) for j, chip in enumerate(chips)]
        return n, c, me, sibling, chips, copy, mine, first

    def start(self, srcs, outs, sems):
        _, _, _, _, _, _, mine, first = self._parts(srcs, outs, sems)
        for cp in mine + first:
            cp.start()

    def finish(self, srcs, outs, sems):
        n, c, me, sibling, chips, copy, mine, first = self._parts(srcs, outs, sems)
        passed = []
        for j, chip in enumerate(chips):
            for a in range(n):
                copy(a, 1 + j, (*chip, c), me).wait_recv()
                fwd = copy(a, 4 + j, (*chip, c), sibling)
                fwd.start()
                passed.append(fwd)
        for a in range(n):
            copy(a, 0, sibling, me).wait_recv()
            for j, chip in enumerate(chips):
                copy(a, 4 + j, (*chip, 1 - c), me).wait_recv()
        for cp in first + passed:
            cp.wait_send()
        for cp in mine:
            cp.wait()


class _RsStage2(_Exchange):
    def start(self, srcs, outs, sems):
        for cp in self._copies(srcs, outs, sems):
            cp.start()

    def finish(self, srcs, outs, sems):
        copies = self._copies(srcs, outs, sems)
        for cp in copies:
            cp.wait_recv()
        for cp in copies:
            cp.wait_send()


    def __init__(self, pbs):
        n = len(pbs)
        self.inputs = list(pbs)
        self.out_shapes = [jax.ShapeDtypeStruct((3,) + p.shape[1:], p.dtype) for p in pbs]
        self.sems = [pltpu.SemaphoreType.DMA((n, 3)), pltpu.SemaphoreType.DMA((n, 3))]

    def _copies(self, srcs, outs, sems):
        send_sems, recv_sems = sems
        x, y, c = lax.axis_index("x"), lax.axis_index("y"), lax.axis_index("c")
        chips = [(1 - x, y), (x, 1 - y), (1 - x, 1 - y)]
        return [pltpu.make_async_remote_copy(
            src_ref=srcs[a].at[2 * chip[0] + chip[1]], dst_ref=outs[a].at[j], send_sem=send_sems.at[a, j],
            recv_sem=recv_sems.at[a, j], device_id=(*chip, c), device_id_type=MESH)
            for a in range(len(srcs)) for j, chip in enumerate(chips)]


class _DirectGather(_Exchange):
    def __init__(self, arrays):
        n = len(arrays)
        self.inputs = list(arrays)
        self.out_shapes = [jax.ShapeDtypeStruct((N_DEV,) + a.shape, a.dtype) for a in arrays]
        self.sems = [pltpu.SemaphoreType.DMA((n, 7)), pltpu.SemaphoreType.DMA((n, 7)), pltpu.SemaphoreType.DMA((n,))]

    def _copies(self, srcs, outs, sems):
        send_sems, recv_sems, local_sems = sems
        x, y, c = lax.axis_index("x"), lax.axis_index("y"), lax.axis_index("c")
        me = 4 * x + 2 * y + c
        remote, local = [], []
        for a in range(len(srcs)):
            local.append(pltpu.make_async_copy(srcs[a], outs[a].at[me], local_sems.at[a]))
            for k in range(1, N_DEV):
                peer = (x ^ (k >> 2), y ^ ((k >> 1) & 1), c ^ (k & 1))
                remote.append(pltpu.make_async_remote_copy(
                    src_ref=srcs[a], dst_ref=outs[a].at[me], send_sem=send_sems.at[a, k - 1],
                    recv_sem=recv_sems.at[a, k - 1], device_id=peer, device_id_type=MESH))
        return remote, local

    def start(self, srcs, outs, sems):
        remote, local = self._copies(srcs, outs, sems)
        for cp in local + remote:
            cp.start()

    def finish(self, srcs, outs, sems):
        remote, local = self._copies(srcs, outs, sems)
        for cp in remote:
            cp.wait_recv()
        for cp in remote:
            cp.wait_send()
        for cp in local:
            cp.wait()


class _Both(_Exchange):
    def __init__(self, a, b):
        self.a, self.b = a, b
        self.inputs = list(a.inputs) + list(b.inputs)
        self.out_shapes = list(a.out_shapes) + list(b.out_shapes)
        self.sems = list(a.sems) + list(b.sems)

    def _split(self, srcs, outs, sems):
        na, oa, sa = len(self.a.inputs), len(self.a.out_shapes), len(self.a.sems)
        return (srcs[:na], outs[:oa], sems[:sa]), (srcs[na:], outs[oa:], sems[sa:])

    def start(self, srcs, outs, sems):
        pa, pb = self._split(srcs, outs, sems)
        self.a.start(*pa)
        self.b.start(*pb)

    def finish(self, srcs, outs, sems):
        pa, pb = self._split(srcs, outs, sems)
        self.a.finish(*pa)
        self.b.finish(*pb)


_ANY = pl.BlockSpec(memory_space=pl.ANY)


def _run_exchange(ex, name):
    n_in, n_out = len(ex.inputs), len(ex.out_shapes)

    def body(*refs):
        srcs, outs, sems = refs[:n_in], refs[n_in:n_in + n_out], refs[n_in + n_out:]
        ex.start(srcs, outs, sems)
        ex.finish(srcs, outs, sems)

    return pl.pallas_call(
        body, name=name, out_shape=list(ex.out_shapes), in_specs=[_ANY] * n_in, out_specs=[_ANY] * n_out,
        scratch_shapes=list(ex.sems),
    )(*ex.inputs)


def _call(body, name, grid, in_specs, out_specs, out_shape, args, sem, carry=None, **kw):
    if carry is None:
        return pl.pallas_call(functools.partial(body), name=name, grid=grid, in_specs=in_specs, out_specs=out_specs,
                              out_shape=out_shape, compiler_params=_cparams(sem), **kw)(*args)
    n_in, n_out = len(in_specs), len(out_specs)
    nc_in, nc_out = len(carry.inputs), len(carry.out_shapes)

    def carried(*refs):
        ins = refs[:n_in]
        c_in = refs[n_in:n_in + nc_in]
        outs = refs[n_in + nc_in:n_in + nc_in + n_out]
        c_out = refs[n_in + nc_in + n_out:n_in + nc_in + n_out + nc_out]
        sems = refs[n_in + nc_in + n_out + nc_out:]
        ids = [pl.program_id(i) for i in range(len(grid))]
        is_first = functools.reduce(jnp.logical_and, [i == 0 for i in ids])
        is_last = functools.reduce(jnp.logical_and, [i == g - 1 for i, g in zip(ids, grid)])

        @pl.when(is_first)
        def _():
            carry.start(c_in, c_out, sems)

        body(*ins, *outs)

        @pl.when(is_last)
        def _():
            carry.finish(c_in, c_out, sems)

    return pl.pallas_call(
        carried, name=name, grid=grid, in_specs=list(in_specs) + [_ANY] * nc_in,
        out_specs=list(out_specs) + [_ANY] * nc_out, out_shape=list(out_shape) + list(carry.out_shapes),
        scratch_shapes=list(carry.sems), compiler_params=_cparams(sem), **kw)(*args, *carry.inputs)


def _rs_stage1_sum(pos, g, gb, name):
    _, rows, d = g.shape

    def body(pos_ref, g_ref, gb_ref, p_ref, pb_ref, land, send_sems, recv_sems):
        q = pl.program_id(0)
        x, y, c = lax.axis_index("x"), lax.axis_index("y"), lax.axis_index("c")

        def copy(k):
            return pltpu.make_async_remote_copy(
                src_ref=gb_ref.at[4 * (k // 2) + 2 * (k % 2) + (1 - c)], dst_ref=land.at[k],
                send_sem=send_sems.at[k], recv_sem=recv_sems.at[k], device_id=(x, y, 1 - c), device_id_type=MESH)

        @pl.when(q == 0)
        def _():
            for k in range(4):
                copy(k).start()

        copy(q).wait_recv()
        p = g_ref[...] + land[q].astype(F32)
        p_ref[...] = p
        pb_ref[...] = p.astype(BF16)

        @pl.when(q == 3)
        def _():
            for k in range(4):
                copy(k).wait_send()

    blk = lambda q, pos_ref: (q, 0, 0)
    return pl.pallas_call(
        body, name=name,
        grid_spec=pltpu.PrefetchScalarGridSpec(
            num_scalar_prefetch=1, grid=(4,),
            in_specs=[pl.BlockSpec((None, rows, d), lambda q, pos_ref: (4 * (q // 2) + 2 * (q % 2) + pos_ref[2], 0, 0)),
                      _ANY],
            out_specs=[pl.BlockSpec((None, rows, d), blk), pl.BlockSpec((None, rows, d), blk)],
            scratch_shapes=[pltpu.VMEM((4, rows, d), BF16), pltpu.SemaphoreType.DMA((4,)),
                            pltpu.SemaphoreType.DMA((4,))]),
        out_shape=[jax.ShapeDtypeStruct((4, rows, d), F32), jax.ShapeDtypeStruct((4, rows, d), BF16)],
        compiler_params=_cparams(("arbitrary",)),
    )(pos, g, gb)


def _rs_sum2_adam(pos, p, r2, w, m, v, name):
    _, rows, d = p.shape
    tr = rows // 2 if rows % 16 == 0 else rows

    def body(pos_ref, p_ref, r_ref, w_ref, m_ref, v_ref, g_ref, d_ref, nm_ref, nv_ref):
        r = r_ref[...].astype(F32)
        g = ((p_ref[...] + r[0]) + r[1]) + r[2]
        g_ref[...] = g
        d_ref[...], nm_ref[...], nv_ref[...] = _adam_math(w_ref[...], g, m_ref[...], v_ref[...])

    blk = pl.BlockSpec((tr, d), lambda i, pos_ref: (i, 0))
    return pl.pallas_call(
        body, name=name,
        grid_spec=pltpu.PrefetchScalarGridSpec(
            num_scalar_prefetch=1, grid=(rows // tr,),
            in_specs=[pl.BlockSpec((None, tr, d), lambda i, pos_ref: (2 * pos_ref[0] + pos_ref[1], i, 0)),
                      pl.BlockSpec((3, tr, d), lambda i, pos_ref: (0, i, 0)), blk, blk, blk],
            out_specs=[blk] * 4),
        out_shape=[jax.ShapeDtypeStruct((rows, d), F32)] * 4,
        compiler_params=_cparams(("arbitrary",)),
    )(pos, p, r2, w, m, v)


def _ffn_chunk(f):
    for cand in (256, 128):
        if f % cand == 0:
            return cand
    return f


def _ffn_fwd(h, g, wg_t, wu_t, wd, name, tm, carry=None):
    s, d = h.shape
    f = wd.shape[0]
    tf = _ffn_chunk(f)

    def body(h_ref, g_ref, wg_ref, wu_ref, wd_ref, o_ref, n_ref, gate_ref, up_ref, act_ref):
        x = h_ref[...]
        r = lax.rsqrt(jnp.mean(x * x, axis=-1, keepdims=True) + EPS)
        nb = (x * r * g_ref[...]).astype(BF16)
        n_ref[...] = nb
        for j in range(f // tf):
            sl = slice(j * tf, (j + 1) * tf)
            gate = _dot_nt(nb, wg_ref[sl, :])
            up = _dot_nt(nb, wu_ref[sl, :])
            gate_ref[:, sl] = gate.astype(BF16)
            up_ref[:, sl] = up.astype(BF16)
            act_ref[:, sl] = (gate * _sigmoid(gate) * up).astype(BF16)
        o_ref[...] = x + 0.5 * _dot_nn(act_ref[...], wd_ref[...])

    row = lambda w: pl.BlockSpec((tm, w), lambda i: (i, 0))
    return _call(
        body, name, (s // tm,),
        [row(d), _whole((1, d)), _whole((f, d)), _whole((f, d)), _whole((f, d))],
        [row(d), row(d), row(f), row(f), row(f)],
        [jax.ShapeDtypeStruct((s, d), F32), jax.ShapeDtypeStruct((s, d), BF16),
         jax.ShapeDtypeStruct((s, f), BF16), jax.ShapeDtypeStruct((s, f), BF16), jax.ShapeDtypeStruct((s, f), BF16)],
        (h, g, wg_t, wu_t, wd), ("arbitrary",), carry)


def _gate_grads(dh_ref, gate_ref, up_ref, wd_ref, dgate_ref, dup_ref, dhh_ref, tf):
    dhh = (0.5 * dh_ref[...]).astype(BF16)
    dhh_ref[...] = dhh
    for j in range(gate_ref.shape[1] // tf):
        sl = slice(j * tf, (j + 1) * tf)
        gt = gate_ref[:, sl].astype(F32)
        u = up_ref[:, sl].astype(F32)
        dact = _dot_nt(dhh, wd_ref[sl, :])
        sg = _sigmoid(gt)
        dup_ref[:, sl] = (dact * (gt * sg)).astype(BF16)
        dgate_ref[:, sl] = (dact * u * (sg * (1.0 + gt * (1.0 - sg)))).astype(BF16)


def _input_grad(h_ref, dh_ref, dgate_ref, dup_ref, g_ref, wg_ref, wu_ref, o_ref, dg_ref):
    x = h_ref[...]
    r = lax.rsqrt(jnp.mean(x * x, axis=-1, keepdims=True) + EPS)
    xhat = x * r
    dn = _dot_nn(dgate_ref[...], wg_ref[...]) + _dot_nn(dup_ref[...], wu_ref[...])
    dxhat = dn * g_ref[...]
    o_ref[...] = dh_ref[...] + r * (dxhat - xhat * jnp.mean(dxhat * xhat, axis=-1, keepdims=True))

    @pl.when(pl.program_id(0) == 0)
    def _():
        dg_ref[...] = jnp.zeros_like(dg_ref)

    dg_ref[...] += jnp.sum(dn * xhat, axis=0, keepdims=True)


def _ffn_bwd(h_in, dh_out, gate, up, g, wg_t, wu_t, wd, name, tm):
    s, d = h_in.shape
    f = gate.shape[1]
    tf = _ffn_chunk(f)

    def body(h_ref, dh_ref, gate_ref, up_ref, g_ref, wg_ref, wu_ref, wd_ref,
             o_ref, dg_ref, dgate_ref, dup_ref, dhh_ref):
        _gate_grads(dh_ref, gate_ref, up_ref, wd_ref, dgate_ref, dup_ref, dhh_ref, tf)
        _input_grad(h_ref, dh_ref, dgate_ref, dup_ref, g_ref, wg_ref, wu_ref, o_ref, dg_ref)

    row = lambda w: pl.BlockSpec((tm, w), lambda i: (i, 0))
    return pl.pallas_call(
        body, name=name, grid=(s // tm,),
        in_specs=[row(d), row(d), row(f), row(f), _whole((1, d)), _whole((f, d)), _whole((f, d)), _whole((f, d))],
        out_specs=[row(d), pl.BlockSpec((1, d), lambda i: (0, 0)), row(f), row(f), row(d)],
        out_shape=[jax.ShapeDtypeStruct((s, d), F32), jax.ShapeDtypeStruct((1, d), F32),
                   jax.ShapeDtypeStruct((s, f), BF16), jax.ShapeDtypeStruct((s, f), BF16),
                   jax.ShapeDtypeStruct((s, d), BF16)],
        compiler_params=_cparams(("arbitrary",)),
    )(h_in, dh_out, gate, up, g, wg_t, wu_t, wd)


def _ffn_bwd_gates(dh_out, gate, up, wd, name, tm, carry=None):
    s, d = dh_out.shape
    f = gate.shape[1]
    tf = _ffn_chunk(f)

    def body(dh_ref, gate_ref, up_ref, wd_ref, dgate_ref, dup_ref, dhh_ref):
        _gate_grads(dh_ref, gate_ref, up_ref, wd_ref, dgate_ref, dup_ref, dhh_ref, tf)

    row = lambda w: pl.BlockSpec((tm, w), lambda i: (i, 0))
    return _call(
        body, name, (s // tm,), [row(d), row(f), row(f), _whole((f, d))], [row(f), row(f), row(d)],
        [jax.ShapeDtypeStruct((s, f), BF16), jax.ShapeDtypeStruct((s, f), BF16), jax.ShapeDtypeStruct((s, d), BF16)],
        (dh_out, gate, up, wd), ("arbitrary",), carry)


def _ffn_bwd_input(h_in, dh_out, dgate, dup, g, wg_t, wu_t, name, tm, carry=None):
    s, d = h_in.shape
    f = dgate.shape[1]

    row = lambda w: pl.BlockSpec((tm, w), lambda i: (i, 0))
    return _call(
        _input_grad, name, (s // tm,),
        [row(d), row(d), row(f), row(f), _whole((1, d)), _whole((f, d)), _whole((f, d))],
        [row(d), pl.BlockSpec((1, d), lambda i: (0, 0))],
        [jax.ShapeDtypeStruct((s, d), F32), jax.ShapeDtypeStruct((1, d), F32)],
        (h_in, dh_out, dgate, dup, g, wg_t, wu_t), ("arbitrary",), carry)


def _wgrad(a, b, name, tf, carry=None):
    s, f = a.shape
    d = b.shape[1]

    def body(a_ref, b_ref, o_ref, ob_ref):
        acc = _dot_tn(a_ref[...], b_ref[...])
        o_ref[...] = acc
        ob_ref[...] = acc.astype(BF16)

    return _call(
        body, name, (f // tf,),
        [pl.BlockSpec((s, tf), lambda i: (0, i)), _whole((s, d))],
        [pl.BlockSpec((tf, d), lambda i: (i, 0)), pl.BlockSpec((tf, d), lambda i: (i, 0))],
        [jax.ShapeDtypeStruct((f, d), F32), jax.ShapeDtypeStruct((f, d), BF16)],
        (a, b), ("arbitrary",), carry)


def _rope(t, c, sa, sb, reps):
    c, sa, sb = (jnp.tile(v, (1, reps)) if reps > 1 else v for v in (c, sa, sb))
    w = t.shape[1]
    return t * c + pltpu.roll(t, w - 8, 1) * sa + pltpu.roll(t, 8, 1) * sb


def _rope_bwd(dt, c, sa, sb, reps):
    c, sa, sb = (jnp.tile(v, (1, reps)) if reps > 1 else v for v in (c, sa, sb))
    w = dt.shape[1]
    return dt * c + pltpu.roll(dt * sa, 8, 1) + pltpu.roll(dt * sb, w - 8, 1)


def _mix_in(h, g, win_t, tabs, name, tm):
    s, d = h.shape
    n_in = win_t.shape[0]

    def body(h_ref, g_ref, w_ref, c_ref, sa_ref, sb_ref, q_ref, k_ref, v_ref, pc_ref, n_ref):
        x = h_ref[...]
        r = lax.rsqrt(jnp.mean(x * x, axis=-1, keepdims=True) + EPS)
        nb = (x * r * g_ref[...]).astype(BF16)
        n_ref[...] = nb
        u = _dot_nt(nb, w_ref[...])
        c, sa, sb = c_ref[...], sa_ref[...], sb_ref[...]
        q_ref[...] = _rope(u[:, :ATTN_W], c, sa, sb, ATTN_W // 128).astype(BF16)
        k_ref[...] = _rope(u[:, ATTN_W:ATTN_W + KV_W], c, sa, sb, 1).astype(BF16)
        v_ref[...] = u[:, ATTN_W + KV_W:ATTN_W + 2 * KV_W].astype(BF16)
        pc_ref[...] = u[:, ATTN_W + 2 * KV_W:]

    row = lambda w: pl.BlockSpec((tm, w), lambda i: (i, 0))
    return pl.pallas_call(
        body, name=name, grid=(s // tm,),
        in_specs=[row(d), _whole((1, d)), _whole((n_in, d)), row(128), row(128), row(128)],
        out_specs=[row(ATTN_W), row(KV_W), row(KV_W), row(POOL_W), row(d)],
        out_shape=[jax.ShapeDtypeStruct((s, ATTN_W), BF16), jax.ShapeDtypeStruct((s, KV_W), BF16),
                   jax.ShapeDtypeStruct((s, KV_W), BF16), jax.ShapeDtypeStruct((s, POOL_W), F32),
                   jax.ShapeDtypeStruct((s, d), BF16)],
        compiler_params=_cparams(("arbitrary",)),
    )(h, g, win_t, *tabs)


def _band_mask(n, nb, transposed):
    shape = (3 * BLK, 2 * BLK) if transposed else (2 * BLK, 3 * BLK)
    i = lax.broadcasted_iota(jnp.int32, shape, 1 if transposed else 0) % BLK
    j = lax.broadcasted_iota(jnp.int32, shape, 0 if transposed else 1)
    kpos = (n - 1) * BLK + j
    return (j >= i) & (j <= i + 2 * BLK) & (kpos >= 0) & (kpos < nb * BLK)


def _block_diag(t, kh):
    tf = t.astype(F32)
    tr = pltpu.roll(tf, HEAD_DIM, 1)
    lo = lax.broadcasted_iota(jnp.int32, tf.shape, 1) < HEAD_DIM
    top, bot = (tf, tr) if kh == 0 else (tr, tf)
    return jnp.concatenate([jnp.where(lo, top, 0.0), jnp.where(lo, 0.0, bot)], axis=0).astype(BF16)


def _fold_diag(tbd):
    lo = lax.broadcasted_iota(jnp.int32, (3 * BLK, 2 * HEAD_DIM), 1) < HEAD_DIM
    t = jnp.where(lo, tbd[:3 * BLK], tbd[3 * BLK:])
    return t + pltpu.roll(t, HEAD_DIM, 1)


def _stack_pairs(x, kh):
    return jnp.concatenate([x[:, (2 * kh) * 128:(2 * kh + 1) * 128], x[:, (2 * kh + 1) * 128:(2 * kh + 2) * 128]], axis=0)


def _sink_of(sink_ref, kh, half, axis):
    shape = (2 * BLK, 1) if axis == 0 else (1, 2 * BLK)
    first = lax.broadcasted_iota(jnp.int32, shape, axis) < BLK
    return jnp.where(first, sink_ref[0, GROUP * kh + half], sink_ref[0, GROUP * kh + 2 + half])


def _softmax_sink(sc, valid, sink, axis):
    sc = jnp.where(valid, sc, -1e30)
    m = jnp.maximum(jnp.max(sc, axis=axis, keepdims=True), sink)
    e = jnp.exp(sc - m)
    es = jnp.exp(sink - m)
    inv = 1.0 / (jnp.sum(e, axis=axis, keepdims=True) + es)
    return e * inv, es * inv


def _attn_pool_fwd(q, k, v, pc, sink, pool_w, pool_scale, pband, name, carry=None):
    s = q.shape[0]
    nb = s // BLK

    def body(sink_ref, q_ref, k0, k1, k2, v0, v1, v2, p0, p1, p2, pw_ref, ps_ref, pb_ref, o_ref):
        n = pl.program_id(0)
        valid = _band_mask(n, nb, False)
        kb = jnp.concatenate([k0[...], k1[...], k2[...]], axis=0)
        vb = jnp.concatenate([v0[...], v1[...], v2[...]], axis=0)
        qs = q_ref[...] * SCORE_SCALE
        for kh in range(N_KV):
            sc = _dot_nt(_stack_pairs(qs, kh), _block_diag(kb, kh))
            p = [_softmax_sink(sc[:, half * 3 * BLK:(half + 1) * 3 * BLK], valid, _sink_of(sink_ref, kh, half, 0), 1)[0]
                 for half in range(2)]
            o2 = _dot_nn(jnp.concatenate(p, axis=1).astype(BF16), _block_diag(vb, kh)).astype(BF16)
            o_ref[:, (2 * kh) * 128:(2 * kh + 1) * 128] = o2[:BLK]
            o_ref[:, (2 * kh + 1) * 128:(2 * kh + 2) * 128] = o2[BLK:]
        ext = jnp.concatenate([p0[...], p1[...], p2[...]], axis=0).astype(BF16)
        for gi in range(POOL_G):
            gsl = slice(gi * POOL_GW, (gi + 1) * POOL_GW)
            dg = _dot_nn(pb_ref[gi], ext[:, gsl])
            yg = _dot_nn(dg.astype(BF16), pw_ref[gi].astype(BF16))
            o_ref[:, ATTN_W + gi * POOL_GW:ATTN_W + (gi + 1) * POOL_GW] = (yg * ps_ref[:, gsl]).astype(BF16)

    def band(w):
        return [pl.BlockSpec((BLK, w), lambda n: (jnp.maximum(n - 1, 0), 0)),
                pl.BlockSpec((BLK, w), lambda n: (n, 0)),
                pl.BlockSpec((BLK, w), lambda n: (jnp.minimum(n + 1, nb - 1), 0))]

    return _call(
        body, name, (nb,),
        [pl.BlockSpec(memory_space=pltpu.SMEM), pl.BlockSpec((BLK, ATTN_W), lambda n: (n, 0)),
         *band(KV_W), *band(KV_W), *band(POOL_W),
         _whole((POOL_G, POOL_GW, POOL_GW)), _whole((1, POOL_W)),
         pl.BlockSpec((None, POOL_G, BLK, 3 * BLK), lambda n: (_variant_index(n, nb), 0, 0, 0))],
        [pl.BlockSpec((BLK, ATTN_W + POOL_W), lambda n: (n, 0))],
        [jax.ShapeDtypeStruct((s, ATTN_W + POOL_W), BF16)],
        (sink, q, k, k, k, v, v, v, pc, pc, pc, pool_w, pool_scale, pband), ("arbitrary",), carry)


def _attn_pool_bwd(q, k, v, pc, dmix, sink, pool_w, pool_scale, pband, ptband, name, carry=None):
    s = q.shape[0]
    nb = s // BLK

    def body(sink_ref, q_ref, k0, k1, k2, v0, v1, v2, p0, p1, p2, da_ref, d0, d1, d2, pw_ref, ps_ref, pb_ref, ptb_ref,
             dq_ref, dk_ref, dv_ref, dpc_ref, dsink_ref, dpw_ref, dps_ref):
        n = pl.program_id(0)

        @pl.when(n == 0)
        def _():
            dsink_ref[...] = jnp.zeros_like(dsink_ref)
            dpw_ref[...] = jnp.zeros_like(dpw_ref)
            dps_ref[...] = jnp.zeros_like(dps_ref)

        valid = _band_mask(n, nb, True)
        kb = jnp.concatenate([k0[...], k1[...], k2[...]], axis=0)
        vb = jnp.concatenate([v0[...], v1[...], v2[...]], axis=0)
        qb = q_ref[...]
        qs = qb * SCORE_SCALE
        da = da_ref[...].astype(BF16)
        dk_fold, dv_fold = [], []
        for kh in range(N_KV):
            kbd, vbd = _block_diag(kb, kh), _block_diag(vb, kh)
            q2, do2 = _stack_pairs(qb, kh), _stack_pairs(da, kh)
            sc_t = _dot_nt(kbd, _stack_pairs(qs, kh))
            dp_t = _dot_nt(vbd, do2)
            p_t, ds_t = [], []
            for half in range(2):
                rows = slice(half * 3 * BLK, (half + 1) * 3 * BLK)
                p, ps = _softmax_sink(sc_t[rows], valid, _sink_of(sink_ref, kh, half, 1), 0)
                delta = jnp.sum(p * dp_t[rows], axis=0, keepdims=True)
                p_t.append(p.astype(BF16))
                ds_t.append((p * (dp_t[rows] - delta)).astype(BF16))
                dsk = -ps * delta
                for pair in range(2):
                    h = GROUP * kh + 2 * pair + half
                    part = jnp.sum(dsk[:, pair * BLK:(pair + 1) * BLK], axis=1, keepdims=True)
                    dsink_ref[h:h + 1, :] += jnp.broadcast_to(part, (1, 128))
            p_t = jnp.concatenate(p_t, axis=0)
            ds_t = jnp.concatenate(ds_t, axis=0)
            dq2 = _dot_tn(ds_t, kbd) * SCORE_SCALE
            dq_ref[:, (2 * kh) * 128:(2 * kh + 1) * 128] = dq2[:BLK]
            dq_ref[:, (2 * kh + 1) * 128:(2 * kh + 2) * 128] = dq2[BLK:]
            dk_fold.append(_fold_diag(_dot_nn(ds_t, q2)) * SCORE_SCALE)
            dv_fold.append(_fold_diag(_dot_nn(p_t, do2)))
        lo = lax.broadcasted_iota(jnp.int32, (3 * BLK, KV_W), 1) < HEAD_DIM
        dk_all = jnp.where(lo, dk_fold[0], dk_fold[1])
        dv_all = jnp.where(lo, dv_fold[0], dv_fold[1])
        for t in range(3):
            dk_ref[t] = dk_all[t * BLK:(t + 1) * BLK]
            dv_ref[t] = dv_all[t * BLK:(t + 1) * BLK]
        ext = jnp.concatenate([p0[...], p1[...], p2[...]], axis=0).astype(BF16)
        dpe = jnp.concatenate([d0[...], d1[...], d2[...]], axis=0)
        dpc_cur = d1[...]
        for gi in range(POOL_G):
            gsl = slice(gi * POOL_GW, (gi + 1) * POOL_GW)
            wg = pw_ref[gi].astype(BF16)
            sc = ps_ref[:, gsl]
            dgr = _dot_nn(pb_ref[gi], ext[:, gsl])
            dgb = dgr.astype(BF16)
            yg = _dot_nn(dgb, wg)
            dps_ref[:, gsl] += jnp.sum(dpc_cur[:, gsl] * yg, axis=0, keepdims=True)
            dpw_ref[gi] += _dot_tn(dgb, (dpc_cur[:, gsl] * sc).astype(BF16))
            dd = _dot_nt((dpe[:, gsl] * sc).astype(BF16), wg)
            dpc_ref[:, gsl] = _dot_nn(ptb_ref[gi], dd.astype(BF16))

    def band(w, col=0):
        return [pl.BlockSpec((BLK, w), lambda n: (jnp.maximum(n - 1, 0), col)),
                pl.BlockSpec((BLK, w), lambda n: (n, col)),
                pl.BlockSpec((BLK, w), lambda n: (jnp.minimum(n + 1, nb - 1), col))]

    tab = lambda: pl.BlockSpec((None, POOL_G, BLK, 3 * BLK), lambda n: (_variant_index(n, nb), 0, 0, 0))
    fixed = lambda shape: pl.BlockSpec(shape, lambda n: (0,) * len(shape))
    return _call(
        body, name, (nb,),
        [pl.BlockSpec(memory_space=pltpu.SMEM), pl.BlockSpec((BLK, ATTN_W), lambda n: (n, 0)),
         *band(KV_W), *band(KV_W), *band(POOL_W),
         pl.BlockSpec((BLK, ATTN_W), lambda n: (n, 0)), *band(POOL_W, 1),
         _whole((POOL_G, POOL_GW, POOL_GW)), _whole((1, POOL_W)), tab(), tab()],
        [pl.BlockSpec((BLK, ATTN_W), lambda n: (n, 0)),
         pl.BlockSpec((None, 3, BLK, KV_W), lambda n: (n, 0, 0, 0)),
         pl.BlockSpec((None, 3, BLK, KV_W), lambda n: (n, 0, 0, 0)),
         pl.BlockSpec((BLK, POOL_W), lambda n: (n, 0)),
         fixed((N_HEADS, 128)), fixed((POOL_G, POOL_GW, POOL_GW)), fixed((1, POOL_W))],
        [jax.ShapeDtypeStruct((s, ATTN_W), F32), jax.ShapeDtypeStruct((nb, 3, BLK, KV_W), F32),
         jax.ShapeDtypeStruct((nb, 3, BLK, KV_W), F32), jax.ShapeDtypeStruct((s, POOL_W), F32),
         jax.ShapeDtypeStruct((N_HEADS, 128), F32),
         jax.ShapeDtypeStruct((POOL_G, POOL_GW, POOL_GW), F32), jax.ShapeDtypeStruct((1, POOL_W), F32)],
        (sink, q, k, k, k, v, v, v, pc, pc, pc, dmix, dmix, dmix, dmix, pool_w, pool_scale, pband, ptband),
        ("arbitrary",), carry)


def _mix_out(h, mix, w_out, name, tm):
    s, d = h.shape
    w = mix.shape[1]

    def body(h_ref, m_ref, w_ref, o_ref):
        o_ref[...] = h_ref[...] + _dot_nn(m_ref[...], w_ref[...])

    row = lambda c: pl.BlockSpec((tm, c), lambda i: (i, 0))
    return pl.pallas_call(
        body, name=name, grid=(s // tm,), in_specs=[row(d), row(w), _whole((w, d))], out_specs=row(d),
        out_shape=jax.ShapeDtypeStruct((s, d), F32), compiler_params=_cparams(("arbitrary",)),
    )(h, mix, w_out)


def _mix_out_bwd(dh, w_out, name, tm):
    s, d = dh.shape
    w = w_out.shape[0]

    def body(dh_ref, w_ref, o_ref, dhb_ref):
        dhb = dh_ref[...].astype(BF16)
        dhb_ref[...] = dhb
        o_ref[...] = _dot_nt(dhb, w_ref[...])

    row = lambda c: pl.BlockSpec((tm, c), lambda i: (i, 0))
    return pl.pallas_call(
        body, name=name, grid=(s // tm,), in_specs=[row(d), _whole((w, d))], out_specs=[row(w), row(d)],
        out_shape=[jax.ShapeDtypeStruct((s, w), F32), jax.ShapeDtypeStruct((s, d), BF16)],
        compiler_params=_cparams(("arbitrary",)),
    )(dh, w_out)


def _mix_in_bwd(h, dh, g, win_t, dq, dkp, dvp, dpc, tabs, name, tm):
    s, d = h.shape
    nb = s // BLK
    nt = tm // BLK
    n_in = win_t.shape[0]

    def band_sum(n, before, own, after, prev_last, next_first):
        lo = (n > 0).astype(F32)
        hi = (n < s // tm - 1).astype(F32)
        blocks = []
        for b in range(nt):
            from_prev = prev_last[...] * lo if b == 0 else before[b - 1]
            from_next = next_first[...] * hi if b == nt - 1 else after[b + 1]
            blocks.append(from_prev + own[b] + from_next)
        return jnp.concatenate(blocks, axis=0)

    def body(h_ref, dh_ref, g_ref, w_ref, dq_ref, k2, k1, k0, kp, kn, v2, v1, v0, vp, vn, dpc_ref, c_ref, sa_ref,
             sb_ref, o_ref, du_ref, dg_ref):
        n = pl.program_id(0)
        dk = band_sum(n, k2, k1, k0, kp, kn)
        dv = band_sum(n, v2, v1, v0, vp, vn)
        c, sa, sb = c_ref[...], sa_ref[...], sb_ref[...]
        du = jnp.concatenate([_rope_bwd(dq_ref[...], c, sa, sb, ATTN_W // 128), _rope_bwd(dk, c, sa, sb, 1), dv,
                              dpc_ref[...]], axis=1).astype(BF16)
        du_ref[...] = du
        dn = _dot_nn(du, w_ref[...])
        x = h_ref[...]
        r = lax.rsqrt(jnp.mean(x * x, axis=-1, keepdims=True) + EPS)
        xhat = x * r
        dxhat = dn * g_ref[...]
        o_ref[...] = dh_ref[...] + r * (dxhat - xhat * jnp.mean(dxhat * xhat, axis=-1, keepdims=True))

        @pl.when(n == 0)
        def _():
            dg_ref[...] = jnp.zeros_like(dg_ref)

        dg_ref[...] += jnp.sum(dn * xhat, axis=0, keepdims=True)

    row = lambda w: pl.BlockSpec((tm, w), lambda n: (n, 0))
    slot = lambda t: pl.BlockSpec((nt, None, BLK, KV_W), lambda n, t=t: (n, t, 0, 0))
    parts = [slot(2), slot(1), slot(0),
             pl.BlockSpec((None, None, BLK, KV_W), lambda n: (jnp.maximum(nt * n - 1, 0), 2, 0, 0)),
             pl.BlockSpec((None, None, BLK, KV_W), lambda n: (jnp.minimum(nt * n + nt, nb - 1), 0, 0, 0))]
    return pl.pallas_call(
        body, name=name, grid=(s // tm,),
        in_specs=[row(d), row(d), _whole((1, d)), _whole((n_in, d)), row(ATTN_W), *parts, *parts, row(POOL_W),
                  row(128), row(128), row(128)],
        out_specs=[row(d), row(n_in), pl.BlockSpec((1, d), lambda n: (0, 0))],
        out_shape=[jax.ShapeDtypeStruct((s, d), F32), jax.ShapeDtypeStruct((s, n_in), BF16),
                   jax.ShapeDtypeStruct((1, d), F32)],
        compiler_params=_cparams(("arbitrary",)),
    )(h, dh, g, win_t, dq, *[dkp] * 5, *[dvp] * 5, dpc, *tabs)


def _final(h, g, target, name, tm):
    s, d = h.shape

    def body(h_ref, g_ref, t_ref, loss_ref, dh_ref, dg_ref):
        @pl.when(pl.program_id(0) == 0)
        def _():
            loss_ref[...] = jnp.zeros_like(loss_ref)
            dg_ref[...] = jnp.zeros_like(dg_ref)

        x = h_ref[...]
        gg = g_ref[...]
        r = lax.rsqrt(jnp.mean(x * x, axis=-1, keepdims=True) + EPS)
        xhat = x * r
        e = xhat * gg - t_ref[...]
        per_tok = jnp.mean(e * e, axis=-1, keepdims=True)
        loss_ref[...] += 0.5 * jnp.sum(per_tok, axis=0, keepdims=True)
        dy = e * (1.0 / d)
        dg_ref[...] += jnp.sum(dy * xhat, axis=0, keepdims=True)
        dxhat = dy * gg
        dh_ref[...] = r * (dxhat - xhat * jnp.mean(dxhat * xhat, axis=-1, keepdims=True))

    row = pl.BlockSpec((tm, d), lambda i: (i, 0))
    return pl.pallas_call(
        body, name=name, grid=(s // tm,), in_specs=[row, _whole((1, d)), row],
        out_specs=[pl.BlockSpec((1, 1), lambda i: (0, 0)), row, pl.BlockSpec((1, d), lambda i: (0, 0))],
        out_shape=[jax.ShapeDtypeStruct((1, 1), F32), jax.ShapeDtypeStruct((s, d), F32),
                   jax.ShapeDtypeStruct((1, d), F32)],
        compiler_params=_cparams(("arbitrary",)),
    )(h, g, target)


def _adam_math(w, g, m, v):
    m = ADAM_B1 * m + (1.0 - ADAM_B1) * g
    v = ADAM_B2 * v + (1.0 - ADAM_B2) * (g * g)
    m_hat = m / (1.0 - ADAM_B1 ** ADAM_STEP)
    v_hat = v / (1.0 - ADAM_B2 ** ADAM_STEP)
    delta = -ADAM_LR * (m_hat / (jnp.sqrt(v_hat) + ADAM_EPS) + ADAM_WD * w)
    return delta, m, v


def _adam_small(w, parts, late, m, v, name):
    rows, cols = w.shape

    def body(w_ref, p_ref, l_ref, m_ref, v_ref, g_ref, d_ref, nm_ref, nv_ref):
        g, gl = p_ref[0], l_ref[0]
        for k in range(1, N_DEV):
            g = g + p_ref[k]
            gl = gl + l_ref[k]
        g_ref[...] = g
        g_ref[SMALL_NORM1:SMALL_NORM1 + 8, :] = g[SMALL_NORM1:SMALL_NORM1 + 8] + gl
        d_ref[...], nm_ref[...], nv_ref[...] = _adam_math(w_ref[...], g_ref[...], m_ref[...], v_ref[...])

    return pl.pallas_call(
        body, name=name, out_shape=[jax.ShapeDtypeStruct((rows, cols), F32)] * 4,
    )(w, parts, late, m, v)


SMALL_NORM1 = 512


def _pack_small(norm1, normm, norm2, normf, sink, pool_w, pool_scale, loss=None):
    scale_rows = jnp.pad(pool_scale.reshape(4, 128), ((0, 4), (0, 0)))
    last_rows = jnp.pad(sink.reshape(1, N_HEADS), ((0, 7), (0, 128 - N_HEADS)))
    if loss is not None:
        last_rows = last_rows + jnp.pad(loss.reshape(1, 1), ((1, 6), (0, 127)))
    return jnp.concatenate([pool_w.reshape(512, 128), norm1.reshape(8, 128), normm.reshape(8, 128),
                            norm2.reshape(8, 128), normf.reshape(8, 128), scale_rows, last_rows], axis=0)


def _unpack_small(p):
    return dict(pool_w=p[:512].reshape(1, POOL_G, POOL_GW, POOL_GW), ffn1_norm=p[512:520].reshape(1, 1024),
                mix_norm=p[520:528].reshape(1, 1024), ffn2_norm=p[528:536].reshape(1, 1024),
                final_norm=p[536:544].reshape(1024), pool_scale=p[544:548].reshape(1, POOL_W),
                sink_logits=p[552, :N_HEADS].reshape(1, N_HEADS), loss=p[553, 0])


def kernel(x, ffn1_norm, ffn1_w_gate, ffn1_w_up, ffn1_w_down, mix_norm, w_in, sink_logits, pool_w, pool_scale, w_out, ffn2_norm, ffn2_w_gate, ffn2_w_up, ffn2_w_down, final_norm, loss_target, m_ffn1_norm, m_ffn1_w_gate, m_ffn1_w_up, m_ffn1_w_down, m_mix_norm, m_w_in, m_sink_logits, m_pool_w, m_pool_scale, m_w_out, m_ffn2_norm, m_ffn2_w_gate, m_ffn2_w_up, m_ffn2_w_down, m_final_norm, v_ffn1_norm, v_ffn1_w_gate, v_ffn1_w_up, v_ffn1_w_down, v_mix_norm, v_w_in, v_sink_logits, v_pool_w, v_pool_scale, v_w_out, v_ffn2_norm, v_ffn2_w_gate, v_ffn2_w_up, v_ffn2_w_down, v_final_norm):
    s, d = x.shape[1], x.shape[2]
    fk = ffn1_w_gate.shape[2]
    f = N_DEV * fk
    ink = w_in.shape[2]
    n_in = N_DEV * ink
    mixk = w_out.shape[1]
    tm = min(512, s)
    tm_bwd = min(256, s)
    pos = jnp.stack([lax.axis_index("x"), lax.axis_index("y"), lax.axis_index("c")]).astype(jnp.int32)

    t_bf = lambda w: w[0].T.astype(BF16)
    full = lambda a: a.reshape(N_DEV * a.shape[1], d)
    first = [t_bf(ffn1_w_gate), t_bf(ffn1_w_up), ffn1_w_down[0].astype(BF16)]
    wg1, wu1, wd1 = map(full, _run_exchange(_AllGather(first), "gather_ffn1"))
    second = _AllGather([t_bf(w_in), w_out[0].astype(BF16), t_bf(ffn2_w_gate)])
    third = _AllGather([t_bf(ffn2_w_up), ffn2_w_down[0].astype(BF16)])

    tabs = _rope_tables(s)
    pband, ptband = _pool_tables(s)
    g1, gm, g2, gf = ffn1_norm, mix_norm, ffn2_norm, final_norm.reshape(1, d)

    x0 = x[0]
    h1, n1, gate1, up1, act1, *gathered = _ffn_fwd(x0, g1, wg1, wu1, wd1, "ffn1_fwd", tm, carry=second)
    win_t, wout, wg2 = map(full, gathered)
    q, k, v, pc, n2 = _mix_in(h1, gm, win_t, tabs, "mix_in", tm)
    mix, *gathered = _attn_pool_fwd(q, k, v, pc, sink_logits, pool_w[0], pool_scale, pband, "attn_pool_fwd",
                                    carry=third)
    wu2, wd2 = map(full, gathered)
    h2 = _mix_out(h1, mix, wout, "mix_out", tm)
    h3, n3, gate2, up2, act2 = _ffn_fwd(h2, g2, wg2, wu2, wd2, "ffn2_fwd", tm)
    loss_part, dh3, dgf = _final(h3, gf, loss_target[0], "final", tm)

    tw = 256 if f % 256 == 0 else 128
    gw, sum1, recv2 = {}, {}, {}

    def stage1(keys, rows):
        for key in keys:
            sum1[key] = _rs_stage1_sum(pos, gw[key][0].reshape(N_DEV, rows, d), gw[key][1].reshape(N_DEV, rows, d),
                                       "rs1_" + key)

    def stage2(keys):
        return _RsStage2([sum1[key][1] for key in keys])

    dh2, dg2, dgate2, dup2, dhh3 = _ffn_bwd(h2, dh3, gate2, up2, g2, wg2, wu2, wd2, "ffn2_bwd", tm_bwd)
    gw["g2"] = _wgrad(dgate2, n3, "wgrad_gate2", tw)
    gw["u2"] = _wgrad(dup2, n3, "wgrad_up2", tw)
    gw["d2"] = _wgrad(act2, dhh3, "wgrad_down2", tw)
    stage1(["g2", "u2", "d2"], fk)
    dmix, dh2b = _mix_out_bwd(dh2, wout, "mix_out_bwd", tm)
    gw["out"] = _wgrad(mix, dh2b, "wgrad_out", tw)
    stage1(["out"], mixk)
    dq, dkp, dvp, dpc, dsink, dpw, dps, *r2 = _attn_pool_bwd(
        q, k, v, pc, dmix, sink_logits, pool_w[0], pool_scale, pband, ptband, "attn_pool_bwd",
        carry=stage2(["g2", "u2", "d2", "out"]))
    recv2.update(zip(["g2", "u2", "d2", "out"], r2))
    dh1, du, dgm = _mix_in_bwd(h1, dh2, gm, win_t, dq, dkp, dvp, dpc, tabs, "mix_in_bwd", tm)
    gw["in"] = _wgrad(du, n2, "wgrad_in", tw)
    stage1(["in"], ink)
    small_part = _pack_small(jnp.zeros_like(dgm), dgm, dg2, dgf, dsink[:, 0], dpw, dps, loss_part)
    dgate1, dup1, dhh1, small_all = _ffn_bwd_gates(dh1, gate1, up1, wd1, "ffn1_bwd_gates", tm,
                                                   carry=_AllGather([small_part]))
    *gw["g1"], recv2["in"] = _wgrad(dgate1, n1, "wgrad_gate1", tw, carry=stage2(["in"]))
    stage1(["g1"], fk)
    *gw["u1"], recv2["g1"] = _wgrad(dup1, n1, "wgrad_up1", tw, carry=stage2(["g1"]))
    stage1(["u1"], fk)
    *gw["d1"], recv2["u1"] = _wgrad(act1, dhh1, "wgrad_down1", tw, carry=stage2(["u1"]))
    stage1(["d1"], fk)
    dx, dg1, recv2["d1"] = _ffn_bwd_input(x0, dh1, dgate1, dup1, g1, wg1, wu1, "ffn1_bwd_input", tm,
                                          carry=stage2(["d1"]))

    (dg1_all,) = _run_exchange(_DirectGather([dg1.reshape(8, 128)]), "gather_norm1_grad")
    pk = lambda a, b, c_, e, s_, pw_, psc: _pack_small(a, b, c_, e, s_[0], pw_[0], psc)
    small_w = pk(ffn1_norm, mix_norm, ffn2_norm, final_norm, sink_logits, pool_w, pool_scale)
    small_m = pk(m_ffn1_norm, m_mix_norm, m_ffn2_norm, m_final_norm, m_sink_logits, m_pool_w, m_pool_scale)
    small_v = pk(v_ffn1_norm, v_mix_norm, v_ffn2_norm, v_final_norm, v_sink_logits, v_pool_w, v_pool_scale)
    sg, sd, sm, sv = [_unpack_small(a)
                      for a in _adam_small(small_w, small_all, dg1_all, small_m, small_v, "adam_small")]

    big = {}
    keys = ["g1", "u1", "d1", "g2", "u2", "d2", "in", "out"]
    names = ["ffn1_w_gate", "ffn1_w_up", "ffn1_w_down", "ffn2_w_gate", "ffn2_w_up", "ffn2_w_down", "w_in", "w_out"]
    transposed = [True, True, False, True, True, False, True, False]
    ws = [ffn1_w_gate, ffn1_w_up, ffn1_w_down, ffn2_w_gate, ffn2_w_up, ffn2_w_down, w_in, w_out]
    ms = [m_ffn1_w_gate, m_ffn1_w_up, m_ffn1_w_down, m_ffn2_w_gate, m_ffn2_w_up, m_ffn2_w_down, m_w_in, m_w_out]
    vs = [v_ffn1_w_gate, v_ffn1_w_up, v_ffn1_w_down, v_ffn2_w_gate, v_ffn2_w_up, v_ffn2_w_down, v_w_in, v_w_out]
    for key, nm, tr, w, m, vv in zip(keys, names, transposed, ws, ms, vs):
        view = (lambda a: jnp.swapaxes(a, 1, 2)[0]) if tr else (lambda a: a[0])
        back = (lambda a: jnp.swapaxes(a[None], 1, 2)) if tr else (lambda a: a[None])
        res = _rs_sum2_adam(pos, sum1[key][0], recv2[key], view(w), view(m), view(vv), "adam_" + nm)
        big[nm] = tuple(back(a) for a in res)

    loss = sg["loss"]
    all_names = ["ffn1_norm", "ffn1_w_gate", "ffn1_w_up", "ffn1_w_down", "mix_norm", "w_in", "sink_logits", "pool_w",
                 "pool_scale", "w_out", "ffn2_norm", "ffn2_w_gate", "ffn2_w_up", "ffn2_w_down", "final_norm"]
    outs = [loss, dx[None]]
    for idx, src in enumerate((sg, sd, sm, sv)):
        for nm in all_names:
            outs.append(big[nm][idx] if nm in big else src[nm])
    return tuple(outs)
```

```python
import functools

import jax
import jax.numpy as jnp
import numpy as np
from jax import lax
from jax.experimental import pallas as pl
from jax.experimental.pallas import tpu as pltpu

F32 = jnp.float32
BF16 = jnp.bfloat16
MESH = pl.DeviceIdType.MESH
N_DEV = 8

EPS = 1e-6
HEAD_DIM = 64
N_HEADS = 8
N_KV = 2
GROUP = N_HEADS // N_KV
ATTN_W = N_HEADS * HEAD_DIM
KV_W = N_KV * HEAD_DIM
POOL_W = 512
POOL_G = 4
POOL_GW = POOL_W // POOL_G
POOL_WINDOWS = (2, 4, 8, 16)
BLK = 128
ROT = 16
ROPE_THETA = 500000.0
SCORE_SCALE = HEAD_DIM ** -0.5

ADAM_LR, ADAM_B1, ADAM_B2, ADAM_EPS, ADAM_WD, ADAM_STEP = 0.001, 0.9, 0.999, 1e-08, 0.01, 10

VMEM_LIMIT = 56 * 1024 * 1024


def _cparams(sem=None, **kw):
    if sem is not None:
        kw["dimension_semantics"] = sem
    return pltpu.CompilerParams(vmem_limit_bytes=VMEM_LIMIT, **kw)


def _whole(shape):
    nd = len(shape)
    return pl.BlockSpec(shape, lambda *_: (0,) * nd, pipeline_mode=pl.Buffered(1))


def _sigmoid(z):
    return 1.0 / (1.0 + jnp.exp(-z))


def _dot_nt(a, b):
    return lax.dot_general(a, b, (((1,), (1,)), ((), ())), preferred_element_type=F32)


def _dot_nn(a, b):
    return lax.dot_general(a, b, (((1,), (0,)), ((), ())), preferred_element_type=F32)


def _dot_tn(a, b):
    return lax.dot_general(a, b, (((0,), (0,)), ((), ())), preferred_element_type=F32)


def _rope_tables(s):
    inv_freq = ROPE_THETA ** (-np.arange(0, ROT, 2, dtype=np.float64) / ROT)
    ang = np.arange(s, dtype=np.float64)[:, None] * inv_freq[None, :]
    c = np.ones((s, HEAD_DIM)); sa = np.zeros((s, HEAD_DIM)); sb = np.zeros((s, HEAD_DIM))
    c[:, :8] = np.cos(ang); c[:, 8:16] = np.cos(ang)
    sa[:, :8] = -np.sin(ang)
    sb[:, 8:16] = np.sin(ang)
    t = lambda a: jnp.asarray(np.tile(a, (1, 2)).astype(np.float32))
    return t(c), t(sa), t(sb)


def _pool_weight(gi, t, s_pos, s):
    half = POOL_WINDOWS[gi] // 2

    def win(lo, hi):
        a = np.clip(lo, 0, s); b = np.clip(hi + 1, 0, s)
        inside = (s_pos >= a) & (s_pos < b)
        return inside / np.maximum(b - a, 1)

    w = 0.5 * (win(t - half, t + half - 1) + win(t - half + 1, t + half)) - (t == s_pos)
    return w * ((t >= 0) & (t < s) & (s_pos >= 0) & (s_pos < s))


def _pool_tables(s):
    nb = s // BLK
    fwd = np.zeros((3, POOL_G, BLK, 3 * BLK), np.float32)
    bwd = np.zeros((3, POOL_G, BLK, 3 * BLK), np.float32)
    for vi, n in enumerate((0, 1 if nb > 2 else 0, nb - 1)):
        i = n * BLK + np.arange(BLK)[:, None]
        j = (n - 1) * BLK + np.arange(3 * BLK)[None, :]
        for gi in range(POOL_G):
            fwd[vi, gi] = _pool_weight(gi, i, j, s)
            bwd[vi, gi] = _pool_weight(gi, j, i, s)
    return jnp.asarray(fwd, dtype=BF16), jnp.asarray(bwd, dtype=BF16)


def _variant_index(n, nb):
    return jnp.where(n == 0, 0, jnp.where(n == nb - 1, 2, 1))


class _Exchange:
    inputs = ()
    out_shapes = ()
    sems = ()

    def start(self, srcs, outs, sems):
        raise NotImplementedError

    def finish(self, srcs, outs, sems):
        raise NotImplementedError


class _AllGather(_Exchange):
    def __init__(self, arrays):
        n = len(arrays)
        self.inputs = list(arrays)
        self.out_shapes = [jax.ShapeDtypeStruct((N_DEV,) + a.shape, a.dtype) for a in arrays]
        self.sems = [pltpu.SemaphoreType.DMA((n, 7)), pltpu.SemaphoreType.DMA((n, 7)), pltpu.SemaphoreType.DMA((n,))]

    def _parts(self, srcs, outs, sems):
        send_sems, recv_sems, local_sems = sems
        n = len(srcs)
        x, y, c = lax.axis_index("x"), lax.axis_index("y"), lax.axis_index("c")
        me, sibling = (x, y, c), (x, y, 1 - c)
        chips = [(1 - x, y), (x, 1 - y), (1 - x, 1 - y)]

        def slot(a, dev):
            return outs[a].at[4 * dev[0] + 2 * dev[1] + dev[2]]

        def copy(a, k, block, to, src=None):
            return pltpu.make_async_remote_copy(
                src_ref=slot(a, block) if src is None else src, dst_ref=slot(a, block),
                send_sem=send_sems.at[a, k], recv_sem=recv_sems.at[a, k], device_id=to, device_id_type=MESH)

        mine = [pltpu.make_async_copy(srcs[a], slot(a, me), local_sems.at[a]) for a in range(n)]
        first = []
        for a in range(n):
            first.append(copy(a, 0, me, sibling, src=srcs[a]))
            first += [copy(a, 1 + j, me, (*chip, c), src=srcs[a]) for j, chip in enumerate(chips)]
        return n, c, me, sibling, chips, copy, mine, first

    def start(self, srcs, outs, sems):
        _, _, _, _, _, _, mine, first = self._parts(srcs, outs, sems)
        for cp in mine + first:
            cp.start()

    def finish(self, srcs, outs, sems):
        n, c, me, sibling, chips, copy, mine, first = self._parts(srcs, outs, sems)
        passed = []
        for j, chip in enumerate(chips):
            for a in range(n):
                copy(a, 1 + j, (*chip, c), me).wait_recv()
                fwd = copy(a, 4 + j, (*chip, c), sibling)
                fwd.start()
                passed.append(fwd)
        for a in range(n):
            copy(a, 0, sibling, me).wait_recv()
            for j, chip in enumerate(chips):
                copy(a, 4 + j, (*chip, 1 - c), me).wait_recv()
        for cp in first + passed:
            cp.wait_send()
        for cp in mine:
            cp.wait()


class _RsStage2(_Exchange):
    def start(self, srcs, outs, sems):
        for cp in self._copies(srcs, outs, sems):
            cp.start()

    def finish(self, srcs, outs, sems):
        copies = self._copies(srcs, outs, sems)
        for cp in copies:
            cp.wait_recv()
        for cp in copies:
            cp.wait_send()


    def __init__(self, pbs):
        n = len(pbs)
        self.inputs = list(pbs)
        self.out_shapes = [jax.ShapeDtypeStruct((3,) + p.shape[1:], p.dtype) for p in pbs]
        self.sems = [pltpu.SemaphoreType.DMA((n, 3)), pltpu.SemaphoreType.DMA((n, 3))]

    def _copies(self, srcs, outs, sems):
        send_sems, recv_sems = sems
        x, y, c = lax.axis_index("x"), lax.axis_index("y"), lax.axis_index("c")
        chips = [(1 - x, y), (x, 1 - y), (1 - x, 1 - y)]
        return [pltpu.make_async_remote_copy(
            src_ref=srcs[a].at[2 * chip[0] + chip[1]], dst_ref=outs[a].at[j], send_sem=send_sems.at[a, j],
            recv_sem=recv_sems.at[a, j], device_id=(*chip, c), device_id_type=MESH)
            for a in range(len(srcs)) for j, chip in enumerate(chips)]


class _DirectGather(_Exchange):
    def __init__(self, arrays):
        n = len(arrays)
        self.inputs = list(arrays)
        self.out_shapes = [jax.ShapeDtypeStruct((N_DEV,) + a.shape, a.dtype) for a in arrays]
        self.sems = [pltpu.SemaphoreType.DMA((n, 7)), pltpu.SemaphoreType.DMA((n, 7)), pltpu.SemaphoreType.DMA((n,))]

    def _copies(self, srcs, outs, sems):
        send_sems, recv_sems, local_sems = sems
        x, y, c = lax.axis_index("x"), lax.axis_index("y"), lax.axis_index("c")
        me = 4 * x + 2 * y + c
        remote, local = [], []
        for a in range(len(srcs)):
            local.append(pltpu.make_async_copy(srcs[a], outs[a].at[me], local_sems.at[a]))
            for k in range(1, N_DEV):
                peer = (x ^ (k >> 2), y ^ ((k >> 1) & 1), c ^ (k & 1))
                remote.append(pltpu.make_async_remote_copy(
                    src_ref=srcs[a], dst_ref=outs[a].at[me], send_sem=send_sems.at[a, k - 1],
                    recv_sem=recv_sems.at[a, k - 1], device_id=peer, device_id_type=MESH))
        return remote, local

    def start(self, srcs, outs, sems):
        remote, local = self._copies(srcs, outs, sems)
        for cp in local + remote:
            cp.start()

    def finish(self, srcs, outs, sems):
        remote, local = self._copies(srcs, outs, sems)
        for cp in remote:
            cp.wait_recv()
        for cp in remote:
            cp.wait_send()
        for cp in local:
            cp.wait()


class _Both(_Exchange):
    def __init__(self, a, b):
        self.a, self.b = a, b
        self.inputs = list(a.inputs) + list(b.inputs)
        self.out_shapes = list(a.out_shapes) + list(b.out_shapes)
        self.sems = list(a.sems) + list(b.sems)

    def _split(self, srcs, outs, sems):
        na, oa, sa = len(self.a.inputs), len(self.a.out_shapes), len(self.a.sems)
        return (srcs[:na], outs[:oa], sems[:sa]), (srcs[na:], outs[oa:], sems[sa:])

    def start(self, srcs, outs, sems):
        pa, pb = self._split(srcs, outs, sems)
        self.a.start(*pa)
        self.b.start(*pb)

    def finish(self, srcs, outs, sems):
        pa, pb = self._split(srcs, outs, sems)
        self.a.finish(*pa)
        self.b.finish(*pb)


_ANY = pl.BlockSpec(memory_space=pl.ANY)


def _run_exchange(ex, name):
    n_in, n_out = len(ex.inputs), len(ex.out_shapes)

    def body(*refs):
        srcs, outs, sems = refs[:n_in], refs[n_in:n_in + n_out], refs[n_in + n_out:]
        ex.start(srcs, outs, sems)
        ex.finish(srcs, outs, sems)

    return pl.pallas_call(
        body, name=name, out_shape=list(ex.out_shapes), in_specs=[_ANY] * n_in, out_specs=[_ANY] * n_out,
        scratch_shapes=list(ex.sems),
    )(*ex.inputs)


def _call(body, name, grid, in_specs, out_specs, out_shape, args, sem, carry=None, **kw):
    if carry is None:
        return pl.pallas_call(functools.partial(body), name=name, grid=grid, in_specs=in_specs, out_specs=out_specs,
                              out_shape=out_shape, compiler_params=_cparams(sem), **kw)(*args)
    n_in, n_out = len(in_specs), len(out_specs)
    nc_in, nc_out = len(carry.inputs), len(carry.out_shapes)

    def carried(*refs):
        ins = refs[:n_in]
        c_in = refs[n_in:n_in + nc_in]
        outs = refs[n_in + nc_in:n_in + nc_in + n_out]
        c_out = refs[n_in + nc_in + n_out:n_in + nc_in + n_out + nc_out]
        sems = refs[n_in + nc_in + n_out + nc_out:]
        ids = [pl.program_id(i) for i in range(len(grid))]
        is_first = functools.reduce(jnp.logical_and, [i == 0 for i in ids])
        is_last = functools.reduce(jnp.logical_and, [i == g - 1 for i, g in zip(ids, grid)])

        @pl.when(is_first)
        def _():
            carry.start(c_in, c_out, sems)

        body(*ins, *outs)

        @pl.when(is_last)
        def _():
            carry.finish(c_in, c_out, sems)

    return pl.pallas_call(
        carried, name=name, grid=grid, in_specs=list(in_specs) + [_ANY] * nc_in,
        out_specs=list(out_specs) + [_ANY] * nc_out, out_shape=list(out_shape) + list(carry.out_shapes),
        scratch_shapes=list(carry.sems), compiler_params=_cparams(sem), **kw)(*args, *carry.inputs)


def _rs_stage1_sum(pos, g, gb, name, local_order=False):
    _, rows, d = g.shape

    def block(px, py, pc, q, other_core):
        if local_order:
            return _local_block(px, py, q // 2, q % 2, 1 if other_core else 0)
        return 4 * (q // 2) + 2 * (q % 2) + ((1 - pc) if other_core else pc)

    def body(pos_ref, g_ref, gb_ref, p_ref, pb_ref, land, send_sems, recv_sems):
        q = pl.program_id(0)
        x, y, c = lax.axis_index("x"), lax.axis_index("y"), lax.axis_index("c")

        def copy(k):
            return pltpu.make_async_remote_copy(
                src_ref=gb_ref.at[block(x, y, c, k, True)], dst_ref=land.at[k],
                send_sem=send_sems.at[k], recv_sem=recv_sems.at[k], device_id=(x, y, 1 - c), device_id_type=MESH)

        @pl.when(q == 0)
        def _():
            for k in range(4):
                copy(k).start()

        copy(q).wait_recv()
        p = g_ref[...] + land[q].astype(F32)
        p_ref[...] = p
        pb_ref[...] = p.astype(BF16)

        @pl.when(q == 3)
        def _():
            for k in range(4):
                copy(k).wait_send()

    blk = lambda q, pos_ref: (q, 0, 0)
    return pl.pallas_call(
        body, name=name,
        grid_spec=pltpu.PrefetchScalarGridSpec(
            num_scalar_prefetch=1, grid=(4,),
            in_specs=[pl.BlockSpec((None, rows, d),
                                   lambda q, pos_ref: (block(pos_ref[0], pos_ref[1], pos_ref[2], q, False), 0, 0)),
                      _ANY],
            out_specs=[pl.BlockSpec((None, rows, d), blk), pl.BlockSpec((None, rows, d), blk)],
            scratch_shapes=[pltpu.VMEM((4, rows, d), BF16), pltpu.SemaphoreType.DMA((4,)),
                            pltpu.SemaphoreType.DMA((4,))]),
        out_shape=[jax.ShapeDtypeStruct((4, rows, d), F32), jax.ShapeDtypeStruct((4, rows, d), BF16)],
        compiler_params=_cparams(("arbitrary",)),
    )(pos, g, gb)


def _rs_sum2_adam(pos, p, r2, w, m, v, name):
    _, rows, d = p.shape
    tr = rows // 2 if rows % 16 == 0 else rows

    def body(pos_ref, p_ref, r_ref, w_ref, m_ref, v_ref, g_ref, d_ref, nm_ref, nv_ref):
        r = r_ref[...].astype(F32)
        g = ((p_ref[...] + r[0]) + r[1]) + r[2]
        g_ref[...] = g
        d_ref[...], nm_ref[...], nv_ref[...] = _adam_math(w_ref[...], g, m_ref[...], v_ref[...])

    blk = pl.BlockSpec((tr, d), lambda i, pos_ref: (i, 0))
    return pl.pallas_call(
        body, name=name,
        grid_spec=pltpu.PrefetchScalarGridSpec(
            num_scalar_prefetch=1, grid=(rows // tr,),
            in_specs=[pl.BlockSpec((None, tr, d), lambda i, pos_ref: (2 * pos_ref[0] + pos_ref[1], i, 0)),
                      pl.BlockSpec((3, tr, d), lambda i, pos_ref: (0, i, 0)), blk, blk, blk],
            out_specs=[blk] * 4),
        out_shape=[jax.ShapeDtypeStruct((rows, d), F32)] * 4,
        compiler_params=_cparams(("arbitrary",)),
    )(pos, p, r2, w, m, v)


def _ffn_chunk(f):
    for cand in (256, 128):
        if f % cand == 0:
            return cand
    return f


def _ffn_fwd(h, g, wg_t, wu_t, wd, name, tm, carry=None):
    s, d = h.shape
    f = wd.shape[0]
    tf = _ffn_chunk(f)

    def body(h_ref, g_ref, wg_ref, wu_ref, wd_ref, o_ref, n_ref, gate_ref, up_ref, act_ref):
        x = h_ref[...]
        r = lax.rsqrt(jnp.mean(x * x, axis=-1, keepdims=True) + EPS)
        nb = (x * r * g_ref[...]).astype(BF16)
        n_ref[...] = nb
        for j in range(f // tf):
            sl = slice(j * tf, (j + 1) * tf)
            gate = _dot_nt(nb, wg_ref[sl, :])
            up = _dot_nt(nb, wu_ref[sl, :])
            gate_ref[:, sl] = gate.astype(BF16)
            up_ref[:, sl] = up.astype(BF16)
            act_ref[:, sl] = (gate * _sigmoid(gate) * up).astype(BF16)
        o_ref[...] = x + 0.5 * _dot_nn(act_ref[...], wd_ref[...])

    row = lambda w: pl.BlockSpec((tm, w), lambda i: (i, 0))
    return _call(
        body, name, (s // tm,),
        [row(d), _whole((1, d)), _whole((f, d)), _whole((f, d)), _whole((f, d))],
        [row(d), row(d), row(f), row(f), row(f)],
        [jax.ShapeDtypeStruct((s, d), F32), jax.ShapeDtypeStruct((s, d), BF16),
         jax.ShapeDtypeStruct((s, f), BF16), jax.ShapeDtypeStruct((s, f), BF16), jax.ShapeDtypeStruct((s, f), BF16)],
        (h, g, wg_t, wu_t, wd), ("arbitrary",), carry)


def _local_block(pos_x, pos_y, dev_x, dev_y, dev_c_differs):
    return 4 * (dev_x ^ pos_x) + 2 * (dev_y ^ pos_y) + dev_c_differs


def _ffn_fwd_gather(h, g, shards, name, tm, carry=None):
    s, d = h.shape
    fk = shards[0].shape[0]
    f = N_DEV * fk
    fh = f // 2
    nt = s // tm
    chunks, off = [], 0
    while off < fh:
        size = 256 if fh - off >= 256 else fh - off
        chunks.append((off, size))
        off += size
    n_carry_in = 0 if carry is None else len(carry.inputs)
    n_carry_out = 0 if carry is None else len(carry.out_shapes)

    def body(h_ref, g_ref, *refs):
        loc = refs[:3]
        c_in = refs[3:3 + n_carry_in]
        o_ref, n_ref, gate_ref, up_ref, act_ref = refs[3 + n_carry_in:8 + n_carry_in]
        full = refs[8 + n_carry_in:11 + n_carry_in]
        c_out = refs[11 + n_carry_in:11 + n_carry_in + n_carry_out]
        wbuf, acc, send_sems, recv_sems, local_sems, load_sems = refs[11 + n_carry_in + n_carry_out:][:6]
        c_sems = refs[17 + n_carry_in + n_carry_out:]
        half, i = pl.program_id(0), pl.program_id(1)
        x, y, c = lax.axis_index("x"), lax.axis_index("y"), lax.axis_index("c")

        def slot(a, k):
            return full[a].at[pl.ds(k * fk, fk), :]

        def send(a, k, src):
            peer = (x ^ (k >> 2), y ^ ((k >> 1) & 1), c ^ (k & 1))
            return pltpu.make_async_remote_copy(src_ref=src, dst_ref=slot(a, k), send_sem=send_sems.at[a, k - 1],
                                                recv_sem=recv_sems.at[a, k - 1], device_id=peer, device_id_type=MESH)

        own = [send(a, k, loc[a]) for a in range(3) for k in (1, 2, 4, 6)]
        passed = {k: [pltpu.make_async_remote_copy(
            src_ref=slot(a, k), dst_ref=slot(a, k + 1), send_sem=send_sems.at[a, k], recv_sem=recv_sems.at[a, k],
            device_id=(x, y, 1 - c), device_id_type=MESH) for a in range(3)] for k in (2, 4, 6)}
        mine = [pltpu.make_async_copy(loc[a], slot(a, 0), local_sems.at[a]) for a in range(3)]

        def arrived(k):
            for a in range(3):
                send(a, k, loc[a]).wait_recv()

        def load(hf):
            cps = [pltpu.make_async_copy(full[a].at[pl.ds(hf * fh, fh), :], wbuf.at[a], load_sems.at[a])
                   for a in range(3)]
            for cp in cps:
                cp.start()
            for cp in cps:
                cp.wait()

        @pl.when((half == 0) & (i == 0))
        def _():
            for cp in mine + own:
                cp.start()
            for cp in mine:
                cp.wait()
            arrived(1)
            arrived(2)
            for cp in passed[2]:
                cp.start()
            arrived(3)
            if carry is not None:
                carry.start(c_in, c_out, c_sems)
            load(0)

        @pl.when((half == 1) & (i == 0))
        def _():
            for k in (4, 6):
                arrived(k)
                for cp in passed[k]:
                    cp.start()
            arrived(5)
            arrived(7)
            load(1)

        xv = h_ref[...]
        r = lax.rsqrt(jnp.mean(xv * xv, axis=-1, keepdims=True) + EPS)
        nb = (xv * r * g_ref[...]).astype(BF16)

        @pl.when(half == 0)
        def _():
            n_ref[...] = nb

        for off, size in chunks:
            sl = slice(off, off + size)
            gate = _dot_nt(nb, wbuf[0, sl, :])
            up = _dot_nt(nb, wbuf[1, sl, :])
            gate_ref[:, sl] = gate.astype(BF16)
            up_ref[:, sl] = up.astype(BF16)
            act_ref[:, sl] = (gate * _sigmoid(gate) * up).astype(BF16)
        part = _dot_nn(act_ref[...], wbuf[2])
        rows = pl.ds(pl.multiple_of(i * tm, tm), tm)

        @pl.when(half == 0)
        def _():
            acc[rows, :] = part

        @pl.when(half == 1)
        def _():
            o_ref[...] = xv + 0.5 * (acc[rows, :] + part)

        @pl.when((half == 1) & (i == nt - 1))
        def _():
            for cp in own + passed[2] + passed[4] + passed[6]:
                cp.wait_send()
            if carry is not None:
                carry.finish(c_in, c_out, c_sems)

    row = pl.BlockSpec((tm, d), lambda hf, i: (i, 0))
    cols = pl.BlockSpec((tm, fh), lambda hf, i: (i, hf))
    outs = pl.pallas_call(
        body, name=name, grid=(2, nt),
        in_specs=[row, _whole((1, d))] + [_ANY] * (3 + n_carry_in),
        out_specs=[pl.BlockSpec((tm, d), lambda hf, i: (i * hf, 0)),
                   pl.BlockSpec((tm, d), lambda hf, i: (i * (1 - hf) + (nt - 1) * hf, 0)), cols, cols, cols]
        + [_ANY] * (3 + n_carry_out),
        out_shape=[jax.ShapeDtypeStruct((s, d), F32), jax.ShapeDtypeStruct((s, d), BF16)]
        + [jax.ShapeDtypeStruct((s, f), BF16)] * 3 + [jax.ShapeDtypeStruct((f, d), BF16)] * 3
        + ([] if carry is None else list(carry.out_shapes)),
        scratch_shapes=[pltpu.VMEM((3, fh, d), BF16), pltpu.VMEM((s, d), F32),
                        pltpu.SemaphoreType.DMA((3, 7)), pltpu.SemaphoreType.DMA((3, 7)),
                        pltpu.SemaphoreType.DMA((3,)), pltpu.SemaphoreType.DMA((3,))]
        + ([] if carry is None else list(carry.sems)),
        compiler_params=_cparams(("arbitrary", "arbitrary")),
    )(h, g, *shards, *([] if carry is None else carry.inputs))
    return outs


def _gate_grads(dh_ref, gate_ref, up_ref, wd_ref, dgate_ref, dup_ref, dhh_ref, tf):
    dhh = (0.5 * dh_ref[...]).astype(BF16)
    dhh_ref[...] = dhh
    for j in range(gate_ref.shape[1] // tf):
        sl = slice(j * tf, (j + 1) * tf)
        gt = gate_ref[:, sl].astype(F32)
        u = up_ref[:, sl].astype(F32)
        dact = _dot_nt(dhh, wd_ref[sl, :])
        sg = _sigmoid(gt)
        dup_ref[:, sl] = (dact * (gt * sg)).astype(BF16)
        dgate_ref[:, sl] = (dact * u * (sg * (1.0 + gt * (1.0 - sg)))).astype(BF16)


def _input_grad(h_ref, dh_ref, dgate_ref, dup_ref, g_ref, wg_ref, wu_ref, o_ref, dg_ref):
    x = h_ref[...]
    r = lax.rsqrt(jnp.mean(x * x, axis=-1, keepdims=True) + EPS)
    xhat = x * r
    dn = _dot_nn(dgate_ref[...], wg_ref[...]) + _dot_nn(dup_ref[...], wu_ref[...])
    dxhat = dn * g_ref[...]
    o_ref[...] = dh_ref[...] + r * (dxhat - xhat * jnp.mean(dxhat * xhat, axis=-1, keepdims=True))

    @pl.when(pl.program_id(0) == 0)
    def _():
        dg_ref[...] = jnp.zeros_like(dg_ref)

    dg_ref[...] += jnp.sum(dn * xhat, axis=0, keepdims=True)


def _ffn_bwd(h_in, dh_out, gate, up, g, wg_t, wu_t, wd, name, tm):
    s, d = h_in.shape
    f = gate.shape[1]
    tf = _ffn_chunk(f)

    def body(h_ref, dh_ref, gate_ref, up_ref, g_ref, wg_ref, wu_ref, wd_ref,
             o_ref, dg_ref, dgate_ref, dup_ref, dhh_ref):
        _gate_grads(dh_ref, gate_ref, up_ref, wd_ref, dgate_ref, dup_ref, dhh_ref, tf)
        _input_grad(h_ref, dh_ref, dgate_ref, dup_ref, g_ref, wg_ref, wu_ref, o_ref, dg_ref)

    row = lambda w: pl.BlockSpec((tm, w), lambda i: (i, 0))
    return pl.pallas_call(
        body, name=name, grid=(s // tm,),
        in_specs=[row(d), row(d), row(f), row(f), _whole((1, d)), _whole((f, d)), _whole((f, d)), _whole((f, d))],
        out_specs=[row(d), pl.BlockSpec((1, d), lambda i: (0, 0)), row(f), row(f), row(d)],
        out_shape=[jax.ShapeDtypeStruct((s, d), F32), jax.ShapeDtypeStruct((1, d), F32),
                   jax.ShapeDtypeStruct((s, f), BF16), jax.ShapeDtypeStruct((s, f), BF16),
                   jax.ShapeDtypeStruct((s, d), BF16)],
        compiler_params=_cparams(("arbitrary",)),
    )(h_in, dh_out, gate, up, g, wg_t, wu_t, wd)


def _ffn_bwd_gates(dh_out, gate, up, wd, name, tm, carry=None):
    s, d = dh_out.shape
    f = gate.shape[1]
    tf = _ffn_chunk(f)

    def body(dh_ref, gate_ref, up_ref, wd_ref, dgate_ref, dup_ref, dhh_ref):
        _gate_grads(dh_ref, gate_ref, up_ref, wd_ref, dgate_ref, dup_ref, dhh_ref, tf)

    row = lambda w: pl.BlockSpec((tm, w), lambda i: (i, 0))
    return _call(
        body, name, (s // tm,), [row(d), row(f), row(f), _whole((f, d))], [row(f), row(f), row(d)],
        [jax.ShapeDtypeStruct((s, f), BF16), jax.ShapeDtypeStruct((s, f), BF16), jax.ShapeDtypeStruct((s, d), BF16)],
        (dh_out, gate, up, wd), ("arbitrary",), carry)


def _ffn_bwd_input(h_in, dh_out, dgate, dup, g, wg_t, wu_t, name, tm, carry=None):
    s, d = h_in.shape
    f = dgate.shape[1]

    row = lambda w: pl.BlockSpec((tm, w), lambda i: (i, 0))
    return _call(
        _input_grad, name, (s // tm,),
        [row(d), row(d), row(f), row(f), _whole((1, d)), _whole((f, d)), _whole((f, d))],
        [row(d), pl.BlockSpec((1, d), lambda i: (0, 0))],
        [jax.ShapeDtypeStruct((s, d), F32), jax.ShapeDtypeStruct((1, d), F32)],
        (h_in, dh_out, dgate, dup, g, wg_t, wu_t), ("arbitrary",), carry)


def _wgrad(a, b, name, tf, carry=None):
    s, f = a.shape
    d = b.shape[1]

    def body(a_ref, b_ref, o_ref, ob_ref):
        acc = _dot_tn(a_ref[...], b_ref[...])
        o_ref[...] = acc
        ob_ref[...] = acc.astype(BF16)

    return _call(
        body, name, (f // tf,),
        [pl.BlockSpec((s, tf), lambda i: (0, i)), _whole((s, d))],
        [pl.BlockSpec((tf, d), lambda i: (i, 0)), pl.BlockSpec((tf, d), lambda i: (i, 0))],
        [jax.ShapeDtypeStruct((f, d), F32), jax.ShapeDtypeStruct((f, d), BF16)],
        (a, b), ("arbitrary",), carry)


def _rope(t, c, sa, sb, reps):
    c, sa, sb = (jnp.tile(v, (1, reps)) if reps > 1 else v for v in (c, sa, sb))
    w = t.shape[1]
    return t * c + pltpu.roll(t, w - 8, 1) * sa + pltpu.roll(t, 8, 1) * sb


def _rope_bwd(dt, c, sa, sb, reps):
    c, sa, sb = (jnp.tile(v, (1, reps)) if reps > 1 else v for v in (c, sa, sb))
    w = dt.shape[1]
    return dt * c + pltpu.roll(dt * sa, 8, 1) + pltpu.roll(dt * sb, w - 8, 1)


def _mix_in(h, g, win_t, tabs, name, tm):
    s, d = h.shape
    n_in = win_t.shape[0]

    def body(h_ref, g_ref, w_ref, c_ref, sa_ref, sb_ref, q_ref, k_ref, v_ref, pc_ref, n_ref):
        x = h_ref[...]
        r = lax.rsqrt(jnp.mean(x * x, axis=-1, keepdims=True) + EPS)
        nb = (x * r * g_ref[...]).astype(BF16)
        n_ref[...] = nb
        u = _dot_nt(nb, w_ref[...])
        c, sa, sb = c_ref[...], sa_ref[...], sb_ref[...]
        q_ref[...] = _rope(u[:, :ATTN_W], c, sa, sb, ATTN_W // 128).astype(BF16)
        k_ref[...] = _rope(u[:, ATTN_W:ATTN_W + KV_W], c, sa, sb, 1).astype(BF16)
        v_ref[...] = u[:, ATTN_W + KV_W:ATTN_W + 2 * KV_W].astype(BF16)
        pc_ref[...] = u[:, ATTN_W + 2 * KV_W:]

    row = lambda w: pl.BlockSpec((tm, w), lambda i: (i, 0))
    return pl.pallas_call(
        body, name=name, grid=(s // tm,),
        in_specs=[row(d), _whole((1, d)), _whole((n_in, d)), row(128), row(128), row(128)],
        out_specs=[row(ATTN_W), row(KV_W), row(KV_W), row(POOL_W), row(d)],
        out_shape=[jax.ShapeDtypeStruct((s, ATTN_W), BF16), jax.ShapeDtypeStruct((s, KV_W), BF16),
                   jax.ShapeDtypeStruct((s, KV_W), BF16), jax.ShapeDtypeStruct((s, POOL_W), F32),
                   jax.ShapeDtypeStruct((s, d), BF16)],
        compiler_params=_cparams(("arbitrary",)),
    )(h, g, win_t, *tabs)


def _band_mask(n, nb, transposed):
    shape = (3 * BLK, 2 * BLK) if transposed else (2 * BLK, 3 * BLK)
    i = lax.broadcasted_iota(jnp.int32, shape, 1 if transposed else 0) % BLK
    j = lax.broadcasted_iota(jnp.int32, shape, 0 if transposed else 1)
    kpos = (n - 1) * BLK + j
    return (j >= i) & (j <= i + 2 * BLK) & (kpos >= 0) & (kpos < nb * BLK)


def _block_diag(t, kh):
    tf = t.astype(F32)
    tr = pltpu.roll(tf, HEAD_DIM, 1)
    lo = lax.broadcasted_iota(jnp.int32, tf.shape, 1) < HEAD_DIM
    top, bot = (tf, tr) if kh == 0 else (tr, tf)
    return jnp.concatenate([jnp.where(lo, top, 0.0), jnp.where(lo, 0.0, bot)], axis=0).astype(BF16)


def _fold_diag(tbd):
    lo = lax.broadcasted_iota(jnp.int32, (3 * BLK, 2 * HEAD_DIM), 1) < HEAD_DIM
    t = jnp.where(lo, tbd[:3 * BLK], tbd[3 * BLK:])
    return t + pltpu.roll(t, HEAD_DIM, 1)


def _stack_pairs(x, kh):
    return jnp.concatenate([x[:, (2 * kh) * 128:(2 * kh + 1) * 128], x[:, (2 * kh + 1) * 128:(2 * kh + 2) * 128]], axis=0)


def _sink_of(sink_ref, kh, half, axis):
    shape = (2 * BLK, 1) if axis == 0 else (1, 2 * BLK)
    first = lax.broadcasted_iota(jnp.int32, shape, axis) < BLK
    return jnp.where(first, sink_ref[0, GROUP * kh + half], sink_ref[0, GROUP * kh + 2 + half])


def _softmax_sink(sc, valid, sink, axis):
    sc = jnp.where(valid, sc, -1e30)
    m = jnp.maximum(jnp.max(sc, axis=axis, keepdims=True), sink)
    e = jnp.exp(sc - m)
    es = jnp.exp(sink - m)
    inv = 1.0 / (jnp.sum(e, axis=axis, keepdims=True) + es)
    return e * inv, es * inv


def _attn_pool_fwd(q, k, v, pc, sink, pool_w, pool_scale, pband, name, carry=None):
    s = q.shape[0]
    nb = s // BLK

    def body(sink_ref, q_ref, k0, k1, k2, v0, v1, v2, p0, p1, p2, pw_ref, ps_ref, pb_ref, o_ref):
        n = pl.program_id(0)
        valid = _band_mask(n, nb, False)
        kb = jnp.concatenate([k0[...], k1[...], k2[...]], axis=0)
        vb = jnp.concatenate([v0[...], v1[...], v2[...]], axis=0)
        qs = q_ref[...] * SCORE_SCALE
        for kh in range(N_KV):
            sc = _dot_nt(_stack_pairs(qs, kh), _block_diag(kb, kh))
            p = [_softmax_sink(sc[:, half * 3 * BLK:(half + 1) * 3 * BLK], valid, _sink_of(sink_ref, kh, half, 0), 1)[0]
                 for half in range(2)]
            o2 = _dot_nn(jnp.concatenate(p, axis=1).astype(BF16), _block_diag(vb, kh)).astype(BF16)
            o_ref[:, (2 * kh) * 128:(2 * kh + 1) * 128] = o2[:BLK]
            o_ref[:, (2 * kh + 1) * 128:(2 * kh + 2) * 128] = o2[BLK:]
        ext = jnp.concatenate([p0[...], p1[...], p2[...]], axis=0).astype(BF16)
        for gi in range(POOL_G):
            gsl = slice(gi * POOL_GW, (gi + 1) * POOL_GW)
            dg = _dot_nn(pb_ref[gi], ext[:, gsl])
            yg = _dot_nn(dg.astype(BF16), pw_ref[gi].astype(BF16))
            o_ref[:, ATTN_W + gi * POOL_GW:ATTN_W + (gi + 1) * POOL_GW] = (yg * ps_ref[:, gsl]).astype(BF16)

    def band(w):
        return [pl.BlockSpec((BLK, w), lambda n: (jnp.maximum(n - 1, 0), 0)),
                pl.BlockSpec((BLK, w), lambda n: (n, 0)),
                pl.BlockSpec((BLK, w), lambda n: (jnp.minimum(n + 1, nb - 1), 0))]

    return _call(
        body, name, (nb,),
        [pl.BlockSpec(memory_space=pltpu.SMEM), pl.BlockSpec((BLK, ATTN_W), lambda n: (n, 0)),
         *band(KV_W), *band(KV_W), *band(POOL_W),
         _whole((POOL_G, POOL_GW, POOL_GW)), _whole((1, POOL_W)),
         pl.BlockSpec((None, POOL_G, BLK, 3 * BLK), lambda n: (_variant_index(n, nb), 0, 0, 0))],
        [pl.BlockSpec((BLK, ATTN_W + POOL_W), lambda n: (n, 0))],
        [jax.ShapeDtypeStruct((s, ATTN_W + POOL_W), BF16)],
        (sink, q, k, k, k, v, v, v, pc, pc, pc, pool_w, pool_scale, pband), ("arbitrary",), carry)


def _attn_pool_bwd(q, k, v, pc, dmix, sink, pool_w, pool_scale, pband, ptband, name, carry=None):
    s = q.shape[0]
    nb = s // BLK

    def body(sink_ref, q_ref, k0, k1, k2, v0, v1, v2, p0, p1, p2, da_ref, d0, d1, d2, pw_ref, ps_ref, pb_ref, ptb_ref,
             dq_ref, dk_ref, dv_ref, dpc_ref, dsink_ref, dpw_ref, dps_ref):
        n = pl.program_id(0)

        @pl.when(n == 0)
        def _():
            dsink_ref[...] = jnp.zeros_like(dsink_ref)
            dpw_ref[...] = jnp.zeros_like(dpw_ref)
            dps_ref[...] = jnp.zeros_like(dps_ref)

        valid = _band_mask(n, nb, True)
        kb = jnp.concatenate([k0[...], k1[...], k2[...]], axis=0)
        vb = jnp.concatenate([v0[...], v1[...], v2[...]], axis=0)
        qb = q_ref[...]
        qs = qb * SCORE_SCALE
        da = da_ref[...].astype(BF16)
        dk_fold, dv_fold = [], []
        for kh in range(N_KV):
            kbd, vbd = _block_diag(kb, kh), _block_diag(vb, kh)
            q2, do2 = _stack_pairs(qb, kh), _stack_pairs(da, kh)
            sc_t = _dot_nt(kbd, _stack_pairs(qs, kh))
            dp_t = _dot_nt(vbd, do2)
            p_t, ds_t = [], []
            for half in range(2):
                rows = slice(half * 3 * BLK, (half + 1) * 3 * BLK)
                p, ps = _softmax_sink(sc_t[rows], valid, _sink_of(sink_ref, kh, half, 1), 0)
                delta = jnp.sum(p * dp_t[rows], axis=0, keepdims=True)
                p_t.append(p.astype(BF16))
                ds_t.append((p * (dp_t[rows] - delta)).astype(BF16))
                dsk = -ps * delta
                for pair in range(2):
                    h = GROUP * kh + 2 * pair + half
                    part = jnp.sum(dsk[:, pair * BLK:(pair + 1) * BLK], axis=1, keepdims=True)
                    dsink_ref[h:h + 1, :] += jnp.broadcast_to(part, (1, 128))
            p_t = jnp.concatenate(p_t, axis=0)
            ds_t = jnp.concatenate(ds_t, axis=0)
            dq2 = _dot_tn(ds_t, kbd) * SCORE_SCALE
            dq_ref[:, (2 * kh) * 128:(2 * kh + 1) * 128] = dq2[:BLK]
            dq_ref[:, (2 * kh + 1) * 128:(2 * kh + 2) * 128] = dq2[BLK:]
            dk_fold.append(_fold_diag(_dot_nn(ds_t, q2)) * SCORE_SCALE)
            dv_fold.append(_fold_diag(_dot_nn(p_t, do2)))
        lo = lax.broadcasted_iota(jnp.int32, (3 * BLK, KV_W), 1) < HEAD_DIM
        dk_all = jnp.where(lo, dk_fold[0], dk_fold[1])
        dv_all = jnp.where(lo, dv_fold[0], dv_fold[1])
        for t in range(3):
            dk_ref[t] = dk_all[t * BLK:(t + 1) * BLK]
            dv_ref[t] = dv_all[t * BLK:(t + 1) * BLK]
        ext = jnp.concatenate([p0[...], p1[...], p2[...]], axis=0).astype(BF16)
        dpe = jnp.concatenate([d0[...], d1[...], d2[...]], axis=0)
        dpc_cur = d1[...]
        for gi in range(POOL_G):
            gsl = slice(gi * POOL_GW, (gi + 1) * POOL_GW)
            wg = pw_ref[gi].astype(BF16)
            sc = ps_ref[:, gsl]
            dgr = _dot_nn(pb_ref[gi], ext[:, gsl])
            dgb = dgr.astype(BF16)
            yg = _dot_nn(dgb, wg)
            dps_ref[:, gsl] += jnp.sum(dpc_cur[:, gsl] * yg, axis=0, keepdims=True)
            dpw_ref[gi] += _dot_tn(dgb, (dpc_cur[:, gsl] * sc).astype(BF16))
            dd = _dot_nt((dpe[:, gsl] * sc).astype(BF16), wg)
            dpc_ref[:, gsl] = _dot_nn(ptb_ref[gi], dd.astype(BF16))

    def band(w, col=0):
        return [pl.BlockSpec((BLK, w), lambda n: (jnp.maximum(n - 1, 0), col)),
                pl.BlockSpec((BLK, w), lambda n: (n, col)),
                pl.BlockSpec((BLK, w), lambda n: (jnp.minimum(n + 1, nb - 1), col))]

    tab = lambda: pl.BlockSpec((None, POOL_G, BLK, 3 * BLK), lambda n: (_variant_index(n, nb), 0, 0, 0))
    fixed = lambda shape: pl.BlockSpec(shape, lambda n: (0,) * len(shape))
    return _call(
        body, name, (nb,),
        [pl.BlockSpec(memory_space=pltpu.SMEM), pl.BlockSpec((BLK, ATTN_W), lambda n: (n, 0)),
         *band(KV_W), *band(KV_W), *band(POOL_W),
         pl.BlockSpec((BLK, ATTN_W), lambda n: (n, 0)), *band(POOL_W, 1),
         _whole((POOL_G, POOL_GW, POOL_GW)), _whole((1, POOL_W)), tab(), tab()],
        [pl.BlockSpec((BLK, ATTN_W), lambda n: (n, 0)),
         pl.BlockSpec((None, 3, BLK, KV_W), lambda n: (n, 0, 0, 0)),
         pl.BlockSpec((None, 3, BLK, KV_W), lambda n: (n, 0, 0, 0)),
         pl.BlockSpec((BLK, POOL_W), lambda n: (n, 0)),
         fixed((N_HEADS, 128)), fixed((POOL_G, POOL_GW, POOL_GW)), fixed((1, POOL_W))],
        [jax.ShapeDtypeStruct((s, ATTN_W), F32), jax.ShapeDtypeStruct((nb, 3, BLK, KV_W), F32),
         jax.ShapeDtypeStruct((nb, 3, BLK, KV_W), F32), jax.ShapeDtypeStruct((s, POOL_W), F32),
         jax.ShapeDtypeStruct((N_HEADS, 128), F32),
         jax.ShapeDtypeStruct((POOL_G, POOL_GW, POOL_GW), F32), jax.ShapeDtypeStruct((1, POOL_W), F32)],
        (sink, q, k, k, k, v, v, v, pc, pc, pc, dmix, dmix, dmix, dmix, pool_w, pool_scale, pband, ptband),
        ("arbitrary",), carry)


def _mix_out(h, mix, w_out, name, tm):
    s, d = h.shape
    w = mix.shape[1]

    def body(h_ref, m_ref, w_ref, o_ref):
        o_ref[...] = h_ref[...] + _dot_nn(m_ref[...], w_ref[...])

    row = lambda c: pl.BlockSpec((tm, c), lambda i: (i, 0))
    return pl.pallas_call(
        body, name=name, grid=(s // tm,), in_specs=[row(d), row(w), _whole((w, d))], out_specs=row(d),
        out_shape=jax.ShapeDtypeStruct((s, d), F32), compiler_params=_cparams(("arbitrary",)),
    )(h, mix, w_out)


def _mix_out_bwd(dh, w_out, name, tm):
    s, d = dh.shape
    w = w_out.shape[0]

    def body(dh_ref, w_ref, o_ref, dhb_ref):
        dhb = dh_ref[...].astype(BF16)
        dhb_ref[...] = dhb
        o_ref[...] = _dot_nt(dhb, w_ref[...])

    row = lambda c: pl.BlockSpec((tm, c), lambda i: (i, 0))
    return pl.pallas_call(
        body, name=name, grid=(s // tm,), in_specs=[row(d), _whole((w, d))], out_specs=[row(w), row(d)],
        out_shape=[jax.ShapeDtypeStruct((s, w), F32), jax.ShapeDtypeStruct((s, d), BF16)],
        compiler_params=_cparams(("arbitrary",)),
    )(dh, w_out)


def _mix_in_bwd(h, dh, g, win_t, dq, dkp, dvp, dpc, tabs, name, tm):
    s, d = h.shape
    nb = s // BLK
    nt = tm // BLK
    n_in = win_t.shape[0]

    def band_sum(n, before, own, after, prev_last, next_first):
        lo = (n > 0).astype(F32)
        hi = (n < s // tm - 1).astype(F32)
        blocks = []
        for b in range(nt):
            from_prev = prev_last[...] * lo if b == 0 else before[b - 1]
            from_next = next_first[...] * hi if b == nt - 1 else after[b + 1]
            blocks.append(from_prev + own[b] + from_next)
        return jnp.concatenate(blocks, axis=0)

    def body(h_ref, dh_ref, g_ref, w_ref, dq_ref, k2, k1, k0, kp, kn, v2, v1, v0, vp, vn, dpc_ref, c_ref, sa_ref,
             sb_ref, o_ref, du_ref, dg_ref):
        n = pl.program_id(0)
        dk = band_sum(n, k2, k1, k0, kp, kn)
        dv = band_sum(n, v2, v1, v0, vp, vn)
        c, sa, sb = c_ref[...], sa_ref[...], sb_ref[...]
        du = jnp.concatenate([_rope_bwd(dq_ref[...], c, sa, sb, ATTN_W // 128), _rope_bwd(dk, c, sa, sb, 1), dv,
                              dpc_ref[...]], axis=1).astype(BF16)
        du_ref[...] = du
        dn = _dot_nn(du, w_ref[...])
        x = h_ref[...]
        r = lax.rsqrt(jnp.mean(x * x, axis=-1, keepdims=True) + EPS)
        xhat = x * r
        dxhat = dn * g_ref[...]
        o_ref[...] = dh_ref[...] + r * (dxhat - xhat * jnp.mean(dxhat * xhat, axis=-1, keepdims=True))

        @pl.when(n == 0)
        def _():
            dg_ref[...] = jnp.zeros_like(dg_ref)

        dg_ref[...] += jnp.sum(dn * xhat, axis=0, keepdims=True)

    row = lambda w: pl.BlockSpec((tm, w), lambda n: (n, 0))
    slot = lambda t: pl.BlockSpec((nt, None, BLK, KV_W), lambda n, t=t: (n, t, 0, 0))
    parts = [slot(2), slot(1), slot(0),
             pl.BlockSpec((None, None, BLK, KV_W), lambda n: (jnp.maximum(nt * n - 1, 0), 2, 0, 0)),
             pl.BlockSpec((None, None, BLK, KV_W), lambda n: (jnp.minimum(nt * n + nt, nb - 1), 0, 0, 0))]
    return pl.pallas_call(
        body, name=name, grid=(s // tm,),
        in_specs=[row(d), row(d), _whole((1, d)), _whole((n_in, d)), row(ATTN_W), *parts, *parts, row(POOL_W),
                  row(128), row(128), row(128)],
        out_specs=[row(d), row(n_in), pl.BlockSpec((1, d), lambda n: (0, 0))],
        out_shape=[jax.ShapeDtypeStruct((s, d), F32), jax.ShapeDtypeStruct((s, n_in), BF16),
                   jax.ShapeDtypeStruct((1, d), F32)],
        compiler_params=_cparams(("arbitrary",)),
    )(h, dh, g, win_t, dq, *[dkp] * 5, *[dvp] * 5, dpc, *tabs)


def _final(h, g, target, name, tm):
    s, d = h.shape

    def body(h_ref, g_ref, t_ref, loss_ref, dh_ref, dg_ref):
        @pl.when(pl.program_id(0) == 0)
        def _():
            loss_ref[...] = jnp.zeros_like(loss_ref)
            dg_ref[...] = jnp.zeros_like(dg_ref)

        x = h_ref[...]
        gg = g_ref[...]
        r = lax.rsqrt(jnp.mean(x * x, axis=-1, keepdims=True) + EPS)
        xhat = x * r
        e = xhat * gg - t_ref[...]
        per_tok = jnp.mean(e * e, axis=-1, keepdims=True)
        loss_ref[...] += 0.5 * jnp.sum(per_tok, axis=0, keepdims=True)
        dy = e * (1.0 / d)
        dg_ref[...] += jnp.sum(dy * xhat, axis=0, keepdims=True)
        dxhat = dy * gg
        dh_ref[...] = r * (dxhat - xhat * jnp.mean(dxhat * xhat, axis=-1, keepdims=True))

    row = pl.BlockSpec((tm, d), lambda i: (i, 0))
    return pl.pallas_call(
        body, name=name, grid=(s // tm,), in_specs=[row, _whole((1, d)), row],
        out_specs=[pl.BlockSpec((1, 1), lambda i: (0, 0)), row, pl.BlockSpec((1, d), lambda i: (0, 0))],
        out_shape=[jax.ShapeDtypeStruct((1, 1), F32), jax.ShapeDtypeStruct((s, d), F32),
                   jax.ShapeDtypeStruct((1, d), F32)],
        compiler_params=_cparams(("arbitrary",)),
    )(h, g, target)


def _adam_math(w, g, m, v):
    m = ADAM_B1 * m + (1.0 - ADAM_B1) * g
    v = ADAM_B2 * v + (1.0 - ADAM_B2) * (g * g)
    m_hat = m / (1.0 - ADAM_B1 ** ADAM_STEP)
    v_hat = v / (1.0 - ADAM_B2 ** ADAM_STEP)
    delta = -ADAM_LR * (m_hat / (jnp.sqrt(v_hat) + ADAM_EPS) + ADAM_WD * w)
    return delta, m, v


def _adam_small(w, parts, late, m, v, name):
    rows, cols = w.shape

    def body(w_ref, p_ref, l_ref, m_ref, v_ref, g_ref, d_ref, nm_ref, nv_ref):
        g, gl = p_ref[0], l_ref[0]
        for k in range(1, N_DEV):
            g = g + p_ref[k]
            gl = gl + l_ref[k]
        g_ref[...] = g
        g_ref[SMALL_NORM1:SMALL_NORM1 + 8, :] = g[SMALL_NORM1:SMALL_NORM1 + 8] + gl
        d_ref[...], nm_ref[...], nv_ref[...] = _adam_math(w_ref[...], g_ref[...], m_ref[...], v_ref[...])

    return pl.pallas_call(
        body, name=name, out_shape=[jax.ShapeDtypeStruct((rows, cols), F32)] * 4,
    )(w, parts, late, m, v)


SMALL_NORM1 = 512


def _pack_small(norm1, normm, norm2, normf, sink, pool_w, pool_scale, loss=None):
    scale_rows = jnp.pad(pool_scale.reshape(4, 128), ((0, 4), (0, 0)))
    last_rows = jnp.pad(sink.reshape(1, N_HEADS), ((0, 7), (0, 128 - N_HEADS)))
    if loss is not None:
        last_rows = last_rows + jnp.pad(loss.reshape(1, 1), ((1, 6), (0, 127)))
    return jnp.concatenate([pool_w.reshape(512, 128), norm1.reshape(8, 128), normm.reshape(8, 128),
                            norm2.reshape(8, 128), normf.reshape(8, 128), scale_rows, last_rows], axis=0)


def _unpack_small(p):
    return dict(pool_w=p[:512].reshape(1, POOL_G, POOL_GW, POOL_GW), ffn1_norm=p[512:520].reshape(1, 1024),
                mix_norm=p[520:528].reshape(1, 1024), ffn2_norm=p[528:536].reshape(1, 1024),
                final_norm=p[536:544].reshape(1024), pool_scale=p[544:548].reshape(1, POOL_W),
                sink_logits=p[552, :N_HEADS].reshape(1, N_HEADS), loss=p[553, 0])


def kernel(x, ffn1_norm, ffn1_w_gate, ffn1_w_up, ffn1_w_down, mix_norm, w_in, sink_logits, pool_w, pool_scale, w_out, ffn2_norm, ffn2_w_gate, ffn2_w_up, ffn2_w_down, final_norm, loss_target, m_ffn1_norm, m_ffn1_w_gate, m_ffn1_w_up, m_ffn1_w_down, m_mix_norm, m_w_in, m_sink_logits, m_pool_w, m_pool_scale, m_w_out, m_ffn2_norm, m_ffn2_w_gate, m_ffn2_w_up, m_ffn2_w_down, m_final_norm, v_ffn1_norm, v_ffn1_w_gate, v_ffn1_w_up, v_ffn1_w_down, v_mix_norm, v_w_in, v_sink_logits, v_pool_w, v_pool_scale, v_w_out, v_ffn2_norm, v_ffn2_w_gate, v_ffn2_w_up, v_ffn2_w_down, v_final_norm):
    s, d = x.shape[1], x.shape[2]
    fk = ffn1_w_gate.shape[2]
    f = N_DEV * fk
    ink = w_in.shape[2]
    n_in = N_DEV * ink
    mixk = w_out.shape[1]
    tm = min(512, s)
    tm_bwd = min(256, s)
    pos = jnp.stack([lax.axis_index("x"), lax.axis_index("y"), lax.axis_index("c")]).astype(jnp.int32)

    t_bf = lambda w: w[0].T.astype(BF16)
    full = lambda a: a.reshape(N_DEV * a.shape[1], d)
    first = [t_bf(ffn1_w_gate), t_bf(ffn1_w_up), ffn1_w_down[0].astype(BF16)]
    second = _AllGather([t_bf(w_in), w_out[0].astype(BF16), t_bf(ffn2_w_gate)])
    third = _AllGather([t_bf(ffn2_w_up), ffn2_w_down[0].astype(BF16)])

    tabs = _rope_tables(s)
    pband, ptband = _pool_tables(s)
    g1, gm, g2, gf = ffn1_norm, mix_norm, ffn2_norm, final_norm.reshape(1, d)

    x0 = x[0]
    h1, n1, gate1, up1, act1, wg1, wu1, wd1, *gathered = _ffn_fwd_gather(x0, g1, first, "ffn1_fwd", tm, carry=second)
    win_t, wout, wg2 = map(full, gathered)
    q, k, v, pc, n2 = _mix_in(h1, gm, win_t, tabs, "mix_in", tm)
    mix, *gathered = _attn_pool_fwd(q, k, v, pc, sink_logits, pool_w[0], pool_scale, pband, "attn_pool_fwd",
                                    carry=third)
    wu2, wd2 = map(full, gathered)
    h2 = _mix_out(h1, mix, wout, "mix_out", tm)
    h3, n3, gate2, up2, act2 = _ffn_fwd(h2, g2, wg2, wu2, wd2, "ffn2_fwd", tm)
    loss_part, dh3, dgf = _final(h3, gf, loss_target[0], "final", tm)

    tw = 256 if f % 256 == 0 else 128
    gw, sum1, recv2 = {}, {}, {}

    def stage1(keys, rows):
        for key in keys:
            sum1[key] = _rs_stage1_sum(pos, gw[key][0].reshape(N_DEV, rows, d), gw[key][1].reshape(N_DEV, rows, d),
                                       "rs1_" + key, local_order=key in ("g1", "u1", "d1"))

    def stage2(keys):
        return _RsStage2([sum1[key][1] for key in keys])

    dh2, dg2, dgate2, dup2, dhh3 = _ffn_bwd(h2, dh3, gate2, up2, g2, wg2, wu2, wd2, "ffn2_bwd", tm_bwd)
    gw["g2"] = _wgrad(dgate2, n3, "wgrad_gate2", tw)
    gw["u2"] = _wgrad(dup2, n3, "wgrad_up2", tw)
    gw["d2"] = _wgrad(act2, dhh3, "wgrad_down2", tw)
    stage1(["g2", "u2", "d2"], fk)
    dmix, dh2b = _mix_out_bwd(dh2, wout, "mix_out_bwd", tm)
    gw["out"] = _wgrad(mix, dh2b, "wgrad_out", tw)
    stage1(["out"], mixk)
    dq, dkp, dvp, dpc, dsink, dpw, dps, *r2 = _attn_pool_bwd(
        q, k, v, pc, dmix, sink_logits, pool_w[0], pool_scale, pband, ptband, "attn_pool_bwd",
        carry=stage2(["g2", "u2", "d2", "out"]))
    recv2.update(zip(["g2", "u2", "d2", "out"], r2))
    dh1, du, dgm = _mix_in_bwd(h1, dh2, gm, win_t, dq, dkp, dvp, dpc, tabs, "mix_in_bwd", tm)
    gw["in"] = _wgrad(du, n2, "wgrad_in", tw)
    stage1(["in"], ink)
    small_part = _pack_small(jnp.zeros_like(dgm), dgm, dg2, dgf, dsink[:, 0], dpw, dps, loss_part)
    dgate1, dup1, dhh1, small_all = _ffn_bwd_gates(dh1, gate1, up1, wd1, "ffn1_bwd_gates", tm,
                                                   carry=_AllGather([small_part]))
    *gw["g1"], recv2["in"] = _wgrad(dgate1, n1, "wgrad_gate1", tw, carry=stage2(["in"]))
    stage1(["g1"], fk)
    *gw["u1"], recv2["g1"] = _wgrad(dup1, n1, "wgrad_up1", tw, carry=stage2(["g1"]))
    stage1(["u1"], fk)
    *gw["d1"], recv2["u1"] = _wgrad(act1, dhh1, "wgrad_down1", tw, carry=stage2(["u1"]))
    stage1(["d1"], fk)
    dx, dg1, recv2["d1"] = _ffn_bwd_input(x0, dh1, dgate1, dup1, g1, wg1, wu1, "ffn1_bwd_input", tm,
                                          carry=stage2(["d1"]))

    (dg1_all,) = _run_exchange(_DirectGather([dg1.reshape(8, 128)]), "gather_norm1_grad")
    pk = lambda a, b, c_, e, s_, pw_, psc: _pack_small(a, b, c_, e, s_[0], pw_[0], psc)
    small_w = pk(ffn1_norm, mix_norm, ffn2_norm, final_norm, sink_logits, pool_w, pool_scale)
    small_m = pk(m_ffn1_norm, m_mix_norm, m_ffn2_norm, m_final_norm, m_sink_logits, m_pool_w, m_pool_scale)
    small_v = pk(v_ffn1_norm, v_mix_norm, v_ffn2_norm, v_final_norm, v_sink_logits, v_pool_w, v_pool_scale)
    sg, sd, sm, sv = [_unpack_small(a)
                      for a in _adam_small(small_w, small_all, dg1_all, small_m, small_v, "adam_small")]

    big = {}
    keys = ["g1", "u1", "d1", "g2", "u2", "d2", "in", "out"]
    names = ["ffn1_w_gate", "ffn1_w_up", "ffn1_w_down", "ffn2_w_gate", "ffn2_w_up", "ffn2_w_down", "w_in", "w_out"]
    transposed = [True, True, False, True, True, False, True, False]
    ws = [ffn1_w_gate, ffn1_w_up, ffn1_w_down, ffn2_w_gate, ffn2_w_up, ffn2_w_down, w_in, w_out]
    ms = [m_ffn1_w_gate, m_ffn1_w_up, m_ffn1_w_down, m_ffn2_w_gate, m_ffn2_w_up, m_ffn2_w_down, m_w_in, m_w_out]
    vs = [v_ffn1_w_gate, v_ffn1_w_up, v_ffn1_w_down, v_ffn2_w_gate, v_ffn2_w_up, v_ffn2_w_down, v_w_in, v_w_out]
    for key, nm, tr, w, m, vv in zip(keys, names, transposed, ws, ms, vs):
        view = (lambda a: jnp.swapaxes(a, 1, 2)[0]) if tr else (lambda a: a[0])
        back = (lambda a: jnp.swapaxes(a[None], 1, 2)) if tr else (lambda a: a[None])
        res = _rs_sum2_adam(pos, sum1[key][0], recv2[key], view(w), view(m), view(vv), "adam_" + nm)
        big[nm] = tuple(back(a) for a in res)

    loss = sg["loss"]
    all_names = ["ffn1_norm", "ffn1_w_gate", "ffn1_w_up", "ffn1_w_down", "mix_norm", "w_in", "sink_logits", "pool_w",
                 "pool_scale", "w_out", "ffn2_norm", "ffn2_w_gate", "ffn2_w_up", "ffn2_w_down", "final_norm"]
    outs = [loss, dx[None]]
    for idx, src in enumerate((sg, sd, sm, sv)):
        for nm in all_names:
            outs.append(big[nm][idx] if nm in big else src[nm])
    return tuple(outs)
```

```python
import functools

import jax
import jax.numpy as jnp
import numpy as np
from jax import lax
from jax.experimental import pallas as pl
from jax.experimental.pallas import tpu as pltpu

F32 = jnp.float32
BF16 = jnp.bfloat16
MESH = pl.DeviceIdType.MESH
N_DEV = 8

EPS = 1e-6
HEAD_DIM = 64
N_HEADS = 8
N_KV = 2
GROUP = N_HEADS // N_KV
ATTN_W = N_HEADS * HEAD_DIM
KV_W = N_KV * HEAD_DIM
POOL_W = 512
POOL_G = 4
POOL_GW = POOL_W // POOL_G
POOL_WINDOWS = (2, 4, 8, 16)
BLK = 128
ROT = 16
ROPE_THETA = 500000.0
SCORE_SCALE = HEAD_DIM ** -0.5

ADAM_LR, ADAM_B1, ADAM_B2, ADAM_EPS, ADAM_WD, ADAM_STEP = 0.001, 0.9, 0.999, 1e-08, 0.01, 10

VMEM_LIMIT = 56 * 1024 * 1024


def _cparams(sem=None, **kw):
    if sem is not None:
        kw["dimension_semantics"] = sem
    return pltpu.CompilerParams(vmem_limit_bytes=VMEM_LIMIT, **kw)


def _whole(shape):
    nd = len(shape)
    return pl.BlockSpec(shape, lambda *_: (0,) * nd, pipeline_mode=pl.Buffered(1))


def _sigmoid(z):
    return 1.0 / (1.0 + jnp.exp(-z))


def _dot_nt(a, b):
    return lax.dot_general(a, b, (((1,), (1,)), ((), ())), preferred_element_type=F32)


def _dot_nn(a, b):
    return lax.dot_general(a, b, (((1,), (0,)), ((), ())), preferred_element_type=F32)


def _dot_tn(a, b):
    return lax.dot_general(a, b, (((0,), (0,)), ((), ())), preferred_element_type=F32)


def _rope_tables(s):
    inv_freq = ROPE_THETA ** (-np.arange(0, ROT, 2, dtype=np.float64) / ROT)
    ang = np.arange(s, dtype=np.float64)[:, None] * inv_freq[None, :]
    c = np.ones((s, HEAD_DIM)); sa = np.zeros((s, HEAD_DIM)); sb = np.zeros((s, HEAD_DIM))
    c[:, :8] = np.cos(ang); c[:, 8:16] = np.cos(ang)
    sa[:, :8] = -np.sin(ang)
    sb[:, 8:16] = np.sin(ang)
    t = lambda a: jnp.asarray(np.tile(a, (1, 2)).astype(np.float32))
    return t(c), t(sa), t(sb)


def _pool_weight(gi, t, s_pos, s):
    half = POOL_WINDOWS[gi] // 2

    def win(lo, hi):
        a = np.clip(lo, 0, s); b = np.clip(hi + 1, 0, s)
        inside = (s_pos >= a) & (s_pos < b)
        return inside / np.maximum(b - a, 1)

    w = 0.5 * (win(t - half, t + half - 1) + win(t - half + 1, t + half)) - (t == s_pos)
    return w * ((t >= 0) & (t < s) & (s_pos >= 0) & (s_pos < s))


def _pool_tables(s):
    nb = s // BLK
    fwd = np.zeros((3, POOL_G, BLK, 3 * BLK), np.float32)
    bwd = np.zeros((3, POOL_G, BLK, 3 * BLK), np.float32)
    for vi, n in enumerate((0, 1 if nb > 2 else 0, nb - 1)):
        i = n * BLK + np.arange(BLK)[:, None]
        j = (n - 1) * BLK + np.arange(3 * BLK)[None, :]
        for gi in range(POOL_G):
            fwd[vi, gi] = _pool_weight(gi, i, j, s)
            bwd[vi, gi] = _pool_weight(gi, j, i, s)
    return jnp.asarray(fwd, dtype=BF16), jnp.asarray(bwd, dtype=BF16)


def _variant_index(n, nb):
    return jnp.where(n == 0, 0, jnp.where(n == nb - 1, 2, 1))


class _Exchange:
    inputs = ()
    out_shapes = ()
    sems = ()

    def start(self, srcs, outs, sems):
        raise NotImplementedError

    def finish(self, srcs, outs, sems):
        raise NotImplementedError


class _AllGather(_Exchange):
    def __init__(self, arrays):
        n = len(arrays)
        self.inputs = list(arrays)
        self.out_shapes = [jax.ShapeDtypeStruct((N_DEV,) + a.shape, a.dtype) for a in arrays]
        self.sems = [pltpu.SemaphoreType.DMA((n, 7)), pltpu.SemaphoreType.DMA((n, 7)), pltpu.SemaphoreType.DMA((n,))]

    def _parts(self, srcs, outs, sems):
        send_sems, recv_sems, local_sems = sems
        n = len(srcs)
        x, y, c = lax.axis_index("x"), lax.axis_index("y"), lax.axis_index("c")
        me, sibling = (x, y, c), (x, y, 1 - c)
        chips = [(1 - x, y), (x, 1 - y), (1 - x, 1 - y)]

        def slot(a, dev):
            return outs[a].at[4 * dev[0] + 2 * dev[1] + dev[2]]

        def copy(a, k, block, to, src=None):
            return pltpu.make_async_remote_copy(
                src_ref=slot(a, block) if src is None else src, dst_ref=slot(a, block),
                send_sem=send_sems.at[a, k], recv_sem=recv_sems.at[a, k], device_id=to, device_id_type=MESH)

        mine = [pltpu.make_async_copy(srcs[a], slot(a, me), local_sems.at[a]) for a in range(n)]
        first = []
        for a in range(n):
            first.append(copy(a, 0, me, sibling, src=srcs[a]))
            first += [copy(a, 1 + j, me, (*chip, c), src=srcs[a]) for j, chip in enumerate(chips)]
        return n, c, me, sibling, chips, copy, mine, first

    def start(self, srcs, outs, sems):
        _, _, _, _, _, _, mine, first = self._parts(srcs, outs, sems)
        for cp in mine + first:
            cp.start()

    def finish(self, srcs, outs, sems):
        n, c, me, sibling, chips, copy, mine, first = self._parts(srcs, outs, sems)
        passed = []
        for j, chip in enumerate(chips):
            for a in range(n):
                copy(a, 1 + j, (*chip, c), me).wait_recv()
                fwd = copy(a, 4 + j, (*chip, c), sibling)
                fwd.start()
                passed.append(fwd)
        for a in range(n):
            copy(a, 0, sibling, me).wait_recv()
            for j, chip in enumerate(chips):
                copy(a, 4 + j, (*chip, 1 - c), me).wait_recv()
        for cp in first + passed:
            cp.wait_send()
        for cp in mine:
            cp.wait()


class _RsStage2(_Exchange):
    def start(self, srcs, outs, sems):
        for cp in self._copies(srcs, outs, sems):
            cp.start()

    def finish(self, srcs, outs, sems):
        copies = self._copies(srcs, outs, sems)
        for cp in copies:
            cp.wait_recv()
        for cp in copies:
            cp.wait_send()


    def __init__(self, pbs):
        n = len(pbs)
        self.inputs = list(pbs)
        self.out_shapes = [jax.ShapeDtypeStruct((3,) + p.shape[1:], p.dtype) for p in pbs]
        self.sems = [pltpu.SemaphoreType.DMA((n, 3)), pltpu.SemaphoreType.DMA((n, 3))]

    def _copies(self, srcs, outs, sems):
        send_sems, recv_sems = sems
        x, y, c = lax.axis_index("x"), lax.axis_index("y"), lax.axis_index("c")
        chips = [(1 - x, y), (x, 1 - y), (1 - x, 1 - y)]
        return [pltpu.make_async_remote_copy(
            src_ref=srcs[a].at[2 * chip[0] + chip[1]], dst_ref=outs[a].at[j], send_sem=send_sems.at[a, j],
            recv_sem=recv_sems.at[a, j], device_id=(*chip, c), device_id_type=MESH)
            for a in range(len(srcs)) for j, chip in enumerate(chips)]


class _DirectGather(_Exchange):
    def __init__(self, arrays):
        n = len(arrays)
        self.inputs = list(arrays)
        self.out_shapes = [jax.ShapeDtypeStruct((N_DEV,) + a.shape, a.dtype) for a in arrays]
        self.sems = [pltpu.SemaphoreType.DMA((n, 7)), pltpu.SemaphoreType.DMA((n, 7)), pltpu.SemaphoreType.DMA((n,))]

    def _copies(self, srcs, outs, sems):
        send_sems, recv_sems, local_sems = sems
        x, y, c = lax.axis_index("x"), lax.axis_index("y"), lax.axis_index("c")
        me = 4 * x + 2 * y + c
        remote, local = [], []
        for a in range(len(srcs)):
            local.append(pltpu.make_async_copy(srcs[a], outs[a].at[me], local_sems.at[a]))
            for k in range(1, N_DEV):
                peer = (x ^ (k >> 2), y ^ ((k >> 1) & 1), c ^ (k & 1))
                remote.append(pltpu.make_async_remote_copy(
                    src_ref=srcs[a], dst_ref=outs[a].at[me], send_sem=send_sems.at[a, k - 1],
                    recv_sem=recv_sems.at[a, k - 1], device_id=peer, device_id_type=MESH))
        return remote, local

    def start(self, srcs, outs, sems):
        remote, local = self._copies(srcs, outs, sems)
        for cp in local + remote:
            cp.start()

    def finish(self, srcs, outs, sems):
        remote, local = self._copies(srcs, outs, sems)
        for cp in remote:
            cp.wait_recv()
        for cp in remote:
            cp.wait_send()
        for cp in local:
            cp.wait()


class _Both(_Exchange):
    def __init__(self, a, b):
        self.a, self.b = a, b
        self.inputs = list(a.inputs) + list(b.inputs)
        self.out_shapes = list(a.out_shapes) + list(b.out_shapes)
        self.sems = list(a.sems) + list(b.sems)

    def _split(self, srcs, outs, sems):
        na, oa, sa = len(self.a.inputs), len(self.a.out_shapes), len(self.a.sems)
        return (srcs[:na], outs[:oa], sems[:sa]), (srcs[na:], outs[oa:], sems[sa:])

    def start(self, srcs, outs, sems):
        pa, pb = self._split(srcs, outs, sems)
        self.a.start(*pa)
        self.b.start(*pb)

    def finish(self, srcs, outs, sems):
        pa, pb = self._split(srcs, outs, sems)
        self.a.finish(*pa)
        self.b.finish(*pb)


_ANY = pl.BlockSpec(memory_space=pl.ANY)


def _run_exchange(ex, name):
    n_in, n_out = len(ex.inputs), len(ex.out_shapes)

    def body(*refs):
        srcs, outs, sems = refs[:n_in], refs[n_in:n_in + n_out], refs[n_in + n_out:]
        ex.start(srcs, outs, sems)
        ex.finish(srcs, outs, sems)

    return pl.pallas_call(
        body, name=name, out_shape=list(ex.out_shapes), in_specs=[_ANY] * n_in, out_specs=[_ANY] * n_out,
        scratch_shapes=list(ex.sems),
    )(*ex.inputs)


def _call(body, name, grid, in_specs, out_specs, out_shape, args, sem, carry=None, **kw):
    if carry is None:
        return pl.pallas_call(functools.partial(body), name=name, grid=grid, in_specs=in_specs, out_specs=out_specs,
                              out_shape=out_shape, compiler_params=_cparams(sem), **kw)(*args)
    n_in, n_out = len(in_specs), len(out_specs)
    nc_in, nc_out = len(carry.inputs), len(carry.out_shapes)

    def carried(*refs):
        ins = refs[:n_in]
        c_in = refs[n_in:n_in + nc_in]
        outs = refs[n_in + nc_in:n_in + nc_in + n_out]
        c_out = refs[n_in + nc_in + n_out:n_in + nc_in + n_out + nc_out]
        sems = refs[n_in + nc_in + n_out + nc_out:]
        ids = [pl.program_id(i) for i in range(len(grid))]
        is_first = functools.reduce(jnp.logical_and, [i == 0 for i in ids])
        is_last = functools.reduce(jnp.logical_and, [i == g - 1 for i, g in zip(ids, grid)])

        @pl.when(is_first)
        def _():
            carry.start(c_in, c_out, sems)

        body(*ins, *outs)

        @pl.when(is_last)
        def _():
            carry.finish(c_in, c_out, sems)

    return pl.pallas_call(
        carried, name=name, grid=grid, in_specs=list(in_specs) + [_ANY] * nc_in,
        out_specs=list(out_specs) + [_ANY] * nc_out, out_shape=list(out_shape) + list(carry.out_shapes),
        scratch_shapes=list(carry.sems), compiler_params=_cparams(sem), **kw)(*args, *carry.inputs)


def _rs_stage1_sum(pos, g, gb, name, local_order=False):
    _, rows, d = g.shape

    def block(px, py, pc, q, other_core):
        if local_order:
            return _local_block(px, py, q // 2, q % 2, 1 if other_core else 0)
        return 4 * (q // 2) + 2 * (q % 2) + ((1 - pc) if other_core else pc)

    def body(pos_ref, g_ref, gb_ref, p_ref, pb_ref, land, send_sems, recv_sems):
        q = pl.program_id(0)
        x, y, c = lax.axis_index("x"), lax.axis_index("y"), lax.axis_index("c")

        def copy(k):
            return pltpu.make_async_remote_copy(
                src_ref=gb_ref.at[block(x, y, c, k, True)], dst_ref=land.at[k],
                send_sem=send_sems.at[k], recv_sem=recv_sems.at[k], device_id=(x, y, 1 - c), device_id_type=MESH)

        @pl.when(q == 0)
        def _():
            for k in range(4):
                copy(k).start()

        copy(q).wait_recv()
        p = g_ref[...] + land[q].astype(F32)
        p_ref[...] = p
        pb_ref[...] = p.astype(BF16)

        @pl.when(q == 3)
        def _():
            for k in range(4):
                copy(k).wait_send()

    blk = lambda q, pos_ref: (q, 0, 0)
    return pl.pallas_call(
        body, name=name,
        grid_spec=pltpu.PrefetchScalarGridSpec(
            num_scalar_prefetch=1, grid=(4,),
            in_specs=[pl.BlockSpec((None, rows, d),
                                   lambda q, pos_ref: (block(pos_ref[0], pos_ref[1], pos_ref[2], q, False), 0, 0)),
                      _ANY],
            out_specs=[pl.BlockSpec((None, rows, d), blk), pl.BlockSpec((None, rows, d), blk)],
            scratch_shapes=[pltpu.VMEM((4, rows, d), BF16), pltpu.SemaphoreType.DMA((4,)),
                            pltpu.SemaphoreType.DMA((4,))]),
        out_shape=[jax.ShapeDtypeStruct((4, rows, d), F32), jax.ShapeDtypeStruct((4, rows, d), BF16)],
        compiler_params=_cparams(("arbitrary",)),
    )(pos, g, gb)


def _rs_sum2_adam(pos, p, r2, w, m, v, name):
    _, rows, d = p.shape
    tr = rows // 2 if rows % 16 == 0 else rows

    def body(pos_ref, p_ref, r_ref, w_ref, m_ref, v_ref, g_ref, d_ref, nm_ref, nv_ref):
        r = r_ref[...].astype(F32)
        g = ((p_ref[...] + r[0]) + r[1]) + r[2]
        g_ref[...] = g
        d_ref[...], nm_ref[...], nv_ref[...] = _adam_math(w_ref[...], g, m_ref[...], v_ref[...])

    blk = pl.BlockSpec((tr, d), lambda i, pos_ref: (i, 0))
    return pl.pallas_call(
        body, name=name,
        grid_spec=pltpu.PrefetchScalarGridSpec(
            num_scalar_prefetch=1, grid=(rows // tr,),
            in_specs=[pl.BlockSpec((None, tr, d), lambda i, pos_ref: (2 * pos_ref[0] + pos_ref[1], i, 0)),
                      pl.BlockSpec((3, tr, d), lambda i, pos_ref: (0, i, 0)), blk, blk, blk],
            out_specs=[blk] * 4),
        out_shape=[jax.ShapeDtypeStruct((rows, d), F32)] * 4,
        compiler_params=_cparams(("arbitrary",)),
    )(pos, p, r2, w, m, v)


def _ffn_chunk(f):
    for cand in (256, 128):
        if f % cand == 0:
            return cand
    return f


def _ffn_fwd(h, g, wg_t, wu_t, wd, name, tm, carry=None):
    s, d = h.shape
    f = wd.shape[0]
    tf = _ffn_chunk(f)

    def body(h_ref, g_ref, wg_ref, wu_ref, wd_ref, o_ref, n_ref, gate_ref, up_ref, act_ref):
        x = h_ref[...]
        r = lax.rsqrt(jnp.mean(x * x, axis=-1, keepdims=True) + EPS)
        nb = (x * r * g_ref[...]).astype(BF16)
        n_ref[...] = nb
        for j in range(f // tf):
            sl = slice(j * tf, (j + 1) * tf)
            gate = _dot_nt(nb, wg_ref[sl, :])
            up = _dot_nt(nb, wu_ref[sl, :])
            gate_ref[:, sl] = gate.astype(BF16)
            up_ref[:, sl] = up.astype(BF16)
            act_ref[:, sl] = (gate * _sigmoid(gate) * up).astype(BF16)
        o_ref[...] = x + 0.5 * _dot_nn(act_ref[...], wd_ref[...])

    row = lambda w: pl.BlockSpec((tm, w), lambda i: (i, 0))
    return _call(
        body, name, (s // tm,),
        [row(d), _whole((1, d)), _whole((f, d)), _whole((f, d)), _whole((f, d))],
        [row(d), row(d), row(f), row(f), row(f)],
        [jax.ShapeDtypeStruct((s, d), F32), jax.ShapeDtypeStruct((s, d), BF16),
         jax.ShapeDtypeStruct((s, f), BF16), jax.ShapeDtypeStruct((s, f), BF16), jax.ShapeDtypeStruct((s, f), BF16)],
        (h, g, wg_t, wu_t, wd), ("arbitrary",), carry)


def _local_block(pos_x, pos_y, dev_x, dev_y, dev_c_differs):
    return 4 * (dev_x ^ pos_x) + 2 * (dev_y ^ pos_y) + dev_c_differs


def _ffn_fwd_gather(h, g, shards, name, tm, carry=None):
    s, d = h.shape
    fk = shards[0].shape[0]
    f = N_DEV * fk
    fh = f // 2
    nt = s // tm
    chunks, off = [], 0
    while off < fh:
        size = 256 if fh - off >= 256 else fh - off
        chunks.append((off, size))
        off += size
    n_carry_in = 0 if carry is None else len(carry.inputs)
    n_carry_out = 0 if carry is None else len(carry.out_shapes)

    def body(h_ref, g_ref, *refs):
        loc = refs[:3]
        c_in = refs[3:3 + n_carry_in]
        o_ref, n_ref, gate_ref, up_ref, act_ref = refs[3 + n_carry_in:8 + n_carry_in]
        full = refs[8 + n_carry_in:11 + n_carry_in]
        c_out = refs[11 + n_carry_in:11 + n_carry_in + n_carry_out]
        wbuf, acc, send_sems, recv_sems, local_sems, load_sems = refs[11 + n_carry_in + n_carry_out:][:6]
        c_sems = refs[17 + n_carry_in + n_carry_out:]
        half, i = pl.program_id(0), pl.program_id(1)
        x, y, c = lax.axis_index("x"), lax.axis_index("y"), lax.axis_index("c")

        def slot(a, k):
            return full[a].at[pl.ds(k * fk, fk), :]

        def send(a, k, src):
            peer = (x ^ (k >> 2), y ^ ((k >> 1) & 1), c ^ (k & 1))
            return pltpu.make_async_remote_copy(src_ref=src, dst_ref=slot(a, k), send_sem=send_sems.at[a, k - 1],
                                                recv_sem=recv_sems.at[a, k - 1], device_id=peer, device_id_type=MESH)

        own = [send(a, k, loc[a]) for k in (1, 2, 4, 6) for a in range(3)]
        passed = {k: [pltpu.make_async_remote_copy(
            src_ref=slot(a, k), dst_ref=slot(a, k + 1), send_sem=send_sems.at[a, k], recv_sem=recv_sems.at[a, k],
            device_id=(x, y, 1 - c), device_id_type=MESH) for a in range(3)] for k in (2, 4, 6)}
        mine = [pltpu.make_async_copy(loc[a], slot(a, 0), local_sems.at[a]) for a in range(3)]

        def arrived(k):
            for a in range(3):
                send(a, k, loc[a]).wait_recv()

        def load(hf):
            cps = [pltpu.make_async_copy(full[a].at[pl.ds(hf * fh, fh), :], wbuf.at[a], load_sems.at[a])
                   for a in range(3)]
            for cp in cps:
                cp.start()
            for cp in cps:
                cp.wait()

        @pl.when((half == 0) & (i == 0))
        def _():
            for cp in mine + own:
                cp.start()
            for cp in mine:
                cp.wait()
            arrived(1)
            arrived(2)
            for cp in passed[2]:
                cp.start()
            arrived(3)
            if carry is not None:
                carry.start(c_in, c_out, c_sems)
            load(0)

        @pl.when((half == 1) & (i == 0))
        def _():
            for k in (4, 6):
                arrived(k)
                for cp in passed[k]:
                    cp.start()
            arrived(5)
            arrived(7)
            load(1)

        xv = h_ref[...]
        r = lax.rsqrt(jnp.mean(xv * xv, axis=-1, keepdims=True) + EPS)
        nb = (xv * r * g_ref[...]).astype(BF16)
        rows = pl.ds(pl.multiple_of(i * tm, tm), tm)

        @pl.when(half == 0)
        def _():
            acc[rows, :] = jnp.zeros((tm, d), F32)

        for off, size in chunks:
            sl = slice(off, off + size)
            gate = _dot_nt(nb, wbuf[0, sl, :])
            up = _dot_nt(nb, wbuf[1, sl, :])
            gate_ref[:, sl] = gate.astype(BF16)
            up_ref[:, sl] = up.astype(BF16)
            act_ref[:, sl] = (gate * _sigmoid(gate) * up).astype(BF16)
        part = _dot_nn(act_ref[...], wbuf[2])
        total = acc[rows, :] + part
        acc[rows, :] = total
        o_ref[...] = xv + 0.5 * total

        @pl.when(half == 0)
        def _():
            n_ref[...] = nb

        @pl.when((half == 1) & (i == nt - 1))
        def _():
            for cp in own + passed[2] + passed[4] + passed[6]:
                cp.wait_send()
            if carry is not None:
                carry.finish(c_in, c_out, c_sems)

    row = pl.BlockSpec((tm, d), lambda hf, i: (i, 0))
    cols = pl.BlockSpec((tm, fh), lambda hf, i: (i, hf))
    outs = pl.pallas_call(
        body, name=name, grid=(2, nt),
        in_specs=[row, _whole((1, d))] + [_ANY] * (3 + n_carry_in),
        out_specs=[pl.BlockSpec((tm, d), lambda hf, i: (i * hf, 0)),
                   pl.BlockSpec((tm, d), lambda hf, i: (i * (1 - hf) + (nt - 1) * hf, 0)), cols, cols, cols]
        + [_ANY] * (3 + n_carry_out),
        out_shape=[jax.ShapeDtypeStruct((s, d), F32), jax.ShapeDtypeStruct((s, d), BF16)]
        + [jax.ShapeDtypeStruct((s, f), BF16)] * 3 + [jax.ShapeDtypeStruct((f, d), BF16)] * 3
        + ([] if carry is None else list(carry.out_shapes)),
        scratch_shapes=[pltpu.VMEM((3, fh, d), BF16), pltpu.VMEM((s, d), F32),
                        pltpu.SemaphoreType.DMA((3, 7)), pltpu.SemaphoreType.DMA((3, 7)),
                        pltpu.SemaphoreType.DMA((3,)), pltpu.SemaphoreType.DMA((3,))]
        + ([] if carry is None else list(carry.sems)),
        compiler_params=_cparams(("arbitrary", "arbitrary")),
    )(h, g, *shards, *([] if carry is None else carry.inputs))
    return outs


def _gate_grads(dh_ref, gate_ref, up_ref, wd_ref, dgate_ref, dup_ref, dhh_ref, tf):
    dhh = (0.5 * dh_ref[...]).astype(BF16)
    dhh_ref[...] = dhh
    for j in range(gate_ref.shape[1] // tf):
        sl = slice(j * tf, (j + 1) * tf)
        gt = gate_ref[:, sl].astype(F32)
        u = up_ref[:, sl].astype(F32)
        dact = _dot_nt(dhh, wd_ref[sl, :])
        sg = _sigmoid(gt)
        dup_ref[:, sl] = (dact * (gt * sg)).astype(BF16)
        dgate_ref[:, sl] = (dact * u * (sg * (1.0 + gt * (1.0 - sg)))).astype(BF16)


def _input_grad(h_ref, dh_ref, dgate_ref, dup_ref, g_ref, wg_ref, wu_ref, o_ref, dg_ref):
    x = h_ref[...]
    r = lax.rsqrt(jnp.mean(x * x, axis=-1, keepdims=True) + EPS)
    xhat = x * r
    dn = _dot_nn(dgate_ref[...], wg_ref[...]) + _dot_nn(dup_ref[...], wu_ref[...])
    dxhat = dn * g_ref[...]
    o_ref[...] = dh_ref[...] + r * (dxhat - xhat * jnp.mean(dxhat * xhat, axis=-1, keepdims=True))

    @pl.when(pl.program_id(0) == 0)
    def _():
        dg_ref[...] = jnp.zeros_like(dg_ref)

    dg_ref[...] += jnp.sum(dn * xhat, axis=0, keepdims=True)


def _ffn_bwd(h_in, dh_out, gate, up, g, wg_t, wu_t, wd, name, tm):
    s, d = h_in.shape
    f = gate.shape[1]
    tf = _ffn_chunk(f)

    def body(h_ref, dh_ref, gate_ref, up_ref, g_ref, wg_ref, wu_ref, wd_ref,
             o_ref, dg_ref, dgate_ref, dup_ref, dhh_ref):
        _gate_grads(dh_ref, gate_ref, up_ref, wd_ref, dgate_ref, dup_ref, dhh_ref, tf)
        _input_grad(h_ref, dh_ref, dgate_ref, dup_ref, g_ref, wg_ref, wu_ref, o_ref, dg_ref)

    row = lambda w: pl.BlockSpec((tm, w), lambda i: (i, 0))
    return pl.pallas_call(
        body, name=name, grid=(s // tm,),
        in_specs=[row(d), row(d), row(f), row(f), _whole((1, d)), _whole((f, d)), _whole((f, d)), _whole((f, d))],
        out_specs=[row(d), pl.BlockSpec((1, d), lambda i: (0, 0)), row(f), row(f), row(d)],
        out_shape=[jax.ShapeDtypeStruct((s, d), F32), jax.ShapeDtypeStruct((1, d), F32),
                   jax.ShapeDtypeStruct((s, f), BF16), jax.ShapeDtypeStruct((s, f), BF16),
                   jax.ShapeDtypeStruct((s, d), BF16)],
        compiler_params=_cparams(("arbitrary",)),
    )(h_in, dh_out, gate, up, g, wg_t, wu_t, wd)


def _ffn_bwd_gates(dh_out, gate, up, wd, name, tm, carry=None):
    s, d = dh_out.shape
    f = gate.shape[1]
    tf = _ffn_chunk(f)

    def body(dh_ref, gate_ref, up_ref, wd_ref, dgate_ref, dup_ref, dhh_ref):
        _gate_grads(dh_ref, gate_ref, up_ref, wd_ref, dgate_ref, dup_ref, dhh_ref, tf)

    row = lambda w: pl.BlockSpec((tm, w), lambda i: (i, 0))
    return _call(
        body, name, (s // tm,), [row(d), row(f), row(f), _whole((f, d))], [row(f), row(f), row(d)],
        [jax.ShapeDtypeStruct((s, f), BF16), jax.ShapeDtypeStruct((s, f), BF16), jax.ShapeDtypeStruct((s, d), BF16)],
        (dh_out, gate, up, wd), ("arbitrary",), carry)


def _ffn_bwd_input(h_in, dh_out, dgate, dup, g, wg_t, wu_t, name, tm, carry=None):
    s, d = h_in.shape
    f = dgate.shape[1]

    row = lambda w: pl.BlockSpec((tm, w), lambda i: (i, 0))
    return _call(
        _input_grad, name, (s // tm,),
        [row(d), row(d), row(f), row(f), _whole((1, d)), _whole((f, d)), _whole((f, d))],
        [row(d), pl.BlockSpec((1, d), lambda i: (0, 0))],
        [jax.ShapeDtypeStruct((s, d), F32), jax.ShapeDtypeStruct((1, d), F32)],
        (h_in, dh_out, dgate, dup, g, wg_t, wu_t), ("arbitrary",), carry)


def _wgrad(a, b, name, tf, carry=None):
    s, f = a.shape
    d = b.shape[1]

    def body(a_ref, b_ref, o_ref, ob_ref):
        acc = _dot_tn(a_ref[...], b_ref[...])
        o_ref[...] = acc
        ob_ref[...] = acc.astype(BF16)

    return _call(
        body, name, (f // tf,),
        [pl.BlockSpec((s, tf), lambda i: (0, i)), _whole((s, d))],
        [pl.BlockSpec((tf, d), lambda i: (i, 0)), pl.BlockSpec((tf, d), lambda i: (i, 0))],
        [jax.ShapeDtypeStruct((f, d), F32), jax.ShapeDtypeStruct((f, d), BF16)],
        (a, b), ("arbitrary",), carry)


def _rope(t, c, sa, sb, reps):
    c, sa, sb = (jnp.tile(v, (1, reps)) if reps > 1 else v for v in (c, sa, sb))
    w = t.shape[1]
    return t * c + pltpu.roll(t, w - 8, 1) * sa + pltpu.roll(t, 8, 1) * sb


def _rope_bwd(dt, c, sa, sb, reps):
    c, sa, sb = (jnp.tile(v, (1, reps)) if reps > 1 else v for v in (c, sa, sb))
    w = dt.shape[1]
    return dt * c + pltpu.roll(dt * sa, 8, 1) + pltpu.roll(dt * sb, w - 8, 1)


def _mix_in(h, g, win_t, tabs, name, tm):
    s, d = h.shape
    n_in = win_t.shape[0]

    def body(h_ref, g_ref, w_ref, c_ref, sa_ref, sb_ref, q_ref, k_ref, v_ref, pc_ref, n_ref):
        x = h_ref[...]
        r = lax.rsqrt(jnp.mean(x * x, axis=-1, keepdims=True) + EPS)
        nb = (x * r * g_ref[...]).astype(BF16)
        n_ref[...] = nb
        u = _dot_nt(nb, w_ref[...])
        c, sa, sb = c_ref[...], sa_ref[...], sb_ref[...]
        q_ref[...] = _rope(u[:, :ATTN_W], c, sa, sb, ATTN_W // 128).astype(BF16)
        k_ref[...] = _rope(u[:, ATTN_W:ATTN_W + KV_W], c, sa, sb, 1).astype(BF16)
        v_ref[...] = u[:, ATTN_W + KV_W:ATTN_W + 2 * KV_W].astype(BF16)
        pc_ref[...] = u[:, ATTN_W + 2 * KV_W:]

    row = lambda w: pl.BlockSpec((tm, w), lambda i: (i, 0))
    return pl.pallas_call(
        body, name=name, grid=(s // tm,),
        in_specs=[row(d), _whole((1, d)), _whole((n_in, d)), row(128), row(128), row(128)],
        out_specs=[row(ATTN_W), row(KV_W), row(KV_W), row(POOL_W), row(d)],
        out_shape=[jax.ShapeDtypeStruct((s, ATTN_W), BF16), jax.ShapeDtypeStruct((s, KV_W), BF16),
                   jax.ShapeDtypeStruct((s, KV_W), BF16), jax.ShapeDtypeStruct((s, POOL_W), F32),
                   jax.ShapeDtypeStruct((s, d), BF16)],
        compiler_params=_cparams(("arbitrary",)),
    )(h, g, win_t, *tabs)


def _band_mask(n, nb, transposed):
    shape = (3 * BLK, 2 * BLK) if transposed else (2 * BLK, 3 * BLK)
    i = lax.broadcasted_iota(jnp.int32, shape, 1 if transposed else 0) % BLK
    j = lax.broadcasted_iota(jnp.int32, shape, 0 if transposed else 1)
    kpos = (n - 1) * BLK + j
    return (j >= i) & (j <= i + 2 * BLK) & (kpos >= 0) & (kpos < nb * BLK)


def _block_diag(t, kh):
    tf = t.astype(F32)
    tr = pltpu.roll(tf, HEAD_DIM, 1)
    lo = lax.broadcasted_iota(jnp.int32, tf.shape, 1) < HEAD_DIM
    top, bot = (tf, tr) if kh == 0 else (tr, tf)
    return jnp.concatenate([jnp.where(lo, top, 0.0), jnp.where(lo, 0.0, bot)], axis=0).astype(BF16)


def _fold_diag(tbd):
    lo = lax.broadcasted_iota(jnp.int32, (3 * BLK, 2 * HEAD_DIM), 1) < HEAD_DIM
    t = jnp.where(lo, tbd[:3 * BLK], tbd[3 * BLK:])
    return t + pltpu.roll(t, HEAD_DIM, 1)


def _stack_pairs(x, kh):
    return jnp.concatenate([x[:, (2 * kh) * 128:(2 * kh + 1) * 128], x[:, (2 * kh + 1) * 128:(2 * kh + 2) * 128]], axis=0)


def _sink_of(sink_ref, kh, half, axis):
    shape = (2 * BLK, 1) if axis == 0 else (1, 2 * BLK)
    first = lax.broadcasted_iota(jnp.int32, shape, axis) < BLK
    return jnp.where(first, sink_ref[0, GROUP * kh + half], sink_ref[0, GROUP * kh + 2 + half])


def _softmax_sink(sc, valid, sink, axis):
    sc = jnp.where(valid, sc, -1e30)
    m = jnp.maximum(jnp.max(sc, axis=axis, keepdims=True), sink)
    e = jnp.exp(sc - m)
    es = jnp.exp(sink - m)
    inv = 1.0 / (jnp.sum(e, axis=axis, keepdims=True) + es)
    return e * inv, es * inv


def _attn_pool_fwd(q, k, v, pc, sink, pool_w, pool_scale, pband, name, carry=None):
    s = q.shape[0]
    nb = s // BLK

    def body(sink_ref, q_ref, k0, k1, k2, v0, v1, v2, p0, p1, p2, pw_ref, ps_ref, pb_ref, o_ref):
        n = pl.program_id(0)
        valid = _band_mask(n, nb, False)
        kb = jnp.concatenate([k0[...], k1[...], k2[...]], axis=0)
        vb = jnp.concatenate([v0[...], v1[...], v2[...]], axis=0)
        qs = q_ref[...] * SCORE_SCALE
        for kh in range(N_KV):
            sc = _dot_nt(_stack_pairs(qs, kh), _block_diag(kb, kh))
            p = [_softmax_sink(sc[:, half * 3 * BLK:(half + 1) * 3 * BLK], valid, _sink_of(sink_ref, kh, half, 0), 1)[0]
                 for half in range(2)]
            o2 = _dot_nn(jnp.concatenate(p, axis=1).astype(BF16), _block_diag(vb, kh)).astype(BF16)
            o_ref[:, (2 * kh) * 128:(2 * kh + 1) * 128] = o2[:BLK]
            o_ref[:, (2 * kh + 1) * 128:(2 * kh + 2) * 128] = o2[BLK:]
        ext = jnp.concatenate([p0[...], p1[...], p2[...]], axis=0).astype(BF16)
        for gi in range(POOL_G):
            gsl = slice(gi * POOL_GW, (gi + 1) * POOL_GW)
            dg = _dot_nn(pb_ref[gi], ext[:, gsl])
            yg = _dot_nn(dg.astype(BF16), pw_ref[gi].astype(BF16))
            o_ref[:, ATTN_W + gi * POOL_GW:ATTN_W + (gi + 1) * POOL_GW] = (yg * ps_ref[:, gsl]).astype(BF16)

    def band(w):
        return [pl.BlockSpec((BLK, w), lambda n: (jnp.maximum(n - 1, 0), 0)),
                pl.BlockSpec((BLK, w), lambda n: (n, 0)),
                pl.BlockSpec((BLK, w), lambda n: (jnp.minimum(n + 1, nb - 1), 0))]

    return _call(
        body, name, (nb,),
        [pl.BlockSpec(memory_space=pltpu.SMEM), pl.BlockSpec((BLK, ATTN_W), lambda n: (n, 0)),
         *band(KV_W), *band(KV_W), *band(POOL_W),
         _whole((POOL_G, POOL_GW, POOL_GW)), _whole((1, POOL_W)),
         pl.BlockSpec((None, POOL_G, BLK, 3 * BLK), lambda n: (_variant_index(n, nb), 0, 0, 0))],
        [pl.BlockSpec((BLK, ATTN_W + POOL_W), lambda n: (n, 0))],
        [jax.ShapeDtypeStruct((s, ATTN_W + POOL_W), BF16)],
        (sink, q, k, k, k, v, v, v, pc, pc, pc, pool_w, pool_scale, pband), ("arbitrary",), carry)


def _attn_pool_bwd(q, k, v, pc, dmix, sink, pool_w, pool_scale, pband, ptband, name, carry=None):
    s = q.shape[0]
    nb = s // BLK

    def body(sink_ref, q_ref, k0, k1, k2, v0, v1, v2, p0, p1, p2, da_ref, d0, d1, d2, pw_ref, ps_ref, pb_ref, ptb_ref,
             dq_ref, dk_ref, dv_ref, dpc_ref, dsink_ref, dpw_ref, dps_ref):
        n = pl.program_id(0)

        @pl.when(n == 0)
        def _():
            dsink_ref[...] = jnp.zeros_like(dsink_ref)
            dpw_ref[...] = jnp.zeros_like(dpw_ref)
            dps_ref[...] = jnp.zeros_like(dps_ref)

        valid = _band_mask(n, nb, True)
        kb = jnp.concatenate([k0[...], k1[...], k2[...]], axis=0)
        vb = jnp.concatenate([v0[...], v1[...], v2[...]], axis=0)
        qb = q_ref[...]
        qs = qb * SCORE_SCALE
        da = da_ref[...].astype(BF16)
        dk_fold, dv_fold = [], []
        for kh in range(N_KV):
            kbd, vbd = _block_diag(kb, kh), _block_diag(vb, kh)
            q2, do2 = _stack_pairs(qb, kh), _stack_pairs(da, kh)
            sc_t = _dot_nt(kbd, _stack_pairs(qs, kh))
            dp_t = _dot_nt(vbd, do2)
            p_t, ds_t = [], []
            for half in range(2):
                rows = slice(half * 3 * BLK, (half + 1) * 3 * BLK)
                p, ps = _softmax_sink(sc_t[rows], valid, _sink_of(sink_ref, kh, half, 1), 0)
                delta = jnp.sum(p * dp_t[rows], axis=0, keepdims=True)
                p_t.append(p.astype(BF16))
                ds_t.append((p * (dp_t[rows] - delta)).astype(BF16))
                dsk = -ps * delta
                for pair in range(2):
                    h = GROUP * kh + 2 * pair + half
                    part = jnp.sum(dsk[:, pair * BLK:(pair + 1) * BLK], axis=1, keepdims=True)
                    dsink_ref[h:h + 1, :] += jnp.broadcast_to(part, (1, 128))
            p_t = jnp.concatenate(p_t, axis=0)
            ds_t = jnp.concatenate(ds_t, axis=0)
            dq2 = _dot_tn(ds_t, kbd) * SCORE_SCALE
            dq_ref[:, (2 * kh) * 128:(2 * kh + 1) * 128] = dq2[:BLK]
            dq_ref[:, (2 * kh + 1) * 128:(2 * kh + 2) * 128] = dq2[BLK:]
            dk_fold.append(_fold_diag(_dot_nn(ds_t, q2)) * SCORE_SCALE)
            dv_fold.append(_fold_diag(_dot_nn(p_t, do2)))
        lo = lax.broadcasted_iota(jnp.int32, (3 * BLK, KV_W), 1) < HEAD_DIM
        dk_all = jnp.where(lo, dk_fold[0], dk_fold[1])
        dv_all = jnp.where(lo, dv_fold[0], dv_fold[1])
        for t in range(3):
            dk_ref[t] = dk_all[t * BLK:(t + 1) * BLK]
            dv_ref[t] = dv_all[t * BLK:(t + 1) * BLK]
        ext = jnp.concatenate([p0[...], p1[...], p2[...]], axis=0).astype(BF16)
        dpe = jnp.concatenate([d0[...], d1[...], d2[...]], axis=0)
        dpc_cur = d1[...]
        for gi in range(POOL_G):
            gsl = slice(gi * POOL_GW, (gi + 1) * POOL_GW)
            wg = pw_ref[gi].astype(BF16)
            sc = ps_ref[:, gsl]
            dgr = _dot_nn(pb_ref[gi], ext[:, gsl])
            dgb = dgr.astype(BF16)
            yg = _dot_nn(dgb, wg)
            dps_ref[:, gsl] += jnp.sum(dpc_cur[:, gsl] * yg, axis=0, keepdims=True)
            dpw_ref[gi] += _dot_tn(dgb, (dpc_cur[:, gsl] * sc).astype(BF16))
            dd = _dot_nt((dpe[:, gsl] * sc).astype(BF16), wg)
            dpc_ref[:, gsl] = _dot_nn(ptb_ref[gi], dd.astype(BF16))

    def band(w, col=0):
        return [pl.BlockSpec((BLK, w), lambda n: (jnp.maximum(n - 1, 0), col)),
                pl.BlockSpec((BLK, w), lambda n: (n, col)),
                pl.BlockSpec((BLK, w), lambda n: (jnp.minimum(n + 1, nb - 1), col))]

    tab = lambda: pl.BlockSpec((None, POOL_G, BLK, 3 * BLK), lambda n: (_variant_index(n, nb), 0, 0, 0))
    fixed = lambda shape: pl.BlockSpec(shape, lambda n: (0,) * len(shape))
    return _call(
        body, name, (nb,),
        [pl.BlockSpec(memory_space=pltpu.SMEM), pl.BlockSpec((BLK, ATTN_W), lambda n: (n, 0)),
         *band(KV_W), *band(KV_W), *band(POOL_W),
         pl.BlockSpec((BLK, ATTN_W), lambda n: (n, 0)), *band(POOL_W, 1),
         _whole((POOL_G, POOL_GW, POOL_GW)), _whole((1, POOL_W)), tab(), tab()],
        [pl.BlockSpec((BLK, ATTN_W), lambda n: (n, 0)),
         pl.BlockSpec((None, 3, BLK, KV_W), lambda n: (n, 0, 0, 0)),
         pl.BlockSpec((None, 3, BLK, KV_W), lambda n: (n, 0, 0, 0)),
         pl.BlockSpec((BLK, POOL_W), lambda n: (n, 0)),
         fixed((N_HEADS, 128)), fixed((POOL_G, POOL_GW, POOL_GW)), fixed((1, POOL_W))],
        [jax.ShapeDtypeStruct((s, ATTN_W), F32), jax.ShapeDtypeStruct((nb, 3, BLK, KV_W), F32),
         jax.ShapeDtypeStruct((nb, 3, BLK, KV_W), F32), jax.ShapeDtypeStruct((s, POOL_W), F32),
         jax.ShapeDtypeStruct((N_HEADS, 128), F32),
         jax.ShapeDtypeStruct((POOL_G, POOL_GW, POOL_GW), F32), jax.ShapeDtypeStruct((1, POOL_W), F32)],
        (sink, q, k, k, k, v, v, v, pc, pc, pc, dmix, dmix, dmix, dmix, pool_w, pool_scale, pband, ptband),
        ("arbitrary",), carry)


def _mix_out(h, mix, w_out, name, tm):
    s, d = h.shape
    w = mix.shape[1]

    def body(h_ref, m_ref, w_ref, o_ref):
        o_ref[...] = h_ref[...] + _dot_nn(m_ref[...], w_ref[...])

    row = lambda c: pl.BlockSpec((tm, c), lambda i: (i, 0))
    return pl.pallas_call(
        body, name=name, grid=(s // tm,), in_specs=[row(d), row(w), _whole((w, d))], out_specs=row(d),
        out_shape=jax.ShapeDtypeStruct((s, d), F32), compiler_params=_cparams(("arbitrary",)),
    )(h, mix, w_out)


def _mix_out_bwd(dh, w_out, name, tm):
    s, d = dh.shape
    w = w_out.shape[0]

    def body(dh_ref, w_ref, o_ref, dhb_ref):
        dhb = dh_ref[...].astype(BF16)
        dhb_ref[...] = dhb
        o_ref[...] = _dot_nt(dhb, w_ref[...])

    row = lambda c: pl.BlockSpec((tm, c), lambda i: (i, 0))
    return pl.pallas_call(
        body, name=name, grid=(s // tm,), in_specs=[row(d), _whole((w, d))], out_specs=[row(w), row(d)],
        out_shape=[jax.ShapeDtypeStruct((s, w), F32), jax.ShapeDtypeStruct((s, d), BF16)],
        compiler_params=_cparams(("arbitrary",)),
    )(dh, w_out)


def _mix_in_bwd(h, dh, g, win_t, dq, dkp, dvp, dpc, tabs, name, tm):
    s, d = h.shape
    nb = s // BLK
    nt = tm // BLK
    n_in = win_t.shape[0]

    def band_sum(n, before, own, after, prev_last, next_first):
        lo = (n > 0).astype(F32)
        hi = (n < s // tm - 1).astype(F32)
        blocks = []
        for b in range(nt):
            from_prev = prev_last[...] * lo if b == 0 else before[b - 1]
            from_next = next_first[...] * hi if b == nt - 1 else after[b + 1]
            blocks.append(from_prev + own[b] + from_next)
        return jnp.concatenate(blocks, axis=0)

    def body(h_ref, dh_ref, g_ref, w_ref, dq_ref, k2, k1, k0, kp, kn, v2, v1, v0, vp, vn, dpc_ref, c_ref, sa_ref,
             sb_ref, o_ref, du_ref, dg_ref):
        n = pl.program_id(0)
        dk = band_sum(n, k2, k1, k0, kp, kn)
        dv = band_sum(n, v2, v1, v0, vp, vn)
        c, sa, sb = c_ref[...], sa_ref[...], sb_ref[...]
        du = jnp.concatenate([_rope_bwd(dq_ref[...], c, sa, sb, ATTN_W // 128), _rope_bwd(dk, c, sa, sb, 1), dv,
                              dpc_ref[...]], axis=1).astype(BF16)
        du_ref[...] = du
        dn = _dot_nn(du, w_ref[...])
        x = h_ref[...]
        r = lax.rsqrt(jnp.mean(x * x, axis=-1, keepdims=True) + EPS)
        xhat = x * r
        dxhat = dn * g_ref[...]
        o_ref[...] = dh_ref[...] + r * (dxhat - xhat * jnp.mean(dxhat * xhat, axis=-1, keepdims=True))

        @pl.when(n == 0)
        def _():
            dg_ref[...] = jnp.zeros_like(dg_ref)

        dg_ref[...] += jnp.sum(dn * xhat, axis=0, keepdims=True)

    row = lambda w: pl.BlockSpec((tm, w), lambda n: (n, 0))
    slot = lambda t: pl.BlockSpec((nt, None, BLK, KV_W), lambda n, t=t: (n, t, 0, 0))
    parts = [slot(2), slot(1), slot(0),
             pl.BlockSpec((None, None, BLK, KV_W), lambda n: (jnp.maximum(nt * n - 1, 0), 2, 0, 0)),
             pl.BlockSpec((None, None, BLK, KV_W), lambda n: (jnp.minimum(nt * n + nt, nb - 1), 0, 0, 0))]
    return pl.pallas_call(
        body, name=name, grid=(s // tm,),
        in_specs=[row(d), row(d), _whole((1, d)), _whole((n_in, d)), row(ATTN_W), *parts, *parts, row(POOL_W),
                  row(128), row(128), row(128)],
        out_specs=[row(d), row(n_in), pl.BlockSpec((1, d), lambda n: (0, 0))],
        out_shape=[jax.ShapeDtypeStruct((s, d), F32), jax.ShapeDtypeStruct((s, n_in), BF16),
                   jax.ShapeDtypeStruct((1, d), F32)],
        compiler_params=_cparams(("arbitrary",)),
    )(h, dh, g, win_t, dq, *[dkp] * 5, *[dvp] * 5, dpc, *tabs)


def _final(h, g, target, name, tm):
    s, d = h.shape

    def body(h_ref, g_ref, t_ref, loss_ref, dh_ref, dg_ref):
        @pl.when(pl.program_id(0) == 0)
        def _():
            loss_ref[...] = jnp.zeros_like(loss_ref)
            dg_ref[...] = jnp.zeros_like(dg_ref)

        x = h_ref[...]
        gg = g_ref[...]
        r = lax.rsqrt(jnp.mean(x * x, axis=-1, keepdims=True) + EPS)
        xhat = x * r
        e = xhat * gg - t_ref[...]
        per_tok = jnp.mean(e * e, axis=-1, keepdims=True)
        loss_ref[...] += 0.5 * jnp.sum(per_tok, axis=0, keepdims=True)
        dy = e * (1.0 / d)
        dg_ref[...] += jnp.sum(dy * xhat, axis=0, keepdims=True)
        dxhat = dy * gg
        dh_ref[...] = r * (dxhat - xhat * jnp.mean(dxhat * xhat, axis=-1, keepdims=True))

    row = pl.BlockSpec((tm, d), lambda i: (i, 0))
    return pl.pallas_call(
        body, name=name, grid=(s // tm,), in_specs=[row, _whole((1, d)), row],
        out_specs=[pl.BlockSpec((1, 1), lambda i: (0, 0)), row, pl.BlockSpec((1, d), lambda i: (0, 0))],
        out_shape=[jax.ShapeDtypeStruct((1, 1), F32), jax.ShapeDtypeStruct((s, d), F32),
                   jax.ShapeDtypeStruct((1, d), F32)],
        compiler_params=_cparams(("arbitrary",)),
    )(h, g, target)


def _adam_math(w, g, m, v):
    m = ADAM_B1 * m + (1.0 - ADAM_B1) * g
    v = ADAM_B2 * v + (1.0 - ADAM_B2) * (g * g)
    m_hat = m / (1.0 - ADAM_B1 ** ADAM_STEP)
    v_hat = v / (1.0 - ADAM_B2 ** ADAM_STEP)
    delta = -ADAM_LR * (m_hat / (jnp.sqrt(v_hat) + ADAM_EPS) + ADAM_WD * w)
    return delta, m, v


def _adam_small(w, parts, late, m, v, name):
    rows, cols = w.shape

    def body(w_ref, p_ref, l_ref, m_ref, v_ref, g_ref, d_ref, nm_ref, nv_ref):
        g, gl = p_ref[0], l_ref[0]
        for k in range(1, N_DEV):
            g = g + p_ref[k]
            gl = gl + l_ref[k]
        g_ref[...] = g
        g_ref[SMALL_NORM1:SMALL_NORM1 + 8, :] = g[SMALL_NORM1:SMALL_NORM1 + 8] + gl
        d_ref[...], nm_ref[...], nv_ref[...] = _adam_math(w_ref[...], g_ref[...], m_ref[...], v_ref[...])

    return pl.pallas_call(
        body, name=name, out_shape=[jax.ShapeDtypeStruct((rows, cols), F32)] * 4,
    )(w, parts, late, m, v)


SMALL_NORM1 = 512


def _pack_small(norm1, normm, norm2, normf, sink, pool_w, pool_scale, loss=None):
    scale_rows = jnp.pad(pool_scale.reshape(4, 128), ((0, 4), (0, 0)))
    last_rows = jnp.pad(sink.reshape(1, N_HEADS), ((0, 7), (0, 128 - N_HEADS)))
    if loss is not None:
        last_rows = last_rows + jnp.pad(loss.reshape(1, 1), ((1, 6), (0, 127)))
    return jnp.concatenate([pool_w.reshape(512, 128), norm1.reshape(8, 128), normm.reshape(8, 128),
                            norm2.reshape(8, 128), normf.reshape(8, 128), scale_rows, last_rows], axis=0)


def _unpack_small(p):
    return dict(pool_w=p[:512].reshape(1, POOL_G, POOL_GW, POOL_GW), ffn1_norm=p[512:520].reshape(1, 1024),
                mix_norm=p[520:528].reshape(1, 1024), ffn2_norm=p[528:536].reshape(1, 1024),
                final_norm=p[536:544].reshape(1024), pool_scale=p[544:548].reshape(1, POOL_W),
                sink_logits=p[552, :N_HEADS].reshape(1, N_HEADS), loss=p[553, 0])


def kernel(x, ffn1_norm, ffn1_w_gate, ffn1_w_up, ffn1_w_down, mix_norm, w_in, sink_logits, pool_w, pool_scale, w_out, ffn2_norm, ffn2_w_gate, ffn2_w_up, ffn2_w_down, final_norm, loss_target, m_ffn1_norm, m_ffn1_w_gate, m_ffn1_w_up, m_ffn1_w_down, m_mix_norm, m_w_in, m_sink_logits, m_pool_w, m_pool_scale, m_w_out, m_ffn2_norm, m_ffn2_w_gate, m_ffn2_w_up, m_ffn2_w_down, m_final_norm, v_ffn1_norm, v_ffn1_w_gate, v_ffn1_w_up, v_ffn1_w_down, v_mix_norm, v_w_in, v_sink_logits, v_pool_w, v_pool_scale, v_w_out, v_ffn2_norm, v_ffn2_w_gate, v_ffn2_w_up, v_ffn2_w_down, v_final_norm):
    s, d = x.shape[1], x.shape[2]
    fk = ffn1_w_gate.shape[2]
    f = N_DEV * fk
    ink = w_in.shape[2]
    n_in = N_DEV * ink
    mixk = w_out.shape[1]
    tm = min(512, s)
    tm_bwd = min(256, s)
    pos = jnp.stack([lax.axis_index("x"), lax.axis_index("y"), lax.axis_index("c")]).astype(jnp.int32)

    t_bf = lambda w: w[0].T.astype(BF16)
    full = lambda a: a.reshape(N_DEV * a.shape[1], d)
    first = [t_bf(ffn1_w_gate), t_bf(ffn1_w_up), ffn1_w_down[0].astype(BF16)]
    second = _AllGather([t_bf(w_in), w_out[0].astype(BF16), t_bf(ffn2_w_gate)])
    third = _AllGather([t_bf(ffn2_w_up), ffn2_w_down[0].astype(BF16)])

    tabs = _rope_tables(s)
    pband, ptband = _pool_tables(s)
    g1, gm, g2, gf = ffn1_norm, mix_norm, ffn2_norm, final_norm.reshape(1, d)

    x0 = x[0]
    h1, n1, gate1, up1, act1, wg1, wu1, wd1, *gathered = _ffn_fwd_gather(x0, g1, first, "ffn1_fwd", tm, carry=second)
    win_t, wout, wg2 = map(full, gathered)
    q, k, v, pc, n2 = _mix_in(h1, gm, win_t, tabs, "mix_in", tm)
    mix, *gathered = _attn_pool_fwd(q, k, v, pc, sink_logits, pool_w[0], pool_scale, pband, "attn_pool_fwd",
                                    carry=third)
    wu2, wd2 = map(full, gathered)
    h2 = _mix_out(h1, mix, wout, "mix_out", tm)
    h3, n3, gate2, up2, act2 = _ffn_fwd(h2, g2, wg2, wu2, wd2, "ffn2_fwd", tm)
    loss_part, dh3, dgf = _final(h3, gf, loss_target[0], "final", tm)

    tw = 256 if f % 256 == 0 else 128
    gw, sum1, recv2 = {}, {}, {}

    def stage1(keys, rows):
        for key in keys:
            sum1[key] = _rs_stage1_sum(pos, gw[key][0].reshape(N_DEV, rows, d), gw[key][1].reshape(N_DEV, rows, d),
                                       "rs1_" + key, local_order=key in ("g1", "u1", "d1"))

    def stage2(keys):
        return _RsStage2([sum1[key][1] for key in keys])

    dh2, dg2, dgate2, dup2, dhh3 = _ffn_bwd(h2, dh3, gate2, up2, g2, wg2, wu2, wd2, "ffn2_bwd", tm_bwd)
    gw["g2"] = _wgrad(dgate2, n3, "wgrad_gate2", tw)
    gw["u2"] = _wgrad(dup2, n3, "wgrad_up2", tw)
    gw["d2"] = _wgrad(act2, dhh3, "wgrad_down2", tw)
    stage1(["g2", "u2", "d2"], fk)
    dmix, dh2b = _mix_out_bwd(dh2, wout, "mix_out_bwd", tm)
    gw["out"] = _wgrad(mix, dh2b, "wgrad_out", tw)
    stage1(["out"], mixk)
    dq, dkp, dvp, dpc, dsink, dpw, dps, *r2 = _attn_pool_bwd(
        q, k, v, pc, dmix, sink_logits, pool_w[0], pool_scale, pband, ptband, "attn_pool_bwd",
        carry=stage2(["g2", "u2", "d2", "out"]))
    recv2.update(zip(["g2", "u2", "d2", "out"], r2))
    dh1, du, dgm = _mix_in_bwd(h1, dh2, gm, win_t, dq, dkp, dvp, dpc, tabs, "mix_in_bwd", tm)
    gw["in"] = _wgrad(du, n2, "wgrad_in", tw)
    stage1(["in"], ink)
    small_part = _pack_small(jnp.zeros_like(dgm), dgm, dg2, dgf, dsink[:, 0], dpw, dps, loss_part)
    dgate1, dup1, dhh1, small_all = _ffn_bwd_gates(dh1, gate1, up1, wd1, "ffn1_bwd_gates", tm,
                                                   carry=_AllGather([small_part]))
    *gw["g1"], recv2["in"] = _wgrad(dgate1, n1, "wgrad_gate1", tw, carry=stage2(["in"]))
    stage1(["g1"], fk)
    *gw["u1"], recv2["g1"] = _wgrad(dup1, n1, "wgrad_up1", tw, carry=stage2(["g1"]))
    stage1(["u1"], fk)
    *gw["d1"], recv2["u1"] = _wgrad(act1, dhh1, "wgrad_down1", tw, carry=stage2(["u1"]))
    stage1(["d1"], fk)
    dx, dg1, recv2["d1"] = _ffn_bwd_input(x0, dh1, dgate1, dup1, g1, wg1, wu1, "ffn1_bwd_input", tm,
                                          carry=stage2(["d1"]))

    (dg1_all,) = _run_exchange(_DirectGather([dg1.reshape(8, 128)]), "gather_norm1_grad")
    pk = lambda a, b, c_, e, s_, pw_, psc: _pack_small(a, b, c_, e, s_[0], pw_[0], psc)
    small_w = pk(ffn1_norm, mix_norm, ffn2_norm, final_norm, sink_logits, pool_w, pool_scale)
    small_m = pk(m_ffn1_norm, m_mix_norm, m_ffn2_norm, m_final_norm, m_sink_logits, m_pool_w, m_pool_scale)
    small_v = pk(v_ffn1_norm, v_mix_norm, v_ffn2_norm, v_final_norm, v_sink_logits, v_pool_w, v_pool_scale)
    sg, sd, sm, sv = [_unpack_small(a)
                      for a in _adam_small(small_w, small_all, dg1_all, small_m, small_v, "adam_small")]

    big = {}
    keys = ["g1", "u1", "d1", "g2", "u2", "d2", "in", "out"]
    names = ["ffn1_w_gate", "ffn1_w_up", "ffn1_w_down", "ffn2_w_gate", "ffn2_w_up", "ffn2_w_down", "w_in", "w_out"]
    transposed = [True, True, False, True, True, False, True, False]
    ws = [ffn1_w_gate, ffn1_w_up, ffn1_w_down, ffn2_w_gate, ffn2_w_up, ffn2_w_down, w_in, w_out]
    ms = [m_ffn1_w_gate, m_ffn1_w_up, m_ffn1_w_down, m_ffn2_w_gate, m_ffn2_w_up, m_ffn2_w_down, m_w_in, m_w_out]
    vs = [v_ffn1_w_gate, v_ffn1_w_up, v_ffn1_w_down, v_ffn2_w_gate, v_ffn2_w_up, v_ffn2_w_down, v_w_in, v_w_out]
    for key, nm, tr, w, m, vv in zip(keys, names, transposed, ws, ms, vs):
        view = (lambda a: jnp.swapaxes(a, 1, 2)[0]) if tr else (lambda a: a[0])
        back = (lambda a: jnp.swapaxes(a[None], 1, 2)) if tr else (lambda a: a[None])
        res = _rs_sum2_adam(pos, sum1[key][0], recv2[key], view(w), view(m), view(vv), "adam_" + nm)
        big[nm] = tuple(back(a) for a in res)

    loss = sg["loss"]
    all_names = ["ffn1_norm", "ffn1_w_gate", "ffn1_w_up", "ffn1_w_down", "mix_norm", "w_in", "sink_logits", "pool_w",
                 "pool_scale", "w_out", "ffn2_norm", "ffn2_w_gate", "ffn2_w_up", "ffn2_w_down", "final_norm"]
    outs = [loss, dx[None]]
    for idx, src in enumerate((sg, sd, sm, sv)):
        for nm in all_names:
            outs.append(big[nm][idx] if nm in big else src[nm])
    return tuple(outs)
```

```python
import functools

import jax
import jax.numpy as jnp
import numpy as np
from jax import lax
from jax.experimental import pallas as pl
from jax.experimental.pallas import tpu as pltpu

F32 = jnp.float32
BF16 = jnp.bfloat16
MESH = pl.DeviceIdType.MESH
N_DEV = 8

EPS = 1e-6
HEAD_DIM = 64
N_HEADS = 8
N_KV = 2
GROUP = N_HEADS // N_KV
ATTN_W = N_HEADS * HEAD_DIM
KV_W = N_KV * HEAD_DIM
POOL_W = 512
POOL_G = 4
POOL_GW = POOL_W // POOL_G
POOL_WINDOWS = (2, 4, 8, 16)
BLK = 128
ROT = 16
ROPE_THETA = 500000.0
SCORE_SCALE = HEAD_DIM ** -0.5

ADAM_LR, ADAM_B1, ADAM_B2, ADAM_EPS, ADAM_WD, ADAM_STEP = 0.001, 0.9, 0.999, 1e-08, 0.01, 10

VMEM_LIMIT = 56 * 1024 * 1024


def _cparams(sem=None, **kw):
    if sem is not None:
        kw["dimension_semantics"] = sem
    return pltpu.CompilerParams(vmem_limit_bytes=VMEM_LIMIT, **kw)


def _whole(shape):
    nd = len(shape)
    return pl.BlockSpec(shape, lambda *_: (0,) * nd, pipeline_mode=pl.Buffered(1))


def _sigmoid(z):
    return 1.0 / (1.0 + jnp.exp(-z))


def _dot_nt(a, b):
    return lax.dot_general(a, b, (((1,), (1,)), ((), ())), preferred_element_type=F32)


def _dot_nn(a, b):
    return lax.dot_general(a, b, (((1,), (0,)), ((), ())), preferred_element_type=F32)


def _dot_tn(a, b):
    return lax.dot_general(a, b, (((0,), (0,)), ((), ())), preferred_element_type=F32)


def _rope_tables(s):
    inv_freq = ROPE_THETA ** (-np.arange(0, ROT, 2, dtype=np.float64) / ROT)
    ang = np.arange(s, dtype=np.float64)[:, None] * inv_freq[None, :]
    c = np.ones((s, HEAD_DIM)); sa = np.zeros((s, HEAD_DIM)); sb = np.zeros((s, HEAD_DIM))
    c[:, :8] = np.cos(ang); c[:, 8:16] = np.cos(ang)
    sa[:, :8] = -np.sin(ang)
    sb[:, 8:16] = np.sin(ang)
    t = lambda a: jnp.asarray(np.tile(a, (1, 2)).astype(np.float32))
    return t(c), t(sa), t(sb)


def _pool_weight(gi, t, s_pos, s):
    half = POOL_WINDOWS[gi] // 2

    def win(lo, hi):
        a = np.clip(lo, 0, s); b = np.clip(hi + 1, 0, s)
        inside = (s_pos >= a) & (s_pos < b)
        return inside / np.maximum(b - a, 1)

    w = 0.5 * (win(t - half, t + half - 1) + win(t - half + 1, t + half)) - (t == s_pos)
    return w * ((t >= 0) & (t < s) & (s_pos >= 0) & (s_pos < s))


def _pool_tables(s):
    nb = s // BLK
    fwd = np.zeros((3, POOL_G, BLK, 3 * BLK), np.float32)
    bwd = np.zeros((3, POOL_G, BLK, 3 * BLK), np.float32)
    for vi, n in enumerate((0, 1 if nb > 2 else 0, nb - 1)):
        i = n * BLK + np.arange(BLK)[:, None]
        j = (n - 1) * BLK + np.arange(3 * BLK)[None, :]
        for gi in range(POOL_G):
            fwd[vi, gi] = _pool_weight(gi, i, j, s)
            bwd[vi, gi] = _pool_weight(gi, j, i, s)
    return jnp.asarray(fwd, dtype=BF16), jnp.asarray(bwd, dtype=BF16)


def _variant_index(n, nb):
    return jnp.where(n == 0, 0, jnp.where(n == nb - 1, 2, 1))


class _Exchange:
    inputs = ()
    out_shapes = ()
    sems = ()

    def start(self, srcs, outs, sems):
        raise NotImplementedError

    def middle(self, srcs, outs, sems):
        pass

    def finish(self, srcs, outs, sems):
        raise NotImplementedError


class _AllGather(_Exchange):
    def __init__(self, arrays):
        n = len(arrays)
        self.inputs = list(arrays)
        self.out_shapes = [jax.ShapeDtypeStruct((N_DEV,) + a.shape, a.dtype) for a in arrays]
        self.sems = [pltpu.SemaphoreType.DMA((n, 7)), pltpu.SemaphoreType.DMA((n, 7)), pltpu.SemaphoreType.DMA((n,))]

    def _parts(self, srcs, outs, sems):
        send_sems, recv_sems, local_sems = sems
        n = len(srcs)
        x, y, c = lax.axis_index("x"), lax.axis_index("y"), lax.axis_index("c")
        me, sibling = (x, y, c), (x, y, 1 - c)
        chips = [(1 - x, y), (x, 1 - y), (1 - x, 1 - y)]

        def slot(a, dev):
            return outs[a].at[4 * dev[0] + 2 * dev[1] + dev[2]]

        def copy(a, k, block, to, src=None):
            return pltpu.make_async_remote_copy(
                src_ref=slot(a, block) if src is None else src, dst_ref=slot(a, block),
                send_sem=send_sems.at[a, k], recv_sem=recv_sems.at[a, k], device_id=to, device_id_type=MESH)

        def mine():
            return [pltpu.make_async_copy(srcs[a], slot(a, me), local_sems.at[a]) for a in range(n)]

        def first():
            return [copy(a, 0, me, sibling, src=srcs[a]) for a in range(n)] + [
                copy(a, 1 + j, me, (*chip, c), src=srcs[a]) for a in range(n) for j, chip in enumerate(chips)]

        return n, c, me, sibling, chips, copy, mine, first

    def start(self, srcs, outs, sems):
        _, _, _, _, _, _, mine, first = self._parts(srcs, outs, sems)
        for cp in mine() + first():
            cp.start()

    def middle(self, srcs, outs, sems):
        n, c, me, sibling, chips, copy, _, _ = self._parts(srcs, outs, sems)
        for j, chip in enumerate(chips):
            for a in range(n):
                copy(a, 1 + j, (*chip, c), me).wait_recv()
                copy(a, 4 + j, (*chip, c), sibling).start()

    def finish(self, srcs, outs, sems):
        n, c, me, sibling, chips, copy, mine, first = self._parts(srcs, outs, sems)
        mine, first = mine(), first()
        passed = [copy(a, 4 + j, (*chip, c), sibling) for j, chip in enumerate(chips) for a in range(n)]
        for a in range(n):
            copy(a, 0, sibling, me).wait_recv()
            for j, chip in enumerate(chips):
                copy(a, 4 + j, (*chip, 1 - c), me).wait_recv()
        for cp in first + passed:
            cp.wait_send()
        for cp in mine:
            cp.wait()


class _RsStage2(_Exchange):
    def start(self, srcs, outs, sems):
        for cp in self._copies(srcs, outs, sems):
            cp.start()

    def finish(self, srcs, outs, sems):
        copies = self._copies(srcs, outs, sems)
        for cp in copies:
            cp.wait_recv()
        for cp in copies:
            cp.wait_send()


    def __init__(self, pbs):
        n = len(pbs)
        self.inputs = list(pbs)
        self.out_shapes = [jax.ShapeDtypeStruct((3,) + p.shape[1:], p.dtype) for p in pbs]
        self.sems = [pltpu.SemaphoreType.DMA((n, 3)), pltpu.SemaphoreType.DMA((n, 3))]

    def _copies(self, srcs, outs, sems):
        send_sems, recv_sems = sems
        x, y, c = lax.axis_index("x"), lax.axis_index("y"), lax.axis_index("c")
        chips = [(1 - x, y), (x, 1 - y), (1 - x, 1 - y)]
        return [pltpu.make_async_remote_copy(
            src_ref=srcs[a].at[2 * chip[0] + chip[1]], dst_ref=outs[a].at[j], send_sem=send_sems.at[a, j],
            recv_sem=recv_sems.at[a, j], device_id=(*chip, c), device_id_type=MESH)
            for a in range(len(srcs)) for j, chip in enumerate(chips)]


class _DirectGather(_Exchange):
    def __init__(self, arrays):
        n = len(arrays)
        self.inputs = list(arrays)
        self.out_shapes = [jax.ShapeDtypeStruct((N_DEV,) + a.shape, a.dtype) for a in arrays]
        self.sems = [pltpu.SemaphoreType.DMA((n, 7)), pltpu.SemaphoreType.DMA((n, 7)), pltpu.SemaphoreType.DMA((n,))]

    def _copies(self, srcs, outs, sems):
        send_sems, recv_sems, local_sems = sems
        x, y, c = lax.axis_index("x"), lax.axis_index("y"), lax.axis_index("c")
        me = 4 * x + 2 * y + c
        remote, local = [], []
        for a in range(len(srcs)):
            local.append(pltpu.make_async_copy(srcs[a], outs[a].at[me], local_sems.at[a]))
            for k in range(1, N_DEV):
                peer = (x ^ (k >> 2), y ^ ((k >> 1) & 1), c ^ (k & 1))
                remote.append(pltpu.make_async_remote_copy(
                    src_ref=srcs[a], dst_ref=outs[a].at[me], send_sem=send_sems.at[a, k - 1],
                    recv_sem=recv_sems.at[a, k - 1], device_id=peer, device_id_type=MESH))
        return remote, local

    def start(self, srcs, outs, sems):
        remote, local = self._copies(srcs, outs, sems)
        for cp in local + remote:
            cp.start()

    def finish(self, srcs, outs, sems):
        remote, local = self._copies(srcs, outs, sems)
        for cp in remote:
            cp.wait_recv()
        for cp in remote:
            cp.wait_send()
        for cp in local:
            cp.wait()


class _Both(_Exchange):
    def __init__(self, a, b):
        self.a, self.b = a, b
        self.inputs = list(a.inputs) + list(b.inputs)
        self.out_shapes = list(a.out_shapes) + list(b.out_shapes)
        self.sems = list(a.sems) + list(b.sems)

    def _split(self, srcs, outs, sems):
        na, oa, sa = len(self.a.inputs), len(self.a.out_shapes), len(self.a.sems)
        return (srcs[:na], outs[:oa], sems[:sa]), (srcs[na:], outs[oa:], sems[sa:])

    def start(self, srcs, outs, sems):
        pa, pb = self._split(srcs, outs, sems)
        self.a.start(*pa)
        self.b.start(*pb)

    def middle(self, srcs, outs, sems):
        pa, pb = self._split(srcs, outs, sems)
        self.a.middle(*pa)
        self.b.middle(*pb)

    def finish(self, srcs, outs, sems):
        pa, pb = self._split(srcs, outs, sems)
        self.a.finish(*pa)
        self.b.finish(*pb)


_ANY = pl.BlockSpec(memory_space=pl.ANY)


def _run_exchange(ex, name):
    n_in, n_out = len(ex.inputs), len(ex.out_shapes)

    def body(*refs):
        srcs, outs, sems = refs[:n_in], refs[n_in:n_in + n_out], refs[n_in + n_out:]
        ex.start(srcs, outs, sems)
        ex.middle(srcs, outs, sems)
        ex.finish(srcs, outs, sems)

    return pl.pallas_call(
        body, name=name, out_shape=list(ex.out_shapes), in_specs=[_ANY] * n_in, out_specs=[_ANY] * n_out,
        scratch_shapes=list(ex.sems),
    )(*ex.inputs)


CARRY_MIDDLE = 0.85


def _call(body, name, grid, in_specs, out_specs, out_shape, args, sem, carry=None, **kw):
    if carry is None:
        return pl.pallas_call(functools.partial(body), name=name, grid=grid, in_specs=in_specs, out_specs=out_specs,
                              out_shape=out_shape, compiler_params=_cparams(sem), **kw)(*args)
    n_in, n_out = len(in_specs), len(out_specs)
    nc_in, nc_out = len(carry.inputs), len(carry.out_shapes)

    def carried(*refs):
        ins = refs[:n_in]
        c_in = refs[n_in:n_in + nc_in]
        outs = refs[n_in + nc_in:n_in + nc_in + n_out]
        c_out = refs[n_in + nc_in + n_out:n_in + nc_in + n_out + nc_out]
        sems = refs[n_in + nc_in + n_out + nc_out:]
        ids = [pl.program_id(i) for i in range(len(grid))]
        is_first = functools.reduce(jnp.logical_and, [i == 0 for i in ids])
        is_last = functools.reduce(jnp.logical_and, [i == g - 1 for i, g in zip(ids, grid)])
        is_middle = functools.reduce(jnp.logical_and, [ids[0] == round(CARRY_MIDDLE * (grid[0] - 1))]
                                     + [i == 0 for i in ids[1:]])

        @pl.when(is_first)
        def _():
            carry.start(c_in, c_out, sems)

        @pl.when(is_middle)
        def _():
            carry.middle(c_in, c_out, sems)

        body(*ins, *outs)

        @pl.when(is_last)
        def _():
            carry.finish(c_in, c_out, sems)

    return pl.pallas_call(
        carried, name=name, grid=grid, in_specs=list(in_specs) + [_ANY] * nc_in,
        out_specs=list(out_specs) + [_ANY] * nc_out, out_shape=list(out_shape) + list(carry.out_shapes),
        scratch_shapes=list(carry.sems), compiler_params=_cparams(sem), **kw)(*args, *carry.inputs)


def _rs_stage1_sum(pos, gs, gbs, name):
    n = len(gs)
    _, rows, d = gs[0].shape

    def block(pc, q, other_core):
        return 4 * (q // 2) + 2 * (q % 2) + ((1 - pc) if other_core else pc)

    def body(pos_ref, *refs):
        g_refs, gb_refs = refs[:n], refs[n:2 * n]
        p_refs, pb_refs = refs[2 * n:3 * n], refs[3 * n:4 * n]
        land, send_sems, recv_sems = refs[4 * n:]
        q = pl.program_id(0)
        x, y, c = lax.axis_index("x"), lax.axis_index("y"), lax.axis_index("c")

        def copy(a, k):
            return pltpu.make_async_remote_copy(
                src_ref=gb_refs[a].at[block(c, k, True)], dst_ref=land.at[a, k],
                send_sem=send_sems.at[a, k], recv_sem=recv_sems.at[a, k], device_id=(x, y, 1 - c),
                device_id_type=MESH)

        @pl.when(q == 0)
        def _():
            for k in range(4):
                for a in range(n):
                    copy(a, k).start()

        for a in range(n):
            copy(a, q).wait_recv()
            p = g_refs[a][...] + land[a, q].astype(F32)
            p_refs[a][...] = p
            pb_refs[a][...] = p.astype(BF16)

        @pl.when(q == 3)
        def _():
            for k in range(4):
                for a in range(n):
                    copy(a, k).wait_send()

    blk = pl.BlockSpec((None, rows, d), lambda q, pos_ref: (q, 0, 0))
    mine = pl.BlockSpec((None, rows, d), lambda q, pos_ref: (block(pos_ref[2], q, False), 0, 0))
    res = pl.pallas_call(
        body, name=name,
        grid_spec=pltpu.PrefetchScalarGridSpec(
            num_scalar_prefetch=1, grid=(4,),
            in_specs=[mine] * n + [_ANY] * n, out_specs=[blk] * (2 * n),
            scratch_shapes=[pltpu.VMEM((n, 4, rows, d), BF16), pltpu.SemaphoreType.DMA((n, 4)),
                            pltpu.SemaphoreType.DMA((n, 4))]),
        out_shape=[jax.ShapeDtypeStruct((4, rows, d), F32)] * n + [jax.ShapeDtypeStruct((4, rows, d), BF16)] * n,
        compiler_params=_cparams(("arbitrary",)),
    )(pos, *gs, *gbs)
    return list(zip(res[:n], res[n:]))


def _rs_sum2_adam(pos, p, r2, w, m, v, name):
    _, rows, d = p.shape
    tr = rows // 2 if rows % 16 == 0 else rows

    def body(pos_ref, p_ref, r_ref, w_ref, m_ref, v_ref, g_ref, d_ref, nm_ref, nv_ref):
        r = r_ref[...].astype(F32)
        g = ((p_ref[...] + r[0]) + r[1]) + r[2]
        g_ref[...] = g
        d_ref[...], nm_ref[...], nv_ref[...] = _adam_math(w_ref[...], g, m_ref[...], v_ref[...])

    blk = pl.BlockSpec((tr, d), lambda i, pos_ref: (i, 0))
    return pl.pallas_call(
        body, name=name,
        grid_spec=pltpu.PrefetchScalarGridSpec(
            num_scalar_prefetch=1, grid=(rows // tr,),
            in_specs=[pl.BlockSpec((None, tr, d), lambda i, pos_ref: (2 * pos_ref[0] + pos_ref[1], i, 0)),
                      pl.BlockSpec((3, tr, d), lambda i, pos_ref: (0, i, 0)), blk, blk, blk],
            out_specs=[blk] * 4),
        out_shape=[jax.ShapeDtypeStruct((rows, d), F32)] * 4,
        compiler_params=_cparams(("arbitrary",)),
    )(pos, p, r2, w, m, v)


def _ffn_chunk(f):
    for cand in (256, 128):
        if f % cand == 0:
            return cand
    return f


def _loss_head(x, gg, target, loss_ref, dg_ref):
    @pl.when(pl.program_id(0) == 0)
    def _():
        loss_ref[...] = jnp.zeros_like(loss_ref)
        dg_ref[...] = jnp.zeros_like(dg_ref)

    r = lax.rsqrt(jnp.mean(x * x, axis=-1, keepdims=True) + EPS)
    xhat = x * r
    e = xhat * gg - target
    loss_ref[...] += 0.5 * jnp.sum(jnp.mean(e * e, axis=-1, keepdims=True), axis=0, keepdims=True)
    dy = e * (1.0 / x.shape[-1])
    dg_ref[...] += jnp.sum(dy * xhat, axis=0, keepdims=True)
    dxhat = dy * gg
    return r * (dxhat - xhat * jnp.mean(dxhat * xhat, axis=-1, keepdims=True))


def _ffn_fwd(h, g, wg_t, wu_t, wd, name, tm, carry=None, head=None):
    s, d = h.shape
    f = wd.shape[0]
    tf = _ffn_chunk(f)

    def body(h_ref, g_ref, wg_ref, wu_ref, wd_ref, *refs):
        if head is None:
            o_ref, n_ref, gate_ref, up_ref, act_ref = refs
        else:
            gf_ref, t_ref, o_ref, n_ref, gate_ref, up_ref, act_ref, loss_ref, dgf_ref = refs
        x = h_ref[...]
        r = lax.rsqrt(jnp.mean(x * x, axis=-1, keepdims=True) + EPS)
        nb = (x * r * g_ref[...]).astype(BF16)
        n_ref[...] = nb
        for j in range(f // tf):
            sl = slice(j * tf, (j + 1) * tf)
            gate = _dot_nt(nb, wg_ref[sl, :])
            up = _dot_nt(nb, wu_ref[sl, :])
            gate_ref[:, sl] = gate.astype(BF16)
            up_ref[:, sl] = up.astype(BF16)
            act_ref[:, sl] = (gate * _sigmoid(gate) * up).astype(BF16)
        h_out = x + 0.5 * _dot_nn(act_ref[...], wd_ref[...])
        o_ref[...] = h_out if head is None else _loss_head(h_out, gf_ref[...], t_ref[...], loss_ref, dgf_ref)

    row = lambda w: pl.BlockSpec((tm, w), lambda i: (i, 0))
    in_specs = [row(d), _whole((1, d)), _whole((f, d)), _whole((f, d)), _whole((f, d))]
    out_specs = [row(d), row(d), row(f), row(f), row(f)]
    out_shape = [jax.ShapeDtypeStruct((s, d), F32), jax.ShapeDtypeStruct((s, d), BF16),
                 jax.ShapeDtypeStruct((s, f), BF16), jax.ShapeDtypeStruct((s, f), BF16),
                 jax.ShapeDtypeStruct((s, f), BF16)]
    args = (h, g, wg_t, wu_t, wd)
    if head is not None:
        in_specs += [_whole((1, d)), row(d)]
        out_specs += [pl.BlockSpec((1, 1), lambda i: (0, 0)), pl.BlockSpec((1, d), lambda i: (0, 0))]
        out_shape += [jax.ShapeDtypeStruct((1, 1), F32), jax.ShapeDtypeStruct((1, d), F32)]
        args += tuple(head)
    return _call(body, name, (s // tm,), in_specs, out_specs, out_shape, args, ("arbitrary",), carry)


def _gate_grads(dh_ref, gate_ref, up_ref, wd_ref, dgate_ref, dup_ref, dhh_ref, tf):
    dhh = (0.5 * dh_ref[...]).astype(BF16)
    dhh_ref[...] = dhh
    for j in range(gate_ref.shape[1] // tf):
        sl = slice(j * tf, (j + 1) * tf)
        gt = gate_ref[:, sl].astype(F32)
        u = up_ref[:, sl].astype(F32)
        dact = _dot_nt(dhh, wd_ref[sl, :])
        sg = _sigmoid(gt)
        dup_ref[:, sl] = (dact * (gt * sg)).astype(BF16)
        dgate_ref[:, sl] = (dact * u * (sg * (1.0 + gt * (1.0 - sg)))).astype(BF16)


def _input_grad(h_ref, dh_ref, dgate_ref, dup_ref, g_ref, wg_ref, wu_ref, o_ref, dg_ref):
    x = h_ref[...]
    r = lax.rsqrt(jnp.mean(x * x, axis=-1, keepdims=True) + EPS)
    xhat = x * r
    dn = _dot_nn(dgate_ref[...], wg_ref[...]) + _dot_nn(dup_ref[...], wu_ref[...])
    dxhat = dn * g_ref[...]
    o_ref[...] = dh_ref[...] + r * (dxhat - xhat * jnp.mean(dxhat * xhat, axis=-1, keepdims=True))

    @pl.when(pl.program_id(0) == 0)
    def _():
        dg_ref[...] = jnp.zeros_like(dg_ref)

    dg_ref[...] += jnp.sum(dn * xhat, axis=0, keepdims=True)


def _ffn_bwd(h_in, dh_out, gate, up, g, wg_t, wu_t, wd, name, tm):
    s, d = h_in.shape
    f = gate.shape[1]
    tf = _ffn_chunk(f)

    def body(h_ref, dh_ref, gate_ref, up_ref, g_ref, wg_ref, wu_ref, wd_ref,
             o_ref, dg_ref, dgate_ref, dup_ref, dhh_ref):
        _gate_grads(dh_ref, gate_ref, up_ref, wd_ref, dgate_ref, dup_ref, dhh_ref, tf)
        _input_grad(h_ref, dh_ref, dgate_ref, dup_ref, g_ref, wg_ref, wu_ref, o_ref, dg_ref)

    row = lambda w: pl.BlockSpec((tm, w), lambda i: (i, 0))
    return pl.pallas_call(
        body, name=name, grid=(s // tm,),
        in_specs=[row(d), row(d), row(f), row(f), _whole((1, d)), _whole((f, d)), _whole((f, d)), _whole((f, d))],
        out_specs=[row(d), pl.BlockSpec((1, d), lambda i: (0, 0)), row(f), row(f), row(d)],
        out_shape=[jax.ShapeDtypeStruct((s, d), F32), jax.ShapeDtypeStruct((1, d), F32),
                   jax.ShapeDtypeStruct((s, f), BF16), jax.ShapeDtypeStruct((s, f), BF16),
                   jax.ShapeDtypeStruct((s, d), BF16)],
        compiler_params=_cparams(("arbitrary",)),
    )(h_in, dh_out, gate, up, g, wg_t, wu_t, wd)


def _ffn_bwd_gates(dh_out, gate, up, wd, name, tm, carry=None):
    s, d = dh_out.shape
    f = gate.shape[1]
    tf = _ffn_chunk(f)

    def body(dh_ref, gate_ref, up_ref, wd_ref, dgate_ref, dup_ref, dhh_ref):
        _gate_grads(dh_ref, gate_ref, up_ref, wd_ref, dgate_ref, dup_ref, dhh_ref, tf)

    row = lambda w: pl.BlockSpec((tm, w), lambda i: (i, 0))
    return _call(
        body, name, (s // tm,), [row(d), row(f), row(f), _whole((f, d))], [row(f), row(f), row(d)],
        [jax.ShapeDtypeStruct((s, f), BF16), jax.ShapeDtypeStruct((s, f), BF16), jax.ShapeDtypeStruct((s, d), BF16)],
        (dh_out, gate, up, wd), ("arbitrary",), carry)


def _ffn_bwd_input(h_in, dh_out, dgate, dup, g, wg_t, wu_t, name, tm, carry=None):
    s, d = h_in.shape
    f = dgate.shape[1]

    row = lambda w: pl.BlockSpec((tm, w), lambda i: (i, 0))
    return _call(
        _input_grad, name, (s // tm,),
        [row(d), row(d), row(f), row(f), _whole((1, d)), _whole((f, d)), _whole((f, d))],
        [row(d), pl.BlockSpec((1, d), lambda i: (0, 0))],
        [jax.ShapeDtypeStruct((s, d), F32), jax.ShapeDtypeStruct((1, d), F32)],
        (h_in, dh_out, dgate, dup, g, wg_t, wu_t), ("arbitrary",), carry)


def _wgrad(a, b, name, tf, carry=None):
    s, f = a.shape
    d = b.shape[1]

    def body(a_ref, b_ref, o_ref, ob_ref):
        acc = _dot_tn(a_ref[...], b_ref[...])
        o_ref[...] = acc
        ob_ref[...] = acc.astype(BF16)

    return _call(
        body, name, (f // tf,),
        [pl.BlockSpec((s, tf), lambda i: (0, i)), _whole((s, d))],
        [pl.BlockSpec((tf, d), lambda i: (i, 0)), pl.BlockSpec((tf, d), lambda i: (i, 0))],
        [jax.ShapeDtypeStruct((f, d), F32), jax.ShapeDtypeStruct((f, d), BF16)],
        (a, b), ("arbitrary",), carry)


def _rope(t, c, sa, sb, reps):
    c, sa, sb = (jnp.tile(v, (1, reps)) if reps > 1 else v for v in (c, sa, sb))
    w = t.shape[1]
    return t * c + pltpu.roll(t, w - 8, 1) * sa + pltpu.roll(t, 8, 1) * sb


def _rope_bwd(dt, c, sa, sb, reps):
    c, sa, sb = (jnp.tile(v, (1, reps)) if reps > 1 else v for v in (c, sa, sb))
    w = dt.shape[1]
    return dt * c + pltpu.roll(dt * sa, 8, 1) + pltpu.roll(dt * sb, w - 8, 1)


def _mix_in(h, g, win_t, tabs, name, tm):
    s, d = h.shape
    n_in = win_t.shape[0]

    def body(h_ref, g_ref, w_ref, c_ref, sa_ref, sb_ref, q_ref, k_ref, v_ref, pc_ref, n_ref):
        x = h_ref[...]
        r = lax.rsqrt(jnp.mean(x * x, axis=-1, keepdims=True) + EPS)
        nb = (x * r * g_ref[...]).astype(BF16)
        n_ref[...] = nb
        u = _dot_nt(nb, w_ref[...])
        c, sa, sb = c_ref[...], sa_ref[...], sb_ref[...]
        q_ref[...] = _rope(u[:, :ATTN_W], c, sa, sb, ATTN_W // 128).astype(BF16)
        k_ref[...] = _rope(u[:, ATTN_W:ATTN_W + KV_W], c, sa, sb, 1).astype(BF16)
        v_ref[...] = u[:, ATTN_W + KV_W:ATTN_W + 2 * KV_W].astype(BF16)
        pc_ref[...] = u[:, ATTN_W + 2 * KV_W:]

    row = lambda w: pl.BlockSpec((tm, w), lambda i: (i, 0))
    return pl.pallas_call(
        body, name=name, grid=(s // tm,),
        in_specs=[row(d), _whole((1, d)), _whole((n_in, d)), row(128), row(128), row(128)],
        out_specs=[row(ATTN_W), row(KV_W), row(KV_W), row(POOL_W), row(d)],
        out_shape=[jax.ShapeDtypeStruct((s, ATTN_W), BF16), jax.ShapeDtypeStruct((s, KV_W), BF16),
                   jax.ShapeDtypeStruct((s, KV_W), BF16), jax.ShapeDtypeStruct((s, POOL_W), F32),
                   jax.ShapeDtypeStruct((s, d), BF16)],
        compiler_params=_cparams(("arbitrary",)),
    )(h, g, win_t, *tabs)


def _band_mask(n, nb, transposed):
    shape = (3 * BLK, 2 * BLK) if transposed else (2 * BLK, 3 * BLK)
    i = lax.broadcasted_iota(jnp.int32, shape, 1 if transposed else 0) % BLK
    j = lax.broadcasted_iota(jnp.int32, shape, 0 if transposed else 1)
    kpos = (n - 1) * BLK + j
    return (j >= i) & (j <= i + 2 * BLK) & (kpos >= 0) & (kpos < nb * BLK)


def _block_diag(t, kh):
    tf = t.astype(F32)
    tr = pltpu.roll(tf, HEAD_DIM, 1)
    lo = lax.broadcasted_iota(jnp.int32, tf.shape, 1) < HEAD_DIM
    top, bot = (tf, tr) if kh == 0 else (tr, tf)
    return jnp.concatenate([jnp.where(lo, top, 0.0), jnp.where(lo, 0.0, bot)], axis=0).astype(BF16)


def _fold_diag(tbd):
    lo = lax.broadcasted_iota(jnp.int32, (3 * BLK, 2 * HEAD_DIM), 1) < HEAD_DIM
    t = jnp.where(lo, tbd[:3 * BLK], tbd[3 * BLK:])
    return t + pltpu.roll(t, HEAD_DIM, 1)


def _stack_pairs(x, kh):
    return jnp.concatenate([x[:, (2 * kh) * 128:(2 * kh + 1) * 128], x[:, (2 * kh + 1) * 128:(2 * kh + 2) * 128]], axis=0)


def _sink_of(sink_ref, kh, half, axis):
    shape = (2 * BLK, 1) if axis == 0 else (1, 2 * BLK)
    first = lax.broadcasted_iota(jnp.int32, shape, axis) < BLK
    return jnp.where(first, sink_ref[0, GROUP * kh + half], sink_ref[0, GROUP * kh + 2 + half])


def _softmax_sink(sc, valid, sink, axis):
    sc = jnp.where(valid, sc, -1e30)
    m = jnp.maximum(jnp.max(sc, axis=axis, keepdims=True), sink)
    e = jnp.exp(sc - m)
    es = jnp.exp(sink - m)
    inv = 1.0 / (jnp.sum(e, axis=axis, keepdims=True) + es)
    return e * inv, es * inv


def _attn_pool_fwd(q, k, v, pc, sink, pool_w, pool_scale, pband, name, carry=None):
    s = q.shape[0]
    nb = s // BLK

    def body(sink_ref, q_ref, k0, k1, k2, v0, v1, v2, p0, p1, p2, pw_ref, ps_ref, pb_ref, o_ref):
        n = pl.program_id(0)
        valid = _band_mask(n, nb, False)
        kb = jnp.concatenate([k0[...], k1[...], k2[...]], axis=0)
        vb = jnp.concatenate([v0[...], v1[...], v2[...]], axis=0)
        qs = q_ref[...] * SCORE_SCALE
        for kh in range(N_KV):
            sc = _dot_nt(_stack_pairs(qs, kh), _block_diag(kb, kh))
            p = [_softmax_sink(sc[:, half * 3 * BLK:(half + 1) * 3 * BLK], valid, _sink_of(sink_ref, kh, half, 0), 1)[0]
                 for half in range(2)]
            o2 = _dot_nn(jnp.concatenate(p, axis=1).astype(BF16), _block_diag(vb, kh)).astype(BF16)
            o_ref[:, (2 * kh) * 128:(2 * kh + 1) * 128] = o2[:BLK]
            o_ref[:, (2 * kh + 1) * 128:(2 * kh + 2) * 128] = o2[BLK:]
        ext = jnp.concatenate([p0[...], p1[...], p2[...]], axis=0).astype(BF16)
        for gi in range(POOL_G):
            gsl = slice(gi * POOL_GW, (gi + 1) * POOL_GW)
            dg = _dot_nn(pb_ref[gi], ext[:, gsl])
            yg = _dot_nn(dg.astype(BF16), pw_ref[gi].astype(BF16))
            o_ref[:, ATTN_W + gi * POOL_GW:ATTN_W + (gi + 1) * POOL_GW] = (yg * ps_ref[:, gsl]).astype(BF16)

    def band(w):
        return [pl.BlockSpec((BLK, w), lambda n: (jnp.maximum(n - 1, 0), 0)),
                pl.BlockSpec((BLK, w), lambda n: (n, 0)),
                pl.BlockSpec((BLK, w), lambda n: (jnp.minimum(n + 1, nb - 1), 0))]

    return _call(
        body, name, (nb,),
        [pl.BlockSpec(memory_space=pltpu.SMEM), pl.BlockSpec((BLK, ATTN_W), lambda n: (n, 0)),
         *band(KV_W), *band(KV_W), *band(POOL_W),
         _whole((POOL_G, POOL_GW, POOL_GW)), _whole((1, POOL_W)),
         pl.BlockSpec((None, POOL_G, BLK, 3 * BLK), lambda n: (_variant_index(n, nb), 0, 0, 0))],
        [pl.BlockSpec((BLK, ATTN_W + POOL_W), lambda n: (n, 0))],
        [jax.ShapeDtypeStruct((s, ATTN_W + POOL_W), BF16)],
        (sink, q, k, k, k, v, v, v, pc, pc, pc, pool_w, pool_scale, pband), ("arbitrary",), carry)


def _attn_pool_bwd(q, k, v, pc, dmix, sink, pool_w, pool_scale, pband, ptband, name, carry=None):
    s = q.shape[0]
    nb = s // BLK

    def body(sink_ref, q_ref, k0, k1, k2, v0, v1, v2, p0, p1, p2, da_ref, d0, d1, d2, pw_ref, ps_ref, pb_ref, ptb_ref,
             dq_ref, dk_ref, dv_ref, dpc_ref, dsink_ref, dpw_ref, dps_ref):
        n = pl.program_id(0)

        @pl.when(n == 0)
        def _():
            dsink_ref[...] = jnp.zeros_like(dsink_ref)
            dpw_ref[...] = jnp.zeros_like(dpw_ref)
            dps_ref[...] = jnp.zeros_like(dps_ref)

        valid = _band_mask(n, nb, True)
        kb = jnp.concatenate([k0[...], k1[...], k2[...]], axis=0)
        vb = jnp.concatenate([v0[...], v1[...], v2[...]], axis=0)
        qb = q_ref[...]
        qs = qb * SCORE_SCALE
        da = da_ref[...].astype(BF16)
        dk_fold, dv_fold = [], []
        for kh in range(N_KV):
            kbd, vbd = _block_diag(kb, kh), _block_diag(vb, kh)
            q2, do2 = _stack_pairs(qb, kh), _stack_pairs(da, kh)
            sc_t = _dot_nt(kbd, _stack_pairs(qs, kh))
            dp_t = _dot_nt(vbd, do2)
            p_t, ds_t = [], []
            for half in range(2):
                rows = slice(half * 3 * BLK, (half + 1) * 3 * BLK)
                p, ps = _softmax_sink(sc_t[rows], valid, _sink_of(sink_ref, kh, half, 1), 0)
                delta = jnp.sum(p * dp_t[rows], axis=0, keepdims=True)
                p_t.append(p.astype(BF16))
                ds_t.append((p * (dp_t[rows] - delta)).astype(BF16))
                dsk = -ps * delta
                for pair in range(2):
                    h = GROUP * kh + 2 * pair + half
                    part = jnp.sum(dsk[:, pair * BLK:(pair + 1) * BLK], axis=1, keepdims=True)
                    dsink_ref[h:h + 1, :] += jnp.broadcast_to(part, (1, 128))
            p_t = jnp.concatenate(p_t, axis=0)
            ds_t = jnp.concatenate(ds_t, axis=0)
            dq2 = _dot_tn(ds_t, kbd) * SCORE_SCALE
            dq_ref[:, (2 * kh) * 128:(2 * kh + 1) * 128] = dq2[:BLK]
            dq_ref[:, (2 * kh + 1) * 128:(2 * kh + 2) * 128] = dq2[BLK:]
            dk_fold.append(_fold_diag(_dot_nn(ds_t, q2)) * SCORE_SCALE)
            dv_fold.append(_fold_diag(_dot_nn(p_t, do2)))
        lo = lax.broadcasted_iota(jnp.int32, (3 * BLK, KV_W), 1) < HEAD_DIM
        dk_all = jnp.where(lo, dk_fold[0], dk_fold[1])
        dv_all = jnp.where(lo, dv_fold[0], dv_fold[1])
        for t in range(3):
            dk_ref[t] = dk_all[t * BLK:(t + 1) * BLK]
            dv_ref[t] = dv_all[t * BLK:(t + 1) * BLK]
        ext = jnp.concatenate([p0[...], p1[...], p2[...]], axis=0).astype(BF16)
        dpe = jnp.concatenate([d0[...], d1[...], d2[...]], axis=0)
        dpc_cur = d1[...]
        for gi in range(POOL_G):
            gsl = slice(gi * POOL_GW, (gi + 1) * POOL_GW)
            wg = pw_ref[gi].astype(BF16)
            sc = ps_ref[:, gsl]
            dgr = _dot_nn(pb_ref[gi], ext[:, gsl])
            dgb = dgr.astype(BF16)
            yg = _dot_nn(dgb, wg)
            dps_ref[:, gsl] += jnp.sum(dpc_cur[:, gsl] * yg, axis=0, keepdims=True)
            dpw_ref[gi] += _dot_tn(dgb, (dpc_cur[:, gsl] * sc).astype(BF16))
            dd = _dot_nt((dpe[:, gsl] * sc).astype(BF16), wg)
            dpc_ref[:, gsl] = _dot_nn(ptb_ref[gi], dd.astype(BF16))

    def band(w, col=0):
        return [pl.BlockSpec((BLK, w), lambda n: (jnp.maximum(n - 1, 0), col)),
                pl.BlockSpec((BLK, w), lambda n: (n, col)),
                pl.BlockSpec((BLK, w), lambda n: (jnp.minimum(n + 1, nb - 1), col))]

    tab = lambda: pl.BlockSpec((None, POOL_G, BLK, 3 * BLK), lambda n: (_variant_index(n, nb), 0, 0, 0))
    fixed = lambda shape: pl.BlockSpec(shape, lambda n: (0,) * len(shape))
    return _call(
        body, name, (nb,),
        [pl.BlockSpec(memory_space=pltpu.SMEM), pl.BlockSpec((BLK, ATTN_W), lambda n: (n, 0)),
         *band(KV_W), *band(KV_W), *band(POOL_W),
         pl.BlockSpec((BLK, ATTN_W), lambda n: (n, 0)), *band(POOL_W, 1),
         _whole((POOL_G, POOL_GW, POOL_GW)), _whole((1, POOL_W)), tab(), tab()],
        [pl.BlockSpec((BLK, ATTN_W), lambda n: (n, 0)),
         pl.BlockSpec((None, 3, BLK, KV_W), lambda n: (n, 0, 0, 0)),
         pl.BlockSpec((None, 3, BLK, KV_W), lambda n: (n, 0, 0, 0)),
         pl.BlockSpec((BLK, POOL_W), lambda n: (n, 0)),
         fixed((N_HEADS, 128)), fixed((POOL_G, POOL_GW, POOL_GW)), fixed((1, POOL_W))],
        [jax.ShapeDtypeStruct((s, ATTN_W), F32), jax.ShapeDtypeStruct((nb, 3, BLK, KV_W), F32),
         jax.ShapeDtypeStruct((nb, 3, BLK, KV_W), F32), jax.ShapeDtypeStruct((s, POOL_W), F32),
         jax.ShapeDtypeStruct((N_HEADS, 128), F32),
         jax.ShapeDtypeStruct((POOL_G, POOL_GW, POOL_GW), F32), jax.ShapeDtypeStruct((1, POOL_W), F32)],
        (sink, q, k, k, k, v, v, v, pc, pc, pc, dmix, dmix, dmix, dmix, pool_w, pool_scale, pband, ptband),
        ("arbitrary",), carry)


def _mix_out(h, mix, w_out, name, tm):
    s, d = h.shape
    w = mix.shape[1]

    def body(h_ref, m_ref, w_ref, o_ref):
        o_ref[...] = h_ref[...] + _dot_nn(m_ref[...], w_ref[...])

    row = lambda c: pl.BlockSpec((tm, c), lambda i: (i, 0))
    return pl.pallas_call(
        body, name=name, grid=(s // tm,), in_specs=[row(d), row(w), _whole((w, d))], out_specs=row(d),
        out_shape=jax.ShapeDtypeStruct((s, d), F32), compiler_params=_cparams(("arbitrary",)),
    )(h, mix, w_out)


def _mix_out_bwd(dh, w_out, name, tm):
    s, d = dh.shape
    w = w_out.shape[0]

    def body(dh_ref, w_ref, o_ref, dhb_ref):
        dhb = dh_ref[...].astype(BF16)
        dhb_ref[...] = dhb
        o_ref[...] = _dot_nt(dhb, w_ref[...])

    row = lambda c: pl.BlockSpec((tm, c), lambda i: (i, 0))
    return pl.pallas_call(
        body, name=name, grid=(s // tm,), in_specs=[row(d), _whole((w, d))], out_specs=[row(w), row(d)],
        out_shape=[jax.ShapeDtypeStruct((s, w), F32), jax.ShapeDtypeStruct((s, d), BF16)],
        compiler_params=_cparams(("arbitrary",)),
    )(dh, w_out)


def _mix_in_bwd(h, dh, g, win_t, dq, dkp, dvp, dpc, tabs, name, tm):
    s, d = h.shape
    nb = s // BLK
    nt = tm // BLK
    n_in = win_t.shape[0]

    def band_sum(n, before, own, after, prev_last, next_first):
        lo = (n > 0).astype(F32)
        hi = (n < s // tm - 1).astype(F32)
        blocks = []
        for b in range(nt):
            from_prev = prev_last[...] * lo if b == 0 else before[b - 1]
            from_next = next_first[...] * hi if b == nt - 1 else after[b + 1]
            blocks.append(from_prev + own[b] + from_next)
        return jnp.concatenate(blocks, axis=0)

    def body(h_ref, dh_ref, g_ref, w_ref, dq_ref, k2, k1, k0, kp, kn, v2, v1, v0, vp, vn, dpc_ref, c_ref, sa_ref,
             sb_ref, o_ref, du_ref, dg_ref):
        n = pl.program_id(0)
        dk = band_sum(n, k2, k1, k0, kp, kn)
        dv = band_sum(n, v2, v1, v0, vp, vn)
        c, sa, sb = c_ref[...], sa_ref[...], sb_ref[...]
        du = jnp.concatenate([_rope_bwd(dq_ref[...], c, sa, sb, ATTN_W // 128), _rope_bwd(dk, c, sa, sb, 1), dv,
                              dpc_ref[...]], axis=1).astype(BF16)
        du_ref[...] = du
        dn = _dot_nn(du, w_ref[...])
        x = h_ref[...]
        r = lax.rsqrt(jnp.mean(x * x, axis=-1, keepdims=True) + EPS)
        xhat = x * r
        dxhat = dn * g_ref[...]
        o_ref[...] = dh_ref[...] + r * (dxhat - xhat * jnp.mean(dxhat * xhat, axis=-1, keepdims=True))

        @pl.when(n == 0)
        def _():
            dg_ref[...] = jnp.zeros_like(dg_ref)

        dg_ref[...] += jnp.sum(dn * xhat, axis=0, keepdims=True)

    row = lambda w: pl.BlockSpec((tm, w), lambda n: (n, 0))
    slot = lambda t: pl.BlockSpec((nt, None, BLK, KV_W), lambda n, t=t: (n, t, 0, 0))
    parts = [slot(2), slot(1), slot(0),
             pl.BlockSpec((None, None, BLK, KV_W), lambda n: (jnp.maximum(nt * n - 1, 0), 2, 0, 0)),
             pl.BlockSpec((None, None, BLK, KV_W), lambda n: (jnp.minimum(nt * n + nt, nb - 1), 0, 0, 0))]
    return pl.pallas_call(
        body, name=name, grid=(s // tm,),
        in_specs=[row(d), row(d), _whole((1, d)), _whole((n_in, d)), row(ATTN_W), *parts, *parts, row(POOL_W),
                  row(128), row(128), row(128)],
        out_specs=[row(d), row(n_in), pl.BlockSpec((1, d), lambda n: (0, 0))],
        out_shape=[jax.ShapeDtypeStruct((s, d), F32), jax.ShapeDtypeStruct((s, n_in), BF16),
                   jax.ShapeDtypeStruct((1, d), F32)],
        compiler_params=_cparams(("arbitrary",)),
    )(h, dh, g, win_t, dq, *[dkp] * 5, *[dvp] * 5, dpc, *tabs)


def _adam_math(w, g, m, v):
    m = ADAM_B1 * m + (1.0 - ADAM_B1) * g
    v = ADAM_B2 * v + (1.0 - ADAM_B2) * (g * g)
    m_hat = m / (1.0 - ADAM_B1 ** ADAM_STEP)
    v_hat = v / (1.0 - ADAM_B2 ** ADAM_STEP)
    delta = -ADAM_LR * (m_hat / (jnp.sqrt(v_hat) + ADAM_EPS) + ADAM_WD * w)
    return delta, m, v


def _adam_small(w, parts, late, m, v, name):
    rows, cols = w.shape

    def body(w_ref, p_ref, l_ref, m_ref, v_ref, g_ref, d_ref, nm_ref, nv_ref):
        g, gl = p_ref[0], l_ref[0]
        for k in range(1, N_DEV):
            g = g + p_ref[k]
            gl = gl + l_ref[k]
        g_ref[...] = g
        g_ref[SMALL_NORM1:SMALL_NORM1 + 8, :] = g[SMALL_NORM1:SMALL_NORM1 + 8] + gl
        d_ref[...], nm_ref[...], nv_ref[...] = _adam_math(w_ref[...], g_ref[...], m_ref[...], v_ref[...])

    return pl.pallas_call(
        body, name=name, out_shape=[jax.ShapeDtypeStruct((rows, cols), F32)] * 4,
    )(w, parts, late, m, v)


SMALL_NORM1 = 512


def _pack_small(norm1, normm, norm2, normf, sink, pool_w, pool_scale, loss=None):
    scale_rows = jnp.pad(pool_scale.reshape(4, 128), ((0, 4), (0, 0)))
    last_rows = jnp.pad(sink.reshape(1, N_HEADS), ((0, 7), (0, 128 - N_HEADS)))
    if loss is not None:
        last_rows = last_rows + jnp.pad(loss.reshape(1, 1), ((1, 6), (0, 127)))
    return jnp.concatenate([pool_w.reshape(512, 128), norm1.reshape(8, 128), normm.reshape(8, 128),
                            norm2.reshape(8, 128), normf.reshape(8, 128), scale_rows, last_rows], axis=0)


def _unpack_small(p):
    return dict(pool_w=p[:512].reshape(1, POOL_G, POOL_GW, POOL_GW), ffn1_norm=p[512:520].reshape(1, 1024),
                mix_norm=p[520:528].reshape(1, 1024), ffn2_norm=p[528:536].reshape(1, 1024),
                final_norm=p[536:544].reshape(1024), pool_scale=p[544:548].reshape(1, POOL_W),
                sink_logits=p[552, :N_HEADS].reshape(1, N_HEADS), loss=p[553, 0])


def kernel(x, ffn1_norm, ffn1_w_gate, ffn1_w_up, ffn1_w_down, mix_norm, w_in, sink_logits, pool_w, pool_scale, w_out, ffn2_norm, ffn2_w_gate, ffn2_w_up, ffn2_w_down, final_norm, loss_target, m_ffn1_norm, m_ffn1_w_gate, m_ffn1_w_up, m_ffn1_w_down, m_mix_norm, m_w_in, m_sink_logits, m_pool_w, m_pool_scale, m_w_out, m_ffn2_norm, m_ffn2_w_gate, m_ffn2_w_up, m_ffn2_w_down, m_final_norm, v_ffn1_norm, v_ffn1_w_gate, v_ffn1_w_up, v_ffn1_w_down, v_mix_norm, v_w_in, v_sink_logits, v_pool_w, v_pool_scale, v_w_out, v_ffn2_norm, v_ffn2_w_gate, v_ffn2_w_up, v_ffn2_w_down, v_final_norm):
    s, d = x.shape[1], x.shape[2]
    fk = ffn1_w_gate.shape[2]
    f = N_DEV * fk
    ink = w_in.shape[2]
    n_in = N_DEV * ink
    mixk = w_out.shape[1]
    tm = min(512, s)
    tm_bwd = min(256, s)
    pos = jnp.stack([lax.axis_index("x"), lax.axis_index("y"), lax.axis_index("c")]).astype(jnp.int32)

    t_bf = lambda w: w[0].T.astype(BF16)
    full = lambda a: a.reshape(N_DEV * a.shape[1], d)
    first = [t_bf(ffn1_w_gate), t_bf(ffn1_w_up), ffn1_w_down[0].astype(BF16)]
    wg1, wu1, wd1 = map(full, _run_exchange(_AllGather(first), "gather_ffn1"))
    second = _AllGather([t_bf(w_in), w_out[0].astype(BF16), t_bf(ffn2_w_gate)])
    third = _AllGather([t_bf(ffn2_w_up), ffn2_w_down[0].astype(BF16)])

    tabs = _rope_tables(s)
    pband, ptband = _pool_tables(s)
    g1, gm, g2, gf = ffn1_norm, mix_norm, ffn2_norm, final_norm.reshape(1, d)

    x0 = x[0]
    h1, n1, gate1, up1, act1, *gathered = _ffn_fwd(x0, g1, wg1, wu1, wd1, "ffn1_fwd", tm, carry=second)
    win_t, wout, wg2 = map(full, gathered)
    q, k, v, pc, n2 = _mix_in(h1, gm, win_t, tabs, "mix_in", tm)
    mix, *gathered = _attn_pool_fwd(q, k, v, pc, sink_logits, pool_w[0], pool_scale, pband, "attn_pool_fwd",
                                    carry=third)
    wu2, wd2 = map(full, gathered)
    h2 = _mix_out(h1, mix, wout, "mix_out", tm)
    dh3, n3, gate2, up2, act2, loss_part, dgf = _ffn_fwd(h2, g2, wg2, wu2, wd2, "ffn2_fwd", tm,
                                                         head=(gf, loss_target[0]))

    tw = 256 if f % 256 == 0 else 128
    gw, sum1, recv2 = {}, {}, {}

    def stage1(keys, rows):
        sums = _rs_stage1_sum(pos, [gw[key][0].reshape(N_DEV, rows, d) for key in keys],
                              [gw[key][1].reshape(N_DEV, rows, d) for key in keys], "rs1_" + keys[0])
        sum1.update(zip(keys, sums))

    def stage2(keys):
        return _RsStage2([sum1[key][1] for key in keys])

    dh2, dg2, dgate2, dup2, dhh3 = _ffn_bwd(h2, dh3, gate2, up2, g2, wg2, wu2, wd2, "ffn2_bwd", tm_bwd)
    gw["g2"] = _wgrad(dgate2, n3, "wgrad_gate2", tw)
    gw["u2"] = _wgrad(dup2, n3, "wgrad_up2", tw)
    gw["d2"] = _wgrad(act2, dhh3, "wgrad_down2", tw)
    stage1(["g2", "u2", "d2"], fk)
    dmix, dh2b = _mix_out_bwd(dh2, wout, "mix_out_bwd", tm)
    gw["out"] = _wgrad(mix, dh2b, "wgrad_out", tw)
    stage1(["out"], mixk)
    dq, dkp, dvp, dpc, dsink, dpw, dps, *r2 = _attn_pool_bwd(
        q, k, v, pc, dmix, sink_logits, pool_w[0], pool_scale, pband, ptband, "attn_pool_bwd",
        carry=stage2(["g2", "u2", "d2", "out"]))
    recv2.update(zip(["g2", "u2", "d2", "out"], r2))
    dh1, du, dgm = _mix_in_bwd(h1, dh2, gm, win_t, dq, dkp, dvp, dpc, tabs, "mix_in_bwd", tm)
    gw["in"] = _wgrad(du, n2, "wgrad_in", tw)
    stage1(["in"], ink)
    small_part = _pack_small(jnp.zeros_like(dgm), dgm, dg2, dgf, dsink[:, 0], dpw, dps, loss_part)
    dgate1, dup1, dhh1, small_all = _ffn_bwd_gates(dh1, gate1, up1, wd1, "ffn1_bwd_gates", tm,
                                                   carry=_AllGather([small_part]))
    *gw["g1"], recv2["in"] = _wgrad(dgate1, n1, "wgrad_gate1", tw, carry=stage2(["in"]))
    stage1(["g1"], fk)
    *gw["u1"], recv2["g1"] = _wgrad(dup1, n1, "wgrad_up1", tw, carry=stage2(["g1"]))
    stage1(["u1"], fk)
    *gw["d1"], recv2["u1"] = _wgrad(act1, dhh1, "wgrad_down1", tw, carry=stage2(["u1"]))
    stage1(["d1"], fk)
    dx, dg1, recv2["d1"] = _ffn_bwd_input(x0, dh1, dgate1, dup1, g1, wg1, wu1, "ffn1_bwd_input", tm,
                                          carry=stage2(["d1"]))

    (dg1_all,) = _run_exchange(_DirectGather([dg1.reshape(8, 128)]), "gather_norm1_grad")
    pk = lambda a, b, c_, e, s_, pw_, psc: _pack_small(a, b, c_, e, s_[0], pw_[0], psc)
    small_w = pk(ffn1_norm, mix_norm, ffn2_norm, final_norm, sink_logits, pool_w, pool_scale)
    small_m = pk(m_ffn1_norm, m_mix_norm, m_ffn2_norm, m_final_norm, m_sink_logits, m_pool_w, m_pool_scale)
    small_v = pk(v_ffn1_norm, v_mix_norm, v_ffn2_norm, v_final_norm, v_sink_logits, v_pool_w, v_pool_scale)
    sg, sd, sm, sv = [_unpack_small(a)
                      for a in _adam_small(small_w, small_all, dg1_all, small_m, small_v, "adam_small")]

    big = {}
    keys = ["g1", "u1", "d1", "g2", "u2", "d2", "in", "out"]
    names = ["ffn1_w_gate", "ffn1_w_up", "ffn1_w_down", "ffn2_w_gate", "ffn2_w_up", "ffn2_w_down", "w_in", "w_out"]
    transposed = [True, True, False, True, True, False, True, False]
    ws = [ffn1_w_gate, ffn1_w_up, ffn1_w_down, ffn2_w_gate, ffn2_w_up, ffn2_w_down, w_in, w_out]
    ms = [m_ffn1_w_gate, m_ffn1_w_up, m_ffn1_w_down, m_ffn2_w_gate, m_ffn2_w_up, m_ffn2_w_down, m_w_in, m_w_out]
    vs = [v_ffn1_w_gate, v_ffn1_w_up, v_ffn1_w_down, v_ffn2_w_gate, v_ffn2_w_up, v_ffn2_w_down, v_w_in, v_w_out]
    for key, nm, tr, w, m, vv in zip(keys, names, transposed, ws, ms, vs):
        view = (lambda a: jnp.swapaxes(a, 1, 2)[0]) if tr else (lambda a: a[0])
        back = (lambda a: jnp.swapaxes(a[None], 1, 2)) if tr else (lambda a: a[None])
        res = _rs_sum2_adam(pos, sum1[key][0], recv2[key], view(w), view(m), view(vv), "adam_" + nm)
        big[nm] = tuple(back(a) for a in res)

    loss = sg["loss"]
    all_names = ["ffn1_norm", "ffn1_w_gate", "ffn1_w_up", "ffn1_w_down", "mix_norm", "w_in", "sink_logits", "pool_w",
                 "pool_scale", "w_out", "ffn2_norm", "ffn2_w_gate", "ffn2_w_up", "ffn2_w_down", "final_norm"]
    outs = [loss, dx[None]]
    for idx, src in enumerate((sg, sd, sm, sv)):
        for nm in all_names:
            outs.append(big[nm][idx] if nm in big else src[nm])
    return tuple(outs)
```

```python
import functools

import jax
import jax.numpy as jnp
import numpy as np
from jax import lax
from jax.experimental import pallas as pl
from jax.experimental.pallas import tpu as pltpu

F32 = jnp.float32
BF16 = jnp.bfloat16
MESH = pl.DeviceIdType.MESH
N_DEV = 8

EPS = 1e-6
HEAD_DIM = 64
N_HEADS = 8
N_KV = 2
GROUP = N_HEADS // N_KV
ATTN_W = N_HEADS * HEAD_DIM
KV_W = N_KV * HEAD_DIM
POOL_W = 512
POOL_G = 4
POOL_GW = POOL_W // POOL_G
POOL_WINDOWS = (2, 4, 8, 16)
BLK = 128
ROT = 16
ROPE_THETA = 500000.0
SCORE_SCALE = HEAD_DIM ** -0.5

ADAM_LR, ADAM_B1, ADAM_B2, ADAM_EPS, ADAM_WD, ADAM_STEP = 0.001, 0.9, 0.999, 1e-08, 0.01, 10

VMEM_LIMIT = 56 * 1024 * 1024


def _cparams(sem=None, **kw):
    if sem is not None:
        kw["dimension_semantics"] = sem
    return pltpu.CompilerParams(vmem_limit_bytes=VMEM_LIMIT, **kw)


def _whole(shape):
    nd = len(shape)
    return pl.BlockSpec(shape, lambda *_: (0,) * nd, pipeline_mode=pl.Buffered(1))


def _sigmoid(z):
    return 1.0 / (1.0 + jnp.exp(-z))


def _dot_nt(a, b):
    return lax.dot_general(a, b, (((1,), (1,)), ((), ())), preferred_element_type=F32)


def _dot_nn(a, b):
    return lax.dot_general(a, b, (((1,), (0,)), ((), ())), preferred_element_type=F32)


def _dot_tn(a, b):
    return lax.dot_general(a, b, (((0,), (0,)), ((), ())), preferred_element_type=F32)


def _rope_tables(s):
    inv_freq = ROPE_THETA ** (-np.arange(0, ROT, 2, dtype=np.float64) / ROT)
    ang = np.arange(s, dtype=np.float64)[:, None] * inv_freq[None, :]
    c = np.ones((s, HEAD_DIM)); sa = np.zeros((s, HEAD_DIM)); sb = np.zeros((s, HEAD_DIM))
    c[:, :8] = np.cos(ang); c[:, 8:16] = np.cos(ang)
    sa[:, :8] = -np.sin(ang)
    sb[:, 8:16] = np.sin(ang)
    t = lambda a: jnp.asarray(np.tile(a, (1, 2)).astype(np.float32))
    return t(c), t(sa), t(sb)


def _pool_weight(gi, t, s_pos, s):
    half = POOL_WINDOWS[gi] // 2

    def win(lo, hi):
        a = np.clip(lo, 0, s); b = np.clip(hi + 1, 0, s)
        inside = (s_pos >= a) & (s_pos < b)
        return inside / np.maximum(b - a, 1)

    w = 0.5 * (win(t - half, t + half - 1) + win(t - half + 1, t + half)) - (t == s_pos)
    return w * ((t >= 0) & (t < s) & (s_pos >= 0) & (s_pos < s))


def _pool_tables(s):
    nb = s // BLK
    fwd = np.zeros((3, POOL_G, BLK, 3 * BLK), np.float32)
    bwd = np.zeros((3, POOL_G, BLK, 3 * BLK), np.float32)
    for vi, n in enumerate((0, 1 if nb > 2 else 0, nb - 1)):
        i = n * BLK + np.arange(BLK)[:, None]
        j = (n - 1) * BLK + np.arange(3 * BLK)[None, :]
        for gi in range(POOL_G):
            fwd[vi, gi] = _pool_weight(gi, i, j, s)
            bwd[vi, gi] = _pool_weight(gi, j, i, s)
    return jnp.asarray(fwd, dtype=BF16), jnp.asarray(bwd, dtype=BF16)


def _variant_index(n, nb):
    return jnp.where(n == 0, 0, jnp.where(n == nb - 1, 2, 1))


class _Exchange:
    inputs = ()
    out_shapes = ()
    sems = ()

    def start(self, srcs, outs, sems):
        raise NotImplementedError

    def middle(self, srcs, outs, sems):
        pass

    def finish(self, srcs, outs, sems):
        raise NotImplementedError


class _AllGather(_Exchange):
    def __init__(self, arrays):
        n = len(arrays)
        self.inputs = list(arrays)
        self.out_shapes = [jax.ShapeDtypeStruct((N_DEV,) + a.shape, a.dtype) for a in arrays]
        self.sems = [pltpu.SemaphoreType.DMA((n, 7)), pltpu.SemaphoreType.DMA((n, 7)), pltpu.SemaphoreType.DMA((n,))]

    def _parts(self, srcs, outs, sems):
        send_sems, recv_sems, local_sems = sems
        n = len(srcs)
        x, y, c = lax.axis_index("x"), lax.axis_index("y"), lax.axis_index("c")
        me, sibling = (x, y, c), (x, y, 1 - c)
        chips = [(1 - x, y), (x, 1 - y), (1 - x, 1 - y)]

        def slot(a, dev):
            return outs[a].at[4 * dev[0] + 2 * dev[1] + dev[2]]

        def copy(a, k, block, to, src=None):
            return pltpu.make_async_remote_copy(
                src_ref=slot(a, block) if src is None else src, dst_ref=slot(a, block),
                send_sem=send_sems.at[a, k], recv_sem=recv_sems.at[a, k], device_id=to, device_id_type=MESH)

        def mine():
            return [pltpu.make_async_copy(srcs[a], slot(a, me), local_sems.at[a]) for a in range(n)]

        def first():
            return [copy(a, 0, me, sibling, src=srcs[a]) for a in range(n)] + [
                copy(a, 1 + j, me, (*chip, c), src=srcs[a]) for a in range(n) for j, chip in enumerate(chips)]

        return n, c, me, sibling, chips, copy, mine, first

    def start(self, srcs, outs, sems):
        _, _, _, _, _, _, mine, first = self._parts(srcs, outs, sems)
        for cp in mine() + first():
            cp.start()

    def middle(self, srcs, outs, sems):
        n, c, me, sibling, chips, copy, _, _ = self._parts(srcs, outs, sems)
        for j, chip in enumerate(chips):
            for a in range(n):
                copy(a, 1 + j, (*chip, c), me).wait_recv()
                copy(a, 4 + j, (*chip, c), sibling).start()

    def finish(self, srcs, outs, sems):
        n, c, me, sibling, chips, copy, mine, first = self._parts(srcs, outs, sems)
        mine, first = mine(), first()
        passed = [copy(a, 4 + j, (*chip, c), sibling) for j, chip in enumerate(chips) for a in range(n)]
        for a in range(n):
            copy(a, 0, sibling, me).wait_recv()
            for j, chip in enumerate(chips):
                copy(a, 4 + j, (*chip, 1 - c), me).wait_recv()
        for cp in first + passed:
            cp.wait_send()
        for cp in mine:
            cp.wait()


class _RsStage2(_Exchange):
    def start(self, srcs, outs, sems):
        for cp in self._copies(srcs, outs, sems):
            cp.start()

    def finish(self, srcs, outs, sems):
        copies = self._copies(srcs, outs, sems)
        for cp in copies:
            cp.wait_recv()
        for cp in copies:
            cp.wait_send()


    def __init__(self, pbs):
        n = len(pbs)
        self.inputs = list(pbs)
        self.out_shapes = [jax.ShapeDtypeStruct((3,) + p.shape[1:], p.dtype) for p in pbs]
        self.sems = [pltpu.SemaphoreType.DMA((n, 3)), pltpu.SemaphoreType.DMA((n, 3))]

    def _copies(self, srcs, outs, sems):
        send_sems, recv_sems = sems
        x, y, c = lax.axis_index("x"), lax.axis_index("y"), lax.axis_index("c")
        chips = [(1 - x, y), (x, 1 - y), (1 - x, 1 - y)]
        return [pltpu.make_async_remote_copy(
            src_ref=srcs[a].at[2 * chip[0] + chip[1]], dst_ref=outs[a].at[j], send_sem=send_sems.at[a, j],
            recv_sem=recv_sems.at[a, j], device_id=(*chip, c), device_id_type=MESH)
            for a in range(len(srcs)) for j, chip in enumerate(chips)]


class _DirectGather(_Exchange):
    def __init__(self, arrays):
        n = len(arrays)
        self.inputs = list(arrays)
        self.out_shapes = [jax.ShapeDtypeStruct((N_DEV,) + a.shape, a.dtype) for a in arrays]
        self.sems = [pltpu.SemaphoreType.DMA((n, 7)), pltpu.SemaphoreType.DMA((n, 7)), pltpu.SemaphoreType.DMA((n,))]

    def _copies(self, srcs, outs, sems):
        send_sems, recv_sems, local_sems = sems
        x, y, c = lax.axis_index("x"), lax.axis_index("y"), lax.axis_index("c")
        me = 4 * x + 2 * y + c
        remote, local = [], []
        for a in range(len(srcs)):
            local.append(pltpu.make_async_copy(srcs[a], outs[a].at[me], local_sems.at[a]))
            for k in range(1, N_DEV):
                peer = (x ^ (k >> 2), y ^ ((k >> 1) & 1), c ^ (k & 1))
                remote.append(pltpu.make_async_remote_copy(
                    src_ref=srcs[a], dst_ref=outs[a].at[me], send_sem=send_sems.at[a, k - 1],
                    recv_sem=recv_sems.at[a, k - 1], device_id=peer, device_id_type=MESH))
        return remote, local

    def start(self, srcs, outs, sems):
        remote, local = self._copies(srcs, outs, sems)
        for cp in local + remote:
            cp.start()

    def finish(self, srcs, outs, sems):
        remote, local = self._copies(srcs, outs, sems)
        for cp in remote:
            cp.wait_recv()
        for cp in remote:
            cp.wait_send()
        for cp in local:
            cp.wait()


class _Both(_Exchange):
    def __init__(self, a, b):
        self.a, self.b = a, b
        self.inputs = list(a.inputs) + list(b.inputs)
        self.out_shapes = list(a.out_shapes) + list(b.out_shapes)
        self.sems = list(a.sems) + list(b.sems)

    def _split(self, srcs, outs, sems):
        na, oa, sa = len(self.a.inputs), len(self.a.out_shapes), len(self.a.sems)
        return (srcs[:na], outs[:oa], sems[:sa]), (srcs[na:], outs[oa:], sems[sa:])

    def start(self, srcs, outs, sems):
        pa, pb = self._split(srcs, outs, sems)
        self.a.start(*pa)
        self.b.start(*pb)

    def middle(self, srcs, outs, sems):
        pa, pb = self._split(srcs, outs, sems)
        self.a.middle(*pa)
        self.b.middle(*pb)

    def finish(self, srcs, outs, sems):
        pa, pb = self._split(srcs, outs, sems)
        self.a.finish(*pa)
        self.b.finish(*pb)


_ANY = pl.BlockSpec(memory_space=pl.ANY)


def _run_exchange(ex, name):
    n_in, n_out = len(ex.inputs), len(ex.out_shapes)

    def body(*refs):
        srcs, outs, sems = refs[:n_in], refs[n_in:n_in + n_out], refs[n_in + n_out:]
        ex.start(srcs, outs, sems)
        ex.middle(srcs, outs, sems)
        ex.finish(srcs, outs, sems)

    return pl.pallas_call(
        body, name=name, out_shape=list(ex.out_shapes), in_specs=[_ANY] * n_in, out_specs=[_ANY] * n_out,
        scratch_shapes=list(ex.sems),
    )(*ex.inputs)


CARRY_MIDDLE = 0.85


def _call(body, name, grid, in_specs, out_specs, out_shape, args, sem, carry=None, **kw):
    if carry is None:
        return pl.pallas_call(functools.partial(body), name=name, grid=grid, in_specs=in_specs, out_specs=out_specs,
                              out_shape=out_shape, compiler_params=_cparams(sem), **kw)(*args)
    n_in, n_out = len(in_specs), len(out_specs)
    nc_in, nc_out = len(carry.inputs), len(carry.out_shapes)

    def carried(*refs):
        ins = refs[:n_in]
        c_in = refs[n_in:n_in + nc_in]
        outs = refs[n_in + nc_in:n_in + nc_in + n_out]
        c_out = refs[n_in + nc_in + n_out:n_in + nc_in + n_out + nc_out]
        sems = refs[n_in + nc_in + n_out + nc_out:]
        ids = [pl.program_id(i) for i in range(len(grid))]
        is_first = functools.reduce(jnp.logical_and, [i == 0 for i in ids])
        is_last = functools.reduce(jnp.logical_and, [i == g - 1 for i, g in zip(ids, grid)])
        is_middle = functools.reduce(jnp.logical_and, [ids[0] == round(CARRY_MIDDLE * (grid[0] - 1))]
                                     + [i == 0 for i in ids[1:]])

        @pl.when(is_first)
        def _():
            carry.start(c_in, c_out, sems)

        @pl.when(is_middle)
        def _():
            carry.middle(c_in, c_out, sems)

        body(*ins, *outs)

        @pl.when(is_last)
        def _():
            carry.finish(c_in, c_out, sems)

    return pl.pallas_call(
        carried, name=name, grid=grid, in_specs=list(in_specs) + [_ANY] * nc_in,
        out_specs=list(out_specs) + [_ANY] * nc_out, out_shape=list(out_shape) + list(carry.out_shapes),
        scratch_shapes=list(carry.sems), compiler_params=_cparams(sem), **kw)(*args, *carry.inputs)


def _rs_stage1_sum(pos, gs, gbs, name):
    n = len(gs)
    _, rows, d = gs[0].shape

    def block(pc, q, other_core):
        return 4 * (q // 2) + 2 * (q % 2) + ((1 - pc) if other_core else pc)

    def body(pos_ref, *refs):
        g_refs, gb_refs = refs[:n], refs[n:2 * n]
        p_refs, pb_refs = refs[2 * n:3 * n], refs[3 * n:4 * n]
        land, send_sems, recv_sems = refs[4 * n:]
        q = pl.program_id(0)
        x, y, c = lax.axis_index("x"), lax.axis_index("y"), lax.axis_index("c")

        def copy(a, k):
            return pltpu.make_async_remote_copy(
                src_ref=gb_refs[a].at[block(c, k, True)], dst_ref=land.at[a, k],
                send_sem=send_sems.at[a, k], recv_sem=recv_sems.at[a, k], device_id=(x, y, 1 - c),
                device_id_type=MESH)

        @pl.when(q == 0)
        def _():
            for k in range(4):
                for a in range(n):
                    copy(a, k).start()

        for a in range(n):
            copy(a, q).wait_recv()
            p = g_refs[a][...] + land[a, q].astype(F32)
            p_refs[a][...] = p
            pb_refs[a][...] = p.astype(BF16)

        @pl.when(q == 3)
        def _():
            for k in range(4):
                for a in range(n):
                    copy(a, k).wait_send()

    blk = pl.BlockSpec((None, rows, d), lambda q, pos_ref: (q, 0, 0))
    mine = pl.BlockSpec((None, rows, d), lambda q, pos_ref: (block(pos_ref[2], q, False), 0, 0))
    res = pl.pallas_call(
        body, name=name,
        grid_spec=pltpu.PrefetchScalarGridSpec(
            num_scalar_prefetch=1, grid=(4,),
            in_specs=[mine] * n + [_ANY] * n, out_specs=[blk] * (2 * n),
            scratch_shapes=[pltpu.VMEM((n, 4, rows, d), BF16), pltpu.SemaphoreType.DMA((n, 4)),
                            pltpu.SemaphoreType.DMA((n, 4))]),
        out_shape=[jax.ShapeDtypeStruct((4, rows, d), F32)] * n + [jax.ShapeDtypeStruct((4, rows, d), BF16)] * n,
        compiler_params=_cparams(("arbitrary",)),
    )(pos, *gs, *gbs)
    return list(zip(res[:n], res[n:]))


def _rs_sum2_adam(pos, p, r2, w, m, v, name):
    _, rows, d = p.shape
    tr = rows // 2 if rows % 16 == 0 else rows

    def body(pos_ref, p_ref, r_ref, w_ref, m_ref, v_ref, g_ref, d_ref, nm_ref, nv_ref):
        r = r_ref[...].astype(F32)
        g = ((p_ref[...] + r[0]) + r[1]) + r[2]
        g_ref[...] = g
        d_ref[...], nm_ref[...], nv_ref[...] = _adam_math(w_ref[...], g, m_ref[...], v_ref[...])

    blk = pl.BlockSpec((tr, d), lambda i, pos_ref: (i, 0))
    return pl.pallas_call(
        body, name=name,
        grid_spec=pltpu.PrefetchScalarGridSpec(
            num_scalar_prefetch=1, grid=(rows // tr,),
            in_specs=[pl.BlockSpec((None, tr, d), lambda i, pos_ref: (2 * pos_ref[0] + pos_ref[1], i, 0)),
                      pl.BlockSpec((3, tr, d), lambda i, pos_ref: (0, i, 0)), blk, blk, blk],
            out_specs=[blk] * 4),
        out_shape=[jax.ShapeDtypeStruct((rows, d), F32)] * 4,
        compiler_params=_cparams(("arbitrary",)),
    )(pos, p, r2, w, m, v)


def _ffn_chunk(f):
    for cand in (256, 128):
        if f % cand == 0:
            return cand
    return f


def _loss_head(x, gg, target, loss_ref, dg_ref):
    @pl.when(pl.program_id(0) == 0)
    def _():
        loss_ref[...] = jnp.zeros_like(loss_ref)
        dg_ref[...] = jnp.zeros_like(dg_ref)

    r = lax.rsqrt(jnp.mean(x * x, axis=-1, keepdims=True) + EPS)
    xhat = x * r
    e = xhat * gg - target
    loss_ref[...] += 0.5 * jnp.sum(jnp.mean(e * e, axis=-1, keepdims=True), axis=0, keepdims=True)
    dy = e * (1.0 / x.shape[-1])
    dg_ref[...] += jnp.sum(dy * xhat, axis=0, keepdims=True)
    dxhat = dy * gg
    return r * (dxhat - xhat * jnp.mean(dxhat * xhat, axis=-1, keepdims=True))


def _ffn_fwd(h, g, wg_t, wu_t, wd, name, tm, carry=None, head=None):
    s, d = h.shape
    f = wd.shape[0]
    tf = _ffn_chunk(f)

    def body(h_ref, g_ref, wg_ref, wu_ref, wd_ref, *refs):
        if head is None:
            o_ref, n_ref, gate_ref, up_ref, act_ref = refs
        else:
            gf_ref, t_ref, o_ref, n_ref, gate_ref, up_ref, act_ref, loss_ref, dgf_ref = refs
        x = h_ref[...]
        r = lax.rsqrt(jnp.mean(x * x, axis=-1, keepdims=True) + EPS)
        nb = (x * r * g_ref[...]).astype(BF16)
        n_ref[...] = nb
        for j in range(f // tf):
            sl = slice(j * tf, (j + 1) * tf)
            gate = _dot_nt(nb, wg_ref[sl, :])
            up = _dot_nt(nb, wu_ref[sl, :])
            gate_ref[:, sl] = gate.astype(BF16)
            up_ref[:, sl] = up.astype(BF16)
            act_ref[:, sl] = (gate * _sigmoid(gate) * up).astype(BF16)
        h_out = x + 0.5 * _dot_nn(act_ref[...], wd_ref[...])
        o_ref[...] = h_out if head is None else _loss_head(h_out, gf_ref[...], t_ref[...], loss_ref, dgf_ref)

    row = lambda w: pl.BlockSpec((tm, w), lambda i: (i, 0))
    in_specs = [row(d), _whole((1, d)), _whole((f, d)), _whole((f, d)), _whole((f, d))]
    out_specs = [row(d), row(d), row(f), row(f), row(f)]
    out_shape = [jax.ShapeDtypeStruct((s, d), F32), jax.ShapeDtypeStruct((s, d), BF16),
                 jax.ShapeDtypeStruct((s, f), BF16), jax.ShapeDtypeStruct((s, f), BF16),
                 jax.ShapeDtypeStruct((s, f), BF16)]
    args = (h, g, wg_t, wu_t, wd)
    if head is not None:
        in_specs += [_whole((1, d)), row(d)]
        out_specs += [pl.BlockSpec((1, 1), lambda i: (0, 0)), pl.BlockSpec((1, d), lambda i: (0, 0))]
        out_shape += [jax.ShapeDtypeStruct((1, 1), F32), jax.ShapeDtypeStruct((1, d), F32)]
        args += tuple(head)
    return _call(body, name, (s // tm,), in_specs, out_specs, out_shape, args, ("arbitrary",), carry)


def _gate_grads(dh_ref, gate_ref, up_ref, wd_ref, dgate_ref, dup_ref, dhh_ref, tf):
    dhh = (0.5 * dh_ref[...]).astype(BF16)
    dhh_ref[...] = dhh
    for j in range(gate_ref.shape[1] // tf):
        sl = slice(j * tf, (j + 1) * tf)
        gt = gate_ref[:, sl].astype(F32)
        u = up_ref[:, sl].astype(F32)
        dact = _dot_nt(dhh, wd_ref[sl, :])
        sg = _sigmoid(gt)
        dup_ref[:, sl] = (dact * (gt * sg)).astype(BF16)
        dgate_ref[:, sl] = (dact * u * (sg * (1.0 + gt * (1.0 - sg)))).astype(BF16)


def _input_grad(h_ref, dh_ref, dgate_ref, dup_ref, g_ref, wg_ref, wu_ref, o_ref, dg_ref):
    x = h_ref[...]
    r = lax.rsqrt(jnp.mean(x * x, axis=-1, keepdims=True) + EPS)
    xhat = x * r
    dn = _dot_nn(dgate_ref[...], wg_ref[...]) + _dot_nn(dup_ref[...], wu_ref[...])
    dxhat = dn * g_ref[...]
    o_ref[...] = dh_ref[...] + r * (dxhat - xhat * jnp.mean(dxhat * xhat, axis=-1, keepdims=True))

    @pl.when(pl.program_id(0) == 0)
    def _():
        dg_ref[...] = jnp.zeros_like(dg_ref)

    dg_ref[...] += jnp.sum(dn * xhat, axis=0, keepdims=True)


def _ffn_bwd(h_in, dh_out, gate, up, g, wg_t, wu_t, wd, name, tm):
    s, d = h_in.shape
    f = gate.shape[1]
    tf = _ffn_chunk(f)

    def body(h_ref, dh_ref, gate_ref, up_ref, g_ref, wg_ref, wu_ref, wd_ref,
             o_ref, dg_ref, dgate_ref, dup_ref, dhh_ref):
        _gate_grads(dh_ref, gate_ref, up_ref, wd_ref, dgate_ref, dup_ref, dhh_ref, tf)
        _input_grad(h_ref, dh_ref, dgate_ref, dup_ref, g_ref, wg_ref, wu_ref, o_ref, dg_ref)

    row = lambda w: pl.BlockSpec((tm, w), lambda i: (i, 0))
    return pl.pallas_call(
        body, name=name, grid=(s // tm,),
        in_specs=[row(d), row(d), row(f), row(f), _whole((1, d)), _whole((f, d)), _whole((f, d)), _whole((f, d))],
        out_specs=[row(d), pl.BlockSpec((1, d), lambda i: (0, 0)), row(f), row(f), row(d)],
        out_shape=[jax.ShapeDtypeStruct((s, d), F32), jax.ShapeDtypeStruct((1, d), F32),
                   jax.ShapeDtypeStruct((s, f), BF16), jax.ShapeDtypeStruct((s, f), BF16),
                   jax.ShapeDtypeStruct((s, d), BF16)],
        compiler_params=_cparams(("arbitrary",)),
    )(h_in, dh_out, gate, up, g, wg_t, wu_t, wd)


def _ffn_bwd_gates(dh_out, gate, up, wd, name, tm, carry=None):
    s, d = dh_out.shape
    f = gate.shape[1]
    tf = _ffn_chunk(f)

    def body(dh_ref, gate_ref, up_ref, wd_ref, dgate_ref, dup_ref, dhh_ref):
        _gate_grads(dh_ref, gate_ref, up_ref, wd_ref, dgate_ref, dup_ref, dhh_ref, tf)

    row = lambda w: pl.BlockSpec((tm, w), lambda i: (i, 0))
    return _call(
        body, name, (s // tm,), [row(d), row(f), row(f), _whole((f, d))], [row(f), row(f), row(d)],
        [jax.ShapeDtypeStruct((s, f), BF16), jax.ShapeDtypeStruct((s, f), BF16), jax.ShapeDtypeStruct((s, d), BF16)],
        (dh_out, gate, up, wd), ("arbitrary",), carry)


def _ffn_bwd_input(h_in, dh_out, dgate, dup, g, wg_t, wu_t, name, tm, carry=None):
    s, d = h_in.shape
    f = dgate.shape[1]

    row = lambda w: pl.BlockSpec((tm, w), lambda i: (i, 0))
    return _call(
        _input_grad, name, (s // tm,),
        [row(d), row(d), row(f), row(f), _whole((1, d)), _whole((f, d)), _whole((f, d))],
        [row(d), pl.BlockSpec((1, d), lambda i: (0, 0))],
        [jax.ShapeDtypeStruct((s, d), F32), jax.ShapeDtypeStruct((1, d), F32)],
        (h_in, dh_out, dgate, dup, g, wg_t, wu_t), ("arbitrary",), carry)


def _wgrad(a, b, name, tf, carry=None):
    s, f = a.shape
    d = b.shape[1]

    def body(a_ref, b_ref, o_ref, ob_ref):
        acc = _dot_tn(a_ref[...], b_ref[...])
        o_ref[...] = acc
        ob_ref[...] = acc.astype(BF16)

    return _call(
        body, name, (f // tf,),
        [pl.BlockSpec((s, tf), lambda i: (0, i)), _whole((s, d))],
        [pl.BlockSpec((tf, d), lambda i: (i, 0)), pl.BlockSpec((tf, d), lambda i: (i, 0))],
        [jax.ShapeDtypeStruct((f, d), F32), jax.ShapeDtypeStruct((f, d), BF16)],
        (a, b), ("arbitrary",), carry)


def _rope(t, c, sa, sb, reps):
    c, sa, sb = (jnp.tile(v, (1, reps)) if reps > 1 else v for v in (c, sa, sb))
    w = t.shape[1]
    return t * c + pltpu.roll(t, w - 8, 1) * sa + pltpu.roll(t, 8, 1) * sb


def _rope_bwd(dt, c, sa, sb, reps):
    c, sa, sb = (jnp.tile(v, (1, reps)) if reps > 1 else v for v in (c, sa, sb))
    w = dt.shape[1]
    return dt * c + pltpu.roll(dt * sa, 8, 1) + pltpu.roll(dt * sb, w - 8, 1)


def _mix_in(h, g, win_t, tabs, name, tm):
    s, d = h.shape
    n_in = win_t.shape[0]

    def body(h_ref, g_ref, w_ref, c_ref, sa_ref, sb_ref, q_ref, k_ref, v_ref, pc_ref, n_ref):
        x = h_ref[...]
        r = lax.rsqrt(jnp.mean(x * x, axis=-1, keepdims=True) + EPS)
        nb = (x * r * g_ref[...]).astype(BF16)
        n_ref[...] = nb
        u = _dot_nt(nb, w_ref[...])
        c, sa, sb = c_ref[...], sa_ref[...], sb_ref[...]
        q_ref[...] = _rope(u[:, :ATTN_W], c, sa, sb, ATTN_W // 128).astype(BF16)
        k_ref[...] = _rope(u[:, ATTN_W:ATTN_W + KV_W], c, sa, sb, 1).astype(BF16)
        v_ref[...] = u[:, ATTN_W + KV_W:ATTN_W + 2 * KV_W].astype(BF16)
        pc_ref[...] = u[:, ATTN_W + 2 * KV_W:]

    row = lambda w: pl.BlockSpec((tm, w), lambda i: (i, 0))
    return pl.pallas_call(
        body, name=name, grid=(s // tm,),
        in_specs=[row(d), _whole((1, d)), _whole((n_in, d)), row(128), row(128), row(128)],
        out_specs=[row(ATTN_W), row(KV_W), row(KV_W), row(POOL_W), row(d)],
        out_shape=[jax.ShapeDtypeStruct((s, ATTN_W), BF16), jax.ShapeDtypeStruct((s, KV_W), BF16),
                   jax.ShapeDtypeStruct((s, KV_W), BF16), jax.ShapeDtypeStruct((s, POOL_W), F32),
                   jax.ShapeDtypeStruct((s, d), BF16)],
        compiler_params=_cparams(("arbitrary",)),
    )(h, g, win_t, *tabs)


def _band_mask(n, nb, transposed):
    shape = (3 * BLK, 2 * BLK) if transposed else (2 * BLK, 3 * BLK)
    i = lax.broadcasted_iota(jnp.int32, shape, 1 if transposed else 0) % BLK
    j = lax.broadcasted_iota(jnp.int32, shape, 0 if transposed else 1)
    kpos = (n - 1) * BLK + j
    return (j >= i) & (j <= i + 2 * BLK) & (kpos >= 0) & (kpos < nb * BLK)


def _block_diag(t, kh):
    tf = t.astype(F32)
    tr = pltpu.roll(tf, HEAD_DIM, 1)
    lo = lax.broadcasted_iota(jnp.int32, tf.shape, 1) < HEAD_DIM
    top, bot = (tf, tr) if kh == 0 else (tr, tf)
    return jnp.concatenate([jnp.where(lo, top, 0.0), jnp.where(lo, 0.0, bot)], axis=0).astype(BF16)


def _fold_diag(tbd):
    lo = lax.broadcasted_iota(jnp.int32, (3 * BLK, 2 * HEAD_DIM), 1) < HEAD_DIM
    t = jnp.where(lo, tbd[:3 * BLK], tbd[3 * BLK:])
    return t + pltpu.roll(t, HEAD_DIM, 1)


def _stack_pairs(x, kh):
    return jnp.concatenate([x[:, (2 * kh) * 128:(2 * kh + 1) * 128], x[:, (2 * kh + 1) * 128:(2 * kh + 2) * 128]], axis=0)


def _sink_of(sink_ref, kh, half, axis):
    shape = (2 * BLK, 1) if axis == 0 else (1, 2 * BLK)
    first = lax.broadcasted_iota(jnp.int32, shape, axis) < BLK
    return jnp.where(first, sink_ref[0, GROUP * kh + half], sink_ref[0, GROUP * kh + 2 + half])


def _softmax_sink(sc, valid, sink, axis):
    sc = jnp.where(valid, sc, -1e30)
    m = jnp.maximum(jnp.max(sc, axis=axis, keepdims=True), sink)
    e = jnp.exp(sc - m)
    es = jnp.exp(sink - m)
    inv = 1.0 / (jnp.sum(e, axis=axis, keepdims=True) + es)
    return e * inv, es * inv


def _attn_blocks_per_step(nb):
    return next(nq for nq in (4, 2, 1) if nb % nq == 0)


def _band_specs(nq, nb, w, col=0):
    return [pl.BlockSpec((BLK, w), lambda m: (jnp.maximum(nq * m - 1, 0), col)),
            pl.BlockSpec((nq * BLK, w), lambda m: (m, col)),
            pl.BlockSpec((BLK, w), lambda m: (jnp.minimum(nq * m + nq, nb - 1), col))]


def _attn_pool_fwd(q, k, v, pc, sink, pool_w, pool_scale, pband, name, carry=None):
    s = q.shape[0]
    nb = s // BLK

    nq = _attn_blocks_per_step(nb)

    def body(sink_ref, q_ref, k0, k1, k2, v0, v1, v2, p0, p1, p2, pw_ref, ps_ref, pb_ref, o_ref):
        kall = jnp.concatenate([k0[...], k1[...], k2[...]], axis=0)
        vall = jnp.concatenate([v0[...], v1[...], v2[...]], axis=0)
        pall = jnp.concatenate([p0[...], p1[...], p2[...]], axis=0).astype(BF16)
        qall = q_ref[...] * SCORE_SCALE
        for j in range(nq):
            n = pl.program_id(0) * nq + j
            rows, band = slice(j * BLK, (j + 1) * BLK), slice(j * BLK, (j + 3) * BLK)
            valid = _band_mask(n, nb, False)
            kb, vb, qs = kall[band], vall[band], qall[rows]
            for kh in range(N_KV):
                sc = _dot_nt(_stack_pairs(qs, kh), _block_diag(kb, kh))
                p = [_softmax_sink(sc[:, half * 3 * BLK:(half + 1) * 3 * BLK], valid,
                                   _sink_of(sink_ref, kh, half, 0), 1)[0] for half in range(2)]
                o2 = _dot_nn(jnp.concatenate(p, axis=1).astype(BF16), _block_diag(vb, kh)).astype(BF16)
                o_ref[rows, (2 * kh) * 128:(2 * kh + 1) * 128] = o2[:BLK]
                o_ref[rows, (2 * kh + 1) * 128:(2 * kh + 2) * 128] = o2[BLK:]
            ext = pall[band]
            var = _variant_index(n, nb)
            for gi in range(POOL_G):
                gsl = slice(gi * POOL_GW, (gi + 1) * POOL_GW)
                dg = _dot_nn(pb_ref[var, gi], ext[:, gsl])
                yg = _dot_nn(dg.astype(BF16), pw_ref[gi].astype(BF16))
                o_ref[rows, ATTN_W + gi * POOL_GW:ATTN_W + (gi + 1) * POOL_GW] = (yg * ps_ref[:, gsl]).astype(BF16)

    return _call(
        body, name, (nb // nq,),
        [pl.BlockSpec(memory_space=pltpu.SMEM), pl.BlockSpec((nq * BLK, ATTN_W), lambda m: (m, 0)),
         *_band_specs(nq, nb, KV_W), *_band_specs(nq, nb, KV_W), *_band_specs(nq, nb, POOL_W),
         _whole((POOL_G, POOL_GW, POOL_GW)), _whole((1, POOL_W)), _whole(pband.shape)],
        [pl.BlockSpec((nq * BLK, ATTN_W + POOL_W), lambda m: (m, 0))],
        [jax.ShapeDtypeStruct((s, ATTN_W + POOL_W), BF16)],
        (sink, q, k, k, k, v, v, v, pc, pc, pc, pool_w, pool_scale, pband), ("arbitrary",), carry)


def _attn_pool_bwd(q, k, v, pc, dmix, sink, pool_w, pool_scale, pband, ptband, name, carry=None):
    s = q.shape[0]
    nb = s // BLK
    nq = _attn_blocks_per_step(nb)

    def body(sink_ref, q_ref, k0, k1, k2, v0, v1, v2, p0, p1, p2, da_ref, d0, d1, d2, pw_ref, ps_ref, pb_ref, ptb_ref,
             dq_ref, dk_ref, dv_ref, dpc_ref, dsink_ref, dpw_ref, dps_ref):
        @pl.when(pl.program_id(0) == 0)
        def _():
            dsink_ref[...] = jnp.zeros_like(dsink_ref)
            dpw_ref[...] = jnp.zeros_like(dpw_ref)
            dps_ref[...] = jnp.zeros_like(dps_ref)

        kall = jnp.concatenate([k0[...], k1[...], k2[...]], axis=0)
        vall = jnp.concatenate([v0[...], v1[...], v2[...]], axis=0)
        pall = jnp.concatenate([p0[...], p1[...], p2[...]], axis=0).astype(BF16)
        dpall = jnp.concatenate([d0[...], d1[...], d2[...]], axis=0)
        lo = lax.broadcasted_iota(jnp.int32, (3 * BLK, KV_W), 1) < HEAD_DIM
        for j in range(nq):
            n = pl.program_id(0) * nq + j
            rows, band = slice(j * BLK, (j + 1) * BLK), slice(j * BLK, (j + 3) * BLK)
            valid = _band_mask(n, nb, True)
            kb, vb, qb = kall[band], vall[band], q_ref[rows, :]
            qs = qb * SCORE_SCALE
            da = da_ref[rows, :].astype(BF16)
            dk_fold, dv_fold = [], []
            for kh in range(N_KV):
                kbd, vbd = _block_diag(kb, kh), _block_diag(vb, kh)
                q2, do2 = _stack_pairs(qb, kh), _stack_pairs(da, kh)
                sc_t = _dot_nt(kbd, _stack_pairs(qs, kh))
                dp_t = _dot_nt(vbd, do2)
                p_t, ds_t = [], []
                for half in range(2):
                    keys = slice(half * 3 * BLK, (half + 1) * 3 * BLK)
                    p, ps = _softmax_sink(sc_t[keys], valid, _sink_of(sink_ref, kh, half, 1), 0)
                    delta = jnp.sum(p * dp_t[keys], axis=0, keepdims=True)
                    p_t.append(p.astype(BF16))
                    ds_t.append((p * (dp_t[keys] - delta)).astype(BF16))
                    dsk = -ps * delta
                    for pair in range(2):
                        h = GROUP * kh + 2 * pair + half
                        part = jnp.sum(dsk[:, pair * BLK:(pair + 1) * BLK], axis=1, keepdims=True)
                        dsink_ref[h:h + 1, :] += jnp.broadcast_to(part, (1, 128))
                p_t = jnp.concatenate(p_t, axis=0)
                ds_t = jnp.concatenate(ds_t, axis=0)
                dq2 = _dot_tn(ds_t, kbd) * SCORE_SCALE
                dq_ref[rows, (2 * kh) * 128:(2 * kh + 1) * 128] = dq2[:BLK]
                dq_ref[rows, (2 * kh + 1) * 128:(2 * kh + 2) * 128] = dq2[BLK:]
                dk_fold.append(_fold_diag(_dot_nn(ds_t, q2)) * SCORE_SCALE)
                dv_fold.append(_fold_diag(_dot_nn(p_t, do2)))
            dk_all = jnp.where(lo, dk_fold[0], dk_fold[1])
            dv_all = jnp.where(lo, dv_fold[0], dv_fold[1])
            for t in range(3):
                dk_ref[j, t] = dk_all[t * BLK:(t + 1) * BLK]
                dv_ref[j, t] = dv_all[t * BLK:(t + 1) * BLK]
            ext, dpe = pall[band], dpall[band]
            dpc_cur = dpall[(j + 1) * BLK:(j + 2) * BLK]
            var = _variant_index(n, nb)
            for gi in range(POOL_G):
                gsl = slice(gi * POOL_GW, (gi + 1) * POOL_GW)
                wg = pw_ref[gi].astype(BF16)
                sc = ps_ref[:, gsl]
                dgb = _dot_nn(pb_ref[var, gi], ext[:, gsl]).astype(BF16)
                yg = _dot_nn(dgb, wg)
                dps_ref[:, gsl] += jnp.sum(dpc_cur[:, gsl] * yg, axis=0, keepdims=True)
                dpw_ref[gi] += _dot_tn(dgb, (dpc_cur[:, gsl] * sc).astype(BF16))
                dd = _dot_nt((dpe[:, gsl] * sc).astype(BF16), wg)
                dpc_ref[rows, gsl] = _dot_nn(ptb_ref[var, gi], dd.astype(BF16))

    fixed = lambda shape: pl.BlockSpec(shape, lambda m: (0,) * len(shape))
    return _call(
        body, name, (nb // nq,),
        [pl.BlockSpec(memory_space=pltpu.SMEM), pl.BlockSpec((nq * BLK, ATTN_W), lambda m: (m, 0)),
         *_band_specs(nq, nb, KV_W), *_band_specs(nq, nb, KV_W), *_band_specs(nq, nb, POOL_W),
         pl.BlockSpec((nq * BLK, ATTN_W), lambda m: (m, 0)), *_band_specs(nq, nb, POOL_W, 1),
         _whole((POOL_G, POOL_GW, POOL_GW)), _whole((1, POOL_W)), _whole(pband.shape), _whole(ptband.shape)],
        [pl.BlockSpec((nq * BLK, ATTN_W), lambda m: (m, 0)),
         pl.BlockSpec((nq, 3, BLK, KV_W), lambda m: (m, 0, 0, 0)),
         pl.BlockSpec((nq, 3, BLK, KV_W), lambda m: (m, 0, 0, 0)),
         pl.BlockSpec((nq * BLK, POOL_W), lambda m: (m, 0)),
         fixed((N_HEADS, 128)), fixed((POOL_G, POOL_GW, POOL_GW)), fixed((1, POOL_W))],
        [jax.ShapeDtypeStruct((s, ATTN_W), F32), jax.ShapeDtypeStruct((nb, 3, BLK, KV_W), F32),
         jax.ShapeDtypeStruct((nb, 3, BLK, KV_W), F32), jax.ShapeDtypeStruct((s, POOL_W), F32),
         jax.ShapeDtypeStruct((N_HEADS, 128), F32),
         jax.ShapeDtypeStruct((POOL_G, POOL_GW, POOL_GW), F32), jax.ShapeDtypeStruct((1, POOL_W), F32)],
        (sink, q, k, k, k, v, v, v, pc, pc, pc, dmix, dmix, dmix, dmix, pool_w, pool_scale, pband, ptband),
        ("arbitrary",), carry)


def _mix_out(h, mix, w_out, name, tm):
    s, d = h.shape
    w = mix.shape[1]

    def body(h_ref, m_ref, w_ref, o_ref):
        o_ref[...] = h_ref[...] + _dot_nn(m_ref[...], w_ref[...])

    row = lambda c: pl.BlockSpec((tm, c), lambda i: (i, 0))
    return pl.pallas_call(
        body, name=name, grid=(s // tm,), in_specs=[row(d), row(w), _whole((w, d))], out_specs=row(d),
        out_shape=jax.ShapeDtypeStruct((s, d), F32), compiler_params=_cparams(("arbitrary",)),
    )(h, mix, w_out)


def _mix_out_bwd(dh, w_out, name, tm):
    s, d = dh.shape
    w = w_out.shape[0]

    def body(dh_ref, w_ref, o_ref, dhb_ref):
        dhb = dh_ref[...].astype(BF16)
        dhb_ref[...] = dhb
        o_ref[...] = _dot_nt(dhb, w_ref[...])

    row = lambda c: pl.BlockSpec((tm, c), lambda i: (i, 0))
    return pl.pallas_call(
        body, name=name, grid=(s // tm,), in_specs=[row(d), _whole((w, d))], out_specs=[row(w), row(d)],
        out_shape=[jax.ShapeDtypeStruct((s, w), F32), jax.ShapeDtypeStruct((s, d), BF16)],
        compiler_params=_cparams(("arbitrary",)),
    )(dh, w_out)


def _mix_in_bwd(h, dh, g, win_t, dq, dkp, dvp, dpc, tabs, name, tm):
    s, d = h.shape
    nb = s // BLK
    nt = tm // BLK
    n_in = win_t.shape[0]

    def band_sum(n, before, own, after, prev_last, next_first):
        lo = (n > 0).astype(F32)
        hi = (n < s // tm - 1).astype(F32)
        blocks = []
        for b in range(nt):
            from_prev = prev_last[...] * lo if b == 0 else before[b - 1]
            from_next = next_first[...] * hi if b == nt - 1 else after[b + 1]
            blocks.append(from_prev + own[b] + from_next)
        return jnp.concatenate(blocks, axis=0)

    def body(h_ref, dh_ref, g_ref, w_ref, dq_ref, k2, k1, k0, kp, kn, v2, v1, v0, vp, vn, dpc_ref, c_ref, sa_ref,
             sb_ref, o_ref, du_ref, dg_ref):
        n = pl.program_id(0)
        dk = band_sum(n, k2, k1, k0, kp, kn)
        dv = band_sum(n, v2, v1, v0, vp, vn)
        c, sa, sb = c_ref[...], sa_ref[...], sb_ref[...]
        du = jnp.concatenate([_rope_bwd(dq_ref[...], c, sa, sb, ATTN_W // 128), _rope_bwd(dk, c, sa, sb, 1), dv,
                              dpc_ref[...]], axis=1).astype(BF16)
        du_ref[...] = du
        dn = _dot_nn(du, w_ref[...])
        x = h_ref[...]
        r = lax.rsqrt(jnp.mean(x * x, axis=-1, keepdims=True) + EPS)
        xhat = x * r
        dxhat = dn * g_ref[...]
        o_ref[...] = dh_ref[...] + r * (dxhat - xhat * jnp.mean(dxhat * xhat, axis=-1, keepdims=True))

        @pl.when(n == 0)
        def _():
            dg_ref[...] = jnp.zeros_like(dg_ref)

        dg_ref[...] += jnp.sum(dn * xhat, axis=0, keepdims=True)

    row = lambda w: pl.BlockSpec((tm, w), lambda n: (n, 0))
    slot = lambda t: pl.BlockSpec((nt, None, BLK, KV_W), lambda n, t=t: (n, t, 0, 0))
    parts = [slot(2), slot(1), slot(0),
             pl.BlockSpec((None, None, BLK, KV_W), lambda n: (jnp.maximum(nt * n - 1, 0), 2, 0, 0)),
             pl.BlockSpec((None, None, BLK, KV_W), lambda n: (jnp.minimum(nt * n + nt, nb - 1), 0, 0, 0))]
    return pl.pallas_call(
        body, name=name, grid=(s // tm,),
        in_specs=[row(d), row(d), _whole((1, d)), _whole((n_in, d)), row(ATTN_W), *parts, *parts, row(POOL_W),
                  row(128), row(128), row(128)],
        out_specs=[row(d), row(n_in), pl.BlockSpec((1, d), lambda n: (0, 0))],
        out_shape=[jax.ShapeDtypeStruct((s, d), F32), jax.ShapeDtypeStruct((s, n_in), BF16),
                   jax.ShapeDtypeStruct((1, d), F32)],
        compiler_params=_cparams(("arbitrary",)),
    )(h, dh, g, win_t, dq, *[dkp] * 5, *[dvp] * 5, dpc, *tabs)


def _adam_math(w, g, m, v):
    m = ADAM_B1 * m + (1.0 - ADAM_B1) * g
    v = ADAM_B2 * v + (1.0 - ADAM_B2) * (g * g)
    m_hat = m / (1.0 - ADAM_B1 ** ADAM_STEP)
    v_hat = v / (1.0 - ADAM_B2 ** ADAM_STEP)
    delta = -ADAM_LR * (m_hat / (jnp.sqrt(v_hat) + ADAM_EPS) + ADAM_WD * w)
    return delta, m, v


def _adam_small(w, parts, late, m, v, name):
    rows, cols = w.shape

    def body(w_ref, p_ref, l_ref, m_ref, v_ref, g_ref, d_ref, nm_ref, nv_ref):
        g, gl = p_ref[0], l_ref[0]
        for k in range(1, N_DEV):
            g = g + p_ref[k]
            gl = gl + l_ref[k]
        g_ref[...] = g
        g_ref[SMALL_NORM1:SMALL_NORM1 + 8, :] = g[SMALL_NORM1:SMALL_NORM1 + 8] + gl
        d_ref[...], nm_ref[...], nv_ref[...] = _adam_math(w_ref[...], g_ref[...], m_ref[...], v_ref[...])

    return pl.pallas_call(
        body, name=name, out_shape=[jax.ShapeDtypeStruct((rows, cols), F32)] * 4,
    )(w, parts, late, m, v)


SMALL_NORM1 = 512


def _pack_small(norm1, normm, norm2, normf, sink, pool_w, pool_scale, loss=None):
    scale_rows = jnp.pad(pool_scale.reshape(4, 128), ((0, 4), (0, 0)))
    last_rows = jnp.pad(sink.reshape(1, N_HEADS), ((0, 7), (0, 128 - N_HEADS)))
    if loss is not None:
        last_rows = last_rows + jnp.pad(loss.reshape(1, 1), ((1, 6), (0, 127)))
    return jnp.concatenate([pool_w.reshape(512, 128), norm1.reshape(8, 128), normm.reshape(8, 128),
                            norm2.reshape(8, 128), normf.reshape(8, 128), scale_rows, last_rows], axis=0)


def _unpack_small(p):
    return dict(pool_w=p[:512].reshape(1, POOL_G, POOL_GW, POOL_GW), ffn1_norm=p[512:520].reshape(1, 1024),
                mix_norm=p[520:528].reshape(1, 1024), ffn2_norm=p[528:536].reshape(1, 1024),
                final_norm=p[536:544].reshape(1024), pool_scale=p[544:548].reshape(1, POOL_W),
                sink_logits=p[552, :N_HEADS].reshape(1, N_HEADS), loss=p[553, 0])


def kernel(x, ffn1_norm, ffn1_w_gate, ffn1_w_up, ffn1_w_down, mix_norm, w_in, sink_logits, pool_w, pool_scale, w_out, ffn2_norm, ffn2_w_gate, ffn2_w_up, ffn2_w_down, final_norm, loss_target, m_ffn1_norm, m_ffn1_w_gate, m_ffn1_w_up, m_ffn1_w_down, m_mix_norm, m_w_in, m_sink_logits, m_pool_w, m_pool_scale, m_w_out, m_ffn2_norm, m_ffn2_w_gate, m_ffn2_w_up, m_ffn2_w_down, m_final_norm, v_ffn1_norm, v_ffn1_w_gate, v_ffn1_w_up, v_ffn1_w_down, v_mix_norm, v_w_in, v_sink_logits, v_pool_w, v_pool_scale, v_w_out, v_ffn2_norm, v_ffn2_w_gate, v_ffn2_w_up, v_ffn2_w_down, v_final_norm):
    s, d = x.shape[1], x.shape[2]
    fk = ffn1_w_gate.shape[2]
    f = N_DEV * fk
    ink = w_in.shape[2]
    n_in = N_DEV * ink
    mixk = w_out.shape[1]
    tm = min(512, s)
    tm_bwd = min(256, s)
    pos = jnp.stack([lax.axis_index("x"), lax.axis_index("y"), lax.axis_index("c")]).astype(jnp.int32)

    t_bf = lambda w: w[0].T.astype(BF16)
    full = lambda a: a.reshape(N_DEV * a.shape[1], d)
    first = [t_bf(ffn1_w_gate), t_bf(ffn1_w_up), ffn1_w_down[0].astype(BF16)]
    wg1, wu1, wd1 = map(full, _run_exchange(_AllGather(first), "gather_ffn1"))
    second = _AllGather([t_bf(w_in), w_out[0].astype(BF16), t_bf(ffn2_w_gate)])
    third = _AllGather([t_bf(ffn2_w_up), ffn2_w_down[0].astype(BF16)])

    tabs = _rope_tables(s)
    pband, ptband = _pool_tables(s)
    g1, gm, g2, gf = ffn1_norm, mix_norm, ffn2_norm, final_norm.reshape(1, d)

    x0 = x[0]
    h1, n1, gate1, up1, act1, *gathered = _ffn_fwd(x0, g1, wg1, wu1, wd1, "ffn1_fwd", tm, carry=second)
    win_t, wout, wg2 = map(full, gathered)
    q, k, v, pc, n2 = _mix_in(h1, gm, win_t, tabs, "mix_in", tm)
    mix, *gathered = _attn_pool_fwd(q, k, v, pc, sink_logits, pool_w[0], pool_scale, pband, "attn_pool_fwd",
                                    carry=third)
    wu2, wd2 = map(full, gathered)
    h2 = _mix_out(h1, mix, wout, "mix_out", tm)
    dh3, n3, gate2, up2, act2, loss_part, dgf = _ffn_fwd(h2, g2, wg2, wu2, wd2, "ffn2_fwd", tm,
                                                         head=(gf, loss_target[0]))

    tw = 256 if f % 256 == 0 else 128
    gw, sum1, recv2 = {}, {}, {}

    def stage1(keys, rows):
        sums = _rs_stage1_sum(pos, [gw[key][0].reshape(N_DEV, rows, d) for key in keys],
                              [gw[key][1].reshape(N_DEV, rows, d) for key in keys], "rs1_" + keys[0])
        sum1.update(zip(keys, sums))

    def stage2(keys):
        return _RsStage2([sum1[key][1] for key in keys])

    dh2, dg2, dgate2, dup2, dhh3 = _ffn_bwd(h2, dh3, gate2, up2, g2, wg2, wu2, wd2, "ffn2_bwd", tm_bwd)
    gw["g2"] = _wgrad(dgate2, n3, "wgrad_gate2", tw)
    gw["u2"] = _wgrad(dup2, n3, "wgrad_up2", tw)
    gw["d2"] = _wgrad(act2, dhh3, "wgrad_down2", tw)
    stage1(["g2", "u2", "d2"], fk)
    dmix, dh2b = _mix_out_bwd(dh2, wout, "mix_out_bwd", tm)
    gw["out"] = _wgrad(mix, dh2b, "wgrad_out", tw)
    stage1(["out"], mixk)
    dq, dkp, dvp, dpc, dsink, dpw, dps, *r2 = _attn_pool_bwd(
        q, k, v, pc, dmix, sink_logits, pool_w[0], pool_scale, pband, ptband, "attn_pool_bwd",
        carry=stage2(["g2", "u2", "d2", "out"]))
    recv2.update(zip(["g2", "u2", "d2", "out"], r2))
    dh1, du, dgm = _mix_in_bwd(h1, dh2, gm, win_t, dq, dkp, dvp, dpc, tabs, "mix_in_bwd", tm)
    gw["in"] = _wgrad(du, n2, "wgrad_in", tw)
    stage1(["in"], ink)
    small_part = _pack_small(jnp.zeros_like(dgm), dgm, dg2, dgf, dsink[:, 0], dpw, dps, loss_part)
    dgate1, dup1, dhh1, small_all = _ffn_bwd_gates(dh1, gate1, up1, wd1, "ffn1_bwd_gates", tm,
                                                   carry=_AllGather([small_part]))
    *gw["g1"], recv2["in"] = _wgrad(dgate1, n1, "wgrad_gate1", tw, carry=stage2(["in"]))
    stage1(["g1"], fk)
    *gw["u1"], recv2["g1"] = _wgrad(dup1, n1, "wgrad_up1", tw, carry=stage2(["g1"]))
    stage1(["u1"], fk)
    *gw["d1"], recv2["u1"] = _wgrad(act1, dhh1, "wgrad_down1", tw, carry=stage2(["u1"]))
    stage1(["d1"], fk)
    dx, dg1, recv2["d1"] = _ffn_bwd_input(x0, dh1, dgate1, dup1, g1, wg1, wu1, "ffn1_bwd_input", tm,
                                          carry=stage2(["d1"]))

    (dg1_all,) = _run_exchange(_DirectGather([dg1.reshape(8, 128)]), "gather_norm1_grad")
    pk = lambda a, b, c_, e, s_, pw_, psc: _pack_small(a, b, c_, e, s_[0], pw_[0], psc)
    small_w = pk(ffn1_norm, mix_norm, ffn2_norm, final_norm, sink_logits, pool_w, pool_scale)
    small_m = pk(m_ffn1_norm, m_mix_norm, m_ffn2_norm, m_final_norm, m_sink_logits, m_pool_w, m_pool_scale)
    small_v = pk(v_ffn1_norm, v_mix_norm, v_ffn2_norm, v_final_norm, v_sink_logits, v_pool_w, v_pool_scale)
    sg, sd, sm, sv = [_unpack_small(a)
                      for a in _adam_small(small_w, small_all, dg1_all, small_m, small_v, "adam_small")]

    big = {}
    keys = ["g1", "u1", "d1", "g2", "u2", "d2", "in", "out"]
    names = ["ffn1_w_gate", "ffn1_w_up", "ffn1_w_down", "ffn2_w_gate", "ffn2_w_up", "ffn2_w_down", "w_in", "w_out"]
    transposed = [True, True, False, True, True, False, True, False]
    ws = [ffn1_w_gate, ffn1_w_up, ffn1_w_down, ffn2_w_gate, ffn2_w_up, ffn2_w_down, w_in, w_out]
    ms = [m_ffn1_w_gate, m_ffn1_w_up, m_ffn1_w_down, m_ffn2_w_gate, m_ffn2_w_up, m_ffn2_w_down, m_w_in, m_w_out]
    vs = [v_ffn1_w_gate, v_ffn1_w_up, v_ffn1_w_down, v_ffn2_w_gate, v_ffn2_w_up, v_ffn2_w_down, v_w_in, v_w_out]
    for key, nm, tr, w, m, vv in zip(keys, names, transposed, ws, ms, vs):
        view = (lambda a: jnp.swapaxes(a, 1, 2)[0]) if tr else (lambda a: a[0])
        back = (lambda a: jnp.swapaxes(a[None], 1, 2)) if tr else (lambda a: a[None])
        res = _rs_sum2_adam(pos, sum1[key][0], recv2[key], view(w), view(m), view(vv), "adam_" + nm)
        big[nm] = tuple(back(a) for a in res)

    loss = sg["loss"]
    all_names = ["ffn1_norm", "ffn1_w_gate", "ffn1_w_up", "ffn1_w_down", "mix_norm", "w_in", "sink_logits", "pool_w",
                 "pool_scale", "w_out", "ffn2_norm", "ffn2_w_gate", "ffn2_w_up", "ffn2_w_down", "final_norm"]
    outs = [loss, dx[None]]
    for idx, src in enumerate((sg, sd, sm, sv)):
        for nm in all_names:
            outs.append(big[nm][idx] if nm in big else src[nm])
    return tuple(outs)
```

```python
import functools

import jax
import jax.numpy as jnp
import numpy as np
from jax import lax
from jax.experimental import pallas as pl
from jax.experimental.pallas import tpu as pltpu

F32 = jnp.float32
BF16 = jnp.bfloat16
MESH = pl.DeviceIdType.MESH
N_DEV = 8

EPS = 1e-6
HEAD_DIM = 64
N_HEADS = 8
N_KV = 2
GROUP = N_HEADS // N_KV
ATTN_W = N_HEADS * HEAD_DIM
KV_W = N_KV * HEAD_DIM
POOL_W = 512
POOL_G = 4
POOL_GW = POOL_W // POOL_G
POOL_WINDOWS = (2, 4, 8, 16)
BLK = 128
ROT = 16
ROPE_THETA = 500000.0
SCORE_SCALE = HEAD_DIM ** -0.5

ADAM_LR, ADAM_B1, ADAM_B2, ADAM_EPS, ADAM_WD, ADAM_STEP = 0.001, 0.9, 0.999, 1e-08, 0.01, 10

VMEM_LIMIT = 56 * 1024 * 1024


def _cparams(sem=None, **kw):
    if sem is not None:
        kw["dimension_semantics"] = sem
    return pltpu.CompilerParams(vmem_limit_bytes=VMEM_LIMIT, **kw)


def _whole(shape):
    nd = len(shape)
    return pl.BlockSpec(shape, lambda *_: (0,) * nd, pipeline_mode=pl.Buffered(1))


def _sigmoid(z):
    return 1.0 / (1.0 + jnp.exp(-z))


def _dot_nt(a, b):
    return lax.dot_general(a, b, (((1,), (1,)), ((), ())), preferred_element_type=F32)


def _dot_nn(a, b):
    return lax.dot_general(a, b, (((1,), (0,)), ((), ())), preferred_element_type=F32)


def _dot_tn(a, b):
    return lax.dot_general(a, b, (((0,), (0,)), ((), ())), preferred_element_type=F32)


def _rope_tables(s):
    inv_freq = ROPE_THETA ** (-np.arange(0, ROT, 2, dtype=np.float64) / ROT)
    ang = np.arange(s, dtype=np.float64)[:, None] * inv_freq[None, :]
    c = np.ones((s, HEAD_DIM)); sa = np.zeros((s, HEAD_DIM)); sb = np.zeros((s, HEAD_DIM))
    c[:, :8] = np.cos(ang); c[:, 8:16] = np.cos(ang)
    sa[:, :8] = -np.sin(ang)
    sb[:, 8:16] = np.sin(ang)
    t = lambda a: jnp.asarray(np.tile(a, (1, 2)).astype(np.float32))
    return t(c), t(sa), t(sb)


def _pool_weight(gi, t, s_pos, s):
    half = POOL_WINDOWS[gi] // 2

    def win(lo, hi):
        a = np.clip(lo, 0, s); b = np.clip(hi + 1, 0, s)
        inside = (s_pos >= a) & (s_pos < b)
        return inside / np.maximum(b - a, 1)

    w = 0.5 * (win(t - half, t + half - 1) + win(t - half + 1, t + half)) - (t == s_pos)
    return w * ((t >= 0) & (t < s) & (s_pos >= 0) & (s_pos < s))


def _pool_tables(s):
    nb = s // BLK
    fwd = np.zeros((3, POOL_G, BLK, 3 * BLK), np.float32)
    bwd = np.zeros((3, POOL_G, BLK, 3 * BLK), np.float32)
    for vi, n in enumerate((0, 1 if nb > 2 else 0, nb - 1)):
        i = n * BLK + np.arange(BLK)[:, None]
        j = (n - 1) * BLK + np.arange(3 * BLK)[None, :]
        for gi in range(POOL_G):
            fwd[vi, gi] = _pool_weight(gi, i, j, s)
            bwd[vi, gi] = _pool_weight(gi, j, i, s)
    return jnp.asarray(fwd, dtype=BF16), jnp.asarray(bwd, dtype=BF16)


def _variant_index(n, nb):
    return jnp.where(n == 0, 0, jnp.where(n == nb - 1, 2, 1))


class _Exchange:
    inputs = ()
    out_shapes = ()
    sems = ()

    def start(self, srcs, outs, sems):
        raise NotImplementedError

    def middle(self, srcs, outs, sems):
        pass

    def finish(self, srcs, outs, sems):
        raise NotImplementedError


class _AllGather(_Exchange):
    def __init__(self, arrays):
        n = len(arrays)
        self.inputs = list(arrays)
        self.out_shapes = [jax.ShapeDtypeStruct((N_DEV,) + a.shape, a.dtype) for a in arrays]
        self.sems = [pltpu.SemaphoreType.DMA((n, 7)), pltpu.SemaphoreType.DMA((n, 7)), pltpu.SemaphoreType.DMA((n,))]

    def _parts(self, srcs, outs, sems):
        send_sems, recv_sems, local_sems = sems
        n = len(srcs)
        x, y, c = lax.axis_index("x"), lax.axis_index("y"), lax.axis_index("c")
        me, sibling = (x, y, c), (x, y, 1 - c)
        chips = [(1 - x, y), (x, 1 - y), (1 - x, 1 - y)]

        def slot(a, dev):
            return outs[a].at[4 * dev[0] + 2 * dev[1] + dev[2]]

        def copy(a, k, block, to, src=None):
            return pltpu.make_async_remote_copy(
                src_ref=slot(a, block) if src is None else src, dst_ref=slot(a, block),
                send_sem=send_sems.at[a, k], recv_sem=recv_sems.at[a, k], device_id=to, device_id_type=MESH)

        def mine():
            return [pltpu.make_async_copy(srcs[a], slot(a, me), local_sems.at[a]) for a in range(n)]

        def first():
            return [copy(a, 0, me, sibling, src=srcs[a]) for a in range(n)] + [
                copy(a, 1 + j, me, (*chip, c), src=srcs[a]) for a in range(n) for j, chip in enumerate(chips)]

        return n, c, me, sibling, chips, copy, mine, first

    def start(self, srcs, outs, sems):
        _, _, _, _, _, _, mine, first = self._parts(srcs, outs, sems)
        for cp in mine() + first():
            cp.start()

    def middle(self, srcs, outs, sems):
        n, c, me, sibling, chips, copy, _, _ = self._parts(srcs, outs, sems)
        for j, chip in enumerate(chips):
            for a in range(n):
                copy(a, 1 + j, (*chip, c), me).wait_recv()
                copy(a, 4 + j, (*chip, c), sibling).start()

    def finish(self, srcs, outs, sems):
        n, c, me, sibling, chips, copy, mine, first = self._parts(srcs, outs, sems)
        mine, first = mine(), first()
        passed = [copy(a, 4 + j, (*chip, c), sibling) for j, chip in enumerate(chips) for a in range(n)]
        for a in range(n):
            copy(a, 0, sibling, me).wait_recv()
            for j, chip in enumerate(chips):
                copy(a, 4 + j, (*chip, 1 - c), me).wait_recv()
        for cp in first + passed:
            cp.wait_send()
        for cp in mine:
            cp.wait()


class _RsStage2(_Exchange):
    def start(self, srcs, outs, sems):
        for cp in self._copies(srcs, outs, sems):
            cp.start()

    def finish(self, srcs, outs, sems):
        copies = self._copies(srcs, outs, sems)
        for cp in copies:
            cp.wait_recv()
        for cp in copies:
            cp.wait_send()


    def __init__(self, pbs):
        n = len(pbs)
        self.inputs = list(pbs)
        self.out_shapes = [jax.ShapeDtypeStruct((3,) + p.shape[1:], p.dtype) for p in pbs]
        self.sems = [pltpu.SemaphoreType.DMA((n, 3)), pltpu.SemaphoreType.DMA((n, 3))]

    def _copies(self, srcs, outs, sems):
        send_sems, recv_sems = sems
        x, y, c = lax.axis_index("x"), lax.axis_index("y"), lax.axis_index("c")
        chips = [(1 - x, y), (x, 1 - y), (1 - x, 1 - y)]
        return [pltpu.make_async_remote_copy(
            src_ref=srcs[a].at[2 * chip[0] + chip[1]], dst_ref=outs[a].at[j], send_sem=send_sems.at[a, j],
            recv_sem=recv_sems.at[a, j], device_id=(*chip, c), device_id_type=MESH)
            for a in range(len(srcs)) for j, chip in enumerate(chips)]


class _DirectGather(_Exchange):
    def __init__(self, arrays):
        n = len(arrays)
        self.inputs = list(arrays)
        self.out_shapes = [jax.ShapeDtypeStruct((N_DEV,) + a.shape, a.dtype) for a in arrays]
        self.sems = [pltpu.SemaphoreType.DMA((n, 7)), pltpu.SemaphoreType.DMA((n, 7)), pltpu.SemaphoreType.DMA((n,))]

    def _copies(self, srcs, outs, sems):
        send_sems, recv_sems, local_sems = sems
        x, y, c = lax.axis_index("x"), lax.axis_index("y"), lax.axis_index("c")
        me = 4 * x + 2 * y + c
        remote, local = [], []
        for a in range(len(srcs)):
            local.append(pltpu.make_async_copy(srcs[a], outs[a].at[me], local_sems.at[a]))
            for k in range(1, N_DEV):
                peer = (x ^ (k >> 2), y ^ ((k >> 1) & 1), c ^ (k & 1))
                remote.append(pltpu.make_async_remote_copy(
                    src_ref=srcs[a], dst_ref=outs[a].at[me], send_sem=send_sems.at[a, k - 1],
                    recv_sem=recv_sems.at[a, k - 1], device_id=peer, device_id_type=MESH))
        return remote, local

    def start(self, srcs, outs, sems):
        remote, local = self._copies(srcs, outs, sems)
        for cp in local + remote:
            cp.start()

    def finish(self, srcs, outs, sems):
        remote, local = self._copies(srcs, outs, sems)
        for cp in remote:
            cp.wait_recv()
        for cp in remote:
            cp.wait_send()
        for cp in local:
            cp.wait()


class _Both(_Exchange):
    def __init__(self, a, b):
        self.a, self.b = a, b
        self.inputs = list(a.inputs) + list(b.inputs)
        self.out_shapes = list(a.out_shapes) + list(b.out_shapes)
        self.sems = list(a.sems) + list(b.sems)

    def _split(self, srcs, outs, sems):
        na, oa, sa = len(self.a.inputs), len(self.a.out_shapes), len(self.a.sems)
        return (srcs[:na], outs[:oa], sems[:sa]), (srcs[na:], outs[oa:], sems[sa:])

    def start(self, srcs, outs, sems):
        pa, pb = self._split(srcs, outs, sems)
        self.a.start(*pa)
        self.b.start(*pb)

    def middle(self, srcs, outs, sems):
        pa, pb = self._split(srcs, outs, sems)
        self.a.middle(*pa)
        self.b.middle(*pb)

    def finish(self, srcs, outs, sems):
        pa, pb = self._split(srcs, outs, sems)
        self.a.finish(*pa)
        self.b.finish(*pb)


_ANY = pl.BlockSpec(memory_space=pl.ANY)


def _run_exchange(ex, name):
    n_in, n_out = len(ex.inputs), len(ex.out_shapes)

    def body(*refs):
        srcs, outs, sems = refs[:n_in], refs[n_in:n_in + n_out], refs[n_in + n_out:]
        ex.start(srcs, outs, sems)
        ex.middle(srcs, outs, sems)
        ex.finish(srcs, outs, sems)

    return pl.pallas_call(
        body, name=name, out_shape=list(ex.out_shapes), in_specs=[_ANY] * n_in, out_specs=[_ANY] * n_out,
        scratch_shapes=list(ex.sems),
    )(*ex.inputs)


CARRY_MIDDLE = 0.85


def _call(body, name, grid, in_specs, out_specs, out_shape, args, sem, carry=None, **kw):
    if carry is None:
        return pl.pallas_call(functools.partial(body), name=name, grid=grid, in_specs=in_specs, out_specs=out_specs,
                              out_shape=out_shape, compiler_params=_cparams(sem), **kw)(*args)
    n_in, n_out = len(in_specs), len(out_specs)
    nc_in, nc_out = len(carry.inputs), len(carry.out_shapes)

    def carried(*refs):
        ins = refs[:n_in]
        c_in = refs[n_in:n_in + nc_in]
        outs = refs[n_in + nc_in:n_in + nc_in + n_out]
        c_out = refs[n_in + nc_in + n_out:n_in + nc_in + n_out + nc_out]
        sems = refs[n_in + nc_in + n_out + nc_out:]
        ids = [pl.program_id(i) for i in range(len(grid))]
        is_first = functools.reduce(jnp.logical_and, [i == 0 for i in ids])
        is_last = functools.reduce(jnp.logical_and, [i == g - 1 for i, g in zip(ids, grid)])
        is_middle = functools.reduce(jnp.logical_and, [ids[0] == round(CARRY_MIDDLE * (grid[0] - 1))]
                                     + [i == 0 for i in ids[1:]])

        @pl.when(is_first)
        def _():
            carry.start(c_in, c_out, sems)

        @pl.when(is_middle)
        def _():
            carry.middle(c_in, c_out, sems)

        body(*ins, *outs)

        @pl.when(is_last)
        def _():
            carry.finish(c_in, c_out, sems)

    return pl.pallas_call(
        carried, name=name, grid=grid, in_specs=list(in_specs) + [_ANY] * nc_in,
        out_specs=list(out_specs) + [_ANY] * nc_out, out_shape=list(out_shape) + list(carry.out_shapes),
        scratch_shapes=list(carry.sems), compiler_params=_cparams(sem), **kw)(*args, *carry.inputs)


def _rs_stage1_sum(pos, gs, gbs, name):
    n = len(gs)
    _, rows, d = gs[0].shape

    def block(pc, q, other_core):
        return 4 * (q // 2) + 2 * (q % 2) + ((1 - pc) if other_core else pc)

    def body(pos_ref, *refs):
        g_refs, gb_refs = refs[:n], refs[n:2 * n]
        p_refs, pb_refs = refs[2 * n:3 * n], refs[3 * n:4 * n]
        land, send_sems, recv_sems = refs[4 * n:]
        q = pl.program_id(0)
        x, y, c = lax.axis_index("x"), lax.axis_index("y"), lax.axis_index("c")

        def copy(a, k):
            return pltpu.make_async_remote_copy(
                src_ref=gb_refs[a].at[block(c, k, True)], dst_ref=land.at[a, k],
                send_sem=send_sems.at[a, k], recv_sem=recv_sems.at[a, k], device_id=(x, y, 1 - c),
                device_id_type=MESH)

        @pl.when(q == 0)
        def _():
            for k in range(4):
                for a in range(n):
                    copy(a, k).start()

        for a in range(n):
            copy(a, q).wait_recv()
            p = g_refs[a][...] + land[a, q].astype(F32)
            p_refs[a][...] = p
            pb_refs[a][...] = p.astype(BF16)

        @pl.when(q == 3)
        def _():
            for k in range(4):
                for a in range(n):
                    copy(a, k).wait_send()

    blk = pl.BlockSpec((None, rows, d), lambda q, pos_ref: (q, 0, 0))
    mine = pl.BlockSpec((None, rows, d), lambda q, pos_ref: (block(pos_ref[2], q, False), 0, 0))
    res = pl.pallas_call(
        body, name=name,
        grid_spec=pltpu.PrefetchScalarGridSpec(
            num_scalar_prefetch=1, grid=(4,),
            in_specs=[mine] * n + [_ANY] * n, out_specs=[blk] * (2 * n),
            scratch_shapes=[pltpu.VMEM((n, 4, rows, d), BF16), pltpu.SemaphoreType.DMA((n, 4)),
                            pltpu.SemaphoreType.DMA((n, 4))]),
        out_shape=[jax.ShapeDtypeStruct((4, rows, d), F32)] * n + [jax.ShapeDtypeStruct((4, rows, d), BF16)] * n,
        compiler_params=_cparams(("arbitrary",)),
    )(pos, *gs, *gbs)
    return list(zip(res[:n], res[n:]))


def _rs_sum2_adam(pos, p, r2, w, m, v, name):
    _, rows, d = p.shape
    tr = rows // 2 if rows % 16 == 0 else rows

    def body(pos_ref, p_ref, r_ref, w_ref, m_ref, v_ref, g_ref, d_ref, nm_ref, nv_ref):
        r = r_ref[...].astype(F32)
        g = ((p_ref[...] + r[0]) + r[1]) + r[2]
        g_ref[...] = g
        d_ref[...], nm_ref[...], nv_ref[...] = _adam_math(w_ref[...], g, m_ref[...], v_ref[...])

    blk = pl.BlockSpec((tr, d), lambda i, pos_ref: (i, 0))
    return pl.pallas_call(
        body, name=name,
        grid_spec=pltpu.PrefetchScalarGridSpec(
            num_scalar_prefetch=1, grid=(rows // tr,),
            in_specs=[pl.BlockSpec((None, tr, d), lambda i, pos_ref: (2 * pos_ref[0] + pos_ref[1], i, 0)),
                      pl.BlockSpec((3, tr, d), lambda i, pos_ref: (0, i, 0)), blk, blk, blk],
            out_specs=[blk] * 4),
        out_shape=[jax.ShapeDtypeStruct((rows, d), F32)] * 4,
        compiler_params=_cparams(("arbitrary",)),
    )(pos, p, r2, w, m, v)


def _ffn_chunk(f):
    for cand in (256, 128):
        if f % cand == 0:
            return cand
    return f


def _loss_head(x, gg, target, loss_ref, dg_ref):
    @pl.when(pl.program_id(0) == 0)
    def _():
        loss_ref[...] = jnp.zeros_like(loss_ref)
        dg_ref[...] = jnp.zeros_like(dg_ref)

    r = lax.rsqrt(jnp.mean(x * x, axis=-1, keepdims=True) + EPS)
    xhat = x * r
    e = xhat * gg - target
    loss_ref[...] += 0.5 * jnp.sum(jnp.mean(e * e, axis=-1, keepdims=True), axis=0, keepdims=True)
    dy = e * (1.0 / x.shape[-1])
    dg_ref[...] += jnp.sum(dy * xhat, axis=0, keepdims=True)
    dxhat = dy * gg
    return r * (dxhat - xhat * jnp.mean(dxhat * xhat, axis=-1, keepdims=True))


def _ffn_fwd(h, g, wg_t, wu_t, wd, name, tm, carry=None, head=None):
    s, d = h.shape
    f = wd.shape[0]
    tf = _ffn_chunk(f)

    def body(h_ref, g_ref, wg_ref, wu_ref, wd_ref, *refs):
        if head is None:
            o_ref, n_ref, gate_ref, up_ref, act_ref = refs
        else:
            gf_ref, t_ref, o_ref, n_ref, gate_ref, up_ref, act_ref, loss_ref, dgf_ref = refs
        x = h_ref[...]
        r = lax.rsqrt(jnp.mean(x * x, axis=-1, keepdims=True) + EPS)
        nb = (x * r * g_ref[...]).astype(BF16)
        n_ref[...] = nb
        for j in range(f // tf):
            sl = slice(j * tf, (j + 1) * tf)
            gate = _dot_nt(nb, wg_ref[sl, :])
            up = _dot_nt(nb, wu_ref[sl, :])
            gate_ref[:, sl] = gate.astype(BF16)
            up_ref[:, sl] = up.astype(BF16)
            act_ref[:, sl] = (gate * _sigmoid(gate) * up).astype(BF16)
        h_out = x + 0.5 * _dot_nn(act_ref[...], wd_ref[...])
        o_ref[...] = h_out if head is None else _loss_head(h_out, gf_ref[...], t_ref[...], loss_ref, dgf_ref)

    row = lambda w: pl.BlockSpec((tm, w), lambda i: (i, 0))
    in_specs = [row(d), _whole((1, d)), _whole((f, d)), _whole((f, d)), _whole((f, d))]
    out_specs = [row(d), row(d), row(f), row(f), row(f)]
    out_shape = [jax.ShapeDtypeStruct((s, d), F32), jax.ShapeDtypeStruct((s, d), BF16),
                 jax.ShapeDtypeStruct((s, f), BF16), jax.ShapeDtypeStruct((s, f), BF16),
                 jax.ShapeDtypeStruct((s, f), BF16)]
    args = (h, g, wg_t, wu_t, wd)
    if head is not None:
        in_specs += [_whole((1, d)), row(d)]
        out_specs += [pl.BlockSpec((1, 1), lambda i: (0, 0)), pl.BlockSpec((1, d), lambda i: (0, 0))]
        out_shape += [jax.ShapeDtypeStruct((1, 1), F32), jax.ShapeDtypeStruct((1, d), F32)]
        args += tuple(head)
    return _call(body, name, (s // tm,), in_specs, out_specs, out_shape, args, ("arbitrary",), carry)


def _gate_grads(dh_ref, gate_ref, up_ref, wd_ref, dgate_ref, dup_ref, dhh_ref, tf):
    dhh = (0.5 * dh_ref[...]).astype(BF16)
    dhh_ref[...] = dhh
    for j in range(gate_ref.shape[1] // tf):
        sl = slice(j * tf, (j + 1) * tf)
        gt = gate_ref[:, sl].astype(F32)
        u = up_ref[:, sl].astype(F32)
        dact = _dot_nt(dhh, wd_ref[sl, :])
        sg = _sigmoid(gt)
        dup_ref[:, sl] = (dact * (gt * sg)).astype(BF16)
        dgate_ref[:, sl] = (dact * u * (sg * (1.0 + gt * (1.0 - sg)))).astype(BF16)


def _input_grad(h_ref, dh_ref, dgate_ref, dup_ref, g_ref, wg_ref, wu_ref, o_ref, dg_ref):
    x = h_ref[...]
    r = lax.rsqrt(jnp.mean(x * x, axis=-1, keepdims=True) + EPS)
    xhat = x * r
    dn = _dot_nn(dgate_ref[...], wg_ref[...]) + _dot_nn(dup_ref[...], wu_ref[...])
    dxhat = dn * g_ref[...]
    o_ref[...] = dh_ref[...] + r * (dxhat - xhat * jnp.mean(dxhat * xhat, axis=-1, keepdims=True))

    @pl.when(pl.program_id(0) == 0)
    def _():
        dg_ref[...] = jnp.zeros_like(dg_ref)

    dg_ref[...] += jnp.sum(dn * xhat, axis=0, keepdims=True)


def _ffn_bwd(h_in, dh_out, gate, up, g, wg_t, wu_t, wd, name, tm):
    s, d = h_in.shape
    f = gate.shape[1]
    tf = _ffn_chunk(f)

    def body(h_ref, dh_ref, gate_ref, up_ref, g_ref, wg_ref, wu_ref, wd_ref,
             o_ref, dg_ref, dgate_ref, dup_ref, dhh_ref):
        _gate_grads(dh_ref, gate_ref, up_ref, wd_ref, dgate_ref, dup_ref, dhh_ref, tf)
        _input_grad(h_ref, dh_ref, dgate_ref, dup_ref, g_ref, wg_ref, wu_ref, o_ref, dg_ref)

    row = lambda w: pl.BlockSpec((tm, w), lambda i: (i, 0))
    return pl.pallas_call(
        body, name=name, grid=(s // tm,),
        in_specs=[row(d), row(d), row(f), row(f), _whole((1, d)), _whole((f, d)), _whole((f, d)), _whole((f, d))],
        out_specs=[row(d), pl.BlockSpec((1, d), lambda i: (0, 0)), row(f), row(f), row(d)],
        out_shape=[jax.ShapeDtypeStruct((s, d), F32), jax.ShapeDtypeStruct((1, d), F32),
                   jax.ShapeDtypeStruct((s, f), BF16), jax.ShapeDtypeStruct((s, f), BF16),
                   jax.ShapeDtypeStruct((s, d), BF16)],
        compiler_params=_cparams(("arbitrary",)),
    )(h_in, dh_out, gate, up, g, wg_t, wu_t, wd)


def _ffn_bwd_gates(dh_out, gate, up, wd, name, tm, carry=None):
    s, d = dh_out.shape
    f = gate.shape[1]
    tf = _ffn_chunk(f)

    def body(dh_ref, gate_ref, up_ref, wd_ref, dgate_ref, dup_ref, dhh_ref):
        _gate_grads(dh_ref, gate_ref, up_ref, wd_ref, dgate_ref, dup_ref, dhh_ref, tf)

    row = lambda w: pl.BlockSpec((tm, w), lambda i: (i, 0))
    return _call(
        body, name, (s // tm,), [row(d), row(f), row(f), _whole((f, d))], [row(f), row(f), row(d)],
        [jax.ShapeDtypeStruct((s, f), BF16), jax.ShapeDtypeStruct((s, f), BF16), jax.ShapeDtypeStruct((s, d), BF16)],
        (dh_out, gate, up, wd), ("arbitrary",), carry)


def _ffn_bwd_input(h_in, dh_out, dgate, dup, g, wg_t, wu_t, name, tm, carry=None):
    s, d = h_in.shape
    f = dgate.shape[1]

    row = lambda w: pl.BlockSpec((tm, w), lambda i: (i, 0))
    return _call(
        _input_grad, name, (s // tm,),
        [row(d), row(d), row(f), row(f), _whole((1, d)), _whole((f, d)), _whole((f, d))],
        [row(d), pl.BlockSpec((1, d), lambda i: (0, 0))],
        [jax.ShapeDtypeStruct((s, d), F32), jax.ShapeDtypeStruct((1, d), F32)],
        (h_in, dh_out, dgate, dup, g, wg_t, wu_t), ("arbitrary",), carry)


def _wgrad(a, b, name, tf, carry=None):
    s, f = a.shape
    d = b.shape[1]

    def body(a_ref, b_ref, o_ref, ob_ref):
        acc = _dot_tn(a_ref[...], b_ref[...])
        o_ref[...] = acc
        ob_ref[...] = acc.astype(BF16)

    return _call(
        body, name, (f // tf,),
        [pl.BlockSpec((s, tf), lambda i: (0, i)), _whole((s, d))],
        [pl.BlockSpec((tf, d), lambda i: (i, 0)), pl.BlockSpec((tf, d), lambda i: (i, 0))],
        [jax.ShapeDtypeStruct((f, d), F32), jax.ShapeDtypeStruct((f, d), BF16)],
        (a, b), ("arbitrary",), carry)


def _rope(t, c, sa, sb, reps):
    c, sa, sb = (jnp.tile(v, (1, reps)) if reps > 1 else v for v in (c, sa, sb))
    w = t.shape[1]
    return t * c + pltpu.roll(t, w - 8, 1) * sa + pltpu.roll(t, 8, 1) * sb


def _rope_bwd(dt, c, sa, sb, reps):
    c, sa, sb = (jnp.tile(v, (1, reps)) if reps > 1 else v for v in (c, sa, sb))
    w = dt.shape[1]
    return dt * c + pltpu.roll(dt * sa, 8, 1) + pltpu.roll(dt * sb, w - 8, 1)


def _mix_in(h, g, win_t, tabs, name, tm):
    s, d = h.shape
    n_in = win_t.shape[0]

    def body(h_ref, g_ref, w_ref, c_ref, sa_ref, sb_ref, q_ref, k_ref, v_ref, pc_ref, n_ref):
        x = h_ref[...]
        r = lax.rsqrt(jnp.mean(x * x, axis=-1, keepdims=True) + EPS)
        nb = (x * r * g_ref[...]).astype(BF16)
        n_ref[...] = nb
        u = _dot_nt(nb, w_ref[...])
        c, sa, sb = c_ref[...], sa_ref[...], sb_ref[...]
        q_ref[...] = _rope(u[:, :ATTN_W], c, sa, sb, ATTN_W // 128).astype(BF16)
        k_ref[...] = _rope(u[:, ATTN_W:ATTN_W + KV_W], c, sa, sb, 1).astype(BF16)
        v_ref[...] = u[:, ATTN_W + KV_W:ATTN_W + 2 * KV_W].astype(BF16)
        pc_ref[...] = u[:, ATTN_W + 2 * KV_W:]

    row = lambda w: pl.BlockSpec((tm, w), lambda i: (i, 0))
    return pl.pallas_call(
        body, name=name, grid=(s // tm,),
        in_specs=[row(d), _whole((1, d)), _whole((n_in, d)), row(128), row(128), row(128)],
        out_specs=[row(ATTN_W), row(KV_W), row(KV_W), row(POOL_W), row(d)],
        out_shape=[jax.ShapeDtypeStruct((s, ATTN_W), BF16), jax.ShapeDtypeStruct((s, KV_W), BF16),
                   jax.ShapeDtypeStruct((s, KV_W), BF16), jax.ShapeDtypeStruct((s, POOL_W), F32),
                   jax.ShapeDtypeStruct((s, d), BF16)],
        compiler_params=_cparams(("arbitrary",)),
    )(h, g, win_t, *tabs)


def _band_mask(n, nb, transposed):
    shape = (3 * BLK, 2 * BLK) if transposed else (2 * BLK, 3 * BLK)
    i = lax.broadcasted_iota(jnp.int32, shape, 1 if transposed else 0) % BLK
    j = lax.broadcasted_iota(jnp.int32, shape, 0 if transposed else 1)
    kpos = (n - 1) * BLK + j
    return (j >= i) & (j <= i + 2 * BLK) & (kpos >= 0) & (kpos < nb * BLK)


def _block_diag(t, kh):
    tf = t.astype(F32)
    tr = pltpu.roll(tf, HEAD_DIM, 1)
    lo = lax.broadcasted_iota(jnp.int32, tf.shape, 1) < HEAD_DIM
    top, bot = (tf, tr) if kh == 0 else (tr, tf)
    return jnp.concatenate([jnp.where(lo, top, 0.0), jnp.where(lo, 0.0, bot)], axis=0).astype(BF16)


def _fold_diag(tbd):
    lo = lax.broadcasted_iota(jnp.int32, (3 * BLK, 2 * HEAD_DIM), 1) < HEAD_DIM
    t = jnp.where(lo, tbd[:3 * BLK], tbd[3 * BLK:])
    return t + pltpu.roll(t, HEAD_DIM, 1)


def _stack_pairs(x, kh):
    return jnp.concatenate([x[:, (2 * kh) * 128:(2 * kh + 1) * 128], x[:, (2 * kh + 1) * 128:(2 * kh + 2) * 128]], axis=0)


def _sink_of(sink_ref, kh, half, axis):
    shape = (2 * BLK, 1) if axis == 0 else (1, 2 * BLK)
    first = lax.broadcasted_iota(jnp.int32, shape, axis) < BLK
    return jnp.where(first, sink_ref[0, GROUP * kh + half], sink_ref[0, GROUP * kh + 2 + half])


def _softmax_sink(sc, valid, sink, axis):
    sc = jnp.where(valid, sc, -1e30)
    m = jnp.maximum(jnp.max(sc, axis=axis, keepdims=True), sink)
    e = jnp.exp(sc - m)
    es = jnp.exp(sink - m)
    inv = 1.0 / (jnp.sum(e, axis=axis, keepdims=True) + es)
    return e * inv, es * inv


def _attn_blocks_per_step(nb):
    return next(nq for nq in (4, 2, 1) if nb % nq == 0)


def _band_specs(nq, nb, w, col=0):
    return [pl.BlockSpec((BLK, w), lambda m: (jnp.maximum(nq * m - 1, 0), col)),
            pl.BlockSpec((nq * BLK, w), lambda m: (m, col)),
            pl.BlockSpec((BLK, w), lambda m: (jnp.minimum(nq * m + nq, nb - 1), col))]


def _attn_pool_fwd(q, k, v, pc, sink, pool_w, pool_scale, pband, name, carry=None):
    s = q.shape[0]
    nb = s // BLK

    nq = _attn_blocks_per_step(nb)

    def body(sink_ref, q_ref, k0, k1, k2, v0, v1, v2, p0, p1, p2, pw_ref, ps_ref, pb_ref, o_ref):
        kall = jnp.concatenate([k0[...], k1[...], k2[...]], axis=0)
        vall = jnp.concatenate([v0[...], v1[...], v2[...]], axis=0)
        pall = jnp.concatenate([p0[...], p1[...], p2[...]], axis=0).astype(BF16)
        qall = q_ref[...] * SCORE_SCALE
        for j in range(nq):
            n = pl.program_id(0) * nq + j
            rows, band = slice(j * BLK, (j + 1) * BLK), slice(j * BLK, (j + 3) * BLK)
            valid = _band_mask(n, nb, False)
            kb, vb, qs = kall[band], vall[band], qall[rows]
            for kh in range(N_KV):
                sc = _dot_nt(_stack_pairs(qs, kh), _block_diag(kb, kh))
                p = [_softmax_sink(sc[:, half * 3 * BLK:(half + 1) * 3 * BLK], valid,
                                   _sink_of(sink_ref, kh, half, 0), 1)[0] for half in range(2)]
                o2 = _dot_nn(jnp.concatenate(p, axis=1).astype(BF16), _block_diag(vb, kh)).astype(BF16)
                o_ref[rows, (2 * kh) * 128:(2 * kh + 1) * 128] = o2[:BLK]
                o_ref[rows, (2 * kh + 1) * 128:(2 * kh + 2) * 128] = o2[BLK:]
            ext = pall[band]
            var = _variant_index(n, nb)
            for gi in range(POOL_G):
                gsl = slice(gi * POOL_GW, (gi + 1) * POOL_GW)
                dg = _dot_nn(pb_ref[var, gi], ext[:, gsl])
                yg = _dot_nn(dg.astype(BF16), pw_ref[gi].astype(BF16))
                o_ref[rows, ATTN_W + gi * POOL_GW:ATTN_W + (gi + 1) * POOL_GW] = (yg * ps_ref[:, gsl]).astype(BF16)

    return _call(
        body, name, (nb // nq,),
        [pl.BlockSpec(memory_space=pltpu.SMEM), pl.BlockSpec((nq * BLK, ATTN_W), lambda m: (m, 0)),
         *_band_specs(nq, nb, KV_W), *_band_specs(nq, nb, KV_W), *_band_specs(nq, nb, POOL_W),
         _whole((POOL_G, POOL_GW, POOL_GW)), _whole((1, POOL_W)), _whole(pband.shape)],
        [pl.BlockSpec((nq * BLK, ATTN_W + POOL_W), lambda m: (m, 0))],
        [jax.ShapeDtypeStruct((s, ATTN_W + POOL_W), BF16)],
        (sink, q, k, k, k, v, v, v, pc, pc, pc, pool_w, pool_scale, pband), ("arbitrary",), carry)


def _attn_pool_bwd(q, k, v, pc, dmix, sink, pool_w, pool_scale, pband, ptband, name, carry=None):
    s = q.shape[0]
    nb = s // BLK
    nq = _attn_blocks_per_step(nb)

    def body(sink_ref, q_ref, k0, k1, k2, v0, v1, v2, p0, p1, p2, da_ref, d0, d1, d2, pw_ref, ps_ref, pb_ref, ptb_ref,
             dq_ref, dk_ref, dv_ref, dpc_ref, dsink_ref, dpw_ref, dps_ref):
        @pl.when(pl.program_id(0) == 0)
        def _():
            dsink_ref[...] = jnp.zeros_like(dsink_ref)
            dpw_ref[...] = jnp.zeros_like(dpw_ref)
            dps_ref[...] = jnp.zeros_like(dps_ref)

        kall = jnp.concatenate([k0[...], k1[...], k2[...]], axis=0)
        vall = jnp.concatenate([v0[...], v1[...], v2[...]], axis=0)
        pall = jnp.concatenate([p0[...], p1[...], p2[...]], axis=0).astype(BF16)
        dpall = jnp.concatenate([d0[...], d1[...], d2[...]], axis=0)
        lo = lax.broadcasted_iota(jnp.int32, (3 * BLK, KV_W), 1) < HEAD_DIM
        for j in range(nq):
            n = pl.program_id(0) * nq + j
            rows, band = slice(j * BLK, (j + 1) * BLK), slice(j * BLK, (j + 3) * BLK)
            valid = _band_mask(n, nb, True)
            kb, vb, qb = kall[band], vall[band], q_ref[rows, :]
            qs = qb * SCORE_SCALE
            da = da_ref[rows, :].astype(BF16)
            dk_fold, dv_fold = [], []
            for kh in range(N_KV):
                kbd, vbd = _block_diag(kb, kh), _block_diag(vb, kh)
                q2, do2 = _stack_pairs(qb, kh), _stack_pairs(da, kh)
                sc_t = _dot_nt(kbd, _stack_pairs(qs, kh))
                dp_t = _dot_nt(vbd, do2)
                p_t, ds_t = [], []
                for half in range(2):
                    keys = slice(half * 3 * BLK, (half + 1) * 3 * BLK)
                    p, ps = _softmax_sink(sc_t[keys], valid, _sink_of(sink_ref, kh, half, 1), 0)
                    delta = jnp.sum(p * dp_t[keys], axis=0, keepdims=True)
                    p_t.append(p.astype(BF16))
                    ds_t.append((p * (dp_t[keys] - delta)).astype(BF16))
                    dsk = -ps * delta
                    for pair in range(2):
                        h = GROUP * kh + 2 * pair + half
                        part = jnp.sum(dsk[:, pair * BLK:(pair + 1) * BLK], axis=1, keepdims=True)
                        dsink_ref[h:h + 1, :] += jnp.broadcast_to(part, (1, 128))
                p_t = jnp.concatenate(p_t, axis=0)
                ds_t = jnp.concatenate(ds_t, axis=0)
                dq2 = _dot_tn(ds_t, kbd) * SCORE_SCALE
                dq_ref[rows, (2 * kh) * 128:(2 * kh + 1) * 128] = dq2[:BLK]
                dq_ref[rows, (2 * kh + 1) * 128:(2 * kh + 2) * 128] = dq2[BLK:]
                dk_fold.append(_fold_diag(_dot_nn(ds_t, q2)) * SCORE_SCALE)
                dv_fold.append(_fold_diag(_dot_nn(p_t, do2)))
            dk_all = jnp.where(lo, dk_fold[0], dk_fold[1])
            dv_all = jnp.where(lo, dv_fold[0], dv_fold[1])
            for t in range(3):
                dk_ref[j, t] = dk_all[t * BLK:(t + 1) * BLK]
                dv_ref[j, t] = dv_all[t * BLK:(t + 1) * BLK]
            ext, dpe = pall[band], dpall[band]
            dpc_cur = dpall[(j + 1) * BLK:(j + 2) * BLK]
            var = _variant_index(n, nb)
            for gi in range(POOL_G):
                gsl = slice(gi * POOL_GW, (gi + 1) * POOL_GW)
                wg = pw_ref[gi].astype(BF16)
                sc = ps_ref[:, gsl]
                dgb = _dot_nn(pb_ref[var, gi], ext[:, gsl]).astype(BF16)
                yg = _dot_nn(dgb, wg)
                dps_ref[:, gsl] += jnp.sum(dpc_cur[:, gsl] * yg, axis=0, keepdims=True)
                dpw_ref[gi] += _dot_tn(dgb, (dpc_cur[:, gsl] * sc).astype(BF16))
                dd = _dot_nt((dpe[:, gsl] * sc).astype(BF16), wg)
                dpc_ref[rows, gsl] = _dot_nn(ptb_ref[var, gi], dd.astype(BF16))

    fixed = lambda shape: pl.BlockSpec(shape, lambda m: (0,) * len(shape))
    return _call(
        body, name, (nb // nq,),
        [pl.BlockSpec(memory_space=pltpu.SMEM), pl.BlockSpec((nq * BLK, ATTN_W), lambda m: (m, 0)),
         *_band_specs(nq, nb, KV_W), *_band_specs(nq, nb, KV_W), *_band_specs(nq, nb, POOL_W),
         pl.BlockSpec((nq * BLK, ATTN_W), lambda m: (m, 0)), *_band_specs(nq, nb, POOL_W, 1),
         _whole((POOL_G, POOL_GW, POOL_GW)), _whole((1, POOL_W)), _whole(pband.shape), _whole(ptband.shape)],
        [pl.BlockSpec((nq * BLK, ATTN_W), lambda m: (m, 0)),
         pl.BlockSpec((nq, 3, BLK, KV_W), lambda m: (m, 0, 0, 0)),
         pl.BlockSpec((nq, 3, BLK, KV_W), lambda m: (m, 0, 0, 0)),
         pl.BlockSpec((nq * BLK, POOL_W), lambda m: (m, 0)),
         fixed((N_HEADS, 128)), fixed((POOL_G, POOL_GW, POOL_GW)), fixed((1, POOL_W))],
        [jax.ShapeDtypeStruct((s, ATTN_W), F32), jax.ShapeDtypeStruct((nb, 3, BLK, KV_W), F32),
         jax.ShapeDtypeStruct((nb, 3, BLK, KV_W), F32), jax.ShapeDtypeStruct((s, POOL_W), F32),
         jax.ShapeDtypeStruct((N_HEADS, 128), F32),
         jax.ShapeDtypeStruct((POOL_G, POOL_GW, POOL_GW), F32), jax.ShapeDtypeStruct((1, POOL_W), F32)],
        (sink, q, k, k, k, v, v, v, pc, pc, pc, dmix, dmix, dmix, dmix, pool_w, pool_scale, pband, ptband),
        ("arbitrary",), carry)


def _mix_out(h, mix, w_out, name, tm):
    s, d = h.shape
    w = mix.shape[1]

    def body(h_ref, m_ref, w_ref, o_ref):
        o_ref[...] = h_ref[...] + _dot_nn(m_ref[...], w_ref[...])

    row = lambda c: pl.BlockSpec((tm, c), lambda i: (i, 0))
    return pl.pallas_call(
        body, name=name, grid=(s // tm,), in_specs=[row(d), row(w), _whole((w, d))], out_specs=row(d),
        out_shape=jax.ShapeDtypeStruct((s, d), F32), compiler_params=_cparams(("arbitrary",)),
    )(h, mix, w_out)


def _mix_out_bwd(dh, w_out, name, tm):
    s, d = dh.shape
    w = w_out.shape[0]

    def body(dh_ref, w_ref, o_ref, dhb_ref):
        dhb = dh_ref[...].astype(BF16)
        dhb_ref[...] = dhb
        o_ref[...] = _dot_nt(dhb, w_ref[...])

    row = lambda c: pl.BlockSpec((tm, c), lambda i: (i, 0))
    return pl.pallas_call(
        body, name=name, grid=(s // tm,), in_specs=[row(d), _whole((w, d))], out_specs=[row(w), row(d)],
        out_shape=[jax.ShapeDtypeStruct((s, w), F32), jax.ShapeDtypeStruct((s, d), BF16)],
        compiler_params=_cparams(("arbitrary",)),
    )(dh, w_out)


def _mix_in_bwd(h, dh, g, win_t, dq, dkp, dvp, dpc, tabs, name, tm, carry=None):
    s, d = h.shape
    nb = s // BLK
    nt = tm // BLK
    n_in = win_t.shape[0]

    def band_sum(n, before, own, after, prev_last, next_first):
        lo = (n > 0).astype(F32)
        hi = (n < s // tm - 1).astype(F32)
        blocks = []
        for b in range(nt):
            from_prev = prev_last[...] * lo if b == 0 else before[b - 1]
            from_next = next_first[...] * hi if b == nt - 1 else after[b + 1]
            blocks.append(from_prev + own[b] + from_next)
        return jnp.concatenate(blocks, axis=0)

    def body(h_ref, dh_ref, g_ref, w_ref, dq_ref, k2, k1, k0, kp, kn, v2, v1, v0, vp, vn, dpc_ref, c_ref, sa_ref,
             sb_ref, o_ref, du_ref, dg_ref):
        n = pl.program_id(0)
        dk = band_sum(n, k2, k1, k0, kp, kn)
        dv = band_sum(n, v2, v1, v0, vp, vn)
        c, sa, sb = c_ref[...], sa_ref[...], sb_ref[...]
        du = jnp.concatenate([_rope_bwd(dq_ref[...], c, sa, sb, ATTN_W // 128), _rope_bwd(dk, c, sa, sb, 1), dv,
                              dpc_ref[...]], axis=1).astype(BF16)
        du_ref[...] = du
        dn = _dot_nn(du, w_ref[...])
        x = h_ref[...]
        r = lax.rsqrt(jnp.mean(x * x, axis=-1, keepdims=True) + EPS)
        xhat = x * r
        dxhat = dn * g_ref[...]
        o_ref[...] = dh_ref[...] + r * (dxhat - xhat * jnp.mean(dxhat * xhat, axis=-1, keepdims=True))

        @pl.when(n == 0)
        def _():
            dg_ref[...] = jnp.zeros_like(dg_ref)

        dg_ref[...] += jnp.sum(dn * xhat, axis=0, keepdims=True)

    row = lambda w: pl.BlockSpec((tm, w), lambda n: (n, 0))
    slot = lambda t: pl.BlockSpec((nt, None, BLK, KV_W), lambda n, t=t: (n, t, 0, 0))
    parts = [slot(2), slot(1), slot(0),
             pl.BlockSpec((None, None, BLK, KV_W), lambda n: (jnp.maximum(nt * n - 1, 0), 2, 0, 0)),
             pl.BlockSpec((None, None, BLK, KV_W), lambda n: (jnp.minimum(nt * n + nt, nb - 1), 0, 0, 0))]
    return _call(
        body, name, (s // tm,),
        [row(d), row(d), _whole((1, d)), _whole((n_in, d)), row(ATTN_W), *parts, *parts, row(POOL_W),
         row(128), row(128), row(128)],
        [row(d), row(n_in), pl.BlockSpec((1, d), lambda n: (0, 0))],
        [jax.ShapeDtypeStruct((s, d), F32), jax.ShapeDtypeStruct((s, n_in), BF16), jax.ShapeDtypeStruct((1, d), F32)],
        (h, dh, g, win_t, dq, *[dkp] * 5, *[dvp] * 5, dpc, *tabs), ("arbitrary",), carry)


def _adam_math(w, g, m, v):
    m = ADAM_B1 * m + (1.0 - ADAM_B1) * g
    v = ADAM_B2 * v + (1.0 - ADAM_B2) * (g * g)
    m_hat = m / (1.0 - ADAM_B1 ** ADAM_STEP)
    v_hat = v / (1.0 - ADAM_B2 ** ADAM_STEP)
    delta = -ADAM_LR * (m_hat / (jnp.sqrt(v_hat) + ADAM_EPS) + ADAM_WD * w)
    return delta, m, v


def _adam_small(w, parts, late, m, v, name):
    rows, cols = w.shape

    def body(w_ref, p_ref, l_ref, m_ref, v_ref, g_ref, d_ref, nm_ref, nv_ref):
        g, gl = p_ref[0], l_ref[0]
        for k in range(1, N_DEV):
            g = g + p_ref[k]
            gl = gl + l_ref[k]
        g_ref[...] = g
        g_ref[SMALL_NORM1:SMALL_NORM1 + 8, :] = g[SMALL_NORM1:SMALL_NORM1 + 8] + gl
        d_ref[...], nm_ref[...], nv_ref[...] = _adam_math(w_ref[...], g_ref[...], m_ref[...], v_ref[...])

    return pl.pallas_call(
        body, name=name, out_shape=[jax.ShapeDtypeStruct((rows, cols), F32)] * 4,
    )(w, parts, late, m, v)


SMALL_NORM1 = 512


def _pack_small(norm1, normm, norm2, normf, sink, pool_w, pool_scale, loss=None):
    scale_rows = jnp.pad(pool_scale.reshape(4, 128), ((0, 4), (0, 0)))
    last_rows = jnp.pad(sink.reshape(1, N_HEADS), ((0, 7), (0, 128 - N_HEADS)))
    if loss is not None:
        last_rows = last_rows + jnp.pad(loss.reshape(1, 1), ((1, 6), (0, 127)))
    return jnp.concatenate([pool_w.reshape(512, 128), norm1.reshape(8, 128), normm.reshape(8, 128),
                            norm2.reshape(8, 128), normf.reshape(8, 128), scale_rows, last_rows], axis=0)


def _unpack_small(p):
    return dict(pool_w=p[:512].reshape(1, POOL_G, POOL_GW, POOL_GW), ffn1_norm=p[512:520].reshape(1, 1024),
                mix_norm=p[520:528].reshape(1, 1024), ffn2_norm=p[528:536].reshape(1, 1024),
                final_norm=p[536:544].reshape(1024), pool_scale=p[544:548].reshape(1, POOL_W),
                sink_logits=p[552, :N_HEADS].reshape(1, N_HEADS), loss=p[553, 0])


def kernel(x, ffn1_norm, ffn1_w_gate, ffn1_w_up, ffn1_w_down, mix_norm, w_in, sink_logits, pool_w, pool_scale, w_out, ffn2_norm, ffn2_w_gate, ffn2_w_up, ffn2_w_down, final_norm, loss_target, m_ffn1_norm, m_ffn1_w_gate, m_ffn1_w_up, m_ffn1_w_down, m_mix_norm, m_w_in, m_sink_logits, m_pool_w, m_pool_scale, m_w_out, m_ffn2_norm, m_ffn2_w_gate, m_ffn2_w_up, m_ffn2_w_down, m_final_norm, v_ffn1_norm, v_ffn1_w_gate, v_ffn1_w_up, v_ffn1_w_down, v_mix_norm, v_w_in, v_sink_logits, v_pool_w, v_pool_scale, v_w_out, v_ffn2_norm, v_ffn2_w_gate, v_ffn2_w_up, v_ffn2_w_down, v_final_norm):
    s, d = x.shape[1], x.shape[2]
    fk = ffn1_w_gate.shape[2]
    f = N_DEV * fk
    ink = w_in.shape[2]
    n_in = N_DEV * ink
    mixk = w_out.shape[1]
    tm = min(512, s)
    tm_bwd = min(256, s)
    pos = jnp.stack([lax.axis_index("x"), lax.axis_index("y"), lax.axis_index("c")]).astype(jnp.int32)

    t_bf = lambda w: w[0].T.astype(BF16)
    full = lambda a: a.reshape(N_DEV * a.shape[1], d)
    first = [t_bf(ffn1_w_gate), t_bf(ffn1_w_up), ffn1_w_down[0].astype(BF16)]
    wg1, wu1, wd1 = map(full, _run_exchange(_AllGather(first), "gather_ffn1"))
    second = _AllGather([t_bf(w_in), w_out[0].astype(BF16), t_bf(ffn2_w_gate), t_bf(ffn2_w_up)])
    third = _AllGather([ffn2_w_down[0].astype(BF16)])

    tabs = _rope_tables(s)
    pband, ptband = _pool_tables(s)
    g1, gm, g2, gf = ffn1_norm, mix_norm, ffn2_norm, final_norm.reshape(1, d)

    x0 = x[0]
    h1, n1, gate1, up1, act1, *gathered = _ffn_fwd(x0, g1, wg1, wu1, wd1, "ffn1_fwd", tm, carry=second)
    win_t, wout, wg2, wu2 = map(full, gathered)
    q, k, v, pc, n2 = _mix_in(h1, gm, win_t, tabs, "mix_in", tm)
    mix, wd2 = _attn_pool_fwd(q, k, v, pc, sink_logits, pool_w[0], pool_scale, pband, "attn_pool_fwd", carry=third)
    wd2 = full(wd2)
    h2 = _mix_out(h1, mix, wout, "mix_out", tm)
    dh3, n3, gate2, up2, act2, loss_part, dgf = _ffn_fwd(h2, g2, wg2, wu2, wd2, "ffn2_fwd", tm,
                                                         head=(gf, loss_target[0]))

    tw = 256 if f % 256 == 0 else 128
    gw, sum1, recv2 = {}, {}, {}

    def stage1(keys, rows):
        sums = _rs_stage1_sum(pos, [gw[key][0].reshape(N_DEV, rows, d) for key in keys],
                              [gw[key][1].reshape(N_DEV, rows, d) for key in keys], "rs1_" + keys[0])
        sum1.update(zip(keys, sums))

    def stage2(keys):
        return _RsStage2([sum1[key][1] for key in keys])

    dh2, dg2, dgate2, dup2, dhh3 = _ffn_bwd(h2, dh3, gate2, up2, g2, wg2, wu2, wd2, "ffn2_bwd", tm_bwd)
    gw["g2"] = _wgrad(dgate2, n3, "wgrad_gate2", tw)
    gw["u2"] = _wgrad(dup2, n3, "wgrad_up2", tw)
    gw["d2"] = _wgrad(act2, dhh3, "wgrad_down2", tw)
    stage1(["g2", "u2", "d2"], fk)
    dmix, dh2b = _mix_out_bwd(dh2, wout, "mix_out_bwd", tm)
    gw["out"] = _wgrad(mix, dh2b, "wgrad_out", tw)
    stage1(["out"], mixk)
    dq, dkp, dvp, dpc, dsink, dpw, dps, recv2["g2"], recv2["u2"] = _attn_pool_bwd(
        q, k, v, pc, dmix, sink_logits, pool_w[0], pool_scale, pband, ptband, "attn_pool_bwd",
        carry=stage2(["g2", "u2"]))
    dh1, du, dgm, recv2["d2"], recv2["out"] = _mix_in_bwd(h1, dh2, gm, win_t, dq, dkp, dvp, dpc, tabs, "mix_in_bwd", tm,
                                                          carry=stage2(["d2", "out"]))
    gw["in"] = _wgrad(du, n2, "wgrad_in", tw)
    stage1(["in"], ink)
    small_part = _pack_small(jnp.zeros_like(dgm), dgm, dg2, dgf, dsink[:, 0], dpw, dps, loss_part)
    dgate1, dup1, dhh1, recv2["in"], small_all = _ffn_bwd_gates(
        dh1, gate1, up1, wd1, "ffn1_bwd_gates", tm, carry=_Both(stage2(["in"]), _AllGather([small_part])))
    gw["g1"] = _wgrad(dgate1, n1, "wgrad_gate1", tw)
    stage1(["g1"], fk)
    *gw["u1"], recv2["g1"] = _wgrad(dup1, n1, "wgrad_up1", tw, carry=stage2(["g1"]))
    stage1(["u1"], fk)
    *gw["d1"], recv2["u1"] = _wgrad(act1, dhh1, "wgrad_down1", tw, carry=stage2(["u1"]))
    stage1(["d1"], fk)
    dx, dg1, recv2["d1"] = _ffn_bwd_input(x0, dh1, dgate1, dup1, g1, wg1, wu1, "ffn1_bwd_input", tm,
                                          carry=stage2(["d1"]))

    (dg1_all,) = _run_exchange(_DirectGather([dg1.reshape(8, 128)]), "gather_norm1_grad")
    pk = lambda a, b, c_, e, s_, pw_, psc: _pack_small(a, b, c_, e, s_[0], pw_[0], psc)
    small_w = pk(ffn1_norm, mix_norm, ffn2_norm, final_norm, sink_logits, pool_w, pool_scale)
    small_m = pk(m_ffn1_norm, m_mix_norm, m_ffn2_norm, m_final_norm, m_sink_logits, m_pool_w, m_pool_scale)
    small_v = pk(v_ffn1_norm, v_mix_norm, v_ffn2_norm, v_final_norm, v_sink_logits, v_pool_w, v_pool_scale)
    sg, sd, sm, sv = [_unpack_small(a)
                      for a in _adam_small(small_w, small_all, dg1_all, small_m, small_v, "adam_small")]

    big = {}
    keys = ["g1", "u1", "d1", "g2", "u2", "d2", "in", "out"]
    names = ["ffn1_w_gate", "ffn1_w_up", "ffn1_w_down", "ffn2_w_gate", "ffn2_w_up", "ffn2_w_down", "w_in", "w_out"]
    transposed = [True, True, False, True, True, False, True, False]
    ws = [ffn1_w_gate, ffn1_w_up, ffn1_w_down, ffn2_w_gate, ffn2_w_up, ffn2_w_down, w_in, w_out]
    ms = [m_ffn1_w_gate, m_ffn1_w_up, m_ffn1_w_down, m_ffn2_w_gate, m_ffn2_w_up, m_ffn2_w_down, m_w_in, m_w_out]
    vs = [v_ffn1_w_gate, v_ffn1_w_up, v_ffn1_w_down, v_ffn2_w_gate, v_ffn2_w_up, v_ffn2_w_down, v_w_in, v_w_out]
    for key, nm, tr, w, m, vv in zip(keys, names, transposed, ws, ms, vs):
        view = (lambda a: jnp.swapaxes(a, 1, 2)[0]) if tr else (lambda a: a[0])
        back = (lambda a: jnp.swapaxes(a[None], 1, 2)) if tr else (lambda a: a[None])
        res = _rs_sum2_adam(pos, sum1[key][0], recv2[key], view(w), view(m), view(vv), "adam_" + nm)
        big[nm] = tuple(back(a) for a in res)

    loss = sg["loss"]
    all_names = ["ffn1_norm", "ffn1_w_gate", "ffn1_w_up", "ffn1_w_down", "mix_norm", "w_in", "sink_logits", "pool_w",
                 "pool_scale", "w_out", "ffn2_norm", "ffn2_w_gate", "ffn2_w_up", "ffn2_w_down", "final_norm"]
    outs = [loss, dx[None]]
    for idx, src in enumerate((sg, sd, sm, sv)):
        for nm in all_names:
            outs.append(big[nm][idx] if nm in big else src[nm])
    return tuple(outs)
```

```python
import functools

import jax
import jax.numpy as jnp
import numpy as np
from jax import lax
from jax.experimental import pallas as pl
from jax.experimental.pallas import tpu as pltpu

F32 = jnp.float32
BF16 = jnp.bfloat16
MESH = pl.DeviceIdType.MESH
N_DEV = 8

EPS = 1e-6
HEAD_DIM = 64
N_HEADS = 8
N_KV = 2
GROUP = N_HEADS // N_KV
ATTN_W = N_HEADS * HEAD_DIM
KV_W = N_KV * HEAD_DIM
POOL_W = 512
POOL_G = 4
POOL_GW = POOL_W // POOL_G
POOL_WINDOWS = (2, 4, 8, 16)
BLK = 128
ROT = 16
ROPE_THETA = 500000.0
SCORE_SCALE = HEAD_DIM ** -0.5

ADAM_LR, ADAM_B1, ADAM_B2, ADAM_EPS, ADAM_WD, ADAM_STEP = 0.001, 0.9, 0.999, 1e-08, 0.01, 10

VMEM_LIMIT = 56 * 1024 * 1024


def _cparams(sem=None, **kw):
    if sem is not None:
        kw["dimension_semantics"] = sem
    return pltpu.CompilerParams(vmem_limit_bytes=VMEM_LIMIT, **kw)


def _whole(shape):
    nd = len(shape)
    return pl.BlockSpec(shape, lambda *_: (0,) * nd, pipeline_mode=pl.Buffered(1))


def _sigmoid(z):
    return 1.0 / (1.0 + jnp.exp(-z))


def _dot_nt(a, b):
    return lax.dot_general(a, b, (((1,), (1,)), ((), ())), preferred_element_type=F32)


def _dot_nn(a, b):
    return lax.dot_general(a, b, (((1,), (0,)), ((), ())), preferred_element_type=F32)


def _dot_tn(a, b):
    return lax.dot_general(a, b, (((0,), (0,)), ((), ())), preferred_element_type=F32)


def _rope_tables(s):
    inv_freq = ROPE_THETA ** (-np.arange(0, ROT, 2, dtype=np.float64) / ROT)
    ang = np.arange(s, dtype=np.float64)[:, None] * inv_freq[None, :]
    c = np.ones((s, HEAD_DIM)); sa = np.zeros((s, HEAD_DIM)); sb = np.zeros((s, HEAD_DIM))
    c[:, :8] = np.cos(ang); c[:, 8:16] = np.cos(ang)
    sa[:, :8] = -np.sin(ang)
    sb[:, 8:16] = np.sin(ang)
    t = lambda a: jnp.asarray(np.tile(a, (1, 2)).astype(np.float32))
    return t(c), t(sa), t(sb)


def _pool_weight(gi, t, s_pos, s):
    half = POOL_WINDOWS[gi] // 2

    def win(lo, hi):
        a = np.clip(lo, 0, s); b = np.clip(hi + 1, 0, s)
        inside = (s_pos >= a) & (s_pos < b)
        return inside / np.maximum(b - a, 1)

    w = 0.5 * (win(t - half, t + half - 1) + win(t - half + 1, t + half)) - (t == s_pos)
    return w * ((t >= 0) & (t < s) & (s_pos >= 0) & (s_pos < s))


def _pool_tables(s):
    nb = s // BLK
    fwd = np.zeros((3, POOL_G, BLK, 3 * BLK), np.float32)
    bwd = np.zeros((3, POOL_G, BLK, 3 * BLK), np.float32)
    for vi, n in enumerate((0, 1 if nb > 2 else 0, nb - 1)):
        i = n * BLK + np.arange(BLK)[:, None]
        j = (n - 1) * BLK + np.arange(3 * BLK)[None, :]
        for gi in range(POOL_G):
            fwd[vi, gi] = _pool_weight(gi, i, j, s)
            bwd[vi, gi] = _pool_weight(gi, j, i, s)
    return jnp.asarray(fwd, dtype=BF16), jnp.asarray(bwd, dtype=BF16)


def _variant_index(n, nb):
    return jnp.where(n == 0, 0, jnp.where(n == nb - 1, 2, 1))


class _Exchange:
    inputs = ()
    out_shapes = ()
    sems = ()

    def start(self, srcs, outs, sems):
        raise NotImplementedError

    def middle(self, srcs, outs, sems):
        pass

    def finish(self, srcs, outs, sems):
        raise NotImplementedError


class _AllGather(_Exchange):
    def __init__(self, arrays):
        n = len(arrays)
        self.inputs = list(arrays)
        self.out_shapes = [jax.ShapeDtypeStruct((N_DEV,) + a.shape, a.dtype) for a in arrays]
        self.sems = [pltpu.SemaphoreType.DMA((n, 7)), pltpu.SemaphoreType.DMA((n, 7)), pltpu.SemaphoreType.DMA((n,))]

    def _parts(self, srcs, outs, sems):
        send_sems, recv_sems, local_sems = sems
        n = len(srcs)
        x, y, c = lax.axis_index("x"), lax.axis_index("y"), lax.axis_index("c")
        me, sibling = (x, y, c), (x, y, 1 - c)
        chips = [(1 - x, y), (x, 1 - y), (1 - x, 1 - y)]

        def slot(a, dev):
            return outs[a].at[4 * dev[0] + 2 * dev[1] + dev[2]]

        def copy(a, k, block, to, src=None):
            return pltpu.make_async_remote_copy(
                src_ref=slot(a, block) if src is None else src, dst_ref=slot(a, block),
                send_sem=send_sems.at[a, k], recv_sem=recv_sems.at[a, k], device_id=to, device_id_type=MESH)

        def mine():
            return [pltpu.make_async_copy(srcs[a], slot(a, me), local_sems.at[a]) for a in range(n)]

        def first():
            return [copy(a, 0, me, sibling, src=srcs[a]) for a in range(n)] + [
                copy(a, 1 + j, me, (*chip, c), src=srcs[a]) for a in range(n) for j, chip in enumerate(chips)]

        return n, c, me, sibling, chips, copy, mine, first

    def start(self, srcs, outs, sems):
        _, _, _, _, _, _, mine, first = self._parts(srcs, outs, sems)
        for cp in mine() + first():
            cp.start()

    def middle(self, srcs, outs, sems):
        n, c, me, sibling, chips, copy, _, _ = self._parts(srcs, outs, sems)
        for j, chip in enumerate(chips):
            for a in range(n):
                copy(a, 1 + j, (*chip, c), me).wait_recv()
                copy(a, 4 + j, (*chip, c), sibling).start()

    def finish(self, srcs, outs, sems):
        n, c, me, sibling, chips, copy, mine, first = self._parts(srcs, outs, sems)
        mine, first = mine(), first()
        passed = [copy(a, 4 + j, (*chip, c), sibling) for j, chip in enumerate(chips) for a in range(n)]
        for a in range(n):
            copy(a, 0, sibling, me).wait_recv()
            for j, chip in enumerate(chips):
                copy(a, 4 + j, (*chip, 1 - c), me).wait_recv()
        for cp in first + passed:
            cp.wait_send()
        for cp in mine:
            cp.wait()


class _RsStage2(_Exchange):
    def start(self, srcs, outs, sems):
        for cp in self._copies(srcs, outs, sems):
            cp.start()

    def finish(self, srcs, outs, sems):
        copies = self._copies(srcs, outs, sems)
        for cp in copies:
            cp.wait_recv()
        for cp in copies:
            cp.wait_send()


    def __init__(self, pbs):
        n = len(pbs)
        self.inputs = list(pbs)
        self.out_shapes = [jax.ShapeDtypeStruct((3,) + p.shape[1:], p.dtype) for p in pbs]
        self.sems = [pltpu.SemaphoreType.DMA((n, 3)), pltpu.SemaphoreType.DMA((n, 3))]

    def _copies(self, srcs, outs, sems):
        send_sems, recv_sems = sems
        x, y, c = lax.axis_index("x"), lax.axis_index("y"), lax.axis_index("c")
        chips = [(1 - x, y), (x, 1 - y), (1 - x, 1 - y)]
        return [pltpu.make_async_remote_copy(
            src_ref=srcs[a].at[2 * chip[0] + chip[1]], dst_ref=outs[a].at[j], send_sem=send_sems.at[a, j],
            recv_sem=recv_sems.at[a, j], device_id=(*chip, c), device_id_type=MESH)
            for a in range(len(srcs)) for j, chip in enumerate(chips)]


class _DirectGather(_Exchange):
    def __init__(self, arrays):
        n = len(arrays)
        self.inputs = list(arrays)
        self.out_shapes = [jax.ShapeDtypeStruct((N_DEV,) + a.shape, a.dtype) for a in arrays]
        self.sems = [pltpu.SemaphoreType.DMA((n, 7)), pltpu.SemaphoreType.DMA((n, 7)), pltpu.SemaphoreType.DMA((n,))]

    def _copies(self, srcs, outs, sems):
        send_sems, recv_sems, local_sems = sems
        x, y, c = lax.axis_index("x"), lax.axis_index("y"), lax.axis_index("c")
        me = 4 * x + 2 * y + c
        remote, local = [], []
        for a in range(len(srcs)):
            local.append(pltpu.make_async_copy(srcs[a], outs[a].at[me], local_sems.at[a]))
            for k in range(1, N_DEV):
                peer = (x ^ (k >> 2), y ^ ((k >> 1) & 1), c ^ (k & 1))
                remote.append(pltpu.make_async_remote_copy(
                    src_ref=srcs[a], dst_ref=outs[a].at[me], send_sem=send_sems.at[a, k - 1],
                    recv_sem=recv_sems.at[a, k - 1], device_id=peer, device_id_type=MESH))
        return remote, local

    def start(self, srcs, outs, sems):
        remote, local = self._copies(srcs, outs, sems)
        for cp in local + remote:
            cp.start()

    def finish(self, srcs, outs, sems):
        remote, local = self._copies(srcs, outs, sems)
        for cp in remote:
            cp.wait_recv()
        for cp in remote:
            cp.wait_send()
        for cp in local:
            cp.wait()


class _Both(_Exchange):
    def __init__(self, a, b):
        self.a, self.b = a, b
        self.inputs = list(a.inputs) + list(b.inputs)
        self.out_shapes = list(a.out_shapes) + list(b.out_shapes)
        self.sems = list(a.sems) + list(b.sems)

    def _split(self, srcs, outs, sems):
        na, oa, sa = len(self.a.inputs), len(self.a.out_shapes), len(self.a.sems)
        return (srcs[:na], outs[:oa], sems[:sa]), (srcs[na:], outs[oa:], sems[sa:])

    def start(self, srcs, outs, sems):
        pa, pb = self._split(srcs, outs, sems)
        self.a.start(*pa)
        self.b.start(*pb)

    def middle(self, srcs, outs, sems):
        pa, pb = self._split(srcs, outs, sems)
        self.a.middle(*pa)
        self.b.middle(*pb)

    def finish(self, srcs, outs, sems):
        pa, pb = self._split(srcs, outs, sems)
        self.a.finish(*pa)
        self.b.finish(*pb)


_ANY = pl.BlockSpec(memory_space=pl.ANY)


def _run_exchange(ex, name):
    n_in, n_out = len(ex.inputs), len(ex.out_shapes)

    def body(*refs):
        srcs, outs, sems = refs[:n_in], refs[n_in:n_in + n_out], refs[n_in + n_out:]
        ex.start(srcs, outs, sems)
        ex.middle(srcs, outs, sems)
        ex.finish(srcs, outs, sems)

    return pl.pallas_call(
        body, name=name, out_shape=list(ex.out_shapes), in_specs=[_ANY] * n_in, out_specs=[_ANY] * n_out,
        scratch_shapes=list(ex.sems),
    )(*ex.inputs)


CARRY_MIDDLE = 0.85


def _call(body, name, grid, in_specs, out_specs, out_shape, args, sem, carry=None, middle=CARRY_MIDDLE, **kw):
    if carry is None:
        return pl.pallas_call(functools.partial(body), name=name, grid=grid, in_specs=in_specs, out_specs=out_specs,
                              out_shape=out_shape, compiler_params=_cparams(sem), **kw)(*args)
    n_in, n_out = len(in_specs), len(out_specs)
    nc_in, nc_out = len(carry.inputs), len(carry.out_shapes)

    def carried(*refs):
        ins = refs[:n_in]
        c_in = refs[n_in:n_in + nc_in]
        outs = refs[n_in + nc_in:n_in + nc_in + n_out]
        c_out = refs[n_in + nc_in + n_out:n_in + nc_in + n_out + nc_out]
        sems = refs[n_in + nc_in + n_out + nc_out:]
        ids = [pl.program_id(i) for i in range(len(grid))]
        is_first = functools.reduce(jnp.logical_and, [i == 0 for i in ids])
        is_last = functools.reduce(jnp.logical_and, [i == g - 1 for i, g in zip(ids, grid)])
        @pl.when(is_first)
        def _():
            carry.start(c_in, c_out, sems)

        if middle is not None:
            @pl.when(functools.reduce(jnp.logical_and, [ids[0] == round(middle * (grid[0] - 1))]
                                      + [i == 0 for i in ids[1:]]))
            def _():
                carry.middle(c_in, c_out, sems)

        body(*ins, *outs)

        @pl.when(is_last)
        def _():
            if middle is None:
                carry.middle(c_in, c_out, sems)
            carry.finish(c_in, c_out, sems)

    return pl.pallas_call(
        carried, name=name, grid=grid, in_specs=list(in_specs) + [_ANY] * nc_in,
        out_specs=list(out_specs) + [_ANY] * nc_out, out_shape=list(out_shape) + list(carry.out_shapes),
        scratch_shapes=list(carry.sems), compiler_params=_cparams(sem), **kw)(*args, *carry.inputs)


def _rs_stage1_sum(pos, gs, gbs, name):
    n = len(gs)
    _, rows, d = gs[0].shape

    def block(pc, q, other_core):
        return 4 * (q // 2) + 2 * (q % 2) + ((1 - pc) if other_core else pc)

    def body(pos_ref, *refs):
        g_refs, gb_refs = refs[:n], refs[n:2 * n]
        p_refs, pb_refs = refs[2 * n:3 * n], refs[3 * n:4 * n]
        land, send_sems, recv_sems = refs[4 * n:]
        q = pl.program_id(0)
        x, y, c = lax.axis_index("x"), lax.axis_index("y"), lax.axis_index("c")

        def copy(a, k):
            return pltpu.make_async_remote_copy(
                src_ref=gb_refs[a].at[block(c, k, True)], dst_ref=land.at[a, k],
                send_sem=send_sems.at[a, k], recv_sem=recv_sems.at[a, k], device_id=(x, y, 1 - c),
                device_id_type=MESH)

        @pl.when(q == 0)
        def _():
            for k in range(4):
                for a in range(n):
                    copy(a, k).start()

        for a in range(n):
            copy(a, q).wait_recv()
            p = g_refs[a][...] + land[a, q].astype(F32)
            p_refs[a][...] = p
            pb_refs[a][...] = p.astype(BF16)

        @pl.when(q == 3)
        def _():
            for k in range(4):
                for a in range(n):
                    copy(a, k).wait_send()

    blk = pl.BlockSpec((None, rows, d), lambda q, pos_ref: (q, 0, 0))
    mine = pl.BlockSpec((None, rows, d), lambda q, pos_ref: (block(pos_ref[2], q, False), 0, 0))
    res = pl.pallas_call(
        body, name=name,
        grid_spec=pltpu.PrefetchScalarGridSpec(
            num_scalar_prefetch=1, grid=(4,),
            in_specs=[mine] * n + [_ANY] * n, out_specs=[blk] * (2 * n),
            scratch_shapes=[pltpu.VMEM((n, 4, rows, d), BF16), pltpu.SemaphoreType.DMA((n, 4)),
                            pltpu.SemaphoreType.DMA((n, 4))]),
        out_shape=[jax.ShapeDtypeStruct((4, rows, d), F32)] * n + [jax.ShapeDtypeStruct((4, rows, d), BF16)] * n,
        compiler_params=_cparams(("arbitrary",)),
    )(pos, *gs, *gbs)
    return list(zip(res[:n], res[n:]))


def _rs_sum2_adam(pos, p, r2, w, m, v, name):
    _, rows, d = p.shape
    tr = rows // 2 if rows % 16 == 0 else rows

    def body(pos_ref, p_ref, r_ref, w_ref, m_ref, v_ref, g_ref, d_ref, nm_ref, nv_ref):
        r = r_ref[...].astype(F32)
        g = ((p_ref[...] + r[0]) + r[1]) + r[2]
        g_ref[...] = g
        d_ref[...], nm_ref[...], nv_ref[...] = _adam_math(w_ref[...], g, m_ref[...], v_ref[...])

    blk = pl.BlockSpec((tr, d), lambda i, pos_ref: (i, 0))
    return pl.pallas_call(
        body, name=name,
        grid_spec=pltpu.PrefetchScalarGridSpec(
            num_scalar_prefetch=1, grid=(rows // tr,),
            in_specs=[pl.BlockSpec((None, tr, d), lambda i, pos_ref: (2 * pos_ref[0] + pos_ref[1], i, 0)),
                      pl.BlockSpec((3, tr, d), lambda i, pos_ref: (0, i, 0)), blk, blk, blk],
            out_specs=[blk] * 4),
        out_shape=[jax.ShapeDtypeStruct((rows, d), F32)] * 4,
        compiler_params=_cparams(("arbitrary",)),
    )(pos, p, r2, w, m, v)


def _ffn_chunk(f):
    for cand in (256, 128):
        if f % cand == 0:
            return cand
    return f


def _loss_head(x, gg, target, loss_ref, dg_ref):
    @pl.when(pl.program_id(0) == 0)
    def _():
        loss_ref[...] = jnp.zeros_like(loss_ref)
        dg_ref[...] = jnp.zeros_like(dg_ref)

    r = lax.rsqrt(jnp.mean(x * x, axis=-1, keepdims=True) + EPS)
    xhat = x * r
    e = xhat * gg - target
    loss_ref[...] += 0.5 * jnp.sum(jnp.mean(e * e, axis=-1, keepdims=True), axis=0, keepdims=True)
    dy = e * (1.0 / x.shape[-1])
    dg_ref[...] += jnp.sum(dy * xhat, axis=0, keepdims=True)
    dxhat = dy * gg
    return r * (dxhat - xhat * jnp.mean(dxhat * xhat, axis=-1, keepdims=True))


def _ffn_fwd(h, g, wg_t, wu_t, wd, name, tm, carry=None, head=None, middle=CARRY_MIDDLE):
    s, d = h.shape
    f = wg_t.shape[0]
    tf = _ffn_chunk(f)

    def body(h_ref, g_ref, wg_ref, wu_ref, *refs):
        if wd is None:
            n_ref, gate_ref, up_ref, act_ref = refs
        elif head is None:
            wd_ref, o_ref, n_ref, gate_ref, up_ref, act_ref = refs
        else:
            wd_ref, gf_ref, t_ref, o_ref, n_ref, gate_ref, up_ref, act_ref, loss_ref, dgf_ref = refs
        x = h_ref[...]
        r = lax.rsqrt(jnp.mean(x * x, axis=-1, keepdims=True) + EPS)
        nb = (x * r * g_ref[...]).astype(BF16)
        n_ref[...] = nb
        for j in range(f // tf):
            sl = slice(j * tf, (j + 1) * tf)
            gate = _dot_nt(nb, wg_ref[sl, :])
            up = _dot_nt(nb, wu_ref[sl, :])
            gate_ref[:, sl] = gate.astype(BF16)
            up_ref[:, sl] = up.astype(BF16)
            act_ref[:, sl] = (gate * _sigmoid(gate) * up).astype(BF16)
        if wd is not None:
            h_out = x + 0.5 * _dot_nn(act_ref[...], wd_ref[...])
            o_ref[...] = h_out if head is None else _loss_head(h_out, gf_ref[...], t_ref[...], loss_ref, dgf_ref)

    row = lambda w: pl.BlockSpec((tm, w), lambda i: (i, 0))
    in_specs = [row(d), _whole((1, d)), _whole((f, d)), _whole((f, d))]
    out_specs = [row(d), row(f), row(f), row(f)]
    out_shape = [jax.ShapeDtypeStruct((s, d), BF16)] + [jax.ShapeDtypeStruct((s, f), BF16)] * 3
    args = (h, g, wg_t, wu_t)
    if wd is not None:
        in_specs, args = in_specs + [_whole((f, d))], args + (wd,)
        out_specs, out_shape = [row(d)] + out_specs, [jax.ShapeDtypeStruct((s, d), F32)] + out_shape
    if head is not None:
        in_specs += [_whole((1, d)), row(d)]
        out_specs += [pl.BlockSpec((1, 1), lambda i: (0, 0)), pl.BlockSpec((1, d), lambda i: (0, 0))]
        out_shape += [jax.ShapeDtypeStruct((1, 1), F32), jax.ShapeDtypeStruct((1, d), F32)]
        args += tuple(head)
    return _call(body, name, (s // tm,), in_specs, out_specs, out_shape, args, ("arbitrary",), carry, middle)


def _gate_grads(dh_ref, gate_ref, up_ref, wd_ref, dgate_ref, dup_ref, dhh_ref, tf):
    dhh = (0.5 * dh_ref[...]).astype(BF16)
    dhh_ref[...] = dhh
    for j in range(gate_ref.shape[1] // tf):
        sl = slice(j * tf, (j + 1) * tf)
        gt = gate_ref[:, sl].astype(F32)
        u = up_ref[:, sl].astype(F32)
        dact = _dot_nt(dhh, wd_ref[sl, :])
        sg = _sigmoid(gt)
        dup_ref[:, sl] = (dact * (gt * sg)).astype(BF16)
        dgate_ref[:, sl] = (dact * u * (sg * (1.0 + gt * (1.0 - sg)))).astype(BF16)


def _input_grad(h_ref, dh_ref, dgate_ref, dup_ref, g_ref, wg_ref, wu_ref, o_ref, dg_ref):
    x = h_ref[...]
    r = lax.rsqrt(jnp.mean(x * x, axis=-1, keepdims=True) + EPS)
    xhat = x * r
    dn = _dot_nn(dgate_ref[...], wg_ref[...]) + _dot_nn(dup_ref[...], wu_ref[...])
    dxhat = dn * g_ref[...]
    o_ref[...] = dh_ref[...] + r * (dxhat - xhat * jnp.mean(dxhat * xhat, axis=-1, keepdims=True))

    @pl.when(pl.program_id(0) == 0)
    def _():
        dg_ref[...] = jnp.zeros_like(dg_ref)

    dg_ref[...] += jnp.sum(dn * xhat, axis=0, keepdims=True)


def _ffn_bwd(h_in, dh_out, gate, up, g, wg_t, wu_t, wd, name, tm):
    s, d = h_in.shape
    f = gate.shape[1]
    tf = _ffn_chunk(f)

    def body(h_ref, dh_ref, gate_ref, up_ref, g_ref, wg_ref, wu_ref, wd_ref,
             o_ref, dg_ref, dgate_ref, dup_ref, dhh_ref):
        _gate_grads(dh_ref, gate_ref, up_ref, wd_ref, dgate_ref, dup_ref, dhh_ref, tf)
        _input_grad(h_ref, dh_ref, dgate_ref, dup_ref, g_ref, wg_ref, wu_ref, o_ref, dg_ref)

    row = lambda w: pl.BlockSpec((tm, w), lambda i: (i, 0))
    return pl.pallas_call(
        body, name=name, grid=(s // tm,),
        in_specs=[row(d), row(d), row(f), row(f), _whole((1, d)), _whole((f, d)), _whole((f, d)), _whole((f, d))],
        out_specs=[row(d), pl.BlockSpec((1, d), lambda i: (0, 0)), row(f), row(f), row(d)],
        out_shape=[jax.ShapeDtypeStruct((s, d), F32), jax.ShapeDtypeStruct((1, d), F32),
                   jax.ShapeDtypeStruct((s, f), BF16), jax.ShapeDtypeStruct((s, f), BF16),
                   jax.ShapeDtypeStruct((s, d), BF16)],
        compiler_params=_cparams(("arbitrary",)),
    )(h_in, dh_out, gate, up, g, wg_t, wu_t, wd)


def _ffn_bwd_gates(dh_out, gate, up, wd, name, tm, carry=None):
    s, d = dh_out.shape
    f = gate.shape[1]
    tf = _ffn_chunk(f)

    def body(dh_ref, gate_ref, up_ref, wd_ref, dgate_ref, dup_ref, dhh_ref):
        _gate_grads(dh_ref, gate_ref, up_ref, wd_ref, dgate_ref, dup_ref, dhh_ref, tf)

    row = lambda w: pl.BlockSpec((tm, w), lambda i: (i, 0))
    return _call(
        body, name, (s // tm,), [row(d), row(f), row(f), _whole((f, d))], [row(f), row(f), row(d)],
        [jax.ShapeDtypeStruct((s, f), BF16), jax.ShapeDtypeStruct((s, f), BF16), jax.ShapeDtypeStruct((s, d), BF16)],
        (dh_out, gate, up, wd), ("arbitrary",), carry)


def _ffn_bwd_input(h_in, dh_out, dgate, dup, g, wg_t, wu_t, name, tm, carry=None):
    s, d = h_in.shape
    f = dgate.shape[1]

    row = lambda w: pl.BlockSpec((tm, w), lambda i: (i, 0))
    return _call(
        _input_grad, name, (s // tm,),
        [row(d), row(d), row(f), row(f), _whole((1, d)), _whole((f, d)), _whole((f, d))],
        [row(d), pl.BlockSpec((1, d), lambda i: (0, 0))],
        [jax.ShapeDtypeStruct((s, d), F32), jax.ShapeDtypeStruct((1, d), F32)],
        (h_in, dh_out, dgate, dup, g, wg_t, wu_t), ("arbitrary",), carry)


def _wgrad(a, b, name, tf, carry=None):
    s, f = a.shape
    d = b.shape[1]

    def body(a_ref, b_ref, o_ref, ob_ref):
        acc = _dot_tn(a_ref[...], b_ref[...])
        o_ref[...] = acc
        ob_ref[...] = acc.astype(BF16)

    return _call(
        body, name, (f // tf,),
        [pl.BlockSpec((s, tf), lambda i: (0, i)), _whole((s, d))],
        [pl.BlockSpec((tf, d), lambda i: (i, 0)), pl.BlockSpec((tf, d), lambda i: (i, 0))],
        [jax.ShapeDtypeStruct((f, d), F32), jax.ShapeDtypeStruct((f, d), BF16)],
        (a, b), ("arbitrary",), carry)


def _rope(t, c, sa, sb, reps):
    c, sa, sb = (jnp.tile(v, (1, reps)) if reps > 1 else v for v in (c, sa, sb))
    w = t.shape[1]
    return t * c + pltpu.roll(t, w - 8, 1) * sa + pltpu.roll(t, 8, 1) * sb


def _rope_bwd(dt, c, sa, sb, reps):
    c, sa, sb = (jnp.tile(v, (1, reps)) if reps > 1 else v for v in (c, sa, sb))
    w = dt.shape[1]
    return dt * c + pltpu.roll(dt * sa, 8, 1) + pltpu.roll(dt * sb, w - 8, 1)


def _mix_in(h, g, win_t, tabs, name, tm):
    s, d = h.shape
    n_in = win_t.shape[0]

    def body(h_ref, g_ref, w_ref, c_ref, sa_ref, sb_ref, q_ref, k_ref, v_ref, pc_ref, n_ref):
        x = h_ref[...]
        r = lax.rsqrt(jnp.mean(x * x, axis=-1, keepdims=True) + EPS)
        nb = (x * r * g_ref[...]).astype(BF16)
        n_ref[...] = nb
        u = _dot_nt(nb, w_ref[...])
        c, sa, sb = c_ref[...], sa_ref[...], sb_ref[...]
        q_ref[...] = _rope(u[:, :ATTN_W], c, sa, sb, ATTN_W // 128).astype(BF16)
        k_ref[...] = _rope(u[:, ATTN_W:ATTN_W + KV_W], c, sa, sb, 1).astype(BF16)
        v_ref[...] = u[:, ATTN_W + KV_W:ATTN_W + 2 * KV_W].astype(BF16)
        pc_ref[...] = u[:, ATTN_W + 2 * KV_W:]

    row = lambda w: pl.BlockSpec((tm, w), lambda i: (i, 0))
    return pl.pallas_call(
        body, name=name, grid=(s // tm,),
        in_specs=[row(d), _whole((1, d)), _whole((n_in, d)), row(128), row(128), row(128)],
        out_specs=[row(ATTN_W), row(KV_W), row(KV_W), row(POOL_W), row(d)],
        out_shape=[jax.ShapeDtypeStruct((s, ATTN_W), BF16), jax.ShapeDtypeStruct((s, KV_W), BF16),
                   jax.ShapeDtypeStruct((s, KV_W), BF16), jax.ShapeDtypeStruct((s, POOL_W), F32),
                   jax.ShapeDtypeStruct((s, d), BF16)],
        compiler_params=_cparams(("arbitrary",)),
    )(h, g, win_t, *tabs)


def _band_mask(n, nb, transposed):
    shape = (3 * BLK, 2 * BLK) if transposed else (2 * BLK, 3 * BLK)
    i = lax.broadcasted_iota(jnp.int32, shape, 1 if transposed else 0) % BLK
    j = lax.broadcasted_iota(jnp.int32, shape, 0 if transposed else 1)
    kpos = (n - 1) * BLK + j
    return (j >= i) & (j <= i + 2 * BLK) & (kpos >= 0) & (kpos < nb * BLK)


def _block_diag(t, kh):
    tf = t.astype(F32)
    tr = pltpu.roll(tf, HEAD_DIM, 1)
    lo = lax.broadcasted_iota(jnp.int32, tf.shape, 1) < HEAD_DIM
    top, bot = (tf, tr) if kh == 0 else (tr, tf)
    return jnp.concatenate([jnp.where(lo, top, 0.0), jnp.where(lo, 0.0, bot)], axis=0).astype(BF16)


def _fold_diag(tbd):
    lo = lax.broadcasted_iota(jnp.int32, (3 * BLK, 2 * HEAD_DIM), 1) < HEAD_DIM
    t = jnp.where(lo, tbd[:3 * BLK], tbd[3 * BLK:])
    return t + pltpu.roll(t, HEAD_DIM, 1)


def _stack_pairs(x, kh):
    return jnp.concatenate([x[:, (2 * kh) * 128:(2 * kh + 1) * 128], x[:, (2 * kh + 1) * 128:(2 * kh + 2) * 128]], axis=0)


def _sink_of(sink_ref, kh, half, axis):
    shape = (2 * BLK, 1) if axis == 0 else (1, 2 * BLK)
    first = lax.broadcasted_iota(jnp.int32, shape, axis) < BLK
    return jnp.where(first, sink_ref[0, GROUP * kh + half], sink_ref[0, GROUP * kh + 2 + half])


def _softmax_sink(sc, valid, sink, axis):
    sc = jnp.where(valid, sc, -1e30)
    m = jnp.maximum(jnp.max(sc, axis=axis, keepdims=True), sink)
    e = jnp.exp(sc - m)
    es = jnp.exp(sink - m)
    inv = 1.0 / (jnp.sum(e, axis=axis, keepdims=True) + es)
    return e * inv, es * inv


def _attn_blocks_per_step(nb):
    return next(nq for nq in (4, 2, 1) if nb % nq == 0)


def _band_specs(nq, nb, w, col=0):
    return [pl.BlockSpec((BLK, w), lambda m: (jnp.maximum(nq * m - 1, 0), col)),
            pl.BlockSpec((nq * BLK, w), lambda m: (m, col)),
            pl.BlockSpec((BLK, w), lambda m: (jnp.minimum(nq * m + nq, nb - 1), col))]


def _attn_pool_fwd(q, k, v, pc, sink, pool_w, pool_scale, pband, name, carry=None, middle=CARRY_MIDDLE):
    s = q.shape[0]
    nb = s // BLK

    nq = _attn_blocks_per_step(nb)

    def body(sink_ref, q_ref, k0, k1, k2, v0, v1, v2, p0, p1, p2, pw_ref, ps_ref, pb_ref, o_ref):
        kall = jnp.concatenate([k0[...], k1[...], k2[...]], axis=0)
        vall = jnp.concatenate([v0[...], v1[...], v2[...]], axis=0)
        pall = jnp.concatenate([p0[...], p1[...], p2[...]], axis=0).astype(BF16)
        qall = q_ref[...] * SCORE_SCALE
        for j in range(nq):
            n = pl.program_id(0) * nq + j
            rows, band = slice(j * BLK, (j + 1) * BLK), slice(j * BLK, (j + 3) * BLK)
            valid = _band_mask(n, nb, False)
            kb, vb, qs = kall[band], vall[band], qall[rows]
            for kh in range(N_KV):
                sc = _dot_nt(_stack_pairs(qs, kh), _block_diag(kb, kh))
                p = [_softmax_sink(sc[:, half * 3 * BLK:(half + 1) * 3 * BLK], valid,
                                   _sink_of(sink_ref, kh, half, 0), 1)[0] for half in range(2)]
                o2 = _dot_nn(jnp.concatenate(p, axis=1).astype(BF16), _block_diag(vb, kh)).astype(BF16)
                o_ref[rows, (2 * kh) * 128:(2 * kh + 1) * 128] = o2[:BLK]
                o_ref[rows, (2 * kh + 1) * 128:(2 * kh + 2) * 128] = o2[BLK:]
            ext = pall[band]
            var = _variant_index(n, nb)
            for gi in range(POOL_G):
                gsl = slice(gi * POOL_GW, (gi + 1) * POOL_GW)
                dg = _dot_nn(pb_ref[var, gi], ext[:, gsl])
                yg = _dot_nn(dg.astype(BF16), pw_ref[gi].astype(BF16))
                o_ref[rows, ATTN_W + gi * POOL_GW:ATTN_W + (gi + 1) * POOL_GW] = (yg * ps_ref[:, gsl]).astype(BF16)

    return _call(
        body, name, (nb // nq,),
        [pl.BlockSpec(memory_space=pltpu.SMEM), pl.BlockSpec((nq * BLK, ATTN_W), lambda m: (m, 0)),
         *_band_specs(nq, nb, KV_W), *_band_specs(nq, nb, KV_W), *_band_specs(nq, nb, POOL_W),
         _whole((POOL_G, POOL_GW, POOL_GW)), _whole((1, POOL_W)), _whole(pband.shape)],
        [pl.BlockSpec((nq * BLK, ATTN_W + POOL_W), lambda m: (m, 0))],
        [jax.ShapeDtypeStruct((s, ATTN_W + POOL_W), BF16)],
        (sink, q, k, k, k, v, v, v, pc, pc, pc, pool_w, pool_scale, pband), ("arbitrary",), carry, middle)


def _attn_pool_bwd(q, k, v, pc, dmix, sink, pool_w, pool_scale, pband, ptband, name, carry=None):
    s = q.shape[0]
    nb = s // BLK
    nq = _attn_blocks_per_step(nb)

    def body(sink_ref, q_ref, k0, k1, k2, v0, v1, v2, p0, p1, p2, da_ref, d0, d1, d2, pw_ref, ps_ref, pb_ref, ptb_ref,
             dq_ref, dk_ref, dv_ref, dpc_ref, dsink_ref, dpw_ref, dps_ref):
        @pl.when(pl.program_id(0) == 0)
        def _():
            dsink_ref[...] = jnp.zeros_like(dsink_ref)
            dpw_ref[...] = jnp.zeros_like(dpw_ref)
            dps_ref[...] = jnp.zeros_like(dps_ref)

        kall = jnp.concatenate([k0[...], k1[...], k2[...]], axis=0)
        vall = jnp.concatenate([v0[...], v1[...], v2[...]], axis=0)
        pall = jnp.concatenate([p0[...], p1[...], p2[...]], axis=0).astype(BF16)
        dpall = jnp.concatenate([d0[...], d1[...], d2[...]], axis=0)
        lo = lax.broadcasted_iota(jnp.int32, (3 * BLK, KV_W), 1) < HEAD_DIM
        for j in range(nq):
            n = pl.program_id(0) * nq + j
            rows, band = slice(j * BLK, (j + 1) * BLK), slice(j * BLK, (j + 3) * BLK)
            valid = _band_mask(n, nb, True)
            kb, vb, qb = kall[band], vall[band], q_ref[rows, :]
            qs = qb * SCORE_SCALE
            da = da_ref[rows, :].astype(BF16)
            dk_fold, dv_fold = [], []
            for kh in range(N_KV):
                kbd, vbd = _block_diag(kb, kh), _block_diag(vb, kh)
                q2, do2 = _stack_pairs(qb, kh), _stack_pairs(da, kh)
                sc_t = _dot_nt(kbd, _stack_pairs(qs, kh))
                dp_t = _dot_nt(vbd, do2)
                p_t, ds_t = [], []
                for half in range(2):
                    keys = slice(half * 3 * BLK, (half + 1) * 3 * BLK)
                    p, ps = _softmax_sink(sc_t[keys], valid, _sink_of(sink_ref, kh, half, 1), 0)
                    delta = jnp.sum(p * dp_t[keys], axis=0, keepdims=True)
                    p_t.append(p.astype(BF16))
                    ds_t.append((p * (dp_t[keys] - delta)).astype(BF16))
                    dsk = -ps * delta
                    for pair in range(2):
                        h = GROUP * kh + 2 * pair + half
                        part = jnp.sum(dsk[:, pair * BLK:(pair + 1) * BLK], axis=1, keepdims=True)
                        dsink_ref[h:h + 1, :] += jnp.broadcast_to(part, (1, 128))
                p_t = jnp.concatenate(p_t, axis=0)
                ds_t = jnp.concatenate(ds_t, axis=0)
                dq2 = _dot_tn(ds_t, kbd) * SCORE_SCALE
                dq_ref[rows, (2 * kh) * 128:(2 * kh + 1) * 128] = dq2[:BLK]
                dq_ref[rows, (2 * kh + 1) * 128:(2 * kh + 2) * 128] = dq2[BLK:]
                dk_fold.append(_fold_diag(_dot_nn(ds_t, q2)) * SCORE_SCALE)
                dv_fold.append(_fold_diag(_dot_nn(p_t, do2)))
            dk_all = jnp.where(lo, dk_fold[0], dk_fold[1])
            dv_all = jnp.where(lo, dv_fold[0], dv_fold[1])
            for t in range(3):
                dk_ref[j, t] = dk_all[t * BLK:(t + 1) * BLK]
                dv_ref[j, t] = dv_all[t * BLK:(t + 1) * BLK]
            ext, dpe = pall[band], dpall[band]
            dpc_cur = dpall[(j + 1) * BLK:(j + 2) * BLK]
            var = _variant_index(n, nb)
            for gi in range(POOL_G):
                gsl = slice(gi * POOL_GW, (gi + 1) * POOL_GW)
                wg = pw_ref[gi].astype(BF16)
                sc = ps_ref[:, gsl]
                dgb = _dot_nn(pb_ref[var, gi], ext[:, gsl]).astype(BF16)
                yg = _dot_nn(dgb, wg)
                dps_ref[:, gsl] += jnp.sum(dpc_cur[:, gsl] * yg, axis=0, keepdims=True)
                dpw_ref[gi] += _dot_tn(dgb, (dpc_cur[:, gsl] * sc).astype(BF16))
                dd = _dot_nt((dpe[:, gsl] * sc).astype(BF16), wg)
                dpc_ref[rows, gsl] = _dot_nn(ptb_ref[var, gi], dd.astype(BF16))

    fixed = lambda shape: pl.BlockSpec(shape, lambda m: (0,) * len(shape))
    return _call(
        body, name, (nb // nq,),
        [pl.BlockSpec(memory_space=pltpu.SMEM), pl.BlockSpec((nq * BLK, ATTN_W), lambda m: (m, 0)),
         *_band_specs(nq, nb, KV_W), *_band_specs(nq, nb, KV_W), *_band_specs(nq, nb, POOL_W),
         pl.BlockSpec((nq * BLK, ATTN_W), lambda m: (m, 0)), *_band_specs(nq, nb, POOL_W, 1),
         _whole((POOL_G, POOL_GW, POOL_GW)), _whole((1, POOL_W)), _whole(pband.shape), _whole(ptband.shape)],
        [pl.BlockSpec((nq * BLK, ATTN_W), lambda m: (m, 0)),
         pl.BlockSpec((nq, 3, BLK, KV_W), lambda m: (m, 0, 0, 0)),
         pl.BlockSpec((nq, 3, BLK, KV_W), lambda m: (m, 0, 0, 0)),
         pl.BlockSpec((nq * BLK, POOL_W), lambda m: (m, 0)),
         fixed((N_HEADS, 128)), fixed((POOL_G, POOL_GW, POOL_GW)), fixed((1, POOL_W))],
        [jax.ShapeDtypeStruct((s, ATTN_W), F32), jax.ShapeDtypeStruct((nb, 3, BLK, KV_W), F32),
         jax.ShapeDtypeStruct((nb, 3, BLK, KV_W), F32), jax.ShapeDtypeStruct((s, POOL_W), F32),
         jax.ShapeDtypeStruct((N_HEADS, 128), F32),
         jax.ShapeDtypeStruct((POOL_G, POOL_GW, POOL_GW), F32), jax.ShapeDtypeStruct((1, POOL_W), F32)],
        (sink, q, k, k, k, v, v, v, pc, pc, pc, dmix, dmix, dmix, dmix, pool_w, pool_scale, pband, ptband),
        ("arbitrary",), carry)


def _mix_out(h, mix, w_out, name, tm, scale=1.0, carry=None, middle=CARRY_MIDDLE):
    s, d = h.shape
    w = mix.shape[1]

    def body(h_ref, m_ref, w_ref, o_ref):
        o_ref[...] = h_ref[...] + scale * _dot_nn(m_ref[...], w_ref[...])

    row = lambda c: pl.BlockSpec((tm, c), lambda i: (i, 0))
    return _call(body, name, (s // tm,), [row(d), row(w), _whole((w, d))], [row(d)],
                 [jax.ShapeDtypeStruct((s, d), F32)], (h, mix, w_out), ("arbitrary",), carry, middle)


def _mix_out_bwd(dh, w_out, name, tm):
    s, d = dh.shape
    w = w_out.shape[0]

    def body(dh_ref, w_ref, o_ref, dhb_ref):
        dhb = dh_ref[...].astype(BF16)
        dhb_ref[...] = dhb
        o_ref[...] = _dot_nt(dhb, w_ref[...])

    row = lambda c: pl.BlockSpec((tm, c), lambda i: (i, 0))
    return pl.pallas_call(
        body, name=name, grid=(s // tm,), in_specs=[row(d), _whole((w, d))], out_specs=[row(w), row(d)],
        out_shape=[jax.ShapeDtypeStruct((s, w), F32), jax.ShapeDtypeStruct((s, d), BF16)],
        compiler_params=_cparams(("arbitrary",)),
    )(dh, w_out)


def _mix_in_bwd(h, dh, g, win_t, dq, dkp, dvp, dpc, tabs, name, tm, carry=None):
    s, d = h.shape
    nb = s // BLK
    nt = tm // BLK
    n_in = win_t.shape[0]

    def band_sum(n, before, own, after, prev_last, next_first):
        lo = (n > 0).astype(F32)
        hi = (n < s // tm - 1).astype(F32)
        blocks = []
        for b in range(nt):
            from_prev = prev_last[...] * lo if b == 0 else before[b - 1]
            from_next = next_first[...] * hi if b == nt - 1 else after[b + 1]
            blocks.append(from_prev + own[b] + from_next)
        return jnp.concatenate(blocks, axis=0)

    def body(h_ref, dh_ref, g_ref, w_ref, dq_ref, k2, k1, k0, kp, kn, v2, v1, v0, vp, vn, dpc_ref, c_ref, sa_ref,
             sb_ref, o_ref, du_ref, dg_ref):
        n = pl.program_id(0)
        dk = band_sum(n, k2, k1, k0, kp, kn)
        dv = band_sum(n, v2, v1, v0, vp, vn)
        c, sa, sb = c_ref[...], sa_ref[...], sb_ref[...]
        du = jnp.concatenate([_rope_bwd(dq_ref[...], c, sa, sb, ATTN_W // 128), _rope_bwd(dk, c, sa, sb, 1), dv,
                              dpc_ref[...]], axis=1).astype(BF16)
        du_ref[...] = du
        dn = _dot_nn(du, w_ref[...])
        x = h_ref[...]
        r = lax.rsqrt(jnp.mean(x * x, axis=-1, keepdims=True) + EPS)
        xhat = x * r
        dxhat = dn * g_ref[...]
        o_ref[...] = dh_ref[...] + r * (dxhat - xhat * jnp.mean(dxhat * xhat, axis=-1, keepdims=True))

        @pl.when(n == 0)
        def _():
            dg_ref[...] = jnp.zeros_like(dg_ref)

        dg_ref[...] += jnp.sum(dn * xhat, axis=0, keepdims=True)

    row = lambda w: pl.BlockSpec((tm, w), lambda n: (n, 0))
    slot = lambda t: pl.BlockSpec((nt, None, BLK, KV_W), lambda n, t=t: (n, t, 0, 0))
    parts = [slot(2), slot(1), slot(0),
             pl.BlockSpec((None, None, BLK, KV_W), lambda n: (jnp.maximum(nt * n - 1, 0), 2, 0, 0)),
             pl.BlockSpec((None, None, BLK, KV_W), lambda n: (jnp.minimum(nt * n + nt, nb - 1), 0, 0, 0))]
    return _call(
        body, name, (s // tm,),
        [row(d), row(d), _whole((1, d)), _whole((n_in, d)), row(ATTN_W), *parts, *parts, row(POOL_W),
         row(128), row(128), row(128)],
        [row(d), row(n_in), pl.BlockSpec((1, d), lambda n: (0, 0))],
        [jax.ShapeDtypeStruct((s, d), F32), jax.ShapeDtypeStruct((s, n_in), BF16), jax.ShapeDtypeStruct((1, d), F32)],
        (h, dh, g, win_t, dq, *[dkp] * 5, *[dvp] * 5, dpc, *tabs), ("arbitrary",), carry)


def _adam_math(w, g, m, v):
    m = ADAM_B1 * m + (1.0 - ADAM_B1) * g
    v = ADAM_B2 * v + (1.0 - ADAM_B2) * (g * g)
    m_hat = m / (1.0 - ADAM_B1 ** ADAM_STEP)
    v_hat = v / (1.0 - ADAM_B2 ** ADAM_STEP)
    delta = -ADAM_LR * (m_hat / (jnp.sqrt(v_hat) + ADAM_EPS) + ADAM_WD * w)
    return delta, m, v


def _adam_small(w, parts, late, m, v, name):
    rows, cols = w.shape

    def body(w_ref, p_ref, l_ref, m_ref, v_ref, g_ref, d_ref, nm_ref, nv_ref):
        g, gl = p_ref[0], l_ref[0]
        for k in range(1, N_DEV):
            g = g + p_ref[k]
            gl = gl + l_ref[k]
        g_ref[...] = g
        g_ref[SMALL_NORM1:SMALL_NORM1 + 8, :] = g[SMALL_NORM1:SMALL_NORM1 + 8] + gl
        d_ref[...], nm_ref[...], nv_ref[...] = _adam_math(w_ref[...], g_ref[...], m_ref[...], v_ref[...])

    return pl.pallas_call(
        body, name=name, out_shape=[jax.ShapeDtypeStruct((rows, cols), F32)] * 4,
    )(w, parts, late, m, v)


SMALL_NORM1 = 512


def _pack_small(norm1, normm, norm2, normf, sink, pool_w, pool_scale, loss=None):
    scale_rows = jnp.pad(pool_scale.reshape(4, 128), ((0, 4), (0, 0)))
    last_rows = jnp.pad(sink.reshape(1, N_HEADS), ((0, 7), (0, 128 - N_HEADS)))
    if loss is not None:
        last_rows = last_rows + jnp.pad(loss.reshape(1, 1), ((1, 6), (0, 127)))
    return jnp.concatenate([pool_w.reshape(512, 128), norm1.reshape(8, 128), normm.reshape(8, 128),
                            norm2.reshape(8, 128), normf.reshape(8, 128), scale_rows, last_rows], axis=0)


def _unpack_small(p):
    return dict(pool_w=p[:512].reshape(1, POOL_G, POOL_GW, POOL_GW), ffn1_norm=p[512:520].reshape(1, 1024),
                mix_norm=p[520:528].reshape(1, 1024), ffn2_norm=p[528:536].reshape(1, 1024),
                final_norm=p[536:544].reshape(1024), pool_scale=p[544:548].reshape(1, POOL_W),
                sink_logits=p[552, :N_HEADS].reshape(1, N_HEADS), loss=p[553, 0])


def kernel(x, ffn1_norm, ffn1_w_gate, ffn1_w_up, ffn1_w_down, mix_norm, w_in, sink_logits, pool_w, pool_scale, w_out, ffn2_norm, ffn2_w_gate, ffn2_w_up, ffn2_w_down, final_norm, loss_target, m_ffn1_norm, m_ffn1_w_gate, m_ffn1_w_up, m_ffn1_w_down, m_mix_norm, m_w_in, m_sink_logits, m_pool_w, m_pool_scale, m_w_out, m_ffn2_norm, m_ffn2_w_gate, m_ffn2_w_up, m_ffn2_w_down, m_final_norm, v_ffn1_norm, v_ffn1_w_gate, v_ffn1_w_up, v_ffn1_w_down, v_mix_norm, v_w_in, v_sink_logits, v_pool_w, v_pool_scale, v_w_out, v_ffn2_norm, v_ffn2_w_gate, v_ffn2_w_up, v_ffn2_w_down, v_final_norm):
    s, d = x.shape[1], x.shape[2]
    fk = ffn1_w_gate.shape[2]
    f = N_DEV * fk
    ink = w_in.shape[2]
    n_in = N_DEV * ink
    mixk = w_out.shape[1]
    tm = min(512, s)
    tm_bwd = min(256, s)
    pos = jnp.stack([lax.axis_index("x"), lax.axis_index("y"), lax.axis_index("c")]).astype(jnp.int32)

    t_bf = lambda w: w[0].T.astype(BF16)
    full = lambda a: a.reshape(N_DEV * a.shape[1], d)
    wg1, wu1 = map(full, _run_exchange(_AllGather([t_bf(ffn1_w_gate), t_bf(ffn1_w_up)]), "gather_ffn1"))
    second = _AllGather([ffn1_w_down[0].astype(BF16), t_bf(w_in), w_out[0].astype(BF16)])
    third = _AllGather([t_bf(ffn2_w_gate)])
    fourth = _AllGather([t_bf(ffn2_w_up), ffn2_w_down[0].astype(BF16)])

    tabs = _rope_tables(s)
    pband, ptband = _pool_tables(s)
    g1, gm, g2, gf = ffn1_norm, mix_norm, ffn2_norm, final_norm.reshape(1, d)

    x0 = x[0]
    n1, gate1, up1, act1, *gathered = _ffn_fwd(x0, g1, wg1, wu1, None, "ffn1_gate_up", tm, carry=second, middle=None)
    wd1, win_t, wout = map(full, gathered)
    h1, wg2 = _mix_out(x0, act1, wd1, "ffn1_down", tm, scale=0.5, carry=third, middle=None)
    wg2 = full(wg2)
    q, k, v, pc, n2 = _mix_in(h1, gm, win_t, tabs, "mix_in", tm)
    mix, *gathered = _attn_pool_fwd(q, k, v, pc, sink_logits, pool_w[0], pool_scale, pband, "attn_pool_fwd",
                                    carry=fourth, middle=None)
    wu2, wd2 = map(full, gathered)
    (h2,) = _mix_out(h1, mix, wout, "mix_out", tm)
    dh3, n3, gate2, up2, act2, loss_part, dgf = _ffn_fwd(h2, g2, wg2, wu2, wd2, "ffn2_fwd", tm,
                                                         head=(gf, loss_target[0]))

    tw = 256 if f % 256 == 0 else 128
    gw, sum1, recv2 = {}, {}, {}

    def stage1(keys, rows):
        sums = _rs_stage1_sum(pos, [gw[key][0].reshape(N_DEV, rows, d) for key in keys],
                              [gw[key][1].reshape(N_DEV, rows, d) for key in keys], "rs1_" + keys[0])
        sum1.update(zip(keys, sums))

    def stage2(keys):
        return _RsStage2([sum1[key][1] for key in keys])

    dh2, dg2, dgate2, dup2, dhh3 = _ffn_bwd(h2, dh3, gate2, up2, g2, wg2, wu2, wd2, "ffn2_bwd", tm_bwd)
    gw["g2"] = _wgrad(dgate2, n3, "wgrad_gate2", tw)
    gw["u2"] = _wgrad(dup2, n3, "wgrad_up2", tw)
    gw["d2"] = _wgrad(act2, dhh3, "wgrad_down2", tw)
    stage1(["g2", "u2", "d2"], fk)
    dmix, dh2b = _mix_out_bwd(dh2, wout, "mix_out_bwd", tm)
    gw["out"] = _wgrad(mix, dh2b, "wgrad_out", tw)
    stage1(["out"], mixk)
    dq, dkp, dvp, dpc, dsink, dpw, dps, *r2 = _attn_pool_bwd(
        q, k, v, pc, dmix, sink_logits, pool_w[0], pool_scale, pband, ptband, "attn_pool_bwd",
        carry=stage2(["g2", "u2", "d2", "out"]))
    recv2.update(zip(["g2", "u2", "d2", "out"], r2))
    dh1, du, dgm = _mix_in_bwd(h1, dh2, gm, win_t, dq, dkp, dvp, dpc, tabs, "mix_in_bwd", tm)
    gw["in"] = _wgrad(du, n2, "wgrad_in", tw)
    stage1(["in"], ink)
    small_part = _pack_small(jnp.zeros_like(dgm), dgm, dg2, dgf, dsink[:, 0], dpw, dps, loss_part)
    dgate1, dup1, dhh1, recv2["in"], small_all = _ffn_bwd_gates(
        dh1, gate1, up1, wd1, "ffn1_bwd_gates", tm, carry=_Both(stage2(["in"]), _AllGather([small_part])))
    gw["g1"] = _wgrad(dgate1, n1, "wgrad_gate1", tw)
    stage1(["g1"], fk)
    *gw["u1"], recv2["g1"] = _wgrad(dup1, n1, "wgrad_up1", tw, carry=stage2(["g1"]))
    stage1(["u1"], fk)
    *gw["d1"], recv2["u1"] = _wgrad(act1, dhh1, "wgrad_down1", tw, carry=stage2(["u1"]))
    stage1(["d1"], fk)
    dx, dg1, recv2["d1"] = _ffn_bwd_input(x0, dh1, dgate1, dup1, g1, wg1, wu1, "ffn1_bwd_input", tm,
                                          carry=stage2(["d1"]))

    (dg1_all,) = _run_exchange(_DirectGather([dg1.reshape(8, 128)]), "gather_norm1_grad")
    pk = lambda a, b, c_, e, s_, pw_, psc: _pack_small(a, b, c_, e, s_[0], pw_[0], psc)
    small_w = pk(ffn1_norm, mix_norm, ffn2_norm, final_norm, sink_logits, pool_w, pool_scale)
    small_m = pk(m_ffn1_norm, m_mix_norm, m_ffn2_norm, m_final_norm, m_sink_logits, m_pool_w, m_pool_scale)
    small_v = pk(v_ffn1_norm, v_mix_norm, v_ffn2_norm, v_final_norm, v_sink_logits, v_pool_w, v_pool_scale)
    sg, sd, sm, sv = [_unpack_small(a)
                      for a in _adam_small(small_w, small_all, dg1_all, small_m, small_v, "adam_small")]

    big = {}
    keys = ["g1", "u1", "d1", "g2", "u2", "d2", "in", "out"]
    names = ["ffn1_w_gate", "ffn1_w_up", "ffn1_w_down", "ffn2_w_gate", "ffn2_w_up", "ffn2_w_down", "w_in", "w_out"]
    transposed = [True, True, False, True, True, False, True, False]
    ws = [ffn1_w_gate, ffn1_w_up, ffn1_w_down, ffn2_w_gate, ffn2_w_up, ffn2_w_down, w_in, w_out]
    ms = [m_ffn1_w_gate, m_ffn1_w_up, m_ffn1_w_down, m_ffn2_w_gate, m_ffn2_w_up, m_ffn2_w_down, m_w_in, m_w_out]
    vs = [v_ffn1_w_gate, v_ffn1_w_up, v_ffn1_w_down, v_ffn2_w_gate, v_ffn2_w_up, v_ffn2_w_down, v_w_in, v_w_out]
    for key, nm, tr, w, m, vv in zip(keys, names, transposed, ws, ms, vs):
        view = (lambda a: jnp.swapaxes(a, 1, 2)[0]) if tr else (lambda a: a[0])
        back = (lambda a: jnp.swapaxes(a[None], 1, 2)) if tr else (lambda a: a[None])
        res = _rs_sum2_adam(pos, sum1[key][0], recv2[key], view(w), view(m), view(vv), "adam_" + nm)
        big[nm] = tuple(back(a) for a in res)

    loss = sg["loss"]
    all_names = ["ffn1_norm", "ffn1_w_gate", "ffn1_w_up", "ffn1_w_down", "mix_norm", "w_in", "sink_logits", "pool_w",
                 "pool_scale", "w_out", "ffn2_norm", "ffn2_w_gate", "ffn2_w_up", "ffn2_w_down", "final_norm"]
    outs = [loss, dx[None]]
    for idx, src in enumerate((sg, sd, sm, sv)):
        for nm in all_names:
            outs.append(big[nm][idx] if nm in big else src[nm])
    return tuple(outs)
```

```python
import functools

import jax
import jax.numpy as jnp
import numpy as np
from jax import lax
from jax.experimental import pallas as pl
from jax.experimental.pallas import tpu as pltpu

F32 = jnp.float32
BF16 = jnp.bfloat16
MESH = pl.DeviceIdType.MESH
N_DEV = 8

EPS = 1e-6
HEAD_DIM = 64
N_HEADS = 8
N_KV = 2
GROUP = N_HEADS // N_KV
ATTN_W = N_HEADS * HEAD_DIM
KV_W = N_KV * HEAD_DIM
POOL_W = 512
POOL_G = 4
POOL_GW = POOL_W // POOL_G
POOL_WINDOWS = (2, 4, 8, 16)
BLK = 128
ROT = 16
ROPE_THETA = 500000.0
SCORE_SCALE = HEAD_DIM ** -0.5

ADAM_LR, ADAM_B1, ADAM_B2, ADAM_EPS, ADAM_WD, ADAM_STEP = 0.001, 0.9, 0.999, 1e-08, 0.01, 10

VMEM_LIMIT = 56 * 1024 * 1024


def _cparams(sem=None, **kw):
    if sem is not None:
        kw["dimension_semantics"] = sem
    return pltpu.CompilerParams(vmem_limit_bytes=VMEM_LIMIT, **kw)


def _whole(shape):
    nd = len(shape)
    return pl.BlockSpec(shape, lambda *_: (0,) * nd, pipeline_mode=pl.Buffered(1))


def _sigmoid(z):
    return 1.0 / (1.0 + jnp.exp(-z))


def _dot_nt(a, b):
    return lax.dot_general(a, b, (((1,), (1,)), ((), ())), preferred_element_type=F32)


def _dot_nn(a, b):
    return lax.dot_general(a, b, (((1,), (0,)), ((), ())), preferred_element_type=F32)


def _dot_tn(a, b):
    return lax.dot_general(a, b, (((0,), (0,)), ((), ())), preferred_element_type=F32)


def _rope_tables(s):
    inv_freq = ROPE_THETA ** (-np.arange(0, ROT, 2, dtype=np.float64) / ROT)
    ang = np.arange(s, dtype=np.float64)[:, None] * inv_freq[None, :]
    c = np.ones((s, HEAD_DIM)); sa = np.zeros((s, HEAD_DIM)); sb = np.zeros((s, HEAD_DIM))
    c[:, :8] = np.cos(ang); c[:, 8:16] = np.cos(ang)
    sa[:, :8] = -np.sin(ang)
    sb[:, 8:16] = np.sin(ang)
    t = lambda a: jnp.asarray(np.tile(a, (1, 2)).astype(np.float32))
    return t(c), t(sa), t(sb)


def _pool_weight(gi, t, s_pos, s):
    half = POOL_WINDOWS[gi] // 2

    def win(lo, hi):
        a = np.clip(lo, 0, s); b = np.clip(hi + 1, 0, s)
        inside = (s_pos >= a) & (s_pos < b)
        return inside / np.maximum(b - a, 1)

    w = 0.5 * (win(t - half, t + half - 1) + win(t - half + 1, t + half)) - (t == s_pos)
    return w * ((t >= 0) & (t < s) & (s_pos >= 0) & (s_pos < s))


def _pool_tables(s):
    nb = s // BLK
    fwd = np.zeros((3, POOL_G, BLK, 3 * BLK), np.float32)
    bwd = np.zeros((3, POOL_G, BLK, 3 * BLK), np.float32)
    for vi, n in enumerate((0, 1 if nb > 2 else 0, nb - 1)):
        i = n * BLK + np.arange(BLK)[:, None]
        j = (n - 1) * BLK + np.arange(3 * BLK)[None, :]
        for gi in range(POOL_G):
            fwd[vi, gi] = _pool_weight(gi, i, j, s)
            bwd[vi, gi] = _pool_weight(gi, j, i, s)
    return jnp.asarray(fwd, dtype=BF16), jnp.asarray(bwd, dtype=BF16)


def _variant_index(n, nb):
    return jnp.where(n == 0, 0, jnp.where(n == nb - 1, 2, 1))


class _Exchange:
    inputs = ()
    out_shapes = ()
    sems = ()

    def start(self, srcs, outs, sems):
        raise NotImplementedError

    def middle(self, srcs, outs, sems):
        pass

    def finish(self, srcs, outs, sems):
        raise NotImplementedError


class _AllGather(_Exchange):
    def __init__(self, arrays):
        n = len(arrays)
        self.inputs = list(arrays)
        self.out_shapes = [jax.ShapeDtypeStruct((N_DEV,) + a.shape, a.dtype) for a in arrays]
        self.sems = [pltpu.SemaphoreType.DMA((n, 7)), pltpu.SemaphoreType.DMA((n, 7)), pltpu.SemaphoreType.DMA((n,))]

    def _parts(self, srcs, outs, sems):
        send_sems, recv_sems, local_sems = sems
        n = len(srcs)
        x, y, c = lax.axis_index("x"), lax.axis_index("y"), lax.axis_index("c")
        me, sibling = (x, y, c), (x, y, 1 - c)
        chips = [(1 - x, y), (x, 1 - y), (1 - x, 1 - y)]

        def slot(a, dev):
            return outs[a].at[4 * dev[0] + 2 * dev[1] + dev[2]]

        def copy(a, k, block, to, src=None):
            return pltpu.make_async_remote_copy(
                src_ref=slot(a, block) if src is None else src, dst_ref=slot(a, block),
                send_sem=send_sems.at[a, k], recv_sem=recv_sems.at[a, k], device_id=to, device_id_type=MESH)

        def mine():
            return [pltpu.make_async_copy(srcs[a], slot(a, me), local_sems.at[a]) for a in range(n)]

        def first():
            return [copy(a, 0, me, sibling, src=srcs[a]) for a in range(n)] + [
                copy(a, 1 + j, me, (*chip, c), src=srcs[a]) for a in range(n) for j, chip in enumerate(chips)]

        return n, c, me, sibling, chips, copy, mine, first

    def start(self, srcs, outs, sems):
        _, _, _, _, _, _, mine, first = self._parts(srcs, outs, sems)
        for cp in mine() + first():
            cp.start()

    def middle(self, srcs, outs, sems):
        n, c, me, sibling, chips, copy, _, _ = self._parts(srcs, outs, sems)
        for j, chip in enumerate(chips):
            for a in range(n):
                copy(a, 1 + j, (*chip, c), me).wait_recv()
                copy(a, 4 + j, (*chip, c), sibling).start()

    def finish(self, srcs, outs, sems):
        n, c, me, sibling, chips, copy, mine, first = self._parts(srcs, outs, sems)
        mine, first = mine(), first()
        passed = [copy(a, 4 + j, (*chip, c), sibling) for j, chip in enumerate(chips) for a in range(n)]
        for a in range(n):
            copy(a, 0, sibling, me).wait_recv()
            for j, chip in enumerate(chips):
                copy(a, 4 + j, (*chip, 1 - c), me).wait_recv()
        for cp in first + passed:
            cp.wait_send()
        for cp in mine:
            cp.wait()


class _RsStage2(_Exchange):
    def start(self, srcs, outs, sems):
        for cp in self._copies(srcs, outs, sems):
            cp.start()

    def finish(self, srcs, outs, sems):
        copies = self._copies(srcs, outs, sems)
        for cp in copies:
            cp.wait_recv()
        for cp in copies:
            cp.wait_send()


    def __init__(self, pbs):
        n = len(pbs)
        self.inputs = list(pbs)
        self.out_shapes = [jax.ShapeDtypeStruct((3,) + p.shape[1:], p.dtype) for p in pbs]
        self.sems = [pltpu.SemaphoreType.DMA((n, 3)), pltpu.SemaphoreType.DMA((n, 3))]

    def _copies(self, srcs, outs, sems):
        send_sems, recv_sems = sems
        x, y, c = lax.axis_index("x"), lax.axis_index("y"), lax.axis_index("c")
        chips = [(1 - x, y), (x, 1 - y), (1 - x, 1 - y)]
        return [pltpu.make_async_remote_copy(
            src_ref=srcs[a].at[2 * chip[0] + chip[1]], dst_ref=outs[a].at[j], send_sem=send_sems.at[a, j],
            recv_sem=recv_sems.at[a, j], device_id=(*chip, c), device_id_type=MESH)
            for a in range(len(srcs)) for j, chip in enumerate(chips)]


class _DirectGather(_Exchange):
    def __init__(self, arrays):
        n = len(arrays)
        self.inputs = list(arrays)
        self.out_shapes = [jax.ShapeDtypeStruct((N_DEV,) + a.shape, a.dtype) for a in arrays]
        self.sems = [pltpu.SemaphoreType.DMA((n, 7)), pltpu.SemaphoreType.DMA((n, 7)), pltpu.SemaphoreType.DMA((n,))]

    def _copies(self, srcs, outs, sems):
        send_sems, recv_sems, local_sems = sems
        x, y, c = lax.axis_index("x"), lax.axis_index("y"), lax.axis_index("c")
        me = 4 * x + 2 * y + c
        remote, local = [], []
        for a in range(len(srcs)):
            local.append(pltpu.make_async_copy(srcs[a], outs[a].at[me], local_sems.at[a]))
            for k in range(1, N_DEV):
                peer = (x ^ (k >> 2), y ^ ((k >> 1) & 1), c ^ (k & 1))
                remote.append(pltpu.make_async_remote_copy(
                    src_ref=srcs[a], dst_ref=outs[a].at[me], send_sem=send_sems.at[a, k - 1],
                    recv_sem=recv_sems.at[a, k - 1], device_id=peer, device_id_type=MESH))
        return remote, local

    def start(self, srcs, outs, sems):
        remote, local = self._copies(srcs, outs, sems)
        for cp in local + remote:
            cp.start()

    def finish(self, srcs, outs, sems):
        remote, local = self._copies(srcs, outs, sems)
        for cp in remote:
            cp.wait_recv()
        for cp in remote:
            cp.wait_send()
        for cp in local:
            cp.wait()


class _Both(_Exchange):
    def __init__(self, a, b):
        self.a, self.b = a, b
        self.inputs = list(a.inputs) + list(b.inputs)
        self.out_shapes = list(a.out_shapes) + list(b.out_shapes)
        self.sems = list(a.sems) + list(b.sems)

    def _split(self, srcs, outs, sems):
        na, oa, sa = len(self.a.inputs), len(self.a.out_shapes), len(self.a.sems)
        return (srcs[:na], outs[:oa], sems[:sa]), (srcs[na:], outs[oa:], sems[sa:])

    def start(self, srcs, outs, sems):
        pa, pb = self._split(srcs, outs, sems)
        self.a.start(*pa)
        self.b.start(*pb)

    def middle(self, srcs, outs, sems):
        pa, pb = self._split(srcs, outs, sems)
        self.a.middle(*pa)
        self.b.middle(*pb)

    def finish(self, srcs, outs, sems):
        pa, pb = self._split(srcs, outs, sems)
        self.a.finish(*pa)
        self.b.finish(*pb)


_ANY = pl.BlockSpec(memory_space=pl.ANY)


def _run_exchange(ex, name):
    n_in, n_out = len(ex.inputs), len(ex.out_shapes)

    def body(*refs):
        srcs, outs, sems = refs[:n_in], refs[n_in:n_in + n_out], refs[n_in + n_out:]
        ex.start(srcs, outs, sems)
        ex.middle(srcs, outs, sems)
        ex.finish(srcs, outs, sems)

    return pl.pallas_call(
        body, name=name, out_shape=list(ex.out_shapes), in_specs=[_ANY] * n_in, out_specs=[_ANY] * n_out,
        scratch_shapes=list(ex.sems),
    )(*ex.inputs)


CARRY_MIDDLE = 0.85


def _call(body, name, grid, in_specs, out_specs, out_shape, args, sem, carry=None, middle=CARRY_MIDDLE, scratch=()):
    if carry is None:
        return pl.pallas_call(functools.partial(body), name=name, grid=grid, in_specs=in_specs, out_specs=out_specs,
                              out_shape=out_shape, scratch_shapes=list(scratch), compiler_params=_cparams(sem))(*args)
    n_in, n_out = len(in_specs), len(out_specs)
    nc_in, nc_out = len(carry.inputs), len(carry.out_shapes)

    def carried(*refs):
        ins = refs[:n_in]
        c_in = refs[n_in:n_in + nc_in]
        outs = refs[n_in + nc_in:n_in + nc_in + n_out]
        c_out = refs[n_in + nc_in + n_out:n_in + nc_in + n_out + nc_out]
        own = refs[n_in + nc_in + n_out + nc_out:n_in + nc_in + n_out + nc_out + len(scratch)]
        sems = refs[n_in + nc_in + n_out + nc_out + len(scratch):]
        ids = [pl.program_id(i) for i in range(len(grid))]
        is_first = functools.reduce(jnp.logical_and, [i == 0 for i in ids])
        is_last = functools.reduce(jnp.logical_and, [i == g - 1 for i, g in zip(ids, grid)])
        @pl.when(is_first)
        def _():
            carry.start(c_in, c_out, sems)

        if middle is not None:
            @pl.when(functools.reduce(jnp.logical_and, [ids[0] == round(middle * (grid[0] - 1))]
                                      + [i == 0 for i in ids[1:]]))
            def _():
                carry.middle(c_in, c_out, sems)

        body(*ins, *outs, *own)

        @pl.when(is_last)
        def _():
            if middle is None:
                carry.middle(c_in, c_out, sems)
            carry.finish(c_in, c_out, sems)

    return pl.pallas_call(
        carried, name=name, grid=grid, in_specs=list(in_specs) + [_ANY] * nc_in,
        out_specs=list(out_specs) + [_ANY] * nc_out, out_shape=list(out_shape) + list(carry.out_shapes),
        scratch_shapes=list(scratch) + list(carry.sems), compiler_params=_cparams(sem))(*args, *carry.inputs)


def _rs_stage1_sum(pos, gs, gbs, name):
    n = len(gs)
    _, rows, d = gs[0].shape

    def block(pc, q, other_core):
        return 4 * (q // 2) + 2 * (q % 2) + ((1 - pc) if other_core else pc)

    def body(pos_ref, *refs):
        g_refs, gb_refs = refs[:n], refs[n:2 * n]
        p_refs, pb_refs = refs[2 * n:3 * n], refs[3 * n:4 * n]
        land, send_sems, recv_sems = refs[4 * n:]
        q = pl.program_id(0)
        x, y, c = lax.axis_index("x"), lax.axis_index("y"), lax.axis_index("c")

        def copy(a, k):
            return pltpu.make_async_remote_copy(
                src_ref=gb_refs[a].at[block(c, k, True)], dst_ref=land.at[a, k],
                send_sem=send_sems.at[a, k], recv_sem=recv_sems.at[a, k], device_id=(x, y, 1 - c),
                device_id_type=MESH)

        @pl.when(q == 0)
        def _():
            for k in range(4):
                for a in range(n):
                    copy(a, k).start()

        for a in range(n):
            copy(a, q).wait_recv()
            p = g_refs[a][...] + land[a, q].astype(F32)
            p_refs[a][...] = p
            pb_refs[a][...] = p.astype(BF16)

        @pl.when(q == 3)
        def _():
            for k in range(4):
                for a in range(n):
                    copy(a, k).wait_send()

    blk = pl.BlockSpec((None, rows, d), lambda q, pos_ref: (q, 0, 0))
    mine = pl.BlockSpec((None, rows, d), lambda q, pos_ref: (block(pos_ref[2], q, False), 0, 0))
    res = pl.pallas_call(
        body, name=name,
        grid_spec=pltpu.PrefetchScalarGridSpec(
            num_scalar_prefetch=1, grid=(4,),
            in_specs=[mine] * n + [_ANY] * n, out_specs=[blk] * (2 * n),
            scratch_shapes=[pltpu.VMEM((n, 4, rows, d), BF16), pltpu.SemaphoreType.DMA((n, 4)),
                            pltpu.SemaphoreType.DMA((n, 4))]),
        out_shape=[jax.ShapeDtypeStruct((4, rows, d), F32)] * n + [jax.ShapeDtypeStruct((4, rows, d), BF16)] * n,
        compiler_params=_cparams(("arbitrary",)),
    )(pos, *gs, *gbs)
    return list(zip(res[:n], res[n:]))


def _rs_sum2_adam(pos, p, r2, w, m, v, name):
    _, rows, d = p.shape
    tr = rows // 2 if rows % 16 == 0 else rows

    def body(pos_ref, p_ref, r_ref, w_ref, m_ref, v_ref, g_ref, d_ref, nm_ref, nv_ref):
        r = r_ref[...].astype(F32)
        g = ((p_ref[...] + r[0]) + r[1]) + r[2]
        g_ref[...] = g
        d_ref[...], nm_ref[...], nv_ref[...] = _adam_math(w_ref[...], g, m_ref[...], v_ref[...])

    blk = pl.BlockSpec((tr, d), lambda i, pos_ref: (i, 0))
    return pl.pallas_call(
        body, name=name,
        grid_spec=pltpu.PrefetchScalarGridSpec(
            num_scalar_prefetch=1, grid=(rows // tr,),
            in_specs=[pl.BlockSpec((None, tr, d), lambda i, pos_ref: (2 * pos_ref[0] + pos_ref[1], i, 0)),
                      pl.BlockSpec((3, tr, d), lambda i, pos_ref: (0, i, 0)), blk, blk, blk],
            out_specs=[blk] * 4),
        out_shape=[jax.ShapeDtypeStruct((rows, d), F32)] * 4,
        compiler_params=_cparams(("arbitrary",)),
    )(pos, p, r2, w, m, v)


def _ffn_chunk(f):
    for cand in (256, 128):
        if f % cand == 0:
            return cand
    return f


def _loss_head(x, gg, target, loss_ref, dg_ref):
    @pl.when(pl.program_id(0) == 0)
    def _():
        loss_ref[...] = jnp.zeros_like(loss_ref)
        dg_ref[...] = jnp.zeros_like(dg_ref)

    r = lax.rsqrt(jnp.mean(x * x, axis=-1, keepdims=True) + EPS)
    xhat = x * r
    e = xhat * gg - target
    loss_ref[...] += 0.5 * jnp.sum(jnp.mean(e * e, axis=-1, keepdims=True), axis=0, keepdims=True)
    dy = e * (1.0 / x.shape[-1])
    dg_ref[...] += jnp.sum(dy * xhat, axis=0, keepdims=True)
    dxhat = dy * gg
    return r * (dxhat - xhat * jnp.mean(dxhat * xhat, axis=-1, keepdims=True))


def _ffn_fwd(h, g, wg_t, wu_t, wd, name, tm, carry=None, head=None, middle=CARRY_MIDDLE):
    s, d = h.shape
    f = wg_t.shape[0]
    tf = _ffn_chunk(f)

    def body(h_ref, g_ref, wg_ref, wu_ref, *refs):
        if wd is None:
            n_ref, gate_ref, up_ref, act_t_ref, act_ref = refs
        elif head is None:
            wd_ref, o_ref, n_ref, gate_ref, up_ref, act_t_ref, act_ref = refs
        else:
            wd_ref, gf_ref, t_ref, o_ref, n_ref, gate_ref, up_ref, act_t_ref, loss_ref, dgf_ref, act_ref = refs
        x = h_ref[...]
        r = lax.rsqrt(jnp.mean(x * x, axis=-1, keepdims=True) + EPS)
        nb = (x * r * g_ref[...]).astype(BF16)
        n_ref[...] = nb
        for j in range(f // tf):
            sl = slice(j * tf, (j + 1) * tf)
            gate = _dot_nt(nb, wg_ref[sl, :])
            up = _dot_nt(nb, wu_ref[sl, :])
            gate_ref[:, sl] = gate.astype(BF16)
            up_ref[:, sl] = up.astype(BF16)
            act = gate * _sigmoid(gate) * up
            act_ref[:, sl] = act.astype(BF16)
            act_t_ref[sl, :] = act.T.astype(BF16)
        if wd is not None:
            h_out = x + 0.5 * _dot_nn(act_ref[...], wd_ref[...])
            o_ref[...] = h_out if head is None else _loss_head(h_out, gf_ref[...], t_ref[...], loss_ref, dgf_ref)

    row = lambda w: pl.BlockSpec((tm, w), lambda i: (i, 0))
    in_specs = [row(d), _whole((1, d)), _whole((f, d)), _whole((f, d))]
    out_specs = [row(d), row(f), row(f), pl.BlockSpec((f, tm), lambda i: (0, i))]
    out_shape = ([jax.ShapeDtypeStruct((s, d), BF16)] + [jax.ShapeDtypeStruct((s, f), BF16)] * 2
                 + [jax.ShapeDtypeStruct((f, s), BF16)])
    args = (h, g, wg_t, wu_t)
    scratch = ()
    if wd is None:
        out_specs, out_shape = out_specs + [row(f)], out_shape + [jax.ShapeDtypeStruct((s, f), BF16)]
    else:
        scratch = (pltpu.VMEM((tm, f), BF16),)
        in_specs, args = in_specs + [_whole((f, d))], args + (wd,)
        out_specs, out_shape = [row(d)] + out_specs, [jax.ShapeDtypeStruct((s, d), F32)] + out_shape
    if head is not None:
        in_specs += [_whole((1, d)), row(d)]
        out_specs += [pl.BlockSpec((1, 1), lambda i: (0, 0)), pl.BlockSpec((1, d), lambda i: (0, 0))]
        out_shape += [jax.ShapeDtypeStruct((1, 1), F32), jax.ShapeDtypeStruct((1, d), F32)]
        args += tuple(head)
    return _call(body, name, (s // tm,), in_specs, out_specs, out_shape, args, ("arbitrary",), carry, middle, scratch)


def _gate_grads(dh_ref, gate_ref, up_ref, wd_ref, dgate_ref, dup_ref, dgate_t_ref, dup_t_ref, dhh_ref, tf):
    dhh = (0.5 * dh_ref[...]).astype(BF16)
    dhh_ref[...] = dhh
    for j in range(gate_ref.shape[1] // tf):
        sl = slice(j * tf, (j + 1) * tf)
        gt = gate_ref[:, sl].astype(F32)
        u = up_ref[:, sl].astype(F32)
        dact = _dot_nt(dhh, wd_ref[sl, :])
        sg = _sigmoid(gt)
        dup = dact * (gt * sg)
        dgate = dact * u * (sg * (1.0 + gt * (1.0 - sg)))
        dup_ref[:, sl] = dup.astype(BF16)
        dgate_ref[:, sl] = dgate.astype(BF16)
        dup_t_ref[sl, :] = dup.T.astype(BF16)
        dgate_t_ref[sl, :] = dgate.T.astype(BF16)


def _input_grad(h_ref, dh_ref, dgate_ref, dup_ref, g_ref, wg_ref, wu_ref, o_ref, dg_ref):
    x = h_ref[...]
    r = lax.rsqrt(jnp.mean(x * x, axis=-1, keepdims=True) + EPS)
    xhat = x * r
    dn = _dot_nn(dgate_ref[...], wg_ref[...]) + _dot_nn(dup_ref[...], wu_ref[...])
    dxhat = dn * g_ref[...]
    o_ref[...] = dh_ref[...] + r * (dxhat - xhat * jnp.mean(dxhat * xhat, axis=-1, keepdims=True))

    @pl.when(pl.program_id(0) == 0)
    def _():
        dg_ref[...] = jnp.zeros_like(dg_ref)

    dg_ref[...] += jnp.sum(dn * xhat, axis=0, keepdims=True)


def _ffn_bwd(h_in, dh_out, gate, up, g, wg_t, wu_t, wd, name, tm):
    s, d = h_in.shape
    f = gate.shape[1]
    tf = _ffn_chunk(f)

    def body(h_ref, dh_ref, gate_ref, up_ref, g_ref, wg_ref, wu_ref, wd_ref,
             o_ref, dg_ref, dgate_t_ref, dup_t_ref, dhh_ref, dgate_ref, dup_ref):
        _gate_grads(dh_ref, gate_ref, up_ref, wd_ref, dgate_ref, dup_ref, dgate_t_ref, dup_t_ref, dhh_ref, tf)
        _input_grad(h_ref, dh_ref, dgate_ref, dup_ref, g_ref, wg_ref, wu_ref, o_ref, dg_ref)

    row = lambda w: pl.BlockSpec((tm, w), lambda i: (i, 0))
    col = pl.BlockSpec((f, tm), lambda i: (0, i))
    return pl.pallas_call(
        body, name=name, grid=(s // tm,),
        in_specs=[row(d), row(d), row(f), row(f), _whole((1, d)), _whole((f, d)), _whole((f, d)), _whole((f, d))],
        out_specs=[row(d), pl.BlockSpec((1, d), lambda i: (0, 0)), col, col, row(d)],
        out_shape=[jax.ShapeDtypeStruct((s, d), F32), jax.ShapeDtypeStruct((1, d), F32),
                   jax.ShapeDtypeStruct((f, s), BF16), jax.ShapeDtypeStruct((f, s), BF16),
                   jax.ShapeDtypeStruct((s, d), BF16)],
        scratch_shapes=[pltpu.VMEM((tm, f), BF16), pltpu.VMEM((tm, f), BF16)],
        compiler_params=_cparams(("arbitrary",)),
    )(h_in, dh_out, gate, up, g, wg_t, wu_t, wd)


def _ffn_bwd_gates(dh_out, gate, up, wd, name, tm, carry=None):
    s, d = dh_out.shape
    f = gate.shape[1]
    tf = _ffn_chunk(f)

    def body(dh_ref, gate_ref, up_ref, wd_ref, dgate_ref, dup_ref, dgate_t_ref, dup_t_ref, dhh_ref):
        _gate_grads(dh_ref, gate_ref, up_ref, wd_ref, dgate_ref, dup_ref, dgate_t_ref, dup_t_ref, dhh_ref, tf)

    row = lambda w: pl.BlockSpec((tm, w), lambda i: (i, 0))
    col = pl.BlockSpec((f, tm), lambda i: (0, i))
    return _call(
        body, name, (s // tm,), [row(d), row(f), row(f), _whole((f, d))], [row(f), row(f), col, col, row(d)],
        [jax.ShapeDtypeStruct((s, f), BF16)] * 2 + [jax.ShapeDtypeStruct((f, s), BF16)] * 2
        + [jax.ShapeDtypeStruct((s, d), BF16)],
        (dh_out, gate, up, wd), ("arbitrary",), carry)


def _ffn_bwd_input(h_in, dh_out, dgate, dup, g, wg_t, wu_t, name, tm, carry=None):
    s, d = h_in.shape
    f = dgate.shape[1]

    row = lambda w: pl.BlockSpec((tm, w), lambda i: (i, 0))
    return _call(
        _input_grad, name, (s // tm,),
        [row(d), row(d), row(f), row(f), _whole((1, d)), _whole((f, d)), _whole((f, d))],
        [row(d), pl.BlockSpec((1, d), lambda i: (0, 0))],
        [jax.ShapeDtypeStruct((s, d), F32), jax.ShapeDtypeStruct((1, d), F32)],
        (h_in, dh_out, dgate, dup, g, wg_t, wu_t), ("arbitrary",), carry)


def _wgrad_rs1(pos, a_t, b, name, carry=None):
    f, s = a_t.shape
    d = b.shape[1]
    fk = f // N_DEV
    nc_in = 0 if carry is None else len(carry.inputs)
    nc_out = 0 if carry is None else len(carry.out_shapes)

    def body(pos_ref, a_ref, b_ref, *refs):
        c_in = refs[:nc_in]
        p_ref, pb_ref = refs[nc_in:nc_in + 2]
        c_out = refs[nc_in + 2:nc_in + 2 + nc_out]
        stage, land, send_sems, recv_sems = refs[nc_in + 2 + nc_out:nc_in + 6 + nc_out]
        c_sems = refs[nc_in + 6 + nc_out:]
        t = pl.program_id(0)
        q = t % 4
        x, y, c = lax.axis_index("x"), lax.axis_index("y"), lax.axis_index("c")

        def push(k):
            return pltpu.make_async_remote_copy(src_ref=stage.at[k], dst_ref=land.at[k], send_sem=send_sems.at[k],
                                                recv_sem=recv_sems.at[k], device_id=(x, y, 1 - c), device_id_type=MESH)

        if carry is not None:
            @pl.when(t == 0)
            def _():
                carry.start(c_in, c_out, c_sems)

        g = _dot_nn(a_ref[...], b_ref[...])

        @pl.when(t < 4)
        def _():
            stage[q] = g.astype(BF16)
            push(q).start()

        @pl.when(t >= 4)
        def _():
            push(q).wait_recv()
            p = g + land[q].astype(F32)
            p_ref[...] = p
            pb_ref[...] = p.astype(BF16)

        @pl.when(t == 7)
        def _():
            for k in range(4):
                push(k).wait_send()
            if carry is not None:
                carry.middle(c_in, c_out, c_sems)
                carry.finish(c_in, c_out, c_sems)

    def shard(t, pos_ref):
        return 4 * ((t % 4) // 2) + 2 * (t % 2) + jnp.where(t < 4, 1 - pos_ref[2], pos_ref[2])

    out = pl.BlockSpec((None, fk, d), lambda t, pos_ref: (jnp.maximum(t - 4, 0), 0, 0))
    return pl.pallas_call(
        body, name=name,
        grid_spec=pltpu.PrefetchScalarGridSpec(
            num_scalar_prefetch=1, grid=(8,),
            in_specs=[pl.BlockSpec((fk, s), lambda t, pos_ref: (shard(t, pos_ref), 0)),
                      pl.BlockSpec((s, d), lambda t, pos_ref: (0, 0), pipeline_mode=pl.Buffered(1))]
            + [_ANY] * nc_in,
            out_specs=[out, out] + [_ANY] * nc_out,
            scratch_shapes=[pltpu.VMEM((4, fk, d), BF16), pltpu.VMEM((4, fk, d), BF16),
                            pltpu.SemaphoreType.DMA((4,)), pltpu.SemaphoreType.DMA((4,))]
            + ([] if carry is None else list(carry.sems))),
        out_shape=[jax.ShapeDtypeStruct((4, fk, d), F32), jax.ShapeDtypeStruct((4, fk, d), BF16)]
        + ([] if carry is None else list(carry.out_shapes)),
        compiler_params=_cparams(("arbitrary",)),
    )(pos, a_t, b, *([] if carry is None else carry.inputs))


def _rope(t, c, sa, sb, reps):
    c, sa, sb = (jnp.tile(v, (1, reps)) if reps > 1 else v for v in (c, sa, sb))
    w = t.shape[1]
    return t * c + pltpu.roll(t, w - 8, 1) * sa + pltpu.roll(t, 8, 1) * sb


def _rope_bwd(dt, c, sa, sb, reps):
    c, sa, sb = (jnp.tile(v, (1, reps)) if reps > 1 else v for v in (c, sa, sb))
    w = dt.shape[1]
    return dt * c + pltpu.roll(dt * sa, 8, 1) + pltpu.roll(dt * sb, w - 8, 1)


def _mix_in(h, g, win_t, tabs, name, tm):
    s, d = h.shape
    n_in = win_t.shape[0]

    def body(h_ref, g_ref, w_ref, c_ref, sa_ref, sb_ref, q_ref, k_ref, v_ref, pc_ref, n_ref):
        x = h_ref[...]
        r = lax.rsqrt(jnp.mean(x * x, axis=-1, keepdims=True) + EPS)
        nb = (x * r * g_ref[...]).astype(BF16)
        n_ref[...] = nb
        u = _dot_nt(nb, w_ref[...])
        c, sa, sb = c_ref[...], sa_ref[...], sb_ref[...]
        q_ref[...] = _rope(u[:, :ATTN_W], c, sa, sb, ATTN_W // 128).astype(BF16)
        k_ref[...] = _rope(u[:, ATTN_W:ATTN_W + KV_W], c, sa, sb, 1).astype(BF16)
        v_ref[...] = u[:, ATTN_W + KV_W:ATTN_W + 2 * KV_W].astype(BF16)
        pc_ref[...] = u[:, ATTN_W + 2 * KV_W:]

    row = lambda w: pl.BlockSpec((tm, w), lambda i: (i, 0))
    return pl.pallas_call(
        body, name=name, grid=(s // tm,),
        in_specs=[row(d), _whole((1, d)), _whole((n_in, d)), row(128), row(128), row(128)],
        out_specs=[row(ATTN_W), row(KV_W), row(KV_W), row(POOL_W), row(d)],
        out_shape=[jax.ShapeDtypeStruct((s, ATTN_W), BF16), jax.ShapeDtypeStruct((s, KV_W), BF16),
                   jax.ShapeDtypeStruct((s, KV_W), BF16), jax.ShapeDtypeStruct((s, POOL_W), F32),
                   jax.ShapeDtypeStruct((s, d), BF16)],
        compiler_params=_cparams(("arbitrary",)),
    )(h, g, win_t, *tabs)


def _band_mask(n, nb, transposed):
    shape = (3 * BLK, 2 * BLK) if transposed else (2 * BLK, 3 * BLK)
    i = lax.broadcasted_iota(jnp.int32, shape, 1 if transposed else 0) % BLK
    j = lax.broadcasted_iota(jnp.int32, shape, 0 if transposed else 1)
    kpos = (n - 1) * BLK + j
    return (j >= i) & (j <= i + 2 * BLK) & (kpos >= 0) & (kpos < nb * BLK)


def _block_diag(t, kh):
    tf = t.astype(F32)
    tr = pltpu.roll(tf, HEAD_DIM, 1)
    lo = lax.broadcasted_iota(jnp.int32, tf.shape, 1) < HEAD_DIM
    top, bot = (tf, tr) if kh == 0 else (tr, tf)
    return jnp.concatenate([jnp.where(lo, top, 0.0), jnp.where(lo, 0.0, bot)], axis=0).astype(BF16)


def _fold_diag(tbd):
    lo = lax.broadcasted_iota(jnp.int32, (3 * BLK, 2 * HEAD_DIM), 1) < HEAD_DIM
    t = jnp.where(lo, tbd[:3 * BLK], tbd[3 * BLK:])
    return t + pltpu.roll(t, HEAD_DIM, 1)


def _stack_pairs(x, kh):
    return jnp.concatenate([x[:, (2 * kh) * 128:(2 * kh + 1) * 128], x[:, (2 * kh + 1) * 128:(2 * kh + 2) * 128]], axis=0)


def _sink_of(sink_ref, kh, half, axis):
    shape = (2 * BLK, 1) if axis == 0 else (1, 2 * BLK)
    first = lax.broadcasted_iota(jnp.int32, shape, axis) < BLK
    return jnp.where(first, sink_ref[0, GROUP * kh + half], sink_ref[0, GROUP * kh + 2 + half])


def _softmax_sink(sc, valid, sink, axis):
    sc = jnp.where(valid, sc, -1e30)
    m = jnp.maximum(jnp.max(sc, axis=axis, keepdims=True), sink)
    e = jnp.exp(sc - m)
    es = jnp.exp(sink - m)
    inv = 1.0 / (jnp.sum(e, axis=axis, keepdims=True) + es)
    return e * inv, es * inv


def _attn_blocks_per_step(nb):
    return next(nq for nq in (4, 2, 1) if nb % nq == 0)


def _band_specs(nq, nb, w, col=0):
    return [pl.BlockSpec((BLK, w), lambda m: (jnp.maximum(nq * m - 1, 0), col)),
            pl.BlockSpec((nq * BLK, w), lambda m: (m, col)),
            pl.BlockSpec((BLK, w), lambda m: (jnp.minimum(nq * m + nq, nb - 1), col))]


def _attn_pool_fwd(q, k, v, pc, sink, pool_w, pool_scale, pband, name, carry=None, middle=CARRY_MIDDLE):
    s = q.shape[0]
    nb = s // BLK

    nq = _attn_blocks_per_step(nb)

    def body(sink_ref, q_ref, k0, k1, k2, v0, v1, v2, p0, p1, p2, pw_ref, ps_ref, pb_ref, o_ref, o_t_ref):
        kall = jnp.concatenate([k0[...], k1[...], k2[...]], axis=0)
        vall = jnp.concatenate([v0[...], v1[...], v2[...]], axis=0)
        pall = jnp.concatenate([p0[...], p1[...], p2[...]], axis=0).astype(BF16)
        qall = q_ref[...] * SCORE_SCALE
        for j in range(nq):
            n = pl.program_id(0) * nq + j
            rows, band = slice(j * BLK, (j + 1) * BLK), slice(j * BLK, (j + 3) * BLK)
            valid = _band_mask(n, nb, False)
            kb, vb, qs = kall[band], vall[band], qall[rows]
            for kh in range(N_KV):
                sc = _dot_nt(_stack_pairs(qs, kh), _block_diag(kb, kh))
                p = [_softmax_sink(sc[:, half * 3 * BLK:(half + 1) * 3 * BLK], valid,
                                   _sink_of(sink_ref, kh, half, 0), 1)[0] for half in range(2)]
                o2 = _dot_nn(jnp.concatenate(p, axis=1).astype(BF16), _block_diag(vb, kh)).astype(BF16)
                o_ref[rows, (2 * kh) * 128:(2 * kh + 1) * 128] = o2[:BLK]
                o_ref[rows, (2 * kh + 1) * 128:(2 * kh + 2) * 128] = o2[BLK:]
            ext = pall[band]
            var = _variant_index(n, nb)
            for gi in range(POOL_G):
                gsl = slice(gi * POOL_GW, (gi + 1) * POOL_GW)
                dg = _dot_nn(pb_ref[var, gi], ext[:, gsl])
                yg = _dot_nn(dg.astype(BF16), pw_ref[gi].astype(BF16))
                o_ref[rows, ATTN_W + gi * POOL_GW:ATTN_W + (gi + 1) * POOL_GW] = (yg * ps_ref[:, gsl]).astype(BF16)
        o_t_ref[...] = o_ref[...].astype(F32).T.astype(BF16)

    return _call(
        body, name, (nb // nq,),
        [pl.BlockSpec(memory_space=pltpu.SMEM), pl.BlockSpec((nq * BLK, ATTN_W), lambda m: (m, 0)),
         *_band_specs(nq, nb, KV_W), *_band_specs(nq, nb, KV_W), *_band_specs(nq, nb, POOL_W),
         _whole((POOL_G, POOL_GW, POOL_GW)), _whole((1, POOL_W)), _whole(pband.shape)],
        [pl.BlockSpec((nq * BLK, ATTN_W + POOL_W), lambda m: (m, 0)),
         pl.BlockSpec((ATTN_W + POOL_W, nq * BLK), lambda m: (0, m))],
        [jax.ShapeDtypeStruct((s, ATTN_W + POOL_W), BF16), jax.ShapeDtypeStruct((ATTN_W + POOL_W, s), BF16)],
        (sink, q, k, k, k, v, v, v, pc, pc, pc, pool_w, pool_scale, pband), ("arbitrary",), carry, middle)


def _attn_pool_bwd(q, k, v, pc, dmix, sink, pool_w, pool_scale, pband, ptband, name, carry=None):
    s = q.shape[0]
    nb = s // BLK
    nq = _attn_blocks_per_step(nb)

    def body(sink_ref, q_ref, k0, k1, k2, v0, v1, v2, p0, p1, p2, da_ref, d0, d1, d2, pw_ref, ps_ref, pb_ref, ptb_ref,
             dq_ref, dk_ref, dv_ref, dpc_ref, dsink_ref, dpw_ref, dps_ref):
        @pl.when(pl.program_id(0) == 0)
        def _():
            dsink_ref[...] = jnp.zeros_like(dsink_ref)
            dpw_ref[...] = jnp.zeros_like(dpw_ref)
            dps_ref[...] = jnp.zeros_like(dps_ref)

        kall = jnp.concatenate([k0[...], k1[...], k2[...]], axis=0)
        vall = jnp.concatenate([v0[...], v1[...], v2[...]], axis=0)
        pall = jnp.concatenate([p0[...], p1[...], p2[...]], axis=0).astype(BF16)
        dpall = jnp.concatenate([d0[...], d1[...], d2[...]], axis=0)
        lo = lax.broadcasted_iota(jnp.int32, (3 * BLK, KV_W), 1) < HEAD_DIM
        for j in range(nq):
            n = pl.program_id(0) * nq + j
            rows, band = slice(j * BLK, (j + 1) * BLK), slice(j * BLK, (j + 3) * BLK)
            valid = _band_mask(n, nb, True)
            kb, vb, qb = kall[band], vall[band], q_ref[rows, :]
            qs = qb * SCORE_SCALE
            da = da_ref[rows, :].astype(BF16)
            dk_fold, dv_fold = [], []
            for kh in range(N_KV):
                kbd, vbd = _block_diag(kb, kh), _block_diag(vb, kh)
                q2, do2 = _stack_pairs(qb, kh), _stack_pairs(da, kh)
                sc_t = _dot_nt(kbd, _stack_pairs(qs, kh))
                dp_t = _dot_nt(vbd, do2)
                p_t, ds_t = [], []
                for half in range(2):
                    keys = slice(half * 3 * BLK, (half + 1) * 3 * BLK)
                    p, ps = _softmax_sink(sc_t[keys], valid, _sink_of(sink_ref, kh, half, 1), 0)
                    delta = jnp.sum(p * dp_t[keys], axis=0, keepdims=True)
                    p_t.append(p.astype(BF16))
                    ds_t.append((p * (dp_t[keys] - delta)).astype(BF16))
                    dsk = -ps * delta
                    for pair in range(2):
                        h = GROUP * kh + 2 * pair + half
                        part = jnp.sum(dsk[:, pair * BLK:(pair + 1) * BLK], axis=1, keepdims=True)
                        dsink_ref[h:h + 1, :] += jnp.broadcast_to(part, (1, 128))
                p_t = jnp.concatenate(p_t, axis=0)
                ds_t = jnp.concatenate(ds_t, axis=0)
                dq2 = _dot_tn(ds_t, kbd) * SCORE_SCALE
                dq_ref[rows, (2 * kh) * 128:(2 * kh + 1) * 128] = dq2[:BLK]
                dq_ref[rows, (2 * kh + 1) * 128:(2 * kh + 2) * 128] = dq2[BLK:]
                dk_fold.append(_fold_diag(_dot_nn(ds_t, q2)) * SCORE_SCALE)
                dv_fold.append(_fold_diag(_dot_nn(p_t, do2)))
            dk_all = jnp.where(lo, dk_fold[0], dk_fold[1])
            dv_all = jnp.where(lo, dv_fold[0], dv_fold[1])
            for t in range(3):
                dk_ref[j, t] = dk_all[t * BLK:(t + 1) * BLK]
                dv_ref[j, t] = dv_all[t * BLK:(t + 1) * BLK]
            ext, dpe = pall[band], dpall[band]
            dpc_cur = dpall[(j + 1) * BLK:(j + 2) * BLK]
            var = _variant_index(n, nb)
            for gi in range(POOL_G):
                gsl = slice(gi * POOL_GW, (gi + 1) * POOL_GW)
                wg = pw_ref[gi].astype(BF16)
                sc = ps_ref[:, gsl]
                dgb = _dot_nn(pb_ref[var, gi], ext[:, gsl]).astype(BF16)
                yg = _dot_nn(dgb, wg)
                dps_ref[:, gsl] += jnp.sum(dpc_cur[:, gsl] * yg, axis=0, keepdims=True)
                dpw_ref[gi] += _dot_tn(dgb, (dpc_cur[:, gsl] * sc).astype(BF16))
                dd = _dot_nt((dpe[:, gsl] * sc).astype(BF16), wg)
                dpc_ref[rows, gsl] = _dot_nn(ptb_ref[var, gi], dd.astype(BF16))

    fixed = lambda shape: pl.BlockSpec(shape, lambda m: (0,) * len(shape))
    return _call(
        body, name, (nb // nq,),
        [pl.BlockSpec(memory_space=pltpu.SMEM), pl.BlockSpec((nq * BLK, ATTN_W), lambda m: (m, 0)),
         *_band_specs(nq, nb, KV_W), *_band_specs(nq, nb, KV_W), *_band_specs(nq, nb, POOL_W),
         pl.BlockSpec((nq * BLK, ATTN_W), lambda m: (m, 0)), *_band_specs(nq, nb, POOL_W, 1),
         _whole((POOL_G, POOL_GW, POOL_GW)), _whole((1, POOL_W)), _whole(pband.shape), _whole(ptband.shape)],
        [pl.BlockSpec((nq * BLK, ATTN_W), lambda m: (m, 0)),
         pl.BlockSpec((nq, 3, BLK, KV_W), lambda m: (m, 0, 0, 0)),
         pl.BlockSpec((nq, 3, BLK, KV_W), lambda m: (m, 0, 0, 0)),
         pl.BlockSpec((nq * BLK, POOL_W), lambda m: (m, 0)),
         fixed((N_HEADS, 128)), fixed((POOL_G, POOL_GW, POOL_GW)), fixed((1, POOL_W))],
        [jax.ShapeDtypeStruct((s, ATTN_W), F32), jax.ShapeDtypeStruct((nb, 3, BLK, KV_W), F32),
         jax.ShapeDtypeStruct((nb, 3, BLK, KV_W), F32), jax.ShapeDtypeStruct((s, POOL_W), F32),
         jax.ShapeDtypeStruct((N_HEADS, 128), F32),
         jax.ShapeDtypeStruct((POOL_G, POOL_GW, POOL_GW), F32), jax.ShapeDtypeStruct((1, POOL_W), F32)],
        (sink, q, k, k, k, v, v, v, pc, pc, pc, dmix, dmix, dmix, dmix, pool_w, pool_scale, pband, ptband),
        ("arbitrary",), carry)


def _mix_out(h, mix, w_out, name, tm, scale=1.0, carry=None, middle=CARRY_MIDDLE):
    s, d = h.shape
    w = mix.shape[1]

    def body(h_ref, m_ref, w_ref, o_ref):
        o_ref[...] = h_ref[...] + scale * _dot_nn(m_ref[...], w_ref[...])

    row = lambda c: pl.BlockSpec((tm, c), lambda i: (i, 0))
    return _call(body, name, (s // tm,), [row(d), row(w), _whole((w, d))], [row(d)],
                 [jax.ShapeDtypeStruct((s, d), F32)], (h, mix, w_out), ("arbitrary",), carry, middle)


def _mix_out_bwd(dh, w_out, name, tm):
    s, d = dh.shape
    w = w_out.shape[0]

    def body(dh_ref, w_ref, o_ref, dhb_ref):
        dhb = dh_ref[...].astype(BF16)
        dhb_ref[...] = dhb
        o_ref[...] = _dot_nt(dhb, w_ref[...])

    row = lambda c: pl.BlockSpec((tm, c), lambda i: (i, 0))
    return pl.pallas_call(
        body, name=name, grid=(s // tm,), in_specs=[row(d), _whole((w, d))], out_specs=[row(w), row(d)],
        out_shape=[jax.ShapeDtypeStruct((s, w), F32), jax.ShapeDtypeStruct((s, d), BF16)],
        compiler_params=_cparams(("arbitrary",)),
    )(dh, w_out)


def _mix_in_bwd(h, dh, g, win_t, dq, dkp, dvp, dpc, tabs, name, tm, carry=None):
    s, d = h.shape
    nb = s // BLK
    nt = tm // BLK
    n_in = win_t.shape[0]

    def band_sum(n, before, own, after, prev_last, next_first):
        lo = (n > 0).astype(F32)
        hi = (n < s // tm - 1).astype(F32)
        blocks = []
        for b in range(nt):
            from_prev = prev_last[...] * lo if b == 0 else before[b - 1]
            from_next = next_first[...] * hi if b == nt - 1 else after[b + 1]
            blocks.append(from_prev + own[b] + from_next)
        return jnp.concatenate(blocks, axis=0)

    def body(h_ref, dh_ref, g_ref, w_ref, dq_ref, k2, k1, k0, kp, kn, v2, v1, v0, vp, vn, dpc_ref, c_ref, sa_ref,
             sb_ref, o_ref, du_ref, dg_ref):
        n = pl.program_id(0)
        dk = band_sum(n, k2, k1, k0, kp, kn)
        dv = band_sum(n, v2, v1, v0, vp, vn)
        c, sa, sb = c_ref[...], sa_ref[...], sb_ref[...]
        du = jnp.concatenate([_rope_bwd(dq_ref[...], c, sa, sb, ATTN_W // 128), _rope_bwd(dk, c, sa, sb, 1), dv,
                              dpc_ref[...]], axis=1)
        du_ref[...] = du.T.astype(BF16)
        dn = _dot_nn(du.astype(BF16), w_ref[...])
        x = h_ref[...]
        r = lax.rsqrt(jnp.mean(x * x, axis=-1, keepdims=True) + EPS)
        xhat = x * r
        dxhat = dn * g_ref[...]
        o_ref[...] = dh_ref[...] + r * (dxhat - xhat * jnp.mean(dxhat * xhat, axis=-1, keepdims=True))

        @pl.when(n == 0)
        def _():
            dg_ref[...] = jnp.zeros_like(dg_ref)

        dg_ref[...] += jnp.sum(dn * xhat, axis=0, keepdims=True)

    row = lambda w: pl.BlockSpec((tm, w), lambda n: (n, 0))
    slot = lambda t: pl.BlockSpec((nt, None, BLK, KV_W), lambda n, t=t: (n, t, 0, 0))
    parts = [slot(2), slot(1), slot(0),
             pl.BlockSpec((None, None, BLK, KV_W), lambda n: (jnp.maximum(nt * n - 1, 0), 2, 0, 0)),
             pl.BlockSpec((None, None, BLK, KV_W), lambda n: (jnp.minimum(nt * n + nt, nb - 1), 0, 0, 0))]
    return _call(
        body, name, (s // tm,),
        [row(d), row(d), _whole((1, d)), _whole((n_in, d)), row(ATTN_W), *parts, *parts, row(POOL_W),
         row(128), row(128), row(128)],
        [row(d), pl.BlockSpec((n_in, tm), lambda n: (0, n)), pl.BlockSpec((1, d), lambda n: (0, 0))],
        [jax.ShapeDtypeStruct((s, d), F32), jax.ShapeDtypeStruct((n_in, s), BF16), jax.ShapeDtypeStruct((1, d), F32)],
        (h, dh, g, win_t, dq, *[dkp] * 5, *[dvp] * 5, dpc, *tabs), ("arbitrary",), carry)


def _adam_math(w, g, m, v):
    m = ADAM_B1 * m + (1.0 - ADAM_B1) * g
    v = ADAM_B2 * v + (1.0 - ADAM_B2) * (g * g)
    m_hat = m / (1.0 - ADAM_B1 ** ADAM_STEP)
    v_hat = v / (1.0 - ADAM_B2 ** ADAM_STEP)
    delta = -ADAM_LR * (m_hat / (jnp.sqrt(v_hat) + ADAM_EPS) + ADAM_WD * w)
    return delta, m, v


def _adam_small(w, parts, late, m, v, name):
    rows, cols = w.shape

    def body(w_ref, p_ref, l_ref, m_ref, v_ref, g_ref, d_ref, nm_ref, nv_ref):
        g, gl = p_ref[0], l_ref[0]
        for k in range(1, N_DEV):
            g = g + p_ref[k]
            gl = gl + l_ref[k]
        g_ref[...] = g
        g_ref[SMALL_NORM1:SMALL_NORM1 + 8, :] = g[SMALL_NORM1:SMALL_NORM1 + 8] + gl
        d_ref[...], nm_ref[...], nv_ref[...] = _adam_math(w_ref[...], g_ref[...], m_ref[...], v_ref[...])

    return pl.pallas_call(
        body, name=name, out_shape=[jax.ShapeDtypeStruct((rows, cols), F32)] * 4,
    )(w, parts, late, m, v)


SMALL_NORM1 = 512


def _pack_small(norm1, normm, norm2, normf, sink, pool_w, pool_scale, loss=None):
    scale_rows = jnp.pad(pool_scale.reshape(4, 128), ((0, 4), (0, 0)))
    last_rows = jnp.pad(sink.reshape(1, N_HEADS), ((0, 7), (0, 128 - N_HEADS)))
    if loss is not None:
        last_rows = last_rows + jnp.pad(loss.reshape(1, 1), ((1, 6), (0, 127)))
    return jnp.concatenate([pool_w.reshape(512, 128), norm1.reshape(8, 128), normm.reshape(8, 128),
                            norm2.reshape(8, 128), normf.reshape(8, 128), scale_rows, last_rows], axis=0)


def _unpack_small(p):
    return dict(pool_w=p[:512].reshape(1, POOL_G, POOL_GW, POOL_GW), ffn1_norm=p[512:520].reshape(1, 1024),
                mix_norm=p[520:528].reshape(1, 1024), ffn2_norm=p[528:536].reshape(1, 1024),
                final_norm=p[536:544].reshape(1024), pool_scale=p[544:548].reshape(1, POOL_W),
                sink_logits=p[552, :N_HEADS].reshape(1, N_HEADS), loss=p[553, 0])


def kernel(x, ffn1_norm, ffn1_w_gate, ffn1_w_up, ffn1_w_down, mix_norm, w_in, sink_logits, pool_w, pool_scale, w_out, ffn2_norm, ffn2_w_gate, ffn2_w_up, ffn2_w_down, final_norm, loss_target, m_ffn1_norm, m_ffn1_w_gate, m_ffn1_w_up, m_ffn1_w_down, m_mix_norm, m_w_in, m_sink_logits, m_pool_w, m_pool_scale, m_w_out, m_ffn2_norm, m_ffn2_w_gate, m_ffn2_w_up, m_ffn2_w_down, m_final_norm, v_ffn1_norm, v_ffn1_w_gate, v_ffn1_w_up, v_ffn1_w_down, v_mix_norm, v_w_in, v_sink_logits, v_pool_w, v_pool_scale, v_w_out, v_ffn2_norm, v_ffn2_w_gate, v_ffn2_w_up, v_ffn2_w_down, v_final_norm):
    s, d = x.shape[1], x.shape[2]
    fk = ffn1_w_gate.shape[2]
    f = N_DEV * fk
    ink = w_in.shape[2]
    n_in = N_DEV * ink
    mixk = w_out.shape[1]
    tm = min(512, s)
    tm_bwd = min(256, s)
    pos = jnp.stack([lax.axis_index("x"), lax.axis_index("y"), lax.axis_index("c")]).astype(jnp.int32)

    t_bf = lambda w: w[0].T.astype(BF16)
    full = lambda a: a.reshape(N_DEV * a.shape[1], d)
    wg1, wu1 = map(full, _run_exchange(_AllGather([t_bf(ffn1_w_gate), t_bf(ffn1_w_up)]), "gather_ffn1"))
    second = _AllGather([ffn1_w_down[0].astype(BF16), t_bf(w_in), w_out[0].astype(BF16)])
    third = _AllGather([t_bf(ffn2_w_gate)])
    fourth = _AllGather([t_bf(ffn2_w_up), ffn2_w_down[0].astype(BF16)])

    tabs = _rope_tables(s)
    pband, ptband = _pool_tables(s)
    g1, gm, g2, gf = ffn1_norm, mix_norm, ffn2_norm, final_norm.reshape(1, d)

    x0 = x[0]
    n1, gate1, up1, act1_t, act1, *gathered = _ffn_fwd(x0, g1, wg1, wu1, None, "ffn1_gate_up", tm, carry=second,
                                                       middle=None)
    wd1, win_t, wout = map(full, gathered)
    h1, wg2 = _mix_out(x0, act1, wd1, "ffn1_down", tm, scale=0.5, carry=third, middle=None)
    wg2 = full(wg2)
    q, k, v, pc, n2 = _mix_in(h1, gm, win_t, tabs, "mix_in", tm)
    mix, mix_t, *gathered = _attn_pool_fwd(q, k, v, pc, sink_logits, pool_w[0], pool_scale, pband, "attn_pool_fwd",
                                           carry=fourth, middle=None)
    wu2, wd2 = map(full, gathered)
    (h2,) = _mix_out(h1, mix, wout, "mix_out", tm)
    dh3, n3, gate2, up2, act2_t, loss_part, dgf = _ffn_fwd(h2, g2, wg2, wu2, wd2, "ffn2_fwd", tm,
                                                           head=(gf, loss_target[0]))

    sum1, recv2 = {}, {}

    def stage2(keys):
        return _RsStage2([sum1[key][1] for key in keys])

    dh2, dg2, dgate2_t, dup2_t, dhh3 = _ffn_bwd(h2, dh3, gate2, up2, g2, wg2, wu2, wd2, "ffn2_bwd", tm_bwd)
    sum1["g2"] = _wgrad_rs1(pos, dgate2_t, n3, "wgrad_gate2")
    sum1["u2"] = _wgrad_rs1(pos, dup2_t, n3, "wgrad_up2")
    sum1["d2"] = _wgrad_rs1(pos, act2_t, dhh3, "wgrad_down2")
    dmix, dh2b = _mix_out_bwd(dh2, wout, "mix_out_bwd", tm)
    sum1["out"] = _wgrad_rs1(pos, mix_t, dh2b, "wgrad_out")
    dq, dkp, dvp, dpc, dsink, dpw, dps, *r2 = _attn_pool_bwd(
        q, k, v, pc, dmix, sink_logits, pool_w[0], pool_scale, pband, ptband, "attn_pool_bwd",
        carry=stage2(["g2", "u2", "d2", "out"]))
    recv2.update(zip(["g2", "u2", "d2", "out"], r2))
    dh1, du_t, dgm = _mix_in_bwd(h1, dh2, gm, win_t, dq, dkp, dvp, dpc, tabs, "mix_in_bwd", tm)
    sum1["in"] = _wgrad_rs1(pos, du_t, n2, "wgrad_in")
    small_part = _pack_small(jnp.zeros_like(dgm), dgm, dg2, dgf, dsink[:, 0], dpw, dps, loss_part)
    dgate1, dup1, dgate1_t, dup1_t, dhh1, recv2["in"], small_all = _ffn_bwd_gates(
        dh1, gate1, up1, wd1, "ffn1_bwd_gates", tm, carry=_Both(stage2(["in"]), _AllGather([small_part])))
    sum1["g1"] = _wgrad_rs1(pos, dgate1_t, n1, "wgrad_gate1")
    *sum1["u1"], recv2["g1"] = _wgrad_rs1(pos, dup1_t, n1, "wgrad_up1", carry=stage2(["g1"]))
    *sum1["d1"], recv2["u1"] = _wgrad_rs1(pos, act1_t, dhh1, "wgrad_down1", carry=stage2(["u1"]))
    dx, dg1, recv2["d1"] = _ffn_bwd_input(x0, dh1, dgate1, dup1, g1, wg1, wu1, "ffn1_bwd_input", tm,
                                          carry=stage2(["d1"]))

    (dg1_all,) = _run_exchange(_DirectGather([dg1.reshape(8, 128)]), "gather_norm1_grad")
    pk = lambda a, b, c_, e, s_, pw_, psc: _pack_small(a, b, c_, e, s_[0], pw_[0], psc)
    small_w = pk(ffn1_norm, mix_norm, ffn2_norm, final_norm, sink_logits, pool_w, pool_scale)
    small_m = pk(m_ffn1_norm, m_mix_norm, m_ffn2_norm, m_final_norm, m_sink_logits, m_pool_w, m_pool_scale)
    small_v = pk(v_ffn1_norm, v_mix_norm, v_ffn2_norm, v_final_norm, v_sink_logits, v_pool_w, v_pool_scale)
    sg, sd, sm, sv = [_unpack_small(a)
                      for a in _adam_small(small_w, small_all, dg1_all, small_m, small_v, "adam_small")]

    big = {}
    keys = ["g1", "u1", "d1", "g2", "u2", "d2", "in", "out"]
    names = ["ffn1_w_gate", "ffn1_w_up", "ffn1_w_down", "ffn2_w_gate", "ffn2_w_up", "ffn2_w_down", "w_in", "w_out"]
    transposed = [True, True, False, True, True, False, True, False]
    ws = [ffn1_w_gate, ffn1_w_up, ffn1_w_down, ffn2_w_gate, ffn2_w_up, ffn2_w_down, w_in, w_out]
    ms = [m_ffn1_w_gate, m_ffn1_w_up, m_ffn1_w_down, m_ffn2_w_gate, m_ffn2_w_up, m_ffn2_w_down, m_w_in, m_w_out]
    vs = [v_ffn1_w_gate, v_ffn1_w_up, v_ffn1_w_down, v_ffn2_w_gate, v_ffn2_w_up, v_ffn2_w_down, v_w_in, v_w_out]
    for key, nm, tr, w, m, vv in zip(keys, names, transposed, ws, ms, vs):
        view = (lambda a: jnp.swapaxes(a, 1, 2)[0]) if tr else (lambda a: a[0])
        back = (lambda a: jnp.swapaxes(a[None], 1, 2)) if tr else (lambda a: a[None])
        res = _rs_sum2_adam(pos, sum1[key][0], recv2[key], view(w), view(m), view(vv), "adam_" + nm)
        big[nm] = tuple(back(a) for a in res)

    loss = sg["loss"]
    all_names = ["ffn1_norm", "ffn1_w_gate", "ffn1_w_up", "ffn1_w_down", "mix_norm", "w_in", "sink_logits", "pool_w",
                 "pool_scale", "w_out", "ffn2_norm", "ffn2_w_gate", "ffn2_w_up", "ffn2_w_down", "final_norm"]
    outs = [loss, dx[None]]
    for idx, src in enumerate((sg, sd, sm, sv)):
        for nm in all_names:
            outs.append(big[nm][idx] if nm in big else src[nm])
    return tuple(outs)
```

```python
import functools

import jax
import jax.numpy as jnp
import numpy as np
from jax import lax
from jax.experimental import pallas as pl
from jax.experimental.pallas import tpu as pltpu

F32 = jnp.float32
BF16 = jnp.bfloat16
MESH = pl.DeviceIdType.MESH
N_DEV = 8

EPS = 1e-6
HEAD_DIM = 64
N_HEADS = 8
N_KV = 2
GROUP = N_HEADS // N_KV
ATTN_W = N_HEADS * HEAD_DIM
KV_W = N_KV * HEAD_DIM
POOL_W = 512
POOL_G = 4
POOL_GW = POOL_W // POOL_G
POOL_WINDOWS = (2, 4, 8, 16)
BLK = 128
ROT = 16
ROPE_THETA = 500000.0
SCORE_SCALE = HEAD_DIM ** -0.5

ADAM_LR, ADAM_B1, ADAM_B2, ADAM_EPS, ADAM_WD, ADAM_STEP = 0.001, 0.9, 0.999, 1e-08, 0.01, 10

VMEM_LIMIT = 56 * 1024 * 1024


def _cparams(sem=None, **kw):
    if sem is not None:
        kw["dimension_semantics"] = sem
    return pltpu.CompilerParams(vmem_limit_bytes=VMEM_LIMIT, **kw)


def _whole(shape):
    nd = len(shape)
    return pl.BlockSpec(shape, lambda *_: (0,) * nd, pipeline_mode=pl.Buffered(1))


def _sigmoid(z):
    return 1.0 / (1.0 + jnp.exp(-z))


def _dot_nt(a, b):
    return lax.dot_general(a, b, (((1,), (1,)), ((), ())), preferred_element_type=F32)


def _dot_nn(a, b):
    return lax.dot_general(a, b, (((1,), (0,)), ((), ())), preferred_element_type=F32)


def _dot_tn(a, b):
    return lax.dot_general(a, b, (((0,), (0,)), ((), ())), preferred_element_type=F32)


def _rope_tables(s):
    inv_freq = ROPE_THETA ** (-np.arange(0, ROT, 2, dtype=np.float64) / ROT)
    ang = np.arange(s, dtype=np.float64)[:, None] * inv_freq[None, :]
    c = np.ones((s, HEAD_DIM)); sa = np.zeros((s, HEAD_DIM)); sb = np.zeros((s, HEAD_DIM))
    c[:, :8] = np.cos(ang); c[:, 8:16] = np.cos(ang)
    sa[:, :8] = -np.sin(ang)
    sb[:, 8:16] = np.sin(ang)
    t = lambda a: jnp.asarray(np.tile(a, (1, 2)).astype(np.float32))
    return t(c), t(sa), t(sb)


def _pool_weight(gi, t, s_pos, s):
    half = POOL_WINDOWS[gi] // 2

    def win(lo, hi):
        a = np.clip(lo, 0, s); b = np.clip(hi + 1, 0, s)
        inside = (s_pos >= a) & (s_pos < b)
        return inside / np.maximum(b - a, 1)

    w = 0.5 * (win(t - half, t + half - 1) + win(t - half + 1, t + half)) - (t == s_pos)
    return w * ((t >= 0) & (t < s) & (s_pos >= 0) & (s_pos < s))


def _pool_tables(s):
    nb = s // BLK
    fwd = np.zeros((3, POOL_G, BLK, 3 * BLK), np.float32)
    bwd = np.zeros((3, POOL_G, BLK, 3 * BLK), np.float32)
    for vi, n in enumerate((0, 1 if nb > 2 else 0, nb - 1)):
        i = n * BLK + np.arange(BLK)[:, None]
        j = (n - 1) * BLK + np.arange(3 * BLK)[None, :]
        for gi in range(POOL_G):
            fwd[vi, gi] = _pool_weight(gi, i, j, s)
            bwd[vi, gi] = _pool_weight(gi, j, i, s)
    return jnp.asarray(fwd, dtype=BF16), jnp.asarray(bwd, dtype=BF16)


def _variant_index(n, nb):
    return jnp.where(n == 0, 0, jnp.where(n == nb - 1, 2, 1))


class _Exchange:
    inputs = ()
    out_shapes = ()
    sems = ()

    def start(self, srcs, outs, sems):
        raise NotImplementedError

    def middle(self, srcs, outs, sems):
        pass

    def finish(self, srcs, outs, sems):
        raise NotImplementedError


class _AllGather(_Exchange):
    def __init__(self, arrays):
        n = len(arrays)
        self.inputs = list(arrays)
        self.out_shapes = [jax.ShapeDtypeStruct((N_DEV,) + a.shape, a.dtype) for a in arrays]
        self.sems = [pltpu.SemaphoreType.DMA((n, 8)), pltpu.SemaphoreType.DMA((n, 8)), pltpu.SemaphoreType.DMA((n,))]

    def _parts(self, srcs, outs, sems):
        send_sems, recv_sems, local_sems = sems
        n = len(srcs)
        x, y, c = lax.axis_index("x"), lax.axis_index("y"), lax.axis_index("c")
        me, sibling, xn, yn, diag = (x, y, c), (x, y, 1 - c), (1 - x, y, c), (x, 1 - y, c), (1 - x, 1 - y, c)

        def place(a, dev, half=None):
            block = outs[a].at[4 * dev[0] + 2 * dev[1] + dev[2]]
            if half is None:
                return block
            r2 = outs[a].shape[1] // 2
            return block.at[pl.ds(half * r2, r2)]

        def copy(a, k, dev, to, half=None, src=None):
            where = place(a, dev, half)
            return pltpu.make_async_remote_copy(
                src_ref=where if src is None else src, dst_ref=where, send_sem=send_sems.at[a, k],
                recv_sem=recv_sems.at[a, k], device_id=to, device_id_type=MESH)

        def other(dev):
            return (dev[0], dev[1], 1 - dev[2])

        class Parts:
            mine = staticmethod(lambda: [pltpu.make_async_copy(srcs[a], place(a, me), local_sems.at[a])
                                         for a in range(n)])
            own = staticmethod(lambda: [copy(a, k, me, to, src=srcs[a]) for a in range(n)
                                        for k, to in ((0, sibling), (1, xn), (2, yn))])
            relay = staticmethod(lambda a: [copy(a, 3, xn, yn, half=0), copy(a, 4, yn, xn, half=1),
                                            copy(a, 5, xn, sibling), copy(a, 6, yn, sibling)])
            last = staticmethod(lambda a: copy(a, 7, diag, sibling))
            from_x = staticmethod(lambda a: copy(a, 1, xn, me))
            from_y = staticmethod(lambda a: copy(a, 2, yn, me))
            diag_halves = staticmethod(lambda a: [copy(a, 3, diag, me, half=0), copy(a, 4, diag, me, half=1)])
            from_sibling = staticmethod(lambda a: [copy(a, 0, sibling, me), copy(a, 5, other(xn), me),
                                                   copy(a, 6, other(yn), me), copy(a, 7, other(diag), me)])

        return n, Parts

    def start(self, srcs, outs, sems):
        _, p = self._parts(srcs, outs, sems)
        for cp in p.mine() + p.own():
            cp.start()

    def middle(self, srcs, outs, sems):
        n, p = self._parts(srcs, outs, sems)
        for a in range(n):
            p.from_x(a).wait_recv()
            p.from_y(a).wait_recv()
            for cp in p.relay(a):
                cp.start()

    def finish(self, srcs, outs, sems):
        n, p = self._parts(srcs, outs, sems)
        for a in range(n):
            for cp in p.diag_halves(a):
                cp.wait_recv()
            p.last(a).start()
        for a in range(n):
            for cp in p.from_sibling(a):
                cp.wait_recv()
        for cp in p.own() + [cp for a in range(n) for cp in p.relay(a) + [p.last(a)]]:
            cp.wait_send()
        for cp in p.mine():
            cp.wait()


class _RsStage2(_Exchange):
    def start(self, srcs, outs, sems):
        for cp in self._copies(srcs, outs, sems):
            cp.start()

    def finish(self, srcs, outs, sems):
        copies = self._copies(srcs, outs, sems)
        for cp in copies:
            cp.wait_recv()
        for cp in copies:
            cp.wait_send()


    def __init__(self, pbs):
        n = len(pbs)
        self.inputs = list(pbs)
        self.out_shapes = [jax.ShapeDtypeStruct((3,) + p.shape[1:], p.dtype) for p in pbs]
        self.sems = [pltpu.SemaphoreType.DMA((n, 3)), pltpu.SemaphoreType.DMA((n, 3))]

    def _copies(self, srcs, outs, sems):
        send_sems, recv_sems = sems
        x, y, c = lax.axis_index("x"), lax.axis_index("y"), lax.axis_index("c")
        chips = [(1 - x, y), (x, 1 - y), (1 - x, 1 - y)]
        return [pltpu.make_async_remote_copy(
            src_ref=srcs[a].at[2 * chip[0] + chip[1]], dst_ref=outs[a].at[j], send_sem=send_sems.at[a, j],
            recv_sem=recv_sems.at[a, j], device_id=(*chip, c), device_id_type=MESH)
            for a in range(len(srcs)) for j, chip in enumerate(chips)]


class _DirectGather(_Exchange):
    def __init__(self, arrays):
        n = len(arrays)
        self.inputs = list(arrays)
        self.out_shapes = [jax.ShapeDtypeStruct((N_DEV,) + a.shape, a.dtype) for a in arrays]
        self.sems = [pltpu.SemaphoreType.DMA((n, 7)), pltpu.SemaphoreType.DMA((n, 7)), pltpu.SemaphoreType.DMA((n,))]

    def _copies(self, srcs, outs, sems):
        send_sems, recv_sems, local_sems = sems
        x, y, c = lax.axis_index("x"), lax.axis_index("y"), lax.axis_index("c")
        me = 4 * x + 2 * y + c
        remote, local = [], []
        for a in range(len(srcs)):
            local.append(pltpu.make_async_copy(srcs[a], outs[a].at[me], local_sems.at[a]))
            for k in range(1, N_DEV):
                peer = (x ^ (k >> 2), y ^ ((k >> 1) & 1), c ^ (k & 1))
                remote.append(pltpu.make_async_remote_copy(
                    src_ref=srcs[a], dst_ref=outs[a].at[me], send_sem=send_sems.at[a, k - 1],
                    recv_sem=recv_sems.at[a, k - 1], device_id=peer, device_id_type=MESH))
        return remote, local

    def start(self, srcs, outs, sems):
        remote, local = self._copies(srcs, outs, sems)
        for cp in local + remote:
            cp.start()

    def finish(self, srcs, outs, sems):
        remote, local = self._copies(srcs, outs, sems)
        for cp in remote:
            cp.wait_recv()
        for cp in remote:
            cp.wait_send()
        for cp in local:
            cp.wait()


class _Both(_Exchange):
    def __init__(self, a, b):
        self.a, self.b = a, b
        self.inputs = list(a.inputs) + list(b.inputs)
        self.out_shapes = list(a.out_shapes) + list(b.out_shapes)
        self.sems = list(a.sems) + list(b.sems)

    def _split(self, srcs, outs, sems):
        na, oa, sa = len(self.a.inputs), len(self.a.out_shapes), len(self.a.sems)
        return (srcs[:na], outs[:oa], sems[:sa]), (srcs[na:], outs[oa:], sems[sa:])

    def start(self, srcs, outs, sems):
        pa, pb = self._split(srcs, outs, sems)
        self.a.start(*pa)
        self.b.start(*pb)

    def middle(self, srcs, outs, sems):
        pa, pb = self._split(srcs, outs, sems)
        self.a.middle(*pa)
        self.b.middle(*pb)

    def finish(self, srcs, outs, sems):
        pa, pb = self._split(srcs, outs, sems)
        self.a.finish(*pa)
        self.b.finish(*pb)


_ANY = pl.BlockSpec(memory_space=pl.ANY)


def _run_exchange(ex, name):
    n_in, n_out = len(ex.inputs), len(ex.out_shapes)

    def body(*refs):
        srcs, outs, sems = refs[:n_in], refs[n_in:n_in + n_out], refs[n_in + n_out:]
        ex.start(srcs, outs, sems)
        ex.middle(srcs, outs, sems)
        ex.finish(srcs, outs, sems)

    return pl.pallas_call(
        body, name=name, out_shape=list(ex.out_shapes), in_specs=[_ANY] * n_in, out_specs=[_ANY] * n_out,
        scratch_shapes=list(ex.sems),
    )(*ex.inputs)


CARRY_MIDDLE = 0.7


def _call(body, name, grid, in_specs, out_specs, out_shape, args, sem, carry=None, middle=CARRY_MIDDLE, scratch=()):
    if carry is None:
        return pl.pallas_call(functools.partial(body), name=name, grid=grid, in_specs=in_specs, out_specs=out_specs,
                              out_shape=out_shape, scratch_shapes=list(scratch), compiler_params=_cparams(sem))(*args)
    n_in, n_out = len(in_specs), len(out_specs)
    nc_in, nc_out = len(carry.inputs), len(carry.out_shapes)

    def carried(*refs):
        ins = refs[:n_in]
        c_in = refs[n_in:n_in + nc_in]
        outs = refs[n_in + nc_in:n_in + nc_in + n_out]
        c_out = refs[n_in + nc_in + n_out:n_in + nc_in + n_out + nc_out]
        own = refs[n_in + nc_in + n_out + nc_out:n_in + nc_in + n_out + nc_out + len(scratch)]
        sems = refs[n_in + nc_in + n_out + nc_out + len(scratch):]
        ids = [pl.program_id(i) for i in range(len(grid))]
        is_first = functools.reduce(jnp.logical_and, [i == 0 for i in ids])
        is_last = functools.reduce(jnp.logical_and, [i == g - 1 for i, g in zip(ids, grid)])
        @pl.when(is_first)
        def _():
            carry.start(c_in, c_out, sems)

        if middle is not None:
            @pl.when(functools.reduce(jnp.logical_and, [ids[0] == round(middle * (grid[0] - 1))]
                                      + [i == 0 for i in ids[1:]]))
            def _():
                carry.middle(c_in, c_out, sems)

        body(*ins, *outs, *own)

        @pl.when(is_last)
        def _():
            if middle is None:
                carry.middle(c_in, c_out, sems)
            carry.finish(c_in, c_out, sems)

    return pl.pallas_call(
        carried, name=name, grid=grid, in_specs=list(in_specs) + [_ANY] * nc_in,
        out_specs=list(out_specs) + [_ANY] * nc_out, out_shape=list(out_shape) + list(carry.out_shapes),
        scratch_shapes=list(scratch) + list(carry.sems), compiler_params=_cparams(sem))(*args, *carry.inputs)


def _rs_stage1_sum(pos, gs, gbs, name):
    n = len(gs)
    _, rows, d = gs[0].shape

    def block(pc, q, other_core):
        return 4 * (q // 2) + 2 * (q % 2) + ((1 - pc) if other_core else pc)

    def body(pos_ref, *refs):
        g_refs, gb_refs = refs[:n], refs[n:2 * n]
        p_refs, pb_refs = refs[2 * n:3 * n], refs[3 * n:4 * n]
        land, send_sems, recv_sems = refs[4 * n:]
        q = pl.program_id(0)
        x, y, c = lax.axis_index("x"), lax.axis_index("y"), lax.axis_index("c")

        def copy(a, k):
            return pltpu.make_async_remote_copy(
                src_ref=gb_refs[a].at[block(c, k, True)], dst_ref=land.at[a, k],
                send_sem=send_sems.at[a, k], recv_sem=recv_sems.at[a, k], device_id=(x, y, 1 - c),
                device_id_type=MESH)

        @pl.when(q == 0)
        def _():
            for k in range(4):
                for a in range(n):
                    copy(a, k).start()

        for a in range(n):
            copy(a, q).wait_recv()
            p = g_refs[a][...] + land[a, q].astype(F32)
            p_refs[a][...] = p
            pb_refs[a][...] = p.astype(BF16)

        @pl.when(q == 3)
        def _():
            for k in range(4):
                for a in range(n):
                    copy(a, k).wait_send()

    blk = pl.BlockSpec((None, rows, d), lambda q, pos_ref: (q, 0, 0))
    mine = pl.BlockSpec((None, rows, d), lambda q, pos_ref: (block(pos_ref[2], q, False), 0, 0))
    res = pl.pallas_call(
        body, name=name,
        grid_spec=pltpu.PrefetchScalarGridSpec(
            num_scalar_prefetch=1, grid=(4,),
            in_specs=[mine] * n + [_ANY] * n, out_specs=[blk] * (2 * n),
            scratch_shapes=[pltpu.VMEM((n, 4, rows, d), BF16), pltpu.SemaphoreType.DMA((n, 4)),
                            pltpu.SemaphoreType.DMA((n, 4))]),
        out_shape=[jax.ShapeDtypeStruct((4, rows, d), F32)] * n + [jax.ShapeDtypeStruct((4, rows, d), BF16)] * n,
        compiler_params=_cparams(("arbitrary",)),
    )(pos, *gs, *gbs)
    return list(zip(res[:n], res[n:]))


def _rs_sum2_adam(pos, p, r2, w, m, v, name):
    _, rows, d = p.shape
    tr = rows // 2 if rows % 16 == 0 else rows

    def body(pos_ref, p_ref, r_ref, w_ref, m_ref, v_ref, g_ref, d_ref, nm_ref, nv_ref):
        r = r_ref[...].astype(F32)
        g = ((p_ref[...] + r[0]) + r[1]) + r[2]
        g_ref[...] = g
        d_ref[...], nm_ref[...], nv_ref[...] = _adam_math(w_ref[...], g, m_ref[...], v_ref[...])

    blk = pl.BlockSpec((tr, d), lambda i, pos_ref: (i, 0))
    return pl.pallas_call(
        body, name=name,
        grid_spec=pltpu.PrefetchScalarGridSpec(
            num_scalar_prefetch=1, grid=(rows // tr,),
            in_specs=[pl.BlockSpec((None, tr, d), lambda i, pos_ref: (2 * pos_ref[0] + pos_ref[1], i, 0)),
                      pl.BlockSpec((3, tr, d), lambda i, pos_ref: (0, i, 0)), blk, blk, blk],
            out_specs=[blk] * 4),
        out_shape=[jax.ShapeDtypeStruct((rows, d), F32)] * 4,
        compiler_params=_cparams(("arbitrary",)),
    )(pos, p, r2, w, m, v)


def _ffn_chunk(f):
    for cand in (256, 128):
        if f % cand == 0:
            return cand
    return f


def _loss_head(x, gg, target, loss_ref, dg_ref):
    @pl.when(pl.program_id(0) == 0)
    def _():
        loss_ref[...] = jnp.zeros_like(loss_ref)
        dg_ref[...] = jnp.zeros_like(dg_ref)

    r = lax.rsqrt(jnp.mean(x * x, axis=-1, keepdims=True) + EPS)
    xhat = x * r
    e = xhat * gg - target
    loss_ref[...] += 0.5 * jnp.sum(jnp.mean(e * e, axis=-1, keepdims=True), axis=0, keepdims=True)
    dy = e * (1.0 / x.shape[-1])
    dg_ref[...] += jnp.sum(dy * xhat, axis=0, keepdims=True)
    dxhat = dy * gg
    return r * (dxhat - xhat * jnp.mean(dxhat * xhat, axis=-1, keepdims=True))


def _ffn_fwd(h, g, wg_t, wu_t, wd, name, tm, carry=None, head=None, middle=CARRY_MIDDLE):
    s, d = h.shape
    f = wg_t.shape[0]
    tf = _ffn_chunk(f)

    def body(h_ref, g_ref, wg_ref, wu_ref, *refs):
        if wd is None:
            n_ref, gate_ref, up_ref, act_t_ref, act_ref = refs
        elif head is None:
            wd_ref, o_ref, n_ref, gate_ref, up_ref, act_t_ref, act_ref = refs
        else:
            wd_ref, gf_ref, t_ref, o_ref, n_ref, gate_ref, up_ref, act_t_ref, loss_ref, dgf_ref, act_ref = refs
        x = h_ref[...]
        r = lax.rsqrt(jnp.mean(x * x, axis=-1, keepdims=True) + EPS)
        nb = (x * r * g_ref[...]).astype(BF16)
        n_ref[...] = nb
        for j in range(f // tf):
            sl = slice(j * tf, (j + 1) * tf)
            gate = _dot_nt(nb, wg_ref[sl, :])
            up = _dot_nt(nb, wu_ref[sl, :])
            gate_ref[:, sl] = gate.astype(BF16)
            up_ref[:, sl] = up.astype(BF16)
            act = gate * _sigmoid(gate) * up
            act = act.astype(BF16)
            act_ref[:, sl] = act
            act_t_ref[sl, :] = act.T
        if wd is not None:
            h_out = x + 0.5 * _dot_nn(act_ref[...], wd_ref[...])
            o_ref[...] = h_out if head is None else _loss_head(h_out, gf_ref[...], t_ref[...], loss_ref, dgf_ref)

    row = lambda w: pl.BlockSpec((tm, w), lambda i: (i, 0))
    in_specs = [row(d), _whole((1, d)), _whole((f, d)), _whole((f, d))]
    out_specs = [row(d), row(f), row(f), pl.BlockSpec((f, tm), lambda i: (0, i))]
    out_shape = ([jax.ShapeDtypeStruct((s, d), BF16)] + [jax.ShapeDtypeStruct((s, f), BF16)] * 2
                 + [jax.ShapeDtypeStruct((f, s), BF16)])
    args = (h, g, wg_t, wu_t)
    scratch = ()
    if wd is None:
        out_specs, out_shape = out_specs + [row(f)], out_shape + [jax.ShapeDtypeStruct((s, f), BF16)]
    else:
        scratch = (pltpu.VMEM((tm, f), BF16),)
        in_specs, args = in_specs + [_whole((f, d))], args + (wd,)
        out_specs, out_shape = [row(d)] + out_specs, [jax.ShapeDtypeStruct((s, d), F32)] + out_shape
    if head is not None:
        in_specs += [_whole((1, d)), row(d)]
        out_specs += [pl.BlockSpec((1, 1), lambda i: (0, 0)), pl.BlockSpec((1, d), lambda i: (0, 0))]
        out_shape += [jax.ShapeDtypeStruct((1, 1), F32), jax.ShapeDtypeStruct((1, d), F32)]
        args += tuple(head)
    return _call(body, name, (s // tm,), in_specs, out_specs, out_shape, args, ("arbitrary",), carry, middle, scratch)


def _gate_grads(dh_ref, gate_ref, up_ref, wd_ref, dgate_ref, dup_ref, dgate_t_ref, dup_t_ref, dhh_ref, tf):
    dhh = (0.5 * dh_ref[...]).astype(BF16)
    dhh_ref[...] = dhh
    for j in range(gate_ref.shape[1] // tf):
        sl = slice(j * tf, (j + 1) * tf)
        gt = gate_ref[:, sl].astype(F32)
        u = up_ref[:, sl].astype(F32)
        dact = _dot_nt(dhh, wd_ref[sl, :])
        sg = _sigmoid(gt)
        dup = dact * (gt * sg)
        dgate = dact * u * (sg * (1.0 + gt * (1.0 - sg)))
        dup, dgate = dup.astype(BF16), dgate.astype(BF16)
        dup_ref[:, sl] = dup
        dgate_ref[:, sl] = dgate
        dup_t_ref[sl, :] = dup.T
        dgate_t_ref[sl, :] = dgate.T


def _input_grad(h_ref, dh_ref, dgate_ref, dup_ref, g_ref, wg_ref, wu_ref, o_ref, dg_ref):
    x = h_ref[...]
    r = lax.rsqrt(jnp.mean(x * x, axis=-1, keepdims=True) + EPS)
    xhat = x * r
    dn = _dot_nn(dgate_ref[...], wg_ref[...]) + _dot_nn(dup_ref[...], wu_ref[...])
    dxhat = dn * g_ref[...]
    o_ref[...] = dh_ref[...] + r * (dxhat - xhat * jnp.mean(dxhat * xhat, axis=-1, keepdims=True))

    @pl.when(pl.program_id(0) == 0)
    def _():
        dg_ref[...] = jnp.zeros_like(dg_ref)

    dg_ref[...] += jnp.sum(dn * xhat, axis=0, keepdims=True)


def _ffn_bwd(h_in, dh_out, gate, up, g, wg_t, wu_t, wd, name, tm):
    s, d = h_in.shape
    f = gate.shape[1]
    tf = _ffn_chunk(f)

    def body(h_ref, dh_ref, gate_ref, up_ref, g_ref, wg_ref, wu_ref, wd_ref,
             o_ref, dg_ref, dgate_t_ref, dup_t_ref, dhh_ref, dgate_ref, dup_ref):
        _gate_grads(dh_ref, gate_ref, up_ref, wd_ref, dgate_ref, dup_ref, dgate_t_ref, dup_t_ref, dhh_ref, tf)
        _input_grad(h_ref, dh_ref, dgate_ref, dup_ref, g_ref, wg_ref, wu_ref, o_ref, dg_ref)

    row = lambda w: pl.BlockSpec((tm, w), lambda i: (i, 0))
    col = pl.BlockSpec((f, tm), lambda i: (0, i))
    return pl.pallas_call(
        body, name=name, grid=(s // tm,),
        in_specs=[row(d), row(d), row(f), row(f), _whole((1, d)), _whole((f, d)), _whole((f, d)), _whole((f, d))],
        out_specs=[row(d), pl.BlockSpec((1, d), lambda i: (0, 0)), col, col, row(d)],
        out_shape=[jax.ShapeDtypeStruct((s, d), F32), jax.ShapeDtypeStruct((1, d), F32),
                   jax.ShapeDtypeStruct((f, s), BF16), jax.ShapeDtypeStruct((f, s), BF16),
                   jax.ShapeDtypeStruct((s, d), BF16)],
        scratch_shapes=[pltpu.VMEM((tm, f), BF16), pltpu.VMEM((tm, f), BF16)],
        compiler_params=_cparams(("arbitrary",)),
    )(h_in, dh_out, gate, up, g, wg_t, wu_t, wd)


def _ffn_bwd_gates(dh_out, gate, up, wd, name, tm, carry=None):
    s, d = dh_out.shape
    f = gate.shape[1]
    tf = _ffn_chunk(f)

    def body(dh_ref, gate_ref, up_ref, wd_ref, dgate_ref, dup_ref, dgate_t_ref, dup_t_ref, dhh_ref):
        _gate_grads(dh_ref, gate_ref, up_ref, wd_ref, dgate_ref, dup_ref, dgate_t_ref, dup_t_ref, dhh_ref, tf)

    row = lambda w: pl.BlockSpec((tm, w), lambda i: (i, 0))
    col = pl.BlockSpec((f, tm), lambda i: (0, i))
    return _call(
        body, name, (s // tm,), [row(d), row(f), row(f), _whole((f, d))], [row(f), row(f), col, col, row(d)],
        [jax.ShapeDtypeStruct((s, f), BF16)] * 2 + [jax.ShapeDtypeStruct((f, s), BF16)] * 2
        + [jax.ShapeDtypeStruct((s, d), BF16)],
        (dh_out, gate, up, wd), ("arbitrary",), carry)


def _ffn_bwd_input(h_in, dh_out, dgate, dup, g, wg_t, wu_t, name, tm, carry=None):
    s, d = h_in.shape
    f = dgate.shape[1]

    row = lambda w: pl.BlockSpec((tm, w), lambda i: (i, 0))
    return _call(
        _input_grad, name, (s // tm,),
        [row(d), row(d), row(f), row(f), _whole((1, d)), _whole((f, d)), _whole((f, d))],
        [row(d), pl.BlockSpec((1, d), lambda i: (0, 0))],
        [jax.ShapeDtypeStruct((s, d), F32), jax.ShapeDtypeStruct((1, d), F32)],
        (h_in, dh_out, dgate, dup, g, wg_t, wu_t), ("arbitrary",), carry)


def _wgrad_rs1(pos, a_t, b, name, carry=None):
    f, s = a_t.shape
    d = b.shape[1]
    fk = f // N_DEV
    nc_in = 0 if carry is None else len(carry.inputs)
    nc_out = 0 if carry is None else len(carry.out_shapes)

    def body(pos_ref, a_ref, b_ref, *refs):
        c_in = refs[:nc_in]
        p_ref, pb_ref = refs[nc_in:nc_in + 2]
        c_out = refs[nc_in + 2:nc_in + 2 + nc_out]
        stage, land, send_sems, recv_sems = refs[nc_in + 2 + nc_out:nc_in + 6 + nc_out]
        c_sems = refs[nc_in + 6 + nc_out:]
        t = pl.program_id(0)
        q = t % 4
        x, y, c = lax.axis_index("x"), lax.axis_index("y"), lax.axis_index("c")

        def push(k):
            return pltpu.make_async_remote_copy(src_ref=stage.at[k], dst_ref=land.at[k], send_sem=send_sems.at[k],
                                                recv_sem=recv_sems.at[k], device_id=(x, y, 1 - c), device_id_type=MESH)

        if carry is not None:
            @pl.when(t == 0)
            def _():
                carry.start(c_in, c_out, c_sems)

        g = _dot_nn(a_ref[...], b_ref[...])

        @pl.when(t < 4)
        def _():
            stage[q] = g.astype(BF16)
            push(q).start()

        @pl.when(t >= 4)
        def _():
            push(q).wait_recv()
            p = g + land[q].astype(F32)
            p_ref[...] = p
            pb_ref[...] = p.astype(BF16)

        @pl.when(t == 7)
        def _():
            for k in range(4):
                push(k).wait_send()
            if carry is not None:
                carry.middle(c_in, c_out, c_sems)
                carry.finish(c_in, c_out, c_sems)

    def shard(t, pos_ref):
        return 4 * ((t % 4) // 2) + 2 * (t % 2) + jnp.where(t < 4, 1 - pos_ref[2], pos_ref[2])

    out = pl.BlockSpec((None, fk, d), lambda t, pos_ref: (jnp.maximum(t - 4, 0), 0, 0))
    return pl.pallas_call(
        body, name=name,
        grid_spec=pltpu.PrefetchScalarGridSpec(
            num_scalar_prefetch=1, grid=(8,),
            in_specs=[pl.BlockSpec((fk, s), lambda t, pos_ref: (shard(t, pos_ref), 0)),
                      pl.BlockSpec((s, d), lambda t, pos_ref: (0, 0), pipeline_mode=pl.Buffered(1))]
            + [_ANY] * nc_in,
            out_specs=[out, out] + [_ANY] * nc_out,
            scratch_shapes=[pltpu.VMEM((4, fk, d), BF16), pltpu.VMEM((4, fk, d), BF16),
                            pltpu.SemaphoreType.DMA((4,)), pltpu.SemaphoreType.DMA((4,))]
            + ([] if carry is None else list(carry.sems))),
        out_shape=[jax.ShapeDtypeStruct((4, fk, d), F32), jax.ShapeDtypeStruct((4, fk, d), BF16)]
        + ([] if carry is None else list(carry.out_shapes)),
        compiler_params=_cparams(("arbitrary",)),
    )(pos, a_t, b, *([] if carry is None else carry.inputs))


def _rope(t, c, sa, sb, reps):
    c, sa, sb = (jnp.tile(v, (1, reps)) if reps > 1 else v for v in (c, sa, sb))
    w = t.shape[1]
    return t * c + pltpu.roll(t, w - 8, 1) * sa + pltpu.roll(t, 8, 1) * sb


def _rope_bwd(dt, c, sa, sb, reps):
    c, sa, sb = (jnp.tile(v, (1, reps)) if reps > 1 else v for v in (c, sa, sb))
    w = dt.shape[1]
    return dt * c + pltpu.roll(dt * sa, 8, 1) + pltpu.roll(dt * sb, w - 8, 1)


def _mix_in(h, g, win_t, tabs, name, tm):
    s, d = h.shape
    n_in = win_t.shape[0]

    def body(h_ref, g_ref, w_ref, c_ref, sa_ref, sb_ref, q_ref, k_ref, v_ref, pc_ref, n_ref):
        x = h_ref[...]
        r = lax.rsqrt(jnp.mean(x * x, axis=-1, keepdims=True) + EPS)
        nb = (x * r * g_ref[...]).astype(BF16)
        n_ref[...] = nb
        u = _dot_nt(nb, w_ref[...])
        c, sa, sb = c_ref[...], sa_ref[...], sb_ref[...]
        q_ref[...] = _rope(u[:, :ATTN_W], c, sa, sb, ATTN_W // 128).astype(BF16)
        k_ref[...] = _rope(u[:, ATTN_W:ATTN_W + KV_W], c, sa, sb, 1).astype(BF16)
        v_ref[...] = u[:, ATTN_W + KV_W:ATTN_W + 2 * KV_W].astype(BF16)
        pc_ref[...] = u[:, ATTN_W + 2 * KV_W:]

    row = lambda w: pl.BlockSpec((tm, w), lambda i: (i, 0))
    return pl.pallas_call(
        body, name=name, grid=(s // tm,),
        in_specs=[row(d), _whole((1, d)), _whole((n_in, d)), row(128), row(128), row(128)],
        out_specs=[row(ATTN_W), row(KV_W), row(KV_W), row(POOL_W), row(d)],
        out_shape=[jax.ShapeDtypeStruct((s, ATTN_W), BF16), jax.ShapeDtypeStruct((s, KV_W), BF16),
                   jax.ShapeDtypeStruct((s, KV_W), BF16), jax.ShapeDtypeStruct((s, POOL_W), F32),
                   jax.ShapeDtypeStruct((s, d), BF16)],
        compiler_params=_cparams(("arbitrary",)),
    )(h, g, win_t, *tabs)


def _band_mask(n, nb, transposed):
    shape = (3 * BLK, 2 * BLK) if transposed else (2 * BLK, 3 * BLK)
    i = lax.broadcasted_iota(jnp.int32, shape, 1 if transposed else 0) % BLK
    j = lax.broadcasted_iota(jnp.int32, shape, 0 if transposed else 1)
    kpos = (n - 1) * BLK + j
    return (j >= i) & (j <= i + 2 * BLK) & (kpos >= 0) & (kpos < nb * BLK)


def _block_diag(t, kh):
    tf = t.astype(F32)
    tr = pltpu.roll(tf, HEAD_DIM, 1)
    lo = lax.broadcasted_iota(jnp.int32, tf.shape, 1) < HEAD_DIM
    top, bot = (tf, tr) if kh == 0 else (tr, tf)
    return jnp.concatenate([jnp.where(lo, top, 0.0), jnp.where(lo, 0.0, bot)], axis=0).astype(BF16)


def _fold_diag(tbd):
    lo = lax.broadcasted_iota(jnp.int32, (3 * BLK, 2 * HEAD_DIM), 1) < HEAD_DIM
    t = jnp.where(lo, tbd[:3 * BLK], tbd[3 * BLK:])
    return t + pltpu.roll(t, HEAD_DIM, 1)


def _stack_pairs(x, kh):
    return jnp.concatenate([x[:, (2 * kh) * 128:(2 * kh + 1) * 128], x[:, (2 * kh + 1) * 128:(2 * kh + 2) * 128]], axis=0)


def _sink_of(sink_ref, kh, half, axis):
    shape = (2 * BLK, 1) if axis == 0 else (1, 2 * BLK)
    first = lax.broadcasted_iota(jnp.int32, shape, axis) < BLK
    return jnp.where(first, sink_ref[0, GROUP * kh + half], sink_ref[0, GROUP * kh + 2 + half])


def _softmax_sink(sc, valid, sink, axis):
    sc = jnp.where(valid, sc, -1e30)
    m = jnp.maximum(jnp.max(sc, axis=axis, keepdims=True), sink)
    e = jnp.exp(sc - m)
    es = jnp.exp(sink - m)
    inv = 1.0 / (jnp.sum(e, axis=axis, keepdims=True) + es)
    return e * inv, es * inv


def _attn_blocks_per_step(nb):
    return next(nq for nq in (4, 2, 1) if nb % nq == 0)


def _band_specs(nq, nb, w, col=0):
    return [pl.BlockSpec((BLK, w), lambda m: (jnp.maximum(nq * m - 1, 0), col)),
            pl.BlockSpec((nq * BLK, w), lambda m: (m, col)),
            pl.BlockSpec((BLK, w), lambda m: (jnp.minimum(nq * m + nq, nb - 1), col))]


def _attn_pool_fwd(q, k, v, pc, sink, pool_w, pool_scale, pband, name, carry=None, middle=CARRY_MIDDLE):
    s = q.shape[0]
    nb = s // BLK

    nq = _attn_blocks_per_step(nb)

    def body(sink_ref, q_ref, k0, k1, k2, v0, v1, v2, p0, p1, p2, pw_ref, ps_ref, pb_ref, o_ref, o_t_ref):
        kall = jnp.concatenate([k0[...], k1[...], k2[...]], axis=0)
        vall = jnp.concatenate([v0[...], v1[...], v2[...]], axis=0)
        pall = jnp.concatenate([p0[...], p1[...], p2[...]], axis=0).astype(BF16)
        qall = q_ref[...] * SCORE_SCALE
        for j in range(nq):
            n = pl.program_id(0) * nq + j
            rows, band = slice(j * BLK, (j + 1) * BLK), slice(j * BLK, (j + 3) * BLK)
            valid = _band_mask(n, nb, False)
            kb, vb, qs = kall[band], vall[band], qall[rows]
            for kh in range(N_KV):
                sc = _dot_nt(_stack_pairs(qs, kh), _block_diag(kb, kh))
                p = [_softmax_sink(sc[:, half * 3 * BLK:(half + 1) * 3 * BLK], valid,
                                   _sink_of(sink_ref, kh, half, 0), 1)[0] for half in range(2)]
                o2 = _dot_nn(jnp.concatenate(p, axis=1).astype(BF16), _block_diag(vb, kh)).astype(BF16)
                o_ref[rows, (2 * kh) * 128:(2 * kh + 1) * 128] = o2[:BLK]
                o_ref[rows, (2 * kh + 1) * 128:(2 * kh + 2) * 128] = o2[BLK:]
            ext = pall[band]
            var = _variant_index(n, nb)
            for gi in range(POOL_G):
                gsl = slice(gi * POOL_GW, (gi + 1) * POOL_GW)
                dg = _dot_nn(pb_ref[var, gi], ext[:, gsl])
                yg = _dot_nn(dg.astype(BF16), pw_ref[gi].astype(BF16))
                o_ref[rows, ATTN_W + gi * POOL_GW:ATTN_W + (gi + 1) * POOL_GW] = (yg * ps_ref[:, gsl]).astype(BF16)
        o_t_ref[...] = o_ref[...].astype(F32).T.astype(BF16)

    return _call(
        body, name, (nb // nq,),
        [pl.BlockSpec(memory_space=pltpu.SMEM), pl.BlockSpec((nq * BLK, ATTN_W), lambda m: (m, 0)),
         *_band_specs(nq, nb, KV_W), *_band_specs(nq, nb, KV_W), *_band_specs(nq, nb, POOL_W),
         _whole((POOL_G, POOL_GW, POOL_GW)), _whole((1, POOL_W)), _whole(pband.shape)],
        [pl.BlockSpec((nq * BLK, ATTN_W + POOL_W), lambda m: (m, 0)),
         pl.BlockSpec((ATTN_W + POOL_W, nq * BLK), lambda m: (0, m))],
        [jax.ShapeDtypeStruct((s, ATTN_W + POOL_W), BF16), jax.ShapeDtypeStruct((ATTN_W + POOL_W, s), BF16)],
        (sink, q, k, k, k, v, v, v, pc, pc, pc, pool_w, pool_scale, pband), ("arbitrary",), carry, middle)


def _attn_pool_bwd(q, k, v, pc, dmix, sink, pool_w, pool_scale, pband, ptband, name, carry=None):
    s = q.shape[0]
    nb = s // BLK
    nq = _attn_blocks_per_step(nb)

    def body(sink_ref, q_ref, k0, k1, k2, v0, v1, v2, p0, p1, p2, da_ref, d0, d1, d2, pw_ref, ps_ref, pb_ref, ptb_ref,
             dq_ref, dk_ref, dv_ref, dpc_ref, dsink_ref, dpw_ref, dps_ref):
        @pl.when(pl.program_id(0) == 0)
        def _():
            dsink_ref[...] = jnp.zeros_like(dsink_ref)
            dpw_ref[...] = jnp.zeros_like(dpw_ref)
            dps_ref[...] = jnp.zeros_like(dps_ref)

        kall = jnp.concatenate([k0[...], k1[...], k2[...]], axis=0)
        vall = jnp.concatenate([v0[...], v1[...], v2[...]], axis=0)
        pall = jnp.concatenate([p0[...], p1[...], p2[...]], axis=0).astype(BF16)
        dpall = jnp.concatenate([d0[...], d1[...], d2[...]], axis=0)
        lo = lax.broadcasted_iota(jnp.int32, (3 * BLK, KV_W), 1) < HEAD_DIM
        for j in range(nq):
            n = pl.program_id(0) * nq + j
            rows, band = slice(j * BLK, (j + 1) * BLK), slice(j * BLK, (j + 3) * BLK)
            valid = _band_mask(n, nb, True)
            kb, vb, qb = kall[band], vall[band], q_ref[rows, :]
            qs = qb * SCORE_SCALE
            da = da_ref[rows, :].astype(BF16)
            dk_fold, dv_fold = [], []
            for kh in range(N_KV):
                kbd, vbd = _block_diag(kb, kh), _block_diag(vb, kh)
                q2, do2 = _stack_pairs(qb, kh), _stack_pairs(da, kh)
                sc_t = _dot_nt(kbd, _stack_pairs(qs, kh))
                dp_t = _dot_nt(vbd, do2)
                p_t, ds_t = [], []
                for half in range(2):
                    keys = slice(half * 3 * BLK, (half + 1) * 3 * BLK)
                    p, ps = _softmax_sink(sc_t[keys], valid, _sink_of(sink_ref, kh, half, 1), 0)
                    delta = jnp.sum(p * dp_t[keys], axis=0, keepdims=True)
                    p_t.append(p.astype(BF16))
                    ds_t.append((p * (dp_t[keys] - delta)).astype(BF16))
                    dsk = -ps * delta
                    for pair in range(2):
                        h = GROUP * kh + 2 * pair + half
                        part = jnp.sum(dsk[:, pair * BLK:(pair + 1) * BLK], axis=1, keepdims=True)
                        dsink_ref[h:h + 1, :] += jnp.broadcast_to(part, (1, 128))
                p_t = jnp.concatenate(p_t, axis=0)
                ds_t = jnp.concatenate(ds_t, axis=0)
                dq2 = _dot_tn(ds_t, kbd) * SCORE_SCALE
                dq_ref[rows, (2 * kh) * 128:(2 * kh + 1) * 128] = dq2[:BLK]
                dq_ref[rows, (2 * kh + 1) * 128:(2 * kh + 2) * 128] = dq2[BLK:]
                dk_fold.append(_fold_diag(_dot_nn(ds_t, q2)) * SCORE_SCALE)
                dv_fold.append(_fold_diag(_dot_nn(p_t, do2)))
            dk_all = jnp.where(lo, dk_fold[0], dk_fold[1])
            dv_all = jnp.where(lo, dv_fold[0], dv_fold[1])
            for t in range(3):
                dk_ref[j, t] = dk_all[t * BLK:(t + 1) * BLK]
                dv_ref[j, t] = dv_all[t * BLK:(t + 1) * BLK]
            ext, dpe = pall[band], dpall[band]
            dpc_cur = dpall[(j + 1) * BLK:(j + 2) * BLK]
            var = _variant_index(n, nb)
            for gi in range(POOL_G):
                gsl = slice(gi * POOL_GW, (gi + 1) * POOL_GW)
                wg = pw_ref[gi].astype(BF16)
                sc = ps_ref[:, gsl]
                dgb = _dot_nn(pb_ref[var, gi], ext[:, gsl]).astype(BF16)
                yg = _dot_nn(dgb, wg)
                dps_ref[:, gsl] += jnp.sum(dpc_cur[:, gsl] * yg, axis=0, keepdims=True)
                dpw_ref[gi] += _dot_tn(dgb, (dpc_cur[:, gsl] * sc).astype(BF16))
                dd = _dot_nt((dpe[:, gsl] * sc).astype(BF16), wg)
                dpc_ref[rows, gsl] = _dot_nn(ptb_ref[var, gi], dd.astype(BF16))

    fixed = lambda shape: pl.BlockSpec(shape, lambda m: (0,) * len(shape))
    return _call(
        body, name, (nb // nq,),
        [pl.BlockSpec(memory_space=pltpu.SMEM), pl.BlockSpec((nq * BLK, ATTN_W), lambda m: (m, 0)),
         *_band_specs(nq, nb, KV_W), *_band_specs(nq, nb, KV_W), *_band_specs(nq, nb, POOL_W),
         pl.BlockSpec((nq * BLK, ATTN_W), lambda m: (m, 0)), *_band_specs(nq, nb, POOL_W, 1),
         _whole((POOL_G, POOL_GW, POOL_GW)), _whole((1, POOL_W)), _whole(pband.shape), _whole(ptband.shape)],
        [pl.BlockSpec((nq * BLK, ATTN_W), lambda m: (m, 0)),
         pl.BlockSpec((nq, 3, BLK, KV_W), lambda m: (m, 0, 0, 0)),
         pl.BlockSpec((nq, 3, BLK, KV_W), lambda m: (m, 0, 0, 0)),
         pl.BlockSpec((nq * BLK, POOL_W), lambda m: (m, 0)),
         fixed((N_HEADS, 128)), fixed((POOL_G, POOL_GW, POOL_GW)), fixed((1, POOL_W))],
        [jax.ShapeDtypeStruct((s, ATTN_W), F32), jax.ShapeDtypeStruct((nb, 3, BLK, KV_W), F32),
         jax.ShapeDtypeStruct((nb, 3, BLK, KV_W), F32), jax.ShapeDtypeStruct((s, POOL_W), F32),
         jax.ShapeDtypeStruct((N_HEADS, 128), F32),
         jax.ShapeDtypeStruct((POOL_G, POOL_GW, POOL_GW), F32), jax.ShapeDtypeStruct((1, POOL_W), F32)],
        (sink, q, k, k, k, v, v, v, pc, pc, pc, dmix, dmix, dmix, dmix, pool_w, pool_scale, pband, ptband),
        ("arbitrary",), carry)


def _mix_out(h, mix, w_out, name, tm, scale=1.0, carry=None, middle=CARRY_MIDDLE):
    s, d = h.shape
    w = mix.shape[1]

    def body(h_ref, m_ref, w_ref, o_ref):
        o_ref[...] = h_ref[...] + scale * _dot_nn(m_ref[...], w_ref[...])

    row = lambda c: pl.BlockSpec((tm, c), lambda i: (i, 0))
    return _call(body, name, (s // tm,), [row(d), row(w), _whole((w, d))], [row(d)],
                 [jax.ShapeDtypeStruct((s, d), F32)], (h, mix, w_out), ("arbitrary",), carry, middle)


def _mix_out_bwd(dh, w_out, name, tm):
    s, d = dh.shape
    w = w_out.shape[0]

    def body(dh_ref, w_ref, o_ref, dhb_ref):
        dhb = dh_ref[...].astype(BF16)
        dhb_ref[...] = dhb
        o_ref[...] = _dot_nt(dhb, w_ref[...])

    row = lambda c: pl.BlockSpec((tm, c), lambda i: (i, 0))
    return pl.pallas_call(
        body, name=name, grid=(s // tm,), in_specs=[row(d), _whole((w, d))], out_specs=[row(w), row(d)],
        out_shape=[jax.ShapeDtypeStruct((s, w), F32), jax.ShapeDtypeStruct((s, d), BF16)],
        compiler_params=_cparams(("arbitrary",)),
    )(dh, w_out)


def _mix_in_bwd(h, dh, g, win_t, dq, dkp, dvp, dpc, tabs, name, tm, carry=None):
    s, d = h.shape
    nb = s // BLK
    nt = tm // BLK
    n_in = win_t.shape[0]

    def band_sum(n, before, own, after, prev_last, next_first):
        lo = (n > 0).astype(F32)
        hi = (n < s // tm - 1).astype(F32)
        blocks = []
        for b in range(nt):
            from_prev = prev_last[...] * lo if b == 0 else before[b - 1]
            from_next = next_first[...] * hi if b == nt - 1 else after[b + 1]
            blocks.append(from_prev + own[b] + from_next)
        return jnp.concatenate(blocks, axis=0)

    def body(h_ref, dh_ref, g_ref, w_ref, dq_ref, k2, k1, k0, kp, kn, v2, v1, v0, vp, vn, dpc_ref, c_ref, sa_ref,
             sb_ref, o_ref, du_ref, dg_ref):
        n = pl.program_id(0)
        dk = band_sum(n, k2, k1, k0, kp, kn)
        dv = band_sum(n, v2, v1, v0, vp, vn)
        c, sa, sb = c_ref[...], sa_ref[...], sb_ref[...]
        du = jnp.concatenate([_rope_bwd(dq_ref[...], c, sa, sb, ATTN_W // 128), _rope_bwd(dk, c, sa, sb, 1), dv,
                              dpc_ref[...]], axis=1)
        du_ref[...] = du.T.astype(BF16)
        dn = _dot_nn(du.astype(BF16), w_ref[...])
        x = h_ref[...]
        r = lax.rsqrt(jnp.mean(x * x, axis=-1, keepdims=True) + EPS)
        xhat = x * r
        dxhat = dn * g_ref[...]
        o_ref[...] = dh_ref[...] + r * (dxhat - xhat * jnp.mean(dxhat * xhat, axis=-1, keepdims=True))

        @pl.when(n == 0)
        def _():
            dg_ref[...] = jnp.zeros_like(dg_ref)

        dg_ref[...] += jnp.sum(dn * xhat, axis=0, keepdims=True)

    row = lambda w: pl.BlockSpec((tm, w), lambda n: (n, 0))
    slot = lambda t: pl.BlockSpec((nt, None, BLK, KV_W), lambda n, t=t: (n, t, 0, 0))
    parts = [slot(2), slot(1), slot(0),
             pl.BlockSpec((None, None, BLK, KV_W), lambda n: (jnp.maximum(nt * n - 1, 0), 2, 0, 0)),
             pl.BlockSpec((None, None, BLK, KV_W), lambda n: (jnp.minimum(nt * n + nt, nb - 1), 0, 0, 0))]
    return _call(
        body, name, (s // tm,),
        [row(d), row(d), _whole((1, d)), _whole((n_in, d)), row(ATTN_W), *parts, *parts, row(POOL_W),
         row(128), row(128), row(128)],
        [row(d), pl.BlockSpec((n_in, tm), lambda n: (0, n)), pl.BlockSpec((1, d), lambda n: (0, 0))],
        [jax.ShapeDtypeStruct((s, d), F32), jax.ShapeDtypeStruct((n_in, s), BF16), jax.ShapeDtypeStruct((1, d), F32)],
        (h, dh, g, win_t, dq, *[dkp] * 5, *[dvp] * 5, dpc, *tabs), ("arbitrary",), carry)


def _adam_math(w, g, m, v):
    m = ADAM_B1 * m + (1.0 - ADAM_B1) * g
    v = ADAM_B2 * v + (1.0 - ADAM_B2) * (g * g)
    m_hat = m / (1.0 - ADAM_B1 ** ADAM_STEP)
    v_hat = v / (1.0 - ADAM_B2 ** ADAM_STEP)
    delta = -ADAM_LR * (m_hat / (jnp.sqrt(v_hat) + ADAM_EPS) + ADAM_WD * w)
    return delta, m, v


def _adam_small(w, parts, late, m, v, name):
    rows, cols = w.shape

    def body(w_ref, p_ref, l_ref, m_ref, v_ref, g_ref, d_ref, nm_ref, nv_ref):
        g, gl = p_ref[0], l_ref[0]
        for k in range(1, N_DEV):
            g = g + p_ref[k]
            gl = gl + l_ref[k]
        g_ref[...] = g
        g_ref[SMALL_NORM1:SMALL_NORM1 + 8, :] = g[SMALL_NORM1:SMALL_NORM1 + 8] + gl
        d_ref[...], nm_ref[...], nv_ref[...] = _adam_math(w_ref[...], g_ref[...], m_ref[...], v_ref[...])

    return pl.pallas_call(
        body, name=name, out_shape=[jax.ShapeDtypeStruct((rows, cols), F32)] * 4,
    )(w, parts, late, m, v)


SMALL_NORM1 = 512


def _pack_small(norm1, normm, norm2, normf, sink, pool_w, pool_scale, loss=None):
    scale_rows = jnp.pad(pool_scale.reshape(4, 128), ((0, 4), (0, 0)))
    last_rows = jnp.pad(sink.reshape(1, N_HEADS), ((0, 7), (0, 128 - N_HEADS)))
    if loss is not None:
        last_rows = last_rows + jnp.pad(loss.reshape(1, 1), ((1, 6), (0, 127)))
    return jnp.concatenate([pool_w.reshape(512, 128), norm1.reshape(8, 128), normm.reshape(8, 128),
                            norm2.reshape(8, 128), normf.reshape(8, 128), scale_rows, last_rows], axis=0)


def _unpack_small(p):
    return dict(pool_w=p[:512].reshape(1, POOL_G, POOL_GW, POOL_GW), ffn1_norm=p[512:520].reshape(1, 1024),
                mix_norm=p[520:528].reshape(1, 1024), ffn2_norm=p[528:536].reshape(1, 1024),
                final_norm=p[536:544].reshape(1024), pool_scale=p[544:548].reshape(1, POOL_W),
                sink_logits=p[552, :N_HEADS].reshape(1, N_HEADS), loss=p[553, 0])


def kernel(x, ffn1_norm, ffn1_w_gate, ffn1_w_up, ffn1_w_down, mix_norm, w_in, sink_logits, pool_w, pool_scale, w_out, ffn2_norm, ffn2_w_gate, ffn2_w_up, ffn2_w_down, final_norm, loss_target, m_ffn1_norm, m_ffn1_w_gate, m_ffn1_w_up, m_ffn1_w_down, m_mix_norm, m_w_in, m_sink_logits, m_pool_w, m_pool_scale, m_w_out, m_ffn2_norm, m_ffn2_w_gate, m_ffn2_w_up, m_ffn2_w_down, m_final_norm, v_ffn1_norm, v_ffn1_w_gate, v_ffn1_w_up, v_ffn1_w_down, v_mix_norm, v_w_in, v_sink_logits, v_pool_w, v_pool_scale, v_w_out, v_ffn2_norm, v_ffn2_w_gate, v_ffn2_w_up, v_ffn2_w_down, v_final_norm):
    s, d = x.shape[1], x.shape[2]
    fk = ffn1_w_gate.shape[2]
    f = N_DEV * fk
    ink = w_in.shape[2]
    n_in = N_DEV * ink
    mixk = w_out.shape[1]
    tm = min(512, s)
    tm_bwd = min(256, s)
    pos = jnp.stack([lax.axis_index("x"), lax.axis_index("y"), lax.axis_index("c")]).astype(jnp.int32)

    t_bf = lambda w: w[0].T.astype(BF16)
    full = lambda a: a.reshape(N_DEV * a.shape[1], d)
    wg1, wu1 = map(full, _run_exchange(_AllGather([t_bf(ffn1_w_gate), t_bf(ffn1_w_up)]), "gather_ffn1"))
    second = _AllGather([ffn1_w_down[0].astype(BF16), t_bf(w_in), w_out[0].astype(BF16)])
    third = _AllGather([t_bf(ffn2_w_gate)])
    fourth = _AllGather([t_bf(ffn2_w_up), ffn2_w_down[0].astype(BF16)])

    tabs = _rope_tables(s)
    pband, ptband = _pool_tables(s)
    g1, gm, g2, gf = ffn1_norm, mix_norm, ffn2_norm, final_norm.reshape(1, d)

    x0 = x[0]
    n1, gate1, up1, act1_t, act1, *gathered = _ffn_fwd(x0, g1, wg1, wu1, None, "ffn1_gate_up", tm, carry=second)
    wd1, win_t, wout = map(full, gathered)
    h1, wg2 = _mix_out(x0, act1, wd1, "ffn1_down", tm, scale=0.5, carry=third)
    wg2 = full(wg2)
    q, k, v, pc, n2 = _mix_in(h1, gm, win_t, tabs, "mix_in", tm)
    mix, mix_t, *gathered = _attn_pool_fwd(q, k, v, pc, sink_logits, pool_w[0], pool_scale, pband, "attn_pool_fwd",
                                           carry=fourth)
    wu2, wd2 = map(full, gathered)
    (h2,) = _mix_out(h1, mix, wout, "mix_out", tm)
    dh3, n3, gate2, up2, act2_t, loss_part, dgf = _ffn_fwd(h2, g2, wg2, wu2, wd2, "ffn2_fwd", tm,
                                                           head=(gf, loss_target[0]))

    sum1, recv2 = {}, {}

    def stage2(keys):
        return _RsStage2([sum1[key][1] for key in keys])

    dh2, dg2, dgate2_t, dup2_t, dhh3 = _ffn_bwd(h2, dh3, gate2, up2, g2, wg2, wu2, wd2, "ffn2_bwd", tm_bwd)
    sum1["g2"] = _wgrad_rs1(pos, dgate2_t, n3, "wgrad_gate2")
    sum1["u2"] = _wgrad_rs1(pos, dup2_t, n3, "wgrad_up2")
    sum1["d2"] = _wgrad_rs1(pos, act2_t, dhh3, "wgrad_down2")
    dmix, dh2b = _mix_out_bwd(dh2, wout, "mix_out_bwd", tm)
    sum1["out"] = _wgrad_rs1(pos, mix_t, dh2b, "wgrad_out")
    dq, dkp, dvp, dpc, dsink, dpw, dps, *r2 = _attn_pool_bwd(
        q, k, v, pc, dmix, sink_logits, pool_w[0], pool_scale, pband, ptband, "attn_pool_bwd",
        carry=stage2(["g2", "u2", "d2"]))
    recv2.update(zip(["g2", "u2", "d2"], r2))
    dh1, du_t, dgm, recv2["out"] = _mix_in_bwd(h1, dh2, gm, win_t, dq, dkp, dvp, dpc, tabs, "mix_in_bwd", tm,
                                               carry=stage2(["out"]))
    sum1["in"] = _wgrad_rs1(pos, du_t, n2, "wgrad_in")
    small_part = _pack_small(jnp.zeros_like(dgm), dgm, dg2, dgf, dsink[:, 0], dpw, dps, loss_part)
    dgate1, dup1, dgate1_t, dup1_t, dhh1, recv2["in"], small_all = _ffn_bwd_gates(
        dh1, gate1, up1, wd1, "ffn1_bwd_gates", tm, carry=_Both(stage2(["in"]), _AllGather([small_part])))
    sum1["g1"] = _wgrad_rs1(pos, dgate1_t, n1, "wgrad_gate1")
    *sum1["u1"], recv2["g1"] = _wgrad_rs1(pos, dup1_t, n1, "wgrad_up1", carry=stage2(["g1"]))
    *sum1["d1"], recv2["u1"] = _wgrad_rs1(pos, act1_t, dhh1, "wgrad_down1", carry=stage2(["u1"]))
    dx, dg1, recv2["d1"] = _ffn_bwd_input(x0, dh1, dgate1, dup1, g1, wg1, wu1, "ffn1_bwd_input", tm,
                                          carry=stage2(["d1"]))

    (dg1_all,) = _run_exchange(_DirectGather([dg1.reshape(8, 128)]), "gather_norm1_grad")
    pk = lambda a, b, c_, e, s_, pw_, psc: _pack_small(a, b, c_, e, s_[0], pw_[0], psc)
    small_w = pk(ffn1_norm, mix_norm, ffn2_norm, final_norm, sink_logits, pool_w, pool_scale)
    small_m = pk(m_ffn1_norm, m_mix_norm, m_ffn2_norm, m_final_norm, m_sink_logits, m_pool_w, m_pool_scale)
    small_v = pk(v_ffn1_norm, v_mix_norm, v_ffn2_norm, v_final_norm, v_sink_logits, v_pool_w, v_pool_scale)
    sg, sd, sm, sv = [_unpack_small(a)
                      for a in _adam_small(small_w, small_all, dg1_all, small_m, small_v, "adam_small")]

    big = {}
    keys = ["g1", "u1", "d1", "g2", "u2", "d2", "in", "out"]
    names = ["ffn1_w_gate", "ffn1_w_up", "ffn1_w_down", "ffn2_w_gate", "ffn2_w_up", "ffn2_w_down", "w_in", "w_out"]
    transposed = [True, True, False, True, True, False, True, False]
    ws = [ffn1_w_gate, ffn1_w_up, ffn1_w_down, ffn2_w_gate, ffn2_w_up, ffn2_w_down, w_in, w_out]
    ms = [m_ffn1_w_gate, m_ffn1_w_up, m_ffn1_w_down, m_ffn2_w_gate, m_ffn2_w_up, m_ffn2_w_down, m_w_in, m_w_out]
    vs = [v_ffn1_w_gate, v_ffn1_w_up, v_ffn1_w_down, v_ffn2_w_gate, v_ffn2_w_up, v_ffn2_w_down, v_w_in, v_w_out]
    for key, nm, tr, w, m, vv in zip(keys, names, transposed, ws, ms, vs):
        view = (lambda a: jnp.swapaxes(a, 1, 2)[0]) if tr else (lambda a: a[0])
        back = (lambda a: jnp.swapaxes(a[None], 1, 2)) if tr else (lambda a: a[None])
        res = _rs_sum2_adam(pos, sum1[key][0], recv2[key], view(w), view(m), view(vv), "adam_" + nm)
        big[nm] = tuple(back(a) for a in res)

    loss = sg["loss"]
    all_names = ["ffn1_norm", "ffn1_w_gate", "ffn1_w_up", "ffn1_w_down", "mix_norm", "w_in", "sink_logits", "pool_w",
                 "pool_scale", "w_out", "ffn2_norm", "ffn2_w_gate", "ffn2_w_up", "ffn2_w_down", "final_norm"]
    outs = [loss, dx[None]]
    for idx, src in enumerate((sg, sd, sm, sv)):
        for nm in all_names:
            outs.append(big[nm][idx] if nm in big else src[nm])
    return tuple(outs)
```

```python
import functools

import jax
import jax.numpy as jnp
import numpy as np
from jax import lax
from jax.experimental import pallas as pl
from jax.experimental.pallas import tpu as pltpu

F32 = jnp.float32
BF16 = jnp.bfloat16
MESH = pl.DeviceIdType.MESH
N_DEV = 8

EPS = 1e-6
HEAD_DIM = 64
N_HEADS = 8
N_KV = 2
GROUP = N_HEADS // N_KV
ATTN_W = N_HEADS * HEAD_DIM
KV_W = N_KV * HEAD_DIM
POOL_W = 512
POOL_G = 4
POOL_GW = POOL_W // POOL_G
POOL_WINDOWS = (2, 4, 8, 16)
BLK = 128
ROT = 16
ROPE_THETA = 500000.0
SCORE_SCALE = HEAD_DIM ** -0.5

ADAM_LR, ADAM_B1, ADAM_B2, ADAM_EPS, ADAM_WD, ADAM_STEP = 0.001, 0.9, 0.999, 1e-08, 0.01, 10

VMEM_LIMIT = 56 * 1024 * 1024


def _cparams(sem=None, **kw):
    if sem is not None:
        kw["dimension_semantics"] = sem
    return pltpu.CompilerParams(vmem_limit_bytes=VMEM_LIMIT, **kw)


def _whole(shape):
    nd = len(shape)
    return pl.BlockSpec(shape, lambda *_: (0,) * nd, pipeline_mode=pl.Buffered(1))


def _sigmoid(z):
    return 1.0 / (1.0 + jnp.exp(-z))


def _dot_nt(a, b):
    return lax.dot_general(a, b, (((1,), (1,)), ((), ())), preferred_element_type=F32)


def _dot_nn(a, b):
    return lax.dot_general(a, b, (((1,), (0,)), ((), ())), preferred_element_type=F32)


def _dot_tn(a, b):
    return lax.dot_general(a, b, (((0,), (0,)), ((), ())), preferred_element_type=F32)


def _rope_tables(s):
    inv_freq = ROPE_THETA ** (-np.arange(0, ROT, 2, dtype=np.float64) / ROT)
    ang = np.arange(s, dtype=np.float64)[:, None] * inv_freq[None, :]
    c = np.ones((s, HEAD_DIM)); sa = np.zeros((s, HEAD_DIM)); sb = np.zeros((s, HEAD_DIM))
    c[:, :8] = np.cos(ang); c[:, 8:16] = np.cos(ang)
    sa[:, :8] = -np.sin(ang)
    sb[:, 8:16] = np.sin(ang)
    t = lambda a: jnp.asarray(np.tile(a, (1, 2)).astype(np.float32))
    return t(c), t(sa), t(sb)


def _pool_weight(gi, t, s_pos, s):
    half = POOL_WINDOWS[gi] // 2

    def win(lo, hi):
        a = np.clip(lo, 0, s); b = np.clip(hi + 1, 0, s)
        inside = (s_pos >= a) & (s_pos < b)
        return inside / np.maximum(b - a, 1)

    w = 0.5 * (win(t - half, t + half - 1) + win(t - half + 1, t + half)) - (t == s_pos)
    return w * ((t >= 0) & (t < s) & (s_pos >= 0) & (s_pos < s))


def _pool_tables(s):
    nb = s // BLK
    fwd = np.zeros((3, POOL_G, BLK, 3 * BLK), np.float32)
    bwd = np.zeros((3, POOL_G, BLK, 3 * BLK), np.float32)
    for vi, n in enumerate((0, 1 if nb > 2 else 0, nb - 1)):
        i = n * BLK + np.arange(BLK)[:, None]
        j = (n - 1) * BLK + np.arange(3 * BLK)[None, :]
        for gi in range(POOL_G):
            fwd[vi, gi] = _pool_weight(gi, i, j, s)
            bwd[vi, gi] = _pool_weight(gi, j, i, s)
    return jnp.asarray(fwd, dtype=BF16), jnp.asarray(bwd, dtype=BF16)


def _variant_index(n, nb):
    return jnp.where(n == 0, 0, jnp.where(n == nb - 1, 2, 1))


class _Exchange:
    inputs = ()
    out_shapes = ()
    sems = ()

    def start(self, srcs, outs, sems):
        raise NotImplementedError

    def middle(self, srcs, outs, sems):
        pass

    def finish(self, srcs, outs, sems):
        raise NotImplementedError


class _AllGather(_Exchange):
    def __init__(self, arrays):
        n = len(arrays)
        self.inputs = list(arrays)
        self.out_shapes = [jax.ShapeDtypeStruct((N_DEV,) + a.shape, a.dtype) for a in arrays]
        self.sems = [pltpu.SemaphoreType.DMA((n, 8)), pltpu.SemaphoreType.DMA((n, 8)), pltpu.SemaphoreType.DMA((n,))]

    def _parts(self, srcs, outs, sems):
        send_sems, recv_sems, local_sems = sems
        n = len(srcs)
        x, y, c = lax.axis_index("x"), lax.axis_index("y"), lax.axis_index("c")
        me, sibling, xn, yn, diag = (x, y, c), (x, y, 1 - c), (1 - x, y, c), (x, 1 - y, c), (1 - x, 1 - y, c)

        def place(a, dev, half=None):
            block = outs[a].at[4 * dev[0] + 2 * dev[1] + dev[2]]
            if half is None:
                return block
            r2 = outs[a].shape[1] // 2
            return block.at[pl.ds(half * r2, r2)]

        def copy(a, k, dev, to, half=None, src=None):
            where = place(a, dev, half)
            return pltpu.make_async_remote_copy(
                src_ref=where if src is None else src, dst_ref=where, send_sem=send_sems.at[a, k],
                recv_sem=recv_sems.at[a, k], device_id=to, device_id_type=MESH)

        def other(dev):
            return (dev[0], dev[1], 1 - dev[2])

        class Parts:
            mine = staticmethod(lambda: [pltpu.make_async_copy(srcs[a], place(a, me), local_sems.at[a])
                                         for a in range(n)])
            own = staticmethod(lambda: [copy(a, k, me, to, src=srcs[a]) for a in range(n)
                                        for k, to in ((0, sibling), (1, xn), (2, yn))])
            relay = staticmethod(lambda a: [copy(a, 3, xn, yn, half=0), copy(a, 4, yn, xn, half=1),
                                            copy(a, 5, xn, sibling), copy(a, 6, yn, sibling)])
            last = staticmethod(lambda a: copy(a, 7, diag, sibling))
            from_x = staticmethod(lambda a: copy(a, 1, xn, me))
            from_y = staticmethod(lambda a: copy(a, 2, yn, me))
            diag_halves = staticmethod(lambda a: [copy(a, 3, diag, me, half=0), copy(a, 4, diag, me, half=1)])
            from_sibling = staticmethod(lambda a: [copy(a, 0, sibling, me), copy(a, 5, other(xn), me),
                                                   copy(a, 6, other(yn), me), copy(a, 7, other(diag), me)])

        return n, Parts

    def start(self, srcs, outs, sems):
        _, p = self._parts(srcs, outs, sems)
        for cp in p.mine() + p.own():
            cp.start()

    def middle(self, srcs, outs, sems):
        n, p = self._parts(srcs, outs, sems)
        for a in range(n):
            p.from_x(a).wait_recv()
            p.from_y(a).wait_recv()
            for cp in p.relay(a):
                cp.start()

    def finish(self, srcs, outs, sems):
        n, p = self._parts(srcs, outs, sems)
        for a in range(n):
            for cp in p.diag_halves(a):
                cp.wait_recv()
            p.last(a).start()
        for a in range(n):
            for cp in p.from_sibling(a):
                cp.wait_recv()
        for cp in p.own() + [cp for a in range(n) for cp in p.relay(a) + [p.last(a)]]:
            cp.wait_send()
        for cp in p.mine():
            cp.wait()


class _RsStage2(_Exchange):
    def start(self, srcs, outs, sems):
        for cp in self._copies(srcs, outs, sems):
            cp.start()

    def finish(self, srcs, outs, sems):
        copies = self._copies(srcs, outs, sems)
        for cp in copies:
            cp.wait_recv()
        for cp in copies:
            cp.wait_send()


    def __init__(self, pbs):
        n = len(pbs)
        self.inputs = list(pbs)
        self.out_shapes = [jax.ShapeDtypeStruct((3,) + p.shape[1:], p.dtype) for p in pbs]
        self.sems = [pltpu.SemaphoreType.DMA((n, 3)), pltpu.SemaphoreType.DMA((n, 3))]

    def _copies(self, srcs, outs, sems):
        send_sems, recv_sems = sems
        x, y, c = lax.axis_index("x"), lax.axis_index("y"), lax.axis_index("c")
        chips = [(1 - x, y), (x, 1 - y), (1 - x, 1 - y)]
        return [pltpu.make_async_remote_copy(
            src_ref=srcs[a].at[2 * chip[0] + chip[1]], dst_ref=outs[a].at[j], send_sem=send_sems.at[a, j],
            recv_sem=recv_sems.at[a, j], device_id=(*chip, c), device_id_type=MESH)
            for a in range(len(srcs)) for j, chip in enumerate(chips)]


class _DirectGather(_Exchange):
    def __init__(self, arrays):
        n = len(arrays)
        self.inputs = list(arrays)
        self.out_shapes = [jax.ShapeDtypeStruct((N_DEV,) + a.shape, a.dtype) for a in arrays]
        self.sems = [pltpu.SemaphoreType.DMA((n, 7)), pltpu.SemaphoreType.DMA((n, 7)), pltpu.SemaphoreType.DMA((n,))]

    def _copies(self, srcs, outs, sems):
        send_sems, recv_sems, local_sems = sems
        x, y, c = lax.axis_index("x"), lax.axis_index("y"), lax.axis_index("c")
        me = 4 * x + 2 * y + c
        remote, local = [], []
        for a in range(len(srcs)):
            local.append(pltpu.make_async_copy(srcs[a], outs[a].at[me], local_sems.at[a]))
            for k in range(1, N_DEV):
                peer = (x ^ (k >> 2), y ^ ((k >> 1) & 1), c ^ (k & 1))
                remote.append(pltpu.make_async_remote_copy(
                    src_ref=srcs[a], dst_ref=outs[a].at[me], send_sem=send_sems.at[a, k - 1],
                    recv_sem=recv_sems.at[a, k - 1], device_id=peer, device_id_type=MESH))
        return remote, local

    def start(self, srcs, outs, sems):
        remote, local = self._copies(srcs, outs, sems)
        for cp in local + remote:
            cp.start()

    def finish(self, srcs, outs, sems):
        remote, local = self._copies(srcs, outs, sems)
        for cp in remote:
            cp.wait_recv()
        for cp in remote:
            cp.wait_send()
        for cp in local:
            cp.wait()


class _Both(_Exchange):
    def __init__(self, a, b):
        self.a, self.b = a, b
        self.inputs = list(a.inputs) + list(b.inputs)
        self.out_shapes = list(a.out_shapes) + list(b.out_shapes)
        self.sems = list(a.sems) + list(b.sems)

    def _split(self, srcs, outs, sems):
        na, oa, sa = len(self.a.inputs), len(self.a.out_shapes), len(self.a.sems)
        return (srcs[:na], outs[:oa], sems[:sa]), (srcs[na:], outs[oa:], sems[sa:])

    def start(self, srcs, outs, sems):
        pa, pb = self._split(srcs, outs, sems)
        self.a.start(*pa)
        self.b.start(*pb)

    def middle(self, srcs, outs, sems):
        pa, pb = self._split(srcs, outs, sems)
        self.a.middle(*pa)
        self.b.middle(*pb)

    def finish(self, srcs, outs, sems):
        pa, pb = self._split(srcs, outs, sems)
        self.a.finish(*pa)
        self.b.finish(*pb)


_ANY = pl.BlockSpec(memory_space=pl.ANY)


def _run_exchange(ex, name):
    n_in, n_out = len(ex.inputs), len(ex.out_shapes)

    def body(*refs):
        srcs, outs, sems = refs[:n_in], refs[n_in:n_in + n_out], refs[n_in + n_out:]
        ex.start(srcs, outs, sems)
        ex.middle(srcs, outs, sems)
        ex.finish(srcs, outs, sems)

    return pl.pallas_call(
        body, name=name, out_shape=list(ex.out_shapes), in_specs=[_ANY] * n_in, out_specs=[_ANY] * n_out,
        scratch_shapes=list(ex.sems),
    )(*ex.inputs)


CARRY_MIDDLE = 0.7


def _call(body, name, grid, in_specs, out_specs, out_shape, args, sem, carry=None, middle=CARRY_MIDDLE, scratch=()):
    if carry is None:
        return pl.pallas_call(functools.partial(body), name=name, grid=grid, in_specs=in_specs, out_specs=out_specs,
                              out_shape=out_shape, scratch_shapes=list(scratch), compiler_params=_cparams(sem))(*args)
    n_in, n_out = len(in_specs), len(out_specs)
    nc_in, nc_out = len(carry.inputs), len(carry.out_shapes)

    def carried(*refs):
        ins = refs[:n_in]
        c_in = refs[n_in:n_in + nc_in]
        outs = refs[n_in + nc_in:n_in + nc_in + n_out]
        c_out = refs[n_in + nc_in + n_out:n_in + nc_in + n_out + nc_out]
        own = refs[n_in + nc_in + n_out + nc_out:n_in + nc_in + n_out + nc_out + len(scratch)]
        sems = refs[n_in + nc_in + n_out + nc_out + len(scratch):]
        ids = [pl.program_id(i) for i in range(len(grid))]
        is_first = functools.reduce(jnp.logical_and, [i == 0 for i in ids])
        is_last = functools.reduce(jnp.logical_and, [i == g - 1 for i, g in zip(ids, grid)])
        @pl.when(is_first)
        def _():
            carry.start(c_in, c_out, sems)

        if middle is not None:
            @pl.when(functools.reduce(jnp.logical_and, [ids[0] == round(middle * (grid[0] - 1))]
                                      + [i == 0 for i in ids[1:]]))
            def _():
                carry.middle(c_in, c_out, sems)

        body(*ins, *outs, *own)

        @pl.when(is_last)
        def _():
            if middle is None:
                carry.middle(c_in, c_out, sems)
            carry.finish(c_in, c_out, sems)

    return pl.pallas_call(
        carried, name=name, grid=grid, in_specs=list(in_specs) + [_ANY] * nc_in,
        out_specs=list(out_specs) + [_ANY] * nc_out, out_shape=list(out_shape) + list(carry.out_shapes),
        scratch_shapes=list(scratch) + list(carry.sems), compiler_params=_cparams(sem))(*args, *carry.inputs)


def _rs_stage1_sum(pos, gs, gbs, name):
    n = len(gs)
    _, rows, d = gs[0].shape

    def block(pc, q, other_core):
        return 4 * (q // 2) + 2 * (q % 2) + ((1 - pc) if other_core else pc)

    def body(pos_ref, *refs):
        g_refs, gb_refs = refs[:n], refs[n:2 * n]
        p_refs, pb_refs = refs[2 * n:3 * n], refs[3 * n:4 * n]
        land, send_sems, recv_sems = refs[4 * n:]
        q = pl.program_id(0)
        x, y, c = lax.axis_index("x"), lax.axis_index("y"), lax.axis_index("c")

        def copy(a, k):
            return pltpu.make_async_remote_copy(
                src_ref=gb_refs[a].at[block(c, k, True)], dst_ref=land.at[a, k],
                send_sem=send_sems.at[a, k], recv_sem=recv_sems.at[a, k], device_id=(x, y, 1 - c),
                device_id_type=MESH)

        @pl.when(q == 0)
        def _():
            for k in range(4):
                for a in range(n):
                    copy(a, k).start()

        for a in range(n):
            copy(a, q).wait_recv()
            p = g_refs[a][...] + land[a, q].astype(F32)
            p_refs[a][...] = p
            pb_refs[a][...] = p.astype(BF16)

        @pl.when(q == 3)
        def _():
            for k in range(4):
                for a in range(n):
                    copy(a, k).wait_send()

    blk = pl.BlockSpec((None, rows, d), lambda q, pos_ref: (q, 0, 0))
    mine = pl.BlockSpec((None, rows, d), lambda q, pos_ref: (block(pos_ref[2], q, False), 0, 0))
    res = pl.pallas_call(
        body, name=name,
        grid_spec=pltpu.PrefetchScalarGridSpec(
            num_scalar_prefetch=1, grid=(4,),
            in_specs=[mine] * n + [_ANY] * n, out_specs=[blk] * (2 * n),
            scratch_shapes=[pltpu.VMEM((n, 4, rows, d), BF16), pltpu.SemaphoreType.DMA((n, 4)),
                            pltpu.SemaphoreType.DMA((n, 4))]),
        out_shape=[jax.ShapeDtypeStruct((4, rows, d), F32)] * n + [jax.ShapeDtypeStruct((4, rows, d), BF16)] * n,
        compiler_params=_cparams(("arbitrary",)),
    )(pos, *gs, *gbs)
    return list(zip(res[:n], res[n:]))


def _rs_sum2_adam(pos, p, r2, w, m, v, name):
    _, rows, d = p.shape
    tr = rows // 2 if rows % 16 == 0 else rows

    def body(pos_ref, p_ref, r_ref, w_ref, m_ref, v_ref, g_ref, d_ref, nm_ref, nv_ref):
        r = r_ref[...].astype(F32)
        g = ((p_ref[...] + r[0]) + r[1]) + r[2]
        g_ref[...] = g
        d_ref[...], nm_ref[...], nv_ref[...] = _adam_math(w_ref[...], g, m_ref[...], v_ref[...])

    blk = pl.BlockSpec((tr, d), lambda i, pos_ref: (i, 0))
    return pl.pallas_call(
        body, name=name,
        grid_spec=pltpu.PrefetchScalarGridSpec(
            num_scalar_prefetch=1, grid=(rows // tr,),
            in_specs=[pl.BlockSpec((None, tr, d), lambda i, pos_ref: (2 * pos_ref[0] + pos_ref[1], i, 0)),
                      pl.BlockSpec((3, tr, d), lambda i, pos_ref: (0, i, 0)), blk, blk, blk],
            out_specs=[blk] * 4),
        out_shape=[jax.ShapeDtypeStruct((rows, d), F32)] * 4,
        compiler_params=_cparams(("arbitrary",)),
    )(pos, p, r2, w, m, v)


def _ffn_chunk(f):
    for cand in (256, 128):
        if f % cand == 0:
            return cand
    return f


def _loss_head(x, gg, target, loss_ref, dg_ref):
    @pl.when(pl.program_id(0) == 0)
    def _():
        loss_ref[...] = jnp.zeros_like(loss_ref)
        dg_ref[...] = jnp.zeros_like(dg_ref)

    r = lax.rsqrt(jnp.mean(x * x, axis=-1, keepdims=True) + EPS)
    xhat = x * r
    e = xhat * gg - target
    loss_ref[...] += 0.5 * jnp.sum(jnp.mean(e * e, axis=-1, keepdims=True), axis=0, keepdims=True)
    dy = e * (1.0 / x.shape[-1])
    dg_ref[...] += jnp.sum(dy * xhat, axis=0, keepdims=True)
    dxhat = dy * gg
    return r * (dxhat - xhat * jnp.mean(dxhat * xhat, axis=-1, keepdims=True))


def _ffn_fwd(h, g, wg_t, wu_t, wd, name, tm, carry=None, head=None, middle=CARRY_MIDDLE):
    s, d = h.shape
    f = wg_t.shape[0]
    tf = _ffn_chunk(f)

    def body(h_ref, g_ref, wg_ref, wu_ref, *refs):
        if wd is None:
            n_ref, gate_ref, up_ref, act_t_ref, act_ref = refs
        elif head is None:
            wd_ref, o_ref, n_ref, gate_ref, up_ref, act_t_ref, act_ref = refs
        else:
            wd_ref, gf_ref, t_ref, o_ref, n_ref, gate_ref, up_ref, act_t_ref, loss_ref, dgf_ref, act_ref = refs
        x = h_ref[...]
        r = lax.rsqrt(jnp.mean(x * x, axis=-1, keepdims=True) + EPS)
        nb = (x * r * g_ref[...]).astype(BF16)
        n_ref[...] = nb
        for j in range(f // tf):
            sl = slice(j * tf, (j + 1) * tf)
            gate = _dot_nt(nb, wg_ref[sl, :])
            up = _dot_nt(nb, wu_ref[sl, :])
            gate_ref[:, sl] = gate.astype(BF16)
            up_ref[:, sl] = up.astype(BF16)
            act = gate * _sigmoid(gate) * up
            act = act.astype(BF16)
            act_ref[:, sl] = act
            act_t_ref[sl, :] = act.T
        if wd is not None:
            h_out = x + 0.5 * _dot_nn(act_ref[...], wd_ref[...])
            o_ref[...] = h_out if head is None else _loss_head(h_out, gf_ref[...], t_ref[...], loss_ref, dgf_ref)

    row = lambda w: pl.BlockSpec((tm, w), lambda i: (i, 0))
    in_specs = [row(d), _whole((1, d)), _whole((f, d)), _whole((f, d))]
    out_specs = [row(d), row(f), row(f), pl.BlockSpec((f, tm), lambda i: (0, i))]
    out_shape = ([jax.ShapeDtypeStruct((s, d), BF16)] + [jax.ShapeDtypeStruct((s, f), BF16)] * 2
                 + [jax.ShapeDtypeStruct((f, s), BF16)])
    args = (h, g, wg_t, wu_t)
    scratch = ()
    if wd is None:
        out_specs, out_shape = out_specs + [row(f)], out_shape + [jax.ShapeDtypeStruct((s, f), BF16)]
    else:
        scratch = (pltpu.VMEM((tm, f), BF16),)
        in_specs, args = in_specs + [_whole((f, d))], args + (wd,)
        out_specs, out_shape = [row(d)] + out_specs, [jax.ShapeDtypeStruct((s, d), F32)] + out_shape
    if head is not None:
        in_specs += [_whole((1, d)), row(d)]
        out_specs += [pl.BlockSpec((1, 1), lambda i: (0, 0)), pl.BlockSpec((1, d), lambda i: (0, 0))]
        out_shape += [jax.ShapeDtypeStruct((1, 1), F32), jax.ShapeDtypeStruct((1, d), F32)]
        args += tuple(head)
    return _call(body, name, (s // tm,), in_specs, out_specs, out_shape, args, ("arbitrary",), carry, middle, scratch)


def _gate_grads(dh_ref, gate_ref, up_ref, wd_ref, dgate_ref, dup_ref, dgate_t_ref, dup_t_ref, dhh_ref, tf):
    dhh = (0.5 * dh_ref[...]).astype(BF16)
    dhh_ref[...] = dhh
    for j in range(gate_ref.shape[1] // tf):
        sl = slice(j * tf, (j + 1) * tf)
        gt = gate_ref[:, sl].astype(F32)
        u = up_ref[:, sl].astype(F32)
        dact = _dot_nt(dhh, wd_ref[sl, :])
        sg = _sigmoid(gt)
        dup = dact * (gt * sg)
        dgate = dact * u * (sg * (1.0 + gt * (1.0 - sg)))
        dup, dgate = dup.astype(BF16), dgate.astype(BF16)
        dup_ref[:, sl] = dup
        dgate_ref[:, sl] = dgate
        dup_t_ref[sl, :] = dup.T
        dgate_t_ref[sl, :] = dgate.T


def _input_grad(h_ref, dh_ref, dgate_ref, dup_ref, g_ref, wg_ref, wu_ref, o_ref, dg_ref):
    x = h_ref[...]
    r = lax.rsqrt(jnp.mean(x * x, axis=-1, keepdims=True) + EPS)
    xhat = x * r
    dn = _dot_nn(dgate_ref[...], wg_ref[...]) + _dot_nn(dup_ref[...], wu_ref[...])
    dxhat = dn * g_ref[...]
    o_ref[...] = dh_ref[...] + r * (dxhat - xhat * jnp.mean(dxhat * xhat, axis=-1, keepdims=True))

    @pl.when(pl.program_id(0) == 0)
    def _():
        dg_ref[...] = jnp.zeros_like(dg_ref)

    dg_ref[...] += jnp.sum(dn * xhat, axis=0, keepdims=True)


def _ffn_bwd(h_in, dh_out, gate, up, g, wg_t, wu_t, wd, name, tm):
    s, d = h_in.shape
    f = gate.shape[1]
    tf = _ffn_chunk(f)

    def body(h_ref, dh_ref, gate_ref, up_ref, g_ref, wg_ref, wu_ref, wd_ref,
             o_ref, dg_ref, dgate_t_ref, dup_t_ref, dhh_ref, dgate_ref, dup_ref):
        _gate_grads(dh_ref, gate_ref, up_ref, wd_ref, dgate_ref, dup_ref, dgate_t_ref, dup_t_ref, dhh_ref, tf)
        _input_grad(h_ref, dh_ref, dgate_ref, dup_ref, g_ref, wg_ref, wu_ref, o_ref, dg_ref)

    row = lambda w: pl.BlockSpec((tm, w), lambda i: (i, 0))
    col = pl.BlockSpec((f, tm), lambda i: (0, i))
    return pl.pallas_call(
        body, name=name, grid=(s // tm,),
        in_specs=[row(d), row(d), row(f), row(f), _whole((1, d)), _whole((f, d)), _whole((f, d)), _whole((f, d))],
        out_specs=[row(d), pl.BlockSpec((1, d), lambda i: (0, 0)), col, col, row(d)],
        out_shape=[jax.ShapeDtypeStruct((s, d), F32), jax.ShapeDtypeStruct((1, d), F32),
                   jax.ShapeDtypeStruct((f, s), BF16), jax.ShapeDtypeStruct((f, s), BF16),
                   jax.ShapeDtypeStruct((s, d), BF16)],
        scratch_shapes=[pltpu.VMEM((tm, f), BF16), pltpu.VMEM((tm, f), BF16)],
        compiler_params=_cparams(("arbitrary",)),
    )(h_in, dh_out, gate, up, g, wg_t, wu_t, wd)


def _ffn_bwd_gates(dh_out, gate, up, wd, name, tm, carry=None):
    s, d = dh_out.shape
    f = gate.shape[1]
    tf = _ffn_chunk(f)

    def body(dh_ref, gate_ref, up_ref, wd_ref, dgate_ref, dup_ref, dgate_t_ref, dup_t_ref, dhh_ref):
        _gate_grads(dh_ref, gate_ref, up_ref, wd_ref, dgate_ref, dup_ref, dgate_t_ref, dup_t_ref, dhh_ref, tf)

    row = lambda w: pl.BlockSpec((tm, w), lambda i: (i, 0))
    col = pl.BlockSpec((f, tm), lambda i: (0, i))
    return _call(
        body, name, (s // tm,), [row(d), row(f), row(f), _whole((f, d))], [row(f), row(f), col, col, row(d)],
        [jax.ShapeDtypeStruct((s, f), BF16)] * 2 + [jax.ShapeDtypeStruct((f, s), BF16)] * 2
        + [jax.ShapeDtypeStruct((s, d), BF16)],
        (dh_out, gate, up, wd), ("arbitrary",), carry)


def _ffn_bwd_input(h_in, dh_out, dgate, dup, g, wg_t, wu_t, name, tm, carry=None):
    s, d = h_in.shape
    f = dgate.shape[1]

    row = lambda w: pl.BlockSpec((tm, w), lambda i: (i, 0))
    return _call(
        _input_grad, name, (s // tm,),
        [row(d), row(d), row(f), row(f), _whole((1, d)), _whole((f, d)), _whole((f, d))],
        [row(d), pl.BlockSpec((1, d), lambda i: (0, 0))],
        [jax.ShapeDtypeStruct((s, d), F32), jax.ShapeDtypeStruct((1, d), F32)],
        (h_in, dh_out, dgate, dup, g, wg_t, wu_t), ("arbitrary",), carry)


def _wgrad_rs1(pos, a_t, b, name, carry=None):
    f, s = a_t.shape
    d = b.shape[1]
    fk = f // N_DEV
    nc_in = 0 if carry is None else len(carry.inputs)
    nc_out = 0 if carry is None else len(carry.out_shapes)

    def body(pos_ref, a_ref, b_ref, *refs):
        c_in = refs[:nc_in]
        p_ref, pb_ref = refs[nc_in:nc_in + 2]
        c_out = refs[nc_in + 2:nc_in + 2 + nc_out]
        stage, land, send_sems, recv_sems = refs[nc_in + 2 + nc_out:nc_in + 6 + nc_out]
        c_sems = refs[nc_in + 6 + nc_out:]
        t = pl.program_id(0)
        q = t % 4
        x, y, c = lax.axis_index("x"), lax.axis_index("y"), lax.axis_index("c")

        def push(k):
            return pltpu.make_async_remote_copy(src_ref=stage.at[k], dst_ref=land.at[k], send_sem=send_sems.at[k],
                                                recv_sem=recv_sems.at[k], device_id=(x, y, 1 - c), device_id_type=MESH)

        if carry is not None:
            @pl.when(t == 0)
            def _():
                carry.start(c_in, c_out, c_sems)

        g = _dot_nn(a_ref[...], b_ref[...])

        @pl.when(t < 4)
        def _():
            stage[q] = g.astype(BF16)
            push(q).start()

        @pl.when(t >= 4)
        def _():
            push(q).wait_recv()
            p = g + land[q].astype(F32)
            p_ref[...] = p
            pb_ref[...] = p.astype(BF16)

        @pl.when(t == 7)
        def _():
            for k in range(4):
                push(k).wait_send()
            if carry is not None:
                carry.middle(c_in, c_out, c_sems)
                carry.finish(c_in, c_out, c_sems)

    def shard(t, pos_ref):
        return 4 * ((t % 4) // 2) + 2 * (t % 2) + jnp.where(t < 4, 1 - pos_ref[2], pos_ref[2])

    out = pl.BlockSpec((None, fk, d), lambda t, pos_ref: (jnp.maximum(t - 4, 0), 0, 0))
    return pl.pallas_call(
        body, name=name,
        grid_spec=pltpu.PrefetchScalarGridSpec(
            num_scalar_prefetch=1, grid=(8,),
            in_specs=[pl.BlockSpec((fk, s), lambda t, pos_ref: (shard(t, pos_ref), 0)),
                      pl.BlockSpec((s, d), lambda t, pos_ref: (0, 0), pipeline_mode=pl.Buffered(1))]
            + [_ANY] * nc_in,
            out_specs=[out, out] + [_ANY] * nc_out,
            scratch_shapes=[pltpu.VMEM((4, fk, d), BF16), pltpu.VMEM((4, fk, d), BF16),
                            pltpu.SemaphoreType.DMA((4,)), pltpu.SemaphoreType.DMA((4,))]
            + ([] if carry is None else list(carry.sems))),
        out_shape=[jax.ShapeDtypeStruct((4, fk, d), F32), jax.ShapeDtypeStruct((4, fk, d), BF16)]
        + ([] if carry is None else list(carry.out_shapes)),
        compiler_params=_cparams(("arbitrary",)),
    )(pos, a_t, b, *([] if carry is None else carry.inputs))


def _rope(t, c, sa, sb, reps):
    c, sa, sb = (jnp.tile(v, (1, reps)) if reps > 1 else v for v in (c, sa, sb))
    w = t.shape[1]
    return t * c + pltpu.roll(t, w - 8, 1) * sa + pltpu.roll(t, 8, 1) * sb


def _rope_bwd(dt, c, sa, sb, reps):
    c, sa, sb = (jnp.tile(v, (1, reps)) if reps > 1 else v for v in (c, sa, sb))
    w = dt.shape[1]
    return dt * c + pltpu.roll(dt * sa, 8, 1) + pltpu.roll(dt * sb, w - 8, 1)


def _mix_in(h, g, win_t, tabs, name, tm):
    s, d = h.shape
    n_in = win_t.shape[0]

    def body(h_ref, g_ref, w_ref, c_ref, sa_ref, sb_ref, q_ref, k_ref, v_ref, pc_ref, n_ref):
        x = h_ref[...]
        r = lax.rsqrt(jnp.mean(x * x, axis=-1, keepdims=True) + EPS)
        nb = (x * r * g_ref[...]).astype(BF16)
        n_ref[...] = nb
        u = _dot_nt(nb, w_ref[...])
        c, sa, sb = c_ref[...], sa_ref[...], sb_ref[...]
        q_ref[...] = _rope(u[:, :ATTN_W], c, sa, sb, ATTN_W // 128).astype(BF16)
        k_ref[...] = _rope(u[:, ATTN_W:ATTN_W + KV_W], c, sa, sb, 1).astype(BF16)
        v_ref[...] = u[:, ATTN_W + KV_W:ATTN_W + 2 * KV_W].astype(BF16)
        pc_ref[...] = u[:, ATTN_W + 2 * KV_W:]

    row = lambda w: pl.BlockSpec((tm, w), lambda i: (i, 0))
    return pl.pallas_call(
        body, name=name, grid=(s // tm,),
        in_specs=[row(d), _whole((1, d)), _whole((n_in, d)), row(128), row(128), row(128)],
        out_specs=[row(ATTN_W), row(KV_W), row(KV_W), row(POOL_W), row(d)],
        out_shape=[jax.ShapeDtypeStruct((s, ATTN_W), BF16), jax.ShapeDtypeStruct((s, KV_W), BF16),
                   jax.ShapeDtypeStruct((s, KV_W), BF16), jax.ShapeDtypeStruct((s, POOL_W), F32),
                   jax.ShapeDtypeStruct((s, d), BF16)],
        compiler_params=_cparams(("arbitrary",)),
    )(h, g, win_t, *tabs)


def _band_mask(n, nb, transposed):
    shape = (3 * BLK, 2 * BLK) if transposed else (2 * BLK, 3 * BLK)
    i = lax.broadcasted_iota(jnp.int32, shape, 1 if transposed else 0) % BLK
    j = lax.broadcasted_iota(jnp.int32, shape, 0 if transposed else 1)
    kpos = (n - 1) * BLK + j
    return (j >= i) & (j <= i + 2 * BLK) & (kpos >= 0) & (kpos < nb * BLK)


def _block_diag(t, kh):
    tf = t.astype(F32)
    tr = pltpu.roll(tf, HEAD_DIM, 1)
    lo = lax.broadcasted_iota(jnp.int32, tf.shape, 1) < HEAD_DIM
    top, bot = (tf, tr) if kh == 0 else (tr, tf)
    return jnp.concatenate([jnp.where(lo, top, 0.0), jnp.where(lo, 0.0, bot)], axis=0).astype(BF16)


def _fold_diag(tbd):
    lo = lax.broadcasted_iota(jnp.int32, (3 * BLK, 2 * HEAD_DIM), 1) < HEAD_DIM
    t = jnp.where(lo, tbd[:3 * BLK], tbd[3 * BLK:])
    return t + pltpu.roll(t, HEAD_DIM, 1)


def _stack_pairs(x, kh):
    return jnp.concatenate([x[:, (2 * kh) * 128:(2 * kh + 1) * 128], x[:, (2 * kh + 1) * 128:(2 * kh + 2) * 128]], axis=0)


def _sink_of(sink_ref, kh, half, axis):
    shape = (2 * BLK, 1) if axis == 0 else (1, 2 * BLK)
    first = lax.broadcasted_iota(jnp.int32, shape, axis) < BLK
    return jnp.where(first, sink_ref[0, GROUP * kh + half], sink_ref[0, GROUP * kh + 2 + half])


def _softmax_sink(sc, valid, sink, axis):
    sc = jnp.where(valid, sc, -1e30)
    m = jnp.maximum(jnp.max(sc, axis=axis, keepdims=True), sink)
    e = jnp.exp(sc - m)
    es = jnp.exp(sink - m)
    inv = 1.0 / (jnp.sum(e, axis=axis, keepdims=True) + es)
    return e * inv, es * inv


def _attn_blocks_per_step(nb):
    return next(nq for nq in (4, 2, 1) if nb % nq == 0)


def _band_specs(nq, nb, w, col=0):
    return [pl.BlockSpec((BLK, w), lambda m: (jnp.maximum(nq * m - 1, 0), col)),
            pl.BlockSpec((nq * BLK, w), lambda m: (m, col)),
            pl.BlockSpec((BLK, w), lambda m: (jnp.minimum(nq * m + nq, nb - 1), col))]


def _attn_pool_fwd(q, k, v, pc, sink, pool_w, pool_scale, pband, name, carry=None, middle=CARRY_MIDDLE):
    s = q.shape[0]
    nb = s // BLK

    nq = _attn_blocks_per_step(nb)

    def body(sink_ref, q_ref, k0, k1, k2, v0, v1, v2, p0, p1, p2, pw_ref, ps_ref, pb_ref, o_ref, o_t_ref):
        kall = jnp.concatenate([k0[...], k1[...], k2[...]], axis=0)
        vall = jnp.concatenate([v0[...], v1[...], v2[...]], axis=0)
        pall = jnp.concatenate([p0[...], p1[...], p2[...]], axis=0).astype(BF16)
        qall = q_ref[...] * SCORE_SCALE
        for j in range(nq):
            n = pl.program_id(0) * nq + j
            rows, band = slice(j * BLK, (j + 1) * BLK), slice(j * BLK, (j + 3) * BLK)
            valid = _band_mask(n, nb, False)
            kb, vb, qs = kall[band], vall[band], qall[rows]
            for kh in range(N_KV):
                sc = _dot_nt(_stack_pairs(qs, kh), _block_diag(kb, kh))
                p = [_softmax_sink(sc[:, half * 3 * BLK:(half + 1) * 3 * BLK], valid,
                                   _sink_of(sink_ref, kh, half, 0), 1)[0] for half in range(2)]
                o2 = _dot_nn(jnp.concatenate(p, axis=1).astype(BF16), _block_diag(vb, kh)).astype(BF16)
                o_ref[rows, (2 * kh) * 128:(2 * kh + 1) * 128] = o2[:BLK]
                o_ref[rows, (2 * kh + 1) * 128:(2 * kh + 2) * 128] = o2[BLK:]
            ext = pall[band]
            var = _variant_index(n, nb)
            for gi in range(POOL_G):
                gsl = slice(gi * POOL_GW, (gi + 1) * POOL_GW)
                dg = _dot_nn(pb_ref[var, gi], ext[:, gsl])
                yg = _dot_nn(dg.astype(BF16), pw_ref[gi].astype(BF16))
                o_ref[rows, ATTN_W + gi * POOL_GW:ATTN_W + (gi + 1) * POOL_GW] = (yg * ps_ref[:, gsl]).astype(BF16)
        o_t_ref[...] = o_ref[...].astype(F32).T.astype(BF16)

    return _call(
        body, name, (nb // nq,),
        [pl.BlockSpec(memory_space=pltpu.SMEM), pl.BlockSpec((nq * BLK, ATTN_W), lambda m: (m, 0)),
         *_band_specs(nq, nb, KV_W), *_band_specs(nq, nb, KV_W), *_band_specs(nq, nb, POOL_W),
         _whole((POOL_G, POOL_GW, POOL_GW)), _whole((1, POOL_W)), _whole(pband.shape)],
        [pl.BlockSpec((nq * BLK, ATTN_W + POOL_W), lambda m: (m, 0)),
         pl.BlockSpec((ATTN_W + POOL_W, nq * BLK), lambda m: (0, m))],
        [jax.ShapeDtypeStruct((s, ATTN_W + POOL_W), BF16), jax.ShapeDtypeStruct((ATTN_W + POOL_W, s), BF16)],
        (sink, q, k, k, k, v, v, v, pc, pc, pc, pool_w, pool_scale, pband), ("arbitrary",), carry, middle)


def _attn_pool_bwd(q, k, v, pc, dmix, sink, pool_w, pool_scale, pband, ptband, name, carry=None):
    s = q.shape[0]
    nb = s // BLK
    nq = _attn_blocks_per_step(nb)

    def body(sink_ref, q_ref, k0, k1, k2, v0, v1, v2, p0, p1, p2, da_ref, d0, d1, d2, pw_ref, ps_ref, pb_ref, ptb_ref,
             dq_ref, dk_ref, dv_ref, dpc_ref, dsink_ref, dpw_ref, dps_ref):
        @pl.when(pl.program_id(0) == 0)
        def _():
            dsink_ref[...] = jnp.zeros_like(dsink_ref)
            dpw_ref[...] = jnp.zeros_like(dpw_ref)
            dps_ref[...] = jnp.zeros_like(dps_ref)

        kall = jnp.concatenate([k0[...], k1[...], k2[...]], axis=0)
        vall = jnp.concatenate([v0[...], v1[...], v2[...]], axis=0)
        pall = jnp.concatenate([p0[...], p1[...], p2[...]], axis=0).astype(BF16)
        dpall = jnp.concatenate([d0[...], d1[...], d2[...]], axis=0)
        lo = lax.broadcasted_iota(jnp.int32, (3 * BLK, KV_W), 1) < HEAD_DIM
        for j in range(nq):
            n = pl.program_id(0) * nq + j
            rows, band = slice(j * BLK, (j + 1) * BLK), slice(j * BLK, (j + 3) * BLK)
            valid = _band_mask(n, nb, True)
            kb, vb, qb = kall[band], vall[band], q_ref[rows, :]
            qs = qb * SCORE_SCALE
            da = da_ref[rows, :].astype(BF16)
            dk_fold, dv_fold = [], []
            for kh in range(N_KV):
                kbd, vbd = _block_diag(kb, kh), _block_diag(vb, kh)
                q2, do2 = _stack_pairs(qb, kh), _stack_pairs(da, kh)
                sc_t = _dot_nt(kbd, _stack_pairs(qs, kh))
                dp_t = _dot_nt(vbd, do2)
                p_t, ds_t = [], []
                for half in range(2):
                    keys = slice(half * 3 * BLK, (half + 1) * 3 * BLK)
                    p, ps = _softmax_sink(sc_t[keys], valid, _sink_of(sink_ref, kh, half, 1), 0)
                    delta = jnp.sum(p * dp_t[keys], axis=0, keepdims=True)
                    p_t.append(p.astype(BF16))
                    ds_t.append((p * (dp_t[keys] - delta)).astype(BF16))
                    dsk = -ps * delta
                    for pair in range(2):
                        h = GROUP * kh + 2 * pair + half
                        part = jnp.sum(dsk[:, pair * BLK:(pair + 1) * BLK], axis=1, keepdims=True)
                        dsink_ref[h:h + 1, :] += jnp.broadcast_to(part, (1, 128))
                p_t = jnp.concatenate(p_t, axis=0)
                ds_t = jnp.concatenate(ds_t, axis=0)
                dq2 = _dot_tn(ds_t, kbd) * SCORE_SCALE
                dq_ref[rows, (2 * kh) * 128:(2 * kh + 1) * 128] = dq2[:BLK]
                dq_ref[rows, (2 * kh + 1) * 128:(2 * kh + 2) * 128] = dq2[BLK:]
                dk_fold.append(_fold_diag(_dot_nn(ds_t, q2)) * SCORE_SCALE)
                dv_fold.append(_fold_diag(_dot_nn(p_t, do2)))
            dk_all = jnp.where(lo, dk_fold[0], dk_fold[1])
            dv_all = jnp.where(lo, dv_fold[0], dv_fold[1])
            for t in range(3):
                dk_ref[j, t] = dk_all[t * BLK:(t + 1) * BLK]
                dv_ref[j, t] = dv_all[t * BLK:(t + 1) * BLK]
            ext, dpe = pall[band], dpall[band]
            dpc_cur = dpall[(j + 1) * BLK:(j + 2) * BLK]
            var = _variant_index(n, nb)
            for gi in range(POOL_G):
                gsl = slice(gi * POOL_GW, (gi + 1) * POOL_GW)
                wg = pw_ref[gi].astype(BF16)
                sc = ps_ref[:, gsl]
                dgb = _dot_nn(pb_ref[var, gi], ext[:, gsl]).astype(BF16)
                yg = _dot_nn(dgb, wg)
                dps_ref[:, gsl] += jnp.sum(dpc_cur[:, gsl] * yg, axis=0, keepdims=True)
                dpw_ref[gi] += _dot_tn(dgb, (dpc_cur[:, gsl] * sc).astype(BF16))
                dd = _dot_nt((dpe[:, gsl] * sc).astype(BF16), wg)
                dpc_ref[rows, gsl] = _dot_nn(ptb_ref[var, gi], dd.astype(BF16))

    fixed = lambda shape: pl.BlockSpec(shape, lambda m: (0,) * len(shape))
    return _call(
        body, name, (nb // nq,),
        [pl.BlockSpec(memory_space=pltpu.SMEM), pl.BlockSpec((nq * BLK, ATTN_W), lambda m: (m, 0)),
         *_band_specs(nq, nb, KV_W), *_band_specs(nq, nb, KV_W), *_band_specs(nq, nb, POOL_W),
         pl.BlockSpec((nq * BLK, ATTN_W), lambda m: (m, 0)), *_band_specs(nq, nb, POOL_W, 1),
         _whole((POOL_G, POOL_GW, POOL_GW)), _whole((1, POOL_W)), _whole(pband.shape), _whole(ptband.shape)],
        [pl.BlockSpec((nq * BLK, ATTN_W), lambda m: (m, 0)),
         pl.BlockSpec((nq, 3, BLK, KV_W), lambda m: (m, 0, 0, 0)),
         pl.BlockSpec((nq, 3, BLK, KV_W), lambda m: (m, 0, 0, 0)),
         pl.BlockSpec((nq * BLK, POOL_W), lambda m: (m, 0)),
         fixed((N_HEADS, 128)), fixed((POOL_G, POOL_GW, POOL_GW)), fixed((1, POOL_W))],
        [jax.ShapeDtypeStruct((s, ATTN_W), F32), jax.ShapeDtypeStruct((nb, 3, BLK, KV_W), F32),
         jax.ShapeDtypeStruct((nb, 3, BLK, KV_W), F32), jax.ShapeDtypeStruct((s, POOL_W), F32),
         jax.ShapeDtypeStruct((N_HEADS, 128), F32),
         jax.ShapeDtypeStruct((POOL_G, POOL_GW, POOL_GW), F32), jax.ShapeDtypeStruct((1, POOL_W), F32)],
        (sink, q, k, k, k, v, v, v, pc, pc, pc, dmix, dmix, dmix, dmix, pool_w, pool_scale, pband, ptband),
        ("arbitrary",), carry)


def _mix_out(h, mix, w_out, name, tm, scale=1.0, carry=None, middle=CARRY_MIDDLE):
    s, d = h.shape
    w = mix.shape[1]

    def body(h_ref, m_ref, w_ref, o_ref):
        o_ref[...] = h_ref[...] + scale * _dot_nn(m_ref[...], w_ref[...])

    row = lambda c: pl.BlockSpec((tm, c), lambda i: (i, 0))
    return _call(body, name, (s // tm,), [row(d), row(w), _whole((w, d))], [row(d)],
                 [jax.ShapeDtypeStruct((s, d), F32)], (h, mix, w_out), ("arbitrary",), carry, middle)


def _mix_out_bwd(dh, w_out, name, tm):
    s, d = dh.shape
    w = w_out.shape[0]

    def body(dh_ref, w_ref, o_ref, dhb_ref):
        dhb = dh_ref[...].astype(BF16)
        dhb_ref[...] = dhb
        o_ref[...] = _dot_nt(dhb, w_ref[...])

    row = lambda c: pl.BlockSpec((tm, c), lambda i: (i, 0))
    return pl.pallas_call(
        body, name=name, grid=(s // tm,), in_specs=[row(d), _whole((w, d))], out_specs=[row(w), row(d)],
        out_shape=[jax.ShapeDtypeStruct((s, w), F32), jax.ShapeDtypeStruct((s, d), BF16)],
        compiler_params=_cparams(("arbitrary",)),
    )(dh, w_out)


def _mix_in_bwd(h, dh, g, win_t, dq, dkp, dvp, dpc, tabs, name, tm, carry=None):
    s, d = h.shape
    nb = s // BLK
    nt = tm // BLK
    n_in = win_t.shape[0]

    def band_sum(n, before, own, after, prev_last, next_first):
        lo = (n > 0).astype(F32)
        hi = (n < s // tm - 1).astype(F32)
        blocks = []
        for b in range(nt):
            from_prev = prev_last[...] * lo if b == 0 else before[b - 1]
            from_next = next_first[...] * hi if b == nt - 1 else after[b + 1]
            blocks.append(from_prev + own[b] + from_next)
        return jnp.concatenate(blocks, axis=0)

    def body(h_ref, dh_ref, g_ref, w_ref, dq_ref, k2, k1, k0, kp, kn, v2, v1, v0, vp, vn, dpc_ref, c_ref, sa_ref,
             sb_ref, o_ref, du_ref, dg_ref, dhh_ref):
        n = pl.program_id(0)
        dk = band_sum(n, k2, k1, k0, kp, kn)
        dv = band_sum(n, v2, v1, v0, vp, vn)
        c, sa, sb = c_ref[...], sa_ref[...], sb_ref[...]
        du = jnp.concatenate([_rope_bwd(dq_ref[...], c, sa, sb, ATTN_W // 128), _rope_bwd(dk, c, sa, sb, 1), dv,
                              dpc_ref[...]], axis=1)
        du_ref[...] = du.T.astype(BF16)
        dn = _dot_nn(du.astype(BF16), w_ref[...])
        x = h_ref[...]
        r = lax.rsqrt(jnp.mean(x * x, axis=-1, keepdims=True) + EPS)
        xhat = x * r
        dxhat = dn * g_ref[...]
        dh_in = dh_ref[...] + r * (dxhat - xhat * jnp.mean(dxhat * xhat, axis=-1, keepdims=True))
        o_ref[...] = dh_in
        dhh_ref[...] = (0.5 * dh_in).astype(BF16)

        @pl.when(n == 0)
        def _():
            dg_ref[...] = jnp.zeros_like(dg_ref)

        dg_ref[...] += jnp.sum(dn * xhat, axis=0, keepdims=True)

    row = lambda w: pl.BlockSpec((tm, w), lambda n: (n, 0))
    slot = lambda t: pl.BlockSpec((nt, None, BLK, KV_W), lambda n, t=t: (n, t, 0, 0))
    parts = [slot(2), slot(1), slot(0),
             pl.BlockSpec((None, None, BLK, KV_W), lambda n: (jnp.maximum(nt * n - 1, 0), 2, 0, 0)),
             pl.BlockSpec((None, None, BLK, KV_W), lambda n: (jnp.minimum(nt * n + nt, nb - 1), 0, 0, 0))]
    return _call(
        body, name, (s // tm,),
        [row(d), row(d), _whole((1, d)), _whole((n_in, d)), row(ATTN_W), *parts, *parts, row(POOL_W),
         row(128), row(128), row(128)],
        [row(d), pl.BlockSpec((n_in, tm), lambda n: (0, n)), pl.BlockSpec((1, d), lambda n: (0, 0)), row(d)],
        [jax.ShapeDtypeStruct((s, d), F32), jax.ShapeDtypeStruct((n_in, s), BF16), jax.ShapeDtypeStruct((1, d), F32),
         jax.ShapeDtypeStruct((s, d), BF16)],
        (h, dh, g, win_t, dq, *[dkp] * 5, *[dvp] * 5, dpc, *tabs), ("arbitrary",), carry)


def _adam_math(w, g, m, v):
    m = ADAM_B1 * m + (1.0 - ADAM_B1) * g
    v = ADAM_B2 * v + (1.0 - ADAM_B2) * (g * g)
    m_hat = m / (1.0 - ADAM_B1 ** ADAM_STEP)
    v_hat = v / (1.0 - ADAM_B2 ** ADAM_STEP)
    delta = -ADAM_LR * (m_hat / (jnp.sqrt(v_hat) + ADAM_EPS) + ADAM_WD * w)
    return delta, m, v


def _adam_small(w, parts, late, m, v, name):
    rows, cols = w.shape

    def body(w_ref, p_ref, l_ref, m_ref, v_ref, g_ref, d_ref, nm_ref, nv_ref):
        g, gl = p_ref[0], l_ref[0]
        for k in range(1, N_DEV):
            g = g + p_ref[k]
            gl = gl + l_ref[k]
        g_ref[...] = g
        g_ref[SMALL_NORM1:SMALL_NORM1 + 8, :] = g[SMALL_NORM1:SMALL_NORM1 + 8] + gl
        d_ref[...], nm_ref[...], nv_ref[...] = _adam_math(w_ref[...], g_ref[...], m_ref[...], v_ref[...])

    return pl.pallas_call(
        body, name=name, out_shape=[jax.ShapeDtypeStruct((rows, cols), F32)] * 4,
    )(w, parts, late, m, v)


SMALL_NORM1 = 512


def _pack_small(norm1, normm, norm2, normf, sink, pool_w, pool_scale, loss=None):
    scale_rows = jnp.pad(pool_scale.reshape(4, 128), ((0, 4), (0, 0)))
    last_rows = jnp.pad(sink.reshape(1, N_HEADS), ((0, 7), (0, 128 - N_HEADS)))
    if loss is not None:
        last_rows = last_rows + jnp.pad(loss.reshape(1, 1), ((1, 6), (0, 127)))
    return jnp.concatenate([pool_w.reshape(512, 128), norm1.reshape(8, 128), normm.reshape(8, 128),
                            norm2.reshape(8, 128), normf.reshape(8, 128), scale_rows, last_rows], axis=0)


def _unpack_small(p):
    return dict(pool_w=p[:512].reshape(1, POOL_G, POOL_GW, POOL_GW), ffn1_norm=p[512:520].reshape(1, 1024),
                mix_norm=p[520:528].reshape(1, 1024), ffn2_norm=p[528:536].reshape(1, 1024),
                final_norm=p[536:544].reshape(1024), pool_scale=p[544:548].reshape(1, POOL_W),
                sink_logits=p[552, :N_HEADS].reshape(1, N_HEADS), loss=p[553, 0])


def kernel(x, ffn1_norm, ffn1_w_gate, ffn1_w_up, ffn1_w_down, mix_norm, w_in, sink_logits, pool_w, pool_scale, w_out, ffn2_norm, ffn2_w_gate, ffn2_w_up, ffn2_w_down, final_norm, loss_target, m_ffn1_norm, m_ffn1_w_gate, m_ffn1_w_up, m_ffn1_w_down, m_mix_norm, m_w_in, m_sink_logits, m_pool_w, m_pool_scale, m_w_out, m_ffn2_norm, m_ffn2_w_gate, m_ffn2_w_up, m_ffn2_w_down, m_final_norm, v_ffn1_norm, v_ffn1_w_gate, v_ffn1_w_up, v_ffn1_w_down, v_mix_norm, v_w_in, v_sink_logits, v_pool_w, v_pool_scale, v_w_out, v_ffn2_norm, v_ffn2_w_gate, v_ffn2_w_up, v_ffn2_w_down, v_final_norm):
    s, d = x.shape[1], x.shape[2]
    fk = ffn1_w_gate.shape[2]
    f = N_DEV * fk
    ink = w_in.shape[2]
    n_in = N_DEV * ink
    mixk = w_out.shape[1]
    tm = min(512, s)
    tm_bwd = min(256, s)
    pos = jnp.stack([lax.axis_index("x"), lax.axis_index("y"), lax.axis_index("c")]).astype(jnp.int32)

    t_bf = lambda w: w[0].T.astype(BF16)
    full = lambda a: a.reshape(N_DEV * a.shape[1], d)
    wg1, wu1 = map(full, _run_exchange(_AllGather([t_bf(ffn1_w_gate), t_bf(ffn1_w_up)]), "gather_ffn1"))
    second = _AllGather([ffn1_w_down[0].astype(BF16), t_bf(w_in), w_out[0].astype(BF16)])
    third = _AllGather([t_bf(ffn2_w_gate)])
    fourth = _AllGather([t_bf(ffn2_w_up), ffn2_w_down[0].astype(BF16)])

    tabs = _rope_tables(s)
    pband, ptband = _pool_tables(s)
    g1, gm, g2, gf = ffn1_norm, mix_norm, ffn2_norm, final_norm.reshape(1, d)

    x0 = x[0]
    n1, gate1, up1, act1_t, act1, *gathered = _ffn_fwd(x0, g1, wg1, wu1, None, "ffn1_gate_up", tm, carry=second)
    wd1, win_t, wout = map(full, gathered)
    h1, wg2 = _mix_out(x0, act1, wd1, "ffn1_down", tm, scale=0.5, carry=third)
    wg2 = full(wg2)
    q, k, v, pc, n2 = _mix_in(h1, gm, win_t, tabs, "mix_in", tm)
    mix, mix_t, *gathered = _attn_pool_fwd(q, k, v, pc, sink_logits, pool_w[0], pool_scale, pband, "attn_pool_fwd",
                                           carry=fourth)
    wu2, wd2 = map(full, gathered)
    (h2,) = _mix_out(h1, mix, wout, "mix_out", tm)
    dh3, n3, gate2, up2, act2_t, loss_part, dgf = _ffn_fwd(h2, g2, wg2, wu2, wd2, "ffn2_fwd", tm,
                                                           head=(gf, loss_target[0]))

    sum1, recv2 = {}, {}

    def stage2(keys):
        return _RsStage2([sum1[key][1] for key in keys])

    dh2, dg2, dgate2_t, dup2_t, dhh3 = _ffn_bwd(h2, dh3, gate2, up2, g2, wg2, wu2, wd2, "ffn2_bwd", tm_bwd)
    sum1["g2"] = _wgrad_rs1(pos, dgate2_t, n3, "wgrad_gate2")
    sum1["u2"] = _wgrad_rs1(pos, dup2_t, n3, "wgrad_up2")
    sum1["d2"] = _wgrad_rs1(pos, act2_t, dhh3, "wgrad_down2")
    dmix, dh2b = _mix_out_bwd(dh2, wout, "mix_out_bwd", tm)
    sum1["out"] = _wgrad_rs1(pos, mix_t, dh2b, "wgrad_out")
    dq, dkp, dvp, dpc, dsink, dpw, dps, *r2 = _attn_pool_bwd(
        q, k, v, pc, dmix, sink_logits, pool_w[0], pool_scale, pband, ptband, "attn_pool_bwd",
        carry=stage2(["g2", "u2"]))
    recv2.update(zip(["g2", "u2"], r2))
    dh1, du_t, dgm, dhh1, recv2["out"] = _mix_in_bwd(h1, dh2, gm, win_t, dq, dkp, dvp, dpc, tabs, "mix_in_bwd", tm,
                                                     carry=stage2(["out"]))
    sum1["in"] = _wgrad_rs1(pos, du_t, n2, "wgrad_in")
    sum1["d1"] = _wgrad_rs1(pos, act1_t, dhh1, "wgrad_down1")
    small_part = _pack_small(jnp.zeros_like(dgm), dgm, dg2, dgf, dsink[:, 0], dpw, dps, loss_part)
    dgate1, dup1, dgate1_t, dup1_t, _, recv2["in"], recv2["d1"], small_all = _ffn_bwd_gates(
        dh1, gate1, up1, wd1, "ffn1_bwd_gates", tm, carry=_Both(stage2(["in", "d1"]), _AllGather([small_part])))
    *sum1["g1"], recv2["d2"] = _wgrad_rs1(pos, dgate1_t, n1, "wgrad_gate1", carry=stage2(["d2"]))
    *sum1["u1"], recv2["g1"] = _wgrad_rs1(pos, dup1_t, n1, "wgrad_up1", carry=stage2(["g1"]))
    dx, dg1, recv2["u1"] = _ffn_bwd_input(x0, dh1, dgate1, dup1, g1, wg1, wu1, "ffn1_bwd_input", tm,
                                          carry=stage2(["u1"]))

    (dg1_all,) = _run_exchange(_DirectGather([dg1.reshape(8, 128)]), "gather_norm1_grad")
    pk = lambda a, b, c_, e, s_, pw_, psc: _pack_small(a, b, c_, e, s_[0], pw_[0], psc)
    small_w = pk(ffn1_norm, mix_norm, ffn2_norm, final_norm, sink_logits, pool_w, pool_scale)
    small_m = pk(m_ffn1_norm, m_mix_norm, m_ffn2_norm, m_final_norm, m_sink_logits, m_pool_w, m_pool_scale)
    small_v = pk(v_ffn1_norm, v_mix_norm, v_ffn2_norm, v_final_norm, v_sink_logits, v_pool_w, v_pool_scale)
    sg, sd, sm, sv = [_unpack_small(a)
                      for a in _adam_small(small_w, small_all, dg1_all, small_m, small_v, "adam_small")]

    big = {}
    keys = ["g1", "u1", "d1", "g2", "u2", "d2", "in", "out"]
    names = ["ffn1_w_gate", "ffn1_w_up", "ffn1_w_down", "ffn2_w_gate", "ffn2_w_up", "ffn2_w_down", "w_in", "w_out"]
    transposed = [True, True, False, True, True, False, True, False]
    ws = [ffn1_w_gate, ffn1_w_up, ffn1_w_down, ffn2_w_gate, ffn2_w_up, ffn2_w_down, w_in, w_out]
    ms = [m_ffn1_w_gate, m_ffn1_w_up, m_ffn1_w_down, m_ffn2_w_gate, m_ffn2_w_up, m_ffn2_w_down, m_w_in, m_w_out]
    vs = [v_ffn1_w_gate, v_ffn1_w_up, v_ffn1_w_down, v_ffn2_w_gate, v_ffn2_w_up, v_ffn2_w_down, v_w_in, v_w_out]
    for key, nm, tr, w, m, vv in zip(keys, names, transposed, ws, ms, vs):
        view = (lambda a: jnp.swapaxes(a, 1, 2)[0]) if tr else (lambda a: a[0])
        back = (lambda a: jnp.swapaxes(a[None], 1, 2)) if tr else (lambda a: a[None])
        res = _rs_sum2_adam(pos, sum1[key][0], recv2[key], view(w), view(m), view(vv), "adam_" + nm)
        big[nm] = tuple(back(a) for a in res)

    loss = sg["loss"]
    all_names = ["ffn1_norm", "ffn1_w_gate", "ffn1_w_up", "ffn1_w_down", "mix_norm", "w_in", "sink_logits", "pool_w",
                 "pool_scale", "w_out", "ffn2_norm", "ffn2_w_gate", "ffn2_w_up", "ffn2_w_down", "final_norm"]
    outs = [loss, dx[None]]
    for idx, src in enumerate((sg, sd, sm, sv)):
        for nm in all_names:
            outs.append(big[nm][idx] if nm in big else src[nm])
    return tuple(outs)
```

```python
import functools

import jax
import jax.numpy as jnp
import numpy as np
from jax import lax
from jax.experimental import pallas as pl
from jax.experimental.pallas import tpu as pltpu

F32 = jnp.float32
BF16 = jnp.bfloat16
MESH = pl.DeviceIdType.MESH
N_DEV = 8

EPS = 1e-6
HEAD_DIM = 64
N_HEADS = 8
N_KV = 2
GROUP = N_HEADS // N_KV
ATTN_W = N_HEADS * HEAD_DIM
KV_W = N_KV * HEAD_DIM
POOL_W = 512
POOL_G = 4
POOL_GW = POOL_W // POOL_G
POOL_WINDOWS = (2, 4, 8, 16)
BLK = 128
ROT = 16
ROPE_THETA = 500000.0
SCORE_SCALE = HEAD_DIM ** -0.5

ADAM_LR, ADAM_B1, ADAM_B2, ADAM_EPS, ADAM_WD, ADAM_STEP = 0.001, 0.9, 0.999, 1e-08, 0.01, 10

VMEM_LIMIT = 56 * 1024 * 1024


def _cparams(sem=None, **kw):
    if sem is not None:
        kw["dimension_semantics"] = sem
    return pltpu.CompilerParams(vmem_limit_bytes=VMEM_LIMIT, **kw)


def _whole(shape):
    nd = len(shape)
    return pl.BlockSpec(shape, lambda *_: (0,) * nd, pipeline_mode=pl.Buffered(1))


def _sigmoid(z):
    return 1.0 / (1.0 + jnp.exp(-z))


def _dot_nt(a, b):
    return lax.dot_general(a, b, (((1,), (1,)), ((), ())), preferred_element_type=F32)


def _dot_nn(a, b):
    return lax.dot_general(a, b, (((1,), (0,)), ((), ())), preferred_element_type=F32)


def _dot_tn(a, b):
    return lax.dot_general(a, b, (((0,), (0,)), ((), ())), preferred_element_type=F32)


def _rope_tables(s):
    inv_freq = ROPE_THETA ** (-np.arange(0, ROT, 2, dtype=np.float64) / ROT)
    ang = np.arange(s, dtype=np.float64)[:, None] * inv_freq[None, :]
    c = np.ones((s, HEAD_DIM)); sa = np.zeros((s, HEAD_DIM)); sb = np.zeros((s, HEAD_DIM))
    c[:, :8] = np.cos(ang); c[:, 8:16] = np.cos(ang)
    sa[:, :8] = -np.sin(ang)
    sb[:, 8:16] = np.sin(ang)
    t = lambda a: jnp.asarray(np.tile(a, (1, 2)).astype(np.float32))
    return t(c), t(sa), t(sb)


def _pool_weight(gi, t, s_pos, s):
    half = POOL_WINDOWS[gi] // 2

    def win(lo, hi):
        a = np.clip(lo, 0, s); b = np.clip(hi + 1, 0, s)
        inside = (s_pos >= a) & (s_pos < b)
        return inside / np.maximum(b - a, 1)

    w = 0.5 * (win(t - half, t + half - 1) + win(t - half + 1, t + half)) - (t == s_pos)
    return w * ((t >= 0) & (t < s) & (s_pos >= 0) & (s_pos < s))


def _pool_tables(s):
    nb = s // BLK
    fwd = np.zeros((3, POOL_G, BLK, 3 * BLK), np.float32)
    bwd = np.zeros((3, POOL_G, BLK, 3 * BLK), np.float32)
    for vi, n in enumerate((0, 1 if nb > 2 else 0, nb - 1)):
        i = n * BLK + np.arange(BLK)[:, None]
        j = (n - 1) * BLK + np.arange(3 * BLK)[None, :]
        for gi in range(POOL_G):
            fwd[vi, gi] = _pool_weight(gi, i, j, s)
            bwd[vi, gi] = _pool_weight(gi, j, i, s)
    return jnp.asarray(fwd, dtype=BF16), jnp.asarray(bwd, dtype=BF16)


def _variant_index(n, nb):
    return jnp.where(n == 0, 0, jnp.where(n == nb - 1, 2, 1))


class _Exchange:
    inputs = ()
    out_shapes = ()
    sems = ()

    def start(self, srcs, outs, sems):
        raise NotImplementedError

    def middle(self, srcs, outs, sems):
        pass

    def finish(self, srcs, outs, sems):
        raise NotImplementedError


class _AllGather(_Exchange):
    def __init__(self, arrays):
        n = len(arrays)
        self.inputs = list(arrays)
        self.out_shapes = [jax.ShapeDtypeStruct((N_DEV,) + a.shape, a.dtype) for a in arrays]
        self.sems = [pltpu.SemaphoreType.DMA((n, 8)), pltpu.SemaphoreType.DMA((n, 8)), pltpu.SemaphoreType.DMA((n,))]

    def _parts(self, srcs, outs, sems):
        send_sems, recv_sems, local_sems = sems
        n = len(srcs)
        x, y, c = lax.axis_index("x"), lax.axis_index("y"), lax.axis_index("c")
        me, sibling, xn, yn, diag = (x, y, c), (x, y, 1 - c), (1 - x, y, c), (x, 1 - y, c), (1 - x, 1 - y, c)

        def place(a, dev, half=None):
            block = outs[a].at[4 * dev[0] + 2 * dev[1] + dev[2]]
            if half is None:
                return block
            r2 = outs[a].shape[1] // 2
            return block.at[pl.ds(half * r2, r2)]

        def copy(a, k, dev, to, half=None, src=None):
            where = place(a, dev, half)
            return pltpu.make_async_remote_copy(
                src_ref=where if src is None else src, dst_ref=where, send_sem=send_sems.at[a, k],
                recv_sem=recv_sems.at[a, k], device_id=to, device_id_type=MESH)

        def other(dev):
            return (dev[0], dev[1], 1 - dev[2])

        class Parts:
            mine = staticmethod(lambda: [pltpu.make_async_copy(srcs[a], place(a, me), local_sems.at[a])
                                         for a in range(n)])
            own = staticmethod(lambda: [copy(a, k, me, to, src=srcs[a]) for a in range(n)
                                        for k, to in ((0, sibling), (1, xn), (2, yn))])
            relay = staticmethod(lambda a: [copy(a, 3, xn, yn, half=0), copy(a, 4, yn, xn, half=1),
                                            copy(a, 5, xn, sibling), copy(a, 6, yn, sibling)])
            last = staticmethod(lambda a: copy(a, 7, diag, sibling))
            from_x = staticmethod(lambda a: copy(a, 1, xn, me))
            from_y = staticmethod(lambda a: copy(a, 2, yn, me))
            diag_halves = staticmethod(lambda a: [copy(a, 3, diag, me, half=0), copy(a, 4, diag, me, half=1)])
            from_sibling = staticmethod(lambda a: [copy(a, 0, sibling, me), copy(a, 5, other(xn), me),
                                                   copy(a, 6, other(yn), me), copy(a, 7, other(diag), me)])

        return n, Parts

    def start(self, srcs, outs, sems):
        _, p = self._parts(srcs, outs, sems)
        for cp in p.mine() + p.own():
            cp.start()

    def middle(self, srcs, outs, sems):
        n, p = self._parts(srcs, outs, sems)
        for a in range(n):
            p.from_x(a).wait_recv()
            p.from_y(a).wait_recv()
            for cp in p.relay(a):
                cp.start()

    def finish(self, srcs, outs, sems):
        n, p = self._parts(srcs, outs, sems)
        for a in range(n):
            for cp in p.diag_halves(a):
                cp.wait_recv()
            p.last(a).start()
        for a in range(n):
            for cp in p.from_sibling(a):
                cp.wait_recv()
        for cp in p.own() + [cp for a in range(n) for cp in p.relay(a) + [p.last(a)]]:
            cp.wait_send()
        for cp in p.mine():
            cp.wait()


class _RsStage2(_Exchange):
    def start(self, srcs, outs, sems):
        for cp in self._copies(srcs, outs, sems):
            cp.start()

    def finish(self, srcs, outs, sems):
        copies = self._copies(srcs, outs, sems)
        for cp in copies:
            cp.wait_recv()
        for cp in copies:
            cp.wait_send()


    def __init__(self, pbs):
        n = len(pbs)
        self.inputs = list(pbs)
        self.out_shapes = [jax.ShapeDtypeStruct((3,) + p.shape[1:], p.dtype) for p in pbs]
        self.sems = [pltpu.SemaphoreType.DMA((n, 3)), pltpu.SemaphoreType.DMA((n, 3))]

    def _copies(self, srcs, outs, sems):
        send_sems, recv_sems = sems
        x, y, c = lax.axis_index("x"), lax.axis_index("y"), lax.axis_index("c")
        chips = [(1 - x, y), (x, 1 - y), (1 - x, 1 - y)]
        return [pltpu.make_async_remote_copy(
            src_ref=srcs[a].at[2 * chip[0] + chip[1]], dst_ref=outs[a].at[j], send_sem=send_sems.at[a, j],
            recv_sem=recv_sems.at[a, j], device_id=(*chip, c), device_id_type=MESH)
            for a in range(len(srcs)) for j, chip in enumerate(chips)]


class _DirectGather(_Exchange):
    def __init__(self, arrays):
        n = len(arrays)
        self.inputs = list(arrays)
        self.out_shapes = [jax.ShapeDtypeStruct((N_DEV,) + a.shape, a.dtype) for a in arrays]
        self.sems = [pltpu.SemaphoreType.DMA((n, 7)), pltpu.SemaphoreType.DMA((n, 7)), pltpu.SemaphoreType.DMA((n,))]

    def _copies(self, srcs, outs, sems):
        send_sems, recv_sems, local_sems = sems
        x, y, c = lax.axis_index("x"), lax.axis_index("y"), lax.axis_index("c")
        me = 4 * x + 2 * y + c
        remote, local = [], []
        for a in range(len(srcs)):
            local.append(pltpu.make_async_copy(srcs[a], outs[a].at[me], local_sems.at[a]))
            for k in range(1, N_DEV):
                peer = (x ^ (k >> 2), y ^ ((k >> 1) & 1), c ^ (k & 1))
                remote.append(pltpu.make_async_remote_copy(
                    src_ref=srcs[a], dst_ref=outs[a].at[me], send_sem=send_sems.at[a, k - 1],
                    recv_sem=recv_sems.at[a, k - 1], device_id=peer, device_id_type=MESH))
        return remote, local

    def start(self, srcs, outs, sems):
        remote, local = self._copies(srcs, outs, sems)
        for cp in local + remote:
            cp.start()

    def finish(self, srcs, outs, sems):
        remote, local = self._copies(srcs, outs, sems)
        for cp in remote:
            cp.wait_recv()
        for cp in remote:
            cp.wait_send()
        for cp in local:
            cp.wait()


class _Both(_Exchange):
    def __init__(self, a, b):
        self.a, self.b = a, b
        self.inputs = list(a.inputs) + list(b.inputs)
        self.out_shapes = list(a.out_shapes) + list(b.out_shapes)
        self.sems = list(a.sems) + list(b.sems)

    def _split(self, srcs, outs, sems):
        na, oa, sa = len(self.a.inputs), len(self.a.out_shapes), len(self.a.sems)
        return (srcs[:na], outs[:oa], sems[:sa]), (srcs[na:], outs[oa:], sems[sa:])

    def start(self, srcs, outs, sems):
        pa, pb = self._split(srcs, outs, sems)
        self.a.start(*pa)
        self.b.start(*pb)

    def middle(self, srcs, outs, sems):
        pa, pb = self._split(srcs, outs, sems)
        self.a.middle(*pa)
        self.b.middle(*pb)

    def finish(self, srcs, outs, sems):
        pa, pb = self._split(srcs, outs, sems)
        self.a.finish(*pa)
        self.b.finish(*pb)


_ANY = pl.BlockSpec(memory_space=pl.ANY)


def _run_exchange(ex, name):
    n_in, n_out = len(ex.inputs), len(ex.out_shapes)

    def body(*refs):
        srcs, outs, sems = refs[:n_in], refs[n_in:n_in + n_out], refs[n_in + n_out:]
        ex.start(srcs, outs, sems)
        ex.middle(srcs, outs, sems)
        ex.finish(srcs, outs, sems)

    return pl.pallas_call(
        body, name=name, out_shape=list(ex.out_shapes), in_specs=[_ANY] * n_in, out_specs=[_ANY] * n_out,
        scratch_shapes=list(ex.sems),
    )(*ex.inputs)


CARRY_MIDDLE = 0.7


def _call(body, name, grid, in_specs, out_specs, out_shape, args, sem, carry=None, middle=CARRY_MIDDLE, scratch=()):
    if carry is None:
        return pl.pallas_call(functools.partial(body), name=name, grid=grid, in_specs=in_specs, out_specs=out_specs,
                              out_shape=out_shape, scratch_shapes=list(scratch), compiler_params=_cparams(sem))(*args)
    n_in, n_out = len(in_specs), len(out_specs)
    nc_in, nc_out = len(carry.inputs), len(carry.out_shapes)

    def carried(*refs):
        ins = refs[:n_in]
        c_in = refs[n_in:n_in + nc_in]
        outs = refs[n_in + nc_in:n_in + nc_in + n_out]
        c_out = refs[n_in + nc_in + n_out:n_in + nc_in + n_out + nc_out]
        own = refs[n_in + nc_in + n_out + nc_out:n_in + nc_in + n_out + nc_out + len(scratch)]
        sems = refs[n_in + nc_in + n_out + nc_out + len(scratch):]
        ids = [pl.program_id(i) for i in range(len(grid))]
        is_first = functools.reduce(jnp.logical_and, [i == 0 for i in ids])
        is_last = functools.reduce(jnp.logical_and, [i == g - 1 for i, g in zip(ids, grid)])
        @pl.when(is_first)
        def _():
            carry.start(c_in, c_out, sems)

        if middle is not None:
            @pl.when(functools.reduce(jnp.logical_and, [ids[0] == round(middle * (grid[0] - 1))]
                                      + [i == 0 for i in ids[1:]]))
            def _():
                carry.middle(c_in, c_out, sems)

        body(*ins, *outs, *own)

        @pl.when(is_last)
        def _():
            if middle is None:
                carry.middle(c_in, c_out, sems)
            carry.finish(c_in, c_out, sems)

    return pl.pallas_call(
        carried, name=name, grid=grid, in_specs=list(in_specs) + [_ANY] * nc_in,
        out_specs=list(out_specs) + [_ANY] * nc_out, out_shape=list(out_shape) + list(carry.out_shapes),
        scratch_shapes=list(scratch) + list(carry.sems), compiler_params=_cparams(sem))(*args, *carry.inputs)


def _rs_stage1_sum(pos, gs, gbs, name):
    n = len(gs)
    _, rows, d = gs[0].shape

    def block(pc, q, other_core):
        return 4 * (q // 2) + 2 * (q % 2) + ((1 - pc) if other_core else pc)

    def body(pos_ref, *refs):
        g_refs, gb_refs = refs[:n], refs[n:2 * n]
        p_refs, pb_refs = refs[2 * n:3 * n], refs[3 * n:4 * n]
        land, send_sems, recv_sems = refs[4 * n:]
        q = pl.program_id(0)
        x, y, c = lax.axis_index("x"), lax.axis_index("y"), lax.axis_index("c")

        def copy(a, k):
            return pltpu.make_async_remote_copy(
                src_ref=gb_refs[a].at[block(c, k, True)], dst_ref=land.at[a, k],
                send_sem=send_sems.at[a, k], recv_sem=recv_sems.at[a, k], device_id=(x, y, 1 - c),
                device_id_type=MESH)

        @pl.when(q == 0)
        def _():
            for k in range(4):
                for a in range(n):
                    copy(a, k).start()

        for a in range(n):
            copy(a, q).wait_recv()
            p = g_refs[a][...] + land[a, q].astype(F32)
            p_refs[a][...] = p
            pb_refs[a][...] = p.astype(BF16)

        @pl.when(q == 3)
        def _():
            for k in range(4):
                for a in range(n):
                    copy(a, k).wait_send()

    blk = pl.BlockSpec((None, rows, d), lambda q, pos_ref: (q, 0, 0))
    mine = pl.BlockSpec((None, rows, d), lambda q, pos_ref: (block(pos_ref[2], q, False), 0, 0))
    res = pl.pallas_call(
        body, name=name,
        grid_spec=pltpu.PrefetchScalarGridSpec(
            num_scalar_prefetch=1, grid=(4,),
            in_specs=[mine] * n + [_ANY] * n, out_specs=[blk] * (2 * n),
            scratch_shapes=[pltpu.VMEM((n, 4, rows, d), BF16), pltpu.SemaphoreType.DMA((n, 4)),
                            pltpu.SemaphoreType.DMA((n, 4))]),
        out_shape=[jax.ShapeDtypeStruct((4, rows, d), F32)] * n + [jax.ShapeDtypeStruct((4, rows, d), BF16)] * n,
        compiler_params=_cparams(("arbitrary",)),
    )(pos, *gs, *gbs)
    return list(zip(res[:n], res[n:]))


def _rs_sum2_adam(pos, p, r2, w, m, v, name):
    _, rows, d = p.shape
    tr = rows // 2 if rows % 16 == 0 else rows

    def body(pos_ref, p_ref, r_ref, w_ref, m_ref, v_ref, g_ref, d_ref, nm_ref, nv_ref):
        r = r_ref[...].astype(F32)
        g = ((p_ref[...] + r[0]) + r[1]) + r[2]
        g_ref[...] = g
        d_ref[...], nm_ref[...], nv_ref[...] = _adam_math(w_ref[...], g, m_ref[...], v_ref[...])

    blk = pl.BlockSpec((tr, d), lambda i, pos_ref: (i, 0))
    return pl.pallas_call(
        body, name=name,
        grid_spec=pltpu.PrefetchScalarGridSpec(
            num_scalar_prefetch=1, grid=(rows // tr,),
            in_specs=[pl.BlockSpec((None, tr, d), lambda i, pos_ref: (2 * pos_ref[0] + pos_ref[1], i, 0)),
                      pl.BlockSpec((3, tr, d), lambda i, pos_ref: (0, i, 0)), blk, blk, blk],
            out_specs=[blk] * 4),
        out_shape=[jax.ShapeDtypeStruct((rows, d), F32)] * 4,
        compiler_params=_cparams(("arbitrary",)),
    )(pos, p, r2, w, m, v)


def _ffn_chunk(f):
    for cand in (256, 128):
        if f % cand == 0:
            return cand
    return f


def _loss_head(x, gg, target, loss_ref, dg_ref):
    @pl.when(pl.program_id(0) == 0)
    def _():
        loss_ref[...] = jnp.zeros_like(loss_ref)
        dg_ref[...] = jnp.zeros_like(dg_ref)

    r = lax.rsqrt(jnp.mean(x * x, axis=-1, keepdims=True) + EPS)
    xhat = x * r
    e = xhat * gg - target
    loss_ref[...] += 0.5 * jnp.sum(jnp.mean(e * e, axis=-1, keepdims=True), axis=0, keepdims=True)
    dy = e * (1.0 / x.shape[-1])
    dg_ref[...] += jnp.sum(dy * xhat, axis=0, keepdims=True)
    dxhat = dy * gg
    return r * (dxhat - xhat * jnp.mean(dxhat * xhat, axis=-1, keepdims=True))


def _ffn_gate(h, g, wg_t, name, tm, carry=None):
    s, d = h.shape
    f = wg_t.shape[0]

    def body(h_ref, g_ref, wg_ref, n_ref, gate_ref):
        x = h_ref[...]
        r = lax.rsqrt(jnp.mean(x * x, axis=-1, keepdims=True) + EPS)
        nb = (x * r * g_ref[...]).astype(BF16)
        n_ref[...] = nb
        gate_ref[...] = _dot_nt(nb, wg_ref[...]).astype(BF16)

    row = lambda w: pl.BlockSpec((tm, w), lambda i: (i, 0))
    return _call(body, name, (s // tm,), [row(d), _whole((1, d)), _whole((f, d))], [row(d), row(f)],
                 [jax.ShapeDtypeStruct((s, d), BF16), jax.ShapeDtypeStruct((s, f), BF16)], (h, g, wg_t),
                 ("arbitrary",), carry)


def _ffn_up(n, gate, wu_t, name, tm, carry=None):
    s, d = n.shape
    f = wu_t.shape[0]
    tf = _ffn_chunk(f)

    def body(n_ref, gate_ref, wu_ref, up_ref, act_ref, act_t_ref):
        nb = n_ref[...]
        for j in range(f // tf):
            sl = slice(j * tf, (j + 1) * tf)
            up = _dot_nt(nb, wu_ref[sl, :])
            gate = gate_ref[:, sl].astype(F32)
            up_ref[:, sl] = up.astype(BF16)
            act = (gate * _sigmoid(gate) * up).astype(BF16)
            act_ref[:, sl] = act
            act_t_ref[sl, :] = act.T

    row = lambda w: pl.BlockSpec((tm, w), lambda i: (i, 0))
    return _call(body, name, (s // tm,), [row(d), row(f), _whole((f, d))],
                 [row(f), row(f), pl.BlockSpec((f, tm), lambda i: (0, i))],
                 [jax.ShapeDtypeStruct((s, f), BF16)] * 2 + [jax.ShapeDtypeStruct((f, s), BF16)], (n, gate, wu_t),
                 ("arbitrary",), carry)


def _ffn_fwd(h, g, wg_t, wu_t, wd, name, tm, carry=None, head=None, middle=CARRY_MIDDLE):
    s, d = h.shape
    f = wg_t.shape[0]
    tf = _ffn_chunk(f)

    def body(h_ref, g_ref, wg_ref, wu_ref, wd_ref, *refs):
        if head is None:
            o_ref, n_ref, gate_ref, up_ref, act_t_ref, act_ref = refs
        else:
            gf_ref, t_ref, o_ref, n_ref, gate_ref, up_ref, act_t_ref, loss_ref, dgf_ref, act_ref = refs
        x = h_ref[...]
        r = lax.rsqrt(jnp.mean(x * x, axis=-1, keepdims=True) + EPS)
        nb = (x * r * g_ref[...]).astype(BF16)
        n_ref[...] = nb
        for j in range(f // tf):
            sl = slice(j * tf, (j + 1) * tf)
            gate = _dot_nt(nb, wg_ref[sl, :])
            up = _dot_nt(nb, wu_ref[sl, :])
            gate_ref[:, sl] = gate.astype(BF16)
            up_ref[:, sl] = up.astype(BF16)
            act = gate * _sigmoid(gate) * up
            act = act.astype(BF16)
            act_ref[:, sl] = act
            act_t_ref[sl, :] = act.T
        h_out = x + 0.5 * _dot_nn(act_ref[...], wd_ref[...])
        o_ref[...] = h_out if head is None else _loss_head(h_out, gf_ref[...], t_ref[...], loss_ref, dgf_ref)

    row = lambda w: pl.BlockSpec((tm, w), lambda i: (i, 0))
    in_specs = [row(d), _whole((1, d)), _whole((f, d)), _whole((f, d)), _whole((f, d))]
    out_specs = [row(d), row(d), row(f), row(f), pl.BlockSpec((f, tm), lambda i: (0, i))]
    out_shape = ([jax.ShapeDtypeStruct((s, d), F32), jax.ShapeDtypeStruct((s, d), BF16)]
                 + [jax.ShapeDtypeStruct((s, f), BF16)] * 2 + [jax.ShapeDtypeStruct((f, s), BF16)])
    args = (h, g, wg_t, wu_t, wd)
    scratch = (pltpu.VMEM((tm, f), BF16),)
    if head is not None:
        in_specs += [_whole((1, d)), row(d)]
        out_specs += [pl.BlockSpec((1, 1), lambda i: (0, 0)), pl.BlockSpec((1, d), lambda i: (0, 0))]
        out_shape += [jax.ShapeDtypeStruct((1, 1), F32), jax.ShapeDtypeStruct((1, d), F32)]
        args += tuple(head)
    return _call(body, name, (s // tm,), in_specs, out_specs, out_shape, args, ("arbitrary",), carry, middle, scratch)


def _gate_grads(dh_ref, gate_ref, up_ref, wd_ref, dgate_ref, dup_ref, dgate_t_ref, dup_t_ref, dhh_ref, tf):
    dhh = (0.5 * dh_ref[...]).astype(BF16)
    dhh_ref[...] = dhh
    for j in range(gate_ref.shape[1] // tf):
        sl = slice(j * tf, (j + 1) * tf)
        gt = gate_ref[:, sl].astype(F32)
        u = up_ref[:, sl].astype(F32)
        dact = _dot_nt(dhh, wd_ref[sl, :])
        sg = _sigmoid(gt)
        dup = dact * (gt * sg)
        dgate = dact * u * (sg * (1.0 + gt * (1.0 - sg)))
        dup, dgate = dup.astype(BF16), dgate.astype(BF16)
        dup_ref[:, sl] = dup
        dgate_ref[:, sl] = dgate
        dup_t_ref[sl, :] = dup.T
        dgate_t_ref[sl, :] = dgate.T


def _input_grad(h_ref, dh_ref, dgate_ref, dup_ref, g_ref, wg_ref, wu_ref, o_ref, dg_ref):
    x = h_ref[...]
    r = lax.rsqrt(jnp.mean(x * x, axis=-1, keepdims=True) + EPS)
    xhat = x * r
    dn = _dot_nn(dgate_ref[...], wg_ref[...]) + _dot_nn(dup_ref[...], wu_ref[...])
    dxhat = dn * g_ref[...]
    o_ref[...] = dh_ref[...] + r * (dxhat - xhat * jnp.mean(dxhat * xhat, axis=-1, keepdims=True))

    @pl.when(pl.program_id(0) == 0)
    def _():
        dg_ref[...] = jnp.zeros_like(dg_ref)

    dg_ref[...] += jnp.sum(dn * xhat, axis=0, keepdims=True)


def _ffn_bwd(h_in, dh_out, gate, up, g, wg_t, wu_t, wd, name, tm):
    s, d = h_in.shape
    f = gate.shape[1]
    tf = _ffn_chunk(f)

    def body(h_ref, dh_ref, gate_ref, up_ref, g_ref, wg_ref, wu_ref, wd_ref,
             o_ref, dg_ref, dgate_t_ref, dup_t_ref, dhh_ref, dgate_ref, dup_ref):
        _gate_grads(dh_ref, gate_ref, up_ref, wd_ref, dgate_ref, dup_ref, dgate_t_ref, dup_t_ref, dhh_ref, tf)
        _input_grad(h_ref, dh_ref, dgate_ref, dup_ref, g_ref, wg_ref, wu_ref, o_ref, dg_ref)

    row = lambda w: pl.BlockSpec((tm, w), lambda i: (i, 0))
    col = pl.BlockSpec((f, tm), lambda i: (0, i))
    return pl.pallas_call(
        body, name=name, grid=(s // tm,),
        in_specs=[row(d), row(d), row(f), row(f), _whole((1, d)), _whole((f, d)), _whole((f, d)), _whole((f, d))],
        out_specs=[row(d), pl.BlockSpec((1, d), lambda i: (0, 0)), col, col, row(d)],
        out_shape=[jax.ShapeDtypeStruct((s, d), F32), jax.ShapeDtypeStruct((1, d), F32),
                   jax.ShapeDtypeStruct((f, s), BF16), jax.ShapeDtypeStruct((f, s), BF16),
                   jax.ShapeDtypeStruct((s, d), BF16)],
        scratch_shapes=[pltpu.VMEM((tm, f), BF16), pltpu.VMEM((tm, f), BF16)],
        compiler_params=_cparams(("arbitrary",)),
    )(h_in, dh_out, gate, up, g, wg_t, wu_t, wd)


def _ffn_bwd_gates(dh_out, gate, up, wd, name, tm, carry=None):
    s, d = dh_out.shape
    f = gate.shape[1]
    tf = _ffn_chunk(f)

    def body(dh_ref, gate_ref, up_ref, wd_ref, dgate_ref, dup_ref, dgate_t_ref, dup_t_ref, dhh_ref):
        _gate_grads(dh_ref, gate_ref, up_ref, wd_ref, dgate_ref, dup_ref, dgate_t_ref, dup_t_ref, dhh_ref, tf)

    row = lambda w: pl.BlockSpec((tm, w), lambda i: (i, 0))
    col = pl.BlockSpec((f, tm), lambda i: (0, i))
    return _call(
        body, name, (s // tm,), [row(d), row(f), row(f), _whole((f, d))], [row(f), row(f), col, col, row(d)],
        [jax.ShapeDtypeStruct((s, f), BF16)] * 2 + [jax.ShapeDtypeStruct((f, s), BF16)] * 2
        + [jax.ShapeDtypeStruct((s, d), BF16)],
        (dh_out, gate, up, wd), ("arbitrary",), carry)


def _ffn_bwd_input(h_in, dh_out, dgate, dup, g, wg_t, wu_t, name, tm, carry=None):
    s, d = h_in.shape
    f = dgate.shape[1]

    row = lambda w: pl.BlockSpec((tm, w), lambda i: (i, 0))
    return _call(
        _input_grad, name, (s // tm,),
        [row(d), row(d), row(f), row(f), _whole((1, d)), _whole((f, d)), _whole((f, d))],
        [row(d), pl.BlockSpec((1, d), lambda i: (0, 0))],
        [jax.ShapeDtypeStruct((s, d), F32), jax.ShapeDtypeStruct((1, d), F32)],
        (h_in, dh_out, dgate, dup, g, wg_t, wu_t), ("arbitrary",), carry)


def _wgrad_rs1(pos, a_t, b, name, carry=None):
    f, s = a_t.shape
    d = b.shape[1]
    fk = f // N_DEV
    nc_in = 0 if carry is None else len(carry.inputs)
    nc_out = 0 if carry is None else len(carry.out_shapes)

    def body(pos_ref, a_ref, b_ref, *refs):
        c_in = refs[:nc_in]
        p_ref, pb_ref = refs[nc_in:nc_in + 2]
        c_out = refs[nc_in + 2:nc_in + 2 + nc_out]
        stage, land, send_sems, recv_sems = refs[nc_in + 2 + nc_out:nc_in + 6 + nc_out]
        c_sems = refs[nc_in + 6 + nc_out:]
        t = pl.program_id(0)
        q = t % 4
        x, y, c = lax.axis_index("x"), lax.axis_index("y"), lax.axis_index("c")

        def push(k):
            return pltpu.make_async_remote_copy(src_ref=stage.at[k], dst_ref=land.at[k], send_sem=send_sems.at[k],
                                                recv_sem=recv_sems.at[k], device_id=(x, y, 1 - c), device_id_type=MESH)

        if carry is not None:
            @pl.when(t == 0)
            def _():
                carry.start(c_in, c_out, c_sems)

        g = _dot_nn(a_ref[...], b_ref[...])

        @pl.when(t < 4)
        def _():
            stage[q] = g.astype(BF16)
            push(q).start()

        @pl.when(t >= 4)
        def _():
            push(q).wait_recv()
            p = g + land[q].astype(F32)
            p_ref[...] = p
            pb_ref[...] = p.astype(BF16)

        @pl.when(t == 7)
        def _():
            for k in range(4):
                push(k).wait_send()
            if carry is not None:
                carry.middle(c_in, c_out, c_sems)
                carry.finish(c_in, c_out, c_sems)

    def shard(t, pos_ref):
        return 4 * ((t % 4) // 2) + 2 * (t % 2) + jnp.where(t < 4, 1 - pos_ref[2], pos_ref[2])

    out = pl.BlockSpec((None, fk, d), lambda t, pos_ref: (jnp.maximum(t - 4, 0), 0, 0))
    return pl.pallas_call(
        body, name=name,
        grid_spec=pltpu.PrefetchScalarGridSpec(
            num_scalar_prefetch=1, grid=(8,),
            in_specs=[pl.BlockSpec((fk, s), lambda t, pos_ref: (shard(t, pos_ref), 0)),
                      pl.BlockSpec((s, d), lambda t, pos_ref: (0, 0), pipeline_mode=pl.Buffered(1))]
            + [_ANY] * nc_in,
            out_specs=[out, out] + [_ANY] * nc_out,
            scratch_shapes=[pltpu.VMEM((4, fk, d), BF16), pltpu.VMEM((4, fk, d), BF16),
                            pltpu.SemaphoreType.DMA((4,)), pltpu.SemaphoreType.DMA((4,))]
            + ([] if carry is None else list(carry.sems))),
        out_shape=[jax.ShapeDtypeStruct((4, fk, d), F32), jax.ShapeDtypeStruct((4, fk, d), BF16)]
        + ([] if carry is None else list(carry.out_shapes)),
        compiler_params=_cparams(("arbitrary",)),
    )(pos, a_t, b, *([] if carry is None else carry.inputs))


def _rope(t, c, sa, sb, reps):
    c, sa, sb = (jnp.tile(v, (1, reps)) if reps > 1 else v for v in (c, sa, sb))
    w = t.shape[1]
    return t * c + pltpu.roll(t, w - 8, 1) * sa + pltpu.roll(t, 8, 1) * sb


def _rope_bwd(dt, c, sa, sb, reps):
    c, sa, sb = (jnp.tile(v, (1, reps)) if reps > 1 else v for v in (c, sa, sb))
    w = dt.shape[1]
    return dt * c + pltpu.roll(dt * sa, 8, 1) + pltpu.roll(dt * sb, w - 8, 1)


def _mix_in(h, g, win_t, tabs, name, tm, carry=None):
    s, d = h.shape
    n_in = win_t.shape[0]

    def body(h_ref, g_ref, w_ref, c_ref, sa_ref, sb_ref, q_ref, k_ref, v_ref, pc_ref, n_ref):
        x = h_ref[...]
        r = lax.rsqrt(jnp.mean(x * x, axis=-1, keepdims=True) + EPS)
        nb = (x * r * g_ref[...]).astype(BF16)
        n_ref[...] = nb
        u = _dot_nt(nb, w_ref[...])
        c, sa, sb = c_ref[...], sa_ref[...], sb_ref[...]
        q_ref[...] = _rope(u[:, :ATTN_W], c, sa, sb, ATTN_W // 128).astype(BF16)
        k_ref[...] = _rope(u[:, ATTN_W:ATTN_W + KV_W], c, sa, sb, 1).astype(BF16)
        v_ref[...] = u[:, ATTN_W + KV_W:ATTN_W + 2 * KV_W].astype(BF16)
        pc_ref[...] = u[:, ATTN_W + 2 * KV_W:]

    row = lambda w: pl.BlockSpec((tm, w), lambda i: (i, 0))
    return _call(
        body, name, (s // tm,),
        [row(d), _whole((1, d)), _whole((n_in, d)), row(128), row(128), row(128)],
        [row(ATTN_W), row(KV_W), row(KV_W), row(POOL_W), row(d)],
        [jax.ShapeDtypeStruct((s, ATTN_W), BF16), jax.ShapeDtypeStruct((s, KV_W), BF16),
         jax.ShapeDtypeStruct((s, KV_W), BF16), jax.ShapeDtypeStruct((s, POOL_W), F32),
         jax.ShapeDtypeStruct((s, d), BF16)],
        (h, g, win_t, *tabs), ("arbitrary",), carry)


def _band_mask(n, nb, transposed):
    shape = (3 * BLK, 2 * BLK) if transposed else (2 * BLK, 3 * BLK)
    i = lax.broadcasted_iota(jnp.int32, shape, 1 if transposed else 0) % BLK
    j = lax.broadcasted_iota(jnp.int32, shape, 0 if transposed else 1)
    kpos = (n - 1) * BLK + j
    return (j >= i) & (j <= i + 2 * BLK) & (kpos >= 0) & (kpos < nb * BLK)


def _block_diag(t, kh):
    tf = t.astype(F32)
    tr = pltpu.roll(tf, HEAD_DIM, 1)
    lo = lax.broadcasted_iota(jnp.int32, tf.shape, 1) < HEAD_DIM
    top, bot = (tf, tr) if kh == 0 else (tr, tf)
    return jnp.concatenate([jnp.where(lo, top, 0.0), jnp.where(lo, 0.0, bot)], axis=0).astype(BF16)


def _fold_diag(tbd):
    lo = lax.broadcasted_iota(jnp.int32, (3 * BLK, 2 * HEAD_DIM), 1) < HEAD_DIM
    t = jnp.where(lo, tbd[:3 * BLK], tbd[3 * BLK:])
    return t + pltpu.roll(t, HEAD_DIM, 1)


def _stack_pairs(x, kh):
    return jnp.concatenate([x[:, (2 * kh) * 128:(2 * kh + 1) * 128], x[:, (2 * kh + 1) * 128:(2 * kh + 2) * 128]], axis=0)


def _sink_of(sink_ref, kh, half, axis):
    shape = (2 * BLK, 1) if axis == 0 else (1, 2 * BLK)
    first = lax.broadcasted_iota(jnp.int32, shape, axis) < BLK
    return jnp.where(first, sink_ref[0, GROUP * kh + half], sink_ref[0, GROUP * kh + 2 + half])


def _softmax_sink(sc, valid, sink, axis):
    sc = jnp.where(valid, sc, -1e30)
    m = jnp.maximum(jnp.max(sc, axis=axis, keepdims=True), sink)
    e = jnp.exp(sc - m)
    es = jnp.exp(sink - m)
    inv = 1.0 / (jnp.sum(e, axis=axis, keepdims=True) + es)
    return e * inv, es * inv


def _attn_blocks_per_step(nb):
    return next(nq for nq in (4, 2, 1) if nb % nq == 0)


def _band_specs(nq, nb, w, col=0):
    return [pl.BlockSpec((BLK, w), lambda m: (jnp.maximum(nq * m - 1, 0), col)),
            pl.BlockSpec((nq * BLK, w), lambda m: (m, col)),
            pl.BlockSpec((BLK, w), lambda m: (jnp.minimum(nq * m + nq, nb - 1), col))]


def _attn_pool_fwd(q, k, v, pc, sink, pool_w, pool_scale, pband, name, carry=None, middle=CARRY_MIDDLE):
    s = q.shape[0]
    nb = s // BLK

    nq = _attn_blocks_per_step(nb)

    def body(sink_ref, q_ref, k0, k1, k2, v0, v1, v2, p0, p1, p2, pw_ref, ps_ref, pb_ref, o_ref, o_t_ref):
        kall = jnp.concatenate([k0[...], k1[...], k2[...]], axis=0)
        vall = jnp.concatenate([v0[...], v1[...], v2[...]], axis=0)
        pall = jnp.concatenate([p0[...], p1[...], p2[...]], axis=0).astype(BF16)
        qall = q_ref[...] * SCORE_SCALE
        for j in range(nq):
            n = pl.program_id(0) * nq + j
            rows, band = slice(j * BLK, (j + 1) * BLK), slice(j * BLK, (j + 3) * BLK)
            valid = _band_mask(n, nb, False)
            kb, vb, qs = kall[band], vall[band], qall[rows]
            for kh in range(N_KV):
                sc = _dot_nt(_stack_pairs(qs, kh), _block_diag(kb, kh))
                p = [_softmax_sink(sc[:, half * 3 * BLK:(half + 1) * 3 * BLK], valid,
                                   _sink_of(sink_ref, kh, half, 0), 1)[0] for half in range(2)]
                o2 = _dot_nn(jnp.concatenate(p, axis=1).astype(BF16), _block_diag(vb, kh)).astype(BF16)
                o_ref[rows, (2 * kh) * 128:(2 * kh + 1) * 128] = o2[:BLK]
                o_ref[rows, (2 * kh + 1) * 128:(2 * kh + 2) * 128] = o2[BLK:]
            ext = pall[band]
            var = _variant_index(n, nb)
            for gi in range(POOL_G):
                gsl = slice(gi * POOL_GW, (gi + 1) * POOL_GW)
                dg = _dot_nn(pb_ref[var, gi], ext[:, gsl])
                yg = _dot_nn(dg.astype(BF16), pw_ref[gi].astype(BF16))
                o_ref[rows, ATTN_W + gi * POOL_GW:ATTN_W + (gi + 1) * POOL_GW] = (yg * ps_ref[:, gsl]).astype(BF16)
        o_t_ref[...] = o_ref[...].astype(F32).T.astype(BF16)

    return _call(
        body, name, (nb // nq,),
        [pl.BlockSpec(memory_space=pltpu.SMEM), pl.BlockSpec((nq * BLK, ATTN_W), lambda m: (m, 0)),
         *_band_specs(nq, nb, KV_W), *_band_specs(nq, nb, KV_W), *_band_specs(nq, nb, POOL_W),
         _whole((POOL_G, POOL_GW, POOL_GW)), _whole((1, POOL_W)), _whole(pband.shape)],
        [pl.BlockSpec((nq * BLK, ATTN_W + POOL_W), lambda m: (m, 0)),
         pl.BlockSpec((ATTN_W + POOL_W, nq * BLK), lambda m: (0, m))],
        [jax.ShapeDtypeStruct((s, ATTN_W + POOL_W), BF16), jax.ShapeDtypeStruct((ATTN_W + POOL_W, s), BF16)],
        (sink, q, k, k, k, v, v, v, pc, pc, pc, pool_w, pool_scale, pband), ("arbitrary",), carry, middle)


def _attn_pool_bwd(q, k, v, pc, dmix, sink, pool_w, pool_scale, pband, ptband, name, carry=None):
    s = q.shape[0]
    nb = s // BLK
    nq = _attn_blocks_per_step(nb)

    def body(sink_ref, q_ref, k0, k1, k2, v0, v1, v2, p0, p1, p2, da_ref, d0, d1, d2, pw_ref, ps_ref, pb_ref, ptb_ref,
             dq_ref, dk_ref, dv_ref, dpc_ref, dsink_ref, dpw_ref, dps_ref):
        @pl.when(pl.program_id(0) == 0)
        def _():
            dsink_ref[...] = jnp.zeros_like(dsink_ref)
            dpw_ref[...] = jnp.zeros_like(dpw_ref)
            dps_ref[...] = jnp.zeros_like(dps_ref)

        kall = jnp.concatenate([k0[...], k1[...], k2[...]], axis=0)
        vall = jnp.concatenate([v0[...], v1[...], v2[...]], axis=0)
        pall = jnp.concatenate([p0[...], p1[...], p2[...]], axis=0).astype(BF16)
        dpall = jnp.concatenate([d0[...], d1[...], d2[...]], axis=0)
        lo = lax.broadcasted_iota(jnp.int32, (3 * BLK, KV_W), 1) < HEAD_DIM
        for j in range(nq):
            n = pl.program_id(0) * nq + j
            rows, band = slice(j * BLK, (j + 1) * BLK), slice(j * BLK, (j + 3) * BLK)
            valid = _band_mask(n, nb, True)
            kb, vb, qb = kall[band], vall[band], q_ref[rows, :]
            qs = qb * SCORE_SCALE
            da = da_ref[rows, :].astype(BF16)
            dk_fold, dv_fold = [], []
            for kh in range(N_KV):
                kbd, vbd = _block_diag(kb, kh), _block_diag(vb, kh)
                q2, do2 = _stack_pairs(qb, kh), _stack_pairs(da, kh)
                sc_t = _dot_nt(kbd, _stack_pairs(qs, kh))
                dp_t = _dot_nt(vbd, do2)
                p_t, ds_t = [], []
                for half in range(2):
                    keys = slice(half * 3 * BLK, (half + 1) * 3 * BLK)
                    p, ps = _softmax_sink(sc_t[keys], valid, _sink_of(sink_ref, kh, half, 1), 0)
                    delta = jnp.sum(p * dp_t[keys], axis=0, keepdims=True)
                    p_t.append(p.astype(BF16))
                    ds_t.append((p * (dp_t[keys] - delta)).astype(BF16))
                    dsk = -ps * delta
                    for pair in range(2):
                        h = GROUP * kh + 2 * pair + half
                        part = jnp.sum(dsk[:, pair * BLK:(pair + 1) * BLK], axis=1, keepdims=True)
                        dsink_ref[h:h + 1, :] += jnp.broadcast_to(part, (1, 128))
                p_t = jnp.concatenate(p_t, axis=0)
                ds_t = jnp.concatenate(ds_t, axis=0)
                dq2 = _dot_tn(ds_t, kbd) * SCORE_SCALE
                dq_ref[rows, (2 * kh) * 128:(2 * kh + 1) * 128] = dq2[:BLK]
                dq_ref[rows, (2 * kh + 1) * 128:(2 * kh + 2) * 128] = dq2[BLK:]
                dk_fold.append(_fold_diag(_dot_nn(ds_t, q2)) * SCORE_SCALE)
                dv_fold.append(_fold_diag(_dot_nn(p_t, do2)))
            dk_all = jnp.where(lo, dk_fold[0], dk_fold[1])
            dv_all = jnp.where(lo, dv_fold[0], dv_fold[1])
            for t in range(3):
                dk_ref[j, t] = dk_all[t * BLK:(t + 1) * BLK]
                dv_ref[j, t] = dv_all[t * BLK:(t + 1) * BLK]
            ext, dpe = pall[band], dpall[band]
            dpc_cur = dpall[(j + 1) * BLK:(j + 2) * BLK]
            var = _variant_index(n, nb)
            for gi in range(POOL_G):
                gsl = slice(gi * POOL_GW, (gi + 1) * POOL_GW)
                wg = pw_ref[gi].astype(BF16)
                sc = ps_ref[:, gsl]
                dgb = _dot_nn(pb_ref[var, gi], ext[:, gsl]).astype(BF16)
                yg = _dot_nn(dgb, wg)
                dps_ref[:, gsl] += jnp.sum(dpc_cur[:, gsl] * yg, axis=0, keepdims=True)
                dpw_ref[gi] += _dot_tn(dgb, (dpc_cur[:, gsl] * sc).astype(BF16))
                dd = _dot_nt((dpe[:, gsl] * sc).astype(BF16), wg)
                dpc_ref[rows, gsl] = _dot_nn(ptb_ref[var, gi], dd.astype(BF16))

    fixed = lambda shape: pl.BlockSpec(shape, lambda m: (0,) * len(shape))
    return _call(
        body, name, (nb // nq,),
        [pl.BlockSpec(memory_space=pltpu.SMEM), pl.BlockSpec((nq * BLK, ATTN_W), lambda m: (m, 0)),
         *_band_specs(nq, nb, KV_W), *_band_specs(nq, nb, KV_W), *_band_specs(nq, nb, POOL_W),
         pl.BlockSpec((nq * BLK, ATTN_W), lambda m: (m, 0)), *_band_specs(nq, nb, POOL_W, 1),
         _whole((POOL_G, POOL_GW, POOL_GW)), _whole((1, POOL_W)), _whole(pband.shape), _whole(ptband.shape)],
        [pl.BlockSpec((nq * BLK, ATTN_W), lambda m: (m, 0)),
         pl.BlockSpec((nq, 3, BLK, KV_W), lambda m: (m, 0, 0, 0)),
         pl.BlockSpec((nq, 3, BLK, KV_W), lambda m: (m, 0, 0, 0)),
         pl.BlockSpec((nq * BLK, POOL_W), lambda m: (m, 0)),
         fixed((N_HEADS, 128)), fixed((POOL_G, POOL_GW, POOL_GW)), fixed((1, POOL_W))],
        [jax.ShapeDtypeStruct((s, ATTN_W), F32), jax.ShapeDtypeStruct((nb, 3, BLK, KV_W), F32),
         jax.ShapeDtypeStruct((nb, 3, BLK, KV_W), F32), jax.ShapeDtypeStruct((s, POOL_W), F32),
         jax.ShapeDtypeStruct((N_HEADS, 128), F32),
         jax.ShapeDtypeStruct((POOL_G, POOL_GW, POOL_GW), F32), jax.ShapeDtypeStruct((1, POOL_W), F32)],
        (sink, q, k, k, k, v, v, v, pc, pc, pc, dmix, dmix, dmix, dmix, pool_w, pool_scale, pband, ptband),
        ("arbitrary",), carry)


def _mix_out(h, mix, w_out, name, tm, scale=1.0, carry=None, middle=CARRY_MIDDLE):
    s, d = h.shape
    w = mix.shape[1]

    def body(h_ref, m_ref, w_ref, o_ref):
        o_ref[...] = h_ref[...] + scale * _dot_nn(m_ref[...], w_ref[...])

    row = lambda c: pl.BlockSpec((tm, c), lambda i: (i, 0))
    return _call(body, name, (s // tm,), [row(d), row(w), _whole((w, d))], [row(d)],
                 [jax.ShapeDtypeStruct((s, d), F32)], (h, mix, w_out), ("arbitrary",), carry, middle)


def _mix_out_bwd(dh, w_out, name, tm):
    s, d = dh.shape
    w = w_out.shape[0]

    def body(dh_ref, w_ref, o_ref, dhb_ref):
        dhb = dh_ref[...].astype(BF16)
        dhb_ref[...] = dhb
        o_ref[...] = _dot_nt(dhb, w_ref[...])

    row = lambda c: pl.BlockSpec((tm, c), lambda i: (i, 0))
    return pl.pallas_call(
        body, name=name, grid=(s // tm,), in_specs=[row(d), _whole((w, d))], out_specs=[row(w), row(d)],
        out_shape=[jax.ShapeDtypeStruct((s, w), F32), jax.ShapeDtypeStruct((s, d), BF16)],
        compiler_params=_cparams(("arbitrary",)),
    )(dh, w_out)


def _mix_in_bwd(h, dh, g, win_t, dq, dkp, dvp, dpc, tabs, name, tm, carry=None):
    s, d = h.shape
    nb = s // BLK
    nt = tm // BLK
    n_in = win_t.shape[0]

    def band_sum(n, before, own, after, prev_last, next_first):
        lo = (n > 0).astype(F32)
        hi = (n < s // tm - 1).astype(F32)
        blocks = []
        for b in range(nt):
            from_prev = prev_last[...] * lo if b == 0 else before[b - 1]
            from_next = next_first[...] * hi if b == nt - 1 else after[b + 1]
            blocks.append(from_prev + own[b] + from_next)
        return jnp.concatenate(blocks, axis=0)

    def body(h_ref, dh_ref, g_ref, w_ref, dq_ref, k2, k1, k0, kp, kn, v2, v1, v0, vp, vn, dpc_ref, c_ref, sa_ref,
             sb_ref, o_ref, du_ref, dg_ref):
        n = pl.program_id(0)
        dk = band_sum(n, k2, k1, k0, kp, kn)
        dv = band_sum(n, v2, v1, v0, vp, vn)
        c, sa, sb = c_ref[...], sa_ref[...], sb_ref[...]
        du = jnp.concatenate([_rope_bwd(dq_ref[...], c, sa, sb, ATTN_W // 128), _rope_bwd(dk, c, sa, sb, 1), dv,
                              dpc_ref[...]], axis=1)
        du_ref[...] = du.T.astype(BF16)
        dn = _dot_nn(du.astype(BF16), w_ref[...])
        x = h_ref[...]
        r = lax.rsqrt(jnp.mean(x * x, axis=-1, keepdims=True) + EPS)
        xhat = x * r
        dxhat = dn * g_ref[...]
        o_ref[...] = dh_ref[...] + r * (dxhat - xhat * jnp.mean(dxhat * xhat, axis=-1, keepdims=True))

        @pl.when(n == 0)
        def _():
            dg_ref[...] = jnp.zeros_like(dg_ref)

        dg_ref[...] += jnp.sum(dn * xhat, axis=0, keepdims=True)

    row = lambda w: pl.BlockSpec((tm, w), lambda n: (n, 0))
    slot = lambda t: pl.BlockSpec((nt, None, BLK, KV_W), lambda n, t=t: (n, t, 0, 0))
    parts = [slot(2), slot(1), slot(0),
             pl.BlockSpec((None, None, BLK, KV_W), lambda n: (jnp.maximum(nt * n - 1, 0), 2, 0, 0)),
             pl.BlockSpec((None, None, BLK, KV_W), lambda n: (jnp.minimum(nt * n + nt, nb - 1), 0, 0, 0))]
    return _call(
        body, name, (s // tm,),
        [row(d), row(d), _whole((1, d)), _whole((n_in, d)), row(ATTN_W), *parts, *parts, row(POOL_W),
         row(128), row(128), row(128)],
        [row(d), pl.BlockSpec((n_in, tm), lambda n: (0, n)), pl.BlockSpec((1, d), lambda n: (0, 0))],
        [jax.ShapeDtypeStruct((s, d), F32), jax.ShapeDtypeStruct((n_in, s), BF16), jax.ShapeDtypeStruct((1, d), F32)],
        (h, dh, g, win_t, dq, *[dkp] * 5, *[dvp] * 5, dpc, *tabs), ("arbitrary",), carry)


def _adam_math(w, g, m, v):
    m = ADAM_B1 * m + (1.0 - ADAM_B1) * g
    v = ADAM_B2 * v + (1.0 - ADAM_B2) * (g * g)
    m_hat = m / (1.0 - ADAM_B1 ** ADAM_STEP)
    v_hat = v / (1.0 - ADAM_B2 ** ADAM_STEP)
    delta = -ADAM_LR * (m_hat / (jnp.sqrt(v_hat) + ADAM_EPS) + ADAM_WD * w)
    return delta, m, v


def _adam_small(w, parts, late, m, v, name):
    rows, cols = w.shape

    def body(w_ref, p_ref, l_ref, m_ref, v_ref, g_ref, d_ref, nm_ref, nv_ref):
        g, gl = p_ref[0], l_ref[0]
        for k in range(1, N_DEV):
            g = g + p_ref[k]
            gl = gl + l_ref[k]
        g_ref[...] = g
        g_ref[SMALL_NORM1:SMALL_NORM1 + 8, :] = g[SMALL_NORM1:SMALL_NORM1 + 8] + gl
        d_ref[...], nm_ref[...], nv_ref[...] = _adam_math(w_ref[...], g_ref[...], m_ref[...], v_ref[...])

    return pl.pallas_call(
        body, name=name, out_shape=[jax.ShapeDtypeStruct((rows, cols), F32)] * 4,
    )(w, parts, late, m, v)


SMALL_NORM1 = 512


def _pack_small(norm1, normm, norm2, normf, sink, pool_w, pool_scale, loss=None):
    scale_rows = jnp.pad(pool_scale.reshape(4, 128), ((0, 4), (0, 0)))
    last_rows = jnp.pad(sink.reshape(1, N_HEADS), ((0, 7), (0, 128 - N_HEADS)))
    if loss is not None:
        last_rows = last_rows + jnp.pad(loss.reshape(1, 1), ((1, 6), (0, 127)))
    return jnp.concatenate([pool_w.reshape(512, 128), norm1.reshape(8, 128), normm.reshape(8, 128),
                            norm2.reshape(8, 128), normf.reshape(8, 128), scale_rows, last_rows], axis=0)


def _unpack_small(p):
    return dict(pool_w=p[:512].reshape(1, POOL_G, POOL_GW, POOL_GW), ffn1_norm=p[512:520].reshape(1, 1024),
                mix_norm=p[520:528].reshape(1, 1024), ffn2_norm=p[528:536].reshape(1, 1024),
                final_norm=p[536:544].reshape(1024), pool_scale=p[544:548].reshape(1, POOL_W),
                sink_logits=p[552, :N_HEADS].reshape(1, N_HEADS), loss=p[553, 0])


def kernel(x, ffn1_norm, ffn1_w_gate, ffn1_w_up, ffn1_w_down, mix_norm, w_in, sink_logits, pool_w, pool_scale, w_out, ffn2_norm, ffn2_w_gate, ffn2_w_up, ffn2_w_down, final_norm, loss_target, m_ffn1_norm, m_ffn1_w_gate, m_ffn1_w_up, m_ffn1_w_down, m_mix_norm, m_w_in, m_sink_logits, m_pool_w, m_pool_scale, m_w_out, m_ffn2_norm, m_ffn2_w_gate, m_ffn2_w_up, m_ffn2_w_down, m_final_norm, v_ffn1_norm, v_ffn1_w_gate, v_ffn1_w_up, v_ffn1_w_down, v_mix_norm, v_w_in, v_sink_logits, v_pool_w, v_pool_scale, v_w_out, v_ffn2_norm, v_ffn2_w_gate, v_ffn2_w_up, v_ffn2_w_down, v_final_norm):
    s, d = x.shape[1], x.shape[2]
    fk = ffn1_w_gate.shape[2]
    f = N_DEV * fk
    ink = w_in.shape[2]
    n_in = N_DEV * ink
    mixk = w_out.shape[1]
    tm = min(512, s)
    tm_bwd = min(256, s)
    pos = jnp.stack([lax.axis_index("x"), lax.axis_index("y"), lax.axis_index("c")]).astype(jnp.int32)

    t_bf = lambda w: w[0].T.astype(BF16)
    full = lambda a: a.reshape(N_DEV * a.shape[1], d)
    (wg1,) = map(full, _run_exchange(_AllGather([t_bf(ffn1_w_gate)]), "gather_ffn1_gate"))

    tabs = _rope_tables(s)
    pband, ptband = _pool_tables(s)
    g1, gm, g2, gf = ffn1_norm, mix_norm, ffn2_norm, final_norm.reshape(1, d)

    x0 = x[0]
    n1, gate1, wu1 = _ffn_gate(x0, g1, wg1, "ffn1_gate", tm, carry=_AllGather([t_bf(ffn1_w_up)]))
    up1, act1, act1_t, wd1 = _ffn_up(n1, gate1, full(wu1), "ffn1_up", tm,
                                     carry=_AllGather([ffn1_w_down[0].astype(BF16)]))
    wu1, wd1 = full(wu1), full(wd1)
    h1, win_t, wout = _mix_out(x0, act1, wd1, "ffn1_down", tm, scale=0.5,
                               carry=_AllGather([t_bf(w_in), w_out[0].astype(BF16)]))
    win_t, wout = full(win_t), full(wout)
    q, k, v, pc, n2, wg2 = _mix_in(h1, gm, win_t, tabs, "mix_in", tm, carry=_AllGather([t_bf(ffn2_w_gate)]))
    wg2 = full(wg2)
    mix, mix_t, *gathered = _attn_pool_fwd(q, k, v, pc, sink_logits, pool_w[0], pool_scale, pband, "attn_pool_fwd",
                                           carry=_AllGather([t_bf(ffn2_w_up), ffn2_w_down[0].astype(BF16)]))
    wu2, wd2 = map(full, gathered)
    (h2,) = _mix_out(h1, mix, wout, "mix_out", tm)
    dh3, n3, gate2, up2, act2_t, loss_part, dgf = _ffn_fwd(h2, g2, wg2, wu2, wd2, "ffn2_fwd", tm,
                                                           head=(gf, loss_target[0]))

    sum1, recv2 = {}, {}

    def stage2(keys):
        return _RsStage2([sum1[key][1] for key in keys])

    dh2, dg2, dgate2_t, dup2_t, dhh3 = _ffn_bwd(h2, dh3, gate2, up2, g2, wg2, wu2, wd2, "ffn2_bwd", tm_bwd)
    sum1["g2"] = _wgrad_rs1(pos, dgate2_t, n3, "wgrad_gate2")
    sum1["u2"] = _wgrad_rs1(pos, dup2_t, n3, "wgrad_up2")
    sum1["d2"] = _wgrad_rs1(pos, act2_t, dhh3, "wgrad_down2")
    dmix, dh2b = _mix_out_bwd(dh2, wout, "mix_out_bwd", tm)
    sum1["out"] = _wgrad_rs1(pos, mix_t, dh2b, "wgrad_out")
    dq, dkp, dvp, dpc, dsink, dpw, dps, *r2 = _attn_pool_bwd(
        q, k, v, pc, dmix, sink_logits, pool_w[0], pool_scale, pband, ptband, "attn_pool_bwd",
        carry=stage2(["g2", "u2", "d2"]))
    recv2.update(zip(["g2", "u2", "d2"], r2))
    dh1, du_t, dgm, recv2["out"] = _mix_in_bwd(h1, dh2, gm, win_t, dq, dkp, dvp, dpc, tabs, "mix_in_bwd", tm,
                                               carry=stage2(["out"]))
    sum1["in"] = _wgrad_rs1(pos, du_t, n2, "wgrad_in")
    small_part = _pack_small(jnp.zeros_like(dgm), dgm, dg2, dgf, dsink[:, 0], dpw, dps, loss_part)
    dgate1, dup1, dgate1_t, dup1_t, dhh1, recv2["in"], small_all = _ffn_bwd_gates(
        dh1, gate1, up1, wd1, "ffn1_bwd_gates", tm, carry=_Both(stage2(["in"]), _AllGather([small_part])))
    sum1["g1"] = _wgrad_rs1(pos, dgate1_t, n1, "wgrad_gate1")
    *sum1["u1"], recv2["g1"] = _wgrad_rs1(pos, dup1_t, n1, "wgrad_up1", carry=stage2(["g1"]))
    *sum1["d1"], recv2["u1"] = _wgrad_rs1(pos, act1_t, dhh1, "wgrad_down1", carry=stage2(["u1"]))
    dx, dg1, recv2["d1"] = _ffn_bwd_input(x0, dh1, dgate1, dup1, g1, wg1, wu1, "ffn1_bwd_input", tm,
                                          carry=stage2(["d1"]))

    (dg1_all,) = _run_exchange(_DirectGather([dg1.reshape(8, 128)]), "gather_norm1_grad")
    pk = lambda a, b, c_, e, s_, pw_, psc: _pack_small(a, b, c_, e, s_[0], pw_[0], psc)
    small_w = pk(ffn1_norm, mix_norm, ffn2_norm, final_norm, sink_logits, pool_w, pool_scale)
    small_m = pk(m_ffn1_norm, m_mix_norm, m_ffn2_norm, m_final_norm, m_sink_logits, m_pool_w, m_pool_scale)
    small_v = pk(v_ffn1_norm, v_mix_norm, v_ffn2_norm, v_final_norm, v_sink_logits, v_pool_w, v_pool_scale)
    sg, sd, sm, sv = [_unpack_small(a)
                      for a in _adam_small(small_w, small_all, dg1_all, small_m, small_v, "adam_small")]

    big = {}
    keys = ["g1", "u1", "d1", "g2", "u2", "d2", "in", "out"]
    names = ["ffn1_w_gate", "ffn1_w_up", "ffn1_w_down", "ffn2_w_gate", "ffn2_w_up", "ffn2_w_down", "w_in", "w_out"]
    transposed = [True, True, False, True, True, False, True, False]
    ws = [ffn1_w_gate, ffn1_w_up, ffn1_w_down, ffn2_w_gate, ffn2_w_up, ffn2_w_down, w_in, w_out]
    ms = [m_ffn1_w_gate, m_ffn1_w_up, m_ffn1_w_down, m_ffn2_w_gate, m_ffn2_w_up, m_ffn2_w_down, m_w_in, m_w_out]
    vs = [v_ffn1_w_gate, v_ffn1_w_up, v_ffn1_w_down, v_ffn2_w_gate, v_ffn2_w_up, v_ffn2_w_down, v_w_in, v_w_out]
    for key, nm, tr, w, m, vv in zip(keys, names, transposed, ws, ms, vs):
        view = (lambda a: jnp.swapaxes(a, 1, 2)[0]) if tr else (lambda a: a[0])
        back = (lambda a: jnp.swapaxes(a[None], 1, 2)) if tr else (lambda a: a[None])
        res = _rs_sum2_adam(pos, sum1[key][0], recv2[key], view(w), view(m), view(vv), "adam_" + nm)
        big[nm] = tuple(back(a) for a in res)

    loss = sg["loss"]
    all_names = ["ffn1_norm", "ffn1_w_gate", "ffn1_w_up", "ffn1_w_down", "mix_norm", "w_in", "sink_logits", "pool_w",
                 "pool_scale", "w_out", "ffn2_norm", "ffn2_w_gate", "ffn2_w_up", "ffn2_w_down", "final_norm"]
    outs = [loss, dx[None]]
    for idx, src in enumerate((sg, sd, sm, sv)):
        for nm in all_names:
            outs.append(big[nm][idx] if nm in big else src[nm])
    return tuple(outs)
```

```python
import functools

import jax
import jax.numpy as jnp
import numpy as np
from jax import lax
from jax.experimental import pallas as pl
from jax.experimental.pallas import tpu as pltpu

F32 = jnp.float32
BF16 = jnp.bfloat16
MESH = pl.DeviceIdType.MESH
N_DEV = 8

EPS = 1e-6
HEAD_DIM = 64
N_HEADS = 8
N_KV = 2
GROUP = N_HEADS // N_KV
ATTN_W = N_HEADS * HEAD_DIM
KV_W = N_KV * HEAD_DIM
POOL_W = 512
POOL_G = 4
POOL_GW = POOL_W // POOL_G
POOL_WINDOWS = (2, 4, 8, 16)
BLK = 128
ROT = 16
ROPE_THETA = 500000.0
SCORE_SCALE = HEAD_DIM ** -0.5

ADAM_LR, ADAM_B1, ADAM_B2, ADAM_EPS, ADAM_WD, ADAM_STEP = 0.001, 0.9, 0.999, 1e-08, 0.01, 10

VMEM_LIMIT = 56 * 1024 * 1024


def _cparams(sem=None, **kw):
    if sem is not None:
        kw["dimension_semantics"] = sem
    return pltpu.CompilerParams(vmem_limit_bytes=VMEM_LIMIT, **kw)


def _whole(shape):
    nd = len(shape)
    return pl.BlockSpec(shape, lambda *_: (0,) * nd, pipeline_mode=pl.Buffered(1))


def _sigmoid(z):
    return 1.0 / (1.0 + jnp.exp(-z))


def _dot_nt(a, b):
    return lax.dot_general(a, b, (((1,), (1,)), ((), ())), preferred_element_type=F32)


def _dot_nn(a, b):
    return lax.dot_general(a, b, (((1,), (0,)), ((), ())), preferred_element_type=F32)


def _dot_tn(a, b):
    return lax.dot_general(a, b, (((0,), (0,)), ((), ())), preferred_element_type=F32)


def _rope_tables(s):
    inv_freq = ROPE_THETA ** (-np.arange(0, ROT, 2, dtype=np.float64) / ROT)
    ang = np.arange(s, dtype=np.float64)[:, None] * inv_freq[None, :]
    c = np.ones((s, HEAD_DIM)); sa = np.zeros((s, HEAD_DIM)); sb = np.zeros((s, HEAD_DIM))
    c[:, :8] = np.cos(ang); c[:, 8:16] = np.cos(ang)
    sa[:, :8] = -np.sin(ang)
    sb[:, 8:16] = np.sin(ang)
    t = lambda a: jnp.asarray(np.tile(a, (1, 2)).astype(np.float32))
    return t(c), t(sa), t(sb)


def _pool_weight(gi, t, s_pos, s):
    half = POOL_WINDOWS[gi] // 2

    def win(lo, hi):
        a = np.clip(lo, 0, s); b = np.clip(hi + 1, 0, s)
        inside = (s_pos >= a) & (s_pos < b)
        return inside / np.maximum(b - a, 1)

    w = 0.5 * (win(t - half, t + half - 1) + win(t - half + 1, t + half)) - (t == s_pos)
    return w * ((t >= 0) & (t < s) & (s_pos >= 0) & (s_pos < s))


def _pool_tables(s):
    nb = s // BLK
    fwd = np.zeros((3, POOL_G, BLK, 3 * BLK), np.float32)
    bwd = np.zeros((3, POOL_G, BLK, 3 * BLK), np.float32)
    for vi, n in enumerate((0, 1 if nb > 2 else 0, nb - 1)):
        i = n * BLK + np.arange(BLK)[:, None]
        j = (n - 1) * BLK + np.arange(3 * BLK)[None, :]
        for gi in range(POOL_G):
            fwd[vi, gi] = _pool_weight(gi, i, j, s)
            bwd[vi, gi] = _pool_weight(gi, j, i, s)
    return jnp.asarray(fwd, dtype=BF16), jnp.asarray(bwd, dtype=BF16)


def _variant_index(n, nb):
    return jnp.where(n == 0, 0, jnp.where(n == nb - 1, 2, 1))


class _Exchange:
    inputs = ()
    out_shapes = ()
    sems = ()

    def start(self, srcs, outs, sems):
        raise NotImplementedError

    def middle(self, srcs, outs, sems):
        pass

    def finish(self, srcs, outs, sems):
        raise NotImplementedError


class _AllGather(_Exchange):
    def __init__(self, arrays):
        n = len(arrays)
        self.inputs = list(arrays)
        self.out_shapes = [jax.ShapeDtypeStruct((N_DEV,) + a.shape, a.dtype) for a in arrays]
        self.sems = [pltpu.SemaphoreType.DMA((n, 8)), pltpu.SemaphoreType.DMA((n, 8)), pltpu.SemaphoreType.DMA((n,))]

    def _parts(self, srcs, outs, sems):
        send_sems, recv_sems, local_sems = sems
        n = len(srcs)
        x, y, c = lax.axis_index("x"), lax.axis_index("y"), lax.axis_index("c")
        me, sibling, xn, yn, diag = (x, y, c), (x, y, 1 - c), (1 - x, y, c), (x, 1 - y, c), (1 - x, 1 - y, c)

        def place(a, dev, half=None):
            block = outs[a].at[4 * dev[0] + 2 * dev[1] + dev[2]]
            if half is None:
                return block
            r2 = outs[a].shape[1] // 2
            return block.at[pl.ds(half * r2, r2)]

        def copy(a, k, dev, to, half=None, src=None):
            where = place(a, dev, half)
            return pltpu.make_async_remote_copy(
                src_ref=where if src is None else src, dst_ref=where, send_sem=send_sems.at[a, k],
                recv_sem=recv_sems.at[a, k], device_id=to, device_id_type=MESH)

        def other(dev):
            return (dev[0], dev[1], 1 - dev[2])

        class Parts:
            mine = staticmethod(lambda: [pltpu.make_async_copy(srcs[a], place(a, me), local_sems.at[a])
                                         for a in range(n)])
            own = staticmethod(lambda: [copy(a, k, me, to, src=srcs[a]) for a in range(n)
                                        for k, to in ((0, sibling), (1, xn), (2, yn))])
            relay = staticmethod(lambda a: [copy(a, 3, xn, yn, half=0), copy(a, 4, yn, xn, half=1),
                                            copy(a, 5, xn, sibling), copy(a, 6, yn, sibling)])
            last = staticmethod(lambda a: copy(a, 7, diag, sibling))
            from_x = staticmethod(lambda a: copy(a, 1, xn, me))
            from_y = staticmethod(lambda a: copy(a, 2, yn, me))
            diag_halves = staticmethod(lambda a: [copy(a, 3, diag, me, half=0), copy(a, 4, diag, me, half=1)])
            from_sibling = staticmethod(lambda a: [copy(a, 0, sibling, me), copy(a, 5, other(xn), me),
                                                   copy(a, 6, other(yn), me), copy(a, 7, other(diag), me)])

        return n, Parts

    def start(self, srcs, outs, sems):
        _, p = self._parts(srcs, outs, sems)
        for cp in p.mine() + p.own():
            cp.start()

    def middle(self, srcs, outs, sems):
        n, p = self._parts(srcs, outs, sems)
        for a in range(n):
            p.from_x(a).wait_recv()
            p.from_y(a).wait_recv()
            for cp in p.relay(a):
                cp.start()

    def finish(self, srcs, outs, sems):
        n, p = self._parts(srcs, outs, sems)
        for a in range(n):
            for cp in p.diag_halves(a):
                cp.wait_recv()
            p.last(a).start()
        for a in range(n):
            for cp in p.from_sibling(a):
                cp.wait_recv()
        for cp in p.own() + [cp for a in range(n) for cp in p.relay(a) + [p.last(a)]]:
            cp.wait_send()
        for cp in p.mine():
            cp.wait()


class _RsStage2(_Exchange):
    def start(self, srcs, outs, sems):
        for cp in self._copies(srcs, outs, sems):
            cp.start()

    def finish(self, srcs, outs, sems):
        copies = self._copies(srcs, outs, sems)
        for cp in copies:
            cp.wait_recv()
        for cp in copies:
            cp.wait_send()


    def __init__(self, pbs):
        n = len(pbs)
        self.inputs = list(pbs)
        self.out_shapes = [jax.ShapeDtypeStruct((3,) + p.shape[1:], p.dtype) for p in pbs]
        self.sems = [pltpu.SemaphoreType.DMA((n, 3)), pltpu.SemaphoreType.DMA((n, 3))]

    def _copies(self, srcs, outs, sems):
        send_sems, recv_sems = sems
        x, y, c = lax.axis_index("x"), lax.axis_index("y"), lax.axis_index("c")
        chips = [(1 - x, y), (x, 1 - y), (1 - x, 1 - y)]
        return [pltpu.make_async_remote_copy(
            src_ref=srcs[a].at[2 * chip[0] + chip[1]], dst_ref=outs[a].at[j], send_sem=send_sems.at[a, j],
            recv_sem=recv_sems.at[a, j], device_id=(*chip, c), device_id_type=MESH)
            for a in range(len(srcs)) for j, chip in enumerate(chips)]


class _DirectGather(_Exchange):
    def __init__(self, arrays):
        n = len(arrays)
        self.inputs = list(arrays)
        self.out_shapes = [jax.ShapeDtypeStruct((N_DEV,) + a.shape, a.dtype) for a in arrays]
        self.sems = [pltpu.SemaphoreType.DMA((n, 7)), pltpu.SemaphoreType.DMA((n, 7)), pltpu.SemaphoreType.DMA((n,))]

    def _copies(self, srcs, outs, sems):
        send_sems, recv_sems, local_sems = sems
        x, y, c = lax.axis_index("x"), lax.axis_index("y"), lax.axis_index("c")
        me = 4 * x + 2 * y + c
        remote, local = [], []
        for a in range(len(srcs)):
            local.append(pltpu.make_async_copy(srcs[a], outs[a].at[me], local_sems.at[a]))
            for k in range(1, N_DEV):
                peer = (x ^ (k >> 2), y ^ ((k >> 1) & 1), c ^ (k & 1))
                remote.append(pltpu.make_async_remote_copy(
                    src_ref=srcs[a], dst_ref=outs[a].at[me], send_sem=send_sems.at[a, k - 1],
                    recv_sem=recv_sems.at[a, k - 1], device_id=peer, device_id_type=MESH))
        return remote, local

    def start(self, srcs, outs, sems):
        remote, local = self._copies(srcs, outs, sems)
        for cp in local + remote:
            cp.start()

    def finish(self, srcs, outs, sems):
        remote, local = self._copies(srcs, outs, sems)
        for cp in remote:
            cp.wait_recv()
        for cp in remote:
            cp.wait_send()
        for cp in local:
            cp.wait()


class _Both(_Exchange):
    def __init__(self, a, b):
        self.a, self.b = a, b
        self.inputs = list(a.inputs) + list(b.inputs)
        self.out_shapes = list(a.out_shapes) + list(b.out_shapes)
        self.sems = list(a.sems) + list(b.sems)

    def _split(self, srcs, outs, sems):
        na, oa, sa = len(self.a.inputs), len(self.a.out_shapes), len(self.a.sems)
        return (srcs[:na], outs[:oa], sems[:sa]), (srcs[na:], outs[oa:], sems[sa:])

    def start(self, srcs, outs, sems):
        pa, pb = self._split(srcs, outs, sems)
        self.a.start(*pa)
        self.b.start(*pb)

    def middle(self, srcs, outs, sems):
        pa, pb = self._split(srcs, outs, sems)
        self.a.middle(*pa)
        self.b.middle(*pb)

    def finish(self, srcs, outs, sems):
        pa, pb = self._split(srcs, outs, sems)
        self.a.finish(*pa)
        self.b.finish(*pb)


_ANY = pl.BlockSpec(memory_space=pl.ANY)


def _run_exchange(ex, name):
    n_in, n_out = len(ex.inputs), len(ex.out_shapes)

    def body(*refs):
        srcs, outs, sems = refs[:n_in], refs[n_in:n_in + n_out], refs[n_in + n_out:]
        ex.start(srcs, outs, sems)
        ex.middle(srcs, outs, sems)
        ex.finish(srcs, outs, sems)

    return pl.pallas_call(
        body, name=name, out_shape=list(ex.out_shapes), in_specs=[_ANY] * n_in, out_specs=[_ANY] * n_out,
        scratch_shapes=list(ex.sems),
    )(*ex.inputs)


CARRY_MIDDLE = 0.7


def _call(body, name, grid, in_specs, out_specs, out_shape, args, sem, carry=None, middle=CARRY_MIDDLE, scratch=()):
    if carry is None:
        return pl.pallas_call(functools.partial(body), name=name, grid=grid, in_specs=in_specs, out_specs=out_specs,
                              out_shape=out_shape, scratch_shapes=list(scratch), compiler_params=_cparams(sem))(*args)
    n_in, n_out = len(in_specs), len(out_specs)
    nc_in, nc_out = len(carry.inputs), len(carry.out_shapes)

    def carried(*refs):
        ins = refs[:n_in]
        c_in = refs[n_in:n_in + nc_in]
        outs = refs[n_in + nc_in:n_in + nc_in + n_out]
        c_out = refs[n_in + nc_in + n_out:n_in + nc_in + n_out + nc_out]
        own = refs[n_in + nc_in + n_out + nc_out:n_in + nc_in + n_out + nc_out + len(scratch)]
        sems = refs[n_in + nc_in + n_out + nc_out + len(scratch):]
        ids = [pl.program_id(i) for i in range(len(grid))]
        is_first = functools.reduce(jnp.logical_and, [i == 0 for i in ids])
        is_last = functools.reduce(jnp.logical_and, [i == g - 1 for i, g in zip(ids, grid)])
        @pl.when(is_first)
        def _():
            carry.start(c_in, c_out, sems)

        if middle is not None:
            @pl.when(functools.reduce(jnp.logical_and, [ids[0] == round(middle * (grid[0] - 1))]
                                      + [i == 0 for i in ids[1:]]))
            def _():
                carry.middle(c_in, c_out, sems)

        body(*ins, *outs, *own)

        @pl.when(is_last)
        def _():
            if middle is None:
                carry.middle(c_in, c_out, sems)
            carry.finish(c_in, c_out, sems)

    return pl.pallas_call(
        carried, name=name, grid=grid, in_specs=list(in_specs) + [_ANY] * nc_in,
        out_specs=list(out_specs) + [_ANY] * nc_out, out_shape=list(out_shape) + list(carry.out_shapes),
        scratch_shapes=list(scratch) + list(carry.sems), compiler_params=_cparams(sem))(*args, *carry.inputs)


def _rs_sum2_adam(pos, p, r2, w, m, v, name):
    _, rows, d = p.shape
    tr = 32 if rows % 32 == 0 else rows

    def body(pos_ref, p_ref, r_ref, w_ref, m_ref, v_ref, g_ref, d_ref, nm_ref, nv_ref):
        r = r_ref[...].astype(F32)
        g = ((p_ref[...] + r[0]) + r[1]) + r[2]
        g_ref[...] = g
        d_ref[...], nm_ref[...], nv_ref[...] = _adam_math(w_ref[...], g, m_ref[...], v_ref[...])

    blk = pl.BlockSpec((tr, d), lambda i, pos_ref: (i, 0))
    return pl.pallas_call(
        body, name=name,
        grid_spec=pltpu.PrefetchScalarGridSpec(
            num_scalar_prefetch=1, grid=(rows // tr,),
            in_specs=[pl.BlockSpec((None, tr, d), lambda i, pos_ref: (2 * pos_ref[0] + pos_ref[1], i, 0)),
                      pl.BlockSpec((3, tr, d), lambda i, pos_ref: (0, i, 0)), blk, blk, blk],
            out_specs=[blk] * 4),
        out_shape=[jax.ShapeDtypeStruct((rows, d), F32)] * 4,
        compiler_params=_cparams(("arbitrary",)),
    )(pos, p, r2, w, m, v)


def _ffn_chunk(f):
    for cand in (256, 128):
        if f % cand == 0:
            return cand
    return f


def _loss_head(x, gg, target, loss_ref, dg_ref):
    @pl.when(pl.program_id(0) == 0)
    def _():
        loss_ref[...] = jnp.zeros_like(loss_ref)
        dg_ref[...] = jnp.zeros_like(dg_ref)

    r = lax.rsqrt(jnp.mean(x * x, axis=-1, keepdims=True) + EPS)
    xhat = x * r
    e = xhat * gg - target
    loss_ref[...] += 0.5 * jnp.sum(jnp.mean(e * e, axis=-1, keepdims=True), axis=0, keepdims=True)
    dy = e * (1.0 / x.shape[-1])
    dg_ref[...] += jnp.sum(dy * xhat, axis=0, keepdims=True)
    dxhat = dy * gg
    return r * (dxhat - xhat * jnp.mean(dxhat * xhat, axis=-1, keepdims=True))


def _ffn_gate(h, g, wg_t, name, tm, carry=None):
    s, d = h.shape
    f = wg_t.shape[0]

    def body(h_ref, g_ref, wg_ref, n_ref, gate_ref):
        x = h_ref[...]
        r = lax.rsqrt(jnp.mean(x * x, axis=-1, keepdims=True) + EPS)
        nb = (x * r * g_ref[...]).astype(BF16)
        n_ref[...] = nb
        gate_ref[...] = _dot_nt(nb, wg_ref[...]).astype(BF16)

    row = lambda w: pl.BlockSpec((tm, w), lambda i: (i, 0))
    return _call(body, name, (s // tm,), [row(d), _whole((1, d)), _whole((f, d))], [row(d), row(f)],
                 [jax.ShapeDtypeStruct((s, d), BF16), jax.ShapeDtypeStruct((s, f), BF16)], (h, g, wg_t),
                 ("arbitrary",), carry)


def _ffn_up(n, gate, wu_t, name, tm, carry=None):
    s, d = n.shape
    f = wu_t.shape[0]
    tf = _ffn_chunk(f)

    def body(n_ref, gate_ref, wu_ref, up_ref, act_ref, act_t_ref):
        nb = n_ref[...]
        for j in range(f // tf):
            sl = slice(j * tf, (j + 1) * tf)
            up = _dot_nt(nb, wu_ref[sl, :])
            gate = gate_ref[:, sl].astype(F32)
            up_ref[:, sl] = up.astype(BF16)
            act = (gate * _sigmoid(gate) * up).astype(BF16)
            act_ref[:, sl] = act
            act_t_ref[sl, :] = act.T

    row = lambda w: pl.BlockSpec((tm, w), lambda i: (i, 0))
    return _call(body, name, (s // tm,), [row(d), row(f), _whole((f, d))],
                 [row(f), row(f), pl.BlockSpec((f, tm), lambda i: (0, i))],
                 [jax.ShapeDtypeStruct((s, f), BF16)] * 2 + [jax.ShapeDtypeStruct((f, s), BF16)], (n, gate, wu_t),
                 ("arbitrary",), carry)


def _ffn_fwd(h, g, wg_t, wu_t, wd, name, tm, carry=None, head=None, middle=CARRY_MIDDLE):
    s, d = h.shape
    f = wg_t.shape[0]
    tf = _ffn_chunk(f)

    def body(h_ref, g_ref, wg_ref, wu_ref, wd_ref, *refs):
        if head is None:
            o_ref, n_ref, gate_ref, up_ref, act_t_ref, act_ref = refs
        else:
            gf_ref, t_ref, o_ref, n_ref, gate_ref, up_ref, act_t_ref, loss_ref, dgf_ref, act_ref = refs
        x = h_ref[...]
        r = lax.rsqrt(jnp.mean(x * x, axis=-1, keepdims=True) + EPS)
        nb = (x * r * g_ref[...]).astype(BF16)
        n_ref[...] = nb
        for j in range(f // tf):
            sl = slice(j * tf, (j + 1) * tf)
            gate = _dot_nt(nb, wg_ref[sl, :])
            up = _dot_nt(nb, wu_ref[sl, :])
            gate_ref[:, sl] = gate.astype(BF16)
            up_ref[:, sl] = up.astype(BF16)
            act = gate * _sigmoid(gate) * up
            act = act.astype(BF16)
            act_ref[:, sl] = act
            act_t_ref[sl, :] = act.T
        h_out = x + 0.5 * _dot_nn(act_ref[...], wd_ref[...])
        o_ref[...] = h_out if head is None else _loss_head(h_out, gf_ref[...], t_ref[...], loss_ref, dgf_ref)

    row = lambda w: pl.BlockSpec((tm, w), lambda i: (i, 0))
    in_specs = [row(d), _whole((1, d)), _whole((f, d)), _whole((f, d)), _whole((f, d))]
    out_specs = [row(d), row(d), row(f), row(f), pl.BlockSpec((f, tm), lambda i: (0, i))]
    out_shape = ([jax.ShapeDtypeStruct((s, d), F32), jax.ShapeDtypeStruct((s, d), BF16)]
                 + [jax.ShapeDtypeStruct((s, f), BF16)] * 2 + [jax.ShapeDtypeStruct((f, s), BF16)])
    args = (h, g, wg_t, wu_t, wd)
    scratch = (pltpu.VMEM((tm, f), BF16),)
    if head is not None:
        in_specs += [_whole((1, d)), row(d)]
        out_specs += [pl.BlockSpec((1, 1), lambda i: (0, 0)), pl.BlockSpec((1, d), lambda i: (0, 0))]
        out_shape += [jax.ShapeDtypeStruct((1, 1), F32), jax.ShapeDtypeStruct((1, d), F32)]
        args += tuple(head)
    return _call(body, name, (s // tm,), in_specs, out_specs, out_shape, args, ("arbitrary",), carry, middle, scratch)


def _gate_grads(dh_ref, gate_ref, up_ref, wd_ref, dgate_ref, dup_ref, dgate_t_ref, dup_t_ref, dhh_ref, tf):
    dhh = (0.5 * dh_ref[...]).astype(BF16)
    dhh_ref[...] = dhh
    for j in range(gate_ref.shape[1] // tf):
        sl = slice(j * tf, (j + 1) * tf)
        gt = gate_ref[:, sl].astype(F32)
        u = up_ref[:, sl].astype(F32)
        dact = _dot_nt(dhh, wd_ref[sl, :])
        sg = _sigmoid(gt)
        dup = dact * (gt * sg)
        dgate = dact * u * (sg * (1.0 + gt * (1.0 - sg)))
        dup, dgate = dup.astype(BF16), dgate.astype(BF16)
        dup_ref[:, sl] = dup
        dgate_ref[:, sl] = dgate
        dup_t_ref[sl, :] = dup.T
        dgate_t_ref[sl, :] = dgate.T


def _input_grad(h_ref, dh_ref, dgate_ref, dup_ref, g_ref, wg_ref, wu_ref, o_ref, dg_ref):
    x = h_ref[...]
    r = lax.rsqrt(jnp.mean(x * x, axis=-1, keepdims=True) + EPS)
    xhat = x * r
    dn = _dot_nn(dgate_ref[...], wg_ref[...]) + _dot_nn(dup_ref[...], wu_ref[...])
    dxhat = dn * g_ref[...]
    o_ref[...] = dh_ref[...] + r * (dxhat - xhat * jnp.mean(dxhat * xhat, axis=-1, keepdims=True))

    @pl.when(pl.program_id(0) == 0)
    def _():
        dg_ref[...] = jnp.zeros_like(dg_ref)

    dg_ref[...] += jnp.sum(dn * xhat, axis=0, keepdims=True)


def _ffn_bwd(h_in, dh_out, gate, up, g, wg_t, wu_t, wd, name, tm):
    s, d = h_in.shape
    f = gate.shape[1]
    tf = _ffn_chunk(f)

    def body(h_ref, dh_ref, gate_ref, up_ref, g_ref, wg_ref, wu_ref, wd_ref,
             o_ref, dg_ref, dgate_t_ref, dup_t_ref, dhh_ref, dgate_ref, dup_ref):
        _gate_grads(dh_ref, gate_ref, up_ref, wd_ref, dgate_ref, dup_ref, dgate_t_ref, dup_t_ref, dhh_ref, tf)
        _input_grad(h_ref, dh_ref, dgate_ref, dup_ref, g_ref, wg_ref, wu_ref, o_ref, dg_ref)

    row = lambda w: pl.BlockSpec((tm, w), lambda i: (i, 0))
    col = pl.BlockSpec((f, tm), lambda i: (0, i))
    return pl.pallas_call(
        body, name=name, grid=(s // tm,),
        in_specs=[row(d), row(d), row(f), row(f), _whole((1, d)), _whole((f, d)), _whole((f, d)), _whole((f, d))],
        out_specs=[row(d), pl.BlockSpec((1, d), lambda i: (0, 0)), col, col, row(d)],
        out_shape=[jax.ShapeDtypeStruct((s, d), F32), jax.ShapeDtypeStruct((1, d), F32),
                   jax.ShapeDtypeStruct((f, s), BF16), jax.ShapeDtypeStruct((f, s), BF16),
                   jax.ShapeDtypeStruct((s, d), BF16)],
        scratch_shapes=[pltpu.VMEM((tm, f), BF16), pltpu.VMEM((tm, f), BF16)],
        compiler_params=_cparams(("arbitrary",)),
    )(h_in, dh_out, gate, up, g, wg_t, wu_t, wd)


def _ffn_bwd_gates(dh_out, gate, up, wd, name, tm, carry=None):
    s, d = dh_out.shape
    f = gate.shape[1]
    tf = _ffn_chunk(f)

    def body(dh_ref, gate_ref, up_ref, wd_ref, dgate_ref, dup_ref, dgate_t_ref, dup_t_ref, dhh_ref):
        _gate_grads(dh_ref, gate_ref, up_ref, wd_ref, dgate_ref, dup_ref, dgate_t_ref, dup_t_ref, dhh_ref, tf)

    row = lambda w: pl.BlockSpec((tm, w), lambda i: (i, 0))
    col = pl.BlockSpec((f, tm), lambda i: (0, i))
    return _call(
        body, name, (s // tm,), [row(d), row(f), row(f), _whole((f, d))], [row(f), row(f), col, col, row(d)],
        [jax.ShapeDtypeStruct((s, f), BF16)] * 2 + [jax.ShapeDtypeStruct((f, s), BF16)] * 2
        + [jax.ShapeDtypeStruct((s, d), BF16)],
        (dh_out, gate, up, wd), ("arbitrary",), carry)


def _ffn_bwd_input(h_in, dh_out, dgate, dup, g, wg_t, wu_t, name, tm, carry=None):
    s, d = h_in.shape
    f = dgate.shape[1]

    row = lambda w: pl.BlockSpec((tm, w), lambda i: (i, 0))
    return _call(
        _input_grad, name, (s // tm,),
        [row(d), row(d), row(f), row(f), _whole((1, d)), _whole((f, d)), _whole((f, d))],
        [row(d), pl.BlockSpec((1, d), lambda i: (0, 0))],
        [jax.ShapeDtypeStruct((s, d), F32), jax.ShapeDtypeStruct((1, d), F32)],
        (h_in, dh_out, dgate, dup, g, wg_t, wu_t), ("arbitrary",), carry)


def _wgrad_rs1(pos, a_t, b, name, carry=None):
    f, s = a_t.shape
    d = b.shape[1]
    fk = f // N_DEV
    nc_in = 0 if carry is None else len(carry.inputs)
    nc_out = 0 if carry is None else len(carry.out_shapes)

    def body(pos_ref, a_ref, b_ref, *refs):
        c_in = refs[:nc_in]
        p_ref, pb_ref = refs[nc_in:nc_in + 2]
        c_out = refs[nc_in + 2:nc_in + 2 + nc_out]
        stage, land, send_sems, recv_sems = refs[nc_in + 2 + nc_out:nc_in + 6 + nc_out]
        c_sems = refs[nc_in + 6 + nc_out:]
        t = pl.program_id(0)
        q = t % 4
        x, y, c = lax.axis_index("x"), lax.axis_index("y"), lax.axis_index("c")

        def push(k):
            return pltpu.make_async_remote_copy(src_ref=stage.at[k], dst_ref=land.at[k], send_sem=send_sems.at[k],
                                                recv_sem=recv_sems.at[k], device_id=(x, y, 1 - c), device_id_type=MESH)

        if carry is not None:
            @pl.when(t == 0)
            def _():
                carry.start(c_in, c_out, c_sems)

        g = _dot_nn(a_ref[...], b_ref[...])

        @pl.when(t < 4)
        def _():
            stage[q] = g.astype(BF16)
            push(q).start()

        @pl.when(t >= 4)
        def _():
            push(q).wait_recv()
            p = g + land[q].astype(F32)
            p_ref[...] = p
            pb_ref[...] = p.astype(BF16)

        @pl.when(t == 7)
        def _():
            for k in range(4):
                push(k).wait_send()
            if carry is not None:
                carry.middle(c_in, c_out, c_sems)
                carry.finish(c_in, c_out, c_sems)

    def shard(t, pos_ref):
        return 4 * ((t % 4) // 2) + 2 * (t % 2) + jnp.where(t < 4, 1 - pos_ref[2], pos_ref[2])

    out = pl.BlockSpec((None, fk, d), lambda t, pos_ref: (jnp.maximum(t - 4, 0), 0, 0))
    return pl.pallas_call(
        body, name=name,
        grid_spec=pltpu.PrefetchScalarGridSpec(
            num_scalar_prefetch=1, grid=(8,),
            in_specs=[pl.BlockSpec((fk, s), lambda t, pos_ref: (shard(t, pos_ref), 0)),
                      pl.BlockSpec((s, d), lambda t, pos_ref: (0, 0), pipeline_mode=pl.Buffered(1))]
            + [_ANY] * nc_in,
            out_specs=[out, out] + [_ANY] * nc_out,
            scratch_shapes=[pltpu.VMEM((4, fk, d), BF16), pltpu.VMEM((4, fk, d), BF16),
                            pltpu.SemaphoreType.DMA((4,)), pltpu.SemaphoreType.DMA((4,))]
            + ([] if carry is None else list(carry.sems))),
        out_shape=[jax.ShapeDtypeStruct((4, fk, d), F32), jax.ShapeDtypeStruct((4, fk, d), BF16)]
        + ([] if carry is None else list(carry.out_shapes)),
        compiler_params=_cparams(("arbitrary",)),
    )(pos, a_t, b, *([] if carry is None else carry.inputs))


def _rope(t, c, sa, sb, reps):
    c, sa, sb = (jnp.tile(v, (1, reps)) if reps > 1 else v for v in (c, sa, sb))
    w = t.shape[1]
    return t * c + pltpu.roll(t, w - 8, 1) * sa + pltpu.roll(t, 8, 1) * sb


def _rope_bwd(dt, c, sa, sb, reps):
    c, sa, sb = (jnp.tile(v, (1, reps)) if reps > 1 else v for v in (c, sa, sb))
    w = dt.shape[1]
    return dt * c + pltpu.roll(dt * sa, 8, 1) + pltpu.roll(dt * sb, w - 8, 1)


def _mix_in(h, g, win_t, tabs, name, tm, carry=None):
    s, d = h.shape
    n_in = win_t.shape[0]

    def body(h_ref, g_ref, w_ref, c_ref, sa_ref, sb_ref, q_ref, k_ref, v_ref, pc_ref, n_ref):
        x = h_ref[...]
        r = lax.rsqrt(jnp.mean(x * x, axis=-1, keepdims=True) + EPS)
        nb = (x * r * g_ref[...]).astype(BF16)
        n_ref[...] = nb
        u = _dot_nt(nb, w_ref[...])
        c, sa, sb = c_ref[...], sa_ref[...], sb_ref[...]
        q_ref[...] = _rope(u[:, :ATTN_W], c, sa, sb, ATTN_W // 128).astype(BF16)
        k_ref[...] = _rope(u[:, ATTN_W:ATTN_W + KV_W], c, sa, sb, 1).astype(BF16)
        v_ref[...] = u[:, ATTN_W + KV_W:ATTN_W + 2 * KV_W].astype(BF16)
        pc_ref[...] = u[:, ATTN_W + 2 * KV_W:]

    row = lambda w: pl.BlockSpec((tm, w), lambda i: (i, 0))
    return _call(
        body, name, (s // tm,),
        [row(d), _whole((1, d)), _whole((n_in, d)), row(128), row(128), row(128)],
        [row(ATTN_W), row(KV_W), row(KV_W), row(POOL_W), row(d)],
        [jax.ShapeDtypeStruct((s, ATTN_W), BF16), jax.ShapeDtypeStruct((s, KV_W), BF16),
         jax.ShapeDtypeStruct((s, KV_W), BF16), jax.ShapeDtypeStruct((s, POOL_W), F32),
         jax.ShapeDtypeStruct((s, d), BF16)],
        (h, g, win_t, *tabs), ("arbitrary",), carry)


def _band_mask(n, nb, transposed):
    shape = (3 * BLK, 2 * BLK) if transposed else (2 * BLK, 3 * BLK)
    i = lax.broadcasted_iota(jnp.int32, shape, 1 if transposed else 0) % BLK
    j = lax.broadcasted_iota(jnp.int32, shape, 0 if transposed else 1)
    kpos = (n - 1) * BLK + j
    return (j >= i) & (j <= i + 2 * BLK) & (kpos >= 0) & (kpos < nb * BLK)


def _block_diag(t, kh):
    tf = t.astype(F32)
    tr = pltpu.roll(tf, HEAD_DIM, 1)
    lo = lax.broadcasted_iota(jnp.int32, tf.shape, 1) < HEAD_DIM
    top, bot = (tf, tr) if kh == 0 else (tr, tf)
    return jnp.concatenate([jnp.where(lo, top, 0.0), jnp.where(lo, 0.0, bot)], axis=0).astype(BF16)


def _fold_diag(tbd):
    lo = lax.broadcasted_iota(jnp.int32, (3 * BLK, 2 * HEAD_DIM), 1) < HEAD_DIM
    t = jnp.where(lo, tbd[:3 * BLK], tbd[3 * BLK:])
    return t + pltpu.roll(t, HEAD_DIM, 1)


def _stack_pairs(x, kh):
    return jnp.concatenate([x[:, (2 * kh) * 128:(2 * kh + 1) * 128], x[:, (2 * kh + 1) * 128:(2 * kh + 2) * 128]], axis=0)


def _sink_of(sink_ref, kh, half, axis):
    shape = (2 * BLK, 1) if axis == 0 else (1, 2 * BLK)
    first = lax.broadcasted_iota(jnp.int32, shape, axis) < BLK
    return jnp.where(first, sink_ref[0, GROUP * kh + half], sink_ref[0, GROUP * kh + 2 + half])


def _softmax_sink(sc, valid, sink, axis):
    sc = jnp.where(valid, sc, -1e30)
    m = jnp.maximum(jnp.max(sc, axis=axis, keepdims=True), sink)
    e = jnp.exp(sc - m)
    es = jnp.exp(sink - m)
    inv = 1.0 / (jnp.sum(e, axis=axis, keepdims=True) + es)
    return e * inv, es * inv


def _attn_blocks_per_step(nb):
    return next(nq for nq in (4, 2, 1) if nb % nq == 0)


def _band_specs(nq, nb, w, col=0):
    return [pl.BlockSpec((BLK, w), lambda m: (jnp.maximum(nq * m - 1, 0), col)),
            pl.BlockSpec((nq * BLK, w), lambda m: (m, col)),
            pl.BlockSpec((BLK, w), lambda m: (jnp.minimum(nq * m + nq, nb - 1), col))]


def _attn_pool_fwd(q, k, v, pc, sink, pool_w, pool_scale, pband, name, carry=None, middle=CARRY_MIDDLE):
    s = q.shape[0]
    nb = s // BLK

    nq = _attn_blocks_per_step(nb)

    def body(sink_ref, q_ref, k0, k1, k2, v0, v1, v2, p0, p1, p2, pw_ref, ps_ref, pb_ref, o_ref, o_t_ref):
        kall = jnp.concatenate([k0[...], k1[...], k2[...]], axis=0)
        vall = jnp.concatenate([v0[...], v1[...], v2[...]], axis=0)
        pall = jnp.concatenate([p0[...], p1[...], p2[...]], axis=0).astype(BF16)
        qall = q_ref[...] * SCORE_SCALE
        for j in range(nq):
            n = pl.program_id(0) * nq + j
            rows, band = slice(j * BLK, (j + 1) * BLK), slice(j * BLK, (j + 3) * BLK)
            valid = _band_mask(n, nb, False)
            kb, vb, qs = kall[band], vall[band], qall[rows]
            for kh in range(N_KV):
                sc = _dot_nt(_stack_pairs(qs, kh), _block_diag(kb, kh))
                p = [_softmax_sink(sc[:, half * 3 * BLK:(half + 1) * 3 * BLK], valid,
                                   _sink_of(sink_ref, kh, half, 0), 1)[0] for half in range(2)]
                o2 = _dot_nn(jnp.concatenate(p, axis=1).astype(BF16), _block_diag(vb, kh)).astype(BF16)
                o_ref[rows, (2 * kh) * 128:(2 * kh + 1) * 128] = o2[:BLK]
                o_ref[rows, (2 * kh + 1) * 128:(2 * kh + 2) * 128] = o2[BLK:]
            ext = pall[band]
            var = _variant_index(n, nb)
            for gi in range(POOL_G):
                gsl = slice(gi * POOL_GW, (gi + 1) * POOL_GW)
                dg = _dot_nn(pb_ref[var, gi], ext[:, gsl])
                yg = _dot_nn(dg.astype(BF16), pw_ref[gi].astype(BF16))
                o_ref[rows, ATTN_W + gi * POOL_GW:ATTN_W + (gi + 1) * POOL_GW] = (yg * ps_ref[:, gsl]).astype(BF16)
        o_t_ref[...] = o_ref[...].astype(F32).T.astype(BF16)

    return _call(
        body, name, (nb // nq,),
        [pl.BlockSpec(memory_space=pltpu.SMEM), pl.BlockSpec((nq * BLK, ATTN_W), lambda m: (m, 0)),
         *_band_specs(nq, nb, KV_W), *_band_specs(nq, nb, KV_W), *_band_specs(nq, nb, POOL_W),
         _whole((POOL_G, POOL_GW, POOL_GW)), _whole((1, POOL_W)), _whole(pband.shape)],
        [pl.BlockSpec((nq * BLK, ATTN_W + POOL_W), lambda m: (m, 0)),
         pl.BlockSpec((ATTN_W + POOL_W, nq * BLK), lambda m: (0, m))],
        [jax.ShapeDtypeStruct((s, ATTN_W + POOL_W), BF16), jax.ShapeDtypeStruct((ATTN_W + POOL_W, s), BF16)],
        (sink, q, k, k, k, v, v, v, pc, pc, pc, pool_w, pool_scale, pband), ("arbitrary",), carry, middle)


def _attn_pool_bwd(q, k, v, pc, dmix, sink, pool_w, pool_scale, pband, ptband, name, carry=None):
    s = q.shape[0]
    nb = s // BLK
    nq = _attn_blocks_per_step(nb)

    def body(sink_ref, q_ref, k0, k1, k2, v0, v1, v2, p0, p1, p2, da_ref, d0, d1, d2, pw_ref, ps_ref, pb_ref, ptb_ref,
             dq_ref, dk_ref, dv_ref, dpc_ref, dsink_ref, dpw_ref, dps_ref):
        @pl.when(pl.program_id(0) == 0)
        def _():
            dsink_ref[...] = jnp.zeros_like(dsink_ref)
            dpw_ref[...] = jnp.zeros_like(dpw_ref)
            dps_ref[...] = jnp.zeros_like(dps_ref)

        kall = jnp.concatenate([k0[...], k1[...], k2[...]], axis=0)
        vall = jnp.concatenate([v0[...], v1[...], v2[...]], axis=0)
        pall = jnp.concatenate([p0[...], p1[...], p2[...]], axis=0).astype(BF16)
        dpall = jnp.concatenate([d0[...], d1[...], d2[...]], axis=0)
        lo = lax.broadcasted_iota(jnp.int32, (3 * BLK, KV_W), 1) < HEAD_DIM
        for j in range(nq):
            n = pl.program_id(0) * nq + j
            rows, band = slice(j * BLK, (j + 1) * BLK), slice(j * BLK, (j + 3) * BLK)
            valid = _band_mask(n, nb, True)
            kb, vb, qb = kall[band], vall[band], q_ref[rows, :]
            qs = qb * SCORE_SCALE
            da = da_ref[rows, :].astype(BF16)
            dk_fold, dv_fold = [], []
            for kh in range(N_KV):
                kbd, vbd = _block_diag(kb, kh), _block_diag(vb, kh)
                q2, do2 = _stack_pairs(qb, kh), _stack_pairs(da, kh)
                sc_t = _dot_nt(kbd, _stack_pairs(qs, kh))
                dp_t = _dot_nt(vbd, do2)
                p_t, ds_t = [], []
                for half in range(2):
                    keys = slice(half * 3 * BLK, (half + 1) * 3 * BLK)
                    p, ps = _softmax_sink(sc_t[keys], valid, _sink_of(sink_ref, kh, half, 1), 0)
                    delta = jnp.sum(p * dp_t[keys], axis=0, keepdims=True)
                    p_t.append(p.astype(BF16))
                    ds_t.append((p * (dp_t[keys] - delta)).astype(BF16))
                    dsk = -ps * delta
                    for pair in range(2):
                        h = GROUP * kh + 2 * pair + half
                        part = jnp.sum(dsk[:, pair * BLK:(pair + 1) * BLK], axis=1, keepdims=True)
                        dsink_ref[h:h + 1, :] += jnp.broadcast_to(part, (1, 128))
                p_t = jnp.concatenate(p_t, axis=0)
                ds_t = jnp.concatenate(ds_t, axis=0)
                dq2 = _dot_tn(ds_t, kbd) * SCORE_SCALE
                dq_ref[rows, (2 * kh) * 128:(2 * kh + 1) * 128] = dq2[:BLK]
                dq_ref[rows, (2 * kh + 1) * 128:(2 * kh + 2) * 128] = dq2[BLK:]
                dk_fold.append(_fold_diag(_dot_nn(ds_t, q2)) * SCORE_SCALE)
                dv_fold.append(_fold_diag(_dot_nn(p_t, do2)))
            dk_all = jnp.where(lo, dk_fold[0], dk_fold[1])
            dv_all = jnp.where(lo, dv_fold[0], dv_fold[1])
            for t in range(3):
                dk_ref[j, t] = dk_all[t * BLK:(t + 1) * BLK]
                dv_ref[j, t] = dv_all[t * BLK:(t + 1) * BLK]
            ext, dpe = pall[band], dpall[band]
            dpc_cur = dpall[(j + 1) * BLK:(j + 2) * BLK]
            var = _variant_index(n, nb)
            for gi in range(POOL_G):
                gsl = slice(gi * POOL_GW, (gi + 1) * POOL_GW)
                wg = pw_ref[gi].astype(BF16)
                sc = ps_ref[:, gsl]
                dgb = _dot_nn(pb_ref[var, gi], ext[:, gsl]).astype(BF16)
                yg = _dot_nn(dgb, wg)
                dps_ref[:, gsl] += jnp.sum(dpc_cur[:, gsl] * yg, axis=0, keepdims=True)
                dpw_ref[gi] += _dot_tn(dgb, (dpc_cur[:, gsl] * sc).astype(BF16))
                dd = _dot_nt((dpe[:, gsl] * sc).astype(BF16), wg)
                dpc_ref[rows, gsl] = _dot_nn(ptb_ref[var, gi], dd.astype(BF16))

    fixed = lambda shape: pl.BlockSpec(shape, lambda m: (0,) * len(shape))
    return _call(
        body, name, (nb // nq,),
        [pl.BlockSpec(memory_space=pltpu.SMEM), pl.BlockSpec((nq * BLK, ATTN_W), lambda m: (m, 0)),
         *_band_specs(nq, nb, KV_W), *_band_specs(nq, nb, KV_W), *_band_specs(nq, nb, POOL_W),
         pl.BlockSpec((nq * BLK, ATTN_W), lambda m: (m, 0)), *_band_specs(nq, nb, POOL_W, 1),
         _whole((POOL_G, POOL_GW, POOL_GW)), _whole((1, POOL_W)), _whole(pband.shape), _whole(ptband.shape)],
        [pl.BlockSpec((nq * BLK, ATTN_W), lambda m: (m, 0)),
         pl.BlockSpec((nq, 3, BLK, KV_W), lambda m: (m, 0, 0, 0)),
         pl.BlockSpec((nq, 3, BLK, KV_W), lambda m: (m, 0, 0, 0)),
         pl.BlockSpec((nq * BLK, POOL_W), lambda m: (m, 0)),
         fixed((N_HEADS, 128)), fixed((POOL_G, POOL_GW, POOL_GW)), fixed((1, POOL_W))],
        [jax.ShapeDtypeStruct((s, ATTN_W), F32), jax.ShapeDtypeStruct((nb, 3, BLK, KV_W), F32),
         jax.ShapeDtypeStruct((nb, 3, BLK, KV_W), F32), jax.ShapeDtypeStruct((s, POOL_W), F32),
         jax.ShapeDtypeStruct((N_HEADS, 128), F32),
         jax.ShapeDtypeStruct((POOL_G, POOL_GW, POOL_GW), F32), jax.ShapeDtypeStruct((1, POOL_W), F32)],
        (sink, q, k, k, k, v, v, v, pc, pc, pc, dmix, dmix, dmix, dmix, pool_w, pool_scale, pband, ptband),
        ("arbitrary",), carry)


def _mix_out(h, mix, w_out, name, tm, scale=1.0, carry=None, middle=CARRY_MIDDLE):
    s, d = h.shape
    w = mix.shape[1]

    def body(h_ref, m_ref, w_ref, o_ref):
        o_ref[...] = h_ref[...] + scale * _dot_nn(m_ref[...], w_ref[...])

    row = lambda c: pl.BlockSpec((tm, c), lambda i: (i, 0))
    return _call(body, name, (s // tm,), [row(d), row(w), _whole((w, d))], [row(d)],
                 [jax.ShapeDtypeStruct((s, d), F32)], (h, mix, w_out), ("arbitrary",), carry, middle)


def _mix_out_bwd(dh, w_out, name, tm):
    s, d = dh.shape
    w = w_out.shape[0]

    def body(dh_ref, w_ref, o_ref, dhb_ref):
        dhb = dh_ref[...].astype(BF16)
        dhb_ref[...] = dhb
        o_ref[...] = _dot_nt(dhb, w_ref[...])

    row = lambda c: pl.BlockSpec((tm, c), lambda i: (i, 0))
    return pl.pallas_call(
        body, name=name, grid=(s // tm,), in_specs=[row(d), _whole((w, d))], out_specs=[row(w), row(d)],
        out_shape=[jax.ShapeDtypeStruct((s, w), F32), jax.ShapeDtypeStruct((s, d), BF16)],
        compiler_params=_cparams(("arbitrary",)),
    )(dh, w_out)


def _mix_in_bwd(h, dh, g, win_t, dq, dkp, dvp, dpc, tabs, name, tm, carry=None):
    s, d = h.shape
    nb = s // BLK
    nt = tm // BLK
    n_in = win_t.shape[0]

    def band_sum(n, before, own, after, prev_last, next_first):
        lo = (n > 0).astype(F32)
        hi = (n < s // tm - 1).astype(F32)
        blocks = []
        for b in range(nt):
            from_prev = prev_last[...] * lo if b == 0 else before[b - 1]
            from_next = next_first[...] * hi if b == nt - 1 else after[b + 1]
            blocks.append(from_prev + own[b] + from_next)
        return jnp.concatenate(blocks, axis=0)

    def body(h_ref, dh_ref, g_ref, w_ref, dq_ref, k2, k1, k0, kp, kn, v2, v1, v0, vp, vn, dpc_ref, c_ref, sa_ref,
             sb_ref, o_ref, du_ref, dg_ref):
        n = pl.program_id(0)
        dk = band_sum(n, k2, k1, k0, kp, kn)
        dv = band_sum(n, v2, v1, v0, vp, vn)
        c, sa, sb = c_ref[...], sa_ref[...], sb_ref[...]
        du = jnp.concatenate([_rope_bwd(dq_ref[...], c, sa, sb, ATTN_W // 128), _rope_bwd(dk, c, sa, sb, 1), dv,
                              dpc_ref[...]], axis=1)
        du_ref[...] = du.T.astype(BF16)
        dn = _dot_nn(du.astype(BF16), w_ref[...])
        x = h_ref[...]
        r = lax.rsqrt(jnp.mean(x * x, axis=-1, keepdims=True) + EPS)
        xhat = x * r
        dxhat = dn * g_ref[...]
        o_ref[...] = dh_ref[...] + r * (dxhat - xhat * jnp.mean(dxhat * xhat, axis=-1, keepdims=True))

        @pl.when(n == 0)
        def _():
            dg_ref[...] = jnp.zeros_like(dg_ref)

        dg_ref[...] += jnp.sum(dn * xhat, axis=0, keepdims=True)

    row = lambda w: pl.BlockSpec((tm, w), lambda n: (n, 0))
    slot = lambda t: pl.BlockSpec((nt, None, BLK, KV_W), lambda n, t=t: (n, t, 0, 0))
    parts = [slot(2), slot(1), slot(0),
             pl.BlockSpec((None, None, BLK, KV_W), lambda n: (jnp.maximum(nt * n - 1, 0), 2, 0, 0)),
             pl.BlockSpec((None, None, BLK, KV_W), lambda n: (jnp.minimum(nt * n + nt, nb - 1), 0, 0, 0))]
    return _call(
        body, name, (s // tm,),
        [row(d), row(d), _whole((1, d)), _whole((n_in, d)), row(ATTN_W), *parts, *parts, row(POOL_W),
         row(128), row(128), row(128)],
        [row(d), pl.BlockSpec((n_in, tm), lambda n: (0, n)), pl.BlockSpec((1, d), lambda n: (0, 0))],
        [jax.ShapeDtypeStruct((s, d), F32), jax.ShapeDtypeStruct((n_in, s), BF16), jax.ShapeDtypeStruct((1, d), F32)],
        (h, dh, g, win_t, dq, *[dkp] * 5, *[dvp] * 5, dpc, *tabs), ("arbitrary",), carry)


def _adam_math(w, g, m, v):
    m = ADAM_B1 * m + (1.0 - ADAM_B1) * g
    v = ADAM_B2 * v + (1.0 - ADAM_B2) * (g * g)
    m_hat = m / (1.0 - ADAM_B1 ** ADAM_STEP)
    v_hat = v / (1.0 - ADAM_B2 ** ADAM_STEP)
    delta = -ADAM_LR * (m_hat / (jnp.sqrt(v_hat) + ADAM_EPS) + ADAM_WD * w)
    return delta, m, v


def _adam_small(w, parts, late, m, v, name):
    rows, cols = w.shape

    def body(w_ref, p_ref, l_ref, m_ref, v_ref, g_ref, d_ref, nm_ref, nv_ref):
        g, gl = p_ref[0], l_ref[0]
        for k in range(1, N_DEV):
            g = g + p_ref[k]
            gl = gl + l_ref[k]
        g_ref[...] = g
        g_ref[SMALL_NORM1:SMALL_NORM1 + 8, :] = g[SMALL_NORM1:SMALL_NORM1 + 8] + gl
        d_ref[...], nm_ref[...], nv_ref[...] = _adam_math(w_ref[...], g_ref[...], m_ref[...], v_ref[...])

    return pl.pallas_call(
        body, name=name, out_shape=[jax.ShapeDtypeStruct((rows, cols), F32)] * 4,
    )(w, parts, late, m, v)


SMALL_NORM1 = 512


def _pack_small(norm1, normm, norm2, normf, sink, pool_w, pool_scale, loss=None):
    scale_rows = jnp.pad(pool_scale.reshape(4, 128), ((0, 4), (0, 0)))
    last_rows = jnp.pad(sink.reshape(1, N_HEADS), ((0, 7), (0, 128 - N_HEADS)))
    if loss is not None:
        last_rows = last_rows + jnp.pad(loss.reshape(1, 1), ((1, 6), (0, 127)))
    return jnp.concatenate([pool_w.reshape(512, 128), norm1.reshape(8, 128), normm.reshape(8, 128),
                            norm2.reshape(8, 128), normf.reshape(8, 128), scale_rows, last_rows], axis=0)


def _unpack_small(p):
    return dict(pool_w=p[:512].reshape(1, POOL_G, POOL_GW, POOL_GW), ffn1_norm=p[512:520].reshape(1, 1024),
                mix_norm=p[520:528].reshape(1, 1024), ffn2_norm=p[528:536].reshape(1, 1024),
                final_norm=p[536:544].reshape(1024), pool_scale=p[544:548].reshape(1, POOL_W),
                sink_logits=p[552, :N_HEADS].reshape(1, N_HEADS), loss=p[553, 0])


def kernel(x, ffn1_norm, ffn1_w_gate, ffn1_w_up, ffn1_w_down, mix_norm, w_in, sink_logits, pool_w, pool_scale, w_out, ffn2_norm, ffn2_w_gate, ffn2_w_up, ffn2_w_down, final_norm, loss_target, m_ffn1_norm, m_ffn1_w_gate, m_ffn1_w_up, m_ffn1_w_down, m_mix_norm, m_w_in, m_sink_logits, m_pool_w, m_pool_scale, m_w_out, m_ffn2_norm, m_ffn2_w_gate, m_ffn2_w_up, m_ffn2_w_down, m_final_norm, v_ffn1_norm, v_ffn1_w_gate, v_ffn1_w_up, v_ffn1_w_down, v_mix_norm, v_w_in, v_sink_logits, v_pool_w, v_pool_scale, v_w_out, v_ffn2_norm, v_ffn2_w_gate, v_ffn2_w_up, v_ffn2_w_down, v_final_norm):
    s, d = x.shape[1], x.shape[2]
    tm = min(512, s)
    tm_bwd = min(256, s)
    pos = jnp.stack([lax.axis_index("x"), lax.axis_index("y"), lax.axis_index("c")]).astype(jnp.int32)

    t_bf = lambda w: w[0].T.astype(BF16)
    full = lambda a: a.reshape(N_DEV * a.shape[1], d)
    (wg1,) = map(full, _run_exchange(_AllGather([t_bf(ffn1_w_gate)]), "gather_ffn1_gate"))

    tabs = _rope_tables(s)
    pband, ptband = _pool_tables(s)
    g1, gm, g2, gf = ffn1_norm, mix_norm, ffn2_norm, final_norm.reshape(1, d)

    x0 = x[0]
    n1, gate1, wu1 = _ffn_gate(x0, g1, wg1, "ffn1_gate", tm, carry=_AllGather([t_bf(ffn1_w_up)]))
    up1, act1, act1_t, wd1 = _ffn_up(n1, gate1, full(wu1), "ffn1_up", tm,
                                     carry=_AllGather([ffn1_w_down[0].astype(BF16)]))
    wu1, wd1 = full(wu1), full(wd1)
    h1, win_t, wout = _mix_out(x0, act1, wd1, "ffn1_down", tm, scale=0.5,
                               carry=_AllGather([t_bf(w_in), w_out[0].astype(BF16)]))
    win_t, wout = full(win_t), full(wout)
    q, k, v, pc, n2, wg2 = _mix_in(h1, gm, win_t, tabs, "mix_in", tm, carry=_AllGather([t_bf(ffn2_w_gate)]))
    wg2 = full(wg2)
    mix, mix_t, *gathered = _attn_pool_fwd(q, k, v, pc, sink_logits, pool_w[0], pool_scale, pband, "attn_pool_fwd",
                                           carry=_AllGather([t_bf(ffn2_w_up), ffn2_w_down[0].astype(BF16)]))
    wu2, wd2 = map(full, gathered)
    (h2,) = _mix_out(h1, mix, wout, "mix_out", tm)
    dh3, n3, gate2, up2, act2_t, loss_part, dgf = _ffn_fwd(h2, g2, wg2, wu2, wd2, "ffn2_fwd", tm,
                                                           head=(gf, loss_target[0]))

    sum1, recv2 = {}, {}

    def stage2(keys):
        return _RsStage2([sum1[key][1] for key in keys])

    dh2, dg2, dgate2_t, dup2_t, dhh3 = _ffn_bwd(h2, dh3, gate2, up2, g2, wg2, wu2, wd2, "ffn2_bwd", tm_bwd)
    sum1["g2"] = _wgrad_rs1(pos, dgate2_t, n3, "wgrad_gate2")
    sum1["u2"] = _wgrad_rs1(pos, dup2_t, n3, "wgrad_up2")
    sum1["d2"] = _wgrad_rs1(pos, act2_t, dhh3, "wgrad_down2")
    dmix, dh2b = _mix_out_bwd(dh2, wout, "mix_out_bwd", tm)
    sum1["out"] = _wgrad_rs1(pos, mix_t, dh2b, "wgrad_out")
    dq, dkp, dvp, dpc, dsink, dpw, dps, *r2 = _attn_pool_bwd(
        q, k, v, pc, dmix, sink_logits, pool_w[0], pool_scale, pband, ptband, "attn_pool_bwd",
        carry=stage2(["g2", "u2", "d2"]))
    recv2.update(zip(["g2", "u2", "d2"], r2))
    dh1, du_t, dgm, recv2["out"] = _mix_in_bwd(h1, dh2, gm, win_t, dq, dkp, dvp, dpc, tabs, "mix_in_bwd", tm,
                                               carry=stage2(["out"]))
    sum1["in"] = _wgrad_rs1(pos, du_t, n2, "wgrad_in")
    small_part = _pack_small(jnp.zeros_like(dgm), dgm, dg2, dgf, dsink[:, 0], dpw, dps, loss_part)
    dgate1, dup1, dgate1_t, dup1_t, dhh1, recv2["in"], small_all = _ffn_bwd_gates(
        dh1, gate1, up1, wd1, "ffn1_bwd_gates", tm, carry=_Both(stage2(["in"]), _AllGather([small_part])))
    sum1["g1"] = _wgrad_rs1(pos, dgate1_t, n1, "wgrad_gate1")
    *sum1["u1"], recv2["g1"] = _wgrad_rs1(pos, dup1_t, n1, "wgrad_up1", carry=stage2(["g1"]))
    *sum1["d1"], recv2["u1"] = _wgrad_rs1(pos, act1_t, dhh1, "wgrad_down1", carry=stage2(["u1"]))
    dx, dg1, recv2["d1"] = _ffn_bwd_input(x0, dh1, dgate1, dup1, g1, wg1, wu1, "ffn1_bwd_input", tm,
                                          carry=stage2(["d1"]))

    (dg1_all,) = _run_exchange(_DirectGather([dg1.reshape(8, 128)]), "gather_norm1_grad")
    pk = lambda a, b, c_, e, s_, pw_, psc: _pack_small(a, b, c_, e, s_[0], pw_[0], psc)
    small_w = pk(ffn1_norm, mix_norm, ffn2_norm, final_norm, sink_logits, pool_w, pool_scale)
    small_m = pk(m_ffn1_norm, m_mix_norm, m_ffn2_norm, m_final_norm, m_sink_logits, m_pool_w, m_pool_scale)
    small_v = pk(v_ffn1_norm, v_mix_norm, v_ffn2_norm, v_final_norm, v_sink_logits, v_pool_w, v_pool_scale)
    sg, sd, sm, sv = [_unpack_small(a)
                      for a in _adam_small(small_w, small_all, dg1_all, small_m, small_v, "adam_small")]

    big = {}
    keys = ["g1", "u1", "d1", "g2", "u2", "d2", "in", "out"]
    names = ["ffn1_w_gate", "ffn1_w_up", "ffn1_w_down", "ffn2_w_gate", "ffn2_w_up", "ffn2_w_down", "w_in", "w_out"]
    transposed = [True, True, False, True, True, False, True, False]
    ws = [ffn1_w_gate, ffn1_w_up, ffn1_w_down, ffn2_w_gate, ffn2_w_up, ffn2_w_down, w_in, w_out]
    ms = [m_ffn1_w_gate, m_ffn1_w_up, m_ffn1_w_down, m_ffn2_w_gate, m_ffn2_w_up, m_ffn2_w_down, m_w_in, m_w_out]
    vs = [v_ffn1_w_gate, v_ffn1_w_up, v_ffn1_w_down, v_ffn2_w_gate, v_ffn2_w_up, v_ffn2_w_down, v_w_in, v_w_out]
    for key, nm, tr, w, m, vv in zip(keys, names, transposed, ws, ms, vs):
        view = (lambda a: jnp.swapaxes(a, 1, 2)[0]) if tr else (lambda a: a[0])
        back = (lambda a: jnp.swapaxes(a[None], 1, 2)) if tr else (lambda a: a[None])
        res = _rs_sum2_adam(pos, sum1[key][0], recv2[key], view(w), view(m), view(vv), "adam_" + nm)
        big[nm] = tuple(back(a) for a in res)

    loss = sg["loss"]
    all_names = ["ffn1_norm", "ffn1_w_gate", "ffn1_w_up", "ffn1_w_down", "mix_norm", "w_in", "sink_logits", "pool_w",
                 "pool_scale", "w_out", "ffn2_norm", "ffn2_w_gate", "ffn2_w_up", "ffn2_w_down", "final_norm"]
    outs = [loss, dx[None]]
    for idx, src in enumerate((sg, sd, sm, sv)):
        for nm in all_names:
            outs.append(big[nm][idx] if nm in big else src[nm])
    return tuple(outs)
```

```python
import functools

import jax
import jax.numpy as jnp
import numpy as np
from jax import lax
from jax.experimental import pallas as pl
from jax.experimental.pallas import tpu as pltpu

F32 = jnp.float32
BF16 = jnp.bfloat16
MESH = pl.DeviceIdType.MESH
N_DEV = 8

EPS = 1e-6
HEAD_DIM = 64
N_HEADS = 8
N_KV = 2
GROUP = N_HEADS // N_KV
ATTN_W = N_HEADS * HEAD_DIM
KV_W = N_KV * HEAD_DIM
POOL_W = 512
POOL_G = 4
POOL_GW = POOL_W // POOL_G
POOL_WINDOWS = (2, 4, 8, 16)
BLK = 128
ROT = 16
ROPE_THETA = 500000.0
SCORE_SCALE = HEAD_DIM ** -0.5

ADAM_LR, ADAM_B1, ADAM_B2, ADAM_EPS, ADAM_WD, ADAM_STEP = 0.001, 0.9, 0.999, 1e-08, 0.01, 10

VMEM_LIMIT = 56 * 1024 * 1024


def _cparams(sem=None, **kw):
    if sem is not None:
        kw["dimension_semantics"] = sem
    return pltpu.CompilerParams(vmem_limit_bytes=VMEM_LIMIT, **kw)


def _whole(shape):
    nd = len(shape)
    return pl.BlockSpec(shape, lambda *_: (0,) * nd, pipeline_mode=pl.Buffered(1))


def _sigmoid(z):
    return 1.0 / (1.0 + jnp.exp(-z))


def _dot_nt(a, b):
    return lax.dot_general(a, b, (((1,), (1,)), ((), ())), preferred_element_type=F32)


def _dot_nn(a, b):
    return lax.dot_general(a, b, (((1,), (0,)), ((), ())), preferred_element_type=F32)


def _dot_tn(a, b):
    return lax.dot_general(a, b, (((0,), (0,)), ((), ())), preferred_element_type=F32)


def _rope_tables(s):
    inv_freq = ROPE_THETA ** (-np.arange(0, ROT, 2, dtype=np.float64) / ROT)
    ang = np.arange(s, dtype=np.float64)[:, None] * inv_freq[None, :]
    c = np.ones((s, HEAD_DIM)); sa = np.zeros((s, HEAD_DIM)); sb = np.zeros((s, HEAD_DIM))
    c[:, :8] = np.cos(ang); c[:, 8:16] = np.cos(ang)
    sa[:, :8] = -np.sin(ang)
    sb[:, 8:16] = np.sin(ang)
    t = lambda a: jnp.asarray(np.tile(a, (1, 2)).astype(np.float32))
    return t(c), t(sa), t(sb)


def _pool_weight(gi, t, s_pos, s):
    half = POOL_WINDOWS[gi] // 2

    def win(lo, hi):
        a = np.clip(lo, 0, s); b = np.clip(hi + 1, 0, s)
        inside = (s_pos >= a) & (s_pos < b)
        return inside / np.maximum(b - a, 1)

    w = 0.5 * (win(t - half, t + half - 1) + win(t - half + 1, t + half)) - (t == s_pos)
    return w * ((t >= 0) & (t < s) & (s_pos >= 0) & (s_pos < s))


def _pool_tables(s):
    nb = s // BLK
    fwd = np.zeros((3, POOL_G, BLK, 3 * BLK), np.float32)
    bwd = np.zeros((3, POOL_G, BLK, 3 * BLK), np.float32)
    for vi, n in enumerate((0, 1 if nb > 2 else 0, nb - 1)):
        i = n * BLK + np.arange(BLK)[:, None]
        j = (n - 1) * BLK + np.arange(3 * BLK)[None, :]
        for gi in range(POOL_G):
            fwd[vi, gi] = _pool_weight(gi, i, j, s)
            bwd[vi, gi] = _pool_weight(gi, j, i, s)
    return jnp.asarray(fwd, dtype=BF16), jnp.asarray(bwd, dtype=BF16)


def _variant_index(n, nb):
    return jnp.where(n == 0, 0, jnp.where(n == nb - 1, 2, 1))


class _Exchange:
    inputs = ()
    out_shapes = ()
    sems = ()

    def start(self, srcs, outs, sems):
        raise NotImplementedError

    def middle(self, srcs, outs, sems):
        pass

    def finish(self, srcs, outs, sems):
        raise NotImplementedError


class _AllGather(_Exchange):
    def __init__(self, arrays):
        n = len(arrays)
        self.inputs = list(arrays)
        self.out_shapes = [jax.ShapeDtypeStruct((N_DEV,) + a.shape, a.dtype) for a in arrays]
        self.sems = [pltpu.SemaphoreType.DMA((n, 8)), pltpu.SemaphoreType.DMA((n, 8)), pltpu.SemaphoreType.DMA((n,))]

    def _parts(self, srcs, outs, sems):
        send_sems, recv_sems, local_sems = sems
        n = len(srcs)
        x, y, c = lax.axis_index("x"), lax.axis_index("y"), lax.axis_index("c")
        me, sibling, xn, yn, diag = (x, y, c), (x, y, 1 - c), (1 - x, y, c), (x, 1 - y, c), (1 - x, 1 - y, c)

        def place(a, dev, half=None):
            block = outs[a].at[4 * dev[0] + 2 * dev[1] + dev[2]]
            if half is None:
                return block
            r2 = outs[a].shape[1] // 2
            return block.at[pl.ds(half * r2, r2)]

        def copy(a, k, dev, to, half=None, src=None):
            where = place(a, dev, half)
            return pltpu.make_async_remote_copy(
                src_ref=where if src is None else src, dst_ref=where, send_sem=send_sems.at[a, k],
                recv_sem=recv_sems.at[a, k], device_id=to, device_id_type=MESH)

        def other(dev):
            return (dev[0], dev[1], 1 - dev[2])

        class Parts:
            mine = staticmethod(lambda: [pltpu.make_async_copy(srcs[a], place(a, me), local_sems.at[a])
                                         for a in range(n)])
            own = staticmethod(lambda: [copy(a, k, me, to, src=srcs[a]) for a in range(n)
                                        for k, to in ((0, sibling), (1, xn), (2, yn))])
            relay = staticmethod(lambda a: [copy(a, 3, xn, yn, half=0), copy(a, 4, yn, xn, half=1),
                                            copy(a, 5, xn, sibling), copy(a, 6, yn, sibling)])
            last = staticmethod(lambda a: copy(a, 7, diag, sibling))
            from_x = staticmethod(lambda a: copy(a, 1, xn, me))
            from_y = staticmethod(lambda a: copy(a, 2, yn, me))
            diag_halves = staticmethod(lambda a: [copy(a, 3, diag, me, half=0), copy(a, 4, diag, me, half=1)])
            from_sibling = staticmethod(lambda a: [copy(a, 0, sibling, me), copy(a, 5, other(xn), me),
                                                   copy(a, 6, other(yn), me), copy(a, 7, other(diag), me)])

        return n, Parts

    def start(self, srcs, outs, sems):
        _, p = self._parts(srcs, outs, sems)
        for cp in p.mine() + p.own():
            cp.start()

    def middle(self, srcs, outs, sems):
        n, p = self._parts(srcs, outs, sems)
        for a in range(n):
            p.from_x(a).wait_recv()
            p.from_y(a).wait_recv()
            for cp in p.relay(a):
                cp.start()

    def finish(self, srcs, outs, sems):
        n, p = self._parts(srcs, outs, sems)
        for a in range(n):
            for cp in p.diag_halves(a):
                cp.wait_recv()
            p.last(a).start()
        for a in range(n):
            for cp in p.from_sibling(a):
                cp.wait_recv()
        for cp in p.own() + [cp for a in range(n) for cp in p.relay(a) + [p.last(a)]]:
            cp.wait_send()
        for cp in p.mine():
            cp.wait()


class _RsStage2(_Exchange):
    def start(self, srcs, outs, sems):
        for cp in self._copies(srcs, outs, sems):
            cp.start()

    def finish(self, srcs, outs, sems):
        copies = self._copies(srcs, outs, sems)
        for cp in copies:
            cp.wait_recv()
        for cp in copies:
            cp.wait_send()


    def __init__(self, pbs):
        n = len(pbs)
        self.inputs = list(pbs)
        self.out_shapes = [jax.ShapeDtypeStruct((3,) + p.shape[1:], p.dtype) for p in pbs]
        self.sems = [pltpu.SemaphoreType.DMA((n, 3)), pltpu.SemaphoreType.DMA((n, 3))]

    def _copies(self, srcs, outs, sems):
        send_sems, recv_sems = sems
        x, y, c = lax.axis_index("x"), lax.axis_index("y"), lax.axis_index("c")
        chips = [(1 - x, y), (x, 1 - y), (1 - x, 1 - y)]
        return [pltpu.make_async_remote_copy(
            src_ref=srcs[a].at[2 * chip[0] + chip[1]], dst_ref=outs[a].at[j], send_sem=send_sems.at[a, j],
            recv_sem=recv_sems.at[a, j], device_id=(*chip, c), device_id_type=MESH)
            for a in range(len(srcs)) for j, chip in enumerate(chips)]


class _DirectGather(_Exchange):
    def __init__(self, arrays):
        n = len(arrays)
        self.inputs = list(arrays)
        self.out_shapes = [jax.ShapeDtypeStruct((N_DEV,) + a.shape, a.dtype) for a in arrays]
        self.sems = [pltpu.SemaphoreType.DMA((n, 7)), pltpu.SemaphoreType.DMA((n, 7)), pltpu.SemaphoreType.DMA((n,))]

    def _copies(self, srcs, outs, sems):
        send_sems, recv_sems, local_sems = sems
        x, y, c = lax.axis_index("x"), lax.axis_index("y"), lax.axis_index("c")
        me = 4 * x + 2 * y + c
        remote, local = [], []
        for a in range(len(srcs)):
            local.append(pltpu.make_async_copy(srcs[a], outs[a].at[me], local_sems.at[a]))
            for k in range(1, N_DEV):
                peer = (x ^ (k >> 2), y ^ ((k >> 1) & 1), c ^ (k & 1))
                remote.append(pltpu.make_async_remote_copy(
                    src_ref=srcs[a], dst_ref=outs[a].at[me], send_sem=send_sems.at[a, k - 1],
                    recv_sem=recv_sems.at[a, k - 1], device_id=peer, device_id_type=MESH))
        return remote, local

    def start(self, srcs, outs, sems):
        remote, local = self._copies(srcs, outs, sems)
        for cp in local + remote:
            cp.start()

    def finish(self, srcs, outs, sems):
        remote, local = self._copies(srcs, outs, sems)
        for cp in remote:
            cp.wait_recv()
        for cp in remote:
            cp.wait_send()
        for cp in local:
            cp.wait()


class _Both(_Exchange):
    def __init__(self, a, b):
        self.a, self.b = a, b
        self.inputs = list(a.inputs) + list(b.inputs)
        self.out_shapes = list(a.out_shapes) + list(b.out_shapes)
        self.sems = list(a.sems) + list(b.sems)

    def _split(self, srcs, outs, sems):
        na, oa, sa = len(self.a.inputs), len(self.a.out_shapes), len(self.a.sems)
        return (srcs[:na], outs[:oa], sems[:sa]), (srcs[na:], outs[oa:], sems[sa:])

    def start(self, srcs, outs, sems):
        pa, pb = self._split(srcs, outs, sems)
        self.a.start(*pa)
        self.b.start(*pb)

    def middle(self, srcs, outs, sems):
        pa, pb = self._split(srcs, outs, sems)
        self.a.middle(*pa)
        self.b.middle(*pb)

    def finish(self, srcs, outs, sems):
        pa, pb = self._split(srcs, outs, sems)
        self.a.finish(*pa)
        self.b.finish(*pb)


_ANY = pl.BlockSpec(memory_space=pl.ANY)


def _run_exchange(ex, name):
    n_in, n_out = len(ex.inputs), len(ex.out_shapes)

    def body(*refs):
        srcs, outs, sems = refs[:n_in], refs[n_in:n_in + n_out], refs[n_in + n_out:]
        ex.start(srcs, outs, sems)
        ex.middle(srcs, outs, sems)
        ex.finish(srcs, outs, sems)

    return pl.pallas_call(
        body, name=name, out_shape=list(ex.out_shapes), in_specs=[_ANY] * n_in, out_specs=[_ANY] * n_out,
        scratch_shapes=list(ex.sems),
    )(*ex.inputs)


CARRY_MIDDLE = 0.7


def _call(body, name, grid, in_specs, out_specs, out_shape, args, sem, carry=None, middle=CARRY_MIDDLE, scratch=()):
    if carry is None:
        return pl.pallas_call(functools.partial(body), name=name, grid=grid, in_specs=in_specs, out_specs=out_specs,
                              out_shape=out_shape, scratch_shapes=list(scratch), compiler_params=_cparams(sem))(*args)
    n_in, n_out = len(in_specs), len(out_specs)
    nc_in, nc_out = len(carry.inputs), len(carry.out_shapes)

    def carried(*refs):
        ins = refs[:n_in]
        c_in = refs[n_in:n_in + nc_in]
        outs = refs[n_in + nc_in:n_in + nc_in + n_out]
        c_out = refs[n_in + nc_in + n_out:n_in + nc_in + n_out + nc_out]
        own = refs[n_in + nc_in + n_out + nc_out:n_in + nc_in + n_out + nc_out + len(scratch)]
        sems = refs[n_in + nc_in + n_out + nc_out + len(scratch):]
        ids = [pl.program_id(i) for i in range(len(grid))]
        is_first = functools.reduce(jnp.logical_and, [i == 0 for i in ids])
        is_last = functools.reduce(jnp.logical_and, [i == g - 1 for i, g in zip(ids, grid)])
        @pl.when(is_first)
        def _():
            carry.start(c_in, c_out, sems)

        if middle is not None:
            @pl.when(functools.reduce(jnp.logical_and, [ids[0] == round(middle * (grid[0] - 1))]
                                      + [i == 0 for i in ids[1:]]))
            def _():
                carry.middle(c_in, c_out, sems)

        body(*ins, *outs, *own)

        @pl.when(is_last)
        def _():
            if middle is None:
                carry.middle(c_in, c_out, sems)
            carry.finish(c_in, c_out, sems)

    return pl.pallas_call(
        carried, name=name, grid=grid, in_specs=list(in_specs) + [_ANY] * nc_in,
        out_specs=list(out_specs) + [_ANY] * nc_out, out_shape=list(out_shape) + list(carry.out_shapes),
        scratch_shapes=list(scratch) + list(carry.sems), compiler_params=_cparams(sem))(*args, *carry.inputs)


def _rs_sum2_adam(pos, p, r2, w, m, v, name):
    _, rows, d = p.shape
    tr = rows // 2 if rows % 16 == 0 else rows

    def body(pos_ref, p_ref, r_ref, w_ref, m_ref, v_ref, g_ref, d_ref, nm_ref, nv_ref):
        r = r_ref[...].astype(F32)
        g = ((p_ref[...] + r[0]) + r[1]) + r[2]
        g_ref[...] = g
        d_ref[...], nm_ref[...], nv_ref[...] = _adam_math(w_ref[...], g, m_ref[...], v_ref[...])

    blk = pl.BlockSpec((tr, d), lambda i, pos_ref: (i, 0))
    return pl.pallas_call(
        body, name=name,
        grid_spec=pltpu.PrefetchScalarGridSpec(
            num_scalar_prefetch=1, grid=(rows // tr,),
            in_specs=[pl.BlockSpec((None, tr, d), lambda i, pos_ref: (2 * pos_ref[0] + pos_ref[1], i, 0)),
                      pl.BlockSpec((3, tr, d), lambda i, pos_ref: (0, i, 0)), blk, blk, blk],
            out_specs=[blk] * 4),
        out_shape=[jax.ShapeDtypeStruct((rows, d), F32)] * 4,
        compiler_params=_cparams(("arbitrary",)),
    )(pos, p, r2, w, m, v)


def _ffn_chunk(f):
    for cand in (256, 128):
        if f % cand == 0:
            return cand
    return f


def _loss_head(x, gg, target, loss_ref, dg_ref):
    @pl.when(pl.program_id(0) == 0)
    def _():
        loss_ref[...] = jnp.zeros_like(loss_ref)
        dg_ref[...] = jnp.zeros_like(dg_ref)

    r = lax.rsqrt(jnp.mean(x * x, axis=-1, keepdims=True) + EPS)
    xhat = x * r
    e = xhat * gg - target
    loss_ref[...] += 0.5 * jnp.sum(jnp.mean(e * e, axis=-1, keepdims=True), axis=0, keepdims=True)
    dy = e * (1.0 / x.shape[-1])
    dg_ref[...] += jnp.sum(dy * xhat, axis=0, keepdims=True)
    dxhat = dy * gg
    return r * (dxhat - xhat * jnp.mean(dxhat * xhat, axis=-1, keepdims=True))


def _ffn_gate(h, g, wg_t, name, tm, carry=None):
    s, d = h.shape
    f = wg_t.shape[0]

    def body(h_ref, g_ref, wg_ref, n_ref, gate_ref):
        x = h_ref[...]
        r = lax.rsqrt(jnp.mean(x * x, axis=-1, keepdims=True) + EPS)
        nb = (x * r * g_ref[...]).astype(BF16)
        n_ref[...] = nb
        gate_ref[...] = _dot_nt(nb, wg_ref[...]).astype(BF16)

    row = lambda w: pl.BlockSpec((tm, w), lambda i: (i, 0))
    return _call(body, name, (s // tm,), [row(d), _whole((1, d)), _whole((f, d))], [row(d), row(f)],
                 [jax.ShapeDtypeStruct((s, d), BF16), jax.ShapeDtypeStruct((s, f), BF16)], (h, g, wg_t),
                 ("arbitrary",), carry)


def _ffn_up(n, gate, wu_t, name, tm, carry=None):
    s, d = n.shape
    f = wu_t.shape[0]
    tf = _ffn_chunk(f)

    def body(n_ref, gate_ref, wu_ref, up_ref, act_ref, act_t_ref):
        nb = n_ref[...]
        for j in range(f // tf):
            sl = slice(j * tf, (j + 1) * tf)
            up = _dot_nt(nb, wu_ref[sl, :])
            gate = gate_ref[:, sl].astype(F32)
            up_ref[:, sl] = up.astype(BF16)
            act = (gate * _sigmoid(gate) * up).astype(BF16)
            act_ref[:, sl] = act
            act_t_ref[sl, :] = act.T

    row = lambda w: pl.BlockSpec((tm, w), lambda i: (i, 0))
    return _call(body, name, (s // tm,), [row(d), row(f), _whole((f, d))],
                 [row(f), row(f), pl.BlockSpec((f, tm), lambda i: (0, i))],
                 [jax.ShapeDtypeStruct((s, f), BF16)] * 2 + [jax.ShapeDtypeStruct((f, s), BF16)], (n, gate, wu_t),
                 ("arbitrary",), carry)


def _ffn_fwd(h, g, wg_t, wu_t, wd, name, tm, carry=None, head=None, middle=CARRY_MIDDLE):
    s, d = h.shape
    f = wg_t.shape[0]
    tf = _ffn_chunk(f)

    def body(h_ref, g_ref, wg_ref, wu_ref, wd_ref, *refs):
        if head is None:
            o_ref, n_ref, gate_ref, up_ref, act_t_ref, act_ref = refs
        else:
            gf_ref, t_ref, o_ref, n_ref, gate_ref, up_ref, act_t_ref, loss_ref, dgf_ref, act_ref = refs
        x = h_ref[...]
        r = lax.rsqrt(jnp.mean(x * x, axis=-1, keepdims=True) + EPS)
        nb = (x * r * g_ref[...]).astype(BF16)
        n_ref[...] = nb
        for j in range(f // tf):
            sl = slice(j * tf, (j + 1) * tf)
            gate = _dot_nt(nb, wg_ref[sl, :])
            up = _dot_nt(nb, wu_ref[sl, :])
            gate_ref[:, sl] = gate.astype(BF16)
            up_ref[:, sl] = up.astype(BF16)
            act = gate * _sigmoid(gate) * up
            act = act.astype(BF16)
            act_ref[:, sl] = act
            act_t_ref[sl, :] = act.T
        h_out = x + 0.5 * _dot_nn(act_ref[...], wd_ref[...])
        o_ref[...] = h_out if head is None else _loss_head(h_out, gf_ref[...], t_ref[...], loss_ref, dgf_ref)

    row = lambda w: pl.BlockSpec((tm, w), lambda i: (i, 0))
    in_specs = [row(d), _whole((1, d)), _whole((f, d)), _whole((f, d)), _whole((f, d))]
    out_specs = [row(d), row(d), row(f), row(f), pl.BlockSpec((f, tm), lambda i: (0, i))]
    out_shape = ([jax.ShapeDtypeStruct((s, d), F32), jax.ShapeDtypeStruct((s, d), BF16)]
                 + [jax.ShapeDtypeStruct((s, f), BF16)] * 2 + [jax.ShapeDtypeStruct((f, s), BF16)])
    args = (h, g, wg_t, wu_t, wd)
    scratch = (pltpu.VMEM((tm, f), BF16),)
    if head is not None:
        in_specs += [_whole((1, d)), row(d)]
        out_specs += [pl.BlockSpec((1, 1), lambda i: (0, 0)), pl.BlockSpec((1, d), lambda i: (0, 0))]
        out_shape += [jax.ShapeDtypeStruct((1, 1), F32), jax.ShapeDtypeStruct((1, d), F32)]
        args += tuple(head)
    return _call(body, name, (s // tm,), in_specs, out_specs, out_shape, args, ("arbitrary",), carry, middle, scratch)


def _gate_grads(dh_ref, gate_ref, up_ref, wd_ref, dgate_ref, dup_ref, dgate_t_ref, dup_t_ref, dhh_ref, tf):
    dhh = (0.5 * dh_ref[...]).astype(BF16)
    dhh_ref[...] = dhh
    for j in range(gate_ref.shape[1] // tf):
        sl = slice(j * tf, (j + 1) * tf)
        gt = gate_ref[:, sl].astype(F32)
        u = up_ref[:, sl].astype(F32)
        dact = _dot_nt(dhh, wd_ref[sl, :])
        sg = _sigmoid(gt)
        dup = dact * (gt * sg)
        dgate = dact * u * (sg * (1.0 + gt * (1.0 - sg)))
        dup, dgate = dup.astype(BF16), dgate.astype(BF16)
        dup_ref[:, sl] = dup
        dgate_ref[:, sl] = dgate
        dup_t_ref[sl, :] = dup.T
        dgate_t_ref[sl, :] = dgate.T


def _input_grad(h_ref, dh_ref, dgate_ref, dup_ref, g_ref, wg_ref, wu_ref, o_ref, dg_ref):
    x = h_ref[...]
    r = lax.rsqrt(jnp.mean(x * x, axis=-1, keepdims=True) + EPS)
    xhat = x * r
    dn = _dot_nn(dgate_ref[...], wg_ref[...]) + _dot_nn(dup_ref[...], wu_ref[...])
    dxhat = dn * g_ref[...]
    o_ref[...] = dh_ref[...] + r * (dxhat - xhat * jnp.mean(dxhat * xhat, axis=-1, keepdims=True))

    @pl.when(pl.program_id(0) == 0)
    def _():
        dg_ref[...] = jnp.zeros_like(dg_ref)

    dg_ref[...] += jnp.sum(dn * xhat, axis=0, keepdims=True)


def _ffn_bwd(h_in, dh_out, gate, up, g, wg_t, wu_t, wd, name, tm):
    s, d = h_in.shape
    f = gate.shape[1]
    tf = _ffn_chunk(f)

    def body(h_ref, dh_ref, gate_ref, up_ref, g_ref, wg_ref, wu_ref, wd_ref,
             o_ref, dg_ref, dgate_t_ref, dup_t_ref, dhh_ref, dgate_ref, dup_ref):
        _gate_grads(dh_ref, gate_ref, up_ref, wd_ref, dgate_ref, dup_ref, dgate_t_ref, dup_t_ref, dhh_ref, tf)
        _input_grad(h_ref, dh_ref, dgate_ref, dup_ref, g_ref, wg_ref, wu_ref, o_ref, dg_ref)

    row = lambda w: pl.BlockSpec((tm, w), lambda i: (i, 0))
    col = pl.BlockSpec((f, tm), lambda i: (0, i))
    return pl.pallas_call(
        body, name=name, grid=(s // tm,),
        in_specs=[row(d), row(d), row(f), row(f), _whole((1, d)), _whole((f, d)), _whole((f, d)), _whole((f, d))],
        out_specs=[row(d), pl.BlockSpec((1, d), lambda i: (0, 0)), col, col, row(d)],
        out_shape=[jax.ShapeDtypeStruct((s, d), F32), jax.ShapeDtypeStruct((1, d), F32),
                   jax.ShapeDtypeStruct((f, s), BF16), jax.ShapeDtypeStruct((f, s), BF16),
                   jax.ShapeDtypeStruct((s, d), BF16)],
        scratch_shapes=[pltpu.VMEM((tm, f), BF16), pltpu.VMEM((tm, f), BF16)],
        compiler_params=_cparams(("arbitrary",)),
    )(h_in, dh_out, gate, up, g, wg_t, wu_t, wd)


def _ffn_bwd_gates(dh_out, gate, up, wd, name, tm, carry=None):
    s, d = dh_out.shape
    f = gate.shape[1]
    tf = _ffn_chunk(f)

    def body(dh_ref, gate_ref, up_ref, wd_ref, dgate_ref, dup_ref, dgate_t_ref, dup_t_ref, dhh_ref):
        _gate_grads(dh_ref, gate_ref, up_ref, wd_ref, dgate_ref, dup_ref, dgate_t_ref, dup_t_ref, dhh_ref, tf)

    row = lambda w: pl.BlockSpec((tm, w), lambda i: (i, 0))
    col = pl.BlockSpec((f, tm), lambda i: (0, i))
    return _call(
        body, name, (s // tm,), [row(d), row(f), row(f), _whole((f, d))], [row(f), row(f), col, col, row(d)],
        [jax.ShapeDtypeStruct((s, f), BF16)] * 2 + [jax.ShapeDtypeStruct((f, s), BF16)] * 2
        + [jax.ShapeDtypeStruct((s, d), BF16)],
        (dh_out, gate, up, wd), ("arbitrary",), carry)


def _ffn_bwd_input(h_in, dh_out, dgate, dup, g, wg_t, wu_t, name, tm, carry=None):
    s, d = h_in.shape
    f = dgate.shape[1]

    row = lambda w: pl.BlockSpec((tm, w), lambda i: (i, 0))
    return _call(
        _input_grad, name, (s // tm,),
        [row(d), row(d), row(f), row(f), _whole((1, d)), _whole((f, d)), _whole((f, d))],
        [row(d), pl.BlockSpec((1, d), lambda i: (0, 0))],
        [jax.ShapeDtypeStruct((s, d), F32), jax.ShapeDtypeStruct((1, d), F32)],
        (h_in, dh_out, dgate, dup, g, wg_t, wu_t), ("arbitrary",), carry)


def _wgrad_rs1(pos, a_t, b, name, carry=None):
    f, s = a_t.shape
    d = b.shape[1]
    fk = f // N_DEV
    nc_in = 0 if carry is None else len(carry.inputs)
    nc_out = 0 if carry is None else len(carry.out_shapes)

    def body(pos_ref, a_ref, b_ref, *refs):
        c_in = refs[:nc_in]
        p_ref, pb_ref = refs[nc_in:nc_in + 2]
        c_out = refs[nc_in + 2:nc_in + 2 + nc_out]
        stage, land, send_sems, recv_sems = refs[nc_in + 2 + nc_out:nc_in + 6 + nc_out]
        c_sems = refs[nc_in + 6 + nc_out:]
        t = pl.program_id(0)
        q = t % 4
        x, y, c = lax.axis_index("x"), lax.axis_index("y"), lax.axis_index("c")

        def push(k):
            return pltpu.make_async_remote_copy(src_ref=stage.at[k], dst_ref=land.at[k], send_sem=send_sems.at[k],
                                                recv_sem=recv_sems.at[k], device_id=(x, y, 1 - c), device_id_type=MESH)

        if carry is not None:
            @pl.when(t == 0)
            def _():
                carry.start(c_in, c_out, c_sems)

        g = _dot_nn(a_ref[...], b_ref[...])

        @pl.when(t < 4)
        def _():
            stage[q] = g.astype(BF16)
            push(q).start()

        @pl.when(t >= 4)
        def _():
            push(q).wait_recv()
            p = g + land[q].astype(F32)
            p_ref[...] = p
            pb_ref[...] = p.astype(BF16)

        @pl.when(t == 7)
        def _():
            for k in range(4):
                push(k).wait_send()
            if carry is not None:
                carry.middle(c_in, c_out, c_sems)
                carry.finish(c_in, c_out, c_sems)

    def shard(t, pos_ref):
        return 4 * ((t % 4) // 2) + 2 * (t % 2) + jnp.where(t < 4, 1 - pos_ref[2], pos_ref[2])

    out = pl.BlockSpec((None, fk, d), lambda t, pos_ref: (jnp.maximum(t - 4, 0), 0, 0))
    return pl.pallas_call(
        body, name=name,
        grid_spec=pltpu.PrefetchScalarGridSpec(
            num_scalar_prefetch=1, grid=(8,),
            in_specs=[pl.BlockSpec((fk, s), lambda t, pos_ref: (shard(t, pos_ref), 0)),
                      pl.BlockSpec((s, d), lambda t, pos_ref: (0, 0), pipeline_mode=pl.Buffered(1))]
            + [_ANY] * nc_in,
            out_specs=[out, out] + [_ANY] * nc_out,
            scratch_shapes=[pltpu.VMEM((4, fk, d), BF16), pltpu.VMEM((4, fk, d), BF16),
                            pltpu.SemaphoreType.DMA((4,)), pltpu.SemaphoreType.DMA((4,))]
            + ([] if carry is None else list(carry.sems))),
        out_shape=[jax.ShapeDtypeStruct((4, fk, d), F32), jax.ShapeDtypeStruct((4, fk, d), BF16)]
        + ([] if carry is None else list(carry.out_shapes)),
        compiler_params=_cparams(("arbitrary",)),
    )(pos, a_t, b, *([] if carry is None else carry.inputs))


def _rope(t, c, sa, sb, reps):
    c, sa, sb = (jnp.tile(v, (1, reps)) if reps > 1 else v for v in (c, sa, sb))
    w = t.shape[1]
    return t * c + pltpu.roll(t, w - 8, 1) * sa + pltpu.roll(t, 8, 1) * sb


def _rope_bwd(dt, c, sa, sb, reps):
    c, sa, sb = (jnp.tile(v, (1, reps)) if reps > 1 else v for v in (c, sa, sb))
    w = dt.shape[1]
    return dt * c + pltpu.roll(dt * sa, 8, 1) + pltpu.roll(dt * sb, w - 8, 1)


def _mix_in(h, g, win_t, tabs, name, tm, carry=None):
    s, d = h.shape
    n_in = win_t.shape[0]

    def body(h_ref, g_ref, w_ref, c_ref, sa_ref, sb_ref, q_ref, k_ref, v_ref, pc_ref, n_ref):
        x = h_ref[...]
        r = lax.rsqrt(jnp.mean(x * x, axis=-1, keepdims=True) + EPS)
        nb = (x * r * g_ref[...]).astype(BF16)
        n_ref[...] = nb
        u = _dot_nt(nb, w_ref[...])
        c, sa, sb = c_ref[...], sa_ref[...], sb_ref[...]
        q_ref[...] = _rope(u[:, :ATTN_W], c, sa, sb, ATTN_W // 128).astype(BF16)
        k_ref[...] = _rope(u[:, ATTN_W:ATTN_W + KV_W], c, sa, sb, 1).astype(BF16)
        v_ref[...] = u[:, ATTN_W + KV_W:ATTN_W + 2 * KV_W].astype(BF16)
        pc_ref[...] = u[:, ATTN_W + 2 * KV_W:]

    row = lambda w: pl.BlockSpec((tm, w), lambda i: (i, 0))
    return _call(
        body, name, (s // tm,),
        [row(d), _whole((1, d)), _whole((n_in, d)), row(128), row(128), row(128)],
        [row(ATTN_W), row(KV_W), row(KV_W), row(POOL_W), row(d)],
        [jax.ShapeDtypeStruct((s, ATTN_W), BF16), jax.ShapeDtypeStruct((s, KV_W), BF16),
         jax.ShapeDtypeStruct((s, KV_W), BF16), jax.ShapeDtypeStruct((s, POOL_W), F32),
         jax.ShapeDtypeStruct((s, d), BF16)],
        (h, g, win_t, *tabs), ("arbitrary",), carry)


def _band_mask(n, nb, transposed):
    shape = (3 * BLK, 2 * BLK) if transposed else (2 * BLK, 3 * BLK)
    i = lax.broadcasted_iota(jnp.int32, shape, 1 if transposed else 0) % BLK
    j = lax.broadcasted_iota(jnp.int32, shape, 0 if transposed else 1)
    kpos = (n - 1) * BLK + j
    return (j >= i) & (j <= i + 2 * BLK) & (kpos >= 0) & (kpos < nb * BLK)


def _block_diag(t, kh):
    tf = t.astype(F32)
    tr = pltpu.roll(tf, HEAD_DIM, 1)
    lo = lax.broadcasted_iota(jnp.int32, tf.shape, 1) < HEAD_DIM
    top, bot = (tf, tr) if kh == 0 else (tr, tf)
    return jnp.concatenate([jnp.where(lo, top, 0.0), jnp.where(lo, 0.0, bot)], axis=0).astype(BF16)


def _fold_diag(tbd):
    lo = lax.broadcasted_iota(jnp.int32, (3 * BLK, 2 * HEAD_DIM), 1) < HEAD_DIM
    t = jnp.where(lo, tbd[:3 * BLK], tbd[3 * BLK:])
    return t + pltpu.roll(t, HEAD_DIM, 1)


def _stack_pairs(x, kh):
    return jnp.concatenate([x[:, (2 * kh) * 128:(2 * kh + 1) * 128], x[:, (2 * kh + 1) * 128:(2 * kh + 2) * 128]], axis=0)


def _sink_of(sink_ref, kh, half, axis):
    shape = (2 * BLK, 1) if axis == 0 else (1, 2 * BLK)
    first = lax.broadcasted_iota(jnp.int32, shape, axis) < BLK
    return jnp.where(first, sink_ref[0, GROUP * kh + half], sink_ref[0, GROUP * kh + 2 + half])


def _softmax_sink(sc, valid, sink, axis):
    sc = jnp.where(valid, sc, -1e30)
    m = jnp.maximum(jnp.max(sc, axis=axis, keepdims=True), sink)
    e = jnp.exp(sc - m)
    es = jnp.exp(sink - m)
    inv = 1.0 / (jnp.sum(e, axis=axis, keepdims=True) + es)
    return e * inv, es * inv


def _attn_blocks_per_step(nb):
    return next(nq for nq in (8, 4, 2, 1) if nb % nq == 0)


def _band_specs(nq, nb, w, col=0):
    return [pl.BlockSpec((BLK, w), lambda m: (jnp.maximum(nq * m - 1, 0), col)),
            pl.BlockSpec((nq * BLK, w), lambda m: (m, col)),
            pl.BlockSpec((BLK, w), lambda m: (jnp.minimum(nq * m + nq, nb - 1), col))]


def _attn_pool_fwd(q, k, v, pc, sink, pool_w, pool_scale, pband, name, carry=None, middle=CARRY_MIDDLE):
    s = q.shape[0]
    nb = s // BLK

    nq = _attn_blocks_per_step(nb)

    def body(sink_ref, q_ref, k0, k1, k2, v0, v1, v2, p0, p1, p2, pw_ref, ps_ref, pb_ref, o_ref, o_t_ref):
        kall = jnp.concatenate([k0[...], k1[...], k2[...]], axis=0)
        vall = jnp.concatenate([v0[...], v1[...], v2[...]], axis=0)
        pall = jnp.concatenate([p0[...], p1[...], p2[...]], axis=0).astype(BF16)
        qall = q_ref[...] * SCORE_SCALE
        for j in range(nq):
            n = pl.program_id(0) * nq + j
            rows, band = slice(j * BLK, (j + 1) * BLK), slice(j * BLK, (j + 3) * BLK)
            valid = _band_mask(n, nb, False)
            kb, vb, qs = kall[band], vall[band], qall[rows]
            for kh in range(N_KV):
                sc = _dot_nt(_stack_pairs(qs, kh), _block_diag(kb, kh))
                p = [_softmax_sink(sc[:, half * 3 * BLK:(half + 1) * 3 * BLK], valid,
                                   _sink_of(sink_ref, kh, half, 0), 1)[0] for half in range(2)]
                o2 = _dot_nn(jnp.concatenate(p, axis=1).astype(BF16), _block_diag(vb, kh)).astype(BF16)
                o_ref[rows, (2 * kh) * 128:(2 * kh + 1) * 128] = o2[:BLK]
                o_ref[rows, (2 * kh + 1) * 128:(2 * kh + 2) * 128] = o2[BLK:]
            ext = pall[band]
            var = _variant_index(n, nb)
            for gi in range(POOL_G):
                gsl = slice(gi * POOL_GW, (gi + 1) * POOL_GW)
                dg = _dot_nn(pb_ref[var, gi], ext[:, gsl])
                yg = _dot_nn(dg.astype(BF16), pw_ref[gi].astype(BF16))
                o_ref[rows, ATTN_W + gi * POOL_GW:ATTN_W + (gi + 1) * POOL_GW] = (yg * ps_ref[:, gsl]).astype(BF16)
        o_t_ref[...] = o_ref[...].astype(F32).T.astype(BF16)

    return _call(
        body, name, (nb // nq,),
        [pl.BlockSpec(memory_space=pltpu.SMEM), pl.BlockSpec((nq * BLK, ATTN_W), lambda m: (m, 0)),
         *_band_specs(nq, nb, KV_W), *_band_specs(nq, nb, KV_W), *_band_specs(nq, nb, POOL_W),
         _whole((POOL_G, POOL_GW, POOL_GW)), _whole((1, POOL_W)), _whole(pband.shape)],
        [pl.BlockSpec((nq * BLK, ATTN_W + POOL_W), lambda m: (m, 0)),
         pl.BlockSpec((ATTN_W + POOL_W, nq * BLK), lambda m: (0, m))],
        [jax.ShapeDtypeStruct((s, ATTN_W + POOL_W), BF16), jax.ShapeDtypeStruct((ATTN_W + POOL_W, s), BF16)],
        (sink, q, k, k, k, v, v, v, pc, pc, pc, pool_w, pool_scale, pband), ("arbitrary",), carry, middle)


def _attn_pool_bwd(q, k, v, pc, dmix, sink, pool_w, pool_scale, pband, ptband, name, carry=None):
    s = q.shape[0]
    nb = s // BLK
    nq = _attn_blocks_per_step(nb)

    def body(sink_ref, q_ref, k0, k1, k2, v0, v1, v2, p0, p1, p2, da_ref, d0, d1, d2, pw_ref, ps_ref, pb_ref, ptb_ref,
             dq_ref, dk_ref, dv_ref, dpc_ref, dsink_ref, dpw_ref, dps_ref):
        @pl.when(pl.program_id(0) == 0)
        def _():
            dsink_ref[...] = jnp.zeros_like(dsink_ref)
            dpw_ref[...] = jnp.zeros_like(dpw_ref)
            dps_ref[...] = jnp.zeros_like(dps_ref)

        kall = jnp.concatenate([k0[...], k1[...], k2[...]], axis=0)
        vall = jnp.concatenate([v0[...], v1[...], v2[...]], axis=0)
        pall = jnp.concatenate([p0[...], p1[...], p2[...]], axis=0).astype(BF16)
        dpall = jnp.concatenate([d0[...], d1[...], d2[...]], axis=0)
        lo = lax.broadcasted_iota(jnp.int32, (3 * BLK, KV_W), 1) < HEAD_DIM
        for j in range(nq):
            n = pl.program_id(0) * nq + j
            rows, band = slice(j * BLK, (j + 1) * BLK), slice(j * BLK, (j + 3) * BLK)
            valid = _band_mask(n, nb, True)
            kb, vb, qb = kall[band], vall[band], q_ref[rows, :]
            qs = qb * SCORE_SCALE
            da = da_ref[rows, :].astype(BF16)
            dk_fold, dv_fold = [], []
            for kh in range(N_KV):
                kbd, vbd = _block_diag(kb, kh), _block_diag(vb, kh)
                q2, do2 = _stack_pairs(qb, kh), _stack_pairs(da, kh)
                sc_t = _dot_nt(kbd, _stack_pairs(qs, kh))
                dp_t = _dot_nt(vbd, do2)
                p_t, ds_t = [], []
                for half in range(2):
                    keys = slice(half * 3 * BLK, (half + 1) * 3 * BLK)
                    p, ps = _softmax_sink(sc_t[keys], valid, _sink_of(sink_ref, kh, half, 1), 0)
                    delta = jnp.sum(p * dp_t[keys], axis=0, keepdims=True)
                    p_t.append(p.astype(BF16))
                    ds_t.append((p * (dp_t[keys] - delta)).astype(BF16))
                    dsk = -ps * delta
                    for pair in range(2):
                        h = GROUP * kh + 2 * pair + half
                        part = jnp.sum(dsk[:, pair * BLK:(pair + 1) * BLK], axis=1, keepdims=True)
                        dsink_ref[h:h + 1, :] += jnp.broadcast_to(part, (1, 128))
                p_t = jnp.concatenate(p_t, axis=0)
                ds_t = jnp.concatenate(ds_t, axis=0)
                dq2 = _dot_tn(ds_t, kbd) * SCORE_SCALE
                dq_ref[rows, (2 * kh) * 128:(2 * kh + 1) * 128] = dq2[:BLK]
                dq_ref[rows, (2 * kh + 1) * 128:(2 * kh + 2) * 128] = dq2[BLK:]
                dk_fold.append(_fold_diag(_dot_nn(ds_t, q2)) * SCORE_SCALE)
                dv_fold.append(_fold_diag(_dot_nn(p_t, do2)))
            dk_all = jnp.where(lo, dk_fold[0], dk_fold[1])
            dv_all = jnp.where(lo, dv_fold[0], dv_fold[1])
            for t in range(3):
                dk_ref[j, t] = dk_all[t * BLK:(t + 1) * BLK]
                dv_ref[j, t] = dv_all[t * BLK:(t + 1) * BLK]
            ext, dpe = pall[band], dpall[band]
            dpc_cur = dpall[(j + 1) * BLK:(j + 2) * BLK]
            var = _variant_index(n, nb)
            for gi in range(POOL_G):
                gsl = slice(gi * POOL_GW, (gi + 1) * POOL_GW)
                wg = pw_ref[gi].astype(BF16)
                sc = ps_ref[:, gsl]
                dgb = _dot_nn(pb_ref[var, gi], ext[:, gsl]).astype(BF16)
                yg = _dot_nn(dgb, wg)
                dps_ref[:, gsl] += jnp.sum(dpc_cur[:, gsl] * yg, axis=0, keepdims=True)
                dpw_ref[gi] += _dot_tn(dgb, (dpc_cur[:, gsl] * sc).astype(BF16))
                dd = _dot_nt((dpe[:, gsl] * sc).astype(BF16), wg)
                dpc_ref[rows, gsl] = _dot_nn(ptb_ref[var, gi], dd.astype(BF16))

    fixed = lambda shape: pl.BlockSpec(shape, lambda m: (0,) * len(shape))
    return _call(
        body, name, (nb // nq,),
        [pl.BlockSpec(memory_space=pltpu.SMEM), pl.BlockSpec((nq * BLK, ATTN_W), lambda m: (m, 0)),
         *_band_specs(nq, nb, KV_W), *_band_specs(nq, nb, KV_W), *_band_specs(nq, nb, POOL_W),
         pl.BlockSpec((nq * BLK, ATTN_W), lambda m: (m, 0)), *_band_specs(nq, nb, POOL_W, 1),
         _whole((POOL_G, POOL_GW, POOL_GW)), _whole((1, POOL_W)), _whole(pband.shape), _whole(ptband.shape)],
        [pl.BlockSpec((nq * BLK, ATTN_W), lambda m: (m, 0)),
         pl.BlockSpec((nq, 3, BLK, KV_W), lambda m: (m, 0, 0, 0)),
         pl.BlockSpec((nq, 3, BLK, KV_W), lambda m: (m, 0, 0, 0)),
         pl.BlockSpec((nq * BLK, POOL_W), lambda m: (m, 0)),
         fixed((N_HEADS, 128)), fixed((POOL_G, POOL_GW, POOL_GW)), fixed((1, POOL_W))],
        [jax.ShapeDtypeStruct((s, ATTN_W), F32), jax.ShapeDtypeStruct((nb, 3, BLK, KV_W), F32),
         jax.ShapeDtypeStruct((nb, 3, BLK, KV_W), F32), jax.ShapeDtypeStruct((s, POOL_W), F32),
         jax.ShapeDtypeStruct((N_HEADS, 128), F32),
         jax.ShapeDtypeStruct((POOL_G, POOL_GW, POOL_GW), F32), jax.ShapeDtypeStruct((1, POOL_W), F32)],
        (sink, q, k, k, k, v, v, v, pc, pc, pc, dmix, dmix, dmix, dmix, pool_w, pool_scale, pband, ptband),
        ("arbitrary",), carry)


def _mix_out(h, mix, w_out, name, tm, scale=1.0, carry=None, middle=CARRY_MIDDLE):
    s, d = h.shape
    w = mix.shape[1]

    def body(h_ref, m_ref, w_ref, o_ref):
        o_ref[...] = h_ref[...] + scale * _dot_nn(m_ref[...], w_ref[...])

    row = lambda c: pl.BlockSpec((tm, c), lambda i: (i, 0))
    return _call(body, name, (s // tm,), [row(d), row(w), _whole((w, d))], [row(d)],
                 [jax.ShapeDtypeStruct((s, d), F32)], (h, mix, w_out), ("arbitrary",), carry, middle)


def _mix_out_bwd(dh, w_out, name, tm):
    s, d = dh.shape
    w = w_out.shape[0]

    def body(dh_ref, w_ref, o_ref, dhb_ref):
        dhb = dh_ref[...].astype(BF16)
        dhb_ref[...] = dhb
        o_ref[...] = _dot_nt(dhb, w_ref[...])

    row = lambda c: pl.BlockSpec((tm, c), lambda i: (i, 0))
    return pl.pallas_call(
        body, name=name, grid=(s // tm,), in_specs=[row(d), _whole((w, d))], out_specs=[row(w), row(d)],
        out_shape=[jax.ShapeDtypeStruct((s, w), F32), jax.ShapeDtypeStruct((s, d), BF16)],
        compiler_params=_cparams(("arbitrary",)),
    )(dh, w_out)


def _mix_in_bwd(h, dh, g, win_t, dq, dkp, dvp, dpc, tabs, name, tm, carry=None):
    s, d = h.shape
    nb = s // BLK
    nt = tm // BLK
    n_in = win_t.shape[0]

    def band_sum(n, before, own, after, prev_last, next_first):
        lo = (n > 0).astype(F32)
        hi = (n < s // tm - 1).astype(F32)
        blocks = []
        for b in range(nt):
            from_prev = prev_last[...] * lo if b == 0 else before[b - 1]
            from_next = next_first[...] * hi if b == nt - 1 else after[b + 1]
            blocks.append(from_prev + own[b] + from_next)
        return jnp.concatenate(blocks, axis=0)

    def body(h_ref, dh_ref, g_ref, w_ref, dq_ref, k2, k1, k0, kp, kn, v2, v1, v0, vp, vn, dpc_ref, c_ref, sa_ref,
             sb_ref, o_ref, du_ref, dg_ref):
        n = pl.program_id(0)
        dk = band_sum(n, k2, k1, k0, kp, kn)
        dv = band_sum(n, v2, v1, v0, vp, vn)
        c, sa, sb = c_ref[...], sa_ref[...], sb_ref[...]
        du = jnp.concatenate([_rope_bwd(dq_ref[...], c, sa, sb, ATTN_W // 128), _rope_bwd(dk, c, sa, sb, 1), dv,
                              dpc_ref[...]], axis=1)
        du_ref[...] = du.T.astype(BF16)
        dn = _dot_nn(du.astype(BF16), w_ref[...])
        x = h_ref[...]
        r = lax.rsqrt(jnp.mean(x * x, axis=-1, keepdims=True) + EPS)
        xhat = x * r
        dxhat = dn * g_ref[...]
        o_ref[...] = dh_ref[...] + r * (dxhat - xhat * jnp.mean(dxhat * xhat, axis=-1, keepdims=True))

        @pl.when(n == 0)
        def _():
            dg_ref[...] = jnp.zeros_like(dg_ref)

        dg_ref[...] += jnp.sum(dn * xhat, axis=0, keepdims=True)

    row = lambda w: pl.BlockSpec((tm, w), lambda n: (n, 0))
    slot = lambda t: pl.BlockSpec((nt, None, BLK, KV_W), lambda n, t=t: (n, t, 0, 0))
    parts = [slot(2), slot(1), slot(0),
             pl.BlockSpec((None, None, BLK, KV_W), lambda n: (jnp.maximum(nt * n - 1, 0), 2, 0, 0)),
             pl.BlockSpec((None, None, BLK, KV_W), lambda n: (jnp.minimum(nt * n + nt, nb - 1), 0, 0, 0))]
    return _call(
        body, name, (s // tm,),
        [row(d), row(d), _whole((1, d)), _whole((n_in, d)), row(ATTN_W), *parts, *parts, row(POOL_W),
         row(128), row(128), row(128)],
        [row(d), pl.BlockSpec((n_in, tm), lambda n: (0, n)), pl.BlockSpec((1, d), lambda n: (0, 0))],
        [jax.ShapeDtypeStruct((s, d), F32), jax.ShapeDtypeStruct((n_in, s), BF16), jax.ShapeDtypeStruct((1, d), F32)],
        (h, dh, g, win_t, dq, *[dkp] * 5, *[dvp] * 5, dpc, *tabs), ("arbitrary",), carry)


def _adam_math(w, g, m, v):
    m = ADAM_B1 * m + (1.0 - ADAM_B1) * g
    v = ADAM_B2 * v + (1.0 - ADAM_B2) * (g * g)
    m_hat = m / (1.0 - ADAM_B1 ** ADAM_STEP)
    v_hat = v / (1.0 - ADAM_B2 ** ADAM_STEP)
    delta = -ADAM_LR * (m_hat / (jnp.sqrt(v_hat) + ADAM_EPS) + ADAM_WD * w)
    return delta, m, v


def _adam_small(w, parts, late, m, v, name):
    rows, cols = w.shape

    def body(w_ref, p_ref, l_ref, m_ref, v_ref, g_ref, d_ref, nm_ref, nv_ref):
        g, gl = p_ref[0], l_ref[0]
        for k in range(1, N_DEV):
            g = g + p_ref[k]
            gl = gl + l_ref[k]
        g_ref[...] = g
        g_ref[SMALL_NORM1:SMALL_NORM1 + 8, :] = g[SMALL_NORM1:SMALL_NORM1 + 8] + gl
        d_ref[...], nm_ref[...], nv_ref[...] = _adam_math(w_ref[...], g_ref[...], m_ref[...], v_ref[...])

    return pl.pallas_call(
        body, name=name, out_shape=[jax.ShapeDtypeStruct((rows, cols), F32)] * 4,
    )(w, parts, late, m, v)


SMALL_NORM1 = 512


def _pack_small(norm1, normm, norm2, normf, sink, pool_w, pool_scale, loss=None):
    scale_rows = jnp.pad(pool_scale.reshape(4, 128), ((0, 4), (0, 0)))
    last_rows = jnp.pad(sink.reshape(1, N_HEADS), ((0, 7), (0, 128 - N_HEADS)))
    if loss is not None:
        last_rows = last_rows + jnp.pad(loss.reshape(1, 1), ((1, 6), (0, 127)))
    return jnp.concatenate([pool_w.reshape(512, 128), norm1.reshape(8, 128), normm.reshape(8, 128),
                            norm2.reshape(8, 128), normf.reshape(8, 128), scale_rows, last_rows], axis=0)


def _unpack_small(p):
    return dict(pool_w=p[:512].reshape(1, POOL_G, POOL_GW, POOL_GW), ffn1_norm=p[512:520].reshape(1, 1024),
                mix_norm=p[520:528].reshape(1, 1024), ffn2_norm=p[528:536].reshape(1, 1024),
                final_norm=p[536:544].reshape(1024), pool_scale=p[544:548].reshape(1, POOL_W),
                sink_logits=p[552, :N_HEADS].reshape(1, N_HEADS), loss=p[553, 0])


def kernel(x, ffn1_norm, ffn1_w_gate, ffn1_w_up, ffn1_w_down, mix_norm, w_in, sink_logits, pool_w, pool_scale, w_out, ffn2_norm, ffn2_w_gate, ffn2_w_up, ffn2_w_down, final_norm, loss_target, m_ffn1_norm, m_ffn1_w_gate, m_ffn1_w_up, m_ffn1_w_down, m_mix_norm, m_w_in, m_sink_logits, m_pool_w, m_pool_scale, m_w_out, m_ffn2_norm, m_ffn2_w_gate, m_ffn2_w_up, m_ffn2_w_down, m_final_norm, v_ffn1_norm, v_ffn1_w_gate, v_ffn1_w_up, v_ffn1_w_down, v_mix_norm, v_w_in, v_sink_logits, v_pool_w, v_pool_scale, v_w_out, v_ffn2_norm, v_ffn2_w_gate, v_ffn2_w_up, v_ffn2_w_down, v_final_norm):
    s, d = x.shape[1], x.shape[2]
    tm = min(512, s)
    tm_bwd = min(256, s)
    pos = jnp.stack([lax.axis_index("x"), lax.axis_index("y"), lax.axis_index("c")]).astype(jnp.int32)

    t_bf = lambda w: w[0].T.astype(BF16)
    full = lambda a: a.reshape(N_DEV * a.shape[1], d)
    (wg1,) = map(full, _run_exchange(_AllGather([t_bf(ffn1_w_gate)]), "gather_ffn1_gate"))

    tabs = _rope_tables(s)
    pband, ptband = _pool_tables(s)
    g1, gm, g2, gf = ffn1_norm, mix_norm, ffn2_norm, final_norm.reshape(1, d)

    x0 = x[0]
    n1, gate1, wu1 = _ffn_gate(x0, g1, wg1, "ffn1_gate", tm, carry=_AllGather([t_bf(ffn1_w_up)]))
    up1, act1, act1_t, wd1 = _ffn_up(n1, gate1, full(wu1), "ffn1_up", tm,
                                     carry=_AllGather([ffn1_w_down[0].astype(BF16)]))
    wu1, wd1 = full(wu1), full(wd1)
    h1, win_t, wout = _mix_out(x0, act1, wd1, "ffn1_down", tm, scale=0.5,
                               carry=_AllGather([t_bf(w_in), w_out[0].astype(BF16)]))
    win_t, wout = full(win_t), full(wout)
    q, k, v, pc, n2, wg2 = _mix_in(h1, gm, win_t, tabs, "mix_in", tm, carry=_AllGather([t_bf(ffn2_w_gate)]))
    wg2 = full(wg2)
    mix, mix_t, *gathered = _attn_pool_fwd(q, k, v, pc, sink_logits, pool_w[0], pool_scale, pband, "attn_pool_fwd",
                                           carry=_AllGather([t_bf(ffn2_w_up), ffn2_w_down[0].astype(BF16)]))
    wu2, wd2 = map(full, gathered)
    (h2,) = _mix_out(h1, mix, wout, "mix_out", tm)
    dh3, n3, gate2, up2, act2_t, loss_part, dgf = _ffn_fwd(h2, g2, wg2, wu2, wd2, "ffn2_fwd", tm,
                                                           head=(gf, loss_target[0]))

    sum1, recv2 = {}, {}

    def stage2(keys):
        return _RsStage2([sum1[key][1] for key in keys])

    dh2, dg2, dgate2_t, dup2_t, dhh3 = _ffn_bwd(h2, dh3, gate2, up2, g2, wg2, wu2, wd2, "ffn2_bwd", tm_bwd)
    sum1["g2"] = _wgrad_rs1(pos, dgate2_t, n3, "wgrad_gate2")
    sum1["u2"] = _wgrad_rs1(pos, dup2_t, n3, "wgrad_up2")
    sum1["d2"] = _wgrad_rs1(pos, act2_t, dhh3, "wgrad_down2")
    dmix, dh2b = _mix_out_bwd(dh2, wout, "mix_out_bwd", tm)
    sum1["out"] = _wgrad_rs1(pos, mix_t, dh2b, "wgrad_out")
    dq, dkp, dvp, dpc, dsink, dpw, dps, *r2 = _attn_pool_bwd(
        q, k, v, pc, dmix, sink_logits, pool_w[0], pool_scale, pband, ptband, "attn_pool_bwd",
        carry=stage2(["g2", "u2", "d2"]))
    recv2.update(zip(["g2", "u2", "d2"], r2))
    dh1, du_t, dgm, recv2["out"] = _mix_in_bwd(h1, dh2, gm, win_t, dq, dkp, dvp, dpc, tabs, "mix_in_bwd", tm,
                                               carry=stage2(["out"]))
    sum1["in"] = _wgrad_rs1(pos, du_t, n2, "wgrad_in")
    small_part = _pack_small(jnp.zeros_like(dgm), dgm, dg2, dgf, dsink[:, 0], dpw, dps, loss_part)
    dgate1, dup1, dgate1_t, dup1_t, dhh1, recv2["in"], small_all = _ffn_bwd_gates(
        dh1, gate1, up1, wd1, "ffn1_bwd_gates", tm, carry=_Both(stage2(["in"]), _AllGather([small_part])))
    sum1["g1"] = _wgrad_rs1(pos, dgate1_t, n1, "wgrad_gate1")
    *sum1["u1"], recv2["g1"] = _wgrad_rs1(pos, dup1_t, n1, "wgrad_up1", carry=stage2(["g1"]))
    *sum1["d1"], recv2["u1"] = _wgrad_rs1(pos, act1_t, dhh1, "wgrad_down1", carry=stage2(["u1"]))
    dx, dg1, recv2["d1"] = _ffn_bwd_input(x0, dh1, dgate1, dup1, g1, wg1, wu1, "ffn1_bwd_input", tm,
                                          carry=stage2(["d1"]))

    (dg1_all,) = _run_exchange(_DirectGather([dg1.reshape(8, 128)]), "gather_norm1_grad")
    pk = lambda a, b, c_, e, s_, pw_, psc: _pack_small(a, b, c_, e, s_[0], pw_[0], psc)
    small_w = pk(ffn1_norm, mix_norm, ffn2_norm, final_norm, sink_logits, pool_w, pool_scale)
    small_m = pk(m_ffn1_norm, m_mix_norm, m_ffn2_norm, m_final_norm, m_sink_logits, m_pool_w, m_pool_scale)
    small_v = pk(v_ffn1_norm, v_mix_norm, v_ffn2_norm, v_final_norm, v_sink_logits, v_pool_w, v_pool_scale)
    sg, sd, sm, sv = [_unpack_small(a)
                      for a in _adam_small(small_w, small_all, dg1_all, small_m, small_v, "adam_small")]

    big = {}
    keys = ["g1", "u1", "d1", "g2", "u2", "d2", "in", "out"]
    names = ["ffn1_w_gate", "ffn1_w_up", "ffn1_w_down", "ffn2_w_gate", "ffn2_w_up", "ffn2_w_down", "w_in", "w_out"]
    transposed = [True, True, False, True, True, False, True, False]
    ws = [ffn1_w_gate, ffn1_w_up, ffn1_w_down, ffn2_w_gate, ffn2_w_up, ffn2_w_down, w_in, w_out]
    ms = [m_ffn1_w_gate, m_ffn1_w_up, m_ffn1_w_down, m_ffn2_w_gate, m_ffn2_w_up, m_ffn2_w_down, m_w_in, m_w_out]
    vs = [v_ffn1_w_gate, v_ffn1_w_up, v_ffn1_w_down, v_ffn2_w_gate, v_ffn2_w_up, v_ffn2_w_down, v_w_in, v_w_out]
    for key, nm, tr, w, m, vv in zip(keys, names, transposed, ws, ms, vs):
        view = (lambda a: jnp.swapaxes(a, 1, 2)[0]) if tr else (lambda a: a[0])
        back = (lambda a: jnp.swapaxes(a[None], 1, 2)) if tr else (lambda a: a[None])
        res = _rs_sum2_adam(pos, sum1[key][0], recv2[key], view(w), view(m), view(vv), "adam_" + nm)
        big[nm] = tuple(back(a) for a in res)

    loss = sg["loss"]
    all_names = ["ffn1_norm", "ffn1_w_gate", "ffn1_w_up", "ffn1_w_down", "mix_norm", "w_in", "sink_logits", "pool_w",
                 "pool_scale", "w_out", "ffn2_norm", "ffn2_w_gate", "ffn2_w_up", "ffn2_w_down", "final_norm"]
    outs = [loss, dx[None]]
    for idx, src in enumerate((sg, sd, sm, sv)):
        for nm in all_names:
            outs.append(big[nm][idx] if nm in big else src[nm])
    return tuple(outs)
```

```python
import functools

import jax
import jax.numpy as jnp
import numpy as np
from jax import lax
from jax.experimental import pallas as pl
from jax.experimental.pallas import tpu as pltpu

F32 = jnp.float32
BF16 = jnp.bfloat16
MESH = pl.DeviceIdType.MESH
N_DEV = 8

EPS = 1e-6
HEAD_DIM = 64
N_HEADS = 8
N_KV = 2
GROUP = N_HEADS // N_KV
ATTN_W = N_HEADS * HEAD_DIM
KV_W = N_KV * HEAD_DIM
POOL_W = 512
POOL_G = 4
POOL_GW = POOL_W // POOL_G
POOL_WINDOWS = (2, 4, 8, 16)
BLK = 128
ROT = 16
ROPE_THETA = 500000.0
SCORE_SCALE = HEAD_DIM ** -0.5

ADAM_LR, ADAM_B1, ADAM_B2, ADAM_EPS, ADAM_WD, ADAM_STEP = 0.001, 0.9, 0.999, 1e-08, 0.01, 10

VMEM_LIMIT = 56 * 1024 * 1024


def _cparams(sem=None, **kw):
    if sem is not None:
        kw["dimension_semantics"] = sem
    return pltpu.CompilerParams(vmem_limit_bytes=VMEM_LIMIT, **kw)


def _whole(shape):
    nd = len(shape)
    return pl.BlockSpec(shape, lambda *_: (0,) * nd, pipeline_mode=pl.Buffered(1))


def _sigmoid(z):
    return 1.0 / (1.0 + jnp.exp(-z))


def _dot_nt(a, b):
    return lax.dot_general(a, b, (((1,), (1,)), ((), ())), preferred_element_type=F32)


def _dot_nn(a, b):
    return lax.dot_general(a, b, (((1,), (0,)), ((), ())), preferred_element_type=F32)


def _dot_tn(a, b):
    return lax.dot_general(a, b, (((0,), (0,)), ((), ())), preferred_element_type=F32)


def _rope_tables(s):
    inv_freq = ROPE_THETA ** (-np.arange(0, ROT, 2, dtype=np.float64) / ROT)
    ang = np.arange(s, dtype=np.float64)[:, None] * inv_freq[None, :]
    c = np.ones((s, HEAD_DIM)); sa = np.zeros((s, HEAD_DIM)); sb = np.zeros((s, HEAD_DIM))
    c[:, :8] = np.cos(ang); c[:, 8:16] = np.cos(ang)
    sa[:, :8] = -np.sin(ang)
    sb[:, 8:16] = np.sin(ang)
    t = lambda a: jnp.asarray(np.tile(a, (1, 2)).astype(np.float32))
    return t(c), t(sa), t(sb)


def _pool_weight(gi, t, s_pos, s):
    half = POOL_WINDOWS[gi] // 2

    def win(lo, hi):
        a = np.clip(lo, 0, s); b = np.clip(hi + 1, 0, s)
        inside = (s_pos >= a) & (s_pos < b)
        return inside / np.maximum(b - a, 1)

    w = 0.5 * (win(t - half, t + half - 1) + win(t - half + 1, t + half)) - (t == s_pos)
    return w * ((t >= 0) & (t < s) & (s_pos >= 0) & (s_pos < s))


def _pool_tables(s):
    nb = s // BLK
    fwd = np.zeros((3, POOL_G, BLK, 3 * BLK), np.float32)
    bwd = np.zeros((3, POOL_G, BLK, 3 * BLK), np.float32)
    for vi, n in enumerate((0, 1 if nb > 2 else 0, nb - 1)):
        i = n * BLK + np.arange(BLK)[:, None]
        j = (n - 1) * BLK + np.arange(3 * BLK)[None, :]
        for gi in range(POOL_G):
            fwd[vi, gi] = _pool_weight(gi, i, j, s)
            bwd[vi, gi] = _pool_weight(gi, j, i, s)
    return jnp.asarray(fwd, dtype=BF16), jnp.asarray(bwd, dtype=BF16)


def _variant_index(n, nb):
    return jnp.where(n == 0, 0, jnp.where(n == nb - 1, 2, 1))


class _Exchange:
    inputs = ()
    out_shapes = ()
    sems = ()

    def start(self, srcs, outs, sems):
        raise NotImplementedError

    def middle(self, srcs, outs, sems):
        pass

    def finish(self, srcs, outs, sems):
        raise NotImplementedError


class _AllGather(_Exchange):
    def __init__(self, arrays):
        n = len(arrays)
        self.inputs = list(arrays)
        self.out_shapes = [jax.ShapeDtypeStruct((N_DEV,) + a.shape, a.dtype) for a in arrays]
        self.sems = [pltpu.SemaphoreType.DMA((n, 8)), pltpu.SemaphoreType.DMA((n, 8)), pltpu.SemaphoreType.DMA((n,))]

    def _parts(self, srcs, outs, sems):
        send_sems, recv_sems, local_sems = sems
        n = len(srcs)
        x, y, c = lax.axis_index("x"), lax.axis_index("y"), lax.axis_index("c")
        me, sibling, xn, yn, diag = (x, y, c), (x, y, 1 - c), (1 - x, y, c), (x, 1 - y, c), (1 - x, 1 - y, c)

        def place(a, dev, half=None):
            block = outs[a].at[4 * dev[0] + 2 * dev[1] + dev[2]]
            if half is None:
                return block
            r2 = outs[a].shape[1] // 2
            return block.at[pl.ds(half * r2, r2)]

        def copy(a, k, dev, to, half=None, src=None):
            where = place(a, dev, half)
            return pltpu.make_async_remote_copy(
                src_ref=where if src is None else src, dst_ref=where, send_sem=send_sems.at[a, k],
                recv_sem=recv_sems.at[a, k], device_id=to, device_id_type=MESH)

        def other(dev):
            return (dev[0], dev[1], 1 - dev[2])

        class Parts:
            mine = staticmethod(lambda: [pltpu.make_async_copy(srcs[a], place(a, me), local_sems.at[a])
                                         for a in range(n)])
            own = staticmethod(lambda: [copy(a, k, me, to, src=srcs[a]) for a in range(n)
                                        for k, to in ((0, sibling), (1, xn), (2, yn))])
            relay = staticmethod(lambda a: [copy(a, 3, xn, yn, half=0), copy(a, 4, yn, xn, half=1),
                                            copy(a, 5, xn, sibling), copy(a, 6, yn, sibling)])
            last = staticmethod(lambda a: copy(a, 7, diag, sibling))
            from_x = staticmethod(lambda a: copy(a, 1, xn, me))
            from_y = staticmethod(lambda a: copy(a, 2, yn, me))
            diag_halves = staticmethod(lambda a: [copy(a, 3, diag, me, half=0), copy(a, 4, diag, me, half=1)])
            from_sibling = staticmethod(lambda a: [copy(a, 0, sibling, me), copy(a, 5, other(xn), me),
                                                   copy(a, 6, other(yn), me), copy(a, 7, other(diag), me)])

        return n, Parts

    def start(self, srcs, outs, sems):
        _, p = self._parts(srcs, outs, sems)
        for cp in p.mine() + p.own():
            cp.start()

    def middle(self, srcs, outs, sems):
        n, p = self._parts(srcs, outs, sems)
        for a in range(n):
            p.from_x(a).wait_recv()
            p.from_y(a).wait_recv()
            for cp in p.relay(a):
                cp.start()

    def finish(self, srcs, outs, sems):
        n, p = self._parts(srcs, outs, sems)
        for a in range(n):
            for cp in p.diag_halves(a):
                cp.wait_recv()
            p.last(a).start()
        for a in range(n):
            for cp in p.from_sibling(a):
                cp.wait_recv()
        for cp in p.own() + [cp for a in range(n) for cp in p.relay(a) + [p.last(a)]]:
            cp.wait_send()
        for cp in p.mine():
            cp.wait()


class _RsStage2(_Exchange):
    def start(self, srcs, outs, sems):
        for cp in self._copies(srcs, outs, sems):
            cp.start()

    def finish(self, srcs, outs, sems):
        copies = self._copies(srcs, outs, sems)
        for cp in copies:
            cp.wait_recv()
        for cp in copies:
            cp.wait_send()


    def __init__(self, pbs):
        n = len(pbs)
        self.inputs = list(pbs)
        self.out_shapes = [jax.ShapeDtypeStruct((3,) + p.shape[1:], p.dtype) for p in pbs]
        self.sems = [pltpu.SemaphoreType.DMA((n, 3)), pltpu.SemaphoreType.DMA((n, 3))]

    def _copies(self, srcs, outs, sems):
        send_sems, recv_sems = sems
        x, y, c = lax.axis_index("x"), lax.axis_index("y"), lax.axis_index("c")
        chips = [(1 - x, y), (x, 1 - y), (1 - x, 1 - y)]
        return [pltpu.make_async_remote_copy(
            src_ref=srcs[a].at[2 * chip[0] + chip[1]], dst_ref=outs[a].at[j], send_sem=send_sems.at[a, j],
            recv_sem=recv_sems.at[a, j], device_id=(*chip, c), device_id_type=MESH)
            for a in range(len(srcs)) for j, chip in enumerate(chips)]


class _DirectGather(_Exchange):
    def __init__(self, arrays):
        n = len(arrays)
        self.inputs = list(arrays)
        self.out_shapes = [jax.ShapeDtypeStruct((N_DEV,) + a.shape, a.dtype) for a in arrays]
        self.sems = [pltpu.SemaphoreType.DMA((n, 7)), pltpu.SemaphoreType.DMA((n, 7)), pltpu.SemaphoreType.DMA((n,))]

    def _copies(self, srcs, outs, sems):
        send_sems, recv_sems, local_sems = sems
        x, y, c = lax.axis_index("x"), lax.axis_index("y"), lax.axis_index("c")
        me = 4 * x + 2 * y + c
        remote, local = [], []
        for a in range(len(srcs)):
            local.append(pltpu.make_async_copy(srcs[a], outs[a].at[me], local_sems.at[a]))
            for k in range(1, N_DEV):
                peer = (x ^ (k >> 2), y ^ ((k >> 1) & 1), c ^ (k & 1))
                remote.append(pltpu.make_async_remote_copy(
                    src_ref=srcs[a], dst_ref=outs[a].at[me], send_sem=send_sems.at[a, k - 1],
                    recv_sem=recv_sems.at[a, k - 1], device_id=peer, device_id_type=MESH))
        return remote, local

    def start(self, srcs, outs, sems):
        remote, local = self._copies(srcs, outs, sems)
        for cp in local + remote:
            cp.start()

    def finish(self, srcs, outs, sems):
        remote, local = self._copies(srcs, outs, sems)
        for cp in remote:
            cp.wait_recv()
        for cp in remote:
            cp.wait_send()
        for cp in local:
            cp.wait()


class _Both(_Exchange):
    def __init__(self, a, b):
        self.a, self.b = a, b
        self.inputs = list(a.inputs) + list(b.inputs)
        self.out_shapes = list(a.out_shapes) + list(b.out_shapes)
        self.sems = list(a.sems) + list(b.sems)

    def _split(self, srcs, outs, sems):
        na, oa, sa = len(self.a.inputs), len(self.a.out_shapes), len(self.a.sems)
        return (srcs[:na], outs[:oa], sems[:sa]), (srcs[na:], outs[oa:], sems[sa:])

    def start(self, srcs, outs, sems):
        pa, pb = self._split(srcs, outs, sems)
        self.a.start(*pa)
        self.b.start(*pb)

    def middle(self, srcs, outs, sems):
        pa, pb = self._split(srcs, outs, sems)
        self.a.middle(*pa)
        self.b.middle(*pb)

    def finish(self, srcs, outs, sems):
        pa, pb = self._split(srcs, outs, sems)
        self.a.finish(*pa)
        self.b.finish(*pb)


_ANY = pl.BlockSpec(memory_space=pl.ANY)


def _run_exchange(ex, name):
    n_in, n_out = len(ex.inputs), len(ex.out_shapes)

    def body(*refs):
        srcs, outs, sems = refs[:n_in], refs[n_in:n_in + n_out], refs[n_in + n_out:]
        ex.start(srcs, outs, sems)
        ex.middle(srcs, outs, sems)
        ex.finish(srcs, outs, sems)

    return pl.pallas_call(
        body, name=name, out_shape=list(ex.out_shapes), in_specs=[_ANY] * n_in, out_specs=[_ANY] * n_out,
        scratch_shapes=list(ex.sems),
    )(*ex.inputs)


CARRY_MIDDLE = 0.7


def _call(body, name, grid, in_specs, out_specs, out_shape, args, sem, carry=None, middle=CARRY_MIDDLE, scratch=()):
    if carry is None:
        return pl.pallas_call(functools.partial(body), name=name, grid=grid, in_specs=in_specs, out_specs=out_specs,
                              out_shape=out_shape, scratch_shapes=list(scratch), compiler_params=_cparams(sem))(*args)
    n_in, n_out = len(in_specs), len(out_specs)
    nc_in, nc_out = len(carry.inputs), len(carry.out_shapes)

    def carried(*refs):
        ins = refs[:n_in]
        c_in = refs[n_in:n_in + nc_in]
        outs = refs[n_in + nc_in:n_in + nc_in + n_out]
        c_out = refs[n_in + nc_in + n_out:n_in + nc_in + n_out + nc_out]
        own = refs[n_in + nc_in + n_out + nc_out:n_in + nc_in + n_out + nc_out + len(scratch)]
        sems = refs[n_in + nc_in + n_out + nc_out + len(scratch):]
        ids = [pl.program_id(i) for i in range(len(grid))]
        is_first = functools.reduce(jnp.logical_and, [i == 0 for i in ids])
        is_last = functools.reduce(jnp.logical_and, [i == g - 1 for i, g in zip(ids, grid)])
        @pl.when(is_first)
        def _():
            carry.start(c_in, c_out, sems)

        if middle is not None:
            @pl.when(functools.reduce(jnp.logical_and, [ids[0] == round(middle * (grid[0] - 1))]
                                      + [i == 0 for i in ids[1:]]))
            def _():
                carry.middle(c_in, c_out, sems)

        body(*ins, *outs, *own)

        @pl.when(is_last)
        def _():
            if middle is None:
                carry.middle(c_in, c_out, sems)
            carry.finish(c_in, c_out, sems)

    return pl.pallas_call(
        carried, name=name, grid=grid, in_specs=list(in_specs) + [_ANY] * nc_in,
        out_specs=list(out_specs) + [_ANY] * nc_out, out_shape=list(out_shape) + list(carry.out_shapes),
        scratch_shapes=list(scratch) + list(carry.sems), compiler_params=_cparams(sem))(*args, *carry.inputs)


def _rs_sum2_adam(pos, p, r2, w, m, v, name):
    _, rows, d = p.shape
    tr = rows // 2 if rows % 16 == 0 else rows

    def body(pos_ref, p_ref, r_ref, w_ref, m_ref, v_ref, g_ref, d_ref, nm_ref, nv_ref):
        r = r_ref[...].astype(F32)
        g = ((p_ref[...] + r[0]) + r[1]) + r[2]
        g_ref[...] = g
        d_ref[...], nm_ref[...], nv_ref[...] = _adam_math(w_ref[...], g, m_ref[...], v_ref[...])

    blk = pl.BlockSpec((tr, d), lambda i, pos_ref: (i, 0))
    return pl.pallas_call(
        body, name=name,
        grid_spec=pltpu.PrefetchScalarGridSpec(
            num_scalar_prefetch=1, grid=(rows // tr,),
            in_specs=[pl.BlockSpec((None, tr, d), lambda i, pos_ref: (2 * pos_ref[0] + pos_ref[1], i, 0)),
                      pl.BlockSpec((3, tr, d), lambda i, pos_ref: (0, i, 0)), blk, blk, blk],
            out_specs=[blk] * 4),
        out_shape=[jax.ShapeDtypeStruct((rows, d), F32)] * 4,
        compiler_params=_cparams(("arbitrary",)),
    )(pos, p, r2, w, m, v)


def _ffn_chunk(f):
    for cand in (256, 128):
        if f % cand == 0:
            return cand
    return f


def _loss_head(x, gg, target, loss_ref, dg_ref):
    @pl.when(pl.program_id(0) == 0)
    def _():
        loss_ref[...] = jnp.zeros_like(loss_ref)
        dg_ref[...] = jnp.zeros_like(dg_ref)

    r = lax.rsqrt(jnp.mean(x * x, axis=-1, keepdims=True) + EPS)
    xhat = x * r
    e = xhat * gg - target
    loss_ref[...] += 0.5 * jnp.sum(jnp.mean(e * e, axis=-1, keepdims=True), axis=0, keepdims=True)
    dy = e * (1.0 / x.shape[-1])
    dg_ref[...] += jnp.sum(dy * xhat, axis=0, keepdims=True)
    dxhat = dy * gg
    return r * (dxhat - xhat * jnp.mean(dxhat * xhat, axis=-1, keepdims=True))


def _ffn_gate(h, g, wg_t, name, tm, carry=None):
    s, d = h.shape
    f = wg_t.shape[0]

    def body(h_ref, g_ref, wg_ref, n_ref, gate_ref):
        x = h_ref[...]
        r = lax.rsqrt(jnp.mean(x * x, axis=-1, keepdims=True) + EPS)
        nb = (x * r * g_ref[...]).astype(BF16)
        n_ref[...] = nb
        gate_ref[...] = _dot_nt(nb, wg_ref[...]).astype(BF16)

    row = lambda w: pl.BlockSpec((tm, w), lambda i: (i, 0))
    return _call(body, name, (s // tm,), [row(d), _whole((1, d)), _whole((f, d))], [row(d), row(f)],
                 [jax.ShapeDtypeStruct((s, d), BF16), jax.ShapeDtypeStruct((s, f), BF16)], (h, g, wg_t),
                 ("arbitrary",), carry)


def _ffn_up(n, gate, wu_t, name, tm, carry=None):
    s, d = n.shape
    f = wu_t.shape[0]
    tf = _ffn_chunk(f)

    def body(n_ref, gate_ref, wu_ref, up_ref, act_ref, act_t_ref):
        nb = n_ref[...]
        for j in range(f // tf):
            sl = slice(j * tf, (j + 1) * tf)
            up = _dot_nt(nb, wu_ref[sl, :])
            gate = gate_ref[:, sl].astype(F32)
            up_ref[:, sl] = up.astype(BF16)
            act = (gate * _sigmoid(gate) * up).astype(BF16)
            act_ref[:, sl] = act
            act_t_ref[sl, :] = act.T

    row = lambda w: pl.BlockSpec((tm, w), lambda i: (i, 0))
    return _call(body, name, (s // tm,), [row(d), row(f), _whole((f, d))],
                 [row(f), row(f), pl.BlockSpec((f, tm), lambda i: (0, i))],
                 [jax.ShapeDtypeStruct((s, f), BF16)] * 2 + [jax.ShapeDtypeStruct((f, s), BF16)], (n, gate, wu_t),
                 ("arbitrary",), carry)


def _ffn_fwd(h, g, wg_t, wu_t, wd, name, tm, carry=None, head=None, middle=CARRY_MIDDLE):
    s, d = h.shape
    f = wg_t.shape[0]
    tf = _ffn_chunk(f)

    def body(h_ref, g_ref, wg_ref, wu_ref, wd_ref, *refs):
        if head is None:
            o_ref, n_ref, gate_ref, up_ref, act_t_ref, act_ref = refs
        else:
            gf_ref, t_ref, o_ref, n_ref, gate_ref, up_ref, act_t_ref, loss_ref, dgf_ref, act_ref = refs
        x = h_ref[...]
        r = lax.rsqrt(jnp.mean(x * x, axis=-1, keepdims=True) + EPS)
        nb = (x * r * g_ref[...]).astype(BF16)
        n_ref[...] = nb
        for j in range(f // tf):
            sl = slice(j * tf, (j + 1) * tf)
            gate = _dot_nt(nb, wg_ref[sl, :])
            up = _dot_nt(nb, wu_ref[sl, :])
            gate_ref[:, sl] = gate.astype(BF16)
            up_ref[:, sl] = up.astype(BF16)
            act = gate * _sigmoid(gate) * up
            act = act.astype(BF16)
            act_ref[:, sl] = act
            act_t_ref[sl, :] = act.T
        h_out = x + 0.5 * _dot_nn(act_ref[...], wd_ref[...])
        o_ref[...] = h_out if head is None else _loss_head(h_out, gf_ref[...], t_ref[...], loss_ref, dgf_ref)

    row = lambda w: pl.BlockSpec((tm, w), lambda i: (i, 0))
    in_specs = [row(d), _whole((1, d)), _whole((f, d)), _whole((f, d)), _whole((f, d))]
    out_specs = [row(d), row(d), row(f), row(f), pl.BlockSpec((f, tm), lambda i: (0, i))]
    out_shape = ([jax.ShapeDtypeStruct((s, d), F32), jax.ShapeDtypeStruct((s, d), BF16)]
                 + [jax.ShapeDtypeStruct((s, f), BF16)] * 2 + [jax.ShapeDtypeStruct((f, s), BF16)])
    args = (h, g, wg_t, wu_t, wd)
    scratch = (pltpu.VMEM((tm, f), BF16),)
    if head is not None:
        in_specs += [_whole((1, d)), row(d)]
        out_specs += [pl.BlockSpec((1, 1), lambda i: (0, 0)), pl.BlockSpec((1, d), lambda i: (0, 0))]
        out_shape += [jax.ShapeDtypeStruct((1, 1), F32), jax.ShapeDtypeStruct((1, d), F32)]
        args += tuple(head)
    return _call(body, name, (s // tm,), in_specs, out_specs, out_shape, args, ("arbitrary",), carry, middle, scratch)


def _gate_grads(dh_ref, gate_ref, up_ref, wd_ref, dgate_ref, dup_ref, dgate_t_ref, dup_t_ref, dhh_ref, tf,
                cols=slice(None)):
    dhh = (0.5 * dh_ref[...]).astype(BF16)
    dhh_ref[...] = dhh
    for j in range(gate_ref.shape[1] // tf):
        sl = slice(j * tf, (j + 1) * tf)
        gt = gate_ref[:, sl].astype(F32)
        u = up_ref[:, sl].astype(F32)
        dact = _dot_nt(dhh, wd_ref[sl, :])
        sg = _sigmoid(gt)
        dup = dact * (gt * sg)
        dgate = dact * u * (sg * (1.0 + gt * (1.0 - sg)))
        dup, dgate = dup.astype(BF16), dgate.astype(BF16)
        dup_ref[:, sl] = dup
        dgate_ref[:, sl] = dgate
        dup_t_ref[sl, cols] = dup.T
        dgate_t_ref[sl, cols] = dgate.T


def _input_grad(h_ref, dh_ref, dgate_ref, dup_ref, g_ref, wg_ref, wu_ref, o_ref, dg_ref):
    x = h_ref[...]
    r = lax.rsqrt(jnp.mean(x * x, axis=-1, keepdims=True) + EPS)
    xhat = x * r
    dn = _dot_nn(dgate_ref[...], wg_ref[...]) + _dot_nn(dup_ref[...], wu_ref[...])
    dxhat = dn * g_ref[...]
    o_ref[...] = dh_ref[...] + r * (dxhat - xhat * jnp.mean(dxhat * xhat, axis=-1, keepdims=True))

    @pl.when(pl.program_id(0) == 0)
    def _():
        dg_ref[...] = jnp.zeros_like(dg_ref)

    dg_ref[...] += jnp.sum(dn * xhat, axis=0, keepdims=True)


def _ffn_bwd(h_in, dh_out, gate, up, g, wg_t, wu_t, wd, name, tm):
    s, d = h_in.shape
    f = gate.shape[1]
    tf = _ffn_chunk(f)

    pair = 2 if (s // tm) % 2 == 0 else 1

    def body(h_ref, dh_ref, gate_ref, up_ref, g_ref, wg_ref, wu_ref, wd_ref,
             o_ref, dg_ref, dgate_t_ref, dup_t_ref, dhh_ref, dgate_ref, dup_ref):
        cols = pl.ds(pl.multiple_of((pl.program_id(0) % pair) * tm, tm), tm)
        _gate_grads(dh_ref, gate_ref, up_ref, wd_ref, dgate_ref, dup_ref, dgate_t_ref, dup_t_ref, dhh_ref, tf, cols)
        _input_grad(h_ref, dh_ref, dgate_ref, dup_ref, g_ref, wg_ref, wu_ref, o_ref, dg_ref)

    row = lambda w: pl.BlockSpec((tm, w), lambda i: (i, 0))
    col = pl.BlockSpec((f, pair * tm), lambda i: (0, i // pair))
    return pl.pallas_call(
        body, name=name, grid=(s // tm,),
        in_specs=[row(d), row(d), row(f), row(f), _whole((1, d)), _whole((f, d)), _whole((f, d)), _whole((f, d))],
        out_specs=[row(d), pl.BlockSpec((1, d), lambda i: (0, 0)), col, col, row(d)],
        out_shape=[jax.ShapeDtypeStruct((s, d), F32), jax.ShapeDtypeStruct((1, d), F32),
                   jax.ShapeDtypeStruct((f, s), BF16), jax.ShapeDtypeStruct((f, s), BF16),
                   jax.ShapeDtypeStruct((s, d), BF16)],
        scratch_shapes=[pltpu.VMEM((tm, f), BF16), pltpu.VMEM((tm, f), BF16)],
        compiler_params=_cparams(("arbitrary",)),
    )(h_in, dh_out, gate, up, g, wg_t, wu_t, wd)


def _ffn_bwd_gates(dh_out, gate, up, wd, name, tm, carry=None):
    s, d = dh_out.shape
    f = gate.shape[1]
    tf = _ffn_chunk(f)

    def body(dh_ref, gate_ref, up_ref, wd_ref, dgate_ref, dup_ref, dgate_t_ref, dup_t_ref, dhh_ref):
        _gate_grads(dh_ref, gate_ref, up_ref, wd_ref, dgate_ref, dup_ref, dgate_t_ref, dup_t_ref, dhh_ref, tf)

    row = lambda w: pl.BlockSpec((tm, w), lambda i: (i, 0))
    col = pl.BlockSpec((f, tm), lambda i: (0, i))
    return _call(
        body, name, (s // tm,), [row(d), row(f), row(f), _whole((f, d))], [row(f), row(f), col, col, row(d)],
        [jax.ShapeDtypeStruct((s, f), BF16)] * 2 + [jax.ShapeDtypeStruct((f, s), BF16)] * 2
        + [jax.ShapeDtypeStruct((s, d), BF16)],
        (dh_out, gate, up, wd), ("arbitrary",), carry)


def _ffn_bwd_input(h_in, dh_out, dgate, dup, g, wg_t, wu_t, name, tm, carry=None):
    s, d = h_in.shape
    f = dgate.shape[1]

    row = lambda w: pl.BlockSpec((tm, w), lambda i: (i, 0))
    return _call(
        _input_grad, name, (s // tm,),
        [row(d), row(d), row(f), row(f), _whole((1, d)), _whole((f, d)), _whole((f, d))],
        [row(d), pl.BlockSpec((1, d), lambda i: (0, 0))],
        [jax.ShapeDtypeStruct((s, d), F32), jax.ShapeDtypeStruct((1, d), F32)],
        (h_in, dh_out, dgate, dup, g, wg_t, wu_t), ("arbitrary",), carry)


def _wgrad_rs1(pos, a_t, b, name, carry=None):
    f, s = a_t.shape
    d = b.shape[1]
    fk = f // N_DEV
    nc_in = 0 if carry is None else len(carry.inputs)
    nc_out = 0 if carry is None else len(carry.out_shapes)

    def body(pos_ref, a_ref, b_ref, *refs):
        c_in = refs[:nc_in]
        p_ref, pb_ref = refs[nc_in:nc_in + 2]
        c_out = refs[nc_in + 2:nc_in + 2 + nc_out]
        stage, land, send_sems, recv_sems = refs[nc_in + 2 + nc_out:nc_in + 6 + nc_out]
        c_sems = refs[nc_in + 6 + nc_out:]
        t = pl.program_id(0)
        q = t % 4
        x, y, c = lax.axis_index("x"), lax.axis_index("y"), lax.axis_index("c")

        def push(k):
            return pltpu.make_async_remote_copy(src_ref=stage.at[k], dst_ref=land.at[k], send_sem=send_sems.at[k],
                                                recv_sem=recv_sems.at[k], device_id=(x, y, 1 - c), device_id_type=MESH)

        if carry is not None:
            @pl.when(t == 0)
            def _():
                carry.start(c_in, c_out, c_sems)

        g = _dot_nn(a_ref[...], b_ref[...])

        @pl.when(t < 4)
        def _():
            stage[q] = g.astype(BF16)
            push(q).start()

        @pl.when(t >= 4)
        def _():
            push(q).wait_recv()
            p = g + land[q].astype(F32)
            p_ref[...] = p
            pb_ref[...] = p.astype(BF16)

        @pl.when(t == 7)
        def _():
            for k in range(4):
                push(k).wait_send()
            if carry is not None:
                carry.middle(c_in, c_out, c_sems)
                carry.finish(c_in, c_out, c_sems)

    def shard(t, pos_ref):
        return 4 * ((t % 4) // 2) + 2 * (t % 2) + jnp.where(t < 4, 1 - pos_ref[2], pos_ref[2])

    out = pl.BlockSpec((None, fk, d), lambda t, pos_ref: (jnp.maximum(t - 4, 0), 0, 0))
    return pl.pallas_call(
        body, name=name,
        grid_spec=pltpu.PrefetchScalarGridSpec(
            num_scalar_prefetch=1, grid=(8,),
            in_specs=[pl.BlockSpec((fk, s), lambda t, pos_ref: (shard(t, pos_ref), 0)),
                      pl.BlockSpec((s, d), lambda t, pos_ref: (0, 0), pipeline_mode=pl.Buffered(1))]
            + [_ANY] * nc_in,
            out_specs=[out, out] + [_ANY] * nc_out,
            scratch_shapes=[pltpu.VMEM((4, fk, d), BF16), pltpu.VMEM((4, fk, d), BF16),
                            pltpu.SemaphoreType.DMA((4,)), pltpu.SemaphoreType.DMA((4,))]
            + ([] if carry is None else list(carry.sems))),
        out_shape=[jax.ShapeDtypeStruct((4, fk, d), F32), jax.ShapeDtypeStruct((4, fk, d), BF16)]
        + ([] if carry is None else list(carry.out_shapes)),
        compiler_params=_cparams(("arbitrary",)),
    )(pos, a_t, b, *([] if carry is None else carry.inputs))


def _rope(t, c, sa, sb, reps):
    c, sa, sb = (jnp.tile(v, (1, reps)) if reps > 1 else v for v in (c, sa, sb))
    w = t.shape[1]
    return t * c + pltpu.roll(t, w - 8, 1) * sa + pltpu.roll(t, 8, 1) * sb


def _rope_bwd(dt, c, sa, sb, reps):
    c, sa, sb = (jnp.tile(v, (1, reps)) if reps > 1 else v for v in (c, sa, sb))
    w = dt.shape[1]
    return dt * c + pltpu.roll(dt * sa, 8, 1) + pltpu.roll(dt * sb, w - 8, 1)


def _mix_in(h, g, win_t, tabs, name, tm, carry=None):
    s, d = h.shape
    n_in = win_t.shape[0]

    def body(h_ref, g_ref, w_ref, c_ref, sa_ref, sb_ref, q_ref, k_ref, v_ref, pc_ref, n_ref):
        x = h_ref[...]
        r = lax.rsqrt(jnp.mean(x * x, axis=-1, keepdims=True) + EPS)
        nb = (x * r * g_ref[...]).astype(BF16)
        n_ref[...] = nb
        u = _dot_nt(nb, w_ref[...])
        c, sa, sb = c_ref[...], sa_ref[...], sb_ref[...]
        q_ref[...] = _rope(u[:, :ATTN_W], c, sa, sb, ATTN_W // 128).astype(BF16)
        k_ref[...] = _rope(u[:, ATTN_W:ATTN_W + KV_W], c, sa, sb, 1).astype(BF16)
        v_ref[...] = u[:, ATTN_W + KV_W:ATTN_W + 2 * KV_W].astype(BF16)
        pc_ref[...] = u[:, ATTN_W + 2 * KV_W:]

    row = lambda w: pl.BlockSpec((tm, w), lambda i: (i, 0))
    return _call(
        body, name, (s // tm,),
        [row(d), _whole((1, d)), _whole((n_in, d)), row(128), row(128), row(128)],
        [row(ATTN_W), row(KV_W), row(KV_W), row(POOL_W), row(d)],
        [jax.ShapeDtypeStruct((s, ATTN_W), BF16), jax.ShapeDtypeStruct((s, KV_W), BF16),
         jax.ShapeDtypeStruct((s, KV_W), BF16), jax.ShapeDtypeStruct((s, POOL_W), F32),
         jax.ShapeDtypeStruct((s, d), BF16)],
        (h, g, win_t, *tabs), ("arbitrary",), carry)


def _band_mask(n, nb, transposed):
    shape = (3 * BLK, 2 * BLK) if transposed else (2 * BLK, 3 * BLK)
    i = lax.broadcasted_iota(jnp.int32, shape, 1 if transposed else 0) % BLK
    j = lax.broadcasted_iota(jnp.int32, shape, 0 if transposed else 1)
    kpos = (n - 1) * BLK + j
    return (j >= i) & (j <= i + 2 * BLK) & (kpos >= 0) & (kpos < nb * BLK)


def _block_diag(t, kh):
    tf = t.astype(F32)
    tr = pltpu.roll(tf, HEAD_DIM, 1)
    lo = lax.broadcasted_iota(jnp.int32, tf.shape, 1) < HEAD_DIM
    top, bot = (tf, tr) if kh == 0 else (tr, tf)
    return jnp.concatenate([jnp.where(lo, top, 0.0), jnp.where(lo, 0.0, bot)], axis=0).astype(BF16)


def _fold_diag(tbd):
    lo = lax.broadcasted_iota(jnp.int32, (3 * BLK, 2 * HEAD_DIM), 1) < HEAD_DIM
    t = jnp.where(lo, tbd[:3 * BLK], tbd[3 * BLK:])
    return t + pltpu.roll(t, HEAD_DIM, 1)


def _stack_pairs(x, kh):
    return jnp.concatenate([x[:, (2 * kh) * 128:(2 * kh + 1) * 128], x[:, (2 * kh + 1) * 128:(2 * kh + 2) * 128]], axis=0)


def _sink_of(sink_ref, kh, half, axis):
    shape = (2 * BLK, 1) if axis == 0 else (1, 2 * BLK)
    first = lax.broadcasted_iota(jnp.int32, shape, axis) < BLK
    return jnp.where(first, sink_ref[0, GROUP * kh + half], sink_ref[0, GROUP * kh + 2 + half])


def _softmax_sink(sc, valid, sink, axis):
    sc = jnp.where(valid, sc, -1e30)
    m = jnp.maximum(jnp.max(sc, axis=axis, keepdims=True), sink)
    e = jnp.exp(sc - m)
    es = jnp.exp(sink - m)
    inv = 1.0 / (jnp.sum(e, axis=axis, keepdims=True) + es)
    return e * inv, es * inv


def _attn_blocks_per_step(nb):
    return next(nq for nq in (4, 2, 1) if nb % nq == 0)


def _band_specs(nq, nb, w, col=0):
    return [pl.BlockSpec((BLK, w), lambda m: (jnp.maximum(nq * m - 1, 0), col)),
            pl.BlockSpec((nq * BLK, w), lambda m: (m, col)),
            pl.BlockSpec((BLK, w), lambda m: (jnp.minimum(nq * m + nq, nb - 1), col))]


def _attn_pool_fwd(q, k, v, pc, sink, pool_w, pool_scale, pband, name, carry=None, middle=CARRY_MIDDLE):
    s = q.shape[0]
    nb = s // BLK

    nq = _attn_blocks_per_step(nb)

    def body(sink_ref, q_ref, k0, k1, k2, v0, v1, v2, p0, p1, p2, pw_ref, ps_ref, pb_ref, o_ref, o_t_ref):
        kall = jnp.concatenate([k0[...], k1[...], k2[...]], axis=0)
        vall = jnp.concatenate([v0[...], v1[...], v2[...]], axis=0)
        pall = jnp.concatenate([p0[...], p1[...], p2[...]], axis=0).astype(BF16)
        qall = q_ref[...] * SCORE_SCALE
        for j in range(nq):
            n = pl.program_id(0) * nq + j
            rows, band = slice(j * BLK, (j + 1) * BLK), slice(j * BLK, (j + 3) * BLK)
            valid = _band_mask(n, nb, False)
            kb, vb, qs = kall[band], vall[band], qall[rows]
            for kh in range(N_KV):
                sc = _dot_nt(_stack_pairs(qs, kh), _block_diag(kb, kh))
                p = [_softmax_sink(sc[:, half * 3 * BLK:(half + 1) * 3 * BLK], valid,
                                   _sink_of(sink_ref, kh, half, 0), 1)[0] for half in range(2)]
                o2 = _dot_nn(jnp.concatenate(p, axis=1).astype(BF16), _block_diag(vb, kh)).astype(BF16)
                o_ref[rows, (2 * kh) * 128:(2 * kh + 1) * 128] = o2[:BLK]
                o_ref[rows, (2 * kh + 1) * 128:(2 * kh + 2) * 128] = o2[BLK:]
            ext = pall[band]
            var = _variant_index(n, nb)
            for gi in range(POOL_G):
                gsl = slice(gi * POOL_GW, (gi + 1) * POOL_GW)
                dg = _dot_nn(pb_ref[var, gi], ext[:, gsl])
                yg = _dot_nn(dg.astype(BF16), pw_ref[gi].astype(BF16))
                o_ref[rows, ATTN_W + gi * POOL_GW:ATTN_W + (gi + 1) * POOL_GW] = (yg * ps_ref[:, gsl]).astype(BF16)
        o_t_ref[...] = o_ref[...].astype(F32).T.astype(BF16)

    return _call(
        body, name, (nb // nq,),
        [pl.BlockSpec(memory_space=pltpu.SMEM), pl.BlockSpec((nq * BLK, ATTN_W), lambda m: (m, 0)),
         *_band_specs(nq, nb, KV_W), *_band_specs(nq, nb, KV_W), *_band_specs(nq, nb, POOL_W),
         _whole((POOL_G, POOL_GW, POOL_GW)), _whole((1, POOL_W)), _whole(pband.shape)],
        [pl.BlockSpec((nq * BLK, ATTN_W + POOL_W), lambda m: (m, 0)),
         pl.BlockSpec((ATTN_W + POOL_W, nq * BLK), lambda m: (0, m))],
        [jax.ShapeDtypeStruct((s, ATTN_W + POOL_W), BF16), jax.ShapeDtypeStruct((ATTN_W + POOL_W, s), BF16)],
        (sink, q, k, k, k, v, v, v, pc, pc, pc, pool_w, pool_scale, pband), ("arbitrary",), carry, middle)


def _attn_pool_bwd(q, k, v, pc, dmix, sink, pool_w, pool_scale, pband, ptband, name, carry=None):
    s = q.shape[0]
    nb = s // BLK
    nq = _attn_blocks_per_step(nb)

    def body(sink_ref, q_ref, k0, k1, k2, v0, v1, v2, p0, p1, p2, da_ref, d0, d1, d2, pw_ref, ps_ref, pb_ref, ptb_ref,
             dq_ref, dk_ref, dv_ref, dpc_ref, dsink_ref, dpw_ref, dps_ref):
        @pl.when(pl.program_id(0) == 0)
        def _():
            dsink_ref[...] = jnp.zeros_like(dsink_ref)
            dpw_ref[...] = jnp.zeros_like(dpw_ref)
            dps_ref[...] = jnp.zeros_like(dps_ref)

        kall = jnp.concatenate([k0[...], k1[...], k2[...]], axis=0)
        vall = jnp.concatenate([v0[...], v1[...], v2[...]], axis=0)
        pall = jnp.concatenate([p0[...], p1[...], p2[...]], axis=0).astype(BF16)
        dpall = jnp.concatenate([d0[...], d1[...], d2[...]], axis=0)
        lo = lax.broadcasted_iota(jnp.int32, (3 * BLK, KV_W), 1) < HEAD_DIM
        for j in range(nq):
            n = pl.program_id(0) * nq + j
            rows, band = slice(j * BLK, (j + 1) * BLK), slice(j * BLK, (j + 3) * BLK)
            valid = _band_mask(n, nb, True)
            kb, vb, qb = kall[band], vall[band], q_ref[rows, :]
            qs = qb * SCORE_SCALE
            da = da_ref[rows, :].astype(BF16)
            dk_fold, dv_fold = [], []
            for kh in range(N_KV):
                kbd, vbd = _block_diag(kb, kh), _block_diag(vb, kh)
                q2, do2 = _stack_pairs(qb, kh), _stack_pairs(da, kh)
                sc_t = _dot_nt(kbd, _stack_pairs(qs, kh))
                dp_t = _dot_nt(vbd, do2)
                p_t, ds_t = [], []
                for half in range(2):
                    keys = slice(half * 3 * BLK, (half + 1) * 3 * BLK)
                    p, ps = _softmax_sink(sc_t[keys], valid, _sink_of(sink_ref, kh, half, 1), 0)
                    delta = jnp.sum(p * dp_t[keys], axis=0, keepdims=True)
                    p_t.append(p.astype(BF16))
                    ds_t.append((p * (dp_t[keys] - delta)).astype(BF16))
                    dsk = -ps * delta
                    for pair in range(2):
                        h = GROUP * kh + 2 * pair + half
                        part = jnp.sum(dsk[:, pair * BLK:(pair + 1) * BLK], axis=1, keepdims=True)
                        dsink_ref[h:h + 1, :] += jnp.broadcast_to(part, (1, 128))
                p_t = jnp.concatenate(p_t, axis=0)
                ds_t = jnp.concatenate(ds_t, axis=0)
                dq2 = _dot_tn(ds_t, kbd) * SCORE_SCALE
                dq_ref[rows, (2 * kh) * 128:(2 * kh + 1) * 128] = dq2[:BLK]
                dq_ref[rows, (2 * kh + 1) * 128:(2 * kh + 2) * 128] = dq2[BLK:]
                dk_fold.append(_fold_diag(_dot_nn(ds_t, q2)) * SCORE_SCALE)
                dv_fold.append(_fold_diag(_dot_nn(p_t, do2)))
            dk_all = jnp.where(lo, dk_fold[0], dk_fold[1])
            dv_all = jnp.where(lo, dv_fold[0], dv_fold[1])
            for t in range(3):
                dk_ref[j, t] = dk_all[t * BLK:(t + 1) * BLK]
                dv_ref[j, t] = dv_all[t * BLK:(t + 1) * BLK]
            ext, dpe = pall[band], dpall[band]
            dpc_cur = dpall[(j + 1) * BLK:(j + 2) * BLK]
            var = _variant_index(n, nb)
            for gi in range(POOL_G):
                gsl = slice(gi * POOL_GW, (gi + 1) * POOL_GW)
                wg = pw_ref[gi].astype(BF16)
                sc = ps_ref[:, gsl]
                dgb = _dot_nn(pb_ref[var, gi], ext[:, gsl]).astype(BF16)
                yg = _dot_nn(dgb, wg)
                dps_ref[:, gsl] += jnp.sum(dpc_cur[:, gsl] * yg, axis=0, keepdims=True)
                dpw_ref[gi] += _dot_tn(dgb, (dpc_cur[:, gsl] * sc).astype(BF16))
                dd = _dot_nt((dpe[:, gsl] * sc).astype(BF16), wg)
                dpc_ref[rows, gsl] = _dot_nn(ptb_ref[var, gi], dd.astype(BF16))

    fixed = lambda shape: pl.BlockSpec(shape, lambda m: (0,) * len(shape))
    return _call(
        body, name, (nb // nq,),
        [pl.BlockSpec(memory_space=pltpu.SMEM), pl.BlockSpec((nq * BLK, ATTN_W), lambda m: (m, 0)),
         *_band_specs(nq, nb, KV_W), *_band_specs(nq, nb, KV_W), *_band_specs(nq, nb, POOL_W),
         pl.BlockSpec((nq * BLK, ATTN_W), lambda m: (m, 0)), *_band_specs(nq, nb, POOL_W, 1),
         _whole((POOL_G, POOL_GW, POOL_GW)), _whole((1, POOL_W)), _whole(pband.shape), _whole(ptband.shape)],
        [pl.BlockSpec((nq * BLK, ATTN_W), lambda m: (m, 0)),
         pl.BlockSpec((nq, 3, BLK, KV_W), lambda m: (m, 0, 0, 0)),
         pl.BlockSpec((nq, 3, BLK, KV_W), lambda m: (m, 0, 0, 0)),
         pl.BlockSpec((nq * BLK, POOL_W), lambda m: (m, 0)),
         fixed((N_HEADS, 128)), fixed((POOL_G, POOL_GW, POOL_GW)), fixed((1, POOL_W))],
        [jax.ShapeDtypeStruct((s, ATTN_W), F32), jax.ShapeDtypeStruct((nb, 3, BLK, KV_W), F32),
         jax.ShapeDtypeStruct((nb, 3, BLK, KV_W), F32), jax.ShapeDtypeStruct((s, POOL_W), F32),
         jax.ShapeDtypeStruct((N_HEADS, 128), F32),
         jax.ShapeDtypeStruct((POOL_G, POOL_GW, POOL_GW), F32), jax.ShapeDtypeStruct((1, POOL_W), F32)],
        (sink, q, k, k, k, v, v, v, pc, pc, pc, dmix, dmix, dmix, dmix, pool_w, pool_scale, pband, ptband),
        ("arbitrary",), carry)


def _mix_out(h, mix, w_out, name, tm, scale=1.0, carry=None, middle=CARRY_MIDDLE):
    s, d = h.shape
    w = mix.shape[1]

    def body(h_ref, m_ref, w_ref, o_ref):
        o_ref[...] = h_ref[...] + scale * _dot_nn(m_ref[...], w_ref[...])

    row = lambda c: pl.BlockSpec((tm, c), lambda i: (i, 0))
    return _call(body, name, (s // tm,), [row(d), row(w), _whole((w, d))], [row(d)],
                 [jax.ShapeDtypeStruct((s, d), F32)], (h, mix, w_out), ("arbitrary",), carry, middle)


def _mix_out_bwd(dh, w_out, name, tm):
    s, d = dh.shape
    w = w_out.shape[0]

    def body(dh_ref, w_ref, o_ref, dhb_ref):
        dhb = dh_ref[...].astype(BF16)
        dhb_ref[...] = dhb
        o_ref[...] = _dot_nt(dhb, w_ref[...])

    row = lambda c: pl.BlockSpec((tm, c), lambda i: (i, 0))
    return pl.pallas_call(
        body, name=name, grid=(s // tm,), in_specs=[row(d), _whole((w, d))], out_specs=[row(w), row(d)],
        out_shape=[jax.ShapeDtypeStruct((s, w), F32), jax.ShapeDtypeStruct((s, d), BF16)],
        compiler_params=_cparams(("arbitrary",)),
    )(dh, w_out)


def _mix_in_bwd(h, dh, g, win_t, dq, dkp, dvp, dpc, tabs, name, tm, carry=None):
    s, d = h.shape
    nb = s // BLK
    nt = tm // BLK
    n_in = win_t.shape[0]

    def band_sum(n, before, own, after, prev_last, next_first):
        lo = (n > 0).astype(F32)
        hi = (n < s // tm - 1).astype(F32)
        blocks = []
        for b in range(nt):
            from_prev = prev_last[...] * lo if b == 0 else before[b - 1]
            from_next = next_first[...] * hi if b == nt - 1 else after[b + 1]
            blocks.append(from_prev + own[b] + from_next)
        return jnp.concatenate(blocks, axis=0)

    def body(h_ref, dh_ref, g_ref, w_ref, dq_ref, k2, k1, k0, kp, kn, v2, v1, v0, vp, vn, dpc_ref, c_ref, sa_ref,
             sb_ref, o_ref, du_ref, dg_ref):
        n = pl.program_id(0)
        dk = band_sum(n, k2, k1, k0, kp, kn)
        dv = band_sum(n, v2, v1, v0, vp, vn)
        c, sa, sb = c_ref[...], sa_ref[...], sb_ref[...]
        du = jnp.concatenate([_rope_bwd(dq_ref[...], c, sa, sb, ATTN_W // 128), _rope_bwd(dk, c, sa, sb, 1), dv,
                              dpc_ref[...]], axis=1)
        du_ref[...] = du.T.astype(BF16)
        dn = _dot_nn(du.astype(BF16), w_ref[...])
        x = h_ref[...]
        r = lax.rsqrt(jnp.mean(x * x, axis=-1, keepdims=True) + EPS)
        xhat = x * r
        dxhat = dn * g_ref[...]
        o_ref[...] = dh_ref[...] + r * (dxhat - xhat * jnp.mean(dxhat * xhat, axis=-1, keepdims=True))

        @pl.when(n == 0)
        def _():
            dg_ref[...] = jnp.zeros_like(dg_ref)

        dg_ref[...] += jnp.sum(dn * xhat, axis=0, keepdims=True)

    row = lambda w: pl.BlockSpec((tm, w), lambda n: (n, 0))
    slot = lambda t: pl.BlockSpec((nt, None, BLK, KV_W), lambda n, t=t: (n, t, 0, 0))
    parts = [slot(2), slot(1), slot(0),
             pl.BlockSpec((None, None, BLK, KV_W), lambda n: (jnp.maximum(nt * n - 1, 0), 2, 0, 0)),
             pl.BlockSpec((None, None, BLK, KV_W), lambda n: (jnp.minimum(nt * n + nt, nb - 1), 0, 0, 0))]
    return _call(
        body, name, (s // tm,),
        [row(d), row(d), _whole((1, d)), _whole((n_in, d)), row(ATTN_W), *parts, *parts, row(POOL_W),
         row(128), row(128), row(128)],
        [row(d), pl.BlockSpec((n_in, tm), lambda n: (0, n)), pl.BlockSpec((1, d), lambda n: (0, 0))],
        [jax.ShapeDtypeStruct((s, d), F32), jax.ShapeDtypeStruct((n_in, s), BF16), jax.ShapeDtypeStruct((1, d), F32)],
        (h, dh, g, win_t, dq, *[dkp] * 5, *[dvp] * 5, dpc, *tabs), ("arbitrary",), carry)


def _adam_math(w, g, m, v):
    m = ADAM_B1 * m + (1.0 - ADAM_B1) * g
    v = ADAM_B2 * v + (1.0 - ADAM_B2) * (g * g)
    m_hat = m / (1.0 - ADAM_B1 ** ADAM_STEP)
    v_hat = v / (1.0 - ADAM_B2 ** ADAM_STEP)
    delta = -ADAM_LR * (m_hat / (jnp.sqrt(v_hat) + ADAM_EPS) + ADAM_WD * w)
    return delta, m, v


def _adam_small(w, parts, late, m, v, name):
    rows, cols = w.shape

    def body(w_ref, p_ref, l_ref, m_ref, v_ref, g_ref, d_ref, nm_ref, nv_ref):
        g, gl = p_ref[0], l_ref[0]
        for k in range(1, N_DEV):
            g = g + p_ref[k]
            gl = gl + l_ref[k]
        g_ref[...] = g
        g_ref[SMALL_NORM1:SMALL_NORM1 + 8, :] = g[SMALL_NORM1:SMALL_NORM1 + 8] + gl
        d_ref[...], nm_ref[...], nv_ref[...] = _adam_math(w_ref[...], g_ref[...], m_ref[...], v_ref[...])

    return pl.pallas_call(
        body, name=name, out_shape=[jax.ShapeDtypeStruct((rows, cols), F32)] * 4,
    )(w, parts, late, m, v)


SMALL_NORM1 = 512


def _pack_small(norm1, normm, norm2, normf, sink, pool_w, pool_scale, loss=None):
    scale_rows = jnp.pad(pool_scale.reshape(4, 128), ((0, 4), (0, 0)))
    last_rows = jnp.pad(sink.reshape(1, N_HEADS), ((0, 7), (0, 128 - N_HEADS)))
    if loss is not None:
        last_rows = last_rows + jnp.pad(loss.reshape(1, 1), ((1, 6), (0, 127)))
    return jnp.concatenate([pool_w.reshape(512, 128), norm1.reshape(8, 128), normm.reshape(8, 128),
                            norm2.reshape(8, 128), normf.reshape(8, 128), scale_rows, last_rows], axis=0)


def _unpack_small(p):
    return dict(pool_w=p[:512].reshape(1, POOL_G, POOL_GW, POOL_GW), ffn1_norm=p[512:520].reshape(1, 1024),
                mix_norm=p[520:528].reshape(1, 1024), ffn2_norm=p[528:536].reshape(1, 1024),
                final_norm=p[536:544].reshape(1024), pool_scale=p[544:548].reshape(1, POOL_W),
                sink_logits=p[552, :N_HEADS].reshape(1, N_HEADS), loss=p[553, 0])


def kernel(x, ffn1_norm, ffn1_w_gate, ffn1_w_up, ffn1_w_down, mix_norm, w_in, sink_logits, pool_w, pool_scale, w_out, ffn2_norm, ffn2_w_gate, ffn2_w_up, ffn2_w_down, final_norm, loss_target, m_ffn1_norm, m_ffn1_w_gate, m_ffn1_w_up, m_ffn1_w_down, m_mix_norm, m_w_in, m_sink_logits, m_pool_w, m_pool_scale, m_w_out, m_ffn2_norm, m_ffn2_w_gate, m_ffn2_w_up, m_ffn2_w_down, m_final_norm, v_ffn1_norm, v_ffn1_w_gate, v_ffn1_w_up, v_ffn1_w_down, v_mix_norm, v_w_in, v_sink_logits, v_pool_w, v_pool_scale, v_w_out, v_ffn2_norm, v_ffn2_w_gate, v_ffn2_w_up, v_ffn2_w_down, v_final_norm):
    s, d = x.shape[1], x.shape[2]
    tm = min(512, s)
    tm_bwd = min(256, s)
    pos = jnp.stack([lax.axis_index("x"), lax.axis_index("y"), lax.axis_index("c")]).astype(jnp.int32)

    t_bf = lambda w: w[0].T.astype(BF16)
    full = lambda a: a.reshape(N_DEV * a.shape[1], d)
    (wg1,) = map(full, _run_exchange(_AllGather([t_bf(ffn1_w_gate)]), "gather_ffn1_gate"))

    tabs = _rope_tables(s)
    pband, ptband = _pool_tables(s)
    g1, gm, g2, gf = ffn1_norm, mix_norm, ffn2_norm, final_norm.reshape(1, d)

    x0 = x[0]
    n1, gate1, wu1 = _ffn_gate(x0, g1, wg1, "ffn1_gate", tm, carry=_AllGather([t_bf(ffn1_w_up)]))
    up1, act1, act1_t, wd1 = _ffn_up(n1, gate1, full(wu1), "ffn1_up", tm,
                                     carry=_AllGather([ffn1_w_down[0].astype(BF16)]))
    wu1, wd1 = full(wu1), full(wd1)
    h1, win_t, wout = _mix_out(x0, act1, wd1, "ffn1_down", tm, scale=0.5,
                               carry=_AllGather([t_bf(w_in), w_out[0].astype(BF16)]))
    win_t, wout = full(win_t), full(wout)
    q, k, v, pc, n2, wg2 = _mix_in(h1, gm, win_t, tabs, "mix_in", tm, carry=_AllGather([t_bf(ffn2_w_gate)]))
    wg2 = full(wg2)
    mix, mix_t, *gathered = _attn_pool_fwd(q, k, v, pc, sink_logits, pool_w[0], pool_scale, pband, "attn_pool_fwd",
                                           carry=_AllGather([t_bf(ffn2_w_up), ffn2_w_down[0].astype(BF16)]))
    wu2, wd2 = map(full, gathered)
    (h2,) = _mix_out(h1, mix, wout, "mix_out", tm)
    dh3, n3, gate2, up2, act2_t, loss_part, dgf = _ffn_fwd(h2, g2, wg2, wu2, wd2, "ffn2_fwd", tm,
                                                           head=(gf, loss_target[0]))

    sum1, recv2 = {}, {}

    def stage2(keys):
        return _RsStage2([sum1[key][1] for key in keys])

    dh2, dg2, dgate2_t, dup2_t, dhh3 = _ffn_bwd(h2, dh3, gate2, up2, g2, wg2, wu2, wd2, "ffn2_bwd", tm_bwd)
    sum1["g2"] = _wgrad_rs1(pos, dgate2_t, n3, "wgrad_gate2")
    sum1["u2"] = _wgrad_rs1(pos, dup2_t, n3, "wgrad_up2")
    sum1["d2"] = _wgrad_rs1(pos, act2_t, dhh3, "wgrad_down2")
    dmix, dh2b = _mix_out_bwd(dh2, wout, "mix_out_bwd", tm)
    sum1["out"] = _wgrad_rs1(pos, mix_t, dh2b, "wgrad_out")
    dq, dkp, dvp, dpc, dsink, dpw, dps, *r2 = _attn_pool_bwd(
        q, k, v, pc, dmix, sink_logits, pool_w[0], pool_scale, pband, ptband, "attn_pool_bwd",
        carry=stage2(["g2", "u2", "d2"]))
    recv2.update(zip(["g2", "u2", "d2"], r2))
    dh1, du_t, dgm, recv2["out"] = _mix_in_bwd(h1, dh2, gm, win_t, dq, dkp, dvp, dpc, tabs, "mix_in_bwd", tm,
                                               carry=stage2(["out"]))
    sum1["in"] = _wgrad_rs1(pos, du_t, n2, "wgrad_in")
    small_part = _pack_small(jnp.zeros_like(dgm), dgm, dg2, dgf, dsink[:, 0], dpw, dps, loss_part)
    dgate1, dup1, dgate1_t, dup1_t, dhh1, recv2["in"], small_all = _ffn_bwd_gates(
        dh1, gate1, up1, wd1, "ffn1_bwd_gates", tm, carry=_Both(stage2(["in"]), _AllGather([small_part])))
    sum1["g1"] = _wgrad_rs1(pos, dgate1_t, n1, "wgrad_gate1")
    *sum1["u1"], recv2["g1"] = _wgrad_rs1(pos, dup1_t, n1, "wgrad_up1", carry=stage2(["g1"]))
    *sum1["d1"], recv2["u1"] = _wgrad_rs1(pos, act1_t, dhh1, "wgrad_down1", carry=stage2(["u1"]))
    dx, dg1, recv2["d1"] = _ffn_bwd_input(x0, dh1, dgate1, dup1, g1, wg1, wu1, "ffn1_bwd_input", tm,
                                          carry=stage2(["d1"]))

    (dg1_all,) = _run_exchange(_DirectGather([dg1.reshape(8, 128)]), "gather_norm1_grad")
    pk = lambda a, b, c_, e, s_, pw_, psc: _pack_small(a, b, c_, e, s_[0], pw_[0], psc)
    small_w = pk(ffn1_norm, mix_norm, ffn2_norm, final_norm, sink_logits, pool_w, pool_scale)
    small_m = pk(m_ffn1_norm, m_mix_norm, m_ffn2_norm, m_final_norm, m_sink_logits, m_pool_w, m_pool_scale)
    small_v = pk(v_ffn1_norm, v_mix_norm, v_ffn2_norm, v_final_norm, v_sink_logits, v_pool_w, v_pool_scale)
    sg, sd, sm, sv = [_unpack_small(a)
                      for a in _adam_small(small_w, small_all, dg1_all, small_m, small_v, "adam_small")]

    big = {}
    keys = ["g1", "u1", "d1", "g2", "u2", "d2", "in", "out"]
    names = ["ffn1_w_gate", "ffn1_w_up", "ffn1_w_down", "ffn2_w_gate", "ffn2_w_up", "ffn2_w_down", "w_in", "w_out"]
    transposed = [True, True, False, True, True, False, True, False]
    ws = [ffn1_w_gate, ffn1_w_up, ffn1_w_down, ffn2_w_gate, ffn2_w_up, ffn2_w_down, w_in, w_out]
    ms = [m_ffn1_w_gate, m_ffn1_w_up, m_ffn1_w_down, m_ffn2_w_gate, m_ffn2_w_up, m_ffn2_w_down, m_w_in, m_w_out]
    vs = [v_ffn1_w_gate, v_ffn1_w_up, v_ffn1_w_down, v_ffn2_w_gate, v_ffn2_w_up, v_ffn2_w_down, v_w_in, v_w_out]
    for key, nm, tr, w, m, vv in zip(keys, names, transposed, ws, ms, vs):
        view = (lambda a: jnp.swapaxes(a, 1, 2)[0]) if tr else (lambda a: a[0])
        back = (lambda a: jnp.swapaxes(a[None], 1, 2)) if tr else (lambda a: a[None])
        res = _rs_sum2_adam(pos, sum1[key][0], recv2[key], view(w), view(m), view(vv), "adam_" + nm)
        big[nm] = tuple(back(a) for a in res)

    loss = sg["loss"]
    all_names = ["ffn1_norm", "ffn1_w_gate", "ffn1_w_up", "ffn1_w_down", "mix_norm", "w_in", "sink_logits", "pool_w",
                 "pool_scale", "w_out", "ffn2_norm", "ffn2_w_gate", "ffn2_w_up", "ffn2_w_down", "final_norm"]
    outs = [loss, dx[None]]
    for idx, src in enumerate((sg, sd, sm, sv)):
        for nm in all_names:
            outs.append(big[nm][idx] if nm in big else src[nm])
    return tuple(outs)
```

```python
import functools

import jax
import jax.numpy as jnp
import numpy as np
from jax import lax
from jax.experimental import pallas as pl
from jax.experimental.pallas import tpu as pltpu

F32 = jnp.float32
BF16 = jnp.bfloat16
MESH = pl.DeviceIdType.MESH
N_DEV = 8

EPS = 1e-6
HEAD_DIM = 64
N_HEADS = 8
N_KV = 2
GROUP = N_HEADS // N_KV
ATTN_W = N_HEADS * HEAD_DIM
KV_W = N_KV * HEAD_DIM
POOL_W = 512
POOL_G = 4
POOL_GW = POOL_W // POOL_G
POOL_WINDOWS = (2, 4, 8, 16)
BLK = 128
ROT = 16
ROPE_THETA = 500000.0
SCORE_SCALE = HEAD_DIM ** -0.5

ADAM_LR, ADAM_B1, ADAM_B2, ADAM_EPS, ADAM_WD, ADAM_STEP = 0.001, 0.9, 0.999, 1e-08, 0.01, 10

VMEM_LIMIT = 56 * 1024 * 1024


def _cparams(sem=None, **kw):
    if sem is not None:
        kw["dimension_semantics"] = sem
    return pltpu.CompilerParams(vmem_limit_bytes=VMEM_LIMIT, **kw)


def _whole(shape):
    nd = len(shape)
    return pl.BlockSpec(shape, lambda *_: (0,) * nd, pipeline_mode=pl.Buffered(1))


def _sigmoid(z):
    return 1.0 / (1.0 + jnp.exp(-z))


def _dot_nt(a, b):
    return lax.dot_general(a, b, (((1,), (1,)), ((), ())), preferred_element_type=F32)


def _dot_nn(a, b):
    return lax.dot_general(a, b, (((1,), (0,)), ((), ())), preferred_element_type=F32)


def _dot_tn(a, b):
    return lax.dot_general(a, b, (((0,), (0,)), ((), ())), preferred_element_type=F32)


def _rope_tables(s):
    inv_freq = ROPE_THETA ** (-np.arange(0, ROT, 2, dtype=np.float64) / ROT)
    ang = np.arange(s, dtype=np.float64)[:, None] * inv_freq[None, :]
    c = np.ones((s, HEAD_DIM)); sa = np.zeros((s, HEAD_DIM)); sb = np.zeros((s, HEAD_DIM))
    c[:, :8] = np.cos(ang); c[:, 8:16] = np.cos(ang)
    sa[:, :8] = -np.sin(ang)
    sb[:, 8:16] = np.sin(ang)
    t = lambda a: jnp.asarray(np.tile(a, (1, 2)).astype(np.float32))
    return t(c), t(sa), t(sb)


def _pool_weight(gi, t, s_pos, s):
    half = POOL_WINDOWS[gi] // 2

    def win(lo, hi):
        a = np.clip(lo, 0, s); b = np.clip(hi + 1, 0, s)
        inside = (s_pos >= a) & (s_pos < b)
        return inside / np.maximum(b - a, 1)

    w = 0.5 * (win(t - half, t + half - 1) + win(t - half + 1, t + half)) - (t == s_pos)
    return w * ((t >= 0) & (t < s) & (s_pos >= 0) & (s_pos < s))


def _pool_tables(s):
    nb = s // BLK
    fwd = np.zeros((3, POOL_G, BLK, 3 * BLK), np.float32)
    bwd = np.zeros((3, POOL_G, BLK, 3 * BLK), np.float32)
    for vi, n in enumerate((0, 1 if nb > 2 else 0, nb - 1)):
        i = n * BLK + np.arange(BLK)[:, None]
        j = (n - 1) * BLK + np.arange(3 * BLK)[None, :]
        for gi in range(POOL_G):
            fwd[vi, gi] = _pool_weight(gi, i, j, s)
            bwd[vi, gi] = _pool_weight(gi, j, i, s)
    return jnp.asarray(fwd, dtype=BF16), jnp.asarray(bwd, dtype=BF16)


def _variant_index(n, nb):
    return jnp.where(n == 0, 0, jnp.where(n == nb - 1, 2, 1))


class _Exchange:
    inputs = ()
    out_shapes = ()
    sems = ()

    def start(self, srcs, outs, sems):
        raise NotImplementedError

    def middle(self, srcs, outs, sems):
        pass

    def finish(self, srcs, outs, sems):
        raise NotImplementedError


class _AllGather(_Exchange):
    def __init__(self, arrays):
        n = len(arrays)
        self.inputs = list(arrays)
        self.out_shapes = [jax.ShapeDtypeStruct((N_DEV,) + a.shape, a.dtype) for a in arrays]
        self.sems = [pltpu.SemaphoreType.DMA((n, 8)), pltpu.SemaphoreType.DMA((n, 8)), pltpu.SemaphoreType.DMA((n,))]

    def _parts(self, srcs, outs, sems):
        send_sems, recv_sems, local_sems = sems
        n = len(srcs)
        x, y, c = lax.axis_index("x"), lax.axis_index("y"), lax.axis_index("c")
        me, sibling, xn, yn, diag = (x, y, c), (x, y, 1 - c), (1 - x, y, c), (x, 1 - y, c), (1 - x, 1 - y, c)

        def place(a, dev, half=None):
            block = outs[a].at[4 * dev[0] + 2 * dev[1] + dev[2]]
            if half is None:
                return block
            r2 = outs[a].shape[1] // 2
            return block.at[pl.ds(half * r2, r2)]

        def copy(a, k, dev, to, half=None, src=None):
            where = place(a, dev, half)
            return pltpu.make_async_remote_copy(
                src_ref=where if src is None else src, dst_ref=where, send_sem=send_sems.at[a, k],
                recv_sem=recv_sems.at[a, k], device_id=to, device_id_type=MESH)

        def other(dev):
            return (dev[0], dev[1], 1 - dev[2])

        class Parts:
            mine = staticmethod(lambda: [pltpu.make_async_copy(srcs[a], place(a, me), local_sems.at[a])
                                         for a in range(n)])
            own = staticmethod(lambda: [copy(a, k, me, to, src=srcs[a]) for a in range(n)
                                        for k, to in ((0, sibling), (1, xn), (2, yn))])
            relay = staticmethod(lambda a: [copy(a, 3, xn, yn, half=0), copy(a, 4, yn, xn, half=1),
                                            copy(a, 5, xn, sibling), copy(a, 6, yn, sibling)])
            last = staticmethod(lambda a: copy(a, 7, diag, sibling))
            from_x = staticmethod(lambda a: copy(a, 1, xn, me))
            from_y = staticmethod(lambda a: copy(a, 2, yn, me))
            diag_halves = staticmethod(lambda a: [copy(a, 3, diag, me, half=0), copy(a, 4, diag, me, half=1)])
            from_sibling = staticmethod(lambda a: [copy(a, 0, sibling, me), copy(a, 5, other(xn), me),
                                                   copy(a, 6, other(yn), me), copy(a, 7, other(diag), me)])

        return n, Parts

    def start(self, srcs, outs, sems):
        _, p = self._parts(srcs, outs, sems)
        for cp in p.mine() + p.own():
            cp.start()

    def middle(self, srcs, outs, sems):
        n, p = self._parts(srcs, outs, sems)
        for a in range(n):
            p.from_x(a).wait_recv()
            p.from_y(a).wait_recv()
            for cp in p.relay(a):
                cp.start()

    def finish(self, srcs, outs, sems):
        n, p = self._parts(srcs, outs, sems)
        for a in range(n):
            for cp in p.diag_halves(a):
                cp.wait_recv()
            p.last(a).start()
        for a in range(n):
            for cp in p.from_sibling(a):
                cp.wait_recv()
        for cp in p.own() + [cp for a in range(n) for cp in p.relay(a) + [p.last(a)]]:
            cp.wait_send()
        for cp in p.mine():
            cp.wait()


class _RsStage2(_Exchange):
    def start(self, srcs, outs, sems):
        for cp in self._copies(srcs, outs, sems):
            cp.start()

    def finish(self, srcs, outs, sems):
        copies = self._copies(srcs, outs, sems)
        for cp in copies:
            cp.wait_recv()
        for cp in copies:
            cp.wait_send()


    def __init__(self, pbs):
        n = len(pbs)
        self.inputs = list(pbs)
        self.out_shapes = [jax.ShapeDtypeStruct((3,) + p.shape[1:], p.dtype) for p in pbs]
        self.sems = [pltpu.SemaphoreType.DMA((n, 3)), pltpu.SemaphoreType.DMA((n, 3))]

    def _copies(self, srcs, outs, sems):
        send_sems, recv_sems = sems
        x, y, c = lax.axis_index("x"), lax.axis_index("y"), lax.axis_index("c")
        chips = [(1 - x, y), (x, 1 - y), (1 - x, 1 - y)]
        return [pltpu.make_async_remote_copy(
            src_ref=srcs[a].at[2 * chip[0] + chip[1]], dst_ref=outs[a].at[j], send_sem=send_sems.at[a, j],
            recv_sem=recv_sems.at[a, j], device_id=(*chip, c), device_id_type=MESH)
            for a in range(len(srcs)) for j, chip in enumerate(chips)]


class _DirectGather(_Exchange):
    def __init__(self, arrays):
        n = len(arrays)
        self.inputs = list(arrays)
        self.out_shapes = [jax.ShapeDtypeStruct((N_DEV,) + a.shape, a.dtype) for a in arrays]
        self.sems = [pltpu.SemaphoreType.DMA((n, 7)), pltpu.SemaphoreType.DMA((n, 7)), pltpu.SemaphoreType.DMA((n,))]

    def _copies(self, srcs, outs, sems):
        send_sems, recv_sems, local_sems = sems
        x, y, c = lax.axis_index("x"), lax.axis_index("y"), lax.axis_index("c")
        me = 4 * x + 2 * y + c
        remote, local = [], []
        for a in range(len(srcs)):
            local.append(pltpu.make_async_copy(srcs[a], outs[a].at[me], local_sems.at[a]))
            for k in range(1, N_DEV):
                peer = (x ^ (k >> 2), y ^ ((k >> 1) & 1), c ^ (k & 1))
                remote.append(pltpu.make_async_remote_copy(
                    src_ref=srcs[a], dst_ref=outs[a].at[me], send_sem=send_sems.at[a, k - 1],
                    recv_sem=recv_sems.at[a, k - 1], device_id=peer, device_id_type=MESH))
        return remote, local

    def start(self, srcs, outs, sems):
        remote, local = self._copies(srcs, outs, sems)
        for cp in local + remote:
            cp.start()

    def finish(self, srcs, outs, sems):
        remote, local = self._copies(srcs, outs, sems)
        for cp in remote:
            cp.wait_recv()
        for cp in remote:
            cp.wait_send()
        for cp in local:
            cp.wait()


class _Both(_Exchange):
    def __init__(self, a, b):
        self.a, self.b = a, b
        self.inputs = list(a.inputs) + list(b.inputs)
        self.out_shapes = list(a.out_shapes) + list(b.out_shapes)
        self.sems = list(a.sems) + list(b.sems)

    def _split(self, srcs, outs, sems):
        na, oa, sa = len(self.a.inputs), len(self.a.out_shapes), len(self.a.sems)
        return (srcs[:na], outs[:oa], sems[:sa]), (srcs[na:], outs[oa:], sems[sa:])

    def start(self, srcs, outs, sems):
        pa, pb = self._split(srcs, outs, sems)
        self.a.start(*pa)
        self.b.start(*pb)

    def middle(self, srcs, outs, sems):
        pa, pb = self._split(srcs, outs, sems)
        self.a.middle(*pa)
        self.b.middle(*pb)

    def finish(self, srcs, outs, sems):
        pa, pb = self._split(srcs, outs, sems)
        self.a.finish(*pa)
        self.b.finish(*pb)


_ANY = pl.BlockSpec(memory_space=pl.ANY)


def _run_exchange(ex, name):
    n_in, n_out = len(ex.inputs), len(ex.out_shapes)

    def body(*refs):
        srcs, outs, sems = refs[:n_in], refs[n_in:n_in + n_out], refs[n_in + n_out:]
        ex.start(srcs, outs, sems)
        ex.middle(srcs, outs, sems)
        ex.finish(srcs, outs, sems)

    return pl.pallas_call(
        body, name=name, out_shape=list(ex.out_shapes), in_specs=[_ANY] * n_in, out_specs=[_ANY] * n_out,
        scratch_shapes=list(ex.sems),
    )(*ex.inputs)


CARRY_MIDDLE = 0.7


def _call(body, name, grid, in_specs, out_specs, out_shape, args, sem, carry=None, middle=CARRY_MIDDLE, scratch=()):
    if carry is None:
        return pl.pallas_call(functools.partial(body), name=name, grid=grid, in_specs=in_specs, out_specs=out_specs,
                              out_shape=out_shape, scratch_shapes=list(scratch), compiler_params=_cparams(sem))(*args)
    n_in, n_out = len(in_specs), len(out_specs)
    nc_in, nc_out = len(carry.inputs), len(carry.out_shapes)

    def carried(*refs):
        ins = refs[:n_in]
        c_in = refs[n_in:n_in + nc_in]
        outs = refs[n_in + nc_in:n_in + nc_in + n_out]
        c_out = refs[n_in + nc_in + n_out:n_in + nc_in + n_out + nc_out]
        own = refs[n_in + nc_in + n_out + nc_out:n_in + nc_in + n_out + nc_out + len(scratch)]
        sems = refs[n_in + nc_in + n_out + nc_out + len(scratch):]
        ids = [pl.program_id(i) for i in range(len(grid))]
        is_first = functools.reduce(jnp.logical_and, [i == 0 for i in ids])
        is_last = functools.reduce(jnp.logical_and, [i == g - 1 for i, g in zip(ids, grid)])
        @pl.when(is_first)
        def _():
            carry.start(c_in, c_out, sems)

        if middle is not None:
            @pl.when(functools.reduce(jnp.logical_and, [ids[0] == round(middle * (grid[0] - 1))]
                                      + [i == 0 for i in ids[1:]]))
            def _():
                carry.middle(c_in, c_out, sems)

        body(*ins, *outs, *own)

        @pl.when(is_last)
        def _():
            if middle is None:
                carry.middle(c_in, c_out, sems)
            carry.finish(c_in, c_out, sems)

    return pl.pallas_call(
        carried, name=name, grid=grid, in_specs=list(in_specs) + [_ANY] * nc_in,
        out_specs=list(out_specs) + [_ANY] * nc_out, out_shape=list(out_shape) + list(carry.out_shapes),
        scratch_shapes=list(scratch) + list(carry.sems), compiler_params=_cparams(sem))(*args, *carry.inputs)


def _rs_sum2_adam(pos, p, r2, w, m, v, name):
    _, rows, d = p.shape
    tr = rows // 2 if rows % 16 == 0 else rows

    def body(pos_ref, p_ref, r_ref, w_ref, m_ref, v_ref, g_ref, d_ref, nm_ref, nv_ref):
        r = r_ref[...].astype(F32)
        g = ((p_ref[...] + r[0]) + r[1]) + r[2]
        g_ref[...] = g
        d_ref[...], nm_ref[...], nv_ref[...] = _adam_math(w_ref[...], g, m_ref[...], v_ref[...])

    blk = pl.BlockSpec((tr, d), lambda i, pos_ref: (i, 0))
    return pl.pallas_call(
        body, name=name,
        grid_spec=pltpu.PrefetchScalarGridSpec(
            num_scalar_prefetch=1, grid=(rows // tr,),
            in_specs=[pl.BlockSpec((None, tr, d), lambda i, pos_ref: (2 * pos_ref[0] + pos_ref[1], i, 0)),
                      pl.BlockSpec((3, tr, d), lambda i, pos_ref: (0, i, 0)), blk, blk, blk],
            out_specs=[blk] * 4),
        out_shape=[jax.ShapeDtypeStruct((rows, d), F32)] * 4,
        compiler_params=_cparams(("arbitrary",)),
    )(pos, p, r2, w, m, v)


def _ffn_chunk(f):
    for cand in (256, 128):
        if f % cand == 0:
            return cand
    return f


def _loss_head(x, gg, target, loss_ref, dg_ref):
    @pl.when(pl.program_id(0) == 0)
    def _():
        loss_ref[...] = jnp.zeros_like(loss_ref)
        dg_ref[...] = jnp.zeros_like(dg_ref)

    r = lax.rsqrt(jnp.mean(x * x, axis=-1, keepdims=True) + EPS)
    xhat = x * r
    e = xhat * gg - target
    loss_ref[...] += 0.5 * jnp.sum(jnp.mean(e * e, axis=-1, keepdims=True), axis=0, keepdims=True)
    dy = e * (1.0 / x.shape[-1])
    dg_ref[...] += jnp.sum(dy * xhat, axis=0, keepdims=True)
    dxhat = dy * gg
    return r * (dxhat - xhat * jnp.mean(dxhat * xhat, axis=-1, keepdims=True))


def _ffn_gate(h, g, wg_t, name, tm, carry=None):
    s, d = h.shape
    f = wg_t.shape[0]

    def body(h_ref, g_ref, wg_ref, n_ref, gate_ref):
        x = h_ref[...]
        r = lax.rsqrt(jnp.mean(x * x, axis=-1, keepdims=True) + EPS)
        nb = (x * r * g_ref[...]).astype(BF16)
        n_ref[...] = nb
        gate_ref[...] = _dot_nt(nb, wg_ref[...]).astype(BF16)

    row = lambda w: pl.BlockSpec((tm, w), lambda i: (i, 0))
    return _call(body, name, (s // tm,), [row(d), _whole((1, d)), _whole((f, d))], [row(d), row(f)],
                 [jax.ShapeDtypeStruct((s, d), BF16), jax.ShapeDtypeStruct((s, f), BF16)], (h, g, wg_t),
                 ("arbitrary",), carry)


def _ffn_up(n, gate, wu_t, name, tm, carry=None):
    s, d = n.shape
    f = wu_t.shape[0]
    tf = _ffn_chunk(f)

    def body(n_ref, gate_ref, wu_ref, up_ref, act_ref, act_t_ref):
        nb = n_ref[...]
        for j in range(f // tf):
            sl = slice(j * tf, (j + 1) * tf)
            up = _dot_nt(nb, wu_ref[sl, :])
            gate = gate_ref[:, sl].astype(F32)
            up_ref[:, sl] = up.astype(BF16)
            act = (gate * _sigmoid(gate) * up).astype(BF16)
            act_ref[:, sl] = act
            act_t_ref[sl, :] = act.T

    row = lambda w: pl.BlockSpec((tm, w), lambda i: (i, 0))
    return _call(body, name, (s // tm,), [row(d), row(f), _whole((f, d))],
                 [row(f), row(f), pl.BlockSpec((f, tm), lambda i: (0, i))],
                 [jax.ShapeDtypeStruct((s, f), BF16)] * 2 + [jax.ShapeDtypeStruct((f, s), BF16)], (n, gate, wu_t),
                 ("arbitrary",), carry)


def _ffn_fwd(h, g, wg_t, wu_t, wd, name, tm, carry=None, head=None, middle=CARRY_MIDDLE):
    s, d = h.shape
    f = wg_t.shape[0]
    tf = _ffn_chunk(f)

    def body(h_ref, g_ref, wg_ref, wu_ref, wd_ref, *refs):
        if head is None:
            o_ref, n_ref, gate_ref, up_ref, act_t_ref, act_ref = refs
        else:
            gf_ref, t_ref, o_ref, n_ref, gate_ref, up_ref, act_t_ref, loss_ref, dgf_ref, act_ref = refs
        x = h_ref[...]
        r = lax.rsqrt(jnp.mean(x * x, axis=-1, keepdims=True) + EPS)
        nb = (x * r * g_ref[...]).astype(BF16)
        n_ref[...] = nb
        for j in range(f // tf):
            sl = slice(j * tf, (j + 1) * tf)
            gate = _dot_nt(nb, wg_ref[sl, :])
            up = _dot_nt(nb, wu_ref[sl, :])
            gate_ref[:, sl] = gate.astype(BF16)
            up_ref[:, sl] = up.astype(BF16)
            act = gate * _sigmoid(gate) * up
            act = act.astype(BF16)
            act_ref[:, sl] = act
            act_t_ref[sl, :] = act.T
        h_out = x + 0.5 * _dot_nn(act_ref[...], wd_ref[...])
        o_ref[...] = h_out if head is None else _loss_head(h_out, gf_ref[...], t_ref[...], loss_ref, dgf_ref)

    row = lambda w: pl.BlockSpec((tm, w), lambda i: (i, 0))
    in_specs = [row(d), _whole((1, d)), _whole((f, d)), _whole((f, d)), _whole((f, d))]
    out_specs = [row(d), row(d), row(f), row(f), pl.BlockSpec((f, tm), lambda i: (0, i))]
    out_shape = ([jax.ShapeDtypeStruct((s, d), F32), jax.ShapeDtypeStruct((s, d), BF16)]
                 + [jax.ShapeDtypeStruct((s, f), BF16)] * 2 + [jax.ShapeDtypeStruct((f, s), BF16)])
    args = (h, g, wg_t, wu_t, wd)
    scratch = (pltpu.VMEM((tm, f), BF16),)
    if head is not None:
        in_specs += [_whole((1, d)), row(d)]
        out_specs += [pl.BlockSpec((1, 1), lambda i: (0, 0)), pl.BlockSpec((1, d), lambda i: (0, 0))]
        out_shape += [jax.ShapeDtypeStruct((1, 1), F32), jax.ShapeDtypeStruct((1, d), F32)]
        args += tuple(head)
    return _call(body, name, (s // tm,), in_specs, out_specs, out_shape, args, ("arbitrary",), carry, middle, scratch)


def _gate_grads(dh_ref, gate_ref, up_ref, wd_ref, dgate_ref, dup_ref, dgate_t_ref, dup_t_ref, dhh_ref, tf):
    dhh = (0.5 * dh_ref[...]).astype(BF16)
    dhh_ref[...] = dhh
    for j in range(gate_ref.shape[1] // tf):
        sl = slice(j * tf, (j + 1) * tf)
        gt = gate_ref[:, sl].astype(F32)
        u = up_ref[:, sl].astype(F32)
        dact = _dot_nt(dhh, wd_ref[sl, :])
        sg = _sigmoid(gt)
        dup = dact * (gt * sg)
        dgate = dact * u * (sg * (1.0 + gt * (1.0 - sg)))
        dup, dgate = dup.astype(BF16), dgate.astype(BF16)
        dup_ref[:, sl] = dup
        dgate_ref[:, sl] = dgate
        dup_t_ref[sl, :] = dup.T
        dgate_t_ref[sl, :] = dgate.T


def _input_grad(h_ref, dh_ref, dgate_ref, dup_ref, g_ref, wg_ref, wu_ref, o_ref, dg_ref):
    x = h_ref[...]
    r = lax.rsqrt(jnp.mean(x * x, axis=-1, keepdims=True) + EPS)
    xhat = x * r
    dn = _dot_nn(dgate_ref[...], wg_ref[...]) + _dot_nn(dup_ref[...], wu_ref[...])
    dxhat = dn * g_ref[...]
    o_ref[...] = dh_ref[...] + r * (dxhat - xhat * jnp.mean(dxhat * xhat, axis=-1, keepdims=True))

    @pl.when(pl.program_id(0) == 0)
    def _():
        dg_ref[...] = jnp.zeros_like(dg_ref)

    dg_ref[...] += jnp.sum(dn * xhat, axis=0, keepdims=True)


def _ffn_bwd(h_in, dh_out, gate, up, g, wg_t, wu_t, wd, w_out, name, tm):
    s, d = h_in.shape
    f = gate.shape[1]
    w = w_out.shape[0]
    tf = _ffn_chunk(f)

    def body(h_ref, dh_ref, gate_ref, up_ref, g_ref, wg_ref, wu_ref, wd_ref, wo_ref,
             o_ref, dg_ref, dgate_t_ref, dup_t_ref, dhh_ref, dmix_ref, dhb_ref, dgate_ref, dup_ref):
        _gate_grads(dh_ref, gate_ref, up_ref, wd_ref, dgate_ref, dup_ref, dgate_t_ref, dup_t_ref, dhh_ref, tf)
        _input_grad(h_ref, dh_ref, dgate_ref, dup_ref, g_ref, wg_ref, wu_ref, o_ref, dg_ref)
        dhb = o_ref[...].astype(BF16)
        dhb_ref[...] = dhb
        dmix_ref[...] = _dot_nt(dhb, wo_ref[...])

    row = lambda c: pl.BlockSpec((tm, c), lambda i: (i, 0))
    col = pl.BlockSpec((f, tm), lambda i: (0, i))
    return pl.pallas_call(
        body, name=name, grid=(s // tm,),
        in_specs=[row(d), row(d), row(f), row(f), _whole((1, d)), _whole((f, d)), _whole((f, d)), _whole((f, d)),
                  _whole((w, d))],
        out_specs=[row(d), pl.BlockSpec((1, d), lambda i: (0, 0)), col, col, row(d), row(w), row(d)],
        out_shape=[jax.ShapeDtypeStruct((s, d), F32), jax.ShapeDtypeStruct((1, d), F32),
                   jax.ShapeDtypeStruct((f, s), BF16), jax.ShapeDtypeStruct((f, s), BF16),
                   jax.ShapeDtypeStruct((s, d), BF16), jax.ShapeDtypeStruct((s, w), F32),
                   jax.ShapeDtypeStruct((s, d), BF16)],
        scratch_shapes=[pltpu.VMEM((tm, f), BF16), pltpu.VMEM((tm, f), BF16)],
        compiler_params=_cparams(("arbitrary",)),
    )(h_in, dh_out, gate, up, g, wg_t, wu_t, wd, w_out)


def _ffn_bwd_gates(dh_out, gate, up, wd, name, tm, carry=None):
    s, d = dh_out.shape
    f = gate.shape[1]
    tf = _ffn_chunk(f)

    def body(dh_ref, gate_ref, up_ref, wd_ref, dgate_ref, dup_ref, dgate_t_ref, dup_t_ref, dhh_ref):
        _gate_grads(dh_ref, gate_ref, up_ref, wd_ref, dgate_ref, dup_ref, dgate_t_ref, dup_t_ref, dhh_ref, tf)

    row = lambda w: pl.BlockSpec((tm, w), lambda i: (i, 0))
    col = pl.BlockSpec((f, tm), lambda i: (0, i))
    return _call(
        body, name, (s // tm,), [row(d), row(f), row(f), _whole((f, d))], [row(f), row(f), col, col, row(d)],
        [jax.ShapeDtypeStruct((s, f), BF16)] * 2 + [jax.ShapeDtypeStruct((f, s), BF16)] * 2
        + [jax.ShapeDtypeStruct((s, d), BF16)],
        (dh_out, gate, up, wd), ("arbitrary",), carry)


def _ffn_bwd_input(h_in, dh_out, dgate, dup, g, wg_t, wu_t, name, tm, carry=None):
    s, d = h_in.shape
    f = dgate.shape[1]

    row = lambda w: pl.BlockSpec((tm, w), lambda i: (i, 0))
    return _call(
        _input_grad, name, (s // tm,),
        [row(d), row(d), row(f), row(f), _whole((1, d)), _whole((f, d)), _whole((f, d))],
        [row(d), pl.BlockSpec((1, d), lambda i: (0, 0))],
        [jax.ShapeDtypeStruct((s, d), F32), jax.ShapeDtypeStruct((1, d), F32)],
        (h_in, dh_out, dgate, dup, g, wg_t, wu_t), ("arbitrary",), carry)


def _wgrad_rs1(pos, a_t, b, name, carry=None):
    f, s = a_t.shape
    d = b.shape[1]
    fk = f // N_DEV
    nc_in = 0 if carry is None else len(carry.inputs)
    nc_out = 0 if carry is None else len(carry.out_shapes)

    def body(pos_ref, a_ref, b_ref, *refs):
        c_in = refs[:nc_in]
        p_ref, pb_ref = refs[nc_in:nc_in + 2]
        c_out = refs[nc_in + 2:nc_in + 2 + nc_out]
        stage, land, send_sems, recv_sems = refs[nc_in + 2 + nc_out:nc_in + 6 + nc_out]
        c_sems = refs[nc_in + 6 + nc_out:]
        t = pl.program_id(0)
        q = t % 4
        x, y, c = lax.axis_index("x"), lax.axis_index("y"), lax.axis_index("c")

        def push(k):
            return pltpu.make_async_remote_copy(src_ref=stage.at[k], dst_ref=land.at[k], send_sem=send_sems.at[k],
                                                recv_sem=recv_sems.at[k], device_id=(x, y, 1 - c), device_id_type=MESH)

        if carry is not None:
            @pl.when(t == 0)
            def _():
                carry.start(c_in, c_out, c_sems)

        g = _dot_nn(a_ref[...], b_ref[...])

        @pl.when(t < 4)
        def _():
            stage[q] = g.astype(BF16)
            push(q).start()

        @pl.when(t >= 4)
        def _():
            push(q).wait_recv()
            p = g + land[q].astype(F32)
            p_ref[...] = p
            pb_ref[...] = p.astype(BF16)

        @pl.when(t == 7)
        def _():
            for k in range(4):
                push(k).wait_send()
            if carry is not None:
                carry.middle(c_in, c_out, c_sems)
                carry.finish(c_in, c_out, c_sems)

    def shard(t, pos_ref):
        return 4 * ((t % 4) // 2) + 2 * (t % 2) + jnp.where(t < 4, 1 - pos_ref[2], pos_ref[2])

    out = pl.BlockSpec((None, fk, d), lambda t, pos_ref: (jnp.maximum(t - 4, 0), 0, 0))
    return pl.pallas_call(
        body, name=name,
        grid_spec=pltpu.PrefetchScalarGridSpec(
            num_scalar_prefetch=1, grid=(8,),
            in_specs=[pl.BlockSpec((fk, s), lambda t, pos_ref: (shard(t, pos_ref), 0)),
                      pl.BlockSpec((s, d), lambda t, pos_ref: (0, 0), pipeline_mode=pl.Buffered(1))]
            + [_ANY] * nc_in,
            out_specs=[out, out] + [_ANY] * nc_out,
            scratch_shapes=[pltpu.VMEM((4, fk, d), BF16), pltpu.VMEM((4, fk, d), BF16),
                            pltpu.SemaphoreType.DMA((4,)), pltpu.SemaphoreType.DMA((4,))]
            + ([] if carry is None else list(carry.sems))),
        out_shape=[jax.ShapeDtypeStruct((4, fk, d), F32), jax.ShapeDtypeStruct((4, fk, d), BF16)]
        + ([] if carry is None else list(carry.out_shapes)),
        compiler_params=_cparams(("arbitrary",)),
    )(pos, a_t, b, *([] if carry is None else carry.inputs))


def _rope(t, c, sa, sb, reps):
    c, sa, sb = (jnp.tile(v, (1, reps)) if reps > 1 else v for v in (c, sa, sb))
    w = t.shape[1]
    return t * c + pltpu.roll(t, w - 8, 1) * sa + pltpu.roll(t, 8, 1) * sb


def _rope_bwd(dt, c, sa, sb, reps):
    c, sa, sb = (jnp.tile(v, (1, reps)) if reps > 1 else v for v in (c, sa, sb))
    w = dt.shape[1]
    return dt * c + pltpu.roll(dt * sa, 8, 1) + pltpu.roll(dt * sb, w - 8, 1)


def _mix_in(h, g, win_t, tabs, name, tm, carry=None):
    s, d = h.shape
    n_in = win_t.shape[0]

    def body(h_ref, g_ref, w_ref, c_ref, sa_ref, sb_ref, q_ref, k_ref, v_ref, pc_ref, n_ref):
        x = h_ref[...]
        r = lax.rsqrt(jnp.mean(x * x, axis=-1, keepdims=True) + EPS)
        nb = (x * r * g_ref[...]).astype(BF16)
        n_ref[...] = nb
        u = _dot_nt(nb, w_ref[...])
        c, sa, sb = c_ref[...], sa_ref[...], sb_ref[...]
        q_ref[...] = _rope(u[:, :ATTN_W], c, sa, sb, ATTN_W // 128).astype(BF16)
        k_ref[...] = _rope(u[:, ATTN_W:ATTN_W + KV_W], c, sa, sb, 1).astype(BF16)
        v_ref[...] = u[:, ATTN_W + KV_W:ATTN_W + 2 * KV_W].astype(BF16)
        pc_ref[...] = u[:, ATTN_W + 2 * KV_W:]

    row = lambda w: pl.BlockSpec((tm, w), lambda i: (i, 0))
    return _call(
        body, name, (s // tm,),
        [row(d), _whole((1, d)), _whole((n_in, d)), row(128), row(128), row(128)],
        [row(ATTN_W), row(KV_W), row(KV_W), row(POOL_W), row(d)],
        [jax.ShapeDtypeStruct((s, ATTN_W), BF16), jax.ShapeDtypeStruct((s, KV_W), BF16),
         jax.ShapeDtypeStruct((s, KV_W), BF16), jax.ShapeDtypeStruct((s, POOL_W), F32),
         jax.ShapeDtypeStruct((s, d), BF16)],
        (h, g, win_t, *tabs), ("arbitrary",), carry)


def _band_mask(n, nb, transposed):
    shape = (3 * BLK, 2 * BLK) if transposed else (2 * BLK, 3 * BLK)
    i = lax.broadcasted_iota(jnp.int32, shape, 1 if transposed else 0) % BLK
    j = lax.broadcasted_iota(jnp.int32, shape, 0 if transposed else 1)
    kpos = (n - 1) * BLK + j
    return (j >= i) & (j <= i + 2 * BLK) & (kpos >= 0) & (kpos < nb * BLK)


def _block_diag(t, kh):
    tf = t.astype(F32)
    tr = pltpu.roll(tf, HEAD_DIM, 1)
    lo = lax.broadcasted_iota(jnp.int32, tf.shape, 1) < HEAD_DIM
    top, bot = (tf, tr) if kh == 0 else (tr, tf)
    return jnp.concatenate([jnp.where(lo, top, 0.0), jnp.where(lo, 0.0, bot)], axis=0).astype(BF16)


def _fold_diag(tbd):
    lo = lax.broadcasted_iota(jnp.int32, (3 * BLK, 2 * HEAD_DIM), 1) < HEAD_DIM
    t = jnp.where(lo, tbd[:3 * BLK], tbd[3 * BLK:])
    return t + pltpu.roll(t, HEAD_DIM, 1)


def _stack_pairs(x, kh):
    return jnp.concatenate([x[:, (2 * kh) * 128:(2 * kh + 1) * 128], x[:, (2 * kh + 1) * 128:(2 * kh + 2) * 128]], axis=0)


def _sink_of(sink_ref, kh, half, axis):
    shape = (2 * BLK, 1) if axis == 0 else (1, 2 * BLK)
    first = lax.broadcasted_iota(jnp.int32, shape, axis) < BLK
    return jnp.where(first, sink_ref[0, GROUP * kh + half], sink_ref[0, GROUP * kh + 2 + half])


def _softmax_sink(sc, valid, sink, axis):
    sc = jnp.where(valid, sc, -1e30)
    m = jnp.maximum(jnp.max(sc, axis=axis, keepdims=True), sink)
    e = jnp.exp(sc - m)
    es = jnp.exp(sink - m)
    inv = 1.0 / (jnp.sum(e, axis=axis, keepdims=True) + es)
    return e * inv, es * inv


def _attn_blocks_per_step(nb):
    return next(nq for nq in (4, 2, 1) if nb % nq == 0)


def _band_specs(nq, nb, w, col=0):
    return [pl.BlockSpec((BLK, w), lambda m: (jnp.maximum(nq * m - 1, 0), col)),
            pl.BlockSpec((nq * BLK, w), lambda m: (m, col)),
            pl.BlockSpec((BLK, w), lambda m: (jnp.minimum(nq * m + nq, nb - 1), col))]


def _attn_pool_fwd(h, w_out, q, k, v, pc, sink, pool_w, pool_scale, pband, name, carry=None, middle=CARRY_MIDDLE):
    s, d = h.shape
    nb = s // BLK
    nq = _attn_blocks_per_step(nb)

    def body(sink_ref, q_ref, k0, k1, k2, v0, v1, v2, p0, p1, p2, pw_ref, ps_ref, pb_ref, h_ref, wo_ref,
             h_out_ref, o_t_ref, o_ref):
        kall = jnp.concatenate([k0[...], k1[...], k2[...]], axis=0)
        vall = jnp.concatenate([v0[...], v1[...], v2[...]], axis=0)
        pall = jnp.concatenate([p0[...], p1[...], p2[...]], axis=0).astype(BF16)
        qall = q_ref[...] * SCORE_SCALE
        for j in range(nq):
            n = pl.program_id(0) * nq + j
            rows, band = slice(j * BLK, (j + 1) * BLK), slice(j * BLK, (j + 3) * BLK)
            valid = _band_mask(n, nb, False)
            kb, vb, qs = kall[band], vall[band], qall[rows]
            for kh in range(N_KV):
                sc = _dot_nt(_stack_pairs(qs, kh), _block_diag(kb, kh))
                p = [_softmax_sink(sc[:, half * 3 * BLK:(half + 1) * 3 * BLK], valid,
                                   _sink_of(sink_ref, kh, half, 0), 1)[0] for half in range(2)]
                o2 = _dot_nn(jnp.concatenate(p, axis=1).astype(BF16), _block_diag(vb, kh)).astype(BF16)
                o_ref[rows, (2 * kh) * 128:(2 * kh + 1) * 128] = o2[:BLK]
                o_ref[rows, (2 * kh + 1) * 128:(2 * kh + 2) * 128] = o2[BLK:]
            ext = pall[band]
            var = _variant_index(n, nb)
            for gi in range(POOL_G):
                gsl = slice(gi * POOL_GW, (gi + 1) * POOL_GW)
                dg = _dot_nn(pb_ref[var, gi], ext[:, gsl])
                yg = _dot_nn(dg.astype(BF16), pw_ref[gi].astype(BF16))
                o_ref[rows, ATTN_W + gi * POOL_GW:ATTN_W + (gi + 1) * POOL_GW] = (yg * ps_ref[:, gsl]).astype(BF16)
        mix = o_ref[...]
        o_t_ref[...] = mix.T
        h_out_ref[...] = h_ref[...] + _dot_nn(mix, wo_ref[...])

    mix_w = ATTN_W + POOL_W
    return _call(
        body, name, (nb // nq,),
        [pl.BlockSpec(memory_space=pltpu.SMEM), pl.BlockSpec((nq * BLK, ATTN_W), lambda m: (m, 0)),
         *_band_specs(nq, nb, KV_W), *_band_specs(nq, nb, KV_W), *_band_specs(nq, nb, POOL_W),
         _whole((POOL_G, POOL_GW, POOL_GW)), _whole((1, POOL_W)), _whole(pband.shape),
         pl.BlockSpec((nq * BLK, d), lambda m: (m, 0)), _whole((mix_w, d))],
        [pl.BlockSpec((nq * BLK, d), lambda m: (m, 0)), pl.BlockSpec((mix_w, nq * BLK), lambda m: (0, m))],
        [jax.ShapeDtypeStruct((s, d), F32), jax.ShapeDtypeStruct((mix_w, s), BF16)],
        (sink, q, k, k, k, v, v, v, pc, pc, pc, pool_w, pool_scale, pband, h, w_out), ("arbitrary",), carry, middle,
        scratch=(pltpu.VMEM((nq * BLK, mix_w), BF16),))


def _attn_pool_bwd(q, k, v, pc, dmix, sink, pool_w, pool_scale, pband, ptband, name, carry=None):
    s = q.shape[0]
    nb = s // BLK
    nq = _attn_blocks_per_step(nb)

    def body(sink_ref, q_ref, k0, k1, k2, v0, v1, v2, p0, p1, p2, da_ref, d0, d1, d2, pw_ref, ps_ref, pb_ref, ptb_ref,
             dq_ref, dk_ref, dv_ref, dpc_ref, dsink_ref, dpw_ref, dps_ref):
        @pl.when(pl.program_id(0) == 0)
        def _():
            dsink_ref[...] = jnp.zeros_like(dsink_ref)
            dpw_ref[...] = jnp.zeros_like(dpw_ref)
            dps_ref[...] = jnp.zeros_like(dps_ref)

        kall = jnp.concatenate([k0[...], k1[...], k2[...]], axis=0)
        vall = jnp.concatenate([v0[...], v1[...], v2[...]], axis=0)
        pall = jnp.concatenate([p0[...], p1[...], p2[...]], axis=0).astype(BF16)
        dpall = jnp.concatenate([d0[...], d1[...], d2[...]], axis=0)
        lo = lax.broadcasted_iota(jnp.int32, (3 * BLK, KV_W), 1) < HEAD_DIM
        for j in range(nq):
            n = pl.program_id(0) * nq + j
            rows, band = slice(j * BLK, (j + 1) * BLK), slice(j * BLK, (j + 3) * BLK)
            valid = _band_mask(n, nb, True)
            kb, vb, qb = kall[band], vall[band], q_ref[rows, :]
            qs = qb * SCORE_SCALE
            da = da_ref[rows, :].astype(BF16)
            dk_fold, dv_fold = [], []
            for kh in range(N_KV):
                kbd, vbd = _block_diag(kb, kh), _block_diag(vb, kh)
                q2, do2 = _stack_pairs(qb, kh), _stack_pairs(da, kh)
                sc_t = _dot_nt(kbd, _stack_pairs(qs, kh))
                dp_t = _dot_nt(vbd, do2)
                p_t, ds_t = [], []
                for half in range(2):
                    keys = slice(half * 3 * BLK, (half + 1) * 3 * BLK)
                    p, ps = _softmax_sink(sc_t[keys], valid, _sink_of(sink_ref, kh, half, 1), 0)
                    delta = jnp.sum(p * dp_t[keys], axis=0, keepdims=True)
                    p_t.append(p.astype(BF16))
                    ds_t.append((p * (dp_t[keys] - delta)).astype(BF16))
                    dsk = -ps * delta
                    for pair in range(2):
                        h = GROUP * kh + 2 * pair + half
                        part = jnp.sum(dsk[:, pair * BLK:(pair + 1) * BLK], axis=1, keepdims=True)
                        dsink_ref[h:h + 1, :] += jnp.broadcast_to(part, (1, 128))
                p_t = jnp.concatenate(p_t, axis=0)
                ds_t = jnp.concatenate(ds_t, axis=0)
                dq2 = _dot_tn(ds_t, kbd) * SCORE_SCALE
                dq_ref[rows, (2 * kh) * 128:(2 * kh + 1) * 128] = dq2[:BLK]
                dq_ref[rows, (2 * kh + 1) * 128:(2 * kh + 2) * 128] = dq2[BLK:]
                dk_fold.append(_fold_diag(_dot_nn(ds_t, q2)) * SCORE_SCALE)
                dv_fold.append(_fold_diag(_dot_nn(p_t, do2)))
            dk_all = jnp.where(lo, dk_fold[0], dk_fold[1])
            dv_all = jnp.where(lo, dv_fold[0], dv_fold[1])
            for t in range(3):
                dk_ref[j, t] = dk_all[t * BLK:(t + 1) * BLK]
                dv_ref[j, t] = dv_all[t * BLK:(t + 1) * BLK]
            ext, dpe = pall[band], dpall[band]
            dpc_cur = dpall[(j + 1) * BLK:(j + 2) * BLK]
            var = _variant_index(n, nb)
            for gi in range(POOL_G):
                gsl = slice(gi * POOL_GW, (gi + 1) * POOL_GW)
                wg = pw_ref[gi].astype(BF16)
                sc = ps_ref[:, gsl]
                dgb = _dot_nn(pb_ref[var, gi], ext[:, gsl]).astype(BF16)
                yg = _dot_nn(dgb, wg)
                dps_ref[:, gsl] += jnp.sum(dpc_cur[:, gsl] * yg, axis=0, keepdims=True)
                dpw_ref[gi] += _dot_tn(dgb, (dpc_cur[:, gsl] * sc).astype(BF16))
                dd = _dot_nt((dpe[:, gsl] * sc).astype(BF16), wg)
                dpc_ref[rows, gsl] = _dot_nn(ptb_ref[var, gi], dd.astype(BF16))

    fixed = lambda shape: pl.BlockSpec(shape, lambda m: (0,) * len(shape))
    return _call(
        body, name, (nb // nq,),
        [pl.BlockSpec(memory_space=pltpu.SMEM), pl.BlockSpec((nq * BLK, ATTN_W), lambda m: (m, 0)),
         *_band_specs(nq, nb, KV_W), *_band_specs(nq, nb, KV_W), *_band_specs(nq, nb, POOL_W),
         pl.BlockSpec((nq * BLK, ATTN_W), lambda m: (m, 0)), *_band_specs(nq, nb, POOL_W, 1),
         _whole((POOL_G, POOL_GW, POOL_GW)), _whole((1, POOL_W)), _whole(pband.shape), _whole(ptband.shape)],
        [pl.BlockSpec((nq * BLK, ATTN_W), lambda m: (m, 0)),
         pl.BlockSpec((nq, 3, BLK, KV_W), lambda m: (m, 0, 0, 0)),
         pl.BlockSpec((nq, 3, BLK, KV_W), lambda m: (m, 0, 0, 0)),
         pl.BlockSpec((nq * BLK, POOL_W), lambda m: (m, 0)),
         fixed((N_HEADS, 128)), fixed((POOL_G, POOL_GW, POOL_GW)), fixed((1, POOL_W))],
        [jax.ShapeDtypeStruct((s, ATTN_W), F32), jax.ShapeDtypeStruct((nb, 3, BLK, KV_W), F32),
         jax.ShapeDtypeStruct((nb, 3, BLK, KV_W), F32), jax.ShapeDtypeStruct((s, POOL_W), F32),
         jax.ShapeDtypeStruct((N_HEADS, 128), F32),
         jax.ShapeDtypeStruct((POOL_G, POOL_GW, POOL_GW), F32), jax.ShapeDtypeStruct((1, POOL_W), F32)],
        (sink, q, k, k, k, v, v, v, pc, pc, pc, dmix, dmix, dmix, dmix, pool_w, pool_scale, pband, ptband),
        ("arbitrary",), carry)


def _mix_out(h, mix, w_out, name, tm, scale=1.0, carry=None, middle=CARRY_MIDDLE):
    s, d = h.shape
    w = mix.shape[1]

    def body(h_ref, m_ref, w_ref, o_ref):
        o_ref[...] = h_ref[...] + scale * _dot_nn(m_ref[...], w_ref[...])

    row = lambda c: pl.BlockSpec((tm, c), lambda i: (i, 0))
    return _call(body, name, (s // tm,), [row(d), row(w), _whole((w, d))], [row(d)],
                 [jax.ShapeDtypeStruct((s, d), F32)], (h, mix, w_out), ("arbitrary",), carry, middle)


def _mix_in_bwd(h, dh, g, win_t, dq, dkp, dvp, dpc, tabs, name, tm, carry=None):
    s, d = h.shape
    nb = s // BLK
    nt = tm // BLK
    n_in = win_t.shape[0]

    def band_sum(n, before, own, after, prev_last, next_first):
        lo = (n > 0).astype(F32)
        hi = (n < s // tm - 1).astype(F32)
        blocks = []
        for b in range(nt):
            from_prev = prev_last[...] * lo if b == 0 else before[b - 1]
            from_next = next_first[...] * hi if b == nt - 1 else after[b + 1]
            blocks.append(from_prev + own[b] + from_next)
        return jnp.concatenate(blocks, axis=0)

    def body(h_ref, dh_ref, g_ref, w_ref, dq_ref, k2, k1, k0, kp, kn, v2, v1, v0, vp, vn, dpc_ref, c_ref, sa_ref,
             sb_ref, o_ref, du_ref, dg_ref):
        n = pl.program_id(0)
        dk = band_sum(n, k2, k1, k0, kp, kn)
        dv = band_sum(n, v2, v1, v0, vp, vn)
        c, sa, sb = c_ref[...], sa_ref[...], sb_ref[...]
        du = jnp.concatenate([_rope_bwd(dq_ref[...], c, sa, sb, ATTN_W // 128), _rope_bwd(dk, c, sa, sb, 1), dv,
                              dpc_ref[...]], axis=1)
        du_ref[...] = du.T.astype(BF16)
        dn = _dot_nn(du.astype(BF16), w_ref[...])
        x = h_ref[...]
        r = lax.rsqrt(jnp.mean(x * x, axis=-1, keepdims=True) + EPS)
        xhat = x * r
        dxhat = dn * g_ref[...]
        o_ref[...] = dh_ref[...] + r * (dxhat - xhat * jnp.mean(dxhat * xhat, axis=-1, keepdims=True))

        @pl.when(n == 0)
        def _():
            dg_ref[...] = jnp.zeros_like(dg_ref)

        dg_ref[...] += jnp.sum(dn * xhat, axis=0, keepdims=True)

    row = lambda w: pl.BlockSpec((tm, w), lambda n: (n, 0))
    slot = lambda t: pl.BlockSpec((nt, None, BLK, KV_W), lambda n, t=t: (n, t, 0, 0))
    parts = [slot(2), slot(1), slot(0),
             pl.BlockSpec((None, None, BLK, KV_W), lambda n: (jnp.maximum(nt * n - 1, 0), 2, 0, 0)),
             pl.BlockSpec((None, None, BLK, KV_W), lambda n: (jnp.minimum(nt * n + nt, nb - 1), 0, 0, 0))]
    return _call(
        body, name, (s // tm,),
        [row(d), row(d), _whole((1, d)), _whole((n_in, d)), row(ATTN_W), *parts, *parts, row(POOL_W),
         row(128), row(128), row(128)],
        [row(d), pl.BlockSpec((n_in, tm), lambda n: (0, n)), pl.BlockSpec((1, d), lambda n: (0, 0))],
        [jax.ShapeDtypeStruct((s, d), F32), jax.ShapeDtypeStruct((n_in, s), BF16), jax.ShapeDtypeStruct((1, d), F32)],
        (h, dh, g, win_t, dq, *[dkp] * 5, *[dvp] * 5, dpc, *tabs), ("arbitrary",), carry)


def _adam_math(w, g, m, v):
    m = ADAM_B1 * m + (1.0 - ADAM_B1) * g
    v = ADAM_B2 * v + (1.0 - ADAM_B2) * (g * g)
    m_hat = m / (1.0 - ADAM_B1 ** ADAM_STEP)
    v_hat = v / (1.0 - ADAM_B2 ** ADAM_STEP)
    delta = -ADAM_LR * (m_hat / (jnp.sqrt(v_hat) + ADAM_EPS) + ADAM_WD * w)
    return delta, m, v


def _adam_small(w, parts, late, m, v, name):
    rows, cols = w.shape

    def body(w_ref, p_ref, l_ref, m_ref, v_ref, g_ref, d_ref, nm_ref, nv_ref):
        g, gl = p_ref[0], l_ref[0]
        for k in range(1, N_DEV):
            g = g + p_ref[k]
            gl = gl + l_ref[k]
        g_ref[...] = g
        g_ref[SMALL_NORM1:SMALL_NORM1 + 8, :] = g[SMALL_NORM1:SMALL_NORM1 + 8] + gl
        d_ref[...], nm_ref[...], nv_ref[...] = _adam_math(w_ref[...], g_ref[...], m_ref[...], v_ref[...])

    return pl.pallas_call(
        body, name=name, out_shape=[jax.ShapeDtypeStruct((rows, cols), F32)] * 4,
    )(w, parts, late, m, v)


SMALL_NORM1 = 512


def _pack_small(norm1, normm, norm2, normf, sink, pool_w, pool_scale, loss=None):
    scale_rows = jnp.pad(pool_scale.reshape(4, 128), ((0, 4), (0, 0)))
    last_rows = jnp.pad(sink.reshape(1, N_HEADS), ((0, 7), (0, 128 - N_HEADS)))
    if loss is not None:
        last_rows = last_rows + jnp.pad(loss.reshape(1, 1), ((1, 6), (0, 127)))
    return jnp.concatenate([pool_w.reshape(512, 128), norm1.reshape(8, 128), normm.reshape(8, 128),
                            norm2.reshape(8, 128), normf.reshape(8, 128), scale_rows, last_rows], axis=0)


def _unpack_small(p):
    return dict(pool_w=p[:512].reshape(1, POOL_G, POOL_GW, POOL_GW), ffn1_norm=p[512:520].reshape(1, 1024),
                mix_norm=p[520:528].reshape(1, 1024), ffn2_norm=p[528:536].reshape(1, 1024),
                final_norm=p[536:544].reshape(1024), pool_scale=p[544:548].reshape(1, POOL_W),
                sink_logits=p[552, :N_HEADS].reshape(1, N_HEADS), loss=p[553, 0])


def kernel(x, ffn1_norm, ffn1_w_gate, ffn1_w_up, ffn1_w_down, mix_norm, w_in, sink_logits, pool_w, pool_scale, w_out, ffn2_norm, ffn2_w_gate, ffn2_w_up, ffn2_w_down, final_norm, loss_target, m_ffn1_norm, m_ffn1_w_gate, m_ffn1_w_up, m_ffn1_w_down, m_mix_norm, m_w_in, m_sink_logits, m_pool_w, m_pool_scale, m_w_out, m_ffn2_norm, m_ffn2_w_gate, m_ffn2_w_up, m_ffn2_w_down, m_final_norm, v_ffn1_norm, v_ffn1_w_gate, v_ffn1_w_up, v_ffn1_w_down, v_mix_norm, v_w_in, v_sink_logits, v_pool_w, v_pool_scale, v_w_out, v_ffn2_norm, v_ffn2_w_gate, v_ffn2_w_up, v_ffn2_w_down, v_final_norm):
    s, d = x.shape[1], x.shape[2]
    tm = min(512, s)
    tm_bwd = min(256, s)
    pos = jnp.stack([lax.axis_index("x"), lax.axis_index("y"), lax.axis_index("c")]).astype(jnp.int32)

    t_bf = lambda w: w[0].T.astype(BF16)
    full = lambda a: a.reshape(N_DEV * a.shape[1], d)
    (wg1,) = map(full, _run_exchange(_AllGather([t_bf(ffn1_w_gate)]), "gather_ffn1_gate"))

    tabs = _rope_tables(s)
    pband, ptband = _pool_tables(s)
    g1, gm, g2, gf = ffn1_norm, mix_norm, ffn2_norm, final_norm.reshape(1, d)

    x0 = x[0]
    n1, gate1, wu1 = _ffn_gate(x0, g1, wg1, "ffn1_gate", tm, carry=_AllGather([t_bf(ffn1_w_up)]))
    up1, act1, act1_t, wd1 = _ffn_up(n1, gate1, full(wu1), "ffn1_up", tm,
                                     carry=_AllGather([ffn1_w_down[0].astype(BF16)]))
    wu1, wd1 = full(wu1), full(wd1)
    h1, win_t, wout = _mix_out(x0, act1, wd1, "ffn1_down", tm, scale=0.5,
                               carry=_AllGather([t_bf(w_in), w_out[0].astype(BF16)]))
    win_t, wout = full(win_t), full(wout)
    q, k, v, pc, n2, wg2 = _mix_in(h1, gm, win_t, tabs, "mix_in", tm, carry=_AllGather([t_bf(ffn2_w_gate)]))
    wg2 = full(wg2)
    h2, mix_t, *gathered = _attn_pool_fwd(h1, wout, q, k, v, pc, sink_logits, pool_w[0], pool_scale, pband,
                                          "attn_pool_fwd",
                                          carry=_AllGather([t_bf(ffn2_w_up), ffn2_w_down[0].astype(BF16)]))
    wu2, wd2 = map(full, gathered)
    dh3, n3, gate2, up2, act2_t, loss_part, dgf = _ffn_fwd(h2, g2, wg2, wu2, wd2, "ffn2_fwd", tm,
                                                           head=(gf, loss_target[0]))

    sum1, recv2 = {}, {}

    def stage2(keys):
        return _RsStage2([sum1[key][1] for key in keys])

    dh2, dg2, dgate2_t, dup2_t, dhh3, dmix, dh2b = _ffn_bwd(h2, dh3, gate2, up2, g2, wg2, wu2, wd2, wout, "ffn2_bwd",
                                                            tm_bwd)
    sum1["g2"] = _wgrad_rs1(pos, dgate2_t, n3, "wgrad_gate2")
    sum1["u2"] = _wgrad_rs1(pos, dup2_t, n3, "wgrad_up2")
    sum1["d2"] = _wgrad_rs1(pos, act2_t, dhh3, "wgrad_down2")
    sum1["out"] = _wgrad_rs1(pos, mix_t, dh2b, "wgrad_out")
    dq, dkp, dvp, dpc, dsink, dpw, dps, *r2 = _attn_pool_bwd(
        q, k, v, pc, dmix, sink_logits, pool_w[0], pool_scale, pband, ptband, "attn_pool_bwd",
        carry=stage2(["g2", "u2", "d2"]))
    recv2.update(zip(["g2", "u2", "d2"], r2))
    dh1, du_t, dgm, recv2["out"] = _mix_in_bwd(h1, dh2, gm, win_t, dq, dkp, dvp, dpc, tabs, "mix_in_bwd", tm,
                                               carry=stage2(["out"]))
    sum1["in"] = _wgrad_rs1(pos, du_t, n2, "wgrad_in")
    small_part = _pack_small(jnp.zeros_like(dgm), dgm, dg2, dgf, dsink[:, 0], dpw, dps, loss_part)
    dgate1, dup1, dgate1_t, dup1_t, dhh1, recv2["in"], small_all = _ffn_bwd_gates(
        dh1, gate1, up1, wd1, "ffn1_bwd_gates", tm, carry=_Both(stage2(["in"]), _AllGather([small_part])))
    sum1["g1"] = _wgrad_rs1(pos, dgate1_t, n1, "wgrad_gate1")
    *sum1["u1"], recv2["g1"] = _wgrad_rs1(pos, dup1_t, n1, "wgrad_up1", carry=stage2(["g1"]))
    *sum1["d1"], recv2["u1"] = _wgrad_rs1(pos, act1_t, dhh1, "wgrad_down1", carry=stage2(["u1"]))
    dx, dg1, recv2["d1"] = _ffn_bwd_input(x0, dh1, dgate1, dup1, g1, wg1, wu1, "ffn1_bwd_input", tm,
                                          carry=stage2(["d1"]))

    (dg1_all,) = _run_exchange(_DirectGather([dg1.reshape(8, 128)]), "gather_norm1_grad")
    pk = lambda a, b, c_, e, s_, pw_, psc: _pack_small(a, b, c_, e, s_[0], pw_[0], psc)
    small_w = pk(ffn1_norm, mix_norm, ffn2_norm, final_norm, sink_logits, pool_w, pool_scale)
    small_m = pk(m_ffn1_norm, m_mix_norm, m_ffn2_norm, m_final_norm, m_sink_logits, m_pool_w, m_pool_scale)
    small_v = pk(v_ffn1_norm, v_mix_norm, v_ffn2_norm, v_final_norm, v_sink_logits, v_pool_w, v_pool_scale)
    sg, sd, sm, sv = [_unpack_small(a)
                      for a in _adam_small(small_w, small_all, dg1_all, small_m, small_v, "adam_small")]

    big = {}
    keys = ["g1", "u1", "d1", "g2", "u2", "d2", "in", "out"]
    names = ["ffn1_w_gate", "ffn1_w_up", "ffn1_w_down", "ffn2_w_gate", "ffn2_w_up", "ffn2_w_down", "w_in", "w_out"]
    transposed = [True, True, False, True, True, False, True, False]
    ws = [ffn1_w_gate, ffn1_w_up, ffn1_w_down, ffn2_w_gate, ffn2_w_up, ffn2_w_down, w_in, w_out]
    ms = [m_ffn1_w_gate, m_ffn1_w_up, m_ffn1_w_down, m_ffn2_w_gate, m_ffn2_w_up, m_ffn2_w_down, m_w_in, m_w_out]
    vs = [v_ffn1_w_gate, v_ffn1_w_up, v_ffn1_w_down, v_ffn2_w_gate, v_ffn2_w_up, v_ffn2_w_down, v_w_in, v_w_out]
    for key, nm, tr, w, m, vv in zip(keys, names, transposed, ws, ms, vs):
        view = (lambda a: jnp.swapaxes(a, 1, 2)[0]) if tr else (lambda a: a[0])
        back = (lambda a: jnp.swapaxes(a[None], 1, 2)) if tr else (lambda a: a[None])
        res = _rs_sum2_adam(pos, sum1[key][0], recv2[key], view(w), view(m), view(vv), "adam_" + nm)
        big[nm] = tuple(back(a) for a in res)

    loss = sg["loss"]
    all_names = ["ffn1_norm", "ffn1_w_gate", "ffn1_w_up", "ffn1_w_down", "mix_norm", "w_in", "sink_logits", "pool_w",
                 "pool_scale", "w_out", "ffn2_norm", "ffn2_w_gate", "ffn2_w_up", "ffn2_w_down", "final_norm"]
    outs = [loss, dx[None]]
    for idx, src in enumerate((sg, sd, sm, sv)):
        for nm in all_names:
            outs.append(big[nm][idx] if nm in big else src[nm])
    return tuple(outs)
```

```python
import functools

import jax
import jax.numpy as jnp
import numpy as np
from jax import lax
from jax.experimental import pallas as pl
from jax.experimental.pallas import tpu as pltpu

F32 = jnp.float32
BF16 = jnp.bfloat16
MESH = pl.DeviceIdType.MESH
N_DEV = 8

EPS = 1e-6
HEAD_DIM = 64
N_HEADS = 8
N_KV = 2
GROUP = N_HEADS // N_KV
ATTN_W = N_HEADS * HEAD_DIM
KV_W = N_KV * HEAD_DIM
POOL_W = 512
POOL_G = 4
POOL_GW = POOL_W // POOL_G
POOL_WINDOWS = (2, 4, 8, 16)
BLK = 128
ROT = 16
ROPE_THETA = 500000.0
SCORE_SCALE = HEAD_DIM ** -0.5

ADAM_LR, ADAM_B1, ADAM_B2, ADAM_EPS, ADAM_WD, ADAM_STEP = 0.001, 0.9, 0.999, 1e-08, 0.01, 10

VMEM_LIMIT = 56 * 1024 * 1024


def _cparams(sem=None, **kw):
    if sem is not None:
        kw["dimension_semantics"] = sem
    return pltpu.CompilerParams(vmem_limit_bytes=VMEM_LIMIT, **kw)


def _whole(shape):
    nd = len(shape)
    return pl.BlockSpec(shape, lambda *_: (0,) * nd, pipeline_mode=pl.Buffered(1))


def _sigmoid(z):
    return 1.0 / (1.0 + jnp.exp(-z))


def _dot_nt(a, b):
    return lax.dot_general(a, b, (((1,), (1,)), ((), ())), preferred_element_type=F32)


def _dot_nn(a, b):
    return lax.dot_general(a, b, (((1,), (0,)), ((), ())), preferred_element_type=F32)


def _dot_tn(a, b):
    return lax.dot_general(a, b, (((0,), (0,)), ((), ())), preferred_element_type=F32)


def _rope_tables(s):
    inv_freq = ROPE_THETA ** (-np.arange(0, ROT, 2, dtype=np.float64) / ROT)
    ang = np.arange(s, dtype=np.float64)[:, None] * inv_freq[None, :]
    c = np.ones((s, HEAD_DIM)); sa = np.zeros((s, HEAD_DIM)); sb = np.zeros((s, HEAD_DIM))
    c[:, :8] = np.cos(ang); c[:, 8:16] = np.cos(ang)
    sa[:, :8] = -np.sin(ang)
    sb[:, 8:16] = np.sin(ang)
    t = lambda a: jnp.asarray(np.tile(a, (1, 2)).astype(np.float32))
    return t(c), t(sa), t(sb)


def _pool_weight(gi, t, s_pos, s):
    half = POOL_WINDOWS[gi] // 2

    def win(lo, hi):
        a = np.clip(lo, 0, s); b = np.clip(hi + 1, 0, s)
        inside = (s_pos >= a) & (s_pos < b)
        return inside / np.maximum(b - a, 1)

    w = 0.5 * (win(t - half, t + half - 1) + win(t - half + 1, t + half)) - (t == s_pos)
    return w * ((t >= 0) & (t < s) & (s_pos >= 0) & (s_pos < s))


def _pool_tables(s):
    nb = s // BLK
    fwd = np.zeros((3, POOL_G, BLK, 3 * BLK), np.float32)
    bwd = np.zeros((3, POOL_G, BLK, 3 * BLK), np.float32)
    for vi, n in enumerate((0, 1 if nb > 2 else 0, nb - 1)):
        i = n * BLK + np.arange(BLK)[:, None]
        j = (n - 1) * BLK + np.arange(3 * BLK)[None, :]
        for gi in range(POOL_G):
            fwd[vi, gi] = _pool_weight(gi, i, j, s)
            bwd[vi, gi] = _pool_weight(gi, j, i, s)
    return jnp.asarray(fwd, dtype=BF16), jnp.asarray(bwd, dtype=BF16)


def _variant_index(n, nb):
    return jnp.where(n == 0, 0, jnp.where(n == nb - 1, 2, 1))


class _Exchange:
    inputs = ()
    out_shapes = ()
    sems = ()

    def start(self, srcs, outs, sems):
        raise NotImplementedError

    def middle(self, srcs, outs, sems):
        pass

    def finish(self, srcs, outs, sems):
        raise NotImplementedError


class _AllGather(_Exchange):
    def __init__(self, arrays):
        n = len(arrays)
        self.inputs = list(arrays)
        self.out_shapes = [jax.ShapeDtypeStruct((N_DEV,) + a.shape, a.dtype) for a in arrays]
        self.sems = [pltpu.SemaphoreType.DMA((n, 8)), pltpu.SemaphoreType.DMA((n, 8)), pltpu.SemaphoreType.DMA((n,))]

    def _parts(self, srcs, outs, sems):
        send_sems, recv_sems, local_sems = sems
        n = len(srcs)
        x, y, c = lax.axis_index("x"), lax.axis_index("y"), lax.axis_index("c")
        me, sibling, xn, yn, diag = (x, y, c), (x, y, 1 - c), (1 - x, y, c), (x, 1 - y, c), (1 - x, 1 - y, c)

        def place(a, dev, half=None):
            block = outs[a].at[4 * dev[0] + 2 * dev[1] + dev[2]]
            if half is None:
                return block
            r2 = outs[a].shape[1] // 2
            return block.at[pl.ds(half * r2, r2)]

        def copy(a, k, dev, to, half=None, src=None):
            where = place(a, dev, half)
            return pltpu.make_async_remote_copy(
                src_ref=where if src is None else src, dst_ref=where, send_sem=send_sems.at[a, k],
                recv_sem=recv_sems.at[a, k], device_id=to, device_id_type=MESH)

        def other(dev):
            return (dev[0], dev[1], 1 - dev[2])

        class Parts:
            mine = staticmethod(lambda: [pltpu.make_async_copy(srcs[a], place(a, me), local_sems.at[a])
                                         for a in range(n)])
            own = staticmethod(lambda: [copy(a, k, me, to, src=srcs[a]) for a in range(n)
                                        for k, to in ((0, sibling), (1, xn), (2, yn))])
            relay = staticmethod(lambda a: [copy(a, 3, xn, yn, half=0), copy(a, 4, yn, xn, half=1),
                                            copy(a, 5, xn, sibling), copy(a, 6, yn, sibling)])
            last = staticmethod(lambda a: copy(a, 7, diag, sibling))
            from_x = staticmethod(lambda a: copy(a, 1, xn, me))
            from_y = staticmethod(lambda a: copy(a, 2, yn, me))
            diag_halves = staticmethod(lambda a: [copy(a, 3, diag, me, half=0), copy(a, 4, diag, me, half=1)])
            from_sibling = staticmethod(lambda a: [copy(a, 0, sibling, me), copy(a, 5, other(xn), me),
                                                   copy(a, 6, other(yn), me), copy(a, 7, other(diag), me)])

        return n, Parts

    def start(self, srcs, outs, sems):
        _, p = self._parts(srcs, outs, sems)
        for cp in p.mine() + p.own():
            cp.start()

    def middle(self, srcs, outs, sems):
        n, p = self._parts(srcs, outs, sems)
        for a in range(n):
            p.from_x(a).wait_recv()
            p.from_y(a).wait_recv()
            for cp in p.relay(a):
                cp.start()

    def finish(self, srcs, outs, sems):
        n, p = self._parts(srcs, outs, sems)
        for a in range(n):
            for cp in p.diag_halves(a):
                cp.wait_recv()
            p.last(a).start()
        for a in range(n):
            for cp in p.from_sibling(a):
                cp.wait_recv()
        for cp in p.own() + [cp for a in range(n) for cp in p.relay(a) + [p.last(a)]]:
            cp.wait_send()
        for cp in p.mine():
            cp.wait()


class _RsStage2(_Exchange):
    def start(self, srcs, outs, sems):
        for cp in self._copies(srcs, outs, sems):
            cp.start()

    def finish(self, srcs, outs, sems):
        copies = self._copies(srcs, outs, sems)
        for cp in copies:
            cp.wait_recv()
        for cp in copies:
            cp.wait_send()


    def __init__(self, pbs):
        n = len(pbs)
        self.inputs = list(pbs)
        self.out_shapes = [jax.ShapeDtypeStruct((3,) + p.shape[1:], p.dtype) for p in pbs]
        self.sems = [pltpu.SemaphoreType.DMA((n, 3)), pltpu.SemaphoreType.DMA((n, 3))]

    def _copies(self, srcs, outs, sems):
        send_sems, recv_sems = sems
        x, y, c = lax.axis_index("x"), lax.axis_index("y"), lax.axis_index("c")
        chips = [(1 - x, y), (x, 1 - y), (1 - x, 1 - y)]
        return [pltpu.make_async_remote_copy(
            src_ref=srcs[a].at[2 * chip[0] + chip[1]], dst_ref=outs[a].at[j], send_sem=send_sems.at[a, j],
            recv_sem=recv_sems.at[a, j], device_id=(*chip, c), device_id_type=MESH)
            for a in range(len(srcs)) for j, chip in enumerate(chips)]


class _DirectGather(_Exchange):
    def __init__(self, arrays):
        n = len(arrays)
        self.inputs = list(arrays)
        self.out_shapes = [jax.ShapeDtypeStruct((N_DEV,) + a.shape, a.dtype) for a in arrays]
        self.sems = [pltpu.SemaphoreType.DMA((n, 7)), pltpu.SemaphoreType.DMA((n, 7)), pltpu.SemaphoreType.DMA((n,))]

    def _copies(self, srcs, outs, sems):
        send_sems, recv_sems, local_sems = sems
        x, y, c = lax.axis_index("x"), lax.axis_index("y"), lax.axis_index("c")
        me = 4 * x + 2 * y + c
        remote, local = [], []
        for a in range(len(srcs)):
            local.append(pltpu.make_async_copy(srcs[a], outs[a].at[me], local_sems.at[a]))
            for k in range(1, N_DEV):
                peer = (x ^ (k >> 2), y ^ ((k >> 1) & 1), c ^ (k & 1))
                remote.append(pltpu.make_async_remote_copy(
                    src_ref=srcs[a], dst_ref=outs[a].at[me], send_sem=send_sems.at[a, k - 1],
                    recv_sem=recv_sems.at[a, k - 1], device_id=peer, device_id_type=MESH))
        return remote, local

    def start(self, srcs, outs, sems):
        remote, local = self._copies(srcs, outs, sems)
        for cp in local + remote:
            cp.start()

    def finish(self, srcs, outs, sems):
        remote, local = self._copies(srcs, outs, sems)
        for cp in remote:
            cp.wait_recv()
        for cp in remote:
            cp.wait_send()
        for cp in local:
            cp.wait()


class _Both(_Exchange):
    def __init__(self, a, b):
        self.a, self.b = a, b
        self.inputs = list(a.inputs) + list(b.inputs)
        self.out_shapes = list(a.out_shapes) + list(b.out_shapes)
        self.sems = list(a.sems) + list(b.sems)

    def _split(self, srcs, outs, sems):
        na, oa, sa = len(self.a.inputs), len(self.a.out_shapes), len(self.a.sems)
        return (srcs[:na], outs[:oa], sems[:sa]), (srcs[na:], outs[oa:], sems[sa:])

    def start(self, srcs, outs, sems):
        pa, pb = self._split(srcs, outs, sems)
        self.a.start(*pa)
        self.b.start(*pb)

    def middle(self, srcs, outs, sems):
        pa, pb = self._split(srcs, outs, sems)
        self.a.middle(*pa)
        self.b.middle(*pb)

    def finish(self, srcs, outs, sems):
        pa, pb = self._split(srcs, outs, sems)
        self.a.finish(*pa)
        self.b.finish(*pb)


_ANY = pl.BlockSpec(memory_space=pl.ANY)


def _run_exchange(ex, name):
    n_in, n_out = len(ex.inputs), len(ex.out_shapes)

    def body(*refs):
        srcs, outs, sems = refs[:n_in], refs[n_in:n_in + n_out], refs[n_in + n_out:]
        ex.start(srcs, outs, sems)
        ex.middle(srcs, outs, sems)
        ex.finish(srcs, outs, sems)

    return pl.pallas_call(
        body, name=name, out_shape=list(ex.out_shapes), in_specs=[_ANY] * n_in, out_specs=[_ANY] * n_out,
        scratch_shapes=list(ex.sems),
    )(*ex.inputs)


CARRY_MIDDLE = 0.7


def _call(body, name, grid, in_specs, out_specs, out_shape, args, sem, carry=None, middle=CARRY_MIDDLE, scratch=()):
    if carry is None:
        return pl.pallas_call(functools.partial(body), name=name, grid=grid, in_specs=in_specs, out_specs=out_specs,
                              out_shape=out_shape, scratch_shapes=list(scratch), compiler_params=_cparams(sem))(*args)
    n_in, n_out = len(in_specs), len(out_specs)
    nc_in, nc_out = len(carry.inputs), len(carry.out_shapes)

    def carried(*refs):
        ins = refs[:n_in]
        c_in = refs[n_in:n_in + nc_in]
        outs = refs[n_in + nc_in:n_in + nc_in + n_out]
        c_out = refs[n_in + nc_in + n_out:n_in + nc_in + n_out + nc_out]
        own = refs[n_in + nc_in + n_out + nc_out:n_in + nc_in + n_out + nc_out + len(scratch)]
        sems = refs[n_in + nc_in + n_out + nc_out + len(scratch):]
        ids = [pl.program_id(i) for i in range(len(grid))]
        is_first = functools.reduce(jnp.logical_and, [i == 0 for i in ids])
        is_last = functools.reduce(jnp.logical_and, [i == g - 1 for i, g in zip(ids, grid)])
        @pl.when(is_first)
        def _():
            carry.start(c_in, c_out, sems)

        if middle is not None:
            @pl.when(functools.reduce(jnp.logical_and, [ids[0] == round(middle * (grid[0] - 1))]
                                      + [i == 0 for i in ids[1:]]))
            def _():
                carry.middle(c_in, c_out, sems)

        body(*ins, *outs, *own)

        @pl.when(is_last)
        def _():
            if middle is None:
                carry.middle(c_in, c_out, sems)
            carry.finish(c_in, c_out, sems)

    return pl.pallas_call(
        carried, name=name, grid=grid, in_specs=list(in_specs) + [_ANY] * nc_in,
        out_specs=list(out_specs) + [_ANY] * nc_out, out_shape=list(out_shape) + list(carry.out_shapes),
        scratch_shapes=list(scratch) + list(carry.sems), compiler_params=_cparams(sem))(*args, *carry.inputs)


def _rs_sum2_adam(pos, p, r2, w, m, v, name):
    _, rows, d = p.shape
    tr = rows // 2 if rows % 16 == 0 else rows

    def body(pos_ref, p_ref, r_ref, w_ref, m_ref, v_ref, g_ref, d_ref, nm_ref, nv_ref):
        r = r_ref[...].astype(F32)
        g = ((p_ref[...] + r[0]) + r[1]) + r[2]
        g_ref[...] = g
        d_ref[...], nm_ref[...], nv_ref[...] = _adam_math(w_ref[...], g, m_ref[...], v_ref[...])

    blk = pl.BlockSpec((tr, d), lambda i, pos_ref: (i, 0))
    return pl.pallas_call(
        body, name=name,
        grid_spec=pltpu.PrefetchScalarGridSpec(
            num_scalar_prefetch=1, grid=(rows // tr,),
            in_specs=[pl.BlockSpec((None, tr, d), lambda i, pos_ref: (2 * pos_ref[0] + pos_ref[1], i, 0)),
                      pl.BlockSpec((3, tr, d), lambda i, pos_ref: (0, i, 0)), blk, blk, blk],
            out_specs=[blk] * 4),
        out_shape=[jax.ShapeDtypeStruct((rows, d), F32)] * 4,
        compiler_params=_cparams(("arbitrary",)),
    )(pos, p, r2, w, m, v)


def _ffn_chunk(f):
    for cand in (256, 128):
        if f % cand == 0:
            return cand
    return f


def _loss_head(x, gg, target, loss_ref, dg_ref):
    @pl.when(pl.program_id(0) == 0)
    def _():
        loss_ref[...] = jnp.zeros_like(loss_ref)
        dg_ref[...] = jnp.zeros_like(dg_ref)

    r = lax.rsqrt(jnp.mean(x * x, axis=-1, keepdims=True) + EPS)
    xhat = x * r
    e = xhat * gg - target
    loss_ref[...] += 0.5 * jnp.sum(jnp.mean(e * e, axis=-1, keepdims=True), axis=0, keepdims=True)
    dy = e * (1.0 / x.shape[-1])
    dg_ref[...] += jnp.sum(dy * xhat, axis=0, keepdims=True)
    dxhat = dy * gg
    return r * (dxhat - xhat * jnp.mean(dxhat * xhat, axis=-1, keepdims=True))


def _ffn_gate(h, g, wg_t, name, tm, carry=None):
    s, d = h.shape
    f = wg_t.shape[0]

    def body(h_ref, g_ref, wg_ref, n_ref, gate_ref):
        x = h_ref[...]
        r = lax.rsqrt(jnp.mean(x * x, axis=-1, keepdims=True) + EPS)
        nb = (x * r * g_ref[...]).astype(BF16)
        n_ref[...] = nb
        gate_ref[...] = _dot_nt(nb, wg_ref[...]).astype(BF16)

    row = lambda w: pl.BlockSpec((tm, w), lambda i: (i, 0))
    return _call(body, name, (s // tm,), [row(d), _whole((1, d)), _whole((f, d))], [row(d), row(f)],
                 [jax.ShapeDtypeStruct((s, d), BF16), jax.ShapeDtypeStruct((s, f), BF16)], (h, g, wg_t),
                 ("arbitrary",), carry)


def _ffn_up(n, gate, wu_t, name, tm, carry=None):
    s, d = n.shape
    f = wu_t.shape[0]
    tf = _ffn_chunk(f)

    def body(n_ref, gate_ref, wu_ref, up_ref, act_ref, act_t_ref):
        nb = n_ref[...]
        for j in range(f // tf):
            sl = slice(j * tf, (j + 1) * tf)
            up = _dot_nt(nb, wu_ref[sl, :])
            gate = gate_ref[:, sl].astype(F32)
            up_ref[:, sl] = up.astype(BF16)
            act = (gate * _sigmoid(gate) * up).astype(BF16)
            act_ref[:, sl] = act
            act_t_ref[sl, :] = act.T

    row = lambda w: pl.BlockSpec((tm, w), lambda i: (i, 0))
    return _call(body, name, (s // tm,), [row(d), row(f), _whole((f, d))],
                 [row(f), row(f), pl.BlockSpec((f, tm), lambda i: (0, i))],
                 [jax.ShapeDtypeStruct((s, f), BF16)] * 2 + [jax.ShapeDtypeStruct((f, s), BF16)], (n, gate, wu_t),
                 ("arbitrary",), carry)


def _ffn_fwd(h, g, wg_t, wu_t, wd, name, tm, carry=None, head=None, middle=CARRY_MIDDLE):
    s, d = h.shape
    f = wg_t.shape[0]
    tf = _ffn_chunk(f)

    def body(h_ref, g_ref, wg_ref, wu_ref, wd_ref, *refs):
        if head is None:
            o_ref, n_ref, gate_ref, up_ref, act_t_ref, act_ref = refs
        else:
            gf_ref, t_ref, o_ref, n_ref, gate_ref, up_ref, act_t_ref, loss_ref, dgf_ref, act_ref = refs
        x = h_ref[...]
        r = lax.rsqrt(jnp.mean(x * x, axis=-1, keepdims=True) + EPS)
        nb = (x * r * g_ref[...]).astype(BF16)
        n_ref[...] = nb
        for j in range(f // tf):
            sl = slice(j * tf, (j + 1) * tf)
            gate = _dot_nt(nb, wg_ref[sl, :])
            up = _dot_nt(nb, wu_ref[sl, :])
            gate_ref[:, sl] = gate.astype(BF16)
            up_ref[:, sl] = up.astype(BF16)
            act = gate * _sigmoid(gate) * up
            act = act.astype(BF16)
            act_ref[:, sl] = act
            act_t_ref[sl, :] = act.T
        h_out = x + 0.5 * _dot_nn(act_ref[...], wd_ref[...])
        o_ref[...] = h_out if head is None else _loss_head(h_out, gf_ref[...], t_ref[...], loss_ref, dgf_ref)

    row = lambda w: pl.BlockSpec((tm, w), lambda i: (i, 0))
    in_specs = [row(d), _whole((1, d)), _whole((f, d)), _whole((f, d)), _whole((f, d))]
    out_specs = [row(d), row(d), row(f), row(f), pl.BlockSpec((f, tm), lambda i: (0, i))]
    out_shape = ([jax.ShapeDtypeStruct((s, d), F32), jax.ShapeDtypeStruct((s, d), BF16)]
                 + [jax.ShapeDtypeStruct((s, f), BF16)] * 2 + [jax.ShapeDtypeStruct((f, s), BF16)])
    args = (h, g, wg_t, wu_t, wd)
    scratch = (pltpu.VMEM((tm, f), BF16),)
    if head is not None:
        in_specs += [_whole((1, d)), row(d)]
        out_specs += [pl.BlockSpec((1, 1), lambda i: (0, 0)), pl.BlockSpec((1, d), lambda i: (0, 0))]
        out_shape += [jax.ShapeDtypeStruct((1, 1), F32), jax.ShapeDtypeStruct((1, d), F32)]
        args += tuple(head)
    return _call(body, name, (s // tm,), in_specs, out_specs, out_shape, args, ("arbitrary",), carry, middle, scratch)


def _gate_grads(dh_ref, gate_ref, up_ref, wd_ref, dgate_ref, dup_ref, dgate_t_ref, dup_t_ref, dhh_ref, tf):
    dhh = (0.5 * dh_ref[...]).astype(BF16)
    dhh_ref[...] = dhh
    for j in range(gate_ref.shape[1] // tf):
        sl = slice(j * tf, (j + 1) * tf)
        gt = gate_ref[:, sl].astype(F32)
        u = up_ref[:, sl].astype(F32)
        dact = _dot_nt(dhh, wd_ref[sl, :])
        sg = _sigmoid(gt)
        dup = dact * (gt * sg)
        dgate = dact * u * (sg * (1.0 + gt * (1.0 - sg)))
        dup, dgate = dup.astype(BF16), dgate.astype(BF16)
        dup_ref[:, sl] = dup
        dgate_ref[:, sl] = dgate
        dup_t_ref[sl, :] = dup.T
        dgate_t_ref[sl, :] = dgate.T


def _input_grad(h_ref, dh_ref, dgate_ref, dup_ref, g_ref, wg_ref, wu_ref, o_ref, dg_ref):
    x = h_ref[...]
    r = lax.rsqrt(jnp.mean(x * x, axis=-1, keepdims=True) + EPS)
    xhat = x * r
    dn = _dot_nn(dgate_ref[...], wg_ref[...]) + _dot_nn(dup_ref[...], wu_ref[...])
    dxhat = dn * g_ref[...]
    o_ref[...] = dh_ref[...] + r * (dxhat - xhat * jnp.mean(dxhat * xhat, axis=-1, keepdims=True))

    @pl.when(pl.program_id(0) == 0)
    def _():
        dg_ref[...] = jnp.zeros_like(dg_ref)

    dg_ref[...] += jnp.sum(dn * xhat, axis=0, keepdims=True)


def _ffn_bwd(h_in, dh_out, gate, up, g, wg_t, wu_t, wd, w_out, name, tm):
    s, d = h_in.shape
    f = gate.shape[1]
    w = w_out.shape[0]
    tf = _ffn_chunk(f)

    def body(h_ref, dh_ref, gate_ref, up_ref, g_ref, wg_ref, wu_ref, wd_ref, wo_ref,
             o_ref, dg_ref, dgate_t_ref, dup_t_ref, dhh_ref, dmix_ref, dhb_ref, dgate_ref, dup_ref):
        _gate_grads(dh_ref, gate_ref, up_ref, wd_ref, dgate_ref, dup_ref, dgate_t_ref, dup_t_ref, dhh_ref, tf)
        _input_grad(h_ref, dh_ref, dgate_ref, dup_ref, g_ref, wg_ref, wu_ref, o_ref, dg_ref)
        dhb = o_ref[...].astype(BF16)
        dhb_ref[...] = dhb
        dmix_ref[...] = _dot_nt(dhb, wo_ref[...])

    row = lambda c: pl.BlockSpec((tm, c), lambda i: (i, 0))
    col = pl.BlockSpec((f, tm), lambda i: (0, i))
    return pl.pallas_call(
        body, name=name, grid=(s // tm,),
        in_specs=[row(d), row(d), row(f), row(f), _whole((1, d)), _whole((f, d)), _whole((f, d)), _whole((f, d)),
                  _whole((w, d))],
        out_specs=[row(d), pl.BlockSpec((1, d), lambda i: (0, 0)), col, col, row(d), row(w), row(d)],
        out_shape=[jax.ShapeDtypeStruct((s, d), F32), jax.ShapeDtypeStruct((1, d), F32),
                   jax.ShapeDtypeStruct((f, s), BF16), jax.ShapeDtypeStruct((f, s), BF16),
                   jax.ShapeDtypeStruct((s, d), BF16), jax.ShapeDtypeStruct((s, w), F32),
                   jax.ShapeDtypeStruct((s, d), BF16)],
        scratch_shapes=[pltpu.VMEM((tm, f), BF16), pltpu.VMEM((tm, f), BF16)],
        compiler_params=_cparams(("arbitrary",)),
    )(h_in, dh_out, gate, up, g, wg_t, wu_t, wd, w_out)


def _ffn_bwd_gates(dh_out, gate, up, wd, name, tm, carry=None):
    s, d = dh_out.shape
    f = gate.shape[1]
    tf = _ffn_chunk(f)

    def body(dh_ref, gate_ref, up_ref, wd_ref, dgate_ref, dup_ref, dgate_t_ref, dup_t_ref, dhh_ref):
        _gate_grads(dh_ref, gate_ref, up_ref, wd_ref, dgate_ref, dup_ref, dgate_t_ref, dup_t_ref, dhh_ref, tf)

    row = lambda w: pl.BlockSpec((tm, w), lambda i: (i, 0))
    col = pl.BlockSpec((f, tm), lambda i: (0, i))
    return _call(
        body, name, (s // tm,), [row(d), row(f), row(f), _whole((f, d))], [row(f), row(f), col, col, row(d)],
        [jax.ShapeDtypeStruct((s, f), BF16)] * 2 + [jax.ShapeDtypeStruct((f, s), BF16)] * 2
        + [jax.ShapeDtypeStruct((s, d), BF16)],
        (dh_out, gate, up, wd), ("arbitrary",), carry)


def _ffn_bwd_input(h_in, dh_out, dgate, dup, g, wg_t, wu_t, name, tm, carry=None):
    s, d = h_in.shape
    f = dgate.shape[1]

    row = lambda w: pl.BlockSpec((tm, w), lambda i: (i, 0))
    return _call(
        _input_grad, name, (s // tm,),
        [row(d), row(d), row(f), row(f), _whole((1, d)), _whole((f, d)), _whole((f, d))],
        [row(d), pl.BlockSpec((1, d), lambda i: (0, 0))],
        [jax.ShapeDtypeStruct((s, d), F32), jax.ShapeDtypeStruct((1, d), F32)],
        (h_in, dh_out, dgate, dup, g, wg_t, wu_t), ("arbitrary",), carry)


def _wgrad_rs1(pos, a_t, b, name, carry=None):
    f, s = a_t.shape
    d = b.shape[1]
    fk = f // N_DEV
    nc_in = 0 if carry is None else len(carry.inputs)
    nc_out = 0 if carry is None else len(carry.out_shapes)

    def body(pos_ref, a_ref, b_ref, *refs):
        c_in = refs[:nc_in]
        p_ref, pb_ref = refs[nc_in:nc_in + 2]
        c_out = refs[nc_in + 2:nc_in + 2 + nc_out]
        stage, land, send_sems, recv_sems = refs[nc_in + 2 + nc_out:nc_in + 6 + nc_out]
        c_sems = refs[nc_in + 6 + nc_out:]
        t = pl.program_id(0)
        q = t % 4
        x, y, c = lax.axis_index("x"), lax.axis_index("y"), lax.axis_index("c")

        def push(k):
            return pltpu.make_async_remote_copy(src_ref=stage.at[k], dst_ref=land.at[k], send_sem=send_sems.at[k],
                                                recv_sem=recv_sems.at[k], device_id=(x, y, 1 - c), device_id_type=MESH)

        if carry is not None:
            @pl.when(t == 0)
            def _():
                carry.start(c_in, c_out, c_sems)

        g = _dot_nn(a_ref[...], b_ref[...])

        @pl.when(t < 4)
        def _():
            stage[q] = g.astype(BF16)
            push(q).start()

        @pl.when(t >= 4)
        def _():
            push(q).wait_recv()
            p = g + land[q].astype(F32)
            p_ref[...] = p
            pb_ref[...] = p.astype(BF16)

        @pl.when(t == 7)
        def _():
            for k in range(4):
                push(k).wait_send()
            if carry is not None:
                carry.middle(c_in, c_out, c_sems)
                carry.finish(c_in, c_out, c_sems)

    def shard(t, pos_ref):
        return 4 * ((t % 4) // 2) + 2 * (t % 2) + jnp.where(t < 4, 1 - pos_ref[2], pos_ref[2])

    out = pl.BlockSpec((None, fk, d), lambda t, pos_ref: (jnp.maximum(t - 4, 0), 0, 0))
    return pl.pallas_call(
        body, name=name,
        grid_spec=pltpu.PrefetchScalarGridSpec(
            num_scalar_prefetch=1, grid=(8,),
            in_specs=[pl.BlockSpec((fk, s), lambda t, pos_ref: (shard(t, pos_ref), 0)),
                      pl.BlockSpec((s, d), lambda t, pos_ref: (0, 0), pipeline_mode=pl.Buffered(1))]
            + [_ANY] * nc_in,
            out_specs=[out, out] + [_ANY] * nc_out,
            scratch_shapes=[pltpu.VMEM((4, fk, d), BF16), pltpu.VMEM((4, fk, d), BF16),
                            pltpu.SemaphoreType.DMA((4,)), pltpu.SemaphoreType.DMA((4,))]
            + ([] if carry is None else list(carry.sems))),
        out_shape=[jax.ShapeDtypeStruct((4, fk, d), F32), jax.ShapeDtypeStruct((4, fk, d), BF16)]
        + ([] if carry is None else list(carry.out_shapes)),
        compiler_params=_cparams(("arbitrary",)),
    )(pos, a_t, b, *([] if carry is None else carry.inputs))


def _rope(t, c, sa, sb, reps):
    c, sa, sb = (jnp.tile(v, (1, reps)) if reps > 1 else v for v in (c, sa, sb))
    w = t.shape[1]
    return t * c + pltpu.roll(t, w - 8, 1) * sa + pltpu.roll(t, 8, 1) * sb


def _rope_bwd(dt, c, sa, sb, reps):
    c, sa, sb = (jnp.tile(v, (1, reps)) if reps > 1 else v for v in (c, sa, sb))
    w = dt.shape[1]
    return dt * c + pltpu.roll(dt * sa, 8, 1) + pltpu.roll(dt * sb, w - 8, 1)


def _ffn_down_mix_in(x, act, wd, g, win_t, tabs, name, tm, carry=None):
    s, d = x.shape
    f = wd.shape[0]
    n_in = win_t.shape[0]

    def body(x_ref, act_ref, wd_ref, g_ref, w_ref, c_ref, sa_ref, sb_ref, h_ref, q_ref, k_ref, v_ref, pc_ref, n_ref):
        x = x_ref[...] + 0.5 * _dot_nn(act_ref[...], wd_ref[...])
        h_ref[...] = x
        r = lax.rsqrt(jnp.mean(x * x, axis=-1, keepdims=True) + EPS)
        nb = (x * r * g_ref[...]).astype(BF16)
        n_ref[...] = nb
        u = _dot_nt(nb, w_ref[...])
        c, sa, sb = c_ref[...], sa_ref[...], sb_ref[...]
        q_ref[...] = _rope(u[:, :ATTN_W], c, sa, sb, ATTN_W // 128).astype(BF16)
        k_ref[...] = _rope(u[:, ATTN_W:ATTN_W + KV_W], c, sa, sb, 1).astype(BF16)
        v_ref[...] = u[:, ATTN_W + KV_W:ATTN_W + 2 * KV_W].astype(BF16)
        pc_ref[...] = u[:, ATTN_W + 2 * KV_W:]

    row = lambda w: pl.BlockSpec((tm, w), lambda i: (i, 0))
    return _call(
        body, name, (s // tm,),
        [row(d), row(f), _whole((f, d)), _whole((1, d)), _whole((n_in, d)), row(128), row(128), row(128)],
        [row(d), row(ATTN_W), row(KV_W), row(KV_W), row(POOL_W), row(d)],
        [jax.ShapeDtypeStruct((s, d), F32), jax.ShapeDtypeStruct((s, ATTN_W), BF16),
         jax.ShapeDtypeStruct((s, KV_W), BF16), jax.ShapeDtypeStruct((s, KV_W), BF16),
         jax.ShapeDtypeStruct((s, POOL_W), F32), jax.ShapeDtypeStruct((s, d), BF16)],
        (x, act, wd, g, win_t, *tabs), ("arbitrary",), carry)


def _band_mask(n, nb, transposed):
    shape = (3 * BLK, 2 * BLK) if transposed else (2 * BLK, 3 * BLK)
    i = lax.broadcasted_iota(jnp.int32, shape, 1 if transposed else 0) % BLK
    j = lax.broadcasted_iota(jnp.int32, shape, 0 if transposed else 1)
    kpos = (n - 1) * BLK + j
    return (j >= i) & (j <= i + 2 * BLK) & (kpos >= 0) & (kpos < nb * BLK)


def _block_diag(t, kh):
    tf = t.astype(F32)
    tr = pltpu.roll(tf, HEAD_DIM, 1)
    lo = lax.broadcasted_iota(jnp.int32, tf.shape, 1) < HEAD_DIM
    top, bot = (tf, tr) if kh == 0 else (tr, tf)
    return jnp.concatenate([jnp.where(lo, top, 0.0), jnp.where(lo, 0.0, bot)], axis=0).astype(BF16)


def _fold_diag(tbd):
    lo = lax.broadcasted_iota(jnp.int32, (3 * BLK, 2 * HEAD_DIM), 1) < HEAD_DIM
    t = jnp.where(lo, tbd[:3 * BLK], tbd[3 * BLK:])
    return t + pltpu.roll(t, HEAD_DIM, 1)


def _stack_pairs(x, kh):
    return jnp.concatenate([x[:, (2 * kh) * 128:(2 * kh + 1) * 128], x[:, (2 * kh + 1) * 128:(2 * kh + 2) * 128]], axis=0)


def _sink_of(sink_ref, kh, half, axis):
    shape = (2 * BLK, 1) if axis == 0 else (1, 2 * BLK)
    first = lax.broadcasted_iota(jnp.int32, shape, axis) < BLK
    return jnp.where(first, sink_ref[0, GROUP * kh + half], sink_ref[0, GROUP * kh + 2 + half])


def _softmax_sink(sc, valid, sink, axis):
    sc = jnp.where(valid, sc, -1e30)
    m = jnp.maximum(jnp.max(sc, axis=axis, keepdims=True), sink)
    e = jnp.exp(sc - m)
    es = jnp.exp(sink - m)
    inv = 1.0 / (jnp.sum(e, axis=axis, keepdims=True) + es)
    return e * inv, es * inv


def _attn_blocks_per_step(nb):
    return next(nq for nq in (4, 2, 1) if nb % nq == 0)


def _band_specs(nq, nb, w, col=0):
    return [pl.BlockSpec((BLK, w), lambda m: (jnp.maximum(nq * m - 1, 0), col)),
            pl.BlockSpec((nq * BLK, w), lambda m: (m, col)),
            pl.BlockSpec((BLK, w), lambda m: (jnp.minimum(nq * m + nq, nb - 1), col))]


def _attn_pool_fwd(h, w_out, q, k, v, pc, sink, pool_w, pool_scale, pband, name, carry=None, middle=CARRY_MIDDLE):
    s, d = h.shape
    nb = s // BLK
    nq = _attn_blocks_per_step(nb)

    def body(sink_ref, q_ref, k0, k1, k2, v0, v1, v2, p0, p1, p2, pw_ref, ps_ref, pb_ref, h_ref, wo_ref,
             h_out_ref, o_t_ref, o_ref):
        kall = jnp.concatenate([k0[...], k1[...], k2[...]], axis=0)
        vall = jnp.concatenate([v0[...], v1[...], v2[...]], axis=0)
        pall = jnp.concatenate([p0[...], p1[...], p2[...]], axis=0).astype(BF16)
        qall = q_ref[...] * SCORE_SCALE
        for j in range(nq):
            n = pl.program_id(0) * nq + j
            rows, band = slice(j * BLK, (j + 1) * BLK), slice(j * BLK, (j + 3) * BLK)
            valid = _band_mask(n, nb, False)
            kb, vb, qs = kall[band], vall[band], qall[rows]
            for kh in range(N_KV):
                sc = _dot_nt(_stack_pairs(qs, kh), _block_diag(kb, kh))
                p = [_softmax_sink(sc[:, half * 3 * BLK:(half + 1) * 3 * BLK], valid,
                                   _sink_of(sink_ref, kh, half, 0), 1)[0] for half in range(2)]
                o2 = _dot_nn(jnp.concatenate(p, axis=1).astype(BF16), _block_diag(vb, kh)).astype(BF16)
                o_ref[rows, (2 * kh) * 128:(2 * kh + 1) * 128] = o2[:BLK]
                o_ref[rows, (2 * kh + 1) * 128:(2 * kh + 2) * 128] = o2[BLK:]
            ext = pall[band]
            var = _variant_index(n, nb)
            for gi in range(POOL_G):
                gsl = slice(gi * POOL_GW, (gi + 1) * POOL_GW)
                dg = _dot_nn(pb_ref[var, gi], ext[:, gsl])
                yg = _dot_nn(dg.astype(BF16), pw_ref[gi].astype(BF16))
                o_ref[rows, ATTN_W + gi * POOL_GW:ATTN_W + (gi + 1) * POOL_GW] = (yg * ps_ref[:, gsl]).astype(BF16)
        mix = o_ref[...]
        o_t_ref[...] = mix.T
        h_out_ref[...] = h_ref[...] + _dot_nn(mix, wo_ref[...])

    mix_w = ATTN_W + POOL_W
    return _call(
        body, name, (nb // nq,),
        [pl.BlockSpec(memory_space=pltpu.SMEM), pl.BlockSpec((nq * BLK, ATTN_W), lambda m: (m, 0)),
         *_band_specs(nq, nb, KV_W), *_band_specs(nq, nb, KV_W), *_band_specs(nq, nb, POOL_W),
         _whole((POOL_G, POOL_GW, POOL_GW)), _whole((1, POOL_W)), _whole(pband.shape),
         pl.BlockSpec((nq * BLK, d), lambda m: (m, 0)), _whole((mix_w, d))],
        [pl.BlockSpec((nq * BLK, d), lambda m: (m, 0)), pl.BlockSpec((mix_w, nq * BLK), lambda m: (0, m))],
        [jax.ShapeDtypeStruct((s, d), F32), jax.ShapeDtypeStruct((mix_w, s), BF16)],
        (sink, q, k, k, k, v, v, v, pc, pc, pc, pool_w, pool_scale, pband, h, w_out), ("arbitrary",), carry, middle,
        scratch=(pltpu.VMEM((nq * BLK, mix_w), BF16),))


def _attn_pool_bwd(q, k, v, pc, dmix, sink, pool_w, pool_scale, pband, ptband, name, carry=None):
    s = q.shape[0]
    nb = s // BLK
    nq = _attn_blocks_per_step(nb)

    def body(sink_ref, q_ref, k0, k1, k2, v0, v1, v2, p0, p1, p2, da_ref, d0, d1, d2, pw_ref, ps_ref, pb_ref, ptb_ref,
             dq_ref, dk_ref, dv_ref, dpc_ref, dsink_ref, dpw_ref, dps_ref):
        @pl.when(pl.program_id(0) == 0)
        def _():
            dsink_ref[...] = jnp.zeros_like(dsink_ref)
            dpw_ref[...] = jnp.zeros_like(dpw_ref)
            dps_ref[...] = jnp.zeros_like(dps_ref)

        kall = jnp.concatenate([k0[...], k1[...], k2[...]], axis=0)
        vall = jnp.concatenate([v0[...], v1[...], v2[...]], axis=0)
        pall = jnp.concatenate([p0[...], p1[...], p2[...]], axis=0).astype(BF16)
        dpall = jnp.concatenate([d0[...], d1[...], d2[...]], axis=0)
        lo = lax.broadcasted_iota(jnp.int32, (3 * BLK, KV_W), 1) < HEAD_DIM
        for j in range(nq):
            n = pl.program_id(0) * nq + j
            rows, band = slice(j * BLK, (j + 1) * BLK), slice(j * BLK, (j + 3) * BLK)
            valid = _band_mask(n, nb, True)
            kb, vb, qb = kall[band], vall[band], q_ref[rows, :]
            qs = qb * SCORE_SCALE
            da = da_ref[rows, :].astype(BF16)
            dk_fold, dv_fold = [], []
            for kh in range(N_KV):
                kbd, vbd = _block_diag(kb, kh), _block_diag(vb, kh)
                q2, do2 = _stack_pairs(qb, kh), _stack_pairs(da, kh)
                sc_t = _dot_nt(kbd, _stack_pairs(qs, kh))
                dp_t = _dot_nt(vbd, do2)
                p_t, ds_t = [], []
                for half in range(2):
                    keys = slice(half * 3 * BLK, (half + 1) * 3 * BLK)
                    p, ps = _softmax_sink(sc_t[keys], valid, _sink_of(sink_ref, kh, half, 1), 0)
                    delta = jnp.sum(p * dp_t[keys], axis=0, keepdims=True)
                    p_t.append(p.astype(BF16))
                    ds_t.append((p * (dp_t[keys] - delta)).astype(BF16))
                    dsk = -ps * delta
                    for pair in range(2):
                        h = GROUP * kh + 2 * pair + half
                        part = jnp.sum(dsk[:, pair * BLK:(pair + 1) * BLK], axis=1, keepdims=True)
                        dsink_ref[h:h + 1, :] += jnp.broadcast_to(part, (1, 128))
                p_t = jnp.concatenate(p_t, axis=0)
                ds_t = jnp.concatenate(ds_t, axis=0)
                dq2 = _dot_tn(ds_t, kbd) * SCORE_SCALE
                dq_ref[rows, (2 * kh) * 128:(2 * kh + 1) * 128] = dq2[:BLK]
                dq_ref[rows, (2 * kh + 1) * 128:(2 * kh + 2) * 128] = dq2[BLK:]
                dk_fold.append(_fold_diag(_dot_nn(ds_t, q2)) * SCORE_SCALE)
                dv_fold.append(_fold_diag(_dot_nn(p_t, do2)))
            dk_all = jnp.where(lo, dk_fold[0], dk_fold[1])
            dv_all = jnp.where(lo, dv_fold[0], dv_fold[1])
            for t in range(3):
                dk_ref[j, t] = dk_all[t * BLK:(t + 1) * BLK]
                dv_ref[j, t] = dv_all[t * BLK:(t + 1) * BLK]
            ext, dpe = pall[band], dpall[band]
            dpc_cur = dpall[(j + 1) * BLK:(j + 2) * BLK]
            var = _variant_index(n, nb)
            for gi in range(POOL_G):
                gsl = slice(gi * POOL_GW, (gi + 1) * POOL_GW)
                wg = pw_ref[gi].astype(BF16)
                sc = ps_ref[:, gsl]
                dgb = _dot_nn(pb_ref[var, gi], ext[:, gsl]).astype(BF16)
                yg = _dot_nn(dgb, wg)
                dps_ref[:, gsl] += jnp.sum(dpc_cur[:, gsl] * yg, axis=0, keepdims=True)
                dpw_ref[gi] += _dot_tn(dgb, (dpc_cur[:, gsl] * sc).astype(BF16))
                dd = _dot_nt((dpe[:, gsl] * sc).astype(BF16), wg)
                dpc_ref[rows, gsl] = _dot_nn(ptb_ref[var, gi], dd.astype(BF16))

    fixed = lambda shape: pl.BlockSpec(shape, lambda m: (0,) * len(shape))
    return _call(
        body, name, (nb // nq,),
        [pl.BlockSpec(memory_space=pltpu.SMEM), pl.BlockSpec((nq * BLK, ATTN_W), lambda m: (m, 0)),
         *_band_specs(nq, nb, KV_W), *_band_specs(nq, nb, KV_W), *_band_specs(nq, nb, POOL_W),
         pl.BlockSpec((nq * BLK, ATTN_W), lambda m: (m, 0)), *_band_specs(nq, nb, POOL_W, 1),
         _whole((POOL_G, POOL_GW, POOL_GW)), _whole((1, POOL_W)), _whole(pband.shape), _whole(ptband.shape)],
        [pl.BlockSpec((nq * BLK, ATTN_W), lambda m: (m, 0)),
         pl.BlockSpec((nq, 3, BLK, KV_W), lambda m: (m, 0, 0, 0)),
         pl.BlockSpec((nq, 3, BLK, KV_W), lambda m: (m, 0, 0, 0)),
         pl.BlockSpec((nq * BLK, POOL_W), lambda m: (m, 0)),
         fixed((N_HEADS, 128)), fixed((POOL_G, POOL_GW, POOL_GW)), fixed((1, POOL_W))],
        [jax.ShapeDtypeStruct((s, ATTN_W), F32), jax.ShapeDtypeStruct((nb, 3, BLK, KV_W), F32),
         jax.ShapeDtypeStruct((nb, 3, BLK, KV_W), F32), jax.ShapeDtypeStruct((s, POOL_W), F32),
         jax.ShapeDtypeStruct((N_HEADS, 128), F32),
         jax.ShapeDtypeStruct((POOL_G, POOL_GW, POOL_GW), F32), jax.ShapeDtypeStruct((1, POOL_W), F32)],
        (sink, q, k, k, k, v, v, v, pc, pc, pc, dmix, dmix, dmix, dmix, pool_w, pool_scale, pband, ptband),
        ("arbitrary",), carry)


def _mix_in_bwd(h, dh, g, win_t, dq, dkp, dvp, dpc, tabs, name, tm, carry=None):
    s, d = h.shape
    nb = s // BLK
    nt = tm // BLK
    n_in = win_t.shape[0]

    def band_sum(n, before, own, after, prev_last, next_first):
        lo = (n > 0).astype(F32)
        hi = (n < s // tm - 1).astype(F32)
        blocks = []
        for b in range(nt):
            from_prev = prev_last[...] * lo if b == 0 else before[b - 1]
            from_next = next_first[...] * hi if b == nt - 1 else after[b + 1]
            blocks.append(from_prev + own[b] + from_next)
        return jnp.concatenate(blocks, axis=0)

    def body(h_ref, dh_ref, g_ref, w_ref, dq_ref, k2, k1, k0, kp, kn, v2, v1, v0, vp, vn, dpc_ref, c_ref, sa_ref,
             sb_ref, o_ref, du_ref, dg_ref):
        n = pl.program_id(0)
        dk = band_sum(n, k2, k1, k0, kp, kn)
        dv = band_sum(n, v2, v1, v0, vp, vn)
        c, sa, sb = c_ref[...], sa_ref[...], sb_ref[...]
        du = jnp.concatenate([_rope_bwd(dq_ref[...], c, sa, sb, ATTN_W // 128), _rope_bwd(dk, c, sa, sb, 1), dv,
                              dpc_ref[...]], axis=1)
        du_ref[...] = du.T.astype(BF16)
        dn = _dot_nn(du.astype(BF16), w_ref[...])
        x = h_ref[...]
        r = lax.rsqrt(jnp.mean(x * x, axis=-1, keepdims=True) + EPS)
        xhat = x * r
        dxhat = dn * g_ref[...]
        o_ref[...] = dh_ref[...] + r * (dxhat - xhat * jnp.mean(dxhat * xhat, axis=-1, keepdims=True))

        @pl.when(n == 0)
        def _():
            dg_ref[...] = jnp.zeros_like(dg_ref)

        dg_ref[...] += jnp.sum(dn * xhat, axis=0, keepdims=True)

    row = lambda w: pl.BlockSpec((tm, w), lambda n: (n, 0))
    slot = lambda t: pl.BlockSpec((nt, None, BLK, KV_W), lambda n, t=t: (n, t, 0, 0))
    parts = [slot(2), slot(1), slot(0),
             pl.BlockSpec((None, None, BLK, KV_W), lambda n: (jnp.maximum(nt * n - 1, 0), 2, 0, 0)),
             pl.BlockSpec((None, None, BLK, KV_W), lambda n: (jnp.minimum(nt * n + nt, nb - 1), 0, 0, 0))]
    return _call(
        body, name, (s // tm,),
        [row(d), row(d), _whole((1, d)), _whole((n_in, d)), row(ATTN_W), *parts, *parts, row(POOL_W),
         row(128), row(128), row(128)],
        [row(d), pl.BlockSpec((n_in, tm), lambda n: (0, n)), pl.BlockSpec((1, d), lambda n: (0, 0))],
        [jax.ShapeDtypeStruct((s, d), F32), jax.ShapeDtypeStruct((n_in, s), BF16), jax.ShapeDtypeStruct((1, d), F32)],
        (h, dh, g, win_t, dq, *[dkp] * 5, *[dvp] * 5, dpc, *tabs), ("arbitrary",), carry)


def _adam_math(w, g, m, v):
    m = ADAM_B1 * m + (1.0 - ADAM_B1) * g
    v = ADAM_B2 * v + (1.0 - ADAM_B2) * (g * g)
    m_hat = m / (1.0 - ADAM_B1 ** ADAM_STEP)
    v_hat = v / (1.0 - ADAM_B2 ** ADAM_STEP)
    delta = -ADAM_LR * (m_hat / (jnp.sqrt(v_hat) + ADAM_EPS) + ADAM_WD * w)
    return delta, m, v


def _adam_small(w, parts, late, m, v, name):
    rows, cols = w.shape

    def body(w_ref, p_ref, l_ref, m_ref, v_ref, g_ref, d_ref, nm_ref, nv_ref):
        g, gl = p_ref[0], l_ref[0]
        for k in range(1, N_DEV):
            g = g + p_ref[k]
            gl = gl + l_ref[k]
        g_ref[...] = g
        g_ref[SMALL_NORM1:SMALL_NORM1 + 8, :] = g[SMALL_NORM1:SMALL_NORM1 + 8] + gl
        d_ref[...], nm_ref[...], nv_ref[...] = _adam_math(w_ref[...], g_ref[...], m_ref[...], v_ref[...])

    return pl.pallas_call(
        body, name=name, out_shape=[jax.ShapeDtypeStruct((rows, cols), F32)] * 4,
    )(w, parts, late, m, v)


SMALL_NORM1 = 512


def _pack_small(norm1, normm, norm2, normf, sink, pool_w, pool_scale, loss=None):
    scale_rows = jnp.pad(pool_scale.reshape(4, 128), ((0, 4), (0, 0)))
    last_rows = jnp.pad(sink.reshape(1, N_HEADS), ((0, 7), (0, 128 - N_HEADS)))
    if loss is not None:
        last_rows = last_rows + jnp.pad(loss.reshape(1, 1), ((1, 6), (0, 127)))
    return jnp.concatenate([pool_w.reshape(512, 128), norm1.reshape(8, 128), normm.reshape(8, 128),
                            norm2.reshape(8, 128), normf.reshape(8, 128), scale_rows, last_rows], axis=0)


def _unpack_small(p):
    return dict(pool_w=p[:512].reshape(1, POOL_G, POOL_GW, POOL_GW), ffn1_norm=p[512:520].reshape(1, 1024),
                mix_norm=p[520:528].reshape(1, 1024), ffn2_norm=p[528:536].reshape(1, 1024),
                final_norm=p[536:544].reshape(1024), pool_scale=p[544:548].reshape(1, POOL_W),
                sink_logits=p[552, :N_HEADS].reshape(1, N_HEADS), loss=p[553, 0])


def kernel(x, ffn1_norm, ffn1_w_gate, ffn1_w_up, ffn1_w_down, mix_norm, w_in, sink_logits, pool_w, pool_scale, w_out, ffn2_norm, ffn2_w_gate, ffn2_w_up, ffn2_w_down, final_norm, loss_target, m_ffn1_norm, m_ffn1_w_gate, m_ffn1_w_up, m_ffn1_w_down, m_mix_norm, m_w_in, m_sink_logits, m_pool_w, m_pool_scale, m_w_out, m_ffn2_norm, m_ffn2_w_gate, m_ffn2_w_up, m_ffn2_w_down, m_final_norm, v_ffn1_norm, v_ffn1_w_gate, v_ffn1_w_up, v_ffn1_w_down, v_mix_norm, v_w_in, v_sink_logits, v_pool_w, v_pool_scale, v_w_out, v_ffn2_norm, v_ffn2_w_gate, v_ffn2_w_up, v_ffn2_w_down, v_final_norm):
    s, d = x.shape[1], x.shape[2]
    tm = min(512, s)
    tm_bwd = min(256, s)
    pos = jnp.stack([lax.axis_index("x"), lax.axis_index("y"), lax.axis_index("c")]).astype(jnp.int32)

    t_bf = lambda w: w[0].T.astype(BF16)
    full = lambda a: a.reshape(N_DEV * a.shape[1], d)
    (wg1,) = map(full, _run_exchange(_AllGather([t_bf(ffn1_w_gate)]), "gather_ffn1_gate"))

    tabs = _rope_tables(s)
    pband, ptband = _pool_tables(s)
    g1, gm, g2, gf = ffn1_norm, mix_norm, ffn2_norm, final_norm.reshape(1, d)

    x0 = x[0]
    n1, gate1, wu1 = _ffn_gate(x0, g1, wg1, "ffn1_gate", tm, carry=_AllGather([t_bf(ffn1_w_up)]))
    up1, act1, act1_t, wd1, win_t = _ffn_up(n1, gate1, full(wu1), "ffn1_up", tm,
                                            carry=_AllGather([ffn1_w_down[0].astype(BF16), t_bf(w_in)]))
    wu1, wd1, win_t = full(wu1), full(wd1), full(win_t)
    h1, q, k, v, pc, n2, wout, wg2 = _ffn_down_mix_in(
        x0, act1, wd1, gm, win_t, tabs, "ffn1_down_mix_in", tm,
        carry=_AllGather([w_out[0].astype(BF16), t_bf(ffn2_w_gate)]))
    wout, wg2 = full(wout), full(wg2)
    h2, mix_t, *gathered = _attn_pool_fwd(h1, wout, q, k, v, pc, sink_logits, pool_w[0], pool_scale, pband,
                                          "attn_pool_fwd",
                                          carry=_AllGather([t_bf(ffn2_w_up), ffn2_w_down[0].astype(BF16)]))
    wu2, wd2 = map(full, gathered)
    dh3, n3, gate2, up2, act2_t, loss_part, dgf = _ffn_fwd(h2, g2, wg2, wu2, wd2, "ffn2_fwd", tm,
                                                           head=(gf, loss_target[0]))

    sum1, recv2 = {}, {}

    def stage2(keys):
        return _RsStage2([sum1[key][1] for key in keys])

    dh2, dg2, dgate2_t, dup2_t, dhh3, dmix, dh2b = _ffn_bwd(h2, dh3, gate2, up2, g2, wg2, wu2, wd2, wout, "ffn2_bwd",
                                                            tm_bwd)
    sum1["g2"] = _wgrad_rs1(pos, dgate2_t, n3, "wgrad_gate2")
    sum1["u2"] = _wgrad_rs1(pos, dup2_t, n3, "wgrad_up2")
    sum1["d2"] = _wgrad_rs1(pos, act2_t, dhh3, "wgrad_down2")
    sum1["out"] = _wgrad_rs1(pos, mix_t, dh2b, "wgrad_out")
    dq, dkp, dvp, dpc, dsink, dpw, dps, *r2 = _attn_pool_bwd(
        q, k, v, pc, dmix, sink_logits, pool_w[0], pool_scale, pband, ptband, "attn_pool_bwd",
        carry=stage2(["g2", "u2", "d2"]))
    recv2.update(zip(["g2", "u2", "d2"], r2))
    dh1, du_t, dgm, recv2["out"] = _mix_in_bwd(h1, dh2, gm, win_t, dq, dkp, dvp, dpc, tabs, "mix_in_bwd", tm,
                                               carry=stage2(["out"]))
    sum1["in"] = _wgrad_rs1(pos, du_t, n2, "wgrad_in")
    small_part = _pack_small(jnp.zeros_like(dgm), dgm, dg2, dgf, dsink[:, 0], dpw, dps, loss_part)
    dgate1, dup1, dgate1_t, dup1_t, dhh1, recv2["in"], small_all = _ffn_bwd_gates(
        dh1, gate1, up1, wd1, "ffn1_bwd_gates", tm, carry=_Both(stage2(["in"]), _AllGather([small_part])))
    sum1["g1"] = _wgrad_rs1(pos, dgate1_t, n1, "wgrad_gate1")
    *sum1["u1"], recv2["g1"] = _wgrad_rs1(pos, dup1_t, n1, "wgrad_up1", carry=stage2(["g1"]))
    *sum1["d1"], recv2["u1"] = _wgrad_rs1(pos, act1_t, dhh1, "wgrad_down1", carry=stage2(["u1"]))
    dx, dg1, recv2["d1"] = _ffn_bwd_input(x0, dh1, dgate1, dup1, g1, wg1, wu1, "ffn1_bwd_input", tm,
                                          carry=stage2(["d1"]))

    (dg1_all,) = _run_exchange(_DirectGather([dg1.reshape(8, 128)]), "gather_norm1_grad")
    pk = lambda a, b, c_, e, s_, pw_, psc: _pack_small(a, b, c_, e, s_[0], pw_[0], psc)
    small_w = pk(ffn1_norm, mix_norm, ffn2_norm, final_norm, sink_logits, pool_w, pool_scale)
    small_m = pk(m_ffn1_norm, m_mix_norm, m_ffn2_norm, m_final_norm, m_sink_logits, m_pool_w, m_pool_scale)
    small_v = pk(v_ffn1_norm, v_mix_norm, v_ffn2_norm, v_final_norm, v_sink_logits, v_pool_w, v_pool_scale)
    sg, sd, sm, sv = [_unpack_small(a)
                      for a in _adam_small(small_w, small_all, dg1_all, small_m, small_v, "adam_small")]

    big = {}
    keys = ["g1", "u1", "d1", "g2", "u2", "d2", "in", "out"]
    names = ["ffn1_w_gate", "ffn1_w_up", "ffn1_w_down", "ffn2_w_gate", "ffn2_w_up", "ffn2_w_down", "w_in", "w_out"]
    transposed = [True, True, False, True, True, False, True, False]
    ws = [ffn1_w_gate, ffn1_w_up, ffn1_w_down, ffn2_w_gate, ffn2_w_up, ffn2_w_down, w_in, w_out]
    ms = [m_ffn1_w_gate, m_ffn1_w_up, m_ffn1_w_down, m_ffn2_w_gate, m_ffn2_w_up, m_ffn2_w_down, m_w_in, m_w_out]
    vs = [v_ffn1_w_gate, v_ffn1_w_up, v_ffn1_w_down, v_ffn2_w_gate, v_ffn2_w_up, v_ffn2_w_down, v_w_in, v_w_out]
    for key, nm, tr, w, m, vv in zip(keys, names, transposed, ws, ms, vs):
        view = (lambda a: jnp.swapaxes(a, 1, 2)[0]) if tr else (lambda a: a[0])
        back = (lambda a: jnp.swapaxes(a[None], 1, 2)) if tr else (lambda a: a[None])
        res = _rs_sum2_adam(pos, sum1[key][0], recv2[key], view(w), view(m), view(vv), "adam_" + nm)
        big[nm] = tuple(back(a) for a in res)

    loss = sg["loss"]
    all_names = ["ffn1_norm", "ffn1_w_gate", "ffn1_w_up", "ffn1_w_down", "mix_norm", "w_in", "sink_logits", "pool_w",
                 "pool_scale", "w_out", "ffn2_norm", "ffn2_w_gate", "ffn2_w_up", "ffn2_w_down", "final_norm"]
    outs = [loss, dx[None]]
    for idx, src in enumerate((sg, sd, sm, sv)):
        for nm in all_names:
            outs.append(big[nm][idx] if nm in big else src[nm])
    return tuple(outs)
```

```python
import functools

import jax
import jax.numpy as jnp
import numpy as np
from jax import lax
from jax.experimental import pallas as pl
from jax.experimental.pallas import tpu as pltpu

F32 = jnp.float32
BF16 = jnp.bfloat16
MESH = pl.DeviceIdType.MESH
N_DEV = 8

EPS = 1e-6
HEAD_DIM = 64
N_HEADS = 8
N_KV = 2
GROUP = N_HEADS // N_KV
ATTN_W = N_HEADS * HEAD_DIM
KV_W = N_KV * HEAD_DIM
POOL_W = 512
POOL_G = 4
POOL_GW = POOL_W // POOL_G
POOL_WINDOWS = (2, 4, 8, 16)
BLK = 128
ROT = 16
ROPE_THETA = 500000.0
SCORE_SCALE = HEAD_DIM ** -0.5

ADAM_LR, ADAM_B1, ADAM_B2, ADAM_EPS, ADAM_WD, ADAM_STEP = 0.001, 0.9, 0.999, 1e-08, 0.01, 10

VMEM_LIMIT = 56 * 1024 * 1024


def _cparams(sem=None, **kw):
    if sem is not None:
        kw["dimension_semantics"] = sem
    return pltpu.CompilerParams(vmem_limit_bytes=VMEM_LIMIT, **kw)


def _whole(shape):
    nd = len(shape)
    return pl.BlockSpec(shape, lambda *_: (0,) * nd, pipeline_mode=pl.Buffered(1))


def _sigmoid(z):
    return 1.0 / (1.0 + jnp.exp(-z))


def _dot_nt(a, b):
    return lax.dot_general(a, b, (((1,), (1,)), ((), ())), preferred_element_type=F32)


def _dot_nn(a, b):
    return lax.dot_general(a, b, (((1,), (0,)), ((), ())), preferred_element_type=F32)


def _dot_tn(a, b):
    return lax.dot_general(a, b, (((0,), (0,)), ((), ())), preferred_element_type=F32)


def _rope_tables(s):
    inv_freq = ROPE_THETA ** (-np.arange(0, ROT, 2, dtype=np.float64) / ROT)
    ang = np.arange(s, dtype=np.float64)[:, None] * inv_freq[None, :]
    c = np.ones((s, HEAD_DIM)); sa = np.zeros((s, HEAD_DIM)); sb = np.zeros((s, HEAD_DIM))
    c[:, :8] = np.cos(ang); c[:, 8:16] = np.cos(ang)
    sa[:, :8] = -np.sin(ang)
    sb[:, 8:16] = np.sin(ang)
    t = lambda a: jnp.asarray(np.tile(a, (1, 2)).astype(np.float32))
    return t(c), t(sa), t(sb)


def _pool_weight(gi, t, s_pos, s):
    half = POOL_WINDOWS[gi] // 2

    def win(lo, hi):
        a = np.clip(lo, 0, s); b = np.clip(hi + 1, 0, s)
        inside = (s_pos >= a) & (s_pos < b)
        return inside / np.maximum(b - a, 1)

    w = 0.5 * (win(t - half, t + half - 1) + win(t - half + 1, t + half)) - (t == s_pos)
    return w * ((t >= 0) & (t < s) & (s_pos >= 0) & (s_pos < s))


def _pool_tables(s):
    nb = s // BLK
    fwd = np.zeros((3, POOL_G, BLK, 3 * BLK), np.float32)
    bwd = np.zeros((3, POOL_G, BLK, 3 * BLK), np.float32)
    for vi, n in enumerate((0, 1 if nb > 2 else 0, nb - 1)):
        i = n * BLK + np.arange(BLK)[:, None]
        j = (n - 1) * BLK + np.arange(3 * BLK)[None, :]
        for gi in range(POOL_G):
            fwd[vi, gi] = _pool_weight(gi, i, j, s)
            bwd[vi, gi] = _pool_weight(gi, j, i, s)
    return jnp.asarray(fwd, dtype=BF16), jnp.asarray(bwd, dtype=BF16)


def _variant_index(n, nb):
    return jnp.where(n == 0, 0, jnp.where(n == nb - 1, 2, 1))


class _Exchange:
    inputs = ()
    out_shapes = ()
    sems = ()

    def start(self, srcs, outs, sems):
        raise NotImplementedError

    def middle(self, srcs, outs, sems):
        pass

    def finish(self, srcs, outs, sems):
        raise NotImplementedError


class _AllGather(_Exchange):
    def __init__(self, arrays):
        n = len(arrays)
        self.inputs = list(arrays)
        self.out_shapes = [jax.ShapeDtypeStruct((N_DEV,) + a.shape, a.dtype) for a in arrays]
        self.sems = [pltpu.SemaphoreType.DMA((n, 8)), pltpu.SemaphoreType.DMA((n, 8)), pltpu.SemaphoreType.DMA((n,))]

    def _parts(self, srcs, outs, sems):
        send_sems, recv_sems, local_sems = sems
        n = len(srcs)
        x, y, c = lax.axis_index("x"), lax.axis_index("y"), lax.axis_index("c")
        me, sibling, xn, yn, diag = (x, y, c), (x, y, 1 - c), (1 - x, y, c), (x, 1 - y, c), (1 - x, 1 - y, c)

        def place(a, dev, half=None):
            block = outs[a].at[4 * dev[0] + 2 * dev[1] + dev[2]]
            if half is None:
                return block
            r2 = outs[a].shape[1] // 2
            return block.at[pl.ds(half * r2, r2)]

        def copy(a, k, dev, to, half=None, src=None):
            where = place(a, dev, half)
            return pltpu.make_async_remote_copy(
                src_ref=where if src is None else src, dst_ref=where, send_sem=send_sems.at[a, k],
                recv_sem=recv_sems.at[a, k], device_id=to, device_id_type=MESH)

        def other(dev):
            return (dev[0], dev[1], 1 - dev[2])

        class Parts:
            mine = staticmethod(lambda: [pltpu.make_async_copy(srcs[a], place(a, me), local_sems.at[a])
                                         for a in range(n)])
            own = staticmethod(lambda: [copy(a, k, me, to, src=srcs[a]) for a in range(n)
                                        for k, to in ((0, sibling), (1, xn), (2, yn))])
            relay = staticmethod(lambda a: [copy(a, 3, xn, yn, half=0), copy(a, 4, yn, xn, half=1),
                                            copy(a, 5, xn, sibling), copy(a, 6, yn, sibling)])
            last = staticmethod(lambda a: copy(a, 7, diag, sibling))
            from_x = staticmethod(lambda a: copy(a, 1, xn, me))
            from_y = staticmethod(lambda a: copy(a, 2, yn, me))
            diag_halves = staticmethod(lambda a: [copy(a, 3, diag, me, half=0), copy(a, 4, diag, me, half=1)])
            from_sibling = staticmethod(lambda a: [copy(a, 0, sibling, me), copy(a, 5, other(xn), me),
                                                   copy(a, 6, other(yn), me), copy(a, 7, other(diag), me)])

        return n, Parts

    def start(self, srcs, outs, sems):
        _, p = self._parts(srcs, outs, sems)
        for cp in p.mine() + p.own():
            cp.start()

    def middle(self, srcs, outs, sems):
        n, p = self._parts(srcs, outs, sems)
        for a in range(n):
            p.from_x(a).wait_recv()
            p.from_y(a).wait_recv()
            for cp in p.relay(a):
                cp.start()

    def finish(self, srcs, outs, sems):
        n, p = self._parts(srcs, outs, sems)
        for a in range(n):
            for cp in p.diag_halves(a):
                cp.wait_recv()
            p.last(a).start()
        for a in range(n):
            for cp in p.from_sibling(a):
                cp.wait_recv()
        for cp in p.own() + [cp for a in range(n) for cp in p.relay(a) + [p.last(a)]]:
            cp.wait_send()
        for cp in p.mine():
            cp.wait()


class _RsStage2(_Exchange):
    def start(self, srcs, outs, sems):
        for cp in self._copies(srcs, outs, sems):
            cp.start()

    def finish(self, srcs, outs, sems):
        copies = self._copies(srcs, outs, sems)
        for cp in copies:
            cp.wait_recv()
        for cp in copies:
            cp.wait_send()


    def __init__(self, pbs):
        n = len(pbs)
        self.inputs = list(pbs)
        self.out_shapes = [jax.ShapeDtypeStruct((3,) + p.shape[1:], p.dtype) for p in pbs]
        self.sems = [pltpu.SemaphoreType.DMA((n, 3)), pltpu.SemaphoreType.DMA((n, 3))]

    def _copies(self, srcs, outs, sems):
        send_sems, recv_sems = sems
        x, y, c = lax.axis_index("x"), lax.axis_index("y"), lax.axis_index("c")
        chips = [(1 - x, y), (x, 1 - y), (1 - x, 1 - y)]
        return [pltpu.make_async_remote_copy(
            src_ref=srcs[a].at[2 * chip[0] + chip[1]], dst_ref=outs[a].at[j], send_sem=send_sems.at[a, j],
            recv_sem=recv_sems.at[a, j], device_id=(*chip, c), device_id_type=MESH)
            for a in range(len(srcs)) for j, chip in enumerate(chips)]


class _DirectGather(_Exchange):
    def __init__(self, arrays):
        n = len(arrays)
        self.inputs = list(arrays)
        self.out_shapes = [jax.ShapeDtypeStruct((N_DEV,) + a.shape, a.dtype) for a in arrays]
        self.sems = [pltpu.SemaphoreType.DMA((n, 7)), pltpu.SemaphoreType.DMA((n, 7)), pltpu.SemaphoreType.DMA((n,))]

    def _copies(self, srcs, outs, sems):
        send_sems, recv_sems, local_sems = sems
        x, y, c = lax.axis_index("x"), lax.axis_index("y"), lax.axis_index("c")
        me = 4 * x + 2 * y + c
        remote, local = [], []
        for a in range(len(srcs)):
            local.append(pltpu.make_async_copy(srcs[a], outs[a].at[me], local_sems.at[a]))
            for k in range(1, N_DEV):
                peer = (x ^ (k >> 2), y ^ ((k >> 1) & 1), c ^ (k & 1))
                remote.append(pltpu.make_async_remote_copy(
                    src_ref=srcs[a], dst_ref=outs[a].at[me], send_sem=send_sems.at[a, k - 1],
                    recv_sem=recv_sems.at[a, k - 1], device_id=peer, device_id_type=MESH))
        return remote, local

    def start(self, srcs, outs, sems):
        remote, local = self._copies(srcs, outs, sems)
        for cp in local + remote:
            cp.start()

    def finish(self, srcs, outs, sems):
        remote, local = self._copies(srcs, outs, sems)
        for cp in remote:
            cp.wait_recv()
        for cp in remote:
            cp.wait_send()
        for cp in local:
            cp.wait()


class _Both(_Exchange):
    def __init__(self, a, b):
        self.a, self.b = a, b
        self.inputs = list(a.inputs) + list(b.inputs)
        self.out_shapes = list(a.out_shapes) + list(b.out_shapes)
        self.sems = list(a.sems) + list(b.sems)

    def _split(self, srcs, outs, sems):
        na, oa, sa = len(self.a.inputs), len(self.a.out_shapes), len(self.a.sems)
        return (srcs[:na], outs[:oa], sems[:sa]), (srcs[na:], outs[oa:], sems[sa:])

    def start(self, srcs, outs, sems):
        pa, pb = self._split(srcs, outs, sems)
        self.a.start(*pa)
        self.b.start(*pb)

    def middle(self, srcs, outs, sems):
        pa, pb = self._split(srcs, outs, sems)
        self.a.middle(*pa)
        self.b.middle(*pb)

    def finish(self, srcs, outs, sems):
        pa, pb = self._split(srcs, outs, sems)
        self.a.finish(*pa)
        self.b.finish(*pb)


_ANY = pl.BlockSpec(memory_space=pl.ANY)


def _run_exchange(ex, name):
    n_in, n_out = len(ex.inputs), len(ex.out_shapes)

    def body(*refs):
        srcs, outs, sems = refs[:n_in], refs[n_in:n_in + n_out], refs[n_in + n_out:]
        ex.start(srcs, outs, sems)
        ex.middle(srcs, outs, sems)
        ex.finish(srcs, outs, sems)

    return pl.pallas_call(
        body, name=name, out_shape=list(ex.out_shapes), in_specs=[_ANY] * n_in, out_specs=[_ANY] * n_out,
        scratch_shapes=list(ex.sems),
    )(*ex.inputs)


CARRY_MIDDLE = 0.7


def _call(body, name, grid, in_specs, out_specs, out_shape, args, sem, carry=None, middle=CARRY_MIDDLE, scratch=()):
    if carry is None:
        return pl.pallas_call(functools.partial(body), name=name, grid=grid, in_specs=in_specs, out_specs=out_specs,
                              out_shape=out_shape, scratch_shapes=list(scratch), compiler_params=_cparams(sem))(*args)
    n_in, n_out = len(in_specs), len(out_specs)
    nc_in, nc_out = len(carry.inputs), len(carry.out_shapes)

    def carried(*refs):
        ins = refs[:n_in]
        c_in = refs[n_in:n_in + nc_in]
        outs = refs[n_in + nc_in:n_in + nc_in + n_out]
        c_out = refs[n_in + nc_in + n_out:n_in + nc_in + n_out + nc_out]
        own = refs[n_in + nc_in + n_out + nc_out:n_in + nc_in + n_out + nc_out + len(scratch)]
        sems = refs[n_in + nc_in + n_out + nc_out + len(scratch):]
        ids = [pl.program_id(i) for i in range(len(grid))]
        is_first = functools.reduce(jnp.logical_and, [i == 0 for i in ids])
        is_last = functools.reduce(jnp.logical_and, [i == g - 1 for i, g in zip(ids, grid)])
        @pl.when(is_first)
        def _():
            carry.start(c_in, c_out, sems)

        if middle is not None:
            @pl.when(functools.reduce(jnp.logical_and, [ids[0] == round(middle * (grid[0] - 1))]
                                      + [i == 0 for i in ids[1:]]))
            def _():
                carry.middle(c_in, c_out, sems)

        body(*ins, *outs, *own)

        @pl.when(is_last)
        def _():
            if middle is None:
                carry.middle(c_in, c_out, sems)
            carry.finish(c_in, c_out, sems)

    return pl.pallas_call(
        carried, name=name, grid=grid, in_specs=list(in_specs) + [_ANY] * nc_in,
        out_specs=list(out_specs) + [_ANY] * nc_out, out_shape=list(out_shape) + list(carry.out_shapes),
        scratch_shapes=list(scratch) + list(carry.sems), compiler_params=_cparams(sem))(*args, *carry.inputs)


def _rs_sum2_adam(pos, p, r2, w, m, v, name):
    _, rows, d = p.shape
    tr = rows // 2 if rows % 16 == 0 else rows

    def body(pos_ref, p_ref, r_ref, w_ref, m_ref, v_ref, g_ref, d_ref, nm_ref, nv_ref):
        r = r_ref[...].astype(F32)
        g = ((p_ref[...] + r[0]) + r[1]) + r[2]
        g_ref[...] = g
        d_ref[...], nm_ref[...], nv_ref[...] = _adam_math(w_ref[...], g, m_ref[...], v_ref[...])

    blk = pl.BlockSpec((tr, d), lambda i, pos_ref: (i, 0))
    return pl.pallas_call(
        body, name=name,
        grid_spec=pltpu.PrefetchScalarGridSpec(
            num_scalar_prefetch=1, grid=(rows // tr,),
            in_specs=[pl.BlockSpec((None, tr, d), lambda i, pos_ref: (2 * pos_ref[0] + pos_ref[1], i, 0)),
                      pl.BlockSpec((3, tr, d), lambda i, pos_ref: (0, i, 0)), blk, blk, blk],
            out_specs=[blk] * 4),
        out_shape=[jax.ShapeDtypeStruct((rows, d), F32)] * 4,
        compiler_params=_cparams(("arbitrary",)),
    )(pos, p, r2, w, m, v)


def _ffn_chunk(f):
    for cand in (256, 128):
        if f % cand == 0:
            return cand
    return f


def _loss_head(x, gg, target, loss_ref, dg_ref):
    @pl.when(pl.program_id(0) == 0)
    def _():
        loss_ref[...] = jnp.zeros_like(loss_ref)
        dg_ref[...] = jnp.zeros_like(dg_ref)

    r = lax.rsqrt(jnp.mean(x * x, axis=-1, keepdims=True) + EPS)
    xhat = x * r
    e = xhat * gg - target
    loss_ref[...] += 0.5 * jnp.sum(jnp.mean(e * e, axis=-1, keepdims=True), axis=0, keepdims=True)
    dy = e * (1.0 / x.shape[-1])
    dg_ref[...] += jnp.sum(dy * xhat, axis=0, keepdims=True)
    dxhat = dy * gg
    return r * (dxhat - xhat * jnp.mean(dxhat * xhat, axis=-1, keepdims=True))


def _ffn_gate(h, g, wg_t, name, tm, carry=None):
    s, d = h.shape
    f = wg_t.shape[0]

    def body(h_ref, g_ref, wg_ref, n_ref, gate_ref):
        x = h_ref[...]
        r = lax.rsqrt(jnp.mean(x * x, axis=-1, keepdims=True) + EPS)
        nb = (x * r * g_ref[...]).astype(BF16)
        n_ref[...] = nb
        gate_ref[...] = _dot_nt(nb, wg_ref[...]).astype(BF16)

    row = lambda w: pl.BlockSpec((tm, w), lambda i: (i, 0))
    return _call(body, name, (s // tm,), [row(d), _whole((1, d)), _whole((f, d))], [row(d), row(f)],
                 [jax.ShapeDtypeStruct((s, d), BF16), jax.ShapeDtypeStruct((s, f), BF16)], (h, g, wg_t),
                 ("arbitrary",), carry)


def _ffn_up(n, gate, wu_t, name, tm, carry=None):
    s, d = n.shape
    f = wu_t.shape[0]
    tf = _ffn_chunk(f)

    def body(n_ref, gate_ref, wu_ref, up_ref, act_ref, act_t_ref):
        nb = n_ref[...]
        for j in range(f // tf):
            sl = slice(j * tf, (j + 1) * tf)
            up = _dot_nt(nb, wu_ref[sl, :])
            gate = gate_ref[:, sl].astype(F32)
            up_ref[:, sl] = up.astype(BF16)
            act = (gate * _sigmoid(gate) * up).astype(BF16)
            act_ref[:, sl] = act
            act_t_ref[sl, :] = act.T

    row = lambda w: pl.BlockSpec((tm, w), lambda i: (i, 0))
    return _call(body, name, (s // tm,), [row(d), row(f), _whole((f, d))],
                 [row(f), row(f), pl.BlockSpec((f, tm), lambda i: (0, i))],
                 [jax.ShapeDtypeStruct((s, f), BF16)] * 2 + [jax.ShapeDtypeStruct((f, s), BF16)], (n, gate, wu_t),
                 ("arbitrary",), carry)


def _ffn_fwd(h, g, wg_t, wu_t, wd, name, tm, carry=None, head=None, middle=CARRY_MIDDLE):
    s, d = h.shape
    f = wg_t.shape[0]
    tf = _ffn_chunk(f)

    def body(h_ref, g_ref, wg_ref, wu_ref, wd_ref, *refs):
        if head is None:
            o_ref, n_ref, gate_ref, up_ref, act_t_ref, act_ref = refs
        else:
            gf_ref, t_ref, o_ref, n_ref, gate_ref, up_ref, act_t_ref, loss_ref, dgf_ref, act_ref = refs
        x = h_ref[...]
        r = lax.rsqrt(jnp.mean(x * x, axis=-1, keepdims=True) + EPS)
        nb = (x * r * g_ref[...]).astype(BF16)
        n_ref[...] = nb
        for j in range(f // tf):
            sl = slice(j * tf, (j + 1) * tf)
            gate = _dot_nt(nb, wg_ref[sl, :])
            up = _dot_nt(nb, wu_ref[sl, :])
            gate_ref[:, sl] = gate.astype(BF16)
            up_ref[:, sl] = up.astype(BF16)
            act = gate * _sigmoid(gate) * up
            act = act.astype(BF16)
            act_ref[:, sl] = act
            act_t_ref[sl, :] = act.T
        h_out = x + 0.5 * _dot_nn(act_ref[...], wd_ref[...])
        o_ref[...] = h_out if head is None else _loss_head(h_out, gf_ref[...], t_ref[...], loss_ref, dgf_ref)

    row = lambda w: pl.BlockSpec((tm, w), lambda i: (i, 0))
    in_specs = [row(d), _whole((1, d)), _whole((f, d)), _whole((f, d)), _whole((f, d))]
    out_specs = [row(d), row(d), row(f), row(f), pl.BlockSpec((f, tm), lambda i: (0, i))]
    out_shape = ([jax.ShapeDtypeStruct((s, d), F32), jax.ShapeDtypeStruct((s, d), BF16)]
                 + [jax.ShapeDtypeStruct((s, f), BF16)] * 2 + [jax.ShapeDtypeStruct((f, s), BF16)])
    args = (h, g, wg_t, wu_t, wd)
    scratch = (pltpu.VMEM((tm, f), BF16),)
    if head is not None:
        in_specs += [_whole((1, d)), row(d)]
        out_specs += [pl.BlockSpec((1, 1), lambda i: (0, 0)), pl.BlockSpec((1, d), lambda i: (0, 0))]
        out_shape += [jax.ShapeDtypeStruct((1, 1), F32), jax.ShapeDtypeStruct((1, d), F32)]
        args += tuple(head)
    return _call(body, name, (s // tm,), in_specs, out_specs, out_shape, args, ("arbitrary",), carry, middle, scratch)


def _gate_grads(dh_ref, gate_ref, up_ref, wd_ref, dgate_ref, dup_ref, dgate_t_ref, dup_t_ref, dhh_ref, tf):
    dhh = (0.5 * dh_ref[...]).astype(BF16)
    dhh_ref[...] = dhh
    for j in range(gate_ref.shape[1] // tf):
        sl = slice(j * tf, (j + 1) * tf)
        gt = gate_ref[:, sl].astype(F32)
        u = up_ref[:, sl].astype(F32)
        dact = _dot_nt(dhh, wd_ref[sl, :])
        sg = _sigmoid(gt)
        dup = dact * (gt * sg)
        dgate = dact * u * (sg * (1.0 + gt * (1.0 - sg)))
        dup, dgate = dup.astype(BF16), dgate.astype(BF16)
        if dup_ref is not None:
            dup_ref[:, sl] = dup
            dgate_ref[:, sl] = dgate
        dup_t_ref[sl, :] = dup.T
        dgate_t_ref[sl, :] = dgate.T


def _input_grad(h_ref, dh_ref, dgate_ref, dup_ref, g_ref, wg_ref, wu_ref, o_ref, dg_ref, transposed=False):
    x = h_ref[...]
    r = lax.rsqrt(jnp.mean(x * x, axis=-1, keepdims=True) + EPS)
    xhat = x * r
    dot = _dot_tn if transposed else _dot_nn
    dn = dot(dgate_ref[...], wg_ref[...]) + dot(dup_ref[...], wu_ref[...])
    dxhat = dn * g_ref[...]
    o_ref[...] = dh_ref[...] + r * (dxhat - xhat * jnp.mean(dxhat * xhat, axis=-1, keepdims=True))

    @pl.when(pl.program_id(0) == 0)
    def _():
        dg_ref[...] = jnp.zeros_like(dg_ref)

    dg_ref[...] += jnp.sum(dn * xhat, axis=0, keepdims=True)


def _ffn_bwd(h_in, dh_out, gate, up, g, wg_t, wu_t, wd, w_out, name, tm):
    s, d = h_in.shape
    f = gate.shape[1]
    w = w_out.shape[0]
    tf = _ffn_chunk(f)

    def body(h_ref, dh_ref, gate_ref, up_ref, g_ref, wg_ref, wu_ref, wd_ref, wo_ref,
             o_ref, dg_ref, dgate_t_ref, dup_t_ref, dhh_ref, dmix_ref, dhb_ref, dgate_ref, dup_ref):
        _gate_grads(dh_ref, gate_ref, up_ref, wd_ref, dgate_ref, dup_ref, dgate_t_ref, dup_t_ref, dhh_ref, tf)
        _input_grad(h_ref, dh_ref, dgate_ref, dup_ref, g_ref, wg_ref, wu_ref, o_ref, dg_ref)
        dhb = o_ref[...].astype(BF16)
        dhb_ref[...] = dhb
        dmix_ref[...] = _dot_nt(dhb, wo_ref[...])

    row = lambda c: pl.BlockSpec((tm, c), lambda i: (i, 0))
    col = pl.BlockSpec((f, tm), lambda i: (0, i))
    return pl.pallas_call(
        body, name=name, grid=(s // tm,),
        in_specs=[row(d), row(d), row(f), row(f), _whole((1, d)), _whole((f, d)), _whole((f, d)), _whole((f, d)),
                  _whole((w, d))],
        out_specs=[row(d), pl.BlockSpec((1, d), lambda i: (0, 0)), col, col, row(d), row(w), row(d)],
        out_shape=[jax.ShapeDtypeStruct((s, d), F32), jax.ShapeDtypeStruct((1, d), F32),
                   jax.ShapeDtypeStruct((f, s), BF16), jax.ShapeDtypeStruct((f, s), BF16),
                   jax.ShapeDtypeStruct((s, d), BF16), jax.ShapeDtypeStruct((s, w), F32),
                   jax.ShapeDtypeStruct((s, d), BF16)],
        scratch_shapes=[pltpu.VMEM((tm, f), BF16), pltpu.VMEM((tm, f), BF16)],
        compiler_params=_cparams(("arbitrary",)),
    )(h_in, dh_out, gate, up, g, wg_t, wu_t, wd, w_out)


def _ffn_bwd_gates(dh_out, gate, up, wd, name, tm, carry=None):
    s, d = dh_out.shape
    f = gate.shape[1]
    tf = _ffn_chunk(f)

    def body(dh_ref, gate_ref, up_ref, wd_ref, dgate_t_ref, dup_t_ref, dhh_ref):
        _gate_grads(dh_ref, gate_ref, up_ref, wd_ref, None, None, dgate_t_ref, dup_t_ref, dhh_ref, tf)

    row = lambda w: pl.BlockSpec((tm, w), lambda i: (i, 0))
    col = pl.BlockSpec((f, tm), lambda i: (0, i))
    return _call(
        body, name, (s // tm,), [row(d), row(f), row(f), _whole((f, d))], [col, col, row(d)],
        [jax.ShapeDtypeStruct((f, s), BF16)] * 2 + [jax.ShapeDtypeStruct((s, d), BF16)],
        (dh_out, gate, up, wd), ("arbitrary",), carry)


def _ffn_bwd_input(h_in, dh_out, dgate_t, dup_t, g, wg_t, wu_t, name, tm, carry=None):
    s, d = h_in.shape
    f = dgate_t.shape[0]

    row = lambda w: pl.BlockSpec((tm, w), lambda i: (i, 0))
    col = pl.BlockSpec((f, tm), lambda i: (0, i))
    return _call(
        functools.partial(_input_grad, transposed=True), name, (s // tm,),
        [row(d), row(d), col, col, _whole((1, d)), _whole((f, d)), _whole((f, d))],
        [row(d), pl.BlockSpec((1, d), lambda i: (0, 0))],
        [jax.ShapeDtypeStruct((s, d), F32), jax.ShapeDtypeStruct((1, d), F32)],
        (h_in, dh_out, dgate_t, dup_t, g, wg_t, wu_t), ("arbitrary",), carry)


def _wgrad_rs1(pos, a_t, b, name, carry=None):
    f, s = a_t.shape
    d = b.shape[1]
    fk = f // N_DEV
    nc_in = 0 if carry is None else len(carry.inputs)
    nc_out = 0 if carry is None else len(carry.out_shapes)

    def body(pos_ref, a_ref, b_ref, *refs):
        c_in = refs[:nc_in]
        p_ref, pb_ref = refs[nc_in:nc_in + 2]
        c_out = refs[nc_in + 2:nc_in + 2 + nc_out]
        stage, land, send_sems, recv_sems = refs[nc_in + 2 + nc_out:nc_in + 6 + nc_out]
        c_sems = refs[nc_in + 6 + nc_out:]
        t = pl.program_id(0)
        q = t % 4
        x, y, c = lax.axis_index("x"), lax.axis_index("y"), lax.axis_index("c")

        def push(k):
            return pltpu.make_async_remote_copy(src_ref=stage.at[k], dst_ref=land.at[k], send_sem=send_sems.at[k],
                                                recv_sem=recv_sems.at[k], device_id=(x, y, 1 - c), device_id_type=MESH)

        if carry is not None:
            @pl.when(t == 0)
            def _():
                carry.start(c_in, c_out, c_sems)

        g = _dot_nn(a_ref[...], b_ref[...])

        @pl.when(t < 4)
        def _():
            stage[q] = g.astype(BF16)
            push(q).start()

        @pl.when(t >= 4)
        def _():
            push(q).wait_recv()
            p = g + land[q].astype(F32)
            p_ref[...] = p
            pb_ref[...] = p.astype(BF16)

        @pl.when(t == 7)
        def _():
            for k in range(4):
                push(k).wait_send()
            if carry is not None:
                carry.middle(c_in, c_out, c_sems)
                carry.finish(c_in, c_out, c_sems)

    def shard(t, pos_ref):
        return 4 * ((t % 4) // 2) + 2 * (t % 2) + jnp.where(t < 4, 1 - pos_ref[2], pos_ref[2])

    out = pl.BlockSpec((None, fk, d), lambda t, pos_ref: (jnp.maximum(t - 4, 0), 0, 0))
    return pl.pallas_call(
        body, name=name,
        grid_spec=pltpu.PrefetchScalarGridSpec(
            num_scalar_prefetch=1, grid=(8,),
            in_specs=[pl.BlockSpec((fk, s), lambda t, pos_ref: (shard(t, pos_ref), 0)),
                      pl.BlockSpec((s, d), lambda t, pos_ref: (0, 0), pipeline_mode=pl.Buffered(1))]
            + [_ANY] * nc_in,
            out_specs=[out, out] + [_ANY] * nc_out,
            scratch_shapes=[pltpu.VMEM((4, fk, d), BF16), pltpu.VMEM((4, fk, d), BF16),
                            pltpu.SemaphoreType.DMA((4,)), pltpu.SemaphoreType.DMA((4,))]
            + ([] if carry is None else list(carry.sems))),
        out_shape=[jax.ShapeDtypeStruct((4, fk, d), F32), jax.ShapeDtypeStruct((4, fk, d), BF16)]
        + ([] if carry is None else list(carry.out_shapes)),
        compiler_params=_cparams(("arbitrary",)),
    )(pos, a_t, b, *([] if carry is None else carry.inputs))


def _rope(t, c, sa, sb, reps):
    c, sa, sb = (jnp.tile(v, (1, reps)) if reps > 1 else v for v in (c, sa, sb))
    w = t.shape[1]
    return t * c + pltpu.roll(t, w - 8, 1) * sa + pltpu.roll(t, 8, 1) * sb


def _rope_bwd(dt, c, sa, sb, reps):
    c, sa, sb = (jnp.tile(v, (1, reps)) if reps > 1 else v for v in (c, sa, sb))
    w = dt.shape[1]
    return dt * c + pltpu.roll(dt * sa, 8, 1) + pltpu.roll(dt * sb, w - 8, 1)


def _ffn_down_mix_in(x, act, wd, g, win_t, tabs, name, tm, carry=None):
    s, d = x.shape
    f = wd.shape[0]
    n_in = win_t.shape[0]

    def body(x_ref, act_ref, wd_ref, g_ref, w_ref, c_ref, sa_ref, sb_ref, h_ref, q_ref, k_ref, v_ref, pc_ref, n_ref):
        x = x_ref[...] + 0.5 * _dot_nn(act_ref[...], wd_ref[...])
        h_ref[...] = x
        r = lax.rsqrt(jnp.mean(x * x, axis=-1, keepdims=True) + EPS)
        nb = (x * r * g_ref[...]).astype(BF16)
        n_ref[...] = nb
        u = _dot_nt(nb, w_ref[...])
        c, sa, sb = c_ref[...], sa_ref[...], sb_ref[...]
        q_ref[...] = _rope(u[:, :ATTN_W], c, sa, sb, ATTN_W // 128).astype(BF16)
        k_ref[...] = _rope(u[:, ATTN_W:ATTN_W + KV_W], c, sa, sb, 1).astype(BF16)
        v_ref[...] = u[:, ATTN_W + KV_W:ATTN_W + 2 * KV_W].astype(BF16)
        pc_ref[...] = u[:, ATTN_W + 2 * KV_W:]

    row = lambda w: pl.BlockSpec((tm, w), lambda i: (i, 0))
    return _call(
        body, name, (s // tm,),
        [row(d), row(f), _whole((f, d)), _whole((1, d)), _whole((n_in, d)), row(128), row(128), row(128)],
        [row(d), row(ATTN_W), row(KV_W), row(KV_W), row(POOL_W), row(d)],
        [jax.ShapeDtypeStruct((s, d), F32), jax.ShapeDtypeStruct((s, ATTN_W), BF16),
         jax.ShapeDtypeStruct((s, KV_W), BF16), jax.ShapeDtypeStruct((s, KV_W), BF16),
         jax.ShapeDtypeStruct((s, POOL_W), F32), jax.ShapeDtypeStruct((s, d), BF16)],
        (x, act, wd, g, win_t, *tabs), ("arbitrary",), carry)


def _band_mask(n, nb, transposed):
    shape = (3 * BLK, 2 * BLK) if transposed else (2 * BLK, 3 * BLK)
    i = lax.broadcasted_iota(jnp.int32, shape, 1 if transposed else 0) % BLK
    j = lax.broadcasted_iota(jnp.int32, shape, 0 if transposed else 1)
    kpos = (n - 1) * BLK + j
    return (j >= i) & (j <= i + 2 * BLK) & (kpos >= 0) & (kpos < nb * BLK)


def _block_diag(t, kh):
    tf = t.astype(F32)
    tr = pltpu.roll(tf, HEAD_DIM, 1)
    lo = lax.broadcasted_iota(jnp.int32, tf.shape, 1) < HEAD_DIM
    top, bot = (tf, tr) if kh == 0 else (tr, tf)
    return jnp.concatenate([jnp.where(lo, top, 0.0), jnp.where(lo, 0.0, bot)], axis=0).astype(BF16)


def _fold_diag(tbd):
    lo = lax.broadcasted_iota(jnp.int32, (3 * BLK, 2 * HEAD_DIM), 1) < HEAD_DIM
    t = jnp.where(lo, tbd[:3 * BLK], tbd[3 * BLK:])
    return t + pltpu.roll(t, HEAD_DIM, 1)


def _stack_pairs(x, kh):
    return jnp.concatenate([x[:, (2 * kh) * 128:(2 * kh + 1) * 128], x[:, (2 * kh + 1) * 128:(2 * kh + 2) * 128]], axis=0)


def _sink_of(sink_ref, kh, half, axis):
    shape = (2 * BLK, 1) if axis == 0 else (1, 2 * BLK)
    first = lax.broadcasted_iota(jnp.int32, shape, axis) < BLK
    return jnp.where(first, sink_ref[0, GROUP * kh + half], sink_ref[0, GROUP * kh + 2 + half])


def _softmax_sink(sc, valid, sink, axis):
    sc = jnp.where(valid, sc, -1e30)
    m = jnp.maximum(jnp.max(sc, axis=axis, keepdims=True), sink)
    e = jnp.exp(sc - m)
    es = jnp.exp(sink - m)
    inv = 1.0 / (jnp.sum(e, axis=axis, keepdims=True) + es)
    return e * inv, es * inv


def _attn_blocks_per_step(nb):
    return next(nq for nq in (4, 2, 1) if nb % nq == 0)


def _band_specs(nq, nb, w, col=0):
    return [pl.BlockSpec((BLK, w), lambda m: (jnp.maximum(nq * m - 1, 0), col)),
            pl.BlockSpec((nq * BLK, w), lambda m: (m, col)),
            pl.BlockSpec((BLK, w), lambda m: (jnp.minimum(nq * m + nq, nb - 1), col))]


def _attn_pool_fwd(h, w_out, q, k, v, pc, sink, pool_w, pool_scale, pband, name, carry=None, middle=CARRY_MIDDLE):
    s, d = h.shape
    nb = s // BLK
    nq = _attn_blocks_per_step(nb)

    def body(sink_ref, q_ref, k0, k1, k2, v0, v1, v2, p0, p1, p2, pw_ref, ps_ref, pb_ref, h_ref, wo_ref,
             h_out_ref, o_t_ref, o_ref):
        kall = jnp.concatenate([k0[...], k1[...], k2[...]], axis=0)
        vall = jnp.concatenate([v0[...], v1[...], v2[...]], axis=0)
        pall = jnp.concatenate([p0[...], p1[...], p2[...]], axis=0).astype(BF16)
        qall = q_ref[...] * SCORE_SCALE
        for j in range(nq):
            n = pl.program_id(0) * nq + j
            rows, band = slice(j * BLK, (j + 1) * BLK), slice(j * BLK, (j + 3) * BLK)
            valid = _band_mask(n, nb, False)
            kb, vb, qs = kall[band], vall[band], qall[rows]
            for kh in range(N_KV):
                sc = _dot_nt(_stack_pairs(qs, kh), _block_diag(kb, kh))
                p = [_softmax_sink(sc[:, half * 3 * BLK:(half + 1) * 3 * BLK], valid,
                                   _sink_of(sink_ref, kh, half, 0), 1)[0] for half in range(2)]
                o2 = _dot_nn(jnp.concatenate(p, axis=1).astype(BF16), _block_diag(vb, kh)).astype(BF16)
                o_ref[rows, (2 * kh) * 128:(2 * kh + 1) * 128] = o2[:BLK]
                o_ref[rows, (2 * kh + 1) * 128:(2 * kh + 2) * 128] = o2[BLK:]
            ext = pall[band]
            var = _variant_index(n, nb)
            for gi in range(POOL_G):
                gsl = slice(gi * POOL_GW, (gi + 1) * POOL_GW)
                dg = _dot_nn(pb_ref[var, gi], ext[:, gsl])
                yg = _dot_nn(dg.astype(BF16), pw_ref[gi].astype(BF16))
                o_ref[rows, ATTN_W + gi * POOL_GW:ATTN_W + (gi + 1) * POOL_GW] = (yg * ps_ref[:, gsl]).astype(BF16)
        mix = o_ref[...]
        o_t_ref[...] = mix.T
        h_out_ref[...] = h_ref[...] + _dot_nn(mix, wo_ref[...])

    mix_w = ATTN_W + POOL_W
    return _call(
        body, name, (nb // nq,),
        [pl.BlockSpec(memory_space=pltpu.SMEM), pl.BlockSpec((nq * BLK, ATTN_W), lambda m: (m, 0)),
         *_band_specs(nq, nb, KV_W), *_band_specs(nq, nb, KV_W), *_band_specs(nq, nb, POOL_W),
         _whole((POOL_G, POOL_GW, POOL_GW)), _whole((1, POOL_W)), _whole(pband.shape),
         pl.BlockSpec((nq * BLK, d), lambda m: (m, 0)), _whole((mix_w, d))],
        [pl.BlockSpec((nq * BLK, d), lambda m: (m, 0)), pl.BlockSpec((mix_w, nq * BLK), lambda m: (0, m))],
        [jax.ShapeDtypeStruct((s, d), F32), jax.ShapeDtypeStruct((mix_w, s), BF16)],
        (sink, q, k, k, k, v, v, v, pc, pc, pc, pool_w, pool_scale, pband, h, w_out), ("arbitrary",), carry, middle,
        scratch=(pltpu.VMEM((nq * BLK, mix_w), BF16),))


def _attn_pool_bwd(q, k, v, pc, dmix, sink, pool_w, pool_scale, pband, ptband, name, carry=None):
    s = q.shape[0]
    nb = s // BLK
    nq = _attn_blocks_per_step(nb)

    def body(sink_ref, q_ref, k0, k1, k2, v0, v1, v2, p0, p1, p2, da_ref, d0, d1, d2, pw_ref, ps_ref, pb_ref, ptb_ref,
             dq_ref, dk_ref, dv_ref, dpc_ref, dsink_ref, dpw_ref, dps_ref):
        @pl.when(pl.program_id(0) == 0)
        def _():
            dsink_ref[...] = jnp.zeros_like(dsink_ref)
            dpw_ref[...] = jnp.zeros_like(dpw_ref)
            dps_ref[...] = jnp.zeros_like(dps_ref)

        kall = jnp.concatenate([k0[...], k1[...], k2[...]], axis=0)
        vall = jnp.concatenate([v0[...], v1[...], v2[...]], axis=0)
        pall = jnp.concatenate([p0[...], p1[...], p2[...]], axis=0).astype(BF16)
        dpall = jnp.concatenate([d0[...], d1[...], d2[...]], axis=0)
        lo = lax.broadcasted_iota(jnp.int32, (3 * BLK, KV_W), 1) < HEAD_DIM
        for j in range(nq):
            n = pl.program_id(0) * nq + j
            rows, band = slice(j * BLK, (j + 1) * BLK), slice(j * BLK, (j + 3) * BLK)
            valid = _band_mask(n, nb, True)
            kb, vb, qb = kall[band], vall[band], q_ref[rows, :]
            qs = qb * SCORE_SCALE
            da = da_ref[rows, :].astype(BF16)
            dk_fold, dv_fold = [], []
            for kh in range(N_KV):
                kbd, vbd = _block_diag(kb, kh), _block_diag(vb, kh)
                q2, do2 = _stack_pairs(qb, kh), _stack_pairs(da, kh)
                sc_t = _dot_nt(kbd, _stack_pairs(qs, kh))
                dp_t = _dot_nt(vbd, do2)
                p_t, ds_t = [], []
                for half in range(2):
                    keys = slice(half * 3 * BLK, (half + 1) * 3 * BLK)
                    p, ps = _softmax_sink(sc_t[keys], valid, _sink_of(sink_ref, kh, half, 1), 0)
                    delta = jnp.sum(p * dp_t[keys], axis=0, keepdims=True)
                    p_t.append(p.astype(BF16))
                    ds_t.append((p * (dp_t[keys] - delta)).astype(BF16))
                    dsk = -ps * delta
                    for pair in range(2):
                        h = GROUP * kh + 2 * pair + half
                        part = jnp.sum(dsk[:, pair * BLK:(pair + 1) * BLK], axis=1, keepdims=True)
                        dsink_ref[h:h + 1, :] += jnp.broadcast_to(part, (1, 128))
                p_t = jnp.concatenate(p_t, axis=0)
                ds_t = jnp.concatenate(ds_t, axis=0)
                dq2 = _dot_tn(ds_t, kbd) * SCORE_SCALE
                dq_ref[rows, (2 * kh) * 128:(2 * kh + 1) * 128] = dq2[:BLK]
                dq_ref[rows, (2 * kh + 1) * 128:(2 * kh + 2) * 128] = dq2[BLK:]
                dk_fold.append(_fold_diag(_dot_nn(ds_t, q2)) * SCORE_SCALE)
                dv_fold.append(_fold_diag(_dot_nn(p_t, do2)))
            dk_all = jnp.where(lo, dk_fold[0], dk_fold[1])
            dv_all = jnp.where(lo, dv_fold[0], dv_fold[1])
            for t in range(3):
                dk_ref[j, t] = dk_all[t * BLK:(t + 1) * BLK]
                dv_ref[j, t] = dv_all[t * BLK:(t + 1) * BLK]
            ext, dpe = pall[band], dpall[band]
            dpc_cur = dpall[(j + 1) * BLK:(j + 2) * BLK]
            var = _variant_index(n, nb)
            for gi in range(POOL_G):
                gsl = slice(gi * POOL_GW, (gi + 1) * POOL_GW)
                wg = pw_ref[gi].astype(BF16)
                sc = ps_ref[:, gsl]
                dgb = _dot_nn(pb_ref[var, gi], ext[:, gsl]).astype(BF16)
                yg = _dot_nn(dgb, wg)
                dps_ref[:, gsl] += jnp.sum(dpc_cur[:, gsl] * yg, axis=0, keepdims=True)
                dpw_ref[gi] += _dot_tn(dgb, (dpc_cur[:, gsl] * sc).astype(BF16))
                dd = _dot_nt((dpe[:, gsl] * sc).astype(BF16), wg)
                dpc_ref[rows, gsl] = _dot_nn(ptb_ref[var, gi], dd.astype(BF16))

    fixed = lambda shape: pl.BlockSpec(shape, lambda m: (0,) * len(shape))
    return _call(
        body, name, (nb // nq,),
        [pl.BlockSpec(memory_space=pltpu.SMEM), pl.BlockSpec((nq * BLK, ATTN_W), lambda m: (m, 0)),
         *_band_specs(nq, nb, KV_W), *_band_specs(nq, nb, KV_W), *_band_specs(nq, nb, POOL_W),
         pl.BlockSpec((nq * BLK, ATTN_W), lambda m: (m, 0)), *_band_specs(nq, nb, POOL_W, 1),
         _whole((POOL_G, POOL_GW, POOL_GW)), _whole((1, POOL_W)), _whole(pband.shape), _whole(ptband.shape)],
        [pl.BlockSpec((nq * BLK, ATTN_W), lambda m: (m, 0)),
         pl.BlockSpec((nq, 3, BLK, KV_W), lambda m: (m, 0, 0, 0)),
         pl.BlockSpec((nq, 3, BLK, KV_W), lambda m: (m, 0, 0, 0)),
         pl.BlockSpec((nq * BLK, POOL_W), lambda m: (m, 0)),
         fixed((N_HEADS, 128)), fixed((POOL_G, POOL_GW, POOL_GW)), fixed((1, POOL_W))],
        [jax.ShapeDtypeStruct((s, ATTN_W), F32), jax.ShapeDtypeStruct((nb, 3, BLK, KV_W), F32),
         jax.ShapeDtypeStruct((nb, 3, BLK, KV_W), F32), jax.ShapeDtypeStruct((s, POOL_W), F32),
         jax.ShapeDtypeStruct((N_HEADS, 128), F32),
         jax.ShapeDtypeStruct((POOL_G, POOL_GW, POOL_GW), F32), jax.ShapeDtypeStruct((1, POOL_W), F32)],
        (sink, q, k, k, k, v, v, v, pc, pc, pc, dmix, dmix, dmix, dmix, pool_w, pool_scale, pband, ptband),
        ("arbitrary",), carry)


def _mix_in_bwd(h, dh, g, win_t, dq, dkp, dvp, dpc, tabs, name, tm, carry=None):
    s, d = h.shape
    nb = s // BLK
    nt = tm // BLK
    n_in = win_t.shape[0]

    def band_sum(n, before, own, after, prev_last, next_first):
        lo = (n > 0).astype(F32)
        hi = (n < s // tm - 1).astype(F32)
        blocks = []
        for b in range(nt):
            from_prev = prev_last[...] * lo if b == 0 else before[b - 1]
            from_next = next_first[...] * hi if b == nt - 1 else after[b + 1]
            blocks.append(from_prev + own[b] + from_next)
        return jnp.concatenate(blocks, axis=0)

    def body(h_ref, dh_ref, g_ref, w_ref, dq_ref, k2, k1, k0, kp, kn, v2, v1, v0, vp, vn, dpc_ref, c_ref, sa_ref,
             sb_ref, o_ref, du_ref, dg_ref):
        n = pl.program_id(0)
        dk = band_sum(n, k2, k1, k0, kp, kn)
        dv = band_sum(n, v2, v1, v0, vp, vn)
        c, sa, sb = c_ref[...], sa_ref[...], sb_ref[...]
        du = jnp.concatenate([_rope_bwd(dq_ref[...], c, sa, sb, ATTN_W // 128), _rope_bwd(dk, c, sa, sb, 1), dv,
                              dpc_ref[...]], axis=1)
        du_ref[...] = du.T.astype(BF16)
        dn = _dot_nn(du.astype(BF16), w_ref[...])
        x = h_ref[...]
        r = lax.rsqrt(jnp.mean(x * x, axis=-1, keepdims=True) + EPS)
        xhat = x * r
        dxhat = dn * g_ref[...]
        o_ref[...] = dh_ref[...] + r * (dxhat - xhat * jnp.mean(dxhat * xhat, axis=-1, keepdims=True))

        @pl.when(n == 0)
        def _():
            dg_ref[...] = jnp.zeros_like(dg_ref)

        dg_ref[...] += jnp.sum(dn * xhat, axis=0, keepdims=True)

    row = lambda w: pl.BlockSpec((tm, w), lambda n: (n, 0))
    slot = lambda t: pl.BlockSpec((nt, None, BLK, KV_W), lambda n, t=t: (n, t, 0, 0))
    parts = [slot(2), slot(1), slot(0),
             pl.BlockSpec((None, None, BLK, KV_W), lambda n: (jnp.maximum(nt * n - 1, 0), 2, 0, 0)),
             pl.BlockSpec((None, None, BLK, KV_W), lambda n: (jnp.minimum(nt * n + nt, nb - 1), 0, 0, 0))]
    return _call(
        body, name, (s // tm,),
        [row(d), row(d), _whole((1, d)), _whole((n_in, d)), row(ATTN_W), *parts, *parts, row(POOL_W),
         row(128), row(128), row(128)],
        [row(d), pl.BlockSpec((n_in, tm), lambda n: (0, n)), pl.BlockSpec((1, d), lambda n: (0, 0))],
        [jax.ShapeDtypeStruct((s, d), F32), jax.ShapeDtypeStruct((n_in, s), BF16), jax.ShapeDtypeStruct((1, d), F32)],
        (h, dh, g, win_t, dq, *[dkp] * 5, *[dvp] * 5, dpc, *tabs), ("arbitrary",), carry)


def _adam_math(w, g, m, v):
    m = ADAM_B1 * m + (1.0 - ADAM_B1) * g
    v = ADAM_B2 * v + (1.0 - ADAM_B2) * (g * g)
    m_hat = m / (1.0 - ADAM_B1 ** ADAM_STEP)
    v_hat = v / (1.0 - ADAM_B2 ** ADAM_STEP)
    delta = -ADAM_LR * (m_hat / (jnp.sqrt(v_hat) + ADAM_EPS) + ADAM_WD * w)
    return delta, m, v


def _adam_small(w, parts, late, m, v, name):
    rows, cols = w.shape

    def body(w_ref, p_ref, l_ref, m_ref, v_ref, g_ref, d_ref, nm_ref, nv_ref):
        g, gl = p_ref[0], l_ref[0]
        for k in range(1, N_DEV):
            g = g + p_ref[k]
            gl = gl + l_ref[k]
        g_ref[...] = g
        g_ref[SMALL_NORM1:SMALL_NORM1 + 8, :] = g[SMALL_NORM1:SMALL_NORM1 + 8] + gl
        d_ref[...], nm_ref[...], nv_ref[...] = _adam_math(w_ref[...], g_ref[...], m_ref[...], v_ref[...])

    return pl.pallas_call(
        body, name=name, out_shape=[jax.ShapeDtypeStruct((rows, cols), F32)] * 4,
    )(w, parts, late, m, v)


SMALL_NORM1 = 512


def _pack_small(norm1, normm, norm2, normf, sink, pool_w, pool_scale, loss=None):
    scale_rows = jnp.pad(pool_scale.reshape(4, 128), ((0, 4), (0, 0)))
    last_rows = jnp.pad(sink.reshape(1, N_HEADS), ((0, 7), (0, 128 - N_HEADS)))
    if loss is not None:
        last_rows = last_rows + jnp.pad(loss.reshape(1, 1), ((1, 6), (0, 127)))
    return jnp.concatenate([pool_w.reshape(512, 128), norm1.reshape(8, 128), normm.reshape(8, 128),
                            norm2.reshape(8, 128), normf.reshape(8, 128), scale_rows, last_rows], axis=0)


def _unpack_small(p):
    return dict(pool_w=p[:512].reshape(1, POOL_G, POOL_GW, POOL_GW), ffn1_norm=p[512:520].reshape(1, 1024),
                mix_norm=p[520:528].reshape(1, 1024), ffn2_norm=p[528:536].reshape(1, 1024),
                final_norm=p[536:544].reshape(1024), pool_scale=p[544:548].reshape(1, POOL_W),
                sink_logits=p[552, :N_HEADS].reshape(1, N_HEADS), loss=p[553, 0])


def kernel(x, ffn1_norm, ffn1_w_gate, ffn1_w_up, ffn1_w_down, mix_norm, w_in, sink_logits, pool_w, pool_scale, w_out, ffn2_norm, ffn2_w_gate, ffn2_w_up, ffn2_w_down, final_norm, loss_target, m_ffn1_norm, m_ffn1_w_gate, m_ffn1_w_up, m_ffn1_w_down, m_mix_norm, m_w_in, m_sink_logits, m_pool_w, m_pool_scale, m_w_out, m_ffn2_norm, m_ffn2_w_gate, m_ffn2_w_up, m_ffn2_w_down, m_final_norm, v_ffn1_norm, v_ffn1_w_gate, v_ffn1_w_up, v_ffn1_w_down, v_mix_norm, v_w_in, v_sink_logits, v_pool_w, v_pool_scale, v_w_out, v_ffn2_norm, v_ffn2_w_gate, v_ffn2_w_up, v_ffn2_w_down, v_final_norm):
    s, d = x.shape[1], x.shape[2]
    tm = min(512, s)
    tm_bwd = min(256, s)
    pos = jnp.stack([lax.axis_index("x"), lax.axis_index("y"), lax.axis_index("c")]).astype(jnp.int32)

    t_bf = lambda w: w[0].T.astype(BF16)
    full = lambda a: a.reshape(N_DEV * a.shape[1], d)
    (wg1,) = map(full, _run_exchange(_AllGather([t_bf(ffn1_w_gate)]), "gather_ffn1_gate"))

    tabs = _rope_tables(s)
    pband, ptband = _pool_tables(s)
    g1, gm, g2, gf = ffn1_norm, mix_norm, ffn2_norm, final_norm.reshape(1, d)

    x0 = x[0]
    n1, gate1, wu1 = _ffn_gate(x0, g1, wg1, "ffn1_gate", tm, carry=_AllGather([t_bf(ffn1_w_up)]))
    up1, act1, act1_t, wd1, win_t = _ffn_up(n1, gate1, full(wu1), "ffn1_up", tm,
                                            carry=_AllGather([ffn1_w_down[0].astype(BF16), t_bf(w_in)]))
    wu1, wd1, win_t = full(wu1), full(wd1), full(win_t)
    h1, q, k, v, pc, n2, wout, wg2 = _ffn_down_mix_in(
        x0, act1, wd1, gm, win_t, tabs, "ffn1_down_mix_in", tm,
        carry=_AllGather([w_out[0].astype(BF16), t_bf(ffn2_w_gate)]))
    wout, wg2 = full(wout), full(wg2)
    h2, mix_t, *gathered = _attn_pool_fwd(h1, wout, q, k, v, pc, sink_logits, pool_w[0], pool_scale, pband,
                                          "attn_pool_fwd",
                                          carry=_AllGather([t_bf(ffn2_w_up), ffn2_w_down[0].astype(BF16)]))
    wu2, wd2 = map(full, gathered)
    dh3, n3, gate2, up2, act2_t, loss_part, dgf = _ffn_fwd(h2, g2, wg2, wu2, wd2, "ffn2_fwd", tm,
                                                           head=(gf, loss_target[0]))

    sum1, recv2 = {}, {}

    def stage2(keys):
        return _RsStage2([sum1[key][1] for key in keys])

    dh2, dg2, dgate2_t, dup2_t, dhh3, dmix, dh2b = _ffn_bwd(h2, dh3, gate2, up2, g2, wg2, wu2, wd2, wout, "ffn2_bwd",
                                                            tm_bwd)
    sum1["g2"] = _wgrad_rs1(pos, dgate2_t, n3, "wgrad_gate2")
    sum1["u2"] = _wgrad_rs1(pos, dup2_t, n3, "wgrad_up2")
    sum1["d2"] = _wgrad_rs1(pos, act2_t, dhh3, "wgrad_down2")
    sum1["out"] = _wgrad_rs1(pos, mix_t, dh2b, "wgrad_out")
    dq, dkp, dvp, dpc, dsink, dpw, dps, *r2 = _attn_pool_bwd(
        q, k, v, pc, dmix, sink_logits, pool_w[0], pool_scale, pband, ptband, "attn_pool_bwd",
        carry=stage2(["g2", "u2", "d2"]))
    recv2.update(zip(["g2", "u2", "d2"], r2))
    dh1, du_t, dgm, recv2["out"] = _mix_in_bwd(h1, dh2, gm, win_t, dq, dkp, dvp, dpc, tabs, "mix_in_bwd", tm,
                                               carry=stage2(["out"]))
    sum1["in"] = _wgrad_rs1(pos, du_t, n2, "wgrad_in")
    small_part = _pack_small(jnp.zeros_like(dgm), dgm, dg2, dgf, dsink[:, 0], dpw, dps, loss_part)
    dgate1_t, dup1_t, dhh1, recv2["in"], small_all = _ffn_bwd_gates(
        dh1, gate1, up1, wd1, "ffn1_bwd_gates", tm, carry=_Both(stage2(["in"]), _AllGather([small_part])))
    sum1["g1"] = _wgrad_rs1(pos, dgate1_t, n1, "wgrad_gate1")
    *sum1["u1"], recv2["g1"] = _wgrad_rs1(pos, dup1_t, n1, "wgrad_up1", carry=stage2(["g1"]))
    *sum1["d1"], recv2["u1"] = _wgrad_rs1(pos, act1_t, dhh1, "wgrad_down1", carry=stage2(["u1"]))
    dx, dg1, recv2["d1"] = _ffn_bwd_input(x0, dh1, dgate1_t, dup1_t, g1, wg1, wu1, "ffn1_bwd_input", tm,
                                          carry=stage2(["d1"]))

    (dg1_all,) = _run_exchange(_DirectGather([dg1.reshape(8, 128)]), "gather_norm1_grad")
    pk = lambda a, b, c_, e, s_, pw_, psc: _pack_small(a, b, c_, e, s_[0], pw_[0], psc)
    small_w = pk(ffn1_norm, mix_norm, ffn2_norm, final_norm, sink_logits, pool_w, pool_scale)
    small_m = pk(m_ffn1_norm, m_mix_norm, m_ffn2_norm, m_final_norm, m_sink_logits, m_pool_w, m_pool_scale)
    small_v = pk(v_ffn1_norm, v_mix_norm, v_ffn2_norm, v_final_norm, v_sink_logits, v_pool_w, v_pool_scale)
    sg, sd, sm, sv = [_unpack_small(a)
                      for a in _adam_small(small_w, small_all, dg1_all, small_m, small_v, "adam_small")]

    big = {}
    keys = ["g1", "u1", "d1", "g2", "u2", "d2", "in", "out"]
    names = ["ffn1_w_gate", "ffn1_w_up", "ffn1_w_down", "ffn2_w_gate", "ffn2_w_up", "ffn2_w_down", "w_in", "w_out"]
    transposed = [True, True, False, True, True, False, True, False]
    ws = [ffn1_w_gate, ffn1_w_up, ffn1_w_down, ffn2_w_gate, ffn2_w_up, ffn2_w_down, w_in, w_out]
    ms = [m_ffn1_w_gate, m_ffn1_w_up, m_ffn1_w_down, m_ffn2_w_gate, m_ffn2_w_up, m_ffn2_w_down, m_w_in, m_w_out]
    vs = [v_ffn1_w_gate, v_ffn1_w_up, v_ffn1_w_down, v_ffn2_w_gate, v_ffn2_w_up, v_ffn2_w_down, v_w_in, v_w_out]
    for key, nm, tr, w, m, vv in zip(keys, names, transposed, ws, ms, vs):
        view = (lambda a: jnp.swapaxes(a, 1, 2)[0]) if tr else (lambda a: a[0])
        back = (lambda a: jnp.swapaxes(a[None], 1, 2)) if tr else (lambda a: a[None])
        res = _rs_sum2_adam(pos, sum1[key][0], recv2[key], view(w), view(m), view(vv), "adam_" + nm)
        big[nm] = tuple(back(a) for a in res)

    loss = sg["loss"]
    all_names = ["ffn1_norm", "ffn1_w_gate", "ffn1_w_up", "ffn1_w_down", "mix_norm", "w_in", "sink_logits", "pool_w",
                 "pool_scale", "w_out", "ffn2_norm", "ffn2_w_gate", "ffn2_w_up", "ffn2_w_down", "final_norm"]
    outs = [loss, dx[None]]
    for idx, src in enumerate((sg, sd, sm, sv)):
        for nm in all_names:
            outs.append(big[nm][idx] if nm in big else src[nm])
    return tuple(outs)
```

```python
import functools

import jax
import jax.numpy as jnp
import numpy as np
from jax import lax
from jax.experimental import pallas as pl
from jax.experimental.pallas import tpu as pltpu

F32 = jnp.float32
BF16 = jnp.bfloat16
MESH = pl.DeviceIdType.MESH
N_DEV = 8

EPS = 1e-6
HEAD_DIM = 64
N_HEADS = 8
N_KV = 2
GROUP = N_HEADS // N_KV
ATTN_W = N_HEADS * HEAD_DIM
KV_W = N_KV * HEAD_DIM
POOL_W = 512
POOL_G = 4
POOL_GW = POOL_W // POOL_G
POOL_WINDOWS = (2, 4, 8, 16)
BLK = 128
ROT = 16
ROPE_THETA = 500000.0
SCORE_SCALE = HEAD_DIM ** -0.5

ADAM_LR, ADAM_B1, ADAM_B2, ADAM_EPS, ADAM_WD, ADAM_STEP = 0.001, 0.9, 0.999, 1e-08, 0.01, 10

VMEM_LIMIT = 56 * 1024 * 1024


def _cparams(sem=None, **kw):
    if sem is not None:
        kw["dimension_semantics"] = sem
    return pltpu.CompilerParams(vmem_limit_bytes=VMEM_LIMIT, **kw)


def _whole(shape):
    nd = len(shape)
    return pl.BlockSpec(shape, lambda *_: (0,) * nd, pipeline_mode=pl.Buffered(1))


def _sigmoid(z):
    return 1.0 / (1.0 + jnp.exp(-z))


def _dot_nt(a, b):
    return lax.dot_general(a, b, (((1,), (1,)), ((), ())), preferred_element_type=F32)


def _dot_nn(a, b):
    return lax.dot_general(a, b, (((1,), (0,)), ((), ())), preferred_element_type=F32)


def _dot_tn(a, b):
    return lax.dot_general(a, b, (((0,), (0,)), ((), ())), preferred_element_type=F32)


def _rope_tables(s):
    inv_freq = ROPE_THETA ** (-np.arange(0, ROT, 2, dtype=np.float64) / ROT)
    ang = np.arange(s, dtype=np.float64)[:, None] * inv_freq[None, :]
    c = np.ones((s, HEAD_DIM)); sa = np.zeros((s, HEAD_DIM)); sb = np.zeros((s, HEAD_DIM))
    c[:, :8] = np.cos(ang); c[:, 8:16] = np.cos(ang)
    sa[:, :8] = -np.sin(ang)
    sb[:, 8:16] = np.sin(ang)
    t = lambda a: jnp.asarray(np.tile(a, (1, 2)).astype(np.float32))
    return t(c), t(sa), t(sb)


def _pool_weight(gi, t, s_pos, s):
    half = POOL_WINDOWS[gi] // 2

    def win(lo, hi):
        a = np.clip(lo, 0, s); b = np.clip(hi + 1, 0, s)
        inside = (s_pos >= a) & (s_pos < b)
        return inside / np.maximum(b - a, 1)

    w = 0.5 * (win(t - half, t + half - 1) + win(t - half + 1, t + half)) - (t == s_pos)
    return w * ((t >= 0) & (t < s) & (s_pos >= 0) & (s_pos < s))


def _pool_tables(s):
    nb = s // BLK
    fwd = np.zeros((3, POOL_G, BLK, 3 * BLK), np.float32)
    bwd = np.zeros((3, POOL_G, BLK, 3 * BLK), np.float32)
    for vi, n in enumerate((0, 1 if nb > 2 else 0, nb - 1)):
        i = n * BLK + np.arange(BLK)[:, None]
        j = (n - 1) * BLK + np.arange(3 * BLK)[None, :]
        for gi in range(POOL_G):
            fwd[vi, gi] = _pool_weight(gi, i, j, s)
            bwd[vi, gi] = _pool_weight(gi, j, i, s)
    return jnp.asarray(fwd, dtype=BF16), jnp.asarray(bwd, dtype=BF16)


def _variant_index(n, nb):
    return jnp.where(n == 0, 0, jnp.where(n == nb - 1, 2, 1))


class _Exchange:
    inputs = ()
    out_shapes = ()
    sems = ()

    def start(self, srcs, outs, sems):
        raise NotImplementedError

    def middle(self, srcs, outs, sems):
        pass

    def finish(self, srcs, outs, sems):
        raise NotImplementedError


class _AllGather(_Exchange):
    def __init__(self, arrays):
        n = len(arrays)
        self.inputs = list(arrays)
        self.out_shapes = [jax.ShapeDtypeStruct((N_DEV,) + a.shape, a.dtype) for a in arrays]
        self.sems = [pltpu.SemaphoreType.DMA((n, 8)), pltpu.SemaphoreType.DMA((n, 8)), pltpu.SemaphoreType.DMA((n,))]

    def _parts(self, srcs, outs, sems):
        send_sems, recv_sems, local_sems = sems
        n = len(srcs)
        x, y, c = lax.axis_index("x"), lax.axis_index("y"), lax.axis_index("c")
        me, sibling, xn, yn, diag = (x, y, c), (x, y, 1 - c), (1 - x, y, c), (x, 1 - y, c), (1 - x, 1 - y, c)

        def place(a, dev, half=None):
            block = outs[a].at[4 * dev[0] + 2 * dev[1] + dev[2]]
            if half is None:
                return block
            r2 = outs[a].shape[1] // 2
            return block.at[pl.ds(half * r2, r2)]

        def copy(a, k, dev, to, half=None, src=None):
            where = place(a, dev, half)
            return pltpu.make_async_remote_copy(
                src_ref=where if src is None else src, dst_ref=where, send_sem=send_sems.at[a, k],
                recv_sem=recv_sems.at[a, k], device_id=to, device_id_type=MESH)

        def other(dev):
            return (dev[0], dev[1], 1 - dev[2])

        class Parts:
            mine = staticmethod(lambda: [pltpu.make_async_copy(srcs[a], place(a, me), local_sems.at[a])
                                         for a in range(n)])
            own = staticmethod(lambda: [copy(a, k, me, to, src=srcs[a]) for a in range(n)
                                        for k, to in ((0, sibling), (1, xn), (2, yn))])
            relay = staticmethod(lambda a: [copy(a, 3, xn, yn, half=0), copy(a, 4, yn, xn, half=1),
                                            copy(a, 5, xn, sibling), copy(a, 6, yn, sibling)])
            last = staticmethod(lambda a: copy(a, 7, diag, sibling))
            from_x = staticmethod(lambda a: copy(a, 1, xn, me))
            from_y = staticmethod(lambda a: copy(a, 2, yn, me))
            diag_halves = staticmethod(lambda a: [copy(a, 3, diag, me, half=0), copy(a, 4, diag, me, half=1)])
            from_sibling = staticmethod(lambda a: [copy(a, 0, sibling, me), copy(a, 5, other(xn), me),
                                                   copy(a, 6, other(yn), me), copy(a, 7, other(diag), me)])

        return n, Parts

    def start(self, srcs, outs, sems):
        _, p = self._parts(srcs, outs, sems)
        for cp in p.mine() + p.own():
            cp.start()

    def middle(self, srcs, outs, sems):
        n, p = self._parts(srcs, outs, sems)
        for a in range(n):
            p.from_x(a).wait_recv()
            p.from_y(a).wait_recv()
            for cp in p.relay(a):
                cp.start()

    def finish(self, srcs, outs, sems):
        n, p = self._parts(srcs, outs, sems)
        for a in range(n):
            for cp in p.diag_halves(a):
                cp.wait_recv()
            p.last(a).start()
        for a in range(n):
            for cp in p.from_sibling(a):
                cp.wait_recv()
        for cp in p.own() + [cp for a in range(n) for cp in p.relay(a) + [p.last(a)]]:
            cp.wait_send()
        for cp in p.mine():
            cp.wait()


class _RsStage2(_Exchange):
    def start(self, srcs, outs, sems):
        for cp in self._copies(srcs, outs, sems):
            cp.start()

    def finish(self, srcs, outs, sems):
        copies = self._copies(srcs, outs, sems)
        for cp in copies:
            cp.wait_recv()
        for cp in copies:
            cp.wait_send()


    def __init__(self, pbs):
        n = len(pbs)
        self.inputs = list(pbs)
        self.out_shapes = [jax.ShapeDtypeStruct((3,) + p.shape[1:], p.dtype) for p in pbs]
        self.sems = [pltpu.SemaphoreType.DMA((n, 3)), pltpu.SemaphoreType.DMA((n, 3))]

    def _copies(self, srcs, outs, sems):
        send_sems, recv_sems = sems
        x, y, c = lax.axis_index("x"), lax.axis_index("y"), lax.axis_index("c")
        chips = [(1 - x, y), (x, 1 - y), (1 - x, 1 - y)]
        return [pltpu.make_async_remote_copy(
            src_ref=srcs[a].at[2 * chip[0] + chip[1]], dst_ref=outs[a].at[j], send_sem=send_sems.at[a, j],
            recv_sem=recv_sems.at[a, j], device_id=(*chip, c), device_id_type=MESH)
            for a in range(len(srcs)) for j, chip in enumerate(chips)]


class _DirectGather(_Exchange):
    def __init__(self, arrays):
        n = len(arrays)
        self.inputs = list(arrays)
        self.out_shapes = [jax.ShapeDtypeStruct((N_DEV,) + a.shape, a.dtype) for a in arrays]
        self.sems = [pltpu.SemaphoreType.DMA((n, 7)), pltpu.SemaphoreType.DMA((n, 7)), pltpu.SemaphoreType.DMA((n,))]

    def _copies(self, srcs, outs, sems):
        send_sems, recv_sems, local_sems = sems
        x, y, c = lax.axis_index("x"), lax.axis_index("y"), lax.axis_index("c")
        me = 4 * x + 2 * y + c
        remote, local = [], []
        for a in range(len(srcs)):
            local.append(pltpu.make_async_copy(srcs[a], outs[a].at[me], local_sems.at[a]))
            for k in range(1, N_DEV):
                peer = (x ^ (k >> 2), y ^ ((k >> 1) & 1), c ^ (k & 1))
                remote.append(pltpu.make_async_remote_copy(
                    src_ref=srcs[a], dst_ref=outs[a].at[me], send_sem=send_sems.at[a, k - 1],
                    recv_sem=recv_sems.at[a, k - 1], device_id=peer, device_id_type=MESH))
        return remote, local

    def start(self, srcs, outs, sems):
        remote, local = self._copies(srcs, outs, sems)
        for cp in local + remote:
            cp.start()

    def finish(self, srcs, outs, sems):
        remote, local = self._copies(srcs, outs, sems)
        for cp in remote:
            cp.wait_recv()
        for cp in remote:
            cp.wait_send()
        for cp in local:
            cp.wait()


class _Both(_Exchange):
    def __init__(self, a, b):
        self.a, self.b = a, b
        self.inputs = list(a.inputs) + list(b.inputs)
        self.out_shapes = list(a.out_shapes) + list(b.out_shapes)
        self.sems = list(a.sems) + list(b.sems)

    def _split(self, srcs, outs, sems):
        na, oa, sa = len(self.a.inputs), len(self.a.out_shapes), len(self.a.sems)
        return (srcs[:na], outs[:oa], sems[:sa]), (srcs[na:], outs[oa:], sems[sa:])

    def start(self, srcs, outs, sems):
        pa, pb = self._split(srcs, outs, sems)
        self.a.start(*pa)
        self.b.start(*pb)

    def middle(self, srcs, outs, sems):
        pa, pb = self._split(srcs, outs, sems)
        self.a.middle(*pa)
        self.b.middle(*pb)

    def finish(self, srcs, outs, sems):
        pa, pb = self._split(srcs, outs, sems)
        self.a.finish(*pa)
        self.b.finish(*pb)


_ANY = pl.BlockSpec(memory_space=pl.ANY)


def _run_exchange(ex, name):
    n_in, n_out = len(ex.inputs), len(ex.out_shapes)

    def body(*refs):
        srcs, outs, sems = refs[:n_in], refs[n_in:n_in + n_out], refs[n_in + n_out:]
        ex.start(srcs, outs, sems)
        ex.middle(srcs, outs, sems)
        ex.finish(srcs, outs, sems)

    return pl.pallas_call(
        body, name=name, out_shape=list(ex.out_shapes), in_specs=[_ANY] * n_in, out_specs=[_ANY] * n_out,
        scratch_shapes=list(ex.sems),
    )(*ex.inputs)


CARRY_MIDDLE = 0.7


def _call(body, name, grid, in_specs, out_specs, out_shape, args, sem, carry=None, middle=CARRY_MIDDLE, scratch=()):
    if carry is None:
        return pl.pallas_call(functools.partial(body), name=name, grid=grid, in_specs=in_specs, out_specs=out_specs,
                              out_shape=out_shape, scratch_shapes=list(scratch), compiler_params=_cparams(sem))(*args)
    n_in, n_out = len(in_specs), len(out_specs)
    nc_in, nc_out = len(carry.inputs), len(carry.out_shapes)

    def carried(*refs):
        ins = refs[:n_in]
        c_in = refs[n_in:n_in + nc_in]
        outs = refs[n_in + nc_in:n_in + nc_in + n_out]
        c_out = refs[n_in + nc_in + n_out:n_in + nc_in + n_out + nc_out]
        own = refs[n_in + nc_in + n_out + nc_out:n_in + nc_in + n_out + nc_out + len(scratch)]
        sems = refs[n_in + nc_in + n_out + nc_out + len(scratch):]
        ids = [pl.program_id(i) for i in range(len(grid))]
        is_first = functools.reduce(jnp.logical_and, [i == 0 for i in ids])
        is_last = functools.reduce(jnp.logical_and, [i == g - 1 for i, g in zip(ids, grid)])
        @pl.when(is_first)
        def _():
            carry.start(c_in, c_out, sems)

        if middle is not None:
            @pl.when(functools.reduce(jnp.logical_and, [ids[0] == round(middle * (grid[0] - 1))]
                                      + [i == 0 for i in ids[1:]]))
            def _():
                carry.middle(c_in, c_out, sems)

        body(*ins, *outs, *own)

        @pl.when(is_last)
        def _():
            if middle is None:
                carry.middle(c_in, c_out, sems)
            carry.finish(c_in, c_out, sems)

    return pl.pallas_call(
        carried, name=name, grid=grid, in_specs=list(in_specs) + [_ANY] * nc_in,
        out_specs=list(out_specs) + [_ANY] * nc_out, out_shape=list(out_shape) + list(carry.out_shapes),
        scratch_shapes=list(scratch) + list(carry.sems), compiler_params=_cparams(sem))(*args, *carry.inputs)


def _rs_sum2_adam(pos, p, r2, w, m, v, name):
    _, rows, d = p.shape
    tr = rows // 2 if rows % 16 == 0 else rows

    def body(pos_ref, p_ref, r_ref, w_ref, m_ref, v_ref, g_ref, d_ref, nm_ref, nv_ref):
        r = r_ref[...].astype(F32)
        g = ((p_ref[...] + r[0]) + r[1]) + r[2]
        g_ref[...] = g
        d_ref[...], nm_ref[...], nv_ref[...] = _adam_math(w_ref[...], g, m_ref[...], v_ref[...])

    blk = pl.BlockSpec((tr, d), lambda i, pos_ref: (i, 0))
    return pl.pallas_call(
        body, name=name,
        grid_spec=pltpu.PrefetchScalarGridSpec(
            num_scalar_prefetch=1, grid=(rows // tr,),
            in_specs=[pl.BlockSpec((None, tr, d), lambda i, pos_ref: (2 * pos_ref[0] + pos_ref[1], i, 0)),
                      pl.BlockSpec((3, tr, d), lambda i, pos_ref: (0, i, 0)), blk, blk, blk],
            out_specs=[blk] * 4),
        out_shape=[jax.ShapeDtypeStruct((rows, d), F32)] * 4,
        compiler_params=_cparams(("arbitrary",)),
    )(pos, p, r2, w, m, v)


def _ffn_chunk(f):
    for cand in (256, 128):
        if f % cand == 0:
            return cand
    return f


def _loss_head(x, gg, target, loss_ref, dg_ref):
    @pl.when(pl.program_id(0) == 0)
    def _():
        loss_ref[...] = jnp.zeros_like(loss_ref)
        dg_ref[...] = jnp.zeros_like(dg_ref)

    r = lax.rsqrt(jnp.mean(x * x, axis=-1, keepdims=True) + EPS)
    xhat = x * r
    e = xhat * gg - target
    loss_ref[...] += 0.5 * jnp.sum(jnp.mean(e * e, axis=-1, keepdims=True), axis=0, keepdims=True)
    dy = e * (1.0 / x.shape[-1])
    dg_ref[...] += jnp.sum(dy * xhat, axis=0, keepdims=True)
    dxhat = dy * gg
    return r * (dxhat - xhat * jnp.mean(dxhat * xhat, axis=-1, keepdims=True))


def _ffn_gate(h, g, wg_t, name, tm, carry=None):
    s, d = h.shape
    f = wg_t.shape[0]

    def body(h_ref, g_ref, wg_ref, n_ref, gate_ref):
        x = h_ref[...]
        r = lax.rsqrt(jnp.mean(x * x, axis=-1, keepdims=True) + EPS)
        nb = (x * r * g_ref[...]).astype(BF16)
        n_ref[...] = nb
        gate_ref[...] = _dot_nt(nb, wg_ref[...]).astype(BF16)

    row = lambda w: pl.BlockSpec((tm, w), lambda i: (i, 0))
    return _call(body, name, (s // tm,), [row(d), _whole((1, d)), _whole((f, d))], [row(d), row(f)],
                 [jax.ShapeDtypeStruct((s, d), BF16), jax.ShapeDtypeStruct((s, f), BF16)], (h, g, wg_t),
                 ("arbitrary",), carry)


def _ffn_up(n, gate, wu_t, name, tm, carry=None):
    s, d = n.shape
    f = wu_t.shape[0]
    tf = _ffn_chunk(f)

    def body(n_ref, gate_ref, wu_ref, up_ref, act_ref, act_t_ref):
        nb = n_ref[...]
        for j in range(f // tf):
            sl = slice(j * tf, (j + 1) * tf)
            up = _dot_nt(nb, wu_ref[sl, :])
            gate = gate_ref[:, sl].astype(F32)
            up_ref[:, sl] = up.astype(BF16)
            act = (gate * _sigmoid(gate) * up).astype(BF16)
            act_ref[:, sl] = act
            act_t_ref[sl, :] = act.T

    row = lambda w: pl.BlockSpec((tm, w), lambda i: (i, 0))
    return _call(body, name, (s // tm,), [row(d), row(f), _whole((f, d))],
                 [row(f), row(f), pl.BlockSpec((f, tm), lambda i: (0, i))],
                 [jax.ShapeDtypeStruct((s, f), BF16)] * 2 + [jax.ShapeDtypeStruct((f, s), BF16)], (n, gate, wu_t),
                 ("arbitrary",), carry)


def _ffn_fwd(h, g, wg_t, wu_t, wd, name, tm, carry=None, head=None, middle=CARRY_MIDDLE):
    s, d = h.shape
    f = wg_t.shape[0]
    tf = _ffn_chunk(f)

    def body(h_ref, g_ref, wg_ref, wu_ref, wd_ref, *refs):
        if head is None:
            o_ref, n_ref, gate_ref, up_ref, act_t_ref, act_ref = refs
        else:
            gf_ref, t_ref, o_ref, n_ref, gate_ref, up_ref, act_t_ref, loss_ref, dgf_ref, act_ref = refs
        x = h_ref[...]
        r = lax.rsqrt(jnp.mean(x * x, axis=-1, keepdims=True) + EPS)
        nb = (x * r * g_ref[...]).astype(BF16)
        n_ref[...] = nb
        for j in range(f // tf):
            sl = slice(j * tf, (j + 1) * tf)
            gate = _dot_nt(nb, wg_ref[sl, :])
            up = _dot_nt(nb, wu_ref[sl, :])
            gate_ref[:, sl] = gate.astype(BF16)
            up_ref[:, sl] = up.astype(BF16)
            act = gate * _sigmoid(gate) * up
            act = act.astype(BF16)
            act_ref[:, sl] = act
            act_t_ref[sl, :] = act.T
        h_out = x + 0.5 * _dot_nn(act_ref[...], wd_ref[...])
        o_ref[...] = h_out if head is None else _loss_head(h_out, gf_ref[...], t_ref[...], loss_ref, dgf_ref)

    row = lambda w: pl.BlockSpec((tm, w), lambda i: (i, 0))
    in_specs = [row(d), _whole((1, d)), _whole((f, d)), _whole((f, d)), _whole((f, d))]
    out_specs = [row(d), row(d), row(f), row(f), pl.BlockSpec((f, tm), lambda i: (0, i))]
    out_shape = ([jax.ShapeDtypeStruct((s, d), F32), jax.ShapeDtypeStruct((s, d), BF16)]
                 + [jax.ShapeDtypeStruct((s, f), BF16)] * 2 + [jax.ShapeDtypeStruct((f, s), BF16)])
    args = (h, g, wg_t, wu_t, wd)
    scratch = (pltpu.VMEM((tm, f), BF16),)
    if head is not None:
        in_specs += [_whole((1, d)), row(d)]
        out_specs += [pl.BlockSpec((1, 1), lambda i: (0, 0)), pl.BlockSpec((1, d), lambda i: (0, 0))]
        out_shape += [jax.ShapeDtypeStruct((1, 1), F32), jax.ShapeDtypeStruct((1, d), F32)]
        args += tuple(head)
    return _call(body, name, (s // tm,), in_specs, out_specs, out_shape, args, ("arbitrary",), carry, middle, scratch)


def _gate_grads(dh_ref, gate_ref, up_ref, wd_ref, dgate_ref, dup_ref, dgate_t_ref, dup_t_ref, dhh_ref, tf):
    dhh = (0.5 * dh_ref[...]).astype(BF16)
    dhh_ref[...] = dhh
    for j in range(gate_ref.shape[1] // tf):
        sl = slice(j * tf, (j + 1) * tf)
        gt = gate_ref[:, sl].astype(F32)
        u = up_ref[:, sl].astype(F32)
        dact = _dot_nt(dhh, wd_ref[sl, :])
        sg = _sigmoid(gt)
        dup = dact * (gt * sg)
        dgate = dact * u * (sg * (1.0 + gt * (1.0 - sg)))
        dup, dgate = dup.astype(BF16), dgate.astype(BF16)
        if dup_ref is not None:
            dup_ref[:, sl] = dup
            dgate_ref[:, sl] = dgate
        dup_t_ref[sl, :] = dup.T
        dgate_t_ref[sl, :] = dgate.T


def _input_grad(h_ref, dh_ref, dgate_ref, dup_ref, g_ref, wg_ref, wu_ref, o_ref, dg_ref, transposed=False):
    x = h_ref[...]
    r = lax.rsqrt(jnp.mean(x * x, axis=-1, keepdims=True) + EPS)
    xhat = x * r
    dot = _dot_tn if transposed else _dot_nn
    dn = dot(dgate_ref[...], wg_ref[...]) + dot(dup_ref[...], wu_ref[...])
    dxhat = dn * g_ref[...]
    o_ref[...] = dh_ref[...] + r * (dxhat - xhat * jnp.mean(dxhat * xhat, axis=-1, keepdims=True))

    @pl.when(pl.program_id(0) == 0)
    def _():
        dg_ref[...] = jnp.zeros_like(dg_ref)

    dg_ref[...] += jnp.sum(dn * xhat, axis=0, keepdims=True)


def _ffn_bwd(h_in, dh_out, gate, up, g, wg_t, wu_t, wd, w_out, name, tm):
    s, d = h_in.shape
    f = gate.shape[1]
    w = w_out.shape[0]
    tf = _ffn_chunk(f)

    def body(h_ref, dh_ref, gate_ref, up_ref, g_ref, wg_ref, wu_ref, wd_ref, wo_ref,
             o_ref, dg_ref, dgate_t_ref, dup_t_ref, dhh_ref, dmix_ref, dhb_ref, dgate_ref, dup_ref):
        _gate_grads(dh_ref, gate_ref, up_ref, wd_ref, dgate_ref, dup_ref, dgate_t_ref, dup_t_ref, dhh_ref, tf)
        _input_grad(h_ref, dh_ref, dgate_ref, dup_ref, g_ref, wg_ref, wu_ref, o_ref, dg_ref)
        dhb = o_ref[...].astype(BF16)
        dhb_ref[...] = dhb
        dmix_ref[...] = _dot_nt(dhb, wo_ref[...])

    row = lambda c: pl.BlockSpec((tm, c), lambda i: (i, 0))
    col = pl.BlockSpec((f, tm), lambda i: (0, i))
    return pl.pallas_call(
        body, name=name, grid=(s // tm,),
        in_specs=[row(d), row(d), row(f), row(f), _whole((1, d)), _whole((f, d)), _whole((f, d)), _whole((f, d)),
                  _whole((w, d))],
        out_specs=[row(d), pl.BlockSpec((1, d), lambda i: (0, 0)), col, col, row(d), row(w), row(d)],
        out_shape=[jax.ShapeDtypeStruct((s, d), F32), jax.ShapeDtypeStruct((1, d), F32),
                   jax.ShapeDtypeStruct((f, s), BF16), jax.ShapeDtypeStruct((f, s), BF16),
                   jax.ShapeDtypeStruct((s, d), BF16), jax.ShapeDtypeStruct((s, w), F32),
                   jax.ShapeDtypeStruct((s, d), BF16)],
        scratch_shapes=[pltpu.VMEM((tm, f), BF16), pltpu.VMEM((tm, f), BF16)],
        compiler_params=_cparams(("arbitrary",)),
    )(h_in, dh_out, gate, up, g, wg_t, wu_t, wd, w_out)


def _ffn_bwd_gates(dh_out, gate, up, wd, name, tm, carry=None):
    s, d = dh_out.shape
    f = gate.shape[1]
    tf = _ffn_chunk(f)

    def body(dh_ref, gate_ref, up_ref, wd_ref, dgate_t_ref, dup_t_ref, dhh_ref):
        _gate_grads(dh_ref, gate_ref, up_ref, wd_ref, None, None, dgate_t_ref, dup_t_ref, dhh_ref, tf)

    row = lambda w: pl.BlockSpec((tm, w), lambda i: (i, 0))
    col = pl.BlockSpec((f, tm), lambda i: (0, i))
    return _call(
        body, name, (s // tm,), [row(d), row(f), row(f), _whole((f, d))], [col, col, row(d)],
        [jax.ShapeDtypeStruct((f, s), BF16)] * 2 + [jax.ShapeDtypeStruct((s, d), BF16)],
        (dh_out, gate, up, wd), ("arbitrary",), carry)


def _ffn_bwd_input(h_in, dh_out, dgate_t, dup_t, g, wg_t, wu_t, name, tm, carry=None):
    s, d = h_in.shape
    f = dgate_t.shape[0]

    row = lambda w: pl.BlockSpec((tm, w), lambda i: (i, 0))
    col = pl.BlockSpec((f, tm), lambda i: (0, i))
    return _call(
        functools.partial(_input_grad, transposed=True), name, (s // tm,),
        [row(d), row(d), col, col, _whole((1, d)), _whole((f, d)), _whole((f, d))],
        [row(d), pl.BlockSpec((1, d), lambda i: (0, 0))],
        [jax.ShapeDtypeStruct((s, d), F32), jax.ShapeDtypeStruct((1, d), F32)],
        (h_in, dh_out, dgate_t, dup_t, g, wg_t, wu_t), ("arbitrary",), carry)


def _wgrad_rs1(pos, a_ts, b, name, carry=None):
    n_a = len(a_ts)
    f, s = a_ts[0].shape
    d = b.shape[1]
    fk = f // N_DEV
    nc_in = 0 if carry is None else len(carry.inputs)
    nc_out = 0 if carry is None else len(carry.out_shapes)

    def body(pos_ref, *refs):
        a_refs, b_ref, refs = refs[:n_a], refs[n_a], refs[n_a + 1:]
        c_in = refs[:nc_in]
        outs = refs[nc_in:nc_in + 2 * n_a]
        c_out = refs[nc_in + 2 * n_a:nc_in + 2 * n_a + nc_out]
        stage, land, send_sems, recv_sems = refs[nc_in + 2 * n_a + nc_out:nc_in + 2 * n_a + nc_out + 4]
        c_sems = refs[nc_in + 2 * n_a + nc_out + 4:]
        t = pl.program_id(0)
        which, step = t // 8, t % 8
        slot = 4 * which + step % 4
        x, y, c = lax.axis_index("x"), lax.axis_index("y"), lax.axis_index("c")

        def push(k):
            return pltpu.make_async_remote_copy(src_ref=stage.at[k], dst_ref=land.at[k], send_sem=send_sems.at[k],
                                                recv_sem=recv_sems.at[k], device_id=(x, y, 1 - c), device_id_type=MESH)

        if carry is not None:
            @pl.when(t == 0)
            def _():
                carry.start(c_in, c_out, c_sems)

        for j in range(n_a):
            @pl.when(which == j)
            def _(j=j):
                g = _dot_nn(a_refs[j][...], b_ref[...])

                @pl.when(step < 4)
                def _():
                    stage[slot] = g.astype(BF16)
                    push(slot).start()

                @pl.when(step >= 4)
                def _():
                    push(slot).wait_recv()
                    p = g + land[slot].astype(F32)
                    outs[2 * j][...] = p
                    outs[2 * j + 1][...] = p.astype(BF16)

        @pl.when(t == 8 * n_a - 1)
        def _():
            for k in range(4 * n_a):
                push(k).wait_send()
            if carry is not None:
                carry.middle(c_in, c_out, c_sems)
                carry.finish(c_in, c_out, c_sems)

    def shard(j):
        def index_map(t, pos_ref):
            step = jnp.clip(t - 8 * j, 0, 7)
            return 4 * ((step % 4) // 2) + 2 * (step % 2) + jnp.where(step < 4, 1 - pos_ref[2], pos_ref[2]), 0
        return index_map

    out = lambda j: pl.BlockSpec((None, fk, d), lambda t, pos_ref, j=j: (jnp.clip(t - 8 * j - 4, 0, 3), 0, 0))
    return pl.pallas_call(
        body, name=name,
        grid_spec=pltpu.PrefetchScalarGridSpec(
            num_scalar_prefetch=1, grid=(8 * n_a,),
            in_specs=[pl.BlockSpec((fk, s), shard(j)) for j in range(n_a)]
            + [pl.BlockSpec((s, d), lambda t, pos_ref: (0, 0), pipeline_mode=pl.Buffered(1))] + [_ANY] * nc_in,
            out_specs=[out(j) for j in range(n_a) for _ in range(2)] + [_ANY] * nc_out,
            scratch_shapes=[pltpu.VMEM((4 * n_a, fk, d), BF16), pltpu.VMEM((4 * n_a, fk, d), BF16),
                            pltpu.SemaphoreType.DMA((4 * n_a,)), pltpu.SemaphoreType.DMA((4 * n_a,))]
            + ([] if carry is None else list(carry.sems))),
        out_shape=[jax.ShapeDtypeStruct((4, fk, d), dt) for _ in range(n_a) for dt in (F32, BF16)]
        + ([] if carry is None else list(carry.out_shapes)),
        compiler_params=_cparams(("arbitrary",)),
    )(pos, *a_ts, b, *([] if carry is None else carry.inputs))


def _rope(t, c, sa, sb, reps):
    c, sa, sb = (jnp.tile(v, (1, reps)) if reps > 1 else v for v in (c, sa, sb))
    w = t.shape[1]
    return t * c + pltpu.roll(t, w - 8, 1) * sa + pltpu.roll(t, 8, 1) * sb


def _rope_bwd(dt, c, sa, sb, reps):
    c, sa, sb = (jnp.tile(v, (1, reps)) if reps > 1 else v for v in (c, sa, sb))
    w = dt.shape[1]
    return dt * c + pltpu.roll(dt * sa, 8, 1) + pltpu.roll(dt * sb, w - 8, 1)


def _ffn_down_mix_in(x, act, wd, g, win_t, tabs, name, tm, carry=None):
    s, d = x.shape
    f = wd.shape[0]
    n_in = win_t.shape[0]

    def body(x_ref, act_ref, wd_ref, g_ref, w_ref, c_ref, sa_ref, sb_ref, h_ref, q_ref, k_ref, v_ref, pc_ref, n_ref):
        x = x_ref[...] + 0.5 * _dot_nn(act_ref[...], wd_ref[...])
        h_ref[...] = x
        r = lax.rsqrt(jnp.mean(x * x, axis=-1, keepdims=True) + EPS)
        nb = (x * r * g_ref[...]).astype(BF16)
        n_ref[...] = nb
        u = _dot_nt(nb, w_ref[...])
        c, sa, sb = c_ref[...], sa_ref[...], sb_ref[...]
        q_ref[...] = _rope(u[:, :ATTN_W], c, sa, sb, ATTN_W // 128).astype(BF16)
        k_ref[...] = _rope(u[:, ATTN_W:ATTN_W + KV_W], c, sa, sb, 1).astype(BF16)
        v_ref[...] = u[:, ATTN_W + KV_W:ATTN_W + 2 * KV_W].astype(BF16)
        pc_ref[...] = u[:, ATTN_W + 2 * KV_W:]

    row = lambda w: pl.BlockSpec((tm, w), lambda i: (i, 0))
    return _call(
        body, name, (s // tm,),
        [row(d), row(f), _whole((f, d)), _whole((1, d)), _whole((n_in, d)), row(128), row(128), row(128)],
        [row(d), row(ATTN_W), row(KV_W), row(KV_W), row(POOL_W), row(d)],
        [jax.ShapeDtypeStruct((s, d), F32), jax.ShapeDtypeStruct((s, ATTN_W), BF16),
         jax.ShapeDtypeStruct((s, KV_W), BF16), jax.ShapeDtypeStruct((s, KV_W), BF16),
         jax.ShapeDtypeStruct((s, POOL_W), F32), jax.ShapeDtypeStruct((s, d), BF16)],
        (x, act, wd, g, win_t, *tabs), ("arbitrary",), carry)


def _band_mask(n, nb, transposed):
    shape = (3 * BLK, 2 * BLK) if transposed else (2 * BLK, 3 * BLK)
    i = lax.broadcasted_iota(jnp.int32, shape, 1 if transposed else 0) % BLK
    j = lax.broadcasted_iota(jnp.int32, shape, 0 if transposed else 1)
    kpos = (n - 1) * BLK + j
    return (j >= i) & (j <= i + 2 * BLK) & (kpos >= 0) & (kpos < nb * BLK)


def _block_diag(t, kh):
    tf = t.astype(F32)
    tr = pltpu.roll(tf, HEAD_DIM, 1)
    lo = lax.broadcasted_iota(jnp.int32, tf.shape, 1) < HEAD_DIM
    top, bot = (tf, tr) if kh == 0 else (tr, tf)
    return jnp.concatenate([jnp.where(lo, top, 0.0), jnp.where(lo, 0.0, bot)], axis=0).astype(BF16)


def _fold_diag(tbd):
    lo = lax.broadcasted_iota(jnp.int32, (3 * BLK, 2 * HEAD_DIM), 1) < HEAD_DIM
    t = jnp.where(lo, tbd[:3 * BLK], tbd[3 * BLK:])
    return t + pltpu.roll(t, HEAD_DIM, 1)


def _stack_pairs(x, kh):
    return jnp.concatenate([x[:, (2 * kh) * 128:(2 * kh + 1) * 128], x[:, (2 * kh + 1) * 128:(2 * kh + 2) * 128]], axis=0)


def _sink_of(sink_ref, kh, half, axis):
    shape = (2 * BLK, 1) if axis == 0 else (1, 2 * BLK)
    first = lax.broadcasted_iota(jnp.int32, shape, axis) < BLK
    return jnp.where(first, sink_ref[0, GROUP * kh + half], sink_ref[0, GROUP * kh + 2 + half])


def _softmax_sink(sc, valid, sink, axis):
    sc = jnp.where(valid, sc, -1e30)
    m = jnp.maximum(jnp.max(sc, axis=axis, keepdims=True), sink)
    e = jnp.exp(sc - m)
    es = jnp.exp(sink - m)
    inv = 1.0 / (jnp.sum(e, axis=axis, keepdims=True) + es)
    return e * inv, es * inv


def _attn_blocks_per_step(nb):
    return next(nq for nq in (4, 2, 1) if nb % nq == 0)


def _band_specs(nq, nb, w, col=0):
    return [pl.BlockSpec((BLK, w), lambda m: (jnp.maximum(nq * m - 1, 0), col)),
            pl.BlockSpec((nq * BLK, w), lambda m: (m, col)),
            pl.BlockSpec((BLK, w), lambda m: (jnp.minimum(nq * m + nq, nb - 1), col))]


def _attn_pool_fwd(h, w_out, q, k, v, pc, sink, pool_w, pool_scale, pband, name, carry=None, middle=CARRY_MIDDLE):
    s, d = h.shape
    nb = s // BLK
    nq = _attn_blocks_per_step(nb)

    def body(sink_ref, q_ref, k0, k1, k2, v0, v1, v2, p0, p1, p2, pw_ref, ps_ref, pb_ref, h_ref, wo_ref,
             h_out_ref, o_t_ref, o_ref):
        kall = jnp.concatenate([k0[...], k1[...], k2[...]], axis=0)
        vall = jnp.concatenate([v0[...], v1[...], v2[...]], axis=0)
        pall = jnp.concatenate([p0[...], p1[...], p2[...]], axis=0).astype(BF16)
        qall = q_ref[...] * SCORE_SCALE
        for j in range(nq):
            n = pl.program_id(0) * nq + j
            rows, band = slice(j * BLK, (j + 1) * BLK), slice(j * BLK, (j + 3) * BLK)
            valid = _band_mask(n, nb, False)
            kb, vb, qs = kall[band], vall[band], qall[rows]
            for kh in range(N_KV):
                sc = _dot_nt(_stack_pairs(qs, kh), _block_diag(kb, kh))
                p = [_softmax_sink(sc[:, half * 3 * BLK:(half + 1) * 3 * BLK], valid,
                                   _sink_of(sink_ref, kh, half, 0), 1)[0] for half in range(2)]
                o2 = _dot_nn(jnp.concatenate(p, axis=1).astype(BF16), _block_diag(vb, kh)).astype(BF16)
                o_ref[rows, (2 * kh) * 128:(2 * kh + 1) * 128] = o2[:BLK]
                o_ref[rows, (2 * kh + 1) * 128:(2 * kh + 2) * 128] = o2[BLK:]
            ext = pall[band]
            var = _variant_index(n, nb)
            for gi in range(POOL_G):
                gsl = slice(gi * POOL_GW, (gi + 1) * POOL_GW)
                dg = _dot_nn(pb_ref[var, gi], ext[:, gsl])
                yg = _dot_nn(dg.astype(BF16), pw_ref[gi].astype(BF16))
                o_ref[rows, ATTN_W + gi * POOL_GW:ATTN_W + (gi + 1) * POOL_GW] = (yg * ps_ref[:, gsl]).astype(BF16)
        mix = o_ref[...]
        o_t_ref[...] = mix.T
        h_out_ref[...] = h_ref[...] + _dot_nn(mix, wo_ref[...])

    mix_w = ATTN_W + POOL_W
    return _call(
        body, name, (nb // nq,),
        [pl.BlockSpec(memory_space=pltpu.SMEM), pl.BlockSpec((nq * BLK, ATTN_W), lambda m: (m, 0)),
         *_band_specs(nq, nb, KV_W), *_band_specs(nq, nb, KV_W), *_band_specs(nq, nb, POOL_W),
         _whole((POOL_G, POOL_GW, POOL_GW)), _whole((1, POOL_W)), _whole(pband.shape),
         pl.BlockSpec((nq * BLK, d), lambda m: (m, 0)), _whole((mix_w, d))],
        [pl.BlockSpec((nq * BLK, d), lambda m: (m, 0)), pl.BlockSpec((mix_w, nq * BLK), lambda m: (0, m))],
        [jax.ShapeDtypeStruct((s, d), F32), jax.ShapeDtypeStruct((mix_w, s), BF16)],
        (sink, q, k, k, k, v, v, v, pc, pc, pc, pool_w, pool_scale, pband, h, w_out), ("arbitrary",), carry, middle,
        scratch=(pltpu.VMEM((nq * BLK, mix_w), BF16),))


def _attn_pool_bwd(q, k, v, pc, dmix, sink, pool_w, pool_scale, pband, ptband, name, carry=None):
    s = q.shape[0]
    nb = s // BLK
    nq = _attn_blocks_per_step(nb)

    def body(sink_ref, q_ref, k0, k1, k2, v0, v1, v2, p0, p1, p2, da_ref, d0, d1, d2, pw_ref, ps_ref, pb_ref, ptb_ref,
             dq_ref, dk_ref, dv_ref, dpc_ref, dsink_ref, dpw_ref, dps_ref):
        @pl.when(pl.program_id(0) == 0)
        def _():
            dsink_ref[...] = jnp.zeros_like(dsink_ref)
            dpw_ref[...] = jnp.zeros_like(dpw_ref)
            dps_ref[...] = jnp.zeros_like(dps_ref)

        kall = jnp.concatenate([k0[...], k1[...], k2[...]], axis=0)
        vall = jnp.concatenate([v0[...], v1[...], v2[...]], axis=0)
        pall = jnp.concatenate([p0[...], p1[...], p2[...]], axis=0).astype(BF16)
        dpall = jnp.concatenate([d0[...], d1[...], d2[...]], axis=0)
        lo = lax.broadcasted_iota(jnp.int32, (3 * BLK, KV_W), 1) < HEAD_DIM
        for j in range(nq):
            n = pl.program_id(0) * nq + j
            rows, band = slice(j * BLK, (j + 1) * BLK), slice(j * BLK, (j + 3) * BLK)
            valid = _band_mask(n, nb, True)
            kb, vb, qb = kall[band], vall[band], q_ref[rows, :]
            qs = qb * SCORE_SCALE
            da = da_ref[rows, :].astype(BF16)
            dk_fold, dv_fold = [], []
            for kh in range(N_KV):
                kbd, vbd = _block_diag(kb, kh), _block_diag(vb, kh)
                q2, do2 = _stack_pairs(qb, kh), _stack_pairs(da, kh)
                sc_t = _dot_nt(kbd, _stack_pairs(qs, kh))
                dp_t = _dot_nt(vbd, do2)
                p_t, ds_t = [], []
                for half in range(2):
                    keys = slice(half * 3 * BLK, (half + 1) * 3 * BLK)
                    p, ps = _softmax_sink(sc_t[keys], valid, _sink_of(sink_ref, kh, half, 1), 0)
                    delta = jnp.sum(p * dp_t[keys], axis=0, keepdims=True)
                    p_t.append(p.astype(BF16))
                    ds_t.append((p * (dp_t[keys] - delta)).astype(BF16))
                    dsk = -ps * delta
                    for pair in range(2):
                        h = GROUP * kh + 2 * pair + half
                        part = jnp.sum(dsk[:, pair * BLK:(pair + 1) * BLK], axis=1, keepdims=True)
                        dsink_ref[h:h + 1, :] += jnp.broadcast_to(part, (1, 128))
                p_t = jnp.concatenate(p_t, axis=0)
                ds_t = jnp.concatenate(ds_t, axis=0)
                dq2 = _dot_tn(ds_t, kbd) * SCORE_SCALE
                dq_ref[rows, (2 * kh) * 128:(2 * kh + 1) * 128] = dq2[:BLK]
                dq_ref[rows, (2 * kh + 1) * 128:(2 * kh + 2) * 128] = dq2[BLK:]
                dk_fold.append(_fold_diag(_dot_nn(ds_t, q2)) * SCORE_SCALE)
                dv_fold.append(_fold_diag(_dot_nn(p_t, do2)))
            dk_all = jnp.where(lo, dk_fold[0], dk_fold[1])
            dv_all = jnp.where(lo, dv_fold[0], dv_fold[1])
            for t in range(3):
                dk_ref[j, t] = dk_all[t * BLK:(t + 1) * BLK]
                dv_ref[j, t] = dv_all[t * BLK:(t + 1) * BLK]
            ext, dpe = pall[band], dpall[band]
            dpc_cur = dpall[(j + 1) * BLK:(j + 2) * BLK]
            var = _variant_index(n, nb)
            for gi in range(POOL_G):
                gsl = slice(gi * POOL_GW, (gi + 1) * POOL_GW)
                wg = pw_ref[gi].astype(BF16)
                sc = ps_ref[:, gsl]
                dgb = _dot_nn(pb_ref[var, gi], ext[:, gsl]).astype(BF16)
                yg = _dot_nn(dgb, wg)
                dps_ref[:, gsl] += jnp.sum(dpc_cur[:, gsl] * yg, axis=0, keepdims=True)
                dpw_ref[gi] += _dot_tn(dgb, (dpc_cur[:, gsl] * sc).astype(BF16))
                dd = _dot_nt((dpe[:, gsl] * sc).astype(BF16), wg)
                dpc_ref[rows, gsl] = _dot_nn(ptb_ref[var, gi], dd.astype(BF16))

    fixed = lambda shape: pl.BlockSpec(shape, lambda m: (0,) * len(shape))
    return _call(
        body, name, (nb // nq,),
        [pl.BlockSpec(memory_space=pltpu.SMEM), pl.BlockSpec((nq * BLK, ATTN_W), lambda m: (m, 0)),
         *_band_specs(nq, nb, KV_W), *_band_specs(nq, nb, KV_W), *_band_specs(nq, nb, POOL_W),
         pl.BlockSpec((nq * BLK, ATTN_W), lambda m: (m, 0)), *_band_specs(nq, nb, POOL_W, 1),
         _whole((POOL_G, POOL_GW, POOL_GW)), _whole((1, POOL_W)), _whole(pband.shape), _whole(ptband.shape)],
        [pl.BlockSpec((nq * BLK, ATTN_W), lambda m: (m, 0)),
         pl.BlockSpec((nq, 3, BLK, KV_W), lambda m: (m, 0, 0, 0)),
         pl.BlockSpec((nq, 3, BLK, KV_W), lambda m: (m, 0, 0, 0)),
         pl.BlockSpec((nq * BLK, POOL_W), lambda m: (m, 0)),
         fixed((N_HEADS, 128)), fixed((POOL_G, POOL_GW, POOL_GW)), fixed((1, POOL_W))],
        [jax.ShapeDtypeStruct((s, ATTN_W), F32), jax.ShapeDtypeStruct((nb, 3, BLK, KV_W), F32),
         jax.ShapeDtypeStruct((nb, 3, BLK, KV_W), F32), jax.ShapeDtypeStruct((s, POOL_W), F32),
         jax.ShapeDtypeStruct((N_HEADS, 128), F32),
         jax.ShapeDtypeStruct((POOL_G, POOL_GW, POOL_GW), F32), jax.ShapeDtypeStruct((1, POOL_W), F32)],
        (sink, q, k, k, k, v, v, v, pc, pc, pc, dmix, dmix, dmix, dmix, pool_w, pool_scale, pband, ptband),
        ("arbitrary",), carry)


def _mix_in_bwd(h, dh, g, win_t, dq, dkp, dvp, dpc, tabs, name, tm, carry=None):
    s, d = h.shape
    nb = s // BLK
    nt = tm // BLK
    n_in = win_t.shape[0]

    def band_sum(n, before, own, after, prev_last, next_first):
        lo = (n > 0).astype(F32)
        hi = (n < s // tm - 1).astype(F32)
        blocks = []
        for b in range(nt):
            from_prev = prev_last[...] * lo if b == 0 else before[b - 1]
            from_next = next_first[...] * hi if b == nt - 1 else after[b + 1]
            blocks.append(from_prev + own[b] + from_next)
        return jnp.concatenate(blocks, axis=0)

    def body(h_ref, dh_ref, g_ref, w_ref, dq_ref, k2, k1, k0, kp, kn, v2, v1, v0, vp, vn, dpc_ref, c_ref, sa_ref,
             sb_ref, o_ref, du_ref, dg_ref):
        n = pl.program_id(0)
        dk = band_sum(n, k2, k1, k0, kp, kn)
        dv = band_sum(n, v2, v1, v0, vp, vn)
        c, sa, sb = c_ref[...], sa_ref[...], sb_ref[...]
        du = jnp.concatenate([_rope_bwd(dq_ref[...], c, sa, sb, ATTN_W // 128), _rope_bwd(dk, c, sa, sb, 1), dv,
                              dpc_ref[...]], axis=1)
        du_ref[...] = du.T.astype(BF16)
        dn = _dot_nn(du.astype(BF16), w_ref[...])
        x = h_ref[...]
        r = lax.rsqrt(jnp.mean(x * x, axis=-1, keepdims=True) + EPS)
        xhat = x * r
        dxhat = dn * g_ref[...]
        o_ref[...] = dh_ref[...] + r * (dxhat - xhat * jnp.mean(dxhat * xhat, axis=-1, keepdims=True))

        @pl.when(n == 0)
        def _():
            dg_ref[...] = jnp.zeros_like(dg_ref)

        dg_ref[...] += jnp.sum(dn * xhat, axis=0, keepdims=True)

    row = lambda w: pl.BlockSpec((tm, w), lambda n: (n, 0))
    slot = lambda t: pl.BlockSpec((nt, None, BLK, KV_W), lambda n, t=t: (n, t, 0, 0))
    parts = [slot(2), slot(1), slot(0),
             pl.BlockSpec((None, None, BLK, KV_W), lambda n: (jnp.maximum(nt * n - 1, 0), 2, 0, 0)),
             pl.BlockSpec((None, None, BLK, KV_W), lambda n: (jnp.minimum(nt * n + nt, nb - 1), 0, 0, 0))]
    return _call(
        body, name, (s // tm,),
        [row(d), row(d), _whole((1, d)), _whole((n_in, d)), row(ATTN_W), *parts, *parts, row(POOL_W),
         row(128), row(128), row(128)],
        [row(d), pl.BlockSpec((n_in, tm), lambda n: (0, n)), pl.BlockSpec((1, d), lambda n: (0, 0))],
        [jax.ShapeDtypeStruct((s, d), F32), jax.ShapeDtypeStruct((n_in, s), BF16), jax.ShapeDtypeStruct((1, d), F32)],
        (h, dh, g, win_t, dq, *[dkp] * 5, *[dvp] * 5, dpc, *tabs), ("arbitrary",), carry)


def _adam_math(w, g, m, v):
    m = ADAM_B1 * m + (1.0 - ADAM_B1) * g
    v = ADAM_B2 * v + (1.0 - ADAM_B2) * (g * g)
    m_hat = m / (1.0 - ADAM_B1 ** ADAM_STEP)
    v_hat = v / (1.0 - ADAM_B2 ** ADAM_STEP)
    delta = -ADAM_LR * (m_hat / (jnp.sqrt(v_hat) + ADAM_EPS) + ADAM_WD * w)
    return delta, m, v


def _adam_small(w, parts, late, m, v, name):
    rows, cols = w.shape

    def body(w_ref, p_ref, l_ref, m_ref, v_ref, g_ref, d_ref, nm_ref, nv_ref):
        g, gl = p_ref[0], l_ref[0]
        for k in range(1, N_DEV):
            g = g + p_ref[k]
            gl = gl + l_ref[k]
        g_ref[...] = g
        g_ref[SMALL_NORM1:SMALL_NORM1 + 8, :] = g[SMALL_NORM1:SMALL_NORM1 + 8] + gl
        d_ref[...], nm_ref[...], nv_ref[...] = _adam_math(w_ref[...], g_ref[...], m_ref[...], v_ref[...])

    return pl.pallas_call(
        body, name=name, out_shape=[jax.ShapeDtypeStruct((rows, cols), F32)] * 4,
    )(w, parts, late, m, v)


SMALL_NORM1 = 512


def _pack_small(norm1, normm, norm2, normf, sink, pool_w, pool_scale, loss=None):
    scale_rows = jnp.pad(pool_scale.reshape(4, 128), ((0, 4), (0, 0)))
    last_rows = jnp.pad(sink.reshape(1, N_HEADS), ((0, 7), (0, 128 - N_HEADS)))
    if loss is not None:
        last_rows = last_rows + jnp.pad(loss.reshape(1, 1), ((1, 6), (0, 127)))
    return jnp.concatenate([pool_w.reshape(512, 128), norm1.reshape(8, 128), normm.reshape(8, 128),
                            norm2.reshape(8, 128), normf.reshape(8, 128), scale_rows, last_rows], axis=0)


def _unpack_small(p):
    return dict(pool_w=p[:512].reshape(1, POOL_G, POOL_GW, POOL_GW), ffn1_norm=p[512:520].reshape(1, 1024),
                mix_norm=p[520:528].reshape(1, 1024), ffn2_norm=p[528:536].reshape(1, 1024),
                final_norm=p[536:544].reshape(1024), pool_scale=p[544:548].reshape(1, POOL_W),
                sink_logits=p[552, :N_HEADS].reshape(1, N_HEADS), loss=p[553, 0])


def kernel(x, ffn1_norm, ffn1_w_gate, ffn1_w_up, ffn1_w_down, mix_norm, w_in, sink_logits, pool_w, pool_scale, w_out, ffn2_norm, ffn2_w_gate, ffn2_w_up, ffn2_w_down, final_norm, loss_target, m_ffn1_norm, m_ffn1_w_gate, m_ffn1_w_up, m_ffn1_w_down, m_mix_norm, m_w_in, m_sink_logits, m_pool_w, m_pool_scale, m_w_out, m_ffn2_norm, m_ffn2_w_gate, m_ffn2_w_up, m_ffn2_w_down, m_final_norm, v_ffn1_norm, v_ffn1_w_gate, v_ffn1_w_up, v_ffn1_w_down, v_mix_norm, v_w_in, v_sink_logits, v_pool_w, v_pool_scale, v_w_out, v_ffn2_norm, v_ffn2_w_gate, v_ffn2_w_up, v_ffn2_w_down, v_final_norm):
    s, d = x.shape[1], x.shape[2]
    tm = min(512, s)
    tm_bwd = min(256, s)
    pos = jnp.stack([lax.axis_index("x"), lax.axis_index("y"), lax.axis_index("c")]).astype(jnp.int32)

    t_bf = lambda w: w[0].T.astype(BF16)
    full = lambda a: a.reshape(N_DEV * a.shape[1], d)
    (wg1,) = map(full, _run_exchange(_AllGather([t_bf(ffn1_w_gate)]), "gather_ffn1_gate"))

    tabs = _rope_tables(s)
    pband, ptband = _pool_tables(s)
    g1, gm, g2, gf = ffn1_norm, mix_norm, ffn2_norm, final_norm.reshape(1, d)

    x0 = x[0]
    n1, gate1, wu1 = _ffn_gate(x0, g1, wg1, "ffn1_gate", tm, carry=_AllGather([t_bf(ffn1_w_up)]))
    up1, act1, act1_t, wd1, win_t = _ffn_up(n1, gate1, full(wu1), "ffn1_up", tm,
                                            carry=_AllGather([ffn1_w_down[0].astype(BF16), t_bf(w_in)]))
    wu1, wd1, win_t = full(wu1), full(wd1), full(win_t)
    h1, q, k, v, pc, n2, wout, wg2 = _ffn_down_mix_in(
        x0, act1, wd1, gm, win_t, tabs, "ffn1_down_mix_in", tm,
        carry=_AllGather([w_out[0].astype(BF16), t_bf(ffn2_w_gate)]))
    wout, wg2 = full(wout), full(wg2)
    h2, mix_t, *gathered = _attn_pool_fwd(h1, wout, q, k, v, pc, sink_logits, pool_w[0], pool_scale, pband,
                                          "attn_pool_fwd",
                                          carry=_AllGather([t_bf(ffn2_w_up), ffn2_w_down[0].astype(BF16)]))
    wu2, wd2 = map(full, gathered)
    dh3, n3, gate2, up2, act2_t, loss_part, dgf = _ffn_fwd(h2, g2, wg2, wu2, wd2, "ffn2_fwd", tm,
                                                           head=(gf, loss_target[0]))

    sum1, recv2 = {}, {}

    def stage2(keys):
        return _RsStage2([sum1[key][1] for key in keys])

    dh2, dg2, dgate2_t, dup2_t, dhh3, dmix, dh2b = _ffn_bwd(h2, dh3, gate2, up2, g2, wg2, wu2, wd2, wout, "ffn2_bwd",
                                                            tm_bwd)
    both = _wgrad_rs1(pos, [dgate2_t, dup2_t], n3, "wgrad_gate_up2")
    sum1["g2"], sum1["u2"] = both[:2], both[2:]
    sum1["d2"] = _wgrad_rs1(pos, [act2_t], dhh3, "wgrad_down2")
    sum1["out"] = _wgrad_rs1(pos, [mix_t], dh2b, "wgrad_out")
    dq, dkp, dvp, dpc, dsink, dpw, dps, *r2 = _attn_pool_bwd(
        q, k, v, pc, dmix, sink_logits, pool_w[0], pool_scale, pband, ptband, "attn_pool_bwd",
        carry=stage2(["g2", "u2", "d2"]))
    recv2.update(zip(["g2", "u2", "d2"], r2))
    dh1, du_t, dgm, recv2["out"] = _mix_in_bwd(h1, dh2, gm, win_t, dq, dkp, dvp, dpc, tabs, "mix_in_bwd", tm,
                                               carry=stage2(["out"]))
    sum1["in"] = _wgrad_rs1(pos, [du_t], n2, "wgrad_in")
    small_part = _pack_small(jnp.zeros_like(dgm), dgm, dg2, dgf, dsink[:, 0], dpw, dps, loss_part)
    dgate1_t, dup1_t, dhh1, recv2["in"], small_all = _ffn_bwd_gates(
        dh1, gate1, up1, wd1, "ffn1_bwd_gates", tm, carry=_Both(stage2(["in"]), _AllGather([small_part])))
    sum1["g1"] = _wgrad_rs1(pos, [dgate1_t], n1, "wgrad_gate1")
    *sum1["u1"], recv2["g1"] = _wgrad_rs1(pos, [dup1_t], n1, "wgrad_up1", carry=stage2(["g1"]))
    *sum1["d1"], recv2["u1"] = _wgrad_rs1(pos, [act1_t], dhh1, "wgrad_down1", carry=stage2(["u1"]))
    dx, dg1, recv2["d1"] = _ffn_bwd_input(x0, dh1, dgate1_t, dup1_t, g1, wg1, wu1, "ffn1_bwd_input", tm,
                                          carry=stage2(["d1"]))

    (dg1_all,) = _run_exchange(_DirectGather([dg1.reshape(8, 128)]), "gather_norm1_grad")
    pk = lambda a, b, c_, e, s_, pw_, psc: _pack_small(a, b, c_, e, s_[0], pw_[0], psc)
    small_w = pk(ffn1_norm, mix_norm, ffn2_norm, final_norm, sink_logits, pool_w, pool_scale)
    small_m = pk(m_ffn1_norm, m_mix_norm, m_ffn2_norm, m_final_norm, m_sink_logits, m_pool_w, m_pool_scale)
    small_v = pk(v_ffn1_norm, v_mix_norm, v_ffn2_norm, v_final_norm, v_sink_logits, v_pool_w, v_pool_scale)
    sg, sd, sm, sv = [_unpack_small(a)
                      for a in _adam_small(small_w, small_all, dg1_all, small_m, small_v, "adam_small")]

    big = {}
    keys = ["g1", "u1", "d1", "g2", "u2", "d2", "in", "out"]
    names = ["ffn1_w_gate", "ffn1_w_up", "ffn1_w_down", "ffn2_w_gate", "ffn2_w_up", "ffn2_w_down", "w_in", "w_out"]
    transposed = [True, True, False, True, True, False, True, False]
    ws = [ffn1_w_gate, ffn1_w_up, ffn1_w_down, ffn2_w_gate, ffn2_w_up, ffn2_w_down, w_in, w_out]
    ms = [m_ffn1_w_gate, m_ffn1_w_up, m_ffn1_w_down, m_ffn2_w_gate, m_ffn2_w_up, m_ffn2_w_down, m_w_in, m_w_out]
    vs = [v_ffn1_w_gate, v_ffn1_w_up, v_ffn1_w_down, v_ffn2_w_gate, v_ffn2_w_up, v_ffn2_w_down, v_w_in, v_w_out]
    for key, nm, tr, w, m, vv in zip(keys, names, transposed, ws, ms, vs):
        view = (lambda a: jnp.swapaxes(a, 1, 2)[0]) if tr else (lambda a: a[0])
        back = (lambda a: jnp.swapaxes(a[None], 1, 2)) if tr else (lambda a: a[None])
        res = _rs_sum2_adam(pos, sum1[key][0], recv2[key], view(w), view(m), view(vv), "adam_" + nm)
        big[nm] = tuple(back(a) for a in res)

    loss = sg["loss"]
    all_names = ["ffn1_norm", "ffn1_w_gate", "ffn1_w_up", "ffn1_w_down", "mix_norm", "w_in", "sink_logits", "pool_w",
                 "pool_scale", "w_out", "ffn2_norm", "ffn2_w_gate", "ffn2_w_up", "ffn2_w_down", "final_norm"]
    outs = [loss, dx[None]]
    for idx, src in enumerate((sg, sd, sm, sv)):
        for nm in all_names:
            outs.append(big[nm][idx] if nm in big else src[nm])
    return tuple(outs)
```

```python
import functools

import jax
import jax.numpy as jnp
import numpy as np
from jax import lax
from jax.experimental import pallas as pl
from jax.experimental.pallas import tpu as pltpu

F32 = jnp.float32
BF16 = jnp.bfloat16
MESH = pl.DeviceIdType.MESH
N_DEV = 8

EPS = 1e-6
HEAD_DIM = 64
N_HEADS = 8
N_KV = 2
GROUP = N_HEADS // N_KV
ATTN_W = N_HEADS * HEAD_DIM
KV_W = N_KV * HEAD_DIM
POOL_W = 512
POOL_G = 4
POOL_GW = POOL_W // POOL_G
POOL_WINDOWS = (2, 4, 8, 16)
BLK = 128
ROT = 16
ROPE_THETA = 500000.0
SCORE_SCALE = HEAD_DIM ** -0.5

ADAM_LR, ADAM_B1, ADAM_B2, ADAM_EPS, ADAM_WD, ADAM_STEP = 0.001, 0.9, 0.999, 1e-08, 0.01, 10

VMEM_LIMIT = 56 * 1024 * 1024


def _cparams(sem=None, **kw):
    if sem is not None:
        kw["dimension_semantics"] = sem
    return pltpu.CompilerParams(vmem_limit_bytes=VMEM_LIMIT, **kw)


def _whole(shape):
    nd = len(shape)
    return pl.BlockSpec(shape, lambda *_: (0,) * nd, pipeline_mode=pl.Buffered(1))


def _sigmoid(z):
    return 1.0 / (1.0 + jnp.exp(-z))


def _dot_nt(a, b):
    return lax.dot_general(a, b, (((1,), (1,)), ((), ())), preferred_element_type=F32)


def _dot_nn(a, b):
    return lax.dot_general(a, b, (((1,), (0,)), ((), ())), preferred_element_type=F32)


def _dot_tn(a, b):
    return lax.dot_general(a, b, (((0,), (0,)), ((), ())), preferred_element_type=F32)


def _rope_tables(s):
    inv_freq = ROPE_THETA ** (-np.arange(0, ROT, 2, dtype=np.float64) / ROT)
    ang = np.arange(s, dtype=np.float64)[:, None] * inv_freq[None, :]
    c = np.ones((s, HEAD_DIM)); sa = np.zeros((s, HEAD_DIM)); sb = np.zeros((s, HEAD_DIM))
    c[:, :8] = np.cos(ang); c[:, 8:16] = np.cos(ang)
    sa[:, :8] = -np.sin(ang)
    sb[:, 8:16] = np.sin(ang)
    t = lambda a: jnp.asarray(np.tile(a, (1, 2)).astype(np.float32))
    return t(c), t(sa), t(sb)


def _pool_weight(gi, t, s_pos, s):
    half = POOL_WINDOWS[gi] // 2

    def win(lo, hi):
        a = np.clip(lo, 0, s); b = np.clip(hi + 1, 0, s)
        inside = (s_pos >= a) & (s_pos < b)
        return inside / np.maximum(b - a, 1)

    w = 0.5 * (win(t - half, t + half - 1) + win(t - half + 1, t + half)) - (t == s_pos)
    return w * ((t >= 0) & (t < s) & (s_pos >= 0) & (s_pos < s))


def _pool_tables(s):
    nb = s // BLK
    fwd = np.zeros((3, POOL_G, BLK, 3 * BLK), np.float32)
    bwd = np.zeros((3, POOL_G, BLK, 3 * BLK), np.float32)
    for vi, n in enumerate((0, 1 if nb > 2 else 0, nb - 1)):
        i = n * BLK + np.arange(BLK)[:, None]
        j = (n - 1) * BLK + np.arange(3 * BLK)[None, :]
        for gi in range(POOL_G):
            fwd[vi, gi] = _pool_weight(gi, i, j, s)
            bwd[vi, gi] = _pool_weight(gi, j, i, s)
    return jnp.asarray(fwd, dtype=BF16), jnp.asarray(bwd, dtype=BF16)


def _variant_index(n, nb):
    return jnp.where(n == 0, 0, jnp.where(n == nb - 1, 2, 1))


class _Exchange:
    inputs = ()
    out_shapes = ()
    sems = ()

    def start(self, srcs, outs, sems):
        raise NotImplementedError

    def middle(self, srcs, outs, sems):
        pass

    def finish(self, srcs, outs, sems):
        raise NotImplementedError


class _AllGather(_Exchange):
    def __init__(self, arrays):
        n = len(arrays)
        self.inputs = list(arrays)
        self.out_shapes = [jax.ShapeDtypeStruct((N_DEV,) + a.shape, a.dtype) for a in arrays]
        self.sems = [pltpu.SemaphoreType.DMA((n, 8)), pltpu.SemaphoreType.DMA((n, 8)), pltpu.SemaphoreType.DMA((n,))]

    def _parts(self, srcs, outs, sems):
        send_sems, recv_sems, local_sems = sems
        n = len(srcs)
        x, y, c = lax.axis_index("x"), lax.axis_index("y"), lax.axis_index("c")
        me, sibling, xn, yn, diag = (x, y, c), (x, y, 1 - c), (1 - x, y, c), (x, 1 - y, c), (1 - x, 1 - y, c)

        def place(a, dev, half=None):
            block = outs[a].at[4 * dev[0] + 2 * dev[1] + dev[2]]
            if half is None:
                return block
            r2 = outs[a].shape[1] // 2
            return block.at[pl.ds(half * r2, r2)]

        def copy(a, k, dev, to, half=None, src=None):
            where = place(a, dev, half)
            return pltpu.make_async_remote_copy(
                src_ref=where if src is None else src, dst_ref=where, send_sem=send_sems.at[a, k],
                recv_sem=recv_sems.at[a, k], device_id=to, device_id_type=MESH)

        def other(dev):
            return (dev[0], dev[1], 1 - dev[2])

        class Parts:
            mine = staticmethod(lambda: [pltpu.make_async_copy(srcs[a], place(a, me), local_sems.at[a])
                                         for a in range(n)])
            own = staticmethod(lambda: [copy(a, k, me, to, src=srcs[a]) for a in range(n)
                                        for k, to in ((0, sibling), (1, xn), (2, yn))])
            relay = staticmethod(lambda a: [copy(a, 3, xn, yn, half=0), copy(a, 4, yn, xn, half=1),
                                            copy(a, 5, xn, sibling), copy(a, 6, yn, sibling)])
            last = staticmethod(lambda a: copy(a, 7, diag, sibling))
            from_x = staticmethod(lambda a: copy(a, 1, xn, me))
            from_y = staticmethod(lambda a: copy(a, 2, yn, me))
            diag_halves = staticmethod(lambda a: [copy(a, 3, diag, me, half=0), copy(a, 4, diag, me, half=1)])
            from_sibling = staticmethod(lambda a: [copy(a, 0, sibling, me), copy(a, 5, other(xn), me),
                                                   copy(a, 6, other(yn), me), copy(a, 7, other(diag), me)])

        return n, Parts

    def start(self, srcs, outs, sems):
        _, p = self._parts(srcs, outs, sems)
        for cp in p.mine() + p.own():
            cp.start()

    def middle(self, srcs, outs, sems):
        n, p = self._parts(srcs, outs, sems)
        for a in range(n):
            p.from_x(a).wait_recv()
            p.from_y(a).wait_recv()
            for cp in p.relay(a):
                cp.start()

    def finish(self, srcs, outs, sems):
        n, p = self._parts(srcs, outs, sems)
        for a in range(n):
            for cp in p.diag_halves(a):
                cp.wait_recv()
            p.last(a).start()
        for a in range(n):
            for cp in p.from_sibling(a):
                cp.wait_recv()
        for cp in p.own() + [cp for a in range(n) for cp in p.relay(a) + [p.last(a)]]:
            cp.wait_send()
        for cp in p.mine():
            cp.wait()


class _RsStage2(_Exchange):
    def start(self, srcs, outs, sems):
        for cp in self._copies(srcs, outs, sems):
            cp.start()

    def finish(self, srcs, outs, sems):
        copies = self._copies(srcs, outs, sems)
        for cp in copies:
            cp.wait_recv()
        for cp in copies:
            cp.wait_send()


    def __init__(self, pbs):
        n = len(pbs)
        self.inputs = list(pbs)
        self.out_shapes = [jax.ShapeDtypeStruct((3,) + p.shape[1:], p.dtype) for p in pbs]
        self.sems = [pltpu.SemaphoreType.DMA((n, 3)), pltpu.SemaphoreType.DMA((n, 3))]

    def _copies(self, srcs, outs, sems):
        send_sems, recv_sems = sems
        x, y, c = lax.axis_index("x"), lax.axis_index("y"), lax.axis_index("c")
        chips = [(1 - x, y), (x, 1 - y), (1 - x, 1 - y)]
        return [pltpu.make_async_remote_copy(
            src_ref=srcs[a].at[2 * chip[0] + chip[1]], dst_ref=outs[a].at[j], send_sem=send_sems.at[a, j],
            recv_sem=recv_sems.at[a, j], device_id=(*chip, c), device_id_type=MESH)
            for a in range(len(srcs)) for j, chip in enumerate(chips)]


class _DirectGather(_Exchange):
    def __init__(self, arrays):
        n = len(arrays)
        self.inputs = list(arrays)
        self.out_shapes = [jax.ShapeDtypeStruct((N_DEV,) + a.shape, a.dtype) for a in arrays]
        self.sems = [pltpu.SemaphoreType.DMA((n, 7)), pltpu.SemaphoreType.DMA((n, 7)), pltpu.SemaphoreType.DMA((n,))]

    def _copies(self, srcs, outs, sems):
        send_sems, recv_sems, local_sems = sems
        x, y, c = lax.axis_index("x"), lax.axis_index("y"), lax.axis_index("c")
        me = 4 * x + 2 * y + c
        remote, local = [], []
        for a in range(len(srcs)):
            local.append(pltpu.make_async_copy(srcs[a], outs[a].at[me], local_sems.at[a]))
            for k in range(1, N_DEV):
                peer = (x ^ (k >> 2), y ^ ((k >> 1) & 1), c ^ (k & 1))
                remote.append(pltpu.make_async_remote_copy(
                    src_ref=srcs[a], dst_ref=outs[a].at[me], send_sem=send_sems.at[a, k - 1],
                    recv_sem=recv_sems.at[a, k - 1], device_id=peer, device_id_type=MESH))
        return remote, local

    def start(self, srcs, outs, sems):
        remote, local = self._copies(srcs, outs, sems)
        for cp in local + remote:
            cp.start()

    def finish(self, srcs, outs, sems):
        remote, local = self._copies(srcs, outs, sems)
        for cp in remote:
            cp.wait_recv()
        for cp in remote:
            cp.wait_send()
        for cp in local:
            cp.wait()


class _Both(_Exchange):
    def __init__(self, a, b):
        self.a, self.b = a, b
        self.inputs = list(a.inputs) + list(b.inputs)
        self.out_shapes = list(a.out_shapes) + list(b.out_shapes)
        self.sems = list(a.sems) + list(b.sems)

    def _split(self, srcs, outs, sems):
        na, oa, sa = len(self.a.inputs), len(self.a.out_shapes), len(self.a.sems)
        return (srcs[:na], outs[:oa], sems[:sa]), (srcs[na:], outs[oa:], sems[sa:])

    def start(self, srcs, outs, sems):
        pa, pb = self._split(srcs, outs, sems)
        self.a.start(*pa)
        self.b.start(*pb)

    def middle(self, srcs, outs, sems):
        pa, pb = self._split(srcs, outs, sems)
        self.a.middle(*pa)
        self.b.middle(*pb)

    def finish(self, srcs, outs, sems):
        pa, pb = self._split(srcs, outs, sems)
        self.a.finish(*pa)
        self.b.finish(*pb)


_ANY = pl.BlockSpec(memory_space=pl.ANY)


def _run_exchange(ex, name):
    n_in, n_out = len(ex.inputs), len(ex.out_shapes)

    def body(*refs):
        srcs, outs, sems = refs[:n_in], refs[n_in:n_in + n_out], refs[n_in + n_out:]
        ex.start(srcs, outs, sems)
        ex.middle(srcs, outs, sems)
        ex.finish(srcs, outs, sems)

    return pl.pallas_call(
        body, name=name, out_shape=list(ex.out_shapes), in_specs=[_ANY] * n_in, out_specs=[_ANY] * n_out,
        scratch_shapes=list(ex.sems),
    )(*ex.inputs)


CARRY_MIDDLE = 0.7


def _call(body, name, grid, in_specs, out_specs, out_shape, args, sem, carry=None, middle=CARRY_MIDDLE, scratch=()):
    if carry is None:
        return pl.pallas_call(functools.partial(body), name=name, grid=grid, in_specs=in_specs, out_specs=out_specs,
                              out_shape=out_shape, scratch_shapes=list(scratch), compiler_params=_cparams(sem))(*args)
    n_in, n_out = len(in_specs), len(out_specs)
    nc_in, nc_out = len(carry.inputs), len(carry.out_shapes)

    def carried(*refs):
        ins = refs[:n_in]
        c_in = refs[n_in:n_in + nc_in]
        outs = refs[n_in + nc_in:n_in + nc_in + n_out]
        c_out = refs[n_in + nc_in + n_out:n_in + nc_in + n_out + nc_out]
        own = refs[n_in + nc_in + n_out + nc_out:n_in + nc_in + n_out + nc_out + len(scratch)]
        sems = refs[n_in + nc_in + n_out + nc_out + len(scratch):]
        ids = [pl.program_id(i) for i in range(len(grid))]
        is_first = functools.reduce(jnp.logical_and, [i == 0 for i in ids])
        is_last = functools.reduce(jnp.logical_and, [i == g - 1 for i, g in zip(ids, grid)])
        @pl.when(is_first)
        def _():
            carry.start(c_in, c_out, sems)

        if middle is not None:
            @pl.when(functools.reduce(jnp.logical_and, [ids[0] == round(middle * (grid[0] - 1))]
                                      + [i == 0 for i in ids[1:]]))
            def _():
                carry.middle(c_in, c_out, sems)

        body(*ins, *outs, *own)

        @pl.when(is_last)
        def _():
            if middle is None:
                carry.middle(c_in, c_out, sems)
            carry.finish(c_in, c_out, sems)

    return pl.pallas_call(
        carried, name=name, grid=grid, in_specs=list(in_specs) + [_ANY] * nc_in,
        out_specs=list(out_specs) + [_ANY] * nc_out, out_shape=list(out_shape) + list(carry.out_shapes),
        scratch_shapes=list(scratch) + list(carry.sems), compiler_params=_cparams(sem))(*args, *carry.inputs)


def _rs_sum2_adam(pos, p, r2, w, m, v, name):
    _, rows, d = p.shape
    tr = rows // 2 if rows % 16 == 0 else rows

    def body(pos_ref, p_ref, r_ref, w_ref, m_ref, v_ref, g_ref, d_ref, nm_ref, nv_ref):
        r = r_ref[...].astype(F32)
        g = ((p_ref[...] + r[0]) + r[1]) + r[2]
        g_ref[...] = g
        d_ref[...], nm_ref[...], nv_ref[...] = _adam_math(w_ref[...], g, m_ref[...], v_ref[...])

    blk = pl.BlockSpec((tr, d), lambda i, pos_ref: (i, 0))
    return pl.pallas_call(
        body, name=name,
        grid_spec=pltpu.PrefetchScalarGridSpec(
            num_scalar_prefetch=1, grid=(rows // tr,),
            in_specs=[pl.BlockSpec((None, tr, d), lambda i, pos_ref: (2 * pos_ref[0] + pos_ref[1], i, 0)),
                      pl.BlockSpec((3, tr, d), lambda i, pos_ref: (0, i, 0)), blk, blk, blk],
            out_specs=[blk] * 4),
        out_shape=[jax.ShapeDtypeStruct((rows, d), F32)] * 4,
        compiler_params=_cparams(("arbitrary",)),
    )(pos, p, r2, w, m, v)


def _ffn_chunk(f):
    for cand in (256, 128):
        if f % cand == 0:
            return cand
    return f


def _loss_head(x, gg, target, loss_ref, dg_ref):
    @pl.when(pl.program_id(0) == 0)
    def _():
        loss_ref[...] = jnp.zeros_like(loss_ref)
        dg_ref[...] = jnp.zeros_like(dg_ref)

    r = lax.rsqrt(jnp.mean(x * x, axis=-1, keepdims=True) + EPS)
    xhat = x * r
    e = xhat * gg - target
    loss_ref[...] += 0.5 * jnp.sum(jnp.mean(e * e, axis=-1, keepdims=True), axis=0, keepdims=True)
    dy = e * (1.0 / x.shape[-1])
    dg_ref[...] += jnp.sum(dy * xhat, axis=0, keepdims=True)
    dxhat = dy * gg
    return r * (dxhat - xhat * jnp.mean(dxhat * xhat, axis=-1, keepdims=True))


def _ffn_gate(h, g, wg_t, name, tm, carry=None):
    s, d = h.shape
    f = wg_t.shape[0]

    def body(h_ref, g_ref, wg_ref, n_ref, gate_ref):
        x = h_ref[...]
        r = lax.rsqrt(jnp.mean(x * x, axis=-1, keepdims=True) + EPS)
        nb = (x * r * g_ref[...]).astype(BF16)
        n_ref[...] = nb
        gate_ref[...] = _dot_nt(nb, wg_ref[...]).astype(BF16)

    row = lambda w: pl.BlockSpec((tm, w), lambda i: (i, 0))
    return _call(body, name, (s // tm,), [row(d), _whole((1, d)), _whole((f, d))], [row(d), row(f)],
                 [jax.ShapeDtypeStruct((s, d), BF16), jax.ShapeDtypeStruct((s, f), BF16)], (h, g, wg_t),
                 ("arbitrary",), carry)


def _ffn_up(n, gate, wu_t, name, tm, carry=None):
    s, d = n.shape
    f = wu_t.shape[0]
    tf = _ffn_chunk(f)

    def body(n_ref, gate_ref, wu_ref, up_ref, act_t_ref):
        nb = n_ref[...]
        for j in range(f // tf):
            sl = slice(j * tf, (j + 1) * tf)
            up = _dot_nt(nb, wu_ref[sl, :])
            gate = gate_ref[:, sl].astype(F32)
            up_ref[:, sl] = up.astype(BF16)
            act_t_ref[sl, :] = (gate * _sigmoid(gate) * up).astype(BF16).T

    row = lambda w: pl.BlockSpec((tm, w), lambda i: (i, 0))
    return _call(body, name, (s // tm,), [row(d), row(f), _whole((f, d))],
                 [row(f), pl.BlockSpec((f, tm), lambda i: (0, i))],
                 [jax.ShapeDtypeStruct((s, f), BF16), jax.ShapeDtypeStruct((f, s), BF16)], (n, gate, wu_t),
                 ("arbitrary",), carry)


def _ffn_fwd(h, g, wg_t, wu_t, wd, name, tm, carry=None, head=None, middle=CARRY_MIDDLE):
    s, d = h.shape
    f = wg_t.shape[0]
    tf = _ffn_chunk(f)

    def body(h_ref, g_ref, wg_ref, wu_ref, wd_ref, *refs):
        if head is None:
            o_ref, n_ref, gate_ref, up_ref, act_t_ref, act_ref = refs
        else:
            gf_ref, t_ref, o_ref, n_ref, gate_ref, up_ref, act_t_ref, loss_ref, dgf_ref, act_ref = refs
        x = h_ref[...]
        r = lax.rsqrt(jnp.mean(x * x, axis=-1, keepdims=True) + EPS)
        nb = (x * r * g_ref[...]).astype(BF16)
        n_ref[...] = nb
        for j in range(f // tf):
            sl = slice(j * tf, (j + 1) * tf)
            gate = _dot_nt(nb, wg_ref[sl, :])
            up = _dot_nt(nb, wu_ref[sl, :])
            gate_ref[:, sl] = gate.astype(BF16)
            up_ref[:, sl] = up.astype(BF16)
            act = gate * _sigmoid(gate) * up
            act = act.astype(BF16)
            act_ref[:, sl] = act
            act_t_ref[sl, :] = act.T
        h_out = x + 0.5 * _dot_nn(act_ref[...], wd_ref[...])
        o_ref[...] = h_out if head is None else _loss_head(h_out, gf_ref[...], t_ref[...], loss_ref, dgf_ref)

    row = lambda w: pl.BlockSpec((tm, w), lambda i: (i, 0))
    in_specs = [row(d), _whole((1, d)), _whole((f, d)), _whole((f, d)), _whole((f, d))]
    out_specs = [row(d), row(d), row(f), row(f), pl.BlockSpec((f, tm), lambda i: (0, i))]
    out_shape = ([jax.ShapeDtypeStruct((s, d), F32), jax.ShapeDtypeStruct((s, d), BF16)]
                 + [jax.ShapeDtypeStruct((s, f), BF16)] * 2 + [jax.ShapeDtypeStruct((f, s), BF16)])
    args = (h, g, wg_t, wu_t, wd)
    scratch = (pltpu.VMEM((tm, f), BF16),)
    if head is not None:
        in_specs += [_whole((1, d)), row(d)]
        out_specs += [pl.BlockSpec((1, 1), lambda i: (0, 0)), pl.BlockSpec((1, d), lambda i: (0, 0))]
        out_shape += [jax.ShapeDtypeStruct((1, 1), F32), jax.ShapeDtypeStruct((1, d), F32)]
        args += tuple(head)
    return _call(body, name, (s // tm,), in_specs, out_specs, out_shape, args, ("arbitrary",), carry, middle, scratch)


def _gate_grads(dh_ref, gate_ref, up_ref, wd_ref, dgate_ref, dup_ref, dgate_t_ref, dup_t_ref, dhh_ref, tf):
    dhh = (0.5 * dh_ref[...]).astype(BF16)
    dhh_ref[...] = dhh
    for j in range(gate_ref.shape[1] // tf):
        sl = slice(j * tf, (j + 1) * tf)
        gt = gate_ref[:, sl].astype(F32)
        u = up_ref[:, sl].astype(F32)
        dact = _dot_nt(dhh, wd_ref[sl, :])
        sg = _sigmoid(gt)
        dup = dact * (gt * sg)
        dgate = dact * u * (sg * (1.0 + gt * (1.0 - sg)))
        dup, dgate = dup.astype(BF16), dgate.astype(BF16)
        if dup_ref is not None:
            dup_ref[:, sl] = dup
            dgate_ref[:, sl] = dgate
        dup_t_ref[sl, :] = dup.T
        dgate_t_ref[sl, :] = dgate.T


def _input_grad(h_ref, dh_ref, dgate_ref, dup_ref, g_ref, wg_ref, wu_ref, o_ref, dg_ref, transposed=False):
    x = h_ref[...]
    r = lax.rsqrt(jnp.mean(x * x, axis=-1, keepdims=True) + EPS)
    xhat = x * r
    dot = _dot_tn if transposed else _dot_nn
    dn = dot(dgate_ref[...], wg_ref[...]) + dot(dup_ref[...], wu_ref[...])
    dxhat = dn * g_ref[...]
    o_ref[...] = dh_ref[...] + r * (dxhat - xhat * jnp.mean(dxhat * xhat, axis=-1, keepdims=True))

    @pl.when(pl.program_id(0) == 0)
    def _():
        dg_ref[...] = jnp.zeros_like(dg_ref)

    dg_ref[...] += jnp.sum(dn * xhat, axis=0, keepdims=True)


def _ffn_bwd(h_in, dh_out, gate, up, g, wg_t, wu_t, wd, w_out, name, tm):
    s, d = h_in.shape
    f = gate.shape[1]
    w = w_out.shape[0]
    tf = _ffn_chunk(f)

    def body(h_ref, dh_ref, gate_ref, up_ref, g_ref, wg_ref, wu_ref, wd_ref, wo_ref,
             o_ref, dg_ref, dgate_t_ref, dup_t_ref, dhh_ref, dmix_ref, dhb_ref, dgate_ref, dup_ref):
        _gate_grads(dh_ref, gate_ref, up_ref, wd_ref, dgate_ref, dup_ref, dgate_t_ref, dup_t_ref, dhh_ref, tf)
        _input_grad(h_ref, dh_ref, dgate_ref, dup_ref, g_ref, wg_ref, wu_ref, o_ref, dg_ref)
        dhb = o_ref[...].astype(BF16)
        dhb_ref[...] = dhb
        dmix_ref[...] = _dot_nt(dhb, wo_ref[...])

    row = lambda c: pl.BlockSpec((tm, c), lambda i: (i, 0))
    col = pl.BlockSpec((f, tm), lambda i: (0, i))
    return pl.pallas_call(
        body, name=name, grid=(s // tm,),
        in_specs=[row(d), row(d), row(f), row(f), _whole((1, d)), _whole((f, d)), _whole((f, d)), _whole((f, d)),
                  _whole((w, d))],
        out_specs=[row(d), pl.BlockSpec((1, d), lambda i: (0, 0)), col, col, row(d), row(w), row(d)],
        out_shape=[jax.ShapeDtypeStruct((s, d), F32), jax.ShapeDtypeStruct((1, d), F32),
                   jax.ShapeDtypeStruct((f, s), BF16), jax.ShapeDtypeStruct((f, s), BF16),
                   jax.ShapeDtypeStruct((s, d), BF16), jax.ShapeDtypeStruct((s, w), F32),
                   jax.ShapeDtypeStruct((s, d), BF16)],
        scratch_shapes=[pltpu.VMEM((tm, f), BF16), pltpu.VMEM((tm, f), BF16)],
        compiler_params=_cparams(("arbitrary",)),
    )(h_in, dh_out, gate, up, g, wg_t, wu_t, wd, w_out)


def _ffn_bwd_gates(dh_out, gate, up, wd, name, tm, carry=None):
    s, d = dh_out.shape
    f = gate.shape[1]
    tf = _ffn_chunk(f)

    def body(dh_ref, gate_ref, up_ref, wd_ref, dgate_t_ref, dup_t_ref, dhh_ref):
        _gate_grads(dh_ref, gate_ref, up_ref, wd_ref, None, None, dgate_t_ref, dup_t_ref, dhh_ref, tf)

    row = lambda w: pl.BlockSpec((tm, w), lambda i: (i, 0))
    col = pl.BlockSpec((f, tm), lambda i: (0, i))
    return _call(
        body, name, (s // tm,), [row(d), row(f), row(f), _whole((f, d))], [col, col, row(d)],
        [jax.ShapeDtypeStruct((f, s), BF16)] * 2 + [jax.ShapeDtypeStruct((s, d), BF16)],
        (dh_out, gate, up, wd), ("arbitrary",), carry)


def _ffn_bwd_input(h_in, dh_out, dgate_t, dup_t, g, wg_t, wu_t, name, tm, carry=None):
    s, d = h_in.shape
    f = dgate_t.shape[0]

    row = lambda w: pl.BlockSpec((tm, w), lambda i: (i, 0))
    col = pl.BlockSpec((f, tm), lambda i: (0, i))
    return _call(
        functools.partial(_input_grad, transposed=True), name, (s // tm,),
        [row(d), row(d), col, col, _whole((1, d)), _whole((f, d)), _whole((f, d))],
        [row(d), pl.BlockSpec((1, d), lambda i: (0, 0))],
        [jax.ShapeDtypeStruct((s, d), F32), jax.ShapeDtypeStruct((1, d), F32)],
        (h_in, dh_out, dgate_t, dup_t, g, wg_t, wu_t), ("arbitrary",), carry)


def _wgrad_rs1(pos, a_ts, b, name, carry=None):
    n_a = len(a_ts)
    f, s = a_ts[0].shape
    d = b.shape[1]
    fk = f // N_DEV
    nc_in = 0 if carry is None else len(carry.inputs)
    nc_out = 0 if carry is None else len(carry.out_shapes)

    def body(pos_ref, *refs):
        a_refs, b_ref, refs = refs[:n_a], refs[n_a], refs[n_a + 1:]
        c_in = refs[:nc_in]
        outs = refs[nc_in:nc_in + 2 * n_a]
        c_out = refs[nc_in + 2 * n_a:nc_in + 2 * n_a + nc_out]
        stage, land, send_sems, recv_sems = refs[nc_in + 2 * n_a + nc_out:nc_in + 2 * n_a + nc_out + 4]
        c_sems = refs[nc_in + 2 * n_a + nc_out + 4:]
        t = pl.program_id(0)
        which, step = t // 8, t % 8
        slot = 4 * which + step % 4
        x, y, c = lax.axis_index("x"), lax.axis_index("y"), lax.axis_index("c")

        def push(k):
            return pltpu.make_async_remote_copy(src_ref=stage.at[k], dst_ref=land.at[k], send_sem=send_sems.at[k],
                                                recv_sem=recv_sems.at[k], device_id=(x, y, 1 - c), device_id_type=MESH)

        if carry is not None:
            @pl.when(t == 0)
            def _():
                carry.start(c_in, c_out, c_sems)

        for j in range(n_a):
            @pl.when(which == j)
            def _(j=j):
                g = _dot_nn(a_refs[j][...], b_ref[...])

                @pl.when(step < 4)
                def _():
                    stage[slot] = g.astype(BF16)
                    push(slot).start()

                @pl.when(step >= 4)
                def _():
                    push(slot).wait_recv()
                    p = g + land[slot].astype(F32)
                    outs[2 * j][...] = p
                    outs[2 * j + 1][...] = p.astype(BF16)

        @pl.when(t == 8 * n_a - 1)
        def _():
            for k in range(4 * n_a):
                push(k).wait_send()
            if carry is not None:
                carry.middle(c_in, c_out, c_sems)
                carry.finish(c_in, c_out, c_sems)

    def shard(j):
        def index_map(t, pos_ref):
            step = jnp.clip(t - 8 * j, 0, 7)
            return 4 * ((step % 4) // 2) + 2 * (step % 2) + jnp.where(step < 4, 1 - pos_ref[2], pos_ref[2]), 0
        return index_map

    out = lambda j: pl.BlockSpec((None, fk, d), lambda t, pos_ref, j=j: (jnp.clip(t - 8 * j - 4, 0, 3), 0, 0))
    return pl.pallas_call(
        body, name=name,
        grid_spec=pltpu.PrefetchScalarGridSpec(
            num_scalar_prefetch=1, grid=(8 * n_a,),
            in_specs=[pl.BlockSpec((fk, s), shard(j)) for j in range(n_a)]
            + [pl.BlockSpec((s, d), lambda t, pos_ref: (0, 0), pipeline_mode=pl.Buffered(1))] + [_ANY] * nc_in,
            out_specs=[out(j) for j in range(n_a) for _ in range(2)] + [_ANY] * nc_out,
            scratch_shapes=[pltpu.VMEM((4 * n_a, fk, d), BF16), pltpu.VMEM((4 * n_a, fk, d), BF16),
                            pltpu.SemaphoreType.DMA((4 * n_a,)), pltpu.SemaphoreType.DMA((4 * n_a,))]
            + ([] if carry is None else list(carry.sems))),
        out_shape=[jax.ShapeDtypeStruct((4, fk, d), dt) for _ in range(n_a) for dt in (F32, BF16)]
        + ([] if carry is None else list(carry.out_shapes)),
        compiler_params=_cparams(("arbitrary",)),
    )(pos, *a_ts, b, *([] if carry is None else carry.inputs))


def _rope(t, c, sa, sb, reps):
    c, sa, sb = (jnp.tile(v, (1, reps)) if reps > 1 else v for v in (c, sa, sb))
    w = t.shape[1]
    return t * c + pltpu.roll(t, w - 8, 1) * sa + pltpu.roll(t, 8, 1) * sb


def _rope_bwd(dt, c, sa, sb, reps):
    c, sa, sb = (jnp.tile(v, (1, reps)) if reps > 1 else v for v in (c, sa, sb))
    w = dt.shape[1]
    return dt * c + pltpu.roll(dt * sa, 8, 1) + pltpu.roll(dt * sb, w - 8, 1)


def _ffn_down_mix_in(x, act_t, wd, g, win_t, tabs, name, tm, carry=None):
    s, d = x.shape
    f = wd.shape[0]
    n_in = win_t.shape[0]

    def body(x_ref, act_ref, wd_ref, g_ref, w_ref, c_ref, sa_ref, sb_ref, h_ref, q_ref, k_ref, v_ref, pc_ref, n_ref):
        x = x_ref[...] + 0.5 * _dot_tn(act_ref[...], wd_ref[...])
        h_ref[...] = x
        r = lax.rsqrt(jnp.mean(x * x, axis=-1, keepdims=True) + EPS)
        nb = (x * r * g_ref[...]).astype(BF16)
        n_ref[...] = nb
        u = _dot_nt(nb, w_ref[...])
        c, sa, sb = c_ref[...], sa_ref[...], sb_ref[...]
        q_ref[...] = _rope(u[:, :ATTN_W], c, sa, sb, ATTN_W // 128).astype(BF16)
        k_ref[...] = _rope(u[:, ATTN_W:ATTN_W + KV_W], c, sa, sb, 1).astype(BF16)
        v_ref[...] = u[:, ATTN_W + KV_W:ATTN_W + 2 * KV_W].astype(BF16)
        pc_ref[...] = u[:, ATTN_W + 2 * KV_W:]

    row = lambda w: pl.BlockSpec((tm, w), lambda i: (i, 0))
    return _call(
        body, name, (s // tm,),
        [row(d), pl.BlockSpec((f, tm), lambda i: (0, i)), _whole((f, d)), _whole((1, d)), _whole((n_in, d)),
         row(128), row(128), row(128)],
        [row(d), row(ATTN_W), row(KV_W), row(KV_W), row(POOL_W), row(d)],
        [jax.ShapeDtypeStruct((s, d), F32), jax.ShapeDtypeStruct((s, ATTN_W), BF16),
         jax.ShapeDtypeStruct((s, KV_W), BF16), jax.ShapeDtypeStruct((s, KV_W), BF16),
         jax.ShapeDtypeStruct((s, POOL_W), F32), jax.ShapeDtypeStruct((s, d), BF16)],
        (x, act_t, wd, g, win_t, *tabs), ("arbitrary",), carry)


def _band_mask(n, nb, transposed):
    shape = (3 * BLK, 2 * BLK) if transposed else (2 * BLK, 3 * BLK)
    i = lax.broadcasted_iota(jnp.int32, shape, 1 if transposed else 0) % BLK
    j = lax.broadcasted_iota(jnp.int32, shape, 0 if transposed else 1)
    kpos = (n - 1) * BLK + j
    return (j >= i) & (j <= i + 2 * BLK) & (kpos >= 0) & (kpos < nb * BLK)


def _block_diag(t, kh):
    tf = t.astype(F32)
    tr = pltpu.roll(tf, HEAD_DIM, 1)
    lo = lax.broadcasted_iota(jnp.int32, tf.shape, 1) < HEAD_DIM
    top, bot = (tf, tr) if kh == 0 else (tr, tf)
    return jnp.concatenate([jnp.where(lo, top, 0.0), jnp.where(lo, 0.0, bot)], axis=0).astype(BF16)


def _fold_diag(tbd):
    lo = lax.broadcasted_iota(jnp.int32, (3 * BLK, 2 * HEAD_DIM), 1) < HEAD_DIM
    t = jnp.where(lo, tbd[:3 * BLK], tbd[3 * BLK:])
    return t + pltpu.roll(t, HEAD_DIM, 1)


def _stack_pairs(x, kh):
    return jnp.concatenate([x[:, (2 * kh) * 128:(2 * kh + 1) * 128], x[:, (2 * kh + 1) * 128:(2 * kh + 2) * 128]], axis=0)


def _sink_of(sink_ref, kh, half, axis):
    shape = (2 * BLK, 1) if axis == 0 else (1, 2 * BLK)
    first = lax.broadcasted_iota(jnp.int32, shape, axis) < BLK
    return jnp.where(first, sink_ref[0, GROUP * kh + half], sink_ref[0, GROUP * kh + 2 + half])


def _softmax_sink(sc, valid, sink, axis):
    sc = jnp.where(valid, sc, -1e30)
    m = jnp.maximum(jnp.max(sc, axis=axis, keepdims=True), sink)
    e = jnp.exp(sc - m)
    es = jnp.exp(sink - m)
    inv = 1.0 / (jnp.sum(e, axis=axis, keepdims=True) + es)
    return e * inv, es * inv


def _attn_blocks_per_step(nb):
    return next(nq for nq in (4, 2, 1) if nb % nq == 0)


def _band_specs(nq, nb, w, col=0):
    return [pl.BlockSpec((BLK, w), lambda m: (jnp.maximum(nq * m - 1, 0), col)),
            pl.BlockSpec((nq * BLK, w), lambda m: (m, col)),
            pl.BlockSpec((BLK, w), lambda m: (jnp.minimum(nq * m + nq, nb - 1), col))]


def _attn_pool_fwd(h, w_out, q, k, v, pc, sink, pool_w, pool_scale, pband, name, carry=None, middle=CARRY_MIDDLE):
    s, d = h.shape
    nb = s // BLK
    nq = _attn_blocks_per_step(nb)

    def body(sink_ref, q_ref, k0, k1, k2, v0, v1, v2, p0, p1, p2, pw_ref, ps_ref, pb_ref, h_ref, wo_ref,
             h_out_ref, o_t_ref, o_ref):
        kall = jnp.concatenate([k0[...], k1[...], k2[...]], axis=0)
        vall = jnp.concatenate([v0[...], v1[...], v2[...]], axis=0)
        pall = jnp.concatenate([p0[...], p1[...], p2[...]], axis=0).astype(BF16)
        qall = q_ref[...] * SCORE_SCALE
        for j in range(nq):
            n = pl.program_id(0) * nq + j
            rows, band = slice(j * BLK, (j + 1) * BLK), slice(j * BLK, (j + 3) * BLK)
            valid = _band_mask(n, nb, False)
            kb, vb, qs = kall[band], vall[band], qall[rows]
            for kh in range(N_KV):
                sc = _dot_nt(_stack_pairs(qs, kh), _block_diag(kb, kh))
                p = [_softmax_sink(sc[:, half * 3 * BLK:(half + 1) * 3 * BLK], valid,
                                   _sink_of(sink_ref, kh, half, 0), 1)[0] for half in range(2)]
                o2 = _dot_nn(jnp.concatenate(p, axis=1).astype(BF16), _block_diag(vb, kh)).astype(BF16)
                o_ref[rows, (2 * kh) * 128:(2 * kh + 1) * 128] = o2[:BLK]
                o_ref[rows, (2 * kh + 1) * 128:(2 * kh + 2) * 128] = o2[BLK:]
            ext = pall[band]
            var = _variant_index(n, nb)
            for gi in range(POOL_G):
                gsl = slice(gi * POOL_GW, (gi + 1) * POOL_GW)
                dg = _dot_nn(pb_ref[var, gi], ext[:, gsl])
                yg = _dot_nn(dg.astype(BF16), pw_ref[gi].astype(BF16))
                o_ref[rows, ATTN_W + gi * POOL_GW:ATTN_W + (gi + 1) * POOL_GW] = (yg * ps_ref[:, gsl]).astype(BF16)
        mix = o_ref[...]
        o_t_ref[...] = mix.T
        h_out_ref[...] = h_ref[...] + _dot_nn(mix, wo_ref[...])

    mix_w = ATTN_W + POOL_W
    return _call(
        body, name, (nb // nq,),
        [pl.BlockSpec(memory_space=pltpu.SMEM), pl.BlockSpec((nq * BLK, ATTN_W), lambda m: (m, 0)),
         *_band_specs(nq, nb, KV_W), *_band_specs(nq, nb, KV_W), *_band_specs(nq, nb, POOL_W),
         _whole((POOL_G, POOL_GW, POOL_GW)), _whole((1, POOL_W)), _whole(pband.shape),
         pl.BlockSpec((nq * BLK, d), lambda m: (m, 0)), _whole((mix_w, d))],
        [pl.BlockSpec((nq * BLK, d), lambda m: (m, 0)), pl.BlockSpec((mix_w, nq * BLK), lambda m: (0, m))],
        [jax.ShapeDtypeStruct((s, d), F32), jax.ShapeDtypeStruct((mix_w, s), BF16)],
        (sink, q, k, k, k, v, v, v, pc, pc, pc, pool_w, pool_scale, pband, h, w_out), ("arbitrary",), carry, middle,
        scratch=(pltpu.VMEM((nq * BLK, mix_w), BF16),))


def _attn_pool_bwd(q, k, v, pc, dmix, sink, pool_w, pool_scale, pband, ptband, name, carry=None):
    s = q.shape[0]
    nb = s // BLK
    nq = _attn_blocks_per_step(nb)

    def body(sink_ref, q_ref, k0, k1, k2, v0, v1, v2, p0, p1, p2, da_ref, d0, d1, d2, pw_ref, ps_ref, pb_ref, ptb_ref,
             dq_ref, dk_ref, dv_ref, dpc_ref, dsink_ref, dpw_ref, dps_ref):
        @pl.when(pl.program_id(0) == 0)
        def _():
            dsink_ref[...] = jnp.zeros_like(dsink_ref)
            dpw_ref[...] = jnp.zeros_like(dpw_ref)
            dps_ref[...] = jnp.zeros_like(dps_ref)

        kall = jnp.concatenate([k0[...], k1[...], k2[...]], axis=0)
        vall = jnp.concatenate([v0[...], v1[...], v2[...]], axis=0)
        pall = jnp.concatenate([p0[...], p1[...], p2[...]], axis=0).astype(BF16)
        dpall = jnp.concatenate([d0[...], d1[...], d2[...]], axis=0)
        lo = lax.broadcasted_iota(jnp.int32, (3 * BLK, KV_W), 1) < HEAD_DIM
        for j in range(nq):
            n = pl.program_id(0) * nq + j
            rows, band = slice(j * BLK, (j + 1) * BLK), slice(j * BLK, (j + 3) * BLK)
            valid = _band_mask(n, nb, True)
            kb, vb, qb = kall[band], vall[band], q_ref[rows, :]
            qs = qb * SCORE_SCALE
            da = da_ref[rows, :].astype(BF16)
            dk_fold, dv_fold = [], []
            for kh in range(N_KV):
                kbd, vbd = _block_diag(kb, kh), _block_diag(vb, kh)
                q2, do2 = _stack_pairs(qb, kh), _stack_pairs(da, kh)
                sc_t = _dot_nt(kbd, _stack_pairs(qs, kh))
                dp_t = _dot_nt(vbd, do2)
                p_t, ds_t = [], []
                for half in range(2):
                    keys = slice(half * 3 * BLK, (half + 1) * 3 * BLK)
                    p, ps = _softmax_sink(sc_t[keys], valid, _sink_of(sink_ref, kh, half, 1), 0)
                    delta = jnp.sum(p * dp_t[keys], axis=0, keepdims=True)
                    p_t.append(p.astype(BF16))
                    ds_t.append((p * (dp_t[keys] - delta)).astype(BF16))
                    dsk = -ps * delta
                    for pair in range(2):
                        h = GROUP * kh + 2 * pair + half
                        part = jnp.sum(dsk[:, pair * BLK:(pair + 1) * BLK], axis=1, keepdims=True)
                        dsink_ref[h:h + 1, :] += jnp.broadcast_to(part, (1, 128))
                p_t = jnp.concatenate(p_t, axis=0)
                ds_t = jnp.concatenate(ds_t, axis=0)
                dq2 = _dot_tn(ds_t, kbd) * SCORE_SCALE
                dq_ref[rows, (2 * kh) * 128:(2 * kh + 1) * 128] = dq2[:BLK]
                dq_ref[rows, (2 * kh + 1) * 128:(2 * kh + 2) * 128] = dq2[BLK:]
                dk_fold.append(_fold_diag(_dot_nn(ds_t, q2)) * SCORE_SCALE)
                dv_fold.append(_fold_diag(_dot_nn(p_t, do2)))
            dk_all = jnp.where(lo, dk_fold[0], dk_fold[1])
            dv_all = jnp.where(lo, dv_fold[0], dv_fold[1])
            for t in range(3):
                dk_ref[j, t] = dk_all[t * BLK:(t + 1) * BLK]
                dv_ref[j, t] = dv_all[t * BLK:(t + 1) * BLK]
            ext, dpe = pall[band], dpall[band]
            dpc_cur = dpall[(j + 1) * BLK:(j + 2) * BLK]
            var = _variant_index(n, nb)
            for gi in range(POOL_G):
                gsl = slice(gi * POOL_GW, (gi + 1) * POOL_GW)
                wg = pw_ref[gi].astype(BF16)
                sc = ps_ref[:, gsl]
                dgb = _dot_nn(pb_ref[var, gi], ext[:, gsl]).astype(BF16)
                yg = _dot_nn(dgb, wg)
                dps_ref[:, gsl] += jnp.sum(dpc_cur[:, gsl] * yg, axis=0, keepdims=True)
                dpw_ref[gi] += _dot_tn(dgb, (dpc_cur[:, gsl] * sc).astype(BF16))
                dd = _dot_nt((dpe[:, gsl] * sc).astype(BF16), wg)
                dpc_ref[rows, gsl] = _dot_nn(ptb_ref[var, gi], dd.astype(BF16))

    fixed = lambda shape: pl.BlockSpec(shape, lambda m: (0,) * len(shape))
    return _call(
        body, name, (nb // nq,),
        [pl.BlockSpec(memory_space=pltpu.SMEM), pl.BlockSpec((nq * BLK, ATTN_W), lambda m: (m, 0)),
         *_band_specs(nq, nb, KV_W), *_band_specs(nq, nb, KV_W), *_band_specs(nq, nb, POOL_W),
         pl.BlockSpec((nq * BLK, ATTN_W), lambda m: (m, 0)), *_band_specs(nq, nb, POOL_W, 1),
         _whole((POOL_G, POOL_GW, POOL_GW)), _whole((1, POOL_W)), _whole(pband.shape), _whole(ptband.shape)],
        [pl.BlockSpec((nq * BLK, ATTN_W), lambda m: (m, 0)),
         pl.BlockSpec((nq, 3, BLK, KV_W), lambda m: (m, 0, 0, 0)),
         pl.BlockSpec((nq, 3, BLK, KV_W), lambda m: (m, 0, 0, 0)),
         pl.BlockSpec((nq * BLK, POOL_W), lambda m: (m, 0)),
         fixed((N_HEADS, 128)), fixed((POOL_G, POOL_GW, POOL_GW)), fixed((1, POOL_W))],
        [jax.ShapeDtypeStruct((s, ATTN_W), F32), jax.ShapeDtypeStruct((nb, 3, BLK, KV_W), F32),
         jax.ShapeDtypeStruct((nb, 3, BLK, KV_W), F32), jax.ShapeDtypeStruct((s, POOL_W), F32),
         jax.ShapeDtypeStruct((N_HEADS, 128), F32),
         jax.ShapeDtypeStruct((POOL_G, POOL_GW, POOL_GW), F32), jax.ShapeDtypeStruct((1, POOL_W), F32)],
        (sink, q, k, k, k, v, v, v, pc, pc, pc, dmix, dmix, dmix, dmix, pool_w, pool_scale, pband, ptband),
        ("arbitrary",), carry)


def _mix_in_bwd(h, dh, g, win_t, dq, dkp, dvp, dpc, tabs, name, tm, carry=None):
    s, d = h.shape
    nb = s // BLK
    nt = tm // BLK
    n_in = win_t.shape[0]

    def band_sum(n, before, own, after, prev_last, next_first):
        lo = (n > 0).astype(F32)
        hi = (n < s // tm - 1).astype(F32)
        blocks = []
        for b in range(nt):
            from_prev = prev_last[...] * lo if b == 0 else before[b - 1]
            from_next = next_first[...] * hi if b == nt - 1 else after[b + 1]
            blocks.append(from_prev + own[b] + from_next)
        return jnp.concatenate(blocks, axis=0)

    def body(h_ref, dh_ref, g_ref, w_ref, dq_ref, k2, k1, k0, kp, kn, v2, v1, v0, vp, vn, dpc_ref, c_ref, sa_ref,
             sb_ref, o_ref, du_ref, dg_ref):
        n = pl.program_id(0)
        dk = band_sum(n, k2, k1, k0, kp, kn)
        dv = band_sum(n, v2, v1, v0, vp, vn)
        c, sa, sb = c_ref[...], sa_ref[...], sb_ref[...]
        du = jnp.concatenate([_rope_bwd(dq_ref[...], c, sa, sb, ATTN_W // 128), _rope_bwd(dk, c, sa, sb, 1), dv,
                              dpc_ref[...]], axis=1)
        du_ref[...] = du.T.astype(BF16)
        dn = _dot_nn(du.astype(BF16), w_ref[...])
        x = h_ref[...]
        r = lax.rsqrt(jnp.mean(x * x, axis=-1, keepdims=True) + EPS)
        xhat = x * r
        dxhat = dn * g_ref[...]
        o_ref[...] = dh_ref[...] + r * (dxhat - xhat * jnp.mean(dxhat * xhat, axis=-1, keepdims=True))

        @pl.when(n == 0)
        def _():
            dg_ref[...] = jnp.zeros_like(dg_ref)

        dg_ref[...] += jnp.sum(dn * xhat, axis=0, keepdims=True)

    row = lambda w: pl.BlockSpec((tm, w), lambda n: (n, 0))
    slot = lambda t: pl.BlockSpec((nt, None, BLK, KV_W), lambda n, t=t: (n, t, 0, 0))
    parts = [slot(2), slot(1), slot(0),
             pl.BlockSpec((None, None, BLK, KV_W), lambda n: (jnp.maximum(nt * n - 1, 0), 2, 0, 0)),
             pl.BlockSpec((None, None, BLK, KV_W), lambda n: (jnp.minimum(nt * n + nt, nb - 1), 0, 0, 0))]
    return _call(
        body, name, (s // tm,),
        [row(d), row(d), _whole((1, d)), _whole((n_in, d)), row(ATTN_W), *parts, *parts, row(POOL_W),
         row(128), row(128), row(128)],
        [row(d), pl.BlockSpec((n_in, tm), lambda n: (0, n)), pl.BlockSpec((1, d), lambda n: (0, 0))],
        [jax.ShapeDtypeStruct((s, d), F32), jax.ShapeDtypeStruct((n_in, s), BF16), jax.ShapeDtypeStruct((1, d), F32)],
        (h, dh, g, win_t, dq, *[dkp] * 5, *[dvp] * 5, dpc, *tabs), ("arbitrary",), carry)


def _adam_math(w, g, m, v):
    m = ADAM_B1 * m + (1.0 - ADAM_B1) * g
    v = ADAM_B2 * v + (1.0 - ADAM_B2) * (g * g)
    m_hat = m / (1.0 - ADAM_B1 ** ADAM_STEP)
    v_hat = v / (1.0 - ADAM_B2 ** ADAM_STEP)
    delta = -ADAM_LR * (m_hat / (jnp.sqrt(v_hat) + ADAM_EPS) + ADAM_WD * w)
    return delta, m, v


def _adam_small(w, parts, late, m, v, name):
    rows, cols = w.shape

    def body(w_ref, p_ref, l_ref, m_ref, v_ref, g_ref, d_ref, nm_ref, nv_ref):
        g, gl = p_ref[0], l_ref[0]
        for k in range(1, N_DEV):
            g = g + p_ref[k]
            gl = gl + l_ref[k]
        g_ref[...] = g
        g_ref[SMALL_NORM1:SMALL_NORM1 + 8, :] = g[SMALL_NORM1:SMALL_NORM1 + 8] + gl
        d_ref[...], nm_ref[...], nv_ref[...] = _adam_math(w_ref[...], g_ref[...], m_ref[...], v_ref[...])

    return pl.pallas_call(
        body, name=name, out_shape=[jax.ShapeDtypeStruct((rows, cols), F32)] * 4,
    )(w, parts, late, m, v)


SMALL_NORM1 = 512


def _pack_small(norm1, normm, norm2, normf, sink, pool_w, pool_scale, loss=None):
    scale_rows = jnp.pad(pool_scale.reshape(4, 128), ((0, 4), (0, 0)))
    last_rows = jnp.pad(sink.reshape(1, N_HEADS), ((0, 7), (0, 128 - N_HEADS)))
    if loss is not None:
        last_rows = last_rows + jnp.pad(loss.reshape(1, 1), ((1, 6), (0, 127)))
    return jnp.concatenate([pool_w.reshape(512, 128), norm1.reshape(8, 128), normm.reshape(8, 128),
                            norm2.reshape(8, 128), normf.reshape(8, 128), scale_rows, last_rows], axis=0)


def _unpack_small(p):
    return dict(pool_w=p[:512].reshape(1, POOL_G, POOL_GW, POOL_GW), ffn1_norm=p[512:520].reshape(1, 1024),
                mix_norm=p[520:528].reshape(1, 1024), ffn2_norm=p[528:536].reshape(1, 1024),
                final_norm=p[536:544].reshape(1024), pool_scale=p[544:548].reshape(1, POOL_W),
                sink_logits=p[552, :N_HEADS].reshape(1, N_HEADS), loss=p[553, 0])


def kernel(x, ffn1_norm, ffn1_w_gate, ffn1_w_up, ffn1_w_down, mix_norm, w_in, sink_logits, pool_w, pool_scale, w_out, ffn2_norm, ffn2_w_gate, ffn2_w_up, ffn2_w_down, final_norm, loss_target, m_ffn1_norm, m_ffn1_w_gate, m_ffn1_w_up, m_ffn1_w_down, m_mix_norm, m_w_in, m_sink_logits, m_pool_w, m_pool_scale, m_w_out, m_ffn2_norm, m_ffn2_w_gate, m_ffn2_w_up, m_ffn2_w_down, m_final_norm, v_ffn1_norm, v_ffn1_w_gate, v_ffn1_w_up, v_ffn1_w_down, v_mix_norm, v_w_in, v_sink_logits, v_pool_w, v_pool_scale, v_w_out, v_ffn2_norm, v_ffn2_w_gate, v_ffn2_w_up, v_ffn2_w_down, v_final_norm):
    s, d = x.shape[1], x.shape[2]
    tm = min(512, s)
    tm_bwd = min(256, s)
    pos = jnp.stack([lax.axis_index("x"), lax.axis_index("y"), lax.axis_index("c")]).astype(jnp.int32)

    t_bf = lambda w: w[0].T.astype(BF16)
    full = lambda a: a.reshape(N_DEV * a.shape[1], d)
    (wg1,) = map(full, _run_exchange(_AllGather([t_bf(ffn1_w_gate)]), "gather_ffn1_gate"))

    tabs = _rope_tables(s)
    pband, ptband = _pool_tables(s)
    g1, gm, g2, gf = ffn1_norm, mix_norm, ffn2_norm, final_norm.reshape(1, d)

    x0 = x[0]
    n1, gate1, wu1 = _ffn_gate(x0, g1, wg1, "ffn1_gate", tm, carry=_AllGather([t_bf(ffn1_w_up)]))
    up1, act1_t, wd1, win_t = _ffn_up(n1, gate1, full(wu1), "ffn1_up", tm,
                                            carry=_AllGather([ffn1_w_down[0].astype(BF16), t_bf(w_in)]))
    wu1, wd1, win_t = full(wu1), full(wd1), full(win_t)
    h1, q, k, v, pc, n2, wout, wg2 = _ffn_down_mix_in(
        x0, act1_t, wd1, gm, win_t, tabs, "ffn1_down_mix_in", tm,
        carry=_AllGather([w_out[0].astype(BF16), t_bf(ffn2_w_gate)]))
    wout, wg2 = full(wout), full(wg2)
    h2, mix_t, *gathered = _attn_pool_fwd(h1, wout, q, k, v, pc, sink_logits, pool_w[0], pool_scale, pband,
                                          "attn_pool_fwd",
                                          carry=_AllGather([t_bf(ffn2_w_up), ffn2_w_down[0].astype(BF16)]))
    wu2, wd2 = map(full, gathered)
    dh3, n3, gate2, up2, act2_t, loss_part, dgf = _ffn_fwd(h2, g2, wg2, wu2, wd2, "ffn2_fwd", tm,
                                                           head=(gf, loss_target[0]))

    sum1, recv2 = {}, {}

    def stage2(keys):
        return _RsStage2([sum1[key][1] for key in keys])

    dh2, dg2, dgate2_t, dup2_t, dhh3, dmix, dh2b = _ffn_bwd(h2, dh3, gate2, up2, g2, wg2, wu2, wd2, wout, "ffn2_bwd",
                                                            tm_bwd)
    both = _wgrad_rs1(pos, [dgate2_t, dup2_t], n3, "wgrad_gate_up2")
    sum1["g2"], sum1["u2"] = both[:2], both[2:]
    sum1["d2"] = _wgrad_rs1(pos, [act2_t], dhh3, "wgrad_down2")
    sum1["out"] = _wgrad_rs1(pos, [mix_t], dh2b, "wgrad_out")
    dq, dkp, dvp, dpc, dsink, dpw, dps, *r2 = _attn_pool_bwd(
        q, k, v, pc, dmix, sink_logits, pool_w[0], pool_scale, pband, ptband, "attn_pool_bwd",
        carry=stage2(["g2", "u2", "d2"]))
    recv2.update(zip(["g2", "u2", "d2"], r2))
    dh1, du_t, dgm, recv2["out"] = _mix_in_bwd(h1, dh2, gm, win_t, dq, dkp, dvp, dpc, tabs, "mix_in_bwd", tm,
                                               carry=stage2(["out"]))
    sum1["in"] = _wgrad_rs1(pos, [du_t], n2, "wgrad_in")
    small_part = _pack_small(jnp.zeros_like(dgm), dgm, dg2, dgf, dsink[:, 0], dpw, dps, loss_part)
    dgate1_t, dup1_t, dhh1, recv2["in"], small_all = _ffn_bwd_gates(
        dh1, gate1, up1, wd1, "ffn1_bwd_gates", tm, carry=_Both(stage2(["in"]), _AllGather([small_part])))
    sum1["g1"] = _wgrad_rs1(pos, [dgate1_t], n1, "wgrad_gate1")
    *sum1["u1"], recv2["g1"] = _wgrad_rs1(pos, [dup1_t], n1, "wgrad_up1", carry=stage2(["g1"]))
    *sum1["d1"], recv2["u1"] = _wgrad_rs1(pos, [act1_t], dhh1, "wgrad_down1", carry=stage2(["u1"]))
    dx, dg1, recv2["d1"] = _ffn_bwd_input(x0, dh1, dgate1_t, dup1_t, g1, wg1, wu1, "ffn1_bwd_input", tm,
                                          carry=stage2(["d1"]))

    (dg1_all,) = _run_exchange(_DirectGather([dg1.reshape(8, 128)]), "gather_norm1_grad")
    pk = lambda a, b, c_, e, s_, pw_, psc: _pack_small(a, b, c_, e, s_[0], pw_[0], psc)
    small_w = pk(ffn1_norm, mix_norm, ffn2_norm, final_norm, sink_logits, pool_w, pool_scale)
    small_m = pk(m_ffn1_norm, m_mix_norm, m_ffn2_norm, m_final_norm, m_sink_logits, m_pool_w, m_pool_scale)
    small_v = pk(v_ffn1_norm, v_mix_norm, v_ffn2_norm, v_final_norm, v_sink_logits, v_pool_w, v_pool_scale)
    sg, sd, sm, sv = [_unpack_small(a)
                      for a in _adam_small(small_w, small_all, dg1_all, small_m, small_v, "adam_small")]

    big = {}
    keys = ["g1", "u1", "d1", "g2", "u2", "d2", "in", "out"]
    names = ["ffn1_w_gate", "ffn1_w_up", "ffn1_w_down", "ffn2_w_gate", "ffn2_w_up", "ffn2_w_down", "w_in", "w_out"]
    transposed = [True, True, False, True, True, False, True, False]
    ws = [ffn1_w_gate, ffn1_w_up, ffn1_w_down, ffn2_w_gate, ffn2_w_up, ffn2_w_down, w_in, w_out]
    ms = [m_ffn1_w_gate, m_ffn1_w_up, m_ffn1_w_down, m_ffn2_w_gate, m_ffn2_w_up, m_ffn2_w_down, m_w_in, m_w_out]
    vs = [v_ffn1_w_gate, v_ffn1_w_up, v_ffn1_w_down, v_ffn2_w_gate, v_ffn2_w_up, v_ffn2_w_down, v_w_in, v_w_out]
    for key, nm, tr, w, m, vv in zip(keys, names, transposed, ws, ms, vs):
        view = (lambda a: jnp.swapaxes(a, 1, 2)[0]) if tr else (lambda a: a[0])
        back = (lambda a: jnp.swapaxes(a[None], 1, 2)) if tr else (lambda a: a[None])
        res = _rs_sum2_adam(pos, sum1[key][0], recv2[key], view(w), view(m), view(vv), "adam_" + nm)
        big[nm] = tuple(back(a) for a in res)

    loss = sg["loss"]
    all_names = ["ffn1_norm", "ffn1_w_gate", "ffn1_w_up", "ffn1_w_down", "mix_norm", "w_in", "sink_logits", "pool_w",
                 "pool_scale", "w_out", "ffn2_norm", "ffn2_w_gate", "ffn2_w_up", "ffn2_w_down", "final_norm"]
    outs = [loss, dx[None]]
    for idx, src in enumerate((sg, sd, sm, sv)):
        for nm in all_names:
            outs.append(big[nm][idx] if nm in big else src[nm])
    return tuple(outs)
```

```python
import functools

import jax
import jax.numpy as jnp
import numpy as np
from jax import lax
from jax.experimental import pallas as pl
from jax.experimental.pallas import tpu as pltpu

F32 = jnp.float32
BF16 = jnp.bfloat16
MESH = pl.DeviceIdType.MESH
N_DEV = 8

EPS = 1e-6
HEAD_DIM = 64
N_HEADS = 8
N_KV = 2
GROUP = N_HEADS // N_KV
ATTN_W = N_HEADS * HEAD_DIM
KV_W = N_KV * HEAD_DIM
POOL_W = 512
POOL_G = 4
POOL_GW = POOL_W // POOL_G
POOL_WINDOWS = (2, 4, 8, 16)
BLK = 128
ROT = 16
ROPE_THETA = 500000.0
SCORE_SCALE = HEAD_DIM ** -0.5

ADAM_LR, ADAM_B1, ADAM_B2, ADAM_EPS, ADAM_WD, ADAM_STEP = 0.001, 0.9, 0.999, 1e-08, 0.01, 10

VMEM_LIMIT = 56 * 1024 * 1024


def _cparams(sem=None, **kw):
    if sem is not None:
        kw["dimension_semantics"] = sem
    return pltpu.CompilerParams(vmem_limit_bytes=VMEM_LIMIT, **kw)


def _whole(shape):
    nd = len(shape)
    return pl.BlockSpec(shape, lambda *_: (0,) * nd, pipeline_mode=pl.Buffered(1))


def _sigmoid(z):
    return 1.0 / (1.0 + jnp.exp(-z))


def _dot_nt(a, b):
    return lax.dot_general(a, b, (((1,), (1,)), ((), ())), preferred_element_type=F32)


def _dot_nn(a, b):
    return lax.dot_general(a, b, (((1,), (0,)), ((), ())), preferred_element_type=F32)


def _dot_tn(a, b):
    return lax.dot_general(a, b, (((0,), (0,)), ((), ())), preferred_element_type=F32)


def _rope_tables(s):
    inv_freq = ROPE_THETA ** (-np.arange(0, ROT, 2, dtype=np.float64) / ROT)
    ang = np.arange(s, dtype=np.float64)[:, None] * inv_freq[None, :]
    c = np.ones((s, HEAD_DIM)); sa = np.zeros((s, HEAD_DIM)); sb = np.zeros((s, HEAD_DIM))
    c[:, :8] = np.cos(ang); c[:, 8:16] = np.cos(ang)
    sa[:, :8] = -np.sin(ang)
    sb[:, 8:16] = np.sin(ang)
    t = lambda a: jnp.asarray(np.tile(a, (1, 2)).astype(np.float32))
    return t(c), t(sa), t(sb)


def _pool_weight(gi, t, s_pos, s):
    half = POOL_WINDOWS[gi] // 2

    def win(lo, hi):
        a = np.clip(lo, 0, s); b = np.clip(hi + 1, 0, s)
        inside = (s_pos >= a) & (s_pos < b)
        return inside / np.maximum(b - a, 1)

    w = 0.5 * (win(t - half, t + half - 1) + win(t - half + 1, t + half)) - (t == s_pos)
    return w * ((t >= 0) & (t < s) & (s_pos >= 0) & (s_pos < s))


def _pool_tables(s):
    nb = s // BLK
    fwd = np.zeros((3, POOL_G, BLK, 3 * BLK), np.float32)
    bwd = np.zeros((3, POOL_G, BLK, 3 * BLK), np.float32)
    for vi, n in enumerate((0, 1 if nb > 2 else 0, nb - 1)):
        i = n * BLK + np.arange(BLK)[:, None]
        j = (n - 1) * BLK + np.arange(3 * BLK)[None, :]
        for gi in range(POOL_G):
            fwd[vi, gi] = _pool_weight(gi, i, j, s)
            bwd[vi, gi] = _pool_weight(gi, j, i, s)
    return jnp.asarray(fwd, dtype=BF16), jnp.asarray(bwd, dtype=BF16)


def _variant_index(n, nb):
    return jnp.where(n == 0, 0, jnp.where(n == nb - 1, 2, 1))


class _Exchange:
    inputs = ()
    out_shapes = ()
    sems = ()

    def start(self, srcs, outs, sems):
        raise NotImplementedError

    def middle(self, srcs, outs, sems):
        pass

    def finish(self, srcs, outs, sems):
        raise NotImplementedError


class _AllGather(_Exchange):
    def __init__(self, arrays):
        n = len(arrays)
        self.inputs = list(arrays)
        self.out_shapes = [jax.ShapeDtypeStruct((N_DEV,) + a.shape, a.dtype) for a in arrays]
        self.sems = [pltpu.SemaphoreType.DMA((n, 8)), pltpu.SemaphoreType.DMA((n, 8)), pltpu.SemaphoreType.DMA((n,))]

    def _parts(self, srcs, outs, sems):
        send_sems, recv_sems, local_sems = sems
        n = len(srcs)
        x, y, c = lax.axis_index("x"), lax.axis_index("y"), lax.axis_index("c")
        me, sibling, xn, yn, diag = (x, y, c), (x, y, 1 - c), (1 - x, y, c), (x, 1 - y, c), (1 - x, 1 - y, c)

        def place(a, dev, half=None):
            block = outs[a].at[4 * dev[0] + 2 * dev[1] + dev[2]]
            if half is None:
                return block
            r2 = outs[a].shape[1] // 2
            return block.at[pl.ds(half * r2, r2)]

        def copy(a, k, dev, to, half=None, src=None):
            where = place(a, dev, half)
            return pltpu.make_async_remote_copy(
                src_ref=where if src is None else src, dst_ref=where, send_sem=send_sems.at[a, k],
                recv_sem=recv_sems.at[a, k], device_id=to, device_id_type=MESH)

        def other(dev):
            return (dev[0], dev[1], 1 - dev[2])

        class Parts:
            mine = staticmethod(lambda: [pltpu.make_async_copy(srcs[a], place(a, me), local_sems.at[a])
                                         for a in range(n)])
            own = staticmethod(lambda: [copy(a, k, me, to, src=srcs[a]) for a in range(n)
                                        for k, to in ((0, sibling), (1, xn), (2, yn))])
            relay = staticmethod(lambda a: [copy(a, 3, xn, yn, half=0), copy(a, 4, yn, xn, half=1),
                                            copy(a, 5, xn, sibling), copy(a, 6, yn, sibling)])
            last = staticmethod(lambda a: copy(a, 7, diag, sibling))
            from_x = staticmethod(lambda a: copy(a, 1, xn, me))
            from_y = staticmethod(lambda a: copy(a, 2, yn, me))
            diag_halves = staticmethod(lambda a: [copy(a, 3, diag, me, half=0), copy(a, 4, diag, me, half=1)])
            from_sibling = staticmethod(lambda a: [copy(a, 0, sibling, me), copy(a, 5, other(xn), me),
                                                   copy(a, 6, other(yn), me), copy(a, 7, other(diag), me)])

        return n, Parts

    def start(self, srcs, outs, sems):
        _, p = self._parts(srcs, outs, sems)
        for cp in p.mine() + p.own():
            cp.start()

    def middle(self, srcs, outs, sems):
        n, p = self._parts(srcs, outs, sems)
        for a in range(n):
            p.from_x(a).wait_recv()
            p.from_y(a).wait_recv()
            for cp in p.relay(a):
                cp.start()

    def finish(self, srcs, outs, sems):
        n, p = self._parts(srcs, outs, sems)
        for a in range(n):
            for cp in p.diag_halves(a):
                cp.wait_recv()
            p.last(a).start()
        for a in range(n):
            for cp in p.from_sibling(a):
                cp.wait_recv()
        for cp in p.own() + [cp for a in range(n) for cp in p.relay(a) + [p.last(a)]]:
            cp.wait_send()
        for cp in p.mine():
            cp.wait()


class _RsStage2(_Exchange):
    def start(self, srcs, outs, sems):
        for cp in self._copies(srcs, outs, sems):
            cp.start()

    def finish(self, srcs, outs, sems):
        copies = self._copies(srcs, outs, sems)
        for cp in copies:
            cp.wait_recv()
        for cp in copies:
            cp.wait_send()


    def __init__(self, pbs):
        n = len(pbs)
        self.inputs = list(pbs)
        self.out_shapes = [jax.ShapeDtypeStruct((3,) + p.shape[1:], p.dtype) for p in pbs]
        self.sems = [pltpu.SemaphoreType.DMA((n, 3)), pltpu.SemaphoreType.DMA((n, 3))]

    def _copies(self, srcs, outs, sems):
        send_sems, recv_sems = sems
        x, y, c = lax.axis_index("x"), lax.axis_index("y"), lax.axis_index("c")
        chips = [(1 - x, y), (x, 1 - y), (1 - x, 1 - y)]
        return [pltpu.make_async_remote_copy(
            src_ref=srcs[a].at[2 * chip[0] + chip[1]], dst_ref=outs[a].at[j], send_sem=send_sems.at[a, j],
            recv_sem=recv_sems.at[a, j], device_id=(*chip, c), device_id_type=MESH)
            for a in range(len(srcs)) for j, chip in enumerate(chips)]


class _DirectGather(_Exchange):
    def __init__(self, arrays):
        n = len(arrays)
        self.inputs = list(arrays)
        self.out_shapes = [jax.ShapeDtypeStruct((N_DEV,) + a.shape, a.dtype) for a in arrays]
        self.sems = [pltpu.SemaphoreType.DMA((n, 7)), pltpu.SemaphoreType.DMA((n, 7)), pltpu.SemaphoreType.DMA((n,))]

    def _copies(self, srcs, outs, sems):
        send_sems, recv_sems, local_sems = sems
        x, y, c = lax.axis_index("x"), lax.axis_index("y"), lax.axis_index("c")
        me = 4 * x + 2 * y + c
        remote, local = [], []
        for a in range(len(srcs)):
            local.append(pltpu.make_async_copy(srcs[a], outs[a].at[me], local_sems.at[a]))
            for k in range(1, N_DEV):
                peer = (x ^ (k >> 2), y ^ ((k >> 1) & 1), c ^ (k & 1))
                remote.append(pltpu.make_async_remote_copy(
                    src_ref=srcs[a], dst_ref=outs[a].at[me], send_sem=send_sems.at[a, k - 1],
                    recv_sem=recv_sems.at[a, k - 1], device_id=peer, device_id_type=MESH))
        return remote, local

    def start(self, srcs, outs, sems):
        remote, local = self._copies(srcs, outs, sems)
        for cp in local + remote:
            cp.start()

    def finish(self, srcs, outs, sems):
        remote, local = self._copies(srcs, outs, sems)
        for cp in remote:
            cp.wait_recv()
        for cp in remote:
            cp.wait_send()
        for cp in local:
            cp.wait()


class _Both(_Exchange):
    def __init__(self, a, b):
        self.a, self.b = a, b
        self.inputs = list(a.inputs) + list(b.inputs)
        self.out_shapes = list(a.out_shapes) + list(b.out_shapes)
        self.sems = list(a.sems) + list(b.sems)

    def _split(self, srcs, outs, sems):
        na, oa, sa = len(self.a.inputs), len(self.a.out_shapes), len(self.a.sems)
        return (srcs[:na], outs[:oa], sems[:sa]), (srcs[na:], outs[oa:], sems[sa:])

    def start(self, srcs, outs, sems):
        pa, pb = self._split(srcs, outs, sems)
        self.a.start(*pa)
        self.b.start(*pb)

    def middle(self, srcs, outs, sems):
        pa, pb = self._split(srcs, outs, sems)
        self.a.middle(*pa)
        self.b.middle(*pb)

    def finish(self, srcs, outs, sems):
        pa, pb = self._split(srcs, outs, sems)
        self.a.finish(*pa)
        self.b.finish(*pb)


_ANY = pl.BlockSpec(memory_space=pl.ANY)


def _run_exchange(ex, name):
    n_in, n_out = len(ex.inputs), len(ex.out_shapes)

    def body(*refs):
        srcs, outs, sems = refs[:n_in], refs[n_in:n_in + n_out], refs[n_in + n_out:]
        ex.start(srcs, outs, sems)
        ex.middle(srcs, outs, sems)
        ex.finish(srcs, outs, sems)

    return pl.pallas_call(
        body, name=name, out_shape=list(ex.out_shapes), in_specs=[_ANY] * n_in, out_specs=[_ANY] * n_out,
        scratch_shapes=list(ex.sems),
    )(*ex.inputs)


CARRY_MIDDLE = 0.55


def _call(body, name, grid, in_specs, out_specs, out_shape, args, sem, carry=None, middle=CARRY_MIDDLE, scratch=()):
    if carry is None:
        return pl.pallas_call(functools.partial(body), name=name, grid=grid, in_specs=in_specs, out_specs=out_specs,
                              out_shape=out_shape, scratch_shapes=list(scratch), compiler_params=_cparams(sem))(*args)
    n_in, n_out = len(in_specs), len(out_specs)
    nc_in, nc_out = len(carry.inputs), len(carry.out_shapes)

    def carried(*refs):
        ins = refs[:n_in]
        c_in = refs[n_in:n_in + nc_in]
        outs = refs[n_in + nc_in:n_in + nc_in + n_out]
        c_out = refs[n_in + nc_in + n_out:n_in + nc_in + n_out + nc_out]
        own = refs[n_in + nc_in + n_out + nc_out:n_in + nc_in + n_out + nc_out + len(scratch)]
        sems = refs[n_in + nc_in + n_out + nc_out + len(scratch):]
        ids = [pl.program_id(i) for i in range(len(grid))]
        is_first = functools.reduce(jnp.logical_and, [i == 0 for i in ids])
        is_last = functools.reduce(jnp.logical_and, [i == g - 1 for i, g in zip(ids, grid)])
        @pl.when(is_first)
        def _():
            carry.start(c_in, c_out, sems)

        if middle is not None:
            @pl.when(functools.reduce(jnp.logical_and, [ids[0] == round(middle * (grid[0] - 1))]
                                      + [i == 0 for i in ids[1:]]))
            def _():
                carry.middle(c_in, c_out, sems)

        body(*ins, *outs, *own)

        @pl.when(is_last)
        def _():
            if middle is None:
                carry.middle(c_in, c_out, sems)
            carry.finish(c_in, c_out, sems)

    return pl.pallas_call(
        carried, name=name, grid=grid, in_specs=list(in_specs) + [_ANY] * nc_in,
        out_specs=list(out_specs) + [_ANY] * nc_out, out_shape=list(out_shape) + list(carry.out_shapes),
        scratch_shapes=list(scratch) + list(carry.sems), compiler_params=_cparams(sem))(*args, *carry.inputs)


def _rs_sum2_adam(pos, p, r2, w, m, v, name):
    _, rows, d = p.shape
    tr = rows // 2 if rows % 16 == 0 else rows

    def body(pos_ref, p_ref, r_ref, w_ref, m_ref, v_ref, g_ref, d_ref, nm_ref, nv_ref):
        r = r_ref[...].astype(F32)
        g = ((p_ref[...] + r[0]) + r[1]) + r[2]
        g_ref[...] = g
        d_ref[...], nm_ref[...], nv_ref[...] = _adam_math(w_ref[...], g, m_ref[...], v_ref[...])

    blk = pl.BlockSpec((tr, d), lambda i, pos_ref: (i, 0))
    return pl.pallas_call(
        body, name=name,
        grid_spec=pltpu.PrefetchScalarGridSpec(
            num_scalar_prefetch=1, grid=(rows // tr,),
            in_specs=[pl.BlockSpec((None, tr, d), lambda i, pos_ref: (2 * pos_ref[0] + pos_ref[1], i, 0)),
                      pl.BlockSpec((3, tr, d), lambda i, pos_ref: (0, i, 0)), blk, blk, blk],
            out_specs=[blk] * 4),
        out_shape=[jax.ShapeDtypeStruct((rows, d), F32)] * 4,
        compiler_params=_cparams(("arbitrary",)),
    )(pos, p, r2, w, m, v)


def _ffn_chunk(f):
    for cand in (256, 128):
        if f % cand == 0:
            return cand
    return f


def _loss_head(x, gg, target, loss_ref, dg_ref):
    @pl.when(pl.program_id(0) == 0)
    def _():
        loss_ref[...] = jnp.zeros_like(loss_ref)
        dg_ref[...] = jnp.zeros_like(dg_ref)

    r = lax.rsqrt(jnp.mean(x * x, axis=-1, keepdims=True) + EPS)
    xhat = x * r
    e = xhat * gg - target
    loss_ref[...] += 0.5 * jnp.sum(jnp.mean(e * e, axis=-1, keepdims=True), axis=0, keepdims=True)
    dy = e * (1.0 / x.shape[-1])
    dg_ref[...] += jnp.sum(dy * xhat, axis=0, keepdims=True)
    dxhat = dy * gg
    return r * (dxhat - xhat * jnp.mean(dxhat * xhat, axis=-1, keepdims=True))


def _ffn_gate(h, g, wg_t, name, tm, carry=None):
    s, d = h.shape
    f = wg_t.shape[0]

    def body(h_ref, g_ref, wg_ref, n_ref, gate_ref):
        x = h_ref[...]
        r = lax.rsqrt(jnp.mean(x * x, axis=-1, keepdims=True) + EPS)
        nb = (x * r * g_ref[...]).astype(BF16)
        n_ref[...] = nb
        gate_ref[...] = _dot_nt(nb, wg_ref[...]).astype(BF16)

    row = lambda w: pl.BlockSpec((tm, w), lambda i: (i, 0))
    return _call(body, name, (s // tm,), [row(d), _whole((1, d)), _whole((f, d))], [row(d), row(f)],
                 [jax.ShapeDtypeStruct((s, d), BF16), jax.ShapeDtypeStruct((s, f), BF16)], (h, g, wg_t),
                 ("arbitrary",), carry)


def _ffn_up(n, gate, wu_t, name, tm, carry=None):
    s, d = n.shape
    f = wu_t.shape[0]
    tf = _ffn_chunk(f)

    def body(n_ref, gate_ref, wu_ref, up_ref, act_t_ref):
        nb = n_ref[...]
        for j in range(f // tf):
            sl = slice(j * tf, (j + 1) * tf)
            up = _dot_nt(nb, wu_ref[sl, :])
            gate = gate_ref[:, sl].astype(F32)
            up_ref[:, sl] = up.astype(BF16)
            act_t_ref[sl, :] = (gate * _sigmoid(gate) * up).astype(BF16).T

    row = lambda w: pl.BlockSpec((tm, w), lambda i: (i, 0))
    return _call(body, name, (s // tm,), [row(d), row(f), _whole((f, d))],
                 [row(f), pl.BlockSpec((f, tm), lambda i: (0, i))],
                 [jax.ShapeDtypeStruct((s, f), BF16), jax.ShapeDtypeStruct((f, s), BF16)], (n, gate, wu_t),
                 ("arbitrary",), carry)


def _ffn_fwd(h, g, wg_t, wu_t, wd, name, tm, carry=None, head=None, middle=CARRY_MIDDLE):
    s, d = h.shape
    f = wg_t.shape[0]
    tf = _ffn_chunk(f)

    def body(h_ref, g_ref, wg_ref, wu_ref, wd_ref, *refs):
        if head is None:
            o_ref, n_ref, gate_ref, up_ref, act_t_ref, act_ref = refs
        else:
            gf_ref, t_ref, o_ref, n_ref, gate_ref, up_ref, act_t_ref, loss_ref, dgf_ref, act_ref = refs
        x = h_ref[...]
        r = lax.rsqrt(jnp.mean(x * x, axis=-1, keepdims=True) + EPS)
        nb = (x * r * g_ref[...]).astype(BF16)
        n_ref[...] = nb
        for j in range(f // tf):
            sl = slice(j * tf, (j + 1) * tf)
            gate = _dot_nt(nb, wg_ref[sl, :])
            up = _dot_nt(nb, wu_ref[sl, :])
            gate_ref[:, sl] = gate.astype(BF16)
            up_ref[:, sl] = up.astype(BF16)
            act = gate * _sigmoid(gate) * up
            act = act.astype(BF16)
            act_ref[:, sl] = act
            act_t_ref[sl, :] = act.T
        h_out = x + 0.5 * _dot_nn(act_ref[...], wd_ref[...])
        o_ref[...] = h_out if head is None else _loss_head(h_out, gf_ref[...], t_ref[...], loss_ref, dgf_ref)

    row = lambda w: pl.BlockSpec((tm, w), lambda i: (i, 0))
    in_specs = [row(d), _whole((1, d)), _whole((f, d)), _whole((f, d)), _whole((f, d))]
    out_specs = [row(d), row(d), row(f), row(f), pl.BlockSpec((f, tm), lambda i: (0, i))]
    out_shape = ([jax.ShapeDtypeStruct((s, d), F32), jax.ShapeDtypeStruct((s, d), BF16)]
                 + [jax.ShapeDtypeStruct((s, f), BF16)] * 2 + [jax.ShapeDtypeStruct((f, s), BF16)])
    args = (h, g, wg_t, wu_t, wd)
    scratch = (pltpu.VMEM((tm, f), BF16),)
    if head is not None:
        in_specs += [_whole((1, d)), row(d)]
        out_specs += [pl.BlockSpec((1, 1), lambda i: (0, 0)), pl.BlockSpec((1, d), lambda i: (0, 0))]
        out_shape += [jax.ShapeDtypeStruct((1, 1), F32), jax.ShapeDtypeStruct((1, d), F32)]
        args += tuple(head)
    return _call(body, name, (s // tm,), in_specs, out_specs, out_shape, args, ("arbitrary",), carry, middle, scratch)


def _gate_grads(dh_ref, gate_ref, up_ref, wd_ref, dgate_ref, dup_ref, dgate_t_ref, dup_t_ref, dhh_ref, tf):
    dhh = (0.5 * dh_ref[...]).astype(BF16)
    dhh_ref[...] = dhh
    for j in range(gate_ref.shape[1] // tf):
        sl = slice(j * tf, (j + 1) * tf)
        gt = gate_ref[:, sl].astype(F32)
        u = up_ref[:, sl].astype(F32)
        dact = _dot_nt(dhh, wd_ref[sl, :])
        sg = _sigmoid(gt)
        dup = dact * (gt * sg)
        dgate = dact * u * (sg * (1.0 + gt * (1.0 - sg)))
        dup, dgate = dup.astype(BF16), dgate.astype(BF16)
        if dup_ref is not None:
            dup_ref[:, sl] = dup
            dgate_ref[:, sl] = dgate
        dup_t_ref[sl, :] = dup.T
        dgate_t_ref[sl, :] = dgate.T


def _input_grad(h_ref, dh_ref, dgate_ref, dup_ref, g_ref, wg_ref, wu_ref, o_ref, dg_ref, transposed=False):
    x = h_ref[...]
    r = lax.rsqrt(jnp.mean(x * x, axis=-1, keepdims=True) + EPS)
    xhat = x * r
    dot = _dot_tn if transposed else _dot_nn
    dn = dot(dgate_ref[...], wg_ref[...]) + dot(dup_ref[...], wu_ref[...])
    dxhat = dn * g_ref[...]
    o_ref[...] = dh_ref[...] + r * (dxhat - xhat * jnp.mean(dxhat * xhat, axis=-1, keepdims=True))

    @pl.when(pl.program_id(0) == 0)
    def _():
        dg_ref[...] = jnp.zeros_like(dg_ref)

    dg_ref[...] += jnp.sum(dn * xhat, axis=0, keepdims=True)


def _ffn_bwd(h_in, dh_out, gate, up, g, wg_t, wu_t, wd, w_out, name, tm):
    s, d = h_in.shape
    f = gate.shape[1]
    w = w_out.shape[0]
    tf = _ffn_chunk(f)

    def body(h_ref, dh_ref, gate_ref, up_ref, g_ref, wg_ref, wu_ref, wd_ref, wo_ref,
             o_ref, dg_ref, dgate_t_ref, dup_t_ref, dhh_ref, dmix_ref, dhb_ref, dgate_ref, dup_ref):
        _gate_grads(dh_ref, gate_ref, up_ref, wd_ref, dgate_ref, dup_ref, dgate_t_ref, dup_t_ref, dhh_ref, tf)
        _input_grad(h_ref, dh_ref, dgate_ref, dup_ref, g_ref, wg_ref, wu_ref, o_ref, dg_ref)
        dhb = o_ref[...].astype(BF16)
        dhb_ref[...] = dhb
        dmix_ref[...] = _dot_nt(dhb, wo_ref[...])

    row = lambda c: pl.BlockSpec((tm, c), lambda i: (i, 0))
    col = pl.BlockSpec((f, tm), lambda i: (0, i))
    return pl.pallas_call(
        body, name=name, grid=(s // tm,),
        in_specs=[row(d), row(d), row(f), row(f), _whole((1, d)), _whole((f, d)), _whole((f, d)), _whole((f, d)),
                  _whole((w, d))],
        out_specs=[row(d), pl.BlockSpec((1, d), lambda i: (0, 0)), col, col, row(d), row(w), row(d)],
        out_shape=[jax.ShapeDtypeStruct((s, d), F32), jax.ShapeDtypeStruct((1, d), F32),
                   jax.ShapeDtypeStruct((f, s), BF16), jax.ShapeDtypeStruct((f, s), BF16),
                   jax.ShapeDtypeStruct((s, d), BF16), jax.ShapeDtypeStruct((s, w), F32),
                   jax.ShapeDtypeStruct((s, d), BF16)],
        scratch_shapes=[pltpu.VMEM((tm, f), BF16), pltpu.VMEM((tm, f), BF16)],
        compiler_params=_cparams(("arbitrary",)),
    )(h_in, dh_out, gate, up, g, wg_t, wu_t, wd, w_out)


def _ffn_bwd_gates(dh_out, gate, up, wd, name, tm, carry=None):
    s, d = dh_out.shape
    f = gate.shape[1]
    tf = _ffn_chunk(f)

    def body(dh_ref, gate_ref, up_ref, wd_ref, dgate_t_ref, dup_t_ref, dhh_ref):
        _gate_grads(dh_ref, gate_ref, up_ref, wd_ref, None, None, dgate_t_ref, dup_t_ref, dhh_ref, tf)

    row = lambda w: pl.BlockSpec((tm, w), lambda i: (i, 0))
    col = pl.BlockSpec((f, tm), lambda i: (0, i))
    return _call(
        body, name, (s // tm,), [row(d), row(f), row(f), _whole((f, d))], [col, col, row(d)],
        [jax.ShapeDtypeStruct((f, s), BF16)] * 2 + [jax.ShapeDtypeStruct((s, d), BF16)],
        (dh_out, gate, up, wd), ("arbitrary",), carry)


def _ffn_bwd_input(h_in, dh_out, dgate_t, dup_t, g, wg_t, wu_t, name, tm, carry=None):
    s, d = h_in.shape
    f = dgate_t.shape[0]

    row = lambda w: pl.BlockSpec((tm, w), lambda i: (i, 0))
    col = pl.BlockSpec((f, tm), lambda i: (0, i))
    return _call(
        functools.partial(_input_grad, transposed=True), name, (s // tm,),
        [row(d), row(d), col, col, _whole((1, d)), _whole((f, d)), _whole((f, d))],
        [row(d), pl.BlockSpec((1, d), lambda i: (0, 0))],
        [jax.ShapeDtypeStruct((s, d), F32), jax.ShapeDtypeStruct((1, d), F32)],
        (h_in, dh_out, dgate_t, dup_t, g, wg_t, wu_t), ("arbitrary",), carry)


def _wgrad_rs1(pos, a_ts, b, name, carry=None):
    n_a = len(a_ts)
    f, s = a_ts[0].shape
    d = b.shape[1]
    fk = f // N_DEV
    nc_in = 0 if carry is None else len(carry.inputs)
    nc_out = 0 if carry is None else len(carry.out_shapes)

    def body(pos_ref, *refs):
        a_refs, b_ref, refs = refs[:n_a], refs[n_a], refs[n_a + 1:]
        c_in = refs[:nc_in]
        outs = refs[nc_in:nc_in + 2 * n_a]
        c_out = refs[nc_in + 2 * n_a:nc_in + 2 * n_a + nc_out]
        stage, land, send_sems, recv_sems = refs[nc_in + 2 * n_a + nc_out:nc_in + 2 * n_a + nc_out + 4]
        c_sems = refs[nc_in + 2 * n_a + nc_out + 4:]
        t = pl.program_id(0)
        which, step = t // 8, t % 8
        slot = 4 * which + step % 4
        x, y, c = lax.axis_index("x"), lax.axis_index("y"), lax.axis_index("c")

        def push(k):
            return pltpu.make_async_remote_copy(src_ref=stage.at[k], dst_ref=land.at[k], send_sem=send_sems.at[k],
                                                recv_sem=recv_sems.at[k], device_id=(x, y, 1 - c), device_id_type=MESH)

        if carry is not None:
            @pl.when(t == 0)
            def _():
                carry.start(c_in, c_out, c_sems)

        for j in range(n_a):
            @pl.when(which == j)
            def _(j=j):
                g = _dot_nn(a_refs[j][...], b_ref[...])

                @pl.when(step < 4)
                def _():
                    stage[slot] = g.astype(BF16)
                    push(slot).start()

                @pl.when(step >= 4)
                def _():
                    push(slot).wait_recv()
                    p = g + land[slot].astype(F32)
                    outs[2 * j][...] = p
                    outs[2 * j + 1][...] = p.astype(BF16)

        @pl.when(t == 8 * n_a - 1)
        def _():
            for k in range(4 * n_a):
                push(k).wait_send()
            if carry is not None:
                carry.middle(c_in, c_out, c_sems)
                carry.finish(c_in, c_out, c_sems)

    def shard(j):
        def index_map(t, pos_ref):
            step = jnp.clip(t - 8 * j, 0, 7)
            return 4 * ((step % 4) // 2) + 2 * (step % 2) + jnp.where(step < 4, 1 - pos_ref[2], pos_ref[2]), 0
        return index_map

    out = lambda j: pl.BlockSpec((None, fk, d), lambda t, pos_ref, j=j: (jnp.clip(t - 8 * j - 4, 0, 3), 0, 0))
    return pl.pallas_call(
        body, name=name,
        grid_spec=pltpu.PrefetchScalarGridSpec(
            num_scalar_prefetch=1, grid=(8 * n_a,),
            in_specs=[pl.BlockSpec((fk, s), shard(j)) for j in range(n_a)]
            + [pl.BlockSpec((s, d), lambda t, pos_ref: (0, 0), pipeline_mode=pl.Buffered(1))] + [_ANY] * nc_in,
            out_specs=[out(j) for j in range(n_a) for _ in range(2)] + [_ANY] * nc_out,
            scratch_shapes=[pltpu.VMEM((4 * n_a, fk, d), BF16), pltpu.VMEM((4 * n_a, fk, d), BF16),
                            pltpu.SemaphoreType.DMA((4 * n_a,)), pltpu.SemaphoreType.DMA((4 * n_a,))]
            + ([] if carry is None else list(carry.sems))),
        out_shape=[jax.ShapeDtypeStruct((4, fk, d), dt) for _ in range(n_a) for dt in (F32, BF16)]
        + ([] if carry is None else list(carry.out_shapes)),
        compiler_params=_cparams(("arbitrary",)),
    )(pos, *a_ts, b, *([] if carry is None else carry.inputs))


def _rope(t, c, sa, sb, reps):
    c, sa, sb = (jnp.tile(v, (1, reps)) if reps > 1 else v for v in (c, sa, sb))
    w = t.shape[1]
    return t * c + pltpu.roll(t, w - 8, 1) * sa + pltpu.roll(t, 8, 1) * sb


def _rope_bwd(dt, c, sa, sb, reps):
    c, sa, sb = (jnp.tile(v, (1, reps)) if reps > 1 else v for v in (c, sa, sb))
    w = dt.shape[1]
    return dt * c + pltpu.roll(dt * sa, 8, 1) + pltpu.roll(dt * sb, w - 8, 1)


def _ffn_down_mix_in(x, act_t, wd, g, win_t, tabs, name, tm, carry=None):
    s, d = x.shape
    f = wd.shape[0]
    n_in = win_t.shape[0]

    def body(x_ref, act_ref, wd_ref, g_ref, w_ref, c_ref, sa_ref, sb_ref, h_ref, q_ref, k_ref, v_ref, pc_ref, n_ref):
        x = x_ref[...] + 0.5 * _dot_tn(act_ref[...], wd_ref[...])
        h_ref[...] = x
        r = lax.rsqrt(jnp.mean(x * x, axis=-1, keepdims=True) + EPS)
        nb = (x * r * g_ref[...]).astype(BF16)
        n_ref[...] = nb
        u = _dot_nt(nb, w_ref[...])
        c, sa, sb = c_ref[...], sa_ref[...], sb_ref[...]
        q_ref[...] = _rope(u[:, :ATTN_W], c, sa, sb, ATTN_W // 128).astype(BF16)
        k_ref[...] = _rope(u[:, ATTN_W:ATTN_W + KV_W], c, sa, sb, 1).astype(BF16)
        v_ref[...] = u[:, ATTN_W + KV_W:ATTN_W + 2 * KV_W].astype(BF16)
        pc_ref[...] = u[:, ATTN_W + 2 * KV_W:]

    row = lambda w: pl.BlockSpec((tm, w), lambda i: (i, 0))
    return _call(
        body, name, (s // tm,),
        [row(d), pl.BlockSpec((f, tm), lambda i: (0, i)), _whole((f, d)), _whole((1, d)), _whole((n_in, d)),
         row(128), row(128), row(128)],
        [row(d), row(ATTN_W), row(KV_W), row(KV_W), row(POOL_W), row(d)],
        [jax.ShapeDtypeStruct((s, d), F32), jax.ShapeDtypeStruct((s, ATTN_W), BF16),
         jax.ShapeDtypeStruct((s, KV_W), BF16), jax.ShapeDtypeStruct((s, KV_W), BF16),
         jax.ShapeDtypeStruct((s, POOL_W), F32), jax.ShapeDtypeStruct((s, d), BF16)],
        (x, act_t, wd, g, win_t, *tabs), ("arbitrary",), carry)


def _band_mask(n, nb, transposed):
    shape = (3 * BLK, 2 * BLK) if transposed else (2 * BLK, 3 * BLK)
    i = lax.broadcasted_iota(jnp.int32, shape, 1 if transposed else 0) % BLK
    j = lax.broadcasted_iota(jnp.int32, shape, 0 if transposed else 1)
    kpos = (n - 1) * BLK + j
    return (j >= i) & (j <= i + 2 * BLK) & (kpos >= 0) & (kpos < nb * BLK)


def _block_diag(t, kh):
    tf = t.astype(F32)
    tr = pltpu.roll(tf, HEAD_DIM, 1)
    lo = lax.broadcasted_iota(jnp.int32, tf.shape, 1) < HEAD_DIM
    top, bot = (tf, tr) if kh == 0 else (tr, tf)
    return jnp.concatenate([jnp.where(lo, top, 0.0), jnp.where(lo, 0.0, bot)], axis=0).astype(BF16)


def _fold_diag(tbd):
    lo = lax.broadcasted_iota(jnp.int32, (3 * BLK, 2 * HEAD_DIM), 1) < HEAD_DIM
    t = jnp.where(lo, tbd[:3 * BLK], tbd[3 * BLK:])
    return t + pltpu.roll(t, HEAD_DIM, 1)


def _stack_pairs(x, kh):
    return jnp.concatenate([x[:, (2 * kh) * 128:(2 * kh + 1) * 128], x[:, (2 * kh + 1) * 128:(2 * kh + 2) * 128]], axis=0)


def _sink_of(sink_ref, kh, half, axis):
    shape = (2 * BLK, 1) if axis == 0 else (1, 2 * BLK)
    first = lax.broadcasted_iota(jnp.int32, shape, axis) < BLK
    return jnp.where(first, sink_ref[0, GROUP * kh + half], sink_ref[0, GROUP * kh + 2 + half])


def _softmax_sink(sc, valid, sink, axis):
    sc = jnp.where(valid, sc, -1e30)
    m = jnp.maximum(jnp.max(sc, axis=axis, keepdims=True), sink)
    e = jnp.exp(sc - m)
    es = jnp.exp(sink - m)
    inv = 1.0 / (jnp.sum(e, axis=axis, keepdims=True) + es)
    return e * inv, es * inv


def _attn_blocks_per_step(nb):
    return next(nq for nq in (4, 2, 1) if nb % nq == 0)


def _band_specs(nq, nb, w, col=0):
    return [pl.BlockSpec((BLK, w), lambda m: (jnp.maximum(nq * m - 1, 0), col)),
            pl.BlockSpec((nq * BLK, w), lambda m: (m, col)),
            pl.BlockSpec((BLK, w), lambda m: (jnp.minimum(nq * m + nq, nb - 1), col))]


def _attn_pool_fwd(h, w_out, q, k, v, pc, sink, pool_w, pool_scale, pband, name, carry=None, middle=CARRY_MIDDLE):
    s, d = h.shape
    nb = s // BLK
    nq = _attn_blocks_per_step(nb)

    def body(sink_ref, q_ref, k0, k1, k2, v0, v1, v2, p0, p1, p2, pw_ref, ps_ref, pb_ref, h_ref, wo_ref,
             h_out_ref, o_t_ref, o_ref):
        kall = jnp.concatenate([k0[...], k1[...], k2[...]], axis=0)
        vall = jnp.concatenate([v0[...], v1[...], v2[...]], axis=0)
        pall = jnp.concatenate([p0[...], p1[...], p2[...]], axis=0).astype(BF16)
        qall = q_ref[...] * SCORE_SCALE
        for j in range(nq):
            n = pl.program_id(0) * nq + j
            rows, band = slice(j * BLK, (j + 1) * BLK), slice(j * BLK, (j + 3) * BLK)
            valid = _band_mask(n, nb, False)
            kb, vb, qs = kall[band], vall[band], qall[rows]
            for kh in range(N_KV):
                sc = _dot_nt(_stack_pairs(qs, kh), _block_diag(kb, kh))
                p = [_softmax_sink(sc[:, half * 3 * BLK:(half + 1) * 3 * BLK], valid,
                                   _sink_of(sink_ref, kh, half, 0), 1)[0] for half in range(2)]
                o2 = _dot_nn(jnp.concatenate(p, axis=1).astype(BF16), _block_diag(vb, kh)).astype(BF16)
                o_ref[rows, (2 * kh) * 128:(2 * kh + 1) * 128] = o2[:BLK]
                o_ref[rows, (2 * kh + 1) * 128:(2 * kh + 2) * 128] = o2[BLK:]
            ext = pall[band]
            var = _variant_index(n, nb)
            for gi in range(POOL_G):
                gsl = slice(gi * POOL_GW, (gi + 1) * POOL_GW)
                dg = _dot_nn(pb_ref[var, gi], ext[:, gsl])
                yg = _dot_nn(dg.astype(BF16), pw_ref[gi].astype(BF16))
                o_ref[rows, ATTN_W + gi * POOL_GW:ATTN_W + (gi + 1) * POOL_GW] = (yg * ps_ref[:, gsl]).astype(BF16)
        mix = o_ref[...]
        o_t_ref[...] = mix.T
        h_out_ref[...] = h_ref[...] + _dot_nn(mix, wo_ref[...])

    mix_w = ATTN_W + POOL_W
    return _call(
        body, name, (nb // nq,),
        [pl.BlockSpec(memory_space=pltpu.SMEM), pl.BlockSpec((nq * BLK, ATTN_W), lambda m: (m, 0)),
         *_band_specs(nq, nb, KV_W), *_band_specs(nq, nb, KV_W), *_band_specs(nq, nb, POOL_W),
         _whole((POOL_G, POOL_GW, POOL_GW)), _whole((1, POOL_W)), _whole(pband.shape),
         pl.BlockSpec((nq * BLK, d), lambda m: (m, 0)), _whole((mix_w, d))],
        [pl.BlockSpec((nq * BLK, d), lambda m: (m, 0)), pl.BlockSpec((mix_w, nq * BLK), lambda m: (0, m))],
        [jax.ShapeDtypeStruct((s, d), F32), jax.ShapeDtypeStruct((mix_w, s), BF16)],
        (sink, q, k, k, k, v, v, v, pc, pc, pc, pool_w, pool_scale, pband, h, w_out), ("arbitrary",), carry, middle,
        scratch=(pltpu.VMEM((nq * BLK, mix_w), BF16),))


def _attn_pool_bwd(q, k, v, pc, dmix, sink, pool_w, pool_scale, pband, ptband, name, carry=None):
    s = q.shape[0]
    nb = s // BLK
    nq = _attn_blocks_per_step(nb)

    def body(sink_ref, q_ref, k0, k1, k2, v0, v1, v2, p0, p1, p2, da_ref, d0, d1, d2, pw_ref, ps_ref, pb_ref, ptb_ref,
             dq_ref, dk_ref, dv_ref, dpc_ref, dsink_ref, dpw_ref, dps_ref):
        @pl.when(pl.program_id(0) == 0)
        def _():
            dsink_ref[...] = jnp.zeros_like(dsink_ref)
            dpw_ref[...] = jnp.zeros_like(dpw_ref)
            dps_ref[...] = jnp.zeros_like(dps_ref)

        kall = jnp.concatenate([k0[...], k1[...], k2[...]], axis=0)
        vall = jnp.concatenate([v0[...], v1[...], v2[...]], axis=0)
        pall = jnp.concatenate([p0[...], p1[...], p2[...]], axis=0).astype(BF16)
        dpall = jnp.concatenate([d0[...], d1[...], d2[...]], axis=0)
        lo = lax.broadcasted_iota(jnp.int32, (3 * BLK, KV_W), 1) < HEAD_DIM
        for j in range(nq):
            n = pl.program_id(0) * nq + j
            rows, band = slice(j * BLK, (j + 1) * BLK), slice(j * BLK, (j + 3) * BLK)
            valid = _band_mask(n, nb, True)
            kb, vb, qb = kall[band], vall[band], q_ref[rows, :]
            qs = qb * SCORE_SCALE
            da = da_ref[rows, :].astype(BF16)
            dk_fold, dv_fold = [], []
            for kh in range(N_KV):
                kbd, vbd = _block_diag(kb, kh), _block_diag(vb, kh)
                q2, do2 = _stack_pairs(qb, kh), _stack_pairs(da, kh)
                sc_t = _dot_nt(kbd, _stack_pairs(qs, kh))
                dp_t = _dot_nt(vbd, do2)
                p_t, ds_t = [], []
                for half in range(2):
                    keys = slice(half * 3 * BLK, (half + 1) * 3 * BLK)
                    p, ps = _softmax_sink(sc_t[keys], valid, _sink_of(sink_ref, kh, half, 1), 0)
                    delta = jnp.sum(p * dp_t[keys], axis=0, keepdims=True)
                    p_t.append(p.astype(BF16))
                    ds_t.append((p * (dp_t[keys] - delta)).astype(BF16))
                    dsk = -ps * delta
                    for pair in range(2):
                        h = GROUP * kh + 2 * pair + half
                        part = jnp.sum(dsk[:, pair * BLK:(pair + 1) * BLK], axis=1, keepdims=True)
                        dsink_ref[h:h + 1, :] += jnp.broadcast_to(part, (1, 128))
                p_t = jnp.concatenate(p_t, axis=0)
                ds_t = jnp.concatenate(ds_t, axis=0)
                dq2 = _dot_tn(ds_t, kbd) * SCORE_SCALE
                dq_ref[rows, (2 * kh) * 128:(2 * kh + 1) * 128] = dq2[:BLK]
                dq_ref[rows, (2 * kh + 1) * 128:(2 * kh + 2) * 128] = dq2[BLK:]
                dk_fold.append(_fold_diag(_dot_nn(ds_t, q2)) * SCORE_SCALE)
                dv_fold.append(_fold_diag(_dot_nn(p_t, do2)))
            dk_all = jnp.where(lo, dk_fold[0], dk_fold[1])
            dv_all = jnp.where(lo, dv_fold[0], dv_fold[1])
            for t in range(3):
                dk_ref[j, t] = dk_all[t * BLK:(t + 1) * BLK]
                dv_ref[j, t] = dv_all[t * BLK:(t + 1) * BLK]
            ext, dpe = pall[band], dpall[band]
            dpc_cur = dpall[(j + 1) * BLK:(j + 2) * BLK]
            var = _variant_index(n, nb)
            for gi in range(POOL_G):
                gsl = slice(gi * POOL_GW, (gi + 1) * POOL_GW)
                wg = pw_ref[gi].astype(BF16)
                sc = ps_ref[:, gsl]
                dgb = _dot_nn(pb_ref[var, gi], ext[:, gsl]).astype(BF16)
                yg = _dot_nn(dgb, wg)
                dps_ref[:, gsl] += jnp.sum(dpc_cur[:, gsl] * yg, axis=0, keepdims=True)
                dpw_ref[gi] += _dot_tn(dgb, (dpc_cur[:, gsl] * sc).astype(BF16))
                dd = _dot_nt((dpe[:, gsl] * sc).astype(BF16), wg)
                dpc_ref[rows, gsl] = _dot_nn(ptb_ref[var, gi], dd.astype(BF16))

    fixed = lambda shape: pl.BlockSpec(shape, lambda m: (0,) * len(shape))
    return _call(
        body, name, (nb // nq,),
        [pl.BlockSpec(memory_space=pltpu.SMEM), pl.BlockSpec((nq * BLK, ATTN_W), lambda m: (m, 0)),
         *_band_specs(nq, nb, KV_W), *_band_specs(nq, nb, KV_W), *_band_specs(nq, nb, POOL_W),
         pl.BlockSpec((nq * BLK, ATTN_W), lambda m: (m, 0)), *_band_specs(nq, nb, POOL_W, 1),
         _whole((POOL_G, POOL_GW, POOL_GW)), _whole((1, POOL_W)), _whole(pband.shape), _whole(ptband.shape)],
        [pl.BlockSpec((nq * BLK, ATTN_W), lambda m: (m, 0)),
         pl.BlockSpec((nq, 3, BLK, KV_W), lambda m: (m, 0, 0, 0)),
         pl.BlockSpec((nq, 3, BLK, KV_W), lambda m: (m, 0, 0, 0)),
         pl.BlockSpec((nq * BLK, POOL_W), lambda m: (m, 0)),
         fixed((N_HEADS, 128)), fixed((POOL_G, POOL_GW, POOL_GW)), fixed((1, POOL_W))],
        [jax.ShapeDtypeStruct((s, ATTN_W), F32), jax.ShapeDtypeStruct((nb, 3, BLK, KV_W), F32),
         jax.ShapeDtypeStruct((nb, 3, BLK, KV_W), F32), jax.ShapeDtypeStruct((s, POOL_W), F32),
         jax.ShapeDtypeStruct((N_HEADS, 128), F32),
         jax.ShapeDtypeStruct((POOL_G, POOL_GW, POOL_GW), F32), jax.ShapeDtypeStruct((1, POOL_W), F32)],
        (sink, q, k, k, k, v, v, v, pc, pc, pc, dmix, dmix, dmix, dmix, pool_w, pool_scale, pband, ptband),
        ("arbitrary",), carry)


def _mix_in_bwd(h, dh, g, win_t, dq, dkp, dvp, dpc, tabs, name, tm, carry=None):
    s, d = h.shape
    nb = s // BLK
    nt = tm // BLK
    n_in = win_t.shape[0]

    def band_sum(n, before, own, after, prev_last, next_first):
        lo = (n > 0).astype(F32)
        hi = (n < s // tm - 1).astype(F32)
        blocks = []
        for b in range(nt):
            from_prev = prev_last[...] * lo if b == 0 else before[b - 1]
            from_next = next_first[...] * hi if b == nt - 1 else after[b + 1]
            blocks.append(from_prev + own[b] + from_next)
        return jnp.concatenate(blocks, axis=0)

    def body(h_ref, dh_ref, g_ref, w_ref, dq_ref, k2, k1, k0, kp, kn, v2, v1, v0, vp, vn, dpc_ref, c_ref, sa_ref,
             sb_ref, o_ref, du_ref, dg_ref):
        n = pl.program_id(0)
        dk = band_sum(n, k2, k1, k0, kp, kn)
        dv = band_sum(n, v2, v1, v0, vp, vn)
        c, sa, sb = c_ref[...], sa_ref[...], sb_ref[...]
        du = jnp.concatenate([_rope_bwd(dq_ref[...], c, sa, sb, ATTN_W // 128), _rope_bwd(dk, c, sa, sb, 1), dv,
                              dpc_ref[...]], axis=1)
        du_ref[...] = du.T.astype(BF16)
        dn = _dot_nn(du.astype(BF16), w_ref[...])
        x = h_ref[...]
        r = lax.rsqrt(jnp.mean(x * x, axis=-1, keepdims=True) + EPS)
        xhat = x * r
        dxhat = dn * g_ref[...]
        o_ref[...] = dh_ref[...] + r * (dxhat - xhat * jnp.mean(dxhat * xhat, axis=-1, keepdims=True))

        @pl.when(n == 0)
        def _():
            dg_ref[...] = jnp.zeros_like(dg_ref)

        dg_ref[...] += jnp.sum(dn * xhat, axis=0, keepdims=True)

    row = lambda w: pl.BlockSpec((tm, w), lambda n: (n, 0))
    slot = lambda t: pl.BlockSpec((nt, None, BLK, KV_W), lambda n, t=t: (n, t, 0, 0))
    parts = [slot(2), slot(1), slot(0),
             pl.BlockSpec((None, None, BLK, KV_W), lambda n: (jnp.maximum(nt * n - 1, 0), 2, 0, 0)),
             pl.BlockSpec((None, None, BLK, KV_W), lambda n: (jnp.minimum(nt * n + nt, nb - 1), 0, 0, 0))]
    return _call(
        body, name, (s // tm,),
        [row(d), row(d), _whole((1, d)), _whole((n_in, d)), row(ATTN_W), *parts, *parts, row(POOL_W),
         row(128), row(128), row(128)],
        [row(d), pl.BlockSpec((n_in, tm), lambda n: (0, n)), pl.BlockSpec((1, d), lambda n: (0, 0))],
        [jax.ShapeDtypeStruct((s, d), F32), jax.ShapeDtypeStruct((n_in, s), BF16), jax.ShapeDtypeStruct((1, d), F32)],
        (h, dh, g, win_t, dq, *[dkp] * 5, *[dvp] * 5, dpc, *tabs), ("arbitrary",), carry)


def _adam_math(w, g, m, v):
    m = ADAM_B1 * m + (1.0 - ADAM_B1) * g
    v = ADAM_B2 * v + (1.0 - ADAM_B2) * (g * g)
    m_hat = m / (1.0 - ADAM_B1 ** ADAM_STEP)
    v_hat = v / (1.0 - ADAM_B2 ** ADAM_STEP)
    delta = -ADAM_LR * (m_hat / (jnp.sqrt(v_hat) + ADAM_EPS) + ADAM_WD * w)
    return delta, m, v


def _adam_small(w, parts, late, m, v, name):
    rows, cols = w.shape

    def body(w_ref, p_ref, l_ref, m_ref, v_ref, g_ref, d_ref, nm_ref, nv_ref):
        g, gl = p_ref[0], l_ref[0]
        for k in range(1, N_DEV):
            g = g + p_ref[k]
            gl = gl + l_ref[k]
        g_ref[...] = g
        g_ref[SMALL_NORM1:SMALL_NORM1 + 8, :] = g[SMALL_NORM1:SMALL_NORM1 + 8] + gl
        d_ref[...], nm_ref[...], nv_ref[...] = _adam_math(w_ref[...], g_ref[...], m_ref[...], v_ref[...])

    return pl.pallas_call(
        body, name=name, out_shape=[jax.ShapeDtypeStruct((rows, cols), F32)] * 4,
    )(w, parts, late, m, v)


SMALL_NORM1 = 512


def _pack_small(norm1, normm, norm2, normf, sink, pool_w, pool_scale, loss=None):
    scale_rows = jnp.pad(pool_scale.reshape(4, 128), ((0, 4), (0, 0)))
    last_rows = jnp.pad(sink.reshape(1, N_HEADS), ((0, 7), (0, 128 - N_HEADS)))
    if loss is not None:
        last_rows = last_rows + jnp.pad(loss.reshape(1, 1), ((1, 6), (0, 127)))
    return jnp.concatenate([pool_w.reshape(512, 128), norm1.reshape(8, 128), normm.reshape(8, 128),
                            norm2.reshape(8, 128), normf.reshape(8, 128), scale_rows, last_rows], axis=0)


def _unpack_small(p):
    return dict(pool_w=p[:512].reshape(1, POOL_G, POOL_GW, POOL_GW), ffn1_norm=p[512:520].reshape(1, 1024),
                mix_norm=p[520:528].reshape(1, 1024), ffn2_norm=p[528:536].reshape(1, 1024),
                final_norm=p[536:544].reshape(1024), pool_scale=p[544:548].reshape(1, POOL_W),
                sink_logits=p[552, :N_HEADS].reshape(1, N_HEADS), loss=p[553, 0])


def kernel(x, ffn1_norm, ffn1_w_gate, ffn1_w_up, ffn1_w_down, mix_norm, w_in, sink_logits, pool_w, pool_scale, w_out, ffn2_norm, ffn2_w_gate, ffn2_w_up, ffn2_w_down, final_norm, loss_target, m_ffn1_norm, m_ffn1_w_gate, m_ffn1_w_up, m_ffn1_w_down, m_mix_norm, m_w_in, m_sink_logits, m_pool_w, m_pool_scale, m_w_out, m_ffn2_norm, m_ffn2_w_gate, m_ffn2_w_up, m_ffn2_w_down, m_final_norm, v_ffn1_norm, v_ffn1_w_gate, v_ffn1_w_up, v_ffn1_w_down, v_mix_norm, v_w_in, v_sink_logits, v_pool_w, v_pool_scale, v_w_out, v_ffn2_norm, v_ffn2_w_gate, v_ffn2_w_up, v_ffn2_w_down, v_final_norm):
    s, d = x.shape[1], x.shape[2]
    tm = min(512, s)
    tm_bwd = min(256, s)
    pos = jnp.stack([lax.axis_index("x"), lax.axis_index("y"), lax.axis_index("c")]).astype(jnp.int32)

    t_bf = lambda w: w[0].T.astype(BF16)
    full = lambda a: a.reshape(N_DEV * a.shape[1], d)
    (wg1,) = map(full, _run_exchange(_AllGather([t_bf(ffn1_w_gate)]), "gather_ffn1_gate"))

    tabs = _rope_tables(s)
    pband, ptband = _pool_tables(s)
    g1, gm, g2, gf = ffn1_norm, mix_norm, ffn2_norm, final_norm.reshape(1, d)

    x0 = x[0]
    n1, gate1, wu1 = _ffn_gate(x0, g1, wg1, "ffn1_gate", tm, carry=_AllGather([t_bf(ffn1_w_up)]))
    up1, act1_t, wd1, win_t = _ffn_up(n1, gate1, full(wu1), "ffn1_up", tm,
                                            carry=_AllGather([ffn1_w_down[0].astype(BF16), t_bf(w_in)]))
    wu1, wd1, win_t = full(wu1), full(wd1), full(win_t)
    h1, q, k, v, pc, n2, wout, wg2 = _ffn_down_mix_in(
        x0, act1_t, wd1, gm, win_t, tabs, "ffn1_down_mix_in", tm,
        carry=_AllGather([w_out[0].astype(BF16), t_bf(ffn2_w_gate)]))
    wout, wg2 = full(wout), full(wg2)
    h2, mix_t, *gathered = _attn_pool_fwd(h1, wout, q, k, v, pc, sink_logits, pool_w[0], pool_scale, pband,
                                          "attn_pool_fwd",
                                          carry=_AllGather([t_bf(ffn2_w_up), ffn2_w_down[0].astype(BF16)]))
    wu2, wd2 = map(full, gathered)
    dh3, n3, gate2, up2, act2_t, loss_part, dgf = _ffn_fwd(h2, g2, wg2, wu2, wd2, "ffn2_fwd", tm,
                                                           head=(gf, loss_target[0]))

    sum1, recv2 = {}, {}

    def stage2(keys):
        return _RsStage2([sum1[key][1] for key in keys])

    dh2, dg2, dgate2_t, dup2_t, dhh3, dmix, dh2b = _ffn_bwd(h2, dh3, gate2, up2, g2, wg2, wu2, wd2, wout, "ffn2_bwd",
                                                            tm_bwd)
    both = _wgrad_rs1(pos, [dgate2_t, dup2_t], n3, "wgrad_gate_up2")
    sum1["g2"], sum1["u2"] = both[:2], both[2:]
    sum1["d2"] = _wgrad_rs1(pos, [act2_t], dhh3, "wgrad_down2")
    sum1["out"] = _wgrad_rs1(pos, [mix_t], dh2b, "wgrad_out")
    dq, dkp, dvp, dpc, dsink, dpw, dps, *r2 = _attn_pool_bwd(
        q, k, v, pc, dmix, sink_logits, pool_w[0], pool_scale, pband, ptband, "attn_pool_bwd",
        carry=stage2(["g2", "u2", "d2"]))
    recv2.update(zip(["g2", "u2", "d2"], r2))
    dh1, du_t, dgm, recv2["out"] = _mix_in_bwd(h1, dh2, gm, win_t, dq, dkp, dvp, dpc, tabs, "mix_in_bwd", tm,
                                               carry=stage2(["out"]))
    sum1["in"] = _wgrad_rs1(pos, [du_t], n2, "wgrad_in")
    small_part = _pack_small(jnp.zeros_like(dgm), dgm, dg2, dgf, dsink[:, 0], dpw, dps, loss_part)
    dgate1_t, dup1_t, dhh1, recv2["in"], small_all = _ffn_bwd_gates(
        dh1, gate1, up1, wd1, "ffn1_bwd_gates", tm, carry=_Both(stage2(["in"]), _AllGather([small_part])))
    sum1["g1"] = _wgrad_rs1(pos, [dgate1_t], n1, "wgrad_gate1")
    *sum1["u1"], recv2["g1"] = _wgrad_rs1(pos, [dup1_t], n1, "wgrad_up1", carry=stage2(["g1"]))
    *sum1["d1"], recv2["u1"] = _wgrad_rs1(pos, [act1_t], dhh1, "wgrad_down1", carry=stage2(["u1"]))
    dx, dg1, recv2["d1"] = _ffn_bwd_input(x0, dh1, dgate1_t, dup1_t, g1, wg1, wu1, "ffn1_bwd_input", tm,
                                          carry=stage2(["d1"]))

    (dg1_all,) = _run_exchange(_DirectGather([dg1.reshape(8, 128)]), "gather_norm1_grad")
    pk = lambda a, b, c_, e, s_, pw_, psc: _pack_small(a, b, c_, e, s_[0], pw_[0], psc)
    small_w = pk(ffn1_norm, mix_norm, ffn2_norm, final_norm, sink_logits, pool_w, pool_scale)
    small_m = pk(m_ffn1_norm, m_mix_norm, m_ffn2_norm, m_final_norm, m_sink_logits, m_pool_w, m_pool_scale)
    small_v = pk(v_ffn1_norm, v_mix_norm, v_ffn2_norm, v_final_norm, v_sink_logits, v_pool_w, v_pool_scale)
    sg, sd, sm, sv = [_unpack_small(a)
                      for a in _adam_small(small_w, small_all, dg1_all, small_m, small_v, "adam_small")]

    big = {}
    keys = ["g1", "u1", "d1", "g2", "u2", "d2", "in", "out"]
    names = ["ffn1_w_gate", "ffn1_w_up", "ffn1_w_down", "ffn2_w_gate", "ffn2_w_up", "ffn2_w_down", "w_in", "w_out"]
    transposed = [True, True, False, True, True, False, True, False]
    ws = [ffn1_w_gate, ffn1_w_up, ffn1_w_down, ffn2_w_gate, ffn2_w_up, ffn2_w_down, w_in, w_out]
    ms = [m_ffn1_w_gate, m_ffn1_w_up, m_ffn1_w_down, m_ffn2_w_gate, m_ffn2_w_up, m_ffn2_w_down, m_w_in, m_w_out]
    vs = [v_ffn1_w_gate, v_ffn1_w_up, v_ffn1_w_down, v_ffn2_w_gate, v_ffn2_w_up, v_ffn2_w_down, v_w_in, v_w_out]
    for key, nm, tr, w, m, vv in zip(keys, names, transposed, ws, ms, vs):
        view = (lambda a: jnp.swapaxes(a, 1, 2)[0]) if tr else (lambda a: a[0])
        back = (lambda a: jnp.swapaxes(a[None], 1, 2)) if tr else (lambda a: a[None])
        res = _rs_sum2_adam(pos, sum1[key][0], recv2[key], view(w), view(m), view(vv), "adam_" + nm)
        big[nm] = tuple(back(a) for a in res)

    loss = sg["loss"]
    all_names = ["ffn1_norm", "ffn1_w_gate", "ffn1_w_up", "ffn1_w_down", "mix_norm", "w_in", "sink_logits", "pool_w",
                 "pool_scale", "w_out", "ffn2_norm", "ffn2_w_gate", "ffn2_w_up", "ffn2_w_down", "final_norm"]
    outs = [loss, dx[None]]
    for idx, src in enumerate((sg, sd, sm, sv)):
        for nm in all_names:
            outs.append(big[nm][idx] if nm in big else src[nm])
    return tuple(outs)
```

```python
import functools

import jax
import jax.numpy as jnp
import numpy as np
from jax import lax
from jax.experimental import pallas as pl
from jax.experimental.pallas import tpu as pltpu

F32 = jnp.float32
BF16 = jnp.bfloat16
MESH = pl.DeviceIdType.MESH
N_DEV = 8

EPS = 1e-6
HEAD_DIM = 64
N_HEADS = 8
N_KV = 2
GROUP = N_HEADS // N_KV
ATTN_W = N_HEADS * HEAD_DIM
KV_W = N_KV * HEAD_DIM
POOL_W = 512
POOL_G = 4
POOL_GW = POOL_W // POOL_G
POOL_WINDOWS = (2, 4, 8, 16)
BLK = 128
ROT = 16
ROPE_THETA = 500000.0
SCORE_SCALE = HEAD_DIM ** -0.5

ADAM_LR, ADAM_B1, ADAM_B2, ADAM_EPS, ADAM_WD, ADAM_STEP = 0.001, 0.9, 0.999, 1e-08, 0.01, 10

VMEM_LIMIT = 56 * 1024 * 1024


def _cparams(sem=None, **kw):
    if sem is not None:
        kw["dimension_semantics"] = sem
    return pltpu.CompilerParams(vmem_limit_bytes=VMEM_LIMIT, **kw)


def _whole(shape):
    nd = len(shape)
    return pl.BlockSpec(shape, lambda *_: (0,) * nd, pipeline_mode=pl.Buffered(1))


def _sigmoid(z):
    return 1.0 / (1.0 + jnp.exp(-z))


def _dot_nt(a, b):
    return lax.dot_general(a, b, (((1,), (1,)), ((), ())), preferred_element_type=F32)


def _dot_nn(a, b):
    return lax.dot_general(a, b, (((1,), (0,)), ((), ())), preferred_element_type=F32)


def _dot_tn(a, b):
    return lax.dot_general(a, b, (((0,), (0,)), ((), ())), preferred_element_type=F32)


def _rope_tables(s):
    inv_freq = ROPE_THETA ** (-np.arange(0, ROT, 2, dtype=np.float64) / ROT)
    ang = np.arange(s, dtype=np.float64)[:, None] * inv_freq[None, :]
    c = np.ones((s, HEAD_DIM)); sa = np.zeros((s, HEAD_DIM)); sb = np.zeros((s, HEAD_DIM))
    c[:, :8] = np.cos(ang); c[:, 8:16] = np.cos(ang)
    sa[:, :8] = -np.sin(ang)
    sb[:, 8:16] = np.sin(ang)
    t = lambda a: jnp.asarray(np.tile(a, (1, 2)).astype(np.float32))
    return t(c), t(sa), t(sb)


def _pool_weight(gi, t, s_pos, s):
    half = POOL_WINDOWS[gi] // 2

    def win(lo, hi):
        a = np.clip(lo, 0, s); b = np.clip(hi + 1, 0, s)
        inside = (s_pos >= a) & (s_pos < b)
        return inside / np.maximum(b - a, 1)

    w = 0.5 * (win(t - half, t + half - 1) + win(t - half + 1, t + half)) - (t == s_pos)
    return w * ((t >= 0) & (t < s) & (s_pos >= 0) & (s_pos < s))


def _pool_tables(s):
    nb = s // BLK
    fwd = np.zeros((3, POOL_G, BLK, 3 * BLK), np.float32)
    bwd = np.zeros((3, POOL_G, BLK, 3 * BLK), np.float32)
    for vi, n in enumerate((0, 1 if nb > 2 else 0, nb - 1)):
        i = n * BLK + np.arange(BLK)[:, None]
        j = (n - 1) * BLK + np.arange(3 * BLK)[None, :]
        for gi in range(POOL_G):
            fwd[vi, gi] = _pool_weight(gi, i, j, s)
            bwd[vi, gi] = _pool_weight(gi, j, i, s)
    return jnp.asarray(fwd, dtype=BF16), jnp.asarray(bwd, dtype=BF16)


def _variant_index(n, nb):
    return jnp.where(n == 0, 0, jnp.where(n == nb - 1, 2, 1))


class _Exchange:
    inputs = ()
    out_shapes = ()
    sems = ()

    def start(self, srcs, outs, sems):
        raise NotImplementedError

    def middle(self, srcs, outs, sems):
        pass

    def finish(self, srcs, outs, sems):
        raise NotImplementedError


class _AllGather(_Exchange):
    def __init__(self, arrays):
        n = len(arrays)
        self.inputs = list(arrays)
        self.out_shapes = [jax.ShapeDtypeStruct((N_DEV,) + a.shape, a.dtype) for a in arrays]
        self.sems = [pltpu.SemaphoreType.DMA((n, 8)), pltpu.SemaphoreType.DMA((n, 8)), pltpu.SemaphoreType.DMA((n,))]

    def _parts(self, srcs, outs, sems):
        send_sems, recv_sems, local_sems = sems
        n = len(srcs)
        x, y, c = lax.axis_index("x"), lax.axis_index("y"), lax.axis_index("c")
        me, sibling, xn, yn, diag = (x, y, c), (x, y, 1 - c), (1 - x, y, c), (x, 1 - y, c), (1 - x, 1 - y, c)

        def place(a, dev, half=None):
            block = outs[a].at[4 * dev[0] + 2 * dev[1] + dev[2]]
            if half is None:
                return block
            r2 = outs[a].shape[1] // 2
            return block.at[pl.ds(half * r2, r2)]

        def copy(a, k, dev, to, half=None, src=None):
            where = place(a, dev, half)
            return pltpu.make_async_remote_copy(
                src_ref=where if src is None else src, dst_ref=where, send_sem=send_sems.at[a, k],
                recv_sem=recv_sems.at[a, k], device_id=to, device_id_type=MESH)

        def other(dev):
            return (dev[0], dev[1], 1 - dev[2])

        class Parts:
            mine = staticmethod(lambda: [pltpu.make_async_copy(srcs[a], place(a, me), local_sems.at[a])
                                         for a in range(n)])
            own = staticmethod(lambda: [copy(a, k, me, to, src=srcs[a]) for a in range(n)
                                        for k, to in ((0, sibling), (1, xn), (2, yn))])
            relay = staticmethod(lambda a: [copy(a, 3, xn, yn, half=0), copy(a, 4, yn, xn, half=1),
                                            copy(a, 5, xn, sibling), copy(a, 6, yn, sibling)])
            last = staticmethod(lambda a: copy(a, 7, diag, sibling))
            from_x = staticmethod(lambda a: copy(a, 1, xn, me))
            from_y = staticmethod(lambda a: copy(a, 2, yn, me))
            diag_halves = staticmethod(lambda a: [copy(a, 3, diag, me, half=0), copy(a, 4, diag, me, half=1)])
            from_sibling = staticmethod(lambda a: [copy(a, 0, sibling, me), copy(a, 5, other(xn), me),
                                                   copy(a, 6, other(yn), me), copy(a, 7, other(diag), me)])

        return n, Parts

    def start(self, srcs, outs, sems):
        _, p = self._parts(srcs, outs, sems)
        for cp in p.mine() + p.own():
            cp.start()

    def middle(self, srcs, outs, sems):
        n, p = self._parts(srcs, outs, sems)
        for a in range(n):
            p.from_x(a).wait_recv()
            p.from_y(a).wait_recv()
            for cp in p.relay(a):
                cp.start()

    def finish(self, srcs, outs, sems):
        n, p = self._parts(srcs, outs, sems)
        for a in range(n):
            for cp in p.diag_halves(a):
                cp.wait_recv()
            p.last(a).start()
        for a in range(n):
            for cp in p.from_sibling(a):
                cp.wait_recv()
        for cp in p.own() + [cp for a in range(n) for cp in p.relay(a) + [p.last(a)]]:
            cp.wait_send()
        for cp in p.mine():
            cp.wait()


class _RsStage2(_Exchange):
    def start(self, srcs, outs, sems):
        for cp in self._copies(srcs, outs, sems):
            cp.start()

    def finish(self, srcs, outs, sems):
        copies = self._copies(srcs, outs, sems)
        for cp in copies:
            cp.wait_recv()
        for cp in copies:
            cp.wait_send()


    def __init__(self, pbs):
        n = len(pbs)
        self.inputs = list(pbs)
        self.out_shapes = [jax.ShapeDtypeStruct((3,) + p.shape[1:], p.dtype) for p in pbs]
        self.sems = [pltpu.SemaphoreType.DMA((n, 3)), pltpu.SemaphoreType.DMA((n, 3))]

    def _copies(self, srcs, outs, sems):
        send_sems, recv_sems = sems
        x, y, c = lax.axis_index("x"), lax.axis_index("y"), lax.axis_index("c")
        chips = [(1 - x, y), (x, 1 - y), (1 - x, 1 - y)]
        return [pltpu.make_async_remote_copy(
            src_ref=srcs[a].at[2 * chip[0] + chip[1]], dst_ref=outs[a].at[j], send_sem=send_sems.at[a, j],
            recv_sem=recv_sems.at[a, j], device_id=(*chip, c), device_id_type=MESH)
            for a in range(len(srcs)) for j, chip in enumerate(chips)]


class _DirectGather(_Exchange):
    def __init__(self, arrays):
        n = len(arrays)
        self.inputs = list(arrays)
        self.out_shapes = [jax.ShapeDtypeStruct((N_DEV,) + a.shape, a.dtype) for a in arrays]
        self.sems = [pltpu.SemaphoreType.DMA((n, 7)), pltpu.SemaphoreType.DMA((n, 7)), pltpu.SemaphoreType.DMA((n,))]

    def _copies(self, srcs, outs, sems):
        send_sems, recv_sems, local_sems = sems
        x, y, c = lax.axis_index("x"), lax.axis_index("y"), lax.axis_index("c")
        me = 4 * x + 2 * y + c
        remote, local = [], []
        for a in range(len(srcs)):
            local.append(pltpu.make_async_copy(srcs[a], outs[a].at[me], local_sems.at[a]))
            for k in range(1, N_DEV):
                peer = (x ^ (k >> 2), y ^ ((k >> 1) & 1), c ^ (k & 1))
                remote.append(pltpu.make_async_remote_copy(
                    src_ref=srcs[a], dst_ref=outs[a].at[me], send_sem=send_sems.at[a, k - 1],
                    recv_sem=recv_sems.at[a, k - 1], device_id=peer, device_id_type=MESH))
        return remote, local

    def start(self, srcs, outs, sems):
        remote, local = self._copies(srcs, outs, sems)
        for cp in local + remote:
            cp.start()

    def finish(self, srcs, outs, sems):
        remote, local = self._copies(srcs, outs, sems)
        for cp in remote:
            cp.wait_recv()
        for cp in remote:
            cp.wait_send()
        for cp in local:
            cp.wait()


class _Both(_Exchange):
    def __init__(self, a, b):
        self.a, self.b = a, b
        self.inputs = list(a.inputs) + list(b.inputs)
        self.out_shapes = list(a.out_shapes) + list(b.out_shapes)
        self.sems = list(a.sems) + list(b.sems)

    def _split(self, srcs, outs, sems):
        na, oa, sa = len(self.a.inputs), len(self.a.out_shapes), len(self.a.sems)
        return (srcs[:na], outs[:oa], sems[:sa]), (srcs[na:], outs[oa:], sems[sa:])

    def start(self, srcs, outs, sems):
        pa, pb = self._split(srcs, outs, sems)
        self.a.start(*pa)
        self.b.start(*pb)

    def middle(self, srcs, outs, sems):
        pa, pb = self._split(srcs, outs, sems)
        self.a.middle(*pa)
        self.b.middle(*pb)

    def finish(self, srcs, outs, sems):
        pa, pb = self._split(srcs, outs, sems)
        self.a.finish(*pa)
        self.b.finish(*pb)


_ANY = pl.BlockSpec(memory_space=pl.ANY)


def _run_exchange(ex, name):
    n_in, n_out = len(ex.inputs), len(ex.out_shapes)

    def body(*refs):
        srcs, outs, sems = refs[:n_in], refs[n_in:n_in + n_out], refs[n_in + n_out:]
        ex.start(srcs, outs, sems)
        ex.middle(srcs, outs, sems)
        ex.finish(srcs, outs, sems)

    return pl.pallas_call(
        body, name=name, out_shape=list(ex.out_shapes), in_specs=[_ANY] * n_in, out_specs=[_ANY] * n_out,
        scratch_shapes=list(ex.sems),
    )(*ex.inputs)


CARRY_MIDDLE = 0.7


def _call(body, name, grid, in_specs, out_specs, out_shape, args, sem, carry=None, middle=CARRY_MIDDLE, scratch=()):
    if carry is None:
        return pl.pallas_call(functools.partial(body), name=name, grid=grid, in_specs=in_specs, out_specs=out_specs,
                              out_shape=out_shape, scratch_shapes=list(scratch), compiler_params=_cparams(sem))(*args)
    n_in, n_out = len(in_specs), len(out_specs)
    nc_in, nc_out = len(carry.inputs), len(carry.out_shapes)

    def carried(*refs):
        ins = refs[:n_in]
        c_in = refs[n_in:n_in + nc_in]
        outs = refs[n_in + nc_in:n_in + nc_in + n_out]
        c_out = refs[n_in + nc_in + n_out:n_in + nc_in + n_out + nc_out]
        own = refs[n_in + nc_in + n_out + nc_out:n_in + nc_in + n_out + nc_out + len(scratch)]
        sems = refs[n_in + nc_in + n_out + nc_out + len(scratch):]
        ids = [pl.program_id(i) for i in range(len(grid))]
        is_first = functools.reduce(jnp.logical_and, [i == 0 for i in ids])
        is_last = functools.reduce(jnp.logical_and, [i == g - 1 for i, g in zip(ids, grid)])
        @pl.when(is_first)
        def _():
            carry.start(c_in, c_out, sems)

        if middle is not None:
            @pl.when(functools.reduce(jnp.logical_and, [ids[0] == round(middle * (grid[0] - 1))]
                                      + [i == 0 for i in ids[1:]]))
            def _():
                carry.middle(c_in, c_out, sems)

        body(*ins, *outs, *own)

        @pl.when(is_last)
        def _():
            if middle is None:
                carry.middle(c_in, c_out, sems)
            carry.finish(c_in, c_out, sems)

    return pl.pallas_call(
        carried, name=name, grid=grid, in_specs=list(in_specs) + [_ANY] * nc_in,
        out_specs=list(out_specs) + [_ANY] * nc_out, out_shape=list(out_shape) + list(carry.out_shapes),
        scratch_shapes=list(scratch) + list(carry.sems), compiler_params=_cparams(sem))(*args, *carry.inputs)


def _rs_sum2_adam(pos, p, r2, w, m, v, name):
    _, rows, d = p.shape
    tr = rows // 2 if rows % 16 == 0 else rows

    def body(pos_ref, p_ref, r_ref, w_ref, m_ref, v_ref, g_ref, d_ref, nm_ref, nv_ref):
        r = r_ref[...].astype(F32)
        g = ((p_ref[...] + r[0]) + r[1]) + r[2]
        g_ref[...] = g
        d_ref[...], nm_ref[...], nv_ref[...] = _adam_math(w_ref[...], g, m_ref[...], v_ref[...])

    blk = pl.BlockSpec((tr, d), lambda i, pos_ref: (i, 0))
    return pl.pallas_call(
        body, name=name,
        grid_spec=pltpu.PrefetchScalarGridSpec(
            num_scalar_prefetch=1, grid=(rows // tr,),
            in_specs=[pl.BlockSpec((None, tr, d), lambda i, pos_ref: (2 * pos_ref[0] + pos_ref[1], i, 0)),
                      pl.BlockSpec((3, tr, d), lambda i, pos_ref: (0, i, 0)), blk, blk, blk],
            out_specs=[blk] * 4),
        out_shape=[jax.ShapeDtypeStruct((rows, d), F32)] * 4,
        compiler_params=_cparams(("arbitrary",)),
    )(pos, p, r2, w, m, v)


def _ffn_chunk(f):
    for cand in (256, 128):
        if f % cand == 0:
            return cand
    return f


def _loss_head(x, gg, target, loss_ref, dg_ref):
    @pl.when(pl.program_id(0) == 0)
    def _():
        loss_ref[...] = jnp.zeros_like(loss_ref)
        dg_ref[...] = jnp.zeros_like(dg_ref)

    r = lax.rsqrt(jnp.mean(x * x, axis=-1, keepdims=True) + EPS)
    xhat = x * r
    e = xhat * gg - target
    loss_ref[...] += 0.5 * jnp.sum(jnp.mean(e * e, axis=-1, keepdims=True), axis=0, keepdims=True)
    dy = e * (1.0 / x.shape[-1])
    dg_ref[...] += jnp.sum(dy * xhat, axis=0, keepdims=True)
    dxhat = dy * gg
    return r * (dxhat - xhat * jnp.mean(dxhat * xhat, axis=-1, keepdims=True))


def _ffn_gate(h, g, wg_t, name, tm, carry=None):
    s, d = h.shape
    f = wg_t.shape[0]

    def body(h_ref, g_ref, wg_ref, n_ref, gate_ref):
        x = h_ref[...]
        r = lax.rsqrt(jnp.mean(x * x, axis=-1, keepdims=True) + EPS)
        nb = (x * r * g_ref[...]).astype(BF16)
        n_ref[...] = nb
        gate_ref[...] = _dot_nt(nb, wg_ref[...]).astype(BF16)

    row = lambda w: pl.BlockSpec((tm, w), lambda i: (i, 0))
    return _call(body, name, (s // tm,), [row(d), _whole((1, d)), _whole((f, d))], [row(d), row(f)],
                 [jax.ShapeDtypeStruct((s, d), BF16), jax.ShapeDtypeStruct((s, f), BF16)], (h, g, wg_t),
                 ("arbitrary",), carry)


def _ffn_up(n, gate, wu_t, name, tm, carry=None):
    s, d = n.shape
    f = wu_t.shape[0]
    tf = _ffn_chunk(f)

    def body(n_ref, gate_ref, wu_ref, up_ref, act_t_ref):
        nb = n_ref[...]
        for j in range(f // tf):
            sl = slice(j * tf, (j + 1) * tf)
            up = _dot_nt(nb, wu_ref[sl, :])
            gate = gate_ref[:, sl].astype(F32)
            up_ref[:, sl] = up.astype(BF16)
            act_t_ref[sl, :] = (gate * _sigmoid(gate) * up).astype(BF16).T

    row = lambda w: pl.BlockSpec((tm, w), lambda i: (i, 0))
    return _call(body, name, (s // tm,), [row(d), row(f), _whole((f, d))],
                 [row(f), pl.BlockSpec((f, tm), lambda i: (0, i))],
                 [jax.ShapeDtypeStruct((s, f), BF16), jax.ShapeDtypeStruct((f, s), BF16)], (n, gate, wu_t),
                 ("arbitrary",), carry)


def _ffn_fwd(h, g, wg_t, wu_t, wd, name, tm, carry=None, head=None, middle=CARRY_MIDDLE):
    s, d = h.shape
    f = wg_t.shape[0]
    tf = _ffn_chunk(f)

    def body(h_ref, g_ref, wg_ref, wu_ref, wd_ref, *refs):
        if head is None:
            o_ref, n_ref, gate_ref, up_ref, act_t_ref, act_ref = refs
        else:
            gf_ref, t_ref, o_ref, n_ref, gate_ref, up_ref, act_t_ref, loss_ref, dgf_ref, act_ref = refs
        x = h_ref[...]
        r = lax.rsqrt(jnp.mean(x * x, axis=-1, keepdims=True) + EPS)
        nb = (x * r * g_ref[...]).astype(BF16)
        n_ref[...] = nb
        for j in range(f // tf):
            sl = slice(j * tf, (j + 1) * tf)
            gate = _dot_nt(nb, wg_ref[sl, :])
            up = _dot_nt(nb, wu_ref[sl, :])
            gate_ref[:, sl] = gate.astype(BF16)
            up_ref[:, sl] = up.astype(BF16)
            act = gate * _sigmoid(gate) * up
            act = act.astype(BF16)
            act_ref[:, sl] = act
            act_t_ref[sl, :] = act.T
        h_out = x + 0.5 * _dot_nn(act_ref[...], wd_ref[...])
        o_ref[...] = h_out if head is None else _loss_head(h_out, gf_ref[...], t_ref[...], loss_ref, dgf_ref)

    row = lambda w: pl.BlockSpec((tm, w), lambda i: (i, 0))
    in_specs = [row(d), _whole((1, d)), _whole((f, d)), _whole((f, d)), _whole((f, d))]
    out_specs = [row(d), row(d), row(f), row(f), pl.BlockSpec((f, tm), lambda i: (0, i))]
    out_shape = ([jax.ShapeDtypeStruct((s, d), F32), jax.ShapeDtypeStruct((s, d), BF16)]
                 + [jax.ShapeDtypeStruct((s, f), BF16)] * 2 + [jax.ShapeDtypeStruct((f, s), BF16)])
    args = (h, g, wg_t, wu_t, wd)
    scratch = (pltpu.VMEM((tm, f), BF16),)
    if head is not None:
        in_specs += [_whole((1, d)), row(d)]
        out_specs += [pl.BlockSpec((1, 1), lambda i: (0, 0)), pl.BlockSpec((1, d), lambda i: (0, 0))]
        out_shape += [jax.ShapeDtypeStruct((1, 1), F32), jax.ShapeDtypeStruct((1, d), F32)]
        args += tuple(head)
    return _call(body, name, (s // tm,), in_specs, out_specs, out_shape, args, ("arbitrary",), carry, middle, scratch)


def _gate_grads(dh_ref, gate_ref, up_ref, wd_ref, dgate_ref, dup_ref, dgate_t_ref, dup_t_ref, dhh_ref, tf):
    dhh = (0.5 * dh_ref[...]).astype(BF16)
    dhh_ref[...] = dhh
    for j in range(gate_ref.shape[1] // tf):
        sl = slice(j * tf, (j + 1) * tf)
        gt = gate_ref[:, sl].astype(F32)
        u = up_ref[:, sl].astype(F32)
        dact = _dot_nt(dhh, wd_ref[sl, :])
        sg = _sigmoid(gt)
        dup = dact * (gt * sg)
        dgate = dact * u * (sg * (1.0 + gt * (1.0 - sg)))
        dup, dgate = dup.astype(BF16), dgate.astype(BF16)
        if dup_ref is not None:
            dup_ref[:, sl] = dup
            dgate_ref[:, sl] = dgate
        dup_t_ref[sl, :] = dup.T
        dgate_t_ref[sl, :] = dgate.T


def _input_grad(h_ref, dh_ref, dgate_ref, dup_ref, g_ref, wg_ref, wu_ref, o_ref, dg_ref, transposed=False):
    x = h_ref[...]
    r = lax.rsqrt(jnp.mean(x * x, axis=-1, keepdims=True) + EPS)
    xhat = x * r
    dot = _dot_tn if transposed else _dot_nn
    dn = dot(dgate_ref[...], wg_ref[...]) + dot(dup_ref[...], wu_ref[...])
    dxhat = dn * g_ref[...]
    o_ref[...] = dh_ref[...] + r * (dxhat - xhat * jnp.mean(dxhat * xhat, axis=-1, keepdims=True))

    @pl.when(pl.program_id(0) == 0)
    def _():
        dg_ref[...] = jnp.zeros_like(dg_ref)

    dg_ref[...] += jnp.sum(dn * xhat, axis=0, keepdims=True)


def _ffn_bwd(h_in, dh_out, gate, up, g, wg_t, wu_t, wd, w_out, name, tm):
    s, d = h_in.shape
    f = gate.shape[1]
    w = w_out.shape[0]
    tf = _ffn_chunk(f)

    def body(h_ref, dh_ref, gate_ref, up_ref, g_ref, wg_ref, wu_ref, wd_ref, wo_ref,
             o_ref, dg_ref, dgate_t_ref, dup_t_ref, dhh_ref, dmix_ref, dhb_ref, dgate_ref, dup_ref):
        _gate_grads(dh_ref, gate_ref, up_ref, wd_ref, dgate_ref, dup_ref, dgate_t_ref, dup_t_ref, dhh_ref, tf)
        _input_grad(h_ref, dh_ref, dgate_ref, dup_ref, g_ref, wg_ref, wu_ref, o_ref, dg_ref)
        dhb = o_ref[...].astype(BF16)
        dhb_ref[...] = dhb
        dmix_ref[...] = _dot_nt(dhb, wo_ref[...])

    row = lambda c: pl.BlockSpec((tm, c), lambda i: (i, 0))
    col = pl.BlockSpec((f, tm), lambda i: (0, i))
    return pl.pallas_call(
        body, name=name, grid=(s // tm,),
        in_specs=[row(d), row(d), row(f), row(f), _whole((1, d)), _whole((f, d)), _whole((f, d)), _whole((f, d)),
                  _whole((w, d))],
        out_specs=[row(d), pl.BlockSpec((1, d), lambda i: (0, 0)), col, col, row(d), row(w), row(d)],
        out_shape=[jax.ShapeDtypeStruct((s, d), F32), jax.ShapeDtypeStruct((1, d), F32),
                   jax.ShapeDtypeStruct((f, s), BF16), jax.ShapeDtypeStruct((f, s), BF16),
                   jax.ShapeDtypeStruct((s, d), BF16), jax.ShapeDtypeStruct((s, w), F32),
                   jax.ShapeDtypeStruct((s, d), BF16)],
        scratch_shapes=[pltpu.VMEM((tm, f), BF16), pltpu.VMEM((tm, f), BF16)],
        compiler_params=_cparams(("arbitrary",)),
    )(h_in, dh_out, gate, up, g, wg_t, wu_t, wd, w_out)


def _ffn_bwd_gates(dh_out, gate, up, wd, name, tm, carry=None):
    s, d = dh_out.shape
    f = gate.shape[1]
    tf = _ffn_chunk(f)

    def body(dh_ref, gate_ref, up_ref, wd_ref, dgate_t_ref, dup_t_ref, dhh_ref):
        _gate_grads(dh_ref, gate_ref, up_ref, wd_ref, None, None, dgate_t_ref, dup_t_ref, dhh_ref, tf)

    row = lambda w: pl.BlockSpec((tm, w), lambda i: (i, 0))
    col = pl.BlockSpec((f, tm), lambda i: (0, i))
    return _call(
        body, name, (s // tm,), [row(d), row(f), row(f), _whole((f, d))], [col, col, row(d)],
        [jax.ShapeDtypeStruct((f, s), BF16)] * 2 + [jax.ShapeDtypeStruct((s, d), BF16)],
        (dh_out, gate, up, wd), ("arbitrary",), carry)


def _ffn_bwd_input(h_in, dh_out, dgate_t, dup_t, g, wg_t, wu_t, name, tm, carry=None):
    s, d = h_in.shape
    f = dgate_t.shape[0]

    row = lambda w: pl.BlockSpec((tm, w), lambda i: (i, 0))
    col = pl.BlockSpec((f, tm), lambda i: (0, i))
    return _call(
        functools.partial(_input_grad, transposed=True), name, (s // tm,),
        [row(d), row(d), col, col, _whole((1, d)), _whole((f, d)), _whole((f, d))],
        [row(d), pl.BlockSpec((1, d), lambda i: (0, 0))],
        [jax.ShapeDtypeStruct((s, d), F32), jax.ShapeDtypeStruct((1, d), F32)],
        (h_in, dh_out, dgate_t, dup_t, g, wg_t, wu_t), ("arbitrary",), carry)


def _wgrad_rs1(pos, a_ts, b, name, carry=None):
    n_a = len(a_ts)
    f, s = a_ts[0].shape
    d = b.shape[1]
    fk = f // N_DEV
    nc_in = 0 if carry is None else len(carry.inputs)
    nc_out = 0 if carry is None else len(carry.out_shapes)

    def body(pos_ref, *refs):
        a_refs, b_ref, refs = refs[:n_a], refs[n_a], refs[n_a + 1:]
        c_in = refs[:nc_in]
        outs = refs[nc_in:nc_in + 2 * n_a]
        c_out = refs[nc_in + 2 * n_a:nc_in + 2 * n_a + nc_out]
        stage, land, send_sems, recv_sems = refs[nc_in + 2 * n_a + nc_out:nc_in + 2 * n_a + nc_out + 4]
        c_sems = refs[nc_in + 2 * n_a + nc_out + 4:]
        t = pl.program_id(0)
        which, step = t // 8, t % 8
        slot = 4 * which + step % 4
        x, y, c = lax.axis_index("x"), lax.axis_index("y"), lax.axis_index("c")

        def push(k):
            return pltpu.make_async_remote_copy(src_ref=stage.at[k], dst_ref=land.at[k], send_sem=send_sems.at[k],
                                                recv_sem=recv_sems.at[k], device_id=(x, y, 1 - c), device_id_type=MESH)

        if carry is not None:
            @pl.when(t == 0)
            def _():
                carry.start(c_in, c_out, c_sems)

        for j in range(n_a):
            @pl.when(which == j)
            def _(j=j):
                g = _dot_nn(a_refs[j][...], b_ref[...])

                @pl.when(step < 4)
                def _():
                    stage[slot] = g.astype(BF16)
                    push(slot).start()

                @pl.when(step >= 4)
                def _():
                    push(slot).wait_recv()
                    p = g + land[slot].astype(F32)
                    outs[2 * j][...] = p
                    outs[2 * j + 1][...] = p.astype(BF16)

        if carry is not None:
            @pl.when(t == round(CARRY_MIDDLE * (8 * n_a - 1)))
            def _():
                carry.middle(c_in, c_out, c_sems)

        @pl.when(t == 8 * n_a - 1)
        def _():
            for k in range(4 * n_a):
                push(k).wait_send()
            if carry is not None:
                carry.finish(c_in, c_out, c_sems)

    def shard(j):
        def index_map(t, pos_ref):
            step = jnp.clip(t - 8 * j, 0, 7)
            return 4 * ((step % 4) // 2) + 2 * (step % 2) + jnp.where(step < 4, 1 - pos_ref[2], pos_ref[2]), 0
        return index_map

    out = lambda j: pl.BlockSpec((None, fk, d), lambda t, pos_ref, j=j: (jnp.clip(t - 8 * j - 4, 0, 3), 0, 0))
    return pl.pallas_call(
        body, name=name,
        grid_spec=pltpu.PrefetchScalarGridSpec(
            num_scalar_prefetch=1, grid=(8 * n_a,),
            in_specs=[pl.BlockSpec((fk, s), shard(j)) for j in range(n_a)]
            + [pl.BlockSpec((s, d), lambda t, pos_ref: (0, 0), pipeline_mode=pl.Buffered(1))] + [_ANY] * nc_in,
            out_specs=[out(j) for j in range(n_a) for _ in range(2)] + [_ANY] * nc_out,
            scratch_shapes=[pltpu.VMEM((4 * n_a, fk, d), BF16), pltpu.VMEM((4 * n_a, fk, d), BF16),
                            pltpu.SemaphoreType.DMA((4 * n_a,)), pltpu.SemaphoreType.DMA((4 * n_a,))]
            + ([] if carry is None else list(carry.sems))),
        out_shape=[jax.ShapeDtypeStruct((4, fk, d), dt) for _ in range(n_a) for dt in (F32, BF16)]
        + ([] if carry is None else list(carry.out_shapes)),
        compiler_params=_cparams(("arbitrary",)),
    )(pos, *a_ts, b, *([] if carry is None else carry.inputs))


def _rope(t, c, sa, sb, reps):
    c, sa, sb = (jnp.tile(v, (1, reps)) if reps > 1 else v for v in (c, sa, sb))
    w = t.shape[1]
    return t * c + pltpu.roll(t, w - 8, 1) * sa + pltpu.roll(t, 8, 1) * sb


def _rope_bwd(dt, c, sa, sb, reps):
    c, sa, sb = (jnp.tile(v, (1, reps)) if reps > 1 else v for v in (c, sa, sb))
    w = dt.shape[1]
    return dt * c + pltpu.roll(dt * sa, 8, 1) + pltpu.roll(dt * sb, w - 8, 1)


def _ffn_down_mix_in(x, act_t, wd, g, win_t, tabs, name, tm, carry=None):
    s, d = x.shape
    f = wd.shape[0]
    n_in = win_t.shape[0]

    def body(x_ref, act_ref, wd_ref, g_ref, w_ref, c_ref, sa_ref, sb_ref, h_ref, q_ref, k_ref, v_ref, pc_ref, n_ref):
        x = x_ref[...] + 0.5 * _dot_tn(act_ref[...], wd_ref[...])
        h_ref[...] = x
        r = lax.rsqrt(jnp.mean(x * x, axis=-1, keepdims=True) + EPS)
        nb = (x * r * g_ref[...]).astype(BF16)
        n_ref[...] = nb
        u = _dot_nt(nb, w_ref[...])
        c, sa, sb = c_ref[...], sa_ref[...], sb_ref[...]
        q_ref[...] = _rope(u[:, :ATTN_W], c, sa, sb, ATTN_W // 128).astype(BF16)
        k_ref[...] = _rope(u[:, ATTN_W:ATTN_W + KV_W], c, sa, sb, 1).astype(BF16)
        v_ref[...] = u[:, ATTN_W + KV_W:ATTN_W + 2 * KV_W].astype(BF16)
        pc_ref[...] = u[:, ATTN_W + 2 * KV_W:]

    row = lambda w: pl.BlockSpec((tm, w), lambda i: (i, 0))
    return _call(
        body, name, (s // tm,),
        [row(d), pl.BlockSpec((f, tm), lambda i: (0, i)), _whole((f, d)), _whole((1, d)), _whole((n_in, d)),
         row(128), row(128), row(128)],
        [row(d), row(ATTN_W), row(KV_W), row(KV_W), row(POOL_W), row(d)],
        [jax.ShapeDtypeStruct((s, d), F32), jax.ShapeDtypeStruct((s, ATTN_W), BF16),
         jax.ShapeDtypeStruct((s, KV_W), BF16), jax.ShapeDtypeStruct((s, KV_W), BF16),
         jax.ShapeDtypeStruct((s, POOL_W), F32), jax.ShapeDtypeStruct((s, d), BF16)],
        (x, act_t, wd, g, win_t, *tabs), ("arbitrary",), carry)


def _band_mask(n, nb, transposed):
    shape = (3 * BLK, 2 * BLK) if transposed else (2 * BLK, 3 * BLK)
    i = lax.broadcasted_iota(jnp.int32, shape, 1 if transposed else 0) % BLK
    j = lax.broadcasted_iota(jnp.int32, shape, 0 if transposed else 1)
    kpos = (n - 1) * BLK + j
    return (j >= i) & (j <= i + 2 * BLK) & (kpos >= 0) & (kpos < nb * BLK)


def _block_diag(t, kh):
    tf = t.astype(F32)
    tr = pltpu.roll(tf, HEAD_DIM, 1)
    lo = lax.broadcasted_iota(jnp.int32, tf.shape, 1) < HEAD_DIM
    top, bot = (tf, tr) if kh == 0 else (tr, tf)
    return jnp.concatenate([jnp.where(lo, top, 0.0), jnp.where(lo, 0.0, bot)], axis=0).astype(BF16)


def _fold_diag(tbd):
    lo = lax.broadcasted_iota(jnp.int32, (3 * BLK, 2 * HEAD_DIM), 1) < HEAD_DIM
    t = jnp.where(lo, tbd[:3 * BLK], tbd[3 * BLK:])
    return t + pltpu.roll(t, HEAD_DIM, 1)


def _stack_pairs(x, kh):
    return jnp.concatenate([x[:, (2 * kh) * 128:(2 * kh + 1) * 128], x[:, (2 * kh + 1) * 128:(2 * kh + 2) * 128]], axis=0)


def _sink_of(sink_ref, kh, half, axis):
    shape = (2 * BLK, 1) if axis == 0 else (1, 2 * BLK)
    first = lax.broadcasted_iota(jnp.int32, shape, axis) < BLK
    return jnp.where(first, sink_ref[0, GROUP * kh + half], sink_ref[0, GROUP * kh + 2 + half])


def _softmax_sink(sc, valid, sink, axis):
    sc = jnp.where(valid, sc, -1e30)
    m = jnp.maximum(jnp.max(sc, axis=axis, keepdims=True), sink)
    e = jnp.exp(sc - m)
    es = jnp.exp(sink - m)
    inv = 1.0 / (jnp.sum(e, axis=axis, keepdims=True) + es)
    return e * inv, es * inv


def _attn_blocks_per_step(nb):
    return next(nq for nq in (4, 2, 1) if nb % nq == 0)


def _band_specs(nq, nb, w, col=0):
    return [pl.BlockSpec((BLK, w), lambda m: (jnp.maximum(nq * m - 1, 0), col)),
            pl.BlockSpec((nq * BLK, w), lambda m: (m, col)),
            pl.BlockSpec((BLK, w), lambda m: (jnp.minimum(nq * m + nq, nb - 1), col))]


def _attn_pool_fwd(h, w_out, q, k, v, pc, sink, pool_w, pool_scale, pband, name, carry=None, middle=CARRY_MIDDLE):
    s, d = h.shape
    nb = s // BLK
    nq = _attn_blocks_per_step(nb)

    def body(sink_ref, q_ref, k0, k1, k2, v0, v1, v2, p0, p1, p2, pw_ref, ps_ref, pb_ref, h_ref, wo_ref,
             h_out_ref, o_t_ref, o_ref):
        kall = jnp.concatenate([k0[...], k1[...], k2[...]], axis=0)
        vall = jnp.concatenate([v0[...], v1[...], v2[...]], axis=0)
        pall = jnp.concatenate([p0[...], p1[...], p2[...]], axis=0).astype(BF16)
        qall = q_ref[...] * SCORE_SCALE
        for j in range(nq):
            n = pl.program_id(0) * nq + j
            rows, band = slice(j * BLK, (j + 1) * BLK), slice(j * BLK, (j + 3) * BLK)
            valid = _band_mask(n, nb, False)
            kb, vb, qs = kall[band], vall[band], qall[rows]
            for kh in range(N_KV):
                sc = _dot_nt(_stack_pairs(qs, kh), _block_diag(kb, kh))
                p = [_softmax_sink(sc[:, half * 3 * BLK:(half + 1) * 3 * BLK], valid,
                                   _sink_of(sink_ref, kh, half, 0), 1)[0] for half in range(2)]
                o2 = _dot_nn(jnp.concatenate(p, axis=1).astype(BF16), _block_diag(vb, kh)).astype(BF16)
                o_ref[rows, (2 * kh) * 128:(2 * kh + 1) * 128] = o2[:BLK]
                o_ref[rows, (2 * kh + 1) * 128:(2 * kh + 2) * 128] = o2[BLK:]
            ext = pall[band]
            var = _variant_index(n, nb)
            for gi in range(POOL_G):
                gsl = slice(gi * POOL_GW, (gi + 1) * POOL_GW)
                dg = _dot_nn(pb_ref[var, gi], ext[:, gsl])
                yg = _dot_nn(dg.astype(BF16), pw_ref[gi].astype(BF16))
                o_ref[rows, ATTN_W + gi * POOL_GW:ATTN_W + (gi + 1) * POOL_GW] = (yg * ps_ref[:, gsl]).astype(BF16)
        mix = o_ref[...]
        o_t_ref[...] = mix.T
        h_out_ref[...] = h_ref[...] + _dot_nn(mix, wo_ref[...])

    mix_w = ATTN_W + POOL_W
    return _call(
        body, name, (nb // nq,),
        [pl.BlockSpec(memory_space=pltpu.SMEM), pl.BlockSpec((nq * BLK, ATTN_W), lambda m: (m, 0)),
         *_band_specs(nq, nb, KV_W), *_band_specs(nq, nb, KV_W), *_band_specs(nq, nb, POOL_W),
         _whole((POOL_G, POOL_GW, POOL_GW)), _whole((1, POOL_W)), _whole(pband.shape),
         pl.BlockSpec((nq * BLK, d), lambda m: (m, 0)), _whole((mix_w, d))],
        [pl.BlockSpec((nq * BLK, d), lambda m: (m, 0)), pl.BlockSpec((mix_w, nq * BLK), lambda m: (0, m))],
        [jax.ShapeDtypeStruct((s, d), F32), jax.ShapeDtypeStruct((mix_w, s), BF16)],
        (sink, q, k, k, k, v, v, v, pc, pc, pc, pool_w, pool_scale, pband, h, w_out), ("arbitrary",), carry, middle,
        scratch=(pltpu.VMEM((nq * BLK, mix_w), BF16),))


def _attn_pool_bwd(q, k, v, pc, dmix, sink, pool_w, pool_scale, pband, ptband, name, carry=None):
    s = q.shape[0]
    nb = s // BLK
    nq = _attn_blocks_per_step(nb)

    def body(sink_ref, q_ref, k0, k1, k2, v0, v1, v2, p0, p1, p2, da_ref, d0, d1, d2, pw_ref, ps_ref, pb_ref, ptb_ref,
             dq_ref, dk_ref, dv_ref, dpc_ref, dsink_ref, dpw_ref, dps_ref):
        @pl.when(pl.program_id(0) == 0)
        def _():
            dsink_ref[...] = jnp.zeros_like(dsink_ref)
            dpw_ref[...] = jnp.zeros_like(dpw_ref)
            dps_ref[...] = jnp.zeros_like(dps_ref)

        kall = jnp.concatenate([k0[...], k1[...], k2[...]], axis=0)
        vall = jnp.concatenate([v0[...], v1[...], v2[...]], axis=0)
        pall = jnp.concatenate([p0[...], p1[...], p2[...]], axis=0).astype(BF16)
        dpall = jnp.concatenate([d0[...], d1[...], d2[...]], axis=0)
        lo = lax.broadcasted_iota(jnp.int32, (3 * BLK, KV_W), 1) < HEAD_DIM
        for j in range(nq):
            n = pl.program_id(0) * nq + j
            rows, band = slice(j * BLK, (j + 1) * BLK), slice(j * BLK, (j + 3) * BLK)
            valid = _band_mask(n, nb, True)
            kb, vb, qb = kall[band], vall[band], q_ref[rows, :]
            qs = qb * SCORE_SCALE
            da = da_ref[rows, :].astype(BF16)
            dk_fold, dv_fold = [], []
            for kh in range(N_KV):
                kbd, vbd = _block_diag(kb, kh), _block_diag(vb, kh)
                q2, do2 = _stack_pairs(qb, kh), _stack_pairs(da, kh)
                sc_t = _dot_nt(kbd, _stack_pairs(qs, kh))
                dp_t = _dot_nt(vbd, do2)
                p_t, ds_t = [], []
                for half in range(2):
                    keys = slice(half * 3 * BLK, (half + 1) * 3 * BLK)
                    p, ps = _softmax_sink(sc_t[keys], valid, _sink_of(sink_ref, kh, half, 1), 0)
                    delta = jnp.sum(p * dp_t[keys], axis=0, keepdims=True)
                    p_t.append(p.astype(BF16))
                    ds_t.append((p * (dp_t[keys] - delta)).astype(BF16))
                    dsk = -ps * delta
                    for pair in range(2):
                        h = GROUP * kh + 2 * pair + half
                        part = jnp.sum(dsk[:, pair * BLK:(pair + 1) * BLK], axis=1, keepdims=True)
                        dsink_ref[h:h + 1, :] += jnp.broadcast_to(part, (1, 128))
                p_t = jnp.concatenate(p_t, axis=0)
                ds_t = jnp.concatenate(ds_t, axis=0)
                dq2 = _dot_tn(ds_t, kbd) * SCORE_SCALE
                dq_ref[rows, (2 * kh) * 128:(2 * kh + 1) * 128] = dq2[:BLK]
                dq_ref[rows, (2 * kh + 1) * 128:(2 * kh + 2) * 128] = dq2[BLK:]
                dk_fold.append(_fold_diag(_dot_nn(ds_t, q2)) * SCORE_SCALE)
                dv_fold.append(_fold_diag(_dot_nn(p_t, do2)))
            dk_all = jnp.where(lo, dk_fold[0], dk_fold[1])
            dv_all = jnp.where(lo, dv_fold[0], dv_fold[1])
            for t in range(3):
                dk_ref[j, t] = dk_all[t * BLK:(t + 1) * BLK]
                dv_ref[j, t] = dv_all[t * BLK:(t + 1) * BLK]
            ext, dpe = pall[band], dpall[band]
            dpc_cur = dpall[(j + 1) * BLK:(j + 2) * BLK]
            var = _variant_index(n, nb)
            for gi in range(POOL_G):
                gsl = slice(gi * POOL_GW, (gi + 1) * POOL_GW)
                wg = pw_ref[gi].astype(BF16)
                sc = ps_ref[:, gsl]
                dgb = _dot_nn(pb_ref[var, gi], ext[:, gsl]).astype(BF16)
                yg = _dot_nn(dgb, wg)
                dps_ref[:, gsl] += jnp.sum(dpc_cur[:, gsl] * yg, axis=0, keepdims=True)
                dpw_ref[gi] += _dot_tn(dgb, (dpc_cur[:, gsl] * sc).astype(BF16))
                dd = _dot_nt((dpe[:, gsl] * sc).astype(BF16), wg)
                dpc_ref[rows, gsl] = _dot_nn(ptb_ref[var, gi], dd.astype(BF16))

    fixed = lambda shape: pl.BlockSpec(shape, lambda m: (0,) * len(shape))
    return _call(
        body, name, (nb // nq,),
        [pl.BlockSpec(memory_space=pltpu.SMEM), pl.BlockSpec((nq * BLK, ATTN_W), lambda m: (m, 0)),
         *_band_specs(nq, nb, KV_W), *_band_specs(nq, nb, KV_W), *_band_specs(nq, nb, POOL_W),
         pl.BlockSpec((nq * BLK, ATTN_W), lambda m: (m, 0)), *_band_specs(nq, nb, POOL_W, 1),
         _whole((POOL_G, POOL_GW, POOL_GW)), _whole((1, POOL_W)), _whole(pband.shape), _whole(ptband.shape)],
        [pl.BlockSpec((nq * BLK, ATTN_W), lambda m: (m, 0)),
         pl.BlockSpec((nq, 3, BLK, KV_W), lambda m: (m, 0, 0, 0)),
         pl.BlockSpec((nq, 3, BLK, KV_W), lambda m: (m, 0, 0, 0)),
         pl.BlockSpec((nq * BLK, POOL_W), lambda m: (m, 0)),
         fixed((N_HEADS, 128)), fixed((POOL_G, POOL_GW, POOL_GW)), fixed((1, POOL_W))],
        [jax.ShapeDtypeStruct((s, ATTN_W), F32), jax.ShapeDtypeStruct((nb, 3, BLK, KV_W), F32),
         jax.ShapeDtypeStruct((nb, 3, BLK, KV_W), F32), jax.ShapeDtypeStruct((s, POOL_W), F32),
         jax.ShapeDtypeStruct((N_HEADS, 128), F32),
         jax.ShapeDtypeStruct((POOL_G, POOL_GW, POOL_GW), F32), jax.ShapeDtypeStruct((1, POOL_W), F32)],
        (sink, q, k, k, k, v, v, v, pc, pc, pc, dmix, dmix, dmix, dmix, pool_w, pool_scale, pband, ptband),
        ("arbitrary",), carry)


def _mix_in_bwd(h, dh, g, win_t, dq, dkp, dvp, dpc, tabs, name, tm, carry=None):
    s, d = h.shape
    nb = s // BLK
    nt = tm // BLK
    n_in = win_t.shape[0]

    def band_sum(n, before, own, after, prev_last, next_first):
        lo = (n > 0).astype(F32)
        hi = (n < s // tm - 1).astype(F32)
        blocks = []
        for b in range(nt):
            from_prev = prev_last[...] * lo if b == 0 else before[b - 1]
            from_next = next_first[...] * hi if b == nt - 1 else after[b + 1]
            blocks.append(from_prev + own[b] + from_next)
        return jnp.concatenate(blocks, axis=0)

    def body(h_ref, dh_ref, g_ref, w_ref, dq_ref, k2, k1, k0, kp, kn, v2, v1, v0, vp, vn, dpc_ref, c_ref, sa_ref,
             sb_ref, o_ref, du_ref, dg_ref):
        n = pl.program_id(0)
        dk = band_sum(n, k2, k1, k0, kp, kn)
        dv = band_sum(n, v2, v1, v0, vp, vn)
        c, sa, sb = c_ref[...], sa_ref[...], sb_ref[...]
        du = jnp.concatenate([_rope_bwd(dq_ref[...], c, sa, sb, ATTN_W // 128), _rope_bwd(dk, c, sa, sb, 1), dv,
                              dpc_ref[...]], axis=1)
        du_ref[...] = du.T.astype(BF16)
        dn = _dot_nn(du.astype(BF16), w_ref[...])
        x = h_ref[...]
        r = lax.rsqrt(jnp.mean(x * x, axis=-1, keepdims=True) + EPS)
        xhat = x * r
        dxhat = dn * g_ref[...]
        o_ref[...] = dh_ref[...] + r * (dxhat - xhat * jnp.mean(dxhat * xhat, axis=-1, keepdims=True))

        @pl.when(n == 0)
        def _():
            dg_ref[...] = jnp.zeros_like(dg_ref)

        dg_ref[...] += jnp.sum(dn * xhat, axis=0, keepdims=True)

    row = lambda w: pl.BlockSpec((tm, w), lambda n: (n, 0))
    slot = lambda t: pl.BlockSpec((nt, None, BLK, KV_W), lambda n, t=t: (n, t, 0, 0))
    parts = [slot(2), slot(1), slot(0),
             pl.BlockSpec((None, None, BLK, KV_W), lambda n: (jnp.maximum(nt * n - 1, 0), 2, 0, 0)),
             pl.BlockSpec((None, None, BLK, KV_W), lambda n: (jnp.minimum(nt * n + nt, nb - 1), 0, 0, 0))]
    return _call(
        body, name, (s // tm,),
        [row(d), row(d), _whole((1, d)), _whole((n_in, d)), row(ATTN_W), *parts, *parts, row(POOL_W),
         row(128), row(128), row(128)],
        [row(d), pl.BlockSpec((n_in, tm), lambda n: (0, n)), pl.BlockSpec((1, d), lambda n: (0, 0))],
        [jax.ShapeDtypeStruct((s, d), F32), jax.ShapeDtypeStruct((n_in, s), BF16), jax.ShapeDtypeStruct((1, d), F32)],
        (h, dh, g, win_t, dq, *[dkp] * 5, *[dvp] * 5, dpc, *tabs), ("arbitrary",), carry)


def _adam_math(w, g, m, v):
    m = ADAM_B1 * m + (1.0 - ADAM_B1) * g
    v = ADAM_B2 * v + (1.0 - ADAM_B2) * (g * g)
    m_hat = m / (1.0 - ADAM_B1 ** ADAM_STEP)
    v_hat = v / (1.0 - ADAM_B2 ** ADAM_STEP)
    delta = -ADAM_LR * (m_hat / (jnp.sqrt(v_hat) + ADAM_EPS) + ADAM_WD * w)
    return delta, m, v


def _adam_small(w, parts, late, m, v, name):
    rows, cols = w.shape

    def body(w_ref, p_ref, l_ref, m_ref, v_ref, g_ref, d_ref, nm_ref, nv_ref):
        g, gl = p_ref[0], l_ref[0]
        for k in range(1, N_DEV):
            g = g + p_ref[k]
            gl = gl + l_ref[k]
        g_ref[...] = g
        g_ref[SMALL_NORM1:SMALL_NORM1 + 8, :] = g[SMALL_NORM1:SMALL_NORM1 + 8] + gl
        d_ref[...], nm_ref[...], nv_ref[...] = _adam_math(w_ref[...], g_ref[...], m_ref[...], v_ref[...])

    return pl.pallas_call(
        body, name=name, out_shape=[jax.ShapeDtypeStruct((rows, cols), F32)] * 4,
    )(w, parts, late, m, v)


SMALL_NORM1 = 512


def _pack_small(norm1, normm, norm2, normf, sink, pool_w, pool_scale, loss=None):
    scale_rows = jnp.pad(pool_scale.reshape(4, 128), ((0, 4), (0, 0)))
    last_rows = jnp.pad(sink.reshape(1, N_HEADS), ((0, 7), (0, 128 - N_HEADS)))
    if loss is not None:
        last_rows = last_rows + jnp.pad(loss.reshape(1, 1), ((1, 6), (0, 127)))
    return jnp.concatenate([pool_w.reshape(512, 128), norm1.reshape(8, 128), normm.reshape(8, 128),
                            norm2.reshape(8, 128), normf.reshape(8, 128), scale_rows, last_rows], axis=0)


def _unpack_small(p):
    return dict(pool_w=p[:512].reshape(1, POOL_G, POOL_GW, POOL_GW), ffn1_norm=p[512:520].reshape(1, 1024),
                mix_norm=p[520:528].reshape(1, 1024), ffn2_norm=p[528:536].reshape(1, 1024),
                final_norm=p[536:544].reshape(1024), pool_scale=p[544:548].reshape(1, POOL_W),
                sink_logits=p[552, :N_HEADS].reshape(1, N_HEADS), loss=p[553, 0])


def kernel(x, ffn1_norm, ffn1_w_gate, ffn1_w_up, ffn1_w_down, mix_norm, w_in, sink_logits, pool_w, pool_scale, w_out, ffn2_norm, ffn2_w_gate, ffn2_w_up, ffn2_w_down, final_norm, loss_target, m_ffn1_norm, m_ffn1_w_gate, m_ffn1_w_up, m_ffn1_w_down, m_mix_norm, m_w_in, m_sink_logits, m_pool_w, m_pool_scale, m_w_out, m_ffn2_norm, m_ffn2_w_gate, m_ffn2_w_up, m_ffn2_w_down, m_final_norm, v_ffn1_norm, v_ffn1_w_gate, v_ffn1_w_up, v_ffn1_w_down, v_mix_norm, v_w_in, v_sink_logits, v_pool_w, v_pool_scale, v_w_out, v_ffn2_norm, v_ffn2_w_gate, v_ffn2_w_up, v_ffn2_w_down, v_final_norm):
    s, d = x.shape[1], x.shape[2]
    tm = min(512, s)
    tm_bwd = min(256, s)
    pos = jnp.stack([lax.axis_index("x"), lax.axis_index("y"), lax.axis_index("c")]).astype(jnp.int32)

    t_bf = lambda w: w[0].T.astype(BF16)
    full = lambda a: a.reshape(N_DEV * a.shape[1], d)
    (wg1,) = map(full, _run_exchange(_AllGather([t_bf(ffn1_w_gate)]), "gather_ffn1_gate"))

    tabs = _rope_tables(s)
    pband, ptband = _pool_tables(s)
    g1, gm, g2, gf = ffn1_norm, mix_norm, ffn2_norm, final_norm.reshape(1, d)

    x0 = x[0]
    n1, gate1, wu1 = _ffn_gate(x0, g1, wg1, "ffn1_gate", tm, carry=_AllGather([t_bf(ffn1_w_up)]))
    up1, act1_t, wd1, win_t = _ffn_up(n1, gate1, full(wu1), "ffn1_up", tm,
                                            carry=_AllGather([ffn1_w_down[0].astype(BF16), t_bf(w_in)]))
    wu1, wd1, win_t = full(wu1), full(wd1), full(win_t)
    h1, q, k, v, pc, n2, wout, wg2 = _ffn_down_mix_in(
        x0, act1_t, wd1, gm, win_t, tabs, "ffn1_down_mix_in", tm,
        carry=_AllGather([w_out[0].astype(BF16), t_bf(ffn2_w_gate)]))
    wout, wg2 = full(wout), full(wg2)
    h2, mix_t, *gathered = _attn_pool_fwd(h1, wout, q, k, v, pc, sink_logits, pool_w[0], pool_scale, pband,
                                          "attn_pool_fwd",
                                          carry=_AllGather([t_bf(ffn2_w_up), ffn2_w_down[0].astype(BF16)]))
    wu2, wd2 = map(full, gathered)
    dh3, n3, gate2, up2, act2_t, loss_part, dgf = _ffn_fwd(h2, g2, wg2, wu2, wd2, "ffn2_fwd", tm,
                                                           head=(gf, loss_target[0]))

    sum1, recv2 = {}, {}

    def stage2(keys):
        return _RsStage2([sum1[key][1] for key in keys])

    dh2, dg2, dgate2_t, dup2_t, dhh3, dmix, dh2b = _ffn_bwd(h2, dh3, gate2, up2, g2, wg2, wu2, wd2, wout, "ffn2_bwd",
                                                            tm_bwd)
    both = _wgrad_rs1(pos, [dgate2_t, dup2_t], n3, "wgrad_gate_up2")
    sum1["g2"], sum1["u2"] = both[:2], both[2:]
    sum1["d2"] = _wgrad_rs1(pos, [act2_t], dhh3, "wgrad_down2")
    sum1["out"] = _wgrad_rs1(pos, [mix_t], dh2b, "wgrad_out")
    dq, dkp, dvp, dpc, dsink, dpw, dps, *r2 = _attn_pool_bwd(
        q, k, v, pc, dmix, sink_logits, pool_w[0], pool_scale, pband, ptband, "attn_pool_bwd",
        carry=stage2(["g2", "u2", "d2"]))
    recv2.update(zip(["g2", "u2", "d2"], r2))
    dh1, du_t, dgm, recv2["out"] = _mix_in_bwd(h1, dh2, gm, win_t, dq, dkp, dvp, dpc, tabs, "mix_in_bwd", tm,
                                               carry=stage2(["out"]))
    sum1["in"] = _wgrad_rs1(pos, [du_t], n2, "wgrad_in")
    small_part = _pack_small(jnp.zeros_like(dgm), dgm, dg2, dgf, dsink[:, 0], dpw, dps, loss_part)
    dgate1_t, dup1_t, dhh1, recv2["in"] = _ffn_bwd_gates(dh1, gate1, up1, wd1, "ffn1_bwd_gates", tm,
                                                         carry=stage2(["in"]))
    *sum1["g1"], small_all = _wgrad_rs1(pos, [dgate1_t], n1, "wgrad_gate1", carry=_AllGather([small_part]))
    *sum1["u1"], recv2["g1"] = _wgrad_rs1(pos, [dup1_t], n1, "wgrad_up1", carry=stage2(["g1"]))
    *sum1["d1"], recv2["u1"] = _wgrad_rs1(pos, [act1_t], dhh1, "wgrad_down1", carry=stage2(["u1"]))
    dx, dg1, recv2["d1"] = _ffn_bwd_input(x0, dh1, dgate1_t, dup1_t, g1, wg1, wu1, "ffn1_bwd_input", tm,
                                          carry=stage2(["d1"]))

    (dg1_all,) = _run_exchange(_DirectGather([dg1.reshape(8, 128)]), "gather_norm1_grad")
    pk = lambda a, b, c_, e, s_, pw_, psc: _pack_small(a, b, c_, e, s_[0], pw_[0], psc)
    small_w = pk(ffn1_norm, mix_norm, ffn2_norm, final_norm, sink_logits, pool_w, pool_scale)
    small_m = pk(m_ffn1_norm, m_mix_norm, m_ffn2_norm, m_final_norm, m_sink_logits, m_pool_w, m_pool_scale)
    small_v = pk(v_ffn1_norm, v_mix_norm, v_ffn2_norm, v_final_norm, v_sink_logits, v_pool_w, v_pool_scale)
    sg, sd, sm, sv = [_unpack_small(a)
                      for a in _adam_small(small_w, small_all, dg1_all, small_m, small_v, "adam_small")]

    big = {}
    keys = ["g1", "u1", "d1", "g2", "u2", "d2", "in", "out"]
    names = ["ffn1_w_gate", "ffn1_w_up", "ffn1_w_down", "ffn2_w_gate", "ffn2_w_up", "ffn2_w_down", "w_in", "w_out"]
    transposed = [True, True, False, True, True, False, True, False]
    ws = [ffn1_w_gate, ffn1_w_up, ffn1_w_down, ffn2_w_gate, ffn2_w_up, ffn2_w_down, w_in, w_out]
    ms = [m_ffn1_w_gate, m_ffn1_w_up, m_ffn1_w_down, m_ffn2_w_gate, m_ffn2_w_up, m_ffn2_w_down, m_w_in, m_w_out]
    vs = [v_ffn1_w_gate, v_ffn1_w_up, v_ffn1_w_down, v_ffn2_w_gate, v_ffn2_w_up, v_ffn2_w_down, v_w_in, v_w_out]
    for key, nm, tr, w, m, vv in zip(keys, names, transposed, ws, ms, vs):
        view = (lambda a: jnp.swapaxes(a, 1, 2)[0]) if tr else (lambda a: a[0])
        back = (lambda a: jnp.swapaxes(a[None], 1, 2)) if tr else (lambda a: a[None])
        res = _rs_sum2_adam(pos, sum1[key][0], recv2[key], view(w), view(m), view(vv), "adam_" + nm)
        big[nm] = tuple(back(a) for a in res)

    loss = sg["loss"]
    all_names = ["ffn1_norm", "ffn1_w_gate", "ffn1_w_up", "ffn1_w_down", "mix_norm", "w_in", "sink_logits", "pool_w",
                 "pool_scale", "w_out", "ffn2_norm", "ffn2_w_gate", "ffn2_w_up", "ffn2_w_down", "final_norm"]
    outs = [loss, dx[None]]
    for idx, src in enumerate((sg, sd, sm, sv)):
        for nm in all_names:
            outs.append(big[nm][idx] if nm in big else src[nm])
    return tuple(outs)
```

```python
import functools

import jax
import jax.numpy as jnp
import numpy as np
from jax import lax
from jax.experimental import pallas as pl
from jax.experimental.pallas import tpu as pltpu

F32 = jnp.float32
BF16 = jnp.bfloat16
MESH = pl.DeviceIdType.MESH
N_DEV = 8

EPS = 1e-6
HEAD_DIM = 64
N_HEADS = 8
N_KV = 2
GROUP = N_HEADS // N_KV
ATTN_W = N_HEADS * HEAD_DIM
KV_W = N_KV * HEAD_DIM
POOL_W = 512
POOL_G = 4
POOL_GW = POOL_W // POOL_G
POOL_WINDOWS = (2, 4, 8, 16)
BLK = 128
ROT = 16
ROPE_THETA = 500000.0
SCORE_SCALE = HEAD_DIM ** -0.5

ADAM_LR, ADAM_B1, ADAM_B2, ADAM_EPS, ADAM_WD, ADAM_STEP = 0.001, 0.9, 0.999, 1e-08, 0.01, 10

VMEM_LIMIT = 56 * 1024 * 1024


def _cparams(sem=None, **kw):
    if sem is not None:
        kw["dimension_semantics"] = sem
    return pltpu.CompilerParams(vmem_limit_bytes=VMEM_LIMIT, **kw)


def _whole(shape):
    nd = len(shape)
    return pl.BlockSpec(shape, lambda *_: (0,) * nd, pipeline_mode=pl.Buffered(1))


def _sigmoid(z):
    return 1.0 / (1.0 + jnp.exp(-z))


def _dot_nt(a, b):
    return lax.dot_general(a, b, (((1,), (1,)), ((), ())), preferred_element_type=F32)


def _dot_nn(a, b):
    return lax.dot_general(a, b, (((1,), (0,)), ((), ())), preferred_element_type=F32)


def _dot_tn(a, b):
    return lax.dot_general(a, b, (((0,), (0,)), ((), ())), preferred_element_type=F32)


def _rope_tables(s):
    inv_freq = ROPE_THETA ** (-np.arange(0, ROT, 2, dtype=np.float64) / ROT)
    ang = np.arange(s, dtype=np.float64)[:, None] * inv_freq[None, :]
    c = np.ones((s, HEAD_DIM)); sa = np.zeros((s, HEAD_DIM)); sb = np.zeros((s, HEAD_DIM))
    c[:, :8] = np.cos(ang); c[:, 8:16] = np.cos(ang)
    sa[:, :8] = -np.sin(ang)
    sb[:, 8:16] = np.sin(ang)
    t = lambda a: jnp.asarray(np.tile(a, (1, 2)).astype(np.float32))
    return t(c), t(sa), t(sb)


def _pool_weight(gi, t, s_pos, s):
    half = POOL_WINDOWS[gi] // 2

    def win(lo, hi):
        a = np.clip(lo, 0, s); b = np.clip(hi + 1, 0, s)
        inside = (s_pos >= a) & (s_pos < b)
        return inside / np.maximum(b - a, 1)

    w = 0.5 * (win(t - half, t + half - 1) + win(t - half + 1, t + half)) - (t == s_pos)
    return w * ((t >= 0) & (t < s) & (s_pos >= 0) & (s_pos < s))


def _pool_tables(s):
    nb = s // BLK
    fwd = np.zeros((3, POOL_G, BLK, 3 * BLK), np.float32)
    bwd = np.zeros((3, POOL_G, BLK, 3 * BLK), np.float32)
    for vi, n in enumerate((0, 1 if nb > 2 else 0, nb - 1)):
        i = n * BLK + np.arange(BLK)[:, None]
        j = (n - 1) * BLK + np.arange(3 * BLK)[None, :]
        for gi in range(POOL_G):
            fwd[vi, gi] = _pool_weight(gi, i, j, s)
            bwd[vi, gi] = _pool_weight(gi, j, i, s)
    return jnp.asarray(fwd, dtype=BF16), jnp.asarray(bwd, dtype=BF16)


def _variant_index(n, nb):
    return jnp.where(n == 0, 0, jnp.where(n == nb - 1, 2, 1))


class _Exchange:
    inputs = ()
    out_shapes = ()
    sems = ()

    def start(self, srcs, outs, sems):
        raise NotImplementedError

    def middle(self, srcs, outs, sems):
        pass

    def finish(self, srcs, outs, sems):
        raise NotImplementedError


class _AllGather(_Exchange):
    def __init__(self, arrays):
        n = len(arrays)
        self.inputs = list(arrays)
        self.out_shapes = [jax.ShapeDtypeStruct((N_DEV,) + a.shape, a.dtype) for a in arrays]
        self.sems = [pltpu.SemaphoreType.DMA((n, 8)), pltpu.SemaphoreType.DMA((n, 8)), pltpu.SemaphoreType.DMA((n,))]

    def _parts(self, srcs, outs, sems):
        send_sems, recv_sems, local_sems = sems
        n = len(srcs)
        x, y, c = lax.axis_index("x"), lax.axis_index("y"), lax.axis_index("c")
        me, sibling, xn, yn, diag = (x, y, c), (x, y, 1 - c), (1 - x, y, c), (x, 1 - y, c), (1 - x, 1 - y, c)

        def place(a, dev, half=None):
            block = outs[a].at[4 * dev[0] + 2 * dev[1] + dev[2]]
            if half is None:
                return block
            r2 = outs[a].shape[1] // 2
            return block.at[pl.ds(half * r2, r2)]

        def copy(a, k, dev, to, half=None, src=None):
            where = place(a, dev, half)
            return pltpu.make_async_remote_copy(
                src_ref=where if src is None else src, dst_ref=where, send_sem=send_sems.at[a, k],
                recv_sem=recv_sems.at[a, k], device_id=to, device_id_type=MESH)

        def other(dev):
            return (dev[0], dev[1], 1 - dev[2])

        class Parts:
            mine = staticmethod(lambda: [pltpu.make_async_copy(srcs[a], place(a, me), local_sems.at[a])
                                         for a in range(n)])
            own = staticmethod(lambda: [copy(a, k, me, to, src=srcs[a]) for a in range(n)
                                        for k, to in ((0, sibling), (1, xn), (2, yn))])
            relay = staticmethod(lambda a: [copy(a, 3, xn, yn, half=0), copy(a, 4, yn, xn, half=1),
                                            copy(a, 5, xn, sibling), copy(a, 6, yn, sibling)])
            last = staticmethod(lambda a: copy(a, 7, diag, sibling))
            from_x = staticmethod(lambda a: copy(a, 1, xn, me))
            from_y = staticmethod(lambda a: copy(a, 2, yn, me))
            diag_halves = staticmethod(lambda a: [copy(a, 3, diag, me, half=0), copy(a, 4, diag, me, half=1)])
            from_sibling = staticmethod(lambda a: [copy(a, 0, sibling, me), copy(a, 5, other(xn), me),
                                                   copy(a, 6, other(yn), me), copy(a, 7, other(diag), me)])

        return n, Parts

    def start(self, srcs, outs, sems):
        _, p = self._parts(srcs, outs, sems)
        for cp in p.mine() + p.own():
            cp.start()

    def middle(self, srcs, outs, sems):
        n, p = self._parts(srcs, outs, sems)
        for a in range(n):
            p.from_x(a).wait_recv()
            p.from_y(a).wait_recv()
            for cp in p.relay(a):
                cp.start()

    def finish(self, srcs, outs, sems):
        n, p = self._parts(srcs, outs, sems)
        for a in range(n):
            for cp in p.diag_halves(a):
                cp.wait_recv()
            p.last(a).start()
        for a in range(n):
            for cp in p.from_sibling(a):
                cp.wait_recv()
        for cp in p.own() + [cp for a in range(n) for cp in p.relay(a) + [p.last(a)]]:
            cp.wait_send()
        for cp in p.mine():
            cp.wait()


class _RsStage2(_Exchange):
    def start(self, srcs, outs, sems):
        for cp in self._copies(srcs, outs, sems):
            cp.start()

    def finish(self, srcs, outs, sems):
        copies = self._copies(srcs, outs, sems)
        for cp in copies:
            cp.wait_recv()
        for cp in copies:
            cp.wait_send()


    def __init__(self, pbs):
        n = len(pbs)
        self.inputs = list(pbs)
        self.out_shapes = [jax.ShapeDtypeStruct((3,) + p.shape[1:], p.dtype) for p in pbs]
        self.sems = [pltpu.SemaphoreType.DMA((n, 3)), pltpu.SemaphoreType.DMA((n, 3))]

    def _copies(self, srcs, outs, sems):
        send_sems, recv_sems = sems
        x, y, c = lax.axis_index("x"), lax.axis_index("y"), lax.axis_index("c")
        chips = [(1 - x, y), (x, 1 - y), (1 - x, 1 - y)]
        return [pltpu.make_async_remote_copy(
            src_ref=srcs[a].at[2 * chip[0] + chip[1]], dst_ref=outs[a].at[j], send_sem=send_sems.at[a, j],
            recv_sem=recv_sems.at[a, j], device_id=(*chip, c), device_id_type=MESH)
            for a in range(len(srcs)) for j, chip in enumerate(chips)]


class _DirectGather(_Exchange):
    def __init__(self, arrays):
        n = len(arrays)
        self.inputs = list(arrays)
        self.out_shapes = [jax.ShapeDtypeStruct((N_DEV,) + a.shape, a.dtype) for a in arrays]
        self.sems = [pltpu.SemaphoreType.DMA((n, 7)), pltpu.SemaphoreType.DMA((n, 7)), pltpu.SemaphoreType.DMA((n,))]

    def _copies(self, srcs, outs, sems):
        send_sems, recv_sems, local_sems = sems
        x, y, c = lax.axis_index("x"), lax.axis_index("y"), lax.axis_index("c")
        me = 4 * x + 2 * y + c
        remote, local = [], []
        for a in range(len(srcs)):
            local.append(pltpu.make_async_copy(srcs[a], outs[a].at[me], local_sems.at[a]))
            for k in range(1, N_DEV):
                peer = (x ^ (k >> 2), y ^ ((k >> 1) & 1), c ^ (k & 1))
                remote.append(pltpu.make_async_remote_copy(
                    src_ref=srcs[a], dst_ref=outs[a].at[me], send_sem=send_sems.at[a, k - 1],
                    recv_sem=recv_sems.at[a, k - 1], device_id=peer, device_id_type=MESH))
        return remote, local

    def start(self, srcs, outs, sems):
        remote, local = self._copies(srcs, outs, sems)
        for cp in local + remote:
            cp.start()

    def finish(self, srcs, outs, sems):
        remote, local = self._copies(srcs, outs, sems)
        for cp in remote:
            cp.wait_recv()
        for cp in remote:
            cp.wait_send()
        for cp in local:
            cp.wait()


class _Both(_Exchange):
    def __init__(self, a, b):
        self.a, self.b = a, b
        self.inputs = list(a.inputs) + list(b.inputs)
        self.out_shapes = list(a.out_shapes) + list(b.out_shapes)
        self.sems = list(a.sems) + list(b.sems)

    def _split(self, srcs, outs, sems):
        na, oa, sa = len(self.a.inputs), len(self.a.out_shapes), len(self.a.sems)
        return (srcs[:na], outs[:oa], sems[:sa]), (srcs[na:], outs[oa:], sems[sa:])

    def start(self, srcs, outs, sems):
        pa, pb = self._split(srcs, outs, sems)
        self.a.start(*pa)
        self.b.start(*pb)

    def middle(self, srcs, outs, sems):
        pa, pb = self._split(srcs, outs, sems)
        self.a.middle(*pa)
        self.b.middle(*pb)

    def finish(self, srcs, outs, sems):
        pa, pb = self._split(srcs, outs, sems)
        self.a.finish(*pa)
        self.b.finish(*pb)


_ANY = pl.BlockSpec(memory_space=pl.ANY)


def _run_exchange(ex, name):
    n_in, n_out = len(ex.inputs), len(ex.out_shapes)

    def body(*refs):
        srcs, outs, sems = refs[:n_in], refs[n_in:n_in + n_out], refs[n_in + n_out:]
        ex.start(srcs, outs, sems)
        ex.middle(srcs, outs, sems)
        ex.finish(srcs, outs, sems)

    return pl.pallas_call(
        body, name=name, out_shape=list(ex.out_shapes), in_specs=[_ANY] * n_in, out_specs=[_ANY] * n_out,
        scratch_shapes=list(ex.sems),
    )(*ex.inputs)


CARRY_MIDDLE = 0.7


def _call(body, name, grid, in_specs, out_specs, out_shape, args, sem, carry=None, middle=CARRY_MIDDLE, scratch=()):
    if carry is None:
        return pl.pallas_call(functools.partial(body), name=name, grid=grid, in_specs=in_specs, out_specs=out_specs,
                              out_shape=out_shape, scratch_shapes=list(scratch), compiler_params=_cparams(sem))(*args)
    n_in, n_out = len(in_specs), len(out_specs)
    nc_in, nc_out = len(carry.inputs), len(carry.out_shapes)

    def carried(*refs):
        ins = refs[:n_in]
        c_in = refs[n_in:n_in + nc_in]
        outs = refs[n_in + nc_in:n_in + nc_in + n_out]
        c_out = refs[n_in + nc_in + n_out:n_in + nc_in + n_out + nc_out]
        own = refs[n_in + nc_in + n_out + nc_out:n_in + nc_in + n_out + nc_out + len(scratch)]
        sems = refs[n_in + nc_in + n_out + nc_out + len(scratch):]
        ids = [pl.program_id(i) for i in range(len(grid))]
        is_first = functools.reduce(jnp.logical_and, [i == 0 for i in ids])
        is_last = functools.reduce(jnp.logical_and, [i == g - 1 for i, g in zip(ids, grid)])
        @pl.when(is_first)
        def _():
            carry.start(c_in, c_out, sems)

        if middle is not None:
            @pl.when(functools.reduce(jnp.logical_and, [ids[0] == round(middle * (grid[0] - 1))]
                                      + [i == 0 for i in ids[1:]]))
            def _():
                carry.middle(c_in, c_out, sems)

        body(*ins, *outs, *own)

        @pl.when(is_last)
        def _():
            if middle is None:
                carry.middle(c_in, c_out, sems)
            carry.finish(c_in, c_out, sems)

    return pl.pallas_call(
        carried, name=name, grid=grid, in_specs=list(in_specs) + [_ANY] * nc_in,
        out_specs=list(out_specs) + [_ANY] * nc_out, out_shape=list(out_shape) + list(carry.out_shapes),
        scratch_shapes=list(scratch) + list(carry.sems), compiler_params=_cparams(sem))(*args, *carry.inputs)


def _rs_sum2_adam(pos, p, r2, w, m, v, name):
    _, rows, d = p.shape
    tr = rows // 2 if rows % 16 == 0 else rows

    def body(pos_ref, p_ref, r_ref, w_ref, m_ref, v_ref, g_ref, d_ref, nm_ref, nv_ref):
        r = r_ref[...].astype(F32)
        g = ((p_ref[...] + r[0]) + r[1]) + r[2]
        g_ref[...] = g
        d_ref[...], nm_ref[...], nv_ref[...] = _adam_math(w_ref[...], g, m_ref[...], v_ref[...])

    blk = pl.BlockSpec((tr, d), lambda i, pos_ref: (i, 0))
    return pl.pallas_call(
        body, name=name,
        grid_spec=pltpu.PrefetchScalarGridSpec(
            num_scalar_prefetch=1, grid=(rows // tr,),
            in_specs=[pl.BlockSpec((None, tr, d), lambda i, pos_ref: (2 * pos_ref[0] + pos_ref[1], i, 0)),
                      pl.BlockSpec((3, tr, d), lambda i, pos_ref: (0, i, 0)), blk, blk, blk],
            out_specs=[blk] * 4),
        out_shape=[jax.ShapeDtypeStruct((rows, d), F32)] * 4,
        compiler_params=_cparams(("arbitrary",)),
    )(pos, p, r2, w, m, v)


def _ffn_chunk(f):
    for cand in (256, 128):
        if f % cand == 0:
            return cand
    return f


def _loss_head(x, gg, target, loss_ref, dg_ref):
    @pl.when(pl.program_id(0) == 0)
    def _():
        loss_ref[...] = jnp.zeros_like(loss_ref)
        dg_ref[...] = jnp.zeros_like(dg_ref)

    r = lax.rsqrt(jnp.mean(x * x, axis=-1, keepdims=True) + EPS)
    xhat = x * r
    e = xhat * gg - target
    loss_ref[...] += 0.5 * jnp.sum(jnp.mean(e * e, axis=-1, keepdims=True), axis=0, keepdims=True)
    dy = e * (1.0 / x.shape[-1])
    dg_ref[...] += jnp.sum(dy * xhat, axis=0, keepdims=True)
    dxhat = dy * gg
    return r * (dxhat - xhat * jnp.mean(dxhat * xhat, axis=-1, keepdims=True))


def _ffn_gate_up(h, g, wg_t, wu_t, name, tm, carry=None):
    s, d = h.shape
    f = wg_t.shape[0]
    tf = _ffn_chunk(f)

    def body(h_ref, g_ref, wg_ref, wu_ref, n_ref, gate_ref, up_ref, act_t_ref):
        x = h_ref[...]
        r = lax.rsqrt(jnp.mean(x * x, axis=-1, keepdims=True) + EPS)
        nb = (x * r * g_ref[...]).astype(BF16)
        n_ref[...] = nb
        for j in range(f // tf):
            sl = slice(j * tf, (j + 1) * tf)
            gate = _dot_nt(nb, wg_ref[sl, :])
            up = _dot_nt(nb, wu_ref[sl, :])
            gate_ref[:, sl] = gate.astype(BF16)
            up_ref[:, sl] = up.astype(BF16)
            act_t_ref[sl, :] = (gate * _sigmoid(gate) * up).astype(BF16).T

    row = lambda w: pl.BlockSpec((tm, w), lambda i: (i, 0))
    return _call(body, name, (s // tm,), [row(d), _whole((1, d)), _whole((f, d)), _whole((f, d))],
                 [row(d), row(f), row(f), pl.BlockSpec((f, tm), lambda i: (0, i))],
                 [jax.ShapeDtypeStruct((s, d), BF16)] + [jax.ShapeDtypeStruct((s, f), BF16)] * 2
                 + [jax.ShapeDtypeStruct((f, s), BF16)], (h, g, wg_t, wu_t), ("arbitrary",), carry)


def _ffn_fwd(h, g, wg_t, wu_t, wd, name, tm, carry=None, head=None, middle=CARRY_MIDDLE):
    s, d = h.shape
    f = wg_t.shape[0]
    tf = _ffn_chunk(f)

    def body(h_ref, g_ref, wg_ref, wu_ref, wd_ref, *refs):
        if head is None:
            o_ref, n_ref, gate_ref, up_ref, act_t_ref, act_ref = refs
        else:
            gf_ref, t_ref, o_ref, n_ref, gate_ref, up_ref, act_t_ref, loss_ref, dgf_ref, act_ref = refs
        x = h_ref[...]
        r = lax.rsqrt(jnp.mean(x * x, axis=-1, keepdims=True) + EPS)
        nb = (x * r * g_ref[...]).astype(BF16)
        n_ref[...] = nb
        for j in range(f // tf):
            sl = slice(j * tf, (j + 1) * tf)
            gate = _dot_nt(nb, wg_ref[sl, :])
            up = _dot_nt(nb, wu_ref[sl, :])
            gate_ref[:, sl] = gate.astype(BF16)
            up_ref[:, sl] = up.astype(BF16)
            act = gate * _sigmoid(gate) * up
            act = act.astype(BF16)
            act_ref[:, sl] = act
            act_t_ref[sl, :] = act.T
        h_out = x + 0.5 * _dot_nn(act_ref[...], wd_ref[...])
        o_ref[...] = h_out if head is None else _loss_head(h_out, gf_ref[...], t_ref[...], loss_ref, dgf_ref)

    row = lambda w: pl.BlockSpec((tm, w), lambda i: (i, 0))
    in_specs = [row(d), _whole((1, d)), _whole((f, d)), _whole((f, d)), _whole((f, d))]
    out_specs = [row(d), row(d), row(f), row(f), pl.BlockSpec((f, tm), lambda i: (0, i))]
    out_shape = ([jax.ShapeDtypeStruct((s, d), F32), jax.ShapeDtypeStruct((s, d), BF16)]
                 + [jax.ShapeDtypeStruct((s, f), BF16)] * 2 + [jax.ShapeDtypeStruct((f, s), BF16)])
    args = (h, g, wg_t, wu_t, wd)
    scratch = (pltpu.VMEM((tm, f), BF16),)
    if head is not None:
        in_specs += [_whole((1, d)), row(d)]
        out_specs += [pl.BlockSpec((1, 1), lambda i: (0, 0)), pl.BlockSpec((1, d), lambda i: (0, 0))]
        out_shape += [jax.ShapeDtypeStruct((1, 1), F32), jax.ShapeDtypeStruct((1, d), F32)]
        args += tuple(head)
    return _call(body, name, (s // tm,), in_specs, out_specs, out_shape, args, ("arbitrary",), carry, middle, scratch)


def _gate_grads(dh_ref, gate_ref, up_ref, wd_ref, dgate_ref, dup_ref, dgate_t_ref, dup_t_ref, dhh_ref, tf):
    dhh = (0.5 * dh_ref[...]).astype(BF16)
    dhh_ref[...] = dhh
    for j in range(gate_ref.shape[1] // tf):
        sl = slice(j * tf, (j + 1) * tf)
        gt = gate_ref[:, sl].astype(F32)
        u = up_ref[:, sl].astype(F32)
        dact = _dot_nt(dhh, wd_ref[sl, :])
        sg = _sigmoid(gt)
        dup = dact * (gt * sg)
        dgate = dact * u * (sg * (1.0 + gt * (1.0 - sg)))
        dup, dgate = dup.astype(BF16), dgate.astype(BF16)
        if dup_ref is not None:
            dup_ref[:, sl] = dup
            dgate_ref[:, sl] = dgate
        dup_t_ref[sl, :] = dup.T
        dgate_t_ref[sl, :] = dgate.T


def _input_grad(h_ref, dh_ref, dgate_ref, dup_ref, g_ref, wg_ref, wu_ref, o_ref, dg_ref, transposed=False):
    x = h_ref[...]
    r = lax.rsqrt(jnp.mean(x * x, axis=-1, keepdims=True) + EPS)
    xhat = x * r
    dot = _dot_tn if transposed else _dot_nn
    dn = dot(dgate_ref[...], wg_ref[...]) + dot(dup_ref[...], wu_ref[...])
    dxhat = dn * g_ref[...]
    o_ref[...] = dh_ref[...] + r * (dxhat - xhat * jnp.mean(dxhat * xhat, axis=-1, keepdims=True))

    @pl.when(pl.program_id(0) == 0)
    def _():
        dg_ref[...] = jnp.zeros_like(dg_ref)

    dg_ref[...] += jnp.sum(dn * xhat, axis=0, keepdims=True)


def _ffn_bwd(h_in, dh_out, gate, up, g, wg_t, wu_t, wd, w_out, name, tm):
    s, d = h_in.shape
    f = gate.shape[1]
    w = w_out.shape[0]
    tf = _ffn_chunk(f)

    def body(h_ref, dh_ref, gate_ref, up_ref, g_ref, wg_ref, wu_ref, wd_ref, wo_ref,
             o_ref, dg_ref, dgate_t_ref, dup_t_ref, dhh_ref, dmix_ref, dhb_ref, dgate_ref, dup_ref):
        _gate_grads(dh_ref, gate_ref, up_ref, wd_ref, dgate_ref, dup_ref, dgate_t_ref, dup_t_ref, dhh_ref, tf)
        _input_grad(h_ref, dh_ref, dgate_ref, dup_ref, g_ref, wg_ref, wu_ref, o_ref, dg_ref)
        dhb = o_ref[...].astype(BF16)
        dhb_ref[...] = dhb
        dmix_ref[...] = _dot_nt(dhb, wo_ref[...])

    row = lambda c: pl.BlockSpec((tm, c), lambda i: (i, 0))
    col = pl.BlockSpec((f, tm), lambda i: (0, i))
    return pl.pallas_call(
        body, name=name, grid=(s // tm,),
        in_specs=[row(d), row(d), row(f), row(f), _whole((1, d)), _whole((f, d)), _whole((f, d)), _whole((f, d)),
                  _whole((w, d))],
        out_specs=[row(d), pl.BlockSpec((1, d), lambda i: (0, 0)), col, col, row(d), row(w), row(d)],
        out_shape=[jax.ShapeDtypeStruct((s, d), F32), jax.ShapeDtypeStruct((1, d), F32),
                   jax.ShapeDtypeStruct((f, s), BF16), jax.ShapeDtypeStruct((f, s), BF16),
                   jax.ShapeDtypeStruct((s, d), BF16), jax.ShapeDtypeStruct((s, w), F32),
                   jax.ShapeDtypeStruct((s, d), BF16)],
        scratch_shapes=[pltpu.VMEM((tm, f), BF16), pltpu.VMEM((tm, f), BF16)],
        compiler_params=_cparams(("arbitrary",)),
    )(h_in, dh_out, gate, up, g, wg_t, wu_t, wd, w_out)


def _ffn_bwd_gates(dh_out, gate, up, wd, name, tm, carry=None):
    s, d = dh_out.shape
    f = gate.shape[1]
    tf = _ffn_chunk(f)

    def body(dh_ref, gate_ref, up_ref, wd_ref, dgate_t_ref, dup_t_ref, dhh_ref):
        _gate_grads(dh_ref, gate_ref, up_ref, wd_ref, None, None, dgate_t_ref, dup_t_ref, dhh_ref, tf)

    row = lambda w: pl.BlockSpec((tm, w), lambda i: (i, 0))
    col = pl.BlockSpec((f, tm), lambda i: (0, i))
    return _call(
        body, name, (s // tm,), [row(d), row(f), row(f), _whole((f, d))], [col, col, row(d)],
        [jax.ShapeDtypeStruct((f, s), BF16)] * 2 + [jax.ShapeDtypeStruct((s, d), BF16)],
        (dh_out, gate, up, wd), ("arbitrary",), carry)


def _ffn_bwd_input(h_in, dh_out, dgate_t, dup_t, g, wg_t, wu_t, name, tm, carry=None):
    s, d = h_in.shape
    f = dgate_t.shape[0]

    row = lambda w: pl.BlockSpec((tm, w), lambda i: (i, 0))
    col = pl.BlockSpec((f, tm), lambda i: (0, i))
    return _call(
        functools.partial(_input_grad, transposed=True), name, (s // tm,),
        [row(d), row(d), col, col, _whole((1, d)), _whole((f, d)), _whole((f, d))],
        [row(d), pl.BlockSpec((1, d), lambda i: (0, 0))],
        [jax.ShapeDtypeStruct((s, d), F32), jax.ShapeDtypeStruct((1, d), F32)],
        (h_in, dh_out, dgate_t, dup_t, g, wg_t, wu_t), ("arbitrary",), carry)


def _wgrad_rs1(pos, a_ts, b, name, carry=None):
    n_a = len(a_ts)
    f, s = a_ts[0].shape
    d = b.shape[1]
    fk = f // N_DEV
    nc_in = 0 if carry is None else len(carry.inputs)
    nc_out = 0 if carry is None else len(carry.out_shapes)

    def body(pos_ref, *refs):
        a_refs, b_ref, refs = refs[:n_a], refs[n_a], refs[n_a + 1:]
        c_in = refs[:nc_in]
        outs = refs[nc_in:nc_in + 2 * n_a]
        c_out = refs[nc_in + 2 * n_a:nc_in + 2 * n_a + nc_out]
        stage, land, send_sems, recv_sems = refs[nc_in + 2 * n_a + nc_out:nc_in + 2 * n_a + nc_out + 4]
        c_sems = refs[nc_in + 2 * n_a + nc_out + 4:]
        t = pl.program_id(0)
        which, step = t // 8, t % 8
        slot = 4 * which + step % 4
        x, y, c = lax.axis_index("x"), lax.axis_index("y"), lax.axis_index("c")

        def push(k):
            return pltpu.make_async_remote_copy(src_ref=stage.at[k], dst_ref=land.at[k], send_sem=send_sems.at[k],
                                                recv_sem=recv_sems.at[k], device_id=(x, y, 1 - c), device_id_type=MESH)

        if carry is not None:
            @pl.when(t == 0)
            def _():
                carry.start(c_in, c_out, c_sems)

        for j in range(n_a):
            @pl.when(which == j)
            def _(j=j):
                g = _dot_nn(a_refs[j][...], b_ref[...])

                @pl.when(step < 4)
                def _():
                    stage[slot] = g.astype(BF16)
                    push(slot).start()

                @pl.when(step >= 4)
                def _():
                    push(slot).wait_recv()
                    p = g + land[slot].astype(F32)
                    outs[2 * j][...] = p
                    outs[2 * j + 1][...] = p.astype(BF16)

        if carry is not None:
            @pl.when(t == round(CARRY_MIDDLE * (8 * n_a - 1)))
            def _():
                carry.middle(c_in, c_out, c_sems)

        @pl.when(t == 8 * n_a - 1)
        def _():
            for k in range(4 * n_a):
                push(k).wait_send()
            if carry is not None:
                carry.finish(c_in, c_out, c_sems)

    def shard(j):
        def index_map(t, pos_ref):
            step = jnp.clip(t - 8 * j, 0, 7)
            return 4 * ((step % 4) // 2) + 2 * (step % 2) + jnp.where(step < 4, 1 - pos_ref[2], pos_ref[2]), 0
        return index_map

    out = lambda j: pl.BlockSpec((None, fk, d), lambda t, pos_ref, j=j: (jnp.clip(t - 8 * j - 4, 0, 3), 0, 0))
    return pl.pallas_call(
        body, name=name,
        grid_spec=pltpu.PrefetchScalarGridSpec(
            num_scalar_prefetch=1, grid=(8 * n_a,),
            in_specs=[pl.BlockSpec((fk, s), shard(j)) for j in range(n_a)]
            + [pl.BlockSpec((s, d), lambda t, pos_ref: (0, 0), pipeline_mode=pl.Buffered(1))] + [_ANY] * nc_in,
            out_specs=[out(j) for j in range(n_a) for _ in range(2)] + [_ANY] * nc_out,
            scratch_shapes=[pltpu.VMEM((4 * n_a, fk, d), BF16), pltpu.VMEM((4 * n_a, fk, d), BF16),
                            pltpu.SemaphoreType.DMA((4 * n_a,)), pltpu.SemaphoreType.DMA((4 * n_a,))]
            + ([] if carry is None else list(carry.sems))),
        out_shape=[jax.ShapeDtypeStruct((4, fk, d), dt) for _ in range(n_a) for dt in (F32, BF16)]
        + ([] if carry is None else list(carry.out_shapes)),
        compiler_params=_cparams(("arbitrary",)),
    )(pos, *a_ts, b, *([] if carry is None else carry.inputs))


def _rope(t, c, sa, sb, reps):
    c, sa, sb = (jnp.tile(v, (1, reps)) if reps > 1 else v for v in (c, sa, sb))
    w = t.shape[1]
    return t * c + pltpu.roll(t, w - 8, 1) * sa + pltpu.roll(t, 8, 1) * sb


def _rope_bwd(dt, c, sa, sb, reps):
    c, sa, sb = (jnp.tile(v, (1, reps)) if reps > 1 else v for v in (c, sa, sb))
    w = dt.shape[1]
    return dt * c + pltpu.roll(dt * sa, 8, 1) + pltpu.roll(dt * sb, w - 8, 1)


def _ffn_down_mix_in(x, act_t, wd, g, win_t, tabs, name, tm, carry=None):
    s, d = x.shape
    f = wd.shape[0]
    n_in = win_t.shape[0]

    def body(x_ref, act_ref, wd_ref, g_ref, w_ref, c_ref, sa_ref, sb_ref, h_ref, q_ref, k_ref, v_ref, pc_ref, n_ref):
        x = x_ref[...] + 0.5 * _dot_tn(act_ref[...], wd_ref[...])
        h_ref[...] = x
        r = lax.rsqrt(jnp.mean(x * x, axis=-1, keepdims=True) + EPS)
        nb = (x * r * g_ref[...]).astype(BF16)
        n_ref[...] = nb
        u = _dot_nt(nb, w_ref[...])
        c, sa, sb = c_ref[...], sa_ref[...], sb_ref[...]
        q_ref[...] = _rope(u[:, :ATTN_W], c, sa, sb, ATTN_W // 128).astype(BF16)
        k_ref[...] = _rope(u[:, ATTN_W:ATTN_W + KV_W], c, sa, sb, 1).astype(BF16)
        v_ref[...] = u[:, ATTN_W + KV_W:ATTN_W + 2 * KV_W].astype(BF16)
        pc_ref[...] = u[:, ATTN_W + 2 * KV_W:]

    row = lambda w: pl.BlockSpec((tm, w), lambda i: (i, 0))
    return _call(
        body, name, (s // tm,),
        [row(d), pl.BlockSpec((f, tm), lambda i: (0, i)), _whole((f, d)), _whole((1, d)), _whole((n_in, d)),
         row(128), row(128), row(128)],
        [row(d), row(ATTN_W), row(KV_W), row(KV_W), row(POOL_W), row(d)],
        [jax.ShapeDtypeStruct((s, d), F32), jax.ShapeDtypeStruct((s, ATTN_W), BF16),
         jax.ShapeDtypeStruct((s, KV_W), BF16), jax.ShapeDtypeStruct((s, KV_W), BF16),
         jax.ShapeDtypeStruct((s, POOL_W), F32), jax.ShapeDtypeStruct((s, d), BF16)],
        (x, act_t, wd, g, win_t, *tabs), ("arbitrary",), carry)


def _band_mask(n, nb, transposed):
    shape = (3 * BLK, 2 * BLK) if transposed else (2 * BLK, 3 * BLK)
    i = lax.broadcasted_iota(jnp.int32, shape, 1 if transposed else 0) % BLK
    j = lax.broadcasted_iota(jnp.int32, shape, 0 if transposed else 1)
    kpos = (n - 1) * BLK + j
    return (j >= i) & (j <= i + 2 * BLK) & (kpos >= 0) & (kpos < nb * BLK)


def _block_diag(t, kh):
    tf = t.astype(F32)
    tr = pltpu.roll(tf, HEAD_DIM, 1)
    lo = lax.broadcasted_iota(jnp.int32, tf.shape, 1) < HEAD_DIM
    top, bot = (tf, tr) if kh == 0 else (tr, tf)
    return jnp.concatenate([jnp.where(lo, top, 0.0), jnp.where(lo, 0.0, bot)], axis=0).astype(BF16)


def _fold_diag(tbd):
    lo = lax.broadcasted_iota(jnp.int32, (3 * BLK, 2 * HEAD_DIM), 1) < HEAD_DIM
    t = jnp.where(lo, tbd[:3 * BLK], tbd[3 * BLK:])
    return t + pltpu.roll(t, HEAD_DIM, 1)


def _stack_pairs(x, kh):
    return jnp.concatenate([x[:, (2 * kh) * 128:(2 * kh + 1) * 128], x[:, (2 * kh + 1) * 128:(2 * kh + 2) * 128]], axis=0)


def _sink_of(sink_ref, kh, half, axis):
    shape = (2 * BLK, 1) if axis == 0 else (1, 2 * BLK)
    first = lax.broadcasted_iota(jnp.int32, shape, axis) < BLK
    return jnp.where(first, sink_ref[0, GROUP * kh + half], sink_ref[0, GROUP * kh + 2 + half])


def _softmax_sink(sc, valid, sink, axis):
    sc = jnp.where(valid, sc, -1e30)
    m = jnp.maximum(jnp.max(sc, axis=axis, keepdims=True), sink)
    e = jnp.exp(sc - m)
    es = jnp.exp(sink - m)
    inv = 1.0 / (jnp.sum(e, axis=axis, keepdims=True) + es)
    return e * inv, es * inv


def _attn_blocks_per_step(nb):
    return next(nq for nq in (4, 2, 1) if nb % nq == 0)


def _band_specs(nq, nb, w, col=0):
    return [pl.BlockSpec((BLK, w), lambda m: (jnp.maximum(nq * m - 1, 0), col)),
            pl.BlockSpec((nq * BLK, w), lambda m: (m, col)),
            pl.BlockSpec((BLK, w), lambda m: (jnp.minimum(nq * m + nq, nb - 1), col))]


def _attn_pool_fwd(h, w_out, q, k, v, pc, sink, pool_w, pool_scale, pband, name, carry=None, middle=CARRY_MIDDLE):
    s, d = h.shape
    nb = s // BLK
    nq = _attn_blocks_per_step(nb)

    def body(sink_ref, q_ref, k0, k1, k2, v0, v1, v2, p0, p1, p2, pw_ref, ps_ref, pb_ref, h_ref, wo_ref,
             h_out_ref, o_t_ref, o_ref):
        kall = jnp.concatenate([k0[...], k1[...], k2[...]], axis=0)
        vall = jnp.concatenate([v0[...], v1[...], v2[...]], axis=0)
        pall = jnp.concatenate([p0[...], p1[...], p2[...]], axis=0).astype(BF16)
        qall = q_ref[...] * SCORE_SCALE
        for j in range(nq):
            n = pl.program_id(0) * nq + j
            rows, band = slice(j * BLK, (j + 1) * BLK), slice(j * BLK, (j + 3) * BLK)
            valid = _band_mask(n, nb, False)
            kb, vb, qs = kall[band], vall[band], qall[rows]
            for kh in range(N_KV):
                sc = _dot_nt(_stack_pairs(qs, kh), _block_diag(kb, kh))
                p = [_softmax_sink(sc[:, half * 3 * BLK:(half + 1) * 3 * BLK], valid,
                                   _sink_of(sink_ref, kh, half, 0), 1)[0] for half in range(2)]
                o2 = _dot_nn(jnp.concatenate(p, axis=1).astype(BF16), _block_diag(vb, kh)).astype(BF16)
                o_ref[rows, (2 * kh) * 128:(2 * kh + 1) * 128] = o2[:BLK]
                o_ref[rows, (2 * kh + 1) * 128:(2 * kh + 2) * 128] = o2[BLK:]
            ext = pall[band]
            var = _variant_index(n, nb)
            for gi in range(POOL_G):
                gsl = slice(gi * POOL_GW, (gi + 1) * POOL_GW)
                dg = _dot_nn(pb_ref[var, gi], ext[:, gsl])
                yg = _dot_nn(dg.astype(BF16), pw_ref[gi].astype(BF16))
                o_ref[rows, ATTN_W + gi * POOL_GW:ATTN_W + (gi + 1) * POOL_GW] = (yg * ps_ref[:, gsl]).astype(BF16)
        mix = o_ref[...]
        o_t_ref[...] = mix.T
        h_out_ref[...] = h_ref[...] + _dot_nn(mix, wo_ref[...])

    mix_w = ATTN_W + POOL_W
    return _call(
        body, name, (nb // nq,),
        [pl.BlockSpec(memory_space=pltpu.SMEM), pl.BlockSpec((nq * BLK, ATTN_W), lambda m: (m, 0)),
         *_band_specs(nq, nb, KV_W), *_band_specs(nq, nb, KV_W), *_band_specs(nq, nb, POOL_W),
         _whole((POOL_G, POOL_GW, POOL_GW)), _whole((1, POOL_W)), _whole(pband.shape),
         pl.BlockSpec((nq * BLK, d), lambda m: (m, 0)), _whole((mix_w, d))],
        [pl.BlockSpec((nq * BLK, d), lambda m: (m, 0)), pl.BlockSpec((mix_w, nq * BLK), lambda m: (0, m))],
        [jax.ShapeDtypeStruct((s, d), F32), jax.ShapeDtypeStruct((mix_w, s), BF16)],
        (sink, q, k, k, k, v, v, v, pc, pc, pc, pool_w, pool_scale, pband, h, w_out), ("arbitrary",), carry, middle,
        scratch=(pltpu.VMEM((nq * BLK, mix_w), BF16),))


def _attn_pool_bwd(q, k, v, pc, dmix, sink, pool_w, pool_scale, pband, ptband, name, carry=None):
    s = q.shape[0]
    nb = s // BLK
    nq = _attn_blocks_per_step(nb)

    def body(sink_ref, q_ref, k0, k1, k2, v0, v1, v2, p0, p1, p2, da_ref, d0, d1, d2, pw_ref, ps_ref, pb_ref, ptb_ref,
             dq_ref, dk_ref, dv_ref, dpc_ref, dsink_ref, dpw_ref, dps_ref):
        @pl.when(pl.program_id(0) == 0)
        def _():
            dsink_ref[...] = jnp.zeros_like(dsink_ref)
            dpw_ref[...] = jnp.zeros_like(dpw_ref)
            dps_ref[...] = jnp.zeros_like(dps_ref)

        kall = jnp.concatenate([k0[...], k1[...], k2[...]], axis=0)
        vall = jnp.concatenate([v0[...], v1[...], v2[...]], axis=0)
        pall = jnp.concatenate([p0[...], p1[...], p2[...]], axis=0).astype(BF16)
        dpall = jnp.concatenate([d0[...], d1[...], d2[...]], axis=0)
        lo = lax.broadcasted_iota(jnp.int32, (3 * BLK, KV_W), 1) < HEAD_DIM
        for j in range(nq):
            n = pl.program_id(0) * nq + j
            rows, band = slice(j * BLK, (j + 1) * BLK), slice(j * BLK, (j + 3) * BLK)
            valid = _band_mask(n, nb, True)
            kb, vb, qb = kall[band], vall[band], q_ref[rows, :]
            qs = qb * SCORE_SCALE
            da = da_ref[rows, :].astype(BF16)
            dk_fold, dv_fold = [], []
            for kh in range(N_KV):
                kbd, vbd = _block_diag(kb, kh), _block_diag(vb, kh)
                q2, do2 = _stack_pairs(qb, kh), _stack_pairs(da, kh)
                sc_t = _dot_nt(kbd, _stack_pairs(qs, kh))
                dp_t = _dot_nt(vbd, do2)
                p_t, ds_t = [], []
                for half in range(2):
                    keys = slice(half * 3 * BLK, (half + 1) * 3 * BLK)
                    p, ps = _softmax_sink(sc_t[keys], valid, _sink_of(sink_ref, kh, half, 1), 0)
                    delta = jnp.sum(p * dp_t[keys], axis=0, keepdims=True)
                    p_t.append(p.astype(BF16))
                    ds_t.append((p * (dp_t[keys] - delta)).astype(BF16))
                    dsk = -ps * delta
                    for pair in range(2):
                        h = GROUP * kh + 2 * pair + half
                        part = jnp.sum(dsk[:, pair * BLK:(pair + 1) * BLK], axis=1, keepdims=True)
                        dsink_ref[h:h + 1, :] += jnp.broadcast_to(part, (1, 128))
                p_t = jnp.concatenate(p_t, axis=0)
                ds_t = jnp.concatenate(ds_t, axis=0)
                dq2 = _dot_tn(ds_t, kbd) * SCORE_SCALE
                dq_ref[rows, (2 * kh) * 128:(2 * kh + 1) * 128] = dq2[:BLK]
                dq_ref[rows, (2 * kh + 1) * 128:(2 * kh + 2) * 128] = dq2[BLK:]
                dk_fold.append(_fold_diag(_dot_nn(ds_t, q2)) * SCORE_SCALE)
                dv_fold.append(_fold_diag(_dot_nn(p_t, do2)))
            dk_all = jnp.where(lo, dk_fold[0], dk_fold[1])
            dv_all = jnp.where(lo, dv_fold[0], dv_fold[1])
            for t in range(3):
                dk_ref[j, t] = dk_all[t * BLK:(t + 1) * BLK]
                dv_ref[j, t] = dv_all[t * BLK:(t + 1) * BLK]
            ext, dpe = pall[band], dpall[band]
            dpc_cur = dpall[(j + 1) * BLK:(j + 2) * BLK]
            var = _variant_index(n, nb)
            for gi in range(POOL_G):
                gsl = slice(gi * POOL_GW, (gi + 1) * POOL_GW)
                wg = pw_ref[gi].astype(BF16)
                sc = ps_ref[:, gsl]
                dgb = _dot_nn(pb_ref[var, gi], ext[:, gsl]).astype(BF16)
                yg = _dot_nn(dgb, wg)
                dps_ref[:, gsl] += jnp.sum(dpc_cur[:, gsl] * yg, axis=0, keepdims=True)
                dpw_ref[gi] += _dot_tn(dgb, (dpc_cur[:, gsl] * sc).astype(BF16))
                dd = _dot_nt((dpe[:, gsl] * sc).astype(BF16), wg)
                dpc_ref[rows, gsl] = _dot_nn(ptb_ref[var, gi], dd.astype(BF16))

    fixed = lambda shape: pl.BlockSpec(shape, lambda m: (0,) * len(shape))
    return _call(
        body, name, (nb // nq,),
        [pl.BlockSpec(memory_space=pltpu.SMEM), pl.BlockSpec((nq * BLK, ATTN_W), lambda m: (m, 0)),
         *_band_specs(nq, nb, KV_W), *_band_specs(nq, nb, KV_W), *_band_specs(nq, nb, POOL_W),
         pl.BlockSpec((nq * BLK, ATTN_W), lambda m: (m, 0)), *_band_specs(nq, nb, POOL_W, 1),
         _whole((POOL_G, POOL_GW, POOL_GW)), _whole((1, POOL_W)), _whole(pband.shape), _whole(ptband.shape)],
        [pl.BlockSpec((nq * BLK, ATTN_W), lambda m: (m, 0)),
         pl.BlockSpec((nq, 3, BLK, KV_W), lambda m: (m, 0, 0, 0)),
         pl.BlockSpec((nq, 3, BLK, KV_W), lambda m: (m, 0, 0, 0)),
         pl.BlockSpec((nq * BLK, POOL_W), lambda m: (m, 0)),
         fixed((N_HEADS, 128)), fixed((POOL_G, POOL_GW, POOL_GW)), fixed((1, POOL_W))],
        [jax.ShapeDtypeStruct((s, ATTN_W), F32), jax.ShapeDtypeStruct((nb, 3, BLK, KV_W), F32),
         jax.ShapeDtypeStruct((nb, 3, BLK, KV_W), F32), jax.ShapeDtypeStruct((s, POOL_W), F32),
         jax.ShapeDtypeStruct((N_HEADS, 128), F32),
         jax.ShapeDtypeStruct((POOL_G, POOL_GW, POOL_GW), F32), jax.ShapeDtypeStruct((1, POOL_W), F32)],
        (sink, q, k, k, k, v, v, v, pc, pc, pc, dmix, dmix, dmix, dmix, pool_w, pool_scale, pband, ptband),
        ("arbitrary",), carry)


def _mix_in_bwd(h, dh, g, win_t, dq, dkp, dvp, dpc, tabs, name, tm, carry=None):
    s, d = h.shape
    nb = s // BLK
    nt = tm // BLK
    n_in = win_t.shape[0]

    def band_sum(n, before, own, after, prev_last, next_first):
        lo = (n > 0).astype(F32)
        hi = (n < s // tm - 1).astype(F32)
        blocks = []
        for b in range(nt):
            from_prev = prev_last[...] * lo if b == 0 else before[b - 1]
            from_next = next_first[...] * hi if b == nt - 1 else after[b + 1]
            blocks.append(from_prev + own[b] + from_next)
        return jnp.concatenate(blocks, axis=0)

    def body(h_ref, dh_ref, g_ref, w_ref, dq_ref, k2, k1, k0, kp, kn, v2, v1, v0, vp, vn, dpc_ref, c_ref, sa_ref,
             sb_ref, o_ref, du_ref, dg_ref):
        n = pl.program_id(0)
        dk = band_sum(n, k2, k1, k0, kp, kn)
        dv = band_sum(n, v2, v1, v0, vp, vn)
        c, sa, sb = c_ref[...], sa_ref[...], sb_ref[...]
        du = jnp.concatenate([_rope_bwd(dq_ref[...], c, sa, sb, ATTN_W // 128), _rope_bwd(dk, c, sa, sb, 1), dv,
                              dpc_ref[...]], axis=1)
        du_ref[...] = du.T.astype(BF16)
        dn = _dot_nn(du.astype(BF16), w_ref[...])
        x = h_ref[...]
        r = lax.rsqrt(jnp.mean(x * x, axis=-1, keepdims=True) + EPS)
        xhat = x * r
        dxhat = dn * g_ref[...]
        o_ref[...] = dh_ref[...] + r * (dxhat - xhat * jnp.mean(dxhat * xhat, axis=-1, keepdims=True))

        @pl.when(n == 0)
        def _():
            dg_ref[...] = jnp.zeros_like(dg_ref)

        dg_ref[...] += jnp.sum(dn * xhat, axis=0, keepdims=True)

    row = lambda w: pl.BlockSpec((tm, w), lambda n: (n, 0))
    slot = lambda t: pl.BlockSpec((nt, None, BLK, KV_W), lambda n, t=t: (n, t, 0, 0))
    parts = [slot(2), slot(1), slot(0),
             pl.BlockSpec((None, None, BLK, KV_W), lambda n: (jnp.maximum(nt * n - 1, 0), 2, 0, 0)),
             pl.BlockSpec((None, None, BLK, KV_W), lambda n: (jnp.minimum(nt * n + nt, nb - 1), 0, 0, 0))]
    return _call(
        body, name, (s // tm,),
        [row(d), row(d), _whole((1, d)), _whole((n_in, d)), row(ATTN_W), *parts, *parts, row(POOL_W),
         row(128), row(128), row(128)],
        [row(d), pl.BlockSpec((n_in, tm), lambda n: (0, n)), pl.BlockSpec((1, d), lambda n: (0, 0))],
        [jax.ShapeDtypeStruct((s, d), F32), jax.ShapeDtypeStruct((n_in, s), BF16), jax.ShapeDtypeStruct((1, d), F32)],
        (h, dh, g, win_t, dq, *[dkp] * 5, *[dvp] * 5, dpc, *tabs), ("arbitrary",), carry)


def _adam_math(w, g, m, v):
    m = ADAM_B1 * m + (1.0 - ADAM_B1) * g
    v = ADAM_B2 * v + (1.0 - ADAM_B2) * (g * g)
    m_hat = m / (1.0 - ADAM_B1 ** ADAM_STEP)
    v_hat = v / (1.0 - ADAM_B2 ** ADAM_STEP)
    delta = -ADAM_LR * (m_hat / (jnp.sqrt(v_hat) + ADAM_EPS) + ADAM_WD * w)
    return delta, m, v


def _adam_small(w, parts, late, m, v, name):
    rows, cols = w.shape

    def body(w_ref, p_ref, l_ref, m_ref, v_ref, g_ref, d_ref, nm_ref, nv_ref):
        g, gl = p_ref[0], l_ref[0]
        for k in range(1, N_DEV):
            g = g + p_ref[k]
            gl = gl + l_ref[k]
        g_ref[...] = g
        g_ref[SMALL_NORM1:SMALL_NORM1 + 8, :] = g[SMALL_NORM1:SMALL_NORM1 + 8] + gl
        d_ref[...], nm_ref[...], nv_ref[...] = _adam_math(w_ref[...], g_ref[...], m_ref[...], v_ref[...])

    return pl.pallas_call(
        body, name=name, out_shape=[jax.ShapeDtypeStruct((rows, cols), F32)] * 4,
    )(w, parts, late, m, v)


SMALL_NORM1 = 512


def _pack_small(norm1, normm, norm2, normf, sink, pool_w, pool_scale, loss=None):
    scale_rows = jnp.pad(pool_scale.reshape(4, 128), ((0, 4), (0, 0)))
    last_rows = jnp.pad(sink.reshape(1, N_HEADS), ((0, 7), (0, 128 - N_HEADS)))
    if loss is not None:
        last_rows = last_rows + jnp.pad(loss.reshape(1, 1), ((1, 6), (0, 127)))
    return jnp.concatenate([pool_w.reshape(512, 128), norm1.reshape(8, 128), normm.reshape(8, 128),
                            norm2.reshape(8, 128), normf.reshape(8, 128), scale_rows, last_rows], axis=0)


def _unpack_small(p):
    return dict(pool_w=p[:512].reshape(1, POOL_G, POOL_GW, POOL_GW), ffn1_norm=p[512:520].reshape(1, 1024),
                mix_norm=p[520:528].reshape(1, 1024), ffn2_norm=p[528:536].reshape(1, 1024),
                final_norm=p[536:544].reshape(1024), pool_scale=p[544:548].reshape(1, POOL_W),
                sink_logits=p[552, :N_HEADS].reshape(1, N_HEADS), loss=p[553, 0])


def kernel(x, ffn1_norm, ffn1_w_gate, ffn1_w_up, ffn1_w_down, mix_norm, w_in, sink_logits, pool_w, pool_scale, w_out, ffn2_norm, ffn2_w_gate, ffn2_w_up, ffn2_w_down, final_norm, loss_target, m_ffn1_norm, m_ffn1_w_gate, m_ffn1_w_up, m_ffn1_w_down, m_mix_norm, m_w_in, m_sink_logits, m_pool_w, m_pool_scale, m_w_out, m_ffn2_norm, m_ffn2_w_gate, m_ffn2_w_up, m_ffn2_w_down, m_final_norm, v_ffn1_norm, v_ffn1_w_gate, v_ffn1_w_up, v_ffn1_w_down, v_mix_norm, v_w_in, v_sink_logits, v_pool_w, v_pool_scale, v_w_out, v_ffn2_norm, v_ffn2_w_gate, v_ffn2_w_up, v_ffn2_w_down, v_final_norm):
    s, d = x.shape[1], x.shape[2]
    tm = min(512, s)
    tm_bwd = min(256, s)
    pos = jnp.stack([lax.axis_index("x"), lax.axis_index("y"), lax.axis_index("c")]).astype(jnp.int32)

    t_bf = lambda w: w[0].T.astype(BF16)
    full = lambda a: a.reshape(N_DEV * a.shape[1], d)
    wg1, wu1 = map(full, _run_exchange(_AllGather([t_bf(ffn1_w_gate), t_bf(ffn1_w_up)]), "gather_ffn1_gate_up"))

    tabs = _rope_tables(s)
    pband, ptband = _pool_tables(s)
    g1, gm, g2, gf = ffn1_norm, mix_norm, ffn2_norm, final_norm.reshape(1, d)

    x0 = x[0]
    n1, gate1, up1, act1_t, wd1, win_t = _ffn_gate_up(
        x0, g1, wg1, wu1, "ffn1_gate_up", tm, carry=_AllGather([ffn1_w_down[0].astype(BF16), t_bf(w_in)]))
    wd1, win_t = full(wd1), full(win_t)
    h1, q, k, v, pc, n2, wout, wg2 = _ffn_down_mix_in(
        x0, act1_t, wd1, gm, win_t, tabs, "ffn1_down_mix_in", tm,
        carry=_AllGather([w_out[0].astype(BF16), t_bf(ffn2_w_gate)]))
    wout, wg2 = full(wout), full(wg2)
    h2, mix_t, *gathered = _attn_pool_fwd(h1, wout, q, k, v, pc, sink_logits, pool_w[0], pool_scale, pband,
                                          "attn_pool_fwd",
                                          carry=_AllGather([t_bf(ffn2_w_up), ffn2_w_down[0].astype(BF16)]))
    wu2, wd2 = map(full, gathered)
    dh3, n3, gate2, up2, act2_t, loss_part, dgf = _ffn_fwd(h2, g2, wg2, wu2, wd2, "ffn2_fwd", tm,
                                                           head=(gf, loss_target[0]))

    sum1, recv2 = {}, {}

    def stage2(keys):
        return _RsStage2([sum1[key][1] for key in keys])

    dh2, dg2, dgate2_t, dup2_t, dhh3, dmix, dh2b = _ffn_bwd(h2, dh3, gate2, up2, g2, wg2, wu2, wd2, wout, "ffn2_bwd",
                                                            tm_bwd)
    both = _wgrad_rs1(pos, [dgate2_t, dup2_t], n3, "wgrad_gate_up2")
    sum1["g2"], sum1["u2"] = both[:2], both[2:]
    sum1["d2"] = _wgrad_rs1(pos, [act2_t], dhh3, "wgrad_down2")
    sum1["out"] = _wgrad_rs1(pos, [mix_t], dh2b, "wgrad_out")
    dq, dkp, dvp, dpc, dsink, dpw, dps, *r2 = _attn_pool_bwd(
        q, k, v, pc, dmix, sink_logits, pool_w[0], pool_scale, pband, ptband, "attn_pool_bwd",
        carry=stage2(["g2", "u2", "d2"]))
    recv2.update(zip(["g2", "u2", "d2"], r2))
    dh1, du_t, dgm, recv2["out"] = _mix_in_bwd(h1, dh2, gm, win_t, dq, dkp, dvp, dpc, tabs, "mix_in_bwd", tm,
                                               carry=stage2(["out"]))
    sum1["in"] = _wgrad_rs1(pos, [du_t], n2, "wgrad_in")
    small_part = _pack_small(jnp.zeros_like(dgm), dgm, dg2, dgf, dsink[:, 0], dpw, dps, loss_part)
    dgate1_t, dup1_t, dhh1, recv2["in"] = _ffn_bwd_gates(dh1, gate1, up1, wd1, "ffn1_bwd_gates", tm,
                                                         carry=stage2(["in"]))
    *sum1["g1"], small_all = _wgrad_rs1(pos, [dgate1_t], n1, "wgrad_gate1", carry=_AllGather([small_part]))
    *sum1["u1"], recv2["g1"] = _wgrad_rs1(pos, [dup1_t], n1, "wgrad_up1", carry=stage2(["g1"]))
    *sum1["d1"], recv2["u1"] = _wgrad_rs1(pos, [act1_t], dhh1, "wgrad_down1", carry=stage2(["u1"]))
    dx, dg1, recv2["d1"] = _ffn_bwd_input(x0, dh1, dgate1_t, dup1_t, g1, wg1, wu1, "ffn1_bwd_input", tm,
                                          carry=stage2(["d1"]))

    (dg1_all,) = _run_exchange(_DirectGather([dg1.reshape(8, 128)]), "gather_norm1_grad")
    pk = lambda a, b, c_, e, s_, pw_, psc: _pack_small(a, b, c_, e, s_[0], pw_[0], psc)
    small_w = pk(ffn1_norm, mix_norm, ffn2_norm, final_norm, sink_logits, pool_w, pool_scale)
    small_m = pk(m_ffn1_norm, m_mix_norm, m_ffn2_norm, m_final_norm, m_sink_logits, m_pool_w, m_pool_scale)
    small_v = pk(v_ffn1_norm, v_mix_norm, v_ffn2_norm, v_final_norm, v_sink_logits, v_pool_w, v_pool_scale)
    sg, sd, sm, sv = [_unpack_small(a)
                      for a in _adam_small(small_w, small_all, dg1_all, small_m, small_v, "adam_small")]

    big = {}
    keys = ["g1", "u1", "d1", "g2", "u2", "d2", "in", "out"]
    names = ["ffn1_w_gate", "ffn1_w_up", "ffn1_w_down", "ffn2_w_gate", "ffn2_w_up", "ffn2_w_down", "w_in", "w_out"]
    transposed = [True, True, False, True, True, False, True, False]
    ws = [ffn1_w_gate, ffn1_w_up, ffn1_w_down, ffn2_w_gate, ffn2_w_up, ffn2_w_down, w_in, w_out]
    ms = [m_ffn1_w_gate, m_ffn1_w_up, m_ffn1_w_down, m_ffn2_w_gate, m_ffn2_w_up, m_ffn2_w_down, m_w_in, m_w_out]
    vs = [v_ffn1_w_gate, v_ffn1_w_up, v_ffn1_w_down, v_ffn2_w_gate, v_ffn2_w_up, v_ffn2_w_down, v_w_in, v_w_out]
    for key, nm, tr, w, m, vv in zip(keys, names, transposed, ws, ms, vs):
        view = (lambda a: jnp.swapaxes(a, 1, 2)[0]) if tr else (lambda a: a[0])
        back = (lambda a: jnp.swapaxes(a[None], 1, 2)) if tr else (lambda a: a[None])
        res = _rs_sum2_adam(pos, sum1[key][0], recv2[key], view(w), view(m), view(vv), "adam_" + nm)
        big[nm] = tuple(back(a) for a in res)

    loss = sg["loss"]
    all_names = ["ffn1_norm", "ffn1_w_gate", "ffn1_w_up", "ffn1_w_down", "mix_norm", "w_in", "sink_logits", "pool_w",
                 "pool_scale", "w_out", "ffn2_norm", "ffn2_w_gate", "ffn2_w_up", "ffn2_w_down", "final_norm"]
    outs = [loss, dx[None]]
    for idx, src in enumerate((sg, sd, sm, sv)):
        for nm in all_names:
            outs.append(big[nm][idx] if nm in big else src[nm])
    return tuple(outs)
```

```python
import functools

import jax
import jax.numpy as jnp
import numpy as np
from jax import lax
from jax.experimental import pallas as pl
from jax.experimental.pallas import tpu as pltpu

F32 = jnp.float32
BF16 = jnp.bfloat16
MESH = pl.DeviceIdType.MESH
N_DEV = 8

EPS = 1e-6
HEAD_DIM = 64
N_HEADS = 8
N_KV = 2
GROUP = N_HEADS // N_KV
ATTN_W = N_HEADS * HEAD_DIM
KV_W = N_KV * HEAD_DIM
POOL_W = 512
POOL_G = 4
POOL_GW = POOL_W // POOL_G
POOL_WINDOWS = (2, 4, 8, 16)
BLK = 128
ROT = 16
ROPE_THETA = 500000.0
SCORE_SCALE = HEAD_DIM ** -0.5

ADAM_LR, ADAM_B1, ADAM_B2, ADAM_EPS, ADAM_WD, ADAM_STEP = 0.001, 0.9, 0.999, 1e-08, 0.01, 10

VMEM_LIMIT = 56 * 1024 * 1024


def _cparams(sem=None, **kw):
    if sem is not None:
        kw["dimension_semantics"] = sem
    return pltpu.CompilerParams(vmem_limit_bytes=VMEM_LIMIT, **kw)


def _whole(shape):
    nd = len(shape)
    return pl.BlockSpec(shape, lambda *_: (0,) * nd, pipeline_mode=pl.Buffered(1))


def _sigmoid(z):
    return 1.0 / (1.0 + jnp.exp(-z))


def _dot_nt(a, b):
    return lax.dot_general(a, b, (((1,), (1,)), ((), ())), preferred_element_type=F32)


def _dot_nn(a, b):
    return lax.dot_general(a, b, (((1,), (0,)), ((), ())), preferred_element_type=F32)


def _dot_tn(a, b):
    return lax.dot_general(a, b, (((0,), (0,)), ((), ())), preferred_element_type=F32)


def _rope_tables(s):
    inv_freq = ROPE_THETA ** (-np.arange(0, ROT, 2, dtype=np.float64) / ROT)
    ang = np.arange(s, dtype=np.float64)[:, None] * inv_freq[None, :]
    c = np.ones((s, HEAD_DIM)); sa = np.zeros((s, HEAD_DIM)); sb = np.zeros((s, HEAD_DIM))
    c[:, :8] = np.cos(ang); c[:, 8:16] = np.cos(ang)
    sa[:, :8] = -np.sin(ang)
    sb[:, 8:16] = np.sin(ang)
    t = lambda a: jnp.asarray(np.tile(a, (1, 2)).astype(np.float32))
    return t(c), t(sa), t(sb)


def _pool_weight(gi, t, s_pos, s):
    half = POOL_WINDOWS[gi] // 2

    def win(lo, hi):
        a = np.clip(lo, 0, s); b = np.clip(hi + 1, 0, s)
        inside = (s_pos >= a) & (s_pos < b)
        return inside / np.maximum(b - a, 1)

    w = 0.5 * (win(t - half, t + half - 1) + win(t - half + 1, t + half)) - (t == s_pos)
    return w * ((t >= 0) & (t < s) & (s_pos >= 0) & (s_pos < s))


def _pool_tables(s):
    nb = s // BLK
    fwd = np.zeros((3, POOL_G, BLK, 3 * BLK), np.float32)
    bwd = np.zeros((3, POOL_G, BLK, 3 * BLK), np.float32)
    for vi, n in enumerate((0, 1 if nb > 2 else 0, nb - 1)):
        i = n * BLK + np.arange(BLK)[:, None]
        j = (n - 1) * BLK + np.arange(3 * BLK)[None, :]
        for gi in range(POOL_G):
            fwd[vi, gi] = _pool_weight(gi, i, j, s)
            bwd[vi, gi] = _pool_weight(gi, j, i, s)
    return jnp.asarray(fwd, dtype=BF16), jnp.asarray(bwd, dtype=BF16)


def _variant_index(n, nb):
    return jnp.where(n == 0, 0, jnp.where(n == nb - 1, 2, 1))


class _Exchange:
    inputs = ()
    out_shapes = ()
    sems = ()

    def start(self, srcs, outs, sems):
        raise NotImplementedError

    def middle(self, srcs, outs, sems):
        pass

    def finish(self, srcs, outs, sems):
        raise NotImplementedError


class _AllGather(_Exchange):
    def __init__(self, arrays):
        n = len(arrays)
        self.inputs = list(arrays)
        self.out_shapes = [jax.ShapeDtypeStruct((N_DEV,) + a.shape, a.dtype) for a in arrays]
        self.sems = [pltpu.SemaphoreType.DMA((n, 8)), pltpu.SemaphoreType.DMA((n, 8)), pltpu.SemaphoreType.DMA((n,))]

    def _parts(self, srcs, outs, sems):
        send_sems, recv_sems, local_sems = sems
        n = len(srcs)
        x, y, c = lax.axis_index("x"), lax.axis_index("y"), lax.axis_index("c")
        me, sibling, xn, yn, diag = (x, y, c), (x, y, 1 - c), (1 - x, y, c), (x, 1 - y, c), (1 - x, 1 - y, c)

        def place(a, dev, half=None):
            block = outs[a].at[4 * dev[0] + 2 * dev[1] + dev[2]]
            if half is None:
                return block
            r2 = outs[a].shape[1] // 2
            return block.at[pl.ds(half * r2, r2)]

        def copy(a, k, dev, to, half=None, src=None):
            where = place(a, dev, half)
            return pltpu.make_async_remote_copy(
                src_ref=where if src is None else src, dst_ref=where, send_sem=send_sems.at[a, k],
                recv_sem=recv_sems.at[a, k], device_id=to, device_id_type=MESH)

        def other(dev):
            return (dev[0], dev[1], 1 - dev[2])

        class Parts:
            mine = staticmethod(lambda: [pltpu.make_async_copy(srcs[a], place(a, me), local_sems.at[a])
                                         for a in range(n)])
            own = staticmethod(lambda: [copy(a, k, me, to, src=srcs[a]) for a in range(n)
                                        for k, to in ((0, sibling), (1, xn), (2, yn))])
            relay = staticmethod(lambda a: [copy(a, 3, xn, yn, half=0), copy(a, 4, yn, xn, half=1),
                                            copy(a, 5, xn, sibling), copy(a, 6, yn, sibling)])
            last = staticmethod(lambda a: copy(a, 7, diag, sibling))
            from_x = staticmethod(lambda a: copy(a, 1, xn, me))
            from_y = staticmethod(lambda a: copy(a, 2, yn, me))
            diag_halves = staticmethod(lambda a: [copy(a, 3, diag, me, half=0), copy(a, 4, diag, me, half=1)])
            from_sibling = staticmethod(lambda a: [copy(a, 0, sibling, me), copy(a, 5, other(xn), me),
                                                   copy(a, 6, other(yn), me), copy(a, 7, other(diag), me)])

        return n, Parts

    def start(self, srcs, outs, sems):
        _, p = self._parts(srcs, outs, sems)
        for cp in p.mine() + p.own():
            cp.start()

    def middle(self, srcs, outs, sems):
        n, p = self._parts(srcs, outs, sems)
        for a in range(n):
            p.from_x(a).wait_recv()
            p.from_y(a).wait_recv()
            for cp in p.relay(a):
                cp.start()

    def finish(self, srcs, outs, sems):
        n, p = self._parts(srcs, outs, sems)
        for a in range(n):
            for cp in p.diag_halves(a):
                cp.wait_recv()
            p.last(a).start()
        for a in range(n):
            for cp in p.from_sibling(a):
                cp.wait_recv()
        for cp in p.own() + [cp for a in range(n) for cp in p.relay(a) + [p.last(a)]]:
            cp.wait_send()
        for cp in p.mine():
            cp.wait()


class _RsStage2(_Exchange):
    def start(self, srcs, outs, sems):
        for cp in self._copies(srcs, outs, sems):
            cp.start()

    def finish(self, srcs, outs, sems):
        copies = self._copies(srcs, outs, sems)
        for cp in copies:
            cp.wait_recv()
        for cp in copies:
            cp.wait_send()


    def __init__(self, pbs):
        n = len(pbs)
        self.inputs = list(pbs)
        self.out_shapes = [jax.ShapeDtypeStruct((3,) + p.shape[1:], p.dtype) for p in pbs]
        self.sems = [pltpu.SemaphoreType.DMA((n, 3)), pltpu.SemaphoreType.DMA((n, 3))]

    def _copies(self, srcs, outs, sems):
        send_sems, recv_sems = sems
        x, y, c = lax.axis_index("x"), lax.axis_index("y"), lax.axis_index("c")
        chips = [(1 - x, y), (x, 1 - y), (1 - x, 1 - y)]
        return [pltpu.make_async_remote_copy(
            src_ref=srcs[a].at[2 * chip[0] + chip[1]], dst_ref=outs[a].at[j], send_sem=send_sems.at[a, j],
            recv_sem=recv_sems.at[a, j], device_id=(*chip, c), device_id_type=MESH)
            for a in range(len(srcs)) for j, chip in enumerate(chips)]


class _DirectGather(_Exchange):
    def __init__(self, arrays):
        n = len(arrays)
        self.inputs = list(arrays)
        self.out_shapes = [jax.ShapeDtypeStruct((N_DEV,) + a.shape, a.dtype) for a in arrays]
        self.sems = [pltpu.SemaphoreType.DMA((n, 7)), pltpu.SemaphoreType.DMA((n, 7)), pltpu.SemaphoreType.DMA((n,))]

    def _copies(self, srcs, outs, sems):
        send_sems, recv_sems, local_sems = sems
        x, y, c = lax.axis_index("x"), lax.axis_index("y"), lax.axis_index("c")
        me = 4 * x + 2 * y + c
        remote, local = [], []
        for a in range(len(srcs)):
            local.append(pltpu.make_async_copy(srcs[a], outs[a].at[me], local_sems.at[a]))
            for k in range(1, N_DEV):
                peer = (x ^ (k >> 2), y ^ ((k >> 1) & 1), c ^ (k & 1))
                remote.append(pltpu.make_async_remote_copy(
                    src_ref=srcs[a], dst_ref=outs[a].at[me], send_sem=send_sems.at[a, k - 1],
                    recv_sem=recv_sems.at[a, k - 1], device_id=peer, device_id_type=MESH))
        return remote, local

    def start(self, srcs, outs, sems):
        remote, local = self._copies(srcs, outs, sems)
        for cp in local + remote:
            cp.start()

    def finish(self, srcs, outs, sems):
        remote, local = self._copies(srcs, outs, sems)
        for cp in remote:
            cp.wait_recv()
        for cp in remote:
            cp.wait_send()
        for cp in local:
            cp.wait()


class _Both(_Exchange):
    def __init__(self, a, b):
        self.a, self.b = a, b
        self.inputs = list(a.inputs) + list(b.inputs)
        self.out_shapes = list(a.out_shapes) + list(b.out_shapes)
        self.sems = list(a.sems) + list(b.sems)

    def _split(self, srcs, outs, sems):
        na, oa, sa = len(self.a.inputs), len(self.a.out_shapes), len(self.a.sems)
        return (srcs[:na], outs[:oa], sems[:sa]), (srcs[na:], outs[oa:], sems[sa:])

    def start(self, srcs, outs, sems):
        pa, pb = self._split(srcs, outs, sems)
        self.a.start(*pa)
        self.b.start(*pb)

    def middle(self, srcs, outs, sems):
        pa, pb = self._split(srcs, outs, sems)
        self.a.middle(*pa)
        self.b.middle(*pb)

    def finish(self, srcs, outs, sems):
        pa, pb = self._split(srcs, outs, sems)
        self.a.finish(*pa)
        self.b.finish(*pb)


_ANY = pl.BlockSpec(memory_space=pl.ANY)


def _run_exchange(ex, name):
    n_in, n_out = len(ex.inputs), len(ex.out_shapes)

    def body(*refs):
        srcs, outs, sems = refs[:n_in], refs[n_in:n_in + n_out], refs[n_in + n_out:]
        ex.start(srcs, outs, sems)
        ex.middle(srcs, outs, sems)
        ex.finish(srcs, outs, sems)

    return pl.pallas_call(
        body, name=name, out_shape=list(ex.out_shapes), in_specs=[_ANY] * n_in, out_specs=[_ANY] * n_out,
        scratch_shapes=list(ex.sems),
    )(*ex.inputs)


CARRY_MIDDLE = 0.7


def _call(body, name, grid, in_specs, out_specs, out_shape, args, sem, carry=None, middle=CARRY_MIDDLE, scratch=()):
    if carry is None:
        return pl.pallas_call(functools.partial(body), name=name, grid=grid, in_specs=in_specs, out_specs=out_specs,
                              out_shape=out_shape, scratch_shapes=list(scratch), compiler_params=_cparams(sem))(*args)
    n_in, n_out = len(in_specs), len(out_specs)
    nc_in, nc_out = len(carry.inputs), len(carry.out_shapes)

    def carried(*refs):
        ins = refs[:n_in]
        c_in = refs[n_in:n_in + nc_in]
        outs = refs[n_in + nc_in:n_in + nc_in + n_out]
        c_out = refs[n_in + nc_in + n_out:n_in + nc_in + n_out + nc_out]
        own = refs[n_in + nc_in + n_out + nc_out:n_in + nc_in + n_out + nc_out + len(scratch)]
        sems = refs[n_in + nc_in + n_out + nc_out + len(scratch):]
        ids = [pl.program_id(i) for i in range(len(grid))]
        is_first = functools.reduce(jnp.logical_and, [i == 0 for i in ids])
        is_last = functools.reduce(jnp.logical_and, [i == g - 1 for i, g in zip(ids, grid)])
        @pl.when(is_first)
        def _():
            carry.start(c_in, c_out, sems)

        if middle is not None:
            @pl.when(functools.reduce(jnp.logical_and, [ids[0] == round(middle * (grid[0] - 1))]
                                      + [i == 0 for i in ids[1:]]))
            def _():
                carry.middle(c_in, c_out, sems)

        body(*ins, *outs, *own)

        @pl.when(is_last)
        def _():
            if middle is None:
                carry.middle(c_in, c_out, sems)
            carry.finish(c_in, c_out, sems)

    return pl.pallas_call(
        carried, name=name, grid=grid, in_specs=list(in_specs) + [_ANY] * nc_in,
        out_specs=list(out_specs) + [_ANY] * nc_out, out_shape=list(out_shape) + list(carry.out_shapes),
        scratch_shapes=list(scratch) + list(carry.sems), compiler_params=_cparams(sem))(*args, *carry.inputs)


def _rs_sum2_adam(pos, p, r2, w, m, v, name):
    _, rows, d = p.shape
    tr = rows // 2 if rows % 16 == 0 else rows

    def body(pos_ref, p_ref, r_ref, w_ref, m_ref, v_ref, g_ref, d_ref, nm_ref, nv_ref):
        r = r_ref[...].astype(F32)
        g = ((p_ref[...] + r[0]) + r[1]) + r[2]
        g_ref[...] = g
        d_ref[...], nm_ref[...], nv_ref[...] = _adam_math(w_ref[...], g, m_ref[...], v_ref[...])

    blk = pl.BlockSpec((tr, d), lambda i, pos_ref: (i, 0))
    return pl.pallas_call(
        body, name=name,
        grid_spec=pltpu.PrefetchScalarGridSpec(
            num_scalar_prefetch=1, grid=(rows // tr,),
            in_specs=[pl.BlockSpec((None, tr, d), lambda i, pos_ref: (2 * pos_ref[0] + pos_ref[1], i, 0)),
                      pl.BlockSpec((3, tr, d), lambda i, pos_ref: (0, i, 0)), blk, blk, blk],
            out_specs=[blk] * 4),
        out_shape=[jax.ShapeDtypeStruct((rows, d), F32)] * 4,
        compiler_params=_cparams(("arbitrary",)),
    )(pos, p, r2, w, m, v)


def _ffn_chunk(f):
    for cand in (256, 128):
        if f % cand == 0:
            return cand
    return f


def _loss_head(x, gg, target, loss_ref, dg_ref):
    @pl.when(pl.program_id(0) == 0)
    def _():
        loss_ref[...] = jnp.zeros_like(loss_ref)
        dg_ref[...] = jnp.zeros_like(dg_ref)

    r = lax.rsqrt(jnp.mean(x * x, axis=-1, keepdims=True) + EPS)
    xhat = x * r
    e = xhat * gg - target
    loss_ref[...] += 0.5 * jnp.sum(jnp.mean(e * e, axis=-1, keepdims=True), axis=0, keepdims=True)
    dy = e * (1.0 / x.shape[-1])
    dg_ref[...] += jnp.sum(dy * xhat, axis=0, keepdims=True)
    dxhat = dy * gg
    return r * (dxhat - xhat * jnp.mean(dxhat * xhat, axis=-1, keepdims=True))


def _ffn_gate(h, g, wg_t, name, tm, carry=None):
    s, d = h.shape
    f = wg_t.shape[0]

    def body(h_ref, g_ref, wg_ref, n_ref, gate_ref):
        x = h_ref[...]
        r = lax.rsqrt(jnp.mean(x * x, axis=-1, keepdims=True) + EPS)
        nb = (x * r * g_ref[...]).astype(BF16)
        n_ref[...] = nb
        gate_ref[...] = _dot_nt(nb, wg_ref[...]).astype(BF16)

    row = lambda w: pl.BlockSpec((tm, w), lambda i: (i, 0))
    return _call(body, name, (s // tm,), [row(d), _whole((1, d)), _whole((f, d))], [row(d), row(f)],
                 [jax.ShapeDtypeStruct((s, d), BF16), jax.ShapeDtypeStruct((s, f), BF16)], (h, g, wg_t),
                 ("arbitrary",), carry)


def _ffn_up(n, gate, wu_t, name, tm, carry=None):
    s, d = n.shape
    f = wu_t.shape[0]
    tf = _ffn_chunk(f)

    def body(n_ref, gate_ref, wu_ref, up_ref, act_t_ref):
        nb = n_ref[...]
        for j in range(f // tf):
            sl = slice(j * tf, (j + 1) * tf)
            up = _dot_nt(nb, wu_ref[sl, :])
            gate = gate_ref[:, sl].astype(F32)
            up_ref[:, sl] = up.astype(BF16)
            act_t_ref[sl, :] = (gate * _sigmoid(gate) * up).astype(BF16).T

    row = lambda w: pl.BlockSpec((tm, w), lambda i: (i, 0))
    return _call(body, name, (s // tm,), [row(d), row(f), _whole((f, d))],
                 [row(f), pl.BlockSpec((f, tm), lambda i: (0, i))],
                 [jax.ShapeDtypeStruct((s, f), BF16), jax.ShapeDtypeStruct((f, s), BF16)], (n, gate, wu_t),
                 ("arbitrary",), carry)


def _ffn_fwd(h, g, wg_t, wu_t, wd, name, tm, carry=None, head=None, middle=CARRY_MIDDLE):
    s, d = h.shape
    f = wg_t.shape[0]
    tf = _ffn_chunk(f)

    def body(h_ref, g_ref, wg_ref, wu_ref, wd_ref, *refs):
        if head is None:
            o_ref, n_ref, gate_ref, up_ref, act_t_ref, act_ref = refs
        else:
            gf_ref, t_ref, o_ref, n_ref, gate_ref, up_ref, act_t_ref, loss_ref, dgf_ref, act_ref = refs
        x = h_ref[...]
        r = lax.rsqrt(jnp.mean(x * x, axis=-1, keepdims=True) + EPS)
        nb = (x * r * g_ref[...]).astype(BF16)
        n_ref[...] = nb
        for j in range(f // tf):
            sl = slice(j * tf, (j + 1) * tf)
            gate = _dot_nt(nb, wg_ref[sl, :])
            up = _dot_nt(nb, wu_ref[sl, :])
            gate_ref[:, sl] = gate.astype(BF16)
            up_ref[:, sl] = up.astype(BF16)
            act = gate * _sigmoid(gate) * up
            act = act.astype(BF16)
            act_ref[:, sl] = act
            act_t_ref[sl, :] = act.T
        h_out = x + 0.5 * _dot_nn(act_ref[...], wd_ref[...])
        o_ref[...] = h_out if head is None else _loss_head(h_out, gf_ref[...], t_ref[...], loss_ref, dgf_ref)

    row = lambda w: pl.BlockSpec((tm, w), lambda i: (i, 0))
    in_specs = [row(d), _whole((1, d)), _whole((f, d)), _whole((f, d)), _whole((f, d))]
    out_specs = [row(d), row(d), row(f), row(f), pl.BlockSpec((f, tm), lambda i: (0, i))]
    out_shape = ([jax.ShapeDtypeStruct((s, d), F32), jax.ShapeDtypeStruct((s, d), BF16)]
                 + [jax.ShapeDtypeStruct((s, f), BF16)] * 2 + [jax.ShapeDtypeStruct((f, s), BF16)])
    args = (h, g, wg_t, wu_t, wd)
    scratch = (pltpu.VMEM((tm, f), BF16),)
    if head is not None:
        in_specs += [_whole((1, d)), row(d)]
        out_specs += [pl.BlockSpec((1, 1), lambda i: (0, 0)), pl.BlockSpec((1, d), lambda i: (0, 0))]
        out_shape += [jax.ShapeDtypeStruct((1, 1), F32), jax.ShapeDtypeStruct((1, d), F32)]
        args += tuple(head)
    return _call(body, name, (s // tm,), in_specs, out_specs, out_shape, args, ("arbitrary",), carry, middle, scratch)


def _gate_grads(dh_ref, gate_ref, up_ref, wd_ref, dgate_t_ref, dup_t_ref, dhh_ref, tf):
    dhh = (0.5 * dh_ref[...]).astype(BF16)
    dhh_ref[...] = dhh
    for j in range(gate_ref.shape[1] // tf):
        sl = slice(j * tf, (j + 1) * tf)
        gt = gate_ref[:, sl].astype(F32)
        u = up_ref[:, sl].astype(F32)
        dact = _dot_nt(dhh, wd_ref[sl, :])
        sg = _sigmoid(gt)
        dup = dact * (gt * sg)
        dgate = dact * u * (sg * (1.0 + gt * (1.0 - sg)))
        dup_t_ref[sl, :] = dup.astype(BF16).T
        dgate_t_ref[sl, :] = dgate.astype(BF16).T


def _input_grad(h_ref, dh_ref, dgate_t_ref, dup_t_ref, g_ref, wg_ref, wu_ref, o_ref, dg_ref):
    x = h_ref[...]
    r = lax.rsqrt(jnp.mean(x * x, axis=-1, keepdims=True) + EPS)
    xhat = x * r
    dn = _dot_tn(dgate_t_ref[...], wg_ref[...]) + _dot_tn(dup_t_ref[...], wu_ref[...])
    dxhat = dn * g_ref[...]
    o_ref[...] = dh_ref[...] + r * (dxhat - xhat * jnp.mean(dxhat * xhat, axis=-1, keepdims=True))

    @pl.when(pl.program_id(0) == 0)
    def _():
        dg_ref[...] = jnp.zeros_like(dg_ref)

    dg_ref[...] += jnp.sum(dn * xhat, axis=0, keepdims=True)


def _ffn_bwd(h_in, dh_out, gate, up, g, wg_t, wu_t, wd, w_out, name, tm):
    s, d = h_in.shape
    f = gate.shape[1]
    w = w_out.shape[0]
    tf = _ffn_chunk(f)

    def body(h_ref, dh_ref, gate_ref, up_ref, g_ref, wg_ref, wu_ref, wd_ref, wo_ref,
             o_ref, dg_ref, dgate_t_ref, dup_t_ref, dhh_ref, dmix_ref, dhb_ref):
        _gate_grads(dh_ref, gate_ref, up_ref, wd_ref, dgate_t_ref, dup_t_ref, dhh_ref, tf)
        _input_grad(h_ref, dh_ref, dgate_t_ref, dup_t_ref, g_ref, wg_ref, wu_ref, o_ref, dg_ref)
        dhb = o_ref[...].astype(BF16)
        dhb_ref[...] = dhb
        dmix_ref[...] = _dot_nt(dhb, wo_ref[...])

    row = lambda c: pl.BlockSpec((tm, c), lambda i: (i, 0))
    col = pl.BlockSpec((f, tm), lambda i: (0, i))
    return pl.pallas_call(
        body, name=name, grid=(s // tm,),
        in_specs=[row(d), row(d), row(f), row(f), _whole((1, d)), _whole((f, d)), _whole((f, d)), _whole((f, d)),
                  _whole((w, d))],
        out_specs=[row(d), pl.BlockSpec((1, d), lambda i: (0, 0)), col, col, row(d), row(w), row(d)],
        out_shape=[jax.ShapeDtypeStruct((s, d), F32), jax.ShapeDtypeStruct((1, d), F32),
                   jax.ShapeDtypeStruct((f, s), BF16), jax.ShapeDtypeStruct((f, s), BF16),
                   jax.ShapeDtypeStruct((s, d), BF16), jax.ShapeDtypeStruct((s, w), F32),
                   jax.ShapeDtypeStruct((s, d), BF16)],
        compiler_params=_cparams(("arbitrary",)),
    )(h_in, dh_out, gate, up, g, wg_t, wu_t, wd, w_out)


def _ffn_bwd_gates(dh_out, gate, up, wd, name, tm, carry=None):
    s, d = dh_out.shape
    f = gate.shape[1]
    tf = _ffn_chunk(f)

    def body(dh_ref, gate_ref, up_ref, wd_ref, dgate_t_ref, dup_t_ref, dhh_ref):
        _gate_grads(dh_ref, gate_ref, up_ref, wd_ref, dgate_t_ref, dup_t_ref, dhh_ref, tf)

    row = lambda w: pl.BlockSpec((tm, w), lambda i: (i, 0))
    col = pl.BlockSpec((f, tm), lambda i: (0, i))
    return _call(
        body, name, (s // tm,), [row(d), row(f), row(f), _whole((f, d))], [col, col, row(d)],
        [jax.ShapeDtypeStruct((f, s), BF16)] * 2 + [jax.ShapeDtypeStruct((s, d), BF16)],
        (dh_out, gate, up, wd), ("arbitrary",), carry)


def _ffn_bwd_input(h_in, dh_out, dgate_t, dup_t, g, wg_t, wu_t, name, tm, carry=None):
    s, d = h_in.shape
    f = dgate_t.shape[0]

    row = lambda w: pl.BlockSpec((tm, w), lambda i: (i, 0))
    col = pl.BlockSpec((f, tm), lambda i: (0, i))
    return _call(
        _input_grad, name, (s // tm,),
        [row(d), row(d), col, col, _whole((1, d)), _whole((f, d)), _whole((f, d))],
        [row(d), pl.BlockSpec((1, d), lambda i: (0, 0))],
        [jax.ShapeDtypeStruct((s, d), F32), jax.ShapeDtypeStruct((1, d), F32)],
        (h_in, dh_out, dgate_t, dup_t, g, wg_t, wu_t), ("arbitrary",), carry)


def _wgrad_rs1(pos, a_ts, b, name, carry=None):
    n_a = len(a_ts)
    f, s = a_ts[0].shape
    d = b.shape[1]
    fk = f // N_DEV
    nc_in = 0 if carry is None else len(carry.inputs)
    nc_out = 0 if carry is None else len(carry.out_shapes)

    def body(pos_ref, *refs):
        a_refs, b_ref, refs = refs[:n_a], refs[n_a], refs[n_a + 1:]
        c_in = refs[:nc_in]
        outs = refs[nc_in:nc_in + 2 * n_a]
        c_out = refs[nc_in + 2 * n_a:nc_in + 2 * n_a + nc_out]
        stage, land, send_sems, recv_sems = refs[nc_in + 2 * n_a + nc_out:nc_in + 2 * n_a + nc_out + 4]
        c_sems = refs[nc_in + 2 * n_a + nc_out + 4:]
        t = pl.program_id(0)
        which, step = t // 8, t % 8
        slot = 4 * which + step % 4
        x, y, c = lax.axis_index("x"), lax.axis_index("y"), lax.axis_index("c")

        def push(k):
            return pltpu.make_async_remote_copy(src_ref=stage.at[k], dst_ref=land.at[k], send_sem=send_sems.at[k],
                                                recv_sem=recv_sems.at[k], device_id=(x, y, 1 - c), device_id_type=MESH)

        if carry is not None:
            @pl.when(t == 0)
            def _():
                carry.start(c_in, c_out, c_sems)

        for j in range(n_a):
            @pl.when(which == j)
            def _(j=j):
                g = _dot_nn(a_refs[j][...], b_ref[...])

                @pl.when(step < 4)
                def _():
                    stage[slot] = g.astype(BF16)
                    push(slot).start()

                @pl.when(step >= 4)
                def _():
                    push(slot).wait_recv()
                    p = g + land[slot].astype(F32)
                    outs[2 * j][...] = p
                    outs[2 * j + 1][...] = p.astype(BF16)

        if carry is not None:
            @pl.when(t == round(CARRY_MIDDLE * (8 * n_a - 1)))
            def _():
                carry.middle(c_in, c_out, c_sems)

        @pl.when(t == 8 * n_a - 1)
        def _():
            for k in range(4 * n_a):
                push(k).wait_send()
            if carry is not None:
                carry.finish(c_in, c_out, c_sems)

    def shard(j):
        def index_map(t, pos_ref):
            step = jnp.clip(t - 8 * j, 0, 7)
            return 4 * ((step % 4) // 2) + 2 * (step % 2) + jnp.where(step < 4, 1 - pos_ref[2], pos_ref[2]), 0
        return index_map

    out = lambda j: pl.BlockSpec((None, fk, d), lambda t, pos_ref, j=j: (jnp.clip(t - 8 * j - 4, 0, 3), 0, 0))
    return pl.pallas_call(
        body, name=name,
        grid_spec=pltpu.PrefetchScalarGridSpec(
            num_scalar_prefetch=1, grid=(8 * n_a,),
            in_specs=[pl.BlockSpec((fk, s), shard(j)) for j in range(n_a)]
            + [pl.BlockSpec((s, d), lambda t, pos_ref: (0, 0), pipeline_mode=pl.Buffered(1))] + [_ANY] * nc_in,
            out_specs=[out(j) for j in range(n_a) for _ in range(2)] + [_ANY] * nc_out,
            scratch_shapes=[pltpu.VMEM((4 * n_a, fk, d), BF16), pltpu.VMEM((4 * n_a, fk, d), BF16),
                            pltpu.SemaphoreType.DMA((4 * n_a,)), pltpu.SemaphoreType.DMA((4 * n_a,))]
            + ([] if carry is None else list(carry.sems))),
        out_shape=[jax.ShapeDtypeStruct((4, fk, d), dt) for _ in range(n_a) for dt in (F32, BF16)]
        + ([] if carry is None else list(carry.out_shapes)),
        compiler_params=_cparams(("arbitrary",)),
    )(pos, *a_ts, b, *([] if carry is None else carry.inputs))


def _rope(t, c, sa, sb, reps):
    c, sa, sb = (jnp.tile(v, (1, reps)) if reps > 1 else v for v in (c, sa, sb))
    w = t.shape[1]
    return t * c + pltpu.roll(t, w - 8, 1) * sa + pltpu.roll(t, 8, 1) * sb


def _rope_bwd(dt, c, sa, sb, reps):
    c, sa, sb = (jnp.tile(v, (1, reps)) if reps > 1 else v for v in (c, sa, sb))
    w = dt.shape[1]
    return dt * c + pltpu.roll(dt * sa, 8, 1) + pltpu.roll(dt * sb, w - 8, 1)


def _ffn_down_mix_in(x, act_t, wd, g, win_t, tabs, name, tm, carry=None):
    s, d = x.shape
    f = wd.shape[0]
    n_in = win_t.shape[0]

    def body(x_ref, act_ref, wd_ref, g_ref, w_ref, c_ref, sa_ref, sb_ref, h_ref, q_ref, k_ref, v_ref, pc_ref, n_ref):
        x = x_ref[...] + 0.5 * _dot_tn(act_ref[...], wd_ref[...])
        h_ref[...] = x
        r = lax.rsqrt(jnp.mean(x * x, axis=-1, keepdims=True) + EPS)
        nb = (x * r * g_ref[...]).astype(BF16)
        n_ref[...] = nb
        u = _dot_nt(nb, w_ref[...])
        c, sa, sb = c_ref[...], sa_ref[...], sb_ref[...]
        q_ref[...] = _rope(u[:, :ATTN_W], c, sa, sb, ATTN_W // 128).astype(BF16)
        k_ref[...] = _rope(u[:, ATTN_W:ATTN_W + KV_W], c, sa, sb, 1).astype(BF16)
        v_ref[...] = u[:, ATTN_W + KV_W:ATTN_W + 2 * KV_W].astype(BF16)
        pc_ref[...] = u[:, ATTN_W + 2 * KV_W:]

    row = lambda w: pl.BlockSpec((tm, w), lambda i: (i, 0))
    return _call(
        body, name, (s // tm,),
        [row(d), pl.BlockSpec((f, tm), lambda i: (0, i)), _whole((f, d)), _whole((1, d)), _whole((n_in, d)),
         row(128), row(128), row(128)],
        [row(d), row(ATTN_W), row(KV_W), row(KV_W), row(POOL_W), row(d)],
        [jax.ShapeDtypeStruct((s, d), F32), jax.ShapeDtypeStruct((s, ATTN_W), BF16),
         jax.ShapeDtypeStruct((s, KV_W), BF16), jax.ShapeDtypeStruct((s, KV_W), BF16),
         jax.ShapeDtypeStruct((s, POOL_W), F32), jax.ShapeDtypeStruct((s, d), BF16)],
        (x, act_t, wd, g, win_t, *tabs), ("arbitrary",), carry)


def _band_mask(n, nb, transposed):
    shape = (3 * BLK, 2 * BLK) if transposed else (2 * BLK, 3 * BLK)
    i = lax.broadcasted_iota(jnp.int32, shape, 1 if transposed else 0) % BLK
    j = lax.broadcasted_iota(jnp.int32, shape, 0 if transposed else 1)
    kpos = (n - 1) * BLK + j
    return (j >= i) & (j <= i + 2 * BLK) & (kpos >= 0) & (kpos < nb * BLK)


def _block_diag(t, kh):
    tf = t.astype(F32)
    tr = pltpu.roll(tf, HEAD_DIM, 1)
    lo = lax.broadcasted_iota(jnp.int32, tf.shape, 1) < HEAD_DIM
    top, bot = (tf, tr) if kh == 0 else (tr, tf)
    return jnp.concatenate([jnp.where(lo, top, 0.0), jnp.where(lo, 0.0, bot)], axis=0).astype(BF16)


def _fold_diag(tbd):
    lo = lax.broadcasted_iota(jnp.int32, (3 * BLK, 2 * HEAD_DIM), 1) < HEAD_DIM
    t = jnp.where(lo, tbd[:3 * BLK], tbd[3 * BLK:])
    return t + pltpu.roll(t, HEAD_DIM, 1)


def _stack_pairs(x, kh):
    return jnp.concatenate([x[:, (2 * kh) * 128:(2 * kh + 1) * 128], x[:, (2 * kh + 1) * 128:(2 * kh + 2) * 128]], axis=0)


def _sink_of(sink_ref, kh, half, axis):
    shape = (2 * BLK, 1) if axis == 0 else (1, 2 * BLK)
    first = lax.broadcasted_iota(jnp.int32, shape, axis) < BLK
    return jnp.where(first, sink_ref[0, GROUP * kh + half], sink_ref[0, GROUP * kh + 2 + half])


def _softmax_sink(sc, valid, sink, axis):
    sc = jnp.where(valid, sc, -1e30)
    m = jnp.maximum(jnp.max(sc, axis=axis, keepdims=True), sink)
    e = jnp.exp(sc - m)
    es = jnp.exp(sink - m)
    inv = 1.0 / (jnp.sum(e, axis=axis, keepdims=True) + es)
    return e * inv, es * inv


def _attn_blocks_per_step(nb):
    return next(nq for nq in (4, 2, 1) if nb % nq == 0)


def _band_specs(nq, nb, w, col=0):
    return [pl.BlockSpec((BLK, w), lambda m: (jnp.maximum(nq * m - 1, 0), col)),
            pl.BlockSpec((nq * BLK, w), lambda m: (m, col)),
            pl.BlockSpec((BLK, w), lambda m: (jnp.minimum(nq * m + nq, nb - 1), col))]


def _attn_pool_fwd(h, w_out, q, k, v, pc, sink, pool_w, pool_scale, pband, name, carry=None, middle=CARRY_MIDDLE):
    s, d = h.shape
    nb = s // BLK
    nq = _attn_blocks_per_step(nb)

    def body(sink_ref, q_ref, k0, k1, k2, v0, v1, v2, p0, p1, p2, pw_ref, ps_ref, pb_ref, h_ref, wo_ref,
             h_out_ref, o_t_ref, o_ref):
        kall = jnp.concatenate([k0[...], k1[...], k2[...]], axis=0)
        vall = jnp.concatenate([v0[...], v1[...], v2[...]], axis=0)
        pall = jnp.concatenate([p0[...], p1[...], p2[...]], axis=0).astype(BF16)
        qall = q_ref[...] * SCORE_SCALE
        for j in range(nq):
            n = pl.program_id(0) * nq + j
            rows, band = slice(j * BLK, (j + 1) * BLK), slice(j * BLK, (j + 3) * BLK)
            valid = _band_mask(n, nb, False)
            kb, vb, qs = kall[band], vall[band], qall[rows]
            for kh in range(N_KV):
                sc = _dot_nt(_stack_pairs(qs, kh), _block_diag(kb, kh))
                p = [_softmax_sink(sc[:, half * 3 * BLK:(half + 1) * 3 * BLK], valid,
                                   _sink_of(sink_ref, kh, half, 0), 1)[0] for half in range(2)]
                o2 = _dot_nn(jnp.concatenate(p, axis=1).astype(BF16), _block_diag(vb, kh)).astype(BF16)
                o_ref[rows, (2 * kh) * 128:(2 * kh + 1) * 128] = o2[:BLK]
                o_ref[rows, (2 * kh + 1) * 128:(2 * kh + 2) * 128] = o2[BLK:]
            ext = pall[band]
            var = _variant_index(n, nb)
            for gi in range(POOL_G):
                gsl = slice(gi * POOL_GW, (gi + 1) * POOL_GW)
                dg = _dot_nn(pb_ref[var, gi], ext[:, gsl])
                yg = _dot_nn(dg.astype(BF16), pw_ref[gi].astype(BF16))
                o_ref[rows, ATTN_W + gi * POOL_GW:ATTN_W + (gi + 1) * POOL_GW] = (yg * ps_ref[:, gsl]).astype(BF16)
        mix = o_ref[...]
        o_t_ref[...] = mix.T
        h_out_ref[...] = h_ref[...] + _dot_nn(mix, wo_ref[...])

    mix_w = ATTN_W + POOL_W
    return _call(
        body, name, (nb // nq,),
        [pl.BlockSpec(memory_space=pltpu.SMEM), pl.BlockSpec((nq * BLK, ATTN_W), lambda m: (m, 0)),
         *_band_specs(nq, nb, KV_W), *_band_specs(nq, nb, KV_W), *_band_specs(nq, nb, POOL_W),
         _whole((POOL_G, POOL_GW, POOL_GW)), _whole((1, POOL_W)), _whole(pband.shape),
         pl.BlockSpec((nq * BLK, d), lambda m: (m, 0)), _whole((mix_w, d))],
        [pl.BlockSpec((nq * BLK, d), lambda m: (m, 0)), pl.BlockSpec((mix_w, nq * BLK), lambda m: (0, m))],
        [jax.ShapeDtypeStruct((s, d), F32), jax.ShapeDtypeStruct((mix_w, s), BF16)],
        (sink, q, k, k, k, v, v, v, pc, pc, pc, pool_w, pool_scale, pband, h, w_out), ("arbitrary",), carry, middle,
        scratch=(pltpu.VMEM((nq * BLK, mix_w), BF16),))


def _attn_pool_bwd(q, k, v, pc, dmix, sink, pool_w, pool_scale, pband, ptband, name, carry=None):
    s = q.shape[0]
    nb = s // BLK
    nq = _attn_blocks_per_step(nb)

    def body(sink_ref, q_ref, k0, k1, k2, v0, v1, v2, p0, p1, p2, da_ref, d0, d1, d2, pw_ref, ps_ref, pb_ref, ptb_ref,
             dq_ref, dk_ref, dv_ref, dpc_ref, dsink_ref, dpw_ref, dps_ref):
        @pl.when(pl.program_id(0) == 0)
        def _():
            dsink_ref[...] = jnp.zeros_like(dsink_ref)
            dpw_ref[...] = jnp.zeros_like(dpw_ref)
            dps_ref[...] = jnp.zeros_like(dps_ref)

        kall = jnp.concatenate([k0[...], k1[...], k2[...]], axis=0)
        vall = jnp.concatenate([v0[...], v1[...], v2[...]], axis=0)
        pall = jnp.concatenate([p0[...], p1[...], p2[...]], axis=0).astype(BF16)
        dpall = jnp.concatenate([d0[...], d1[...], d2[...]], axis=0)
        lo = lax.broadcasted_iota(jnp.int32, (3 * BLK, KV_W), 1) < HEAD_DIM
        for j in range(nq):
            n = pl.program_id(0) * nq + j
            rows, band = slice(j * BLK, (j + 1) * BLK), slice(j * BLK, (j + 3) * BLK)
            valid = _band_mask(n, nb, True)
            kb, vb, qb = kall[band], vall[band], q_ref[rows, :]
            qs = qb * SCORE_SCALE
            da = da_ref[rows, :].astype(BF16)
            dk_fold, dv_fold = [], []
            for kh in range(N_KV):
                kbd, vbd = _block_diag(kb, kh), _block_diag(vb, kh)
                q2, do2 = _stack_pairs(qb, kh), _stack_pairs(da, kh)
                sc_t = _dot_nt(kbd, _stack_pairs(qs, kh))
                dp_t = _dot_nt(vbd, do2)
                p_t, ds_t = [], []
                for half in range(2):
                    keys = slice(half * 3 * BLK, (half + 1) * 3 * BLK)
                    p, ps = _softmax_sink(sc_t[keys], valid, _sink_of(sink_ref, kh, half, 1), 0)
                    delta = jnp.sum(p * dp_t[keys], axis=0, keepdims=True)
                    p_t.append(p.astype(BF16))
                    ds_t.append((p * (dp_t[keys] - delta)).astype(BF16))
                    dsk = -ps * delta
                    for pair in range(2):
                        h = GROUP * kh + 2 * pair + half
                        part = jnp.sum(dsk[:, pair * BLK:(pair + 1) * BLK], axis=1, keepdims=True)
                        dsink_ref[h:h + 1, :] += jnp.broadcast_to(part, (1, 128))
                p_t = jnp.concatenate(p_t, axis=0)
                ds_t = jnp.concatenate(ds_t, axis=0)
                dq2 = _dot_tn(ds_t, kbd) * SCORE_SCALE
                dq_ref[rows, (2 * kh) * 128:(2 * kh + 1) * 128] = dq2[:BLK]
                dq_ref[rows, (2 * kh + 1) * 128:(2 * kh + 2) * 128] = dq2[BLK:]
                dk_fold.append(_fold_diag(_dot_nn(ds_t, q2)) * SCORE_SCALE)
                dv_fold.append(_fold_diag(_dot_nn(p_t, do2)))
            dk_all = jnp.where(lo, dk_fold[0], dk_fold[1])
            dv_all = jnp.where(lo, dv_fold[0], dv_fold[1])
            for t in range(3):
                dk_ref[j, t] = dk_all[t * BLK:(t + 1) * BLK]
                dv_ref[j, t] = dv_all[t * BLK:(t + 1) * BLK]
            ext, dpe = pall[band], dpall[band]
            dpc_cur = dpall[(j + 1) * BLK:(j + 2) * BLK]
            var = _variant_index(n, nb)
            for gi in range(POOL_G):
                gsl = slice(gi * POOL_GW, (gi + 1) * POOL_GW)
                wg = pw_ref[gi].astype(BF16)
                sc = ps_ref[:, gsl]
                dgb = _dot_nn(pb_ref[var, gi], ext[:, gsl]).astype(BF16)
                yg = _dot_nn(dgb, wg)
                dps_ref[:, gsl] += jnp.sum(dpc_cur[:, gsl] * yg, axis=0, keepdims=True)
                dpw_ref[gi] += _dot_tn(dgb, (dpc_cur[:, gsl] * sc).astype(BF16))
                dd = _dot_nt((dpe[:, gsl] * sc).astype(BF16), wg)
                dpc_ref[rows, gsl] = _dot_nn(ptb_ref[var, gi], dd.astype(BF16))

    fixed = lambda shape: pl.BlockSpec(shape, lambda m: (0,) * len(shape))
    return _call(
        body, name, (nb // nq,),
        [pl.BlockSpec(memory_space=pltpu.SMEM), pl.BlockSpec((nq * BLK, ATTN_W), lambda m: (m, 0)),
         *_band_specs(nq, nb, KV_W), *_band_specs(nq, nb, KV_W), *_band_specs(nq, nb, POOL_W),
         pl.BlockSpec((nq * BLK, ATTN_W), lambda m: (m, 0)), *_band_specs(nq, nb, POOL_W, 1),
         _whole((POOL_G, POOL_GW, POOL_GW)), _whole((1, POOL_W)), _whole(pband.shape), _whole(ptband.shape)],
        [pl.BlockSpec((nq * BLK, ATTN_W), lambda m: (m, 0)),
         pl.BlockSpec((nq, 3, BLK, KV_W), lambda m: (m, 0, 0, 0)),
         pl.BlockSpec((nq, 3, BLK, KV_W), lambda m: (m, 0, 0, 0)),
         pl.BlockSpec((nq * BLK, POOL_W), lambda m: (m, 0)),
         fixed((N_HEADS, 128)), fixed((POOL_G, POOL_GW, POOL_GW)), fixed((1, POOL_W))],
        [jax.ShapeDtypeStruct((s, ATTN_W), F32), jax.ShapeDtypeStruct((nb, 3, BLK, KV_W), F32),
         jax.ShapeDtypeStruct((nb, 3, BLK, KV_W), F32), jax.ShapeDtypeStruct((s, POOL_W), F32),
         jax.ShapeDtypeStruct((N_HEADS, 128), F32),
         jax.ShapeDtypeStruct((POOL_G, POOL_GW, POOL_GW), F32), jax.ShapeDtypeStruct((1, POOL_W), F32)],
        (sink, q, k, k, k, v, v, v, pc, pc, pc, dmix, dmix, dmix, dmix, pool_w, pool_scale, pband, ptband),
        ("arbitrary",), carry)


def _mix_in_bwd(h, dh, g, win_t, dq, dkp, dvp, dpc, tabs, name, tm, carry=None):
    s, d = h.shape
    nb = s // BLK
    nt = tm // BLK
    n_in = win_t.shape[0]

    def band_sum(n, before, own, after, prev_last, next_first):
        lo = (n > 0).astype(F32)
        hi = (n < s // tm - 1).astype(F32)
        blocks = []
        for b in range(nt):
            from_prev = prev_last[...] * lo if b == 0 else before[b - 1]
            from_next = next_first[...] * hi if b == nt - 1 else after[b + 1]
            blocks.append(from_prev + own[b] + from_next)
        return jnp.concatenate(blocks, axis=0)

    def body(h_ref, dh_ref, g_ref, w_ref, dq_ref, k2, k1, k0, kp, kn, v2, v1, v0, vp, vn, dpc_ref, c_ref, sa_ref,
             sb_ref, o_ref, du_ref, dg_ref):
        n = pl.program_id(0)
        dk = band_sum(n, k2, k1, k0, kp, kn)
        dv = band_sum(n, v2, v1, v0, vp, vn)
        c, sa, sb = c_ref[...], sa_ref[...], sb_ref[...]
        du = jnp.concatenate([_rope_bwd(dq_ref[...], c, sa, sb, ATTN_W // 128), _rope_bwd(dk, c, sa, sb, 1), dv,
                              dpc_ref[...]], axis=1)
        du_ref[...] = du.T.astype(BF16)
        dn = _dot_nn(du.astype(BF16), w_ref[...])
        x = h_ref[...]
        r = lax.rsqrt(jnp.mean(x * x, axis=-1, keepdims=True) + EPS)
        xhat = x * r
        dxhat = dn * g_ref[...]
        o_ref[...] = dh_ref[...] + r * (dxhat - xhat * jnp.mean(dxhat * xhat, axis=-1, keepdims=True))

        @pl.when(n == 0)
        def _():
            dg_ref[...] = jnp.zeros_like(dg_ref)

        dg_ref[...] += jnp.sum(dn * xhat, axis=0, keepdims=True)

    row = lambda w: pl.BlockSpec((tm, w), lambda n: (n, 0))
    slot = lambda t: pl.BlockSpec((nt, None, BLK, KV_W), lambda n, t=t: (n, t, 0, 0))
    parts = [slot(2), slot(1), slot(0),
             pl.BlockSpec((None, None, BLK, KV_W), lambda n: (jnp.maximum(nt * n - 1, 0), 2, 0, 0)),
             pl.BlockSpec((None, None, BLK, KV_W), lambda n: (jnp.minimum(nt * n + nt, nb - 1), 0, 0, 0))]
    return _call(
        body, name, (s // tm,),
        [row(d), row(d), _whole((1, d)), _whole((n_in, d)), row(ATTN_W), *parts, *parts, row(POOL_W),
         row(128), row(128), row(128)],
        [row(d), pl.BlockSpec((n_in, tm), lambda n: (0, n)), pl.BlockSpec((1, d), lambda n: (0, 0))],
        [jax.ShapeDtypeStruct((s, d), F32), jax.ShapeDtypeStruct((n_in, s), BF16), jax.ShapeDtypeStruct((1, d), F32)],
        (h, dh, g, win_t, dq, *[dkp] * 5, *[dvp] * 5, dpc, *tabs), ("arbitrary",), carry)


def _adam_math(w, g, m, v):
    m = ADAM_B1 * m + (1.0 - ADAM_B1) * g
    v = ADAM_B2 * v + (1.0 - ADAM_B2) * (g * g)
    m_hat = m / (1.0 - ADAM_B1 ** ADAM_STEP)
    v_hat = v / (1.0 - ADAM_B2 ** ADAM_STEP)
    delta = -ADAM_LR * (m_hat / (jnp.sqrt(v_hat) + ADAM_EPS) + ADAM_WD * w)
    return delta, m, v


def _adam_small(w, parts, late, m, v, name):
    rows, cols = w.shape

    def body(w_ref, p_ref, l_ref, m_ref, v_ref, g_ref, d_ref, nm_ref, nv_ref):
        g, gl = p_ref[0], l_ref[0]
        for k in range(1, N_DEV):
            g = g + p_ref[k]
            gl = gl + l_ref[k]
        g_ref[...] = g
        g_ref[SMALL_NORM1:SMALL_NORM1 + 8, :] = g[SMALL_NORM1:SMALL_NORM1 + 8] + gl
        d_ref[...], nm_ref[...], nv_ref[...] = _adam_math(w_ref[...], g_ref[...], m_ref[...], v_ref[...])

    return pl.pallas_call(
        body, name=name, out_shape=[jax.ShapeDtypeStruct((rows, cols), F32)] * 4,
    )(w, parts, late, m, v)


SMALL_NORM1 = 512


def _pack_small(norm1, normm, norm2, normf, sink, pool_w, pool_scale, loss=None):
    scale_rows = jnp.pad(pool_scale.reshape(4, 128), ((0, 4), (0, 0)))
    last_rows = jnp.pad(sink.reshape(1, N_HEADS), ((0, 7), (0, 128 - N_HEADS)))
    if loss is not None:
        last_rows = last_rows + jnp.pad(loss.reshape(1, 1), ((1, 6), (0, 127)))
    return jnp.concatenate([pool_w.reshape(512, 128), norm1.reshape(8, 128), normm.reshape(8, 128),
                            norm2.reshape(8, 128), normf.reshape(8, 128), scale_rows, last_rows], axis=0)


def _unpack_small(p):
    return dict(pool_w=p[:512].reshape(1, POOL_G, POOL_GW, POOL_GW), ffn1_norm=p[512:520].reshape(1, 1024),
                mix_norm=p[520:528].reshape(1, 1024), ffn2_norm=p[528:536].reshape(1, 1024),
                final_norm=p[536:544].reshape(1024), pool_scale=p[544:548].reshape(1, POOL_W),
                sink_logits=p[552, :N_HEADS].reshape(1, N_HEADS), loss=p[553, 0])


def kernel(x, ffn1_norm, ffn1_w_gate, ffn1_w_up, ffn1_w_down, mix_norm, w_in, sink_logits, pool_w, pool_scale, w_out, ffn2_norm, ffn2_w_gate, ffn2_w_up, ffn2_w_down, final_norm, loss_target, m_ffn1_norm, m_ffn1_w_gate, m_ffn1_w_up, m_ffn1_w_down, m_mix_norm, m_w_in, m_sink_logits, m_pool_w, m_pool_scale, m_w_out, m_ffn2_norm, m_ffn2_w_gate, m_ffn2_w_up, m_ffn2_w_down, m_final_norm, v_ffn1_norm, v_ffn1_w_gate, v_ffn1_w_up, v_ffn1_w_down, v_mix_norm, v_w_in, v_sink_logits, v_pool_w, v_pool_scale, v_w_out, v_ffn2_norm, v_ffn2_w_gate, v_ffn2_w_up, v_ffn2_w_down, v_final_norm):
    s, d = x.shape[1], x.shape[2]
    tm = min(512, s)
    tm_bwd = min(256, s)
    pos = jnp.stack([lax.axis_index("x"), lax.axis_index("y"), lax.axis_index("c")]).astype(jnp.int32)

    t_bf = lambda w: w[0].T.astype(BF16)
    full = lambda a: a.reshape(N_DEV * a.shape[1], d)
    (wg1,) = map(full, _run_exchange(_AllGather([t_bf(ffn1_w_gate)]), "gather_ffn1_gate"))

    tabs = _rope_tables(s)
    pband, ptband = _pool_tables(s)
    g1, gm, g2, gf = ffn1_norm, mix_norm, ffn2_norm, final_norm.reshape(1, d)

    x0 = x[0]
    n1, gate1, wu1 = _ffn_gate(x0, g1, wg1, "ffn1_gate", tm, carry=_AllGather([t_bf(ffn1_w_up)]))
    up1, act1_t, wd1, win_t = _ffn_up(n1, gate1, full(wu1), "ffn1_up", tm,
                                            carry=_AllGather([ffn1_w_down[0].astype(BF16), t_bf(w_in)]))
    wu1, wd1, win_t = full(wu1), full(wd1), full(win_t)
    h1, q, k, v, pc, n2, wout, wg2 = _ffn_down_mix_in(
        x0, act1_t, wd1, gm, win_t, tabs, "ffn1_down_mix_in", tm,
        carry=_AllGather([w_out[0].astype(BF16), t_bf(ffn2_w_gate)]))
    wout, wg2 = full(wout), full(wg2)
    h2, mix_t, *gathered = _attn_pool_fwd(h1, wout, q, k, v, pc, sink_logits, pool_w[0], pool_scale, pband,
                                          "attn_pool_fwd",
                                          carry=_AllGather([t_bf(ffn2_w_up), ffn2_w_down[0].astype(BF16)]))
    wu2, wd2 = map(full, gathered)
    dh3, n3, gate2, up2, act2_t, loss_part, dgf = _ffn_fwd(h2, g2, wg2, wu2, wd2, "ffn2_fwd", tm,
                                                           head=(gf, loss_target[0]))

    sum1, recv2 = {}, {}

    def stage2(keys):
        return _RsStage2([sum1[key][1] for key in keys])

    dh2, dg2, dgate2_t, dup2_t, dhh3, dmix, dh2b = _ffn_bwd(h2, dh3, gate2, up2, g2, wg2, wu2, wd2, wout, "ffn2_bwd",
                                                            tm_bwd)
    both = _wgrad_rs1(pos, [dgate2_t, dup2_t], n3, "wgrad_gate_up2")
    sum1["g2"], sum1["u2"] = both[:2], both[2:]
    sum1["d2"] = _wgrad_rs1(pos, [act2_t], dhh3, "wgrad_down2")
    sum1["out"] = _wgrad_rs1(pos, [mix_t], dh2b, "wgrad_out")
    dq, dkp, dvp, dpc, dsink, dpw, dps, *r2 = _attn_pool_bwd(
        q, k, v, pc, dmix, sink_logits, pool_w[0], pool_scale, pband, ptband, "attn_pool_bwd",
        carry=stage2(["g2", "u2", "d2"]))
    recv2.update(zip(["g2", "u2", "d2"], r2))
    dh1, du_t, dgm, recv2["out"] = _mix_in_bwd(h1, dh2, gm, win_t, dq, dkp, dvp, dpc, tabs, "mix_in_bwd", tm,
                                               carry=stage2(["out"]))
    sum1["in"] = _wgrad_rs1(pos, [du_t], n2, "wgrad_in")
    small_part = _pack_small(jnp.zeros_like(dgm), dgm, dg2, dgf, dsink[:, 0], dpw, dps, loss_part)
    dgate1_t, dup1_t, dhh1, recv2["in"] = _ffn_bwd_gates(dh1, gate1, up1, wd1, "ffn1_bwd_gates", tm,
                                                         carry=stage2(["in"]))
    *sum1["g1"], small_all = _wgrad_rs1(pos, [dgate1_t], n1, "wgrad_gate1", carry=_AllGather([small_part]))
    *sum1["u1"], recv2["g1"] = _wgrad_rs1(pos, [dup1_t], n1, "wgrad_up1", carry=stage2(["g1"]))
    *sum1["d1"], recv2["u1"] = _wgrad_rs1(pos, [act1_t], dhh1, "wgrad_down1", carry=stage2(["u1"]))
    dx, dg1, recv2["d1"] = _ffn_bwd_input(x0, dh1, dgate1_t, dup1_t, g1, wg1, wu1, "ffn1_bwd_input", tm,
                                          carry=stage2(["d1"]))

    (dg1_all,) = _run_exchange(_DirectGather([dg1.reshape(8, 128)]), "gather_norm1_grad")
    pk = lambda a, b, c_, e, s_, pw_, psc: _pack_small(a, b, c_, e, s_[0], pw_[0], psc)
    small_w = pk(ffn1_norm, mix_norm, ffn2_norm, final_norm, sink_logits, pool_w, pool_scale)
    small_m = pk(m_ffn1_norm, m_mix_norm, m_ffn2_norm, m_final_norm, m_sink_logits, m_pool_w, m_pool_scale)
    small_v = pk(v_ffn1_norm, v_mix_norm, v_ffn2_norm, v_final_norm, v_sink_logits, v_pool_w, v_pool_scale)
    sg, sd, sm, sv = [_unpack_small(a)
                      for a in _adam_small(small_w, small_all, dg1_all, small_m, small_v, "adam_small")]

    big = {}
    keys = ["g1", "u1", "d1", "g2", "u2", "d2", "in", "out"]
    names = ["ffn1_w_gate", "ffn1_w_up", "ffn1_w_down", "ffn2_w_gate", "ffn2_w_up", "ffn2_w_down", "w_in", "w_out"]
    transposed = [True, True, False, True, True, False, True, False]
    ws = [ffn1_w_gate, ffn1_w_up, ffn1_w_down, ffn2_w_gate, ffn2_w_up, ffn2_w_down, w_in, w_out]
    ms = [m_ffn1_w_gate, m_ffn1_w_up, m_ffn1_w_down, m_ffn2_w_gate, m_ffn2_w_up, m_ffn2_w_down, m_w_in, m_w_out]
    vs = [v_ffn1_w_gate, v_ffn1_w_up, v_ffn1_w_down, v_ffn2_w_gate, v_ffn2_w_up, v_ffn2_w_down, v_w_in, v_w_out]
    for key, nm, tr, w, m, vv in zip(keys, names, transposed, ws, ms, vs):
        view = (lambda a: jnp.swapaxes(a, 1, 2)[0]) if tr else (lambda a: a[0])
        back = (lambda a: jnp.swapaxes(a[None], 1, 2)) if tr else (lambda a: a[None])
        res = _rs_sum2_adam(pos, sum1[key][0], recv2[key], view(w), view(m), view(vv), "adam_" + nm)
        big[nm] = tuple(back(a) for a in res)

    loss = sg["loss"]
    all_names = ["ffn1_norm", "ffn1_w_gate", "ffn1_w_up", "ffn1_w_down", "mix_norm", "w_in", "sink_logits", "pool_w",
                 "pool_scale", "w_out", "ffn2_norm", "ffn2_w_gate", "ffn2_w_up", "ffn2_w_down", "final_norm"]
    outs = [loss, dx[None]]
    for idx, src in enumerate((sg, sd, sm, sv)):
        for nm in all_names:
            outs.append(big[nm][idx] if nm in big else src[nm])
    return tuple(outs)
```

```python
import functools

import jax
import jax.numpy as jnp
import numpy as np
from jax import lax
from jax.experimental import pallas as pl
from jax.experimental.pallas import tpu as pltpu

F32 = jnp.float32
BF16 = jnp.bfloat16
MESH = pl.DeviceIdType.MESH
N_DEV = 8

EPS = 1e-6
HEAD_DIM = 64
N_HEADS = 8
N_KV = 2
GROUP = N_HEADS // N_KV
ATTN_W = N_HEADS * HEAD_DIM
KV_W = N_KV * HEAD_DIM
POOL_W = 512
POOL_G = 4
POOL_GW = POOL_W // POOL_G
POOL_WINDOWS = (2, 4, 8, 16)
BLK = 128
ROT = 16
ROPE_THETA = 500000.0
SCORE_SCALE = HEAD_DIM ** -0.5

ADAM_LR, ADAM_B1, ADAM_B2, ADAM_EPS, ADAM_WD, ADAM_STEP = 0.001, 0.9, 0.999, 1e-08, 0.01, 10

VMEM_LIMIT = 56 * 1024 * 1024


def _cparams(sem=None, **kw):
    if sem is not None:
        kw["dimension_semantics"] = sem
    return pltpu.CompilerParams(vmem_limit_bytes=VMEM_LIMIT, **kw)


def _whole(shape):
    nd = len(shape)
    return pl.BlockSpec(shape, lambda *_: (0,) * nd, pipeline_mode=pl.Buffered(1))


def _sigmoid(z):
    return 1.0 / (1.0 + jnp.exp(-z))


def _dot_nt(a, b):
    return lax.dot_general(a, b, (((1,), (1,)), ((), ())), preferred_element_type=F32)


def _dot_nn(a, b):
    return lax.dot_general(a, b, (((1,), (0,)), ((), ())), preferred_element_type=F32)


def _dot_tn(a, b):
    return lax.dot_general(a, b, (((0,), (0,)), ((), ())), preferred_element_type=F32)


def _rope_tables(s):
    inv_freq = ROPE_THETA ** (-np.arange(0, ROT, 2, dtype=np.float64) / ROT)
    ang = np.arange(s, dtype=np.float64)[:, None] * inv_freq[None, :]
    c = np.ones((s, HEAD_DIM)); sa = np.zeros((s, HEAD_DIM)); sb = np.zeros((s, HEAD_DIM))
    c[:, :8] = np.cos(ang); c[:, 8:16] = np.cos(ang)
    sa[:, :8] = -np.sin(ang)
    sb[:, 8:16] = np.sin(ang)
    t = lambda a: jnp.asarray(np.tile(a, (1, 2)).astype(np.float32))
    return t(c), t(sa), t(sb)


def _pool_weight(gi, t, s_pos, s):
    half = POOL_WINDOWS[gi] // 2

    def win(lo, hi):
        a = np.clip(lo, 0, s); b = np.clip(hi + 1, 0, s)
        inside = (s_pos >= a) & (s_pos < b)
        return inside / np.maximum(b - a, 1)

    w = 0.5 * (win(t - half, t + half - 1) + win(t - half + 1, t + half)) - (t == s_pos)
    return w * ((t >= 0) & (t < s) & (s_pos >= 0) & (s_pos < s))


def _pool_tables(s):
    nb = s // BLK
    fwd = np.zeros((3, POOL_G, BLK, 3 * BLK), np.float32)
    bwd = np.zeros((3, POOL_G, BLK, 3 * BLK), np.float32)
    for vi, n in enumerate((0, 1 if nb > 2 else 0, nb - 1)):
        i = n * BLK + np.arange(BLK)[:, None]
        j = (n - 1) * BLK + np.arange(3 * BLK)[None, :]
        for gi in range(POOL_G):
            fwd[vi, gi] = _pool_weight(gi, i, j, s)
            bwd[vi, gi] = _pool_weight(gi, j, i, s)
    return jnp.asarray(fwd, dtype=BF16), jnp.asarray(bwd, dtype=BF16)


def _variant_index(n, nb):
    return jnp.where(n == 0, 0, jnp.where(n == nb - 1, 2, 1))


class _Exchange:
    inputs = ()
    out_shapes = ()
    sems = ()

    def start(self, srcs, outs, sems):
        raise NotImplementedError

    def middle(self, srcs, outs, sems):
        pass

    def finish(self, srcs, outs, sems):
        raise NotImplementedError


class _AllGather(_Exchange):
    def __init__(self, arrays):
        n = len(arrays)
        self.inputs = list(arrays)
        self.out_shapes = [jax.ShapeDtypeStruct((N_DEV,) + a.shape, a.dtype) for a in arrays]
        self.sems = [pltpu.SemaphoreType.DMA((n, 8)), pltpu.SemaphoreType.DMA((n, 8)), pltpu.SemaphoreType.DMA((n,))]

    def _parts(self, srcs, outs, sems):
        send_sems, recv_sems, local_sems = sems
        n = len(srcs)
        x, y, c = lax.axis_index("x"), lax.axis_index("y"), lax.axis_index("c")
        me, sibling, xn, yn, diag = (x, y, c), (x, y, 1 - c), (1 - x, y, c), (x, 1 - y, c), (1 - x, 1 - y, c)

        def place(a, dev, half=None):
            block = outs[a].at[4 * dev[0] + 2 * dev[1] + dev[2]]
            if half is None:
                return block
            r2 = outs[a].shape[1] // 2
            return block.at[pl.ds(half * r2, r2)]

        def copy(a, k, dev, to, half=None, src=None):
            where = place(a, dev, half)
            return pltpu.make_async_remote_copy(
                src_ref=where if src is None else src, dst_ref=where, send_sem=send_sems.at[a, k],
                recv_sem=recv_sems.at[a, k], device_id=to, device_id_type=MESH)

        def other(dev):
            return (dev[0], dev[1], 1 - dev[2])

        class Parts:
            mine = staticmethod(lambda: [pltpu.make_async_copy(srcs[a], place(a, me), local_sems.at[a])
                                         for a in range(n)])
            own = staticmethod(lambda: [copy(a, k, me, to, src=srcs[a]) for a in range(n)
                                        for k, to in ((0, sibling), (1, xn), (2, yn))])
            relay = staticmethod(lambda a: [copy(a, 3, xn, yn, half=0), copy(a, 4, yn, xn, half=1),
                                            copy(a, 5, xn, sibling), copy(a, 6, yn, sibling)])
            last = staticmethod(lambda a: copy(a, 7, diag, sibling))
            from_x = staticmethod(lambda a: copy(a, 1, xn, me))
            from_y = staticmethod(lambda a: copy(a, 2, yn, me))
            diag_halves = staticmethod(lambda a: [copy(a, 3, diag, me, half=0), copy(a, 4, diag, me, half=1)])
            from_sibling = staticmethod(lambda a: [copy(a, 0, sibling, me), copy(a, 5, other(xn), me),
                                                   copy(a, 6, other(yn), me), copy(a, 7, other(diag), me)])

        return n, Parts

    def start(self, srcs, outs, sems):
        _, p = self._parts(srcs, outs, sems)
        for cp in p.mine() + p.own():
            cp.start()

    def middle(self, srcs, outs, sems):
        n, p = self._parts(srcs, outs, sems)
        for a in range(n):
            p.from_x(a).wait_recv()
            p.from_y(a).wait_recv()
            for cp in p.relay(a):
                cp.start()

    def finish(self, srcs, outs, sems):
        n, p = self._parts(srcs, outs, sems)
        for a in range(n):
            for cp in p.diag_halves(a):
                cp.wait_recv()
            p.last(a).start()
        for a in range(n):
            for cp in p.from_sibling(a):
                cp.wait_recv()
        for cp in p.own() + [cp for a in range(n) for cp in p.relay(a) + [p.last(a)]]:
            cp.wait_send()
        for cp in p.mine():
            cp.wait()


class _RsStage2(_Exchange):
    def start(self, srcs, outs, sems):
        for cp in self._copies(srcs, outs, sems):
            cp.start()

    def finish(self, srcs, outs, sems):
        copies = self._copies(srcs, outs, sems)
        for cp in copies:
            cp.wait_recv()
        for cp in copies:
            cp.wait_send()


    def __init__(self, pbs):
        n = len(pbs)
        self.inputs = list(pbs)
        self.out_shapes = [jax.ShapeDtypeStruct((3,) + p.shape[1:], p.dtype) for p in pbs]
        self.sems = [pltpu.SemaphoreType.DMA((n, 3)), pltpu.SemaphoreType.DMA((n, 3))]

    def _copies(self, srcs, outs, sems):
        send_sems, recv_sems = sems
        x, y, c = lax.axis_index("x"), lax.axis_index("y"), lax.axis_index("c")
        chips = [(1 - x, y), (x, 1 - y), (1 - x, 1 - y)]
        return [pltpu.make_async_remote_copy(
            src_ref=srcs[a].at[2 * chip[0] + chip[1]], dst_ref=outs[a].at[j], send_sem=send_sems.at[a, j],
            recv_sem=recv_sems.at[a, j], device_id=(*chip, c), device_id_type=MESH)
            for a in range(len(srcs)) for j, chip in enumerate(chips)]


class _DirectGather(_Exchange):
    def __init__(self, arrays):
        n = len(arrays)
        self.inputs = list(arrays)
        self.out_shapes = [jax.ShapeDtypeStruct((N_DEV,) + a.shape, a.dtype) for a in arrays]
        self.sems = [pltpu.SemaphoreType.DMA((n, 7)), pltpu.SemaphoreType.DMA((n, 7)), pltpu.SemaphoreType.DMA((n,))]

    def _copies(self, srcs, outs, sems):
        send_sems, recv_sems, local_sems = sems
        x, y, c = lax.axis_index("x"), lax.axis_index("y"), lax.axis_index("c")
        me = 4 * x + 2 * y + c
        remote, local = [], []
        for a in range(len(srcs)):
            local.append(pltpu.make_async_copy(srcs[a], outs[a].at[me], local_sems.at[a]))
            for k in range(1, N_DEV):
                peer = (x ^ (k >> 2), y ^ ((k >> 1) & 1), c ^ (k & 1))
                remote.append(pltpu.make_async_remote_copy(
                    src_ref=srcs[a], dst_ref=outs[a].at[me], send_sem=send_sems.at[a, k - 1],
                    recv_sem=recv_sems.at[a, k - 1], device_id=peer, device_id_type=MESH))
        return remote, local

    def start(self, srcs, outs, sems):
        remote, local = self._copies(srcs, outs, sems)
        for cp in local + remote:
            cp.start()

    def finish(self, srcs, outs, sems):
        remote, local = self._copies(srcs, outs, sems)
        for cp in remote:
            cp.wait_recv()
        for cp in remote:
            cp.wait_send()
        for cp in local:
            cp.wait()


class _Both(_Exchange):
    def __init__(self, a, b):
        self.a, self.b = a, b
        self.inputs = list(a.inputs) + list(b.inputs)
        self.out_shapes = list(a.out_shapes) + list(b.out_shapes)
        self.sems = list(a.sems) + list(b.sems)

    def _split(self, srcs, outs, sems):
        na, oa, sa = len(self.a.inputs), len(self.a.out_shapes), len(self.a.sems)
        return (srcs[:na], outs[:oa], sems[:sa]), (srcs[na:], outs[oa:], sems[sa:])

    def start(self, srcs, outs, sems):
        pa, pb = self._split(srcs, outs, sems)
        self.a.start(*pa)
        self.b.start(*pb)

    def middle(self, srcs, outs, sems):
        pa, pb = self._split(srcs, outs, sems)
        self.a.middle(*pa)
        self.b.middle(*pb)

    def finish(self, srcs, outs, sems):
        pa, pb = self._split(srcs, outs, sems)
        self.a.finish(*pa)
        self.b.finish(*pb)


_ANY = pl.BlockSpec(memory_space=pl.ANY)


def _run_exchange(ex, name):
    n_in, n_out = len(ex.inputs), len(ex.out_shapes)

    def body(*refs):
        srcs, outs, sems = refs[:n_in], refs[n_in:n_in + n_out], refs[n_in + n_out:]
        ex.start(srcs, outs, sems)
        ex.middle(srcs, outs, sems)
        ex.finish(srcs, outs, sems)

    return pl.pallas_call(
        body, name=name, out_shape=list(ex.out_shapes), in_specs=[_ANY] * n_in, out_specs=[_ANY] * n_out,
        scratch_shapes=list(ex.sems),
    )(*ex.inputs)


CARRY_MIDDLE = 0.7


def _call(body, name, grid, in_specs, out_specs, out_shape, args, sem, carry=None, middle=CARRY_MIDDLE, scratch=()):
    if carry is None:
        return pl.pallas_call(functools.partial(body), name=name, grid=grid, in_specs=in_specs, out_specs=out_specs,
                              out_shape=out_shape, scratch_shapes=list(scratch), compiler_params=_cparams(sem))(*args)
    n_in, n_out = len(in_specs), len(out_specs)
    nc_in, nc_out = len(carry.inputs), len(carry.out_shapes)

    def carried(*refs):
        ins = refs[:n_in]
        c_in = refs[n_in:n_in + nc_in]
        outs = refs[n_in + nc_in:n_in + nc_in + n_out]
        c_out = refs[n_in + nc_in + n_out:n_in + nc_in + n_out + nc_out]
        own = refs[n_in + nc_in + n_out + nc_out:n_in + nc_in + n_out + nc_out + len(scratch)]
        sems = refs[n_in + nc_in + n_out + nc_out + len(scratch):]
        ids = [pl.program_id(i) for i in range(len(grid))]
        is_first = functools.reduce(jnp.logical_and, [i == 0 for i in ids])
        is_last = functools.reduce(jnp.logical_and, [i == g - 1 for i, g in zip(ids, grid)])
        @pl.when(is_first)
        def _():
            carry.start(c_in, c_out, sems)

        if middle is not None:
            @pl.when(functools.reduce(jnp.logical_and, [ids[0] == round(middle * (grid[0] - 1))]
                                      + [i == 0 for i in ids[1:]]))
            def _():
                carry.middle(c_in, c_out, sems)

        body(*ins, *outs, *own)

        @pl.when(is_last)
        def _():
            if middle is None:
                carry.middle(c_in, c_out, sems)
            carry.finish(c_in, c_out, sems)

    return pl.pallas_call(
        carried, name=name, grid=grid, in_specs=list(in_specs) + [_ANY] * nc_in,
        out_specs=list(out_specs) + [_ANY] * nc_out, out_shape=list(out_shape) + list(carry.out_shapes),
        scratch_shapes=list(scratch) + list(carry.sems), compiler_params=_cparams(sem))(*args, *carry.inputs)


def _rs_sum2_adam(pos, p, r2, w, m, v, name):
    _, rows, d = p.shape
    tr = rows // 2 if rows % 16 == 0 else rows

    def body(pos_ref, p_ref, r_ref, w_ref, m_ref, v_ref, g_ref, d_ref, nm_ref, nv_ref):
        r = r_ref[...].astype(F32)
        g = ((p_ref[...] + r[0]) + r[1]) + r[2]
        g_ref[...] = g
        d_ref[...], nm_ref[...], nv_ref[...] = _adam_math(w_ref[...], g, m_ref[...], v_ref[...])

    blk = pl.BlockSpec((tr, d), lambda i, pos_ref: (i, 0))
    return pl.pallas_call(
        body, name=name,
        grid_spec=pltpu.PrefetchScalarGridSpec(
            num_scalar_prefetch=1, grid=(rows // tr,),
            in_specs=[pl.BlockSpec((None, tr, d), lambda i, pos_ref: (2 * pos_ref[0] + pos_ref[1], i, 0)),
                      pl.BlockSpec((3, tr, d), lambda i, pos_ref: (0, i, 0)), blk, blk, blk],
            out_specs=[blk] * 4),
        out_shape=[jax.ShapeDtypeStruct((rows, d), F32)] * 4,
        compiler_params=_cparams(("arbitrary",)),
    )(pos, p, r2, w, m, v)


def _ffn_chunk(f):
    for cand in (256, 128):
        if f % cand == 0:
            return cand
    return f


def _loss_head(x, gg, target, loss_ref, dg_ref):
    @pl.when(pl.program_id(0) == 0)
    def _():
        loss_ref[...] = jnp.zeros_like(loss_ref)
        dg_ref[...] = jnp.zeros_like(dg_ref)

    r = lax.rsqrt(jnp.mean(x * x, axis=-1, keepdims=True) + EPS)
    xhat = x * r
    e = xhat * gg - target
    loss_ref[...] += 0.5 * jnp.sum(jnp.mean(e * e, axis=-1, keepdims=True), axis=0, keepdims=True)
    dy = e * (1.0 / x.shape[-1])
    dg_ref[...] += jnp.sum(dy * xhat, axis=0, keepdims=True)
    dxhat = dy * gg
    return r * (dxhat - xhat * jnp.mean(dxhat * xhat, axis=-1, keepdims=True))


def _ffn_gate(h, g, wg_t, name, tm, carry=None):
    s, d = h.shape
    f = wg_t.shape[0]

    def body(h_ref, g_ref, wg_ref, n_ref, gate_ref):
        x = h_ref[...]
        r = lax.rsqrt(jnp.mean(x * x, axis=-1, keepdims=True) + EPS)
        nb = (x * r * g_ref[...]).astype(BF16)
        n_ref[...] = nb
        gate_ref[...] = _dot_nt(nb, wg_ref[...]).astype(BF16)

    row = lambda w: pl.BlockSpec((tm, w), lambda i: (i, 0))
    return _call(body, name, (s // tm,), [row(d), _whole((1, d)), _whole((f, d))], [row(d), row(f)],
                 [jax.ShapeDtypeStruct((s, d), BF16), jax.ShapeDtypeStruct((s, f), BF16)], (h, g, wg_t),
                 ("arbitrary",), carry)


def _ffn_up(n, gate, wu_t, name, tm, carry=None):
    s, d = n.shape
    f = wu_t.shape[0]
    tf = _ffn_chunk(f)

    def body(n_ref, gate_ref, wu_ref, up_ref, act_t_ref):
        nb = n_ref[...]
        for j in range(f // tf):
            sl = slice(j * tf, (j + 1) * tf)
            up = _dot_nt(nb, wu_ref[sl, :])
            gate = gate_ref[:, sl].astype(F32)
            up_ref[:, sl] = up.astype(BF16)
            act_t_ref[sl, :] = (gate * _sigmoid(gate) * up).astype(BF16).T

    row = lambda w: pl.BlockSpec((tm, w), lambda i: (i, 0))
    return _call(body, name, (s // tm,), [row(d), row(f), _whole((f, d))],
                 [row(f), pl.BlockSpec((f, tm), lambda i: (0, i))],
                 [jax.ShapeDtypeStruct((s, f), BF16), jax.ShapeDtypeStruct((f, s), BF16)], (n, gate, wu_t),
                 ("arbitrary",), carry)


def _ffn_fwd(h, g, wg_t, wu_t, wd, name, tm, carry=None, head=None, middle=CARRY_MIDDLE):
    s, d = h.shape
    f = wg_t.shape[0]
    tf = _ffn_chunk(f)

    def body(h_ref, g_ref, wg_ref, wu_ref, wd_ref, *refs):
        if head is None:
            o_ref, n_ref, gate_ref, up_ref, act_t_ref = refs
        else:
            gf_ref, t_ref, o_ref, n_ref, gate_ref, up_ref, act_t_ref, loss_ref, dgf_ref = refs
        x = h_ref[...]
        r = lax.rsqrt(jnp.mean(x * x, axis=-1, keepdims=True) + EPS)
        nb = (x * r * g_ref[...]).astype(BF16)
        n_ref[...] = nb
        for j in range(f // tf):
            sl = slice(j * tf, (j + 1) * tf)
            gate = _dot_nt(nb, wg_ref[sl, :])
            up = _dot_nt(nb, wu_ref[sl, :])
            gate_ref[:, sl] = gate.astype(BF16)
            up_ref[:, sl] = up.astype(BF16)
            act_t_ref[sl, :] = (gate * _sigmoid(gate) * up).astype(BF16).T
        h_out = x + 0.5 * _dot_tn(act_t_ref[...], wd_ref[...])
        o_ref[...] = h_out if head is None else _loss_head(h_out, gf_ref[...], t_ref[...], loss_ref, dgf_ref)

    row = lambda w: pl.BlockSpec((tm, w), lambda i: (i, 0))
    in_specs = [row(d), _whole((1, d)), _whole((f, d)), _whole((f, d)), _whole((f, d))]
    out_specs = [row(d), row(d), row(f), row(f), pl.BlockSpec((f, tm), lambda i: (0, i))]
    out_shape = ([jax.ShapeDtypeStruct((s, d), F32), jax.ShapeDtypeStruct((s, d), BF16)]
                 + [jax.ShapeDtypeStruct((s, f), BF16)] * 2 + [jax.ShapeDtypeStruct((f, s), BF16)])
    args = (h, g, wg_t, wu_t, wd)
    if head is not None:
        in_specs += [_whole((1, d)), row(d)]
        out_specs += [pl.BlockSpec((1, 1), lambda i: (0, 0)), pl.BlockSpec((1, d), lambda i: (0, 0))]
        out_shape += [jax.ShapeDtypeStruct((1, 1), F32), jax.ShapeDtypeStruct((1, d), F32)]
        args += tuple(head)
    return _call(body, name, (s // tm,), in_specs, out_specs, out_shape, args, ("arbitrary",), carry, middle)


def _gate_grads(dh_ref, gate_ref, up_ref, wd_ref, dgate_t_ref, dup_t_ref, dhh_ref, tf):
    dhh = (0.5 * dh_ref[...]).astype(BF16)
    dhh_ref[...] = dhh
    for j in range(gate_ref.shape[1] // tf):
        sl = slice(j * tf, (j + 1) * tf)
        gt = gate_ref[:, sl].astype(F32)
        u = up_ref[:, sl].astype(F32)
        dact = _dot_nt(dhh, wd_ref[sl, :])
        sg = _sigmoid(gt)
        dup = dact * (gt * sg)
        dgate = dact * u * (sg * (1.0 + gt * (1.0 - sg)))
        dup_t_ref[sl, :] = dup.astype(BF16).T
        dgate_t_ref[sl, :] = dgate.astype(BF16).T


def _input_grad(h_ref, dh_ref, dgate_t_ref, dup_t_ref, g_ref, wg_ref, wu_ref, o_ref, dg_ref):
    x = h_ref[...]
    r = lax.rsqrt(jnp.mean(x * x, axis=-1, keepdims=True) + EPS)
    xhat = x * r
    dn = _dot_tn(dgate_t_ref[...], wg_ref[...]) + _dot_tn(dup_t_ref[...], wu_ref[...])
    dxhat = dn * g_ref[...]
    o_ref[...] = dh_ref[...] + r * (dxhat - xhat * jnp.mean(dxhat * xhat, axis=-1, keepdims=True))

    @pl.when(pl.program_id(0) == 0)
    def _():
        dg_ref[...] = jnp.zeros_like(dg_ref)

    dg_ref[...] += jnp.sum(dn * xhat, axis=0, keepdims=True)


def _ffn_bwd(h_in, dh_out, gate, up, g, wg_t, wu_t, wd, w_out, name, tm):
    s, d = h_in.shape
    f = gate.shape[1]
    w = w_out.shape[0]
    tf = _ffn_chunk(f)

    def body(h_ref, dh_ref, gate_ref, up_ref, g_ref, wg_ref, wu_ref, wd_ref, wo_ref,
             o_ref, dg_ref, dgate_t_ref, dup_t_ref, dhh_ref, dmix_ref, dhb_ref):
        _gate_grads(dh_ref, gate_ref, up_ref, wd_ref, dgate_t_ref, dup_t_ref, dhh_ref, tf)
        _input_grad(h_ref, dh_ref, dgate_t_ref, dup_t_ref, g_ref, wg_ref, wu_ref, o_ref, dg_ref)
        dhb = o_ref[...].astype(BF16)
        dhb_ref[...] = dhb
        dmix_ref[...] = _dot_nt(dhb, wo_ref[...])

    row = lambda c: pl.BlockSpec((tm, c), lambda i: (i, 0))
    col = pl.BlockSpec((f, tm), lambda i: (0, i))
    return pl.pallas_call(
        body, name=name, grid=(s // tm,),
        in_specs=[row(d), row(d), row(f), row(f), _whole((1, d)), _whole((f, d)), _whole((f, d)), _whole((f, d)),
                  _whole((w, d))],
        out_specs=[row(d), pl.BlockSpec((1, d), lambda i: (0, 0)), col, col, row(d), row(w), row(d)],
        out_shape=[jax.ShapeDtypeStruct((s, d), F32), jax.ShapeDtypeStruct((1, d), F32),
                   jax.ShapeDtypeStruct((f, s), BF16), jax.ShapeDtypeStruct((f, s), BF16),
                   jax.ShapeDtypeStruct((s, d), BF16), jax.ShapeDtypeStruct((s, w), F32),
                   jax.ShapeDtypeStruct((s, d), BF16)],
        compiler_params=_cparams(("arbitrary",)),
    )(h_in, dh_out, gate, up, g, wg_t, wu_t, wd, w_out)


def _ffn_bwd_gates(dh_out, gate, up, wd, name, tm, carry=None):
    s, d = dh_out.shape
    f = gate.shape[1]
    tf = _ffn_chunk(f)

    def body(dh_ref, gate_ref, up_ref, wd_ref, dgate_t_ref, dup_t_ref, dhh_ref):
        _gate_grads(dh_ref, gate_ref, up_ref, wd_ref, dgate_t_ref, dup_t_ref, dhh_ref, tf)

    row = lambda w: pl.BlockSpec((tm, w), lambda i: (i, 0))
    col = pl.BlockSpec((f, tm), lambda i: (0, i))
    return _call(
        body, name, (s // tm,), [row(d), row(f), row(f), _whole((f, d))], [col, col, row(d)],
        [jax.ShapeDtypeStruct((f, s), BF16)] * 2 + [jax.ShapeDtypeStruct((s, d), BF16)],
        (dh_out, gate, up, wd), ("arbitrary",), carry)


def _ffn_bwd_input(h_in, dh_out, dgate_t, dup_t, g, wg_t, wu_t, name, tm, carry=None):
    s, d = h_in.shape
    f = dgate_t.shape[0]

    row = lambda w: pl.BlockSpec((tm, w), lambda i: (i, 0))
    col = pl.BlockSpec((f, tm), lambda i: (0, i))
    return _call(
        _input_grad, name, (s // tm,),
        [row(d), row(d), col, col, _whole((1, d)), _whole((f, d)), _whole((f, d))],
        [row(d), pl.BlockSpec((1, d), lambda i: (0, 0))],
        [jax.ShapeDtypeStruct((s, d), F32), jax.ShapeDtypeStruct((1, d), F32)],
        (h_in, dh_out, dgate_t, dup_t, g, wg_t, wu_t), ("arbitrary",), carry)


def _wgrad_rs1(pos, a_ts, b, name, carry=None):
    n_a = len(a_ts)
    f, s = a_ts[0].shape
    d = b.shape[1]
    fk = f // N_DEV
    nc_in = 0 if carry is None else len(carry.inputs)
    nc_out = 0 if carry is None else len(carry.out_shapes)

    def body(pos_ref, *refs):
        a_refs, b_ref, refs = refs[:n_a], refs[n_a], refs[n_a + 1:]
        c_in = refs[:nc_in]
        outs = refs[nc_in:nc_in + 2 * n_a]
        c_out = refs[nc_in + 2 * n_a:nc_in + 2 * n_a + nc_out]
        stage, land, send_sems, recv_sems = refs[nc_in + 2 * n_a + nc_out:nc_in + 2 * n_a + nc_out + 4]
        c_sems = refs[nc_in + 2 * n_a + nc_out + 4:]
        t = pl.program_id(0)
        which, step = t // 8, t % 8
        slot = 4 * which + step % 4
        x, y, c = lax.axis_index("x"), lax.axis_index("y"), lax.axis_index("c")

        def push(k):
            return pltpu.make_async_remote_copy(src_ref=stage.at[k], dst_ref=land.at[k], send_sem=send_sems.at[k],
                                                recv_sem=recv_sems.at[k], device_id=(x, y, 1 - c), device_id_type=MESH)

        if carry is not None:
            @pl.when(t == 0)
            def _():
                carry.start(c_in, c_out, c_sems)

        for j in range(n_a):
            @pl.when(which == j)
            def _(j=j):
                g = _dot_nn(a_refs[j][...], b_ref[...])

                @pl.when(step < 4)
                def _():
                    stage[slot] = g.astype(BF16)
                    push(slot).start()

                @pl.when(step >= 4)
                def _():
                    push(slot).wait_recv()
                    p = g + land[slot].astype(F32)
                    outs[2 * j][...] = p
                    outs[2 * j + 1][...] = p.astype(BF16)

        if carry is not None:
            @pl.when(t == round(CARRY_MIDDLE * (8 * n_a - 1)))
            def _():
                carry.middle(c_in, c_out, c_sems)

        @pl.when(t == 8 * n_a - 1)
        def _():
            for k in range(4 * n_a):
                push(k).wait_send()
            if carry is not None:
                carry.finish(c_in, c_out, c_sems)

    def shard(j):
        def index_map(t, pos_ref):
            step = jnp.clip(t - 8 * j, 0, 7)
            return 4 * ((step % 4) // 2) + 2 * (step % 2) + jnp.where(step < 4, 1 - pos_ref[2], pos_ref[2]), 0
        return index_map

    out = lambda j: pl.BlockSpec((None, fk, d), lambda t, pos_ref, j=j: (jnp.clip(t - 8 * j - 4, 0, 3), 0, 0))
    return pl.pallas_call(
        body, name=name,
        grid_spec=pltpu.PrefetchScalarGridSpec(
            num_scalar_prefetch=1, grid=(8 * n_a,),
            in_specs=[pl.BlockSpec((fk, s), shard(j)) for j in range(n_a)]
            + [pl.BlockSpec((s, d), lambda t, pos_ref: (0, 0), pipeline_mode=pl.Buffered(1))] + [_ANY] * nc_in,
            out_specs=[out(j) for j in range(n_a) for _ in range(2)] + [_ANY] * nc_out,
            scratch_shapes=[pltpu.VMEM((4 * n_a, fk, d), BF16), pltpu.VMEM((4 * n_a, fk, d), BF16),
                            pltpu.SemaphoreType.DMA((4 * n_a,)), pltpu.SemaphoreType.DMA((4 * n_a,))]
            + ([] if carry is None else list(carry.sems))),
        out_shape=[jax.ShapeDtypeStruct((4, fk, d), dt) for _ in range(n_a) for dt in (F32, BF16)]
        + ([] if carry is None else list(carry.out_shapes)),
        compiler_params=_cparams(("arbitrary",)),
    )(pos, *a_ts, b, *([] if carry is None else carry.inputs))


def _rope(t, c, sa, sb, reps):
    c, sa, sb = (jnp.tile(v, (1, reps)) if reps > 1 else v for v in (c, sa, sb))
    w = t.shape[1]
    return t * c + pltpu.roll(t, w - 8, 1) * sa + pltpu.roll(t, 8, 1) * sb


def _rope_bwd(dt, c, sa, sb, reps):
    c, sa, sb = (jnp.tile(v, (1, reps)) if reps > 1 else v for v in (c, sa, sb))
    w = dt.shape[1]
    return dt * c + pltpu.roll(dt * sa, 8, 1) + pltpu.roll(dt * sb, w - 8, 1)


def _ffn_down_mix_in(x, act_t, wd, g, win_t, tabs, name, tm, carry=None):
    s, d = x.shape
    f = wd.shape[0]
    n_in = win_t.shape[0]

    def body(x_ref, act_ref, wd_ref, g_ref, w_ref, c_ref, sa_ref, sb_ref, h_ref, q_ref, k_ref, v_ref, pc_ref, n_ref):
        x = x_ref[...] + 0.5 * _dot_tn(act_ref[...], wd_ref[...])
        h_ref[...] = x
        r = lax.rsqrt(jnp.mean(x * x, axis=-1, keepdims=True) + EPS)
        nb = (x * r * g_ref[...]).astype(BF16)
        n_ref[...] = nb
        u = _dot_nt(nb, w_ref[...])
        c, sa, sb = c_ref[...], sa_ref[...], sb_ref[...]
        q_ref[...] = _rope(u[:, :ATTN_W], c, sa, sb, ATTN_W // 128).astype(BF16)
        k_ref[...] = _rope(u[:, ATTN_W:ATTN_W + KV_W], c, sa, sb, 1).astype(BF16)
        v_ref[...] = u[:, ATTN_W + KV_W:ATTN_W + 2 * KV_W].astype(BF16)
        pc_ref[...] = u[:, ATTN_W + 2 * KV_W:]

    row = lambda w: pl.BlockSpec((tm, w), lambda i: (i, 0))
    return _call(
        body, name, (s // tm,),
        [row(d), pl.BlockSpec((f, tm), lambda i: (0, i)), _whole((f, d)), _whole((1, d)), _whole((n_in, d)),
         row(128), row(128), row(128)],
        [row(d), row(ATTN_W), row(KV_W), row(KV_W), row(POOL_W), row(d)],
        [jax.ShapeDtypeStruct((s, d), F32), jax.ShapeDtypeStruct((s, ATTN_W), BF16),
         jax.ShapeDtypeStruct((s, KV_W), BF16), jax.ShapeDtypeStruct((s, KV_W), BF16),
         jax.ShapeDtypeStruct((s, POOL_W), F32), jax.ShapeDtypeStruct((s, d), BF16)],
        (x, act_t, wd, g, win_t, *tabs), ("arbitrary",), carry)


def _band_mask(n, nb, transposed):
    shape = (3 * BLK, 2 * BLK) if transposed else (2 * BLK, 3 * BLK)
    i = lax.broadcasted_iota(jnp.int32, shape, 1 if transposed else 0) % BLK
    j = lax.broadcasted_iota(jnp.int32, shape, 0 if transposed else 1)
    kpos = (n - 1) * BLK + j
    return (j >= i) & (j <= i + 2 * BLK) & (kpos >= 0) & (kpos < nb * BLK)


def _block_diag(t, kh):
    tf = t.astype(F32)
    tr = pltpu.roll(tf, HEAD_DIM, 1)
    lo = lax.broadcasted_iota(jnp.int32, tf.shape, 1) < HEAD_DIM
    top, bot = (tf, tr) if kh == 0 else (tr, tf)
    return jnp.concatenate([jnp.where(lo, top, 0.0), jnp.where(lo, 0.0, bot)], axis=0).astype(BF16)


def _fold_diag(tbd):
    lo = lax.broadcasted_iota(jnp.int32, (3 * BLK, 2 * HEAD_DIM), 1) < HEAD_DIM
    t = jnp.where(lo, tbd[:3 * BLK], tbd[3 * BLK:])
    return t + pltpu.roll(t, HEAD_DIM, 1)


def _stack_pairs(x, kh):
    return jnp.concatenate([x[:, (2 * kh) * 128:(2 * kh + 1) * 128], x[:, (2 * kh + 1) * 128:(2 * kh + 2) * 128]], axis=0)


def _sink_of(sink_ref, kh, half, axis):
    shape = (2 * BLK, 1) if axis == 0 else (1, 2 * BLK)
    first = lax.broadcasted_iota(jnp.int32, shape, axis) < BLK
    return jnp.where(first, sink_ref[0, GROUP * kh + half], sink_ref[0, GROUP * kh + 2 + half])


def _softmax_sink(sc, valid, sink, axis):
    sc = jnp.where(valid, sc, -1e30)
    m = jnp.maximum(jnp.max(sc, axis=axis, keepdims=True), sink)
    e = jnp.exp(sc - m)
    es = jnp.exp(sink - m)
    inv = 1.0 / (jnp.sum(e, axis=axis, keepdims=True) + es)
    return e * inv, es * inv


def _attn_blocks_per_step(nb):
    return next(nq for nq in (4, 2, 1) if nb % nq == 0)


def _band_specs(nq, nb, w, col=0):
    return [pl.BlockSpec((BLK, w), lambda m: (jnp.maximum(nq * m - 1, 0), col)),
            pl.BlockSpec((nq * BLK, w), lambda m: (m, col)),
            pl.BlockSpec((BLK, w), lambda m: (jnp.minimum(nq * m + nq, nb - 1), col))]


def _attn_pool_fwd(h, w_out, q, k, v, pc, sink, pool_w, pool_scale, pband, name, carry=None, middle=CARRY_MIDDLE):
    s, d = h.shape
    nb = s // BLK
    nq = _attn_blocks_per_step(nb)

    def body(sink_ref, q_ref, k0, k1, k2, v0, v1, v2, p0, p1, p2, pw_ref, ps_ref, pb_ref, h_ref, wo_ref,
             h_out_ref, o_t_ref, o_ref):
        kall = jnp.concatenate([k0[...], k1[...], k2[...]], axis=0)
        vall = jnp.concatenate([v0[...], v1[...], v2[...]], axis=0)
        pall = jnp.concatenate([p0[...], p1[...], p2[...]], axis=0).astype(BF16)
        qall = q_ref[...] * SCORE_SCALE
        for j in range(nq):
            n = pl.program_id(0) * nq + j
            rows, band = slice(j * BLK, (j + 1) * BLK), slice(j * BLK, (j + 3) * BLK)
            valid = _band_mask(n, nb, False)
            kb, vb, qs = kall[band], vall[band], qall[rows]
            for kh in range(N_KV):
                sc = _dot_nt(_stack_pairs(qs, kh), _block_diag(kb, kh))
                p = [_softmax_sink(sc[:, half * 3 * BLK:(half + 1) * 3 * BLK], valid,
                                   _sink_of(sink_ref, kh, half, 0), 1)[0] for half in range(2)]
                o2 = _dot_nn(jnp.concatenate(p, axis=1).astype(BF16), _block_diag(vb, kh)).astype(BF16)
                o_ref[rows, (2 * kh) * 128:(2 * kh + 1) * 128] = o2[:BLK]
                o_ref[rows, (2 * kh + 1) * 128:(2 * kh + 2) * 128] = o2[BLK:]
            ext = pall[band]
            var = _variant_index(n, nb)
            for gi in range(POOL_G):
                gsl = slice(gi * POOL_GW, (gi + 1) * POOL_GW)
                dg = _dot_nn(pb_ref[var, gi], ext[:, gsl])
                yg = _dot_nn(dg.astype(BF16), pw_ref[gi].astype(BF16))
                o_ref[rows, ATTN_W + gi * POOL_GW:ATTN_W + (gi + 1) * POOL_GW] = (yg * ps_ref[:, gsl]).astype(BF16)
        mix = o_ref[...]
        o_t_ref[...] = mix.T
        h_out_ref[...] = h_ref[...] + _dot_nn(mix, wo_ref[...])

    mix_w = ATTN_W + POOL_W
    return _call(
        body, name, (nb // nq,),
        [pl.BlockSpec(memory_space=pltpu.SMEM), pl.BlockSpec((nq * BLK, ATTN_W), lambda m: (m, 0)),
         *_band_specs(nq, nb, KV_W), *_band_specs(nq, nb, KV_W), *_band_specs(nq, nb, POOL_W),
         _whole((POOL_G, POOL_GW, POOL_GW)), _whole((1, POOL_W)), _whole(pband.shape),
         pl.BlockSpec((nq * BLK, d), lambda m: (m, 0)), _whole((mix_w, d))],
        [pl.BlockSpec((nq * BLK, d), lambda m: (m, 0)), pl.BlockSpec((mix_w, nq * BLK), lambda m: (0, m))],
        [jax.ShapeDtypeStruct((s, d), F32), jax.ShapeDtypeStruct((mix_w, s), BF16)],
        (sink, q, k, k, k, v, v, v, pc, pc, pc, pool_w, pool_scale, pband, h, w_out), ("arbitrary",), carry, middle,
        scratch=(pltpu.VMEM((nq * BLK, mix_w), BF16),))


def _attn_pool_bwd(q, k, v, pc, dmix, sink, pool_w, pool_scale, pband, ptband, name, carry=None):
    s = q.shape[0]
    nb = s // BLK
    nq = _attn_blocks_per_step(nb)

    def body(sink_ref, q_ref, k0, k1, k2, v0, v1, v2, p0, p1, p2, da_ref, d0, d1, d2, pw_ref, ps_ref, pb_ref, ptb_ref,
             dq_ref, dk_ref, dv_ref, dpc_ref, dsink_ref, dpw_ref, dps_ref):
        @pl.when(pl.program_id(0) == 0)
        def _():
            dsink_ref[...] = jnp.zeros_like(dsink_ref)
            dpw_ref[...] = jnp.zeros_like(dpw_ref)
            dps_ref[...] = jnp.zeros_like(dps_ref)

        kall = jnp.concatenate([k0[...], k1[...], k2[...]], axis=0)
        vall = jnp.concatenate([v0[...], v1[...], v2[...]], axis=0)
        pall = jnp.concatenate([p0[...], p1[...], p2[...]], axis=0).astype(BF16)
        dpall = jnp.concatenate([d0[...], d1[...], d2[...]], axis=0)
        lo = lax.broadcasted_iota(jnp.int32, (3 * BLK, KV_W), 1) < HEAD_DIM
        for j in range(nq):
            n = pl.program_id(0) * nq + j
            rows, band = slice(j * BLK, (j + 1) * BLK), slice(j * BLK, (j + 3) * BLK)
            valid = _band_mask(n, nb, True)
            kb, vb, qb = kall[band], vall[band], q_ref[rows, :]
            qs = qb * SCORE_SCALE
            da = da_ref[rows, :].astype(BF16)
            dk_fold, dv_fold = [], []
            for kh in range(N_KV):
                kbd, vbd = _block_diag(kb, kh), _block_diag(vb, kh)
                q2, do2 = _stack_pairs(qb, kh), _stack_pairs(da, kh)
                sc_t = _dot_nt(kbd, _stack_pairs(qs, kh))
                dp_t = _dot_nt(vbd, do2)
                p_t, ds_t = [], []
                for half in range(2):
                    keys = slice(half * 3 * BLK, (half + 1) * 3 * BLK)
                    p, ps = _softmax_sink(sc_t[keys], valid, _sink_of(sink_ref, kh, half, 1), 0)
                    delta = jnp.sum(p * dp_t[keys], axis=0, keepdims=True)
                    p_t.append(p.astype(BF16))
                    ds_t.append((p * (dp_t[keys] - delta)).astype(BF16))
                    dsk = -ps * delta
                    for pair in range(2):
                        h = GROUP * kh + 2 * pair + half
                        part = jnp.sum(dsk[:, pair * BLK:(pair + 1) * BLK], axis=1, keepdims=True)
                        dsink_ref[h:h + 1, :] += jnp.broadcast_to(part, (1, 128))
                p_t = jnp.concatenate(p_t, axis=0)
                ds_t = jnp.concatenate(ds_t, axis=0)
                dq2 = _dot_tn(ds_t, kbd) * SCORE_SCALE
                dq_ref[rows, (2 * kh) * 128:(2 * kh + 1) * 128] = dq2[:BLK]
                dq_ref[rows, (2 * kh + 1) * 128:(2 * kh + 2) * 128] = dq2[BLK:]
                dk_fold.append(_fold_diag(_dot_nn(ds_t, q2)) * SCORE_SCALE)
                dv_fold.append(_fold_diag(_dot_nn(p_t, do2)))
            dk_all = jnp.where(lo, dk_fold[0], dk_fold[1])
            dv_all = jnp.where(lo, dv_fold[0], dv_fold[1])
            for t in range(3):
                dk_ref[j, t] = dk_all[t * BLK:(t + 1) * BLK]
                dv_ref[j, t] = dv_all[t * BLK:(t + 1) * BLK]
            ext, dpe = pall[band], dpall[band]
            dpc_cur = dpall[(j + 1) * BLK:(j + 2) * BLK]
            var = _variant_index(n, nb)
            for gi in range(POOL_G):
                gsl = slice(gi * POOL_GW, (gi + 1) * POOL_GW)
                wg = pw_ref[gi].astype(BF16)
                sc = ps_ref[:, gsl]
                dgb = _dot_nn(pb_ref[var, gi], ext[:, gsl]).astype(BF16)
                yg = _dot_nn(dgb, wg)
                dps_ref[:, gsl] += jnp.sum(dpc_cur[:, gsl] * yg, axis=0, keepdims=True)
                dpw_ref[gi] += _dot_tn(dgb, (dpc_cur[:, gsl] * sc).astype(BF16))
                dd = _dot_nt((dpe[:, gsl] * sc).astype(BF16), wg)
                dpc_ref[rows, gsl] = _dot_nn(ptb_ref[var, gi], dd.astype(BF16))

    fixed = lambda shape: pl.BlockSpec(shape, lambda m: (0,) * len(shape))
    return _call(
        body, name, (nb // nq,),
        [pl.BlockSpec(memory_space=pltpu.SMEM), pl.BlockSpec((nq * BLK, ATTN_W), lambda m: (m, 0)),
         *_band_specs(nq, nb, KV_W), *_band_specs(nq, nb, KV_W), *_band_specs(nq, nb, POOL_W),
         pl.BlockSpec((nq * BLK, ATTN_W), lambda m: (m, 0)), *_band_specs(nq, nb, POOL_W, 1),
         _whole((POOL_G, POOL_GW, POOL_GW)), _whole((1, POOL_W)), _whole(pband.shape), _whole(ptband.shape)],
        [pl.BlockSpec((nq * BLK, ATTN_W), lambda m: (m, 0)),
         pl.BlockSpec((nq, 3, BLK, KV_W), lambda m: (m, 0, 0, 0)),
         pl.BlockSpec((nq, 3, BLK, KV_W), lambda m: (m, 0, 0, 0)),
         pl.BlockSpec((nq * BLK, POOL_W), lambda m: (m, 0)),
         fixed((N_HEADS, 128)), fixed((POOL_G, POOL_GW, POOL_GW)), fixed((1, POOL_W))],
        [jax.ShapeDtypeStruct((s, ATTN_W), F32), jax.ShapeDtypeStruct((nb, 3, BLK, KV_W), F32),
         jax.ShapeDtypeStruct((nb, 3, BLK, KV_W), F32), jax.ShapeDtypeStruct((s, POOL_W), F32),
         jax.ShapeDtypeStruct((N_HEADS, 128), F32),
         jax.ShapeDtypeStruct((POOL_G, POOL_GW, POOL_GW), F32), jax.ShapeDtypeStruct((1, POOL_W), F32)],
        (sink, q, k, k, k, v, v, v, pc, pc, pc, dmix, dmix, dmix, dmix, pool_w, pool_scale, pband, ptband),
        ("arbitrary",), carry)


def _mix_in_bwd(h, dh, g, win_t, dq, dkp, dvp, dpc, tabs, name, tm, carry=None):
    s, d = h.shape
    nb = s // BLK
    nt = tm // BLK
    n_in = win_t.shape[0]

    def band_sum(n, before, own, after, prev_last, next_first):
        lo = (n > 0).astype(F32)
        hi = (n < s // tm - 1).astype(F32)
        blocks = []
        for b in range(nt):
            from_prev = prev_last[...] * lo if b == 0 else before[b - 1]
            from_next = next_first[...] * hi if b == nt - 1 else after[b + 1]
            blocks.append(from_prev + own[b] + from_next)
        return jnp.concatenate(blocks, axis=0)

    def body(h_ref, dh_ref, g_ref, w_ref, dq_ref, k2, k1, k0, kp, kn, v2, v1, v0, vp, vn, dpc_ref, c_ref, sa_ref,
             sb_ref, o_ref, du_ref, dg_ref):
        n = pl.program_id(0)
        dk = band_sum(n, k2, k1, k0, kp, kn)
        dv = band_sum(n, v2, v1, v0, vp, vn)
        c, sa, sb = c_ref[...], sa_ref[...], sb_ref[...]
        du = jnp.concatenate([_rope_bwd(dq_ref[...], c, sa, sb, ATTN_W // 128), _rope_bwd(dk, c, sa, sb, 1), dv,
                              dpc_ref[...]], axis=1)
        du_ref[...] = du.T.astype(BF16)
        dn = _dot_nn(du.astype(BF16), w_ref[...])
        x = h_ref[...]
        r = lax.rsqrt(jnp.mean(x * x, axis=-1, keepdims=True) + EPS)
        xhat = x * r
        dxhat = dn * g_ref[...]
        o_ref[...] = dh_ref[...] + r * (dxhat - xhat * jnp.mean(dxhat * xhat, axis=-1, keepdims=True))

        @pl.when(n == 0)
        def _():
            dg_ref[...] = jnp.zeros_like(dg_ref)

        dg_ref[...] += jnp.sum(dn * xhat, axis=0, keepdims=True)

    row = lambda w: pl.BlockSpec((tm, w), lambda n: (n, 0))
    slot = lambda t: pl.BlockSpec((nt, None, BLK, KV_W), lambda n, t=t: (n, t, 0, 0))
    parts = [slot(2), slot(1), slot(0),
             pl.BlockSpec((None, None, BLK, KV_W), lambda n: (jnp.maximum(nt * n - 1, 0), 2, 0, 0)),
             pl.BlockSpec((None, None, BLK, KV_W), lambda n: (jnp.minimum(nt * n + nt, nb - 1), 0, 0, 0))]
    return _call(
        body, name, (s // tm,),
        [row(d), row(d), _whole((1, d)), _whole((n_in, d)), row(ATTN_W), *parts, *parts, row(POOL_W),
         row(128), row(128), row(128)],
        [row(d), pl.BlockSpec((n_in, tm), lambda n: (0, n)), pl.BlockSpec((1, d), lambda n: (0, 0))],
        [jax.ShapeDtypeStruct((s, d), F32), jax.ShapeDtypeStruct((n_in, s), BF16), jax.ShapeDtypeStruct((1, d), F32)],
        (h, dh, g, win_t, dq, *[dkp] * 5, *[dvp] * 5, dpc, *tabs), ("arbitrary",), carry)


def _adam_math(w, g, m, v):
    m = ADAM_B1 * m + (1.0 - ADAM_B1) * g
    v = ADAM_B2 * v + (1.0 - ADAM_B2) * (g * g)
    m_hat = m / (1.0 - ADAM_B1 ** ADAM_STEP)
    v_hat = v / (1.0 - ADAM_B2 ** ADAM_STEP)
    delta = -ADAM_LR * (m_hat / (jnp.sqrt(v_hat) + ADAM_EPS) + ADAM_WD * w)
    return delta, m, v


def _adam_small(w, parts, late, m, v, name):
    rows, cols = w.shape

    def body(w_ref, p_ref, l_ref, m_ref, v_ref, g_ref, d_ref, nm_ref, nv_ref):
        g, gl = p_ref[0], l_ref[0]
        for k in range(1, N_DEV):
            g = g + p_ref[k]
            gl = gl + l_ref[k]
        g_ref[...] = g
        g_ref[SMALL_NORM1:SMALL_NORM1 + 8, :] = g[SMALL_NORM1:SMALL_NORM1 + 8] + gl
        d_ref[...], nm_ref[...], nv_ref[...] = _adam_math(w_ref[...], g_ref[...], m_ref[...], v_ref[...])

    return pl.pallas_call(
        body, name=name, out_shape=[jax.ShapeDtypeStruct((rows, cols), F32)] * 4,
    )(w, parts, late, m, v)


SMALL_NORM1 = 512


def _pack_small(norm1, normm, norm2, normf, sink, pool_w, pool_scale, loss=None):
    scale_rows = jnp.pad(pool_scale.reshape(4, 128), ((0, 4), (0, 0)))
    last_rows = jnp.pad(sink.reshape(1, N_HEADS), ((0, 7), (0, 128 - N_HEADS)))
    if loss is not None:
        last_rows = last_rows + jnp.pad(loss.reshape(1, 1), ((1, 6), (0, 127)))
    return jnp.concatenate([pool_w.reshape(512, 128), norm1.reshape(8, 128), normm.reshape(8, 128),
                            norm2.reshape(8, 128), normf.reshape(8, 128), scale_rows, last_rows], axis=0)


def _unpack_small(p):
    return dict(pool_w=p[:512].reshape(1, POOL_G, POOL_GW, POOL_GW), ffn1_norm=p[512:520].reshape(1, 1024),
                mix_norm=p[520:528].reshape(1, 1024), ffn2_norm=p[528:536].reshape(1, 1024),
                final_norm=p[536:544].reshape(1024), pool_scale=p[544:548].reshape(1, POOL_W),
                sink_logits=p[552, :N_HEADS].reshape(1, N_HEADS), loss=p[553, 0])


def kernel(x, ffn1_norm, ffn1_w_gate, ffn1_w_up, ffn1_w_down, mix_norm, w_in, sink_logits, pool_w, pool_scale, w_out, ffn2_norm, ffn2_w_gate, ffn2_w_up, ffn2_w_down, final_norm, loss_target, m_ffn1_norm, m_ffn1_w_gate, m_ffn1_w_up, m_ffn1_w_down, m_mix_norm, m_w_in, m_sink_logits, m_pool_w, m_pool_scale, m_w_out, m_ffn2_norm, m_ffn2_w_gate, m_ffn2_w_up, m_ffn2_w_down, m_final_norm, v_ffn1_norm, v_ffn1_w_gate, v_ffn1_w_up, v_ffn1_w_down, v_mix_norm, v_w_in, v_sink_logits, v_pool_w, v_pool_scale, v_w_out, v_ffn2_norm, v_ffn2_w_gate, v_ffn2_w_up, v_ffn2_w_down, v_final_norm):
    s, d = x.shape[1], x.shape[2]
    tm = min(512, s)
    tm_bwd = min(256, s)
    pos = jnp.stack([lax.axis_index("x"), lax.axis_index("y"), lax.axis_index("c")]).astype(jnp.int32)

    t_bf = lambda w: w[0].T.astype(BF16)
    full = lambda a: a.reshape(N_DEV * a.shape[1], d)
    (wg1,) = map(full, _run_exchange(_AllGather([t_bf(ffn1_w_gate)]), "gather_ffn1_gate"))

    tabs = _rope_tables(s)
    pband, ptband = _pool_tables(s)
    g1, gm, g2, gf = ffn1_norm, mix_norm, ffn2_norm, final_norm.reshape(1, d)

    x0 = x[0]
    n1, gate1, wu1 = _ffn_gate(x0, g1, wg1, "ffn1_gate", tm, carry=_AllGather([t_bf(ffn1_w_up)]))
    up1, act1_t, wd1, win_t = _ffn_up(n1, gate1, full(wu1), "ffn1_up", tm,
                                            carry=_AllGather([ffn1_w_down[0].astype(BF16), t_bf(w_in)]))
    wu1, wd1, win_t = full(wu1), full(wd1), full(win_t)
    h1, q, k, v, pc, n2, wout, wg2 = _ffn_down_mix_in(
        x0, act1_t, wd1, gm, win_t, tabs, "ffn1_down_mix_in", tm,
        carry=_AllGather([w_out[0].astype(BF16), t_bf(ffn2_w_gate)]))
    wout, wg2 = full(wout), full(wg2)
    h2, mix_t, *gathered = _attn_pool_fwd(h1, wout, q, k, v, pc, sink_logits, pool_w[0], pool_scale, pband,
                                          "attn_pool_fwd",
                                          carry=_AllGather([t_bf(ffn2_w_up), ffn2_w_down[0].astype(BF16)]))
    wu2, wd2 = map(full, gathered)
    dh3, n3, gate2, up2, act2_t, loss_part, dgf = _ffn_fwd(h2, g2, wg2, wu2, wd2, "ffn2_fwd", tm,
                                                           head=(gf, loss_target[0]))

    sum1, recv2 = {}, {}

    def stage2(keys):
        return _RsStage2([sum1[key][1] for key in keys])

    dh2, dg2, dgate2_t, dup2_t, dhh3, dmix, dh2b = _ffn_bwd(h2, dh3, gate2, up2, g2, wg2, wu2, wd2, wout, "ffn2_bwd",
                                                            tm_bwd)
    both = _wgrad_rs1(pos, [dgate2_t, dup2_t], n3, "wgrad_gate_up2")
    sum1["g2"], sum1["u2"] = both[:2], both[2:]
    sum1["d2"] = _wgrad_rs1(pos, [act2_t], dhh3, "wgrad_down2")
    sum1["out"] = _wgrad_rs1(pos, [mix_t], dh2b, "wgrad_out")
    dq, dkp, dvp, dpc, dsink, dpw, dps, *r2 = _attn_pool_bwd(
        q, k, v, pc, dmix, sink_logits, pool_w[0], pool_scale, pband, ptband, "attn_pool_bwd",
        carry=stage2(["g2", "u2", "d2"]))
    recv2.update(zip(["g2", "u2", "d2"], r2))
    dh1, du_t, dgm, recv2["out"] = _mix_in_bwd(h1, dh2, gm, win_t, dq, dkp, dvp, dpc, tabs, "mix_in_bwd", tm,
                                               carry=stage2(["out"]))
    sum1["in"] = _wgrad_rs1(pos, [du_t], n2, "wgrad_in")
    small_part = _pack_small(jnp.zeros_like(dgm), dgm, dg2, dgf, dsink[:, 0], dpw, dps, loss_part)
    dgate1_t, dup1_t, dhh1, recv2["in"] = _ffn_bwd_gates(dh1, gate1, up1, wd1, "ffn1_bwd_gates", tm,
                                                         carry=stage2(["in"]))
    *sum1["g1"], small_all = _wgrad_rs1(pos, [dgate1_t], n1, "wgrad_gate1", carry=_AllGather([small_part]))
    *sum1["u1"], recv2["g1"] = _wgrad_rs1(pos, [dup1_t], n1, "wgrad_up1", carry=stage2(["g1"]))
    *sum1["d1"], recv2["u1"] = _wgrad_rs1(pos, [act1_t], dhh1, "wgrad_down1", carry=stage2(["u1"]))
    dx, dg1, recv2["d1"] = _ffn_bwd_input(x0, dh1, dgate1_t, dup1_t, g1, wg1, wu1, "ffn1_bwd_input", tm,
                                          carry=stage2(["d1"]))

    (dg1_all,) = _run_exchange(_DirectGather([dg1.reshape(8, 128)]), "gather_norm1_grad")
    pk = lambda a, b, c_, e, s_, pw_, psc: _pack_small(a, b, c_, e, s_[0], pw_[0], psc)
    small_w = pk(ffn1_norm, mix_norm, ffn2_norm, final_norm, sink_logits, pool_w, pool_scale)
    small_m = pk(m_ffn1_norm, m_mix_norm, m_ffn2_norm, m_final_norm, m_sink_logits, m_pool_w, m_pool_scale)
    small_v = pk(v_ffn1_norm, v_mix_norm, v_ffn2_norm, v_final_norm, v_sink_logits, v_pool_w, v_pool_scale)
    sg, sd, sm, sv = [_unpack_small(a)
                      for a in _adam_small(small_w, small_all, dg1_all, small_m, small_v, "adam_small")]

    big = {}
    keys = ["g1", "u1", "d1", "g2", "u2", "d2", "in", "out"]
    names = ["ffn1_w_gate", "ffn1_w_up", "ffn1_w_down", "ffn2_w_gate", "ffn2_w_up", "ffn2_w_down", "w_in", "w_out"]
    transposed = [True, True, False, True, True, False, True, False]
    ws = [ffn1_w_gate, ffn1_w_up, ffn1_w_down, ffn2_w_gate, ffn2_w_up, ffn2_w_down, w_in, w_out]
    ms = [m_ffn1_w_gate, m_ffn1_w_up, m_ffn1_w_down, m_ffn2_w_gate, m_ffn2_w_up, m_ffn2_w_down, m_w_in, m_w_out]
    vs = [v_ffn1_w_gate, v_ffn1_w_up, v_ffn1_w_down, v_ffn2_w_gate, v_ffn2_w_up, v_ffn2_w_down, v_w_in, v_w_out]
    for key, nm, tr, w, m, vv in zip(keys, names, transposed, ws, ms, vs):
        view = (lambda a: jnp.swapaxes(a, 1, 2)[0]) if tr else (lambda a: a[0])
        back = (lambda a: jnp.swapaxes(a[None], 1, 2)) if tr else (lambda a: a[None])
        res = _rs_sum2_adam(pos, sum1[key][0], recv2[key], view(w), view(m), view(vv), "adam_" + nm)
        big[nm] = tuple(back(a) for a in res)

    loss = sg["loss"]
    all_names = ["ffn1_norm", "ffn1_w_gate", "ffn1_w_up", "ffn1_w_down", "mix_norm", "w_in", "sink_logits", "pool_w",
                 "pool_scale", "w_out", "ffn2_norm", "ffn2_w_gate", "ffn2_w_up", "ffn2_w_down", "final_norm"]
    outs = [loss, dx[None]]
    for idx, src in enumerate((sg, sd, sm, sv)):
        for nm in all_names:
            outs.append(big[nm][idx] if nm in big else src[nm])
    return tuple(outs)
```

```python
import functools

import jax
import jax.numpy as jnp
import numpy as np
from jax import lax
from jax.experimental import pallas as pl
from jax.experimental.pallas import tpu as pltpu

F32 = jnp.float32
BF16 = jnp.bfloat16
MESH = pl.DeviceIdType.MESH
N_DEV = 8

EPS = 1e-6
HEAD_DIM = 64
N_HEADS = 8
N_KV = 2
GROUP = N_HEADS // N_KV
ATTN_W = N_HEADS * HEAD_DIM
KV_W = N_KV * HEAD_DIM
POOL_W = 512
POOL_G = 4
POOL_GW = POOL_W // POOL_G
POOL_WINDOWS = (2, 4, 8, 16)
BLK = 128
ROT = 16
ROPE_THETA = 500000.0
SCORE_SCALE = HEAD_DIM ** -0.5

ADAM_LR, ADAM_B1, ADAM_B2, ADAM_EPS, ADAM_WD, ADAM_STEP = 0.001, 0.9, 0.999, 1e-08, 0.01, 10

VMEM_LIMIT = 56 * 1024 * 1024


def _cparams(sem=None, **kw):
    if sem is not None:
        kw["dimension_semantics"] = sem
    return pltpu.CompilerParams(vmem_limit_bytes=VMEM_LIMIT, **kw)


def _whole(shape):
    nd = len(shape)
    return pl.BlockSpec(shape, lambda *_: (0,) * nd, pipeline_mode=pl.Buffered(1))


def _sigmoid(z):
    return 1.0 / (1.0 + jnp.exp(-z))


def _dot_nt(a, b):
    return lax.dot_general(a, b, (((1,), (1,)), ((), ())), preferred_element_type=F32)


def _dot_nn(a, b):
    return lax.dot_general(a, b, (((1,), (0,)), ((), ())), preferred_element_type=F32)


def _dot_tn(a, b):
    return lax.dot_general(a, b, (((0,), (0,)), ((), ())), preferred_element_type=F32)


def _rope_tables(s):
    inv_freq = ROPE_THETA ** (-np.arange(0, ROT, 2, dtype=np.float64) / ROT)
    ang = np.arange(s, dtype=np.float64)[:, None] * inv_freq[None, :]
    c = np.ones((s, HEAD_DIM)); sa = np.zeros((s, HEAD_DIM)); sb = np.zeros((s, HEAD_DIM))
    c[:, :8] = np.cos(ang); c[:, 8:16] = np.cos(ang)
    sa[:, :8] = -np.sin(ang)
    sb[:, 8:16] = np.sin(ang)
    t = lambda a: jnp.asarray(np.tile(a, (1, 2)).astype(np.float32))
    return t(c), t(sa), t(sb)


def _pool_weight(gi, t, s_pos, s):
    half = POOL_WINDOWS[gi] // 2

    def win(lo, hi):
        a = np.clip(lo, 0, s); b = np.clip(hi + 1, 0, s)
        inside = (s_pos >= a) & (s_pos < b)
        return inside / np.maximum(b - a, 1)

    w = 0.5 * (win(t - half, t + half - 1) + win(t - half + 1, t + half)) - (t == s_pos)
    return w * ((t >= 0) & (t < s) & (s_pos >= 0) & (s_pos < s))


def _pool_tables(s):
    nb = s // BLK
    fwd = np.zeros((3, POOL_G, BLK, 3 * BLK), np.float32)
    bwd = np.zeros((3, POOL_G, BLK, 3 * BLK), np.float32)
    for vi, n in enumerate((0, 1 if nb > 2 else 0, nb - 1)):
        i = n * BLK + np.arange(BLK)[:, None]
        j = (n - 1) * BLK + np.arange(3 * BLK)[None, :]
        for gi in range(POOL_G):
            fwd[vi, gi] = _pool_weight(gi, i, j, s)
            bwd[vi, gi] = _pool_weight(gi, j, i, s)
    return jnp.asarray(fwd, dtype=BF16), jnp.asarray(bwd, dtype=BF16)


def _variant_index(n, nb):
    return jnp.where(n == 0, 0, jnp.where(n == nb - 1, 2, 1))


class _Exchange:
    inputs = ()
    out_shapes = ()
    sems = ()

    def start(self, srcs, outs, sems):
        raise NotImplementedError

    def middle(self, srcs, outs, sems):
        pass

    def finish(self, srcs, outs, sems):
        raise NotImplementedError


class _AllGather(_Exchange):
    def __init__(self, arrays):
        n = len(arrays)
        self.inputs = list(arrays)
        self.out_shapes = [jax.ShapeDtypeStruct((N_DEV,) + a.shape, a.dtype) for a in arrays]
        self.sems = [pltpu.SemaphoreType.DMA((n, 8)), pltpu.SemaphoreType.DMA((n, 8)), pltpu.SemaphoreType.DMA((n,))]

    def _parts(self, srcs, outs, sems):
        send_sems, recv_sems, local_sems = sems
        n = len(srcs)
        x, y, c = lax.axis_index("x"), lax.axis_index("y"), lax.axis_index("c")
        me, sibling, xn, yn, diag = (x, y, c), (x, y, 1 - c), (1 - x, y, c), (x, 1 - y, c), (1 - x, 1 - y, c)

        def place(a, dev, half=None):
            block = outs[a].at[4 * dev[0] + 2 * dev[1] + dev[2]]
            if half is None:
                return block
            r2 = outs[a].shape[1] // 2
            return block.at[pl.ds(half * r2, r2)]

        def copy(a, k, dev, to, half=None, src=None):
            where = place(a, dev, half)
            return pltpu.make_async_remote_copy(
                src_ref=where if src is None else src, dst_ref=where, send_sem=send_sems.at[a, k],
                recv_sem=recv_sems.at[a, k], device_id=to, device_id_type=MESH)

        def other(dev):
            return (dev[0], dev[1], 1 - dev[2])

        class Parts:
            mine = staticmethod(lambda: [pltpu.make_async_copy(srcs[a], place(a, me), local_sems.at[a])
                                         for a in range(n)])
            own = staticmethod(lambda: [copy(a, k, me, to, src=srcs[a]) for a in range(n)
                                        for k, to in ((0, sibling), (1, xn), (2, yn))])
            relay = staticmethod(lambda a: [copy(a, 3, xn, yn, half=0), copy(a, 4, yn, xn, half=1),
                                            copy(a, 5, xn, sibling), copy(a, 6, yn, sibling)])
            last = staticmethod(lambda a: copy(a, 7, diag, sibling))
            from_x = staticmethod(lambda a: copy(a, 1, xn, me))
            from_y = staticmethod(lambda a: copy(a, 2, yn, me))
            diag_halves = staticmethod(lambda a: [copy(a, 3, diag, me, half=0), copy(a, 4, diag, me, half=1)])
            from_sibling = staticmethod(lambda a: [copy(a, 0, sibling, me), copy(a, 5, other(xn), me),
                                                   copy(a, 6, other(yn), me), copy(a, 7, other(diag), me)])

        return n, Parts

    def start(self, srcs, outs, sems):
        _, p = self._parts(srcs, outs, sems)
        for cp in p.mine() + p.own():
            cp.start()

    def middle(self, srcs, outs, sems):
        n, p = self._parts(srcs, outs, sems)
        for a in range(n):
            p.from_x(a).wait_recv()
            p.from_y(a).wait_recv()
            for cp in p.relay(a):
                cp.start()

    def finish(self, srcs, outs, sems):
        n, p = self._parts(srcs, outs, sems)
        for a in range(n):
            for cp in p.diag_halves(a):
                cp.wait_recv()
            p.last(a).start()
        for a in range(n):
            for cp in p.from_sibling(a):
                cp.wait_recv()
        for cp in p.own() + [cp for a in range(n) for cp in p.relay(a) + [p.last(a)]]:
            cp.wait_send()
        for cp in p.mine():
            cp.wait()


class _RsStage2(_Exchange):
    def start(self, srcs, outs, sems):
        for cp in self._copies(srcs, outs, sems):
            cp.start()

    def finish(self, srcs, outs, sems):
        copies = self._copies(srcs, outs, sems)
        for cp in copies:
            cp.wait_recv()
        for cp in copies:
            cp.wait_send()


    def __init__(self, pbs):
        n = len(pbs)
        self.inputs = list(pbs)
        self.out_shapes = [jax.ShapeDtypeStruct((3,) + p.shape[1:], p.dtype) for p in pbs]
        self.sems = [pltpu.SemaphoreType.DMA((n, 3)), pltpu.SemaphoreType.DMA((n, 3))]

    def _copies(self, srcs, outs, sems):
        send_sems, recv_sems = sems
        x, y, c = lax.axis_index("x"), lax.axis_index("y"), lax.axis_index("c")
        chips = [(1 - x, y), (x, 1 - y), (1 - x, 1 - y)]
        return [pltpu.make_async_remote_copy(
            src_ref=srcs[a].at[2 * chip[0] + chip[1]], dst_ref=outs[a].at[j], send_sem=send_sems.at[a, j],
            recv_sem=recv_sems.at[a, j], device_id=(*chip, c), device_id_type=MESH)
            for a in range(len(srcs)) for j, chip in enumerate(chips)]


class _DirectGather(_Exchange):
    def __init__(self, arrays):
        n = len(arrays)
        self.inputs = list(arrays)
        self.out_shapes = [jax.ShapeDtypeStruct((N_DEV,) + a.shape, a.dtype) for a in arrays]
        self.sems = [pltpu.SemaphoreType.DMA((n, 7)), pltpu.SemaphoreType.DMA((n, 7)), pltpu.SemaphoreType.DMA((n,))]

    def _copies(self, srcs, outs, sems):
        send_sems, recv_sems, local_sems = sems
        x, y, c = lax.axis_index("x"), lax.axis_index("y"), lax.axis_index("c")
        me = 4 * x + 2 * y + c
        remote, local = [], []
        for a in range(len(srcs)):
            local.append(pltpu.make_async_copy(srcs[a], outs[a].at[me], local_sems.at[a]))
            for k in range(1, N_DEV):
                peer = (x ^ (k >> 2), y ^ ((k >> 1) & 1), c ^ (k & 1))
                remote.append(pltpu.make_async_remote_copy(
                    src_ref=srcs[a], dst_ref=outs[a].at[me], send_sem=send_sems.at[a, k - 1],
                    recv_sem=recv_sems.at[a, k - 1], device_id=peer, device_id_type=MESH))
        return remote, local

    def start(self, srcs, outs, sems):
        remote, local = self._copies(srcs, outs, sems)
        for cp in local + remote:
            cp.start()

    def finish(self, srcs, outs, sems):
        remote, local = self._copies(srcs, outs, sems)
        for cp in remote:
            cp.wait_recv()
        for cp in remote:
            cp.wait_send()
        for cp in local:
            cp.wait()


class _Both(_Exchange):
    def __init__(self, a, b):
        self.a, self.b = a, b
        self.inputs = list(a.inputs) + list(b.inputs)
        self.out_shapes = list(a.out_shapes) + list(b.out_shapes)
        self.sems = list(a.sems) + list(b.sems)

    def _split(self, srcs, outs, sems):
        na, oa, sa = len(self.a.inputs), len(self.a.out_shapes), len(self.a.sems)
        return (srcs[:na], outs[:oa], sems[:sa]), (srcs[na:], outs[oa:], sems[sa:])

    def start(self, srcs, outs, sems):
        pa, pb = self._split(srcs, outs, sems)
        self.a.start(*pa)
        self.b.start(*pb)

    def middle(self, srcs, outs, sems):
        pa, pb = self._split(srcs, outs, sems)
        self.a.middle(*pa)
        self.b.middle(*pb)

    def finish(self, srcs, outs, sems):
        pa, pb = self._split(srcs, outs, sems)
        self.a.finish(*pa)
        self.b.finish(*pb)


_ANY = pl.BlockSpec(memory_space=pl.ANY)


def _run_exchange(ex, name):
    n_in, n_out = len(ex.inputs), len(ex.out_shapes)

    def body(*refs):
        srcs, outs, sems = refs[:n_in], refs[n_in:n_in + n_out], refs[n_in + n_out:]
        ex.start(srcs, outs, sems)
        ex.middle(srcs, outs, sems)
        ex.finish(srcs, outs, sems)

    return pl.pallas_call(
        body, name=name, out_shape=list(ex.out_shapes), in_specs=[_ANY] * n_in, out_specs=[_ANY] * n_out,
        scratch_shapes=list(ex.sems),
    )(*ex.inputs)


CARRY_MIDDLE = 0.7


def _call(body, name, grid, in_specs, out_specs, out_shape, args, sem, carry=None, middle=CARRY_MIDDLE, scratch=()):
    if carry is None:
        return pl.pallas_call(functools.partial(body), name=name, grid=grid, in_specs=in_specs, out_specs=out_specs,
                              out_shape=out_shape, scratch_shapes=list(scratch), compiler_params=_cparams(sem))(*args)
    n_in, n_out = len(in_specs), len(out_specs)
    nc_in, nc_out = len(carry.inputs), len(carry.out_shapes)

    def carried(*refs):
        ins = refs[:n_in]
        c_in = refs[n_in:n_in + nc_in]
        outs = refs[n_in + nc_in:n_in + nc_in + n_out]
        c_out = refs[n_in + nc_in + n_out:n_in + nc_in + n_out + nc_out]
        own = refs[n_in + nc_in + n_out + nc_out:n_in + nc_in + n_out + nc_out + len(scratch)]
        sems = refs[n_in + nc_in + n_out + nc_out + len(scratch):]
        ids = [pl.program_id(i) for i in range(len(grid))]
        is_first = functools.reduce(jnp.logical_and, [i == 0 for i in ids])
        is_last = functools.reduce(jnp.logical_and, [i == g - 1 for i, g in zip(ids, grid)])
        @pl.when(is_first)
        def _():
            carry.start(c_in, c_out, sems)

        if middle is not None:
            @pl.when(functools.reduce(jnp.logical_and, [ids[0] == round(middle * (grid[0] - 1))]
                                      + [i == 0 for i in ids[1:]]))
            def _():
                carry.middle(c_in, c_out, sems)

        body(*ins, *outs, *own)

        @pl.when(is_last)
        def _():
            if middle is None:
                carry.middle(c_in, c_out, sems)
            carry.finish(c_in, c_out, sems)

    return pl.pallas_call(
        carried, name=name, grid=grid, in_specs=list(in_specs) + [_ANY] * nc_in,
        out_specs=list(out_specs) + [_ANY] * nc_out, out_shape=list(out_shape) + list(carry.out_shapes),
        scratch_shapes=list(scratch) + list(carry.sems), compiler_params=_cparams(sem))(*args, *carry.inputs)


def _rs_sum2_adam(pos, p, r2, w, m, v, name):
    _, rows, d = p.shape
    tr = rows // 2 if rows % 16 == 0 else rows

    def body(pos_ref, p_ref, r_ref, w_ref, m_ref, v_ref, g_ref, d_ref, nm_ref, nv_ref):
        r = r_ref[...].astype(F32)
        g = ((p_ref[...] + r[0]) + r[1]) + r[2]
        g_ref[...] = g
        d_ref[...], nm_ref[...], nv_ref[...] = _adam_math(w_ref[...], g, m_ref[...], v_ref[...])

    blk = pl.BlockSpec((tr, d), lambda i, pos_ref: (i, 0))
    return pl.pallas_call(
        body, name=name,
        grid_spec=pltpu.PrefetchScalarGridSpec(
            num_scalar_prefetch=1, grid=(rows // tr,),
            in_specs=[pl.BlockSpec((None, tr, d), lambda i, pos_ref: (2 * pos_ref[0] + pos_ref[1], i, 0)),
                      pl.BlockSpec((3, tr, d), lambda i, pos_ref: (0, i, 0)), blk, blk, blk],
            out_specs=[blk] * 4),
        out_shape=[jax.ShapeDtypeStruct((rows, d), F32)] * 4,
        compiler_params=_cparams(("arbitrary",)),
    )(pos, p, r2, w, m, v)


def _ffn_chunk(f):
    for cand in (256, 128):
        if f % cand == 0:
            return cand
    return f


def _loss_head(x, gg, target, loss_ref, dg_ref):
    @pl.when(pl.program_id(0) == 0)
    def _():
        loss_ref[...] = jnp.zeros_like(loss_ref)
        dg_ref[...] = jnp.zeros_like(dg_ref)

    r = lax.rsqrt(jnp.mean(x * x, axis=-1, keepdims=True) + EPS)
    xhat = x * r
    e = xhat * gg - target
    loss_ref[...] += 0.5 * jnp.sum(jnp.mean(e * e, axis=-1, keepdims=True), axis=0, keepdims=True)
    dy = e * (1.0 / x.shape[-1])
    dg_ref[...] += jnp.sum(dy * xhat, axis=0, keepdims=True)
    dxhat = dy * gg
    return r * (dxhat - xhat * jnp.mean(dxhat * xhat, axis=-1, keepdims=True))


def _ffn_gate(h, g, wg_t, name, tm, carry=None):
    s, d = h.shape
    f = wg_t.shape[0]

    def body(h_ref, g_ref, wg_ref, n_ref, gate_ref):
        x = h_ref[...]
        r = lax.rsqrt(jnp.mean(x * x, axis=-1, keepdims=True) + EPS)
        nb = (x * r * g_ref[...]).astype(BF16)
        n_ref[...] = nb
        gate_ref[...] = _dot_nt(nb, wg_ref[...]).astype(BF16)

    row = lambda w: pl.BlockSpec((tm, w), lambda i: (i, 0))
    return _call(body, name, (s // tm,), [row(d), _whole((1, d)), _whole((f, d))], [row(d), row(f)],
                 [jax.ShapeDtypeStruct((s, d), BF16), jax.ShapeDtypeStruct((s, f), BF16)], (h, g, wg_t),
                 ("arbitrary",), carry)


def _ffn_up(n, gate, wu_t, name, tm, carry=None):
    s, d = n.shape
    f = wu_t.shape[0]
    tf = _ffn_chunk(f)

    def body(n_ref, gate_ref, wu_ref, up_ref, act_t_ref):
        nb = n_ref[...]
        for j in range(f // tf):
            sl = slice(j * tf, (j + 1) * tf)
            up = _dot_nt(nb, wu_ref[sl, :])
            gate = gate_ref[:, sl].astype(F32)
            up_ref[:, sl] = up.astype(BF16)
            act_t_ref[sl, :] = (gate * _sigmoid(gate) * up).astype(BF16).T

    row = lambda w: pl.BlockSpec((tm, w), lambda i: (i, 0))
    return _call(body, name, (s // tm,), [row(d), row(f), _whole((f, d))],
                 [row(f), pl.BlockSpec((f, tm), lambda i: (0, i))],
                 [jax.ShapeDtypeStruct((s, f), BF16), jax.ShapeDtypeStruct((f, s), BF16)], (n, gate, wu_t),
                 ("arbitrary",), carry)


def _ffn_fwd(h, g, wg_t, wu_t, wd, name, tm, carry=None, head=None, middle=CARRY_MIDDLE):
    s, d = h.shape
    f = wg_t.shape[0]
    tf = _ffn_chunk(f)

    def body(h_ref, g_ref, wg_ref, wu_ref, wd_ref, *refs):
        if head is None:
            o_ref, n_ref, gate_ref, up_ref, act_t_ref = refs
        else:
            gf_ref, t_ref, o_ref, n_ref, gate_ref, up_ref, act_t_ref, loss_ref, dgf_ref = refs
        x = h_ref[...]
        r = lax.rsqrt(jnp.mean(x * x, axis=-1, keepdims=True) + EPS)
        nb = (x * r * g_ref[...]).astype(BF16)
        n_ref[...] = nb
        for j in range(f // tf):
            sl = slice(j * tf, (j + 1) * tf)
            gate = _dot_nt(nb, wg_ref[sl, :])
            up = _dot_nt(nb, wu_ref[sl, :])
            gate_ref[:, sl] = gate.astype(BF16)
            up_ref[:, sl] = up.astype(BF16)
            act_t_ref[sl, :] = (gate * _sigmoid(gate) * up).astype(BF16).T
        h_out = x + 0.5 * _dot_tn(act_t_ref[...], wd_ref[...])
        o_ref[...] = h_out if head is None else _loss_head(h_out, gf_ref[...], t_ref[...], loss_ref, dgf_ref)

    row = lambda w: pl.BlockSpec((tm, w), lambda i: (i, 0))
    in_specs = [row(d), _whole((1, d)), _whole((f, d)), _whole((f, d)), _whole((f, d))]
    out_specs = [row(d), row(d), row(f), row(f), pl.BlockSpec((f, tm), lambda i: (0, i))]
    out_shape = ([jax.ShapeDtypeStruct((s, d), F32), jax.ShapeDtypeStruct((s, d), BF16)]
                 + [jax.ShapeDtypeStruct((s, f), BF16)] * 2 + [jax.ShapeDtypeStruct((f, s), BF16)])
    args = (h, g, wg_t, wu_t, wd)
    if head is not None:
        in_specs += [_whole((1, d)), row(d)]
        out_specs += [pl.BlockSpec((1, 1), lambda i: (0, 0)), pl.BlockSpec((1, d), lambda i: (0, 0))]
        out_shape += [jax.ShapeDtypeStruct((1, 1), F32), jax.ShapeDtypeStruct((1, d), F32)]
        args += tuple(head)
    return _call(body, name, (s // tm,), in_specs, out_specs, out_shape, args, ("arbitrary",), carry, middle)


def _gate_grads(dh_ref, gate_ref, up_ref, wd_ref, dgate_t_ref, dup_t_ref, dhh_ref, tf):
    dhh = (0.5 * dh_ref[...]).astype(BF16)
    dhh_ref[...] = dhh
    for j in range(gate_ref.shape[1] // tf):
        sl = slice(j * tf, (j + 1) * tf)
        gt = gate_ref[:, sl].astype(F32)
        u = up_ref[:, sl].astype(F32)
        dact = _dot_nt(dhh, wd_ref[sl, :])
        sg = _sigmoid(gt)
        dup = dact * (gt * sg)
        dgate = dact * u * (sg * (1.0 + gt * (1.0 - sg)))
        dup_t_ref[sl, :] = dup.astype(BF16).T
        dgate_t_ref[sl, :] = dgate.astype(BF16).T


def _input_grad(h_ref, dh_ref, dgate_t_ref, dup_t_ref, g_ref, wg_ref, wu_ref, o_ref, dg_ref):
    x = h_ref[...]
    r = lax.rsqrt(jnp.mean(x * x, axis=-1, keepdims=True) + EPS)
    xhat = x * r
    dn = _dot_tn(dgate_t_ref[...], wg_ref[...]) + _dot_tn(dup_t_ref[...], wu_ref[...])
    dxhat = dn * g_ref[...]
    o_ref[...] = dh_ref[...] + r * (dxhat - xhat * jnp.mean(dxhat * xhat, axis=-1, keepdims=True))

    @pl.when(pl.program_id(0) == 0)
    def _():
        dg_ref[...] = jnp.zeros_like(dg_ref)

    dg_ref[...] += jnp.sum(dn * xhat, axis=0, keepdims=True)


def _ffn_bwd(h_in, dh_out, gate, up, g, wg_t, wu_t, wd, w_out, name, tm):
    s, d = h_in.shape
    f = gate.shape[1]
    w = w_out.shape[0]
    tf = _ffn_chunk(f)

    def body(h_ref, dh_ref, gate_ref, up_ref, g_ref, wg_ref, wu_ref, wd_ref, wo_ref,
             o_ref, dg_ref, dgate_t_ref, dup_t_ref, dhh_ref, dmix_ref, dhb_ref):
        _gate_grads(dh_ref, gate_ref, up_ref, wd_ref, dgate_t_ref, dup_t_ref, dhh_ref, tf)
        _input_grad(h_ref, dh_ref, dgate_t_ref, dup_t_ref, g_ref, wg_ref, wu_ref, o_ref, dg_ref)
        dhb = o_ref[...].astype(BF16)
        dhb_ref[...] = dhb
        dmix_ref[...] = _dot_nt(dhb, wo_ref[...])

    row = lambda c: pl.BlockSpec((tm, c), lambda i: (i, 0))
    col = pl.BlockSpec((f, tm), lambda i: (0, i))
    return pl.pallas_call(
        body, name=name, grid=(s // tm,),
        in_specs=[row(d), row(d), row(f), row(f), _whole((1, d)), _whole((f, d)), _whole((f, d)), _whole((f, d)),
                  _whole((w, d))],
        out_specs=[row(d), pl.BlockSpec((1, d), lambda i: (0, 0)), col, col, row(d), row(w), row(d)],
        out_shape=[jax.ShapeDtypeStruct((s, d), F32), jax.ShapeDtypeStruct((1, d), F32),
                   jax.ShapeDtypeStruct((f, s), BF16), jax.ShapeDtypeStruct((f, s), BF16),
                   jax.ShapeDtypeStruct((s, d), BF16), jax.ShapeDtypeStruct((s, w), F32),
                   jax.ShapeDtypeStruct((s, d), BF16)],
        compiler_params=_cparams(("arbitrary",)),
    )(h_in, dh_out, gate, up, g, wg_t, wu_t, wd, w_out)


def _ffn_bwd_gates(dh_out, gate, up, wd, name, tm, carry=None):
    s, d = dh_out.shape
    f = gate.shape[1]
    tf = _ffn_chunk(f)

    n = s // tm
    nbuf = 3

    def body(dh_ref, gate_hbm, up_hbm, wd_ref, dgate_t_ref, dup_t_ref, dhh_ref, gbuf, ubuf, fsem):
        i = pl.program_id(0)

        def fetch(t, slot):
            return (pltpu.make_async_copy(gate_hbm.at[pl.ds(t * tm, tm)], gbuf.at[slot], fsem.at[0, slot]),
                    pltpu.make_async_copy(up_hbm.at[pl.ds(t * tm, tm)], ubuf.at[slot], fsem.at[1, slot]))

        @pl.when(i == 0)
        def _():
            for t in range(nbuf - 1):
                for c in fetch(t, t):
                    c.start()

        @pl.when(i + nbuf - 1 < n)
        def _():
            for c in fetch(i + nbuf - 1, (i + nbuf - 1) % nbuf):
                c.start()

        slot = i % nbuf
        for c in fetch(i, slot):
            c.wait()
        _gate_grads(dh_ref, gbuf.at[slot], ubuf.at[slot], wd_ref, dgate_t_ref, dup_t_ref, dhh_ref, tf)

    row = lambda w: pl.BlockSpec((tm, w), lambda i: (i, 0))
    col = pl.BlockSpec((f, tm), lambda i: (0, i))
    return _call(
        body, name, (n,), [row(d), _ANY, _ANY, _whole((f, d))], [col, col, row(d)],
        [jax.ShapeDtypeStruct((f, s), BF16)] * 2 + [jax.ShapeDtypeStruct((s, d), BF16)],
        (dh_out, gate, up, wd), ("arbitrary",), carry,
        scratch=(pltpu.VMEM((nbuf, tm, f), BF16), pltpu.VMEM((nbuf, tm, f), BF16), pltpu.SemaphoreType.DMA((2, nbuf))))


def _ffn_bwd_input(h_in, dh_out, dgate_t, dup_t, g, wg_t, wu_t, name, tm, carry=None):
    s, d = h_in.shape
    f = dgate_t.shape[0]

    row = lambda w: pl.BlockSpec((tm, w), lambda i: (i, 0))
    col = pl.BlockSpec((f, tm), lambda i: (0, i))
    return _call(
        _input_grad, name, (s // tm,),
        [row(d), row(d), col, col, _whole((1, d)), _whole((f, d)), _whole((f, d))],
        [row(d), pl.BlockSpec((1, d), lambda i: (0, 0))],
        [jax.ShapeDtypeStruct((s, d), F32), jax.ShapeDtypeStruct((1, d), F32)],
        (h_in, dh_out, dgate_t, dup_t, g, wg_t, wu_t), ("arbitrary",), carry)


def _wgrad_rs1(pos, a_ts, b, name, carry=None):
    n_a = len(a_ts)
    f, s = a_ts[0].shape
    d = b.shape[1]
    fk = f // N_DEV
    nc_in = 0 if carry is None else len(carry.inputs)
    nc_out = 0 if carry is None else len(carry.out_shapes)

    def body(pos_ref, *refs):
        a_refs, b_ref, refs = refs[:n_a], refs[n_a], refs[n_a + 1:]
        c_in = refs[:nc_in]
        outs = refs[nc_in:nc_in + 2 * n_a]
        c_out = refs[nc_in + 2 * n_a:nc_in + 2 * n_a + nc_out]
        stage, land, send_sems, recv_sems = refs[nc_in + 2 * n_a + nc_out:nc_in + 2 * n_a + nc_out + 4]
        c_sems = refs[nc_in + 2 * n_a + nc_out + 4:]
        t = pl.program_id(0)
        which, step = t // 8, t % 8
        slot = 4 * which + step % 4
        x, y, c = lax.axis_index("x"), lax.axis_index("y"), lax.axis_index("c")

        def push(k):
            return pltpu.make_async_remote_copy(src_ref=stage.at[k], dst_ref=land.at[k], send_sem=send_sems.at[k],
                                                recv_sem=recv_sems.at[k], device_id=(x, y, 1 - c), device_id_type=MESH)

        if carry is not None:
            @pl.when(t == 0)
            def _():
                carry.start(c_in, c_out, c_sems)

        for j in range(n_a):
            @pl.when(which == j)
            def _(j=j):
                g = _dot_nn(a_refs[j][...], b_ref[...])

                @pl.when(step < 4)
                def _():
                    stage[slot] = g.astype(BF16)
                    push(slot).start()

                @pl.when(step >= 4)
                def _():
                    push(slot).wait_recv()
                    p = g + land[slot].astype(F32)
                    outs[2 * j][...] = p
                    outs[2 * j + 1][...] = p.astype(BF16)

        if carry is not None:
            @pl.when(t == round(CARRY_MIDDLE * (8 * n_a - 1)))
            def _():
                carry.middle(c_in, c_out, c_sems)

        @pl.when(t == 8 * n_a - 1)
        def _():
            for k in range(4 * n_a):
                push(k).wait_send()
            if carry is not None:
                carry.finish(c_in, c_out, c_sems)

    def shard(j):
        def index_map(t, pos_ref):
            step = jnp.clip(t - 8 * j, 0, 7)
            return 4 * ((step % 4) // 2) + 2 * (step % 2) + jnp.where(step < 4, 1 - pos_ref[2], pos_ref[2]), 0
        return index_map

    out = lambda j: pl.BlockSpec((None, fk, d), lambda t, pos_ref, j=j: (jnp.clip(t - 8 * j - 4, 0, 3), 0, 0))
    return pl.pallas_call(
        body, name=name,
        grid_spec=pltpu.PrefetchScalarGridSpec(
            num_scalar_prefetch=1, grid=(8 * n_a,),
            in_specs=[pl.BlockSpec((fk, s), shard(j)) for j in range(n_a)]
            + [pl.BlockSpec((s, d), lambda t, pos_ref: (0, 0), pipeline_mode=pl.Buffered(1))] + [_ANY] * nc_in,
            out_specs=[out(j) for j in range(n_a) for _ in range(2)] + [_ANY] * nc_out,
            scratch_shapes=[pltpu.VMEM((4 * n_a, fk, d), BF16), pltpu.VMEM((4 * n_a, fk, d), BF16),
                            pltpu.SemaphoreType.DMA((4 * n_a,)), pltpu.SemaphoreType.DMA((4 * n_a,))]
            + ([] if carry is None else list(carry.sems))),
        out_shape=[jax.ShapeDtypeStruct((4, fk, d), dt) for _ in range(n_a) for dt in (F32, BF16)]
        + ([] if carry is None else list(carry.out_shapes)),
        compiler_params=_cparams(("arbitrary",)),
    )(pos, *a_ts, b, *([] if carry is None else carry.inputs))


def _rope(t, c, sa, sb, reps):
    c, sa, sb = (jnp.tile(v, (1, reps)) if reps > 1 else v for v in (c, sa, sb))
    w = t.shape[1]
    return t * c + pltpu.roll(t, w - 8, 1) * sa + pltpu.roll(t, 8, 1) * sb


def _rope_bwd(dt, c, sa, sb, reps):
    c, sa, sb = (jnp.tile(v, (1, reps)) if reps > 1 else v for v in (c, sa, sb))
    w = dt.shape[1]
    return dt * c + pltpu.roll(dt * sa, 8, 1) + pltpu.roll(dt * sb, w - 8, 1)


def _ffn_down_mix_in(x, act_t, wd, g, win_t, tabs, name, tm, carry=None):
    s, d = x.shape
    f = wd.shape[0]
    n_in = win_t.shape[0]

    def body(x_ref, act_ref, wd_ref, g_ref, w_ref, c_ref, sa_ref, sb_ref, h_ref, q_ref, k_ref, v_ref, pc_ref, n_ref):
        x = x_ref[...] + 0.5 * _dot_tn(act_ref[...], wd_ref[...])
        h_ref[...] = x
        r = lax.rsqrt(jnp.mean(x * x, axis=-1, keepdims=True) + EPS)
        nb = (x * r * g_ref[...]).astype(BF16)
        n_ref[...] = nb
        u = _dot_nt(nb, w_ref[...])
        c, sa, sb = c_ref[...], sa_ref[...], sb_ref[...]
        q_ref[...] = _rope(u[:, :ATTN_W], c, sa, sb, ATTN_W // 128).astype(BF16)
        k_ref[...] = _rope(u[:, ATTN_W:ATTN_W + KV_W], c, sa, sb, 1).astype(BF16)
        v_ref[...] = u[:, ATTN_W + KV_W:ATTN_W + 2 * KV_W].astype(BF16)
        pc_ref[...] = u[:, ATTN_W + 2 * KV_W:]

    row = lambda w: pl.BlockSpec((tm, w), lambda i: (i, 0))
    return _call(
        body, name, (s // tm,),
        [row(d), pl.BlockSpec((f, tm), lambda i: (0, i)), _whole((f, d)), _whole((1, d)), _whole((n_in, d)),
         row(128), row(128), row(128)],
        [row(d), row(ATTN_W), row(KV_W), row(KV_W), row(POOL_W), row(d)],
        [jax.ShapeDtypeStruct((s, d), F32), jax.ShapeDtypeStruct((s, ATTN_W), BF16),
         jax.ShapeDtypeStruct((s, KV_W), BF16), jax.ShapeDtypeStruct((s, KV_W), BF16),
         jax.ShapeDtypeStruct((s, POOL_W), F32), jax.ShapeDtypeStruct((s, d), BF16)],
        (x, act_t, wd, g, win_t, *tabs), ("arbitrary",), carry)


def _band_mask(n, nb, transposed):
    shape = (3 * BLK, 2 * BLK) if transposed else (2 * BLK, 3 * BLK)
    i = lax.broadcasted_iota(jnp.int32, shape, 1 if transposed else 0) % BLK
    j = lax.broadcasted_iota(jnp.int32, shape, 0 if transposed else 1)
    kpos = (n - 1) * BLK + j
    return (j >= i) & (j <= i + 2 * BLK) & (kpos >= 0) & (kpos < nb * BLK)


def _block_diag(t, kh):
    tf = t.astype(F32)
    tr = pltpu.roll(tf, HEAD_DIM, 1)
    lo = lax.broadcasted_iota(jnp.int32, tf.shape, 1) < HEAD_DIM
    top, bot = (tf, tr) if kh == 0 else (tr, tf)
    return jnp.concatenate([jnp.where(lo, top, 0.0), jnp.where(lo, 0.0, bot)], axis=0).astype(BF16)


def _fold_diag(tbd):
    lo = lax.broadcasted_iota(jnp.int32, (3 * BLK, 2 * HEAD_DIM), 1) < HEAD_DIM
    t = jnp.where(lo, tbd[:3 * BLK], tbd[3 * BLK:])
    return t + pltpu.roll(t, HEAD_DIM, 1)


def _stack_pairs(x, kh):
    return jnp.concatenate([x[:, (2 * kh) * 128:(2 * kh + 1) * 128], x[:, (2 * kh + 1) * 128:(2 * kh + 2) * 128]], axis=0)


def _sink_of(sink_ref, kh, half, axis):
    shape = (2 * BLK, 1) if axis == 0 else (1, 2 * BLK)
    first = lax.broadcasted_iota(jnp.int32, shape, axis) < BLK
    return jnp.where(first, sink_ref[0, GROUP * kh + half], sink_ref[0, GROUP * kh + 2 + half])


def _softmax_sink(sc, valid, sink, axis):
    sc = jnp.where(valid, sc, -1e30)
    m = jnp.maximum(jnp.max(sc, axis=axis, keepdims=True), sink)
    e = jnp.exp(sc - m)
    es = jnp.exp(sink - m)
    inv = 1.0 / (jnp.sum(e, axis=axis, keepdims=True) + es)
    return e * inv, es * inv


def _attn_blocks_per_step(nb):
    return next(nq for nq in (4, 2, 1) if nb % nq == 0)


def _band_specs(nq, nb, w, col=0):
    return [pl.BlockSpec((BLK, w), lambda m: (jnp.maximum(nq * m - 1, 0), col)),
            pl.BlockSpec((nq * BLK, w), lambda m: (m, col)),
            pl.BlockSpec((BLK, w), lambda m: (jnp.minimum(nq * m + nq, nb - 1), col))]


def _attn_pool_fwd(h, w_out, q, k, v, pc, sink, pool_w, pool_scale, pband, name, carry=None, middle=CARRY_MIDDLE):
    s, d = h.shape
    nb = s // BLK
    nq = _attn_blocks_per_step(nb)

    def body(sink_ref, q_ref, k0, k1, k2, v0, v1, v2, p0, p1, p2, pw_ref, ps_ref, pb_ref, h_ref, wo_ref,
             h_out_ref, o_t_ref, o_ref):
        kall = jnp.concatenate([k0[...], k1[...], k2[...]], axis=0)
        vall = jnp.concatenate([v0[...], v1[...], v2[...]], axis=0)
        pall = jnp.concatenate([p0[...], p1[...], p2[...]], axis=0).astype(BF16)
        qall = q_ref[...] * SCORE_SCALE
        for j in range(nq):
            n = pl.program_id(0) * nq + j
            rows, band = slice(j * BLK, (j + 1) * BLK), slice(j * BLK, (j + 3) * BLK)
            valid = _band_mask(n, nb, False)
            kb, vb, qs = kall[band], vall[band], qall[rows]
            for kh in range(N_KV):
                sc = _dot_nt(_stack_pairs(qs, kh), _block_diag(kb, kh))
                p = [_softmax_sink(sc[:, half * 3 * BLK:(half + 1) * 3 * BLK], valid,
                                   _sink_of(sink_ref, kh, half, 0), 1)[0] for half in range(2)]
                o2 = _dot_nn(jnp.concatenate(p, axis=1).astype(BF16), _block_diag(vb, kh)).astype(BF16)
                o_ref[rows, (2 * kh) * 128:(2 * kh + 1) * 128] = o2[:BLK]
                o_ref[rows, (2 * kh + 1) * 128:(2 * kh + 2) * 128] = o2[BLK:]
            ext = pall[band]
            var = _variant_index(n, nb)
            for gi in range(POOL_G):
                gsl = slice(gi * POOL_GW, (gi + 1) * POOL_GW)
                dg = _dot_nn(pb_ref[var, gi], ext[:, gsl])
                yg = _dot_nn(dg.astype(BF16), pw_ref[gi].astype(BF16))
                o_ref[rows, ATTN_W + gi * POOL_GW:ATTN_W + (gi + 1) * POOL_GW] = (yg * ps_ref[:, gsl]).astype(BF16)
        mix = o_ref[...]
        o_t_ref[...] = mix.T
        h_out_ref[...] = h_ref[...] + _dot_nn(mix, wo_ref[...])

    mix_w = ATTN_W + POOL_W
    return _call(
        body, name, (nb // nq,),
        [pl.BlockSpec(memory_space=pltpu.SMEM), pl.BlockSpec((nq * BLK, ATTN_W), lambda m: (m, 0)),
         *_band_specs(nq, nb, KV_W), *_band_specs(nq, nb, KV_W), *_band_specs(nq, nb, POOL_W),
         _whole((POOL_G, POOL_GW, POOL_GW)), _whole((1, POOL_W)), _whole(pband.shape),
         pl.BlockSpec((nq * BLK, d), lambda m: (m, 0)), _whole((mix_w, d))],
        [pl.BlockSpec((nq * BLK, d), lambda m: (m, 0)), pl.BlockSpec((mix_w, nq * BLK), lambda m: (0, m))],
        [jax.ShapeDtypeStruct((s, d), F32), jax.ShapeDtypeStruct((mix_w, s), BF16)],
        (sink, q, k, k, k, v, v, v, pc, pc, pc, pool_w, pool_scale, pband, h, w_out), ("arbitrary",), carry, middle,
        scratch=(pltpu.VMEM((nq * BLK, mix_w), BF16),))


def _attn_pool_bwd(q, k, v, pc, dmix, sink, pool_w, pool_scale, pband, ptband, name, carry=None):
    s = q.shape[0]
    nb = s // BLK
    nq = _attn_blocks_per_step(nb)

    def body(sink_ref, q_ref, k0, k1, k2, v0, v1, v2, p0, p1, p2, da_ref, d0, d1, d2, pw_ref, ps_ref, pb_ref, ptb_ref,
             dq_ref, dk_ref, dv_ref, dpc_ref, dsink_ref, dpw_ref, dps_ref):
        @pl.when(pl.program_id(0) == 0)
        def _():
            dsink_ref[...] = jnp.zeros_like(dsink_ref)
            dpw_ref[...] = jnp.zeros_like(dpw_ref)
            dps_ref[...] = jnp.zeros_like(dps_ref)

        kall = jnp.concatenate([k0[...], k1[...], k2[...]], axis=0)
        vall = jnp.concatenate([v0[...], v1[...], v2[...]], axis=0)
        pall = jnp.concatenate([p0[...], p1[...], p2[...]], axis=0).astype(BF16)
        dpall = jnp.concatenate([d0[...], d1[...], d2[...]], axis=0)
        lo = lax.broadcasted_iota(jnp.int32, (3 * BLK, KV_W), 1) < HEAD_DIM
        for j in range(nq):
            n = pl.program_id(0) * nq + j
            rows, band = slice(j * BLK, (j + 1) * BLK), slice(j * BLK, (j + 3) * BLK)
            valid = _band_mask(n, nb, True)
            kb, vb, qb = kall[band], vall[band], q_ref[rows, :]
            qs = qb * SCORE_SCALE
            da = da_ref[rows, :].astype(BF16)
            dk_fold, dv_fold = [], []
            for kh in range(N_KV):
                kbd, vbd = _block_diag(kb, kh), _block_diag(vb, kh)
                q2, do2 = _stack_pairs(qb, kh), _stack_pairs(da, kh)
                sc_t = _dot_nt(kbd, _stack_pairs(qs, kh))
                dp_t = _dot_nt(vbd, do2)
                p_t, ds_t = [], []
                for half in range(2):
                    keys = slice(half * 3 * BLK, (half + 1) * 3 * BLK)
                    p, ps = _softmax_sink(sc_t[keys], valid, _sink_of(sink_ref, kh, half, 1), 0)
                    delta = jnp.sum(p * dp_t[keys], axis=0, keepdims=True)
                    p_t.append(p.astype(BF16))
                    ds_t.append((p * (dp_t[keys] - delta)).astype(BF16))
                    dsk = -ps * delta
                    for pair in range(2):
                        h = GROUP * kh + 2 * pair + half
                        part = jnp.sum(dsk[:, pair * BLK:(pair + 1) * BLK], axis=1, keepdims=True)
                        dsink_ref[h:h + 1, :] += jnp.broadcast_to(part, (1, 128))
                p_t = jnp.concatenate(p_t, axis=0)
                ds_t = jnp.concatenate(ds_t, axis=0)
                dq2 = _dot_tn(ds_t, kbd) * SCORE_SCALE
                dq_ref[rows, (2 * kh) * 128:(2 * kh + 1) * 128] = dq2[:BLK]
                dq_ref[rows, (2 * kh + 1) * 128:(2 * kh + 2) * 128] = dq2[BLK:]
                dk_fold.append(_fold_diag(_dot_nn(ds_t, q2)) * SCORE_SCALE)
                dv_fold.append(_fold_diag(_dot_nn(p_t, do2)))
            dk_all = jnp.where(lo, dk_fold[0], dk_fold[1])
            dv_all = jnp.where(lo, dv_fold[0], dv_fold[1])
            for t in range(3):
                dk_ref[j, t] = dk_all[t * BLK:(t + 1) * BLK]
                dv_ref[j, t] = dv_all[t * BLK:(t + 1) * BLK]
            ext, dpe = pall[band], dpall[band]
            dpc_cur = dpall[(j + 1) * BLK:(j + 2) * BLK]
            var = _variant_index(n, nb)
            for gi in range(POOL_G):
                gsl = slice(gi * POOL_GW, (gi + 1) * POOL_GW)
                wg = pw_ref[gi].astype(BF16)
                sc = ps_ref[:, gsl]
                dgb = _dot_nn(pb_ref[var, gi], ext[:, gsl]).astype(BF16)
                yg = _dot_nn(dgb, wg)
                dps_ref[:, gsl] += jnp.sum(dpc_cur[:, gsl] * yg, axis=0, keepdims=True)
                dpw_ref[gi] += _dot_tn(dgb, (dpc_cur[:, gsl] * sc).astype(BF16))
                dd = _dot_nt((dpe[:, gsl] * sc).astype(BF16), wg)
                dpc_ref[rows, gsl] = _dot_nn(ptb_ref[var, gi], dd.astype(BF16))

    fixed = lambda shape: pl.BlockSpec(shape, lambda m: (0,) * len(shape))
    return _call(
        body, name, (nb // nq,),
        [pl.BlockSpec(memory_space=pltpu.SMEM), pl.BlockSpec((nq * BLK, ATTN_W), lambda m: (m, 0)),
         *_band_specs(nq, nb, KV_W), *_band_specs(nq, nb, KV_W), *_band_specs(nq, nb, POOL_W),
         pl.BlockSpec((nq * BLK, ATTN_W), lambda m: (m, 0)), *_band_specs(nq, nb, POOL_W, 1),
         _whole((POOL_G, POOL_GW, POOL_GW)), _whole((1, POOL_W)), _whole(pband.shape), _whole(ptband.shape)],
        [pl.BlockSpec((nq * BLK, ATTN_W), lambda m: (m, 0)),
         pl.BlockSpec((nq, 3, BLK, KV_W), lambda m: (m, 0, 0, 0)),
         pl.BlockSpec((nq, 3, BLK, KV_W), lambda m: (m, 0, 0, 0)),
         pl.BlockSpec((nq * BLK, POOL_W), lambda m: (m, 0)),
         fixed((N_HEADS, 128)), fixed((POOL_G, POOL_GW, POOL_GW)), fixed((1, POOL_W))],
        [jax.ShapeDtypeStruct((s, ATTN_W), F32), jax.ShapeDtypeStruct((nb, 3, BLK, KV_W), F32),
         jax.ShapeDtypeStruct((nb, 3, BLK, KV_W), F32), jax.ShapeDtypeStruct((s, POOL_W), F32),
         jax.ShapeDtypeStruct((N_HEADS, 128), F32),
         jax.ShapeDtypeStruct((POOL_G, POOL_GW, POOL_GW), F32), jax.ShapeDtypeStruct((1, POOL_W), F32)],
        (sink, q, k, k, k, v, v, v, pc, pc, pc, dmix, dmix, dmix, dmix, pool_w, pool_scale, pband, ptband),
        ("arbitrary",), carry)


def _mix_in_bwd(h, dh, g, win_t, dq, dkp, dvp, dpc, tabs, name, tm, carry=None):
    s, d = h.shape
    nb = s // BLK
    nt = tm // BLK
    n_in = win_t.shape[0]

    def band_sum(n, before, own, after, prev_last, next_first):
        lo = (n > 0).astype(F32)
        hi = (n < s // tm - 1).astype(F32)
        blocks = []
        for b in range(nt):
            from_prev = prev_last[...] * lo if b == 0 else before[b - 1]
            from_next = next_first[...] * hi if b == nt - 1 else after[b + 1]
            blocks.append(from_prev + own[b] + from_next)
        return jnp.concatenate(blocks, axis=0)

    def body(h_ref, dh_ref, g_ref, w_ref, dq_ref, k2, k1, k0, kp, kn, v2, v1, v0, vp, vn, dpc_ref, c_ref, sa_ref,
             sb_ref, o_ref, du_ref, dg_ref):
        n = pl.program_id(0)
        dk = band_sum(n, k2, k1, k0, kp, kn)
        dv = band_sum(n, v2, v1, v0, vp, vn)
        c, sa, sb = c_ref[...], sa_ref[...], sb_ref[...]
        du = jnp.concatenate([_rope_bwd(dq_ref[...], c, sa, sb, ATTN_W // 128), _rope_bwd(dk, c, sa, sb, 1), dv,
                              dpc_ref[...]], axis=1)
        du_ref[...] = du.T.astype(BF16)
        dn = _dot_nn(du.astype(BF16), w_ref[...])
        x = h_ref[...]
        r = lax.rsqrt(jnp.mean(x * x, axis=-1, keepdims=True) + EPS)
        xhat = x * r
        dxhat = dn * g_ref[...]
        o_ref[...] = dh_ref[...] + r * (dxhat - xhat * jnp.mean(dxhat * xhat, axis=-1, keepdims=True))

        @pl.when(n == 0)
        def _():
            dg_ref[...] = jnp.zeros_like(dg_ref)

        dg_ref[...] += jnp.sum(dn * xhat, axis=0, keepdims=True)

    row = lambda w: pl.BlockSpec((tm, w), lambda n: (n, 0))
    slot = lambda t: pl.BlockSpec((nt, None, BLK, KV_W), lambda n, t=t: (n, t, 0, 0))
    parts = [slot(2), slot(1), slot(0),
             pl.BlockSpec((None, None, BLK, KV_W), lambda n: (jnp.maximum(nt * n - 1, 0), 2, 0, 0)),
             pl.BlockSpec((None, None, BLK, KV_W), lambda n: (jnp.minimum(nt * n + nt, nb - 1), 0, 0, 0))]
    return _call(
        body, name, (s // tm,),
        [row(d), row(d), _whole((1, d)), _whole((n_in, d)), row(ATTN_W), *parts, *parts, row(POOL_W),
         row(128), row(128), row(128)],
        [row(d), pl.BlockSpec((n_in, tm), lambda n: (0, n)), pl.BlockSpec((1, d), lambda n: (0, 0))],
        [jax.ShapeDtypeStruct((s, d), F32), jax.ShapeDtypeStruct((n_in, s), BF16), jax.ShapeDtypeStruct((1, d), F32)],
        (h, dh, g, win_t, dq, *[dkp] * 5, *[dvp] * 5, dpc, *tabs), ("arbitrary",), carry)


def _adam_math(w, g, m, v):
    m = ADAM_B1 * m + (1.0 - ADAM_B1) * g
    v = ADAM_B2 * v + (1.0 - ADAM_B2) * (g * g)
    m_hat = m / (1.0 - ADAM_B1 ** ADAM_STEP)
    v_hat = v / (1.0 - ADAM_B2 ** ADAM_STEP)
    delta = -ADAM_LR * (m_hat / (jnp.sqrt(v_hat) + ADAM_EPS) + ADAM_WD * w)
    return delta, m, v


def _adam_small(w, parts, late, m, v, name):
    rows, cols = w.shape

    def body(w_ref, p_ref, l_ref, m_ref, v_ref, g_ref, d_ref, nm_ref, nv_ref):
        g, gl = p_ref[0], l_ref[0]
        for k in range(1, N_DEV):
            g = g + p_ref[k]
            gl = gl + l_ref[k]
        g_ref[...] = g
        g_ref[SMALL_NORM1:SMALL_NORM1 + 8, :] = g[SMALL_NORM1:SMALL_NORM1 + 8] + gl
        d_ref[...], nm_ref[...], nv_ref[...] = _adam_math(w_ref[...], g_ref[...], m_ref[...], v_ref[...])

    return pl.pallas_call(
        body, name=name, out_shape=[jax.ShapeDtypeStruct((rows, cols), F32)] * 4,
    )(w, parts, late, m, v)


SMALL_NORM1 = 512


def _pack_small(norm1, normm, norm2, normf, sink, pool_w, pool_scale, loss=None):
    scale_rows = jnp.pad(pool_scale.reshape(4, 128), ((0, 4), (0, 0)))
    last_rows = jnp.pad(sink.reshape(1, N_HEADS), ((0, 7), (0, 128 - N_HEADS)))
    if loss is not None:
        last_rows = last_rows + jnp.pad(loss.reshape(1, 1), ((1, 6), (0, 127)))
    return jnp.concatenate([pool_w.reshape(512, 128), norm1.reshape(8, 128), normm.reshape(8, 128),
                            norm2.reshape(8, 128), normf.reshape(8, 128), scale_rows, last_rows], axis=0)


def _unpack_small(p):
    return dict(pool_w=p[:512].reshape(1, POOL_G, POOL_GW, POOL_GW), ffn1_norm=p[512:520].reshape(1, 1024),
                mix_norm=p[520:528].reshape(1, 1024), ffn2_norm=p[528:536].reshape(1, 1024),
                final_norm=p[536:544].reshape(1024), pool_scale=p[544:548].reshape(1, POOL_W),
                sink_logits=p[552, :N_HEADS].reshape(1, N_HEADS), loss=p[553, 0])


def kernel(x, ffn1_norm, ffn1_w_gate, ffn1_w_up, ffn1_w_down, mix_norm, w_in, sink_logits, pool_w, pool_scale, w_out, ffn2_norm, ffn2_w_gate, ffn2_w_up, ffn2_w_down, final_norm, loss_target, m_ffn1_norm, m_ffn1_w_gate, m_ffn1_w_up, m_ffn1_w_down, m_mix_norm, m_w_in, m_sink_logits, m_pool_w, m_pool_scale, m_w_out, m_ffn2_norm, m_ffn2_w_gate, m_ffn2_w_up, m_ffn2_w_down, m_final_norm, v_ffn1_norm, v_ffn1_w_gate, v_ffn1_w_up, v_ffn1_w_down, v_mix_norm, v_w_in, v_sink_logits, v_pool_w, v_pool_scale, v_w_out, v_ffn2_norm, v_ffn2_w_gate, v_ffn2_w_up, v_ffn2_w_down, v_final_norm):
    s, d = x.shape[1], x.shape[2]
    tm = min(512, s)
    tm_bwd = min(256, s)
    pos = jnp.stack([lax.axis_index("x"), lax.axis_index("y"), lax.axis_index("c")]).astype(jnp.int32)

    t_bf = lambda w: w[0].T.astype(BF16)
    full = lambda a: a.reshape(N_DEV * a.shape[1], d)
    (wg1,) = map(full, _run_exchange(_AllGather([t_bf(ffn1_w_gate)]), "gather_ffn1_gate"))

    tabs = _rope_tables(s)
    pband, ptband = _pool_tables(s)
    g1, gm, g2, gf = ffn1_norm, mix_norm, ffn2_norm, final_norm.reshape(1, d)

    x0 = x[0]
    n1, gate1, wu1 = _ffn_gate(x0, g1, wg1, "ffn1_gate", tm, carry=_AllGather([t_bf(ffn1_w_up)]))
    up1, act1_t, wd1, win_t = _ffn_up(n1, gate1, full(wu1), "ffn1_up", tm,
                                            carry=_AllGather([ffn1_w_down[0].astype(BF16), t_bf(w_in)]))
    wu1, wd1, win_t = full(wu1), full(wd1), full(win_t)
    h1, q, k, v, pc, n2, wout, wg2 = _ffn_down_mix_in(
        x0, act1_t, wd1, gm, win_t, tabs, "ffn1_down_mix_in", tm,
        carry=_AllGather([w_out[0].astype(BF16), t_bf(ffn2_w_gate)]))
    wout, wg2 = full(wout), full(wg2)
    h2, mix_t, *gathered = _attn_pool_fwd(h1, wout, q, k, v, pc, sink_logits, pool_w[0], pool_scale, pband,
                                          "attn_pool_fwd",
                                          carry=_AllGather([t_bf(ffn2_w_up), ffn2_w_down[0].astype(BF16)]))
    wu2, wd2 = map(full, gathered)
    dh3, n3, gate2, up2, act2_t, loss_part, dgf = _ffn_fwd(h2, g2, wg2, wu2, wd2, "ffn2_fwd", tm,
                                                           head=(gf, loss_target[0]))

    sum1, recv2 = {}, {}

    def stage2(keys):
        return _RsStage2([sum1[key][1] for key in keys])

    dh2, dg2, dgate2_t, dup2_t, dhh3, dmix, dh2b = _ffn_bwd(h2, dh3, gate2, up2, g2, wg2, wu2, wd2, wout, "ffn2_bwd",
                                                            tm_bwd)
    both = _wgrad_rs1(pos, [dgate2_t, dup2_t], n3, "wgrad_gate_up2")
    sum1["g2"], sum1["u2"] = both[:2], both[2:]
    sum1["d2"] = _wgrad_rs1(pos, [act2_t], dhh3, "wgrad_down2")
    sum1["out"] = _wgrad_rs1(pos, [mix_t], dh2b, "wgrad_out")
    dq, dkp, dvp, dpc, dsink, dpw, dps, *r2 = _attn_pool_bwd(
        q, k, v, pc, dmix, sink_logits, pool_w[0], pool_scale, pband, ptband, "attn_pool_bwd",
        carry=stage2(["g2", "u2", "d2"]))
    recv2.update(zip(["g2", "u2", "d2"], r2))
    dh1, du_t, dgm, recv2["out"] = _mix_in_bwd(h1, dh2, gm, win_t, dq, dkp, dvp, dpc, tabs, "mix_in_bwd", tm,
                                               carry=stage2(["out"]))
    sum1["in"] = _wgrad_rs1(pos, [du_t], n2, "wgrad_in")
    small_part = _pack_small(jnp.zeros_like(dgm), dgm, dg2, dgf, dsink[:, 0], dpw, dps, loss_part)
    dgate1_t, dup1_t, dhh1, recv2["in"] = _ffn_bwd_gates(dh1, gate1, up1, wd1, "ffn1_bwd_gates", tm,
                                                         carry=stage2(["in"]))
    *sum1["g1"], small_all = _wgrad_rs1(pos, [dgate1_t], n1, "wgrad_gate1", carry=_AllGather([small_part]))
    *sum1["u1"], recv2["g1"] = _wgrad_rs1(pos, [dup1_t], n1, "wgrad_up1", carry=stage2(["g1"]))
    *sum1["d1"], recv2["u1"] = _wgrad_rs1(pos, [act1_t], dhh1, "wgrad_down1", carry=stage2(["u1"]))
    dx, dg1, recv2["d1"] = _ffn_bwd_input(x0, dh1, dgate1_t, dup1_t, g1, wg1, wu1, "ffn1_bwd_input", tm,
                                          carry=stage2(["d1"]))

    (dg1_all,) = _run_exchange(_DirectGather([dg1.reshape(8, 128)]), "gather_norm1_grad")
    pk = lambda a, b, c_, e, s_, pw_, psc: _pack_small(a, b, c_, e, s_[0], pw_[0], psc)
    small_w = pk(ffn1_norm, mix_norm, ffn2_norm, final_norm, sink_logits, pool_w, pool_scale)
    small_m = pk(m_ffn1_norm, m_mix_norm, m_ffn2_norm, m_final_norm, m_sink_logits, m_pool_w, m_pool_scale)
    small_v = pk(v_ffn1_norm, v_mix_norm, v_ffn2_norm, v_final_norm, v_sink_logits, v_pool_w, v_pool_scale)
    sg, sd, sm, sv = [_unpack_small(a)
                      for a in _adam_small(small_w, small_all, dg1_all, small_m, small_v, "adam_small")]

    big = {}
    keys = ["g1", "u1", "d1", "g2", "u2", "d2", "in", "out"]
    names = ["ffn1_w_gate", "ffn1_w_up", "ffn1_w_down", "ffn2_w_gate", "ffn2_w_up", "ffn2_w_down", "w_in", "w_out"]
    transposed = [True, True, False, True, True, False, True, False]
    ws = [ffn1_w_gate, ffn1_w_up, ffn1_w_down, ffn2_w_gate, ffn2_w_up, ffn2_w_down, w_in, w_out]
    ms = [m_ffn1_w_gate, m_ffn1_w_up, m_ffn1_w_down, m_ffn2_w_gate, m_ffn2_w_up, m_ffn2_w_down, m_w_in, m_w_out]
    vs = [v_ffn1_w_gate, v_ffn1_w_up, v_ffn1_w_down, v_ffn2_w_gate, v_ffn2_w_up, v_ffn2_w_down, v_w_in, v_w_out]
    for key, nm, tr, w, m, vv in zip(keys, names, transposed, ws, ms, vs):
        view = (lambda a: jnp.swapaxes(a, 1, 2)[0]) if tr else (lambda a: a[0])
        back = (lambda a: jnp.swapaxes(a[None], 1, 2)) if tr else (lambda a: a[None])
        res = _rs_sum2_adam(pos, sum1[key][0], recv2[key], view(w), view(m), view(vv), "adam_" + nm)
        big[nm] = tuple(back(a) for a in res)

    loss = sg["loss"]
    all_names = ["ffn1_norm", "ffn1_w_gate", "ffn1_w_up", "ffn1_w_down", "mix_norm", "w_in", "sink_logits", "pool_w",
                 "pool_scale", "w_out", "ffn2_norm", "ffn2_w_gate", "ffn2_w_up", "ffn2_w_down", "final_norm"]
    outs = [loss, dx[None]]
    for idx, src in enumerate((sg, sd, sm, sv)):
        for nm in all_names:
            outs.append(big[nm][idx] if nm in big else src[nm])
    return tuple(outs)
```
